```python
import jax, jax.numpy as jnp
from jax import lax
import numpy as np

D_MODEL = 2048
BATCH = 8
SEQ = 2048
DEPTH = 1

N_MEM = 256
NORM_EPS = 1e-6
NEG_INF = -1e30
A_HEADS = 12
A_KV_HEADS = 4
A_HEAD_DIM = 64
A_WIDTH = A_HEADS * A_HEAD_DIM
A_KV_WIDTH = A_KV_HEADS * A_HEAD_DIM
WINDOW = 128
BLOCK = 128
ROPE_THETA = 10000.0
R_HEADS = 12
R_HEAD_DIM = 64
R_WIDTH = R_HEADS * R_HEAD_DIM
DECAY_LORA = 64
ICLR_LORA = 64
N_DIR = 2
GN_EPS = 64e-5
R_SHIFT_WIDTH = 3 * R_WIDTH + N_DIR * DECAY_LORA + N_DIR * ICLR_LORA
X_HEADS = 4
X_HEAD_DIM = 128
X_WIDTH = X_HEADS * X_HEAD_DIM
N_BRANCH = 3
IN_WIDTH = (2 * A_WIDTH + 2 * A_KV_WIDTH + R_SHIFT_WIDTH + R_WIDTH
            + 2 * X_WIDTH + N_BRANCH * D_MODEL)

kernel_name = "hybrid_swa_rwkv7_memxattn_gated_encoder"


def _split(t, sizes):
    idx = np.cumsum(sizes)[:-1].tolist()
    return jnp.split(t, idx, axis=-1)


def rms_norm(t, g, eps=NORM_EPS):
    tf = t.astype(jnp.float32)
    y = tf * lax.rsqrt(jnp.mean(tf * tf, axis=-1, keepdims=True) + eps)
    return (y * g.astype(jnp.float32)).astype(t.dtype)


def rope(t, positions):
    half = t.shape[-1] // 2
    inv = ROPE_THETA ** (-jnp.arange(half, dtype=jnp.float32) / half)
    ang = positions.astype(jnp.float32)[:, None] * inv[None, :]
    cos = jnp.cos(ang)[None, :, None, :]
    sin = jnp.sin(ang)[None, :, None, :]
    t1 = t[..., :half].astype(jnp.float32)
    t2 = t[..., half:].astype(jnp.float32)
    return jnp.concatenate([t1 * cos - t2 * sin, t2 * cos + t1 * sin], axis=-1).astype(t.dtype)


def windowed_gqa(q, k, v, sink):
    B, S, Hq, D = q.shape
    nb = S // BLOCK
    G = Hq // A_KV_HEADS
    qb = q.reshape(B, nb, BLOCK, A_KV_HEADS, G, D)
    pad = ((0, 0), (BLOCK, BLOCK), (0, 0), (0, 0))
    kp = jnp.pad(k, pad).reshape(B, nb + 2, BLOCK, A_KV_HEADS, D)
    vp = jnp.pad(v, pad).reshape(B, nb + 2, BLOCK, A_KV_HEADS, D)
    kb = jnp.concatenate([kp[:, :-2], kp[:, 1:-1], kp[:, 2:]], axis=2)
    vb = jnp.concatenate([vp[:, :-2], vp[:, 1:-1], vp[:, 2:]], axis=2)
    s = jnp.einsum('bnqhgd,bnkhd->bnhgqk', qb, kb).astype(jnp.float32) * (D ** -0.5)
    blk = jnp.arange(nb)[:, None] * BLOCK
    qpos = blk + jnp.arange(BLOCK)[None, :]
    kpos = blk - BLOCK + jnp.arange(3 * BLOCK)[None, :]
    rel = kpos[:, None, :] - qpos[:, :, None]
    valid = (jnp.abs(rel) <= WINDOW) & (kpos[:, None, :] >= 0) & (kpos[:, None, :] < S)
    s = jnp.where(valid[None, :, None, None, :, :], s, NEG_INF)
    sink_l = sink.astype(jnp.float32).reshape(A_KV_HEADS, G)[None, None, :, :, None, None]
    sink_l = jnp.broadcast_to(sink_l, s.shape[:-1] + (1,))
    p = jax.nn.softmax(jnp.concatenate([s, sink_l], axis=-1), axis=-1)[..., :-1]
    o = jnp.einsum('bnhgqk,bnkhd->bnqhgd', p.astype(v.dtype), vb)
    return o.reshape(B, S, Hq * D)


def rwkv7_scan(r, w, k, v, kk, a):
    B, T, H, N = r.shape
    decay = jnp.exp(-jnp.exp(w))
    b = kk * a
    xs = tuple(t.swapaxes(0, 1) for t in (r, decay, k, v, kk, b))

    def step(S, inp):
        r_t, d_t, k_t, v_t, kk_t, b_t = inp
        sa = jnp.einsum('bhij,bhj->bhi', S, -kk_t)
        S = S * d_t[:, :, None, :] + sa[..., None] * b_t[:, :, None, :] + v_t[..., None] * k_t[:, :, None, :]
        return S, jnp.einsum('bhij,bhj->bhi', S, r_t)

    S0 = jnp.zeros((B, H, N, N), jnp.float32)
    _, y = lax.scan(step, S0, xs)
    return y.swapaxes(0, 1)


def rwkv7_branch(p_shift, mu, k_k, k_a, r_k, w0, w2, a0, a2, ln_w, ln_b):
    B, T, _ = p_shift.shape
    p = p_shift.astype(jnp.float32)
    prev = jnp.pad(p, ((0, 0), (1, 0), (0, 0)))[:, :-1]
    nxt = jnp.pad(p, ((0, 0), (0, 1), (0, 0)))[:, 1:]
    p = p + mu.astype(jnp.float32) * (0.5 * (prev + nxt) - p)
    r, k, v, wf, wb, af, ab = _split(p, (R_WIDTH, R_WIDTH, R_WIDTH, DECAY_LORA, DECAY_LORA,
                                         ICLR_LORA, ICLR_LORA))
    hs = (B, T, R_HEADS, R_HEAD_DIM)
    r4 = r.reshape(hs)
    v4 = v.reshape(hs)
    kk = (k * k_k.astype(jnp.float32)).reshape(hs)
    kk = kk / jnp.maximum(jnp.linalg.norm(kk, axis=-1, keepdims=True), 1e-12)
    w_in = (wf, wb)
    a_in = (af, ab)
    y_sum = jnp.zeros(hs, jnp.float32)
    bonus = jnp.zeros(hs, jnp.float32)
    r_k4 = r_k.astype(jnp.float32)[None, None]
    for d in range(N_DIR):
        wd = -jax.nn.softplus(-(w0[d].astype(jnp.float32)
                                + jnp.tanh(w_in[d]) @ w2[d].astype(jnp.float32))) - 0.5
        ad = jax.nn.sigmoid(a0[d].astype(jnp.float32) + a_in[d] @ a2[d].astype(jnp.float32))
        kd = k * (1.0 + (ad - 1.0) * k_a.astype(jnp.float32))
        wd4, ad4, kd4 = wd.reshape(hs), ad.reshape(hs), kd.reshape(hs)
        if d == 0:
            y_sum = y_sum + rwkv7_scan(r4, wd4, kd4, v4, kk, ad4)
        else:
            flip = lambda t: jnp.flip(t, axis=1)
            y_sum = y_sum + flip(rwkv7_scan(flip(r4), flip(wd4), flip(kd4), flip(v4), flip(kk), flip(ad4)))
        bonus = bonus + jnp.sum(r4 * kd4 * r_k4, axis=-1, keepdims=True) * v4
    mean = jnp.mean(y_sum, axis=-1, keepdims=True)
    var = jnp.mean(jnp.square(y_sum - mean), axis=-1, keepdims=True)
    y = ((y_sum - mean) * lax.rsqrt(var + GN_EPS)).reshape(B, T, R_WIDTH)
    y = y * ln_w.astype(jnp.float32) + ln_b.astype(jnp.float32)
    return y + bonus.reshape(B, T, R_WIDTH)


def memory_xattn(q, km, vm):
    s = jnp.einsum('bshd,bmhd->bhsm', q, km).astype(jnp.float32) * (X_HEAD_DIM ** -0.5)
    p = jax.nn.softmax(s, axis=-1)
    o = jnp.einsum('bhsm,bmhd->bshd', p.astype(vm.dtype), vm)
    return o.reshape(q.shape[0], q.shape[1], X_WIDTH)


def hybrid_layer(x, mem, norm_g, mem_norm_g, w_in, gate_b, attn_q_norm_g, attn_k_norm_g,
                 attn_sink, attn_w_o, rwkv_mu, rwkv_k_k, rwkv_k_a, rwkv_r_k, rwkv_w0, rwkv_w2,
                 rwkv_a0, rwkv_a2, rwkv_ln_w, rwkv_ln_b, rwkv_w_o, x_w_kv, x_q_norm_g,
                 x_k_norm_g, x_w_o, w_out):
    B, S, _ = x.shape
    positions = jnp.arange(S)
    h = rms_norm(x, norm_g)
    proj = h @ w_in
    aq, ak, av, ag, rs, rg, xq, xg, mg = _split(
        proj, (A_WIDTH, A_KV_WIDTH, A_KV_WIDTH, A_WIDTH, R_SHIFT_WIDTH, R_WIDTH,
               X_WIDTH, X_WIDTH, N_BRANCH * D_MODEL))

    q = rope(rms_norm(aq.reshape(B, S, A_HEADS, A_HEAD_DIM), attn_q_norm_g), positions)
    k = rope(rms_norm(ak.reshape(B, S, A_KV_HEADS, A_HEAD_DIM), attn_k_norm_g), positions)
    va = av.reshape(B, S, A_KV_HEADS, A_HEAD_DIM)
    y_a = windowed_gqa(q, k, va, attn_sink) * jax.nn.silu(ag)

    y_r = rwkv7_branch(rs, rwkv_mu, rwkv_k_k, rwkv_k_a, rwkv_r_k, rwkv_w0, rwkv_w2,
                       rwkv_a0, rwkv_a2, rwkv_ln_w, rwkv_ln_b)
    y_r = y_r.astype(x.dtype) * jax.nn.silu(rg)

    mkv = rms_norm(mem, mem_norm_g) @ x_w_kv
    km, vm = _split(mkv, (X_WIDTH, X_WIDTH))
    km = rms_norm(km.reshape(B, -1, X_HEADS, X_HEAD_DIM), x_k_norm_g)
    vm = vm.reshape(B, -1, X_HEADS, X_HEAD_DIM)
    qx = rms_norm(xq.reshape(B, S, X_HEADS, X_HEAD_DIM), x_q_norm_g)
    y_x = memory_xattn(qx, km, vm) * jax.nn.silu(xg)

    gates = jax.nn.sigmoid((mg + gate_b).astype(jnp.float32)).reshape(B, S, N_BRANCH, D_MODEL)
    gates = gates.astype(x.dtype)
    merged = (gates[:, :, 0] * (y_a @ attn_w_o)
              + gates[:, :, 1] * (y_r @ rwkv_w_o)
              + gates[:, :, 2] * (y_x @ x_w_o))
    return x + merged @ w_out


def _fwd_setup_inputs(seed: int = 0) -> dict:
    key = jax.random.key(seed)
    ks = jax.random.split(key, 32)
    n = lambda i, shape, s=1.0: jax.random.normal(ks[i], shape, jnp.float32) * s
    L = DEPTH
    return {
        "x": n(0, (BATCH, SEQ, D_MODEL)),
        "mem": n(1, (BATCH, N_MEM, D_MODEL)),
        "norm_g": 1.0 + n(2, (L, D_MODEL), 0.02),
        "mem_norm_g": 1.0 + n(3, (L, D_MODEL), 0.02),
        "w_in": n(4, (L, D_MODEL, IN_WIDTH), D_MODEL ** -0.5),
        "gate_b": n(5, (L, N_BRANCH * D_MODEL), 0.1),
        "attn_q_norm_g": 1.0 + n(6, (L, A_HEAD_DIM), 0.02),
        "attn_k_norm_g": 1.0 + n(7, (L, A_HEAD_DIM), 0.02),
        "attn_sink": n(8, (L, A_HEADS), 0.5),
        "attn_w_o": n(9, (L, A_WIDTH, D_MODEL), A_WIDTH ** -0.5),
        "rwkv_mu": jax.random.uniform(ks[10], (L, R_SHIFT_WIDTH), jnp.float32, 0.0, 1.0),
        "rwkv_k_k": 0.85 + n(11, (L, R_WIDTH), 0.05),
        "rwkv_k_a": 1.0 + n(12, (L, R_WIDTH), 0.05),
        "rwkv_r_k": n(13, (L, R_HEADS, R_HEAD_DIM), 0.1),
        "rwkv_w0": jax.random.uniform(ks[14], (L, N_DIR, R_WIDTH), jnp.float32, -5.0, -1.0),
        "rwkv_w2": n(15, (L, N_DIR, DECAY_LORA, R_WIDTH), 0.1),
        "rwkv_a0": n(16, (L, N_DIR, R_WIDTH), 0.1),
        "rwkv_a2": n(17, (L, N_DIR, ICLR_LORA, R_WIDTH), 0.5 * ICLR_LORA ** -0.5),
        "rwkv_ln_w": 1.0 + n(18, (L, R_WIDTH), 0.02),
        "rwkv_ln_b": n(19, (L, R_WIDTH), 0.02),
        "rwkv_w_o": n(20, (L, R_WIDTH, D_MODEL), R_WIDTH ** -0.5),
        "x_w_kv": n(21, (L, D_MODEL, 2 * X_WIDTH), D_MODEL ** -0.5),
        "x_q_norm_g": 1.0 + n(22, (L, X_HEAD_DIM), 0.02),
        "x_k_norm_g": 1.0 + n(23, (L, X_HEAD_DIM), 0.02),
        "x_w_o": n(24, (L, X_WIDTH, D_MODEL), X_WIDTH ** -0.5),
        "w_out": n(25, (L, D_MODEL, D_MODEL), D_MODEL ** -0.5),
    }


def _fwd_reference(x, mem, norm_g, mem_norm_g, w_in, gate_b, attn_q_norm_g, attn_k_norm_g,
              attn_sink, attn_w_o, rwkv_mu, rwkv_k_k, rwkv_k_a, rwkv_r_k, rwkv_w0, rwkv_w2,
              rwkv_a0, rwkv_a2, rwkv_ln_w, rwkv_ln_b, rwkv_w_o, x_w_kv, x_q_norm_g,
              x_k_norm_g, x_w_o, w_out):
    h = x
    for l in range(DEPTH):
        h = hybrid_layer(h, mem, norm_g[l], mem_norm_g[l], w_in[l], gate_b[l],
                         attn_q_norm_g[l], attn_k_norm_g[l], attn_sink[l], attn_w_o[l],
                         rwkv_mu[l], rwkv_k_k[l], rwkv_k_a[l], rwkv_r_k[l], rwkv_w0[l],
                         rwkv_w2[l], rwkv_a0[l], rwkv_a2[l], rwkv_ln_w[l], rwkv_ln_b[l],
                         rwkv_w_o[l], x_w_kv[l], x_q_norm_g[l], x_k_norm_g[l], x_w_o[l],
                         w_out[l])
    return h


import jax as _jax
import jax.numpy as _jnp

TWIN_FORMAT = 'train_step'
FWD_PARAMS = ['x', 'mem', 'norm_g', 'mem_norm_g', 'w_in', 'gate_b', 'attn_q_norm_g', 'attn_k_norm_g', 'attn_sink', 'attn_w_o', 'rwkv_mu', 'rwkv_k_k', 'rwkv_k_a', 'rwkv_r_k', 'rwkv_w0', 'rwkv_w2', 'rwkv_a0', 'rwkv_a2', 'rwkv_ln_w', 'rwkv_ln_b', 'rwkv_w_o', 'x_w_kv', 'x_q_norm_g', 'x_k_norm_g', 'x_w_o', 'w_out']
TWIN_WEIGHTS = ['norm_g', 'mem_norm_g', 'w_in', 'gate_b', 'attn_q_norm_g', 'attn_k_norm_g', 'attn_sink', 'attn_w_o', 'rwkv_mu', 'rwkv_k_k', 'rwkv_k_a', 'rwkv_r_k', 'rwkv_w0', 'rwkv_w2', 'rwkv_a0', 'rwkv_a2', 'rwkv_ln_w', 'rwkv_ln_b', 'rwkv_w_o', 'x_w_kv', 'x_q_norm_g', 'x_k_norm_g', 'x_w_o', 'w_out']
TWIN_DIFF_INPUT = 'x'
TWIN_INPUTS = ['x', 'mem', 'norm_g', 'mem_norm_g', 'w_in', 'gate_b', 'attn_q_norm_g', 'attn_k_norm_g', 'attn_sink', 'attn_w_o', 'rwkv_mu', 'rwkv_k_k', 'rwkv_k_a', 'rwkv_r_k', 'rwkv_w0', 'rwkv_w2', 'rwkv_a0', 'rwkv_a2', 'rwkv_ln_w', 'rwkv_ln_b', 'rwkv_w_o', 'x_w_kv', 'x_q_norm_g', 'x_k_norm_g', 'x_w_o', 'w_out', 'loss_target', 'm_norm_g', 'm_mem_norm_g', 'm_w_in', 'm_gate_b', 'm_attn_q_norm_g', 'm_attn_k_norm_g', 'm_attn_sink', 'm_attn_w_o', 'm_rwkv_mu', 'm_rwkv_k_k', 'm_rwkv_k_a', 'm_rwkv_r_k', 'm_rwkv_w0', 'm_rwkv_w2', 'm_rwkv_a0', 'm_rwkv_a2', 'm_rwkv_ln_w', 'm_rwkv_ln_b', 'm_rwkv_w_o', 'm_x_w_kv', 'm_x_q_norm_g', 'm_x_k_norm_g', 'm_x_w_o', 'm_w_out', 'v_norm_g', 'v_mem_norm_g', 'v_w_in', 'v_gate_b', 'v_attn_q_norm_g', 'v_attn_k_norm_g', 'v_attn_sink', 'v_attn_w_o', 'v_rwkv_mu', 'v_rwkv_k_k', 'v_rwkv_k_a', 'v_rwkv_r_k', 'v_rwkv_w0', 'v_rwkv_w2', 'v_rwkv_a0', 'v_rwkv_a2', 'v_rwkv_ln_w', 'v_rwkv_ln_b', 'v_rwkv_w_o', 'v_x_w_kv', 'v_x_q_norm_g', 'v_x_k_norm_g', 'v_x_w_o', 'v_w_out']
TWIN_OUTPUTS = ['loss', 'grad_x', 'grad_norm_g', 'grad_mem_norm_g', 'grad_w_in', 'grad_gate_b', 'grad_attn_q_norm_g', 'grad_attn_k_norm_g', 'grad_attn_sink', 'grad_attn_w_o', 'grad_rwkv_mu', 'grad_rwkv_k_k', 'grad_rwkv_k_a', 'grad_rwkv_r_k', 'grad_rwkv_w0', 'grad_rwkv_w2', 'grad_rwkv_a0', 'grad_rwkv_a2', 'grad_rwkv_ln_w', 'grad_rwkv_ln_b', 'grad_rwkv_w_o', 'grad_x_w_kv', 'grad_x_q_norm_g', 'grad_x_k_norm_g', 'grad_x_w_o', 'grad_w_out', 'delta_norm_g', 'delta_mem_norm_g', 'delta_w_in', 'delta_gate_b', 'delta_attn_q_norm_g', 'delta_attn_k_norm_g', 'delta_attn_sink', 'delta_attn_w_o', 'delta_rwkv_mu', 'delta_rwkv_k_k', 'delta_rwkv_k_a', 'delta_rwkv_r_k', 'delta_rwkv_w0', 'delta_rwkv_w2', 'delta_rwkv_a0', 'delta_rwkv_a2', 'delta_rwkv_ln_w', 'delta_rwkv_ln_b', 'delta_rwkv_w_o', 'delta_x_w_kv', 'delta_x_q_norm_g', 'delta_x_k_norm_g', 'delta_x_w_o', 'delta_w_out', 'new_m_norm_g', 'new_m_mem_norm_g', 'new_m_w_in', 'new_m_gate_b', 'new_m_attn_q_norm_g', 'new_m_attn_k_norm_g', 'new_m_attn_sink', 'new_m_attn_w_o', 'new_m_rwkv_mu', 'new_m_rwkv_k_k', 'new_m_rwkv_k_a', 'new_m_rwkv_r_k', 'new_m_rwkv_w0', 'new_m_rwkv_w2', 'new_m_rwkv_a0', 'new_m_rwkv_a2', 'new_m_rwkv_ln_w', 'new_m_rwkv_ln_b', 'new_m_rwkv_w_o', 'new_m_x_w_kv', 'new_m_x_q_norm_g', 'new_m_x_k_norm_g', 'new_m_x_w_o', 'new_m_w_out', 'new_v_norm_g', 'new_v_mem_norm_g', 'new_v_w_in', 'new_v_gate_b', 'new_v_attn_q_norm_g', 'new_v_attn_k_norm_g', 'new_v_attn_sink', 'new_v_attn_w_o', 'new_v_rwkv_mu', 'new_v_rwkv_k_k', 'new_v_rwkv_k_a', 'new_v_rwkv_r_k', 'new_v_rwkv_w0', 'new_v_rwkv_w2', 'new_v_rwkv_a0', 'new_v_rwkv_a2', 'new_v_rwkv_ln_w', 'new_v_rwkv_ln_b', 'new_v_rwkv_w_o', 'new_v_x_w_kv', 'new_v_x_q_norm_g', 'new_v_x_k_norm_g', 'new_v_x_w_o', 'new_v_w_out']
TWIN_LEAF_KINDS = {'loss': 'loss', 'grad_x': 'grad_x', 'grad_norm_g': 'grad_w', 'grad_mem_norm_g': 'grad_w', 'grad_w_in': 'grad_w', 'grad_gate_b': 'grad_w', 'grad_attn_q_norm_g': 'grad_w', 'grad_attn_k_norm_g': 'grad_w', 'grad_attn_sink': 'grad_w', 'grad_attn_w_o': 'grad_w', 'grad_rwkv_mu': 'grad_w', 'grad_rwkv_k_k': 'grad_w', 'grad_rwkv_k_a': 'grad_w', 'grad_rwkv_r_k': 'grad_w', 'grad_rwkv_w0': 'grad_w', 'grad_rwkv_w2': 'grad_w', 'grad_rwkv_a0': 'grad_w', 'grad_rwkv_a2': 'grad_w', 'grad_rwkv_ln_w': 'grad_w', 'grad_rwkv_ln_b': 'grad_w', 'grad_rwkv_w_o': 'grad_w', 'grad_x_w_kv': 'grad_w', 'grad_x_q_norm_g': 'grad_w', 'grad_x_k_norm_g': 'grad_w', 'grad_x_w_o': 'grad_w', 'grad_w_out': 'grad_w', 'delta_norm_g': 'delta_w', 'delta_mem_norm_g': 'delta_w', 'delta_w_in': 'delta_w', 'delta_gate_b': 'delta_w', 'delta_attn_q_norm_g': 'delta_w', 'delta_attn_k_norm_g': 'delta_w', 'delta_attn_sink': 'delta_w', 'delta_attn_w_o': 'delta_w', 'delta_rwkv_mu': 'delta_w', 'delta_rwkv_k_k': 'delta_w', 'delta_rwkv_k_a': 'delta_w', 'delta_rwkv_r_k': 'delta_w', 'delta_rwkv_w0': 'delta_w', 'delta_rwkv_w2': 'delta_w', 'delta_rwkv_a0': 'delta_w', 'delta_rwkv_a2': 'delta_w', 'delta_rwkv_ln_w': 'delta_w', 'delta_rwkv_ln_b': 'delta_w', 'delta_rwkv_w_o': 'delta_w', 'delta_x_w_kv': 'delta_w', 'delta_x_q_norm_g': 'delta_w', 'delta_x_k_norm_g': 'delta_w', 'delta_x_w_o': 'delta_w', 'delta_w_out': 'delta_w', 'new_m_norm_g': 'new_m', 'new_m_mem_norm_g': 'new_m', 'new_m_w_in': 'new_m', 'new_m_gate_b': 'new_m', 'new_m_attn_q_norm_g': 'new_m', 'new_m_attn_k_norm_g': 'new_m', 'new_m_attn_sink': 'new_m', 'new_m_attn_w_o': 'new_m', 'new_m_rwkv_mu': 'new_m', 'new_m_rwkv_k_k': 'new_m', 'new_m_rwkv_k_a': 'new_m', 'new_m_rwkv_r_k': 'new_m', 'new_m_rwkv_w0': 'new_m', 'new_m_rwkv_w2': 'new_m', 'new_m_rwkv_a0': 'new_m', 'new_m_rwkv_a2': 'new_m', 'new_m_rwkv_ln_w': 'new_m', 'new_m_rwkv_ln_b': 'new_m', 'new_m_rwkv_w_o': 'new_m', 'new_m_x_w_kv': 'new_m', 'new_m_x_q_norm_g': 'new_m', 'new_m_x_k_norm_g': 'new_m', 'new_m_x_w_o': 'new_m', 'new_m_w_out': 'new_m', 'new_v_norm_g': 'new_v', 'new_v_mem_norm_g': 'new_v', 'new_v_w_in': 'new_v', 'new_v_gate_b': 'new_v', 'new_v_attn_q_norm_g': 'new_v', 'new_v_attn_k_norm_g': 'new_v', 'new_v_attn_sink': 'new_v', 'new_v_attn_w_o': 'new_v', 'new_v_rwkv_mu': 'new_v', 'new_v_rwkv_k_k': 'new_v', 'new_v_rwkv_k_a': 'new_v', 'new_v_rwkv_r_k': 'new_v', 'new_v_rwkv_w0': 'new_v', 'new_v_rwkv_w2': 'new_v', 'new_v_rwkv_a0': 'new_v', 'new_v_rwkv_a2': 'new_v', 'new_v_rwkv_ln_w': 'new_v', 'new_v_rwkv_ln_b': 'new_v', 'new_v_rwkv_w_o': 'new_v', 'new_v_x_w_kv': 'new_v', 'new_v_x_q_norm_g': 'new_v', 'new_v_x_k_norm_g': 'new_v', 'new_v_x_w_o': 'new_v', 'new_v_w_out': 'new_v'}


def _forward(args):
    return _fwd_reference(*[args[k] for k in FWD_PARAMS])


def _output_shape():
    out = _jax.eval_shape(lambda: _forward(_fwd_setup_inputs(0)))
    return out.shape, out.dtype

N_MICROBATCH = 1
ADAM_LR = 0.001
ADAM_B1 = 0.9
ADAM_B2 = 0.999
ADAM_EPS = 1e-08
ADAM_WD = 0.01
ADAM_STEP = 10
PER_EXAMPLE_BATCH_AXIS = {'x': 0, 'mem': 0, 'loss_target': 0}
SHARED_INPUTS = []
_WEIGHT_DTYPES = {'norm_g': _jnp.float32, 'mem_norm_g': _jnp.float32, 'w_in': _jnp.float32, 'gate_b': _jnp.float32, 'attn_q_norm_g': _jnp.float32, 'attn_k_norm_g': _jnp.float32, 'attn_sink': _jnp.float32, 'attn_w_o': _jnp.float32, 'rwkv_mu': _jnp.float32, 'rwkv_k_k': _jnp.float32, 'rwkv_k_a': _jnp.float32, 'rwkv_r_k': _jnp.float32, 'rwkv_w0': _jnp.float32, 'rwkv_w2': _jnp.float32, 'rwkv_a0': _jnp.float32, 'rwkv_a2': _jnp.float32, 'rwkv_ln_w': _jnp.float32, 'rwkv_ln_b': _jnp.float32, 'rwkv_w_o': _jnp.float32, 'x_w_kv': _jnp.float32, 'x_q_norm_g': _jnp.float32, 'x_k_norm_g': _jnp.float32, 'x_w_o': _jnp.float32, 'w_out': _jnp.float32}
MOMENT_SCALE = {'norm_g': 1.392763e+00, 'mem_norm_g': 8.443034e-03, 'w_in': 4.108639e-02, 'gate_b': 1.852417e-01, 'attn_q_norm_g': 1.970713e-01, 'attn_k_norm_g': 1.972459e-01, 'attn_sink': 2.382334e-03, 'attn_w_o': 3.560163e-03, 'rwkv_mu': 3.541632e-01, 'rwkv_k_k': 5.113602e-02, 'rwkv_k_a': 5.332725e-01, 'rwkv_r_k': 2.847147e+00, 'rwkv_w0': 2.313314e-02, 'rwkv_w2': 3.060563e-03, 'rwkv_a0': 1.282450e-01, 'rwkv_a2': 1.679383e-02, 'rwkv_ln_w': 2.298001e+00, 'rwkv_ln_b': 1.011673e-01, 'rwkv_w_o': 4.512513e-02, 'x_w_kv': 7.098297e-03, 'x_q_norm_g': 1.120385e-01, 'x_k_norm_g': 1.121777e-01, 'x_w_o': 3.284955e-03, 'w_out': 3.963889e-02}


def _to_microbatches(a, axis):
    t = _jnp.moveaxis(a, axis, 0)
    t = t.reshape((N_MICROBATCH, t.shape[0] // N_MICROBATCH) + t.shape[1:])
    return _jnp.moveaxis(t, 1, axis + 1)


def setup_inputs(seed: int = 0) -> dict:
    inp = _fwd_setup_inputs(seed)
    key = _jax.random.fold_in(_jax.random.key(seed), 7919)
    shape, _ = _output_shape()
    out = dict(inp)
    out["loss_target"] = _jax.random.normal(_jax.random.fold_in(key, 0), shape, _jnp.float32)
    for i, name in enumerate(TWIN_WEIGHTS):
        w = inp[name].astype(_jnp.float32)
        if MOMENT_SCALE is None:
            s = _jnp.sqrt(_jnp.mean(_jnp.square(w)) + 1e-30)
        else:
            s = MOMENT_SCALE[name]
        km, kv = _jax.random.split(_jax.random.fold_in(key, i + 1))
        out[name] = w
        out["m_" + name] = s * _jax.random.normal(km, w.shape, _jnp.float32)
        out["v_" + name] = (s * s) * _jax.random.uniform(kv, w.shape, _jnp.float32, 0.5, 1.5)
    if N_MICROBATCH > 1:
        for name, axis in PER_EXAMPLE_BATCH_AXIS.items():
            out[name] = _to_microbatches(out[name], axis)
    return {'x': out['x'], 'mem': out['mem'], 'norm_g': out['norm_g'], 'mem_norm_g': out['mem_norm_g'], 'w_in': out['w_in'], 'gate_b': out['gate_b'], 'attn_q_norm_g': out['attn_q_norm_g'], 'attn_k_norm_g': out['attn_k_norm_g'], 'attn_sink': out['attn_sink'], 'attn_w_o': out['attn_w_o'], 'rwkv_mu': out['rwkv_mu'], 'rwkv_k_k': out['rwkv_k_k'], 'rwkv_k_a': out['rwkv_k_a'], 'rwkv_r_k': out['rwkv_r_k'], 'rwkv_w0': out['rwkv_w0'], 'rwkv_w2': out['rwkv_w2'], 'rwkv_a0': out['rwkv_a0'], 'rwkv_a2': out['rwkv_a2'], 'rwkv_ln_w': out['rwkv_ln_w'], 'rwkv_ln_b': out['rwkv_ln_b'], 'rwkv_w_o': out['rwkv_w_o'], 'x_w_kv': out['x_w_kv'], 'x_q_norm_g': out['x_q_norm_g'], 'x_k_norm_g': out['x_k_norm_g'], 'x_w_o': out['x_w_o'], 'w_out': out['w_out'], 'loss_target': out['loss_target'], 'm_norm_g': out['m_norm_g'], 'm_mem_norm_g': out['m_mem_norm_g'], 'm_w_in': out['m_w_in'], 'm_gate_b': out['m_gate_b'], 'm_attn_q_norm_g': out['m_attn_q_norm_g'], 'm_attn_k_norm_g': out['m_attn_k_norm_g'], 'm_attn_sink': out['m_attn_sink'], 'm_attn_w_o': out['m_attn_w_o'], 'm_rwkv_mu': out['m_rwkv_mu'], 'm_rwkv_k_k': out['m_rwkv_k_k'], 'm_rwkv_k_a': out['m_rwkv_k_a'], 'm_rwkv_r_k': out['m_rwkv_r_k'], 'm_rwkv_w0': out['m_rwkv_w0'], 'm_rwkv_w2': out['m_rwkv_w2'], 'm_rwkv_a0': out['m_rwkv_a0'], 'm_rwkv_a2': out['m_rwkv_a2'], 'm_rwkv_ln_w': out['m_rwkv_ln_w'], 'm_rwkv_ln_b': out['m_rwkv_ln_b'], 'm_rwkv_w_o': out['m_rwkv_w_o'], 'm_x_w_kv': out['m_x_w_kv'], 'm_x_q_norm_g': out['m_x_q_norm_g'], 'm_x_k_norm_g': out['m_x_k_norm_g'], 'm_x_w_o': out['m_x_w_o'], 'm_w_out': out['m_w_out'], 'v_norm_g': out['v_norm_g'], 'v_mem_norm_g': out['v_mem_norm_g'], 'v_w_in': out['v_w_in'], 'v_gate_b': out['v_gate_b'], 'v_attn_q_norm_g': out['v_attn_q_norm_g'], 'v_attn_k_norm_g': out['v_attn_k_norm_g'], 'v_attn_sink': out['v_attn_sink'], 'v_attn_w_o': out['v_attn_w_o'], 'v_rwkv_mu': out['v_rwkv_mu'], 'v_rwkv_k_k': out['v_rwkv_k_k'], 'v_rwkv_k_a': out['v_rwkv_k_a'], 'v_rwkv_r_k': out['v_rwkv_r_k'], 'v_rwkv_w0': out['v_rwkv_w0'], 'v_rwkv_w2': out['v_rwkv_w2'], 'v_rwkv_a0': out['v_rwkv_a0'], 'v_rwkv_a2': out['v_rwkv_a2'], 'v_rwkv_ln_w': out['v_rwkv_ln_w'], 'v_rwkv_ln_b': out['v_rwkv_ln_b'], 'v_rwkv_w_o': out['v_rwkv_w_o'], 'v_x_w_kv': out['v_x_w_kv'], 'v_x_q_norm_g': out['v_x_q_norm_g'], 'v_x_k_norm_g': out['v_x_k_norm_g'], 'v_x_w_o': out['v_x_w_o'], 'v_w_out': out['v_w_out']}


def _loss(weights, diff, rest, loss_target):
    with _jax.named_scope("forward"):
        args = {**rest, TWIN_DIFF_INPUT: diff, **{k: w.astype(_WEIGHT_DTYPES[k]) for k, w in weights.items()}}
        y = _forward(args)
    with _jax.named_scope("loss_head"):
        err = _jnp.square(y.astype(_jnp.float32) - loss_target)
        return 0.5 * _jnp.sum(_jnp.mean(err, axis=-1)) if err.ndim else 0.5 * err


def _adamw(w, g, m, v):
    m = ADAM_B1 * m + (1.0 - ADAM_B1) * g
    v = ADAM_B2 * v + (1.0 - ADAM_B2) * _jnp.square(g)
    m_hat = m / (1.0 - ADAM_B1 ** ADAM_STEP)
    v_hat = v / (1.0 - ADAM_B2 ** ADAM_STEP)
    delta = -ADAM_LR * (m_hat / (_jnp.sqrt(v_hat) + ADAM_EPS) + ADAM_WD * w)
    return delta, m, v


def reference(x, mem, norm_g, mem_norm_g, w_in, gate_b, attn_q_norm_g, attn_k_norm_g, attn_sink, attn_w_o, rwkv_mu, rwkv_k_k, rwkv_k_a, rwkv_r_k, rwkv_w0, rwkv_w2, rwkv_a0, rwkv_a2, rwkv_ln_w, rwkv_ln_b, rwkv_w_o, x_w_kv, x_q_norm_g, x_k_norm_g, x_w_o, w_out, loss_target, m_norm_g, m_mem_norm_g, m_w_in, m_gate_b, m_attn_q_norm_g, m_attn_k_norm_g, m_attn_sink, m_attn_w_o, m_rwkv_mu, m_rwkv_k_k, m_rwkv_k_a, m_rwkv_r_k, m_rwkv_w0, m_rwkv_w2, m_rwkv_a0, m_rwkv_a2, m_rwkv_ln_w, m_rwkv_ln_b, m_rwkv_w_o, m_x_w_kv, m_x_q_norm_g, m_x_k_norm_g, m_x_w_o, m_w_out, v_norm_g, v_mem_norm_g, v_w_in, v_gate_b, v_attn_q_norm_g, v_attn_k_norm_g, v_attn_sink, v_attn_w_o, v_rwkv_mu, v_rwkv_k_k, v_rwkv_k_a, v_rwkv_r_k, v_rwkv_w0, v_rwkv_w2, v_rwkv_a0, v_rwkv_a2, v_rwkv_ln_w, v_rwkv_ln_b, v_rwkv_w_o, v_x_w_kv, v_x_q_norm_g, v_x_k_norm_g, v_x_w_o, v_w_out):
    given = dict(x=x, mem=mem, norm_g=norm_g, mem_norm_g=mem_norm_g, w_in=w_in, gate_b=gate_b, attn_q_norm_g=attn_q_norm_g, attn_k_norm_g=attn_k_norm_g, attn_sink=attn_sink, attn_w_o=attn_w_o, rwkv_mu=rwkv_mu, rwkv_k_k=rwkv_k_k, rwkv_k_a=rwkv_k_a, rwkv_r_k=rwkv_r_k, rwkv_w0=rwkv_w0, rwkv_w2=rwkv_w2, rwkv_a0=rwkv_a0, rwkv_a2=rwkv_a2, rwkv_ln_w=rwkv_ln_w, rwkv_ln_b=rwkv_ln_b, rwkv_w_o=rwkv_w_o, x_w_kv=x_w_kv, x_q_norm_g=x_q_norm_g, x_k_norm_g=x_k_norm_g, x_w_o=x_w_o, w_out=w_out, loss_target=loss_target, m_norm_g=m_norm_g, m_mem_norm_g=m_mem_norm_g, m_w_in=m_w_in, m_gate_b=m_gate_b, m_attn_q_norm_g=m_attn_q_norm_g, m_attn_k_norm_g=m_attn_k_norm_g, m_attn_sink=m_attn_sink, m_attn_w_o=m_attn_w_o, m_rwkv_mu=m_rwkv_mu, m_rwkv_k_k=m_rwkv_k_k, m_rwkv_k_a=m_rwkv_k_a, m_rwkv_r_k=m_rwkv_r_k, m_rwkv_w0=m_rwkv_w0, m_rwkv_w2=m_rwkv_w2, m_rwkv_a0=m_rwkv_a0, m_rwkv_a2=m_rwkv_a2, m_rwkv_ln_w=m_rwkv_ln_w, m_rwkv_ln_b=m_rwkv_ln_b, m_rwkv_w_o=m_rwkv_w_o, m_x_w_kv=m_x_w_kv, m_x_q_norm_g=m_x_q_norm_g, m_x_k_norm_g=m_x_k_norm_g, m_x_w_o=m_x_w_o, m_w_out=m_w_out, v_norm_g=v_norm_g, v_mem_norm_g=v_mem_norm_g, v_w_in=v_w_in, v_gate_b=v_gate_b, v_attn_q_norm_g=v_attn_q_norm_g, v_attn_k_norm_g=v_attn_k_norm_g, v_attn_sink=v_attn_sink, v_attn_w_o=v_attn_w_o, v_rwkv_mu=v_rwkv_mu, v_rwkv_k_k=v_rwkv_k_k, v_rwkv_k_a=v_rwkv_k_a, v_rwkv_r_k=v_rwkv_r_k, v_rwkv_w0=v_rwkv_w0, v_rwkv_w2=v_rwkv_w2, v_rwkv_a0=v_rwkv_a0, v_rwkv_a2=v_rwkv_a2, v_rwkv_ln_w=v_rwkv_ln_w, v_rwkv_ln_b=v_rwkv_ln_b, v_rwkv_w_o=v_rwkv_w_o, v_x_w_kv=v_x_w_kv, v_x_q_norm_g=v_x_q_norm_g, v_x_k_norm_g=v_x_k_norm_g, v_x_w_o=v_x_w_o, v_w_out=v_w_out)
    weights = {n: given[n] for n in TWIN_WEIGHTS}
    shared = {n: given[n] for n in SHARED_INPUTS}
    per_example = {n: given[n] for n in ['x', 'mem']}
    grad_fn = _jax.value_and_grad(_loss, argnums=(0, 1))

    def one_microbatch(ex, loss_target):
        ex = dict(ex)
        diff = ex.pop(TWIN_DIFF_INPUT)
        return grad_fn(weights, diff, {**shared, **ex}, loss_target)

    if N_MICROBATCH == 1:
        loss, (grad_w, grad_x) = one_microbatch(per_example, given["loss_target"])
    else:
        def body(carry, xs):
            loss_sum, grad_sum = carry
            l_k, (gw_k, gx_k) = one_microbatch(xs[0], xs[1])
            with _jax.named_scope("update"):
                return (loss_sum + l_k, _jax.tree.map(_jnp.add, grad_sum, gw_k)), gx_k

        init = (_jnp.zeros((), _jnp.float32), _jax.tree.map(_jnp.zeros_like, weights))
        (loss, grad_w), grad_x = _jax.lax.scan(body, init, (per_example, given["loss_target"]))
    with _jax.named_scope("update"):
        delta_w, new_m, new_v = {}, {}, {}
        for n in TWIN_WEIGHTS:
            delta_w[n], new_m[n], new_v[n] = _adamw(weights[n], grad_w[n], given["m_" + n], given["v_" + n])
    return (loss, grad_x, *[grad_w[n] for n in TWIN_WEIGHTS], *[delta_w[n] for n in TWIN_WEIGHTS],
            *[new_m[n] for n in TWIN_WEIGHTS], *[new_v[n] for n in TWIN_WEIGHTS])
```

```python
import functools

import jax
import jax.numpy as jnp
from jax import lax
from jax.experimental import pallas as pl
from jax.experimental.pallas import tpu as pltpu

F32 = jnp.float32
BF16 = jnp.bfloat16
HI = lax.Precision.HIGHEST
MESH = pl.DeviceIdType.MESH

D = 2048
NMEM = 256
NORM_EPS = 1e-6
NEG_INF = -1e30
GN_EPS = 64e-5
HD = 64
AH = 12
AKV = 4
RW = 768
XH = 4
XD = 128
XW = 512
NIN = 12544
RSW = 2560
C_AQ, C_AK, C_AV, C_AG, C_RS, C_RG, C_XQ, C_XG, C_MG = 0, 768, 1024, 1280, 2048, 4608, 5376, 5888, 6400
WIN = 384
QB = 128
TC = 64
NPAIR = 6

ADAM_LR, ADAM_B1, ADAM_B2, ADAM_EPS, ADAM_WD, ADAM_STEP = 0.001, 0.9, 0.999, 1e-08, 0.01, 10

VMEM_LIMIT = 56 * 1024 * 1024


def _bs(shape, imap):
    return pl.BlockSpec(shape, imap)


def _params(sem=None, vmem=VMEM_LIMIT):
    return pltpu.CompilerParams(dimension_semantics=sem, vmem_limit_bytes=vmem)


def _dot(a, b, dims):
    return lax.dot_general(a.astype(BF16), b.astype(BF16), (dims, ((), ())), preferred_element_type=F32)


@jax.custom_vjp
def _mm_nn(a, b):
    return _dot(a, b, ((1,), (0,)))


def _mm_nn_fwd(a, b):
    return _mm_nn(a, b), (a, b)


def _mm_nn_bwd(res, ct):
    a, b = res
    return _dot(ct, b, ((1,), (1,))), _dot(a, ct, ((0,), (0,)))


_mm_nn.defvjp(_mm_nn_fwd, _mm_nn_bwd)


@jax.custom_vjp
def _mm_nt(a, b):
    return _dot(a, b, ((1,), (1,)))


def _mm_nt_fwd(a, b):
    return _mm_nt(a, b), (a, b)


def _mm_nt_bwd(res, ct):
    a, b = res
    return _dot(ct, b, ((1,), (0,))), _dot(ct, a, ((0,), (0,)))


_mm_nt.defvjp(_mm_nt_fwd, _mm_nt_bwd)


def _seg_matrix(n, seg):
    r = lax.broadcasted_iota(jnp.int32, (n, n), 0) // seg
    c = lax.broadcasted_iota(jnp.int32, (n, n), 1) // seg
    return (r == c).astype(F32)


def _rot_matrix():
    r = lax.broadcasted_iota(jnp.int32, (HD, HD), 0)
    c = lax.broadcasted_iota(jnp.int32, (HD, HD), 1)
    return jnp.where(c == r + HD // 2, 1.0, 0.0).astype(F32) - jnp.where(c == r - HD // 2, 1.0, 0.0).astype(F32)


def _hdot(a, m):
    return jnp.dot(a, m, precision=HI, preferred_element_type=F32)


def _rms(t, g):
    return t * lax.rsqrt(jnp.mean(t * t, axis=-1, keepdims=True) + NORM_EPS) * g


def _silu(t):
    return t * jax.nn.sigmoid(t)


def _softplus(z):
    return jnp.maximum(z, 0.0) + jnp.log(1.0 + jnp.exp(-jnp.abs(z)))


def _matmul(a, b, *, mode, m, n, k, tm, tn, tk, name, a_off=(0, 0), b_off=(0, 0), out_dtype=F32):
    nk = k // tk
    if mode == "tn":
        a_spec = _bs((tk, tm), lambda i, j, kk: (kk + a_off[0], i + a_off[1]))
        dims = ((0,), (0,))
    else:
        a_spec = _bs((tm, tk), lambda i, j, kk: (i + a_off[0], kk + a_off[1]))
        dims = ((1,), (1,)) if mode == "nt" else ((1,), (0,))
    if mode == "nt":
        b_spec = _bs((tn, tk), lambda i, j, kk: (j + b_off[0], kk + b_off[1]))
    else:
        b_spec = _bs((tk, tn), lambda i, j, kk: (kk + b_off[0], j + b_off[1]))

    def body(a_ref, b_ref, o_ref, acc):
        kk = pl.program_id(2)

        @pl.when(kk == 0)
        def _():
            acc[...] = jnp.zeros_like(acc)

        acc[...] += _dot(a_ref[...], b_ref[...], dims)

        @pl.when(kk == nk - 1)
        def _():
            o_ref[...] = acc[...].astype(out_dtype)

    return pl.pallas_call(
        body, name=name, grid=(m // tm, n // tn, nk),
        in_specs=[a_spec, b_spec], out_specs=_bs((tm, tn), lambda i, j, kk: (i, j)),
        out_shape=jax.ShapeDtypeStruct((m, n), out_dtype),
        scratch_shapes=[pltpu.VMEM((tm, tn), F32)],
        compiler_params=_params(("parallel", "parallel", "arbitrary")),
    )(a, b)


def _proj_fwd(x, g, w):
    s = x.shape[0]
    tm, tn = min(512, s), 896

    def body(x_ref, g_ref, w_ref, o_ref, h_ref, hs):
        @pl.when(pl.program_id(1) == 0)
        def _():
            h = _rms(x_ref[...], g_ref[...]).astype(BF16)
            hs[...] = h
            h_ref[...] = h

        o_ref[...] = jnp.dot(hs[...], w_ref[...], preferred_element_type=F32)

    return pl.pallas_call(
        body, name="proj_fwd", grid=(s // tm, NIN // tn),
        in_specs=[_bs((tm, D), lambda i, j: (i, 0)), _bs((1, D), lambda i, j: (0, 0)), _bs((D, tn), lambda i, j: (0, j))],
        out_specs=[_bs((tm, tn), lambda i, j: (i, j)), _bs((tm, D), lambda i, j: (i, 0))],
        out_shape=[jax.ShapeDtypeStruct((s, NIN), F32), jax.ShapeDtypeStruct((s, D), BF16)],
        scratch_shapes=[pltpu.VMEM((tm, D), BF16)],
        compiler_params=_params(("parallel", "arbitrary")),
    )(x, g, w)


def _rope(t, cos, sin, rot):
    return t * cos + _hdot(t, rot) * sin


def _attn_tile(qs, ks, vs, gs, sinks, gq, gk, cq, sq, ck, sk, mask, rot):
    outs = []
    for hk in range(AKV):
        kh = _rope(_rms(ks[hk], gk), ck, sk, rot)
        for g in range(AH // AKV):
            h = hk * (AH // AKV) + g
            qh = _rope(_rms(qs[h], gq), cq, sq, rot)
            sc = _mm_nt(qh, kh) * (HD ** -0.5)
            sc = jnp.where(mask, sc, NEG_INF)
            mx = lax.stop_gradient(jnp.maximum(jnp.max(sc, axis=-1, keepdims=True), sinks[h]))
            p = jnp.exp(sc - mx)
            den = jnp.sum(p, axis=-1, keepdims=True) + jnp.exp(sinks[h] - mx)
            o = _mm_nn(p / den, vs[hk])
            outs.append(o * _silu(gs[h]))
    return outs


def _attn_load(n, s, aq_ref, ak_ref, av_ref, ag_refs, cos_ref, sin_ref, sink_ref):
    start = pl.multiple_of(jnp.clip((n - 1) * QB, 0, s - WIN), QB)
    q0 = pl.multiple_of(n * QB, QB)
    qs = [aq_ref[:, h * HD:(h + 1) * HD] for h in range(AH)]
    ks = [ak_ref[pl.ds(start, WIN), h * HD:(h + 1) * HD] for h in range(AKV)]
    vs = [av_ref[pl.ds(start, WIN), h * HD:(h + 1) * HD] for h in range(AKV)]
    gs = [ag_refs[h // 4][:, (h % 4) * HD:(h % 4 + 1) * HD] for h in range(AH)]
    sinks = [sink_ref[0:1, h:h + 1] for h in range(AH)]
    cq, sq = cos_ref[pl.ds(q0, QB), :], sin_ref[pl.ds(q0, QB), :]
    ck, sk = cos_ref[pl.ds(start, WIN), :], sin_ref[pl.ds(start, WIN), :]
    qpos = q0 + lax.broadcasted_iota(jnp.int32, (QB, WIN), 0)
    kpos = start + lax.broadcasted_iota(jnp.int32, (QB, WIN), 1)
    mask = jnp.abs(kpos - qpos) <= QB
    return start, qs, ks, vs, gs, sinks, cq, sq, ck, sk, mask


def _attn_specs(s):
    return [
        _bs((QB, 768), lambda n: (n, 0)),
        _bs((s, 256), lambda n: (0, C_AK // 256)),
        _bs((s, 256), lambda n: (0, C_AV // 256)),
        _bs((QB, 256), lambda n: (n, C_AG // 256)),
        _bs((QB, 256), lambda n: (n, C_AG // 256 + 1)),
        _bs((QB, 256), lambda n: (n, C_AG // 256 + 2)),
        _bs((s, HD), lambda n: (0, 0)),
        _bs((s, HD), lambda n: (0, 0)),
        _bs((1, HD), lambda n: (0, 0)),
        _bs((1, HD), lambda n: (0, 0)),
        _bs((1, AH), lambda n: (0, 0)),
    ]


def _attn_fwd(proj, cos, sin, gq, gk, sink):
    s = proj.shape[0]

    def body(aq_ref, ak_ref, av_ref, ag0, ag1, ag2, cos_ref, sin_ref, gq_ref, gk_ref, sink_ref, o_ref):
        n = pl.program_id(0)
        _, qs, ks, vs, gs, sinks, cq, sq, ck, sk, mask = _attn_load(
            n, s, aq_ref, ak_ref, av_ref, (ag0, ag1, ag2), cos_ref, sin_ref, sink_ref)
        outs = _attn_tile(qs, ks, vs, gs, sinks, gq_ref[...], gk_ref[...], cq, sq, ck, sk, mask, _rot_matrix())
        for h in range(AH):
            o_ref[:, h * HD:(h + 1) * HD] = outs[h]

    return pl.pallas_call(
        body, name="attn_fwd", grid=(s // QB,),
        in_specs=_attn_specs(s), out_specs=_bs((QB, 768), lambda n: (n, 0)),
        out_shape=jax.ShapeDtypeStruct((s, 768), F32),
        compiler_params=_params(("arbitrary",)),
    )(proj, proj, proj, proj, proj, proj, cos, sin, gq, gk, sink)


def _attn_bwd(proj, cos, sin, gq, gk, sink, dy):
    s = proj.shape[0]

    def body(aq_ref, ak_ref, av_ref, ag0, ag1, ag2, cos_ref, sin_ref, gq_ref, gk_ref, sink_ref, dy_ref,
             daq_ref, dak_ref, dav_ref, dag_ref, dgq_ref, dgk_ref, dsink_ref):
        n = pl.program_id(0)

        @pl.when(n == 0)
        def _():
            dak_ref[...] = jnp.zeros_like(dak_ref)
            dav_ref[...] = jnp.zeros_like(dav_ref)
            dgq_ref[...] = jnp.zeros_like(dgq_ref)
            dgk_ref[...] = jnp.zeros_like(dgk_ref)
            dsink_ref[...] = jnp.zeros_like(dsink_ref)

        start, qs, ks, vs, gs, sinks, cq, sq, ck, sk, mask = _attn_load(
            n, s, aq_ref, ak_ref, av_ref, (ag0, ag1, ag2), cos_ref, sin_ref, sink_ref)
        rot = _rot_matrix()

        def f(qs, ks, vs, gs, sinks, gq, gk):
            return _attn_tile(qs, ks, vs, gs, sinks, gq, gk, cq, sq, ck, sk, mask, rot)

        _, vjp = jax.vjp(f, qs, ks, vs, gs, sinks, gq_ref[...], gk_ref[...])
        dys = [dy_ref[:, h * HD:(h + 1) * HD] for h in range(AH)]
        dqs, dks, dvs, dgs, dsinks, dgq, dgk = vjp(dys)
        for h in range(AH):
            daq_ref[:, h * HD:(h + 1) * HD] = dqs[h]
            dag_ref[:, h * HD:(h + 1) * HD] = dgs[h]
            dsink_ref[0:1, h:h + 1] += dsinks[h]
        for h in range(AKV):
            dak_ref[pl.ds(start, WIN), h * HD:(h + 1) * HD] += dks[h]
            dav_ref[pl.ds(start, WIN), h * HD:(h + 1) * HD] += dvs[h]
        dgq_ref[...] += dgq
        dgk_ref[...] += dgk

    whole = lambda shape: _bs(shape, lambda n: (0, 0))
    return pl.pallas_call(
        body, name="attn_bwd", grid=(s // QB,),
        in_specs=_attn_specs(s) + [_bs((QB, 768), lambda n: (n, 0))],
        out_specs=[_bs((QB, 768), lambda n: (n, 0)), whole((s, 256)), whole((s, 256)), _bs((QB, 768), lambda n: (n, 0)),
                   whole((1, HD)), whole((1, HD)), whole((1, AH))],
        out_shape=[jax.ShapeDtypeStruct((s, 768), F32), jax.ShapeDtypeStruct((s, 256), F32),
                   jax.ShapeDtypeStruct((s, 256), F32), jax.ShapeDtypeStruct((s, 768), F32),
                   jax.ShapeDtypeStruct((1, HD), F32), jax.ShapeDtypeStruct((1, HD), F32),
                   jax.ShapeDtypeStruct((1, AH), F32)],
        compiler_params=_params(("arbitrary",)),
    )(proj, proj, proj, proj, proj, proj, cos, sin, gq, gk, sink, dy)


def _mem_kv(mem, g, w):
    def body(m_ref, g_ref, w_ref, o_ref, mn_ref):
        mn = _rms(m_ref[...], g_ref[...]).astype(BF16)
        mn_ref[...] = mn
        o_ref[...] = jnp.dot(mn, w_ref[...], preferred_element_type=F32)

    return pl.pallas_call(
        body, name="mem_kv",
        out_shape=[jax.ShapeDtypeStruct((NMEM, 2 * XW), F32), jax.ShapeDtypeStruct((NMEM, D), BF16)],
        compiler_params=_params(),
    )(mem, g, w)


def _xattn_tile(qs, gs, kms, vms, gxq, gxk):
    outs = []
    for h in range(XH):
        q = _rms(qs[h], gxq)
        km = _rms(kms[h], gxk)
        sc = _mm_nt(q, km) * (XD ** -0.5)
        mx = lax.stop_gradient(jnp.max(sc, axis=-1, keepdims=True))
        p = jnp.exp(sc - mx)
        p = p / jnp.sum(p, axis=-1, keepdims=True)
        outs.append(_mm_nn(p, vms[h]) * _silu(gs[h]))
    return outs


XT = 256


def _xattn_specs():
    return [
        _bs((XT, 256), lambda i: (i, C_XQ // 256)), _bs((XT, 256), lambda i: (i, C_XQ // 256 + 1)),
        _bs((XT, 256), lambda i: (i, C_XG // 256)), _bs((XT, 256), lambda i: (i, C_XG // 256 + 1)),
        _bs((NMEM, 2 * XW), lambda i: (0, 0)),
        _bs((1, XD), lambda i: (0, 0)), _bs((1, XD), lambda i: (0, 0)),
    ]


def _xattn_load(q0, q1, g0, g1, mkv_ref):
    qs = [(q0, q1)[h // 2][:, (h % 2) * XD:(h % 2 + 1) * XD] for h in range(XH)]
    gs = [(g0, g1)[h // 2][:, (h % 2) * XD:(h % 2 + 1) * XD] for h in range(XH)]
    kms = [mkv_ref[:, h * XD:(h + 1) * XD] for h in range(XH)]
    vms = [mkv_ref[:, XW + h * XD:XW + (h + 1) * XD] for h in range(XH)]
    return qs, gs, kms, vms


def _xattn_fwd(proj, mkv, gxq, gxk):
    s = proj.shape[0]

    def body(q0, q1, g0, g1, mkv_ref, gxq_ref, gxk_ref, o_ref):
        qs, gs, kms, vms = _xattn_load(q0, q1, g0, g1, mkv_ref)
        outs = _xattn_tile(qs, gs, kms, vms, gxq_ref[...], gxk_ref[...])
        for h in range(XH):
            o_ref[:, h * XD:(h + 1) * XD] = outs[h]

    return pl.pallas_call(
        body, name="xattn_fwd", grid=(s // XT,),
        in_specs=_xattn_specs(), out_specs=_bs((XT, XW), lambda i: (i, 0)),
        out_shape=jax.ShapeDtypeStruct((s, XW), F32),
        compiler_params=_params(("arbitrary",)),
    )(proj, proj, proj, proj, mkv, gxq, gxk)


def _xattn_bwd(proj, mkv, gxq, gxk, dy):
    s = proj.shape[0]

    def body(q0, q1, g0, g1, mkv_ref, gxq_ref, gxk_ref, dy_ref, dq_ref, dg_ref, dmkv_ref, dgxq_ref, dgxk_ref):
        @pl.when(pl.program_id(0) == 0)
        def _():
            dmkv_ref[...] = jnp.zeros_like(dmkv_ref)
            dgxq_ref[...] = jnp.zeros_like(dgxq_ref)
            dgxk_ref[...] = jnp.zeros_like(dgxk_ref)

        qs, gs, kms, vms = _xattn_load(q0, q1, g0, g1, mkv_ref)
        _, vjp = jax.vjp(_xattn_tile, qs, gs, kms, vms, gxq_ref[...], gxk_ref[...])
        dqs, dgs, dkms, dvms, dgxq, dgxk = vjp([dy_ref[:, h * XD:(h + 1) * XD] for h in range(XH)])
        for h in range(XH):
            dq_ref[:, h * XD:(h + 1) * XD] = dqs[h]
            dg_ref[:, h * XD:(h + 1) * XD] = dgs[h]
            dmkv_ref[:, h * XD:(h + 1) * XD] += dkms[h]
            dmkv_ref[:, XW + h * XD:XW + (h + 1) * XD] += dvms[h]
        dgxq_ref[...] += dgxq
        dgxk_ref[...] += dgxk

    whole = lambda shape: _bs(shape, lambda i: (0, 0))
    return pl.pallas_call(
        body, name="xattn_bwd", grid=(s // XT,),
        in_specs=_xattn_specs() + [_bs((XT, XW), lambda i: (i, 0))],
        out_specs=[_bs((XT, XW), lambda i: (i, 0)), _bs((XT, XW), lambda i: (i, 0)), whole((NMEM, 2 * XW)),
                   whole((1, XD)), whole((1, XD))],
        out_shape=[jax.ShapeDtypeStruct((s, XW), F32), jax.ShapeDtypeStruct((s, XW), F32),
                   jax.ShapeDtypeStruct((NMEM, 2 * XW), F32), jax.ShapeDtypeStruct((1, XD), F32),
                   jax.ShapeDtypeStruct((1, XD), F32)],
        compiler_params=_params(("arbitrary",)),
    )(proj, proj, proj, proj, mkv, gxq, gxk, dy)


def _mem_bwd(mem, g, dmn):
    def body(m_ref, dmn_ref, o_ref):
        m = m_ref[...]
        r = lax.rsqrt(jnp.mean(m * m, axis=-1, keepdims=True) + NORM_EPS)
        o_ref[...] = jnp.sum(dmn_ref[...] * m * r, axis=0, keepdims=True)

    del g
    return pl.pallas_call(body, name="mem_norm_bwd", out_shape=jax.ShapeDtypeStruct((1, D), F32),
                          compiler_params=_params())(mem, dmn)


SHIFT_W = 512


def _shift_rows(p, s):
    row = lax.broadcasted_iota(jnp.int32, p.shape, 0)
    prev = jnp.where(row == 0, 0.0, pltpu.roll(p, 1, 0))
    nxt = jnp.where(row == s - 1, 0.0, pltpu.roll(p, s - 1, 0))
    return prev, nxt


def _shift_fwd(proj, mu):
    s = proj.shape[0]

    def body(p_ref, mu_ref, o_ref):
        p = p_ref[...]
        prev, nxt = _shift_rows(p, s)
        o_ref[...] = p + mu_ref[...] * (0.5 * (prev + nxt) - p)

    return pl.pallas_call(
        body, name="shift_fwd", grid=(RSW // SHIFT_W,),
        in_specs=[_bs((s, SHIFT_W), lambda j: (0, C_RS // SHIFT_W + j)), _bs((1, SHIFT_W), lambda j: (0, j))],
        out_specs=_bs((s, SHIFT_W), lambda j: (0, j)),
        out_shape=jax.ShapeDtypeStruct((s, RSW), F32),
        compiler_params=_params(("parallel",)),
    )(proj, mu)


def _shift_bwd(proj, mu, dps):
    s = proj.shape[0]

    def body(p_ref, mu_ref, g_ref, o_ref, dmu_ref):
        p, g, mu_v = p_ref[...], g_ref[...], mu_ref[...]
        prev, nxt = _shift_rows(p, s)
        dmu_ref[...] = jnp.sum(g * (0.5 * (prev + nxt) - p), axis=0, keepdims=True)
        mg = mu_v * g
        down, up = _shift_rows(mg, s)
        o_ref[...] = g * (1.0 - mu_v) + 0.5 * (down + up)

    return pl.pallas_call(
        body, name="shift_bwd", grid=(RSW // SHIFT_W,),
        in_specs=[_bs((s, SHIFT_W), lambda j: (0, C_RS // SHIFT_W + j)), _bs((1, SHIFT_W), lambda j: (0, j)),
                  _bs((s, SHIFT_W), lambda j: (0, j))],
        out_specs=[_bs((s, SHIFT_W), lambda j: (0, j)), _bs((1, SHIFT_W), lambda j: (0, j))],
        out_shape=[jax.ShapeDtypeStruct((s, RSW), F32), jax.ShapeDtypeStruct((1, RSW), F32)],
        compiler_params=_params(("parallel",)),
    )(proj, mu, dps)


def _pre_tile(k, wf, wb, af, ab, k_k, k_a, w0s, w2s, a0s, a2s, seg):
    kx = k * k_k
    ss = _hdot(kx * kx, seg)
    kk = kx / jnp.maximum(jnp.sqrt(ss), 1e-12)
    outs = [kk]
    for d, (w_in, a_in) in enumerate(((wf, af), (wb, ab))):
        z = w0s[d] + _mm_nn(jnp.tanh(w_in), w2s[d])
        wd = -_softplus(-z) - 0.5
        dec = jnp.exp(-jnp.exp(wd))
        ad = jax.nn.sigmoid(a0s[d] + _mm_nn(a_in, a2s[d]))
        kd = k * (1.0 + (ad - 1.0) * k_a)
        outs += [dec, kd, kk * ad]
    return outs


PT = 256


def _pre_load(ps_ref, kk_ref, ka_ref, w0_ref, w2_ref, a0_ref, a2_ref):
    k = ps_ref[:, RW:2 * RW]
    wf, wb = ps_ref[:, 3 * RW:3 * RW + 64], ps_ref[:, 3 * RW + 64:3 * RW + 128]
    af, ab = ps_ref[:, 3 * RW + 128:3 * RW + 192], ps_ref[:, 3 * RW + 192:3 * RW + 256]
    w0s = [w0_ref[0:1, :], w0_ref[1:2, :]]
    a0s = [a0_ref[0:1, :], a0_ref[1:2, :]]
    w2s = [w2_ref[0], w2_ref[1]]
    a2s = [a2_ref[0], a2_ref[1]]
    return (k, wf, wb, af, ab, kk_ref[...], ka_ref[...], w0s, w2s, a0s, a2s)


def _pre_specs():
    c = lambda shape: _bs(shape, lambda i: tuple(0 for _ in shape))
    return [_bs((PT, RSW), lambda i: (i, 0)), c((1, RW)), c((1, RW)), c((2, RW)), c((2, 64, RW)), c((2, RW)),
            c((2, 64, RW))]


def _pre_fwd(ps, k_k, k_a, w0, w2, a0, a2):
    s = ps.shape[0]

    def body(ps_ref, kk_ref, ka_ref, w0_ref, w2_ref, a0_ref, a2_ref, *outs):
        args = _pre_load(ps_ref, kk_ref, ka_ref, w0_ref, w2_ref, a0_ref, a2_ref)
        res = _pre_tile(*args, _seg_matrix(RW, HD))
        for o_ref, v in zip(outs, res):
            o_ref[...] = v

    return pl.pallas_call(
        body, name="rwkv_pre_fwd", grid=(s // PT,),
        in_specs=_pre_specs(), out_specs=[_bs((PT, RW), lambda i: (i, 0))] * 7,
        out_shape=[jax.ShapeDtypeStruct((s, RW), F32)] * 7,
        compiler_params=_params(("parallel",)),
    )(ps, k_k, k_a, w0, w2, a0, a2)


def _pre_bwd(ps, k_k, k_a, w0, w2, a0, a2, dr, dv, cts):
    s = ps.shape[0]

    def body(ps_ref, kk_ref, ka_ref, w0_ref, w2_ref, a0_ref, a2_ref, dr_ref, dv_ref, c0, c1, c2, c3, c4, c5, c6,
             dps_ref, dkk_ref, dka_ref, dw0_ref, dw2_ref, da0_ref, da2_ref):
        @pl.when(pl.program_id(0) == 0)
        def _():
            for r in (dkk_ref, dka_ref, dw0_ref, dw2_ref, da0_ref, da2_ref):
                r[...] = jnp.zeros_like(r)

        args = _pre_load(ps_ref, kk_ref, ka_ref, w0_ref, w2_ref, a0_ref, a2_ref)
        seg = _seg_matrix(RW, HD)
        _, vjp = jax.vjp(lambda *a: _pre_tile(*a, seg), *args)
        dk, dwf, dwb, daf, dab, dk_k, dk_a, dw0s, dw2s, da0s, da2s = vjp([c[...] for c in (c0, c1, c2, c3, c4, c5, c6)])
        dps_ref[:, 0:RW] = dr_ref[...]
        dps_ref[:, RW:2 * RW] = dk
        dps_ref[:, 2 * RW:3 * RW] = dv_ref[...]
        for j, t in enumerate((dwf, dwb, daf, dab)):
            dps_ref[:, 3 * RW + 64 * j:3 * RW + 64 * (j + 1)] = t
        dkk_ref[...] += dk_k
        dka_ref[...] += dk_a
        for d in range(2):
            dw0_ref[d:d + 1, :] += dw0s[d]
            da0_ref[d:d + 1, :] += da0s[d]
            dw2_ref[d] += dw2s[d]
            da2_ref[d] += da2s[d]

    c = lambda shape: _bs(shape, lambda i: tuple(0 for _ in shape))
    row = _bs((PT, RW), lambda i: (i, 0))
    return pl.pallas_call(
        body, name="rwkv_pre_bwd", grid=(s // PT,),
        in_specs=_pre_specs() + [row] * 9,
        out_specs=[_bs((PT, RSW), lambda i: (i, 0)), c((1, RW)), c((1, RW)), c((2, RW)), c((2, 64, RW)), c((2, RW)),
                   c((2, 64, RW))],
        out_shape=[jax.ShapeDtypeStruct((s, RSW), F32), jax.ShapeDtypeStruct((1, RW), F32),
                   jax.ShapeDtypeStruct((1, RW), F32), jax.ShapeDtypeStruct((2, RW), F32),
                   jax.ShapeDtypeStruct((2, 64, RW), F32), jax.ShapeDtypeStruct((2, RW), F32),
                   jax.ShapeDtypeStruct((2, 64, RW), F32)],
        compiler_params=_params(("arbitrary",)),
    )(ps, k_k, k_a, w0, w2, a0, a2, dr, dv, *cts)


def _post_tile(y0, y1, r, v, kd0, kd1, rg, r_k, ln_w, ln_b, seg):
    ysum = y0 + y1
    bonus = (_hdot(r * kd0 * r_k, seg) + _hdot(r * kd1 * r_k, seg)) * v
    mean = _hdot(ysum, seg) * (1.0 / HD)
    cen = ysum - mean
    var = _hdot(cen * cen, seg) * (1.0 / HD)
    y = cen * lax.rsqrt(var + GN_EPS) * ln_w + ln_b + bonus
    return y * _silu(rg)


def _post_specs():
    row = _bs((PT, RW), lambda i: (i, 0))
    c = _bs((1, RW), lambda i: (0, 0))
    return [row, row, _bs((PT, RW), lambda i: (i, 0)), _bs((PT, RW), lambda i: (i, 2)), row, row,
            _bs((PT, RW), lambda i: (i, C_RG // RW)), c, c, c]


def _post_fwd(y0, y1, ps, kd0, kd1, proj, r_k, ln_w, ln_b):
    s = ps.shape[0]

    def body(y0_ref, y1_ref, r_ref, v_ref, kd0_ref, kd1_ref, rg_ref, rk_ref, lw_ref, lb_ref, o_ref):
        o_ref[...] = _post_tile(y0_ref[...], y1_ref[...], r_ref[...], v_ref[...], kd0_ref[...], kd1_ref[...],
                                rg_ref[...], rk_ref[...], lw_ref[...], lb_ref[...], _seg_matrix(RW, HD))

    return pl.pallas_call(
        body, name="rwkv_post_fwd", grid=(s // PT,),
        in_specs=_post_specs(), out_specs=_bs((PT, RW), lambda i: (i, 0)),
        out_shape=jax.ShapeDtypeStruct((s, RW), F32),
        compiler_params=_params(("parallel",)),
    )(y0, y1, ps, ps, kd0, kd1, proj, r_k, ln_w, ln_b)


def _post_bwd(y0, y1, ps, kd0, kd1, proj, r_k, ln_w, ln_b, dy):
    s = ps.shape[0]

    def body(y0_ref, y1_ref, r_ref, v_ref, kd0_ref, kd1_ref, rg_ref, rk_ref, lw_ref, lb_ref, dy_ref,
             dys_ref, dr_ref, dv_ref, dkd0_ref, dkd1_ref, drg_ref, drk_ref, dlw_ref, dlb_ref):
        @pl.when(pl.program_id(0) == 0)
        def _():
            for r in (drk_ref, dlw_ref, dlb_ref):
                r[...] = jnp.zeros_like(r)

        seg = _seg_matrix(RW, HD)
        args = [t[...] for t in (y0_ref, y1_ref, r_ref, v_ref, kd0_ref, kd1_ref, rg_ref, rk_ref, lw_ref, lb_ref)]
        _, vjp = jax.vjp(lambda *a: _post_tile(*a, seg), *args)
        dy0, _, dr, dv, dkd0, dkd1, drg, drk, dlw, dlb = vjp(dy_ref[...])
        dys_ref[...] = dy0
        dr_ref[...] = dr
        dv_ref[...] = dv
        dkd0_ref[...] = dkd0
        dkd1_ref[...] = dkd1
        drg_ref[...] = drg
        drk_ref[...] += drk
        dlw_ref[...] += dlw
        dlb_ref[...] += dlb

    row = _bs((PT, RW), lambda i: (i, 0))
    c = _bs((1, RW), lambda i: (0, 0))
    return pl.pallas_call(
        body, name="rwkv_post_bwd", grid=(s // PT,),
        in_specs=_post_specs() + [row], out_specs=[row] * 6 + [c] * 3,
        out_shape=[jax.ShapeDtypeStruct((s, RW), F32)] * 6 + [jax.ShapeDtypeStruct((1, RW), F32)] * 3,
        compiler_params=_params(("arbitrary",)),
    )(y0, y1, ps, ps, kd0, kd1, proj, r_k, ln_w, ln_b, dy)


def _lane_a():
    return lax.broadcasted_iota(jnp.int32, (HD, 128), 1) < HD


def _segsum(p, lane_a):
    sa = jnp.sum(jnp.where(lane_a, p, 0.0), axis=1, keepdims=True)
    sb = jnp.sum(jnp.where(lane_a, 0.0, p), axis=1, keepdims=True)
    return jnp.where(lane_a, sa, sb)


def _col_pair(ref8, g, u, p, lane_a):
    a = ref8[g, p * 128:p * 128 + HD, u:u + 1]
    b = ref8[g, p * 128 + HD:(p + 1) * 128, u:u + 1]
    return jnp.where(lane_a, a, b)


def _scan_specs(direction, nc, fwd_order):
    def tb(c):
        sc = c if fwd_order else nc - 1 - c
        return sc if direction == 0 else nc - 1 - sc

    row = _bs((TC, RW), lambda c: (tb(c), 0))
    t8 = _bs((TC // 8, RW, 8), lambda c: (tb(c), 0, 0))
    return row, t8


def _scan_fwd(dec, kd, b, ps, kk, v8, direction):
    s = dec.shape[0]
    nc, ng = s // TC, TC // 8
    row, t8 = _scan_specs(direction, nc, True)

    def body(dec_ref, kd_ref, b_ref, r_ref, kk_ref, v8_ref, y8_ref, ck_ref, st):
        @pl.when(pl.program_id(0) == 0)
        def _():
            st[...] = jnp.zeros_like(st)

        ck_ref[0] = st[...]
        lane_a = _lane_a()

        def group(gi, carry):
            g = gi if direction == 0 else ng - 1 - gi
            rows8 = pl.ds(pl.multiple_of(g * 8, 8), 8)
            for p in range(NPAIR):
                cols = slice(p * 128, (p + 1) * 128)
                d8, k8, b8 = dec_ref[rows8, cols], kd_ref[rows8, cols], b_ref[rows8, cols]
                r8, kk8 = r_ref[rows8, cols], kk_ref[rows8, cols]
                sp = st[p]
                for ui in range(8):
                    u = ui if direction == 0 else 7 - ui
                    vb = _col_pair(v8_ref, g, u, p, lane_a)
                    sab = -_segsum(sp * kk8[u:u + 1], lane_a)
                    sp = sp * d8[u:u + 1] + sab * b8[u:u + 1] + vb * k8[u:u + 1]
                    yb = _segsum(sp * r8[u:u + 1], lane_a)
                    y8_ref[g, p * 128:p * 128 + HD, u:u + 1] = yb[:, 0:1]
                    y8_ref[g, p * 128 + HD:(p + 1) * 128, u:u + 1] = yb[:, HD:HD + 1]
                st[p] = sp
            return carry

        lax.fori_loop(0, ng, group, 0)

    return pl.pallas_call(
        body, name=f"rwkv_scan_fwd{direction}", grid=(nc,),
        in_specs=[row, row, row, row, row, t8],
        out_specs=[t8, _bs((1, NPAIR, HD, 128), lambda c: (c, 0, 0, 0))],
        out_shape=[jax.ShapeDtypeStruct((s // 8, RW, 8), F32), jax.ShapeDtypeStruct((nc, NPAIR, HD, 128), F32)],
        scratch_shapes=[pltpu.VMEM((NPAIR, HD, 128), F32)],
        compiler_params=_params(("arbitrary",)),
    )(dec, kd, b, ps, kk, v8)


def _scan_bwd(dec, kd, b, ps, kk, v8, dy8, ck, direction):
    s = dec.shape[0]
    nc, ng = s // TC, TC // 8
    row, t8 = _scan_specs(direction, nc, False)

    def body(dec_ref, kd_ref, b_ref, r_ref, kk_ref, v8_ref, dy8_ref, ck_ref,
             dr_ref, dd_ref, db_ref, dk_ref, dkk_ref, dv8_ref, st, ds):
        @pl.when(pl.program_id(0) == 0)
        def _():
            ds[...] = jnp.zeros_like(ds)

        st[0] = ck_ref[0]
        lane_a = _lane_a()

        row_id = lax.broadcasted_iota(jnp.int32, (8, 128), 0)

        def fgroup(gi, carry):
            g = gi if direction == 0 else ng - 1 - gi
            rows8 = pl.ds(pl.multiple_of(g * 8, 8), 8)
            for p in range(NPAIR):
                cols = slice(p * 128, (p + 1) * 128)
                d8, k8, b8, kk8 = dec_ref[rows8, cols], kd_ref[rows8, cols], b_ref[rows8, cols], kk_ref[rows8, cols]
                sp = st[gi * 8, p]
                for ui in range(8):
                    u = ui if direction == 0 else 7 - ui
                    vb = _col_pair(v8_ref, g, u, p, lane_a)
                    sab = -_segsum(sp * kk8[u:u + 1], lane_a)
                    sp = sp * d8[u:u + 1] + sab * b8[u:u + 1] + vb * k8[u:u + 1]
                    st[gi * 8 + ui + 1, p] = sp
            return carry

        lax.fori_loop(0, ng, fgroup, 0)

        def bgroup(gj, carry):
            gi = ng - 1 - gj
            g = gi if direction == 0 else ng - 1 - gi
            rows8 = pl.ds(pl.multiple_of(g * 8, 8), 8)
            for p in range(NPAIR):
                cols = slice(p * 128, (p + 1) * 128)
                d8, k8, b8 = dec_ref[rows8, cols], kd_ref[rows8, cols], b_ref[rows8, cols]
                r8, kk8 = r_ref[rows8, cols], kk_ref[rows8, cols]
                dsv = ds[p]
                acc = [jnp.zeros((8, 128), F32) for _ in range(5)]
                for uj in range(8):
                    ui = 7 - uj
                    u = ui if direction == 0 else 7 - ui
                    i = gi * 8 + ui
                    d_t, k_t, b_t, r_t, kk_t = (q[u:u + 1] for q in (d8, k8, b8, r8, kk8))
                    vb = _col_pair(v8_ref, g, u, p, lane_a)
                    dyb = _col_pair(dy8_ref, g, u, p, lane_a)
                    sp, sn = st[i, p], st[i + 1, p]
                    sab = -_segsum(sp * kk_t, lane_a)
                    dsv = dsv + dyb * r_t
                    dsab = _segsum(dsv * b_t, lane_a)
                    dvb = _segsum(dsv * k_t, lane_a)
                    outs = (jnp.sum(sn * dyb, axis=0, keepdims=True), jnp.sum(dsv * sp, axis=0, keepdims=True),
                            jnp.sum(dsv * sab, axis=0, keepdims=True), jnp.sum(dsv * vb, axis=0, keepdims=True),
                            -jnp.sum(sp * dsab, axis=0, keepdims=True))
                    acc = [jnp.where(row_id == u, o, a_) for o, a_ in zip(outs, acc)]
                    dsv = dsv * d_t - dsab * kk_t
                    dv8_ref[g, p * 128:p * 128 + HD, u:u + 1] = dvb[:, 0:1]
                    dv8_ref[g, p * 128 + HD:(p + 1) * 128, u:u + 1] = dvb[:, HD:HD + 1]
                ds[p] = dsv
                for o_ref, a_ in zip((dr_ref, dd_ref, db_ref, dk_ref, dkk_ref), acc):
                    o_ref[rows8, cols] = a_
            return carry

        lax.fori_loop(0, ng, bgroup, 0)

    return pl.pallas_call(
        body, name=f"rwkv_scan_bwd{direction}", grid=(nc,),
        in_specs=[row, row, row, row, row, t8, t8, _bs((1, NPAIR, HD, 128), lambda c: (nc - 1 - c, 0, 0, 0))],
        out_specs=[row] * 5 + [t8],
        out_shape=[jax.ShapeDtypeStruct((s, RW), F32)] * 5 + [jax.ShapeDtypeStruct((s // 8, RW, 8), F32)],
        scratch_shapes=[pltpu.VMEM((TC + 1, NPAIR, HD, 128), F32), pltpu.VMEM((NPAIR, HD, 128), F32)],
        compiler_params=_params(("arbitrary",)),
    )(dec, kd, b, ps, kk, v8, dy8, ck)


def _to8(a):
    s = a.shape[0]
    return a.reshape(s // 8, 8, RW).transpose(0, 2, 1)


def _from8(a8):
    return a8.transpose(0, 2, 1).reshape(a8.shape[0] * 8, RW)


MT = 256
MN = 256


def _merge_fwd(ya, yr, yx, wa, wr, wx, proj, gate_b):
    s = ya.shape[0]

    def body(ya_ref, yr_ref, yx_ref, wa_ref, wr_ref, wx_ref, m0, m1, m2, b0, b1, b2, o_ref):
        acc = jnp.zeros((MT, MN), F32)
        for y_ref, w_ref, m_ref, b_ref in ((ya_ref, wa_ref, m0, b0), (yr_ref, wr_ref, m1, b1), (yx_ref, wx_ref, m2, b2)):
            u = _dot(y_ref[...], w_ref[...], ((1,), (0,)))
            acc = acc + jax.nn.sigmoid(m_ref[...] + b_ref[...]) * u
        o_ref[...] = acc.astype(BF16)

    mg = lambda br: _bs((MT, MN), lambda i, j: (i, C_MG // MN + br * (D // MN) + j))
    gb = lambda br: _bs((1, MN), lambda i, j: (0, br * (D // MN) + j))
    return pl.pallas_call(
        body, name="merge_fwd", grid=(s // MT, D // MN),
        in_specs=[_bs((MT, RW), lambda i, j: (i, 0)), _bs((MT, RW), lambda i, j: (i, 0)), _bs((MT, XW), lambda i, j: (i, 0)),
                  _bs((RW, MN), lambda i, j: (0, j)), _bs((RW, MN), lambda i, j: (0, j)), _bs((XW, MN), lambda i, j: (0, j)),
                  mg(0), mg(1), mg(2), gb(0), gb(1), gb(2)],
        out_specs=_bs((MT, MN), lambda i, j: (i, j)),
        out_shape=jax.ShapeDtypeStruct((s, D), BF16),
        compiler_params=_params(("parallel", "arbitrary")),
    )(ya, yr, yx, wa, wr, wx, proj, proj, proj, gate_b, gate_b, gate_b)


def _out_fwd(merged, w_out, x, target):
    s = x.shape[0]
    tm, tn = min(512, s), 512

    def body(m_ref, w_ref, x_ref, t_ref, loss_ref, d_ref):
        @pl.when((pl.program_id(0) == 0) & (pl.program_id(1) == 0))
        def _():
            loss_ref[...] = jnp.zeros_like(loss_ref)

        out = x_ref[...] + jnp.dot(m_ref[...], w_ref[...], preferred_element_type=F32)
        err = out - t_ref[...]
        d_ref[...] = err * (1.0 / D)
        loss_ref[...] += jnp.sum(err * err)

    return pl.pallas_call(
        body, name="out_fwd", grid=(s // tm, D // tn),
        in_specs=[_bs((tm, D), lambda i, j: (i, 0)), _bs((D, tn), lambda i, j: (0, j)),
                  _bs((tm, tn), lambda i, j: (i, j)), _bs((tm, tn), lambda i, j: (i, j))],
        out_specs=[_bs((8, 128), lambda i, j: (0, 0)), _bs((tm, tn), lambda i, j: (i, j))],
        out_shape=[jax.ShapeDtypeStruct((8, 128), F32), jax.ShapeDtypeStruct((s, D), F32)],
        compiler_params=_params(("arbitrary", "arbitrary")),
    )(merged, w_out, x, target)


def _merge_bwd(ya, yr, yx, wa, wr, wx, proj, gate_b, dmerged):
    s = ya.shape[0]

    def body(ya_ref, yr_ref, yx_ref, wa_ref, wr_ref, wx_ref, m0, m1, m2, b0, b1, b2, dm_ref,
             dg0, dg1, dg2, du0, du1, du2, dya_ref, dyr_ref, dyx_ref):
        @pl.when(pl.program_id(1) == 0)
        def _():
            dya_ref[...] = jnp.zeros_like(dya_ref)
            dyr_ref[...] = jnp.zeros_like(dyr_ref)
            dyx_ref[...] = jnp.zeros_like(dyx_ref)

        dm = dm_ref[...]
        for y_ref, w_ref, m_ref, b_ref, dg_ref, du_ref, dy_ref in (
                (ya_ref, wa_ref, m0, b0, dg0, du0, dya_ref), (yr_ref, wr_ref, m1, b1, dg1, du1, dyr_ref),
                (yx_ref, wx_ref, m2, b2, dg2, du2, dyx_ref)):
            w = w_ref[...]
            u = _dot(y_ref[...], w, ((1,), (0,)))
            gt = jax.nn.sigmoid(m_ref[...] + b_ref[...])
            dg_ref[...] = (dm * u * gt * (1.0 - gt)).astype(BF16)
            du = (dm * gt).astype(BF16)
            du_ref[...] = du
            dy_ref[...] += _dot(du, w, ((1,), (1,)))

    mg = lambda br: _bs((MT, MN), lambda i, j: (i, C_MG // MN + br * (D // MN) + j))
    gb = lambda br: _bs((1, MN), lambda i, j: (0, br * (D // MN) + j))
    tile = _bs((MT, MN), lambda i, j: (i, j))
    return pl.pallas_call(
        body, name="merge_bwd", grid=(s // MT, D // MN),
        in_specs=[_bs((MT, RW), lambda i, j: (i, 0)), _bs((MT, RW), lambda i, j: (i, 0)), _bs((MT, XW), lambda i, j: (i, 0)),
                  _bs((RW, MN), lambda i, j: (0, j)), _bs((RW, MN), lambda i, j: (0, j)), _bs((XW, MN), lambda i, j: (0, j)),
                  mg(0), mg(1), mg(2), gb(0), gb(1), gb(2), tile],
        out_specs=[tile] * 6 + [_bs((MT, RW), lambda i, j: (i, 0)), _bs((MT, RW), lambda i, j: (i, 0)),
                                _bs((MT, XW), lambda i, j: (i, 0))],
        out_shape=[jax.ShapeDtypeStruct((s, D), BF16)] * 6 + [jax.ShapeDtypeStruct((s, RW), F32),
                                                               jax.ShapeDtypeStruct((s, RW), F32),
                                                               jax.ShapeDtypeStruct((s, XW), F32)],
        compiler_params=_params(("parallel", "arbitrary")),
    )(ya, yr, yx, wa, wr, wx, proj, proj, proj, gate_b, gate_b, gate_b, dmerged)


def _colsum(a, name):
    m, n = a.shape
    tm, tn = min(512, m), 512

    def body(a_ref, o_ref):
        @pl.when(pl.program_id(1) == 0)
        def _():
            o_ref[...] = jnp.zeros_like(o_ref)

        o_ref[...] += jnp.sum(a_ref[...].astype(F32), axis=0, keepdims=True)

    return pl.pallas_call(
        body, name=name, grid=(n // tn, m // tm),
        in_specs=[_bs((tm, tn), lambda j, i: (i, j))], out_specs=_bs((1, tn), lambda j, i: (0, j)),
        out_shape=jax.ShapeDtypeStruct((1, n), F32),
        compiler_params=_params(("parallel", "arbitrary")),
    )(a)


def _in_bwd(dproj, w_in, x, g, dout):
    s = x.shape[0]
    tm, tk = 256, 896
    nk = NIN // tk

    def body(dp_ref, w_ref, x_ref, g_ref, do_ref, gx_ref, gg_ref, acc):
        i, kk = pl.program_id(0), pl.program_id(1)

        @pl.when((i == 0) & (kk == 0))
        def _():
            gg_ref[...] = jnp.zeros_like(gg_ref)

        @pl.when(kk == 0)
        def _():
            acc[...] = jnp.zeros_like(acc)

        acc[...] += _dot(dp_ref[...], w_ref[...], ((1,), (1,)))

        @pl.when(kk == nk - 1)
        def _():
            xv, dh, gv = x_ref[...], acc[...], g_ref[...]
            r = lax.rsqrt(jnp.mean(xv * xv, axis=-1, keepdims=True) + NORM_EPS)
            xn = xv * r
            gg_ref[...] += jnp.sum(dh * xn, axis=0, keepdims=True)
            dxn = dh * gv
            dx = r * (dxn - xn * jnp.mean(dxn * xn, axis=-1, keepdims=True))
            gx_ref[...] = do_ref[...] + dx

    return pl.pallas_call(
        body, name="in_bwd", grid=(s // tm, nk),
        in_specs=[_bs((tm, tk), lambda i, kk: (i, kk)), _bs((D, tk), lambda i, kk: (0, kk)),
                  _bs((tm, D), lambda i, kk: (i, 0)), _bs((1, D), lambda i, kk: (0, 0)), _bs((tm, D), lambda i, kk: (i, 0))],
        out_specs=[_bs((tm, D), lambda i, kk: (i, 0)), _bs((1, D), lambda i, kk: (0, 0))],
        out_shape=[jax.ShapeDtypeStruct((s, D), F32), jax.ShapeDtypeStruct((1, D), F32)],
        scratch_shapes=[pltpu.VMEM((tm, D), F32)],
        compiler_params=_params(("arbitrary", "arbitrary")),
    )(dproj, w_in, x, g, dout)


def _adamw_math(w, g, m, v):
    m = ADAM_B1 * m + (1.0 - ADAM_B1) * g
    v = ADAM_B2 * v + (1.0 - ADAM_B2) * jnp.square(g)
    m_hat = m / (1.0 - ADAM_B1 ** ADAM_STEP)
    v_hat = v / (1.0 - ADAM_B2 ** ADAM_STEP)
    delta = -ADAM_LR * (m_hat / (jnp.sqrt(v_hat) + ADAM_EPS) + ADAM_WD * w)
    return delta, m, v


def _adamw(parts, w, m, v, name):
    rows, cols = w.shape
    tr = rows
    for cand in (256, 128, 64, 32, 16, 8):
        if rows % cand == 0 and cand * cols * 4 <= (1 << 20):
            tr = cand
            break
    n = len(parts)

    def body(*refs):
        g = refs[0][...].astype(F32)
        for r in refs[1:n]:
            g = g + r[...].astype(F32)
        w_ref, m_ref, v_ref, g_out, d_out, m_out, v_out = refs[n:]
        delta, m_new, v_new = _adamw_math(w_ref[...], g, m_ref[...], v_ref[...])
        g_out[...] = g
        d_out[...] = delta
        m_out[...] = m_new
        v_out[...] = v_new

    spec = _bs((tr, cols), lambda i: (i, 0))
    return pl.pallas_call(
        body, name=name, grid=(rows // tr,),
        in_specs=[spec] * (n + 3), out_specs=[spec] * 4,
        out_shape=[jax.ShapeDtypeStruct((rows, cols), F32)] * 4,
        compiler_params=_params(("parallel",)),
    )(*parts, w, m, v)


def _sum_parts(parts, name):
    rows, cols = parts[0].shape
    tr = rows
    for cand in (256, 128, 64, 32, 16, 8):
        if rows % cand == 0 and cand * cols * 4 <= (1 << 20):
            tr = cand
            break

    def body(*refs):
        acc = refs[0][...].astype(F32)
        for r in refs[1:-1]:
            acc = acc + r[...].astype(F32)
        refs[-1][...] = acc

    spec = _bs((tr, cols), lambda i: (i, 0))
    return pl.pallas_call(
        body, name=name, grid=(rows // tr,), in_specs=[spec] * len(parts), out_specs=spec,
        out_shape=jax.ShapeDtypeStruct((rows, cols), F32), compiler_params=_params(("parallel",)),
    )(*parts)


ANY = pl.BlockSpec(memory_space=pl.ANY)


def _other_chips(x, y):
    return [(1 - x, y), (x, 1 - y), (1 - x, 1 - y)]


def _gather_shards(arrays, name):
    n = len(arrays)

    def body(*refs):
        ins, outs = refs[:n], refs[n:2 * n]
        send_sems, recv_sems, local_sems = refs[2 * n:]
        x, y, c = lax.axis_index("x"), lax.axis_index("y"), lax.axis_index("c")
        me = 2 * x + y
        chips = _other_chips(x, y)
        locals_, sends = [], []
        for i in range(n):
            cp = pltpu.make_async_copy(ins[i], outs[i].at[me], local_sems.at[i])
            cp.start()
            locals_.append(cp)
            for j, (px, py) in enumerate(chips):
                rc = pltpu.make_async_remote_copy(
                    src_ref=ins[i], dst_ref=outs[i].at[me], send_sem=send_sems.at[3 * i + j],
                    recv_sem=recv_sems.at[3 * i + j], device_id=(px, py, c), device_id_type=MESH)
                rc.start()
                sends.append(rc)
        for i in range(n):
            for j, (px, py) in enumerate(chips):
                pltpu.make_async_remote_copy(
                    src_ref=ins[i], dst_ref=outs[i].at[2 * px + py], send_sem=send_sems.at[3 * i + j],
                    recv_sem=recv_sems.at[3 * i + j], device_id=(px, py, c), device_id_type=MESH).wait_recv()
        for rc in sends:
            rc.wait_send()
        for cp in locals_:
            cp.wait()

    return pl.pallas_call(
        body, name=name, in_specs=[ANY] * n, out_specs=[ANY] * n,
        out_shape=[jax.ShapeDtypeStruct((4,) + a.shape, a.dtype) for a in arrays],
        scratch_shapes=[pltpu.SemaphoreType.DMA((3 * n,)), pltpu.SemaphoreType.DMA((3 * n,)),
                        pltpu.SemaphoreType.DMA((n,))],
        compiler_params=pltpu.CompilerParams(has_side_effects=True),
    )(*arrays)


def _scatter_shards(stacks, name):
    n = len(stacks)

    def body(*refs):
        ins, outs = refs[:n], refs[n:2 * n]
        send_sems, recv_sems = refs[2 * n:]
        x, y, c = lax.axis_index("x"), lax.axis_index("y"), lax.axis_index("c")
        chips = _other_chips(x, y)
        sends = []
        for i in range(n):
            for j, (px, py) in enumerate(chips):
                rc = pltpu.make_async_remote_copy(
                    src_ref=ins[i].at[2 * px + py], dst_ref=outs[i].at[j], send_sem=send_sems.at[3 * i + j],
                    recv_sem=recv_sems.at[3 * i + j], device_id=(px, py, c), device_id_type=MESH)
                rc.start()
                sends.append(rc)
        for rc in sends:
            rc.wait_recv()
        for rc in sends:
            rc.wait_send()

    return pl.pallas_call(
        body, name=name, in_specs=[ANY] * n, out_specs=[ANY] * n,
        out_shape=[jax.ShapeDtypeStruct((3,) + a.shape[1:], a.dtype) for a in stacks],
        scratch_shapes=[pltpu.SemaphoreType.DMA((3 * n,)), pltpu.SemaphoreType.DMA((3 * n,))],
        compiler_params=pltpu.CompilerParams(has_side_effects=True),
    )(*stacks)


def _swap_sibling(arrays, name):
    n = len(arrays)

    def body(*refs):
        ins, outs = refs[:n], refs[n:2 * n]
        send_sems, recv_sems = refs[2 * n:]
        sib = (lax.axis_index("x"), lax.axis_index("y"), 1 - lax.axis_index("c"))
        cps = []
        for i in range(n):
            rc = pltpu.make_async_remote_copy(src_ref=ins[i], dst_ref=outs[i], send_sem=send_sems.at[i],
                                              recv_sem=recv_sems.at[i], device_id=sib, device_id_type=MESH)
            rc.start()
            cps.append(rc)
        for rc in cps:
            rc.wait_recv()
        for rc in cps:
            rc.wait_send()

    return pl.pallas_call(
        body, name=name, in_specs=[ANY] * n, out_specs=[ANY] * n,
        out_shape=[jax.ShapeDtypeStruct(a.shape, a.dtype) for a in arrays],
        scratch_shapes=[pltpu.SemaphoreType.DMA((n,)), pltpu.SemaphoreType.DMA((n,))],
        compiler_params=pltpu.CompilerParams(has_side_effects=True),
    )(*arrays)


def _all_reduce_small(v):
    rows = v.shape[0]

    def body(v_ref, o_ref, buf, send_sems, recv_sems):
        x, y, c = lax.axis_index("x"), lax.axis_index("y"), lax.axis_index("c")
        me = 4 * x + 2 * y + c
        buf[me] = v_ref[...]
        cps = []
        for kbits in range(1, 8):
            bx, by, bc = (kbits >> 2) & 1, (kbits >> 1) & 1, kbits & 1
            px = jnp.where(bx == 1, 1 - x, x)
            py = jnp.where(by == 1, 1 - y, y)
            pc = jnp.where(bc == 1, 1 - c, c)
            rc = pltpu.make_async_remote_copy(src_ref=v_ref, dst_ref=buf.at[me], send_sem=send_sems.at[kbits - 1],
                                              recv_sem=recv_sems.at[kbits - 1], device_id=(px, py, pc),
                                              device_id_type=MESH)
            rc.start()
            cps.append((rc, 4 * px + 2 * py + pc))
        for kbits, (rc, src) in enumerate(cps):
            pltpu.make_async_remote_copy(src_ref=v_ref, dst_ref=buf.at[src], send_sem=send_sems.at[kbits],
                                         recv_sem=recv_sems.at[kbits], device_id=(x, y, c),
                                         device_id_type=MESH).wait_recv()
        for rc, _ in cps:
            rc.wait_send()
        acc = buf[0]
        for d in range(1, 8):
            acc = acc + buf[d]
        o_ref[...] = acc

    return pl.pallas_call(
        body, name="all_reduce_small",
        in_specs=[pl.BlockSpec(memory_space=pltpu.VMEM)], out_specs=pl.BlockSpec(memory_space=pltpu.VMEM),
        out_shape=jax.ShapeDtypeStruct((rows, 128), F32),
        scratch_shapes=[pltpu.VMEM((8, rows, 128), F32), pltpu.SemaphoreType.DMA((7,)), pltpu.SemaphoreType.DMA((7,))],
        compiler_params=pltpu.CompilerParams(has_side_effects=True, vmem_limit_bytes=VMEM_LIMIT),
    )(v)


def _rope_tables(s):
    half = HD // 2
    inv = 10000.0 ** (-jnp.arange(half, dtype=F32) / half)
    ang = jnp.arange(s, dtype=F32)[:, None] * inv[None, :]
    cos, sin = jnp.cos(ang), jnp.sin(ang)
    return jnp.concatenate([cos, cos], axis=1), jnp.concatenate([sin, sin], axis=1)


def _local_step(x, mem, target, norm_g, mem_norm_g, w_in, gate_b, gq, gk, sink, wa, mu, k_k, k_a, r_k, w0, w2, a0, a2,
                ln_w, ln_b, wr, w_kv, gxq, gxk, wx, w_out):
    s = x.shape[0]
    cos, sin = _rope_tables(s)
    r_k = r_k.reshape(1, RW)

    proj, h = _proj_fwd(x, norm_g, w_in)
    ya = _attn_fwd(proj, cos, sin, gq, gk, sink)
    mkv, mn = _mem_kv(mem, mem_norm_g, w_kv)
    yx = _xattn_fwd(proj, mkv, gxq, gxk)
    ps = _shift_fwd(proj, mu)
    kk, dec0, kd0, b0, dec1, kd1, b1 = _pre_fwd(ps, k_k, k_a, w0, w2, a0, a2)
    v8 = _to8(ps[:, 2 * RW:3 * RW])
    y80, ck0 = _scan_fwd(dec0, kd0, b0, ps, kk, v8, 0)
    y81, ck1 = _scan_fwd(dec1, kd1, b1, ps, kk, v8, 1)
    y0, y1 = _from8(y80), _from8(y81)
    yr = _post_fwd(y0, y1, ps, kd0, kd1, proj, r_k, ln_w, ln_b)
    merged = _merge_fwd(ya, yr, yx, wa, wr, wx, proj, gate_b)
    loss_tile, dout = _out_fwd(merged, w_out, x, target)
    loss_sum = loss_tile[0, 0]

    g = {}
    dmerged = _matmul(dout, w_out, mode="nt", m=s, n=D, k=D, tm=min(512, s), tn=512, tk=512, name="dmerged")
    g["w_out"] = _matmul(merged, dout, mode="tn", m=D, n=D, k=s, tm=512, tn=512, tk=min(512, s), name="grad_w_out")
    dg0, dg1, dg2, du0, du1, du2, dya, dyr, dyx = _merge_bwd(ya, yr, yx, wa, wr, wx, proj, gate_b, dmerged)
    g["attn_w_o"] = _matmul(ya, du0, mode="tn", m=RW, n=D, k=s, tm=RW, tn=512, tk=min(512, s), name="grad_attn_w_o")
    g["rwkv_w_o"] = _matmul(yr, du1, mode="tn", m=RW, n=D, k=s, tm=RW, tn=512, tk=min(512, s), name="grad_rwkv_w_o")
    g["x_w_o"] = _matmul(yx, du2, mode="tn", m=XW, n=D, k=s, tm=XW, tn=512, tk=min(512, s), name="grad_x_w_o")
    dmg = jnp.concatenate([dg0, dg1, dg2], axis=1)
    g["gate_b"] = _colsum(dmg, "grad_gate_b")

    daq, dak, dav, dag, g["attn_q_norm_g"], g["attn_k_norm_g"], g["attn_sink"] = _attn_bwd(proj, cos, sin, gq, gk, sink, dya)

    dxq, dxg, dmkv, g["x_q_norm_g"], g["x_k_norm_g"] = _xattn_bwd(proj, mkv, gxq, gxk, dyx)
    g["x_w_kv"] = _matmul(mn, dmkv, mode="tn", m=D, n=2 * XW, k=NMEM, tm=512, tn=512, tk=NMEM, name="grad_x_w_kv")
    dmn = _matmul(dmkv, w_kv, mode="nt", m=NMEM, n=D, k=2 * XW, tm=NMEM, tn=512, tk=2 * XW, name="dmn")
    g["mem_norm_g"] = _mem_bwd(mem, mem_norm_g, dmn)

    dys, dr_p, dv_p, dkd0_p, dkd1_p, drg, g["rwkv_r_k"], g["rwkv_ln_w"], g["rwkv_ln_b"] = _post_bwd(
        y0, y1, ps, kd0, kd1, proj, r_k, ln_w, ln_b, dyr)
    dy8 = _to8(dys)
    dr0, dd0, db0, dk0, dkk0, dv80 = _scan_bwd(dec0, kd0, b0, ps, kk, v8, dy8, ck0, 0)
    dr1, dd1, db1, dk1, dkk1, dv81 = _scan_bwd(dec1, kd1, b1, ps, kk, v8, dy8, ck1, 1)
    dr = dr_p + dr0 + dr1
    dv = dv_p + _from8(dv80) + _from8(dv81)
    cts = (dkk0 + dkk1, dd0, dk0 + dkd0_p, db0, dd1, dk1 + dkd1_p, db1)
    dps, g["rwkv_k_k"], g["rwkv_k_a"], g["rwkv_w0"], g["rwkv_w2"], g["rwkv_a0"], g["rwkv_a2"] = _pre_bwd(
        ps, k_k, k_a, w0, w2, a0, a2, dr, dv, cts)
    drs, g["rwkv_mu"] = _shift_bwd(proj, mu, dps)

    dproj = jnp.concatenate([daq.astype(BF16), dak.astype(BF16), dav.astype(BF16), dag.astype(BF16), drs.astype(BF16),
                             drg.astype(BF16), dxq.astype(BF16), dxg.astype(BF16), dmg], axis=1)
    g["w_in"] = _matmul(h, dproj, mode="tn", m=D, n=NIN, k=s, tm=512, tn=896, tk=min(512, s), name="grad_w_in")
    grad_x, g["norm_g"] = _in_bwd(dproj, w_in, x, norm_g, dout)
    g["rwkv_r_k"] = g["rwkv_r_k"].reshape(AH, HD)
    return loss_sum, grad_x, g


WEIGHTS = ['norm_g', 'mem_norm_g', 'w_in', 'gate_b', 'attn_q_norm_g', 'attn_k_norm_g', 'attn_sink', 'attn_w_o',
           'rwkv_mu', 'rwkv_k_k', 'rwkv_k_a', 'rwkv_r_k', 'rwkv_w0', 'rwkv_w2', 'rwkv_a0', 'rwkv_a2', 'rwkv_ln_w',
           'rwkv_ln_b', 'rwkv_w_o', 'x_w_kv', 'x_q_norm_g', 'x_k_norm_g', 'x_w_o', 'w_out']
BIG = ['w_in', 'attn_w_o', 'rwkv_w_o', 'x_w_kv', 'x_w_o', 'w_out']
COL_SHARDED = ['w_in', 'attn_w_o', 'rwkv_w_o', 'x_w_o']
LORA = ['rwkv_w0', 'rwkv_w2', 'rwkv_a0', 'rwkv_a2']
SMALL = [n for n in WEIGHTS if n not in BIG]


def _unshard_cols(stack):
    return jnp.concatenate([stack[i] for i in range(4)], axis=-1)


def _shard_cols(full):
    w = full.shape[-1] // 4
    return [full[..., i * w:(i + 1) * w] for i in range(4)]


def kernel(x, mem, norm_g, mem_norm_g, w_in, gate_b, attn_q_norm_g, attn_k_norm_g, attn_sink, attn_w_o, rwkv_mu, rwkv_k_k, rwkv_k_a, rwkv_r_k, rwkv_w0, rwkv_w2, rwkv_a0, rwkv_a2, rwkv_ln_w, rwkv_ln_b, rwkv_w_o, x_w_kv, x_q_norm_g, x_k_norm_g, x_w_o, w_out, loss_target, m_norm_g, m_mem_norm_g, m_w_in, m_gate_b, m_attn_q_norm_g, m_attn_k_norm_g, m_attn_sink, m_attn_w_o, m_rwkv_mu, m_rwkv_k_k, m_rwkv_k_a, m_rwkv_r_k, m_rwkv_w0, m_rwkv_w2, m_rwkv_a0, m_rwkv_a2, m_rwkv_ln_w, m_rwkv_ln_b, m_rwkv_w_o, m_x_w_kv, m_x_q_norm_g, m_x_k_norm_g, m_x_w_o, m_w_out, v_norm_g, v_mem_norm_g, v_w_in, v_gate_b, v_attn_q_norm_g, v_attn_k_norm_g, v_attn_sink, v_attn_w_o, v_rwkv_mu, v_rwkv_k_k, v_rwkv_k_a, v_rwkv_r_k, v_rwkv_w0, v_rwkv_w2, v_rwkv_a0, v_rwkv_a2, v_rwkv_ln_w, v_rwkv_ln_b, v_rwkv_w_o, v_x_w_kv, v_x_q_norm_g, v_x_k_norm_g, v_x_w_o, v_w_out):
    args = dict(locals())
    canon = lambda a: a[0] if a.ndim > 2 else a
    w = {n: canon(args[n]) for n in WEIGHTS}
    m = {n: canon(args["m_" + n]) for n in WEIGHTS}
    v = {n: canon(args["v_" + n]) for n in WEIGHTS}
    shard = 2 * lax.axis_index("x") + lax.axis_index("y")

    local = [w[n].astype(BF16) for n in BIG] + [w[n] for n in LORA]
    stacks = dict(zip(BIG + LORA, _gather_shards(local, "gather_weights")))
    full = {}
    for n in COL_SHARDED + LORA:
        full[n] = _unshard_cols(stacks[n])
    full["x_w_kv"] = stacks["x_w_kv"].reshape(D, 2 * XW)
    full["w_out"] = stacks["w_out"].reshape(D, D)

    loss_sum, grad_x, g = _local_step(
        x[0], mem[0], loss_target[0], w["norm_g"], w["mem_norm_g"], full["w_in"], w["gate_b"], w["attn_q_norm_g"],
        w["attn_k_norm_g"], w["attn_sink"], full["attn_w_o"], w["rwkv_mu"], w["rwkv_k_k"], w["rwkv_k_a"], w["rwkv_r_k"],
        full["rwkv_w0"], full["rwkv_w2"], full["rwkv_a0"], full["rwkv_a2"], w["rwkv_ln_w"], w["rwkv_ln_b"],
        full["rwkv_w_o"], full["x_w_kv"], w["x_q_norm_g"], w["x_k_norm_g"], full["x_w_o"], full["w_out"])

    loss = lax.psum(0.5 * loss_sum / D, ("x", "y", "c"))

    def as_stack(n, dtype):
        if n in COL_SHARDED:
            return jnp.stack([p.astype(dtype) for p in _shard_cols(g[n])])
        return g[n].reshape((4, g[n].shape[0] // 4) + g[n].shape[1:]).astype(dtype)

    recv = _scatter_shards([as_stack(n, BF16) for n in BIG], "scatter_grads")
    partial = []
    for n, r in zip(BIG, recv):
        own = lax.dynamic_index_in_dim(as_stack(n, F32), shard, 0, keepdims=False)
        partial.append(_sum_parts([own, r[0], r[1], r[2]], "sum_" + n))
    theirs = _swap_sibling(partial, "swap_partials")

    out_g, out_d, out_m, out_v = {}, {}, {}, {}
    for n, mine, other in zip(BIG, partial, theirs):
        out_g[n], out_d[n], out_m[n], out_v[n] = _adamw([mine, other], w[n], m[n], v[n], "adamw_" + n)

    flat = jnp.concatenate([g[n].reshape(-1) for n in SMALL])
    total = flat.shape[0]
    padded = -(-total // 1024) * 1024
    flat = jnp.pad(flat, (0, padded - total)).reshape(padded // 128, 128)
    red = _all_reduce_small(flat).reshape(-1)
    off = 0
    gs = {}
    for n in SMALL:
        size = g[n].size
        t = red[off:off + size].reshape(g[n].shape)
        off += size
        if n in LORA:
            wd = t.shape[-1] // 4
            t = lax.dynamic_slice_in_dim(t, shard * wd, wd, axis=t.ndim - 1)
        gs[n] = t

    def pack(d):
        f = jnp.concatenate([d[n].reshape(-1) for n in SMALL])
        return jnp.pad(f, (0, -(-f.shape[0] // 1024) * 1024 - f.shape[0])).reshape(-1, 128)

    pg, pd, pm, pv = _adamw([pack(gs)], pack(w), pack(m), pack(v), "adamw_small")
    off = 0
    for n in SMALL:
        size = w[n].size
        for dst, src in ((out_g, pg), (out_d, pd), (out_m, pm), (out_v, pv)):
            dst[n] = src.reshape(-1)[off:off + size].reshape(w[n].shape)
        off += size

    lead = lambda d: [d[n][None] if args[n].ndim > 2 else d[n] for n in WEIGHTS]
    return (loss, grad_x[None], *lead(out_g), *lead(out_d), *lead(out_m), *lead(out_v))
```

```python
import functools

import jax
import jax.numpy as jnp
from jax import lax
from jax.experimental import pallas as pl
from jax.experimental.pallas import tpu as pltpu

F32 = jnp.float32
BF16 = jnp.bfloat16
HI = lax.Precision.HIGHEST
MESH = pl.DeviceIdType.MESH

D = 2048
NMEM = 256
NORM_EPS = 1e-6
NEG_INF = -1e30
GN_EPS = 64e-5
HD = 64
AH = 12
AKV = 4
RW = 768
XH = 4
XD = 128
XW = 512
NIN = 12544
RSW = 2560
C_AQ, C_AK, C_AV, C_AG, C_RS, C_RG, C_XQ, C_XG, C_MG = 0, 768, 1024, 1280, 2048, 4608, 5376, 5888, 6400
WIN = 384
QB = 128
TC = 32
NPAIR = 6

ADAM_LR, ADAM_B1, ADAM_B2, ADAM_EPS, ADAM_WD, ADAM_STEP = 0.001, 0.9, 0.999, 1e-08, 0.01, 10

VMEM_LIMIT = 56 * 1024 * 1024


def _bs(shape, imap):
    return pl.BlockSpec(shape, imap)


def _params(sem=None, vmem=VMEM_LIMIT):
    return pltpu.CompilerParams(dimension_semantics=sem, vmem_limit_bytes=vmem)


def _dot(a, b, dims):
    return lax.dot_general(a.astype(BF16), b.astype(BF16), (dims, ((), ())), preferred_element_type=F32)


@jax.custom_vjp
def _mm_nn(a, b):
    return _dot(a, b, ((1,), (0,)))


def _mm_nn_fwd(a, b):
    return _mm_nn(a, b), (a, b)


def _mm_nn_bwd(res, ct):
    a, b = res
    return _dot(ct, b, ((1,), (1,))), _dot(a, ct, ((0,), (0,)))


_mm_nn.defvjp(_mm_nn_fwd, _mm_nn_bwd)


@jax.custom_vjp
def _mm_nt(a, b):
    return _dot(a, b, ((1,), (1,)))


def _mm_nt_fwd(a, b):
    return _mm_nt(a, b), (a, b)


def _mm_nt_bwd(res, ct):
    a, b = res
    return _dot(ct, b, ((1,), (0,))), _dot(ct, a, ((0,), (0,)))


_mm_nt.defvjp(_mm_nt_fwd, _mm_nt_bwd)


def _seg_matrix(n, seg):
    r = lax.broadcasted_iota(jnp.int32, (n, n), 0) // seg
    c = lax.broadcasted_iota(jnp.int32, (n, n), 1) // seg
    return (r == c).astype(F32)


def _rot_matrix():
    r = lax.broadcasted_iota(jnp.int32, (HD, HD), 0)
    c = lax.broadcasted_iota(jnp.int32, (HD, HD), 1)
    return jnp.where(c == r + HD // 2, 1.0, 0.0).astype(F32) - jnp.where(c == r - HD // 2, 1.0, 0.0).astype(F32)


def _hdot(a, m):
    return jnp.dot(a, m, precision=HI, preferred_element_type=F32)


def _rms(t, g):
    return t * lax.rsqrt(jnp.mean(t * t, axis=-1, keepdims=True) + NORM_EPS) * g


def _silu(t):
    return t * jax.nn.sigmoid(t)


def _softplus(z):
    return jnp.maximum(z, 0.0) + jnp.log(1.0 + jnp.exp(-jnp.abs(z)))


def _matmul(a, b, *, mode, m, n, k, tm, tn, tk, name, a_off=(0, 0), b_off=(0, 0), out_dtype=F32):
    nk = k // tk
    if mode == "tn":
        a_spec = _bs((tk, tm), lambda i, j, kk: (kk + a_off[0], i + a_off[1]))
        dims = ((0,), (0,))
    else:
        a_spec = _bs((tm, tk), lambda i, j, kk: (i + a_off[0], kk + a_off[1]))
        dims = ((1,), (1,)) if mode == "nt" else ((1,), (0,))
    if mode == "nt":
        b_spec = _bs((tn, tk), lambda i, j, kk: (j + b_off[0], kk + b_off[1]))
    else:
        b_spec = _bs((tk, tn), lambda i, j, kk: (kk + b_off[0], j + b_off[1]))

    def body(a_ref, b_ref, o_ref, acc):
        kk = pl.program_id(2)

        @pl.when(kk == 0)
        def _():
            acc[...] = jnp.zeros_like(acc)

        acc[...] += _dot(a_ref[...], b_ref[...], dims)

        @pl.when(kk == nk - 1)
        def _():
            o_ref[...] = acc[...].astype(out_dtype)

    return pl.pallas_call(
        body, name=name, grid=(m // tm, n // tn, nk),
        in_specs=[a_spec, b_spec], out_specs=_bs((tm, tn), lambda i, j, kk: (i, j)),
        out_shape=jax.ShapeDtypeStruct((m, n), out_dtype),
        scratch_shapes=[pltpu.VMEM((tm, tn), F32)],
        compiler_params=_params(("parallel", "parallel", "arbitrary")),
    )(a, b)


def _proj_fwd(x, g, w):
    s = x.shape[0]
    tm, tn = min(512, s), 896

    def body(x_ref, g_ref, w_ref, o_ref, h_ref, hs):
        @pl.when(pl.program_id(1) == 0)
        def _():
            h = _rms(x_ref[...], g_ref[...]).astype(BF16)
            hs[...] = h
            h_ref[...] = h

        o_ref[...] = jnp.dot(hs[...], w_ref[...], preferred_element_type=F32)

    return pl.pallas_call(
        body, name="proj_fwd", grid=(s // tm, NIN // tn),
        in_specs=[_bs((tm, D), lambda i, j: (i, 0)), _bs((1, D), lambda i, j: (0, 0)), _bs((D, tn), lambda i, j: (0, j))],
        out_specs=[_bs((tm, tn), lambda i, j: (i, j)), _bs((tm, D), lambda i, j: (i, 0))],
        out_shape=[jax.ShapeDtypeStruct((s, NIN), F32), jax.ShapeDtypeStruct((s, D), BF16)],
        scratch_shapes=[pltpu.VMEM((tm, D), BF16)],
        compiler_params=_params(("parallel", "arbitrary")),
    )(x, g, w)


def _rope(t, cos, sin, rot):
    return t * cos + _hdot(t, rot) * sin


def _attn_tile(qs, ks, vs, gs, sinks, gq, gk, cq, sq, ck, sk, mask, rot):
    outs = []
    for hk in range(AKV):
        kh = _rope(_rms(ks[hk], gk), ck, sk, rot)
        for g in range(AH // AKV):
            h = hk * (AH // AKV) + g
            qh = _rope(_rms(qs[h], gq), cq, sq, rot)
            sc = _mm_nt(qh, kh) * (HD ** -0.5)
            sc = jnp.where(mask, sc, NEG_INF)
            mx = lax.stop_gradient(jnp.maximum(jnp.max(sc, axis=-1, keepdims=True), sinks[h]))
            p = jnp.exp(sc - mx)
            den = jnp.sum(p, axis=-1, keepdims=True) + jnp.exp(sinks[h] - mx)
            o = _mm_nn(p / den, vs[hk])
            outs.append(o * _silu(gs[h]))
    return outs


def _attn_load(n, s, aq_ref, ak_ref, av_ref, ag_refs, cos_ref, sin_ref, sink_ref):
    start = pl.multiple_of(jnp.clip((n - 1) * QB, 0, s - WIN), QB)
    q0 = pl.multiple_of(n * QB, QB)
    qs = [aq_ref[:, h * HD:(h + 1) * HD] for h in range(AH)]
    ks = [ak_ref[pl.ds(start, WIN), h * HD:(h + 1) * HD] for h in range(AKV)]
    vs = [av_ref[pl.ds(start, WIN), h * HD:(h + 1) * HD] for h in range(AKV)]
    gs = [ag_refs[h // 4][:, (h % 4) * HD:(h % 4 + 1) * HD] for h in range(AH)]
    sinks = [sink_ref[0:1, h:h + 1] for h in range(AH)]
    cq, sq = cos_ref[pl.ds(q0, QB), :], sin_ref[pl.ds(q0, QB), :]
    ck, sk = cos_ref[pl.ds(start, WIN), :], sin_ref[pl.ds(start, WIN), :]
    qpos = q0 + lax.broadcasted_iota(jnp.int32, (QB, WIN), 0)
    kpos = start + lax.broadcasted_iota(jnp.int32, (QB, WIN), 1)
    mask = jnp.abs(kpos - qpos) <= QB
    return start, qs, ks, vs, gs, sinks, cq, sq, ck, sk, mask


def _attn_specs(s):
    return [
        _bs((QB, 768), lambda n: (n, 0)),
        _bs((s, 256), lambda n: (0, C_AK // 256)),
        _bs((s, 256), lambda n: (0, C_AV // 256)),
        _bs((QB, 256), lambda n: (n, C_AG // 256)),
        _bs((QB, 256), lambda n: (n, C_AG // 256 + 1)),
        _bs((QB, 256), lambda n: (n, C_AG // 256 + 2)),
        _bs((s, HD), lambda n: (0, 0)),
        _bs((s, HD), lambda n: (0, 0)),
        _bs((1, HD), lambda n: (0, 0)),
        _bs((1, HD), lambda n: (0, 0)),
        _bs((1, AH), lambda n: (0, 0)),
    ]


def _attn_fwd(proj, cos, sin, gq, gk, sink):
    s = proj.shape[0]

    def body(aq_ref, ak_ref, av_ref, ag0, ag1, ag2, cos_ref, sin_ref, gq_ref, gk_ref, sink_ref, o_ref):
        n = pl.program_id(0)
        _, qs, ks, vs, gs, sinks, cq, sq, ck, sk, mask = _attn_load(
            n, s, aq_ref, ak_ref, av_ref, (ag0, ag1, ag2), cos_ref, sin_ref, sink_ref)
        outs = _attn_tile(qs, ks, vs, gs, sinks, gq_ref[...], gk_ref[...], cq, sq, ck, sk, mask, _rot_matrix())
        for h in range(AH):
            o_ref[:, h * HD:(h + 1) * HD] = outs[h]

    return pl.pallas_call(
        body, name="attn_fwd", grid=(s // QB,),
        in_specs=_attn_specs(s), out_specs=_bs((QB, 768), lambda n: (n, 0)),
        out_shape=jax.ShapeDtypeStruct((s, 768), F32),
        compiler_params=_params(("arbitrary",)),
    )(proj, proj, proj, proj, proj, proj, cos, sin, gq, gk, sink)


def _attn_bwd(proj, cos, sin, gq, gk, sink, dy):
    s = proj.shape[0]

    def body(aq_ref, ak_ref, av_ref, ag0, ag1, ag2, cos_ref, sin_ref, gq_ref, gk_ref, sink_ref, dy_ref,
             daq_ref, dak_ref, dav_ref, dag_ref, dgq_ref, dgk_ref, dsink_ref):
        n = pl.program_id(0)

        @pl.when(n == 0)
        def _():
            dak_ref[...] = jnp.zeros_like(dak_ref)
            dav_ref[...] = jnp.zeros_like(dav_ref)
            dgq_ref[...] = jnp.zeros_like(dgq_ref)
            dgk_ref[...] = jnp.zeros_like(dgk_ref)
            dsink_ref[...] = jnp.zeros_like(dsink_ref)

        start, qs, ks, vs, gs, sinks, cq, sq, ck, sk, mask = _attn_load(
            n, s, aq_ref, ak_ref, av_ref, (ag0, ag1, ag2), cos_ref, sin_ref, sink_ref)
        rot = _rot_matrix()

        def f(qs, ks, vs, gs, sinks, gq, gk):
            return _attn_tile(qs, ks, vs, gs, sinks, gq, gk, cq, sq, ck, sk, mask, rot)

        _, vjp = jax.vjp(f, qs, ks, vs, gs, sinks, gq_ref[...], gk_ref[...])
        dys = [dy_ref[:, h * HD:(h + 1) * HD] for h in range(AH)]
        dqs, dks, dvs, dgs, dsinks, dgq, dgk = vjp(dys)
        for h in range(AH):
            daq_ref[:, h * HD:(h + 1) * HD] = dqs[h]
            dag_ref[:, h * HD:(h + 1) * HD] = dgs[h]
            dsink_ref[0:1, h:h + 1] += dsinks[h]
        for h in range(AKV):
            dak_ref[pl.ds(start, WIN), h * HD:(h + 1) * HD] += dks[h]
            dav_ref[pl.ds(start, WIN), h * HD:(h + 1) * HD] += dvs[h]
        dgq_ref[...] += dgq
        dgk_ref[...] += dgk

    whole = lambda shape: _bs(shape, lambda n: (0, 0))
    return pl.pallas_call(
        body, name="attn_bwd", grid=(s // QB,),
        in_specs=_attn_specs(s) + [_bs((QB, 768), lambda n: (n, 0))],
        out_specs=[_bs((QB, 768), lambda n: (n, 0)), whole((s, 256)), whole((s, 256)), _bs((QB, 768), lambda n: (n, 0)),
                   whole((1, HD)), whole((1, HD)), whole((1, AH))],
        out_shape=[jax.ShapeDtypeStruct((s, 768), F32), jax.ShapeDtypeStruct((s, 256), F32),
                   jax.ShapeDtypeStruct((s, 256), F32), jax.ShapeDtypeStruct((s, 768), F32),
                   jax.ShapeDtypeStruct((1, HD), F32), jax.ShapeDtypeStruct((1, HD), F32),
                   jax.ShapeDtypeStruct((1, AH), F32)],
        compiler_params=_params(("arbitrary",)),
    )(proj, proj, proj, proj, proj, proj, cos, sin, gq, gk, sink, dy)


def _mem_kv(mem, g, w):
    def body(m_ref, g_ref, w_ref, o_ref, mn_ref):
        mn = _rms(m_ref[...], g_ref[...]).astype(BF16)
        mn_ref[...] = mn
        o_ref[...] = jnp.dot(mn, w_ref[...], preferred_element_type=F32)

    return pl.pallas_call(
        body, name="mem_kv",
        out_shape=[jax.ShapeDtypeStruct((NMEM, 2 * XW), F32), jax.ShapeDtypeStruct((NMEM, D), BF16)],
        compiler_params=_params(),
    )(mem, g, w)


def _xattn_tile(qs, gs, kms, vms, gxq, gxk):
    outs = []
    for h in range(XH):
        q = _rms(qs[h], gxq)
        km = _rms(kms[h], gxk)
        sc = _mm_nt(q, km) * (XD ** -0.5)
        mx = lax.stop_gradient(jnp.max(sc, axis=-1, keepdims=True))
        p = jnp.exp(sc - mx)
        p = p / jnp.sum(p, axis=-1, keepdims=True)
        outs.append(_mm_nn(p, vms[h]) * _silu(gs[h]))
    return outs


XT = 256


def _xattn_specs():
    return [
        _bs((XT, 256), lambda i: (i, C_XQ // 256)), _bs((XT, 256), lambda i: (i, C_XQ // 256 + 1)),
        _bs((XT, 256), lambda i: (i, C_XG // 256)), _bs((XT, 256), lambda i: (i, C_XG // 256 + 1)),
        _bs((NMEM, 2 * XW), lambda i: (0, 0)),
        _bs((1, XD), lambda i: (0, 0)), _bs((1, XD), lambda i: (0, 0)),
    ]


def _xattn_load(q0, q1, g0, g1, mkv_ref):
    qs = [(q0, q1)[h // 2][:, (h % 2) * XD:(h % 2 + 1) * XD] for h in range(XH)]
    gs = [(g0, g1)[h // 2][:, (h % 2) * XD:(h % 2 + 1) * XD] for h in range(XH)]
    kms = [mkv_ref[:, h * XD:(h + 1) * XD] for h in range(XH)]
    vms = [mkv_ref[:, XW + h * XD:XW + (h + 1) * XD] for h in range(XH)]
    return qs, gs, kms, vms


def _xattn_fwd(proj, mkv, gxq, gxk):
    s = proj.shape[0]

    def body(q0, q1, g0, g1, mkv_ref, gxq_ref, gxk_ref, o_ref):
        qs, gs, kms, vms = _xattn_load(q0, q1, g0, g1, mkv_ref)
        outs = _xattn_tile(qs, gs, kms, vms, gxq_ref[...], gxk_ref[...])
        for h in range(XH):
            o_ref[:, h * XD:(h + 1) * XD] = outs[h]

    return pl.pallas_call(
        body, name="xattn_fwd", grid=(s // XT,),
        in_specs=_xattn_specs(), out_specs=_bs((XT, XW), lambda i: (i, 0)),
        out_shape=jax.ShapeDtypeStruct((s, XW), F32),
        compiler_params=_params(("arbitrary",)),
    )(proj, proj, proj, proj, mkv, gxq, gxk)


def _xattn_bwd(proj, mkv, gxq, gxk, dy):
    s = proj.shape[0]

    def body(q0, q1, g0, g1, mkv_ref, gxq_ref, gxk_ref, dy_ref, dq_ref, dg_ref, dmkv_ref, dgxq_ref, dgxk_ref):
        @pl.when(pl.program_id(0) == 0)
        def _():
            dmkv_ref[...] = jnp.zeros_like(dmkv_ref)
            dgxq_ref[...] = jnp.zeros_like(dgxq_ref)
            dgxk_ref[...] = jnp.zeros_like(dgxk_ref)

        qs, gs, kms, vms = _xattn_load(q0, q1, g0, g1, mkv_ref)
        _, vjp = jax.vjp(_xattn_tile, qs, gs, kms, vms, gxq_ref[...], gxk_ref[...])
        dqs, dgs, dkms, dvms, dgxq, dgxk = vjp([dy_ref[:, h * XD:(h + 1) * XD] for h in range(XH)])
        for h in range(XH):
            dq_ref[:, h * XD:(h + 1) * XD] = dqs[h]
            dg_ref[:, h * XD:(h + 1) * XD] = dgs[h]
            dmkv_ref[:, h * XD:(h + 1) * XD] += dkms[h]
            dmkv_ref[:, XW + h * XD:XW + (h + 1) * XD] += dvms[h]
        dgxq_ref[...] += dgxq
        dgxk_ref[...] += dgxk

    whole = lambda shape: _bs(shape, lambda i: (0, 0))
    return pl.pallas_call(
        body, name="xattn_bwd", grid=(s // XT,),
        in_specs=_xattn_specs() + [_bs((XT, XW), lambda i: (i, 0))],
        out_specs=[_bs((XT, XW), lambda i: (i, 0)), _bs((XT, XW), lambda i: (i, 0)), whole((NMEM, 2 * XW)),
                   whole((1, XD)), whole((1, XD))],
        out_shape=[jax.ShapeDtypeStruct((s, XW), F32), jax.ShapeDtypeStruct((s, XW), F32),
                   jax.ShapeDtypeStruct((NMEM, 2 * XW), F32), jax.ShapeDtypeStruct((1, XD), F32),
                   jax.ShapeDtypeStruct((1, XD), F32)],
        compiler_params=_params(("arbitrary",)),
    )(proj, proj, proj, proj, mkv, gxq, gxk, dy)


def _mem_bwd(mem, g, dmn):
    def body(m_ref, dmn_ref, o_ref):
        m = m_ref[...]
        r = lax.rsqrt(jnp.mean(m * m, axis=-1, keepdims=True) + NORM_EPS)
        o_ref[...] = jnp.sum(dmn_ref[...] * m * r, axis=0, keepdims=True)

    del g
    return pl.pallas_call(body, name="mem_norm_bwd", out_shape=jax.ShapeDtypeStruct((1, D), F32),
                          compiler_params=_params())(mem, dmn)


SHIFT_W = 512


def _shift_rows(p, s):
    row = lax.broadcasted_iota(jnp.int32, p.shape, 0)
    prev = jnp.where(row == 0, 0.0, pltpu.roll(p, 1, 0))
    nxt = jnp.where(row == s - 1, 0.0, pltpu.roll(p, s - 1, 0))
    return prev, nxt


def _shift_fwd(proj, mu):
    s = proj.shape[0]

    def body(p_ref, mu_ref, o_ref):
        p = p_ref[...]
        prev, nxt = _shift_rows(p, s)
        o_ref[...] = p + mu_ref[...] * (0.5 * (prev + nxt) - p)

    return pl.pallas_call(
        body, name="shift_fwd", grid=(RSW // SHIFT_W,),
        in_specs=[_bs((s, SHIFT_W), lambda j: (0, C_RS // SHIFT_W + j)), _bs((1, SHIFT_W), lambda j: (0, j))],
        out_specs=_bs((s, SHIFT_W), lambda j: (0, j)),
        out_shape=jax.ShapeDtypeStruct((s, RSW), F32),
        compiler_params=_params(("parallel",)),
    )(proj, mu)


def _shift_bwd(proj, mu, dps):
    s = proj.shape[0]

    def body(p_ref, mu_ref, g_ref, o_ref, dmu_ref):
        p, g, mu_v = p_ref[...], g_ref[...], mu_ref[...]
        prev, nxt = _shift_rows(p, s)
        dmu_ref[...] = jnp.sum(g * (0.5 * (prev + nxt) - p), axis=0, keepdims=True)
        mg = mu_v * g
        down, up = _shift_rows(mg, s)
        o_ref[...] = g * (1.0 - mu_v) + 0.5 * (down + up)

    return pl.pallas_call(
        body, name="shift_bwd", grid=(RSW // SHIFT_W,),
        in_specs=[_bs((s, SHIFT_W), lambda j: (0, C_RS // SHIFT_W + j)), _bs((1, SHIFT_W), lambda j: (0, j)),
                  _bs((s, SHIFT_W), lambda j: (0, j))],
        out_specs=[_bs((s, SHIFT_W), lambda j: (0, j)), _bs((1, SHIFT_W), lambda j: (0, j))],
        out_shape=[jax.ShapeDtypeStruct((s, RSW), F32), jax.ShapeDtypeStruct((1, RSW), F32)],
        compiler_params=_params(("parallel",)),
    )(proj, mu, dps)


def _pre_tile(k, wf, wb, af, ab, k_k, k_a, w0s, w2s, a0s, a2s, seg):
    kx = k * k_k
    ss = _hdot(kx * kx, seg)
    kk = kx / jnp.maximum(jnp.sqrt(ss), 1e-12)
    outs = [kk]
    for d, (w_in, a_in) in enumerate(((wf, af), (wb, ab))):
        z = w0s[d] + _mm_nn(jnp.tanh(w_in), w2s[d])
        wd = -_softplus(-z) - 0.5
        dec = jnp.exp(-jnp.exp(wd))
        ad = jax.nn.sigmoid(a0s[d] + _mm_nn(a_in, a2s[d]))
        kd = k * (1.0 + (ad - 1.0) * k_a)
        outs += [dec, kd, kk * ad]
    return outs


PT = 256


def _pre_load(ps_ref, kk_ref, ka_ref, w0_ref, w2_ref, a0_ref, a2_ref):
    k = ps_ref[:, RW:2 * RW]
    wf, wb = ps_ref[:, 3 * RW:3 * RW + 64], ps_ref[:, 3 * RW + 64:3 * RW + 128]
    af, ab = ps_ref[:, 3 * RW + 128:3 * RW + 192], ps_ref[:, 3 * RW + 192:3 * RW + 256]
    w0s = [w0_ref[0:1, :], w0_ref[1:2, :]]
    a0s = [a0_ref[0:1, :], a0_ref[1:2, :]]
    w2s = [w2_ref[0], w2_ref[1]]
    a2s = [a2_ref[0], a2_ref[1]]
    return (k, wf, wb, af, ab, kk_ref[...], ka_ref[...], w0s, w2s, a0s, a2s)


def _pre_specs():
    c = lambda shape: _bs(shape, lambda i: tuple(0 for _ in shape))
    return [_bs((PT, RSW), lambda i: (i, 0)), c((1, RW)), c((1, RW)), c((2, RW)), c((2, 64, RW)), c((2, RW)),
            c((2, 64, RW))]


def _pre_fwd(ps, k_k, k_a, w0, w2, a0, a2):
    s = ps.shape[0]

    def body(ps_ref, kk_ref, ka_ref, w0_ref, w2_ref, a0_ref, a2_ref, *outs):
        args = _pre_load(ps_ref, kk_ref, ka_ref, w0_ref, w2_ref, a0_ref, a2_ref)
        res = _pre_tile(*args, _seg_matrix(RW, HD))
        for o_ref, v in zip(outs, res):
            o_ref[...] = v

    return pl.pallas_call(
        body, name="rwkv_pre_fwd", grid=(s // PT,),
        in_specs=_pre_specs(), out_specs=[_bs((PT, RW), lambda i: (i, 0))] * 7,
        out_shape=[jax.ShapeDtypeStruct((s, RW), F32)] * 7,
        compiler_params=_params(("parallel",)),
    )(ps, k_k, k_a, w0, w2, a0, a2)


def _pre_bwd(ps, k_k, k_a, w0, w2, a0, a2, dr, dv, cts):
    s = ps.shape[0]

    def body(ps_ref, kk_ref, ka_ref, w0_ref, w2_ref, a0_ref, a2_ref, dr_ref, dv_ref, c0, c1, c2, c3, c4, c5, c6,
             dps_ref, dkk_ref, dka_ref, dw0_ref, dw2_ref, da0_ref, da2_ref):
        @pl.when(pl.program_id(0) == 0)
        def _():
            for r in (dkk_ref, dka_ref, dw0_ref, dw2_ref, da0_ref, da2_ref):
                r[...] = jnp.zeros_like(r)

        args = _pre_load(ps_ref, kk_ref, ka_ref, w0_ref, w2_ref, a0_ref, a2_ref)
        seg = _seg_matrix(RW, HD)
        _, vjp = jax.vjp(lambda *a: _pre_tile(*a, seg), *args)
        dk, dwf, dwb, daf, dab, dk_k, dk_a, dw0s, dw2s, da0s, da2s = vjp([c[...] for c in (c0, c1, c2, c3, c4, c5, c6)])
        dps_ref[:, 0:RW] = dr_ref[...]
        dps_ref[:, RW:2 * RW] = dk
        dps_ref[:, 2 * RW:3 * RW] = dv_ref[...]
        for j, t in enumerate((dwf, dwb, daf, dab)):
            dps_ref[:, 3 * RW + 64 * j:3 * RW + 64 * (j + 1)] = t
        dkk_ref[...] += dk_k
        dka_ref[...] += dk_a
        for d in range(2):
            dw0_ref[d:d + 1, :] += dw0s[d]
            da0_ref[d:d + 1, :] += da0s[d]
            dw2_ref[d] += dw2s[d]
            da2_ref[d] += da2s[d]

    c = lambda shape: _bs(shape, lambda i: tuple(0 for _ in shape))
    row = _bs((PT, RW), lambda i: (i, 0))
    return pl.pallas_call(
        body, name="rwkv_pre_bwd", grid=(s // PT,),
        in_specs=_pre_specs() + [row] * 9,
        out_specs=[_bs((PT, RSW), lambda i: (i, 0)), c((1, RW)), c((1, RW)), c((2, RW)), c((2, 64, RW)), c((2, RW)),
                   c((2, 64, RW))],
        out_shape=[jax.ShapeDtypeStruct((s, RSW), F32), jax.ShapeDtypeStruct((1, RW), F32),
                   jax.ShapeDtypeStruct((1, RW), F32), jax.ShapeDtypeStruct((2, RW), F32),
                   jax.ShapeDtypeStruct((2, 64, RW), F32), jax.ShapeDtypeStruct((2, RW), F32),
                   jax.ShapeDtypeStruct((2, 64, RW), F32)],
        compiler_params=_params(("arbitrary",)),
    )(ps, k_k, k_a, w0, w2, a0, a2, dr, dv, *cts)


def _post_tile(y0, y1, r, v, kd0, kd1, rg, r_k, ln_w, ln_b, seg):
    ysum = y0 + y1
    bonus = (_hdot(r * kd0 * r_k, seg) + _hdot(r * kd1 * r_k, seg)) * v
    mean = _hdot(ysum, seg) * (1.0 / HD)
    cen = ysum - mean
    var = _hdot(cen * cen, seg) * (1.0 / HD)
    y = cen * lax.rsqrt(var + GN_EPS) * ln_w + ln_b + bonus
    return y * _silu(rg)


def _post_specs():
    row = _bs((PT, RW), lambda i: (i, 0))
    c = _bs((1, RW), lambda i: (0, 0))
    return [row, row, _bs((PT, RW), lambda i: (i, 0)), _bs((PT, RW), lambda i: (i, 2)), row, row,
            _bs((PT, RW), lambda i: (i, C_RG // RW)), c, c, c]


def _post_fwd(y0, y1, ps, kd0, kd1, proj, r_k, ln_w, ln_b):
    s = ps.shape[0]

    def body(y0_ref, y1_ref, r_ref, v_ref, kd0_ref, kd1_ref, rg_ref, rk_ref, lw_ref, lb_ref, o_ref):
        o_ref[...] = _post_tile(y0_ref[...], y1_ref[...], r_ref[...], v_ref[...], kd0_ref[...], kd1_ref[...],
                                rg_ref[...], rk_ref[...], lw_ref[...], lb_ref[...], _seg_matrix(RW, HD))

    return pl.pallas_call(
        body, name="rwkv_post_fwd", grid=(s // PT,),
        in_specs=_post_specs(), out_specs=_bs((PT, RW), lambda i: (i, 0)),
        out_shape=jax.ShapeDtypeStruct((s, RW), F32),
        compiler_params=_params(("parallel",)),
    )(y0, y1, ps, ps, kd0, kd1, proj, r_k, ln_w, ln_b)


def _post_bwd(y0, y1, ps, kd0, kd1, proj, r_k, ln_w, ln_b, dy):
    s = ps.shape[0]

    def body(y0_ref, y1_ref, r_ref, v_ref, kd0_ref, kd1_ref, rg_ref, rk_ref, lw_ref, lb_ref, dy_ref,
             dys_ref, dr_ref, dv_ref, dkd0_ref, dkd1_ref, drg_ref, drk_ref, dlw_ref, dlb_ref):
        @pl.when(pl.program_id(0) == 0)
        def _():
            for r in (drk_ref, dlw_ref, dlb_ref):
                r[...] = jnp.zeros_like(r)

        seg = _seg_matrix(RW, HD)
        args = [t[...] for t in (y0_ref, y1_ref, r_ref, v_ref, kd0_ref, kd1_ref, rg_ref, rk_ref, lw_ref, lb_ref)]
        _, vjp = jax.vjp(lambda *a: _post_tile(*a, seg), *args)
        dy0, _, dr, dv, dkd0, dkd1, drg, drk, dlw, dlb = vjp(dy_ref[...])
        dys_ref[...] = dy0
        dr_ref[...] = dr
        dv_ref[...] = dv
        dkd0_ref[...] = dkd0
        dkd1_ref[...] = dkd1
        drg_ref[...] = drg
        drk_ref[...] += drk
        dlw_ref[...] += dlw
        dlb_ref[...] += dlb

    row = _bs((PT, RW), lambda i: (i, 0))
    c = _bs((1, RW), lambda i: (0, 0))
    return pl.pallas_call(
        body, name="rwkv_post_bwd", grid=(s // PT,),
        in_specs=_post_specs() + [row], out_specs=[row] * 6 + [c] * 3,
        out_shape=[jax.ShapeDtypeStruct((s, RW), F32)] * 6 + [jax.ShapeDtypeStruct((1, RW), F32)] * 3,
        compiler_params=_params(("arbitrary",)),
    )(y0, y1, ps, ps, kd0, kd1, proj, r_k, ln_w, ln_b, dy)


def _ones2():
    r = lax.broadcasted_iota(jnp.int32, (256, 128), 0) % 128 // HD
    c = lax.broadcasted_iota(jnp.int32, (256, 128), 1) // HD
    return (r == c).astype(BF16)


def _split(p):
    hi = p.astype(BF16)
    lo = (p - hi.astype(F32)).astype(BF16)
    return jnp.concatenate([hi, lo], axis=1)


def _to_t8(a):
    s = a.shape[0]
    t = a.reshape(s // 8, 8, NPAIR, 2, HD).transpose(0, 2, 4, 3, 1)
    t = jnp.pad(t, ((0, 0), (0, 0), (0, 0), (0, 0), (0, HD - 8))).reshape(s // 8, NPAIR, HD, 128)
    hi = t.astype(BF16)
    lo = (t - hi.astype(F32)).astype(BF16)
    return jnp.concatenate([hi, lo], axis=-1)


def _from_t8(t8):
    g = t8.shape[0]
    t = t8.reshape(g, NPAIR, HD, 2, HD)[..., :8]
    return t.transpose(0, 4, 1, 3, 2).reshape(g * 8, RW)


def _scan_specs(direction, nc, fwd_order):
    def tb(c):
        sc = c if fwd_order else nc - 1 - c
        return sc if direction == 0 else nc - 1 - sc

    row = _bs((TC, RW), lambda c: (tb(c), 0))
    t8_in = _bs((TC // 8, NPAIR, HD, 256), lambda c: (tb(c), 0, 0, 0))
    t8_out = _bs((TC // 8, NPAIR, HD, 128), lambda c: (tb(c), 0, 0, 0))
    return row, t8_in, t8_out


def _put_t8(ref, g, u, tiles):
    for p in range(NPAIR):
        ref[g, p, :, u:u + 1] = tiles[p][:, u:u + 1]
        ref[g, p, :, HD + u:HD + u + 1] = tiles[p][:, HD + u:HD + u + 1]


def _scan_fwd(dec, kd, b, ps, kk, vl, direction):
    s = dec.shape[0]
    nc, ng = s // TC, TC // 8
    row, t8_in, t8_out = _scan_specs(direction, nc, True)
    n = NPAIR * HD

    def body(dec_ref, kd_ref, b_ref, r_ref, kk_ref, vl_ref, y8_ref, ck_ref, st):
        @pl.when(pl.program_id(0) == 0)
        def _():
            st[...] = jnp.zeros_like(st)

        ck_ref[0] = st[...]
        ones2 = _ones2()
        lane_u = lax.broadcasted_iota(jnp.int32, (HD, 256), 1) % HD
        tiles = lambda res, k: [res[k * n + p * HD:k * n + (p + 1) * HD] for p in range(NPAIR)]

        def group(gi, carry):
            g = gi if direction == 0 else ng - 1 - gi
            rows8 = pl.ds(pl.multiple_of(g * 8, 8), 8)
            d8, k8, b8, r8, kk8 = (q[rows8, :] for q in (dec_ref, kd_ref, b_ref, r_ref, kk_ref))
            pc = [slice(p * 128, (p + 1) * 128) for p in range(NPAIR)]
            ss = [st[p] for p in range(NPAIR)]
            u_prev = None
            for ui in range(8):
                u = ui if direction == 0 else 7 - ui
                lhs = [_split(ss[p] * kk8[u:u + 1, pc[p]]) for p in range(NPAIR)]
                for p in range(NPAIR):
                    vt = vl_ref[g, p]
                    lhs.append(jnp.where(lane_u == u, vt, jnp.zeros_like(vt)))
                if u_prev is not None:
                    lhs += [_split(ss[p] * r8[u_prev:u_prev + 1, pc[p]]) for p in range(NPAIR)]
                res = jnp.dot(jnp.concatenate(lhs, axis=0), ones2, preferred_element_type=F32)
                if u_prev is not None:
                    _put_t8(y8_ref, g, u_prev, tiles(res, 2))
                sa, vb = tiles(res, 0), tiles(res, 1)
                for p in range(NPAIR):
                    ss[p] = ss[p] * d8[u:u + 1, pc[p]] - sa[p] * b8[u:u + 1, pc[p]] + vb[p] * k8[u:u + 1, pc[p]]
                u_prev = u
            lhs = [_split(ss[p] * r8[u_prev:u_prev + 1, pc[p]]) for p in range(NPAIR)]
            res = jnp.dot(jnp.concatenate(lhs, axis=0), ones2, preferred_element_type=F32)
            _put_t8(y8_ref, g, u_prev, tiles(res, 0))
            for p in range(NPAIR):
                st[p] = ss[p]
            return carry

        lax.fori_loop(0, ng, group, 0)

    return pl.pallas_call(
        body, name=f"rwkv_scan_fwd{direction}", grid=(nc,),
        in_specs=[row, row, row, row, row, t8_in],
        out_specs=[t8_out, _bs((1, NPAIR, HD, 128), lambda c: (c, 0, 0, 0))],
        out_shape=[jax.ShapeDtypeStruct((s // 8, NPAIR, HD, 128), F32),
                   jax.ShapeDtypeStruct((nc, NPAIR, HD, 128), F32)],
        scratch_shapes=[pltpu.VMEM((NPAIR, HD, 128), F32)],
        compiler_params=_params(("arbitrary",)),
    )(dec, kd, b, ps, kk, vl)


def _scan_bwd(dec, kd, b, ps, kk, vl, dyl, ck, direction):
    s = dec.shape[0]
    nc, ng = s // TC, TC // 8
    row, t8_in, t8_out = _scan_specs(direction, nc, False)
    n = NPAIR * HD

    def body(dec_ref, kd_ref, b_ref, r_ref, kk_ref, vl_ref, dyl_ref, ck_ref,
             dr_ref, dd_ref, db_ref, dk_ref, dkk_ref, dv8_ref, st, sa_s, vb_s, dy_s, ds):
        @pl.when(pl.program_id(0) == 0)
        def _():
            ds[...] = jnp.zeros_like(ds)

        st[0] = ck_ref[0]
        ones2 = _ones2()
        lane_u = lax.broadcasted_iota(jnp.int32, (HD, 256), 1) % HD
        row_id = lax.broadcasted_iota(jnp.int32, (8, 128), 0)
        pc = [slice(p * 128, (p + 1) * 128) for p in range(NPAIR)]
        tiles = lambda res, k: [res[k * n + p * HD:k * n + (p + 1) * HD] for p in range(NPAIR)]

        def fgroup(gi, carry):
            g = gi if direction == 0 else ng - 1 - gi
            rows8 = pl.ds(pl.multiple_of(g * 8, 8), 8)
            d8, k8, b8, kk8 = (q[rows8, :] for q in (dec_ref, kd_ref, b_ref, kk_ref))
            ss = [st[gi * 8, p] for p in range(NPAIR)]
            for ui in range(8):
                u = ui if direction == 0 else 7 - ui
                i = gi * 8 + ui
                lhs = [_split(ss[p] * kk8[u:u + 1, pc[p]]) for p in range(NPAIR)]
                for ref in (vl_ref, dyl_ref):
                    for p in range(NPAIR):
                        t = ref[g, p]
                        lhs.append(jnp.where(lane_u == u, t, jnp.zeros_like(t)))
                res = jnp.dot(jnp.concatenate(lhs, axis=0), ones2, preferred_element_type=F32)
                sa, vb, dyb = tiles(res, 0), tiles(res, 1), tiles(res, 2)
                for p in range(NPAIR):
                    sa_s[i, p] = sa[p]
                    vb_s[i, p] = vb[p]
                    dy_s[i, p] = dyb[p]
                    ss[p] = ss[p] * d8[u:u + 1, pc[p]] - sa[p] * b8[u:u + 1, pc[p]] + vb[p] * k8[u:u + 1, pc[p]]
                    st[i + 1, p] = ss[p]
            return carry

        lax.fori_loop(0, ng, fgroup, 0)

        def bgroup(gj, carry):
            gi = ng - 1 - gj
            g = gi if direction == 0 else ng - 1 - gi
            rows8 = pl.ds(pl.multiple_of(g * 8, 8), 8)
            d8, k8, b8, r8, kk8 = (q[rows8, :] for q in (dec_ref, kd_ref, b_ref, r_ref, kk_ref))
            dss = [ds[p] for p in range(NPAIR)]
            acc = [[jnp.zeros((8, 128), F32) for _ in range(5)] for _ in range(NPAIR)]
            for uj in range(8):
                ui = 7 - uj
                u = ui if direction == 0 else 7 - ui
                i = gi * 8 + ui
                dyb = [dy_s[i, p] for p in range(NPAIR)]
                for p in range(NPAIR):
                    dss[p] = dss[p] + dyb[p] * r8[u:u + 1, pc[p]]
                lhs = [_split(dss[p] * b8[u:u + 1, pc[p]]) for p in range(NPAIR)]
                lhs += [_split(dss[p] * k8[u:u + 1, pc[p]]) for p in range(NPAIR)]
                res = jnp.dot(jnp.concatenate(lhs, axis=0), ones2, preferred_element_type=F32)
                dsa, dvb = tiles(res, 0), tiles(res, 1)
                _put_t8(dv8_ref, g, u, dvb)
                for p in range(NPAIR):
                    sp, sn = st[i, p], st[i + 1, p]
                    outs = (jnp.sum(sn * dyb[p], axis=0, keepdims=True), jnp.sum(dss[p] * sp, axis=0, keepdims=True),
                            -jnp.sum(dss[p] * sa_s[i, p], axis=0, keepdims=True),
                            jnp.sum(dss[p] * vb_s[i, p], axis=0, keepdims=True),
                            -jnp.sum(sp * dsa[p], axis=0, keepdims=True))
                    acc[p] = [jnp.where(row_id == u, o, a_) for o, a_ in zip(outs, acc[p])]
                    dss[p] = dss[p] * d8[u:u + 1, pc[p]] - dsa[p] * kk8[u:u + 1, pc[p]]
            for p in range(NPAIR):
                ds[p] = dss[p]
                for o_ref, a_ in zip((dr_ref, dd_ref, db_ref, dk_ref, dkk_ref), acc[p]):
                    o_ref[rows8, pc[p]] = a_
            return carry

        lax.fori_loop(0, ng, bgroup, 0)

    chunk = lambda k: pltpu.VMEM((k, NPAIR, HD, 128), F32)
    return pl.pallas_call(
        body, name=f"rwkv_scan_bwd{direction}", grid=(nc,),
        in_specs=[row, row, row, row, row, t8_in, t8_in, _bs((1, NPAIR, HD, 128), lambda c: (nc - 1 - c, 0, 0, 0))],
        out_specs=[row] * 5 + [t8_out],
        out_shape=[jax.ShapeDtypeStruct((s, RW), F32)] * 5 + [jax.ShapeDtypeStruct((s // 8, NPAIR, HD, 128), F32)],
        scratch_shapes=[chunk(TC + 1), chunk(TC), chunk(TC), chunk(TC), pltpu.VMEM((NPAIR, HD, 128), F32)],
        compiler_params=_params(("arbitrary",)),
    )(dec, kd, b, ps, kk, vl, dyl, ck)


MT = 256
MN = 256


def _merge_fwd(ya, yr, yx, wa, wr, wx, proj, gate_b):
    s = ya.shape[0]

    def body(ya_ref, yr_ref, yx_ref, wa_ref, wr_ref, wx_ref, m0, m1, m2, b0, b1, b2, o_ref):
        acc = jnp.zeros((MT, MN), F32)
        for y_ref, w_ref, m_ref, b_ref in ((ya_ref, wa_ref, m0, b0), (yr_ref, wr_ref, m1, b1), (yx_ref, wx_ref, m2, b2)):
            u = _dot(y_ref[...], w_ref[...], ((1,), (0,)))
            acc = acc + jax.nn.sigmoid(m_ref[...] + b_ref[...]) * u
        o_ref[...] = acc.astype(BF16)

    mg = lambda br: _bs((MT, MN), lambda i, j: (i, C_MG // MN + br * (D // MN) + j))
    gb = lambda br: _bs((1, MN), lambda i, j: (0, br * (D // MN) + j))
    return pl.pallas_call(
        body, name="merge_fwd", grid=(s // MT, D // MN),
        in_specs=[_bs((MT, RW), lambda i, j: (i, 0)), _bs((MT, RW), lambda i, j: (i, 0)), _bs((MT, XW), lambda i, j: (i, 0)),
                  _bs((RW, MN), lambda i, j: (0, j)), _bs((RW, MN), lambda i, j: (0, j)), _bs((XW, MN), lambda i, j: (0, j)),
                  mg(0), mg(1), mg(2), gb(0), gb(1), gb(2)],
        out_specs=_bs((MT, MN), lambda i, j: (i, j)),
        out_shape=jax.ShapeDtypeStruct((s, D), BF16),
        compiler_params=_params(("parallel", "arbitrary")),
    )(ya, yr, yx, wa, wr, wx, proj, proj, proj, gate_b, gate_b, gate_b)


def _out_fwd(merged, w_out, x, target):
    s = x.shape[0]
    tm, tn = min(512, s), 512

    def body(m_ref, w_ref, x_ref, t_ref, loss_ref, d_ref):
        @pl.when((pl.program_id(0) == 0) & (pl.program_id(1) == 0))
        def _():
            loss_ref[...] = jnp.zeros_like(loss_ref)

        out = x_ref[...] + jnp.dot(m_ref[...], w_ref[...], preferred_element_type=F32)
        err = out - t_ref[...]
        d_ref[...] = err * (1.0 / D)
        loss_ref[...] += jnp.sum(err * err)

    return pl.pallas_call(
        body, name="out_fwd", grid=(s // tm, D // tn),
        in_specs=[_bs((tm, D), lambda i, j: (i, 0)), _bs((D, tn), lambda i, j: (0, j)),
                  _bs((tm, tn), lambda i, j: (i, j)), _bs((tm, tn), lambda i, j: (i, j))],
        out_specs=[_bs((8, 128), lambda i, j: (0, 0)), _bs((tm, tn), lambda i, j: (i, j))],
        out_shape=[jax.ShapeDtypeStruct((8, 128), F32), jax.ShapeDtypeStruct((s, D), F32)],
        compiler_params=_params(("arbitrary", "arbitrary")),
    )(merged, w_out, x, target)


def _merge_bwd(ya, yr, yx, wa, wr, wx, proj, gate_b, dmerged):
    s = ya.shape[0]

    def body(ya_ref, yr_ref, yx_ref, wa_ref, wr_ref, wx_ref, m0, m1, m2, b0, b1, b2, dm_ref,
             dg0, dg1, dg2, du0, du1, du2, dya_ref, dyr_ref, dyx_ref):
        @pl.when(pl.program_id(1) == 0)
        def _():
            dya_ref[...] = jnp.zeros_like(dya_ref)
            dyr_ref[...] = jnp.zeros_like(dyr_ref)
            dyx_ref[...] = jnp.zeros_like(dyx_ref)

        dm = dm_ref[...]
        for y_ref, w_ref, m_ref, b_ref, dg_ref, du_ref, dy_ref in (
                (ya_ref, wa_ref, m0, b0, dg0, du0, dya_ref), (yr_ref, wr_ref, m1, b1, dg1, du1, dyr_ref),
                (yx_ref, wx_ref, m2, b2, dg2, du2, dyx_ref)):
            w = w_ref[...]
            u = _dot(y_ref[...], w, ((1,), (0,)))
            gt = jax.nn.sigmoid(m_ref[...] + b_ref[...])
            dg_ref[...] = (dm * u * gt * (1.0 - gt)).astype(BF16)
            du = (dm * gt).astype(BF16)
            du_ref[...] = du
            dy_ref[...] += _dot(du, w, ((1,), (1,)))

    mg = lambda br: _bs((MT, MN), lambda i, j: (i, C_MG // MN + br * (D // MN) + j))
    gb = lambda br: _bs((1, MN), lambda i, j: (0, br * (D // MN) + j))
    tile = _bs((MT, MN), lambda i, j: (i, j))
    return pl.pallas_call(
        body, name="merge_bwd", grid=(s // MT, D // MN),
        in_specs=[_bs((MT, RW), lambda i, j: (i, 0)), _bs((MT, RW), lambda i, j: (i, 0)), _bs((MT, XW), lambda i, j: (i, 0)),
                  _bs((RW, MN), lambda i, j: (0, j)), _bs((RW, MN), lambda i, j: (0, j)), _bs((XW, MN), lambda i, j: (0, j)),
                  mg(0), mg(1), mg(2), gb(0), gb(1), gb(2), tile],
        out_specs=[tile] * 6 + [_bs((MT, RW), lambda i, j: (i, 0)), _bs((MT, RW), lambda i, j: (i, 0)),
                                _bs((MT, XW), lambda i, j: (i, 0))],
        out_shape=[jax.ShapeDtypeStruct((s, D), BF16)] * 6 + [jax.ShapeDtypeStruct((s, RW), F32),
                                                               jax.ShapeDtypeStruct((s, RW), F32),
                                                               jax.ShapeDtypeStruct((s, XW), F32)],
        compiler_params=_params(("parallel", "arbitrary")),
    )(ya, yr, yx, wa, wr, wx, proj, proj, proj, gate_b, gate_b, gate_b, dmerged)


def _colsum(a, name):
    m, n = a.shape
    tm, tn = min(512, m), 512

    def body(a_ref, o_ref):
        @pl.when(pl.program_id(1) == 0)
        def _():
            o_ref[...] = jnp.zeros_like(o_ref)

        o_ref[...] += jnp.sum(a_ref[...].astype(F32), axis=0, keepdims=True)

    return pl.pallas_call(
        body, name=name, grid=(n // tn, m // tm),
        in_specs=[_bs((tm, tn), lambda j, i: (i, j))], out_specs=_bs((1, tn), lambda j, i: (0, j)),
        out_shape=jax.ShapeDtypeStruct((1, n), F32),
        compiler_params=_params(("parallel", "arbitrary")),
    )(a)


def _in_bwd(dproj, w_in, x, g, dout):
    s = x.shape[0]
    tm, tk = 256, 896
    nk = NIN // tk

    def body(dp_ref, w_ref, x_ref, g_ref, do_ref, gx_ref, gg_ref, acc):
        i, kk = pl.program_id(0), pl.program_id(1)

        @pl.when((i == 0) & (kk == 0))
        def _():
            gg_ref[...] = jnp.zeros_like(gg_ref)

        @pl.when(kk == 0)
        def _():
            acc[...] = jnp.zeros_like(acc)

        acc[...] += _dot(dp_ref[...], w_ref[...], ((1,), (1,)))

        @pl.when(kk == nk - 1)
        def _():
            xv, dh, gv = x_ref[...], acc[...], g_ref[...]
            r = lax.rsqrt(jnp.mean(xv * xv, axis=-1, keepdims=True) + NORM_EPS)
            xn = xv * r
            gg_ref[...] += jnp.sum(dh * xn, axis=0, keepdims=True)
            dxn = dh * gv
            dx = r * (dxn - xn * jnp.mean(dxn * xn, axis=-1, keepdims=True))
            gx_ref[...] = do_ref[...] + dx

    return pl.pallas_call(
        body, name="in_bwd", grid=(s // tm, nk),
        in_specs=[_bs((tm, tk), lambda i, kk: (i, kk)), _bs((D, tk), lambda i, kk: (0, kk)),
                  _bs((tm, D), lambda i, kk: (i, 0)), _bs((1, D), lambda i, kk: (0, 0)), _bs((tm, D), lambda i, kk: (i, 0))],
        out_specs=[_bs((tm, D), lambda i, kk: (i, 0)), _bs((1, D), lambda i, kk: (0, 0))],
        out_shape=[jax.ShapeDtypeStruct((s, D), F32), jax.ShapeDtypeStruct((1, D), F32)],
        scratch_shapes=[pltpu.VMEM((tm, D), F32)],
        compiler_params=_params(("arbitrary", "arbitrary")),
    )(dproj, w_in, x, g, dout)


def _adamw_math(w, g, m, v):
    m = ADAM_B1 * m + (1.0 - ADAM_B1) * g
    v = ADAM_B2 * v + (1.0 - ADAM_B2) * jnp.square(g)
    m_hat = m / (1.0 - ADAM_B1 ** ADAM_STEP)
    v_hat = v / (1.0 - ADAM_B2 ** ADAM_STEP)
    delta = -ADAM_LR * (m_hat / (jnp.sqrt(v_hat) + ADAM_EPS) + ADAM_WD * w)
    return delta, m, v


def _adamw(parts, w, m, v, name):
    rows, cols = w.shape
    tr = rows
    for cand in (256, 128, 64, 32, 16, 8):
        if rows % cand == 0 and cand * cols * 4 <= (1 << 20):
            tr = cand
            break
    n = len(parts)

    def body(*refs):
        g = refs[0][...].astype(F32)
        for r in refs[1:n]:
            g = g + r[...].astype(F32)
        w_ref, m_ref, v_ref, g_out, d_out, m_out, v_out = refs[n:]
        delta, m_new, v_new = _adamw_math(w_ref[...], g, m_ref[...], v_ref[...])
        g_out[...] = g
        d_out[...] = delta
        m_out[...] = m_new
        v_out[...] = v_new

    spec = _bs((tr, cols), lambda i: (i, 0))
    return pl.pallas_call(
        body, name=name, grid=(rows // tr,),
        in_specs=[spec] * (n + 3), out_specs=[spec] * 4,
        out_shape=[jax.ShapeDtypeStruct((rows, cols), F32)] * 4,
        compiler_params=_params(("parallel",)),
    )(*parts, w, m, v)


def _sum_parts(parts, name):
    rows, cols = parts[0].shape
    tr = rows
    for cand in (256, 128, 64, 32, 16, 8):
        if rows % cand == 0 and cand * cols * 4 <= (1 << 20):
            tr = cand
            break

    def body(*refs):
        acc = refs[0][...].astype(F32)
        for r in refs[1:-1]:
            acc = acc + r[...].astype(F32)
        refs[-1][...] = acc

    spec = _bs((tr, cols), lambda i: (i, 0))
    return pl.pallas_call(
        body, name=name, grid=(rows // tr,), in_specs=[spec] * len(parts), out_specs=spec,
        out_shape=jax.ShapeDtypeStruct((rows, cols), F32), compiler_params=_params(("parallel",)),
    )(*parts)


ANY = pl.BlockSpec(memory_space=pl.ANY)


def _other_chips(x, y):
    return [(1 - x, y), (x, 1 - y), (1 - x, 1 - y)]


def _gather_shards(arrays, name):
    n = len(arrays)

    def body(*refs):
        ins, outs = refs[:n], refs[n:2 * n]
        send_sems, recv_sems, local_sems = refs[2 * n:]
        x, y, c = lax.axis_index("x"), lax.axis_index("y"), lax.axis_index("c")
        me = 2 * x + y
        chips = _other_chips(x, y)
        locals_, sends = [], []
        for i in range(n):
            cp = pltpu.make_async_copy(ins[i], outs[i].at[me], local_sems.at[i])
            cp.start()
            locals_.append(cp)
            for j, (px, py) in enumerate(chips):
                rc = pltpu.make_async_remote_copy(
                    src_ref=ins[i], dst_ref=outs[i].at[me], send_sem=send_sems.at[3 * i + j],
                    recv_sem=recv_sems.at[3 * i + j], device_id=(px, py, c), device_id_type=MESH)
                rc.start()
                sends.append(rc)
        for i in range(n):
            for j, (px, py) in enumerate(chips):
                pltpu.make_async_remote_copy(
                    src_ref=ins[i], dst_ref=outs[i].at[2 * px + py], send_sem=send_sems.at[3 * i + j],
                    recv_sem=recv_sems.at[3 * i + j], device_id=(px, py, c), device_id_type=MESH).wait_recv()
        for rc in sends:
            rc.wait_send()
        for cp in locals_:
            cp.wait()

    return pl.pallas_call(
        body, name=name, in_specs=[ANY] * n, out_specs=[ANY] * n,
        out_shape=[jax.ShapeDtypeStruct((4,) + a.shape, a.dtype) for a in arrays],
        scratch_shapes=[pltpu.SemaphoreType.DMA((3 * n,)), pltpu.SemaphoreType.DMA((3 * n,)),
                        pltpu.SemaphoreType.DMA((n,))],
        compiler_params=pltpu.CompilerParams(has_side_effects=True),
    )(*arrays)


def _scatter_shards(stacks, name):
    n = len(stacks)

    def body(*refs):
        ins, outs = refs[:n], refs[n:2 * n]
        send_sems, recv_sems = refs[2 * n:]
        x, y, c = lax.axis_index("x"), lax.axis_index("y"), lax.axis_index("c")
        chips = _other_chips(x, y)
        sends = []
        for i in range(n):
            for j, (px, py) in enumerate(chips):
                rc = pltpu.make_async_remote_copy(
                    src_ref=ins[i].at[2 * px + py], dst_ref=outs[i].at[j], send_sem=send_sems.at[3 * i + j],
                    recv_sem=recv_sems.at[3 * i + j], device_id=(px, py, c), device_id_type=MESH)
                rc.start()
                sends.append(rc)
        for rc in sends:
            rc.wait_recv()
        for rc in sends:
            rc.wait_send()

    return pl.pallas_call(
        body, name=name, in_specs=[ANY] * n, out_specs=[ANY] * n,
        out_shape=[jax.ShapeDtypeStruct((3,) + a.shape[1:], a.dtype) for a in stacks],
        scratch_shapes=[pltpu.SemaphoreType.DMA((3 * n,)), pltpu.SemaphoreType.DMA((3 * n,))],
        compiler_params=pltpu.CompilerParams(has_side_effects=True),
    )(*stacks)


def _swap_sibling(arrays, name):
    n = len(arrays)

    def body(*refs):
        ins, outs = refs[:n], refs[n:2 * n]
        send_sems, recv_sems = refs[2 * n:]
        sib = (lax.axis_index("x"), lax.axis_index("y"), 1 - lax.axis_index("c"))
        cps = []
        for i in range(n):
            rc = pltpu.make_async_remote_copy(src_ref=ins[i], dst_ref=outs[i], send_sem=send_sems.at[i],
                                              recv_sem=recv_sems.at[i], device_id=sib, device_id_type=MESH)
            rc.start()
            cps.append(rc)
        for rc in cps:
            rc.wait_recv()
        for rc in cps:
            rc.wait_send()

    return pl.pallas_call(
        body, name=name, in_specs=[ANY] * n, out_specs=[ANY] * n,
        out_shape=[jax.ShapeDtypeStruct(a.shape, a.dtype) for a in arrays],
        scratch_shapes=[pltpu.SemaphoreType.DMA((n,)), pltpu.SemaphoreType.DMA((n,))],
        compiler_params=pltpu.CompilerParams(has_side_effects=True),
    )(*arrays)


def _all_reduce_small(v):
    rows = v.shape[0]

    def body(v_ref, o_ref, buf, send_sems, recv_sems):
        x, y, c = lax.axis_index("x"), lax.axis_index("y"), lax.axis_index("c")
        me = 4 * x + 2 * y + c
        buf[me] = v_ref[...]
        cps = []
        for kbits in range(1, 8):
            bx, by, bc = (kbits >> 2) & 1, (kbits >> 1) & 1, kbits & 1
            px = jnp.where(bx == 1, 1 - x, x)
            py = jnp.where(by == 1, 1 - y, y)
            pc = jnp.where(bc == 1, 1 - c, c)
            rc = pltpu.make_async_remote_copy(src_ref=v_ref, dst_ref=buf.at[me], send_sem=send_sems.at[kbits - 1],
                                              recv_sem=recv_sems.at[kbits - 1], device_id=(px, py, pc),
                                              device_id_type=MESH)
            rc.start()
            cps.append((rc, 4 * px + 2 * py + pc))
        for kbits, (rc, src) in enumerate(cps):
            pltpu.make_async_remote_copy(src_ref=v_ref, dst_ref=buf.at[src], send_sem=send_sems.at[kbits],
                                         recv_sem=recv_sems.at[kbits], device_id=(x, y, c),
                                         device_id_type=MESH).wait_recv()
        for rc, _ in cps:
            rc.wait_send()
        acc = buf[0]
        for d in range(1, 8):
            acc = acc + buf[d]
        o_ref[...] = acc

    return pl.pallas_call(
        body, name="all_reduce_small",
        in_specs=[pl.BlockSpec(memory_space=pltpu.VMEM)], out_specs=pl.BlockSpec(memory_space=pltpu.VMEM),
        out_shape=jax.ShapeDtypeStruct((rows, 128), F32),
        scratch_shapes=[pltpu.VMEM((8, rows, 128), F32), pltpu.SemaphoreType.DMA((7,)), pltpu.SemaphoreType.DMA((7,))],
        compiler_params=pltpu.CompilerParams(has_side_effects=True, vmem_limit_bytes=VMEM_LIMIT),
    )(v)


def _rope_tables(s):
    half = HD // 2
    inv = 10000.0 ** (-jnp.arange(half, dtype=F32) / half)
    ang = jnp.arange(s, dtype=F32)[:, None] * inv[None, :]
    cos, sin = jnp.cos(ang), jnp.sin(ang)
    return jnp.concatenate([cos, cos], axis=1), jnp.concatenate([sin, sin], axis=1)


def _local_step(x, mem, target, norm_g, mem_norm_g, w_in, gate_b, gq, gk, sink, wa, mu, k_k, k_a, r_k, w0, w2, a0, a2,
                ln_w, ln_b, wr, w_kv, gxq, gxk, wx, w_out):
    s = x.shape[0]
    cos, sin = _rope_tables(s)
    r_k = r_k.reshape(1, RW)

    proj, h = _proj_fwd(x, norm_g, w_in)
    ya = _attn_fwd(proj, cos, sin, gq, gk, sink)
    mkv, mn = _mem_kv(mem, mem_norm_g, w_kv)
    yx = _xattn_fwd(proj, mkv, gxq, gxk)
    ps = _shift_fwd(proj, mu)
    kk, dec0, kd0, b0, dec1, kd1, b1 = _pre_fwd(ps, k_k, k_a, w0, w2, a0, a2)
    v8 = _to_t8(ps[:, 2 * RW:3 * RW])
    y80, ck0 = _scan_fwd(dec0, kd0, b0, ps, kk, v8, 0)
    y81, ck1 = _scan_fwd(dec1, kd1, b1, ps, kk, v8, 1)
    y0, y1 = _from_t8(y80), _from_t8(y81)
    yr = _post_fwd(y0, y1, ps, kd0, kd1, proj, r_k, ln_w, ln_b)
    merged = _merge_fwd(ya, yr, yx, wa, wr, wx, proj, gate_b)
    loss_tile, dout = _out_fwd(merged, w_out, x, target)
    loss_sum = loss_tile[0, 0]

    g = {}
    dmerged = _matmul(dout, w_out, mode="nt", m=s, n=D, k=D, tm=min(512, s), tn=512, tk=512, name="dmerged")
    g["w_out"] = _matmul(merged, dout, mode="tn", m=D, n=D, k=s, tm=512, tn=512, tk=min(512, s), name="grad_w_out")
    dg0, dg1, dg2, du0, du1, du2, dya, dyr, dyx = _merge_bwd(ya, yr, yx, wa, wr, wx, proj, gate_b, dmerged)
    g["attn_w_o"] = _matmul(ya, du0, mode="tn", m=RW, n=D, k=s, tm=RW, tn=512, tk=min(512, s), name="grad_attn_w_o")
    g["rwkv_w_o"] = _matmul(yr, du1, mode="tn", m=RW, n=D, k=s, tm=RW, tn=512, tk=min(512, s), name="grad_rwkv_w_o")
    g["x_w_o"] = _matmul(yx, du2, mode="tn", m=XW, n=D, k=s, tm=XW, tn=512, tk=min(512, s), name="grad_x_w_o")
    dmg = jnp.concatenate([dg0, dg1, dg2], axis=1)
    g["gate_b"] = _colsum(dmg, "grad_gate_b")

    daq, dak, dav, dag, g["attn_q_norm_g"], g["attn_k_norm_g"], g["attn_sink"] = _attn_bwd(proj, cos, sin, gq, gk, sink, dya)

    dxq, dxg, dmkv, g["x_q_norm_g"], g["x_k_norm_g"] = _xattn_bwd(proj, mkv, gxq, gxk, dyx)
    g["x_w_kv"] = _matmul(mn, dmkv, mode="tn", m=D, n=2 * XW, k=NMEM, tm=512, tn=512, tk=NMEM, name="grad_x_w_kv")
    dmn = _matmul(dmkv, w_kv, mode="nt", m=NMEM, n=D, k=2 * XW, tm=NMEM, tn=512, tk=2 * XW, name="dmn")
    g["mem_norm_g"] = _mem_bwd(mem, mem_norm_g, dmn)

    dys, dr_p, dv_p, dkd0_p, dkd1_p, drg, g["rwkv_r_k"], g["rwkv_ln_w"], g["rwkv_ln_b"] = _post_bwd(
        y0, y1, ps, kd0, kd1, proj, r_k, ln_w, ln_b, dyr)
    dy8 = _to_t8(dys)
    dr0, dd0, db0, dk0, dkk0, dv80 = _scan_bwd(dec0, kd0, b0, ps, kk, v8, dy8, ck0, 0)
    dr1, dd1, db1, dk1, dkk1, dv81 = _scan_bwd(dec1, kd1, b1, ps, kk, v8, dy8, ck1, 1)
    dr = dr_p + dr0 + dr1
    dv = dv_p + _from_t8(dv80) + _from_t8(dv81)
    cts = (dkk0 + dkk1, dd0, dk0 + dkd0_p, db0, dd1, dk1 + dkd1_p, db1)
    dps, g["rwkv_k_k"], g["rwkv_k_a"], g["rwkv_w0"], g["rwkv_w2"], g["rwkv_a0"], g["rwkv_a2"] = _pre_bwd(
        ps, k_k, k_a, w0, w2, a0, a2, dr, dv, cts)
    drs, g["rwkv_mu"] = _shift_bwd(proj, mu, dps)

    dproj = jnp.concatenate([daq.astype(BF16), dak.astype(BF16), dav.astype(BF16), dag.astype(BF16), drs.astype(BF16),
                             drg.astype(BF16), dxq.astype(BF16), dxg.astype(BF16), dmg], axis=1)
    g["w_in"] = _matmul(h, dproj, mode="tn", m=D, n=NIN, k=s, tm=512, tn=896, tk=min(512, s), name="grad_w_in")
    grad_x, g["norm_g"] = _in_bwd(dproj, w_in, x, norm_g, dout)
    g["rwkv_r_k"] = g["rwkv_r_k"].reshape(AH, HD)
    return loss_sum, grad_x, g


WEIGHTS = ['norm_g', 'mem_norm_g', 'w_in', 'gate_b', 'attn_q_norm_g', 'attn_k_norm_g', 'attn_sink', 'attn_w_o',
           'rwkv_mu', 'rwkv_k_k', 'rwkv_k_a', 'rwkv_r_k', 'rwkv_w0', 'rwkv_w2', 'rwkv_a0', 'rwkv_a2', 'rwkv_ln_w',
           'rwkv_ln_b', 'rwkv_w_o', 'x_w_kv', 'x_q_norm_g', 'x_k_norm_g', 'x_w_o', 'w_out']
BIG = ['w_in', 'attn_w_o', 'rwkv_w_o', 'x_w_kv', 'x_w_o', 'w_out']
COL_SHARDED = ['w_in', 'attn_w_o', 'rwkv_w_o', 'x_w_o']
LORA = ['rwkv_w0', 'rwkv_w2', 'rwkv_a0', 'rwkv_a2']
SMALL = [n for n in WEIGHTS if n not in BIG]


def _unshard_cols(stack):
    return jnp.concatenate([stack[i] for i in range(4)], axis=-1)


def _shard_cols(full):
    w = full.shape[-1] // 4
    return [full[..., i * w:(i + 1) * w] for i in range(4)]


def kernel(x, mem, norm_g, mem_norm_g, w_in, gate_b, attn_q_norm_g, attn_k_norm_g, attn_sink, attn_w_o, rwkv_mu, rwkv_k_k, rwkv_k_a, rwkv_r_k, rwkv_w0, rwkv_w2, rwkv_a0, rwkv_a2, rwkv_ln_w, rwkv_ln_b, rwkv_w_o, x_w_kv, x_q_norm_g, x_k_norm_g, x_w_o, w_out, loss_target, m_norm_g, m_mem_norm_g, m_w_in, m_gate_b, m_attn_q_norm_g, m_attn_k_norm_g, m_attn_sink, m_attn_w_o, m_rwkv_mu, m_rwkv_k_k, m_rwkv_k_a, m_rwkv_r_k, m_rwkv_w0, m_rwkv_w2, m_rwkv_a0, m_rwkv_a2, m_rwkv_ln_w, m_rwkv_ln_b, m_rwkv_w_o, m_x_w_kv, m_x_q_norm_g, m_x_k_norm_g, m_x_w_o, m_w_out, v_norm_g, v_mem_norm_g, v_w_in, v_gate_b, v_attn_q_norm_g, v_attn_k_norm_g, v_attn_sink, v_attn_w_o, v_rwkv_mu, v_rwkv_k_k, v_rwkv_k_a, v_rwkv_r_k, v_rwkv_w0, v_rwkv_w2, v_rwkv_a0, v_rwkv_a2, v_rwkv_ln_w, v_rwkv_ln_b, v_rwkv_w_o, v_x_w_kv, v_x_q_norm_g, v_x_k_norm_g, v_x_w_o, v_w_out):
    args = dict(locals())
    canon = lambda a: a[0] if a.ndim > 2 else a
    w = {n: canon(args[n]) for n in WEIGHTS}
    m = {n: canon(args["m_" + n]) for n in WEIGHTS}
    v = {n: canon(args["v_" + n]) for n in WEIGHTS}
    shard = 2 * lax.axis_index("x") + lax.axis_index("y")

    local = [w[n].astype(BF16) for n in BIG] + [w[n] for n in LORA]
    stacks = dict(zip(BIG + LORA, _gather_shards(local, "gather_weights")))
    full = {}
    for n in COL_SHARDED + LORA:
        full[n] = _unshard_cols(stacks[n])
    full["x_w_kv"] = stacks["x_w_kv"].reshape(D, 2 * XW)
    full["w_out"] = stacks["w_out"].reshape(D, D)

    loss_sum, grad_x, g = _local_step(
        x[0], mem[0], loss_target[0], w["norm_g"], w["mem_norm_g"], full["w_in"], w["gate_b"], w["attn_q_norm_g"],
        w["attn_k_norm_g"], w["attn_sink"], full["attn_w_o"], w["rwkv_mu"], w["rwkv_k_k"], w["rwkv_k_a"], w["rwkv_r_k"],
        full["rwkv_w0"], full["rwkv_w2"], full["rwkv_a0"], full["rwkv_a2"], w["rwkv_ln_w"], w["rwkv_ln_b"],
        full["rwkv_w_o"], full["x_w_kv"], w["x_q_norm_g"], w["x_k_norm_g"], full["x_w_o"], full["w_out"])

    loss = lax.psum(0.5 * loss_sum / D, ("x", "y", "c"))

    def as_stack(n, dtype):
        if n in COL_SHARDED:
            return jnp.stack([p.astype(dtype) for p in _shard_cols(g[n])])
        return g[n].reshape((4, g[n].shape[0] // 4) + g[n].shape[1:]).astype(dtype)

    recv = _scatter_shards([as_stack(n, BF16) for n in BIG], "scatter_grads")
    partial = []
    for n, r in zip(BIG, recv):
        own = lax.dynamic_index_in_dim(as_stack(n, F32), shard, 0, keepdims=False)
        partial.append(_sum_parts([own, r[0], r[1], r[2]], "sum_" + n))
    theirs = _swap_sibling(partial, "swap_partials")

    out_g, out_d, out_m, out_v = {}, {}, {}, {}
    for n, mine, other in zip(BIG, partial, theirs):
        out_g[n], out_d[n], out_m[n], out_v[n] = _adamw([mine, other], w[n], m[n], v[n], "adamw_" + n)

    flat = jnp.concatenate([g[n].reshape(-1) for n in SMALL])
    total = flat.shape[0]
    padded = -(-total // 1024) * 1024
    flat = jnp.pad(flat, (0, padded - total)).reshape(padded // 128, 128)
    red = _all_reduce_small(flat).reshape(-1)
    off = 0
    gs = {}
    for n in SMALL:
        size = g[n].size
        t = red[off:off + size].reshape(g[n].shape)
        off += size
        if n in LORA:
            wd = t.shape[-1] // 4
            t = lax.dynamic_slice_in_dim(t, shard * wd, wd, axis=t.ndim - 1)
        gs[n] = t

    def pack(d):
        f = jnp.concatenate([d[n].reshape(-1) for n in SMALL])
        return jnp.pad(f, (0, -(-f.shape[0] // 1024) * 1024 - f.shape[0])).reshape(-1, 128)

    pg, pd, pm, pv = _adamw([pack(gs)], pack(w), pack(m), pack(v), "adamw_small")
    off = 0
    for n in SMALL:
        size = w[n].size
        for dst, src in ((out_g, pg), (out_d, pd), (out_m, pm), (out_v, pv)):
            dst[n] = src.reshape(-1)[off:off + size].reshape(w[n].shape)
        off += size

    lead = lambda d: [d[n][None] if args[n].ndim > 2 else d[n] for n in WEIGHTS]
    return (loss, grad_x[None], *lead(out_g), *lead(out_d), *lead(out_m), *lead(out_v))
```

```python
import functools

import jax
import jax.numpy as jnp
from jax import lax
from jax.experimental import pallas as pl
from jax.experimental.pallas import tpu as pltpu

F32 = jnp.float32
BF16 = jnp.bfloat16
HI = lax.Precision.HIGHEST
MESH = pl.DeviceIdType.MESH

D = 2048
NMEM = 256
NORM_EPS = 1e-6
NEG_INF = -1e30
GN_EPS = 64e-5
HD = 64
AH = 12
AKV = 4
RW = 768
XH = 4
XD = 128
XW = 512
NIN = 12544
RSW = 2560
C_AQ, C_AK, C_AV, C_AG, C_RS, C_RG, C_XQ, C_XG, C_MG = 0, 768, 1024, 1280, 2048, 4608, 5376, 5888, 6400
WIN = 384
QB = 128
TC = 16
NPAIR = 6

ADAM_LR, ADAM_B1, ADAM_B2, ADAM_EPS, ADAM_WD, ADAM_STEP = 0.001, 0.9, 0.999, 1e-08, 0.01, 10

VMEM_LIMIT = 56 * 1024 * 1024


def _bs(shape, imap):
    return pl.BlockSpec(shape, imap)


def _params(sem=None, vmem=VMEM_LIMIT):
    return pltpu.CompilerParams(dimension_semantics=sem, vmem_limit_bytes=vmem)


def _dot(a, b, dims):
    return lax.dot_general(a.astype(BF16), b.astype(BF16), (dims, ((), ())), preferred_element_type=F32)


@jax.custom_vjp
def _mm_nn(a, b):
    return _dot(a, b, ((1,), (0,)))


def _mm_nn_fwd(a, b):
    return _mm_nn(a, b), (a, b)


def _mm_nn_bwd(res, ct):
    a, b = res
    return _dot(ct, b, ((1,), (1,))), _dot(a, ct, ((0,), (0,)))


_mm_nn.defvjp(_mm_nn_fwd, _mm_nn_bwd)


@jax.custom_vjp
def _mm_nt(a, b):
    return _dot(a, b, ((1,), (1,)))


def _mm_nt_fwd(a, b):
    return _mm_nt(a, b), (a, b)


def _mm_nt_bwd(res, ct):
    a, b = res
    return _dot(ct, b, ((1,), (0,))), _dot(ct, a, ((0,), (0,)))


_mm_nt.defvjp(_mm_nt_fwd, _mm_nt_bwd)


def _seg_matrix(n, seg):
    r = lax.broadcasted_iota(jnp.int32, (n, n), 0) // seg
    c = lax.broadcasted_iota(jnp.int32, (n, n), 1) // seg
    return (r == c).astype(F32)


def _rot_matrix():
    r = lax.broadcasted_iota(jnp.int32, (HD, HD), 0)
    c = lax.broadcasted_iota(jnp.int32, (HD, HD), 1)
    return jnp.where(c == r + HD // 2, 1.0, 0.0).astype(F32) - jnp.where(c == r - HD // 2, 1.0, 0.0).astype(F32)


def _hdot(a, m):
    return jnp.dot(a, m, precision=HI, preferred_element_type=F32)


def _rms(t, g):
    return t * lax.rsqrt(jnp.mean(t * t, axis=-1, keepdims=True) + NORM_EPS) * g


def _silu(t):
    return t * jax.nn.sigmoid(t)


def _softplus(z):
    return jnp.maximum(z, 0.0) + jnp.log(1.0 + jnp.exp(-jnp.abs(z)))


def _matmul(a, b, *, mode, m, n, k, tm, tn, tk, name, a_off=(0, 0), b_off=(0, 0), out_dtype=F32):
    nk = k // tk
    if mode == "tn":
        a_spec = _bs((tk, tm), lambda i, j, kk: (kk + a_off[0], i + a_off[1]))
        dims = ((0,), (0,))
    else:
        a_spec = _bs((tm, tk), lambda i, j, kk: (i + a_off[0], kk + a_off[1]))
        dims = ((1,), (1,)) if mode == "nt" else ((1,), (0,))
    if mode == "nt":
        b_spec = _bs((tn, tk), lambda i, j, kk: (j + b_off[0], kk + b_off[1]))
    else:
        b_spec = _bs((tk, tn), lambda i, j, kk: (kk + b_off[0], j + b_off[1]))

    def body(a_ref, b_ref, o_ref, acc):
        kk = pl.program_id(2)

        @pl.when(kk == 0)
        def _():
            acc[...] = jnp.zeros_like(acc)

        acc[...] += _dot(a_ref[...], b_ref[...], dims)

        @pl.when(kk == nk - 1)
        def _():
            o_ref[...] = acc[...].astype(out_dtype)

    return pl.pallas_call(
        body, name=name, grid=(m // tm, n // tn, nk),
        in_specs=[a_spec, b_spec], out_specs=_bs((tm, tn), lambda i, j, kk: (i, j)),
        out_shape=jax.ShapeDtypeStruct((m, n), out_dtype),
        scratch_shapes=[pltpu.VMEM((tm, tn), F32)],
        compiler_params=_params(("parallel", "parallel", "arbitrary")),
    )(a, b)


def _proj_fwd(x, g, w):
    s = x.shape[0]
    tm, tn = min(512, s), 896

    def body(x_ref, g_ref, w_ref, o_ref, h_ref, hs):
        @pl.when(pl.program_id(1) == 0)
        def _():
            h = _rms(x_ref[...], g_ref[...]).astype(BF16)
            hs[...] = h
            h_ref[...] = h

        o_ref[...] = jnp.dot(hs[...], w_ref[...], preferred_element_type=F32)

    return pl.pallas_call(
        body, name="proj_fwd", grid=(s // tm, NIN // tn),
        in_specs=[_bs((tm, D), lambda i, j: (i, 0)), _bs((1, D), lambda i, j: (0, 0)), _bs((D, tn), lambda i, j: (0, j))],
        out_specs=[_bs((tm, tn), lambda i, j: (i, j)), _bs((tm, D), lambda i, j: (i, 0))],
        out_shape=[jax.ShapeDtypeStruct((s, NIN), F32), jax.ShapeDtypeStruct((s, D), BF16)],
        scratch_shapes=[pltpu.VMEM((tm, D), BF16)],
        compiler_params=_params(("parallel", "arbitrary")),
    )(x, g, w)


def _rope(t, cos, sin, rot):
    return t * cos + _hdot(t, rot) * sin


def _attn_tile(qs, ks, vs, gs, sinks, gq, gk, cq, sq, ck, sk, mask, rot):
    outs = []
    for hk in range(AKV):
        kh = _rope(_rms(ks[hk], gk), ck, sk, rot)
        for g in range(AH // AKV):
            h = hk * (AH // AKV) + g
            qh = _rope(_rms(qs[h], gq), cq, sq, rot)
            sc = _mm_nt(qh, kh) * (HD ** -0.5)
            sc = jnp.where(mask, sc, NEG_INF)
            mx = lax.stop_gradient(jnp.maximum(jnp.max(sc, axis=-1, keepdims=True), sinks[h]))
            p = jnp.exp(sc - mx)
            den = jnp.sum(p, axis=-1, keepdims=True) + jnp.exp(sinks[h] - mx)
            o = _mm_nn(p / den, vs[hk])
            outs.append(o * _silu(gs[h]))
    return outs


def _attn_load(n, s, aq_ref, ak_ref, av_ref, ag_refs, cos_ref, sin_ref, sink_ref):
    start = pl.multiple_of(jnp.clip((n - 1) * QB, 0, s - WIN), QB)
    q0 = pl.multiple_of(n * QB, QB)
    qs = [aq_ref[:, h * HD:(h + 1) * HD] for h in range(AH)]
    ks = [ak_ref[pl.ds(start, WIN), h * HD:(h + 1) * HD] for h in range(AKV)]
    vs = [av_ref[pl.ds(start, WIN), h * HD:(h + 1) * HD] for h in range(AKV)]
    gs = [ag_refs[h // 4][:, (h % 4) * HD:(h % 4 + 1) * HD] for h in range(AH)]
    sinks = [sink_ref[0:1, h:h + 1] for h in range(AH)]
    cq, sq = cos_ref[pl.ds(q0, QB), :], sin_ref[pl.ds(q0, QB), :]
    ck, sk = cos_ref[pl.ds(start, WIN), :], sin_ref[pl.ds(start, WIN), :]
    qpos = q0 + lax.broadcasted_iota(jnp.int32, (QB, WIN), 0)
    kpos = start + lax.broadcasted_iota(jnp.int32, (QB, WIN), 1)
    mask = jnp.abs(kpos - qpos) <= QB
    return start, qs, ks, vs, gs, sinks, cq, sq, ck, sk, mask


def _attn_specs(s):
    return [
        _bs((QB, 768), lambda n: (n, 0)),
        _bs((s, 256), lambda n: (0, C_AK // 256)),
        _bs((s, 256), lambda n: (0, C_AV // 256)),
        _bs((QB, 256), lambda n: (n, C_AG // 256)),
        _bs((QB, 256), lambda n: (n, C_AG // 256 + 1)),
        _bs((QB, 256), lambda n: (n, C_AG // 256 + 2)),
        _bs((s, HD), lambda n: (0, 0)),
        _bs((s, HD), lambda n: (0, 0)),
        _bs((1, HD), lambda n: (0, 0)),
        _bs((1, HD), lambda n: (0, 0)),
        _bs((1, AH), lambda n: (0, 0)),
    ]


def _attn_fwd(proj, cos, sin, gq, gk, sink):
    s = proj.shape[0]

    def body(aq_ref, ak_ref, av_ref, ag0, ag1, ag2, cos_ref, sin_ref, gq_ref, gk_ref, sink_ref, o_ref):
        n = pl.program_id(0)
        _, qs, ks, vs, gs, sinks, cq, sq, ck, sk, mask = _attn_load(
            n, s, aq_ref, ak_ref, av_ref, (ag0, ag1, ag2), cos_ref, sin_ref, sink_ref)
        outs = _attn_tile(qs, ks, vs, gs, sinks, gq_ref[...], gk_ref[...], cq, sq, ck, sk, mask, _rot_matrix())
        for h in range(AH):
            o_ref[:, h * HD:(h + 1) * HD] = outs[h]

    return pl.pallas_call(
        body, name="attn_fwd", grid=(s // QB,),
        in_specs=_attn_specs(s), out_specs=_bs((QB, 768), lambda n: (n, 0)),
        out_shape=jax.ShapeDtypeStruct((s, 768), F32),
        compiler_params=_params(("arbitrary",)),
    )(proj, proj, proj, proj, proj, proj, cos, sin, gq, gk, sink)


def _attn_bwd(proj, cos, sin, gq, gk, sink, dy):
    s = proj.shape[0]

    def body(aq_ref, ak_ref, av_ref, ag0, ag1, ag2, cos_ref, sin_ref, gq_ref, gk_ref, sink_ref, dy_ref,
             daq_ref, dak_ref, dav_ref, dag_ref, dgq_ref, dgk_ref, dsink_ref):
        n = pl.program_id(0)

        @pl.when(n == 0)
        def _():
            dak_ref[...] = jnp.zeros_like(dak_ref)
            dav_ref[...] = jnp.zeros_like(dav_ref)
            dgq_ref[...] = jnp.zeros_like(dgq_ref)
            dgk_ref[...] = jnp.zeros_like(dgk_ref)
            dsink_ref[...] = jnp.zeros_like(dsink_ref)

        start, qs, ks, vs, gs, sinks, cq, sq, ck, sk, mask = _attn_load(
            n, s, aq_ref, ak_ref, av_ref, (ag0, ag1, ag2), cos_ref, sin_ref, sink_ref)
        rot = _rot_matrix()

        def f(qs, ks, vs, gs, sinks, gq, gk):
            return _attn_tile(qs, ks, vs, gs, sinks, gq, gk, cq, sq, ck, sk, mask, rot)

        _, vjp = jax.vjp(f, qs, ks, vs, gs, sinks, gq_ref[...], gk_ref[...])
        dys = [dy_ref[:, h * HD:(h + 1) * HD] for h in range(AH)]
        dqs, dks, dvs, dgs, dsinks, dgq, dgk = vjp(dys)
        for h in range(AH):
            daq_ref[:, h * HD:(h + 1) * HD] = dqs[h]
            dag_ref[:, h * HD:(h + 1) * HD] = dgs[h]
            dsink_ref[0:1, h:h + 1] += dsinks[h]
        for h in range(AKV):
            dak_ref[pl.ds(start, WIN), h * HD:(h + 1) * HD] += dks[h]
            dav_ref[pl.ds(start, WIN), h * HD:(h + 1) * HD] += dvs[h]
        dgq_ref[...] += dgq
        dgk_ref[...] += dgk

    whole = lambda shape: _bs(shape, lambda n: (0, 0))
    return pl.pallas_call(
        body, name="attn_bwd", grid=(s // QB,),
        in_specs=_attn_specs(s) + [_bs((QB, 768), lambda n: (n, 0))],
        out_specs=[_bs((QB, 768), lambda n: (n, 0)), whole((s, 256)), whole((s, 256)), _bs((QB, 768), lambda n: (n, 0)),
                   whole((1, HD)), whole((1, HD)), whole((1, AH))],
        out_shape=[jax.ShapeDtypeStruct((s, 768), F32), jax.ShapeDtypeStruct((s, 256), F32),
                   jax.ShapeDtypeStruct((s, 256), F32), jax.ShapeDtypeStruct((s, 768), F32),
                   jax.ShapeDtypeStruct((1, HD), F32), jax.ShapeDtypeStruct((1, HD), F32),
                   jax.ShapeDtypeStruct((1, AH), F32)],
        compiler_params=_params(("arbitrary",)),
    )(proj, proj, proj, proj, proj, proj, cos, sin, gq, gk, sink, dy)


def _mem_kv(mem, g, w):
    def body(m_ref, g_ref, w_ref, o_ref, mn_ref):
        mn = _rms(m_ref[...], g_ref[...]).astype(BF16)
        mn_ref[...] = mn
        o_ref[...] = jnp.dot(mn, w_ref[...], preferred_element_type=F32)

    return pl.pallas_call(
        body, name="mem_kv",
        out_shape=[jax.ShapeDtypeStruct((NMEM, 2 * XW), F32), jax.ShapeDtypeStruct((NMEM, D), BF16)],
        compiler_params=_params(),
    )(mem, g, w)


def _xattn_tile(qs, gs, kms, vms, gxq, gxk):
    outs = []
    for h in range(XH):
        q = _rms(qs[h], gxq)
        km = _rms(kms[h], gxk)
        sc = _mm_nt(q, km) * (XD ** -0.5)
        mx = lax.stop_gradient(jnp.max(sc, axis=-1, keepdims=True))
        p = jnp.exp(sc - mx)
        p = p / jnp.sum(p, axis=-1, keepdims=True)
        outs.append(_mm_nn(p, vms[h]) * _silu(gs[h]))
    return outs


XT = 256


def _xattn_specs():
    return [
        _bs((XT, 256), lambda i: (i, C_XQ // 256)), _bs((XT, 256), lambda i: (i, C_XQ // 256 + 1)),
        _bs((XT, 256), lambda i: (i, C_XG // 256)), _bs((XT, 256), lambda i: (i, C_XG // 256 + 1)),
        _bs((NMEM, 2 * XW), lambda i: (0, 0)),
        _bs((1, XD), lambda i: (0, 0)), _bs((1, XD), lambda i: (0, 0)),
    ]


def _xattn_load(q0, q1, g0, g1, mkv_ref):
    qs = [(q0, q1)[h // 2][:, (h % 2) * XD:(h % 2 + 1) * XD] for h in range(XH)]
    gs = [(g0, g1)[h // 2][:, (h % 2) * XD:(h % 2 + 1) * XD] for h in range(XH)]
    kms = [mkv_ref[:, h * XD:(h + 1) * XD] for h in range(XH)]
    vms = [mkv_ref[:, XW + h * XD:XW + (h + 1) * XD] for h in range(XH)]
    return qs, gs, kms, vms


def _xattn_fwd(proj, mkv, gxq, gxk):
    s = proj.shape[0]

    def body(q0, q1, g0, g1, mkv_ref, gxq_ref, gxk_ref, o_ref):
        qs, gs, kms, vms = _xattn_load(q0, q1, g0, g1, mkv_ref)
        outs = _xattn_tile(qs, gs, kms, vms, gxq_ref[...], gxk_ref[...])
        for h in range(XH):
            o_ref[:, h * XD:(h + 1) * XD] = outs[h]

    return pl.pallas_call(
        body, name="xattn_fwd", grid=(s // XT,),
        in_specs=_xattn_specs(), out_specs=_bs((XT, XW), lambda i: (i, 0)),
        out_shape=jax.ShapeDtypeStruct((s, XW), F32),
        compiler_params=_params(("arbitrary",)),
    )(proj, proj, proj, proj, mkv, gxq, gxk)


def _xattn_bwd(proj, mkv, gxq, gxk, dy):
    s = proj.shape[0]

    def body(q0, q1, g0, g1, mkv_ref, gxq_ref, gxk_ref, dy_ref, dq_ref, dg_ref, dmkv_ref, dgxq_ref, dgxk_ref):
        @pl.when(pl.program_id(0) == 0)
        def _():
            dmkv_ref[...] = jnp.zeros_like(dmkv_ref)
            dgxq_ref[...] = jnp.zeros_like(dgxq_ref)
            dgxk_ref[...] = jnp.zeros_like(dgxk_ref)

        qs, gs, kms, vms = _xattn_load(q0, q1, g0, g1, mkv_ref)
        _, vjp = jax.vjp(_xattn_tile, qs, gs, kms, vms, gxq_ref[...], gxk_ref[...])
        dqs, dgs, dkms, dvms, dgxq, dgxk = vjp([dy_ref[:, h * XD:(h + 1) * XD] for h in range(XH)])
        for h in range(XH):
            dq_ref[:, h * XD:(h + 1) * XD] = dqs[h]
            dg_ref[:, h * XD:(h + 1) * XD] = dgs[h]
            dmkv_ref[:, h * XD:(h + 1) * XD] += dkms[h]
            dmkv_ref[:, XW + h * XD:XW + (h + 1) * XD] += dvms[h]
        dgxq_ref[...] += dgxq
        dgxk_ref[...] += dgxk

    whole = lambda shape: _bs(shape, lambda i: (0, 0))
    return pl.pallas_call(
        body, name="xattn_bwd", grid=(s // XT,),
        in_specs=_xattn_specs() + [_bs((XT, XW), lambda i: (i, 0))],
        out_specs=[_bs((XT, XW), lambda i: (i, 0)), _bs((XT, XW), lambda i: (i, 0)), whole((NMEM, 2 * XW)),
                   whole((1, XD)), whole((1, XD))],
        out_shape=[jax.ShapeDtypeStruct((s, XW), F32), jax.ShapeDtypeStruct((s, XW), F32),
                   jax.ShapeDtypeStruct((NMEM, 2 * XW), F32), jax.ShapeDtypeStruct((1, XD), F32),
                   jax.ShapeDtypeStruct((1, XD), F32)],
        compiler_params=_params(("arbitrary",)),
    )(proj, proj, proj, proj, mkv, gxq, gxk, dy)


def _mem_bwd(mem, g, dmn):
    def body(m_ref, dmn_ref, o_ref):
        m = m_ref[...]
        r = lax.rsqrt(jnp.mean(m * m, axis=-1, keepdims=True) + NORM_EPS)
        o_ref[...] = jnp.sum(dmn_ref[...] * m * r, axis=0, keepdims=True)

    del g
    return pl.pallas_call(body, name="mem_norm_bwd", out_shape=jax.ShapeDtypeStruct((1, D), F32),
                          compiler_params=_params())(mem, dmn)


SHIFT_W = 512


def _shift_rows(p, s):
    row = lax.broadcasted_iota(jnp.int32, p.shape, 0)
    prev = jnp.where(row == 0, 0.0, pltpu.roll(p, 1, 0))
    nxt = jnp.where(row == s - 1, 0.0, pltpu.roll(p, s - 1, 0))
    return prev, nxt


def _shift_fwd(proj, mu):
    s = proj.shape[0]

    def body(p_ref, mu_ref, o_ref):
        p = p_ref[...]
        prev, nxt = _shift_rows(p, s)
        o_ref[...] = p + mu_ref[...] * (0.5 * (prev + nxt) - p)

    return pl.pallas_call(
        body, name="shift_fwd", grid=(RSW // SHIFT_W,),
        in_specs=[_bs((s, SHIFT_W), lambda j: (0, C_RS // SHIFT_W + j)), _bs((1, SHIFT_W), lambda j: (0, j))],
        out_specs=_bs((s, SHIFT_W), lambda j: (0, j)),
        out_shape=jax.ShapeDtypeStruct((s, RSW), F32),
        compiler_params=_params(("parallel",)),
    )(proj, mu)


def _shift_bwd(proj, mu, dps):
    s = proj.shape[0]

    def body(p_ref, mu_ref, g_ref, o_ref, dmu_ref):
        p, g, mu_v = p_ref[...], g_ref[...], mu_ref[...]
        prev, nxt = _shift_rows(p, s)
        dmu_ref[...] = jnp.sum(g * (0.5 * (prev + nxt) - p), axis=0, keepdims=True)
        mg = mu_v * g
        down, up = _shift_rows(mg, s)
        o_ref[...] = g * (1.0 - mu_v) + 0.5 * (down + up)

    return pl.pallas_call(
        body, name="shift_bwd", grid=(RSW // SHIFT_W,),
        in_specs=[_bs((s, SHIFT_W), lambda j: (0, C_RS // SHIFT_W + j)), _bs((1, SHIFT_W), lambda j: (0, j)),
                  _bs((s, SHIFT_W), lambda j: (0, j))],
        out_specs=[_bs((s, SHIFT_W), lambda j: (0, j)), _bs((1, SHIFT_W), lambda j: (0, j))],
        out_shape=[jax.ShapeDtypeStruct((s, RSW), F32), jax.ShapeDtypeStruct((1, RSW), F32)],
        compiler_params=_params(("parallel",)),
    )(proj, mu, dps)


def _pre_tile(k, wf, wb, af, ab, k_k, k_a, w0s, w2s, a0s, a2s, seg):
    kx = k * k_k
    ss = _hdot(kx * kx, seg)
    kk = kx / jnp.maximum(jnp.sqrt(ss), 1e-12)
    outs = [kk]
    for d, (w_in, a_in) in enumerate(((wf, af), (wb, ab))):
        z = w0s[d] + _mm_nn(jnp.tanh(w_in), w2s[d])
        wd = -_softplus(-z) - 0.5
        dec = jnp.exp(-jnp.exp(wd))
        ad = jax.nn.sigmoid(a0s[d] + _mm_nn(a_in, a2s[d]))
        kd = k * (1.0 + (ad - 1.0) * k_a)
        outs += [dec, kd, kk * ad]
    return outs


PT = 256


def _pre_load(ps_ref, kk_ref, ka_ref, w0_ref, w2_ref, a0_ref, a2_ref):
    k = ps_ref[:, RW:2 * RW]
    wf, wb = ps_ref[:, 3 * RW:3 * RW + 64], ps_ref[:, 3 * RW + 64:3 * RW + 128]
    af, ab = ps_ref[:, 3 * RW + 128:3 * RW + 192], ps_ref[:, 3 * RW + 192:3 * RW + 256]
    w0s = [w0_ref[0:1, :], w0_ref[1:2, :]]
    a0s = [a0_ref[0:1, :], a0_ref[1:2, :]]
    w2s = [w2_ref[0], w2_ref[1]]
    a2s = [a2_ref[0], a2_ref[1]]
    return (k, wf, wb, af, ab, kk_ref[...], ka_ref[...], w0s, w2s, a0s, a2s)


def _pre_specs():
    c = lambda shape: _bs(shape, lambda i: tuple(0 for _ in shape))
    return [_bs((PT, RSW), lambda i: (i, 0)), c((1, RW)), c((1, RW)), c((2, RW)), c((2, 64, RW)), c((2, RW)),
            c((2, 64, RW))]


def _pre_fwd(ps, k_k, k_a, w0, w2, a0, a2):
    s = ps.shape[0]

    def body(ps_ref, kk_ref, ka_ref, w0_ref, w2_ref, a0_ref, a2_ref, *outs):
        args = _pre_load(ps_ref, kk_ref, ka_ref, w0_ref, w2_ref, a0_ref, a2_ref)
        res = _pre_tile(*args, _seg_matrix(RW, HD))
        for o_ref, v in zip(outs, res):
            o_ref[...] = v

    return pl.pallas_call(
        body, name="rwkv_pre_fwd", grid=(s // PT,),
        in_specs=_pre_specs(), out_specs=[_bs((PT, RW), lambda i: (i, 0))] * 7,
        out_shape=[jax.ShapeDtypeStruct((s, RW), F32)] * 7,
        compiler_params=_params(("parallel",)),
    )(ps, k_k, k_a, w0, w2, a0, a2)


def _pre_bwd(ps, k_k, k_a, w0, w2, a0, a2, dr, dv, cts):
    s = ps.shape[0]

    def body(ps_ref, kk_ref, ka_ref, w0_ref, w2_ref, a0_ref, a2_ref, dr_ref, dv_ref, c0, c1, c2, c3, c4, c5, c6,
             dps_ref, dkk_ref, dka_ref, dw0_ref, dw2_ref, da0_ref, da2_ref):
        @pl.when(pl.program_id(0) == 0)
        def _():
            for r in (dkk_ref, dka_ref, dw0_ref, dw2_ref, da0_ref, da2_ref):
                r[...] = jnp.zeros_like(r)

        args = _pre_load(ps_ref, kk_ref, ka_ref, w0_ref, w2_ref, a0_ref, a2_ref)
        seg = _seg_matrix(RW, HD)
        _, vjp = jax.vjp(lambda *a: _pre_tile(*a, seg), *args)
        dk, dwf, dwb, daf, dab, dk_k, dk_a, dw0s, dw2s, da0s, da2s = vjp([c[...] for c in (c0, c1, c2, c3, c4, c5, c6)])
        dps_ref[:, 0:RW] = dr_ref[...]
        dps_ref[:, RW:2 * RW] = dk
        dps_ref[:, 2 * RW:3 * RW] = dv_ref[...]
        for j, t in enumerate((dwf, dwb, daf, dab)):
            dps_ref[:, 3 * RW + 64 * j:3 * RW + 64 * (j + 1)] = t
        dkk_ref[...] += dk_k
        dka_ref[...] += dk_a
        for d in range(2):
            dw0_ref[d:d + 1, :] += dw0s[d]
            da0_ref[d:d + 1, :] += da0s[d]
            dw2_ref[d] += dw2s[d]
            da2_ref[d] += da2s[d]

    c = lambda shape: _bs(shape, lambda i: tuple(0 for _ in shape))
    row = _bs((PT, RW), lambda i: (i, 0))
    return pl.pallas_call(
        body, name="rwkv_pre_bwd", grid=(s // PT,),
        in_specs=_pre_specs() + [row] * 9,
        out_specs=[_bs((PT, RSW), lambda i: (i, 0)), c((1, RW)), c((1, RW)), c((2, RW)), c((2, 64, RW)), c((2, RW)),
                   c((2, 64, RW))],
        out_shape=[jax.ShapeDtypeStruct((s, RSW), F32), jax.ShapeDtypeStruct((1, RW), F32),
                   jax.ShapeDtypeStruct((1, RW), F32), jax.ShapeDtypeStruct((2, RW), F32),
                   jax.ShapeDtypeStruct((2, 64, RW), F32), jax.ShapeDtypeStruct((2, RW), F32),
                   jax.ShapeDtypeStruct((2, 64, RW), F32)],
        compiler_params=_params(("arbitrary",)),
    )(ps, k_k, k_a, w0, w2, a0, a2, dr, dv, *cts)


def _post_tile(y0, y1, r, v, kd0, kd1, rg, r_k, ln_w, ln_b, seg):
    ysum = y0 + y1
    bonus = (_hdot(r * kd0 * r_k, seg) + _hdot(r * kd1 * r_k, seg)) * v
    mean = _hdot(ysum, seg) * (1.0 / HD)
    cen = ysum - mean
    var = _hdot(cen * cen, seg) * (1.0 / HD)
    y = cen * lax.rsqrt(var + GN_EPS) * ln_w + ln_b + bonus
    return y * _silu(rg)


def _post_specs():
    row = _bs((PT, RW), lambda i: (i, 0))
    c = _bs((1, RW), lambda i: (0, 0))
    return [row, row, _bs((PT, RW), lambda i: (i, 0)), _bs((PT, RW), lambda i: (i, 2)), row, row,
            _bs((PT, RW), lambda i: (i, C_RG // RW)), c, c, c]


def _post_fwd(y0, y1, ps, kd0, kd1, proj, r_k, ln_w, ln_b):
    s = ps.shape[0]

    def body(y0_ref, y1_ref, r_ref, v_ref, kd0_ref, kd1_ref, rg_ref, rk_ref, lw_ref, lb_ref, o_ref):
        o_ref[...] = _post_tile(y0_ref[...], y1_ref[...], r_ref[...], v_ref[...], kd0_ref[...], kd1_ref[...],
                                rg_ref[...], rk_ref[...], lw_ref[...], lb_ref[...], _seg_matrix(RW, HD))

    return pl.pallas_call(
        body, name="rwkv_post_fwd", grid=(s // PT,),
        in_specs=_post_specs(), out_specs=_bs((PT, RW), lambda i: (i, 0)),
        out_shape=jax.ShapeDtypeStruct((s, RW), F32),
        compiler_params=_params(("parallel",)),
    )(y0, y1, ps, ps, kd0, kd1, proj, r_k, ln_w, ln_b)


def _post_bwd(y0, y1, ps, kd0, kd1, proj, r_k, ln_w, ln_b, dy):
    s = ps.shape[0]

    def body(y0_ref, y1_ref, r_ref, v_ref, kd0_ref, kd1_ref, rg_ref, rk_ref, lw_ref, lb_ref, dy_ref,
             dys_ref, dr_ref, dv_ref, dkd0_ref, dkd1_ref, drg_ref, drk_ref, dlw_ref, dlb_ref):
        @pl.when(pl.program_id(0) == 0)
        def _():
            for r in (drk_ref, dlw_ref, dlb_ref):
                r[...] = jnp.zeros_like(r)

        seg = _seg_matrix(RW, HD)
        args = [t[...] for t in (y0_ref, y1_ref, r_ref, v_ref, kd0_ref, kd1_ref, rg_ref, rk_ref, lw_ref, lb_ref)]
        _, vjp = jax.vjp(lambda *a: _post_tile(*a, seg), *args)
        dy0, _, dr, dv, dkd0, dkd1, drg, drk, dlw, dlb = vjp(dy_ref[...])
        dys_ref[...] = dy0
        dr_ref[...] = dr
        dv_ref[...] = dv
        dkd0_ref[...] = dkd0
        dkd1_ref[...] = dkd1
        drg_ref[...] = drg
        drk_ref[...] += drk
        dlw_ref[...] += dlw
        dlb_ref[...] += dlb

    row = _bs((PT, RW), lambda i: (i, 0))
    c = _bs((1, RW), lambda i: (0, 0))
    return pl.pallas_call(
        body, name="rwkv_post_bwd", grid=(s // PT,),
        in_specs=_post_specs() + [row], out_specs=[row] * 6 + [c] * 3,
        out_shape=[jax.ShapeDtypeStruct((s, RW), F32)] * 6 + [jax.ShapeDtypeStruct((1, RW), F32)] * 3,
        compiler_params=_params(("arbitrary",)),
    )(y0, y1, ps, ps, kd0, kd1, proj, r_k, ln_w, ln_b, dy)


def _ones2():
    r = lax.broadcasted_iota(jnp.int32, (256, 128), 0) % 128 // HD
    c = lax.broadcasted_iota(jnp.int32, (256, 128), 1) // HD
    return (r == c).astype(BF16)


def _split(p):
    hi = p.astype(BF16)
    lo = (p - hi.astype(F32)).astype(BF16)
    return jnp.concatenate([hi, lo], axis=1)


def _to_t8(a):
    s = a.shape[0]
    t = a.reshape(s // 8, 8, NPAIR, 2, HD).transpose(0, 2, 4, 3, 1)
    t = jnp.pad(t, ((0, 0), (0, 0), (0, 0), (0, 0), (0, HD - 8))).reshape(s // 8, NPAIR, HD, 128)
    hi = t.astype(BF16)
    lo = (t - hi.astype(F32)).astype(BF16)
    return jnp.concatenate([hi, lo], axis=-1)


def _from_t8(t8):
    g = t8.shape[0]
    t = t8.reshape(g, NPAIR, HD, 2, HD)[..., :8]
    return t.transpose(0, 4, 1, 3, 2).reshape(g * 8, RW)


def _scan_specs(direction, nc, fwd_order):
    def tb(c):
        sc = c if fwd_order else nc - 1 - c
        return sc if direction == 0 else nc - 1 - sc

    row = _bs((TC, RW), lambda c: (tb(c), 0))
    t8_in = _bs((TC // 8, NPAIR, HD, 256), lambda c: (tb(c), 0, 0, 0))
    t8_out = _bs((TC // 8, NPAIR, HD, 128), lambda c: (tb(c), 0, 0, 0))
    return row, t8_in, t8_out


def _put_t8(ref, g, u, tiles):
    for p in range(NPAIR):
        ref[g, p, :, u:u + 1] = tiles[p][:, u:u + 1]
        ref[g, p, :, HD + u:HD + u + 1] = tiles[p][:, HD + u:HD + u + 1]


def _scan_fwd(dec, kd, b, ps, kk, vl, direction):
    s = dec.shape[0]
    nc, ng = s // TC, TC // 8
    row, t8_in, t8_out = _scan_specs(direction, nc, True)
    n = NPAIR * HD

    def body(dec_ref, kd_ref, b_ref, r_ref, kk_ref, vl_ref, y8_ref, ck_ref, st):
        @pl.when(pl.program_id(0) == 0)
        def _():
            st[...] = jnp.zeros_like(st)

        ck_ref[0] = st[...]
        ones2 = _ones2()
        lane_u = lax.broadcasted_iota(jnp.int32, (HD, 256), 1) % HD
        tiles = lambda res, k: [res[k * n + p * HD:k * n + (p + 1) * HD] for p in range(NPAIR)]

        def group(gi, carry):
            g = gi if direction == 0 else ng - 1 - gi
            rows8 = pl.ds(pl.multiple_of(g * 8, 8), 8)
            d8, k8, b8, r8, kk8 = (q[rows8, :] for q in (dec_ref, kd_ref, b_ref, r_ref, kk_ref))
            pc = [slice(p * 128, (p + 1) * 128) for p in range(NPAIR)]
            ss = [st[p] for p in range(NPAIR)]
            u_prev = None
            for ui in range(8):
                u = ui if direction == 0 else 7 - ui
                lhs = [_split(ss[p] * kk8[u:u + 1, pc[p]]) for p in range(NPAIR)]
                for p in range(NPAIR):
                    vt = vl_ref[g, p]
                    lhs.append(jnp.where(lane_u == u, vt, jnp.zeros_like(vt)))
                if u_prev is not None:
                    lhs += [_split(ss[p] * r8[u_prev:u_prev + 1, pc[p]]) for p in range(NPAIR)]
                res = jnp.dot(jnp.concatenate(lhs, axis=0), ones2, preferred_element_type=F32)
                if u_prev is not None:
                    _put_t8(y8_ref, g, u_prev, tiles(res, 2))
                sa, vb = tiles(res, 0), tiles(res, 1)
                for p in range(NPAIR):
                    ss[p] = ss[p] * d8[u:u + 1, pc[p]] - sa[p] * b8[u:u + 1, pc[p]] + vb[p] * k8[u:u + 1, pc[p]]
                u_prev = u
            lhs = [_split(ss[p] * r8[u_prev:u_prev + 1, pc[p]]) for p in range(NPAIR)]
            res = jnp.dot(jnp.concatenate(lhs, axis=0), ones2, preferred_element_type=F32)
            _put_t8(y8_ref, g, u_prev, tiles(res, 0))
            for p in range(NPAIR):
                st[p] = ss[p]
            return carry

        lax.fori_loop(0, ng, group, 0)

    return pl.pallas_call(
        body, name=f"rwkv_scan_fwd{direction}", grid=(nc,),
        in_specs=[row, row, row, row, row, t8_in],
        out_specs=[t8_out, _bs((1, NPAIR, HD, 128), lambda c: (c, 0, 0, 0))],
        out_shape=[jax.ShapeDtypeStruct((s // 8, NPAIR, HD, 128), F32),
                   jax.ShapeDtypeStruct((nc, NPAIR, HD, 128), F32)],
        scratch_shapes=[pltpu.VMEM((NPAIR, HD, 128), F32)],
        compiler_params=_params(("arbitrary",)),
    )(dec, kd, b, ps, kk, vl)


def _scan_bwd(dec, kd, b, ps, kk, vl, dyl, ck, direction):
    s = dec.shape[0]
    nc, ng = s // TC, TC // 8
    row, t8_in, t8_out = _scan_specs(direction, nc, False)
    n = NPAIR * HD

    def body(dec_ref, kd_ref, b_ref, r_ref, kk_ref, vl_ref, dyl_ref, ck_ref,
             dr_ref, dd_ref, db_ref, dk_ref, dkk_ref, dv8_ref, st, sa_s, vb_s, dy_s, ds):
        @pl.when(pl.program_id(0) == 0)
        def _():
            ds[...] = jnp.zeros_like(ds)

        st[0] = ck_ref[0]
        ones2 = _ones2()
        lane_u = lax.broadcasted_iota(jnp.int32, (HD, 256), 1) % HD
        row_id = lax.broadcasted_iota(jnp.int32, (8, 128), 0)
        pc = [slice(p * 128, (p + 1) * 128) for p in range(NPAIR)]
        tiles = lambda res, k: [res[k * n + p * HD:k * n + (p + 1) * HD] for p in range(NPAIR)]

        def fgroup(gi, carry):
            g = gi if direction == 0 else ng - 1 - gi
            rows8 = pl.ds(pl.multiple_of(g * 8, 8), 8)
            d8, k8, b8, kk8 = (q[rows8, :] for q in (dec_ref, kd_ref, b_ref, kk_ref))
            ss = [st[gi * 8, p] for p in range(NPAIR)]
            for ui in range(8):
                u = ui if direction == 0 else 7 - ui
                i = gi * 8 + ui
                lhs = [_split(ss[p] * kk8[u:u + 1, pc[p]]) for p in range(NPAIR)]
                for ref in (vl_ref, dyl_ref):
                    for p in range(NPAIR):
                        t = ref[g, p]
                        lhs.append(jnp.where(lane_u == u, t, jnp.zeros_like(t)))
                res = jnp.dot(jnp.concatenate(lhs, axis=0), ones2, preferred_element_type=F32)
                sa, vb, dyb = tiles(res, 0), tiles(res, 1), tiles(res, 2)
                for p in range(NPAIR):
                    sa_s[i, p] = sa[p]
                    vb_s[i, p] = vb[p]
                    dy_s[i, p] = dyb[p]
                    ss[p] = ss[p] * d8[u:u + 1, pc[p]] - sa[p] * b8[u:u + 1, pc[p]] + vb[p] * k8[u:u + 1, pc[p]]
                    st[i + 1, p] = ss[p]
            return carry

        lax.fori_loop(0, ng, fgroup, 0)

        def bgroup(gj, carry):
            gi = ng - 1 - gj
            g = gi if direction == 0 else ng - 1 - gi
            rows8 = pl.ds(pl.multiple_of(g * 8, 8), 8)
            d8, k8, b8, r8, kk8 = (q[rows8, :] for q in (dec_ref, kd_ref, b_ref, r_ref, kk_ref))
            dss = [ds[p] for p in range(NPAIR)]
            acc = [[jnp.zeros((8, 128), F32) for _ in range(5)] for _ in range(NPAIR)]
            for uj in range(8):
                ui = 7 - uj
                u = ui if direction == 0 else 7 - ui
                i = gi * 8 + ui
                dyb = [dy_s[i, p] for p in range(NPAIR)]
                for p in range(NPAIR):
                    dss[p] = dss[p] + dyb[p] * r8[u:u + 1, pc[p]]
                lhs = [_split(dss[p] * b8[u:u + 1, pc[p]]) for p in range(NPAIR)]
                lhs += [_split(dss[p] * k8[u:u + 1, pc[p]]) for p in range(NPAIR)]
                res = jnp.dot(jnp.concatenate(lhs, axis=0), ones2, preferred_element_type=F32)
                dsa, dvb = tiles(res, 0), tiles(res, 1)
                _put_t8(dv8_ref, g, u, dvb)
                for p in range(NPAIR):
                    sp, sn = st[i, p], st[i + 1, p]
                    outs = (jnp.sum(sn * dyb[p], axis=0, keepdims=True), jnp.sum(dss[p] * sp, axis=0, keepdims=True),
                            -jnp.sum(dss[p] * sa_s[i, p], axis=0, keepdims=True),
                            jnp.sum(dss[p] * vb_s[i, p], axis=0, keepdims=True),
                            -jnp.sum(sp * dsa[p], axis=0, keepdims=True))
                    acc[p] = [jnp.where(row_id == u, o, a_) for o, a_ in zip(outs, acc[p])]
                    dss[p] = dss[p] * d8[u:u + 1, pc[p]] - dsa[p] * kk8[u:u + 1, pc[p]]
            for p in range(NPAIR):
                ds[p] = dss[p]
                for o_ref, a_ in zip((dr_ref, dd_ref, db_ref, dk_ref, dkk_ref), acc[p]):
                    o_ref[rows8, pc[p]] = a_
            return carry

        lax.fori_loop(0, ng, bgroup, 0)

    chunk = lambda k: pltpu.VMEM((k, NPAIR, HD, 128), F32)
    return pl.pallas_call(
        body, name=f"rwkv_scan_bwd{direction}", grid=(nc,),
        in_specs=[row, row, row, row, row, t8_in, t8_in, _bs((1, NPAIR, HD, 128), lambda c: (nc - 1 - c, 0, 0, 0))],
        out_specs=[row] * 5 + [t8_out],
        out_shape=[jax.ShapeDtypeStruct((s, RW), F32)] * 5 + [jax.ShapeDtypeStruct((s // 8, NPAIR, HD, 128), F32)],
        scratch_shapes=[chunk(TC + 1), chunk(TC), chunk(TC), chunk(TC), pltpu.VMEM((NPAIR, HD, 128), F32)],
        compiler_params=_params(("arbitrary",)),
    )(dec, kd, b, ps, kk, vl, dyl, ck)


def _tiles(res, k):
    n = NPAIR * HD
    return [res[k * n + p * HD:k * n + (p + 1) * HD] for p in range(NPAIR)]


def _scan2_fwd(per_dir, ps, kk, vl):
    s = ps.shape[0]
    nc, ng = s // TC, TC // 8
    in_specs, operands, out_specs, out_shape = [], [], [], []
    for d in (0, 1):
        row, t8_in, t8_out = _scan_specs(d, nc, True)
        in_specs += [row] * 5 + [t8_in]
        operands += list(per_dir[d]) + [ps, kk, vl]
        out_specs += [t8_out, _bs((1, NPAIR, HD, 128), lambda c: (c, 0, 0, 0))]
        out_shape += [jax.ShapeDtypeStruct((s // 8, NPAIR, HD, 128), F32),
                      jax.ShapeDtypeStruct((nc, NPAIR, HD, 128), F32)]

    def body(*refs):
        ins = [refs[0:6], refs[6:12]]
        y_refs, ck_refs, st = (refs[12], refs[14]), (refs[13], refs[15]), refs[16]

        @pl.when(pl.program_id(0) == 0)
        def _():
            st[...] = jnp.zeros_like(st)

        for d in (0, 1):
            ck_refs[d][0] = st[d * NPAIR:(d + 1) * NPAIR]
        ones2 = _ones2()
        lane_u = lax.broadcasted_iota(jnp.int32, (HD, 256), 1) % HD
        pc = [slice(p * 128, (p + 1) * 128) for p in range(NPAIR)]

        def group(gi, carry):
            gs = (gi, ng - 1 - gi)
            blk = [[q[pl.ds(pl.multiple_of(gs[d] * 8, 8), 8), :] for q in ins[d][:5]] for d in (0, 1)]
            ss = [[st[d * NPAIR + p] for p in range(NPAIR)] for d in (0, 1)]
            for ui in range(9):
                us, ups = (ui, 7 - ui), (ui - 1, 8 - ui)
                lhs = []
                for d in (0, 1):
                    _, _, _, r8, kk8 = blk[d]
                    if ui < 8:
                        lhs += [_split(ss[d][p] * kk8[us[d]:us[d] + 1, pc[p]]) for p in range(NPAIR)]
                        for p in range(NPAIR):
                            vt = ins[d][5][gs[d], p]
                            lhs.append(jnp.where(lane_u == us[d], vt, jnp.zeros_like(vt)))
                    if ui > 0:
                        lhs += [_split(ss[d][p] * r8[ups[d]:ups[d] + 1, pc[p]]) for p in range(NPAIR)]
                res = jnp.dot(jnp.concatenate(lhs, axis=0), ones2, preferred_element_type=F32)
                k = 0
                for d in (0, 1):
                    d8, k8, b8, _, _ = blk[d]
                    u = us[d]
                    if ui < 8:
                        sa, vb = _tiles(res, k), _tiles(res, k + 1)
                        k += 2
                    if ui > 0:
                        _put_t8(y_refs[d], gs[d], ups[d], _tiles(res, k))
                        k += 1
                    if ui < 8:
                        for p in range(NPAIR):
                            ss[d][p] = (ss[d][p] * d8[u:u + 1, pc[p]] - sa[p] * b8[u:u + 1, pc[p]]
                                        + vb[p] * k8[u:u + 1, pc[p]])
            for d in (0, 1):
                for p in range(NPAIR):
                    st[d * NPAIR + p] = ss[d][p]
            return carry

        lax.fori_loop(0, ng, group, 0)

    outs = pl.pallas_call(
        body, name="rwkv_scan_fwd", grid=(nc,), in_specs=in_specs, out_specs=out_specs, out_shape=out_shape,
        scratch_shapes=[pltpu.VMEM((2 * NPAIR, HD, 128), F32)],
        compiler_params=_params(("arbitrary",)),
    )(*operands)
    return [(outs[0], outs[1]), (outs[2], outs[3])]


def _scan2_bwd(per_dir, ps, kk, vl, dyl):
    s = ps.shape[0]
    nc, ng = s // TC, TC // 8
    in_specs, operands, out_specs, out_shape = [], [], [], []
    for d in (0, 1):
        row, t8_in, t8_out = _scan_specs(d, nc, False)
        dec, kd, b, ck = per_dir[d]
        in_specs += [row] * 5 + [t8_in, t8_in, _bs((1, NPAIR, HD, 128), lambda c: (nc - 1 - c, 0, 0, 0))]
        operands += [dec, kd, b, ps, kk, vl, dyl, ck]
        out_specs += [row] * 5 + [t8_out]
        out_shape += [jax.ShapeDtypeStruct((s, RW), F32)] * 5 + [jax.ShapeDtypeStruct((s // 8, NPAIR, HD, 128), F32)]

    def body(*refs):
        ins = [refs[0:8], refs[8:16]]
        outs = [refs[16:22], refs[22:28]]
        st, sa_s, vb_s, dy_s, ds = refs[28:]

        @pl.when(pl.program_id(0) == 0)
        def _():
            ds[...] = jnp.zeros_like(ds)

        for d in (0, 1):
            st[d * (TC + 1)] = ins[d][7][0]
        ones2 = _ones2()
        lane_u = lax.broadcasted_iota(jnp.int32, (HD, 256), 1) % HD
        row_id = lax.broadcasted_iota(jnp.int32, (8, 128), 0)
        pc = [slice(p * 128, (p + 1) * 128) for p in range(NPAIR)]

        def load_rows(gs):
            return [[q[pl.ds(pl.multiple_of(gs[d] * 8, 8), 8), :] for q in ins[d][:5]] for d in (0, 1)]

        def fgroup(gi, carry):
            gs = (gi, ng - 1 - gi)
            blk = load_rows(gs)
            ss = [[st[d * (TC + 1) + gi * 8, p] for p in range(NPAIR)] for d in (0, 1)]
            for ui in range(8):
                us = (ui, 7 - ui)
                i = gi * 8 + ui
                lhs = []
                for d in (0, 1):
                    kk8 = blk[d][4]
                    lhs += [_split(ss[d][p] * kk8[us[d]:us[d] + 1, pc[p]]) for p in range(NPAIR)]
                    for ref in (ins[d][5], ins[d][6]):
                        for p in range(NPAIR):
                            t = ref[gs[d], p]
                            lhs.append(jnp.where(lane_u == us[d], t, jnp.zeros_like(t)))
                res = jnp.dot(jnp.concatenate(lhs, axis=0), ones2, preferred_element_type=F32)
                for d in (0, 1):
                    d8, k8, b8, _, _ = blk[d]
                    u = us[d]
                    sa, vb, dyb = _tiles(res, 3 * d), _tiles(res, 3 * d + 1), _tiles(res, 3 * d + 2)
                    for p in range(NPAIR):
                        sa_s[d * TC + i, p] = sa[p]
                        vb_s[d * TC + i, p] = vb[p]
                        dy_s[d * TC + i, p] = dyb[p]
                        ss[d][p] = ss[d][p] * d8[u:u + 1, pc[p]] - sa[p] * b8[u:u + 1, pc[p]] + vb[p] * k8[u:u + 1, pc[p]]
                        st[d * (TC + 1) + i + 1, p] = ss[d][p]
            return carry

        lax.fori_loop(0, ng, fgroup, 0)

        def bgroup(gj, carry):
            gi = ng - 1 - gj
            gs = (gi, ng - 1 - gi)
            blk = load_rows(gs)
            dss = [[ds[d * NPAIR + p] for p in range(NPAIR)] for d in (0, 1)]
            acc = [[[jnp.zeros((8, 128), F32) for _ in range(5)] for _ in range(NPAIR)] for _ in (0, 1)]
            for uj in range(8):
                ui = 7 - uj
                us = (ui, 7 - ui)
                i = gi * 8 + ui
                lhs_b, lhs_k, dyb = [], [], [None, None]
                for d in (0, 1):
                    _, k8, b8, r8, _ = blk[d]
                    u = us[d]
                    dyb[d] = [dy_s[d * TC + i, p] for p in range(NPAIR)]
                    for p in range(NPAIR):
                        dss[d][p] = dss[d][p] + dyb[d][p] * r8[u:u + 1, pc[p]]
                    lhs_b += [_split(dss[d][p] * b8[u:u + 1, pc[p]]) for p in range(NPAIR)]
                    lhs_b += [_split(dss[d][p] * k8[u:u + 1, pc[p]]) for p in range(NPAIR)]
                res = jnp.dot(jnp.concatenate(lhs_b + lhs_k, axis=0), ones2, preferred_element_type=F32)
                for d in (0, 1):
                    d8, _, _, _, kk8 = blk[d]
                    u = us[d]
                    dsa, dvb = _tiles(res, 2 * d), _tiles(res, 2 * d + 1)
                    _put_t8(outs[d][5], gs[d], u, dvb)
                    for p in range(NPAIR):
                        sp, sn = st[d * (TC + 1) + i, p], st[d * (TC + 1) + i + 1, p]
                        dsv = dss[d][p]
                        vals = (jnp.sum(sn * dyb[d][p], axis=0, keepdims=True), jnp.sum(dsv * sp, axis=0, keepdims=True),
                                -jnp.sum(dsv * sa_s[d * TC + i, p], axis=0, keepdims=True),
                                jnp.sum(dsv * vb_s[d * TC + i, p], axis=0, keepdims=True),
                                -jnp.sum(sp * dsa[p], axis=0, keepdims=True))
                        acc[d][p] = [jnp.where(row_id == u, o, a_) for o, a_ in zip(vals, acc[d][p])]
                        dss[d][p] = dsv * d8[u:u + 1, pc[p]] - dsa[p] * kk8[u:u + 1, pc[p]]
            for d in (0, 1):
                rows8 = pl.ds(pl.multiple_of(gs[d] * 8, 8), 8)
                for p in range(NPAIR):
                    ds[d * NPAIR + p] = dss[d][p]
                    for o_ref, a_ in zip(outs[d][:5], acc[d][p]):
                        o_ref[rows8, pc[p]] = a_
            return carry

        lax.fori_loop(0, ng, bgroup, 0)

    chunk = lambda k: pltpu.VMEM((k, NPAIR, HD, 128), F32)
    res = pl.pallas_call(
        body, name="rwkv_scan_bwd", grid=(nc,), in_specs=in_specs, out_specs=out_specs, out_shape=out_shape,
        scratch_shapes=[chunk(2 * (TC + 1)), chunk(2 * TC), chunk(2 * TC), chunk(2 * TC),
                        pltpu.VMEM((2 * NPAIR, HD, 128), F32)],
        compiler_params=_params(("arbitrary",)),
    )(*operands)
    return [res[0:6], res[6:12]]


MT = 256
MN = 256


def _merge_fwd(ya, yr, yx, wa, wr, wx, proj, gate_b):
    s = ya.shape[0]

    def body(ya_ref, yr_ref, yx_ref, wa_ref, wr_ref, wx_ref, m0, m1, m2, b0, b1, b2, o_ref):
        acc = jnp.zeros((MT, MN), F32)
        for y_ref, w_ref, m_ref, b_ref in ((ya_ref, wa_ref, m0, b0), (yr_ref, wr_ref, m1, b1), (yx_ref, wx_ref, m2, b2)):
            u = _dot(y_ref[...], w_ref[...], ((1,), (0,)))
            acc = acc + jax.nn.sigmoid(m_ref[...] + b_ref[...]) * u
        o_ref[...] = acc.astype(BF16)

    mg = lambda br: _bs((MT, MN), lambda i, j: (i, C_MG // MN + br * (D // MN) + j))
    gb = lambda br: _bs((1, MN), lambda i, j: (0, br * (D // MN) + j))
    return pl.pallas_call(
        body, name="merge_fwd", grid=(s // MT, D // MN),
        in_specs=[_bs((MT, RW), lambda i, j: (i, 0)), _bs((MT, RW), lambda i, j: (i, 0)), _bs((MT, XW), lambda i, j: (i, 0)),
                  _bs((RW, MN), lambda i, j: (0, j)), _bs((RW, MN), lambda i, j: (0, j)), _bs((XW, MN), lambda i, j: (0, j)),
                  mg(0), mg(1), mg(2), gb(0), gb(1), gb(2)],
        out_specs=_bs((MT, MN), lambda i, j: (i, j)),
        out_shape=jax.ShapeDtypeStruct((s, D), BF16),
        compiler_params=_params(("parallel", "arbitrary")),
    )(ya, yr, yx, wa, wr, wx, proj, proj, proj, gate_b, gate_b, gate_b)


def _out_fwd(merged, w_out, x, target):
    s = x.shape[0]
    tm, tn = min(512, s), 512

    def body(m_ref, w_ref, x_ref, t_ref, loss_ref, d_ref):
        @pl.when((pl.program_id(0) == 0) & (pl.program_id(1) == 0))
        def _():
            loss_ref[...] = jnp.zeros_like(loss_ref)

        out = x_ref[...] + jnp.dot(m_ref[...], w_ref[...], preferred_element_type=F32)
        err = out - t_ref[...]
        d_ref[...] = err * (1.0 / D)
        loss_ref[...] += jnp.sum(err * err)

    return pl.pallas_call(
        body, name="out_fwd", grid=(s // tm, D // tn),
        in_specs=[_bs((tm, D), lambda i, j: (i, 0)), _bs((D, tn), lambda i, j: (0, j)),
                  _bs((tm, tn), lambda i, j: (i, j)), _bs((tm, tn), lambda i, j: (i, j))],
        out_specs=[_bs((8, 128), lambda i, j: (0, 0)), _bs((tm, tn), lambda i, j: (i, j))],
        out_shape=[jax.ShapeDtypeStruct((8, 128), F32), jax.ShapeDtypeStruct((s, D), F32)],
        compiler_params=_params(("arbitrary", "arbitrary")),
    )(merged, w_out, x, target)


def _merge_bwd(ya, yr, yx, wa, wr, wx, proj, gate_b, dmerged):
    s = ya.shape[0]

    def body(ya_ref, yr_ref, yx_ref, wa_ref, wr_ref, wx_ref, m0, m1, m2, b0, b1, b2, dm_ref,
             dg0, dg1, dg2, du0, du1, du2, dya_ref, dyr_ref, dyx_ref):
        @pl.when(pl.program_id(1) == 0)
        def _():
            dya_ref[...] = jnp.zeros_like(dya_ref)
            dyr_ref[...] = jnp.zeros_like(dyr_ref)
            dyx_ref[...] = jnp.zeros_like(dyx_ref)

        dm = dm_ref[...]
        for y_ref, w_ref, m_ref, b_ref, dg_ref, du_ref, dy_ref in (
                (ya_ref, wa_ref, m0, b0, dg0, du0, dya_ref), (yr_ref, wr_ref, m1, b1, dg1, du1, dyr_ref),
                (yx_ref, wx_ref, m2, b2, dg2, du2, dyx_ref)):
            w = w_ref[...]
            u = _dot(y_ref[...], w, ((1,), (0,)))
            gt = jax.nn.sigmoid(m_ref[...] + b_ref[...])
            dg_ref[...] = (dm * u * gt * (1.0 - gt)).astype(BF16)
            du = (dm * gt).astype(BF16)
            du_ref[...] = du
            dy_ref[...] += _dot(du, w, ((1,), (1,)))

    mg = lambda br: _bs((MT, MN), lambda i, j: (i, C_MG // MN + br * (D // MN) + j))
    gb = lambda br: _bs((1, MN), lambda i, j: (0, br * (D // MN) + j))
    tile = _bs((MT, MN), lambda i, j: (i, j))
    return pl.pallas_call(
        body, name="merge_bwd", grid=(s // MT, D // MN),
        in_specs=[_bs((MT, RW), lambda i, j: (i, 0)), _bs((MT, RW), lambda i, j: (i, 0)), _bs((MT, XW), lambda i, j: (i, 0)),
                  _bs((RW, MN), lambda i, j: (0, j)), _bs((RW, MN), lambda i, j: (0, j)), _bs((XW, MN), lambda i, j: (0, j)),
                  mg(0), mg(1), mg(2), gb(0), gb(1), gb(2), tile],
        out_specs=[tile] * 6 + [_bs((MT, RW), lambda i, j: (i, 0)), _bs((MT, RW), lambda i, j: (i, 0)),
                                _bs((MT, XW), lambda i, j: (i, 0))],
        out_shape=[jax.ShapeDtypeStruct((s, D), BF16)] * 6 + [jax.ShapeDtypeStruct((s, RW), F32),
                                                               jax.ShapeDtypeStruct((s, RW), F32),
                                                               jax.ShapeDtypeStruct((s, XW), F32)],
        compiler_params=_params(("parallel", "arbitrary")),
    )(ya, yr, yx, wa, wr, wx, proj, proj, proj, gate_b, gate_b, gate_b, dmerged)


def _colsum(a, name):
    m, n = a.shape
    tm, tn = min(512, m), 512

    def body(a_ref, o_ref):
        @pl.when(pl.program_id(1) == 0)
        def _():
            o_ref[...] = jnp.zeros_like(o_ref)

        o_ref[...] += jnp.sum(a_ref[...].astype(F32), axis=0, keepdims=True)

    return pl.pallas_call(
        body, name=name, grid=(n // tn, m // tm),
        in_specs=[_bs((tm, tn), lambda j, i: (i, j))], out_specs=_bs((1, tn), lambda j, i: (0, j)),
        out_shape=jax.ShapeDtypeStruct((1, n), F32),
        compiler_params=_params(("parallel", "arbitrary")),
    )(a)


def _in_bwd(dproj, w_in, x, g, dout):
    s = x.shape[0]
    tm, tk = 256, 896
    nk = NIN // tk

    def body(dp_ref, w_ref, x_ref, g_ref, do_ref, gx_ref, gg_ref, acc):
        i, kk = pl.program_id(0), pl.program_id(1)

        @pl.when((i == 0) & (kk == 0))
        def _():
            gg_ref[...] = jnp.zeros_like(gg_ref)

        @pl.when(kk == 0)
        def _():
            acc[...] = jnp.zeros_like(acc)

        acc[...] += _dot(dp_ref[...], w_ref[...], ((1,), (1,)))

        @pl.when(kk == nk - 1)
        def _():
            xv, dh, gv = x_ref[...], acc[...], g_ref[...]
            r = lax.rsqrt(jnp.mean(xv * xv, axis=-1, keepdims=True) + NORM_EPS)
            xn = xv * r
            gg_ref[...] += jnp.sum(dh * xn, axis=0, keepdims=True)
            dxn = dh * gv
            dx = r * (dxn - xn * jnp.mean(dxn * xn, axis=-1, keepdims=True))
            gx_ref[...] = do_ref[...] + dx

    return pl.pallas_call(
        body, name="in_bwd", grid=(s // tm, nk),
        in_specs=[_bs((tm, tk), lambda i, kk: (i, kk)), _bs((D, tk), lambda i, kk: (0, kk)),
                  _bs((tm, D), lambda i, kk: (i, 0)), _bs((1, D), lambda i, kk: (0, 0)), _bs((tm, D), lambda i, kk: (i, 0))],
        out_specs=[_bs((tm, D), lambda i, kk: (i, 0)), _bs((1, D), lambda i, kk: (0, 0))],
        out_shape=[jax.ShapeDtypeStruct((s, D), F32), jax.ShapeDtypeStruct((1, D), F32)],
        scratch_shapes=[pltpu.VMEM((tm, D), F32)],
        compiler_params=_params(("arbitrary", "arbitrary")),
    )(dproj, w_in, x, g, dout)


def _adamw_math(w, g, m, v):
    m = ADAM_B1 * m + (1.0 - ADAM_B1) * g
    v = ADAM_B2 * v + (1.0 - ADAM_B2) * jnp.square(g)
    m_hat = m / (1.0 - ADAM_B1 ** ADAM_STEP)
    v_hat = v / (1.0 - ADAM_B2 ** ADAM_STEP)
    delta = -ADAM_LR * (m_hat / (jnp.sqrt(v_hat) + ADAM_EPS) + ADAM_WD * w)
    return delta, m, v


def _adamw(parts, w, m, v, name):
    rows, cols = w.shape
    tr = rows
    for cand in (256, 128, 64, 32, 16, 8):
        if rows % cand == 0 and cand * cols * 4 <= (1 << 20):
            tr = cand
            break
    n = len(parts)

    def body(*refs):
        g = refs[0][...].astype(F32)
        for r in refs[1:n]:
            g = g + r[...].astype(F32)
        w_ref, m_ref, v_ref, g_out, d_out, m_out, v_out = refs[n:]
        delta, m_new, v_new = _adamw_math(w_ref[...], g, m_ref[...], v_ref[...])
        g_out[...] = g
        d_out[...] = delta
        m_out[...] = m_new
        v_out[...] = v_new

    spec = _bs((tr, cols), lambda i: (i, 0))
    return pl.pallas_call(
        body, name=name, grid=(rows // tr,),
        in_specs=[spec] * (n + 3), out_specs=[spec] * 4,
        out_shape=[jax.ShapeDtypeStruct((rows, cols), F32)] * 4,
        compiler_params=_params(("parallel",)),
    )(*parts, w, m, v)


def _sum_parts(parts, name):
    rows, cols = parts[0].shape
    tr = rows
    for cand in (256, 128, 64, 32, 16, 8):
        if rows % cand == 0 and cand * cols * 4 <= (1 << 20):
            tr = cand
            break

    def body(*refs):
        acc = refs[0][...].astype(F32)
        for r in refs[1:-1]:
            acc = acc + r[...].astype(F32)
        refs[-1][...] = acc

    spec = _bs((tr, cols), lambda i: (i, 0))
    return pl.pallas_call(
        body, name=name, grid=(rows // tr,), in_specs=[spec] * len(parts), out_specs=spec,
        out_shape=jax.ShapeDtypeStruct((rows, cols), F32), compiler_params=_params(("parallel",)),
    )(*parts)


ANY = pl.BlockSpec(memory_space=pl.ANY)


def _other_chips(x, y):
    return [(1 - x, y), (x, 1 - y), (1 - x, 1 - y)]


def _gather_shards(arrays, name):
    n = len(arrays)

    def body(*refs):
        ins, outs = refs[:n], refs[n:2 * n]
        send_sems, recv_sems, local_sems = refs[2 * n:]
        x, y, c = lax.axis_index("x"), lax.axis_index("y"), lax.axis_index("c")
        me = 2 * x + y
        chips = _other_chips(x, y)
        locals_, sends = [], []
        for i in range(n):
            cp = pltpu.make_async_copy(ins[i], outs[i].at[me], local_sems.at[i])
            cp.start()
            locals_.append(cp)
            for j, (px, py) in enumerate(chips):
                rc = pltpu.make_async_remote_copy(
                    src_ref=ins[i], dst_ref=outs[i].at[me], send_sem=send_sems.at[3 * i + j],
                    recv_sem=recv_sems.at[3 * i + j], device_id=(px, py, c), device_id_type=MESH)
                rc.start()
                sends.append(rc)
        for i in range(n):
            for j, (px, py) in enumerate(chips):
                pltpu.make_async_remote_copy(
                    src_ref=ins[i], dst_ref=outs[i].at[2 * px + py], send_sem=send_sems.at[3 * i + j],
                    recv_sem=recv_sems.at[3 * i + j], device_id=(px, py, c), device_id_type=MESH).wait_recv()
        for rc in sends:
            rc.wait_send()
        for cp in locals_:
            cp.wait()

    return pl.pallas_call(
        body, name=name, in_specs=[ANY] * n, out_specs=[ANY] * n,
        out_shape=[jax.ShapeDtypeStruct((4,) + a.shape, a.dtype) for a in arrays],
        scratch_shapes=[pltpu.SemaphoreType.DMA((3 * n,)), pltpu.SemaphoreType.DMA((3 * n,)),
                        pltpu.SemaphoreType.DMA((n,))],
        compiler_params=pltpu.CompilerParams(has_side_effects=True),
    )(*arrays)


def _scatter_shards(stacks, name):
    n = len(stacks)

    def body(*refs):
        ins, outs = refs[:n], refs[n:2 * n]
        send_sems, recv_sems = refs[2 * n:]
        x, y, c = lax.axis_index("x"), lax.axis_index("y"), lax.axis_index("c")
        chips = _other_chips(x, y)
        sends = []
        for i in range(n):
            for j, (px, py) in enumerate(chips):
                rc = pltpu.make_async_remote_copy(
                    src_ref=ins[i].at[2 * px + py], dst_ref=outs[i].at[j], send_sem=send_sems.at[3 * i + j],
                    recv_sem=recv_sems.at[3 * i + j], device_id=(px, py, c), device_id_type=MESH)
                rc.start()
                sends.append(rc)
        for rc in sends:
            rc.wait_recv()
        for rc in sends:
            rc.wait_send()

    return pl.pallas_call(
        body, name=name, in_specs=[ANY] * n, out_specs=[ANY] * n,
        out_shape=[jax.ShapeDtypeStruct((3,) + a.shape[1:], a.dtype) for a in stacks],
        scratch_shapes=[pltpu.SemaphoreType.DMA((3 * n,)), pltpu.SemaphoreType.DMA((3 * n,))],
        compiler_params=pltpu.CompilerParams(has_side_effects=True),
    )(*stacks)


def _swap_sibling(arrays, name):
    n = len(arrays)

    def body(*refs):
        ins, outs = refs[:n], refs[n:2 * n]
        send_sems, recv_sems = refs[2 * n:]
        sib = (lax.axis_index("x"), lax.axis_index("y"), 1 - lax.axis_index("c"))
        cps = []
        for i in range(n):
            rc = pltpu.make_async_remote_copy(src_ref=ins[i], dst_ref=outs[i], send_sem=send_sems.at[i],
                                              recv_sem=recv_sems.at[i], device_id=sib, device_id_type=MESH)
            rc.start()
            cps.append(rc)
        for rc in cps:
            rc.wait_recv()
        for rc in cps:
            rc.wait_send()

    return pl.pallas_call(
        body, name=name, in_specs=[ANY] * n, out_specs=[ANY] * n,
        out_shape=[jax.ShapeDtypeStruct(a.shape, a.dtype) for a in arrays],
        scratch_shapes=[pltpu.SemaphoreType.DMA((n,)), pltpu.SemaphoreType.DMA((n,))],
        compiler_params=pltpu.CompilerParams(has_side_effects=True),
    )(*arrays)


def _all_reduce_small(v):
    rows = v.shape[0]

    def body(v_ref, o_ref, buf, send_sems, recv_sems):
        x, y, c = lax.axis_index("x"), lax.axis_index("y"), lax.axis_index("c")
        me = 4 * x + 2 * y + c
        buf[me] = v_ref[...]
        cps = []
        for kbits in range(1, 8):
            bx, by, bc = (kbits >> 2) & 1, (kbits >> 1) & 1, kbits & 1
            px = jnp.where(bx == 1, 1 - x, x)
            py = jnp.where(by == 1, 1 - y, y)
            pc = jnp.where(bc == 1, 1 - c, c)
            rc = pltpu.make_async_remote_copy(src_ref=v_ref, dst_ref=buf.at[me], send_sem=send_sems.at[kbits - 1],
                                              recv_sem=recv_sems.at[kbits - 1], device_id=(px, py, pc),
                                              device_id_type=MESH)
            rc.start()
            cps.append((rc, 4 * px + 2 * py + pc))
        for kbits, (rc, src) in enumerate(cps):
            pltpu.make_async_remote_copy(src_ref=v_ref, dst_ref=buf.at[src], send_sem=send_sems.at[kbits],
                                         recv_sem=recv_sems.at[kbits], device_id=(x, y, c),
                                         device_id_type=MESH).wait_recv()
        for rc, _ in cps:
            rc.wait_send()
        acc = buf[0]
        for d in range(1, 8):
            acc = acc + buf[d]
        o_ref[...] = acc

    return pl.pallas_call(
        body, name="all_reduce_small",
        in_specs=[pl.BlockSpec(memory_space=pltpu.VMEM)], out_specs=pl.BlockSpec(memory_space=pltpu.VMEM),
        out_shape=jax.ShapeDtypeStruct((rows, 128), F32),
        scratch_shapes=[pltpu.VMEM((8, rows, 128), F32), pltpu.SemaphoreType.DMA((7,)), pltpu.SemaphoreType.DMA((7,))],
        compiler_params=pltpu.CompilerParams(has_side_effects=True, vmem_limit_bytes=VMEM_LIMIT),
    )(v)


def _rope_tables(s):
    half = HD // 2
    inv = 10000.0 ** (-jnp.arange(half, dtype=F32) / half)
    ang = jnp.arange(s, dtype=F32)[:, None] * inv[None, :]
    cos, sin = jnp.cos(ang), jnp.sin(ang)
    return jnp.concatenate([cos, cos], axis=1), jnp.concatenate([sin, sin], axis=1)


def _local_step(x, mem, target, norm_g, mem_norm_g, w_in, gate_b, gq, gk, sink, wa, mu, k_k, k_a, r_k, w0, w2, a0, a2,
                ln_w, ln_b, wr, w_kv, gxq, gxk, wx, w_out):
    s = x.shape[0]
    cos, sin = _rope_tables(s)
    r_k = r_k.reshape(1, RW)

    proj, h = _proj_fwd(x, norm_g, w_in)
    ya = _attn_fwd(proj, cos, sin, gq, gk, sink)
    mkv, mn = _mem_kv(mem, mem_norm_g, w_kv)
    yx = _xattn_fwd(proj, mkv, gxq, gxk)
    ps = _shift_fwd(proj, mu)
    kk, dec0, kd0, b0, dec1, kd1, b1 = _pre_fwd(ps, k_k, k_a, w0, w2, a0, a2)
    v8 = _to_t8(ps[:, 2 * RW:3 * RW])
    (y80, ck0), (y81, ck1) = _scan2_fwd([(dec0, kd0, b0), (dec1, kd1, b1)], ps, kk, v8)
    y0, y1 = _from_t8(y80), _from_t8(y81)
    yr = _post_fwd(y0, y1, ps, kd0, kd1, proj, r_k, ln_w, ln_b)
    merged = _merge_fwd(ya, yr, yx, wa, wr, wx, proj, gate_b)
    loss_tile, dout = _out_fwd(merged, w_out, x, target)
    loss_sum = loss_tile[0, 0]

    g = {}
    dmerged = _matmul(dout, w_out, mode="nt", m=s, n=D, k=D, tm=min(512, s), tn=512, tk=512, name="dmerged")
    g["w_out"] = _matmul(merged, dout, mode="tn", m=D, n=D, k=s, tm=512, tn=512, tk=min(512, s), name="grad_w_out")
    dg0, dg1, dg2, du0, du1, du2, dya, dyr, dyx = _merge_bwd(ya, yr, yx, wa, wr, wx, proj, gate_b, dmerged)
    g["attn_w_o"] = _matmul(ya, du0, mode="tn", m=RW, n=D, k=s, tm=RW, tn=512, tk=min(512, s), name="grad_attn_w_o")
    g["rwkv_w_o"] = _matmul(yr, du1, mode="tn", m=RW, n=D, k=s, tm=RW, tn=512, tk=min(512, s), name="grad_rwkv_w_o")
    g["x_w_o"] = _matmul(yx, du2, mode="tn", m=XW, n=D, k=s, tm=XW, tn=512, tk=min(512, s), name="grad_x_w_o")
    dmg = jnp.concatenate([dg0, dg1, dg2], axis=1)
    g["gate_b"] = _colsum(dmg, "grad_gate_b")

    daq, dak, dav, dag, g["attn_q_norm_g"], g["attn_k_norm_g"], g["attn_sink"] = _attn_bwd(proj, cos, sin, gq, gk, sink, dya)

    dxq, dxg, dmkv, g["x_q_norm_g"], g["x_k_norm_g"] = _xattn_bwd(proj, mkv, gxq, gxk, dyx)
    g["x_w_kv"] = _matmul(mn, dmkv, mode="tn", m=D, n=2 * XW, k=NMEM, tm=512, tn=512, tk=NMEM, name="grad_x_w_kv")
    dmn = _matmul(dmkv, w_kv, mode="nt", m=NMEM, n=D, k=2 * XW, tm=NMEM, tn=512, tk=2 * XW, name="dmn")
    g["mem_norm_g"] = _mem_bwd(mem, mem_norm_g, dmn)

    dys, dr_p, dv_p, dkd0_p, dkd1_p, drg, g["rwkv_r_k"], g["rwkv_ln_w"], g["rwkv_ln_b"] = _post_bwd(
        y0, y1, ps, kd0, kd1, proj, r_k, ln_w, ln_b, dyr)
    dy8 = _to_t8(dys)
    (dr0, dd0, db0, dk0, dkk0, dv80), (dr1, dd1, db1, dk1, dkk1, dv81) = _scan2_bwd(
        [(dec0, kd0, b0, ck0), (dec1, kd1, b1, ck1)], ps, kk, v8, dy8)
    dr = dr_p + dr0 + dr1
    dv = dv_p + _from_t8(dv80) + _from_t8(dv81)
    cts = (dkk0 + dkk1, dd0, dk0 + dkd0_p, db0, dd1, dk1 + dkd1_p, db1)
    dps, g["rwkv_k_k"], g["rwkv_k_a"], g["rwkv_w0"], g["rwkv_w2"], g["rwkv_a0"], g["rwkv_a2"] = _pre_bwd(
        ps, k_k, k_a, w0, w2, a0, a2, dr, dv, cts)
    drs, g["rwkv_mu"] = _shift_bwd(proj, mu, dps)

    dproj = jnp.concatenate([daq.astype(BF16), dak.astype(BF16), dav.astype(BF16), dag.astype(BF16), drs.astype(BF16),
                             drg.astype(BF16), dxq.astype(BF16), dxg.astype(BF16), dmg], axis=1)
    g["w_in"] = _matmul(h, dproj, mode="tn", m=D, n=NIN, k=s, tm=512, tn=896, tk=min(512, s), name="grad_w_in")
    grad_x, g["norm_g"] = _in_bwd(dproj, w_in, x, norm_g, dout)
    g["rwkv_r_k"] = g["rwkv_r_k"].reshape(AH, HD)
    return loss_sum, grad_x, g


WEIGHTS = ['norm_g', 'mem_norm_g', 'w_in', 'gate_b', 'attn_q_norm_g', 'attn_k_norm_g', 'attn_sink', 'attn_w_o',
           'rwkv_mu', 'rwkv_k_k', 'rwkv_k_a', 'rwkv_r_k', 'rwkv_w0', 'rwkv_w2', 'rwkv_a0', 'rwkv_a2', 'rwkv_ln_w',
           'rwkv_ln_b', 'rwkv_w_o', 'x_w_kv', 'x_q_norm_g', 'x_k_norm_g', 'x_w_o', 'w_out']
BIG = ['w_in', 'attn_w_o', 'rwkv_w_o', 'x_w_kv', 'x_w_o', 'w_out']
COL_SHARDED = ['w_in', 'attn_w_o', 'rwkv_w_o', 'x_w_o']
LORA = ['rwkv_w0', 'rwkv_w2', 'rwkv_a0', 'rwkv_a2']
SMALL = [n for n in WEIGHTS if n not in BIG]


def _unshard_cols(stack):
    return jnp.concatenate([stack[i] for i in range(4)], axis=-1)


def _shard_cols(full):
    w = full.shape[-1] // 4
    return [full[..., i * w:(i + 1) * w] for i in range(4)]


def kernel(x, mem, norm_g, mem_norm_g, w_in, gate_b, attn_q_norm_g, attn_k_norm_g, attn_sink, attn_w_o, rwkv_mu, rwkv_k_k, rwkv_k_a, rwkv_r_k, rwkv_w0, rwkv_w2, rwkv_a0, rwkv_a2, rwkv_ln_w, rwkv_ln_b, rwkv_w_o, x_w_kv, x_q_norm_g, x_k_norm_g, x_w_o, w_out, loss_target, m_norm_g, m_mem_norm_g, m_w_in, m_gate_b, m_attn_q_norm_g, m_attn_k_norm_g, m_attn_sink, m_attn_w_o, m_rwkv_mu, m_rwkv_k_k, m_rwkv_k_a, m_rwkv_r_k, m_rwkv_w0, m_rwkv_w2, m_rwkv_a0, m_rwkv_a2, m_rwkv_ln_w, m_rwkv_ln_b, m_rwkv_w_o, m_x_w_kv, m_x_q_norm_g, m_x_k_norm_g, m_x_w_o, m_w_out, v_norm_g, v_mem_norm_g, v_w_in, v_gate_b, v_attn_q_norm_g, v_attn_k_norm_g, v_attn_sink, v_attn_w_o, v_rwkv_mu, v_rwkv_k_k, v_rwkv_k_a, v_rwkv_r_k, v_rwkv_w0, v_rwkv_w2, v_rwkv_a0, v_rwkv_a2, v_rwkv_ln_w, v_rwkv_ln_b, v_rwkv_w_o, v_x_w_kv, v_x_q_norm_g, v_x_k_norm_g, v_x_w_o, v_w_out):
    args = dict(locals())
    canon = lambda a: a[0] if a.ndim > 2 else a
    w = {n: canon(args[n]) for n in WEIGHTS}
    m = {n: canon(args["m_" + n]) for n in WEIGHTS}
    v = {n: canon(args["v_" + n]) for n in WEIGHTS}
    shard = 2 * lax.axis_index("x") + lax.axis_index("y")

    local = [w[n].astype(BF16) for n in BIG] + [w[n] for n in LORA]
    stacks = dict(zip(BIG + LORA, _gather_shards(local, "gather_weights")))
    full = {}
    for n in COL_SHARDED + LORA:
        full[n] = _unshard_cols(stacks[n])
    full["x_w_kv"] = stacks["x_w_kv"].reshape(D, 2 * XW)
    full["w_out"] = stacks["w_out"].reshape(D, D)

    loss_sum, grad_x, g = _local_step(
        x[0], mem[0], loss_target[0], w["norm_g"], w["mem_norm_g"], full["w_in"], w["gate_b"], w["attn_q_norm_g"],
        w["attn_k_norm_g"], w["attn_sink"], full["attn_w_o"], w["rwkv_mu"], w["rwkv_k_k"], w["rwkv_k_a"], w["rwkv_r_k"],
        full["rwkv_w0"], full["rwkv_w2"], full["rwkv_a0"], full["rwkv_a2"], w["rwkv_ln_w"], w["rwkv_ln_b"],
        full["rwkv_w_o"], full["x_w_kv"], w["x_q_norm_g"], w["x_k_norm_g"], full["x_w_o"], full["w_out"])

    loss = lax.psum(0.5 * loss_sum / D, ("x", "y", "c"))

    def as_stack(n, dtype):
        if n in COL_SHARDED:
            return jnp.stack([p.astype(dtype) for p in _shard_cols(g[n])])
        return g[n].reshape((4, g[n].shape[0] // 4) + g[n].shape[1:]).astype(dtype)

    recv = _scatter_shards([as_stack(n, BF16) for n in BIG], "scatter_grads")
    partial = []
    for n, r in zip(BIG, recv):
        own = lax.dynamic_index_in_dim(as_stack(n, F32), shard, 0, keepdims=False)
        partial.append(_sum_parts([own, r[0], r[1], r[2]], "sum_" + n))
    theirs = _swap_sibling(partial, "swap_partials")

    out_g, out_d, out_m, out_v = {}, {}, {}, {}
    for n, mine, other in zip(BIG, partial, theirs):
        out_g[n], out_d[n], out_m[n], out_v[n] = _adamw([mine, other], w[n], m[n], v[n], "adamw_" + n)

    flat = jnp.concatenate([g[n].reshape(-1) for n in SMALL])
    total = flat.shape[0]
    padded = -(-total // 1024) * 1024
    flat = jnp.pad(flat, (0, padded - total)).reshape(padded // 128, 128)
    red = _all_reduce_small(flat).reshape(-1)
    off = 0
    gs = {}
    for n in SMALL:
        size = g[n].size
        t = red[off:off + size].reshape(g[n].shape)
        off += size
        if n in LORA:
            wd = t.shape[-1] // 4
            t = lax.dynamic_slice_in_dim(t, shard * wd, wd, axis=t.ndim - 1)
        gs[n] = t

    def pack(d):
        f = jnp.concatenate([d[n].reshape(-1) for n in SMALL])
        return jnp.pad(f, (0, -(-f.shape[0] // 1024) * 1024 - f.shape[0])).reshape(-1, 128)

    pg, pd, pm, pv = _adamw([pack(gs)], pack(w), pack(m), pack(v), "adamw_small")
    off = 0
    for n in SMALL:
        size = w[n].size
        for dst, src in ((out_g, pg), (out_d, pd), (out_m, pm), (out_v, pv)):
            dst[n] = src.reshape(-1)[off:off + size].reshape(w[n].shape)
        off += size

    lead = lambda d: [d[n][None] if args[n].ndim > 2 else d[n] for n in WEIGHTS]
    return (loss, grad_x[None], *lead(out_g), *lead(out_d), *lead(out_m), *lead(out_v))
```

```python
import functools

import jax
import jax.numpy as jnp
from jax import lax
from jax.experimental import pallas as pl
from jax.experimental.pallas import tpu as pltpu

F32 = jnp.float32
BF16 = jnp.bfloat16
HI = lax.Precision.HIGHEST
MESH = pl.DeviceIdType.MESH

D = 2048
NMEM = 256
NORM_EPS = 1e-6
NEG_INF = -1e30
GN_EPS = 64e-5
HD = 64
AH = 12
AKV = 4
RW = 768
XH = 4
XD = 128
XW = 512
NIN = 12544
RSW = 2560
C_AQ, C_AK, C_AV, C_AG, C_RS, C_RG, C_XQ, C_XG, C_MG = 0, 768, 1024, 1280, 2048, 4608, 5376, 5888, 6400
WIN = 384
QB = 128
TC = 16
NPAIR = 6

ADAM_LR, ADAM_B1, ADAM_B2, ADAM_EPS, ADAM_WD, ADAM_STEP = 0.001, 0.9, 0.999, 1e-08, 0.01, 10

VMEM_LIMIT = 56 * 1024 * 1024


def _bs(shape, imap):
    return pl.BlockSpec(shape, imap)


def _params(sem=None, vmem=VMEM_LIMIT):
    return pltpu.CompilerParams(dimension_semantics=sem, vmem_limit_bytes=vmem)


def _dot(a, b, dims):
    return lax.dot_general(a.astype(BF16), b.astype(BF16), (dims, ((), ())), preferred_element_type=F32)


@jax.custom_vjp
def _mm_nn(a, b):
    return _dot(a, b, ((1,), (0,)))


def _mm_nn_fwd(a, b):
    return _mm_nn(a, b), (a, b)


def _mm_nn_bwd(res, ct):
    a, b = res
    return _dot(ct, b, ((1,), (1,))), _dot(a, ct, ((0,), (0,)))


_mm_nn.defvjp(_mm_nn_fwd, _mm_nn_bwd)


@jax.custom_vjp
def _mm_nt(a, b):
    return _dot(a, b, ((1,), (1,)))


def _mm_nt_fwd(a, b):
    return _mm_nt(a, b), (a, b)


def _mm_nt_bwd(res, ct):
    a, b = res
    return _dot(ct, b, ((1,), (0,))), _dot(ct, a, ((0,), (0,)))


_mm_nt.defvjp(_mm_nt_fwd, _mm_nt_bwd)


def _seg_matrix(n, seg):
    r = lax.broadcasted_iota(jnp.int32, (n, n), 0) // seg
    c = lax.broadcasted_iota(jnp.int32, (n, n), 1) // seg
    return (r == c).astype(F32)


def _rot_matrix():
    r = lax.broadcasted_iota(jnp.int32, (HD, HD), 0)
    c = lax.broadcasted_iota(jnp.int32, (HD, HD), 1)
    return jnp.where(c == r + HD // 2, 1.0, 0.0).astype(F32) - jnp.where(c == r - HD // 2, 1.0, 0.0).astype(F32)


def _hdot(a, m):
    return jnp.dot(a, m, precision=HI, preferred_element_type=F32)


def _rms(t, g):
    return t * lax.rsqrt(jnp.mean(t * t, axis=-1, keepdims=True) + NORM_EPS) * g


def _silu(t):
    return t * jax.nn.sigmoid(t)


def _softplus(z):
    return jnp.maximum(z, 0.0) + jnp.log(1.0 + jnp.exp(-jnp.abs(z)))


def _matmul(a, b, *, mode, m, n, k, tm, tn, tk, name, a_off=(0, 0), b_off=(0, 0), out_dtype=F32):
    nk = k // tk
    if mode == "tn":
        a_spec = _bs((tk, tm), lambda i, j, kk: (kk + a_off[0], i + a_off[1]))
        dims = ((0,), (0,))
    else:
        a_spec = _bs((tm, tk), lambda i, j, kk: (i + a_off[0], kk + a_off[1]))
        dims = ((1,), (1,)) if mode == "nt" else ((1,), (0,))
    if mode == "nt":
        b_spec = _bs((tn, tk), lambda i, j, kk: (j + b_off[0], kk + b_off[1]))
    else:
        b_spec = _bs((tk, tn), lambda i, j, kk: (kk + b_off[0], j + b_off[1]))

    def body(a_ref, b_ref, o_ref, acc):
        kk = pl.program_id(2)

        @pl.when(kk == 0)
        def _():
            acc[...] = jnp.zeros_like(acc)

        acc[...] += _dot(a_ref[...], b_ref[...], dims)

        @pl.when(kk == nk - 1)
        def _():
            o_ref[...] = acc[...].astype(out_dtype)

    return pl.pallas_call(
        body, name=name, grid=(m // tm, n // tn, nk),
        in_specs=[a_spec, b_spec], out_specs=_bs((tm, tn), lambda i, j, kk: (i, j)),
        out_shape=jax.ShapeDtypeStruct((m, n), out_dtype),
        scratch_shapes=[pltpu.VMEM((tm, tn), F32)],
        compiler_params=_params(("parallel", "parallel", "arbitrary")),
    )(a, b)


def _proj_fwd(x, g, w):
    s = x.shape[0]
    tm, tn = min(512, s), 896

    def body(x_ref, g_ref, w_ref, o_ref, h_ref, hs):
        @pl.when(pl.program_id(1) == 0)
        def _():
            h = _rms(x_ref[...], g_ref[...]).astype(BF16)
            hs[...] = h
            h_ref[...] = h

        o_ref[...] = jnp.dot(hs[...], w_ref[...], preferred_element_type=F32)

    return pl.pallas_call(
        body, name="proj_fwd", grid=(s // tm, NIN // tn),
        in_specs=[_bs((tm, D), lambda i, j: (i, 0)), _bs((1, D), lambda i, j: (0, 0)), _bs((D, tn), lambda i, j: (0, j))],
        out_specs=[_bs((tm, tn), lambda i, j: (i, j)), _bs((tm, D), lambda i, j: (i, 0))],
        out_shape=[jax.ShapeDtypeStruct((s, NIN), F32), jax.ShapeDtypeStruct((s, D), BF16)],
        scratch_shapes=[pltpu.VMEM((tm, D), BF16)],
        compiler_params=_params(("parallel", "arbitrary")),
    )(x, g, w)


def _rope(t, cos, sin, rot):
    return t * cos + _hdot(t, rot) * sin


def _attn_tile(qs, ks, vs, gs, sinks, gq, gk, cq, sq, ck, sk, mask, rot):
    outs = []
    for hk in range(AKV):
        kh = _rope(_rms(ks[hk], gk), ck, sk, rot)
        for g in range(AH // AKV):
            h = hk * (AH // AKV) + g
            qh = _rope(_rms(qs[h], gq), cq, sq, rot)
            sc = _mm_nt(qh, kh) * (HD ** -0.5)
            sc = jnp.where(mask, sc, NEG_INF)
            mx = lax.stop_gradient(jnp.maximum(jnp.max(sc, axis=-1, keepdims=True), sinks[h]))
            p = jnp.exp(sc - mx)
            den = jnp.sum(p, axis=-1, keepdims=True) + jnp.exp(sinks[h] - mx)
            o = _mm_nn(p / den, vs[hk])
            outs.append(o * _silu(gs[h]))
    return outs


def _attn_load(n, s, aq_ref, ak_ref, av_ref, ag_refs, cos_ref, sin_ref, sink_ref):
    start = pl.multiple_of(jnp.clip((n - 1) * QB, 0, s - WIN), QB)
    q0 = pl.multiple_of(n * QB, QB)
    qs = [aq_ref[:, h * HD:(h + 1) * HD] for h in range(AH)]
    ks = [ak_ref[pl.ds(start, WIN), h * HD:(h + 1) * HD] for h in range(AKV)]
    vs = [av_ref[pl.ds(start, WIN), h * HD:(h + 1) * HD] for h in range(AKV)]
    gs = [ag_refs[h // 4][:, (h % 4) * HD:(h % 4 + 1) * HD] for h in range(AH)]
    sinks = [sink_ref[0:1, h:h + 1] for h in range(AH)]
    cq, sq = cos_ref[pl.ds(q0, QB), :], sin_ref[pl.ds(q0, QB), :]
    ck, sk = cos_ref[pl.ds(start, WIN), :], sin_ref[pl.ds(start, WIN), :]
    qpos = q0 + lax.broadcasted_iota(jnp.int32, (QB, WIN), 0)
    kpos = start + lax.broadcasted_iota(jnp.int32, (QB, WIN), 1)
    mask = jnp.abs(kpos - qpos) <= QB
    return start, qs, ks, vs, gs, sinks, cq, sq, ck, sk, mask


def _attn_specs(s):
    return [
        _bs((QB, 768), lambda n: (n, 0)),
        _bs((s, 256), lambda n: (0, C_AK // 256)),
        _bs((s, 256), lambda n: (0, C_AV // 256)),
        _bs((QB, 256), lambda n: (n, C_AG // 256)),
        _bs((QB, 256), lambda n: (n, C_AG // 256 + 1)),
        _bs((QB, 256), lambda n: (n, C_AG // 256 + 2)),
        _bs((s, HD), lambda n: (0, 0)),
        _bs((s, HD), lambda n: (0, 0)),
        _bs((1, HD), lambda n: (0, 0)),
        _bs((1, HD), lambda n: (0, 0)),
        _bs((1, AH), lambda n: (0, 0)),
    ]


def _attn_fwd(proj, cos, sin, gq, gk, sink):
    s = proj.shape[0]

    def body(aq_ref, ak_ref, av_ref, ag0, ag1, ag2, cos_ref, sin_ref, gq_ref, gk_ref, sink_ref, o_ref):
        n = pl.program_id(0)
        _, qs, ks, vs, gs, sinks, cq, sq, ck, sk, mask = _attn_load(
            n, s, aq_ref, ak_ref, av_ref, (ag0, ag1, ag2), cos_ref, sin_ref, sink_ref)
        outs = _attn_tile(qs, ks, vs, gs, sinks, gq_ref[...], gk_ref[...], cq, sq, ck, sk, mask, _rot_matrix())
        for h in range(AH):
            o_ref[:, h * HD:(h + 1) * HD] = outs[h]

    return pl.pallas_call(
        body, name="attn_fwd", grid=(s // QB,),
        in_specs=_attn_specs(s), out_specs=_bs((QB, 768), lambda n: (n, 0)),
        out_shape=jax.ShapeDtypeStruct((s, 768), F32),
        compiler_params=_params(("arbitrary",)),
    )(proj, proj, proj, proj, proj, proj, cos, sin, gq, gk, sink)


def _attn_bwd(proj, cos, sin, gq, gk, sink, dy):
    s = proj.shape[0]

    def body(aq_ref, ak_ref, av_ref, ag0, ag1, ag2, cos_ref, sin_ref, gq_ref, gk_ref, sink_ref, dy_ref,
             daq_ref, dak_ref, dav_ref, dag_ref, dgq_ref, dgk_ref, dsink_ref):
        n = pl.program_id(0)

        @pl.when(n == 0)
        def _():
            dak_ref[...] = jnp.zeros_like(dak_ref)
            dav_ref[...] = jnp.zeros_like(dav_ref)
            dgq_ref[...] = jnp.zeros_like(dgq_ref)
            dgk_ref[...] = jnp.zeros_like(dgk_ref)
            dsink_ref[...] = jnp.zeros_like(dsink_ref)

        start, qs, ks, vs, gs, sinks, cq, sq, ck, sk, mask = _attn_load(
            n, s, aq_ref, ak_ref, av_ref, (ag0, ag1, ag2), cos_ref, sin_ref, sink_ref)
        rot = _rot_matrix()

        def f(qs, ks, vs, gs, sinks, gq, gk):
            return _attn_tile(qs, ks, vs, gs, sinks, gq, gk, cq, sq, ck, sk, mask, rot)

        _, vjp = jax.vjp(f, qs, ks, vs, gs, sinks, gq_ref[...], gk_ref[...])
        dys = [dy_ref[:, h * HD:(h + 1) * HD] for h in range(AH)]
        dqs, dks, dvs, dgs, dsinks, dgq, dgk = vjp(dys)
        for h in range(AH):
            daq_ref[:, h * HD:(h + 1) * HD] = dqs[h]
            dag_ref[:, h * HD:(h + 1) * HD] = dgs[h]
            dsink_ref[0:1, h:h + 1] += dsinks[h]
        for h in range(AKV):
            dak_ref[pl.ds(start, WIN), h * HD:(h + 1) * HD] += dks[h]
            dav_ref[pl.ds(start, WIN), h * HD:(h + 1) * HD] += dvs[h]
        dgq_ref[...] += dgq
        dgk_ref[...] += dgk

    whole = lambda shape: _bs(shape, lambda n: (0, 0))
    return pl.pallas_call(
        body, name="attn_bwd", grid=(s // QB,),
        in_specs=_attn_specs(s) + [_bs((QB, 768), lambda n: (n, 0))],
        out_specs=[_bs((QB, 768), lambda n: (n, 0)), whole((s, 256)), whole((s, 256)), _bs((QB, 768), lambda n: (n, 0)),
                   whole((1, HD)), whole((1, HD)), whole((1, AH))],
        out_shape=[jax.ShapeDtypeStruct((s, 768), F32), jax.ShapeDtypeStruct((s, 256), F32),
                   jax.ShapeDtypeStruct((s, 256), F32), jax.ShapeDtypeStruct((s, 768), F32),
                   jax.ShapeDtypeStruct((1, HD), F32), jax.ShapeDtypeStruct((1, HD), F32),
                   jax.ShapeDtypeStruct((1, AH), F32)],
        compiler_params=_params(("arbitrary",)),
    )(proj, proj, proj, proj, proj, proj, cos, sin, gq, gk, sink, dy)


def _mem_kv(mem, g, w):
    def body(m_ref, g_ref, w_ref, o_ref, mn_ref):
        mn = _rms(m_ref[...], g_ref[...]).astype(BF16)
        mn_ref[...] = mn
        o_ref[...] = jnp.dot(mn, w_ref[...], preferred_element_type=F32)

    return pl.pallas_call(
        body, name="mem_kv",
        out_shape=[jax.ShapeDtypeStruct((NMEM, 2 * XW), F32), jax.ShapeDtypeStruct((NMEM, D), BF16)],
        compiler_params=_params(),
    )(mem, g, w)


def _xattn_tile(qs, gs, kms, vms, gxq, gxk):
    outs = []
    for h in range(XH):
        q = _rms(qs[h], gxq)
        km = _rms(kms[h], gxk)
        sc = _mm_nt(q, km) * (XD ** -0.5)
        mx = lax.stop_gradient(jnp.max(sc, axis=-1, keepdims=True))
        p = jnp.exp(sc - mx)
        p = p / jnp.sum(p, axis=-1, keepdims=True)
        outs.append(_mm_nn(p, vms[h]) * _silu(gs[h]))
    return outs


XT = 256


def _xattn_specs():
    return [
        _bs((XT, 256), lambda i: (i, C_XQ // 256)), _bs((XT, 256), lambda i: (i, C_XQ // 256 + 1)),
        _bs((XT, 256), lambda i: (i, C_XG // 256)), _bs((XT, 256), lambda i: (i, C_XG // 256 + 1)),
        _bs((NMEM, 2 * XW), lambda i: (0, 0)),
        _bs((1, XD), lambda i: (0, 0)), _bs((1, XD), lambda i: (0, 0)),
    ]


def _xattn_load(q0, q1, g0, g1, mkv_ref):
    qs = [(q0, q1)[h // 2][:, (h % 2) * XD:(h % 2 + 1) * XD] for h in range(XH)]
    gs = [(g0, g1)[h // 2][:, (h % 2) * XD:(h % 2 + 1) * XD] for h in range(XH)]
    kms = [mkv_ref[:, h * XD:(h + 1) * XD] for h in range(XH)]
    vms = [mkv_ref[:, XW + h * XD:XW + (h + 1) * XD] for h in range(XH)]
    return qs, gs, kms, vms


def _xattn_fwd(proj, mkv, gxq, gxk):
    s = proj.shape[0]

    def body(q0, q1, g0, g1, mkv_ref, gxq_ref, gxk_ref, o_ref):
        qs, gs, kms, vms = _xattn_load(q0, q1, g0, g1, mkv_ref)
        outs = _xattn_tile(qs, gs, kms, vms, gxq_ref[...], gxk_ref[...])
        for h in range(XH):
            o_ref[:, h * XD:(h + 1) * XD] = outs[h]

    return pl.pallas_call(
        body, name="xattn_fwd", grid=(s // XT,),
        in_specs=_xattn_specs(), out_specs=_bs((XT, XW), lambda i: (i, 0)),
        out_shape=jax.ShapeDtypeStruct((s, XW), F32),
        compiler_params=_params(("arbitrary",)),
    )(proj, proj, proj, proj, mkv, gxq, gxk)


def _xattn_bwd(proj, mkv, gxq, gxk, dy):
    s = proj.shape[0]

    def body(q0, q1, g0, g1, mkv_ref, gxq_ref, gxk_ref, dy_ref, dq_ref, dg_ref, dmkv_ref, dgxq_ref, dgxk_ref):
        @pl.when(pl.program_id(0) == 0)
        def _():
            dmkv_ref[...] = jnp.zeros_like(dmkv_ref)
            dgxq_ref[...] = jnp.zeros_like(dgxq_ref)
            dgxk_ref[...] = jnp.zeros_like(dgxk_ref)

        qs, gs, kms, vms = _xattn_load(q0, q1, g0, g1, mkv_ref)
        _, vjp = jax.vjp(_xattn_tile, qs, gs, kms, vms, gxq_ref[...], gxk_ref[...])
        dqs, dgs, dkms, dvms, dgxq, dgxk = vjp([dy_ref[:, h * XD:(h + 1) * XD] for h in range(XH)])
        for h in range(XH):
            dq_ref[:, h * XD:(h + 1) * XD] = dqs[h]
            dg_ref[:, h * XD:(h + 1) * XD] = dgs[h]
            dmkv_ref[:, h * XD:(h + 1) * XD] += dkms[h]
            dmkv_ref[:, XW + h * XD:XW + (h + 1) * XD] += dvms[h]
        dgxq_ref[...] += dgxq
        dgxk_ref[...] += dgxk

    whole = lambda shape: _bs(shape, lambda i: (0, 0))
    return pl.pallas_call(
        body, name="xattn_bwd", grid=(s // XT,),
        in_specs=_xattn_specs() + [_bs((XT, XW), lambda i: (i, 0))],
        out_specs=[_bs((XT, XW), lambda i: (i, 0)), _bs((XT, XW), lambda i: (i, 0)), whole((NMEM, 2 * XW)),
                   whole((1, XD)), whole((1, XD))],
        out_shape=[jax.ShapeDtypeStruct((s, XW), F32), jax.ShapeDtypeStruct((s, XW), F32),
                   jax.ShapeDtypeStruct((NMEM, 2 * XW), F32), jax.ShapeDtypeStruct((1, XD), F32),
                   jax.ShapeDtypeStruct((1, XD), F32)],
        compiler_params=_params(("arbitrary",)),
    )(proj, proj, proj, proj, mkv, gxq, gxk, dy)


def _mem_bwd(mem, g, dmn):
    def body(m_ref, dmn_ref, o_ref):
        m = m_ref[...]
        r = lax.rsqrt(jnp.mean(m * m, axis=-1, keepdims=True) + NORM_EPS)
        o_ref[...] = jnp.sum(dmn_ref[...] * m * r, axis=0, keepdims=True)

    del g
    return pl.pallas_call(body, name="mem_norm_bwd", out_shape=jax.ShapeDtypeStruct((1, D), F32),
                          compiler_params=_params())(mem, dmn)


SHIFT_W = 512


def _shift_rows(p, s):
    row = lax.broadcasted_iota(jnp.int32, p.shape, 0)
    prev = jnp.where(row == 0, 0.0, pltpu.roll(p, 1, 0))
    nxt = jnp.where(row == s - 1, 0.0, pltpu.roll(p, s - 1, 0))
    return prev, nxt


def _shift_fwd(proj, mu):
    s = proj.shape[0]

    def body(p_ref, mu_ref, o_ref):
        p = p_ref[...]
        prev, nxt = _shift_rows(p, s)
        o_ref[...] = p + mu_ref[...] * (0.5 * (prev + nxt) - p)

    return pl.pallas_call(
        body, name="shift_fwd", grid=(RSW // SHIFT_W,),
        in_specs=[_bs((s, SHIFT_W), lambda j: (0, C_RS // SHIFT_W + j)), _bs((1, SHIFT_W), lambda j: (0, j))],
        out_specs=_bs((s, SHIFT_W), lambda j: (0, j)),
        out_shape=jax.ShapeDtypeStruct((s, RSW), F32),
        compiler_params=_params(("parallel",)),
    )(proj, mu)


def _shift_bwd(proj, mu, dps):
    s = proj.shape[0]

    def body(p_ref, mu_ref, g_ref, o_ref, dmu_ref):
        p, g, mu_v = p_ref[...], g_ref[...], mu_ref[...]
        prev, nxt = _shift_rows(p, s)
        dmu_ref[...] = jnp.sum(g * (0.5 * (prev + nxt) - p), axis=0, keepdims=True)
        mg = mu_v * g
        down, up = _shift_rows(mg, s)
        o_ref[...] = g * (1.0 - mu_v) + 0.5 * (down + up)

    return pl.pallas_call(
        body, name="shift_bwd", grid=(RSW // SHIFT_W,),
        in_specs=[_bs((s, SHIFT_W), lambda j: (0, C_RS // SHIFT_W + j)), _bs((1, SHIFT_W), lambda j: (0, j)),
                  _bs((s, SHIFT_W), lambda j: (0, j))],
        out_specs=[_bs((s, SHIFT_W), lambda j: (0, j)), _bs((1, SHIFT_W), lambda j: (0, j))],
        out_shape=[jax.ShapeDtypeStruct((s, RSW), F32), jax.ShapeDtypeStruct((1, RSW), F32)],
        compiler_params=_params(("parallel",)),
    )(proj, mu, dps)


def _pre_tile(k, wf, wb, af, ab, k_k, k_a, w0s, w2s, a0s, a2s, seg):
    kx = k * k_k
    ss = _hdot(kx * kx, seg)
    kk = kx / jnp.maximum(jnp.sqrt(ss), 1e-12)
    outs = [kk]
    for d, (w_in, a_in) in enumerate(((wf, af), (wb, ab))):
        z = w0s[d] + _mm_nn(jnp.tanh(w_in), w2s[d])
        wd = -_softplus(-z) - 0.5
        dec = jnp.exp(-jnp.exp(wd))
        ad = jax.nn.sigmoid(a0s[d] + _mm_nn(a_in, a2s[d]))
        kd = k * (1.0 + (ad - 1.0) * k_a)
        outs += [dec, kd, kk * ad]
    return outs


PT = 256


def _pre_load(ps_ref, kk_ref, ka_ref, w0_ref, w2_ref, a0_ref, a2_ref):
    k = ps_ref[:, RW:2 * RW]
    wf, wb = ps_ref[:, 3 * RW:3 * RW + 64], ps_ref[:, 3 * RW + 64:3 * RW + 128]
    af, ab = ps_ref[:, 3 * RW + 128:3 * RW + 192], ps_ref[:, 3 * RW + 192:3 * RW + 256]
    w0s = [w0_ref[0:1, :], w0_ref[1:2, :]]
    a0s = [a0_ref[0:1, :], a0_ref[1:2, :]]
    w2s = [w2_ref[0], w2_ref[1]]
    a2s = [a2_ref[0], a2_ref[1]]
    return (k, wf, wb, af, ab, kk_ref[...], ka_ref[...], w0s, w2s, a0s, a2s)


def _pre_specs():
    c = lambda shape: _bs(shape, lambda i: tuple(0 for _ in shape))
    return [_bs((PT, RSW), lambda i: (i, 0)), c((1, RW)), c((1, RW)), c((2, RW)), c((2, 64, RW)), c((2, RW)),
            c((2, 64, RW))]


def _pre_fwd(ps, k_k, k_a, w0, w2, a0, a2):
    s = ps.shape[0]

    def body(ps_ref, kk_ref, ka_ref, w0_ref, w2_ref, a0_ref, a2_ref, *outs):
        args = _pre_load(ps_ref, kk_ref, ka_ref, w0_ref, w2_ref, a0_ref, a2_ref)
        res = _pre_tile(*args, _seg_matrix(RW, HD))
        for o_ref, v in zip(outs, res):
            o_ref[...] = v

    return pl.pallas_call(
        body, name="rwkv_pre_fwd", grid=(s // PT,),
        in_specs=_pre_specs(), out_specs=[_bs((PT, RW), lambda i: (i, 0))] * 7,
        out_shape=[jax.ShapeDtypeStruct((s, RW), F32)] * 7,
        compiler_params=_params(("parallel",)),
    )(ps, k_k, k_a, w0, w2, a0, a2)


def _pre_bwd(ps, k_k, k_a, w0, w2, a0, a2, dr, dv, cts):
    s = ps.shape[0]

    def body(ps_ref, kk_ref, ka_ref, w0_ref, w2_ref, a0_ref, a2_ref, dr_ref, dv_ref, c0, c1, c2, c3, c4, c5, c6,
             dps_ref, dkk_ref, dka_ref, dw0_ref, dw2_ref, da0_ref, da2_ref):
        @pl.when(pl.program_id(0) == 0)
        def _():
            for r in (dkk_ref, dka_ref, dw0_ref, dw2_ref, da0_ref, da2_ref):
                r[...] = jnp.zeros_like(r)

        args = _pre_load(ps_ref, kk_ref, ka_ref, w0_ref, w2_ref, a0_ref, a2_ref)
        seg = _seg_matrix(RW, HD)
        _, vjp = jax.vjp(lambda *a: _pre_tile(*a, seg), *args)
        dk, dwf, dwb, daf, dab, dk_k, dk_a, dw0s, dw2s, da0s, da2s = vjp([c[...] for c in (c0, c1, c2, c3, c4, c5, c6)])
        dps_ref[:, 0:RW] = dr_ref[...]
        dps_ref[:, RW:2 * RW] = dk
        dps_ref[:, 2 * RW:3 * RW] = dv_ref[...]
        for j, t in enumerate((dwf, dwb, daf, dab)):
            dps_ref[:, 3 * RW + 64 * j:3 * RW + 64 * (j + 1)] = t
        dkk_ref[...] += dk_k
        dka_ref[...] += dk_a
        for d in range(2):
            dw0_ref[d:d + 1, :] += dw0s[d]
            da0_ref[d:d + 1, :] += da0s[d]
            dw2_ref[d] += dw2s[d]
            da2_ref[d] += da2s[d]

    c = lambda shape: _bs(shape, lambda i: tuple(0 for _ in shape))
    row = _bs((PT, RW), lambda i: (i, 0))
    return pl.pallas_call(
        body, name="rwkv_pre_bwd", grid=(s // PT,),
        in_specs=_pre_specs() + [row] * 9,
        out_specs=[_bs((PT, RSW), lambda i: (i, 0)), c((1, RW)), c((1, RW)), c((2, RW)), c((2, 64, RW)), c((2, RW)),
                   c((2, 64, RW))],
        out_shape=[jax.ShapeDtypeStruct((s, RSW), F32), jax.ShapeDtypeStruct((1, RW), F32),
                   jax.ShapeDtypeStruct((1, RW), F32), jax.ShapeDtypeStruct((2, RW), F32),
                   jax.ShapeDtypeStruct((2, 64, RW), F32), jax.ShapeDtypeStruct((2, RW), F32),
                   jax.ShapeDtypeStruct((2, 64, RW), F32)],
        compiler_params=_params(("arbitrary",)),
    )(ps, k_k, k_a, w0, w2, a0, a2, dr, dv, *cts)


def _post_tile(y0, y1, r, v, kd0, kd1, rg, r_k, ln_w, ln_b, seg):
    ysum = y0 + y1
    bonus = (_hdot(r * kd0 * r_k, seg) + _hdot(r * kd1 * r_k, seg)) * v
    mean = _hdot(ysum, seg) * (1.0 / HD)
    cen = ysum - mean
    var = _hdot(cen * cen, seg) * (1.0 / HD)
    y = cen * lax.rsqrt(var + GN_EPS) * ln_w + ln_b + bonus
    return y * _silu(rg)


def _post_specs():
    row = _bs((PT, RW), lambda i: (i, 0))
    c = _bs((1, RW), lambda i: (0, 0))
    return [row, row, _bs((PT, RW), lambda i: (i, 0)), _bs((PT, RW), lambda i: (i, 2)), row, row,
            _bs((PT, RW), lambda i: (i, C_RG // RW)), c, c, c]


def _post_fwd(y0, y1, ps, kd0, kd1, proj, r_k, ln_w, ln_b):
    s = ps.shape[0]

    def body(y0_ref, y1_ref, r_ref, v_ref, kd0_ref, kd1_ref, rg_ref, rk_ref, lw_ref, lb_ref, o_ref):
        o_ref[...] = _post_tile(y0_ref[...], y1_ref[...], r_ref[...], v_ref[...], kd0_ref[...], kd1_ref[...],
                                rg_ref[...], rk_ref[...], lw_ref[...], lb_ref[...], _seg_matrix(RW, HD))

    return pl.pallas_call(
        body, name="rwkv_post_fwd", grid=(s // PT,),
        in_specs=_post_specs(), out_specs=_bs((PT, RW), lambda i: (i, 0)),
        out_shape=jax.ShapeDtypeStruct((s, RW), F32),
        compiler_params=_params(("parallel",)),
    )(y0, y1, ps, ps, kd0, kd1, proj, r_k, ln_w, ln_b)


def _post_bwd(y0, y1, ps, kd0, kd1, proj, r_k, ln_w, ln_b, dy):
    s = ps.shape[0]

    def body(y0_ref, y1_ref, r_ref, v_ref, kd0_ref, kd1_ref, rg_ref, rk_ref, lw_ref, lb_ref, dy_ref,
             dys_ref, dr_ref, dv_ref, dkd0_ref, dkd1_ref, drg_ref, drk_ref, dlw_ref, dlb_ref):
        @pl.when(pl.program_id(0) == 0)
        def _():
            for r in (drk_ref, dlw_ref, dlb_ref):
                r[...] = jnp.zeros_like(r)

        seg = _seg_matrix(RW, HD)
        args = [t[...] for t in (y0_ref, y1_ref, r_ref, v_ref, kd0_ref, kd1_ref, rg_ref, rk_ref, lw_ref, lb_ref)]
        _, vjp = jax.vjp(lambda *a: _post_tile(*a, seg), *args)
        dy0, _, dr, dv, dkd0, dkd1, drg, drk, dlw, dlb = vjp(dy_ref[...])
        dys_ref[...] = dy0
        dr_ref[...] = dr
        dv_ref[...] = dv
        dkd0_ref[...] = dkd0
        dkd1_ref[...] = dkd1
        drg_ref[...] = drg
        drk_ref[...] += drk
        dlw_ref[...] += dlw
        dlb_ref[...] += dlb

    row = _bs((PT, RW), lambda i: (i, 0))
    c = _bs((1, RW), lambda i: (0, 0))
    return pl.pallas_call(
        body, name="rwkv_post_bwd", grid=(s // PT,),
        in_specs=_post_specs() + [row], out_specs=[row] * 6 + [c] * 3,
        out_shape=[jax.ShapeDtypeStruct((s, RW), F32)] * 6 + [jax.ShapeDtypeStruct((1, RW), F32)] * 3,
        compiler_params=_params(("arbitrary",)),
    )(y0, y1, ps, ps, kd0, kd1, proj, r_k, ln_w, ln_b, dy)


def _ones2():
    r = lax.broadcasted_iota(jnp.int32, (256, 128), 0) % 128 // HD
    c = lax.broadcasted_iota(jnp.int32, (256, 128), 1) // HD
    return (r == c).astype(BF16)


def _split(p):
    hi = p.astype(BF16)
    lo = (p - hi.astype(F32)).astype(BF16)
    return jnp.concatenate([hi, lo], axis=1)


def _to_t8(a):
    s = a.shape[0]
    t = a.reshape(s // 8, 8, NPAIR, 2, HD).transpose(0, 2, 4, 3, 1)
    t = jnp.pad(t, ((0, 0), (0, 0), (0, 0), (0, 0), (0, HD - 8))).reshape(s // 8, NPAIR, HD, 128)
    hi = t.astype(BF16)
    lo = (t - hi.astype(F32)).astype(BF16)
    return jnp.concatenate([hi, lo], axis=-1)


def _from_t8(t8):
    g = t8.shape[0]
    t = t8.reshape(g, NPAIR, HD, 2, HD)[..., :8]
    return t.transpose(0, 4, 1, 3, 2).reshape(g * 8, RW)


def _scan_specs(direction, nc, fwd_order):
    def tb(c):
        sc = c if fwd_order else nc - 1 - c
        return sc if direction == 0 else nc - 1 - sc

    row = _bs((TC, RW), lambda c: (tb(c), 0))
    t8_in = _bs((TC // 8, NPAIR, HD, 256), lambda c: (tb(c), 0, 0, 0))
    t8_out = _bs((TC // 8, NPAIR, HD, 128), lambda c: (tb(c), 0, 0, 0))
    return row, t8_in, t8_out


def _put_t8(ref, g, u, tiles):
    for p in range(NPAIR):
        ref[g, p, :, u:u + 1] = tiles[p][:, u:u + 1]
        ref[g, p, :, HD + u:HD + u + 1] = tiles[p][:, HD + u:HD + u + 1]


def _scan_fwd(dec, kd, b, ps, kk, vl, direction):
    s = dec.shape[0]
    nc, ng = s // TC, TC // 8
    row, t8_in, t8_out = _scan_specs(direction, nc, True)
    n = NPAIR * HD

    def body(dec_ref, kd_ref, b_ref, r_ref, kk_ref, vl_ref, y8_ref, ck_ref, st):
        @pl.when(pl.program_id(0) == 0)
        def _():
            st[...] = jnp.zeros_like(st)

        ck_ref[0] = st[...]
        ones2 = _ones2()
        lane_u = lax.broadcasted_iota(jnp.int32, (HD, 256), 1) % HD
        tiles = lambda res, k: [res[k * n + p * HD:k * n + (p + 1) * HD] for p in range(NPAIR)]

        def group(gi, carry):
            g = gi if direction == 0 else ng - 1 - gi
            rows8 = pl.ds(pl.multiple_of(g * 8, 8), 8)
            d8, k8, b8, r8, kk8 = (q[rows8, :] for q in (dec_ref, kd_ref, b_ref, r_ref, kk_ref))
            pc = [slice(p * 128, (p + 1) * 128) for p in range(NPAIR)]
            ss = [st[p] for p in range(NPAIR)]
            u_prev = None
            for ui in range(8):
                u = ui if direction == 0 else 7 - ui
                lhs = [_split(ss[p] * kk8[u:u + 1, pc[p]]) for p in range(NPAIR)]
                for p in range(NPAIR):
                    vt = vl_ref[g, p]
                    lhs.append(jnp.where(lane_u == u, vt, jnp.zeros_like(vt)))
                if u_prev is not None:
                    lhs += [_split(ss[p] * r8[u_prev:u_prev + 1, pc[p]]) for p in range(NPAIR)]
                res = jnp.dot(jnp.concatenate(lhs, axis=0), ones2, preferred_element_type=F32)
                if u_prev is not None:
                    _put_t8(y8_ref, g, u_prev, tiles(res, 2))
                sa, vb = tiles(res, 0), tiles(res, 1)
                for p in range(NPAIR):
                    ss[p] = ss[p] * d8[u:u + 1, pc[p]] - sa[p] * b8[u:u + 1, pc[p]] + vb[p] * k8[u:u + 1, pc[p]]
                u_prev = u
            lhs = [_split(ss[p] * r8[u_prev:u_prev + 1, pc[p]]) for p in range(NPAIR)]
            res = jnp.dot(jnp.concatenate(lhs, axis=0), ones2, preferred_element_type=F32)
            _put_t8(y8_ref, g, u_prev, tiles(res, 0))
            for p in range(NPAIR):
                st[p] = ss[p]
            return carry

        lax.fori_loop(0, ng, group, 0)

    return pl.pallas_call(
        body, name=f"rwkv_scan_fwd{direction}", grid=(nc,),
        in_specs=[row, row, row, row, row, t8_in],
        out_specs=[t8_out, _bs((1, NPAIR, HD, 128), lambda c: (c, 0, 0, 0))],
        out_shape=[jax.ShapeDtypeStruct((s // 8, NPAIR, HD, 128), F32),
                   jax.ShapeDtypeStruct((nc, NPAIR, HD, 128), F32)],
        scratch_shapes=[pltpu.VMEM((NPAIR, HD, 128), F32)],
        compiler_params=_params(("arbitrary",)),
    )(dec, kd, b, ps, kk, vl)


def _scan_bwd(dec, kd, b, ps, kk, vl, dyl, ck, direction):
    s = dec.shape[0]
    nc, ng = s // TC, TC // 8
    row, t8_in, t8_out = _scan_specs(direction, nc, False)
    n = NPAIR * HD

    def body(dec_ref, kd_ref, b_ref, r_ref, kk_ref, vl_ref, dyl_ref, ck_ref,
             dr_ref, dd_ref, db_ref, dk_ref, dkk_ref, dv8_ref, st, sa_s, vb_s, dy_s, ds):
        @pl.when(pl.program_id(0) == 0)
        def _():
            ds[...] = jnp.zeros_like(ds)

        st[0] = ck_ref[0]
        ones2 = _ones2()
        lane_u = lax.broadcasted_iota(jnp.int32, (HD, 256), 1) % HD
        row_id = lax.broadcasted_iota(jnp.int32, (8, 128), 0)
        pc = [slice(p * 128, (p + 1) * 128) for p in range(NPAIR)]
        tiles = lambda res, k: [res[k * n + p * HD:k * n + (p + 1) * HD] for p in range(NPAIR)]

        def fgroup(gi, carry):
            g = gi if direction == 0 else ng - 1 - gi
            rows8 = pl.ds(pl.multiple_of(g * 8, 8), 8)
            d8, k8, b8, kk8 = (q[rows8, :] for q in (dec_ref, kd_ref, b_ref, kk_ref))
            ss = [st[gi * 8, p] for p in range(NPAIR)]
            for ui in range(8):
                u = ui if direction == 0 else 7 - ui
                i = gi * 8 + ui
                lhs = [_split(ss[p] * kk8[u:u + 1, pc[p]]) for p in range(NPAIR)]
                for ref in (vl_ref, dyl_ref):
                    for p in range(NPAIR):
                        t = ref[g, p]
                        lhs.append(jnp.where(lane_u == u, t, jnp.zeros_like(t)))
                res = jnp.dot(jnp.concatenate(lhs, axis=0), ones2, preferred_element_type=F32)
                sa, vb, dyb = tiles(res, 0), tiles(res, 1), tiles(res, 2)
                for p in range(NPAIR):
                    sa_s[i, p] = sa[p]
                    vb_s[i, p] = vb[p]
                    dy_s[i, p] = dyb[p]
                    ss[p] = ss[p] * d8[u:u + 1, pc[p]] - sa[p] * b8[u:u + 1, pc[p]] + vb[p] * k8[u:u + 1, pc[p]]
                    st[i + 1, p] = ss[p]
            return carry

        lax.fori_loop(0, ng, fgroup, 0)

        def bgroup(gj, carry):
            gi = ng - 1 - gj
            g = gi if direction == 0 else ng - 1 - gi
            rows8 = pl.ds(pl.multiple_of(g * 8, 8), 8)
            d8, k8, b8, r8, kk8 = (q[rows8, :] for q in (dec_ref, kd_ref, b_ref, r_ref, kk_ref))
            dss = [ds[p] for p in range(NPAIR)]
            acc = [[jnp.zeros((8, 128), F32) for _ in range(5)] for _ in range(NPAIR)]
            for uj in range(8):
                ui = 7 - uj
                u = ui if direction == 0 else 7 - ui
                i = gi * 8 + ui
                dyb = [dy_s[i, p] for p in range(NPAIR)]
                for p in range(NPAIR):
                    dss[p] = dss[p] + dyb[p] * r8[u:u + 1, pc[p]]
                lhs = [_split(dss[p] * b8[u:u + 1, pc[p]]) for p in range(NPAIR)]
                lhs += [_split(dss[p] * k8[u:u + 1, pc[p]]) for p in range(NPAIR)]
                res = jnp.dot(jnp.concatenate(lhs, axis=0), ones2, preferred_element_type=F32)
                dsa, dvb = tiles(res, 0), tiles(res, 1)
                _put_t8(dv8_ref, g, u, dvb)
                for p in range(NPAIR):
                    sp, sn = st[i, p], st[i + 1, p]
                    outs = (jnp.sum(sn * dyb[p], axis=0, keepdims=True), jnp.sum(dss[p] * sp, axis=0, keepdims=True),
                            -jnp.sum(dss[p] * sa_s[i, p], axis=0, keepdims=True),
                            jnp.sum(dss[p] * vb_s[i, p], axis=0, keepdims=True),
                            -jnp.sum(sp * dsa[p], axis=0, keepdims=True))
                    acc[p] = [jnp.where(row_id == u, o, a_) for o, a_ in zip(outs, acc[p])]
                    dss[p] = dss[p] * d8[u:u + 1, pc[p]] - dsa[p] * kk8[u:u + 1, pc[p]]
            for p in range(NPAIR):
                ds[p] = dss[p]
                for o_ref, a_ in zip((dr_ref, dd_ref, db_ref, dk_ref, dkk_ref), acc[p]):
                    o_ref[rows8, pc[p]] = a_
            return carry

        lax.fori_loop(0, ng, bgroup, 0)

    chunk = lambda k: pltpu.VMEM((k, NPAIR, HD, 128), F32)
    return pl.pallas_call(
        body, name=f"rwkv_scan_bwd{direction}", grid=(nc,),
        in_specs=[row, row, row, row, row, t8_in, t8_in, _bs((1, NPAIR, HD, 128), lambda c: (nc - 1 - c, 0, 0, 0))],
        out_specs=[row] * 5 + [t8_out],
        out_shape=[jax.ShapeDtypeStruct((s, RW), F32)] * 5 + [jax.ShapeDtypeStruct((s // 8, NPAIR, HD, 128), F32)],
        scratch_shapes=[chunk(TC + 1), chunk(TC), chunk(TC), chunk(TC), pltpu.VMEM((NPAIR, HD, 128), F32)],
        compiler_params=_params(("arbitrary",)),
    )(dec, kd, b, ps, kk, vl, dyl, ck)


def _tiles(res, k):
    n = NPAIR * HD
    return [res[k * n + p * HD:k * n + (p + 1) * HD] for p in range(NPAIR)]


def _scan2_fwd(per_dir, ps, kk, vl):
    s = ps.shape[0]
    nc, ng = s // TC, TC // 8
    in_specs, operands, out_specs, out_shape = [], [], [], []
    for d in (0, 1):
        row, t8_in, t8_out = _scan_specs(d, nc, True)
        in_specs += [row] * 5 + [t8_in]
        operands += list(per_dir[d]) + [ps, kk, vl]
        out_specs += [t8_out, _bs((1, NPAIR, HD, 128), lambda c: (c, 0, 0, 0))]
        out_shape += [jax.ShapeDtypeStruct((s // 8, NPAIR, HD, 128), F32),
                      jax.ShapeDtypeStruct((nc, NPAIR, HD, 128), F32)]

    def body(*refs):
        ins = [refs[0:6], refs[6:12]]
        y_refs, ck_refs, st = (refs[12], refs[14]), (refs[13], refs[15]), refs[16]

        @pl.when(pl.program_id(0) == 0)
        def _():
            st[...] = jnp.zeros_like(st)

        for d in (0, 1):
            ck_refs[d][0] = st[d * NPAIR:(d + 1) * NPAIR]
        ones2 = _ones2()
        lane_u = lax.broadcasted_iota(jnp.int32, (HD, 256), 1) % HD
        pc = [slice(p * 128, (p + 1) * 128) for p in range(NPAIR)]

        def group(gi, carry):
            gs = (gi, ng - 1 - gi)
            blk = [[q[pl.ds(pl.multiple_of(gs[d] * 8, 8), 8), :] for q in ins[d][:5]] for d in (0, 1)]
            ss = [[st[d * NPAIR + p] for p in range(NPAIR)] for d in (0, 1)]
            for ui in range(9):
                us, ups = (ui, 7 - ui), (ui - 1, 8 - ui)
                lhs = []
                for d in (0, 1):
                    _, _, _, r8, kk8 = blk[d]
                    if ui < 8:
                        lhs += [_split(ss[d][p] * kk8[us[d]:us[d] + 1, pc[p]]) for p in range(NPAIR)]
                        for p in range(NPAIR):
                            vt = ins[d][5][gs[d], p]
                            lhs.append(jnp.where(lane_u == us[d], vt, jnp.zeros_like(vt)))
                    if ui > 0:
                        lhs += [_split(ss[d][p] * r8[ups[d]:ups[d] + 1, pc[p]]) for p in range(NPAIR)]
                res = jnp.dot(jnp.concatenate(lhs, axis=0), ones2, preferred_element_type=F32)
                k = 0
                for d in (0, 1):
                    d8, k8, b8, _, _ = blk[d]
                    u = us[d]
                    if ui < 8:
                        sa, vb = _tiles(res, k), _tiles(res, k + 1)
                        k += 2
                    if ui > 0:
                        _put_t8(y_refs[d], gs[d], ups[d], _tiles(res, k))
                        k += 1
                    if ui < 8:
                        for p in range(NPAIR):
                            ss[d][p] = (ss[d][p] * d8[u:u + 1, pc[p]] - sa[p] * b8[u:u + 1, pc[p]]
                                        + vb[p] * k8[u:u + 1, pc[p]])
            for d in (0, 1):
                for p in range(NPAIR):
                    st[d * NPAIR + p] = ss[d][p]
            return carry

        lax.fori_loop(0, ng, group, 0)

    outs = pl.pallas_call(
        body, name="rwkv_scan_fwd", grid=(nc,), in_specs=in_specs, out_specs=out_specs, out_shape=out_shape,
        scratch_shapes=[pltpu.VMEM((2 * NPAIR, HD, 128), F32)],
        compiler_params=_params(("arbitrary",)),
    )(*operands)
    return [(outs[0], outs[1]), (outs[2], outs[3])]


def _scan2_bwd(per_dir, ps, kk, vl, dyl):
    s = ps.shape[0]
    nc, ng = s // TC, TC // 8
    in_specs, operands, out_specs, out_shape = [], [], [], []
    for d in (0, 1):
        row, t8_in, t8_out = _scan_specs(d, nc, False)
        dec, kd, b, ck = per_dir[d]
        in_specs += [row] * 5 + [t8_in, t8_in, _bs((1, NPAIR, HD, 128), lambda c: (nc - 1 - c, 0, 0, 0))]
        operands += [dec, kd, b, ps, kk, vl, dyl, ck]
        out_specs += [row] * 5 + [t8_out]
        out_shape += [jax.ShapeDtypeStruct((s, RW), F32)] * 5 + [jax.ShapeDtypeStruct((s // 8, NPAIR, HD, 128), F32)]

    def body(*refs):
        ins = [refs[0:8], refs[8:16]]
        outs = [refs[16:22], refs[22:28]]
        st, sa_s, vb_s, dy_s, ds = refs[28:]

        @pl.when(pl.program_id(0) == 0)
        def _():
            ds[...] = jnp.zeros_like(ds)

        for d in (0, 1):
            st[d * (TC + 1)] = ins[d][7][0]
        ones2 = _ones2()
        lane_u = lax.broadcasted_iota(jnp.int32, (HD, 256), 1) % HD
        row_id = lax.broadcasted_iota(jnp.int32, (8, 128), 0)
        pc = [slice(p * 128, (p + 1) * 128) for p in range(NPAIR)]

        def load_rows(gs):
            return [[q[pl.ds(pl.multiple_of(gs[d] * 8, 8), 8), :] for q in ins[d][:5]] for d in (0, 1)]

        def fgroup(gi, carry):
            gs = (gi, ng - 1 - gi)
            blk = load_rows(gs)
            ss = [[st[d * (TC + 1) + gi * 8, p] for p in range(NPAIR)] for d in (0, 1)]
            for ui in range(8):
                us = (ui, 7 - ui)
                i = gi * 8 + ui
                lhs = []
                for d in (0, 1):
                    kk8 = blk[d][4]
                    lhs += [_split(ss[d][p] * kk8[us[d]:us[d] + 1, pc[p]]) for p in range(NPAIR)]
                    for ref in (ins[d][5], ins[d][6]):
                        for p in range(NPAIR):
                            t = ref[gs[d], p]
                            lhs.append(jnp.where(lane_u == us[d], t, jnp.zeros_like(t)))
                res = jnp.dot(jnp.concatenate(lhs, axis=0), ones2, preferred_element_type=F32)
                for d in (0, 1):
                    d8, k8, b8, _, _ = blk[d]
                    u = us[d]
                    sa, vb, dyb = _tiles(res, 3 * d), _tiles(res, 3 * d + 1), _tiles(res, 3 * d + 2)
                    for p in range(NPAIR):
                        sa_s[d * TC + i, p] = sa[p]
                        vb_s[d * TC + i, p] = vb[p]
                        dy_s[d * TC + i, p] = dyb[p]
                        ss[d][p] = ss[d][p] * d8[u:u + 1, pc[p]] - sa[p] * b8[u:u + 1, pc[p]] + vb[p] * k8[u:u + 1, pc[p]]
                        st[d * (TC + 1) + i + 1, p] = ss[d][p]
            return carry

        lax.fori_loop(0, ng, fgroup, 0)

        def bgroup(gj, carry):
            gi = ng - 1 - gj
            gs = (gi, ng - 1 - gi)
            blk = load_rows(gs)
            dss = [[ds[d * NPAIR + p] for p in range(NPAIR)] for d in (0, 1)]
            acc = [[[jnp.zeros((8, 128), F32) for _ in range(5)] for _ in range(NPAIR)] for _ in (0, 1)]
            for uj in range(8):
                ui = 7 - uj
                us = (ui, 7 - ui)
                i = gi * 8 + ui
                lhs_b, lhs_k, dyb = [], [], [None, None]
                for d in (0, 1):
                    _, k8, b8, r8, _ = blk[d]
                    u = us[d]
                    dyb[d] = [dy_s[d * TC + i, p] for p in range(NPAIR)]
                    for p in range(NPAIR):
                        dss[d][p] = dss[d][p] + dyb[d][p] * r8[u:u + 1, pc[p]]
                    lhs_b += [_split(dss[d][p] * b8[u:u + 1, pc[p]]) for p in range(NPAIR)]
                    lhs_b += [_split(dss[d][p] * k8[u:u + 1, pc[p]]) for p in range(NPAIR)]
                res = jnp.dot(jnp.concatenate(lhs_b + lhs_k, axis=0), ones2, preferred_element_type=F32)
                for d in (0, 1):
                    d8, _, _, _, kk8 = blk[d]
                    u = us[d]
                    dsa, dvb = _tiles(res, 2 * d), _tiles(res, 2 * d + 1)
                    _put_t8(outs[d][5], gs[d], u, dvb)
                    for p in range(NPAIR):
                        sp, sn = st[d * (TC + 1) + i, p], st[d * (TC + 1) + i + 1, p]
                        dsv = dss[d][p]
                        vals = (jnp.sum(sn * dyb[d][p], axis=0, keepdims=True), jnp.sum(dsv * sp, axis=0, keepdims=True),
                                -jnp.sum(dsv * sa_s[d * TC + i, p], axis=0, keepdims=True),
                                jnp.sum(dsv * vb_s[d * TC + i, p], axis=0, keepdims=True),
                                -jnp.sum(sp * dsa[p], axis=0, keepdims=True))
                        acc[d][p] = [jnp.where(row_id == u, o, a_) for o, a_ in zip(vals, acc[d][p])]
                        dss[d][p] = dsv * d8[u:u + 1, pc[p]] - dsa[p] * kk8[u:u + 1, pc[p]]
            for d in (0, 1):
                rows8 = pl.ds(pl.multiple_of(gs[d] * 8, 8), 8)
                for p in range(NPAIR):
                    ds[d * NPAIR + p] = dss[d][p]
                    for o_ref, a_ in zip(outs[d][:5], acc[d][p]):
                        o_ref[rows8, pc[p]] = a_
            return carry

        lax.fori_loop(0, ng, bgroup, 0)

    chunk = lambda k: pltpu.VMEM((k, NPAIR, HD, 128), F32)
    res = pl.pallas_call(
        body, name="rwkv_scan_bwd", grid=(nc,), in_specs=in_specs, out_specs=out_specs, out_shape=out_shape,
        scratch_shapes=[chunk(2 * (TC + 1)), chunk(2 * TC), chunk(2 * TC), chunk(2 * TC),
                        pltpu.VMEM((2 * NPAIR, HD, 128), F32)],
        compiler_params=_params(("arbitrary",)),
    )(*operands)
    return [res[0:6], res[6:12]]


MT = 256
MN = 256


def _merge_fwd(ya, yr, yx, wa, wr, wx, proj, gate_b):
    s = ya.shape[0]

    def body(ya_ref, yr_ref, yx_ref, wa_ref, wr_ref, wx_ref, m0, m1, m2, b0, b1, b2, o_ref):
        acc = jnp.zeros((MT, MN), F32)
        for y_ref, w_ref, m_ref, b_ref in ((ya_ref, wa_ref, m0, b0), (yr_ref, wr_ref, m1, b1), (yx_ref, wx_ref, m2, b2)):
            u = _dot(y_ref[...], w_ref[...], ((1,), (0,)))
            acc = acc + jax.nn.sigmoid(m_ref[...] + b_ref[...]) * u
        o_ref[...] = acc.astype(BF16)

    mg = lambda br: _bs((MT, MN), lambda i, j: (i, C_MG // MN + br * (D // MN) + j))
    gb = lambda br: _bs((1, MN), lambda i, j: (0, br * (D // MN) + j))
    return pl.pallas_call(
        body, name="merge_fwd", grid=(s // MT, D // MN),
        in_specs=[_bs((MT, RW), lambda i, j: (i, 0)), _bs((MT, RW), lambda i, j: (i, 0)), _bs((MT, XW), lambda i, j: (i, 0)),
                  _bs((RW, MN), lambda i, j: (0, j)), _bs((RW, MN), lambda i, j: (0, j)), _bs((XW, MN), lambda i, j: (0, j)),
                  mg(0), mg(1), mg(2), gb(0), gb(1), gb(2)],
        out_specs=_bs((MT, MN), lambda i, j: (i, j)),
        out_shape=jax.ShapeDtypeStruct((s, D), BF16),
        compiler_params=_params(("parallel", "arbitrary")),
    )(ya, yr, yx, wa, wr, wx, proj, proj, proj, gate_b, gate_b, gate_b)


def _out_fwd(merged, w_out, x, target):
    s = x.shape[0]
    tm, tn = min(512, s), 512

    def body(m_ref, w_ref, x_ref, t_ref, loss_ref, d_ref):
        @pl.when((pl.program_id(0) == 0) & (pl.program_id(1) == 0))
        def _():
            loss_ref[...] = jnp.zeros_like(loss_ref)

        out = x_ref[...] + jnp.dot(m_ref[...], w_ref[...], preferred_element_type=F32)
        err = out - t_ref[...]
        d_ref[...] = err * (1.0 / D)
        loss_ref[...] += jnp.sum(err * err)

    return pl.pallas_call(
        body, name="out_fwd", grid=(s // tm, D // tn),
        in_specs=[_bs((tm, D), lambda i, j: (i, 0)), _bs((D, tn), lambda i, j: (0, j)),
                  _bs((tm, tn), lambda i, j: (i, j)), _bs((tm, tn), lambda i, j: (i, j))],
        out_specs=[_bs((8, 128), lambda i, j: (0, 0)), _bs((tm, tn), lambda i, j: (i, j))],
        out_shape=[jax.ShapeDtypeStruct((8, 128), F32), jax.ShapeDtypeStruct((s, D), F32)],
        compiler_params=_params(("arbitrary", "arbitrary")),
    )(merged, w_out, x, target)


def _merge_bwd(ya, yr, yx, wa, wr, wx, proj, gate_b, dmerged):
    s = ya.shape[0]

    def body(ya_ref, yr_ref, yx_ref, wa_ref, wr_ref, wx_ref, m0, m1, m2, b0, b1, b2, dm_ref,
             dg0, dg1, dg2, du0, du1, du2, dya_ref, dyr_ref, dyx_ref):
        @pl.when(pl.program_id(1) == 0)
        def _():
            dya_ref[...] = jnp.zeros_like(dya_ref)
            dyr_ref[...] = jnp.zeros_like(dyr_ref)
            dyx_ref[...] = jnp.zeros_like(dyx_ref)

        dm = dm_ref[...]
        for y_ref, w_ref, m_ref, b_ref, dg_ref, du_ref, dy_ref in (
                (ya_ref, wa_ref, m0, b0, dg0, du0, dya_ref), (yr_ref, wr_ref, m1, b1, dg1, du1, dyr_ref),
                (yx_ref, wx_ref, m2, b2, dg2, du2, dyx_ref)):
            w = w_ref[...]
            u = _dot(y_ref[...], w, ((1,), (0,)))
            gt = jax.nn.sigmoid(m_ref[...] + b_ref[...])
            dg_ref[...] = (dm * u * gt * (1.0 - gt)).astype(BF16)
            du = (dm * gt).astype(BF16)
            du_ref[...] = du
            dy_ref[...] += _dot(du, w, ((1,), (1,)))

    mg = lambda br: _bs((MT, MN), lambda i, j: (i, C_MG // MN + br * (D // MN) + j))
    gb = lambda br: _bs((1, MN), lambda i, j: (0, br * (D // MN) + j))
    tile = _bs((MT, MN), lambda i, j: (i, j))
    return pl.pallas_call(
        body, name="merge_bwd", grid=(s // MT, D // MN),
        in_specs=[_bs((MT, RW), lambda i, j: (i, 0)), _bs((MT, RW), lambda i, j: (i, 0)), _bs((MT, XW), lambda i, j: (i, 0)),
                  _bs((RW, MN), lambda i, j: (0, j)), _bs((RW, MN), lambda i, j: (0, j)), _bs((XW, MN), lambda i, j: (0, j)),
                  mg(0), mg(1), mg(2), gb(0), gb(1), gb(2), tile],
        out_specs=[tile] * 6 + [_bs((MT, RW), lambda i, j: (i, 0)), _bs((MT, RW), lambda i, j: (i, 0)),
                                _bs((MT, XW), lambda i, j: (i, 0))],
        out_shape=[jax.ShapeDtypeStruct((s, D), BF16)] * 6 + [jax.ShapeDtypeStruct((s, RW), F32),
                                                               jax.ShapeDtypeStruct((s, RW), F32),
                                                               jax.ShapeDtypeStruct((s, XW), F32)],
        compiler_params=_params(("parallel", "arbitrary")),
    )(ya, yr, yx, wa, wr, wx, proj, proj, proj, gate_b, gate_b, gate_b, dmerged)


def _colsum(a, name):
    m, n = a.shape
    tm, tn = min(512, m), 512

    def body(a_ref, o_ref):
        @pl.when(pl.program_id(1) == 0)
        def _():
            o_ref[...] = jnp.zeros_like(o_ref)

        o_ref[...] += jnp.sum(a_ref[...].astype(F32), axis=0, keepdims=True)

    return pl.pallas_call(
        body, name=name, grid=(n // tn, m // tm),
        in_specs=[_bs((tm, tn), lambda j, i: (i, j))], out_specs=_bs((1, tn), lambda j, i: (0, j)),
        out_shape=jax.ShapeDtypeStruct((1, n), F32),
        compiler_params=_params(("parallel", "arbitrary")),
    )(a)


def _in_bwd(dproj, w_in, x, g, dout):
    s = x.shape[0]
    tm, tk = 256, 896
    nk = NIN // tk

    def body(dp_ref, w_ref, x_ref, g_ref, do_ref, gx_ref, gg_ref, acc):
        i, kk = pl.program_id(0), pl.program_id(1)

        @pl.when((i == 0) & (kk == 0))
        def _():
            gg_ref[...] = jnp.zeros_like(gg_ref)

        @pl.when(kk == 0)
        def _():
            acc[...] = jnp.zeros_like(acc)

        acc[...] += _dot(dp_ref[...], w_ref[...], ((1,), (1,)))

        @pl.when(kk == nk - 1)
        def _():
            xv, dh, gv = x_ref[...], acc[...], g_ref[...]
            r = lax.rsqrt(jnp.mean(xv * xv, axis=-1, keepdims=True) + NORM_EPS)
            xn = xv * r
            gg_ref[...] += jnp.sum(dh * xn, axis=0, keepdims=True)
            dxn = dh * gv
            dx = r * (dxn - xn * jnp.mean(dxn * xn, axis=-1, keepdims=True))
            gx_ref[...] = do_ref[...] + dx

    return pl.pallas_call(
        body, name="in_bwd", grid=(s // tm, nk),
        in_specs=[_bs((tm, tk), lambda i, kk: (i, kk)), _bs((D, tk), lambda i, kk: (0, kk)),
                  _bs((tm, D), lambda i, kk: (i, 0)), _bs((1, D), lambda i, kk: (0, 0)), _bs((tm, D), lambda i, kk: (i, 0))],
        out_specs=[_bs((tm, D), lambda i, kk: (i, 0)), _bs((1, D), lambda i, kk: (0, 0))],
        out_shape=[jax.ShapeDtypeStruct((s, D), F32), jax.ShapeDtypeStruct((1, D), F32)],
        scratch_shapes=[pltpu.VMEM((tm, D), F32)],
        compiler_params=_params(("arbitrary", "arbitrary")),
    )(dproj, w_in, x, g, dout)


def _adamw_math(w, g, m, v):
    m = ADAM_B1 * m + (1.0 - ADAM_B1) * g
    v = ADAM_B2 * v + (1.0 - ADAM_B2) * jnp.square(g)
    m_hat = m / (1.0 - ADAM_B1 ** ADAM_STEP)
    v_hat = v / (1.0 - ADAM_B2 ** ADAM_STEP)
    delta = -ADAM_LR * (m_hat / (jnp.sqrt(v_hat) + ADAM_EPS) + ADAM_WD * w)
    return delta, m, v


def _adamw(parts, w, m, v, name):
    rows, cols = w.shape
    tr = rows
    for cand in (256, 128, 64, 32, 16, 8):
        if rows % cand == 0 and cand * cols * 4 <= (1 << 20):
            tr = cand
            break
    n = len(parts)

    def body(*refs):
        g = refs[0][...].astype(F32)
        for r in refs[1:n]:
            g = g + r[...].astype(F32)
        w_ref, m_ref, v_ref, g_out, d_out, m_out, v_out = refs[n:]
        delta, m_new, v_new = _adamw_math(w_ref[...], g, m_ref[...], v_ref[...])
        g_out[...] = g
        d_out[...] = delta
        m_out[...] = m_new
        v_out[...] = v_new

    spec = _bs((tr, cols), lambda i: (i, 0))
    return pl.pallas_call(
        body, name=name, grid=(rows // tr,),
        in_specs=[spec] * (n + 3), out_specs=[spec] * 4,
        out_shape=[jax.ShapeDtypeStruct((rows, cols), F32)] * 4,
        compiler_params=_params(("parallel",)),
    )(*parts, w, m, v)


def _sum_parts(parts, name):
    rows, cols = parts[0].shape
    tr = rows
    for cand in (256, 128, 64, 32, 16, 8):
        if rows % cand == 0 and cand * cols * 4 <= (1 << 20):
            tr = cand
            break

    def body(*refs):
        acc = refs[0][...].astype(F32)
        for r in refs[1:-1]:
            acc = acc + r[...].astype(F32)
        refs[-1][...] = acc

    spec = _bs((tr, cols), lambda i: (i, 0))
    return pl.pallas_call(
        body, name=name, grid=(rows // tr,), in_specs=[spec] * len(parts), out_specs=spec,
        out_shape=jax.ShapeDtypeStruct((rows, cols), F32), compiler_params=_params(("parallel",)),
    )(*parts)


ANY = pl.BlockSpec(memory_space=pl.ANY)


def _other_chips(x, y):
    return [(1 - x, y), (x, 1 - y), (1 - x, 1 - y)]


def _gather_shards(arrays, name):
    n = len(arrays)

    def body(*refs):
        ins, outs = refs[:n], refs[n:2 * n]
        ici_send, ici_recv, d2d_send, d2d_recv, local_sems = refs[2 * n:]
        x, y, c = lax.axis_index("x"), lax.axis_index("y"), lax.axis_index("c")
        me = 2 * x + y
        chips = _other_chips(x, y)

        def half(i, who):
            h = arrays[i].shape[0] // 2
            return pl.ds(who * h, h)

        def ici(i, j, src_chip, to):
            return pltpu.make_async_remote_copy(
                src_ref=ins[i].at[half(i, c)], dst_ref=outs[i].at[src_chip, half(i, c)], send_sem=ici_send.at[3 * i + j],
                recv_sem=ici_recv.at[3 * i + j], device_id=to, device_id_type=MESH)

        def d2d(i, j, src_chip, who):
            piece = outs[i].at[src_chip, half(i, who)]
            return pltpu.make_async_remote_copy(
                src_ref=piece, dst_ref=piece, send_sem=d2d_send.at[3 * i + j], recv_sem=d2d_recv.at[3 * i + j],
                device_id=(x, y, 1 - c), device_id_type=MESH)

        local_copies, sends = [], []
        for i in range(n):
            cp = pltpu.make_async_copy(ins[i], outs[i].at[me], local_sems.at[i])
            cp.start()
            local_copies.append(cp)
            for j, (px, py) in enumerate(chips):
                rc = ici(i, j, me, (px, py, c))
                rc.start()
                sends.append(rc)
        for i in range(n):
            for j, (px, py) in enumerate(chips):
                ici(i, j, 2 * px + py, (px, py, c)).wait_recv()
                fw = d2d(i, j, 2 * px + py, c)
                fw.start()
                sends.append(fw)
        for i in range(n):
            for j, (px, py) in enumerate(chips):
                d2d(i, j, 2 * px + py, 1 - c).wait_recv()
        for rc in sends:
            rc.wait_send()
        for cp in local_copies:
            cp.wait()

    dma = lambda k: pltpu.SemaphoreType.DMA((k,))
    return pl.pallas_call(
        body, name=name, in_specs=[ANY] * n, out_specs=[ANY] * n,
        out_shape=[jax.ShapeDtypeStruct((4,) + a.shape, a.dtype) for a in arrays],
        scratch_shapes=[dma(3 * n), dma(3 * n), dma(3 * n), dma(3 * n), dma(n)],
        compiler_params=pltpu.CompilerParams(has_side_effects=True),
    )(*arrays)


def _scatter_shards(stacks, name):
    n = len(stacks)

    def body(*refs):
        ins, outs = refs[:n], refs[n:2 * n]
        send_sems, recv_sems = refs[2 * n:]
        x, y, c = lax.axis_index("x"), lax.axis_index("y"), lax.axis_index("c")
        chips = _other_chips(x, y)
        sends = []
        for i in range(n):
            for j, (px, py) in enumerate(chips):
                rc = pltpu.make_async_remote_copy(
                    src_ref=ins[i].at[2 * px + py], dst_ref=outs[i].at[j], send_sem=send_sems.at[3 * i + j],
                    recv_sem=recv_sems.at[3 * i + j], device_id=(px, py, c), device_id_type=MESH)
                rc.start()
                sends.append(rc)
        for rc in sends:
            rc.wait_recv()
        for rc in sends:
            rc.wait_send()

    return pl.pallas_call(
        body, name=name, in_specs=[ANY] * n, out_specs=[ANY] * n,
        out_shape=[jax.ShapeDtypeStruct((3,) + a.shape[1:], a.dtype) for a in stacks],
        scratch_shapes=[pltpu.SemaphoreType.DMA((3 * n,)), pltpu.SemaphoreType.DMA((3 * n,))],
        compiler_params=pltpu.CompilerParams(has_side_effects=True),
    )(*stacks)


def _pair_exchange(stacks, name):
    n = len(stacks)

    def body(*refs):
        ins, outs = refs[:n], refs[n:2 * n]
        send_sems, recv_sems = refs[2 * n:]
        x, y, c = lax.axis_index("x"), lax.axis_index("y"), lax.axis_index("c")
        cps = []
        for i in range(n):
            h = stacks[i].shape[1] // 2
            rc = pltpu.make_async_remote_copy(
                src_ref=ins[i].at[:, pl.ds((1 - c) * h, h)], dst_ref=outs[i], send_sem=send_sems.at[i],
                recv_sem=recv_sems.at[i], device_id=(x, y, 1 - c), device_id_type=MESH)
            rc.start()
            cps.append(rc)
        for rc in cps:
            rc.wait_recv()
        for rc in cps:
            rc.wait_send()

    return pl.pallas_call(
        body, name=name, in_specs=[ANY] * n, out_specs=[ANY] * n,
        out_shape=[jax.ShapeDtypeStruct((4, a.shape[1] // 2) + a.shape[2:], a.dtype) for a in stacks],
        scratch_shapes=[pltpu.SemaphoreType.DMA((n,)), pltpu.SemaphoreType.DMA((n,))],
        compiler_params=pltpu.CompilerParams(has_side_effects=True),
    )(*stacks)


def _pair_sum(own, theirs, core, name):
    _, r, cols = own.shape
    h = r // 2
    tr = next(t for t in (256, 128, 64, 32, 16) if h % t == 0 and t * cols * 4 <= (1 << 20))
    nt = h // tr

    def body(core_ref, own_ref, th_ref, o32_ref, o16_ref):
        del core_ref
        acc = own_ref[...] + th_ref[...].astype(F32)
        o32_ref[...] = acc
        o16_ref[...] = acc.astype(BF16)

    out = _bs((1, tr, cols), lambda j, t, core_ref: (j, t, 0))
    return pl.pallas_call(
        body, name=name,
        grid_spec=pltpu.PrefetchScalarGridSpec(
            num_scalar_prefetch=1, grid=(4, nt),
            in_specs=[_bs((1, tr, cols), lambda j, t, core_ref: (j, core_ref[0] * nt + t, 0)), out],
            out_specs=[out, out]),
        out_shape=[jax.ShapeDtypeStruct((4, h, cols), F32), jax.ShapeDtypeStruct((4, h, cols), BF16)],
        compiler_params=_params(("parallel", "parallel")),
    )(core, own, theirs)


def _assemble_halves(halves, name):
    n = len(halves)

    def body(*refs):
        ins, outs = refs[:n], refs[n:2 * n]
        send_sems, recv_sems, local_sems = refs[2 * n:]
        x, y, c = lax.axis_index("x"), lax.axis_index("y"), lax.axis_index("c")
        cps, local_copies = [], []
        for i in range(n):
            h = halves[i].shape[0]
            mine = outs[i].at[pl.ds(c * h, h)]
            cp = pltpu.make_async_copy(ins[i], mine, local_sems.at[i])
            cp.start()
            local_copies.append(cp)
            rc = pltpu.make_async_remote_copy(src_ref=ins[i], dst_ref=mine, send_sem=send_sems.at[i],
                                              recv_sem=recv_sems.at[i], device_id=(x, y, 1 - c), device_id_type=MESH)
            rc.start()
            cps.append(rc)
        for i in range(n):
            h = halves[i].shape[0]
            pltpu.make_async_remote_copy(src_ref=ins[i], dst_ref=outs[i].at[pl.ds((1 - c) * h, h)],
                                         send_sem=send_sems.at[i], recv_sem=recv_sems.at[i],
                                         device_id=(x, y, 1 - c), device_id_type=MESH).wait_recv()
        for rc in cps:
            rc.wait_send()
        for cp in local_copies:
            cp.wait()

    dma = lambda: pltpu.SemaphoreType.DMA((n,))
    return pl.pallas_call(
        body, name=name, in_specs=[ANY] * n, out_specs=[ANY] * n,
        out_shape=[jax.ShapeDtypeStruct((2 * a.shape[0],) + a.shape[1:], a.dtype) for a in halves],
        scratch_shapes=[dma(), dma(), dma()],
        compiler_params=pltpu.CompilerParams(has_side_effects=True),
    )(*halves)


def _all_reduce_small(v):
    rows = v.shape[0]

    def body(v_ref, o_ref, buf, send_sems, recv_sems):
        x, y, c = lax.axis_index("x"), lax.axis_index("y"), lax.axis_index("c")
        me = 4 * x + 2 * y + c
        buf[me] = v_ref[...]
        cps = []
        for kbits in range(1, 8):
            bx, by, bc = (kbits >> 2) & 1, (kbits >> 1) & 1, kbits & 1
            px = jnp.where(bx == 1, 1 - x, x)
            py = jnp.where(by == 1, 1 - y, y)
            pc = jnp.where(bc == 1, 1 - c, c)
            rc = pltpu.make_async_remote_copy(src_ref=v_ref, dst_ref=buf.at[me], send_sem=send_sems.at[kbits - 1],
                                              recv_sem=recv_sems.at[kbits - 1], device_id=(px, py, pc),
                                              device_id_type=MESH)
            rc.start()
            cps.append((rc, 4 * px + 2 * py + pc))
        for kbits, (rc, src) in enumerate(cps):
            pltpu.make_async_remote_copy(src_ref=v_ref, dst_ref=buf.at[src], send_sem=send_sems.at[kbits],
                                         recv_sem=recv_sems.at[kbits], device_id=(x, y, c),
                                         device_id_type=MESH).wait_recv()
        for rc, _ in cps:
            rc.wait_send()
        acc = buf[0]
        for d in range(1, 8):
            acc = acc + buf[d]
        o_ref[...] = acc

    return pl.pallas_call(
        body, name="all_reduce_small",
        in_specs=[pl.BlockSpec(memory_space=pltpu.VMEM)], out_specs=pl.BlockSpec(memory_space=pltpu.VMEM),
        out_shape=jax.ShapeDtypeStruct((rows, 128), F32),
        scratch_shapes=[pltpu.VMEM((8, rows, 128), F32), pltpu.SemaphoreType.DMA((7,)), pltpu.SemaphoreType.DMA((7,))],
        compiler_params=pltpu.CompilerParams(has_side_effects=True, vmem_limit_bytes=VMEM_LIMIT),
    )(v)


def _rope_tables(s):
    half = HD // 2
    inv = 10000.0 ** (-jnp.arange(half, dtype=F32) / half)
    ang = jnp.arange(s, dtype=F32)[:, None] * inv[None, :]
    cos, sin = jnp.cos(ang), jnp.sin(ang)
    return jnp.concatenate([cos, cos], axis=1), jnp.concatenate([sin, sin], axis=1)


def _local_step(x, mem, target, norm_g, mem_norm_g, w_in, gate_b, gq, gk, sink, wa, mu, k_k, k_a, r_k, w0, w2, a0, a2,
                ln_w, ln_b, wr, w_kv, gxq, gxk, wx, w_out):
    s = x.shape[0]
    cos, sin = _rope_tables(s)
    r_k = r_k.reshape(1, RW)

    proj, h = _proj_fwd(x, norm_g, w_in)
    ya = _attn_fwd(proj, cos, sin, gq, gk, sink)
    mkv, mn = _mem_kv(mem, mem_norm_g, w_kv)
    yx = _xattn_fwd(proj, mkv, gxq, gxk)
    ps = _shift_fwd(proj, mu)
    kk, dec0, kd0, b0, dec1, kd1, b1 = _pre_fwd(ps, k_k, k_a, w0, w2, a0, a2)
    v8 = _to_t8(ps[:, 2 * RW:3 * RW])
    (y80, ck0), (y81, ck1) = _scan2_fwd([(dec0, kd0, b0), (dec1, kd1, b1)], ps, kk, v8)
    y0, y1 = _from_t8(y80), _from_t8(y81)
    yr = _post_fwd(y0, y1, ps, kd0, kd1, proj, r_k, ln_w, ln_b)
    merged = _merge_fwd(ya, yr, yx, wa, wr, wx, proj, gate_b)
    loss_tile, dout = _out_fwd(merged, w_out, x, target)
    loss_sum = loss_tile[0, 0]

    g = {}
    dmerged = _matmul(dout, w_out, mode="nt", m=s, n=D, k=D, tm=min(512, s), tn=512, tk=512, name="dmerged")
    g["w_out"] = _matmul(merged, dout, mode="tn", m=D, n=D, k=s, tm=512, tn=512, tk=min(512, s), name="grad_w_out")
    dg0, dg1, dg2, du0, du1, du2, dya, dyr, dyx = _merge_bwd(ya, yr, yx, wa, wr, wx, proj, gate_b, dmerged)
    g["attn_w_o"] = _matmul(ya, du0, mode="tn", m=RW, n=D, k=s, tm=RW, tn=512, tk=min(512, s), name="grad_attn_w_o")
    g["rwkv_w_o"] = _matmul(yr, du1, mode="tn", m=RW, n=D, k=s, tm=RW, tn=512, tk=min(512, s), name="grad_rwkv_w_o")
    g["x_w_o"] = _matmul(yx, du2, mode="tn", m=XW, n=D, k=s, tm=XW, tn=512, tk=min(512, s), name="grad_x_w_o")
    dmg = jnp.concatenate([dg0, dg1, dg2], axis=1)
    g["gate_b"] = _colsum(dmg, "grad_gate_b")

    daq, dak, dav, dag, g["attn_q_norm_g"], g["attn_k_norm_g"], g["attn_sink"] = _attn_bwd(proj, cos, sin, gq, gk, sink, dya)

    dxq, dxg, dmkv, g["x_q_norm_g"], g["x_k_norm_g"] = _xattn_bwd(proj, mkv, gxq, gxk, dyx)
    g["x_w_kv"] = _matmul(mn, dmkv, mode="tn", m=D, n=2 * XW, k=NMEM, tm=512, tn=512, tk=NMEM, name="grad_x_w_kv")
    dmn = _matmul(dmkv, w_kv, mode="nt", m=NMEM, n=D, k=2 * XW, tm=NMEM, tn=512, tk=2 * XW, name="dmn")
    g["mem_norm_g"] = _mem_bwd(mem, mem_norm_g, dmn)

    dys, dr_p, dv_p, dkd0_p, dkd1_p, drg, g["rwkv_r_k"], g["rwkv_ln_w"], g["rwkv_ln_b"] = _post_bwd(
        y0, y1, ps, kd0, kd1, proj, r_k, ln_w, ln_b, dyr)
    dy8 = _to_t8(dys)
    (dr0, dd0, db0, dk0, dkk0, dv80), (dr1, dd1, db1, dk1, dkk1, dv81) = _scan2_bwd(
        [(dec0, kd0, b0, ck0), (dec1, kd1, b1, ck1)], ps, kk, v8, dy8)
    dr = dr_p + dr0 + dr1
    dv = dv_p + _from_t8(dv80) + _from_t8(dv81)
    cts = (dkk0 + dkk1, dd0, dk0 + dkd0_p, db0, dd1, dk1 + dkd1_p, db1)
    dps, g["rwkv_k_k"], g["rwkv_k_a"], g["rwkv_w0"], g["rwkv_w2"], g["rwkv_a0"], g["rwkv_a2"] = _pre_bwd(
        ps, k_k, k_a, w0, w2, a0, a2, dr, dv, cts)
    drs, g["rwkv_mu"] = _shift_bwd(proj, mu, dps)

    dproj = jnp.concatenate([daq.astype(BF16), dak.astype(BF16), dav.astype(BF16), dag.astype(BF16), drs.astype(BF16),
                             drg.astype(BF16), dxq.astype(BF16), dxg.astype(BF16), dmg], axis=1)
    g["w_in"] = _matmul(h, dproj, mode="tn", m=D, n=NIN, k=s, tm=512, tn=896, tk=min(512, s), name="grad_w_in")
    grad_x, g["norm_g"] = _in_bwd(dproj, w_in, x, norm_g, dout)
    g["rwkv_r_k"] = g["rwkv_r_k"].reshape(AH, HD)
    return loss_sum, grad_x, g


WEIGHTS = ['norm_g', 'mem_norm_g', 'w_in', 'gate_b', 'attn_q_norm_g', 'attn_k_norm_g', 'attn_sink', 'attn_w_o',
           'rwkv_mu', 'rwkv_k_k', 'rwkv_k_a', 'rwkv_r_k', 'rwkv_w0', 'rwkv_w2', 'rwkv_a0', 'rwkv_a2', 'rwkv_ln_w',
           'rwkv_ln_b', 'rwkv_w_o', 'x_w_kv', 'x_q_norm_g', 'x_k_norm_g', 'x_w_o', 'w_out']
BIG = ['w_in', 'attn_w_o', 'rwkv_w_o', 'x_w_kv', 'x_w_o', 'w_out']
COL_SHARDED = ['w_in', 'attn_w_o', 'rwkv_w_o', 'x_w_o']
LORA = ['rwkv_w0', 'rwkv_w2', 'rwkv_a0', 'rwkv_a2']
SMALL = [n for n in WEIGHTS if n not in BIG]


def _unshard_cols(stack):
    return jnp.concatenate([stack[i] for i in range(4)], axis=-1)


def _shard_cols(full):
    w = full.shape[-1] // 4
    return [full[..., i * w:(i + 1) * w] for i in range(4)]


def kernel(x, mem, norm_g, mem_norm_g, w_in, gate_b, attn_q_norm_g, attn_k_norm_g, attn_sink, attn_w_o, rwkv_mu, rwkv_k_k, rwkv_k_a, rwkv_r_k, rwkv_w0, rwkv_w2, rwkv_a0, rwkv_a2, rwkv_ln_w, rwkv_ln_b, rwkv_w_o, x_w_kv, x_q_norm_g, x_k_norm_g, x_w_o, w_out, loss_target, m_norm_g, m_mem_norm_g, m_w_in, m_gate_b, m_attn_q_norm_g, m_attn_k_norm_g, m_attn_sink, m_attn_w_o, m_rwkv_mu, m_rwkv_k_k, m_rwkv_k_a, m_rwkv_r_k, m_rwkv_w0, m_rwkv_w2, m_rwkv_a0, m_rwkv_a2, m_rwkv_ln_w, m_rwkv_ln_b, m_rwkv_w_o, m_x_w_kv, m_x_q_norm_g, m_x_k_norm_g, m_x_w_o, m_w_out, v_norm_g, v_mem_norm_g, v_w_in, v_gate_b, v_attn_q_norm_g, v_attn_k_norm_g, v_attn_sink, v_attn_w_o, v_rwkv_mu, v_rwkv_k_k, v_rwkv_k_a, v_rwkv_r_k, v_rwkv_w0, v_rwkv_w2, v_rwkv_a0, v_rwkv_a2, v_rwkv_ln_w, v_rwkv_ln_b, v_rwkv_w_o, v_x_w_kv, v_x_q_norm_g, v_x_k_norm_g, v_x_w_o, v_w_out):
    args = dict(locals())
    canon = lambda a: a[0] if a.ndim > 2 else a
    w = {n: canon(args[n]) for n in WEIGHTS}
    m = {n: canon(args["m_" + n]) for n in WEIGHTS}
    v = {n: canon(args["v_" + n]) for n in WEIGHTS}
    shard = 2 * lax.axis_index("x") + lax.axis_index("y")

    local = [w[n].astype(BF16) for n in BIG] + [w[n].reshape(2, -1, w[n].shape[-1]) for n in LORA]
    stacks = dict(zip(BIG + LORA, _gather_shards(local, "gather_weights")))
    full = {}
    for n in COL_SHARDED:
        full[n] = _unshard_cols(stacks[n])
    for n in LORA:
        full[n] = _unshard_cols(stacks[n]).reshape(w[n].shape[:-1] + (RW,))
    full["x_w_kv"] = stacks["x_w_kv"].reshape(D, 2 * XW)
    full["w_out"] = stacks["w_out"].reshape(D, D)

    loss_sum, grad_x, g = _local_step(
        x[0], mem[0], loss_target[0], w["norm_g"], w["mem_norm_g"], full["w_in"], w["gate_b"], w["attn_q_norm_g"],
        w["attn_k_norm_g"], w["attn_sink"], full["attn_w_o"], w["rwkv_mu"], w["rwkv_k_k"], w["rwkv_k_a"], w["rwkv_r_k"],
        full["rwkv_w0"], full["rwkv_w2"], full["rwkv_a0"], full["rwkv_a2"], w["rwkv_ln_w"], w["rwkv_ln_b"],
        full["rwkv_w_o"], full["x_w_kv"], w["x_q_norm_g"], w["x_k_norm_g"], full["x_w_o"], full["w_out"])

    loss = lax.psum(0.5 * loss_sum / D, ("x", "y", "c"))

    def as_stack(n, dtype):
        if n in COL_SHARDED:
            return jnp.stack([p.astype(dtype) for p in _shard_cols(g[n])])
        return g[n].reshape((4, g[n].shape[0] // 4) + g[n].shape[1:]).astype(dtype)

    core = lax.axis_index("c").astype(jnp.int32).reshape(1)
    sibling = _pair_exchange([as_stack(n, BF16) for n in BIG], "pair_exchange")
    pair32, pair16 = [], []
    for n, th in zip(BIG, sibling):
        a32, a16 = _pair_sum(as_stack(n, F32), th, core, "pair_sum_" + n)
        pair32.append(a32)
        pair16.append(a16)
    recv = _scatter_shards(pair16, "scatter_grads")
    halves = []
    for n, p32, r in zip(BIG, pair32, recv):
        own = lax.dynamic_index_in_dim(p32, shard, 0, keepdims=False)
        halves.append(_sum_parts([own, r[0], r[1], r[2]], "sum_" + n))
    totals = _assemble_halves(halves, "assemble_grads")

    out_g, out_d, out_m, out_v = {}, {}, {}, {}
    for n, tot in zip(BIG, totals):
        out_g[n], out_d[n], out_m[n], out_v[n] = _adamw([tot], w[n], m[n], v[n], "adamw_" + n)

    flat = jnp.concatenate([g[n].reshape(-1) for n in SMALL])
    total = flat.shape[0]
    padded = -(-total // 1024) * 1024
    flat = jnp.pad(flat, (0, padded - total)).reshape(padded // 128, 128)
    red = _all_reduce_small(flat).reshape(-1)
    off = 0
    gs = {}
    for n in SMALL:
        size = g[n].size
        t = red[off:off + size].reshape(g[n].shape)
        off += size
        if n in LORA:
            wd = t.shape[-1] // 4
            t = lax.dynamic_slice_in_dim(t, shard * wd, wd, axis=t.ndim - 1)
        gs[n] = t

    def pack(d):
        f = jnp.concatenate([d[n].reshape(-1) for n in SMALL])
        return jnp.pad(f, (0, -(-f.shape[0] // 1024) * 1024 - f.shape[0])).reshape(-1, 128)

    pg, pd, pm, pv = _adamw([pack(gs)], pack(w), pack(m), pack(v), "adamw_small")
    off = 0
    for n in SMALL:
        size = w[n].size
        for dst, src in ((out_g, pg), (out_d, pd), (out_m, pm), (out_v, pv)):
            dst[n] = src.reshape(-1)[off:off + size].reshape(w[n].shape)
        off += size

    lead = lambda d: [d[n][None] if args[n].ndim > 2 else d[n] for n in WEIGHTS]
    return (loss, grad_x[None], *lead(out_g), *lead(out_d), *lead(out_m), *lead(out_v))
```

```python
import functools

import jax
import jax.numpy as jnp
from jax import lax
from jax.experimental import pallas as pl
from jax.experimental.pallas import tpu as pltpu

F32 = jnp.float32
BF16 = jnp.bfloat16
HI = lax.Precision.HIGHEST
MESH = pl.DeviceIdType.MESH

D = 2048
NMEM = 256
NORM_EPS = 1e-6
NEG_INF = -1e30
GN_EPS = 64e-5
HD = 64
AH = 12
AKV = 4
RW = 768
XH = 4
XD = 128
XW = 512
NIN = 12544
RSW = 2560
C_AQ, C_AK, C_AV, C_AG, C_RS, C_RG, C_XQ, C_XG, C_MG = 0, 768, 1024, 1280, 2048, 4608, 5376, 5888, 6400
WIN = 384
QB = 128
TC = 16
NPAIR = 6

ADAM_LR, ADAM_B1, ADAM_B2, ADAM_EPS, ADAM_WD, ADAM_STEP = 0.001, 0.9, 0.999, 1e-08, 0.01, 10

VMEM_LIMIT = 56 * 1024 * 1024


def _bs(shape, imap):
    return pl.BlockSpec(shape, imap)


def _params(sem=None, vmem=VMEM_LIMIT):
    return pltpu.CompilerParams(dimension_semantics=sem, vmem_limit_bytes=vmem)


def _dot(a, b, dims):
    return lax.dot_general(a.astype(BF16), b.astype(BF16), (dims, ((), ())), preferred_element_type=F32)


@jax.custom_vjp
def _mm_nn(a, b):
    return _dot(a, b, ((1,), (0,)))


def _mm_nn_fwd(a, b):
    return _mm_nn(a, b), (a, b)


def _mm_nn_bwd(res, ct):
    a, b = res
    return _dot(ct, b, ((1,), (1,))), _dot(a, ct, ((0,), (0,)))


_mm_nn.defvjp(_mm_nn_fwd, _mm_nn_bwd)


@jax.custom_vjp
def _mm_nt(a, b):
    return _dot(a, b, ((1,), (1,)))


def _mm_nt_fwd(a, b):
    return _mm_nt(a, b), (a, b)


def _mm_nt_bwd(res, ct):
    a, b = res
    return _dot(ct, b, ((1,), (0,))), _dot(ct, a, ((0,), (0,)))


_mm_nt.defvjp(_mm_nt_fwd, _mm_nt_bwd)


def _seg_matrix(n, seg):
    r = lax.broadcasted_iota(jnp.int32, (n, n), 0) // seg
    c = lax.broadcasted_iota(jnp.int32, (n, n), 1) // seg
    return (r == c).astype(F32)


def _rot_matrix():
    r = lax.broadcasted_iota(jnp.int32, (HD, HD), 0)
    c = lax.broadcasted_iota(jnp.int32, (HD, HD), 1)
    return jnp.where(c == r + HD // 2, 1.0, 0.0).astype(F32) - jnp.where(c == r - HD // 2, 1.0, 0.0).astype(F32)


def _hdot(a, m):
    return jnp.dot(a, m, precision=HI, preferred_element_type=F32)


def _rms(t, g):
    return t * lax.rsqrt(jnp.mean(t * t, axis=-1, keepdims=True) + NORM_EPS) * g


def _silu(t):
    return t * jax.nn.sigmoid(t)


def _softplus(z):
    return jnp.maximum(z, 0.0) + jnp.log(1.0 + jnp.exp(-jnp.abs(z)))


def _matmul(a, b, *, mode, m, n, k, tm, tn, tk, name, a_off=(0, 0), b_off=(0, 0), out_dtype=F32):
    nk = k // tk
    if mode == "tn":
        a_spec = _bs((tk, tm), lambda i, j, kk: (kk + a_off[0], i + a_off[1]))
        dims = ((0,), (0,))
    else:
        a_spec = _bs((tm, tk), lambda i, j, kk: (i + a_off[0], kk + a_off[1]))
        dims = ((1,), (1,)) if mode == "nt" else ((1,), (0,))
    if mode == "nt":
        b_spec = _bs((tn, tk), lambda i, j, kk: (j + b_off[0], kk + b_off[1]))
    else:
        b_spec = _bs((tk, tn), lambda i, j, kk: (kk + b_off[0], j + b_off[1]))

    def body(a_ref, b_ref, o_ref, acc):
        kk = pl.program_id(2)

        @pl.when(kk == 0)
        def _():
            acc[...] = jnp.zeros_like(acc)

        acc[...] += _dot(a_ref[...], b_ref[...], dims)

        @pl.when(kk == nk - 1)
        def _():
            o_ref[...] = acc[...].astype(out_dtype)

    return pl.pallas_call(
        body, name=name, grid=(m // tm, n // tn, nk),
        in_specs=[a_spec, b_spec], out_specs=_bs((tm, tn), lambda i, j, kk: (i, j)),
        out_shape=jax.ShapeDtypeStruct((m, n), out_dtype),
        scratch_shapes=[pltpu.VMEM((tm, tn), F32)],
        compiler_params=_params(("parallel", "parallel", "arbitrary")),
    )(a, b)


def _proj_fwd(x, g, w):
    s = x.shape[0]
    tm, tn = min(512, s), 896

    def body(x_ref, g_ref, w_ref, o_ref, h_ref, hs):
        @pl.when(pl.program_id(1) == 0)
        def _():
            h = _rms(x_ref[...], g_ref[...]).astype(BF16)
            hs[...] = h
            h_ref[...] = h

        o_ref[...] = jnp.dot(hs[...], w_ref[...], preferred_element_type=F32)

    return pl.pallas_call(
        body, name="proj_fwd", grid=(s // tm, NIN // tn),
        in_specs=[_bs((tm, D), lambda i, j: (i, 0)), _bs((1, D), lambda i, j: (0, 0)), _bs((D, tn), lambda i, j: (0, j))],
        out_specs=[_bs((tm, tn), lambda i, j: (i, j)), _bs((tm, D), lambda i, j: (i, 0))],
        out_shape=[jax.ShapeDtypeStruct((s, NIN), F32), jax.ShapeDtypeStruct((s, D), BF16)],
        scratch_shapes=[pltpu.VMEM((tm, D), BF16)],
        compiler_params=_params(("parallel", "arbitrary")),
    )(x, g, w)


def _rope(t, cos, sin, rot):
    return t * cos + _hdot(t, rot) * sin


def _attn_tile(qs, ks, vs, gs, sinks, gq, gk, cq, sq, ck, sk, mask, rot):
    outs = []
    for hk in range(AKV):
        kh = _rope(_rms(ks[hk], gk), ck, sk, rot)
        for g in range(AH // AKV):
            h = hk * (AH // AKV) + g
            qh = _rope(_rms(qs[h], gq), cq, sq, rot)
            sc = _mm_nt(qh, kh) * (HD ** -0.5)
            sc = jnp.where(mask, sc, NEG_INF)
            mx = lax.stop_gradient(jnp.maximum(jnp.max(sc, axis=-1, keepdims=True), sinks[h]))
            p = jnp.exp(sc - mx)
            den = jnp.sum(p, axis=-1, keepdims=True) + jnp.exp(sinks[h] - mx)
            o = _mm_nn(p / den, vs[hk])
            outs.append(o * _silu(gs[h]))
    return outs


def _attn_load(n, s, aq_ref, ak_ref, av_ref, ag_refs, cos_ref, sin_ref, sink_ref):
    start = pl.multiple_of(jnp.clip((n - 1) * QB, 0, s - WIN), QB)
    q0 = pl.multiple_of(n * QB, QB)
    qs = [aq_ref[:, h * HD:(h + 1) * HD] for h in range(AH)]
    ks = [ak_ref[pl.ds(start, WIN), h * HD:(h + 1) * HD] for h in range(AKV)]
    vs = [av_ref[pl.ds(start, WIN), h * HD:(h + 1) * HD] for h in range(AKV)]
    gs = [ag_refs[h // 4][:, (h % 4) * HD:(h % 4 + 1) * HD] for h in range(AH)]
    sinks = [sink_ref[0:1, h:h + 1] for h in range(AH)]
    cq, sq = cos_ref[pl.ds(q0, QB), :], sin_ref[pl.ds(q0, QB), :]
    ck, sk = cos_ref[pl.ds(start, WIN), :], sin_ref[pl.ds(start, WIN), :]
    qpos = q0 + lax.broadcasted_iota(jnp.int32, (QB, WIN), 0)
    kpos = start + lax.broadcasted_iota(jnp.int32, (QB, WIN), 1)
    mask = jnp.abs(kpos - qpos) <= QB
    return start, qs, ks, vs, gs, sinks, cq, sq, ck, sk, mask


def _attn_specs(s):
    return [
        _bs((QB, 768), lambda n: (n, 0)),
        _bs((s, 256), lambda n: (0, C_AK // 256)),
        _bs((s, 256), lambda n: (0, C_AV // 256)),
        _bs((QB, 256), lambda n: (n, C_AG // 256)),
        _bs((QB, 256), lambda n: (n, C_AG // 256 + 1)),
        _bs((QB, 256), lambda n: (n, C_AG // 256 + 2)),
        _bs((s, HD), lambda n: (0, 0)),
        _bs((s, HD), lambda n: (0, 0)),
        _bs((1, HD), lambda n: (0, 0)),
        _bs((1, HD), lambda n: (0, 0)),
        _bs((1, AH), lambda n: (0, 0)),
    ]


def _attn_fwd(proj, cos, sin, gq, gk, sink):
    s = proj.shape[0]

    def body(aq_ref, ak_ref, av_ref, ag0, ag1, ag2, cos_ref, sin_ref, gq_ref, gk_ref, sink_ref, o_ref):
        n = pl.program_id(0)
        _, qs, ks, vs, gs, sinks, cq, sq, ck, sk, mask = _attn_load(
            n, s, aq_ref, ak_ref, av_ref, (ag0, ag1, ag2), cos_ref, sin_ref, sink_ref)
        outs = _attn_tile(qs, ks, vs, gs, sinks, gq_ref[...], gk_ref[...], cq, sq, ck, sk, mask, _rot_matrix())
        for h in range(AH):
            o_ref[:, h * HD:(h + 1) * HD] = outs[h]

    return pl.pallas_call(
        body, name="attn_fwd", grid=(s // QB,),
        in_specs=_attn_specs(s), out_specs=_bs((QB, 768), lambda n: (n, 0)),
        out_shape=jax.ShapeDtypeStruct((s, 768), F32),
        compiler_params=_params(("arbitrary",)),
    )(proj, proj, proj, proj, proj, proj, cos, sin, gq, gk, sink)


def _attn_bwd(proj, cos, sin, gq, gk, sink, dy):
    s = proj.shape[0]

    def body(aq_ref, ak_ref, av_ref, ag0, ag1, ag2, cos_ref, sin_ref, gq_ref, gk_ref, sink_ref, dy_ref,
             daq_ref, dak_ref, dav_ref, dag_ref, dgq_ref, dgk_ref, dsink_ref):
        n = pl.program_id(0)

        @pl.when(n == 0)
        def _():
            dak_ref[...] = jnp.zeros_like(dak_ref)
            dav_ref[...] = jnp.zeros_like(dav_ref)
            dgq_ref[...] = jnp.zeros_like(dgq_ref)
            dgk_ref[...] = jnp.zeros_like(dgk_ref)
            dsink_ref[...] = jnp.zeros_like(dsink_ref)

        start, qs, ks, vs, gs, sinks, cq, sq, ck, sk, mask = _attn_load(
            n, s, aq_ref, ak_ref, av_ref, (ag0, ag1, ag2), cos_ref, sin_ref, sink_ref)
        rot = _rot_matrix()

        def f(qs, ks, vs, gs, sinks, gq, gk):
            return _attn_tile(qs, ks, vs, gs, sinks, gq, gk, cq, sq, ck, sk, mask, rot)

        _, vjp = jax.vjp(f, qs, ks, vs, gs, sinks, gq_ref[...], gk_ref[...])
        dys = [dy_ref[:, h * HD:(h + 1) * HD] for h in range(AH)]
        dqs, dks, dvs, dgs, dsinks, dgq, dgk = vjp(dys)
        for h in range(AH):
            daq_ref[:, h * HD:(h + 1) * HD] = dqs[h]
            dag_ref[:, h * HD:(h + 1) * HD] = dgs[h]
            dsink_ref[0:1, h:h + 1] += dsinks[h]
        for h in range(AKV):
            dak_ref[pl.ds(start, WIN), h * HD:(h + 1) * HD] += dks[h]
            dav_ref[pl.ds(start, WIN), h * HD:(h + 1) * HD] += dvs[h]
        dgq_ref[...] += dgq
        dgk_ref[...] += dgk

    whole = lambda shape: _bs(shape, lambda n: (0, 0))
    return pl.pallas_call(
        body, name="attn_bwd", grid=(s // QB,),
        in_specs=_attn_specs(s) + [_bs((QB, 768), lambda n: (n, 0))],
        out_specs=[_bs((QB, 768), lambda n: (n, 0)), whole((s, 256)), whole((s, 256)), _bs((QB, 768), lambda n: (n, 0)),
                   whole((1, HD)), whole((1, HD)), whole((1, AH))],
        out_shape=[jax.ShapeDtypeStruct((s, 768), F32), jax.ShapeDtypeStruct((s, 256), F32),
                   jax.ShapeDtypeStruct((s, 256), F32), jax.ShapeDtypeStruct((s, 768), F32),
                   jax.ShapeDtypeStruct((1, HD), F32), jax.ShapeDtypeStruct((1, HD), F32),
                   jax.ShapeDtypeStruct((1, AH), F32)],
        compiler_params=_params(("arbitrary",)),
    )(proj, proj, proj, proj, proj, proj, cos, sin, gq, gk, sink, dy)


def _mem_kv(mem, g, w):
    def body(m_ref, g_ref, w_ref, o_ref, mn_ref):
        mn = _rms(m_ref[...], g_ref[...]).astype(BF16)
        mn_ref[...] = mn
        o_ref[...] = jnp.dot(mn, w_ref[...], preferred_element_type=F32)

    return pl.pallas_call(
        body, name="mem_kv",
        out_shape=[jax.ShapeDtypeStruct((NMEM, 2 * XW), F32), jax.ShapeDtypeStruct((NMEM, D), BF16)],
        compiler_params=_params(),
    )(mem, g, w)


def _xattn_tile(qs, gs, kms, vms, gxq, gxk):
    outs = []
    for h in range(XH):
        q = _rms(qs[h], gxq)
        km = _rms(kms[h], gxk)
        sc = _mm_nt(q, km) * (XD ** -0.5)
        mx = lax.stop_gradient(jnp.max(sc, axis=-1, keepdims=True))
        p = jnp.exp(sc - mx)
        p = p / jnp.sum(p, axis=-1, keepdims=True)
        outs.append(_mm_nn(p, vms[h]) * _silu(gs[h]))
    return outs


XT = 256


def _xattn_specs():
    return [
        _bs((XT, 256), lambda i: (i, C_XQ // 256)), _bs((XT, 256), lambda i: (i, C_XQ // 256 + 1)),
        _bs((XT, 256), lambda i: (i, C_XG // 256)), _bs((XT, 256), lambda i: (i, C_XG // 256 + 1)),
        _bs((NMEM, 2 * XW), lambda i: (0, 0)),
        _bs((1, XD), lambda i: (0, 0)), _bs((1, XD), lambda i: (0, 0)),
    ]


def _xattn_load(q0, q1, g0, g1, mkv_ref):
    qs = [(q0, q1)[h // 2][:, (h % 2) * XD:(h % 2 + 1) * XD] for h in range(XH)]
    gs = [(g0, g1)[h // 2][:, (h % 2) * XD:(h % 2 + 1) * XD] for h in range(XH)]
    kms = [mkv_ref[:, h * XD:(h + 1) * XD] for h in range(XH)]
    vms = [mkv_ref[:, XW + h * XD:XW + (h + 1) * XD] for h in range(XH)]
    return qs, gs, kms, vms


def _xattn_fwd(proj, mkv, gxq, gxk):
    s = proj.shape[0]

    def body(q0, q1, g0, g1, mkv_ref, gxq_ref, gxk_ref, o_ref):
        qs, gs, kms, vms = _xattn_load(q0, q1, g0, g1, mkv_ref)
        outs = _xattn_tile(qs, gs, kms, vms, gxq_ref[...], gxk_ref[...])
        for h in range(XH):
            o_ref[:, h * XD:(h + 1) * XD] = outs[h]

    return pl.pallas_call(
        body, name="xattn_fwd", grid=(s // XT,),
        in_specs=_xattn_specs(), out_specs=_bs((XT, XW), lambda i: (i, 0)),
        out_shape=jax.ShapeDtypeStruct((s, XW), F32),
        compiler_params=_params(("arbitrary",)),
    )(proj, proj, proj, proj, mkv, gxq, gxk)


def _xattn_bwd(proj, mkv, gxq, gxk, dy):
    s = proj.shape[0]

    def body(q0, q1, g0, g1, mkv_ref, gxq_ref, gxk_ref, dy_ref, dq_ref, dg_ref, dmkv_ref, dgxq_ref, dgxk_ref):
        @pl.when(pl.program_id(0) == 0)
        def _():
            dmkv_ref[...] = jnp.zeros_like(dmkv_ref)
            dgxq_ref[...] = jnp.zeros_like(dgxq_ref)
            dgxk_ref[...] = jnp.zeros_like(dgxk_ref)

        qs, gs, kms, vms = _xattn_load(q0, q1, g0, g1, mkv_ref)
        _, vjp = jax.vjp(_xattn_tile, qs, gs, kms, vms, gxq_ref[...], gxk_ref[...])
        dqs, dgs, dkms, dvms, dgxq, dgxk = vjp([dy_ref[:, h * XD:(h + 1) * XD] for h in range(XH)])
        for h in range(XH):
            dq_ref[:, h * XD:(h + 1) * XD] = dqs[h]
            dg_ref[:, h * XD:(h + 1) * XD] = dgs[h]
            dmkv_ref[:, h * XD:(h + 1) * XD] += dkms[h]
            dmkv_ref[:, XW + h * XD:XW + (h + 1) * XD] += dvms[h]
        dgxq_ref[...] += dgxq
        dgxk_ref[...] += dgxk

    whole = lambda shape: _bs(shape, lambda i: (0, 0))
    return pl.pallas_call(
        body, name="xattn_bwd", grid=(s // XT,),
        in_specs=_xattn_specs() + [_bs((XT, XW), lambda i: (i, 0))],
        out_specs=[_bs((XT, XW), lambda i: (i, 0)), _bs((XT, XW), lambda i: (i, 0)), whole((NMEM, 2 * XW)),
                   whole((1, XD)), whole((1, XD))],
        out_shape=[jax.ShapeDtypeStruct((s, XW), F32), jax.ShapeDtypeStruct((s, XW), F32),
                   jax.ShapeDtypeStruct((NMEM, 2 * XW), F32), jax.ShapeDtypeStruct((1, XD), F32),
                   jax.ShapeDtypeStruct((1, XD), F32)],
        compiler_params=_params(("arbitrary",)),
    )(proj, proj, proj, proj, mkv, gxq, gxk, dy)


def _mem_bwd(mem, g, dmn):
    def body(m_ref, dmn_ref, o_ref):
        m = m_ref[...]
        r = lax.rsqrt(jnp.mean(m * m, axis=-1, keepdims=True) + NORM_EPS)
        o_ref[...] = jnp.sum(dmn_ref[...] * m * r, axis=0, keepdims=True)

    del g
    return pl.pallas_call(body, name="mem_norm_bwd", out_shape=jax.ShapeDtypeStruct((1, D), F32),
                          compiler_params=_params())(mem, dmn)


SHIFT_W = 512


def _shift_rows(p, s):
    row = lax.broadcasted_iota(jnp.int32, p.shape, 0)
    prev = jnp.where(row == 0, 0.0, pltpu.roll(p, 1, 0))
    nxt = jnp.where(row == s - 1, 0.0, pltpu.roll(p, s - 1, 0))
    return prev, nxt


def _shift_fwd(proj, mu):
    s = proj.shape[0]

    def body(p_ref, mu_ref, o_ref):
        p = p_ref[...]
        prev, nxt = _shift_rows(p, s)
        o_ref[...] = p + mu_ref[...] * (0.5 * (prev + nxt) - p)

    return pl.pallas_call(
        body, name="shift_fwd", grid=(RSW // SHIFT_W,),
        in_specs=[_bs((s, SHIFT_W), lambda j: (0, C_RS // SHIFT_W + j)), _bs((1, SHIFT_W), lambda j: (0, j))],
        out_specs=_bs((s, SHIFT_W), lambda j: (0, j)),
        out_shape=jax.ShapeDtypeStruct((s, RSW), F32),
        compiler_params=_params(("parallel",)),
    )(proj, mu)


def _shift_bwd(proj, mu, dps):
    s = proj.shape[0]

    def body(p_ref, mu_ref, g_ref, o_ref, dmu_ref):
        p, g, mu_v = p_ref[...], g_ref[...], mu_ref[...]
        prev, nxt = _shift_rows(p, s)
        dmu_ref[...] = jnp.sum(g * (0.5 * (prev + nxt) - p), axis=0, keepdims=True)
        mg = mu_v * g
        down, up = _shift_rows(mg, s)
        o_ref[...] = g * (1.0 - mu_v) + 0.5 * (down + up)

    return pl.pallas_call(
        body, name="shift_bwd", grid=(RSW // SHIFT_W,),
        in_specs=[_bs((s, SHIFT_W), lambda j: (0, C_RS // SHIFT_W + j)), _bs((1, SHIFT_W), lambda j: (0, j)),
                  _bs((s, SHIFT_W), lambda j: (0, j))],
        out_specs=[_bs((s, SHIFT_W), lambda j: (0, j)), _bs((1, SHIFT_W), lambda j: (0, j))],
        out_shape=[jax.ShapeDtypeStruct((s, RSW), F32), jax.ShapeDtypeStruct((1, RSW), F32)],
        compiler_params=_params(("parallel",)),
    )(proj, mu, dps)


def _pre_tile(k, wf, wb, af, ab, k_k, k_a, w0s, w2s, a0s, a2s, seg):
    kx = k * k_k
    ss = _hdot(kx * kx, seg)
    kk = kx / jnp.maximum(jnp.sqrt(ss), 1e-12)
    outs = [kk]
    for d, (w_in, a_in) in enumerate(((wf, af), (wb, ab))):
        z = w0s[d] + _mm_nn(jnp.tanh(w_in), w2s[d])
        wd = -_softplus(-z) - 0.5
        dec = jnp.exp(-jnp.exp(wd))
        ad = jax.nn.sigmoid(a0s[d] + _mm_nn(a_in, a2s[d]))
        kd = k * (1.0 + (ad - 1.0) * k_a)
        outs += [dec, kd, kk * ad]
    return outs


PT = 256


def _pre_load(ps_ref, kk_ref, ka_ref, w0_ref, w2_ref, a0_ref, a2_ref):
    k = ps_ref[:, RW:2 * RW]
    wf, wb = ps_ref[:, 3 * RW:3 * RW + 64], ps_ref[:, 3 * RW + 64:3 * RW + 128]
    af, ab = ps_ref[:, 3 * RW + 128:3 * RW + 192], ps_ref[:, 3 * RW + 192:3 * RW + 256]
    w0s = [w0_ref[0:1, :], w0_ref[1:2, :]]
    a0s = [a0_ref[0:1, :], a0_ref[1:2, :]]
    w2s = [w2_ref[0], w2_ref[1]]
    a2s = [a2_ref[0], a2_ref[1]]
    return (k, wf, wb, af, ab, kk_ref[...], ka_ref[...], w0s, w2s, a0s, a2s)


def _pre_specs():
    c = lambda shape: _bs(shape, lambda i: tuple(0 for _ in shape))
    return [_bs((PT, RSW), lambda i: (i, 0)), c((1, RW)), c((1, RW)), c((2, RW)), c((2, 64, RW)), c((2, RW)),
            c((2, 64, RW))]


def _pre_fwd(ps, k_k, k_a, w0, w2, a0, a2):
    s = ps.shape[0]

    def body(ps_ref, kk_ref, ka_ref, w0_ref, w2_ref, a0_ref, a2_ref, *outs):
        args = _pre_load(ps_ref, kk_ref, ka_ref, w0_ref, w2_ref, a0_ref, a2_ref)
        res = _pre_tile(*args, _seg_matrix(RW, HD))
        for o_ref, v in zip(outs, res):
            o_ref[...] = v

    return pl.pallas_call(
        body, name="rwkv_pre_fwd", grid=(s // PT,),
        in_specs=_pre_specs(), out_specs=[_bs((PT, RW), lambda i: (i, 0))] * 7,
        out_shape=[jax.ShapeDtypeStruct((s, RW), F32)] * 7,
        compiler_params=_params(("parallel",)),
    )(ps, k_k, k_a, w0, w2, a0, a2)


def _pre_bwd(ps, k_k, k_a, w0, w2, a0, a2, dr, dv, cts):
    s = ps.shape[0]

    def body(ps_ref, kk_ref, ka_ref, w0_ref, w2_ref, a0_ref, a2_ref, dr_ref, dv_ref, c0, c1, c2, c3, c4, c5, c6,
             dps_ref, dkk_ref, dka_ref, dw0_ref, dw2_ref, da0_ref, da2_ref):
        @pl.when(pl.program_id(0) == 0)
        def _():
            for r in (dkk_ref, dka_ref, dw0_ref, dw2_ref, da0_ref, da2_ref):
                r[...] = jnp.zeros_like(r)

        args = _pre_load(ps_ref, kk_ref, ka_ref, w0_ref, w2_ref, a0_ref, a2_ref)
        seg = _seg_matrix(RW, HD)
        _, vjp = jax.vjp(lambda *a: _pre_tile(*a, seg), *args)
        dk, dwf, dwb, daf, dab, dk_k, dk_a, dw0s, dw2s, da0s, da2s = vjp([c[...] for c in (c0, c1, c2, c3, c4, c5, c6)])
        dps_ref[:, 0:RW] = dr_ref[...]
        dps_ref[:, RW:2 * RW] = dk
        dps_ref[:, 2 * RW:3 * RW] = dv_ref[...]
        for j, t in enumerate((dwf, dwb, daf, dab)):
            dps_ref[:, 3 * RW + 64 * j:3 * RW + 64 * (j + 1)] = t
        dkk_ref[...] += dk_k
        dka_ref[...] += dk_a
        for d in range(2):
            dw0_ref[d:d + 1, :] += dw0s[d]
            da0_ref[d:d + 1, :] += da0s[d]
            dw2_ref[d] += dw2s[d]
            da2_ref[d] += da2s[d]

    c = lambda shape: _bs(shape, lambda i: tuple(0 for _ in shape))
    row = _bs((PT, RW), lambda i: (i, 0))
    return pl.pallas_call(
        body, name="rwkv_pre_bwd", grid=(s // PT,),
        in_specs=_pre_specs() + [row] * 9,
        out_specs=[_bs((PT, RSW), lambda i: (i, 0)), c((1, RW)), c((1, RW)), c((2, RW)), c((2, 64, RW)), c((2, RW)),
                   c((2, 64, RW))],
        out_shape=[jax.ShapeDtypeStruct((s, RSW), F32), jax.ShapeDtypeStruct((1, RW), F32),
                   jax.ShapeDtypeStruct((1, RW), F32), jax.ShapeDtypeStruct((2, RW), F32),
                   jax.ShapeDtypeStruct((2, 64, RW), F32), jax.ShapeDtypeStruct((2, RW), F32),
                   jax.ShapeDtypeStruct((2, 64, RW), F32)],
        compiler_params=_params(("arbitrary",)),
    )(ps, k_k, k_a, w0, w2, a0, a2, dr, dv, *cts)


def _post_tile(y0, y1, r, v, kd0, kd1, rg, r_k, ln_w, ln_b, seg):
    ysum = y0 + y1
    bonus = (_hdot(r * kd0 * r_k, seg) + _hdot(r * kd1 * r_k, seg)) * v
    mean = _hdot(ysum, seg) * (1.0 / HD)
    cen = ysum - mean
    var = _hdot(cen * cen, seg) * (1.0 / HD)
    y = cen * lax.rsqrt(var + GN_EPS) * ln_w + ln_b + bonus
    return y * _silu(rg)


def _post_specs():
    row = _bs((PT, RW), lambda i: (i, 0))
    c = _bs((1, RW), lambda i: (0, 0))
    return [row, row, _bs((PT, RW), lambda i: (i, 0)), _bs((PT, RW), lambda i: (i, 2)), row, row,
            _bs((PT, RW), lambda i: (i, C_RG // RW)), c, c, c]


def _post_fwd(y0, y1, ps, kd0, kd1, proj, r_k, ln_w, ln_b):
    s = ps.shape[0]

    def body(y0_ref, y1_ref, r_ref, v_ref, kd0_ref, kd1_ref, rg_ref, rk_ref, lw_ref, lb_ref, o_ref):
        o_ref[...] = _post_tile(y0_ref[...], y1_ref[...], r_ref[...], v_ref[...], kd0_ref[...], kd1_ref[...],
                                rg_ref[...], rk_ref[...], lw_ref[...], lb_ref[...], _seg_matrix(RW, HD))

    return pl.pallas_call(
        body, name="rwkv_post_fwd", grid=(s // PT,),
        in_specs=_post_specs(), out_specs=_bs((PT, RW), lambda i: (i, 0)),
        out_shape=jax.ShapeDtypeStruct((s, RW), F32),
        compiler_params=_params(("parallel",)),
    )(y0, y1, ps, ps, kd0, kd1, proj, r_k, ln_w, ln_b)


def _post_bwd(y0, y1, ps, kd0, kd1, proj, r_k, ln_w, ln_b, dy):
    s = ps.shape[0]

    def body(y0_ref, y1_ref, r_ref, v_ref, kd0_ref, kd1_ref, rg_ref, rk_ref, lw_ref, lb_ref, dy_ref,
             dys_ref, dr_ref, dv_ref, dkd0_ref, dkd1_ref, drg_ref, drk_ref, dlw_ref, dlb_ref):
        @pl.when(pl.program_id(0) == 0)
        def _():
            for r in (drk_ref, dlw_ref, dlb_ref):
                r[...] = jnp.zeros_like(r)

        seg = _seg_matrix(RW, HD)
        args = [t[...] for t in (y0_ref, y1_ref, r_ref, v_ref, kd0_ref, kd1_ref, rg_ref, rk_ref, lw_ref, lb_ref)]
        _, vjp = jax.vjp(lambda *a: _post_tile(*a, seg), *args)
        dy0, _, dr, dv, dkd0, dkd1, drg, drk, dlw, dlb = vjp(dy_ref[...])
        dys_ref[...] = dy0
        dr_ref[...] = dr
        dv_ref[...] = dv
        dkd0_ref[...] = dkd0
        dkd1_ref[...] = dkd1
        drg_ref[...] = drg
        drk_ref[...] += drk
        dlw_ref[...] += dlw
        dlb_ref[...] += dlb

    row = _bs((PT, RW), lambda i: (i, 0))
    c = _bs((1, RW), lambda i: (0, 0))
    return pl.pallas_call(
        body, name="rwkv_post_bwd", grid=(s // PT,),
        in_specs=_post_specs() + [row], out_specs=[row] * 6 + [c] * 3,
        out_shape=[jax.ShapeDtypeStruct((s, RW), F32)] * 6 + [jax.ShapeDtypeStruct((1, RW), F32)] * 3,
        compiler_params=_params(("arbitrary",)),
    )(y0, y1, ps, ps, kd0, kd1, proj, r_k, ln_w, ln_b, dy)


def _ones2():
    r = lax.broadcasted_iota(jnp.int32, (256, 128), 0) % 128 // HD
    c = lax.broadcasted_iota(jnp.int32, (256, 128), 1) // HD
    return (r == c).astype(BF16)


def _split(p):
    hi = p.astype(BF16)
    lo = (p - hi.astype(F32)).astype(BF16)
    return jnp.concatenate([hi, lo], axis=1)


def _to_t8(a):
    s = a.shape[0]
    t = a.reshape(s // 8, 8, NPAIR, 2, HD).transpose(0, 2, 4, 3, 1)
    t = jnp.pad(t, ((0, 0), (0, 0), (0, 0), (0, 0), (0, HD - 8))).reshape(s // 8, NPAIR, HD, 128)
    hi = t.astype(BF16)
    lo = (t - hi.astype(F32)).astype(BF16)
    return jnp.concatenate([hi, lo], axis=-1)


def _from_t8(t8):
    g = t8.shape[0]
    t = t8.reshape(g, NPAIR, HD, 2, HD)[..., :8]
    return t.transpose(0, 4, 1, 3, 2).reshape(g * 8, RW)


def _scan_specs(direction, nc, fwd_order):
    def tb(c):
        sc = c if fwd_order else nc - 1 - c
        return sc if direction == 0 else nc - 1 - sc

    row = _bs((TC, RW), lambda c: (tb(c), 0))
    t8_in = _bs((TC // 8, NPAIR, HD, 256), lambda c: (tb(c), 0, 0, 0))
    t8_out = _bs((TC // 8, NPAIR, HD, 128), lambda c: (tb(c), 0, 0, 0))
    return row, t8_in, t8_out


def _put_t8(ref, g, u, tiles):
    for p in range(NPAIR):
        ref[g, p, :, u:u + 1] = tiles[p][:, u:u + 1]
        ref[g, p, :, HD + u:HD + u + 1] = tiles[p][:, HD + u:HD + u + 1]


def _scan_fwd(dec, kd, b, ps, kk, vl, direction):
    s = dec.shape[0]
    nc, ng = s // TC, TC // 8
    row, t8_in, t8_out = _scan_specs(direction, nc, True)
    n = NPAIR * HD

    def body(dec_ref, kd_ref, b_ref, r_ref, kk_ref, vl_ref, y8_ref, ck_ref, st):
        @pl.when(pl.program_id(0) == 0)
        def _():
            st[...] = jnp.zeros_like(st)

        ck_ref[0] = st[...]
        ones2 = _ones2()
        lane_u = lax.broadcasted_iota(jnp.int32, (HD, 256), 1) % HD
        tiles = lambda res, k: [res[k * n + p * HD:k * n + (p + 1) * HD] for p in range(NPAIR)]

        def group(gi, carry):
            g = gi if direction == 0 else ng - 1 - gi
            rows8 = pl.ds(pl.multiple_of(g * 8, 8), 8)
            d8, k8, b8, r8, kk8 = (q[rows8, :] for q in (dec_ref, kd_ref, b_ref, r_ref, kk_ref))
            pc = [slice(p * 128, (p + 1) * 128) for p in range(NPAIR)]
            ss = [st[p] for p in range(NPAIR)]
            u_prev = None
            for ui in range(8):
                u = ui if direction == 0 else 7 - ui
                lhs = [_split(ss[p] * kk8[u:u + 1, pc[p]]) for p in range(NPAIR)]
                for p in range(NPAIR):
                    vt = vl_ref[g, p]
                    lhs.append(jnp.where(lane_u == u, vt, jnp.zeros_like(vt)))
                if u_prev is not None:
                    lhs += [_split(ss[p] * r8[u_prev:u_prev + 1, pc[p]]) for p in range(NPAIR)]
                res = jnp.dot(jnp.concatenate(lhs, axis=0), ones2, preferred_element_type=F32)
                if u_prev is not None:
                    _put_t8(y8_ref, g, u_prev, tiles(res, 2))
                sa, vb = tiles(res, 0), tiles(res, 1)
                for p in range(NPAIR):
                    ss[p] = ss[p] * d8[u:u + 1, pc[p]] - sa[p] * b8[u:u + 1, pc[p]] + vb[p] * k8[u:u + 1, pc[p]]
                u_prev = u
            lhs = [_split(ss[p] * r8[u_prev:u_prev + 1, pc[p]]) for p in range(NPAIR)]
            res = jnp.dot(jnp.concatenate(lhs, axis=0), ones2, preferred_element_type=F32)
            _put_t8(y8_ref, g, u_prev, tiles(res, 0))
            for p in range(NPAIR):
                st[p] = ss[p]
            return carry

        lax.fori_loop(0, ng, group, 0)

    return pl.pallas_call(
        body, name=f"rwkv_scan_fwd{direction}", grid=(nc,),
        in_specs=[row, row, row, row, row, t8_in],
        out_specs=[t8_out, _bs((1, NPAIR, HD, 128), lambda c: (c, 0, 0, 0))],
        out_shape=[jax.ShapeDtypeStruct((s // 8, NPAIR, HD, 128), F32),
                   jax.ShapeDtypeStruct((nc, NPAIR, HD, 128), F32)],
        scratch_shapes=[pltpu.VMEM((NPAIR, HD, 128), F32)],
        compiler_params=_params(("arbitrary",)),
    )(dec, kd, b, ps, kk, vl)


def _scan_bwd(dec, kd, b, ps, kk, vl, dyl, ck, direction):
    s = dec.shape[0]
    nc, ng = s // TC, TC // 8
    row, t8_in, t8_out = _scan_specs(direction, nc, False)
    n = NPAIR * HD

    def body(dec_ref, kd_ref, b_ref, r_ref, kk_ref, vl_ref, dyl_ref, ck_ref,
             dr_ref, dd_ref, db_ref, dk_ref, dkk_ref, dv8_ref, st, sa_s, vb_s, dy_s, ds):
        @pl.when(pl.program_id(0) == 0)
        def _():
            ds[...] = jnp.zeros_like(ds)

        st[0] = ck_ref[0]
        ones2 = _ones2()
        lane_u = lax.broadcasted_iota(jnp.int32, (HD, 256), 1) % HD
        row_id = lax.broadcasted_iota(jnp.int32, (8, 128), 0)
        pc = [slice(p * 128, (p + 1) * 128) for p in range(NPAIR)]
        tiles = lambda res, k: [res[k * n + p * HD:k * n + (p + 1) * HD] for p in range(NPAIR)]

        def fgroup(gi, carry):
            g = gi if direction == 0 else ng - 1 - gi
            rows8 = pl.ds(pl.multiple_of(g * 8, 8), 8)
            d8, k8, b8, kk8 = (q[rows8, :] for q in (dec_ref, kd_ref, b_ref, kk_ref))
            ss = [st[gi * 8, p] for p in range(NPAIR)]
            for ui in range(8):
                u = ui if direction == 0 else 7 - ui
                i = gi * 8 + ui
                lhs = [_split(ss[p] * kk8[u:u + 1, pc[p]]) for p in range(NPAIR)]
                for ref in (vl_ref, dyl_ref):
                    for p in range(NPAIR):
                        t = ref[g, p]
                        lhs.append(jnp.where(lane_u == u, t, jnp.zeros_like(t)))
                res = jnp.dot(jnp.concatenate(lhs, axis=0), ones2, preferred_element_type=F32)
                sa, vb, dyb = tiles(res, 0), tiles(res, 1), tiles(res, 2)
                for p in range(NPAIR):
                    sa_s[i, p] = sa[p]
                    vb_s[i, p] = vb[p]
                    dy_s[i, p] = dyb[p]
                    ss[p] = ss[p] * d8[u:u + 1, pc[p]] - sa[p] * b8[u:u + 1, pc[p]] + vb[p] * k8[u:u + 1, pc[p]]
                    st[i + 1, p] = ss[p]
            return carry

        lax.fori_loop(0, ng, fgroup, 0)

        def bgroup(gj, carry):
            gi = ng - 1 - gj
            g = gi if direction == 0 else ng - 1 - gi
            rows8 = pl.ds(pl.multiple_of(g * 8, 8), 8)
            d8, k8, b8, r8, kk8 = (q[rows8, :] for q in (dec_ref, kd_ref, b_ref, r_ref, kk_ref))
            dss = [ds[p] for p in range(NPAIR)]
            acc = [[jnp.zeros((8, 128), F32) for _ in range(5)] for _ in range(NPAIR)]
            for uj in range(8):
                ui = 7 - uj
                u = ui if direction == 0 else 7 - ui
                i = gi * 8 + ui
                dyb = [dy_s[i, p] for p in range(NPAIR)]
                for p in range(NPAIR):
                    dss[p] = dss[p] + dyb[p] * r8[u:u + 1, pc[p]]
                lhs = [_split(dss[p] * b8[u:u + 1, pc[p]]) for p in range(NPAIR)]
                lhs += [_split(dss[p] * k8[u:u + 1, pc[p]]) for p in range(NPAIR)]
                res = jnp.dot(jnp.concatenate(lhs, axis=0), ones2, preferred_element_type=F32)
                dsa, dvb = tiles(res, 0), tiles(res, 1)
                _put_t8(dv8_ref, g, u, dvb)
                for p in range(NPAIR):
                    sp, sn = st[i, p], st[i + 1, p]
                    outs = (jnp.sum(sn * dyb[p], axis=0, keepdims=True), jnp.sum(dss[p] * sp, axis=0, keepdims=True),
                            -jnp.sum(dss[p] * sa_s[i, p], axis=0, keepdims=True),
                            jnp.sum(dss[p] * vb_s[i, p], axis=0, keepdims=True),
                            -jnp.sum(sp * dsa[p], axis=0, keepdims=True))
                    acc[p] = [jnp.where(row_id == u, o, a_) for o, a_ in zip(outs, acc[p])]
                    dss[p] = dss[p] * d8[u:u + 1, pc[p]] - dsa[p] * kk8[u:u + 1, pc[p]]
            for p in range(NPAIR):
                ds[p] = dss[p]
                for o_ref, a_ in zip((dr_ref, dd_ref, db_ref, dk_ref, dkk_ref), acc[p]):
                    o_ref[rows8, pc[p]] = a_
            return carry

        lax.fori_loop(0, ng, bgroup, 0)

    chunk = lambda k: pltpu.VMEM((k, NPAIR, HD, 128), F32)
    return pl.pallas_call(
        body, name=f"rwkv_scan_bwd{direction}", grid=(nc,),
        in_specs=[row, row, row, row, row, t8_in, t8_in, _bs((1, NPAIR, HD, 128), lambda c: (nc - 1 - c, 0, 0, 0))],
        out_specs=[row] * 5 + [t8_out],
        out_shape=[jax.ShapeDtypeStruct((s, RW), F32)] * 5 + [jax.ShapeDtypeStruct((s // 8, NPAIR, HD, 128), F32)],
        scratch_shapes=[chunk(TC + 1), chunk(TC), chunk(TC), chunk(TC), pltpu.VMEM((NPAIR, HD, 128), F32)],
        compiler_params=_params(("arbitrary",)),
    )(dec, kd, b, ps, kk, vl, dyl, ck)


def _tiles(res, k):
    n = NPAIR * HD
    return [res[k * n + p * HD:k * n + (p + 1) * HD] for p in range(NPAIR)]


def _scan2_fwd(per_dir, ps, kk, vl):
    s = ps.shape[0]
    nc, ng = s // TC, TC // 8
    in_specs, operands, out_specs, out_shape = [], [], [], []
    for d in (0, 1):
        row, t8_in, t8_out = _scan_specs(d, nc, True)
        in_specs += [row] * 5 + [t8_in]
        operands += list(per_dir[d]) + [ps, kk, vl]
        out_specs += [t8_out, _bs((1, NPAIR, HD, 128), lambda c: (c, 0, 0, 0))]
        out_shape += [jax.ShapeDtypeStruct((s // 8, NPAIR, HD, 128), F32),
                      jax.ShapeDtypeStruct((nc, NPAIR, HD, 128), F32)]

    def body(*refs):
        ins = [refs[0:6], refs[6:12]]
        y_refs, ck_refs, st = (refs[12], refs[14]), (refs[13], refs[15]), refs[16]

        @pl.when(pl.program_id(0) == 0)
        def _():
            st[...] = jnp.zeros_like(st)

        for d in (0, 1):
            ck_refs[d][0] = st[d * NPAIR:(d + 1) * NPAIR]
        ones2 = _ones2()
        lane_u = lax.broadcasted_iota(jnp.int32, (HD, 256), 1) % HD
        pc = [slice(p * 128, (p + 1) * 128) for p in range(NPAIR)]

        def group(gi, carry):
            gs = (gi, ng - 1 - gi)
            blk = [[q[pl.ds(pl.multiple_of(gs[d] * 8, 8), 8), :] for q in ins[d][:5]] for d in (0, 1)]
            ss = [[st[d * NPAIR + p] for p in range(NPAIR)] for d in (0, 1)]
            for ui in range(9):
                us, ups = (ui, 7 - ui), (ui - 1, 8 - ui)
                lhs = []
                for d in (0, 1):
                    _, _, _, r8, kk8 = blk[d]
                    if ui < 8:
                        lhs += [_split(ss[d][p] * kk8[us[d]:us[d] + 1, pc[p]]) for p in range(NPAIR)]
                        for p in range(NPAIR):
                            vt = ins[d][5][gs[d], p]
                            lhs.append(jnp.where(lane_u == us[d], vt, jnp.zeros_like(vt)))
                    if ui > 0:
                        lhs += [_split(ss[d][p] * r8[ups[d]:ups[d] + 1, pc[p]]) for p in range(NPAIR)]
                res = jnp.dot(jnp.concatenate(lhs, axis=0), ones2, preferred_element_type=F32)
                k = 0
                for d in (0, 1):
                    d8, k8, b8, _, _ = blk[d]
                    u = us[d]
                    if ui < 8:
                        sa, vb = _tiles(res, k), _tiles(res, k + 1)
                        k += 2
                    if ui > 0:
                        _put_t8(y_refs[d], gs[d], ups[d], _tiles(res, k))
                        k += 1
                    if ui < 8:
                        for p in range(NPAIR):
                            ss[d][p] = (ss[d][p] * d8[u:u + 1, pc[p]] - sa[p] * b8[u:u + 1, pc[p]]
                                        + vb[p] * k8[u:u + 1, pc[p]])
            for d in (0, 1):
                for p in range(NPAIR):
                    st[d * NPAIR + p] = ss[d][p]
            return carry

        lax.fori_loop(0, ng, group, 0)

    outs = pl.pallas_call(
        body, name="rwkv_scan_fwd", grid=(nc,), in_specs=in_specs, out_specs=out_specs, out_shape=out_shape,
        scratch_shapes=[pltpu.VMEM((2 * NPAIR, HD, 128), F32)],
        compiler_params=_params(("arbitrary",)),
    )(*operands)
    return [(outs[0], outs[1]), (outs[2], outs[3])]


def _scan2_bwd(per_dir, ps, kk, vl, dyl):
    s = ps.shape[0]
    nc, ng = s // TC, TC // 8
    in_specs, operands, out_specs, out_shape = [], [], [], []
    for d in (0, 1):
        row, t8_in, t8_out = _scan_specs(d, nc, False)
        dec, kd, b, ck = per_dir[d]
        in_specs += [row] * 5 + [t8_in, t8_in, _bs((1, NPAIR, HD, 128), lambda c: (nc - 1 - c, 0, 0, 0))]
        operands += [dec, kd, b, ps, kk, vl, dyl, ck]
        out_specs += [row] * 5 + [t8_out]
        out_shape += [jax.ShapeDtypeStruct((s, RW), F32)] * 5 + [jax.ShapeDtypeStruct((s // 8, NPAIR, HD, 128), F32)]

    def body(*refs):
        ins = [refs[0:8], refs[8:16]]
        outs = [refs[16:22], refs[22:28]]
        st, sa_s, vb_s, dy_s, ds = refs[28:]

        @pl.when(pl.program_id(0) == 0)
        def _():
            ds[...] = jnp.zeros_like(ds)

        for d in (0, 1):
            st[d * (TC + 1)] = ins[d][7][0]
        ones2 = _ones2()
        lane_u = lax.broadcasted_iota(jnp.int32, (HD, 256), 1) % HD
        row_id = lax.broadcasted_iota(jnp.int32, (8, 128), 0)
        pc = [slice(p * 128, (p + 1) * 128) for p in range(NPAIR)]

        def load_rows(gs):
            return [[q[pl.ds(pl.multiple_of(gs[d] * 8, 8), 8), :] for q in ins[d][:5]] for d in (0, 1)]

        def fgroup(gi, carry):
            gs = (gi, ng - 1 - gi)
            blk = load_rows(gs)
            ss = [[st[d * (TC + 1) + gi * 8, p] for p in range(NPAIR)] for d in (0, 1)]
            for ui in range(8):
                us = (ui, 7 - ui)
                i = gi * 8 + ui
                lhs = []
                for d in (0, 1):
                    kk8 = blk[d][4]
                    lhs += [_split(ss[d][p] * kk8[us[d]:us[d] + 1, pc[p]]) for p in range(NPAIR)]
                    for ref in (ins[d][5], ins[d][6]):
                        for p in range(NPAIR):
                            t = ref[gs[d], p]
                            lhs.append(jnp.where(lane_u == us[d], t, jnp.zeros_like(t)))
                res = jnp.dot(jnp.concatenate(lhs, axis=0), ones2, preferred_element_type=F32)
                for d in (0, 1):
                    d8, k8, b8, _, _ = blk[d]
                    u = us[d]
                    sa, vb, dyb = _tiles(res, 3 * d), _tiles(res, 3 * d + 1), _tiles(res, 3 * d + 2)
                    for p in range(NPAIR):
                        sa_s[d * TC + i, p] = sa[p]
                        vb_s[d * TC + i, p] = vb[p]
                        dy_s[d * TC + i, p] = dyb[p]
                        ss[d][p] = ss[d][p] * d8[u:u + 1, pc[p]] - sa[p] * b8[u:u + 1, pc[p]] + vb[p] * k8[u:u + 1, pc[p]]
                        st[d * (TC + 1) + i + 1, p] = ss[d][p]
            return carry

        lax.fori_loop(0, ng, fgroup, 0)

        def bgroup(gj, carry):
            gi = ng - 1 - gj
            gs = (gi, ng - 1 - gi)
            blk = load_rows(gs)
            dss = [[ds[d * NPAIR + p] for p in range(NPAIR)] for d in (0, 1)]
            acc = [[[jnp.zeros((8, 128), F32) for _ in range(5)] for _ in range(NPAIR)] for _ in (0, 1)]
            for uj in range(8):
                ui = 7 - uj
                us = (ui, 7 - ui)
                i = gi * 8 + ui
                lhs_b, lhs_k, dyb = [], [], [None, None]
                for d in (0, 1):
                    _, k8, b8, r8, _ = blk[d]
                    u = us[d]
                    dyb[d] = [dy_s[d * TC + i, p] for p in range(NPAIR)]
                    for p in range(NPAIR):
                        dss[d][p] = dss[d][p] + dyb[d][p] * r8[u:u + 1, pc[p]]
                    lhs_b += [_split(dss[d][p] * b8[u:u + 1, pc[p]]) for p in range(NPAIR)]
                    lhs_b += [_split(dss[d][p] * k8[u:u + 1, pc[p]]) for p in range(NPAIR)]
                res = jnp.dot(jnp.concatenate(lhs_b + lhs_k, axis=0), ones2, preferred_element_type=F32)
                for d in (0, 1):
                    d8, _, _, _, kk8 = blk[d]
                    u = us[d]
                    dsa, dvb = _tiles(res, 2 * d), _tiles(res, 2 * d + 1)
                    _put_t8(outs[d][5], gs[d], u, dvb)
                    for p in range(NPAIR):
                        sp, sn = st[d * (TC + 1) + i, p], st[d * (TC + 1) + i + 1, p]
                        dsv = dss[d][p]
                        vals = (jnp.sum(sn * dyb[d][p], axis=0, keepdims=True), jnp.sum(dsv * sp, axis=0, keepdims=True),
                                -jnp.sum(dsv * sa_s[d * TC + i, p], axis=0, keepdims=True),
                                jnp.sum(dsv * vb_s[d * TC + i, p], axis=0, keepdims=True),
                                -jnp.sum(sp * dsa[p], axis=0, keepdims=True))
                        acc[d][p] = [jnp.where(row_id == u, o, a_) for o, a_ in zip(vals, acc[d][p])]
                        dss[d][p] = dsv * d8[u:u + 1, pc[p]] - dsa[p] * kk8[u:u + 1, pc[p]]
            for d in (0, 1):
                rows8 = pl.ds(pl.multiple_of(gs[d] * 8, 8), 8)
                for p in range(NPAIR):
                    ds[d * NPAIR + p] = dss[d][p]
                    for o_ref, a_ in zip(outs[d][:5], acc[d][p]):
                        o_ref[rows8, pc[p]] = a_
            return carry

        lax.fori_loop(0, ng, bgroup, 0)

    chunk = lambda k: pltpu.VMEM((k, NPAIR, HD, 128), F32)
    res = pl.pallas_call(
        body, name="rwkv_scan_bwd", grid=(nc,), in_specs=in_specs, out_specs=out_specs, out_shape=out_shape,
        scratch_shapes=[chunk(2 * (TC + 1)), chunk(2 * TC), chunk(2 * TC), chunk(2 * TC),
                        pltpu.VMEM((2 * NPAIR, HD, 128), F32)],
        compiler_params=_params(("arbitrary",)),
    )(*operands)
    return [res[0:6], res[6:12]]


MT = 256
MN = 256


def _merge_fwd(ya, yr, yx, wa, wr, wx, proj, gate_b):
    s = ya.shape[0]

    def body(ya_ref, yr_ref, yx_ref, wa_ref, wr_ref, wx_ref, m0, m1, m2, b0, b1, b2, o_ref):
        acc = jnp.zeros((MT, MN), F32)
        for y_ref, w_ref, m_ref, b_ref in ((ya_ref, wa_ref, m0, b0), (yr_ref, wr_ref, m1, b1), (yx_ref, wx_ref, m2, b2)):
            u = _dot(y_ref[...], w_ref[...], ((1,), (0,)))
            acc = acc + jax.nn.sigmoid(m_ref[...] + b_ref[...]) * u
        o_ref[...] = acc.astype(BF16)

    mg = lambda br: _bs((MT, MN), lambda i, j: (i, C_MG // MN + br * (D // MN) + j))
    gb = lambda br: _bs((1, MN), lambda i, j: (0, br * (D // MN) + j))
    return pl.pallas_call(
        body, name="merge_fwd", grid=(s // MT, D // MN),
        in_specs=[_bs((MT, RW), lambda i, j: (i, 0)), _bs((MT, RW), lambda i, j: (i, 0)), _bs((MT, XW), lambda i, j: (i, 0)),
                  _bs((RW, MN), lambda i, j: (0, j)), _bs((RW, MN), lambda i, j: (0, j)), _bs((XW, MN), lambda i, j: (0, j)),
                  mg(0), mg(1), mg(2), gb(0), gb(1), gb(2)],
        out_specs=_bs((MT, MN), lambda i, j: (i, j)),
        out_shape=jax.ShapeDtypeStruct((s, D), BF16),
        compiler_params=_params(("parallel", "arbitrary")),
    )(ya, yr, yx, wa, wr, wx, proj, proj, proj, gate_b, gate_b, gate_b)


def _out_fwd(merged, w_out, x, target):
    s = x.shape[0]
    tm, tn = min(512, s), 512

    def body(m_ref, w_ref, x_ref, t_ref, loss_ref, d_ref):
        @pl.when((pl.program_id(0) == 0) & (pl.program_id(1) == 0))
        def _():
            loss_ref[...] = jnp.zeros_like(loss_ref)

        out = x_ref[...] + jnp.dot(m_ref[...], w_ref[...], preferred_element_type=F32)
        err = out - t_ref[...]
        d_ref[...] = err * (1.0 / D)
        loss_ref[...] += jnp.sum(err * err)

    return pl.pallas_call(
        body, name="out_fwd", grid=(s // tm, D // tn),
        in_specs=[_bs((tm, D), lambda i, j: (i, 0)), _bs((D, tn), lambda i, j: (0, j)),
                  _bs((tm, tn), lambda i, j: (i, j)), _bs((tm, tn), lambda i, j: (i, j))],
        out_specs=[_bs((8, 128), lambda i, j: (0, 0)), _bs((tm, tn), lambda i, j: (i, j))],
        out_shape=[jax.ShapeDtypeStruct((8, 128), F32), jax.ShapeDtypeStruct((s, D), F32)],
        compiler_params=_params(("arbitrary", "arbitrary")),
    )(merged, w_out, x, target)


def _merge_bwd(ya, yr, yx, wa, wr, wx, proj, gate_b, dmerged):
    s = ya.shape[0]

    def body(ya_ref, yr_ref, yx_ref, wa_ref, wr_ref, wx_ref, m0, m1, m2, b0, b1, b2, dm_ref,
             dg0, dg1, dg2, du0, du1, du2, dya_ref, dyr_ref, dyx_ref):
        @pl.when(pl.program_id(1) == 0)
        def _():
            dya_ref[...] = jnp.zeros_like(dya_ref)
            dyr_ref[...] = jnp.zeros_like(dyr_ref)
            dyx_ref[...] = jnp.zeros_like(dyx_ref)

        dm = dm_ref[...]
        for y_ref, w_ref, m_ref, b_ref, dg_ref, du_ref, dy_ref in (
                (ya_ref, wa_ref, m0, b0, dg0, du0, dya_ref), (yr_ref, wr_ref, m1, b1, dg1, du1, dyr_ref),
                (yx_ref, wx_ref, m2, b2, dg2, du2, dyx_ref)):
            w = w_ref[...]
            u = _dot(y_ref[...], w, ((1,), (0,)))
            gt = jax.nn.sigmoid(m_ref[...] + b_ref[...])
            dg_ref[...] = (dm * u * gt * (1.0 - gt)).astype(BF16)
            du = (dm * gt).astype(BF16)
            du_ref[...] = du
            dy_ref[...] += _dot(du, w, ((1,), (1,)))

    mg = lambda br: _bs((MT, MN), lambda i, j: (i, C_MG // MN + br * (D // MN) + j))
    gb = lambda br: _bs((1, MN), lambda i, j: (0, br * (D // MN) + j))
    tile = _bs((MT, MN), lambda i, j: (i, j))
    return pl.pallas_call(
        body, name="merge_bwd", grid=(s // MT, D // MN),
        in_specs=[_bs((MT, RW), lambda i, j: (i, 0)), _bs((MT, RW), lambda i, j: (i, 0)), _bs((MT, XW), lambda i, j: (i, 0)),
                  _bs((RW, MN), lambda i, j: (0, j)), _bs((RW, MN), lambda i, j: (0, j)), _bs((XW, MN), lambda i, j: (0, j)),
                  mg(0), mg(1), mg(2), gb(0), gb(1), gb(2), tile],
        out_specs=[tile] * 6 + [_bs((MT, RW), lambda i, j: (i, 0)), _bs((MT, RW), lambda i, j: (i, 0)),
                                _bs((MT, XW), lambda i, j: (i, 0))],
        out_shape=[jax.ShapeDtypeStruct((s, D), BF16)] * 6 + [jax.ShapeDtypeStruct((s, RW), F32),
                                                               jax.ShapeDtypeStruct((s, RW), F32),
                                                               jax.ShapeDtypeStruct((s, XW), F32)],
        compiler_params=_params(("parallel", "arbitrary")),
    )(ya, yr, yx, wa, wr, wx, proj, proj, proj, gate_b, gate_b, gate_b, dmerged)


def _colsum(a, name):
    m, n = a.shape
    tm, tn = min(512, m), 512

    def body(a_ref, o_ref):
        @pl.when(pl.program_id(1) == 0)
        def _():
            o_ref[...] = jnp.zeros_like(o_ref)

        o_ref[...] += jnp.sum(a_ref[...].astype(F32), axis=0, keepdims=True)

    return pl.pallas_call(
        body, name=name, grid=(n // tn, m // tm),
        in_specs=[_bs((tm, tn), lambda j, i: (i, j))], out_specs=_bs((1, tn), lambda j, i: (0, j)),
        out_shape=jax.ShapeDtypeStruct((1, n), F32),
        compiler_params=_params(("parallel", "arbitrary")),
    )(a)


def _in_bwd(dproj, w_in, x, g, dout):
    s = x.shape[0]
    tm, tk = 256, 896
    nk = NIN // tk

    def body(dp_ref, w_ref, x_ref, g_ref, do_ref, gx_ref, gg_ref, acc):
        i, kk = pl.program_id(0), pl.program_id(1)

        @pl.when((i == 0) & (kk == 0))
        def _():
            gg_ref[...] = jnp.zeros_like(gg_ref)

        @pl.when(kk == 0)
        def _():
            acc[...] = jnp.zeros_like(acc)

        acc[...] += _dot(dp_ref[...], w_ref[...], ((1,), (1,)))

        @pl.when(kk == nk - 1)
        def _():
            xv, dh, gv = x_ref[...], acc[...], g_ref[...]
            r = lax.rsqrt(jnp.mean(xv * xv, axis=-1, keepdims=True) + NORM_EPS)
            xn = xv * r
            gg_ref[...] += jnp.sum(dh * xn, axis=0, keepdims=True)
            dxn = dh * gv
            dx = r * (dxn - xn * jnp.mean(dxn * xn, axis=-1, keepdims=True))
            gx_ref[...] = do_ref[...] + dx

    return pl.pallas_call(
        body, name="in_bwd", grid=(s // tm, nk),
        in_specs=[_bs((tm, tk), lambda i, kk: (i, kk)), _bs((D, tk), lambda i, kk: (0, kk)),
                  _bs((tm, D), lambda i, kk: (i, 0)), _bs((1, D), lambda i, kk: (0, 0)), _bs((tm, D), lambda i, kk: (i, 0))],
        out_specs=[_bs((tm, D), lambda i, kk: (i, 0)), _bs((1, D), lambda i, kk: (0, 0))],
        out_shape=[jax.ShapeDtypeStruct((s, D), F32), jax.ShapeDtypeStruct((1, D), F32)],
        scratch_shapes=[pltpu.VMEM((tm, D), F32)],
        compiler_params=_params(("arbitrary", "arbitrary")),
    )(dproj, w_in, x, g, dout)


def _adamw_math(w, g, m, v):
    m = ADAM_B1 * m + (1.0 - ADAM_B1) * g
    v = ADAM_B2 * v + (1.0 - ADAM_B2) * jnp.square(g)
    m_hat = m / (1.0 - ADAM_B1 ** ADAM_STEP)
    v_hat = v / (1.0 - ADAM_B2 ** ADAM_STEP)
    delta = -ADAM_LR * (m_hat / (jnp.sqrt(v_hat) + ADAM_EPS) + ADAM_WD * w)
    return delta, m, v


def _adamw(parts, w, m, v, name):
    rows, cols = w.shape
    tr = rows
    for cand in (256, 128, 64, 32, 16, 8):
        if rows % cand == 0 and cand * cols * 4 <= (1 << 20):
            tr = cand
            break
    n = len(parts)

    def body(*refs):
        g = refs[0][...].astype(F32)
        for r in refs[1:n]:
            g = g + r[...].astype(F32)
        w_ref, m_ref, v_ref, g_out, d_out, m_out, v_out = refs[n:]
        delta, m_new, v_new = _adamw_math(w_ref[...], g, m_ref[...], v_ref[...])
        g_out[...] = g
        d_out[...] = delta
        m_out[...] = m_new
        v_out[...] = v_new

    spec = _bs((tr, cols), lambda i: (i, 0))
    return pl.pallas_call(
        body, name=name, grid=(rows // tr,),
        in_specs=[spec] * (n + 3), out_specs=[spec] * 4,
        out_shape=[jax.ShapeDtypeStruct((rows, cols), F32)] * 4,
        compiler_params=_params(("parallel",)),
    )(*parts, w, m, v)


def _adamw_halves(mine, theirs, core, w, m, v, name):
    rows, cols = w.shape
    h = rows // 2
    tr = next(t for t in (256, 128, 64, 32, 16, 8) if h % t == 0 and t * cols * 4 <= (1 << 20))
    nt = h // tr

    def body(core_ref, mine_ref, theirs_ref, w_ref, m_ref, v_ref, g_out, d_out, m_out, v_out):
        is_mine = pl.program_id(0) // nt == core_ref[0]
        g = jnp.where(is_mine, mine_ref[...], theirs_ref[...])
        delta, m_new, v_new = _adamw_math(w_ref[...], g, m_ref[...], v_ref[...])
        g_out[...] = g
        d_out[...] = delta
        m_out[...] = m_new
        v_out[...] = v_new

    spec = _bs((tr, cols), lambda i, core_ref: (i, 0))
    return pl.pallas_call(
        body, name=name,
        grid_spec=pltpu.PrefetchScalarGridSpec(
            num_scalar_prefetch=1, grid=(2 * nt,),
            in_specs=[_bs((tr, cols), lambda i, core_ref: (jnp.clip(i - core_ref[0] * nt, 0, nt - 1), 0)),
                      _bs((tr, cols), lambda i, core_ref: (jnp.clip(i - (1 - core_ref[0]) * nt, 0, nt - 1), 0)),
                      spec, spec, spec],
            out_specs=[spec] * 4),
        out_shape=[jax.ShapeDtypeStruct((rows, cols), F32)] * 4,
        compiler_params=_params(("parallel",)),
    )(core, mine, theirs, w, m, v)


def _sum_parts(parts, name):
    rows, cols = parts[0].shape
    tr = rows
    for cand in (256, 128, 64, 32, 16, 8):
        if rows % cand == 0 and cand * cols * 4 <= (1 << 20):
            tr = cand
            break

    def body(*refs):
        acc = refs[0][...].astype(F32)
        for r in refs[1:-1]:
            acc = acc + r[...].astype(F32)
        refs[-1][...] = acc

    spec = _bs((tr, cols), lambda i: (i, 0))
    return pl.pallas_call(
        body, name=name, grid=(rows // tr,), in_specs=[spec] * len(parts), out_specs=spec,
        out_shape=jax.ShapeDtypeStruct((rows, cols), F32), compiler_params=_params(("parallel",)),
    )(*parts)


ANY = pl.BlockSpec(memory_space=pl.ANY)


def _other_chips(x, y):
    return [(1 - x, y), (x, 1 - y), (1 - x, 1 - y)]


def _gather_shards(arrays, name):
    n = len(arrays)

    def body(*refs):
        ins, outs = refs[:n], refs[n:2 * n]
        ici_send, ici_recv, d2d_send, d2d_recv, local_sems, own_recv = refs[2 * n:]
        x, y, c = lax.axis_index("x"), lax.axis_index("y"), lax.axis_index("c")
        me = 2 * x + y
        chips = _other_chips(x, y)

        def half(i, who):
            h = arrays[i].shape[0] // 2
            return pl.ds(who * h, h)

        def ici(i, j, src_chip, to):
            return pltpu.make_async_remote_copy(
                src_ref=ins[i].at[half(i, c)], dst_ref=outs[i].at[src_chip, half(i, c)], send_sem=ici_send.at[3 * i + j],
                recv_sem=ici_recv.at[3 * i + j], device_id=to, device_id_type=MESH)

        def d2d(i, j, src_chip, who):
            piece = outs[i].at[src_chip, half(i, who)]
            return pltpu.make_async_remote_copy(
                src_ref=piece, dst_ref=piece, send_sem=d2d_send.at[3 * i + j], recv_sem=d2d_recv.at[3 * i + j],
                device_id=(x, y, 1 - c), device_id_type=MESH)

        def own(i):
            return pltpu.make_async_remote_copy(
                src_ref=ins[i], dst_ref=outs[i].at[me], send_sem=local_sems.at[i], recv_sem=own_recv.at[i],
                device_id=(x, y, 1 - c), device_id_type=MESH)

        sends = []
        for i in range(n):
            cp = own(i)
            cp.start()
            sends.append(cp)
            for j, (px, py) in enumerate(chips):
                rc = ici(i, j, me, (px, py, c))
                rc.start()
                sends.append(rc)
        for i in range(n):
            for j, (px, py) in enumerate(chips):
                ici(i, j, 2 * px + py, (px, py, c)).wait_recv()
                fw = d2d(i, j, 2 * px + py, c)
                fw.start()
                sends.append(fw)
        for i in range(n):
            for j, (px, py) in enumerate(chips):
                d2d(i, j, 2 * px + py, 1 - c).wait_recv()
            own(i).wait_recv()
        for rc in sends:
            rc.wait_send()

    dma = lambda k: pltpu.SemaphoreType.DMA((k,))
    return pl.pallas_call(
        body, name=name, in_specs=[ANY] * n, out_specs=[ANY] * n,
        out_shape=[jax.ShapeDtypeStruct((4,) + a.shape, a.dtype) for a in arrays],
        scratch_shapes=[dma(3 * n), dma(3 * n), dma(3 * n), dma(3 * n), dma(n), dma(n)],
        compiler_params=pltpu.CompilerParams(has_side_effects=True),
    )(*arrays)


def _scatter_shards(stacks, name):
    n = len(stacks)

    def body(*refs):
        ins, outs = refs[:n], refs[n:2 * n]
        send_sems, recv_sems = refs[2 * n:]
        x, y, c = lax.axis_index("x"), lax.axis_index("y"), lax.axis_index("c")
        chips = _other_chips(x, y)
        sends = []
        for i in range(n):
            for j, (px, py) in enumerate(chips):
                rc = pltpu.make_async_remote_copy(
                    src_ref=ins[i].at[2 * px + py], dst_ref=outs[i].at[j], send_sem=send_sems.at[3 * i + j],
                    recv_sem=recv_sems.at[3 * i + j], device_id=(px, py, c), device_id_type=MESH)
                rc.start()
                sends.append(rc)
        for rc in sends:
            rc.wait_recv()
        for rc in sends:
            rc.wait_send()

    return pl.pallas_call(
        body, name=name, in_specs=[ANY] * n, out_specs=[ANY] * n,
        out_shape=[jax.ShapeDtypeStruct((3,) + a.shape[1:], a.dtype) for a in stacks],
        scratch_shapes=[pltpu.SemaphoreType.DMA((3 * n,)), pltpu.SemaphoreType.DMA((3 * n,))],
        compiler_params=pltpu.CompilerParams(has_side_effects=True),
    )(*stacks)


def _pair_exchange(stacks, name):
    n = len(stacks)

    def body(*refs):
        ins, outs = refs[:n], refs[n:2 * n]
        send_sems, recv_sems = refs[2 * n:]
        x, y, c = lax.axis_index("x"), lax.axis_index("y"), lax.axis_index("c")
        cps = []
        for i in range(n):
            h = stacks[i].shape[1] // 2
            rc = pltpu.make_async_remote_copy(
                src_ref=ins[i].at[:, pl.ds((1 - c) * h, h)], dst_ref=outs[i], send_sem=send_sems.at[i],
                recv_sem=recv_sems.at[i], device_id=(x, y, 1 - c), device_id_type=MESH)
            rc.start()
            cps.append(rc)
        for rc in cps:
            rc.wait_recv()
        for rc in cps:
            rc.wait_send()

    return pl.pallas_call(
        body, name=name, in_specs=[ANY] * n, out_specs=[ANY] * n,
        out_shape=[jax.ShapeDtypeStruct((4, a.shape[1] // 2) + a.shape[2:], a.dtype) for a in stacks],
        scratch_shapes=[pltpu.SemaphoreType.DMA((n,)), pltpu.SemaphoreType.DMA((n,))],
        compiler_params=pltpu.CompilerParams(has_side_effects=True),
    )(*stacks)


def _pair_sum(own, theirs, core, name):
    _, r, cols = own.shape
    h = r // 2
    tr = next(t for t in (256, 128, 64, 32, 16) if h % t == 0 and t * cols * 4 <= (1 << 20))
    nt = h // tr

    def body(core_ref, own_ref, th_ref, o32_ref, o16_ref):
        del core_ref
        acc = own_ref[...] + th_ref[...].astype(F32)
        o32_ref[...] = acc
        o16_ref[...] = acc.astype(BF16)

    out = _bs((1, tr, cols), lambda j, t, core_ref: (j, t, 0))
    return pl.pallas_call(
        body, name=name,
        grid_spec=pltpu.PrefetchScalarGridSpec(
            num_scalar_prefetch=1, grid=(4, nt),
            in_specs=[_bs((1, tr, cols), lambda j, t, core_ref: (j, core_ref[0] * nt + t, 0)), out],
            out_specs=[out, out]),
        out_shape=[jax.ShapeDtypeStruct((4, h, cols), F32), jax.ShapeDtypeStruct((4, h, cols), BF16)],
        compiler_params=_params(("parallel", "parallel")),
    )(core, own, theirs)


def _swap_sibling(arrays, name):
    n = len(arrays)

    def body(*refs):
        ins, outs = refs[:n], refs[n:2 * n]
        send_sems, recv_sems = refs[2 * n:]
        sib = (lax.axis_index("x"), lax.axis_index("y"), 1 - lax.axis_index("c"))
        cps = []
        for i in range(n):
            rc = pltpu.make_async_remote_copy(src_ref=ins[i], dst_ref=outs[i], send_sem=send_sems.at[i],
                                              recv_sem=recv_sems.at[i], device_id=sib, device_id_type=MESH)
            rc.start()
            cps.append(rc)
        for rc in cps:
            rc.wait_recv()
        for rc in cps:
            rc.wait_send()

    return pl.pallas_call(
        body, name=name, in_specs=[ANY] * n, out_specs=[ANY] * n,
        out_shape=[jax.ShapeDtypeStruct(a.shape, a.dtype) for a in arrays],
        scratch_shapes=[pltpu.SemaphoreType.DMA((n,)), pltpu.SemaphoreType.DMA((n,))],
        compiler_params=pltpu.CompilerParams(has_side_effects=True),
    )(*arrays)


def _all_reduce_small(v):
    rows = v.shape[0]

    def body(v_ref, o_ref, buf, send_sems, recv_sems):
        x, y, c = lax.axis_index("x"), lax.axis_index("y"), lax.axis_index("c")
        me = 4 * x + 2 * y + c
        buf[me] = v_ref[...]
        cps = []
        for kbits in range(1, 8):
            bx, by, bc = (kbits >> 2) & 1, (kbits >> 1) & 1, kbits & 1
            px = jnp.where(bx == 1, 1 - x, x)
            py = jnp.where(by == 1, 1 - y, y)
            pc = jnp.where(bc == 1, 1 - c, c)
            rc = pltpu.make_async_remote_copy(src_ref=v_ref, dst_ref=buf.at[me], send_sem=send_sems.at[kbits - 1],
                                              recv_sem=recv_sems.at[kbits - 1], device_id=(px, py, pc),
                                              device_id_type=MESH)
            rc.start()
            cps.append((rc, 4 * px + 2 * py + pc))
        for kbits, (rc, src) in enumerate(cps):
            pltpu.make_async_remote_copy(src_ref=v_ref, dst_ref=buf.at[src], send_sem=send_sems.at[kbits],
                                         recv_sem=recv_sems.at[kbits], device_id=(x, y, c),
                                         device_id_type=MESH).wait_recv()
        for rc, _ in cps:
            rc.wait_send()
        acc = buf[0]
        for d in range(1, 8):
            acc = acc + buf[d]
        o_ref[...] = acc

    return pl.pallas_call(
        body, name="all_reduce_small",
        in_specs=[pl.BlockSpec(memory_space=pltpu.VMEM)], out_specs=pl.BlockSpec(memory_space=pltpu.VMEM),
        out_shape=jax.ShapeDtypeStruct((rows, 128), F32),
        scratch_shapes=[pltpu.VMEM((8, rows, 128), F32), pltpu.SemaphoreType.DMA((7,)), pltpu.SemaphoreType.DMA((7,))],
        compiler_params=pltpu.CompilerParams(has_side_effects=True, vmem_limit_bytes=VMEM_LIMIT),
    )(v)


def _rope_tables(s):
    half = HD // 2
    inv = 10000.0 ** (-jnp.arange(half, dtype=F32) / half)
    ang = jnp.arange(s, dtype=F32)[:, None] * inv[None, :]
    cos, sin = jnp.cos(ang), jnp.sin(ang)
    return jnp.concatenate([cos, cos], axis=1), jnp.concatenate([sin, sin], axis=1)


def _local_step(x, mem, target, norm_g, mem_norm_g, w_in, gate_b, gq, gk, sink, wa, mu, k_k, k_a, r_k, w0, w2, a0, a2,
                ln_w, ln_b, wr, w_kv, gxq, gxk, wx, w_out):
    s = x.shape[0]
    cos, sin = _rope_tables(s)
    r_k = r_k.reshape(1, RW)

    proj, h = _proj_fwd(x, norm_g, w_in)
    ya = _attn_fwd(proj, cos, sin, gq, gk, sink)
    mkv, mn = _mem_kv(mem, mem_norm_g, w_kv)
    yx = _xattn_fwd(proj, mkv, gxq, gxk)
    ps = _shift_fwd(proj, mu)
    kk, dec0, kd0, b0, dec1, kd1, b1 = _pre_fwd(ps, k_k, k_a, w0, w2, a0, a2)
    v8 = _to_t8(ps[:, 2 * RW:3 * RW])
    (y80, ck0), (y81, ck1) = _scan2_fwd([(dec0, kd0, b0), (dec1, kd1, b1)], ps, kk, v8)
    y0, y1 = _from_t8(y80), _from_t8(y81)
    yr = _post_fwd(y0, y1, ps, kd0, kd1, proj, r_k, ln_w, ln_b)
    merged = _merge_fwd(ya, yr, yx, wa, wr, wx, proj, gate_b)
    loss_tile, dout = _out_fwd(merged, w_out, x, target)
    loss_sum = loss_tile[0, 0]

    g = {}
    dmerged = _matmul(dout, w_out, mode="nt", m=s, n=D, k=D, tm=min(512, s), tn=512, tk=512, name="dmerged")
    g["w_out"] = _matmul(merged, dout, mode="tn", m=D, n=D, k=s, tm=512, tn=512, tk=min(512, s), name="grad_w_out")
    dg0, dg1, dg2, du0, du1, du2, dya, dyr, dyx = _merge_bwd(ya, yr, yx, wa, wr, wx, proj, gate_b, dmerged)
    g["attn_w_o"] = _matmul(ya, du0, mode="tn", m=RW, n=D, k=s, tm=RW, tn=512, tk=min(512, s), name="grad_attn_w_o")
    g["rwkv_w_o"] = _matmul(yr, du1, mode="tn", m=RW, n=D, k=s, tm=RW, tn=512, tk=min(512, s), name="grad_rwkv_w_o")
    g["x_w_o"] = _matmul(yx, du2, mode="tn", m=XW, n=D, k=s, tm=XW, tn=512, tk=min(512, s), name="grad_x_w_o")
    dmg = jnp.concatenate([dg0, dg1, dg2], axis=1)
    g["gate_b"] = _colsum(dmg, "grad_gate_b")

    daq, dak, dav, dag, g["attn_q_norm_g"], g["attn_k_norm_g"], g["attn_sink"] = _attn_bwd(proj, cos, sin, gq, gk, sink, dya)

    dxq, dxg, dmkv, g["x_q_norm_g"], g["x_k_norm_g"] = _xattn_bwd(proj, mkv, gxq, gxk, dyx)
    g["x_w_kv"] = _matmul(mn, dmkv, mode="tn", m=D, n=2 * XW, k=NMEM, tm=512, tn=512, tk=NMEM, name="grad_x_w_kv")
    dmn = _matmul(dmkv, w_kv, mode="nt", m=NMEM, n=D, k=2 * XW, tm=NMEM, tn=512, tk=2 * XW, name="dmn")
    g["mem_norm_g"] = _mem_bwd(mem, mem_norm_g, dmn)

    dys, dr_p, dv_p, dkd0_p, dkd1_p, drg, g["rwkv_r_k"], g["rwkv_ln_w"], g["rwkv_ln_b"] = _post_bwd(
        y0, y1, ps, kd0, kd1, proj, r_k, ln_w, ln_b, dyr)
    dy8 = _to_t8(dys)
    (dr0, dd0, db0, dk0, dkk0, dv80), (dr1, dd1, db1, dk1, dkk1, dv81) = _scan2_bwd(
        [(dec0, kd0, b0, ck0), (dec1, kd1, b1, ck1)], ps, kk, v8, dy8)
    dr = dr_p + dr0 + dr1
    dv = dv_p + _from_t8(dv80) + _from_t8(dv81)
    cts = (dkk0 + dkk1, dd0, dk0 + dkd0_p, db0, dd1, dk1 + dkd1_p, db1)
    dps, g["rwkv_k_k"], g["rwkv_k_a"], g["rwkv_w0"], g["rwkv_w2"], g["rwkv_a0"], g["rwkv_a2"] = _pre_bwd(
        ps, k_k, k_a, w0, w2, a0, a2, dr, dv, cts)
    drs, g["rwkv_mu"] = _shift_bwd(proj, mu, dps)

    dproj = jnp.concatenate([daq.astype(BF16), dak.astype(BF16), dav.astype(BF16), dag.astype(BF16), drs.astype(BF16),
                             drg.astype(BF16), dxq.astype(BF16), dxg.astype(BF16), dmg], axis=1)
    g["w_in"] = _matmul(h, dproj, mode="tn", m=D, n=NIN, k=s, tm=512, tn=896, tk=min(512, s), name="grad_w_in")
    grad_x, g["norm_g"] = _in_bwd(dproj, w_in, x, norm_g, dout)
    g["rwkv_r_k"] = g["rwkv_r_k"].reshape(AH, HD)
    return loss_sum, grad_x, g


WEIGHTS = ['norm_g', 'mem_norm_g', 'w_in', 'gate_b', 'attn_q_norm_g', 'attn_k_norm_g', 'attn_sink', 'attn_w_o',
           'rwkv_mu', 'rwkv_k_k', 'rwkv_k_a', 'rwkv_r_k', 'rwkv_w0', 'rwkv_w2', 'rwkv_a0', 'rwkv_a2', 'rwkv_ln_w',
           'rwkv_ln_b', 'rwkv_w_o', 'x_w_kv', 'x_q_norm_g', 'x_k_norm_g', 'x_w_o', 'w_out']
BIG = ['w_in', 'attn_w_o', 'rwkv_w_o', 'x_w_kv', 'x_w_o', 'w_out']
COL_SHARDED = ['w_in', 'attn_w_o', 'rwkv_w_o', 'x_w_o']
LORA = ['rwkv_w0', 'rwkv_w2', 'rwkv_a0', 'rwkv_a2']
SMALL = [n for n in WEIGHTS if n not in BIG]


def _unshard_cols(stack):
    return jnp.concatenate([stack[i] for i in range(4)], axis=-1)


def _shard_cols(full):
    w = full.shape[-1] // 4
    return [full[..., i * w:(i + 1) * w] for i in range(4)]


def kernel(x, mem, norm_g, mem_norm_g, w_in, gate_b, attn_q_norm_g, attn_k_norm_g, attn_sink, attn_w_o, rwkv_mu, rwkv_k_k, rwkv_k_a, rwkv_r_k, rwkv_w0, rwkv_w2, rwkv_a0, rwkv_a2, rwkv_ln_w, rwkv_ln_b, rwkv_w_o, x_w_kv, x_q_norm_g, x_k_norm_g, x_w_o, w_out, loss_target, m_norm_g, m_mem_norm_g, m_w_in, m_gate_b, m_attn_q_norm_g, m_attn_k_norm_g, m_attn_sink, m_attn_w_o, m_rwkv_mu, m_rwkv_k_k, m_rwkv_k_a, m_rwkv_r_k, m_rwkv_w0, m_rwkv_w2, m_rwkv_a0, m_rwkv_a2, m_rwkv_ln_w, m_rwkv_ln_b, m_rwkv_w_o, m_x_w_kv, m_x_q_norm_g, m_x_k_norm_g, m_x_w_o, m_w_out, v_norm_g, v_mem_norm_g, v_w_in, v_gate_b, v_attn_q_norm_g, v_attn_k_norm_g, v_attn_sink, v_attn_w_o, v_rwkv_mu, v_rwkv_k_k, v_rwkv_k_a, v_rwkv_r_k, v_rwkv_w0, v_rwkv_w2, v_rwkv_a0, v_rwkv_a2, v_rwkv_ln_w, v_rwkv_ln_b, v_rwkv_w_o, v_x_w_kv, v_x_q_norm_g, v_x_k_norm_g, v_x_w_o, v_w_out):
    args = dict(locals())
    canon = lambda a: a[0] if a.ndim > 2 else a
    w = {n: canon(args[n]) for n in WEIGHTS}
    m = {n: canon(args["m_" + n]) for n in WEIGHTS}
    v = {n: canon(args["v_" + n]) for n in WEIGHTS}
    shard = 2 * lax.axis_index("x") + lax.axis_index("y")

    local = [w[n].astype(BF16) for n in BIG] + [w[n].reshape(2, -1, w[n].shape[-1]) for n in LORA]
    stacks = dict(zip(BIG + LORA, _gather_shards(local, "gather_weights")))
    full = {}
    for n in COL_SHARDED:
        full[n] = _unshard_cols(stacks[n])
    for n in LORA:
        full[n] = _unshard_cols(stacks[n]).reshape(w[n].shape[:-1] + (RW,))
    full["x_w_kv"] = stacks["x_w_kv"].reshape(D, 2 * XW)
    full["w_out"] = stacks["w_out"].reshape(D, D)

    loss_sum, grad_x, g = _local_step(
        x[0], mem[0], loss_target[0], w["norm_g"], w["mem_norm_g"], full["w_in"], w["gate_b"], w["attn_q_norm_g"],
        w["attn_k_norm_g"], w["attn_sink"], full["attn_w_o"], w["rwkv_mu"], w["rwkv_k_k"], w["rwkv_k_a"], w["rwkv_r_k"],
        full["rwkv_w0"], full["rwkv_w2"], full["rwkv_a0"], full["rwkv_a2"], w["rwkv_ln_w"], w["rwkv_ln_b"],
        full["rwkv_w_o"], full["x_w_kv"], w["x_q_norm_g"], w["x_k_norm_g"], full["x_w_o"], full["w_out"])

    loss = lax.psum(0.5 * loss_sum / D, ("x", "y", "c"))

    def as_stack(n, dtype):
        if n in COL_SHARDED:
            return jnp.stack([p.astype(dtype) for p in _shard_cols(g[n])])
        return g[n].reshape((4, g[n].shape[0] // 4) + g[n].shape[1:]).astype(dtype)

    core = lax.axis_index("c").astype(jnp.int32).reshape(1)
    sibling = _pair_exchange([as_stack(n, BF16) for n in BIG], "pair_exchange")
    pair32, pair16 = [], []
    for n, th in zip(BIG, sibling):
        a32, a16 = _pair_sum(as_stack(n, F32), th, core, "pair_sum_" + n)
        pair32.append(a32)
        pair16.append(a16)
    recv = _scatter_shards(pair16, "scatter_grads")
    halves = []
    for n, p32, r in zip(BIG, pair32, recv):
        own = lax.dynamic_index_in_dim(p32, shard, 0, keepdims=False)
        halves.append(_sum_parts([own, r[0], r[1], r[2]], "sum_" + n))
    other_halves = _swap_sibling(halves, "swap_halves")

    out_g, out_d, out_m, out_v = {}, {}, {}, {}
    for n, mine, theirs in zip(BIG, halves, other_halves):
        out_g[n], out_d[n], out_m[n], out_v[n] = _adamw_halves(mine, theirs, core, w[n], m[n], v[n], "adamw_" + n)

    flat = jnp.concatenate([g[n].reshape(-1) for n in SMALL])
    total = flat.shape[0]
    padded = -(-total // 1024) * 1024
    flat = jnp.pad(flat, (0, padded - total)).reshape(padded // 128, 128)
    red = _all_reduce_small(flat).reshape(-1)
    off = 0
    gs = {}
    for n in SMALL:
        size = g[n].size
        t = red[off:off + size].reshape(g[n].shape)
        off += size
        if n in LORA:
            wd = t.shape[-1] // 4
            t = lax.dynamic_slice_in_dim(t, shard * wd, wd, axis=t.ndim - 1)
        gs[n] = t

    def pack(d):
        f = jnp.concatenate([d[n].reshape(-1) for n in SMALL])
        return jnp.pad(f, (0, -(-f.shape[0] // 1024) * 1024 - f.shape[0])).reshape(-1, 128)

    pg, pd, pm, pv = _adamw([pack(gs)], pack(w), pack(m), pack(v), "adamw_small")
    off = 0
    for n in SMALL:
        size = w[n].size
        for dst, src in ((out_g, pg), (out_d, pd), (out_m, pm), (out_v, pv)):
            dst[n] = src.reshape(-1)[off:off + size].reshape(w[n].shape)
        off += size

    lead = lambda d: [d[n][None] if args[n].ndim > 2 else d[n] for n in WEIGHTS]
    return (loss, grad_x[None], *lead(out_g), *lead(out_d), *lead(out_m), *lead(out_v))
```

```python
import functools

import jax
import jax.numpy as jnp
from jax import lax
from jax.experimental import pallas as pl
from jax.experimental.pallas import tpu as pltpu

F32 = jnp.float32
BF16 = jnp.bfloat16
HI = lax.Precision.HIGHEST
MESH = pl.DeviceIdType.MESH

D = 2048
NMEM = 256
NORM_EPS = 1e-6
NEG_INF = -1e30
GN_EPS = 64e-5
HD = 64
AH = 12
AKV = 4
RW = 768
XH = 4
XD = 128
XW = 512
NIN = 12544
RSW = 2560
C_AQ, C_AK, C_AV, C_AG, C_RS, C_RG, C_XQ, C_XG, C_MG = 0, 768, 1024, 1280, 2048, 4608, 5376, 5888, 6400
WIN = 384
QB = 128
TC = 16
NPAIR = 6

ADAM_LR, ADAM_B1, ADAM_B2, ADAM_EPS, ADAM_WD, ADAM_STEP = 0.001, 0.9, 0.999, 1e-08, 0.01, 10

VMEM_LIMIT = 56 * 1024 * 1024


def _bs(shape, imap):
    return pl.BlockSpec(shape, imap)


def _params(sem=None, vmem=VMEM_LIMIT):
    return pltpu.CompilerParams(dimension_semantics=sem, vmem_limit_bytes=vmem)


def _dot(a, b, dims):
    return lax.dot_general(a.astype(BF16), b.astype(BF16), (dims, ((), ())), preferred_element_type=F32)


@jax.custom_vjp
def _mm_nn(a, b):
    return _dot(a, b, ((1,), (0,)))


def _mm_nn_fwd(a, b):
    return _mm_nn(a, b), (a, b)


def _mm_nn_bwd(res, ct):
    a, b = res
    return _dot(ct, b, ((1,), (1,))), _dot(a, ct, ((0,), (0,)))


_mm_nn.defvjp(_mm_nn_fwd, _mm_nn_bwd)


@jax.custom_vjp
def _mm_nt(a, b):
    return _dot(a, b, ((1,), (1,)))


def _mm_nt_fwd(a, b):
    return _mm_nt(a, b), (a, b)


def _mm_nt_bwd(res, ct):
    a, b = res
    return _dot(ct, b, ((1,), (0,))), _dot(ct, a, ((0,), (0,)))


_mm_nt.defvjp(_mm_nt_fwd, _mm_nt_bwd)


def _seg_matrix(n, seg):
    r = lax.broadcasted_iota(jnp.int32, (n, n), 0) // seg
    c = lax.broadcasted_iota(jnp.int32, (n, n), 1) // seg
    return (r == c).astype(F32)


def _rot_matrix():
    r = lax.broadcasted_iota(jnp.int32, (HD, HD), 0)
    c = lax.broadcasted_iota(jnp.int32, (HD, HD), 1)
    return jnp.where(c == r + HD // 2, 1.0, 0.0).astype(F32) - jnp.where(c == r - HD // 2, 1.0, 0.0).astype(F32)


def _hdot(a, m):
    return jnp.dot(a, m, precision=HI, preferred_element_type=F32)


def _rms(t, g):
    return t * lax.rsqrt(jnp.mean(t * t, axis=-1, keepdims=True) + NORM_EPS) * g


def _silu(t):
    return t * jax.nn.sigmoid(t)


def _softplus(z):
    return jnp.maximum(z, 0.0) + jnp.log(1.0 + jnp.exp(-jnp.abs(z)))


def _matmul(a, b, *, mode, m, n, k, tm, tn, tk, name, a_off=(0, 0), b_off=(0, 0), out_dtype=F32):
    nk = k // tk
    if mode == "tn":
        a_spec = _bs((tk, tm), lambda i, j, kk: (kk + a_off[0], i + a_off[1]))
        dims = ((0,), (0,))
    else:
        a_spec = _bs((tm, tk), lambda i, j, kk: (i + a_off[0], kk + a_off[1]))
        dims = ((1,), (1,)) if mode == "nt" else ((1,), (0,))
    if mode == "nt":
        b_spec = _bs((tn, tk), lambda i, j, kk: (j + b_off[0], kk + b_off[1]))
    else:
        b_spec = _bs((tk, tn), lambda i, j, kk: (kk + b_off[0], j + b_off[1]))

    def body(a_ref, b_ref, o_ref, acc):
        kk = pl.program_id(2)

        @pl.when(kk == 0)
        def _():
            acc[...] = jnp.zeros_like(acc)

        acc[...] += _dot(a_ref[...], b_ref[...], dims)

        @pl.when(kk == nk - 1)
        def _():
            o_ref[...] = acc[...].astype(out_dtype)

    return pl.pallas_call(
        body, name=name, grid=(m // tm, n // tn, nk),
        in_specs=[a_spec, b_spec], out_specs=_bs((tm, tn), lambda i, j, kk: (i, j)),
        out_shape=jax.ShapeDtypeStruct((m, n), out_dtype),
        scratch_shapes=[pltpu.VMEM((tm, tn), F32)],
        compiler_params=_params(("parallel", "parallel", "arbitrary")),
    )(a, b)


def _proj_fwd(x, g, w):
    s = x.shape[0]
    tm, tn = min(512, s), 896

    def body(x_ref, g_ref, w_ref, o_ref, h_ref, hs):
        @pl.when(pl.program_id(1) == 0)
        def _():
            h = _rms(x_ref[...], g_ref[...]).astype(BF16)
            hs[...] = h
            h_ref[...] = h

        o_ref[...] = jnp.dot(hs[...], w_ref[...], preferred_element_type=F32)

    return pl.pallas_call(
        body, name="proj_fwd", grid=(s // tm, NIN // tn),
        in_specs=[_bs((tm, D), lambda i, j: (i, 0)), _bs((1, D), lambda i, j: (0, 0)), _bs((D, tn), lambda i, j: (0, j))],
        out_specs=[_bs((tm, tn), lambda i, j: (i, j)), _bs((tm, D), lambda i, j: (i, 0))],
        out_shape=[jax.ShapeDtypeStruct((s, NIN), F32), jax.ShapeDtypeStruct((s, D), BF16)],
        scratch_shapes=[pltpu.VMEM((tm, D), BF16)],
        compiler_params=_params(("parallel", "arbitrary")),
    )(x, g, w)


def _rope(t, cos, sin, rot):
    return t * cos + _hdot(t, rot) * sin


def _attn_tile(qs, ks, vs, gs, sinks, gq, gk, cq, sq, ck, sk, mask, rot):
    outs = []
    for hk in range(AKV):
        kh = _rope(_rms(ks[hk], gk), ck, sk, rot)
        for g in range(AH // AKV):
            h = hk * (AH // AKV) + g
            qh = _rope(_rms(qs[h], gq), cq, sq, rot)
            sc = _mm_nt(qh, kh) * (HD ** -0.5)
            sc = jnp.where(mask, sc, NEG_INF)
            mx = lax.stop_gradient(jnp.maximum(jnp.max(sc, axis=-1, keepdims=True), sinks[h]))
            p = jnp.exp(sc - mx)
            den = jnp.sum(p, axis=-1, keepdims=True) + jnp.exp(sinks[h] - mx)
            o = _mm_nn(p / den, vs[hk])
            outs.append(o * _silu(gs[h]))
    return outs


def _attn_load(n, s, aq_ref, ak_ref, av_ref, ag_refs, cos_ref, sin_ref, sink_ref):
    start = pl.multiple_of(jnp.clip((n - 1) * QB, 0, s - WIN), QB)
    q0 = pl.multiple_of(n * QB, QB)
    qs = [aq_ref[:, h * HD:(h + 1) * HD] for h in range(AH)]
    ks = [ak_ref[pl.ds(start, WIN), h * HD:(h + 1) * HD] for h in range(AKV)]
    vs = [av_ref[pl.ds(start, WIN), h * HD:(h + 1) * HD] for h in range(AKV)]
    gs = [ag_refs[h // 4][:, (h % 4) * HD:(h % 4 + 1) * HD] for h in range(AH)]
    sinks = [sink_ref[0:1, h:h + 1] for h in range(AH)]
    cq, sq = cos_ref[pl.ds(q0, QB), :], sin_ref[pl.ds(q0, QB), :]
    ck, sk = cos_ref[pl.ds(start, WIN), :], sin_ref[pl.ds(start, WIN), :]
    qpos = q0 + lax.broadcasted_iota(jnp.int32, (QB, WIN), 0)
    kpos = start + lax.broadcasted_iota(jnp.int32, (QB, WIN), 1)
    mask = jnp.abs(kpos - qpos) <= QB
    return start, qs, ks, vs, gs, sinks, cq, sq, ck, sk, mask


def _attn_specs(s):
    return [
        _bs((QB, 768), lambda n: (n, 0)),
        _bs((s, 256), lambda n: (0, C_AK // 256)),
        _bs((s, 256), lambda n: (0, C_AV // 256)),
        _bs((QB, 256), lambda n: (n, C_AG // 256)),
        _bs((QB, 256), lambda n: (n, C_AG // 256 + 1)),
        _bs((QB, 256), lambda n: (n, C_AG // 256 + 2)),
        _bs((s, HD), lambda n: (0, 0)),
        _bs((s, HD), lambda n: (0, 0)),
        _bs((1, HD), lambda n: (0, 0)),
        _bs((1, HD), lambda n: (0, 0)),
        _bs((1, AH), lambda n: (0, 0)),
    ]


def _attn_fwd(proj, cos, sin, gq, gk, sink):
    s = proj.shape[0]

    def body(aq_ref, ak_ref, av_ref, ag0, ag1, ag2, cos_ref, sin_ref, gq_ref, gk_ref, sink_ref, o_ref):
        n = pl.program_id(0)
        _, qs, ks, vs, gs, sinks, cq, sq, ck, sk, mask = _attn_load(
            n, s, aq_ref, ak_ref, av_ref, (ag0, ag1, ag2), cos_ref, sin_ref, sink_ref)
        outs = _attn_tile(qs, ks, vs, gs, sinks, gq_ref[...], gk_ref[...], cq, sq, ck, sk, mask, _rot_matrix())
        for h in range(AH):
            o_ref[:, h * HD:(h + 1) * HD] = outs[h]

    return pl.pallas_call(
        body, name="attn_fwd", grid=(s // QB,),
        in_specs=_attn_specs(s), out_specs=_bs((QB, 768), lambda n: (n, 0)),
        out_shape=jax.ShapeDtypeStruct((s, 768), F32),
        compiler_params=_params(("arbitrary",)),
    )(proj, proj, proj, proj, proj, proj, cos, sin, gq, gk, sink)


def _attn_bwd(proj, cos, sin, gq, gk, sink, dy):
    s = proj.shape[0]

    def body(aq_ref, ak_ref, av_ref, ag0, ag1, ag2, cos_ref, sin_ref, gq_ref, gk_ref, sink_ref, dy_ref,
             daq_ref, dak_ref, dav_ref, dag_ref, dgq_ref, dgk_ref, dsink_ref):
        n = pl.program_id(0)

        @pl.when(n == 0)
        def _():
            dak_ref[...] = jnp.zeros_like(dak_ref)
            dav_ref[...] = jnp.zeros_like(dav_ref)
            dgq_ref[...] = jnp.zeros_like(dgq_ref)
            dgk_ref[...] = jnp.zeros_like(dgk_ref)
            dsink_ref[...] = jnp.zeros_like(dsink_ref)

        start, qs, ks, vs, gs, sinks, cq, sq, ck, sk, mask = _attn_load(
            n, s, aq_ref, ak_ref, av_ref, (ag0, ag1, ag2), cos_ref, sin_ref, sink_ref)
        rot = _rot_matrix()

        def f(qs, ks, vs, gs, sinks, gq, gk):
            return _attn_tile(qs, ks, vs, gs, sinks, gq, gk, cq, sq, ck, sk, mask, rot)

        _, vjp = jax.vjp(f, qs, ks, vs, gs, sinks, gq_ref[...], gk_ref[...])
        dys = [dy_ref[:, h * HD:(h + 1) * HD] for h in range(AH)]
        dqs, dks, dvs, dgs, dsinks, dgq, dgk = vjp(dys)
        for h in range(AH):
            daq_ref[:, h * HD:(h + 1) * HD] = dqs[h]
            dag_ref[:, h * HD:(h + 1) * HD] = dgs[h]
            dsink_ref[0:1, h:h + 1] += dsinks[h]
        for h in range(AKV):
            dak_ref[pl.ds(start, WIN), h * HD:(h + 1) * HD] += dks[h]
            dav_ref[pl.ds(start, WIN), h * HD:(h + 1) * HD] += dvs[h]
        dgq_ref[...] += dgq
        dgk_ref[...] += dgk

    whole = lambda shape: _bs(shape, lambda n: (0, 0))
    return pl.pallas_call(
        body, name="attn_bwd", grid=(s // QB,),
        in_specs=_attn_specs(s) + [_bs((QB, 768), lambda n: (n, 0))],
        out_specs=[_bs((QB, 768), lambda n: (n, 0)), whole((s, 256)), whole((s, 256)), _bs((QB, 768), lambda n: (n, 0)),
                   whole((1, HD)), whole((1, HD)), whole((1, AH))],
        out_shape=[jax.ShapeDtypeStruct((s, 768), F32), jax.ShapeDtypeStruct((s, 256), F32),
                   jax.ShapeDtypeStruct((s, 256), F32), jax.ShapeDtypeStruct((s, 768), F32),
                   jax.ShapeDtypeStruct((1, HD), F32), jax.ShapeDtypeStruct((1, HD), F32),
                   jax.ShapeDtypeStruct((1, AH), F32)],
        compiler_params=_params(("arbitrary",)),
    )(proj, proj, proj, proj, proj, proj, cos, sin, gq, gk, sink, dy)


def _mem_kv(mem, g, w):
    def body(m_ref, g_ref, w_ref, o_ref, mn_ref):
        mn = _rms(m_ref[...], g_ref[...]).astype(BF16)
        mn_ref[...] = mn
        o_ref[...] = jnp.dot(mn, w_ref[...], preferred_element_type=F32)

    return pl.pallas_call(
        body, name="mem_kv",
        out_shape=[jax.ShapeDtypeStruct((NMEM, 2 * XW), F32), jax.ShapeDtypeStruct((NMEM, D), BF16)],
        compiler_params=_params(),
    )(mem, g, w)


def _xattn_tile(qs, gs, kms, vms, gxq, gxk):
    outs = []
    for h in range(XH):
        q = _rms(qs[h], gxq)
        km = _rms(kms[h], gxk)
        sc = _mm_nt(q, km) * (XD ** -0.5)
        mx = lax.stop_gradient(jnp.max(sc, axis=-1, keepdims=True))
        p = jnp.exp(sc - mx)
        p = p / jnp.sum(p, axis=-1, keepdims=True)
        outs.append(_mm_nn(p, vms[h]) * _silu(gs[h]))
    return outs


XT = 256


def _xattn_specs():
    return [
        _bs((XT, 256), lambda i: (i, C_XQ // 256)), _bs((XT, 256), lambda i: (i, C_XQ // 256 + 1)),
        _bs((XT, 256), lambda i: (i, C_XG // 256)), _bs((XT, 256), lambda i: (i, C_XG // 256 + 1)),
        _bs((NMEM, 2 * XW), lambda i: (0, 0)),
        _bs((1, XD), lambda i: (0, 0)), _bs((1, XD), lambda i: (0, 0)),
    ]


def _xattn_load(q0, q1, g0, g1, mkv_ref):
    qs = [(q0, q1)[h // 2][:, (h % 2) * XD:(h % 2 + 1) * XD] for h in range(XH)]
    gs = [(g0, g1)[h // 2][:, (h % 2) * XD:(h % 2 + 1) * XD] for h in range(XH)]
    kms = [mkv_ref[:, h * XD:(h + 1) * XD] for h in range(XH)]
    vms = [mkv_ref[:, XW + h * XD:XW + (h + 1) * XD] for h in range(XH)]
    return qs, gs, kms, vms


def _xattn_fwd(proj, mkv, gxq, gxk):
    s = proj.shape[0]

    def body(q0, q1, g0, g1, mkv_ref, gxq_ref, gxk_ref, o_ref):
        qs, gs, kms, vms = _xattn_load(q0, q1, g0, g1, mkv_ref)
        outs = _xattn_tile(qs, gs, kms, vms, gxq_ref[...], gxk_ref[...])
        for h in range(XH):
            o_ref[:, h * XD:(h + 1) * XD] = outs[h]

    return pl.pallas_call(
        body, name="xattn_fwd", grid=(s // XT,),
        in_specs=_xattn_specs(), out_specs=_bs((XT, XW), lambda i: (i, 0)),
        out_shape=jax.ShapeDtypeStruct((s, XW), F32),
        compiler_params=_params(("arbitrary",)),
    )(proj, proj, proj, proj, mkv, gxq, gxk)


def _xattn_bwd(proj, mkv, gxq, gxk, dy):
    s = proj.shape[0]

    def body(q0, q1, g0, g1, mkv_ref, gxq_ref, gxk_ref, dy_ref, dq_ref, dg_ref, dmkv_ref, dgxq_ref, dgxk_ref):
        @pl.when(pl.program_id(0) == 0)
        def _():
            dmkv_ref[...] = jnp.zeros_like(dmkv_ref)
            dgxq_ref[...] = jnp.zeros_like(dgxq_ref)
            dgxk_ref[...] = jnp.zeros_like(dgxk_ref)

        qs, gs, kms, vms = _xattn_load(q0, q1, g0, g1, mkv_ref)
        _, vjp = jax.vjp(_xattn_tile, qs, gs, kms, vms, gxq_ref[...], gxk_ref[...])
        dqs, dgs, dkms, dvms, dgxq, dgxk = vjp([dy_ref[:, h * XD:(h + 1) * XD] for h in range(XH)])
        for h in range(XH):
            dq_ref[:, h * XD:(h + 1) * XD] = dqs[h]
            dg_ref[:, h * XD:(h + 1) * XD] = dgs[h]
            dmkv_ref[:, h * XD:(h + 1) * XD] += dkms[h]
            dmkv_ref[:, XW + h * XD:XW + (h + 1) * XD] += dvms[h]
        dgxq_ref[...] += dgxq
        dgxk_ref[...] += dgxk

    whole = lambda shape: _bs(shape, lambda i: (0, 0))
    return pl.pallas_call(
        body, name="xattn_bwd", grid=(s // XT,),
        in_specs=_xattn_specs() + [_bs((XT, XW), lambda i: (i, 0))],
        out_specs=[_bs((XT, XW), lambda i: (i, 0)), _bs((XT, XW), lambda i: (i, 0)), whole((NMEM, 2 * XW)),
                   whole((1, XD)), whole((1, XD))],
        out_shape=[jax.ShapeDtypeStruct((s, XW), F32), jax.ShapeDtypeStruct((s, XW), F32),
                   jax.ShapeDtypeStruct((NMEM, 2 * XW), F32), jax.ShapeDtypeStruct((1, XD), F32),
                   jax.ShapeDtypeStruct((1, XD), F32)],
        compiler_params=_params(("arbitrary",)),
    )(proj, proj, proj, proj, mkv, gxq, gxk, dy)


def _mem_bwd(mem, g, dmn):
    def body(m_ref, dmn_ref, o_ref):
        m = m_ref[...]
        r = lax.rsqrt(jnp.mean(m * m, axis=-1, keepdims=True) + NORM_EPS)
        o_ref[...] = jnp.sum(dmn_ref[...] * m * r, axis=0, keepdims=True)

    del g
    return pl.pallas_call(body, name="mem_norm_bwd", out_shape=jax.ShapeDtypeStruct((1, D), F32),
                          compiler_params=_params())(mem, dmn)


SHIFT_W = 512


def _shift_rows(p, s):
    row = lax.broadcasted_iota(jnp.int32, p.shape, 0)
    prev = jnp.where(row == 0, 0.0, pltpu.roll(p, 1, 0))
    nxt = jnp.where(row == s - 1, 0.0, pltpu.roll(p, s - 1, 0))
    return prev, nxt


def _shift_fwd(proj, mu):
    s = proj.shape[0]

    def body(p_ref, mu_ref, o_ref):
        p = p_ref[...]
        prev, nxt = _shift_rows(p, s)
        o_ref[...] = p + mu_ref[...] * (0.5 * (prev + nxt) - p)

    return pl.pallas_call(
        body, name="shift_fwd", grid=(RSW // SHIFT_W,),
        in_specs=[_bs((s, SHIFT_W), lambda j: (0, C_RS // SHIFT_W + j)), _bs((1, SHIFT_W), lambda j: (0, j))],
        out_specs=_bs((s, SHIFT_W), lambda j: (0, j)),
        out_shape=jax.ShapeDtypeStruct((s, RSW), F32),
        compiler_params=_params(("parallel",)),
    )(proj, mu)


def _shift_bwd(proj, mu, dps):
    s = proj.shape[0]

    def body(p_ref, mu_ref, g_ref, o_ref, dmu_ref):
        p, g, mu_v = p_ref[...], g_ref[...], mu_ref[...]
        prev, nxt = _shift_rows(p, s)
        dmu_ref[...] = jnp.sum(g * (0.5 * (prev + nxt) - p), axis=0, keepdims=True)
        mg = mu_v * g
        down, up = _shift_rows(mg, s)
        o_ref[...] = g * (1.0 - mu_v) + 0.5 * (down + up)

    return pl.pallas_call(
        body, name="shift_bwd", grid=(RSW // SHIFT_W,),
        in_specs=[_bs((s, SHIFT_W), lambda j: (0, C_RS // SHIFT_W + j)), _bs((1, SHIFT_W), lambda j: (0, j)),
                  _bs((s, SHIFT_W), lambda j: (0, j))],
        out_specs=[_bs((s, SHIFT_W), lambda j: (0, j)), _bs((1, SHIFT_W), lambda j: (0, j))],
        out_shape=[jax.ShapeDtypeStruct((s, RSW), F32), jax.ShapeDtypeStruct((1, RSW), F32)],
        compiler_params=_params(("parallel",)),
    )(proj, mu, dps)


def _pre_tile(k, wf, wb, af, ab, k_k, k_a, w0s, w2s, a0s, a2s, seg):
    kx = k * k_k
    ss = _hdot(kx * kx, seg)
    kk = kx / jnp.maximum(jnp.sqrt(ss), 1e-12)
    outs = [kk]
    for d, (w_in, a_in) in enumerate(((wf, af), (wb, ab))):
        z = w0s[d] + _mm_nn(jnp.tanh(w_in), w2s[d])
        wd = -_softplus(-z) - 0.5
        dec = jnp.exp(-jnp.exp(wd))
        ad = jax.nn.sigmoid(a0s[d] + _mm_nn(a_in, a2s[d]))
        kd = k * (1.0 + (ad - 1.0) * k_a)
        outs += [dec, kd, kk * ad]
    return outs


PT = 256


def _pre_load(ps_ref, kk_ref, ka_ref, w0_ref, w2_ref, a0_ref, a2_ref):
    k = ps_ref[:, RW:2 * RW]
    wf, wb = ps_ref[:, 3 * RW:3 * RW + 64], ps_ref[:, 3 * RW + 64:3 * RW + 128]
    af, ab = ps_ref[:, 3 * RW + 128:3 * RW + 192], ps_ref[:, 3 * RW + 192:3 * RW + 256]
    w0s = [w0_ref[0:1, :], w0_ref[1:2, :]]
    a0s = [a0_ref[0:1, :], a0_ref[1:2, :]]
    w2s = [w2_ref[0], w2_ref[1]]
    a2s = [a2_ref[0], a2_ref[1]]
    return (k, wf, wb, af, ab, kk_ref[...], ka_ref[...], w0s, w2s, a0s, a2s)


def _pre_specs():
    c = lambda shape: _bs(shape, lambda i: tuple(0 for _ in shape))
    return [_bs((PT, RSW), lambda i: (i, 0)), c((1, RW)), c((1, RW)), c((2, RW)), c((2, 64, RW)), c((2, RW)),
            c((2, 64, RW))]


def _pre_fwd(ps, k_k, k_a, w0, w2, a0, a2):
    s = ps.shape[0]

    def body(ps_ref, kk_ref, ka_ref, w0_ref, w2_ref, a0_ref, a2_ref, *outs):
        args = _pre_load(ps_ref, kk_ref, ka_ref, w0_ref, w2_ref, a0_ref, a2_ref)
        res = _pre_tile(*args, _seg_matrix(RW, HD))
        for o_ref, v in zip(outs, res):
            o_ref[...] = v

    return pl.pallas_call(
        body, name="rwkv_pre_fwd", grid=(s // PT,),
        in_specs=_pre_specs(), out_specs=[_bs((PT, RW), lambda i: (i, 0))] * 7,
        out_shape=[jax.ShapeDtypeStruct((s, RW), F32)] * 7,
        compiler_params=_params(("parallel",)),
    )(ps, k_k, k_a, w0, w2, a0, a2)


def _pre_bwd(ps, k_k, k_a, w0, w2, a0, a2, dr, dv, cts):
    s = ps.shape[0]

    def body(ps_ref, kk_ref, ka_ref, w0_ref, w2_ref, a0_ref, a2_ref, dr_ref, dv_ref, c0, c1, c2, c3, c4, c5, c6,
             dps_ref, dkk_ref, dka_ref, dw0_ref, dw2_ref, da0_ref, da2_ref):
        @pl.when(pl.program_id(0) == 0)
        def _():
            for r in (dkk_ref, dka_ref, dw0_ref, dw2_ref, da0_ref, da2_ref):
                r[...] = jnp.zeros_like(r)

        args = _pre_load(ps_ref, kk_ref, ka_ref, w0_ref, w2_ref, a0_ref, a2_ref)
        seg = _seg_matrix(RW, HD)
        _, vjp = jax.vjp(lambda *a: _pre_tile(*a, seg), *args)
        dk, dwf, dwb, daf, dab, dk_k, dk_a, dw0s, dw2s, da0s, da2s = vjp([c[...] for c in (c0, c1, c2, c3, c4, c5, c6)])
        dps_ref[:, 0:RW] = dr_ref[...]
        dps_ref[:, RW:2 * RW] = dk
        dps_ref[:, 2 * RW:3 * RW] = dv_ref[...]
        for j, t in enumerate((dwf, dwb, daf, dab)):
            dps_ref[:, 3 * RW + 64 * j:3 * RW + 64 * (j + 1)] = t
        dkk_ref[...] += dk_k
        dka_ref[...] += dk_a
        for d in range(2):
            dw0_ref[d:d + 1, :] += dw0s[d]
            da0_ref[d:d + 1, :] += da0s[d]
            dw2_ref[d] += dw2s[d]
            da2_ref[d] += da2s[d]

    c = lambda shape: _bs(shape, lambda i: tuple(0 for _ in shape))
    row = _bs((PT, RW), lambda i: (i, 0))
    return pl.pallas_call(
        body, name="rwkv_pre_bwd", grid=(s // PT,),
        in_specs=_pre_specs() + [row] * 9,
        out_specs=[_bs((PT, RSW), lambda i: (i, 0)), c((1, RW)), c((1, RW)), c((2, RW)), c((2, 64, RW)), c((2, RW)),
                   c((2, 64, RW))],
        out_shape=[jax.ShapeDtypeStruct((s, RSW), F32), jax.ShapeDtypeStruct((1, RW), F32),
                   jax.ShapeDtypeStruct((1, RW), F32), jax.ShapeDtypeStruct((2, RW), F32),
                   jax.ShapeDtypeStruct((2, 64, RW), F32), jax.ShapeDtypeStruct((2, RW), F32),
                   jax.ShapeDtypeStruct((2, 64, RW), F32)],
        compiler_params=_params(("arbitrary",)),
    )(ps, k_k, k_a, w0, w2, a0, a2, dr, dv, *cts)


def _post_tile(y0, y1, r, v, kd0, kd1, rg, r_k, ln_w, ln_b, seg):
    ysum = y0 + y1
    bonus = (_hdot(r * kd0 * r_k, seg) + _hdot(r * kd1 * r_k, seg)) * v
    mean = _hdot(ysum, seg) * (1.0 / HD)
    cen = ysum - mean
    var = _hdot(cen * cen, seg) * (1.0 / HD)
    y = cen * lax.rsqrt(var + GN_EPS) * ln_w + ln_b + bonus
    return y * _silu(rg)


def _post_specs():
    row = _bs((PT, RW), lambda i: (i, 0))
    c = _bs((1, RW), lambda i: (0, 0))
    return [row, row, _bs((PT, RW), lambda i: (i, 0)), _bs((PT, RW), lambda i: (i, 2)), row, row,
            _bs((PT, RW), lambda i: (i, C_RG // RW)), c, c, c]


def _post_fwd(y0, y1, ps, kd0, kd1, proj, r_k, ln_w, ln_b):
    s = ps.shape[0]

    def body(y0_ref, y1_ref, r_ref, v_ref, kd0_ref, kd1_ref, rg_ref, rk_ref, lw_ref, lb_ref, o_ref):
        o_ref[...] = _post_tile(y0_ref[...], y1_ref[...], r_ref[...], v_ref[...], kd0_ref[...], kd1_ref[...],
                                rg_ref[...], rk_ref[...], lw_ref[...], lb_ref[...], _seg_matrix(RW, HD))

    return pl.pallas_call(
        body, name="rwkv_post_fwd", grid=(s // PT,),
        in_specs=_post_specs(), out_specs=_bs((PT, RW), lambda i: (i, 0)),
        out_shape=jax.ShapeDtypeStruct((s, RW), F32),
        compiler_params=_params(("parallel",)),
    )(y0, y1, ps, ps, kd0, kd1, proj, r_k, ln_w, ln_b)


def _post_bwd(y0, y1, ps, kd0, kd1, proj, r_k, ln_w, ln_b, dy):
    s = ps.shape[0]

    def body(y0_ref, y1_ref, r_ref, v_ref, kd0_ref, kd1_ref, rg_ref, rk_ref, lw_ref, lb_ref, dy_ref,
             dys_ref, dr_ref, dv_ref, dkd0_ref, dkd1_ref, drg_ref, drk_ref, dlw_ref, dlb_ref):
        @pl.when(pl.program_id(0) == 0)
        def _():
            for r in (drk_ref, dlw_ref, dlb_ref):
                r[...] = jnp.zeros_like(r)

        seg = _seg_matrix(RW, HD)
        args = [t[...] for t in (y0_ref, y1_ref, r_ref, v_ref, kd0_ref, kd1_ref, rg_ref, rk_ref, lw_ref, lb_ref)]
        _, vjp = jax.vjp(lambda *a: _post_tile(*a, seg), *args)
        dy0, _, dr, dv, dkd0, dkd1, drg, drk, dlw, dlb = vjp(dy_ref[...])
        dys_ref[...] = dy0
        dr_ref[...] = dr
        dv_ref[...] = dv
        dkd0_ref[...] = dkd0
        dkd1_ref[...] = dkd1
        drg_ref[...] = drg
        drk_ref[...] += drk
        dlw_ref[...] += dlw
        dlb_ref[...] += dlb

    row = _bs((PT, RW), lambda i: (i, 0))
    c = _bs((1, RW), lambda i: (0, 0))
    return pl.pallas_call(
        body, name="rwkv_post_bwd", grid=(s // PT,),
        in_specs=_post_specs() + [row], out_specs=[row] * 6 + [c] * 3,
        out_shape=[jax.ShapeDtypeStruct((s, RW), F32)] * 6 + [jax.ShapeDtypeStruct((1, RW), F32)] * 3,
        compiler_params=_params(("arbitrary",)),
    )(y0, y1, ps, ps, kd0, kd1, proj, r_k, ln_w, ln_b, dy)


def _ones2():
    r = lax.broadcasted_iota(jnp.int32, (256, 128), 0) % 128 // HD
    c = lax.broadcasted_iota(jnp.int32, (256, 128), 1) // HD
    return (r == c).astype(BF16)


def _split(p):
    hi = p.astype(BF16)
    lo = (p - hi.astype(F32)).astype(BF16)
    return jnp.concatenate([hi, lo], axis=1)


def _to_t8(a):
    s = a.shape[0]
    t = a.reshape(s // 8, 8, NPAIR, 2, HD).transpose(0, 2, 4, 3, 1)
    t = jnp.pad(t, ((0, 0), (0, 0), (0, 0), (0, 0), (0, HD - 8))).reshape(s // 8, NPAIR, HD, 128)
    hi = t.astype(BF16)
    lo = (t - hi.astype(F32)).astype(BF16)
    return jnp.concatenate([hi, lo], axis=-1)


def _from_t8(t8):
    g = t8.shape[0]
    t = t8.reshape(g, NPAIR, HD, 2, HD)[..., :8]
    return t.transpose(0, 4, 1, 3, 2).reshape(g * 8, RW)


def _scan_specs(direction, nc, fwd_order):
    def tb(c):
        sc = c if fwd_order else nc - 1 - c
        return sc if direction == 0 else nc - 1 - sc

    row = _bs((TC, RW), lambda c: (tb(c), 0))
    t8_in = _bs((TC // 8, NPAIR, HD, 256), lambda c: (tb(c), 0, 0, 0))
    t8_out = _bs((TC // 8, NPAIR, HD, 128), lambda c: (tb(c), 0, 0, 0))
    return row, t8_in, t8_out


def _put_t8(ref, g, u, tiles):
    for p in range(NPAIR):
        ref[g, p, :, u:u + 1] = tiles[p][:, u:u + 1]
        ref[g, p, :, HD + u:HD + u + 1] = tiles[p][:, HD + u:HD + u + 1]


def _scan_fwd(dec, kd, b, ps, kk, vl, direction):
    s = dec.shape[0]
    nc, ng = s // TC, TC // 8
    row, t8_in, t8_out = _scan_specs(direction, nc, True)
    n = NPAIR * HD

    def body(dec_ref, kd_ref, b_ref, r_ref, kk_ref, vl_ref, y8_ref, ck_ref, st):
        @pl.when(pl.program_id(0) == 0)
        def _():
            st[...] = jnp.zeros_like(st)

        ck_ref[0] = st[...]
        ones2 = _ones2()
        lane_u = lax.broadcasted_iota(jnp.int32, (HD, 256), 1) % HD
        tiles = lambda res, k: [res[k * n + p * HD:k * n + (p + 1) * HD] for p in range(NPAIR)]

        def group(gi, carry):
            g = gi if direction == 0 else ng - 1 - gi
            rows8 = pl.ds(pl.multiple_of(g * 8, 8), 8)
            d8, k8, b8, r8, kk8 = (q[rows8, :] for q in (dec_ref, kd_ref, b_ref, r_ref, kk_ref))
            pc = [slice(p * 128, (p + 1) * 128) for p in range(NPAIR)]
            ss = [st[p] for p in range(NPAIR)]
            u_prev = None
            for ui in range(8):
                u = ui if direction == 0 else 7 - ui
                lhs = [_split(ss[p] * kk8[u:u + 1, pc[p]]) for p in range(NPAIR)]
                for p in range(NPAIR):
                    vt = vl_ref[g, p]
                    lhs.append(jnp.where(lane_u == u, vt, jnp.zeros_like(vt)))
                if u_prev is not None:
                    lhs += [_split(ss[p] * r8[u_prev:u_prev + 1, pc[p]]) for p in range(NPAIR)]
                res = jnp.dot(jnp.concatenate(lhs, axis=0), ones2, preferred_element_type=F32)
                if u_prev is not None:
                    _put_t8(y8_ref, g, u_prev, tiles(res, 2))
                sa, vb = tiles(res, 0), tiles(res, 1)
                for p in range(NPAIR):
                    ss[p] = ss[p] * d8[u:u + 1, pc[p]] - sa[p] * b8[u:u + 1, pc[p]] + vb[p] * k8[u:u + 1, pc[p]]
                u_prev = u
            lhs = [_split(ss[p] * r8[u_prev:u_prev + 1, pc[p]]) for p in range(NPAIR)]
            res = jnp.dot(jnp.concatenate(lhs, axis=0), ones2, preferred_element_type=F32)
            _put_t8(y8_ref, g, u_prev, tiles(res, 0))
            for p in range(NPAIR):
                st[p] = ss[p]
            return carry

        lax.fori_loop(0, ng, group, 0)

    return pl.pallas_call(
        body, name=f"rwkv_scan_fwd{direction}", grid=(nc,),
        in_specs=[row, row, row, row, row, t8_in],
        out_specs=[t8_out, _bs((1, NPAIR, HD, 128), lambda c: (c, 0, 0, 0))],
        out_shape=[jax.ShapeDtypeStruct((s // 8, NPAIR, HD, 128), F32),
                   jax.ShapeDtypeStruct((nc, NPAIR, HD, 128), F32)],
        scratch_shapes=[pltpu.VMEM((NPAIR, HD, 128), F32)],
        compiler_params=_params(("arbitrary",)),
    )(dec, kd, b, ps, kk, vl)


def _scan_bwd(dec, kd, b, ps, kk, vl, dyl, ck, direction):
    s = dec.shape[0]
    nc, ng = s // TC, TC // 8
    row, t8_in, t8_out = _scan_specs(direction, nc, False)
    n = NPAIR * HD

    def body(dec_ref, kd_ref, b_ref, r_ref, kk_ref, vl_ref, dyl_ref, ck_ref,
             dr_ref, dd_ref, db_ref, dk_ref, dkk_ref, dv8_ref, st, sa_s, vb_s, dy_s, ds):
        @pl.when(pl.program_id(0) == 0)
        def _():
            ds[...] = jnp.zeros_like(ds)

        st[0] = ck_ref[0]
        ones2 = _ones2()
        lane_u = lax.broadcasted_iota(jnp.int32, (HD, 256), 1) % HD
        row_id = lax.broadcasted_iota(jnp.int32, (8, 128), 0)
        pc = [slice(p * 128, (p + 1) * 128) for p in range(NPAIR)]
        tiles = lambda res, k: [res[k * n + p * HD:k * n + (p + 1) * HD] for p in range(NPAIR)]

        def fgroup(gi, carry):
            g = gi if direction == 0 else ng - 1 - gi
            rows8 = pl.ds(pl.multiple_of(g * 8, 8), 8)
            d8, k8, b8, kk8 = (q[rows8, :] for q in (dec_ref, kd_ref, b_ref, kk_ref))
            ss = [st[gi * 8, p] for p in range(NPAIR)]
            for ui in range(8):
                u = ui if direction == 0 else 7 - ui
                i = gi * 8 + ui
                lhs = [_split(ss[p] * kk8[u:u + 1, pc[p]]) for p in range(NPAIR)]
                for ref in (vl_ref, dyl_ref):
                    for p in range(NPAIR):
                        t = ref[g, p]
                        lhs.append(jnp.where(lane_u == u, t, jnp.zeros_like(t)))
                res = jnp.dot(jnp.concatenate(lhs, axis=0), ones2, preferred_element_type=F32)
                sa, vb, dyb = tiles(res, 0), tiles(res, 1), tiles(res, 2)
                for p in range(NPAIR):
                    sa_s[i, p] = sa[p]
                    vb_s[i, p] = vb[p]
                    dy_s[i, p] = dyb[p]
                    ss[p] = ss[p] * d8[u:u + 1, pc[p]] - sa[p] * b8[u:u + 1, pc[p]] + vb[p] * k8[u:u + 1, pc[p]]
                    st[i + 1, p] = ss[p]
            return carry

        lax.fori_loop(0, ng, fgroup, 0)

        def bgroup(gj, carry):
            gi = ng - 1 - gj
            g = gi if direction == 0 else ng - 1 - gi
            rows8 = pl.ds(pl.multiple_of(g * 8, 8), 8)
            d8, k8, b8, r8, kk8 = (q[rows8, :] for q in (dec_ref, kd_ref, b_ref, r_ref, kk_ref))
            dss = [ds[p] for p in range(NPAIR)]
            acc = [[jnp.zeros((8, 128), F32) for _ in range(5)] for _ in range(NPAIR)]
            for uj in range(8):
                ui = 7 - uj
                u = ui if direction == 0 else 7 - ui
                i = gi * 8 + ui
                dyb = [dy_s[i, p] for p in range(NPAIR)]
                for p in range(NPAIR):
                    dss[p] = dss[p] + dyb[p] * r8[u:u + 1, pc[p]]
                lhs = [_split(dss[p] * b8[u:u + 1, pc[p]]) for p in range(NPAIR)]
                lhs += [_split(dss[p] * k8[u:u + 1, pc[p]]) for p in range(NPAIR)]
                res = jnp.dot(jnp.concatenate(lhs, axis=0), ones2, preferred_element_type=F32)
                dsa, dvb = tiles(res, 0), tiles(res, 1)
                _put_t8(dv8_ref, g, u, dvb)
                for p in range(NPAIR):
                    sp, sn = st[i, p], st[i + 1, p]
                    outs = (jnp.sum(sn * dyb[p], axis=0, keepdims=True), jnp.sum(dss[p] * sp, axis=0, keepdims=True),
                            -jnp.sum(dss[p] * sa_s[i, p], axis=0, keepdims=True),
                            jnp.sum(dss[p] * vb_s[i, p], axis=0, keepdims=True),
                            -jnp.sum(sp * dsa[p], axis=0, keepdims=True))
                    acc[p] = [jnp.where(row_id == u, o, a_) for o, a_ in zip(outs, acc[p])]
                    dss[p] = dss[p] * d8[u:u + 1, pc[p]] - dsa[p] * kk8[u:u + 1, pc[p]]
            for p in range(NPAIR):
                ds[p] = dss[p]
                for o_ref, a_ in zip((dr_ref, dd_ref, db_ref, dk_ref, dkk_ref), acc[p]):
                    o_ref[rows8, pc[p]] = a_
            return carry

        lax.fori_loop(0, ng, bgroup, 0)

    chunk = lambda k: pltpu.VMEM((k, NPAIR, HD, 128), F32)
    return pl.pallas_call(
        body, name=f"rwkv_scan_bwd{direction}", grid=(nc,),
        in_specs=[row, row, row, row, row, t8_in, t8_in, _bs((1, NPAIR, HD, 128), lambda c: (nc - 1 - c, 0, 0, 0))],
        out_specs=[row] * 5 + [t8_out],
        out_shape=[jax.ShapeDtypeStruct((s, RW), F32)] * 5 + [jax.ShapeDtypeStruct((s // 8, NPAIR, HD, 128), F32)],
        scratch_shapes=[chunk(TC + 1), chunk(TC), chunk(TC), chunk(TC), pltpu.VMEM((NPAIR, HD, 128), F32)],
        compiler_params=_params(("arbitrary",)),
    )(dec, kd, b, ps, kk, vl, dyl, ck)


def _tiles(res, k):
    n = NPAIR * HD
    return [res[k * n + p * HD:k * n + (p + 1) * HD] for p in range(NPAIR)]


def _scan2_fwd(per_dir, ps, kk, vl):
    s = ps.shape[0]
    nc, ng = s // TC, TC // 8
    in_specs, operands, out_specs, out_shape = [], [], [], []
    for d in (0, 1):
        row, t8_in, t8_out = _scan_specs(d, nc, True)
        in_specs += [row] * 5 + [t8_in]
        operands += list(per_dir[d]) + [ps, kk, vl]
        out_specs += [t8_out, _bs((1, NPAIR, HD, 128), lambda c: (c, 0, 0, 0))]
        out_shape += [jax.ShapeDtypeStruct((s // 8, NPAIR, HD, 128), F32),
                      jax.ShapeDtypeStruct((nc, NPAIR, HD, 128), F32)]

    def body(*refs):
        ins = [refs[0:6], refs[6:12]]
        y_refs, ck_refs, st = (refs[12], refs[14]), (refs[13], refs[15]), refs[16]

        @pl.when(pl.program_id(0) == 0)
        def _():
            st[...] = jnp.zeros_like(st)

        for d in (0, 1):
            ck_refs[d][0] = st[d * NPAIR:(d + 1) * NPAIR]
        ones2 = _ones2()
        lane_u = lax.broadcasted_iota(jnp.int32, (HD, 256), 1) % HD
        pc = [slice(p * 128, (p + 1) * 128) for p in range(NPAIR)]

        def group(gi, carry):
            gs = (gi, ng - 1 - gi)
            blk = [[q[pl.ds(pl.multiple_of(gs[d] * 8, 8), 8), :] for q in ins[d][:5]] for d in (0, 1)]
            ss = [[st[d * NPAIR + p] for p in range(NPAIR)] for d in (0, 1)]
            for ui in range(9):
                us, ups = (ui, 7 - ui), (ui - 1, 8 - ui)
                lhs, where = [], {}
                for d in (0, 1):
                    if ui < 8:
                        where["sa", d] = len(lhs) // NPAIR
                        lhs += [_split(ss[d][p] * blk[d][4][us[d]:us[d] + 1, pc[p]]) for p in range(NPAIR)]
                        where["vb", d] = len(lhs) // NPAIR
                        for p in range(NPAIR):
                            vt = ins[d][5][gs[d], p]
                            lhs.append(jnp.where(lane_u == us[d], vt, jnp.zeros_like(vt)))
                    if ui > 0:
                        where["y", d] = len(lhs) // NPAIR
                        lhs += [_split(ss[d][p] * blk[d][3][ups[d]:ups[d] + 1, pc[p]]) for p in range(NPAIR)]
                res = jnp.dot(jnp.concatenate(lhs, axis=0), ones2, preferred_element_type=F32)
                for d in (0, 1):
                    d8, k8, b8, _, _ = blk[d]
                    u = us[d]
                    if ui < 8:
                        sa, vb = _tiles(res, where["sa", d]), _tiles(res, where["vb", d])
                        for p in range(NPAIR):
                            ss[d][p] = (ss[d][p] * d8[u:u + 1, pc[p]] - sa[p] * b8[u:u + 1, pc[p]]
                                        + vb[p] * k8[u:u + 1, pc[p]])
                    if ui > 0:
                        _put_t8(y_refs[d], gs[d], ups[d], _tiles(res, where["y", d]))
            for d in (0, 1):
                for p in range(NPAIR):
                    st[d * NPAIR + p] = ss[d][p]
            return carry

        lax.fori_loop(0, ng, group, 0)

    outs = pl.pallas_call(
        body, name="rwkv_scan_fwd", grid=(nc,), in_specs=in_specs, out_specs=out_specs, out_shape=out_shape,
        scratch_shapes=[pltpu.VMEM((2 * NPAIR, HD, 128), F32)],
        compiler_params=_params(("arbitrary",)),
    )(*operands)
    return [(outs[0], outs[1]), (outs[2], outs[3])]


def _scan2_bwd(per_dir, ps, kk, vl, dyl):
    s = ps.shape[0]
    nc, ng = s // TC, TC // 8
    in_specs, operands, out_specs, out_shape = [], [], [], []
    for d in (0, 1):
        row, t8_in, t8_out = _scan_specs(d, nc, False)
        dec, kd, b, ck = per_dir[d]
        in_specs += [row] * 5 + [t8_in, t8_in, _bs((1, NPAIR, HD, 128), lambda c: (nc - 1 - c, 0, 0, 0))]
        operands += [dec, kd, b, ps, kk, vl, dyl, ck]
        out_specs += [row] * 5 + [t8_out]
        out_shape += [jax.ShapeDtypeStruct((s, RW), F32)] * 5 + [jax.ShapeDtypeStruct((s // 8, NPAIR, HD, 128), F32)]

    def body(*refs):
        ins = [refs[0:8], refs[8:16]]
        outs = [refs[16:22], refs[22:28]]
        st, sa_s, vb_s, dy_s, ds = refs[28:]

        @pl.when(pl.program_id(0) == 0)
        def _():
            ds[...] = jnp.zeros_like(ds)

        for d in (0, 1):
            st[d * (TC + 1)] = ins[d][7][0]
        ones2 = _ones2()
        lane_u = lax.broadcasted_iota(jnp.int32, (HD, 256), 1) % HD
        row_id = lax.broadcasted_iota(jnp.int32, (8, 128), 0)
        pc = [slice(p * 128, (p + 1) * 128) for p in range(NPAIR)]

        def load_rows(gs):
            return [[q[pl.ds(pl.multiple_of(gs[d] * 8, 8), 8), :] for q in ins[d][:5]] for d in (0, 1)]

        def fgroup(gi, carry):
            gs = (gi, ng - 1 - gi)
            blk = load_rows(gs)
            ss = [[st[d * (TC + 1) + gi * 8, p] for p in range(NPAIR)] for d in (0, 1)]
            for ui in range(8):
                us = (ui, 7 - ui)
                i = gi * 8 + ui
                lhs = []
                for d in (0, 1):
                    kk8 = blk[d][4]
                    lhs += [_split(ss[d][p] * kk8[us[d]:us[d] + 1, pc[p]]) for p in range(NPAIR)]
                    for ref in (ins[d][5], ins[d][6]):
                        for p in range(NPAIR):
                            t = ref[gs[d], p]
                            lhs.append(jnp.where(lane_u == us[d], t, jnp.zeros_like(t)))
                res = jnp.dot(jnp.concatenate(lhs, axis=0), ones2, preferred_element_type=F32)
                for d in (0, 1):
                    d8, k8, b8, _, _ = blk[d]
                    u = us[d]
                    sa, vb, dyb = _tiles(res, 3 * d), _tiles(res, 3 * d + 1), _tiles(res, 3 * d + 2)
                    for p in range(NPAIR):
                        sa_s[d * TC + i, p] = sa[p]
                        vb_s[d * TC + i, p] = vb[p]
                        dy_s[d * TC + i, p] = dyb[p]
                        ss[d][p] = ss[d][p] * d8[u:u + 1, pc[p]] - sa[p] * b8[u:u + 1, pc[p]] + vb[p] * k8[u:u + 1, pc[p]]
                        st[d * (TC + 1) + i + 1, p] = ss[d][p]
            return carry

        lax.fori_loop(0, ng, fgroup, 0)

        def bgroup(gj, carry):
            gi = ng - 1 - gj
            gs = (gi, ng - 1 - gi)
            blk = load_rows(gs)
            dss = [[ds[d * NPAIR + p] for p in range(NPAIR)] for d in (0, 1)]
            acc = [[[jnp.zeros((8, 128), F32) for _ in range(5)] for _ in range(NPAIR)] for _ in (0, 1)]
            for uj in range(8):
                ui = 7 - uj
                us = (ui, 7 - ui)
                i = gi * 8 + ui
                lhs_b, lhs_k, dyb = [], [], [None, None]
                for d in (0, 1):
                    _, k8, b8, r8, _ = blk[d]
                    u = us[d]
                    dyb[d] = [dy_s[d * TC + i, p] for p in range(NPAIR)]
                    for p in range(NPAIR):
                        dss[d][p] = dss[d][p] + dyb[d][p] * r8[u:u + 1, pc[p]]
                    lhs_b += [_split(dss[d][p] * b8[u:u + 1, pc[p]]) for p in range(NPAIR)]
                    lhs_b += [_split(dss[d][p] * k8[u:u + 1, pc[p]]) for p in range(NPAIR)]
                res = jnp.dot(jnp.concatenate(lhs_b + lhs_k, axis=0), ones2, preferred_element_type=F32)
                for d in (0, 1):
                    d8, _, _, _, kk8 = blk[d]
                    u = us[d]
                    dsa, dvb = _tiles(res, 2 * d), _tiles(res, 2 * d + 1)
                    _put_t8(outs[d][5], gs[d], u, dvb)
                    for p in range(NPAIR):
                        sp, sn = st[d * (TC + 1) + i, p], st[d * (TC + 1) + i + 1, p]
                        dsv = dss[d][p]
                        vals = (jnp.sum(sn * dyb[d][p], axis=0, keepdims=True), jnp.sum(dsv * sp, axis=0, keepdims=True),
                                -jnp.sum(dsv * sa_s[d * TC + i, p], axis=0, keepdims=True),
                                jnp.sum(dsv * vb_s[d * TC + i, p], axis=0, keepdims=True),
                                -jnp.sum(sp * dsa[p], axis=0, keepdims=True))
                        acc[d][p] = [jnp.where(row_id == u, o, a_) for o, a_ in zip(vals, acc[d][p])]
                        dss[d][p] = dsv * d8[u:u + 1, pc[p]] - dsa[p] * kk8[u:u + 1, pc[p]]
            for d in (0, 1):
                rows8 = pl.ds(pl.multiple_of(gs[d] * 8, 8), 8)
                for p in range(NPAIR):
                    ds[d * NPAIR + p] = dss[d][p]
                    for o_ref, a_ in zip(outs[d][:5], acc[d][p]):
                        o_ref[rows8, pc[p]] = a_
            return carry

        lax.fori_loop(0, ng, bgroup, 0)

    chunk = lambda k: pltpu.VMEM((k, NPAIR, HD, 128), F32)
    res = pl.pallas_call(
        body, name="rwkv_scan_bwd", grid=(nc,), in_specs=in_specs, out_specs=out_specs, out_shape=out_shape,
        scratch_shapes=[chunk(2 * (TC + 1)), chunk(2 * TC), chunk(2 * TC), chunk(2 * TC),
                        pltpu.VMEM((2 * NPAIR, HD, 128), F32)],
        compiler_params=_params(("arbitrary",)),
    )(*operands)
    return [res[0:6], res[6:12]]


MT = 256
MN = 256


def _merge_fwd(ya, yr, yx, wa, wr, wx, proj, gate_b):
    s = ya.shape[0]

    def body(ya_ref, yr_ref, yx_ref, wa_ref, wr_ref, wx_ref, m0, m1, m2, b0, b1, b2, o_ref):
        acc = jnp.zeros((MT, MN), F32)
        for y_ref, w_ref, m_ref, b_ref in ((ya_ref, wa_ref, m0, b0), (yr_ref, wr_ref, m1, b1), (yx_ref, wx_ref, m2, b2)):
            u = _dot(y_ref[...], w_ref[...], ((1,), (0,)))
            acc = acc + jax.nn.sigmoid(m_ref[...] + b_ref[...]) * u
        o_ref[...] = acc.astype(BF16)

    mg = lambda br: _bs((MT, MN), lambda i, j: (i, C_MG // MN + br * (D // MN) + j))
    gb = lambda br: _bs((1, MN), lambda i, j: (0, br * (D // MN) + j))
    return pl.pallas_call(
        body, name="merge_fwd", grid=(s // MT, D // MN),
        in_specs=[_bs((MT, RW), lambda i, j: (i, 0)), _bs((MT, RW), lambda i, j: (i, 0)), _bs((MT, XW), lambda i, j: (i, 0)),
                  _bs((RW, MN), lambda i, j: (0, j)), _bs((RW, MN), lambda i, j: (0, j)), _bs((XW, MN), lambda i, j: (0, j)),
                  mg(0), mg(1), mg(2), gb(0), gb(1), gb(2)],
        out_specs=_bs((MT, MN), lambda i, j: (i, j)),
        out_shape=jax.ShapeDtypeStruct((s, D), BF16),
        compiler_params=_params(("parallel", "arbitrary")),
    )(ya, yr, yx, wa, wr, wx, proj, proj, proj, gate_b, gate_b, gate_b)


def _out_fwd(merged, w_out, x, target):
    s = x.shape[0]
    tm, tn = min(512, s), 512

    def body(m_ref, w_ref, x_ref, t_ref, loss_ref, d_ref, d16_ref):
        @pl.when((pl.program_id(0) == 0) & (pl.program_id(1) == 0))
        def _():
            loss_ref[...] = jnp.zeros_like(loss_ref)

        out = x_ref[...] + jnp.dot(m_ref[...], w_ref[...], preferred_element_type=F32)
        err = out - t_ref[...]
        dout = err * (1.0 / D)
        d_ref[...] = dout
        d16_ref[...] = dout.astype(BF16)
        loss_ref[...] += jnp.sum(err * err)

    tile = _bs((tm, tn), lambda i, j: (i, j))
    return pl.pallas_call(
        body, name="out_fwd", grid=(s // tm, D // tn),
        in_specs=[_bs((tm, D), lambda i, j: (i, 0)), _bs((D, tn), lambda i, j: (0, j)), tile, tile],
        out_specs=[_bs((8, 128), lambda i, j: (0, 0)), tile, tile],
        out_shape=[jax.ShapeDtypeStruct((8, 128), F32), jax.ShapeDtypeStruct((s, D), F32),
                   jax.ShapeDtypeStruct((s, D), BF16)],
        compiler_params=_params(("arbitrary", "arbitrary")),
    )(merged, w_out, x, target)


def _merge_bwd(ya, yr, yx, wa, wr, wx, proj, gate_b, dmerged):
    s = ya.shape[0]

    def body(ya_ref, yr_ref, yx_ref, wa_ref, wr_ref, wx_ref, m0, m1, m2, b0, b1, b2, dm_ref,
             dg0, dg1, dg2, du0, du1, du2, dya_ref, dyr_ref, dyx_ref):
        @pl.when(pl.program_id(1) == 0)
        def _():
            dya_ref[...] = jnp.zeros_like(dya_ref)
            dyr_ref[...] = jnp.zeros_like(dyr_ref)
            dyx_ref[...] = jnp.zeros_like(dyx_ref)

        dm = dm_ref[...]
        for y_ref, w_ref, m_ref, b_ref, dg_ref, du_ref, dy_ref in (
                (ya_ref, wa_ref, m0, b0, dg0, du0, dya_ref), (yr_ref, wr_ref, m1, b1, dg1, du1, dyr_ref),
                (yx_ref, wx_ref, m2, b2, dg2, du2, dyx_ref)):
            w = w_ref[...]
            u = _dot(y_ref[...], w, ((1,), (0,)))
            gt = jax.nn.sigmoid(m_ref[...] + b_ref[...])
            dg_ref[...] = (dm * u * gt * (1.0 - gt)).astype(BF16)
            du = (dm * gt).astype(BF16)
            du_ref[...] = du
            dy_ref[...] += _dot(du, w, ((1,), (1,)))

    mg = lambda br: _bs((MT, MN), lambda i, j: (i, C_MG // MN + br * (D // MN) + j))
    gb = lambda br: _bs((1, MN), lambda i, j: (0, br * (D // MN) + j))
    tile = _bs((MT, MN), lambda i, j: (i, j))
    return pl.pallas_call(
        body, name="merge_bwd", grid=(s // MT, D // MN),
        in_specs=[_bs((MT, RW), lambda i, j: (i, 0)), _bs((MT, RW), lambda i, j: (i, 0)), _bs((MT, XW), lambda i, j: (i, 0)),
                  _bs((RW, MN), lambda i, j: (0, j)), _bs((RW, MN), lambda i, j: (0, j)), _bs((XW, MN), lambda i, j: (0, j)),
                  mg(0), mg(1), mg(2), gb(0), gb(1), gb(2), tile],
        out_specs=[tile] * 6 + [_bs((MT, RW), lambda i, j: (i, 0)), _bs((MT, RW), lambda i, j: (i, 0)),
                                _bs((MT, XW), lambda i, j: (i, 0))],
        out_shape=[jax.ShapeDtypeStruct((s, D), BF16)] * 6 + [jax.ShapeDtypeStruct((s, RW), F32),
                                                               jax.ShapeDtypeStruct((s, RW), F32),
                                                               jax.ShapeDtypeStruct((s, XW), F32)],
        compiler_params=_params(("parallel", "arbitrary")),
    )(ya, yr, yx, wa, wr, wx, proj, proj, proj, gate_b, gate_b, gate_b, dmerged)


def _colsum(a, name):
    m, n = a.shape
    tm, tn = min(512, m), 512

    def body(a_ref, o_ref):
        @pl.when(pl.program_id(1) == 0)
        def _():
            o_ref[...] = jnp.zeros_like(o_ref)

        o_ref[...] += jnp.sum(a_ref[...].astype(F32), axis=0, keepdims=True)

    return pl.pallas_call(
        body, name=name, grid=(n // tn, m // tm),
        in_specs=[_bs((tm, tn), lambda j, i: (i, j))], out_specs=_bs((1, tn), lambda j, i: (0, j)),
        out_shape=jax.ShapeDtypeStruct((1, n), F32),
        compiler_params=_params(("parallel", "arbitrary")),
    )(a)


def _in_bwd(dproj, w_in, x, g, dout):
    s = x.shape[0]
    tm, tk = min(512, s), 896
    nk = NIN // tk

    def body(dp_ref, w_ref, x_ref, g_ref, do_ref, gx_ref, gg_ref, acc):
        i, kk = pl.program_id(0), pl.program_id(1)

        @pl.when((i == 0) & (kk == 0))
        def _():
            gg_ref[...] = jnp.zeros_like(gg_ref)

        @pl.when(kk == 0)
        def _():
            acc[...] = jnp.zeros_like(acc)

        acc[...] += _dot(dp_ref[...], w_ref[...], ((1,), (1,)))

        @pl.when(kk == nk - 1)
        def _():
            xv, dh, gv = x_ref[...], acc[...], g_ref[...]
            r = lax.rsqrt(jnp.mean(xv * xv, axis=-1, keepdims=True) + NORM_EPS)
            xn = xv * r
            gg_ref[...] += jnp.sum(dh * xn, axis=0, keepdims=True)
            dxn = dh * gv
            dx = r * (dxn - xn * jnp.mean(dxn * xn, axis=-1, keepdims=True))
            gx_ref[...] = do_ref[...] + dx

    return pl.pallas_call(
        body, name="in_bwd", grid=(s // tm, nk),
        in_specs=[_bs((tm, tk), lambda i, kk: (i, kk)), _bs((D, tk), lambda i, kk: (0, kk)),
                  _bs((tm, D), lambda i, kk: (i, 0)), _bs((1, D), lambda i, kk: (0, 0)), _bs((tm, D), lambda i, kk: (i, 0))],
        out_specs=[_bs((tm, D), lambda i, kk: (i, 0)), _bs((1, D), lambda i, kk: (0, 0))],
        out_shape=[jax.ShapeDtypeStruct((s, D), F32), jax.ShapeDtypeStruct((1, D), F32)],
        scratch_shapes=[pltpu.VMEM((tm, D), F32)],
        compiler_params=_params(("arbitrary", "arbitrary")),
    )(dproj, w_in, x, g, dout)


def _adamw_math(w, g, m, v):
    m = ADAM_B1 * m + (1.0 - ADAM_B1) * g
    v = ADAM_B2 * v + (1.0 - ADAM_B2) * jnp.square(g)
    m_hat = m / (1.0 - ADAM_B1 ** ADAM_STEP)
    v_hat = v / (1.0 - ADAM_B2 ** ADAM_STEP)
    delta = -ADAM_LR * (m_hat / (jnp.sqrt(v_hat) + ADAM_EPS) + ADAM_WD * w)
    return delta, m, v


def _adamw(parts, w, m, v, name):
    rows, cols = w.shape
    tr = rows
    for cand in (256, 128, 64, 32, 16, 8):
        if rows % cand == 0 and cand * cols * 4 <= (1 << 20):
            tr = cand
            break
    n = len(parts)

    def body(*refs):
        g = refs[0][...].astype(F32)
        for r in refs[1:n]:
            g = g + r[...].astype(F32)
        w_ref, m_ref, v_ref, g_out, d_out, m_out, v_out = refs[n:]
        delta, m_new, v_new = _adamw_math(w_ref[...], g, m_ref[...], v_ref[...])
        g_out[...] = g
        d_out[...] = delta
        m_out[...] = m_new
        v_out[...] = v_new

    spec = _bs((tr, cols), lambda i: (i, 0))
    return pl.pallas_call(
        body, name=name, grid=(rows // tr,),
        in_specs=[spec] * (n + 3), out_specs=[spec] * 4,
        out_shape=[jax.ShapeDtypeStruct((rows, cols), F32)] * 4,
        compiler_params=_params(("parallel",)),
    )(*parts, w, m, v)


def _adamw_halves(mine, theirs, core, w, m, v, name):
    rows, cols = w.shape
    h = rows // 2
    tr = next(t for t in (256, 128, 64, 32, 16, 8) if h % t == 0 and t * cols * 4 <= (1 << 20))
    nt = h // tr

    def body(core_ref, mine_ref, theirs_ref, w_ref, m_ref, v_ref, g_out, d_out, m_out, v_out):
        is_mine = pl.program_id(0) // nt == core_ref[0]
        g = jnp.where(is_mine, mine_ref[...], theirs_ref[...])
        delta, m_new, v_new = _adamw_math(w_ref[...], g, m_ref[...], v_ref[...])
        g_out[...] = g
        d_out[...] = delta
        m_out[...] = m_new
        v_out[...] = v_new

    spec = _bs((tr, cols), lambda i, core_ref: (i, 0))
    return pl.pallas_call(
        body, name=name,
        grid_spec=pltpu.PrefetchScalarGridSpec(
            num_scalar_prefetch=1, grid=(2 * nt,),
            in_specs=[_bs((tr, cols), lambda i, core_ref: (jnp.clip(i - core_ref[0] * nt, 0, nt - 1), 0)),
                      _bs((tr, cols), lambda i, core_ref: (jnp.clip(i - (1 - core_ref[0]) * nt, 0, nt - 1), 0)),
                      spec, spec, spec],
            out_specs=[spec] * 4),
        out_shape=[jax.ShapeDtypeStruct((rows, cols), F32)] * 4,
        compiler_params=_params(("parallel",)),
    )(core, mine, theirs, w, m, v)


def _sum_parts(parts, name):
    rows, cols = parts[0].shape
    tr = rows
    for cand in (256, 128, 64, 32, 16, 8):
        if rows % cand == 0 and cand * cols * 4 <= (1 << 20):
            tr = cand
            break

    def body(*refs):
        acc = refs[0][...].astype(F32)
        for r in refs[1:-1]:
            acc = acc + r[...].astype(F32)
        refs[-1][...] = acc

    spec = _bs((tr, cols), lambda i: (i, 0))
    return pl.pallas_call(
        body, name=name, grid=(rows // tr,), in_specs=[spec] * len(parts), out_specs=spec,
        out_shape=jax.ShapeDtypeStruct((rows, cols), F32), compiler_params=_params(("parallel",)),
    )(*parts)


ANY = pl.BlockSpec(memory_space=pl.ANY)


def _other_chips(x, y):
    return [(1 - x, y), (x, 1 - y), (1 - x, 1 - y)]


def _gather_shards(arrays, name):
    n = len(arrays)

    def body(*refs):
        ins, outs = refs[:n], refs[n:2 * n]
        ici_send, ici_recv, d2d_send, d2d_recv, local_sems, own_recv = refs[2 * n:]
        x, y, c = lax.axis_index("x"), lax.axis_index("y"), lax.axis_index("c")
        me = 2 * x + y
        chips = _other_chips(x, y)

        def half(i, who):
            h = arrays[i].shape[0] // 2
            return pl.ds(who * h, h)

        def ici(i, j, src_chip, to):
            return pltpu.make_async_remote_copy(
                src_ref=ins[i].at[half(i, c)], dst_ref=outs[i].at[src_chip, half(i, c)], send_sem=ici_send.at[3 * i + j],
                recv_sem=ici_recv.at[3 * i + j], device_id=to, device_id_type=MESH)

        def d2d(i, j, src_chip, who):
            piece = outs[i].at[src_chip, half(i, who)]
            return pltpu.make_async_remote_copy(
                src_ref=piece, dst_ref=piece, send_sem=d2d_send.at[3 * i + j], recv_sem=d2d_recv.at[3 * i + j],
                device_id=(x, y, 1 - c), device_id_type=MESH)

        def own(i):
            return pltpu.make_async_remote_copy(
                src_ref=ins[i], dst_ref=outs[i].at[me], send_sem=local_sems.at[i], recv_sem=own_recv.at[i],
                device_id=(x, y, 1 - c), device_id_type=MESH)

        sends = []
        for i in range(n):
            cp = own(i)
            cp.start()
            sends.append(cp)
            for j, (px, py) in enumerate(chips):
                rc = ici(i, j, me, (px, py, c))
                rc.start()
                sends.append(rc)
        for i in range(n):
            for j, (px, py) in enumerate(chips):
                ici(i, j, 2 * px + py, (px, py, c)).wait_recv()
                fw = d2d(i, j, 2 * px + py, c)
                fw.start()
                sends.append(fw)
        for i in range(n):
            for j, (px, py) in enumerate(chips):
                d2d(i, j, 2 * px + py, 1 - c).wait_recv()
            own(i).wait_recv()
        for rc in sends:
            rc.wait_send()

    dma = lambda k: pltpu.SemaphoreType.DMA((k,))
    return pl.pallas_call(
        body, name=name, in_specs=[ANY] * n, out_specs=[ANY] * n,
        out_shape=[jax.ShapeDtypeStruct((4,) + a.shape, a.dtype) for a in arrays],
        scratch_shapes=[dma(3 * n), dma(3 * n), dma(3 * n), dma(3 * n), dma(n), dma(n)],
        compiler_params=pltpu.CompilerParams(has_side_effects=True),
    )(*arrays)


def _scatter_shards(stacks, name):
    n = len(stacks)

    def body(*refs):
        ins, outs = refs[:n], refs[n:2 * n]
        send_sems, recv_sems = refs[2 * n:]
        x, y, c = lax.axis_index("x"), lax.axis_index("y"), lax.axis_index("c")
        chips = _other_chips(x, y)
        sends = []
        for i in range(n):
            for j, (px, py) in enumerate(chips):
                rc = pltpu.make_async_remote_copy(
                    src_ref=ins[i].at[2 * px + py], dst_ref=outs[i].at[j], send_sem=send_sems.at[3 * i + j],
                    recv_sem=recv_sems.at[3 * i + j], device_id=(px, py, c), device_id_type=MESH)
                rc.start()
                sends.append(rc)
        for rc in sends:
            rc.wait_recv()
        for rc in sends:
            rc.wait_send()

    return pl.pallas_call(
        body, name=name, in_specs=[ANY] * n, out_specs=[ANY] * n,
        out_shape=[jax.ShapeDtypeStruct((3,) + a.shape[1:], a.dtype) for a in stacks],
        scratch_shapes=[pltpu.SemaphoreType.DMA((3 * n,)), pltpu.SemaphoreType.DMA((3 * n,))],
        compiler_params=pltpu.CompilerParams(has_side_effects=True),
    )(*stacks)


def _pair_exchange(stacks, name):
    n = len(stacks)

    def body(*refs):
        ins, outs = refs[:n], refs[n:2 * n]
        send_sems, recv_sems = refs[2 * n:]
        x, y, c = lax.axis_index("x"), lax.axis_index("y"), lax.axis_index("c")
        cps = []
        for i in range(n):
            h = stacks[i].shape[1] // 2
            rc = pltpu.make_async_remote_copy(
                src_ref=ins[i].at[:, pl.ds((1 - c) * h, h)], dst_ref=outs[i], send_sem=send_sems.at[i],
                recv_sem=recv_sems.at[i], device_id=(x, y, 1 - c), device_id_type=MESH)
            rc.start()
            cps.append(rc)
        for rc in cps:
            rc.wait_recv()
        for rc in cps:
            rc.wait_send()

    return pl.pallas_call(
        body, name=name, in_specs=[ANY] * n, out_specs=[ANY] * n,
        out_shape=[jax.ShapeDtypeStruct((4, a.shape[1] // 2) + a.shape[2:], a.dtype) for a in stacks],
        scratch_shapes=[pltpu.SemaphoreType.DMA((n,)), pltpu.SemaphoreType.DMA((n,))],
        compiler_params=pltpu.CompilerParams(has_side_effects=True),
    )(*stacks)


def _pair_sum(own, theirs, core, name):
    _, r, cols = own.shape
    h = r // 2
    tr = next(t for t in (256, 128, 64, 32, 16) if h % t == 0 and t * cols * 4 <= (1 << 20))
    nt = h // tr

    def body(core_ref, own_ref, th_ref, o32_ref, o16_ref):
        del core_ref
        acc = own_ref[...] + th_ref[...].astype(F32)
        o32_ref[...] = acc
        o16_ref[...] = acc.astype(BF16)

    out = _bs((1, tr, cols), lambda j, t, core_ref: (j, t, 0))
    return pl.pallas_call(
        body, name=name,
        grid_spec=pltpu.PrefetchScalarGridSpec(
            num_scalar_prefetch=1, grid=(4, nt),
            in_specs=[_bs((1, tr, cols), lambda j, t, core_ref: (j, core_ref[0] * nt + t, 0)), out],
            out_specs=[out, out]),
        out_shape=[jax.ShapeDtypeStruct((4, h, cols), F32), jax.ShapeDtypeStruct((4, h, cols), BF16)],
        compiler_params=_params(("parallel", "parallel")),
    )(core, own, theirs)


def _swap_sibling(arrays, name):
    n = len(arrays)

    def body(*refs):
        ins, outs = refs[:n], refs[n:2 * n]
        send_sems, recv_sems = refs[2 * n:]
        sib = (lax.axis_index("x"), lax.axis_index("y"), 1 - lax.axis_index("c"))
        cps = []
        for i in range(n):
            rc = pltpu.make_async_remote_copy(src_ref=ins[i], dst_ref=outs[i], send_sem=send_sems.at[i],
                                              recv_sem=recv_sems.at[i], device_id=sib, device_id_type=MESH)
            rc.start()
            cps.append(rc)
        for rc in cps:
            rc.wait_recv()
        for rc in cps:
            rc.wait_send()

    return pl.pallas_call(
        body, name=name, in_specs=[ANY] * n, out_specs=[ANY] * n,
        out_shape=[jax.ShapeDtypeStruct(a.shape, a.dtype) for a in arrays],
        scratch_shapes=[pltpu.SemaphoreType.DMA((n,)), pltpu.SemaphoreType.DMA((n,))],
        compiler_params=pltpu.CompilerParams(has_side_effects=True),
    )(*arrays)


def _all_reduce_small(v):
    rows = v.shape[0]

    def body(v_ref, o_ref, buf, send_sems, recv_sems):
        x, y, c = lax.axis_index("x"), lax.axis_index("y"), lax.axis_index("c")
        me = 4 * x + 2 * y + c
        buf[me] = v_ref[...]
        cps = []
        for kbits in range(1, 8):
            bx, by, bc = (kbits >> 2) & 1, (kbits >> 1) & 1, kbits & 1
            px = jnp.where(bx == 1, 1 - x, x)
            py = jnp.where(by == 1, 1 - y, y)
            pc = jnp.where(bc == 1, 1 - c, c)
            rc = pltpu.make_async_remote_copy(src_ref=v_ref, dst_ref=buf.at[me], send_sem=send_sems.at[kbits - 1],
                                              recv_sem=recv_sems.at[kbits - 1], device_id=(px, py, pc),
                                              device_id_type=MESH)
            rc.start()
            cps.append((rc, 4 * px + 2 * py + pc))
        for kbits, (rc, src) in enumerate(cps):
            pltpu.make_async_remote_copy(src_ref=v_ref, dst_ref=buf.at[src], send_sem=send_sems.at[kbits],
                                         recv_sem=recv_sems.at[kbits], device_id=(x, y, c),
                                         device_id_type=MESH).wait_recv()
        for rc, _ in cps:
            rc.wait_send()
        acc = buf[0]
        for d in range(1, 8):
            acc = acc + buf[d]
        o_ref[...] = acc

    return pl.pallas_call(
        body, name="all_reduce_small",
        in_specs=[pl.BlockSpec(memory_space=pltpu.VMEM)], out_specs=pl.BlockSpec(memory_space=pltpu.VMEM),
        out_shape=jax.ShapeDtypeStruct((rows, 128), F32),
        scratch_shapes=[pltpu.VMEM((8, rows, 128), F32), pltpu.SemaphoreType.DMA((7,)), pltpu.SemaphoreType.DMA((7,))],
        compiler_params=pltpu.CompilerParams(has_side_effects=True, vmem_limit_bytes=VMEM_LIMIT),
    )(v)


def _rope_tables(s):
    half = HD // 2
    inv = 10000.0 ** (-jnp.arange(half, dtype=F32) / half)
    ang = jnp.arange(s, dtype=F32)[:, None] * inv[None, :]
    cos, sin = jnp.cos(ang), jnp.sin(ang)
    return jnp.concatenate([cos, cos], axis=1), jnp.concatenate([sin, sin], axis=1)


def _local_step(x, mem, target, norm_g, mem_norm_g, w_in, gate_b, gq, gk, sink, wa, mu, k_k, k_a, r_k, w0, w2, a0, a2,
                ln_w, ln_b, wr, w_kv, gxq, gxk, wx, w_out):
    s = x.shape[0]
    cos, sin = _rope_tables(s)
    r_k = r_k.reshape(1, RW)

    proj, h = _proj_fwd(x, norm_g, w_in)
    ya = _attn_fwd(proj, cos, sin, gq, gk, sink)
    mkv, mn = _mem_kv(mem, mem_norm_g, w_kv)
    yx = _xattn_fwd(proj, mkv, gxq, gxk)
    ps = _shift_fwd(proj, mu)
    kk, dec0, kd0, b0, dec1, kd1, b1 = _pre_fwd(ps, k_k, k_a, w0, w2, a0, a2)
    v8 = _to_t8(ps[:, 2 * RW:3 * RW])
    (y80, ck0), (y81, ck1) = _scan2_fwd([(dec0, kd0, b0), (dec1, kd1, b1)], ps, kk, v8)
    y0, y1 = _from_t8(y80), _from_t8(y81)
    yr = _post_fwd(y0, y1, ps, kd0, kd1, proj, r_k, ln_w, ln_b)
    merged = _merge_fwd(ya, yr, yx, wa, wr, wx, proj, gate_b)
    loss_tile, dout, dout16 = _out_fwd(merged, w_out, x, target)
    loss_sum = loss_tile[0, 0]

    g = {}
    t16 = lambda a: a.astype(BF16).T
    sk = min(1024, s)
    dmerged = _matmul(dout16, w_out, mode="nt", m=s, n=D, k=D, tm=sk, tn=1024, tk=1024, name="dmerged")
    g["w_out"] = _matmul(merged.T, dout16, mode="nn", m=D, n=D, k=s, tm=1024, tn=1024, tk=sk, name="grad_w_out")
    dg0, dg1, dg2, du0, du1, du2, dya, dyr, dyx = _merge_bwd(ya, yr, yx, wa, wr, wx, proj, gate_b, dmerged)
    g["attn_w_o"] = _matmul(t16(ya), du0, mode="nn", m=RW, n=D, k=s, tm=RW, tn=1024, tk=s, name="grad_attn_w_o")
    g["rwkv_w_o"] = _matmul(t16(yr), du1, mode="nn", m=RW, n=D, k=s, tm=RW, tn=1024, tk=s, name="grad_rwkv_w_o")
    g["x_w_o"] = _matmul(t16(yx), du2, mode="nn", m=XW, n=D, k=s, tm=XW, tn=1024, tk=s, name="grad_x_w_o")
    dmg = jnp.concatenate([dg0, dg1, dg2], axis=1)
    g["gate_b"] = _colsum(dmg, "grad_gate_b")

    daq, dak, dav, dag, g["attn_q_norm_g"], g["attn_k_norm_g"], g["attn_sink"] = _attn_bwd(proj, cos, sin, gq, gk, sink, dya)

    dxq, dxg, dmkv, g["x_q_norm_g"], g["x_k_norm_g"] = _xattn_bwd(proj, mkv, gxq, gxk, dyx)
    g["x_w_kv"] = _matmul(mn, dmkv, mode="tn", m=D, n=2 * XW, k=NMEM, tm=512, tn=512, tk=NMEM, name="grad_x_w_kv")
    dmn = _matmul(dmkv, w_kv, mode="nt", m=NMEM, n=D, k=2 * XW, tm=NMEM, tn=512, tk=2 * XW, name="dmn")
    g["mem_norm_g"] = _mem_bwd(mem, mem_norm_g, dmn)

    dys, dr_p, dv_p, dkd0_p, dkd1_p, drg, g["rwkv_r_k"], g["rwkv_ln_w"], g["rwkv_ln_b"] = _post_bwd(
        y0, y1, ps, kd0, kd1, proj, r_k, ln_w, ln_b, dyr)
    dy8 = _to_t8(dys)
    (dr0, dd0, db0, dk0, dkk0, dv80), (dr1, dd1, db1, dk1, dkk1, dv81) = _scan2_bwd(
        [(dec0, kd0, b0, ck0), (dec1, kd1, b1, ck1)], ps, kk, v8, dy8)
    dr = dr_p + dr0 + dr1
    dv = dv_p + _from_t8(dv80) + _from_t8(dv81)
    cts = (dkk0 + dkk1, dd0, dk0 + dkd0_p, db0, dd1, dk1 + dkd1_p, db1)
    dps, g["rwkv_k_k"], g["rwkv_k_a"], g["rwkv_w0"], g["rwkv_w2"], g["rwkv_a0"], g["rwkv_a2"] = _pre_bwd(
        ps, k_k, k_a, w0, w2, a0, a2, dr, dv, cts)
    drs, g["rwkv_mu"] = _shift_bwd(proj, mu, dps)

    dproj = jnp.concatenate([daq.astype(BF16), dak.astype(BF16), dav.astype(BF16), dag.astype(BF16), drs.astype(BF16),
                             drg.astype(BF16), dxq.astype(BF16), dxg.astype(BF16), dmg], axis=1)
    g["w_in"] = _matmul(h.T, dproj, mode="nn", m=D, n=NIN, k=s, tm=512, tn=896, tk=s, name="grad_w_in")
    grad_x, g["norm_g"] = _in_bwd(dproj, w_in, x, norm_g, dout)
    g["rwkv_r_k"] = g["rwkv_r_k"].reshape(AH, HD)
    return loss_sum, grad_x, g


WEIGHTS = ['norm_g', 'mem_norm_g', 'w_in', 'gate_b', 'attn_q_norm_g', 'attn_k_norm_g', 'attn_sink', 'attn_w_o',
           'rwkv_mu', 'rwkv_k_k', 'rwkv_k_a', 'rwkv_r_k', 'rwkv_w0', 'rwkv_w2', 'rwkv_a0', 'rwkv_a2', 'rwkv_ln_w',
           'rwkv_ln_b', 'rwkv_w_o', 'x_w_kv', 'x_q_norm_g', 'x_k_norm_g', 'x_w_o', 'w_out']
BIG = ['w_in', 'attn_w_o', 'rwkv_w_o', 'x_w_kv', 'x_w_o', 'w_out']
COL_SHARDED = ['w_in', 'attn_w_o', 'rwkv_w_o', 'x_w_o']
LORA = ['rwkv_w0', 'rwkv_w2', 'rwkv_a0', 'rwkv_a2']
SMALL = [n for n in WEIGHTS if n not in BIG]


def _unshard_cols(stack):
    return jnp.concatenate([stack[i] for i in range(4)], axis=-1)


def _shard_cols(full):
    w = full.shape[-1] // 4
    return [full[..., i * w:(i + 1) * w] for i in range(4)]


def kernel(x, mem, norm_g, mem_norm_g, w_in, gate_b, attn_q_norm_g, attn_k_norm_g, attn_sink, attn_w_o, rwkv_mu, rwkv_k_k, rwkv_k_a, rwkv_r_k, rwkv_w0, rwkv_w2, rwkv_a0, rwkv_a2, rwkv_ln_w, rwkv_ln_b, rwkv_w_o, x_w_kv, x_q_norm_g, x_k_norm_g, x_w_o, w_out, loss_target, m_norm_g, m_mem_norm_g, m_w_in, m_gate_b, m_attn_q_norm_g, m_attn_k_norm_g, m_attn_sink, m_attn_w_o, m_rwkv_mu, m_rwkv_k_k, m_rwkv_k_a, m_rwkv_r_k, m_rwkv_w0, m_rwkv_w2, m_rwkv_a0, m_rwkv_a2, m_rwkv_ln_w, m_rwkv_ln_b, m_rwkv_w_o, m_x_w_kv, m_x_q_norm_g, m_x_k_norm_g, m_x_w_o, m_w_out, v_norm_g, v_mem_norm_g, v_w_in, v_gate_b, v_attn_q_norm_g, v_attn_k_norm_g, v_attn_sink, v_attn_w_o, v_rwkv_mu, v_rwkv_k_k, v_rwkv_k_a, v_rwkv_r_k, v_rwkv_w0, v_rwkv_w2, v_rwkv_a0, v_rwkv_a2, v_rwkv_ln_w, v_rwkv_ln_b, v_rwkv_w_o, v_x_w_kv, v_x_q_norm_g, v_x_k_norm_g, v_x_w_o, v_w_out):
    args = dict(locals())
    canon = lambda a: a[0] if a.ndim > 2 else a
    w = {n: canon(args[n]) for n in WEIGHTS}
    m = {n: canon(args["m_" + n]) for n in WEIGHTS}
    v = {n: canon(args["v_" + n]) for n in WEIGHTS}
    shard = 2 * lax.axis_index("x") + lax.axis_index("y")

    local = [w[n].astype(BF16) for n in BIG] + [w[n].reshape(2, -1, w[n].shape[-1]) for n in LORA]
    stacks = dict(zip(BIG + LORA, _gather_shards(local, "gather_weights")))
    full = {}
    for n in COL_SHARDED:
        full[n] = _unshard_cols(stacks[n])
    for n in LORA:
        full[n] = _unshard_cols(stacks[n]).reshape(w[n].shape[:-1] + (RW,))
    full["x_w_kv"] = stacks["x_w_kv"].reshape(D, 2 * XW)
    full["w_out"] = stacks["w_out"].reshape(D, D)

    loss_sum, grad_x, g = _local_step(
        x[0], mem[0], loss_target[0], w["norm_g"], w["mem_norm_g"], full["w_in"], w["gate_b"], w["attn_q_norm_g"],
        w["attn_k_norm_g"], w["attn_sink"], full["attn_w_o"], w["rwkv_mu"], w["rwkv_k_k"], w["rwkv_k_a"], w["rwkv_r_k"],
        full["rwkv_w0"], full["rwkv_w2"], full["rwkv_a0"], full["rwkv_a2"], w["rwkv_ln_w"], w["rwkv_ln_b"],
        full["rwkv_w_o"], full["x_w_kv"], w["x_q_norm_g"], w["x_k_norm_g"], full["x_w_o"], full["w_out"])

    loss = lax.psum(0.5 * loss_sum / D, ("x", "y", "c"))

    def as_stack(n, dtype):
        if n in COL_SHARDED:
            return jnp.stack([p.astype(dtype) for p in _shard_cols(g[n])])
        return g[n].reshape((4, g[n].shape[0] // 4) + g[n].shape[1:]).astype(dtype)

    core = lax.axis_index("c").astype(jnp.int32).reshape(1)
    sibling = _pair_exchange([as_stack(n, BF16) for n in BIG], "pair_exchange")
    pair32, pair16 = [], []
    for n, th in zip(BIG, sibling):
        a32, a16 = _pair_sum(as_stack(n, F32), th, core, "pair_sum_" + n)
        pair32.append(a32)
        pair16.append(a16)
    recv = _scatter_shards(pair16, "scatter_grads")
    halves = []
    for n, p32, r in zip(BIG, pair32, recv):
        own = lax.dynamic_index_in_dim(p32, shard, 0, keepdims=False)
        halves.append(_sum_parts([own, r[0], r[1], r[2]], "sum_" + n))
    other_halves = _swap_sibling(halves, "swap_halves")

    out_g, out_d, out_m, out_v = {}, {}, {}, {}
    for n, mine, theirs in zip(BIG, halves, other_halves):
        out_g[n], out_d[n], out_m[n], out_v[n] = _adamw_halves(mine, theirs, core, w[n], m[n], v[n], "adamw_" + n)

    flat = jnp.concatenate([g[n].reshape(-1) for n in SMALL])
    total = flat.shape[0]
    padded = -(-total // 1024) * 1024
    flat = jnp.pad(flat, (0, padded - total)).reshape(padded // 128, 128)
    red = _all_reduce_small(flat).reshape(-1)
    off = 0
    gs = {}
    for n in SMALL:
        size = g[n].size
        t = red[off:off + size].reshape(g[n].shape)
        off += size
        if n in LORA:
            wd = t.shape[-1] // 4
            t = lax.dynamic_slice_in_dim(t, shard * wd, wd, axis=t.ndim - 1)
        gs[n] = t

    def pack(d):
        f = jnp.concatenate([d[n].reshape(-1) for n in SMALL])
        return jnp.pad(f, (0, -(-f.shape[0] // 1024) * 1024 - f.shape[0])).reshape(-1, 128)

    pg, pd, pm, pv = _adamw([pack(gs)], pack(w), pack(m), pack(v), "adamw_small")
    off = 0
    for n in SMALL:
        size = w[n].size
        for dst, src in ((out_g, pg), (out_d, pd), (out_m, pm), (out_v, pv)):
            dst[n] = src.reshape(-1)[off:off + size].reshape(w[n].shape)
        off += size

    lead = lambda d: [d[n][None] if args[n].ndim > 2 else d[n] for n in WEIGHTS]
    return (loss, grad_x[None], *lead(out_g), *lead(out_d), *lead(out_m), *lead(out_v))
```

```python
import functools

import jax
import jax.numpy as jnp
from jax import lax
from jax.experimental import pallas as pl
from jax.experimental.pallas import tpu as pltpu

F32 = jnp.float32
BF16 = jnp.bfloat16
HI = lax.Precision.HIGHEST
MESH = pl.DeviceIdType.MESH

D = 2048
NMEM = 256
NORM_EPS = 1e-6
NEG_INF = -1e30
GN_EPS = 64e-5
HD = 64
AH = 12
AKV = 4
RW = 768
XH = 4
XD = 128
XW = 512
NIN = 12544
RSW = 2560
C_AQ, C_AK, C_AV, C_AG, C_RS, C_RG, C_XQ, C_XG, C_MG = 0, 768, 1024, 1280, 2048, 4608, 5376, 5888, 6400
WIN = 384
QB = 128
TC = 16
NPAIR = 6

ADAM_LR, ADAM_B1, ADAM_B2, ADAM_EPS, ADAM_WD, ADAM_STEP = 0.001, 0.9, 0.999, 1e-08, 0.01, 10

VMEM_LIMIT = 56 * 1024 * 1024


def _bs(shape, imap):
    return pl.BlockSpec(shape, imap)


def _params(sem=None, vmem=VMEM_LIMIT):
    return pltpu.CompilerParams(dimension_semantics=sem, vmem_limit_bytes=vmem)


def _dot(a, b, dims):
    return lax.dot_general(a.astype(BF16), b.astype(BF16), (dims, ((), ())), preferred_element_type=F32)


@jax.custom_vjp
def _mm_nn(a, b):
    return _dot(a, b, ((1,), (0,)))


def _mm_nn_fwd(a, b):
    return _mm_nn(a, b), (a, b)


def _mm_nn_bwd(res, ct):
    a, b = res
    return _dot(ct, b, ((1,), (1,))), _dot(a, ct, ((0,), (0,)))


_mm_nn.defvjp(_mm_nn_fwd, _mm_nn_bwd)


@jax.custom_vjp
def _mm_nt(a, b):
    return _dot(a, b, ((1,), (1,)))


def _mm_nt_fwd(a, b):
    return _mm_nt(a, b), (a, b)


def _mm_nt_bwd(res, ct):
    a, b = res
    return _dot(ct, b, ((1,), (0,))), _dot(ct, a, ((0,), (0,)))


_mm_nt.defvjp(_mm_nt_fwd, _mm_nt_bwd)


def _seg_matrix(n, seg):
    r = lax.broadcasted_iota(jnp.int32, (n, n), 0) // seg
    c = lax.broadcasted_iota(jnp.int32, (n, n), 1) // seg
    return (r == c).astype(F32)


def _rot_matrix():
    r = lax.broadcasted_iota(jnp.int32, (HD, HD), 0)
    c = lax.broadcasted_iota(jnp.int32, (HD, HD), 1)
    return jnp.where(c == r + HD // 2, 1.0, 0.0).astype(F32) - jnp.where(c == r - HD // 2, 1.0, 0.0).astype(F32)


def _hdot(a, m):
    return jnp.dot(a, m, precision=HI, preferred_element_type=F32)


def _rms(t, g):
    return t * lax.rsqrt(jnp.mean(t * t, axis=-1, keepdims=True) + NORM_EPS) * g


def _silu(t):
    return t * jax.nn.sigmoid(t)


def _softplus(z):
    return jnp.maximum(z, 0.0) + jnp.log(1.0 + jnp.exp(-jnp.abs(z)))


def _matmul(a, b, *, mode, m, n, k, tm, tn, tk, name, a_off=(0, 0), b_off=(0, 0), out_dtype=F32):
    nk = k // tk
    if mode == "tn":
        a_spec = _bs((tk, tm), lambda i, j, kk: (kk + a_off[0], i + a_off[1]))
        dims = ((0,), (0,))
    else:
        a_spec = _bs((tm, tk), lambda i, j, kk: (i + a_off[0], kk + a_off[1]))
        dims = ((1,), (1,)) if mode == "nt" else ((1,), (0,))
    if mode == "nt":
        b_spec = _bs((tn, tk), lambda i, j, kk: (j + b_off[0], kk + b_off[1]))
    else:
        b_spec = _bs((tk, tn), lambda i, j, kk: (kk + b_off[0], j + b_off[1]))

    def body(a_ref, b_ref, o_ref, acc):
        kk = pl.program_id(2)

        @pl.when(kk == 0)
        def _():
            acc[...] = jnp.zeros_like(acc)

        acc[...] += _dot(a_ref[...], b_ref[...], dims)

        @pl.when(kk == nk - 1)
        def _():
            o_ref[...] = acc[...].astype(out_dtype)

    return pl.pallas_call(
        body, name=name, grid=(m // tm, n // tn, nk),
        in_specs=[a_spec, b_spec], out_specs=_bs((tm, tn), lambda i, j, kk: (i, j)),
        out_shape=jax.ShapeDtypeStruct((m, n), out_dtype),
        scratch_shapes=[pltpu.VMEM((tm, tn), F32)],
        compiler_params=_params(("parallel", "parallel", "arbitrary")),
    )(a, b)


def _proj_fwd(x, g, w):
    s = x.shape[0]
    tm, tn = min(512, s), 896

    def body(x_ref, g_ref, w_ref, o_ref, h_ref, hs):
        @pl.when(pl.program_id(1) == 0)
        def _():
            h = _rms(x_ref[...], g_ref[...]).astype(BF16)
            hs[...] = h
            h_ref[...] = h

        o_ref[...] = jnp.dot(hs[...], w_ref[...], preferred_element_type=F32)

    return pl.pallas_call(
        body, name="proj_fwd", grid=(s // tm, NIN // tn),
        in_specs=[_bs((tm, D), lambda i, j: (i, 0)), _bs((1, D), lambda i, j: (0, 0)), _bs((D, tn), lambda i, j: (0, j))],
        out_specs=[_bs((tm, tn), lambda i, j: (i, j)), _bs((tm, D), lambda i, j: (i, 0))],
        out_shape=[jax.ShapeDtypeStruct((s, NIN), F32), jax.ShapeDtypeStruct((s, D), BF16)],
        scratch_shapes=[pltpu.VMEM((tm, D), BF16)],
        compiler_params=_params(("parallel", "arbitrary")),
    )(x, g, w)


def _rope(t, cos, sin, rot):
    return t * cos + _hdot(t, rot) * sin


def _attn_tile(qs, ks, vs, gs, sinks, gq, gk, cq, sq, ck, sk, mask, rot):
    outs = []
    for hk in range(AKV):
        kh = _rope(_rms(ks[hk], gk), ck, sk, rot)
        for g in range(AH // AKV):
            h = hk * (AH // AKV) + g
            qh = _rope(_rms(qs[h], gq), cq, sq, rot)
            sc = _mm_nt(qh, kh) * (HD ** -0.5)
            sc = jnp.where(mask, sc, NEG_INF)
            mx = lax.stop_gradient(jnp.maximum(jnp.max(sc, axis=-1, keepdims=True), sinks[h]))
            p = jnp.exp(sc - mx)
            den = jnp.sum(p, axis=-1, keepdims=True) + jnp.exp(sinks[h] - mx)
            o = _mm_nn(p / den, vs[hk])
            outs.append(o * _silu(gs[h]))
    return outs


def _attn_load(n, s, aq_ref, ak_ref, av_ref, ag_refs, cos_ref, sin_ref, sink_ref):
    start = pl.multiple_of(jnp.clip((n - 1) * QB, 0, s - WIN), QB)
    q0 = pl.multiple_of(n * QB, QB)
    qs = [aq_ref[:, h * HD:(h + 1) * HD] for h in range(AH)]
    ks = [ak_ref[pl.ds(start, WIN), h * HD:(h + 1) * HD] for h in range(AKV)]
    vs = [av_ref[pl.ds(start, WIN), h * HD:(h + 1) * HD] for h in range(AKV)]
    gs = [ag_refs[h // 4][:, (h % 4) * HD:(h % 4 + 1) * HD] for h in range(AH)]
    sinks = [sink_ref[0:1, h:h + 1] for h in range(AH)]
    cq, sq = cos_ref[pl.ds(q0, QB), :], sin_ref[pl.ds(q0, QB), :]
    ck, sk = cos_ref[pl.ds(start, WIN), :], sin_ref[pl.ds(start, WIN), :]
    qpos = q0 + lax.broadcasted_iota(jnp.int32, (QB, WIN), 0)
    kpos = start + lax.broadcasted_iota(jnp.int32, (QB, WIN), 1)
    mask = jnp.abs(kpos - qpos) <= QB
    return start, qs, ks, vs, gs, sinks, cq, sq, ck, sk, mask


def _attn_specs(s):
    return [
        _bs((QB, 768), lambda n: (n, 0)),
        _bs((s, 256), lambda n: (0, C_AK // 256)),
        _bs((s, 256), lambda n: (0, C_AV // 256)),
        _bs((QB, 256), lambda n: (n, C_AG // 256)),
        _bs((QB, 256), lambda n: (n, C_AG // 256 + 1)),
        _bs((QB, 256), lambda n: (n, C_AG // 256 + 2)),
        _bs((s, HD), lambda n: (0, 0)),
        _bs((s, HD), lambda n: (0, 0)),
        _bs((1, HD), lambda n: (0, 0)),
        _bs((1, HD), lambda n: (0, 0)),
        _bs((1, AH), lambda n: (0, 0)),
    ]


def _attn_fwd(proj, cos, sin, gq, gk, sink):
    s = proj.shape[0]

    def body(aq_ref, ak_ref, av_ref, ag0, ag1, ag2, cos_ref, sin_ref, gq_ref, gk_ref, sink_ref, o_ref):
        n = pl.program_id(0)
        _, qs, ks, vs, gs, sinks, cq, sq, ck, sk, mask = _attn_load(
            n, s, aq_ref, ak_ref, av_ref, (ag0, ag1, ag2), cos_ref, sin_ref, sink_ref)
        outs = _attn_tile(qs, ks, vs, gs, sinks, gq_ref[...], gk_ref[...], cq, sq, ck, sk, mask, _rot_matrix())
        for h in range(AH):
            o_ref[:, h * HD:(h + 1) * HD] = outs[h]

    return pl.pallas_call(
        body, name="attn_fwd", grid=(s // QB,),
        in_specs=_attn_specs(s), out_specs=_bs((QB, 768), lambda n: (n, 0)),
        out_shape=jax.ShapeDtypeStruct((s, 768), F32),
        compiler_params=_params(("arbitrary",)),
    )(proj, proj, proj, proj, proj, proj, cos, sin, gq, gk, sink)


def _attn_bwd(proj, cos, sin, gq, gk, sink, dy):
    s = proj.shape[0]

    def body(aq_ref, ak_ref, av_ref, ag0, ag1, ag2, cos_ref, sin_ref, gq_ref, gk_ref, sink_ref, dy_ref,
             daq_ref, dak_ref, dav_ref, dag_ref, dgq_ref, dgk_ref, dsink_ref):
        n = pl.program_id(0)

        @pl.when(n == 0)
        def _():
            dak_ref[...] = jnp.zeros_like(dak_ref)
            dav_ref[...] = jnp.zeros_like(dav_ref)
            dgq_ref[...] = jnp.zeros_like(dgq_ref)
            dgk_ref[...] = jnp.zeros_like(dgk_ref)
            dsink_ref[...] = jnp.zeros_like(dsink_ref)

        start, qs, ks, vs, gs, sinks, cq, sq, ck, sk, mask = _attn_load(
            n, s, aq_ref, ak_ref, av_ref, (ag0, ag1, ag2), cos_ref, sin_ref, sink_ref)
        rot = _rot_matrix()

        def f(qs, ks, vs, gs, sinks, gq, gk):
            return _attn_tile(qs, ks, vs, gs, sinks, gq, gk, cq, sq, ck, sk, mask, rot)

        _, vjp = jax.vjp(f, qs, ks, vs, gs, sinks, gq_ref[...], gk_ref[...])
        dys = [dy_ref[:, h * HD:(h + 1) * HD] for h in range(AH)]
        dqs, dks, dvs, dgs, dsinks, dgq, dgk = vjp(dys)
        for h in range(AH):
            daq_ref[:, h * HD:(h + 1) * HD] = dqs[h]
            dag_ref[:, h * HD:(h + 1) * HD] = dgs[h]
            dsink_ref[0:1, h:h + 1] += dsinks[h]
        for h in range(AKV):
            dak_ref[pl.ds(start, WIN), h * HD:(h + 1) * HD] += dks[h]
            dav_ref[pl.ds(start, WIN), h * HD:(h + 1) * HD] += dvs[h]
        dgq_ref[...] += dgq
        dgk_ref[...] += dgk

    whole = lambda shape: _bs(shape, lambda n: (0, 0))
    return pl.pallas_call(
        body, name="attn_bwd", grid=(s // QB,),
        in_specs=_attn_specs(s) + [_bs((QB, 768), lambda n: (n, 0))],
        out_specs=[_bs((QB, 768), lambda n: (n, 0)), whole((s, 256)), whole((s, 256)), _bs((QB, 768), lambda n: (n, 0)),
                   whole((1, HD)), whole((1, HD)), whole((1, AH))],
        out_shape=[jax.ShapeDtypeStruct((s, 768), F32), jax.ShapeDtypeStruct((s, 256), F32),
                   jax.ShapeDtypeStruct((s, 256), F32), jax.ShapeDtypeStruct((s, 768), F32),
                   jax.ShapeDtypeStruct((1, HD), F32), jax.ShapeDtypeStruct((1, HD), F32),
                   jax.ShapeDtypeStruct((1, AH), F32)],
        compiler_params=_params(("arbitrary",)),
    )(proj, proj, proj, proj, proj, proj, cos, sin, gq, gk, sink, dy)


def _mem_kv(mem, g, w):
    def body(m_ref, g_ref, w_ref, o_ref, mn_ref):
        mn = _rms(m_ref[...], g_ref[...]).astype(BF16)
        mn_ref[...] = mn
        o_ref[...] = jnp.dot(mn, w_ref[...], preferred_element_type=F32)

    return pl.pallas_call(
        body, name="mem_kv",
        out_shape=[jax.ShapeDtypeStruct((NMEM, 2 * XW), F32), jax.ShapeDtypeStruct((NMEM, D), BF16)],
        compiler_params=_params(),
    )(mem, g, w)


def _xattn_tile(qs, gs, kms, vms, gxq, gxk):
    outs = []
    for h in range(XH):
        q = _rms(qs[h], gxq)
        km = _rms(kms[h], gxk)
        sc = _mm_nt(q, km) * (XD ** -0.5)
        mx = lax.stop_gradient(jnp.max(sc, axis=-1, keepdims=True))
        p = jnp.exp(sc - mx)
        p = p / jnp.sum(p, axis=-1, keepdims=True)
        outs.append(_mm_nn(p, vms[h]) * _silu(gs[h]))
    return outs


XT = 256


def _xattn_specs():
    return [
        _bs((XT, 256), lambda i: (i, C_XQ // 256)), _bs((XT, 256), lambda i: (i, C_XQ // 256 + 1)),
        _bs((XT, 256), lambda i: (i, C_XG // 256)), _bs((XT, 256), lambda i: (i, C_XG // 256 + 1)),
        _bs((NMEM, 2 * XW), lambda i: (0, 0)),
        _bs((1, XD), lambda i: (0, 0)), _bs((1, XD), lambda i: (0, 0)),
    ]


def _xattn_load(q0, q1, g0, g1, mkv_ref):
    qs = [(q0, q1)[h // 2][:, (h % 2) * XD:(h % 2 + 1) * XD] for h in range(XH)]
    gs = [(g0, g1)[h // 2][:, (h % 2) * XD:(h % 2 + 1) * XD] for h in range(XH)]
    kms = [mkv_ref[:, h * XD:(h + 1) * XD] for h in range(XH)]
    vms = [mkv_ref[:, XW + h * XD:XW + (h + 1) * XD] for h in range(XH)]
    return qs, gs, kms, vms


def _xattn_fwd(proj, mkv, gxq, gxk):
    s = proj.shape[0]

    def body(q0, q1, g0, g1, mkv_ref, gxq_ref, gxk_ref, o_ref):
        qs, gs, kms, vms = _xattn_load(q0, q1, g0, g1, mkv_ref)
        outs = _xattn_tile(qs, gs, kms, vms, gxq_ref[...], gxk_ref[...])
        for h in range(XH):
            o_ref[:, h * XD:(h + 1) * XD] = outs[h]

    return pl.pallas_call(
        body, name="xattn_fwd", grid=(s // XT,),
        in_specs=_xattn_specs(), out_specs=_bs((XT, XW), lambda i: (i, 0)),
        out_shape=jax.ShapeDtypeStruct((s, XW), F32),
        compiler_params=_params(("arbitrary",)),
    )(proj, proj, proj, proj, mkv, gxq, gxk)


def _xattn_bwd(proj, mkv, gxq, gxk, dy):
    s = proj.shape[0]

    def body(q0, q1, g0, g1, mkv_ref, gxq_ref, gxk_ref, dy_ref, dq_ref, dg_ref, dmkv_ref, dgxq_ref, dgxk_ref):
        @pl.when(pl.program_id(0) == 0)
        def _():
            dmkv_ref[...] = jnp.zeros_like(dmkv_ref)
            dgxq_ref[...] = jnp.zeros_like(dgxq_ref)
            dgxk_ref[...] = jnp.zeros_like(dgxk_ref)

        qs, gs, kms, vms = _xattn_load(q0, q1, g0, g1, mkv_ref)
        _, vjp = jax.vjp(_xattn_tile, qs, gs, kms, vms, gxq_ref[...], gxk_ref[...])
        dqs, dgs, dkms, dvms, dgxq, dgxk = vjp([dy_ref[:, h * XD:(h + 1) * XD] for h in range(XH)])
        for h in range(XH):
            dq_ref[:, h * XD:(h + 1) * XD] = dqs[h]
            dg_ref[:, h * XD:(h + 1) * XD] = dgs[h]
            dmkv_ref[:, h * XD:(h + 1) * XD] += dkms[h]
            dmkv_ref[:, XW + h * XD:XW + (h + 1) * XD] += dvms[h]
        dgxq_ref[...] += dgxq
        dgxk_ref[...] += dgxk

    whole = lambda shape: _bs(shape, lambda i: (0, 0))
    return pl.pallas_call(
        body, name="xattn_bwd", grid=(s // XT,),
        in_specs=_xattn_specs() + [_bs((XT, XW), lambda i: (i, 0))],
        out_specs=[_bs((XT, XW), lambda i: (i, 0)), _bs((XT, XW), lambda i: (i, 0)), whole((NMEM, 2 * XW)),
                   whole((1, XD)), whole((1, XD))],
        out_shape=[jax.ShapeDtypeStruct((s, XW), F32), jax.ShapeDtypeStruct((s, XW), F32),
                   jax.ShapeDtypeStruct((NMEM, 2 * XW), F32), jax.ShapeDtypeStruct((1, XD), F32),
                   jax.ShapeDtypeStruct((1, XD), F32)],
        compiler_params=_params(("arbitrary",)),
    )(proj, proj, proj, proj, mkv, gxq, gxk, dy)


def _mem_bwd(mem, g, dmn):
    def body(m_ref, dmn_ref, o_ref):
        m = m_ref[...]
        r = lax.rsqrt(jnp.mean(m * m, axis=-1, keepdims=True) + NORM_EPS)
        o_ref[...] = jnp.sum(dmn_ref[...] * m * r, axis=0, keepdims=True)

    del g
    return pl.pallas_call(body, name="mem_norm_bwd", out_shape=jax.ShapeDtypeStruct((1, D), F32),
                          compiler_params=_params())(mem, dmn)


SHIFT_W = 512


def _shift_rows(p, s):
    row = lax.broadcasted_iota(jnp.int32, p.shape, 0)
    prev = jnp.where(row == 0, 0.0, pltpu.roll(p, 1, 0))
    nxt = jnp.where(row == s - 1, 0.0, pltpu.roll(p, s - 1, 0))
    return prev, nxt


def _shift_fwd(proj, mu):
    s = proj.shape[0]

    def body(p_ref, mu_ref, o_ref):
        p = p_ref[...]
        prev, nxt = _shift_rows(p, s)
        o_ref[...] = p + mu_ref[...] * (0.5 * (prev + nxt) - p)

    return pl.pallas_call(
        body, name="shift_fwd", grid=(RSW // SHIFT_W,),
        in_specs=[_bs((s, SHIFT_W), lambda j: (0, C_RS // SHIFT_W + j)), _bs((1, SHIFT_W), lambda j: (0, j))],
        out_specs=_bs((s, SHIFT_W), lambda j: (0, j)),
        out_shape=jax.ShapeDtypeStruct((s, RSW), F32),
        compiler_params=_params(("parallel",)),
    )(proj, mu)


def _shift_bwd(proj, mu, dps):
    s = proj.shape[0]

    def body(p_ref, mu_ref, g_ref, o_ref, dmu_ref):
        p, g, mu_v = p_ref[...], g_ref[...], mu_ref[...]
        prev, nxt = _shift_rows(p, s)
        dmu_ref[...] = jnp.sum(g * (0.5 * (prev + nxt) - p), axis=0, keepdims=True)
        mg = mu_v * g
        down, up = _shift_rows(mg, s)
        o_ref[...] = g * (1.0 - mu_v) + 0.5 * (down + up)

    return pl.pallas_call(
        body, name="shift_bwd", grid=(RSW // SHIFT_W,),
        in_specs=[_bs((s, SHIFT_W), lambda j: (0, C_RS // SHIFT_W + j)), _bs((1, SHIFT_W), lambda j: (0, j)),
                  _bs((s, SHIFT_W), lambda j: (0, j))],
        out_specs=[_bs((s, SHIFT_W), lambda j: (0, j)), _bs((1, SHIFT_W), lambda j: (0, j))],
        out_shape=[jax.ShapeDtypeStruct((s, RSW), F32), jax.ShapeDtypeStruct((1, RSW), F32)],
        compiler_params=_params(("parallel",)),
    )(proj, mu, dps)


def _pre_tile(k, wf, wb, af, ab, k_k, k_a, w0s, w2s, a0s, a2s, seg):
    kx = k * k_k
    ss = _hdot(kx * kx, seg)
    kk = kx / jnp.maximum(jnp.sqrt(ss), 1e-12)
    outs = [kk]
    for d, (w_in, a_in) in enumerate(((wf, af), (wb, ab))):
        z = w0s[d] + _mm_nn(jnp.tanh(w_in), w2s[d])
        wd = -_softplus(-z) - 0.5
        dec = jnp.exp(-jnp.exp(wd))
        ad = jax.nn.sigmoid(a0s[d] + _mm_nn(a_in, a2s[d]))
        kd = k * (1.0 + (ad - 1.0) * k_a)
        outs += [dec, kd, kk * ad]
    return outs


PT = 256


def _pre_load(ps_ref, kk_ref, ka_ref, w0_ref, w2_ref, a0_ref, a2_ref):
    k = ps_ref[:, RW:2 * RW]
    wf, wb = ps_ref[:, 3 * RW:3 * RW + 64], ps_ref[:, 3 * RW + 64:3 * RW + 128]
    af, ab = ps_ref[:, 3 * RW + 128:3 * RW + 192], ps_ref[:, 3 * RW + 192:3 * RW + 256]
    w0s = [w0_ref[0:1, :], w0_ref[1:2, :]]
    a0s = [a0_ref[0:1, :], a0_ref[1:2, :]]
    w2s = [w2_ref[0], w2_ref[1]]
    a2s = [a2_ref[0], a2_ref[1]]
    return (k, wf, wb, af, ab, kk_ref[...], ka_ref[...], w0s, w2s, a0s, a2s)


def _pre_specs():
    c = lambda shape: _bs(shape, lambda i: tuple(0 for _ in shape))
    return [_bs((PT, RSW), lambda i: (i, 0)), c((1, RW)), c((1, RW)), c((2, RW)), c((2, 64, RW)), c((2, RW)),
            c((2, 64, RW))]


def _pre_fwd(ps, k_k, k_a, w0, w2, a0, a2):
    s = ps.shape[0]

    def body(ps_ref, kk_ref, ka_ref, w0_ref, w2_ref, a0_ref, a2_ref, *outs):
        args = _pre_load(ps_ref, kk_ref, ka_ref, w0_ref, w2_ref, a0_ref, a2_ref)
        res = _pre_tile(*args, _seg_matrix(RW, HD))
        for o_ref, v in zip(outs, res):
            o_ref[...] = v

    return pl.pallas_call(
        body, name="rwkv_pre_fwd", grid=(s // PT,),
        in_specs=_pre_specs(), out_specs=[_bs((PT, RW), lambda i: (i, 0))] * 7,
        out_shape=[jax.ShapeDtypeStruct((s, RW), F32)] * 7,
        compiler_params=_params(("parallel",)),
    )(ps, k_k, k_a, w0, w2, a0, a2)


def _pre_bwd(ps, k_k, k_a, w0, w2, a0, a2, dr, dv, cts):
    s = ps.shape[0]

    def body(ps_ref, kk_ref, ka_ref, w0_ref, w2_ref, a0_ref, a2_ref, dr_ref, dv_ref, c0, c1, c2, c3, c4, c5, c6,
             dps_ref, dkk_ref, dka_ref, dw0_ref, dw2_ref, da0_ref, da2_ref):
        @pl.when(pl.program_id(0) == 0)
        def _():
            for r in (dkk_ref, dka_ref, dw0_ref, dw2_ref, da0_ref, da2_ref):
                r[...] = jnp.zeros_like(r)

        args = _pre_load(ps_ref, kk_ref, ka_ref, w0_ref, w2_ref, a0_ref, a2_ref)
        seg = _seg_matrix(RW, HD)
        _, vjp = jax.vjp(lambda *a: _pre_tile(*a, seg), *args)
        dk, dwf, dwb, daf, dab, dk_k, dk_a, dw0s, dw2s, da0s, da2s = vjp([c[...] for c in (c0, c1, c2, c3, c4, c5, c6)])
        dps_ref[:, 0:RW] = dr_ref[...]
        dps_ref[:, RW:2 * RW] = dk
        dps_ref[:, 2 * RW:3 * RW] = dv_ref[...]
        for j, t in enumerate((dwf, dwb, daf, dab)):
            dps_ref[:, 3 * RW + 64 * j:3 * RW + 64 * (j + 1)] = t
        dkk_ref[...] += dk_k
        dka_ref[...] += dk_a
        for d in range(2):
            dw0_ref[d:d + 1, :] += dw0s[d]
            da0_ref[d:d + 1, :] += da0s[d]
            dw2_ref[d] += dw2s[d]
            da2_ref[d] += da2s[d]

    c = lambda shape: _bs(shape, lambda i: tuple(0 for _ in shape))
    row = _bs((PT, RW), lambda i: (i, 0))
    return pl.pallas_call(
        body, name="rwkv_pre_bwd", grid=(s // PT,),
        in_specs=_pre_specs() + [row] * 9,
        out_specs=[_bs((PT, RSW), lambda i: (i, 0)), c((1, RW)), c((1, RW)), c((2, RW)), c((2, 64, RW)), c((2, RW)),
                   c((2, 64, RW))],
        out_shape=[jax.ShapeDtypeStruct((s, RSW), F32), jax.ShapeDtypeStruct((1, RW), F32),
                   jax.ShapeDtypeStruct((1, RW), F32), jax.ShapeDtypeStruct((2, RW), F32),
                   jax.ShapeDtypeStruct((2, 64, RW), F32), jax.ShapeDtypeStruct((2, RW), F32),
                   jax.ShapeDtypeStruct((2, 64, RW), F32)],
        compiler_params=_params(("arbitrary",)),
    )(ps, k_k, k_a, w0, w2, a0, a2, dr, dv, *cts)


def _post_tile(y0, y1, r, v, kd0, kd1, rg, r_k, ln_w, ln_b, seg):
    ysum = y0 + y1
    bonus = (_hdot(r * kd0 * r_k, seg) + _hdot(r * kd1 * r_k, seg)) * v
    mean = _hdot(ysum, seg) * (1.0 / HD)
    cen = ysum - mean
    var = _hdot(cen * cen, seg) * (1.0 / HD)
    y = cen * lax.rsqrt(var + GN_EPS) * ln_w + ln_b + bonus
    return y * _silu(rg)


def _post_specs():
    row = _bs((PT, RW), lambda i: (i, 0))
    c = _bs((1, RW), lambda i: (0, 0))
    return [row, row, _bs((PT, RW), lambda i: (i, 0)), _bs((PT, RW), lambda i: (i, 2)), row, row,
            _bs((PT, RW), lambda i: (i, C_RG // RW)), c, c, c]


def _post_fwd(y0, y1, ps, kd0, kd1, proj, r_k, ln_w, ln_b):
    s = ps.shape[0]

    def body(y0_ref, y1_ref, r_ref, v_ref, kd0_ref, kd1_ref, rg_ref, rk_ref, lw_ref, lb_ref, o_ref):
        o_ref[...] = _post_tile(y0_ref[...], y1_ref[...], r_ref[...], v_ref[...], kd0_ref[...], kd1_ref[...],
                                rg_ref[...], rk_ref[...], lw_ref[...], lb_ref[...], _seg_matrix(RW, HD))

    return pl.pallas_call(
        body, name="rwkv_post_fwd", grid=(s // PT,),
        in_specs=_post_specs(), out_specs=_bs((PT, RW), lambda i: (i, 0)),
        out_shape=jax.ShapeDtypeStruct((s, RW), F32),
        compiler_params=_params(("parallel",)),
    )(y0, y1, ps, ps, kd0, kd1, proj, r_k, ln_w, ln_b)


def _post_bwd(y0, y1, ps, kd0, kd1, proj, r_k, ln_w, ln_b, dy):
    s = ps.shape[0]

    def body(y0_ref, y1_ref, r_ref, v_ref, kd0_ref, kd1_ref, rg_ref, rk_ref, lw_ref, lb_ref, dy_ref,
             dys_ref, dr_ref, dv_ref, dkd0_ref, dkd1_ref, drg_ref, drk_ref, dlw_ref, dlb_ref):
        @pl.when(pl.program_id(0) == 0)
        def _():
            for r in (drk_ref, dlw_ref, dlb_ref):
                r[...] = jnp.zeros_like(r)

        seg = _seg_matrix(RW, HD)
        args = [t[...] for t in (y0_ref, y1_ref, r_ref, v_ref, kd0_ref, kd1_ref, rg_ref, rk_ref, lw_ref, lb_ref)]
        _, vjp = jax.vjp(lambda *a: _post_tile(*a, seg), *args)
        dy0, _, dr, dv, dkd0, dkd1, drg, drk, dlw, dlb = vjp(dy_ref[...])
        dys_ref[...] = dy0
        dr_ref[...] = dr
        dv_ref[...] = dv
        dkd0_ref[...] = dkd0
        dkd1_ref[...] = dkd1
        drg_ref[...] = drg
        drk_ref[...] += drk
        dlw_ref[...] += dlw
        dlb_ref[...] += dlb

    row = _bs((PT, RW), lambda i: (i, 0))
    c = _bs((1, RW), lambda i: (0, 0))
    return pl.pallas_call(
        body, name="rwkv_post_bwd", grid=(s // PT,),
        in_specs=_post_specs() + [row], out_specs=[row] * 6 + [c] * 3,
        out_shape=[jax.ShapeDtypeStruct((s, RW), F32)] * 6 + [jax.ShapeDtypeStruct((1, RW), F32)] * 3,
        compiler_params=_params(("arbitrary",)),
    )(y0, y1, ps, ps, kd0, kd1, proj, r_k, ln_w, ln_b, dy)


def _ones2():
    r = lax.broadcasted_iota(jnp.int32, (256, 128), 0) % 128 // HD
    c = lax.broadcasted_iota(jnp.int32, (256, 128), 1) // HD
    return (r == c).astype(BF16)


def _split(p):
    hi = p.astype(BF16)
    lo = (p - hi.astype(F32)).astype(BF16)
    return jnp.concatenate([hi, lo], axis=1)


def _to_t8(a):
    s = a.shape[0]
    t = a.reshape(s // 8, 8, NPAIR, 2, HD).transpose(0, 2, 4, 3, 1)
    t = jnp.pad(t, ((0, 0), (0, 0), (0, 0), (0, 0), (0, HD - 8))).reshape(s // 8, NPAIR, HD, 128)
    hi = t.astype(BF16)
    lo = (t - hi.astype(F32)).astype(BF16)
    return jnp.concatenate([hi, lo], axis=-1)


def _from_t8(t8):
    g = t8.shape[0]
    t = t8.reshape(g, NPAIR, HD, 2, HD)[..., :8]
    return t.transpose(0, 4, 1, 3, 2).reshape(g * 8, RW)


def _scan_specs(direction, nc, fwd_order):
    def tb(c):
        sc = c if fwd_order else nc - 1 - c
        return sc if direction == 0 else nc - 1 - sc

    row = _bs((TC, RW), lambda c: (tb(c), 0))
    rowv = _bs((TC, RW), lambda c: (tb(c), 2))
    return row, rowv


def _put_t8(ref, g, u, tiles):
    for p in range(NPAIR):
        ref[g, p, :, u:u + 1] = tiles[p][:, u:u + 1]
        ref[g, p, :, HD + u:HD + u + 1] = tiles[p][:, HD + u:HD + u + 1]


def _scan_fwd(dec, kd, b, ps, kk, vl, direction):
    s = dec.shape[0]
    nc, ng = s // TC, TC // 8
    row, t8_in, t8_out = _scan_specs(direction, nc, True)
    n = NPAIR * HD

    def body(dec_ref, kd_ref, b_ref, r_ref, kk_ref, vl_ref, y8_ref, ck_ref, st):
        @pl.when(pl.program_id(0) == 0)
        def _():
            st[...] = jnp.zeros_like(st)

        ck_ref[0] = st[...]
        ones2 = _ones2()
        lane_u = lax.broadcasted_iota(jnp.int32, (HD, 256), 1) % HD
        tiles = lambda res, k: [res[k * n + p * HD:k * n + (p + 1) * HD] for p in range(NPAIR)]

        def group(gi, carry):
            g = gi if direction == 0 else ng - 1 - gi
            rows8 = pl.ds(pl.multiple_of(g * 8, 8), 8)
            d8, k8, b8, r8, kk8 = (q[rows8, :] for q in (dec_ref, kd_ref, b_ref, r_ref, kk_ref))
            pc = [slice(p * 128, (p + 1) * 128) for p in range(NPAIR)]
            ss = [st[p] for p in range(NPAIR)]
            u_prev = None
            for ui in range(8):
                u = ui if direction == 0 else 7 - ui
                lhs = [_split(ss[p] * kk8[u:u + 1, pc[p]]) for p in range(NPAIR)]
                for p in range(NPAIR):
                    vt = vl_ref[g, p]
                    lhs.append(jnp.where(lane_u == u, vt, jnp.zeros_like(vt)))
                if u_prev is not None:
                    lhs += [_split(ss[p] * r8[u_prev:u_prev + 1, pc[p]]) for p in range(NPAIR)]
                res = jnp.dot(jnp.concatenate(lhs, axis=0), ones2, preferred_element_type=F32)
                if u_prev is not None:
                    _put_t8(y8_ref, g, u_prev, tiles(res, 2))
                sa, vb = tiles(res, 0), tiles(res, 1)
                for p in range(NPAIR):
                    ss[p] = ss[p] * d8[u:u + 1, pc[p]] - sa[p] * b8[u:u + 1, pc[p]] + vb[p] * k8[u:u + 1, pc[p]]
                u_prev = u
            lhs = [_split(ss[p] * r8[u_prev:u_prev + 1, pc[p]]) for p in range(NPAIR)]
            res = jnp.dot(jnp.concatenate(lhs, axis=0), ones2, preferred_element_type=F32)
            _put_t8(y8_ref, g, u_prev, tiles(res, 0))
            for p in range(NPAIR):
                st[p] = ss[p]
            return carry

        lax.fori_loop(0, ng, group, 0)

    return pl.pallas_call(
        body, name=f"rwkv_scan_fwd{direction}", grid=(nc,),
        in_specs=[row, row, row, row, row, t8_in],
        out_specs=[t8_out, _bs((1, NPAIR, HD, 128), lambda c: (c, 0, 0, 0))],
        out_shape=[jax.ShapeDtypeStruct((s // 8, NPAIR, HD, 128), F32),
                   jax.ShapeDtypeStruct((nc, NPAIR, HD, 128), F32)],
        scratch_shapes=[pltpu.VMEM((NPAIR, HD, 128), F32)],
        compiler_params=_params(("arbitrary",)),
    )(dec, kd, b, ps, kk, vl)


def _scan_bwd(dec, kd, b, ps, kk, vl, dyl, ck, direction):
    s = dec.shape[0]
    nc, ng = s // TC, TC // 8
    row, t8_in, t8_out = _scan_specs(direction, nc, False)
    n = NPAIR * HD

    def body(dec_ref, kd_ref, b_ref, r_ref, kk_ref, vl_ref, dyl_ref, ck_ref,
             dr_ref, dd_ref, db_ref, dk_ref, dkk_ref, dv8_ref, st, sa_s, vb_s, dy_s, ds):
        @pl.when(pl.program_id(0) == 0)
        def _():
            ds[...] = jnp.zeros_like(ds)

        st[0] = ck_ref[0]
        ones2 = _ones2()
        lane_u = lax.broadcasted_iota(jnp.int32, (HD, 256), 1) % HD
        row_id = lax.broadcasted_iota(jnp.int32, (8, 128), 0)
        pc = [slice(p * 128, (p + 1) * 128) for p in range(NPAIR)]
        tiles = lambda res, k: [res[k * n + p * HD:k * n + (p + 1) * HD] for p in range(NPAIR)]

        def fgroup(gi, carry):
            g = gi if direction == 0 else ng - 1 - gi
            rows8 = pl.ds(pl.multiple_of(g * 8, 8), 8)
            d8, k8, b8, kk8 = (q[rows8, :] for q in (dec_ref, kd_ref, b_ref, kk_ref))
            ss = [st[gi * 8, p] for p in range(NPAIR)]
            for ui in range(8):
                u = ui if direction == 0 else 7 - ui
                i = gi * 8 + ui
                lhs = [_split(ss[p] * kk8[u:u + 1, pc[p]]) for p in range(NPAIR)]
                for ref in (vl_ref, dyl_ref):
                    for p in range(NPAIR):
                        t = ref[g, p]
                        lhs.append(jnp.where(lane_u == u, t, jnp.zeros_like(t)))
                res = jnp.dot(jnp.concatenate(lhs, axis=0), ones2, preferred_element_type=F32)
                sa, vb, dyb = tiles(res, 0), tiles(res, 1), tiles(res, 2)
                for p in range(NPAIR):
                    sa_s[i, p] = sa[p]
                    vb_s[i, p] = vb[p]
                    dy_s[i, p] = dyb[p]
                    ss[p] = ss[p] * d8[u:u + 1, pc[p]] - sa[p] * b8[u:u + 1, pc[p]] + vb[p] * k8[u:u + 1, pc[p]]
                    st[i + 1, p] = ss[p]
            return carry

        lax.fori_loop(0, ng, fgroup, 0)

        def bgroup(gj, carry):
            gi = ng - 1 - gj
            g = gi if direction == 0 else ng - 1 - gi
            rows8 = pl.ds(pl.multiple_of(g * 8, 8), 8)
            d8, k8, b8, r8, kk8 = (q[rows8, :] for q in (dec_ref, kd_ref, b_ref, r_ref, kk_ref))
            dss = [ds[p] for p in range(NPAIR)]
            acc = [[jnp.zeros((8, 128), F32) for _ in range(5)] for _ in range(NPAIR)]
            for uj in range(8):
                ui = 7 - uj
                u = ui if direction == 0 else 7 - ui
                i = gi * 8 + ui
                dyb = [dy_s[i, p] for p in range(NPAIR)]
                for p in range(NPAIR):
                    dss[p] = dss[p] + dyb[p] * r8[u:u + 1, pc[p]]
                lhs = [_split(dss[p] * b8[u:u + 1, pc[p]]) for p in range(NPAIR)]
                lhs += [_split(dss[p] * k8[u:u + 1, pc[p]]) for p in range(NPAIR)]
                res = jnp.dot(jnp.concatenate(lhs, axis=0), ones2, preferred_element_type=F32)
                dsa, dvb = tiles(res, 0), tiles(res, 1)
                _put_t8(dv8_ref, g, u, dvb)
                for p in range(NPAIR):
                    sp, sn = st[i, p], st[i + 1, p]
                    outs = (jnp.sum(sn * dyb[p], axis=0, keepdims=True), jnp.sum(dss[p] * sp, axis=0, keepdims=True),
                            -jnp.sum(dss[p] * sa_s[i, p], axis=0, keepdims=True),
                            jnp.sum(dss[p] * vb_s[i, p], axis=0, keepdims=True),
                            -jnp.sum(sp * dsa[p], axis=0, keepdims=True))
                    acc[p] = [jnp.where(row_id == u, o, a_) for o, a_ in zip(outs, acc[p])]
                    dss[p] = dss[p] * d8[u:u + 1, pc[p]] - dsa[p] * kk8[u:u + 1, pc[p]]
            for p in range(NPAIR):
                ds[p] = dss[p]
                for o_ref, a_ in zip((dr_ref, dd_ref, db_ref, dk_ref, dkk_ref), acc[p]):
                    o_ref[rows8, pc[p]] = a_
            return carry

        lax.fori_loop(0, ng, bgroup, 0)

    chunk = lambda k: pltpu.VMEM((k, NPAIR, HD, 128), F32)
    return pl.pallas_call(
        body, name=f"rwkv_scan_bwd{direction}", grid=(nc,),
        in_specs=[row, row, row, row, row, t8_in, t8_in, _bs((1, NPAIR, HD, 128), lambda c: (nc - 1 - c, 0, 0, 0))],
        out_specs=[row] * 5 + [t8_out],
        out_shape=[jax.ShapeDtypeStruct((s, RW), F32)] * 5 + [jax.ShapeDtypeStruct((s // 8, NPAIR, HD, 128), F32)],
        scratch_shapes=[chunk(TC + 1), chunk(TC), chunk(TC), chunk(TC), pltpu.VMEM((NPAIR, HD, 128), F32)],
        compiler_params=_params(("arbitrary",)),
    )(dec, kd, b, ps, kk, vl, dyl, ck)


def _tiles(res, k):
    n = NPAIR * HD
    return [res[k * n + p * HD:k * n + (p + 1) * HD] for p in range(NPAIR)]


def _rows_to_tiles(src_ref, rows8, stage, out_s, base):
    for p in range(NPAIR):
        stage[base + p, 0:8, 0:HD] = src_ref[rows8, p * 128:p * 128 + HD]
        stage[base + p, HD:HD + 8, 0:HD] = src_ref[rows8, p * 128 + HD:(p + 1) * 128]
        out_s[base + p] = _split(stage[base + p].T[0:HD])


def _tiles_to_rows(tile_s, base, dst_ref, rows8):
    for p in range(NPAIR):
        t = jnp.concatenate([tile_s[base + p], jnp.zeros((HD, 128), F32)], axis=0).T
        dst_ref[rows8, p * 128:p * 128 + HD] = t[0:8, 0:HD]
        dst_ref[rows8, p * 128 + HD:(p + 1) * 128] = t[HD:HD + 8, 0:HD]


def _put_cols(tile_s, base, u, tiles):
    for p in range(NPAIR):
        tile_s[base + p, :, u:u + 1] = tiles[p][:, u:u + 1]
        tile_s[base + p, :, HD + u:HD + u + 1] = tiles[p][:, HD + u:HD + u + 1]


def _scan2_fwd(per_dir, ps, kk):
    s = ps.shape[0]
    nc, ng = s // TC, TC // 8
    in_specs, operands, out_specs, out_shape = [], [], [], []
    for d in (0, 1):
        row, rowv = _scan_specs(d, nc, True)
        in_specs += [row] * 5 + [rowv]
        operands += list(per_dir[d]) + [ps, kk, ps]
        out_specs += [row, _bs((1, NPAIR, HD, 128), lambda c: (c, 0, 0, 0))]
        out_shape += [jax.ShapeDtypeStruct((s, RW), F32), jax.ShapeDtypeStruct((nc, NPAIR, HD, 128), F32)]

    def body(*refs):
        ins = [refs[0:6], refs[6:12]]
        y_refs, ck_refs = (refs[12], refs[14]), (refs[13], refs[15])
        st, vt_s, yt_s, stage = refs[16:]

        @pl.when(pl.program_id(0) == 0)
        def _():
            st[...] = jnp.zeros_like(st)
            yt_s[...] = jnp.zeros_like(yt_s)
            stage[...] = jnp.zeros_like(stage)

        for d in (0, 1):
            ck_refs[d][0] = st[d * NPAIR:(d + 1) * NPAIR]
        ones2 = _ones2()
        lane_u = lax.broadcasted_iota(jnp.int32, (HD, 256), 1) % HD
        pc = [slice(p * 128, (p + 1) * 128) for p in range(NPAIR)]

        def group(gi, carry):
            gs = (gi, ng - 1 - gi)
            rows8 = [pl.ds(pl.multiple_of(gs[d] * 8, 8), 8) for d in (0, 1)]
            blk = [[q[rows8[d], :] for q in ins[d][:5]] for d in (0, 1)]
            for d in (0, 1):
                _rows_to_tiles(ins[d][5], rows8[d], stage, vt_s, d * NPAIR)
            ss = [[st[d * NPAIR + p] for p in range(NPAIR)] for d in (0, 1)]
            for ui in range(9):
                us, ups = (ui, 7 - ui), (ui - 1, 8 - ui)
                lhs, where = [], {}
                for d in (0, 1):
                    if ui < 8:
                        where["sa", d] = len(lhs) // NPAIR
                        lhs += [_split(ss[d][p] * blk[d][4][us[d]:us[d] + 1, pc[p]]) for p in range(NPAIR)]
                        where["vb", d] = len(lhs) // NPAIR
                        for p in range(NPAIR):
                            vt = vt_s[d * NPAIR + p]
                            lhs.append(jnp.where(lane_u == us[d], vt, jnp.zeros_like(vt)))
                    if ui > 0:
                        where["y", d] = len(lhs) // NPAIR
                        lhs += [_split(ss[d][p] * blk[d][3][ups[d]:ups[d] + 1, pc[p]]) for p in range(NPAIR)]
                res = jnp.dot(jnp.concatenate(lhs, axis=0), ones2, preferred_element_type=F32)
                for d in (0, 1):
                    d8, k8, b8, _, _ = blk[d]
                    u = us[d]
                    if ui < 8:
                        sa, vb = _tiles(res, where["sa", d]), _tiles(res, where["vb", d])
                        for p in range(NPAIR):
                            ss[d][p] = (ss[d][p] * d8[u:u + 1, pc[p]] - sa[p] * b8[u:u + 1, pc[p]]
                                        + vb[p] * k8[u:u + 1, pc[p]])
                    if ui > 0:
                        _put_cols(yt_s, d * NPAIR, ups[d], _tiles(res, where["y", d]))
            for d in (0, 1):
                _tiles_to_rows(yt_s, d * NPAIR, y_refs[d], rows8[d])
                for p in range(NPAIR):
                    st[d * NPAIR + p] = ss[d][p]
            return carry

        lax.fori_loop(0, ng, group, 0)

    outs = pl.pallas_call(
        body, name="rwkv_scan_fwd", grid=(nc,), in_specs=in_specs, out_specs=out_specs, out_shape=out_shape,
        scratch_shapes=[pltpu.VMEM((2 * NPAIR, HD, 128), F32), pltpu.VMEM((2 * NPAIR, HD, 256), BF16),
                        pltpu.VMEM((2 * NPAIR, HD, 128), F32), pltpu.VMEM((2 * NPAIR, 128, 128), F32)],
        compiler_params=_params(("arbitrary",)),
    )(*operands)
    return [(outs[0], outs[1]), (outs[2], outs[3])]


def _scan2_bwd(per_dir, ps, kk, dy):
    s = ps.shape[0]
    nc, ng = s // TC, TC // 8
    in_specs, operands, out_specs, out_shape = [], [], [], []
    for d in (0, 1):
        row, rowv = _scan_specs(d, nc, False)
        dec, kd, b, ck = per_dir[d]
        in_specs += [row] * 5 + [rowv, row, _bs((1, NPAIR, HD, 128), lambda c: (nc - 1 - c, 0, 0, 0))]
        operands += [dec, kd, b, ps, kk, ps, dy, ck]
        out_specs += [row] * 6
        out_shape += [jax.ShapeDtypeStruct((s, RW), F32)] * 6

    def body(*refs):
        ins = [refs[0:8], refs[8:16]]
        outs = [refs[16:22], refs[22:28]]
        st, sa_s, vb_s, dy_s, ds, vt_s, dyt_s, dvt_s, stage = refs[28:]

        @pl.when(pl.program_id(0) == 0)
        def _():
            dvt_s[...] = jnp.zeros_like(dvt_s)
            stage[...] = jnp.zeros_like(stage)
            ds[...] = jnp.zeros_like(ds)

        for d in (0, 1):
            st[d * (TC + 1)] = ins[d][7][0]
        ones2 = _ones2()
        lane_u = lax.broadcasted_iota(jnp.int32, (HD, 256), 1) % HD
        row_id = lax.broadcasted_iota(jnp.int32, (8, 128), 0)
        pc = [slice(p * 128, (p + 1) * 128) for p in range(NPAIR)]

        def load_rows(gs):
            return [[q[pl.ds(pl.multiple_of(gs[d] * 8, 8), 8), :] for q in ins[d][:5]] for d in (0, 1)]

        def fgroup(gi, carry):
            gs = (gi, ng - 1 - gi)
            blk = load_rows(gs)
            for d in (0, 1):
                rows8 = pl.ds(pl.multiple_of(gs[d] * 8, 8), 8)
                _rows_to_tiles(ins[d][5], rows8, stage, vt_s, d * NPAIR)
                _rows_to_tiles(ins[d][6], rows8, stage, dyt_s, d * NPAIR)
            ss = [[st[d * (TC + 1) + gi * 8, p] for p in range(NPAIR)] for d in (0, 1)]
            for ui in range(8):
                us = (ui, 7 - ui)
                i = gi * 8 + ui
                lhs = []
                for d in (0, 1):
                    kk8 = blk[d][4]
                    lhs += [_split(ss[d][p] * kk8[us[d]:us[d] + 1, pc[p]]) for p in range(NPAIR)]
                    for tile_s in (vt_s, dyt_s):
                        for p in range(NPAIR):
                            t = tile_s[d * NPAIR + p]
                            lhs.append(jnp.where(lane_u == us[d], t, jnp.zeros_like(t)))
                res = jnp.dot(jnp.concatenate(lhs, axis=0), ones2, preferred_element_type=F32)
                for d in (0, 1):
                    d8, k8, b8, _, _ = blk[d]
                    u = us[d]
                    sa, vb, dyb = _tiles(res, 3 * d), _tiles(res, 3 * d + 1), _tiles(res, 3 * d + 2)
                    for p in range(NPAIR):
                        sa_s[d * TC + i, p] = sa[p]
                        vb_s[d * TC + i, p] = vb[p]
                        dy_s[d * TC + i, p] = dyb[p]
                        ss[d][p] = ss[d][p] * d8[u:u + 1, pc[p]] - sa[p] * b8[u:u + 1, pc[p]] + vb[p] * k8[u:u + 1, pc[p]]
                        st[d * (TC + 1) + i + 1, p] = ss[d][p]
            return carry

        lax.fori_loop(0, ng, fgroup, 0)

        def bgroup(gj, carry):
            gi = ng - 1 - gj
            gs = (gi, ng - 1 - gi)
            blk = load_rows(gs)
            dss = [[ds[d * NPAIR + p] for p in range(NPAIR)] for d in (0, 1)]
            acc = [[[jnp.zeros((8, 128), F32) for _ in range(5)] for _ in range(NPAIR)] for _ in (0, 1)]
            for uj in range(8):
                ui = 7 - uj
                us = (ui, 7 - ui)
                i = gi * 8 + ui
                lhs_b, lhs_k, dyb = [], [], [None, None]
                for d in (0, 1):
                    _, k8, b8, r8, _ = blk[d]
                    u = us[d]
                    dyb[d] = [dy_s[d * TC + i, p] for p in range(NPAIR)]
                    for p in range(NPAIR):
                        dss[d][p] = dss[d][p] + dyb[d][p] * r8[u:u + 1, pc[p]]
                    lhs_b += [_split(dss[d][p] * b8[u:u + 1, pc[p]]) for p in range(NPAIR)]
                    lhs_b += [_split(dss[d][p] * k8[u:u + 1, pc[p]]) for p in range(NPAIR)]
                res = jnp.dot(jnp.concatenate(lhs_b + lhs_k, axis=0), ones2, preferred_element_type=F32)
                for d in (0, 1):
                    d8, _, _, _, kk8 = blk[d]
                    u = us[d]
                    dsa, dvb = _tiles(res, 2 * d), _tiles(res, 2 * d + 1)
                    _put_cols(dvt_s, d * NPAIR, u, dvb)
                    for p in range(NPAIR):
                        sp, sn = st[d * (TC + 1) + i, p], st[d * (TC + 1) + i + 1, p]
                        dsv = dss[d][p]
                        vals = (jnp.sum(sn * dyb[d][p], axis=0, keepdims=True), jnp.sum(dsv * sp, axis=0, keepdims=True),
                                -jnp.sum(dsv * sa_s[d * TC + i, p], axis=0, keepdims=True),
                                jnp.sum(dsv * vb_s[d * TC + i, p], axis=0, keepdims=True),
                                -jnp.sum(sp * dsa[p], axis=0, keepdims=True))
                        acc[d][p] = [jnp.where(row_id == u, o, a_) for o, a_ in zip(vals, acc[d][p])]
                        dss[d][p] = dsv * d8[u:u + 1, pc[p]] - dsa[p] * kk8[u:u + 1, pc[p]]
            for d in (0, 1):
                rows8 = pl.ds(pl.multiple_of(gs[d] * 8, 8), 8)
                _tiles_to_rows(dvt_s, d * NPAIR, outs[d][5], rows8)
                for p in range(NPAIR):
                    ds[d * NPAIR + p] = dss[d][p]
                    for o_ref, a_ in zip(outs[d][:5], acc[d][p]):
                        o_ref[rows8, pc[p]] = a_
            return carry

        lax.fori_loop(0, ng, bgroup, 0)

    chunk = lambda k: pltpu.VMEM((k, NPAIR, HD, 128), F32)
    pairs = lambda w, dt: pltpu.VMEM((2 * NPAIR, HD, w), dt)
    res = pl.pallas_call(
        body, name="rwkv_scan_bwd", grid=(nc,), in_specs=in_specs, out_specs=out_specs, out_shape=out_shape,
        scratch_shapes=[chunk(2 * (TC + 1)), chunk(2 * TC), chunk(2 * TC), chunk(2 * TC), pairs(128, F32),
                        pairs(256, BF16), pairs(256, BF16), pairs(128, F32), pltpu.VMEM((2 * NPAIR, 128, 128), F32)],
        compiler_params=_params(("arbitrary",)),
    )(*operands)
    return [res[0:6], res[6:12]]


MT = 256
MN = 256


def _merge_fwd(ya, yr, yx, wa, wr, wx, proj, gate_b):
    s = ya.shape[0]

    def body(ya_ref, yr_ref, yx_ref, wa_ref, wr_ref, wx_ref, m0, m1, m2, b0, b1, b2, o_ref):
        acc = jnp.zeros((MT, MN), F32)
        for y_ref, w_ref, m_ref, b_ref in ((ya_ref, wa_ref, m0, b0), (yr_ref, wr_ref, m1, b1), (yx_ref, wx_ref, m2, b2)):
            u = _dot(y_ref[...], w_ref[...], ((1,), (0,)))
            acc = acc + jax.nn.sigmoid(m_ref[...] + b_ref[...]) * u
        o_ref[...] = acc.astype(BF16)

    mg = lambda br: _bs((MT, MN), lambda i, j: (i, C_MG // MN + br * (D // MN) + j))
    gb = lambda br: _bs((1, MN), lambda i, j: (0, br * (D // MN) + j))
    return pl.pallas_call(
        body, name="merge_fwd", grid=(s // MT, D // MN),
        in_specs=[_bs((MT, RW), lambda i, j: (i, 0)), _bs((MT, RW), lambda i, j: (i, 0)), _bs((MT, XW), lambda i, j: (i, 0)),
                  _bs((RW, MN), lambda i, j: (0, j)), _bs((RW, MN), lambda i, j: (0, j)), _bs((XW, MN), lambda i, j: (0, j)),
                  mg(0), mg(1), mg(2), gb(0), gb(1), gb(2)],
        out_specs=_bs((MT, MN), lambda i, j: (i, j)),
        out_shape=jax.ShapeDtypeStruct((s, D), BF16),
        compiler_params=_params(("parallel", "arbitrary")),
    )(ya, yr, yx, wa, wr, wx, proj, proj, proj, gate_b, gate_b, gate_b)


def _out_fwd(merged, w_out, x, target):
    s = x.shape[0]
    tm, tn = min(512, s), 512

    def body(m_ref, w_ref, x_ref, t_ref, loss_ref, d_ref, d16_ref):
        @pl.when((pl.program_id(0) == 0) & (pl.program_id(1) == 0))
        def _():
            loss_ref[...] = jnp.zeros_like(loss_ref)

        out = x_ref[...] + jnp.dot(m_ref[...], w_ref[...], preferred_element_type=F32)
        err = out - t_ref[...]
        dout = err * (1.0 / D)
        d_ref[...] = dout
        d16_ref[...] = dout.astype(BF16)
        loss_ref[...] += jnp.sum(err * err)

    tile = _bs((tm, tn), lambda i, j: (i, j))
    return pl.pallas_call(
        body, name="out_fwd", grid=(s // tm, D // tn),
        in_specs=[_bs((tm, D), lambda i, j: (i, 0)), _bs((D, tn), lambda i, j: (0, j)), tile, tile],
        out_specs=[_bs((8, 128), lambda i, j: (0, 0)), tile, tile],
        out_shape=[jax.ShapeDtypeStruct((8, 128), F32), jax.ShapeDtypeStruct((s, D), F32),
                   jax.ShapeDtypeStruct((s, D), BF16)],
        compiler_params=_params(("arbitrary", "arbitrary")),
    )(merged, w_out, x, target)


def _merge_bwd(ya, yr, yx, wa, wr, wx, proj, gate_b, dmerged):
    s = ya.shape[0]

    def body(ya_ref, yr_ref, yx_ref, wa_ref, wr_ref, wx_ref, m0, m1, m2, b0, b1, b2, dm_ref,
             dg0, dg1, dg2, du0, du1, du2, dya_ref, dyr_ref, dyx_ref):
        @pl.when(pl.program_id(1) == 0)
        def _():
            dya_ref[...] = jnp.zeros_like(dya_ref)
            dyr_ref[...] = jnp.zeros_like(dyr_ref)
            dyx_ref[...] = jnp.zeros_like(dyx_ref)

        dm = dm_ref[...]
        for y_ref, w_ref, m_ref, b_ref, dg_ref, du_ref, dy_ref in (
                (ya_ref, wa_ref, m0, b0, dg0, du0, dya_ref), (yr_ref, wr_ref, m1, b1, dg1, du1, dyr_ref),
                (yx_ref, wx_ref, m2, b2, dg2, du2, dyx_ref)):
            w = w_ref[...]
            u = _dot(y_ref[...], w, ((1,), (0,)))
            gt = jax.nn.sigmoid(m_ref[...] + b_ref[...])
            dg_ref[...] = (dm * u * gt * (1.0 - gt)).astype(BF16)
            du = (dm * gt).astype(BF16)
            du_ref[...] = du
            dy_ref[...] += _dot(du, w, ((1,), (1,)))

    mg = lambda br: _bs((MT, MN), lambda i, j: (i, C_MG // MN + br * (D // MN) + j))
    gb = lambda br: _bs((1, MN), lambda i, j: (0, br * (D // MN) + j))
    tile = _bs((MT, MN), lambda i, j: (i, j))
    return pl.pallas_call(
        body, name="merge_bwd", grid=(s // MT, D // MN),
        in_specs=[_bs((MT, RW), lambda i, j: (i, 0)), _bs((MT, RW), lambda i, j: (i, 0)), _bs((MT, XW), lambda i, j: (i, 0)),
                  _bs((RW, MN), lambda i, j: (0, j)), _bs((RW, MN), lambda i, j: (0, j)), _bs((XW, MN), lambda i, j: (0, j)),
                  mg(0), mg(1), mg(2), gb(0), gb(1), gb(2), tile],
        out_specs=[tile] * 6 + [_bs((MT, RW), lambda i, j: (i, 0)), _bs((MT, RW), lambda i, j: (i, 0)),
                                _bs((MT, XW), lambda i, j: (i, 0))],
        out_shape=[jax.ShapeDtypeStruct((s, D), BF16)] * 6 + [jax.ShapeDtypeStruct((s, RW), F32),
                                                               jax.ShapeDtypeStruct((s, RW), F32),
                                                               jax.ShapeDtypeStruct((s, XW), F32)],
        compiler_params=_params(("parallel", "arbitrary")),
    )(ya, yr, yx, wa, wr, wx, proj, proj, proj, gate_b, gate_b, gate_b, dmerged)


def _colsum(a, name):
    m, n = a.shape
    tm, tn = min(512, m), 512

    def body(a_ref, o_ref):
        @pl.when(pl.program_id(1) == 0)
        def _():
            o_ref[...] = jnp.zeros_like(o_ref)

        o_ref[...] += jnp.sum(a_ref[...].astype(F32), axis=0, keepdims=True)

    return pl.pallas_call(
        body, name=name, grid=(n // tn, m // tm),
        in_specs=[_bs((tm, tn), lambda j, i: (i, j))], out_specs=_bs((1, tn), lambda j, i: (0, j)),
        out_shape=jax.ShapeDtypeStruct((1, n), F32),
        compiler_params=_params(("parallel", "arbitrary")),
    )(a)


def _in_bwd(dproj, w_in, x, g, dout):
    s = x.shape[0]
    tm, tk = min(512, s), 896
    nk = NIN // tk

    def body(dp_ref, w_ref, x_ref, g_ref, do_ref, gx_ref, gg_ref, acc):
        i, kk = pl.program_id(0), pl.program_id(1)

        @pl.when((i == 0) & (kk == 0))
        def _():
            gg_ref[...] = jnp.zeros_like(gg_ref)

        @pl.when(kk == 0)
        def _():
            acc[...] = jnp.zeros_like(acc)

        acc[...] += _dot(dp_ref[...], w_ref[...], ((1,), (1,)))

        @pl.when(kk == nk - 1)
        def _():
            xv, dh, gv = x_ref[...], acc[...], g_ref[...]
            r = lax.rsqrt(jnp.mean(xv * xv, axis=-1, keepdims=True) + NORM_EPS)
            xn = xv * r
            gg_ref[...] += jnp.sum(dh * xn, axis=0, keepdims=True)
            dxn = dh * gv
            dx = r * (dxn - xn * jnp.mean(dxn * xn, axis=-1, keepdims=True))
            gx_ref[...] = do_ref[...] + dx

    return pl.pallas_call(
        body, name="in_bwd", grid=(s // tm, nk),
        in_specs=[_bs((tm, tk), lambda i, kk: (i, kk)), _bs((D, tk), lambda i, kk: (0, kk)),
                  _bs((tm, D), lambda i, kk: (i, 0)), _bs((1, D), lambda i, kk: (0, 0)), _bs((tm, D), lambda i, kk: (i, 0))],
        out_specs=[_bs((tm, D), lambda i, kk: (i, 0)), _bs((1, D), lambda i, kk: (0, 0))],
        out_shape=[jax.ShapeDtypeStruct((s, D), F32), jax.ShapeDtypeStruct((1, D), F32)],
        scratch_shapes=[pltpu.VMEM((tm, D), F32)],
        compiler_params=_params(("arbitrary", "arbitrary")),
    )(dproj, w_in, x, g, dout)


def _adamw_math(w, g, m, v):
    m = ADAM_B1 * m + (1.0 - ADAM_B1) * g
    v = ADAM_B2 * v + (1.0 - ADAM_B2) * jnp.square(g)
    m_hat = m / (1.0 - ADAM_B1 ** ADAM_STEP)
    v_hat = v / (1.0 - ADAM_B2 ** ADAM_STEP)
    delta = -ADAM_LR * (m_hat / (jnp.sqrt(v_hat) + ADAM_EPS) + ADAM_WD * w)
    return delta, m, v


def _adamw(parts, w, m, v, name):
    rows, cols = w.shape
    tr = rows
    for cand in (256, 128, 64, 32, 16, 8):
        if rows % cand == 0 and cand * cols * 4 <= (1 << 20):
            tr = cand
            break
    n = len(parts)

    def body(*refs):
        g = refs[0][...].astype(F32)
        for r in refs[1:n]:
            g = g + r[...].astype(F32)
        w_ref, m_ref, v_ref, g_out, d_out, m_out, v_out = refs[n:]
        delta, m_new, v_new = _adamw_math(w_ref[...], g, m_ref[...], v_ref[...])
        g_out[...] = g
        d_out[...] = delta
        m_out[...] = m_new
        v_out[...] = v_new

    spec = _bs((tr, cols), lambda i: (i, 0))
    return pl.pallas_call(
        body, name=name, grid=(rows // tr,),
        in_specs=[spec] * (n + 3), out_specs=[spec] * 4,
        out_shape=[jax.ShapeDtypeStruct((rows, cols), F32)] * 4,
        compiler_params=_params(("parallel",)),
    )(*parts, w, m, v)


def _adamw_halves(mine, theirs, core, w, m, v, name):
    rows, cols = w.shape
    h = rows // 2
    tr = next(t for t in (256, 128, 64, 32, 16, 8) if h % t == 0 and t * cols * 4 <= (1 << 20))
    nt = h // tr

    def body(core_ref, mine_ref, theirs_ref, w_ref, m_ref, v_ref, g_out, d_out, m_out, v_out):
        is_mine = pl.program_id(0) // nt == core_ref[0]
        g = jnp.where(is_mine, mine_ref[...], theirs_ref[...])
        delta, m_new, v_new = _adamw_math(w_ref[...], g, m_ref[...], v_ref[...])
        g_out[...] = g
        d_out[...] = delta
        m_out[...] = m_new
        v_out[...] = v_new

    spec = _bs((tr, cols), lambda i, core_ref: (i, 0))
    return pl.pallas_call(
        body, name=name,
        grid_spec=pltpu.PrefetchScalarGridSpec(
            num_scalar_prefetch=1, grid=(2 * nt,),
            in_specs=[_bs((tr, cols), lambda i, core_ref: (jnp.clip(i - core_ref[0] * nt, 0, nt - 1), 0)),
                      _bs((tr, cols), lambda i, core_ref: (jnp.clip(i - (1 - core_ref[0]) * nt, 0, nt - 1), 0)),
                      spec, spec, spec],
            out_specs=[spec] * 4),
        out_shape=[jax.ShapeDtypeStruct((rows, cols), F32)] * 4,
        compiler_params=_params(("parallel",)),
    )(core, mine, theirs, w, m, v)


def _sum_parts(parts, name):
    rows, cols = parts[0].shape
    tr = rows
    for cand in (256, 128, 64, 32, 16, 8):
        if rows % cand == 0 and cand * cols * 4 <= (1 << 20):
            tr = cand
            break

    def body(*refs):
        acc = refs[0][...].astype(F32)
        for r in refs[1:-1]:
            acc = acc + r[...].astype(F32)
        refs[-1][...] = acc

    spec = _bs((tr, cols), lambda i: (i, 0))
    return pl.pallas_call(
        body, name=name, grid=(rows // tr,), in_specs=[spec] * len(parts), out_specs=spec,
        out_shape=jax.ShapeDtypeStruct((rows, cols), F32), compiler_params=_params(("parallel",)),
    )(*parts)


ANY = pl.BlockSpec(memory_space=pl.ANY)


def _other_chips(x, y):
    return [(1 - x, y), (x, 1 - y), (1 - x, 1 - y)]


def _gather_shards(arrays, name):
    n = len(arrays)

    def body(*refs):
        ins, outs = refs[:n], refs[n:2 * n]
        ici_send, ici_recv, d2d_send, d2d_recv, local_sems, own_recv = refs[2 * n:]
        x, y, c = lax.axis_index("x"), lax.axis_index("y"), lax.axis_index("c")
        me = 2 * x + y
        chips = _other_chips(x, y)

        def half(i, who):
            h = arrays[i].shape[0] // 2
            return pl.ds(who * h, h)

        def ici(i, j, src_chip, to):
            return pltpu.make_async_remote_copy(
                src_ref=ins[i].at[half(i, c)], dst_ref=outs[i].at[src_chip, half(i, c)], send_sem=ici_send.at[3 * i + j],
                recv_sem=ici_recv.at[3 * i + j], device_id=to, device_id_type=MESH)

        def d2d(i, j, src_chip, who):
            piece = outs[i].at[src_chip, half(i, who)]
            return pltpu.make_async_remote_copy(
                src_ref=piece, dst_ref=piece, send_sem=d2d_send.at[3 * i + j], recv_sem=d2d_recv.at[3 * i + j],
                device_id=(x, y, 1 - c), device_id_type=MESH)

        def own(i):
            return pltpu.make_async_remote_copy(
                src_ref=ins[i], dst_ref=outs[i].at[me], send_sem=local_sems.at[i], recv_sem=own_recv.at[i],
                device_id=(x, y, 1 - c), device_id_type=MESH)

        sends = []
        for i in range(n):
            cp = own(i)
            cp.start()
            sends.append(cp)
            for j, (px, py) in enumerate(chips):
                rc = ici(i, j, me, (px, py, c))
                rc.start()
                sends.append(rc)
        for i in range(n):
            for j, (px, py) in enumerate(chips):
                ici(i, j, 2 * px + py, (px, py, c)).wait_recv()
                fw = d2d(i, j, 2 * px + py, c)
                fw.start()
                sends.append(fw)
        for i in range(n):
            for j, (px, py) in enumerate(chips):
                d2d(i, j, 2 * px + py, 1 - c).wait_recv()
            own(i).wait_recv()
        for rc in sends:
            rc.wait_send()

    dma = lambda k: pltpu.SemaphoreType.DMA((k,))
    return pl.pallas_call(
        body, name=name, in_specs=[ANY] * n, out_specs=[ANY] * n,
        out_shape=[jax.ShapeDtypeStruct((4,) + a.shape, a.dtype) for a in arrays],
        scratch_shapes=[dma(3 * n), dma(3 * n), dma(3 * n), dma(3 * n), dma(n), dma(n)],
        compiler_params=pltpu.CompilerParams(has_side_effects=True),
    )(*arrays)


def _scatter_shards(stacks, name):
    n = len(stacks)

    def body(*refs):
        ins, outs = refs[:n], refs[n:2 * n]
        send_sems, recv_sems = refs[2 * n:]
        x, y, c = lax.axis_index("x"), lax.axis_index("y"), lax.axis_index("c")
        chips = _other_chips(x, y)
        sends = []
        for i in range(n):
            for j, (px, py) in enumerate(chips):
                rc = pltpu.make_async_remote_copy(
                    src_ref=ins[i].at[2 * px + py], dst_ref=outs[i].at[j], send_sem=send_sems.at[3 * i + j],
                    recv_sem=recv_sems.at[3 * i + j], device_id=(px, py, c), device_id_type=MESH)
                rc.start()
                sends.append(rc)
        for rc in sends:
            rc.wait_recv()
        for rc in sends:
            rc.wait_send()

    return pl.pallas_call(
        body, name=name, in_specs=[ANY] * n, out_specs=[ANY] * n,
        out_shape=[jax.ShapeDtypeStruct((3,) + a.shape[1:], a.dtype) for a in stacks],
        scratch_shapes=[pltpu.SemaphoreType.DMA((3 * n,)), pltpu.SemaphoreType.DMA((3 * n,))],
        compiler_params=pltpu.CompilerParams(has_side_effects=True),
    )(*stacks)


def _pair_exchange(stacks, name):
    n = len(stacks)

    def body(*refs):
        ins, outs = refs[:n], refs[n:2 * n]
        send_sems, recv_sems = refs[2 * n:]
        x, y, c = lax.axis_index("x"), lax.axis_index("y"), lax.axis_index("c")
        cps = []
        for i in range(n):
            h = stacks[i].shape[1] // 2
            rc = pltpu.make_async_remote_copy(
                src_ref=ins[i].at[:, pl.ds((1 - c) * h, h)], dst_ref=outs[i], send_sem=send_sems.at[i],
                recv_sem=recv_sems.at[i], device_id=(x, y, 1 - c), device_id_type=MESH)
            rc.start()
            cps.append(rc)
        for rc in cps:
            rc.wait_recv()
        for rc in cps:
            rc.wait_send()

    return pl.pallas_call(
        body, name=name, in_specs=[ANY] * n, out_specs=[ANY] * n,
        out_shape=[jax.ShapeDtypeStruct((4, a.shape[1] // 2) + a.shape[2:], a.dtype) for a in stacks],
        scratch_shapes=[pltpu.SemaphoreType.DMA((n,)), pltpu.SemaphoreType.DMA((n,))],
        compiler_params=pltpu.CompilerParams(has_side_effects=True),
    )(*stacks)


def _pair_sum(own, theirs, core, name):
    _, r, cols = own.shape
    h = r // 2
    tr = next(t for t in (256, 128, 64, 32, 16) if h % t == 0 and t * cols * 4 <= (1 << 20))
    nt = h // tr

    def body(core_ref, own_ref, th_ref, o32_ref, o16_ref):
        del core_ref
        acc = own_ref[...] + th_ref[...].astype(F32)
        o32_ref[...] = acc
        o16_ref[...] = acc.astype(BF16)

    out = _bs((1, tr, cols), lambda j, t, core_ref: (j, t, 0))
    return pl.pallas_call(
        body, name=name,
        grid_spec=pltpu.PrefetchScalarGridSpec(
            num_scalar_prefetch=1, grid=(4, nt),
            in_specs=[_bs((1, tr, cols), lambda j, t, core_ref: (j, core_ref[0] * nt + t, 0)), out],
            out_specs=[out, out]),
        out_shape=[jax.ShapeDtypeStruct((4, h, cols), F32), jax.ShapeDtypeStruct((4, h, cols), BF16)],
        compiler_params=_params(("parallel", "parallel")),
    )(core, own, theirs)


def _swap_sibling(arrays, name):
    n = len(arrays)

    def body(*refs):
        ins, outs = refs[:n], refs[n:2 * n]
        send_sems, recv_sems = refs[2 * n:]
        sib = (lax.axis_index("x"), lax.axis_index("y"), 1 - lax.axis_index("c"))
        cps = []
        for i in range(n):
            rc = pltpu.make_async_remote_copy(src_ref=ins[i], dst_ref=outs[i], send_sem=send_sems.at[i],
                                              recv_sem=recv_sems.at[i], device_id=sib, device_id_type=MESH)
            rc.start()
            cps.append(rc)
        for rc in cps:
            rc.wait_recv()
        for rc in cps:
            rc.wait_send()

    return pl.pallas_call(
        body, name=name, in_specs=[ANY] * n, out_specs=[ANY] * n,
        out_shape=[jax.ShapeDtypeStruct(a.shape, a.dtype) for a in arrays],
        scratch_shapes=[pltpu.SemaphoreType.DMA((n,)), pltpu.SemaphoreType.DMA((n,))],
        compiler_params=pltpu.CompilerParams(has_side_effects=True),
    )(*arrays)


def _all_reduce_small(v):
    rows = v.shape[0]

    def body(v_ref, o_ref, buf, send_sems, recv_sems):
        x, y, c = lax.axis_index("x"), lax.axis_index("y"), lax.axis_index("c")
        me = 4 * x + 2 * y + c
        buf[me] = v_ref[...]
        cps = []
        for kbits in range(1, 8):
            bx, by, bc = (kbits >> 2) & 1, (kbits >> 1) & 1, kbits & 1
            px = jnp.where(bx == 1, 1 - x, x)
            py = jnp.where(by == 1, 1 - y, y)
            pc = jnp.where(bc == 1, 1 - c, c)
            rc = pltpu.make_async_remote_copy(src_ref=v_ref, dst_ref=buf.at[me], send_sem=send_sems.at[kbits - 1],
                                              recv_sem=recv_sems.at[kbits - 1], device_id=(px, py, pc),
                                              device_id_type=MESH)
            rc.start()
            cps.append((rc, 4 * px + 2 * py + pc))
        for kbits, (rc, src) in enumerate(cps):
            pltpu.make_async_remote_copy(src_ref=v_ref, dst_ref=buf.at[src], send_sem=send_sems.at[kbits],
                                         recv_sem=recv_sems.at[kbits], device_id=(x, y, c),
                                         device_id_type=MESH).wait_recv()
        for rc, _ in cps:
            rc.wait_send()
        acc = buf[0]
        for d in range(1, 8):
            acc = acc + buf[d]
        o_ref[...] = acc

    return pl.pallas_call(
        body, name="all_reduce_small",
        in_specs=[pl.BlockSpec(memory_space=pltpu.VMEM)], out_specs=pl.BlockSpec(memory_space=pltpu.VMEM),
        out_shape=jax.ShapeDtypeStruct((rows, 128), F32),
        scratch_shapes=[pltpu.VMEM((8, rows, 128), F32), pltpu.SemaphoreType.DMA((7,)), pltpu.SemaphoreType.DMA((7,))],
        compiler_params=pltpu.CompilerParams(has_side_effects=True, vmem_limit_bytes=VMEM_LIMIT),
    )(v)


def _rope_tables(s):
    half = HD // 2
    inv = 10000.0 ** (-jnp.arange(half, dtype=F32) / half)
    ang = jnp.arange(s, dtype=F32)[:, None] * inv[None, :]
    cos, sin = jnp.cos(ang), jnp.sin(ang)
    return jnp.concatenate([cos, cos], axis=1), jnp.concatenate([sin, sin], axis=1)


def _local_step(x, mem, target, norm_g, mem_norm_g, w_in, gate_b, gq, gk, sink, wa, mu, k_k, k_a, r_k, w0, w2, a0, a2,
                ln_w, ln_b, wr, w_kv, gxq, gxk, wx, w_out):
    s = x.shape[0]
    cos, sin = _rope_tables(s)
    r_k = r_k.reshape(1, RW)

    proj, h = _proj_fwd(x, norm_g, w_in)
    ya = _attn_fwd(proj, cos, sin, gq, gk, sink)
    mkv, mn = _mem_kv(mem, mem_norm_g, w_kv)
    yx = _xattn_fwd(proj, mkv, gxq, gxk)
    ps = _shift_fwd(proj, mu)
    kk, dec0, kd0, b0, dec1, kd1, b1 = _pre_fwd(ps, k_k, k_a, w0, w2, a0, a2)
    (y0, ck0), (y1, ck1) = _scan2_fwd([(dec0, kd0, b0), (dec1, kd1, b1)], ps, kk)
    yr = _post_fwd(y0, y1, ps, kd0, kd1, proj, r_k, ln_w, ln_b)
    merged = _merge_fwd(ya, yr, yx, wa, wr, wx, proj, gate_b)
    loss_tile, dout, dout16 = _out_fwd(merged, w_out, x, target)
    loss_sum = loss_tile[0, 0]

    g = {}
    t16 = lambda a: a.astype(BF16).T
    sk = min(1024, s)
    dmerged = _matmul(dout16, w_out, mode="nt", m=s, n=D, k=D, tm=sk, tn=1024, tk=1024, name="dmerged")
    g["w_out"] = _matmul(merged.T, dout16, mode="nn", m=D, n=D, k=s, tm=1024, tn=1024, tk=sk, name="grad_w_out")
    dg0, dg1, dg2, du0, du1, du2, dya, dyr, dyx = _merge_bwd(ya, yr, yx, wa, wr, wx, proj, gate_b, dmerged)
    g["attn_w_o"] = _matmul(t16(ya), du0, mode="nn", m=RW, n=D, k=s, tm=RW, tn=1024, tk=s, name="grad_attn_w_o")
    g["rwkv_w_o"] = _matmul(t16(yr), du1, mode="nn", m=RW, n=D, k=s, tm=RW, tn=1024, tk=s, name="grad_rwkv_w_o")
    g["x_w_o"] = _matmul(t16(yx), du2, mode="nn", m=XW, n=D, k=s, tm=XW, tn=1024, tk=s, name="grad_x_w_o")
    dmg = jnp.concatenate([dg0, dg1, dg2], axis=1)
    g["gate_b"] = _colsum(dmg, "grad_gate_b")

    daq, dak, dav, dag, g["attn_q_norm_g"], g["attn_k_norm_g"], g["attn_sink"] = _attn_bwd(proj, cos, sin, gq, gk, sink, dya)

    dxq, dxg, dmkv, g["x_q_norm_g"], g["x_k_norm_g"] = _xattn_bwd(proj, mkv, gxq, gxk, dyx)
    g["x_w_kv"] = _matmul(mn, dmkv, mode="tn", m=D, n=2 * XW, k=NMEM, tm=512, tn=512, tk=NMEM, name="grad_x_w_kv")
    dmn = _matmul(dmkv, w_kv, mode="nt", m=NMEM, n=D, k=2 * XW, tm=NMEM, tn=512, tk=2 * XW, name="dmn")
    g["mem_norm_g"] = _mem_bwd(mem, mem_norm_g, dmn)

    dys, dr_p, dv_p, dkd0_p, dkd1_p, drg, g["rwkv_r_k"], g["rwkv_ln_w"], g["rwkv_ln_b"] = _post_bwd(
        y0, y1, ps, kd0, kd1, proj, r_k, ln_w, ln_b, dyr)
    (dr0, dd0, db0, dk0, dkk0, dv0), (dr1, dd1, db1, dk1, dkk1, dv1) = _scan2_bwd(
        [(dec0, kd0, b0, ck0), (dec1, kd1, b1, ck1)], ps, kk, dys)
    dr = dr_p + dr0 + dr1
    dv = dv_p + dv0 + dv1
    cts = (dkk0 + dkk1, dd0, dk0 + dkd0_p, db0, dd1, dk1 + dkd1_p, db1)
    dps, g["rwkv_k_k"], g["rwkv_k_a"], g["rwkv_w0"], g["rwkv_w2"], g["rwkv_a0"], g["rwkv_a2"] = _pre_bwd(
        ps, k_k, k_a, w0, w2, a0, a2, dr, dv, cts)
    drs, g["rwkv_mu"] = _shift_bwd(proj, mu, dps)

    dproj = jnp.concatenate([daq.astype(BF16), dak.astype(BF16), dav.astype(BF16), dag.astype(BF16), drs.astype(BF16),
                             drg.astype(BF16), dxq.astype(BF16), dxg.astype(BF16), dmg], axis=1)
    g["w_in"] = _matmul(h.T, dproj, mode="nn", m=D, n=NIN, k=s, tm=512, tn=896, tk=s, name="grad_w_in")
    grad_x, g["norm_g"] = _in_bwd(dproj, w_in, x, norm_g, dout)
    g["rwkv_r_k"] = g["rwkv_r_k"].reshape(AH, HD)
    return loss_sum, grad_x, g


WEIGHTS = ['norm_g', 'mem_norm_g', 'w_in', 'gate_b', 'attn_q_norm_g', 'attn_k_norm_g', 'attn_sink', 'attn_w_o',
           'rwkv_mu', 'rwkv_k_k', 'rwkv_k_a', 'rwkv_r_k', 'rwkv_w0', 'rwkv_w2', 'rwkv_a0', 'rwkv_a2', 'rwkv_ln_w',
           'rwkv_ln_b', 'rwkv_w_o', 'x_w_kv', 'x_q_norm_g', 'x_k_norm_g', 'x_w_o', 'w_out']
BIG = ['w_in', 'attn_w_o', 'rwkv_w_o', 'x_w_kv', 'x_w_o', 'w_out']
COL_SHARDED = ['w_in', 'attn_w_o', 'rwkv_w_o', 'x_w_o']
LORA = ['rwkv_w0', 'rwkv_w2', 'rwkv_a0', 'rwkv_a2']
SMALL = [n for n in WEIGHTS if n not in BIG]


def _unshard_cols(stack):
    return jnp.concatenate([stack[i] for i in range(4)], axis=-1)


def _shard_cols(full):
    w = full.shape[-1] // 4
    return [full[..., i * w:(i + 1) * w] for i in range(4)]


def kernel(x, mem, norm_g, mem_norm_g, w_in, gate_b, attn_q_norm_g, attn_k_norm_g, attn_sink, attn_w_o, rwkv_mu, rwkv_k_k, rwkv_k_a, rwkv_r_k, rwkv_w0, rwkv_w2, rwkv_a0, rwkv_a2, rwkv_ln_w, rwkv_ln_b, rwkv_w_o, x_w_kv, x_q_norm_g, x_k_norm_g, x_w_o, w_out, loss_target, m_norm_g, m_mem_norm_g, m_w_in, m_gate_b, m_attn_q_norm_g, m_attn_k_norm_g, m_attn_sink, m_attn_w_o, m_rwkv_mu, m_rwkv_k_k, m_rwkv_k_a, m_rwkv_r_k, m_rwkv_w0, m_rwkv_w2, m_rwkv_a0, m_rwkv_a2, m_rwkv_ln_w, m_rwkv_ln_b, m_rwkv_w_o, m_x_w_kv, m_x_q_norm_g, m_x_k_norm_g, m_x_w_o, m_w_out, v_norm_g, v_mem_norm_g, v_w_in, v_gate_b, v_attn_q_norm_g, v_attn_k_norm_g, v_attn_sink, v_attn_w_o, v_rwkv_mu, v_rwkv_k_k, v_rwkv_k_a, v_rwkv_r_k, v_rwkv_w0, v_rwkv_w2, v_rwkv_a0, v_rwkv_a2, v_rwkv_ln_w, v_rwkv_ln_b, v_rwkv_w_o, v_x_w_kv, v_x_q_norm_g, v_x_k_norm_g, v_x_w_o, v_w_out):
    args = dict(locals())
    canon = lambda a: a[0] if a.ndim > 2 else a
    w = {n: canon(args[n]) for n in WEIGHTS}
    m = {n: canon(args["m_" + n]) for n in WEIGHTS}
    v = {n: canon(args["v_" + n]) for n in WEIGHTS}
    shard = 2 * lax.axis_index("x") + lax.axis_index("y")

    local = [w[n].astype(BF16) for n in BIG] + [w[n].reshape(2, -1, w[n].shape[-1]) for n in LORA]
    stacks = dict(zip(BIG + LORA, _gather_shards(local, "gather_weights")))
    full = {}
    for n in COL_SHARDED:
        full[n] = _unshard_cols(stacks[n])
    for n in LORA:
        full[n] = _unshard_cols(stacks[n]).reshape(w[n].shape[:-1] + (RW,))
    full["x_w_kv"] = stacks["x_w_kv"].reshape(D, 2 * XW)
    full["w_out"] = stacks["w_out"].reshape(D, D)

    loss_sum, grad_x, g = _local_step(
        x[0], mem[0], loss_target[0], w["norm_g"], w["mem_norm_g"], full["w_in"], w["gate_b"], w["attn_q_norm_g"],
        w["attn_k_norm_g"], w["attn_sink"], full["attn_w_o"], w["rwkv_mu"], w["rwkv_k_k"], w["rwkv_k_a"], w["rwkv_r_k"],
        full["rwkv_w0"], full["rwkv_w2"], full["rwkv_a0"], full["rwkv_a2"], w["rwkv_ln_w"], w["rwkv_ln_b"],
        full["rwkv_w_o"], full["x_w_kv"], w["x_q_norm_g"], w["x_k_norm_g"], full["x_w_o"], full["w_out"])

    loss = lax.psum(0.5 * loss_sum / D, ("x", "y", "c"))

    def as_stack(n, dtype):
        if n in COL_SHARDED:
            return jnp.stack([p.astype(dtype) for p in _shard_cols(g[n])])
        return g[n].reshape((4, g[n].shape[0] // 4) + g[n].shape[1:]).astype(dtype)

    core = lax.axis_index("c").astype(jnp.int32).reshape(1)
    sibling = _pair_exchange([as_stack(n, BF16) for n in BIG], "pair_exchange")
    pair32, pair16 = [], []
    for n, th in zip(BIG, sibling):
        a32, a16 = _pair_sum(as_stack(n, F32), th, core, "pair_sum_" + n)
        pair32.append(a32)
        pair16.append(a16)
    recv = _scatter_shards(pair16, "scatter_grads")
    halves = []
    for n, p32, r in zip(BIG, pair32, recv):
        own = lax.dynamic_index_in_dim(p32, shard, 0, keepdims=False)
        halves.append(_sum_parts([own, r[0], r[1], r[2]], "sum_" + n))
    other_halves = _swap_sibling(halves, "swap_halves")

    out_g, out_d, out_m, out_v = {}, {}, {}, {}
    for n, mine, theirs in zip(BIG, halves, other_halves):
        out_g[n], out_d[n], out_m[n], out_v[n] = _adamw_halves(mine, theirs, core, w[n], m[n], v[n], "adamw_" + n)

    flat = jnp.concatenate([g[n].reshape(-1) for n in SMALL])
    total = flat.shape[0]
    padded = -(-total // 1024) * 1024
    flat = jnp.pad(flat, (0, padded - total)).reshape(padded // 128, 128)
    red = _all_reduce_small(flat).reshape(-1)
    off = 0
    gs = {}
    for n in SMALL:
        size = g[n].size
        t = red[off:off + size].reshape(g[n].shape)
        off += size
        if n in LORA:
            wd = t.shape[-1] // 4
            t = lax.dynamic_slice_in_dim(t, shard * wd, wd, axis=t.ndim - 1)
        gs[n] = t

    def pack(d):
        f = jnp.concatenate([d[n].reshape(-1) for n in SMALL])
        return jnp.pad(f, (0, -(-f.shape[0] // 1024) * 1024 - f.shape[0])).reshape(-1, 128)

    pg, pd, pm, pv = _adamw([pack(gs)], pack(w), pack(m), pack(v), "adamw_small")
    off = 0
    for n in SMALL:
        size = w[n].size
        for dst, src in ((out_g, pg), (out_d, pd), (out_m, pm), (out_v, pv)):
            dst[n] = src.reshape(-1)[off:off + size].reshape(w[n].shape)
        off += size

    lead = lambda d: [d[n][None] if args[n].ndim > 2 else d[n] for n in WEIGHTS]
    return (loss, grad_x[None], *lead(out_g), *lead(out_d), *lead(out_m), *lead(out_v))
```

```python
import functools

import jax
import jax.numpy as jnp
from jax import lax
from jax.experimental import pallas as pl
from jax.experimental.pallas import tpu as pltpu

F32 = jnp.float32
BF16 = jnp.bfloat16
HI = lax.Precision.HIGHEST
MESH = pl.DeviceIdType.MESH

D = 2048
NMEM = 256
NORM_EPS = 1e-6
NEG_INF = -1e30
GN_EPS = 64e-5
HD = 64
AH = 12
AKV = 4
RW = 768
XH = 4
XD = 128
XW = 512
NIN = 12544
RSW = 2560
C_AQ, C_AK, C_AV, C_AG, C_RS, C_RG, C_XQ, C_XG, C_MG = 0, 768, 1024, 1280, 2048, 4608, 5376, 5888, 6400
WIN = 384
QB = 128
TC = 16
NPAIR = 6

ADAM_LR, ADAM_B1, ADAM_B2, ADAM_EPS, ADAM_WD, ADAM_STEP = 0.001, 0.9, 0.999, 1e-08, 0.01, 10

VMEM_LIMIT = 56 * 1024 * 1024


def _bs(shape, imap):
    return pl.BlockSpec(shape, imap)


def _params(sem=None, vmem=VMEM_LIMIT):
    return pltpu.CompilerParams(dimension_semantics=sem, vmem_limit_bytes=vmem)


def _dot(a, b, dims):
    return lax.dot_general(a.astype(BF16), b.astype(BF16), (dims, ((), ())), preferred_element_type=F32)


@jax.custom_vjp
def _mm_nn(a, b):
    return _dot(a, b, ((1,), (0,)))


def _mm_nn_fwd(a, b):
    return _mm_nn(a, b), (a, b)


def _mm_nn_bwd(res, ct):
    a, b = res
    return _dot(ct, b, ((1,), (1,))), _dot(a, ct, ((0,), (0,)))


_mm_nn.defvjp(_mm_nn_fwd, _mm_nn_bwd)


@jax.custom_vjp
def _mm_nt(a, b):
    return _dot(a, b, ((1,), (1,)))


def _mm_nt_fwd(a, b):
    return _mm_nt(a, b), (a, b)


def _mm_nt_bwd(res, ct):
    a, b = res
    return _dot(ct, b, ((1,), (0,))), _dot(ct, a, ((0,), (0,)))


_mm_nt.defvjp(_mm_nt_fwd, _mm_nt_bwd)


def _seg_matrix(n, seg):
    r = lax.broadcasted_iota(jnp.int32, (n, n), 0) // seg
    c = lax.broadcasted_iota(jnp.int32, (n, n), 1) // seg
    return (r == c).astype(F32)


def _rot_matrix():
    r = lax.broadcasted_iota(jnp.int32, (HD, HD), 0)
    c = lax.broadcasted_iota(jnp.int32, (HD, HD), 1)
    return jnp.where(c == r + HD // 2, 1.0, 0.0).astype(F32) - jnp.where(c == r - HD // 2, 1.0, 0.0).astype(F32)


def _hdot(a, m):
    return jnp.dot(a, m, precision=HI, preferred_element_type=F32)


def _rms(t, g):
    return t * lax.rsqrt(jnp.mean(t * t, axis=-1, keepdims=True) + NORM_EPS) * g


def _silu(t):
    return t * jax.nn.sigmoid(t)


def _softplus(z):
    return jnp.maximum(z, 0.0) + jnp.log(1.0 + jnp.exp(-jnp.abs(z)))


def _matmul(a, b, *, mode, m, n, k, tm, tn, tk, name, a_off=(0, 0), b_off=(0, 0), out_dtype=F32):
    nk = k // tk
    if mode == "tn":
        a_spec = _bs((tk, tm), lambda i, j, kk: (kk + a_off[0], i + a_off[1]))
        dims = ((0,), (0,))
    else:
        a_spec = _bs((tm, tk), lambda i, j, kk: (i + a_off[0], kk + a_off[1]))
        dims = ((1,), (1,)) if mode == "nt" else ((1,), (0,))
    if mode == "nt":
        b_spec = _bs((tn, tk), lambda i, j, kk: (j + b_off[0], kk + b_off[1]))
    else:
        b_spec = _bs((tk, tn), lambda i, j, kk: (kk + b_off[0], j + b_off[1]))

    def body(a_ref, b_ref, o_ref, acc):
        kk = pl.program_id(2)

        @pl.when(kk == 0)
        def _():
            acc[...] = jnp.zeros_like(acc)

        acc[...] += _dot(a_ref[...], b_ref[...], dims)

        @pl.when(kk == nk - 1)
        def _():
            o_ref[...] = acc[...].astype(out_dtype)

    return pl.pallas_call(
        body, name=name, grid=(m // tm, n // tn, nk),
        in_specs=[a_spec, b_spec], out_specs=_bs((tm, tn), lambda i, j, kk: (i, j)),
        out_shape=jax.ShapeDtypeStruct((m, n), out_dtype),
        scratch_shapes=[pltpu.VMEM((tm, tn), F32)],
        compiler_params=_params(("parallel", "parallel", "arbitrary")),
    )(a, b)


def _proj_fwd(x, g, w):
    s = x.shape[0]
    tm, tn = min(512, s), 896

    def body(x_ref, g_ref, w_ref, o_ref, h_ref, hs):
        @pl.when(pl.program_id(1) == 0)
        def _():
            h = _rms(x_ref[...], g_ref[...]).astype(BF16)
            hs[...] = h
            h_ref[...] = h

        o_ref[...] = jnp.dot(hs[...], w_ref[...], preferred_element_type=F32)

    return pl.pallas_call(
        body, name="proj_fwd", grid=(s // tm, NIN // tn),
        in_specs=[_bs((tm, D), lambda i, j: (i, 0)), _bs((1, D), lambda i, j: (0, 0)), _bs((D, tn), lambda i, j: (0, j))],
        out_specs=[_bs((tm, tn), lambda i, j: (i, j)), _bs((tm, D), lambda i, j: (i, 0))],
        out_shape=[jax.ShapeDtypeStruct((s, NIN), F32), jax.ShapeDtypeStruct((s, D), BF16)],
        scratch_shapes=[pltpu.VMEM((tm, D), BF16)],
        compiler_params=_params(("parallel", "arbitrary")),
    )(x, g, w)


def _rope(t, cos, sin, rot):
    return t * cos + _hdot(t, rot) * sin


def _attn_tile(qs, ks, vs, gs, sinks, gq, gk, cq, sq, ck, sk, mask, rot):
    outs = []
    for hk in range(AKV):
        kh = _rope(_rms(ks[hk], gk), ck, sk, rot)
        for g in range(AH // AKV):
            h = hk * (AH // AKV) + g
            qh = _rope(_rms(qs[h], gq), cq, sq, rot)
            sc = _mm_nt(qh, kh) * (HD ** -0.5)
            sc = jnp.where(mask, sc, NEG_INF)
            mx = lax.stop_gradient(jnp.maximum(jnp.max(sc, axis=-1, keepdims=True), sinks[h]))
            p = jnp.exp(sc - mx)
            den = jnp.sum(p, axis=-1, keepdims=True) + jnp.exp(sinks[h] - mx)
            o = _mm_nn(p / den, vs[hk])
            outs.append(o * _silu(gs[h]))
    return outs


def _attn_load(n, s, aq_ref, ak_ref, av_ref, ag_refs, cos_ref, sin_ref, sink_ref):
    start = pl.multiple_of(jnp.clip((n - 1) * QB, 0, s - WIN), QB)
    q0 = pl.multiple_of(n * QB, QB)
    qs = [aq_ref[:, h * HD:(h + 1) * HD] for h in range(AH)]
    ks = [ak_ref[pl.ds(start, WIN), h * HD:(h + 1) * HD] for h in range(AKV)]
    vs = [av_ref[pl.ds(start, WIN), h * HD:(h + 1) * HD] for h in range(AKV)]
    gs = [ag_refs[h // 4][:, (h % 4) * HD:(h % 4 + 1) * HD] for h in range(AH)]
    sinks = [sink_ref[0:1, h:h + 1] for h in range(AH)]
    cq, sq = cos_ref[pl.ds(q0, QB), :], sin_ref[pl.ds(q0, QB), :]
    ck, sk = cos_ref[pl.ds(start, WIN), :], sin_ref[pl.ds(start, WIN), :]
    qpos = q0 + lax.broadcasted_iota(jnp.int32, (QB, WIN), 0)
    kpos = start + lax.broadcasted_iota(jnp.int32, (QB, WIN), 1)
    mask = jnp.abs(kpos - qpos) <= QB
    return start, qs, ks, vs, gs, sinks, cq, sq, ck, sk, mask


def _attn_specs(s):
    return [
        _bs((QB, 768), lambda n: (n, 0)),
        _bs((s, 256), lambda n: (0, C_AK // 256)),
        _bs((s, 256), lambda n: (0, C_AV // 256)),
        _bs((QB, 256), lambda n: (n, C_AG // 256)),
        _bs((QB, 256), lambda n: (n, C_AG // 256 + 1)),
        _bs((QB, 256), lambda n: (n, C_AG // 256 + 2)),
        _bs((s, HD), lambda n: (0, 0)),
        _bs((s, HD), lambda n: (0, 0)),
        _bs((1, HD), lambda n: (0, 0)),
        _bs((1, HD), lambda n: (0, 0)),
        _bs((1, AH), lambda n: (0, 0)),
    ]


def _attn_fwd(proj, cos, sin, gq, gk, sink):
    s = proj.shape[0]

    def body(aq_ref, ak_ref, av_ref, ag0, ag1, ag2, cos_ref, sin_ref, gq_ref, gk_ref, sink_ref, o_ref):
        n = pl.program_id(0)
        _, qs, ks, vs, gs, sinks, cq, sq, ck, sk, mask = _attn_load(
            n, s, aq_ref, ak_ref, av_ref, (ag0, ag1, ag2), cos_ref, sin_ref, sink_ref)
        outs = _attn_tile(qs, ks, vs, gs, sinks, gq_ref[...], gk_ref[...], cq, sq, ck, sk, mask, _rot_matrix())
        for h in range(AH):
            o_ref[:, h * HD:(h + 1) * HD] = outs[h]

    return pl.pallas_call(
        body, name="attn_fwd", grid=(s // QB,),
        in_specs=_attn_specs(s), out_specs=_bs((QB, 768), lambda n: (n, 0)),
        out_shape=jax.ShapeDtypeStruct((s, 768), F32),
        compiler_params=_params(("arbitrary",)),
    )(proj, proj, proj, proj, proj, proj, cos, sin, gq, gk, sink)


def _attn_bwd(proj, cos, sin, gq, gk, sink, dy):
    s = proj.shape[0]

    def body(aq_ref, ak_ref, av_ref, ag0, ag1, ag2, cos_ref, sin_ref, gq_ref, gk_ref, sink_ref, dy_ref,
             daq_ref, dak_ref, dav_ref, dag_ref, dgq_ref, dgk_ref, dsink_ref):
        n = pl.program_id(0)

        @pl.when(n == 0)
        def _():
            dak_ref[...] = jnp.zeros_like(dak_ref)
            dav_ref[...] = jnp.zeros_like(dav_ref)
            dgq_ref[...] = jnp.zeros_like(dgq_ref)
            dgk_ref[...] = jnp.zeros_like(dgk_ref)
            dsink_ref[...] = jnp.zeros_like(dsink_ref)

        start, qs, ks, vs, gs, sinks, cq, sq, ck, sk, mask = _attn_load(
            n, s, aq_ref, ak_ref, av_ref, (ag0, ag1, ag2), cos_ref, sin_ref, sink_ref)
        rot = _rot_matrix()

        def f(qs, ks, vs, gs, sinks, gq, gk):
            return _attn_tile(qs, ks, vs, gs, sinks, gq, gk, cq, sq, ck, sk, mask, rot)

        _, vjp = jax.vjp(f, qs, ks, vs, gs, sinks, gq_ref[...], gk_ref[...])
        dys = [dy_ref[:, h * HD:(h + 1) * HD] for h in range(AH)]
        dqs, dks, dvs, dgs, dsinks, dgq, dgk = vjp(dys)
        for h in range(AH):
            daq_ref[:, h * HD:(h + 1) * HD] = dqs[h]
            dag_ref[:, h * HD:(h + 1) * HD] = dgs[h]
            dsink_ref[0:1, h:h + 1] += dsinks[h]
        for h in range(AKV):
            dak_ref[pl.ds(start, WIN), h * HD:(h + 1) * HD] += dks[h]
            dav_ref[pl.ds(start, WIN), h * HD:(h + 1) * HD] += dvs[h]
        dgq_ref[...] += dgq
        dgk_ref[...] += dgk

    whole = lambda shape: _bs(shape, lambda n: (0, 0))
    return pl.pallas_call(
        body, name="attn_bwd", grid=(s // QB,),
        in_specs=_attn_specs(s) + [_bs((QB, 768), lambda n: (n, 0))],
        out_specs=[_bs((QB, 768), lambda n: (n, 0)), whole((s, 256)), whole((s, 256)), _bs((QB, 768), lambda n: (n, 0)),
                   whole((1, HD)), whole((1, HD)), whole((1, AH))],
        out_shape=[jax.ShapeDtypeStruct((s, 768), F32), jax.ShapeDtypeStruct((s, 256), F32),
                   jax.ShapeDtypeStruct((s, 256), F32), jax.ShapeDtypeStruct((s, 768), F32),
                   jax.ShapeDtypeStruct((1, HD), F32), jax.ShapeDtypeStruct((1, HD), F32),
                   jax.ShapeDtypeStruct((1, AH), F32)],
        compiler_params=_params(("arbitrary",)),
    )(proj, proj, proj, proj, proj, proj, cos, sin, gq, gk, sink, dy)


def _mem_kv(mem, g, w):
    def body(m_ref, g_ref, w_ref, o_ref, mn_ref):
        mn = _rms(m_ref[...], g_ref[...]).astype(BF16)
        mn_ref[...] = mn
        o_ref[...] = jnp.dot(mn, w_ref[...], preferred_element_type=F32)

    return pl.pallas_call(
        body, name="mem_kv",
        out_shape=[jax.ShapeDtypeStruct((NMEM, 2 * XW), F32), jax.ShapeDtypeStruct((NMEM, D), BF16)],
        compiler_params=_params(),
    )(mem, g, w)


def _xattn_tile(qs, gs, kms, vms, gxq, gxk):
    outs = []
    for h in range(XH):
        q = _rms(qs[h], gxq)
        km = _rms(kms[h], gxk)
        sc = _mm_nt(q, km) * (XD ** -0.5)
        mx = lax.stop_gradient(jnp.max(sc, axis=-1, keepdims=True))
        p = jnp.exp(sc - mx)
        p = p / jnp.sum(p, axis=-1, keepdims=True)
        outs.append(_mm_nn(p, vms[h]) * _silu(gs[h]))
    return outs


XT = 256


def _xattn_specs():
    return [
        _bs((XT, 256), lambda i: (i, C_XQ // 256)), _bs((XT, 256), lambda i: (i, C_XQ // 256 + 1)),
        _bs((XT, 256), lambda i: (i, C_XG // 256)), _bs((XT, 256), lambda i: (i, C_XG // 256 + 1)),
        _bs((NMEM, 2 * XW), lambda i: (0, 0)),
        _bs((1, XD), lambda i: (0, 0)), _bs((1, XD), lambda i: (0, 0)),
    ]


def _xattn_load(q0, q1, g0, g1, mkv_ref):
    qs = [(q0, q1)[h // 2][:, (h % 2) * XD:(h % 2 + 1) * XD] for h in range(XH)]
    gs = [(g0, g1)[h // 2][:, (h % 2) * XD:(h % 2 + 1) * XD] for h in range(XH)]
    kms = [mkv_ref[:, h * XD:(h + 1) * XD] for h in range(XH)]
    vms = [mkv_ref[:, XW + h * XD:XW + (h + 1) * XD] for h in range(XH)]
    return qs, gs, kms, vms


def _xattn_fwd(proj, mkv, gxq, gxk):
    s = proj.shape[0]

    def body(q0, q1, g0, g1, mkv_ref, gxq_ref, gxk_ref, o_ref):
        qs, gs, kms, vms = _xattn_load(q0, q1, g0, g1, mkv_ref)
        outs = _xattn_tile(qs, gs, kms, vms, gxq_ref[...], gxk_ref[...])
        for h in range(XH):
            o_ref[:, h * XD:(h + 1) * XD] = outs[h]

    return pl.pallas_call(
        body, name="xattn_fwd", grid=(s // XT,),
        in_specs=_xattn_specs(), out_specs=_bs((XT, XW), lambda i: (i, 0)),
        out_shape=jax.ShapeDtypeStruct((s, XW), F32),
        compiler_params=_params(("arbitrary",)),
    )(proj, proj, proj, proj, mkv, gxq, gxk)


def _xattn_bwd(proj, mkv, gxq, gxk, dy):
    s = proj.shape[0]

    def body(q0, q1, g0, g1, mkv_ref, gxq_ref, gxk_ref, dy_ref, dq_ref, dg_ref, dmkv_ref, dgxq_ref, dgxk_ref):
        @pl.when(pl.program_id(0) == 0)
        def _():
            dmkv_ref[...] = jnp.zeros_like(dmkv_ref)
            dgxq_ref[...] = jnp.zeros_like(dgxq_ref)
            dgxk_ref[...] = jnp.zeros_like(dgxk_ref)

        qs, gs, kms, vms = _xattn_load(q0, q1, g0, g1, mkv_ref)
        _, vjp = jax.vjp(_xattn_tile, qs, gs, kms, vms, gxq_ref[...], gxk_ref[...])
        dqs, dgs, dkms, dvms, dgxq, dgxk = vjp([dy_ref[:, h * XD:(h + 1) * XD] for h in range(XH)])
        for h in range(XH):
            dq_ref[:, h * XD:(h + 1) * XD] = dqs[h]
            dg_ref[:, h * XD:(h + 1) * XD] = dgs[h]
            dmkv_ref[:, h * XD:(h + 1) * XD] += dkms[h]
            dmkv_ref[:, XW + h * XD:XW + (h + 1) * XD] += dvms[h]
        dgxq_ref[...] += dgxq
        dgxk_ref[...] += dgxk

    whole = lambda shape: _bs(shape, lambda i: (0, 0))
    return pl.pallas_call(
        body, name="xattn_bwd", grid=(s // XT,),
        in_specs=_xattn_specs() + [_bs((XT, XW), lambda i: (i, 0))],
        out_specs=[_bs((XT, XW), lambda i: (i, 0)), _bs((XT, XW), lambda i: (i, 0)), whole((NMEM, 2 * XW)),
                   whole((1, XD)), whole((1, XD))],
        out_shape=[jax.ShapeDtypeStruct((s, XW), F32), jax.ShapeDtypeStruct((s, XW), F32),
                   jax.ShapeDtypeStruct((NMEM, 2 * XW), F32), jax.ShapeDtypeStruct((1, XD), F32),
                   jax.ShapeDtypeStruct((1, XD), F32)],
        compiler_params=_params(("arbitrary",)),
    )(proj, proj, proj, proj, mkv, gxq, gxk, dy)


def _mem_bwd(mem, g, dmn):
    def body(m_ref, dmn_ref, o_ref):
        m = m_ref[...]
        r = lax.rsqrt(jnp.mean(m * m, axis=-1, keepdims=True) + NORM_EPS)
        o_ref[...] = jnp.sum(dmn_ref[...] * m * r, axis=0, keepdims=True)

    del g
    return pl.pallas_call(body, name="mem_norm_bwd", out_shape=jax.ShapeDtypeStruct((1, D), F32),
                          compiler_params=_params())(mem, dmn)


SHIFT_W = 512


def _shift_rows(p, s):
    row = lax.broadcasted_iota(jnp.int32, p.shape, 0)
    prev = jnp.where(row == 0, 0.0, pltpu.roll(p, 1, 0))
    nxt = jnp.where(row == s - 1, 0.0, pltpu.roll(p, s - 1, 0))
    return prev, nxt


def _shift_fwd(proj, mu):
    s = proj.shape[0]

    def body(p_ref, mu_ref, o_ref):
        p = p_ref[...]
        prev, nxt = _shift_rows(p, s)
        o_ref[...] = p + mu_ref[...] * (0.5 * (prev + nxt) - p)

    return pl.pallas_call(
        body, name="shift_fwd", grid=(RSW // SHIFT_W,),
        in_specs=[_bs((s, SHIFT_W), lambda j: (0, C_RS // SHIFT_W + j)), _bs((1, SHIFT_W), lambda j: (0, j))],
        out_specs=_bs((s, SHIFT_W), lambda j: (0, j)),
        out_shape=jax.ShapeDtypeStruct((s, RSW), F32),
        compiler_params=_params(("parallel",)),
    )(proj, mu)


def _shift_bwd(proj, mu, dps):
    s = proj.shape[0]

    def body(p_ref, mu_ref, g_ref, o_ref, dmu_ref):
        p, g, mu_v = p_ref[...], g_ref[...], mu_ref[...]
        prev, nxt = _shift_rows(p, s)
        dmu_ref[...] = jnp.sum(g * (0.5 * (prev + nxt) - p), axis=0, keepdims=True)
        mg = mu_v * g
        down, up = _shift_rows(mg, s)
        o_ref[...] = g * (1.0 - mu_v) + 0.5 * (down + up)

    return pl.pallas_call(
        body, name="shift_bwd", grid=(RSW // SHIFT_W,),
        in_specs=[_bs((s, SHIFT_W), lambda j: (0, C_RS // SHIFT_W + j)), _bs((1, SHIFT_W), lambda j: (0, j)),
                  _bs((s, SHIFT_W), lambda j: (0, j))],
        out_specs=[_bs((s, SHIFT_W), lambda j: (0, j)), _bs((1, SHIFT_W), lambda j: (0, j))],
        out_shape=[jax.ShapeDtypeStruct((s, RSW), F32), jax.ShapeDtypeStruct((1, RSW), F32)],
        compiler_params=_params(("parallel",)),
    )(proj, mu, dps)


def _pre_tile(k, wf, wb, af, ab, k_k, k_a, w0s, w2s, a0s, a2s, seg):
    kx = k * k_k
    ss = _hdot(kx * kx, seg)
    kk = kx / jnp.maximum(jnp.sqrt(ss), 1e-12)
    outs = [kk]
    for d, (w_in, a_in) in enumerate(((wf, af), (wb, ab))):
        z = w0s[d] + _mm_nn(jnp.tanh(w_in), w2s[d])
        wd = -_softplus(-z) - 0.5
        dec = jnp.exp(-jnp.exp(wd))
        ad = jax.nn.sigmoid(a0s[d] + _mm_nn(a_in, a2s[d]))
        kd = k * (1.0 + (ad - 1.0) * k_a)
        outs += [dec, kd, kk * ad]
    return outs


PT = 256


def _pre_load(ps_ref, kk_ref, ka_ref, w0_ref, w2_ref, a0_ref, a2_ref):
    k = ps_ref[:, RW:2 * RW]
    wf, wb = ps_ref[:, 3 * RW:3 * RW + 64], ps_ref[:, 3 * RW + 64:3 * RW + 128]
    af, ab = ps_ref[:, 3 * RW + 128:3 * RW + 192], ps_ref[:, 3 * RW + 192:3 * RW + 256]
    w0s = [w0_ref[0:1, :], w0_ref[1:2, :]]
    a0s = [a0_ref[0:1, :], a0_ref[1:2, :]]
    w2s = [w2_ref[0], w2_ref[1]]
    a2s = [a2_ref[0], a2_ref[1]]
    return (k, wf, wb, af, ab, kk_ref[...], ka_ref[...], w0s, w2s, a0s, a2s)


def _pre_specs():
    c = lambda shape: _bs(shape, lambda i: tuple(0 for _ in shape))
    return [_bs((PT, RSW), lambda i: (i, 0)), c((1, RW)), c((1, RW)), c((2, RW)), c((2, 64, RW)), c((2, RW)),
            c((2, 64, RW))]


def _pre_fwd(ps, k_k, k_a, w0, w2, a0, a2):
    s = ps.shape[0]

    def body(ps_ref, kk_ref, ka_ref, w0_ref, w2_ref, a0_ref, a2_ref, *outs):
        args = _pre_load(ps_ref, kk_ref, ka_ref, w0_ref, w2_ref, a0_ref, a2_ref)
        res = _pre_tile(*args, _seg_matrix(RW, HD))
        for o_ref, v in zip(outs, res):
            o_ref[...] = v

    return pl.pallas_call(
        body, name="rwkv_pre_fwd", grid=(s // PT,),
        in_specs=_pre_specs(), out_specs=[_bs((PT, RW), lambda i: (i, 0))] * 7,
        out_shape=[jax.ShapeDtypeStruct((s, RW), F32)] * 7,
        compiler_params=_params(("parallel",)),
    )(ps, k_k, k_a, w0, w2, a0, a2)


def _pre_bwd(ps, k_k, k_a, w0, w2, a0, a2, dr, dv, cts):
    s = ps.shape[0]

    def body(ps_ref, kk_ref, ka_ref, w0_ref, w2_ref, a0_ref, a2_ref, dr_ref, dv_ref, c0, c1, c2, c3, c4, c5, c6,
             dps_ref, dkk_ref, dka_ref, dw0_ref, dw2_ref, da0_ref, da2_ref):
        @pl.when(pl.program_id(0) == 0)
        def _():
            for r in (dkk_ref, dka_ref, dw0_ref, dw2_ref, da0_ref, da2_ref):
                r[...] = jnp.zeros_like(r)

        args = _pre_load(ps_ref, kk_ref, ka_ref, w0_ref, w2_ref, a0_ref, a2_ref)
        seg = _seg_matrix(RW, HD)
        _, vjp = jax.vjp(lambda *a: _pre_tile(*a, seg), *args)
        dk, dwf, dwb, daf, dab, dk_k, dk_a, dw0s, dw2s, da0s, da2s = vjp([c[...] for c in (c0, c1, c2, c3, c4, c5, c6)])
        dps_ref[:, 0:RW] = dr_ref[...]
        dps_ref[:, RW:2 * RW] = dk
        dps_ref[:, 2 * RW:3 * RW] = dv_ref[...]
        for j, t in enumerate((dwf, dwb, daf, dab)):
            dps_ref[:, 3 * RW + 64 * j:3 * RW + 64 * (j + 1)] = t
        dkk_ref[...] += dk_k
        dka_ref[...] += dk_a
        for d in range(2):
            dw0_ref[d:d + 1, :] += dw0s[d]
            da0_ref[d:d + 1, :] += da0s[d]
            dw2_ref[d] += dw2s[d]
            da2_ref[d] += da2s[d]

    c = lambda shape: _bs(shape, lambda i: tuple(0 for _ in shape))
    row = _bs((PT, RW), lambda i: (i, 0))
    return pl.pallas_call(
        body, name="rwkv_pre_bwd", grid=(s // PT,),
        in_specs=_pre_specs() + [row] * 9,
        out_specs=[_bs((PT, RSW), lambda i: (i, 0)), c((1, RW)), c((1, RW)), c((2, RW)), c((2, 64, RW)), c((2, RW)),
                   c((2, 64, RW))],
        out_shape=[jax.ShapeDtypeStruct((s, RSW), F32), jax.ShapeDtypeStruct((1, RW), F32),
                   jax.ShapeDtypeStruct((1, RW), F32), jax.ShapeDtypeStruct((2, RW), F32),
                   jax.ShapeDtypeStruct((2, 64, RW), F32), jax.ShapeDtypeStruct((2, RW), F32),
                   jax.ShapeDtypeStruct((2, 64, RW), F32)],
        compiler_params=_params(("arbitrary",)),
    )(ps, k_k, k_a, w0, w2, a0, a2, dr, dv, *cts)


def _post_tile(y0, y1, r, v, kd0, kd1, rg, r_k, ln_w, ln_b, seg):
    ysum = y0 + y1
    bonus = (_hdot(r * kd0 * r_k, seg) + _hdot(r * kd1 * r_k, seg)) * v
    mean = _hdot(ysum, seg) * (1.0 / HD)
    cen = ysum - mean
    var = _hdot(cen * cen, seg) * (1.0 / HD)
    y = cen * lax.rsqrt(var + GN_EPS) * ln_w + ln_b + bonus
    return y * _silu(rg)


def _post_specs():
    row = _bs((PT, RW), lambda i: (i, 0))
    c = _bs((1, RW), lambda i: (0, 0))
    return [row, row, _bs((PT, RW), lambda i: (i, 0)), _bs((PT, RW), lambda i: (i, 2)), row, row,
            _bs((PT, RW), lambda i: (i, C_RG // RW)), c, c, c]


def _post_fwd(y0, y1, ps, kd0, kd1, proj, r_k, ln_w, ln_b):
    s = ps.shape[0]

    def body(y0_ref, y1_ref, r_ref, v_ref, kd0_ref, kd1_ref, rg_ref, rk_ref, lw_ref, lb_ref, o_ref):
        o_ref[...] = _post_tile(y0_ref[...], y1_ref[...], r_ref[...], v_ref[...], kd0_ref[...], kd1_ref[...],
                                rg_ref[...], rk_ref[...], lw_ref[...], lb_ref[...], _seg_matrix(RW, HD))

    return pl.pallas_call(
        body, name="rwkv_post_fwd", grid=(s // PT,),
        in_specs=_post_specs(), out_specs=_bs((PT, RW), lambda i: (i, 0)),
        out_shape=jax.ShapeDtypeStruct((s, RW), F32),
        compiler_params=_params(("parallel",)),
    )(y0, y1, ps, ps, kd0, kd1, proj, r_k, ln_w, ln_b)


def _post_bwd(y0, y1, ps, kd0, kd1, proj, r_k, ln_w, ln_b, dy):
    s = ps.shape[0]

    def body(y0_ref, y1_ref, r_ref, v_ref, kd0_ref, kd1_ref, rg_ref, rk_ref, lw_ref, lb_ref, dy_ref,
             dys_ref, dr_ref, dv_ref, dkd0_ref, dkd1_ref, drg_ref, drk_ref, dlw_ref, dlb_ref):
        @pl.when(pl.program_id(0) == 0)
        def _():
            for r in (drk_ref, dlw_ref, dlb_ref):
                r[...] = jnp.zeros_like(r)

        seg = _seg_matrix(RW, HD)
        args = [t[...] for t in (y0_ref, y1_ref, r_ref, v_ref, kd0_ref, kd1_ref, rg_ref, rk_ref, lw_ref, lb_ref)]
        _, vjp = jax.vjp(lambda *a: _post_tile(*a, seg), *args)
        dy0, _, dr, dv, dkd0, dkd1, drg, drk, dlw, dlb = vjp(dy_ref[...])
        dys_ref[...] = dy0
        dr_ref[...] = dr
        dv_ref[...] = dv
        dkd0_ref[...] = dkd0
        dkd1_ref[...] = dkd1
        drg_ref[...] = drg
        drk_ref[...] += drk
        dlw_ref[...] += dlw
        dlb_ref[...] += dlb

    row = _bs((PT, RW), lambda i: (i, 0))
    c = _bs((1, RW), lambda i: (0, 0))
    return pl.pallas_call(
        body, name="rwkv_post_bwd", grid=(s // PT,),
        in_specs=_post_specs() + [row], out_specs=[row] * 6 + [c] * 3,
        out_shape=[jax.ShapeDtypeStruct((s, RW), F32)] * 6 + [jax.ShapeDtypeStruct((1, RW), F32)] * 3,
        compiler_params=_params(("arbitrary",)),
    )(y0, y1, ps, ps, kd0, kd1, proj, r_k, ln_w, ln_b, dy)


def _ones2():
    r = lax.broadcasted_iota(jnp.int32, (256, 128), 0) % 128 // HD
    c = lax.broadcasted_iota(jnp.int32, (256, 128), 1) // HD
    return (r == c).astype(BF16)


def _split(p):
    hi = p.astype(BF16)
    lo = (p - hi.astype(F32)).astype(BF16)
    return jnp.concatenate([hi, lo], axis=1)


def _to_t8(a):
    s = a.shape[0]
    t = a.reshape(s // 8, 8, NPAIR, 2, HD).transpose(0, 2, 4, 3, 1)
    t = jnp.pad(t, ((0, 0), (0, 0), (0, 0), (0, 0), (0, HD - 8))).reshape(s // 8, NPAIR, HD, 128)
    hi = t.astype(BF16)
    lo = (t - hi.astype(F32)).astype(BF16)
    return jnp.concatenate([hi, lo], axis=-1)


def _from_t8(t8):
    g = t8.shape[0]
    t = t8.reshape(g, NPAIR, HD, 2, HD)[..., :8]
    return t.transpose(0, 4, 1, 3, 2).reshape(g * 8, RW)


def _scan_specs(direction, nc, fwd_order):
    def tb(c):
        sc = c if fwd_order else nc - 1 - c
        return sc if direction == 0 else nc - 1 - sc

    row = _bs((TC, RW), lambda c: (tb(c), 0))
    rowv = _bs((TC, RW), lambda c: (tb(c), 2))
    return row, rowv


def _put_t8(ref, g, u, tiles):
    for p in range(NPAIR):
        ref[g, p, :, u:u + 1] = tiles[p][:, u:u + 1]
        ref[g, p, :, HD + u:HD + u + 1] = tiles[p][:, HD + u:HD + u + 1]


def _scan_fwd(dec, kd, b, ps, kk, vl, direction):
    s = dec.shape[0]
    nc, ng = s // TC, TC // 8
    row, t8_in, t8_out = _scan_specs(direction, nc, True)
    n = NPAIR * HD

    def body(dec_ref, kd_ref, b_ref, r_ref, kk_ref, vl_ref, y8_ref, ck_ref, st):
        @pl.when(pl.program_id(0) == 0)
        def _():
            st[...] = jnp.zeros_like(st)

        ck_ref[0] = st[...]
        ones2 = _ones2()
        lane_u = lax.broadcasted_iota(jnp.int32, (HD, 256), 1) % HD
        tiles = lambda res, k: [res[k * n + p * HD:k * n + (p + 1) * HD] for p in range(NPAIR)]

        def group(gi, carry):
            g = gi if direction == 0 else ng - 1 - gi
            rows8 = pl.ds(pl.multiple_of(g * 8, 8), 8)
            d8, k8, b8, r8, kk8 = (q[rows8, :] for q in (dec_ref, kd_ref, b_ref, r_ref, kk_ref))
            pc = [slice(p * 128, (p + 1) * 128) for p in range(NPAIR)]
            ss = [st[p] for p in range(NPAIR)]
            u_prev = None
            for ui in range(8):
                u = ui if direction == 0 else 7 - ui
                lhs = [_split(ss[p] * kk8[u:u + 1, pc[p]]) for p in range(NPAIR)]
                for p in range(NPAIR):
                    vt = vl_ref[g, p]
                    lhs.append(jnp.where(lane_u == u, vt, jnp.zeros_like(vt)))
                if u_prev is not None:
                    lhs += [_split(ss[p] * r8[u_prev:u_prev + 1, pc[p]]) for p in range(NPAIR)]
                res = jnp.dot(jnp.concatenate(lhs, axis=0), ones2, preferred_element_type=F32)
                if u_prev is not None:
                    _put_t8(y8_ref, g, u_prev, tiles(res, 2))
                sa, vb = tiles(res, 0), tiles(res, 1)
                for p in range(NPAIR):
                    ss[p] = ss[p] * d8[u:u + 1, pc[p]] - sa[p] * b8[u:u + 1, pc[p]] + vb[p] * k8[u:u + 1, pc[p]]
                u_prev = u
            lhs = [_split(ss[p] * r8[u_prev:u_prev + 1, pc[p]]) for p in range(NPAIR)]
            res = jnp.dot(jnp.concatenate(lhs, axis=0), ones2, preferred_element_type=F32)
            _put_t8(y8_ref, g, u_prev, tiles(res, 0))
            for p in range(NPAIR):
                st[p] = ss[p]
            return carry

        lax.fori_loop(0, ng, group, 0)

    return pl.pallas_call(
        body, name=f"rwkv_scan_fwd{direction}", grid=(nc,),
        in_specs=[row, row, row, row, row, t8_in],
        out_specs=[t8_out, _bs((1, NPAIR, HD, 128), lambda c: (c, 0, 0, 0))],
        out_shape=[jax.ShapeDtypeStruct((s // 8, NPAIR, HD, 128), F32),
                   jax.ShapeDtypeStruct((nc, NPAIR, HD, 128), F32)],
        scratch_shapes=[pltpu.VMEM((NPAIR, HD, 128), F32)],
        compiler_params=_params(("arbitrary",)),
    )(dec, kd, b, ps, kk, vl)


def _scan_bwd(dec, kd, b, ps, kk, vl, dyl, ck, direction):
    s = dec.shape[0]
    nc, ng = s // TC, TC // 8
    row, t8_in, t8_out = _scan_specs(direction, nc, False)
    n = NPAIR * HD

    def body(dec_ref, kd_ref, b_ref, r_ref, kk_ref, vl_ref, dyl_ref, ck_ref,
             dr_ref, dd_ref, db_ref, dk_ref, dkk_ref, dv8_ref, st, sa_s, vb_s, dy_s, ds):
        @pl.when(pl.program_id(0) == 0)
        def _():
            ds[...] = jnp.zeros_like(ds)

        st[0] = ck_ref[0]
        ones2 = _ones2()
        lane_u = lax.broadcasted_iota(jnp.int32, (HD, 256), 1) % HD
        row_id = lax.broadcasted_iota(jnp.int32, (8, 128), 0)
        pc = [slice(p * 128, (p + 1) * 128) for p in range(NPAIR)]
        tiles = lambda res, k: [res[k * n + p * HD:k * n + (p + 1) * HD] for p in range(NPAIR)]

        def fgroup(gi, carry):
            g = gi if direction == 0 else ng - 1 - gi
            rows8 = pl.ds(pl.multiple_of(g * 8, 8), 8)
            d8, k8, b8, kk8 = (q[rows8, :] for q in (dec_ref, kd_ref, b_ref, kk_ref))
            ss = [st[gi * 8, p] for p in range(NPAIR)]
            for ui in range(8):
                u = ui if direction == 0 else 7 - ui
                i = gi * 8 + ui
                lhs = [_split(ss[p] * kk8[u:u + 1, pc[p]]) for p in range(NPAIR)]
                for ref in (vl_ref, dyl_ref):
                    for p in range(NPAIR):
                        t = ref[g, p]
                        lhs.append(jnp.where(lane_u == u, t, jnp.zeros_like(t)))
                res = jnp.dot(jnp.concatenate(lhs, axis=0), ones2, preferred_element_type=F32)
                sa, vb, dyb = tiles(res, 0), tiles(res, 1), tiles(res, 2)
                for p in range(NPAIR):
                    sa_s[i, p] = sa[p]
                    vb_s[i, p] = vb[p]
                    dy_s[i, p] = dyb[p]
                    ss[p] = ss[p] * d8[u:u + 1, pc[p]] - sa[p] * b8[u:u + 1, pc[p]] + vb[p] * k8[u:u + 1, pc[p]]
                    st[i + 1, p] = ss[p]
            return carry

        lax.fori_loop(0, ng, fgroup, 0)

        def bgroup(gj, carry):
            gi = ng - 1 - gj
            g = gi if direction == 0 else ng - 1 - gi
            rows8 = pl.ds(pl.multiple_of(g * 8, 8), 8)
            d8, k8, b8, r8, kk8 = (q[rows8, :] for q in (dec_ref, kd_ref, b_ref, r_ref, kk_ref))
            dss = [ds[p] for p in range(NPAIR)]
            acc = [[jnp.zeros((8, 128), F32) for _ in range(5)] for _ in range(NPAIR)]
            for uj in range(8):
                ui = 7 - uj
                u = ui if direction == 0 else 7 - ui
                i = gi * 8 + ui
                dyb = [dy_s[i, p] for p in range(NPAIR)]
                for p in range(NPAIR):
                    dss[p] = dss[p] + dyb[p] * r8[u:u + 1, pc[p]]
                lhs = [_split(dss[p] * b8[u:u + 1, pc[p]]) for p in range(NPAIR)]
                lhs += [_split(dss[p] * k8[u:u + 1, pc[p]]) for p in range(NPAIR)]
                res = jnp.dot(jnp.concatenate(lhs, axis=0), ones2, preferred_element_type=F32)
                dsa, dvb = tiles(res, 0), tiles(res, 1)
                _put_t8(dv8_ref, g, u, dvb)
                for p in range(NPAIR):
                    sp, sn = st[i, p], st[i + 1, p]
                    outs = (jnp.sum(sn * dyb[p], axis=0, keepdims=True), jnp.sum(dss[p] * sp, axis=0, keepdims=True),
                            -jnp.sum(dss[p] * sa_s[i, p], axis=0, keepdims=True),
                            jnp.sum(dss[p] * vb_s[i, p], axis=0, keepdims=True),
                            -jnp.sum(sp * dsa[p], axis=0, keepdims=True))
                    acc[p] = [jnp.where(row_id == u, o, a_) for o, a_ in zip(outs, acc[p])]
                    dss[p] = dss[p] * d8[u:u + 1, pc[p]] - dsa[p] * kk8[u:u + 1, pc[p]]
            for p in range(NPAIR):
                ds[p] = dss[p]
                for o_ref, a_ in zip((dr_ref, dd_ref, db_ref, dk_ref, dkk_ref), acc[p]):
                    o_ref[rows8, pc[p]] = a_
            return carry

        lax.fori_loop(0, ng, bgroup, 0)

    chunk = lambda k: pltpu.VMEM((k, NPAIR, HD, 128), F32)
    return pl.pallas_call(
        body, name=f"rwkv_scan_bwd{direction}", grid=(nc,),
        in_specs=[row, row, row, row, row, t8_in, t8_in, _bs((1, NPAIR, HD, 128), lambda c: (nc - 1 - c, 0, 0, 0))],
        out_specs=[row] * 5 + [t8_out],
        out_shape=[jax.ShapeDtypeStruct((s, RW), F32)] * 5 + [jax.ShapeDtypeStruct((s // 8, NPAIR, HD, 128), F32)],
        scratch_shapes=[chunk(TC + 1), chunk(TC), chunk(TC), chunk(TC), pltpu.VMEM((NPAIR, HD, 128), F32)],
        compiler_params=_params(("arbitrary",)),
    )(dec, kd, b, ps, kk, vl, dyl, ck)


def _tiles(res, k):
    n = NPAIR * HD
    return [res[k * n + p * HD:k * n + (p + 1) * HD] for p in range(NPAIR)]


def _rows_to_tiles(src_ref, rows8, stage, out_s, base):
    for p in range(NPAIR):
        stage[base + p, 0:8, 0:HD] = src_ref[rows8, p * 128:p * 128 + HD]
        stage[base + p, HD:HD + 8, 0:HD] = src_ref[rows8, p * 128 + HD:(p + 1) * 128]
        out_s[base + p] = stage[base + p].T[0:HD].astype(BF16)


def _tiles_to_rows(tile_s, base, dst_ref, rows8):
    for p in range(NPAIR):
        t = jnp.concatenate([tile_s[base + p], jnp.zeros((HD, 128), F32)], axis=0).T
        dst_ref[rows8, p * 128:p * 128 + HD] = t[0:8, 0:HD]
        dst_ref[rows8, p * 128 + HD:(p + 1) * 128] = t[HD:HD + 8, 0:HD]


def _put_cols(tile_s, base, u, tiles):
    for p in range(NPAIR):
        tile_s[base + p, :, u:u + 1] = tiles[p][:, u:u + 1]
        tile_s[base + p, :, HD + u:HD + u + 1] = tiles[p][:, HD + u:HD + u + 1]


def _scan2_fwd(per_dir, ps, kk):
    s = ps.shape[0]
    nc, ng = s // TC, TC // 8
    in_specs, operands, out_specs, out_shape = [], [], [], []
    for d in (0, 1):
        row, rowv = _scan_specs(d, nc, True)
        in_specs += [row] * 5 + [rowv]
        operands += list(per_dir[d]) + [ps, kk, ps]
        out_specs += [row, _bs((1, NPAIR, HD, 128), lambda c: (c, 0, 0, 0))]
        out_shape += [jax.ShapeDtypeStruct((s, RW), F32), jax.ShapeDtypeStruct((nc, NPAIR, HD, 128), F32)]

    def body(*refs):
        ins = [refs[0:6], refs[6:12]]
        y_refs, ck_refs = (refs[12], refs[14]), (refs[13], refs[15])
        st, vt_s, yt_s, stage = refs[16:]

        @pl.when(pl.program_id(0) == 0)
        def _():
            st[...] = jnp.zeros_like(st)
            yt_s[...] = jnp.zeros_like(yt_s)
            stage[...] = jnp.zeros_like(stage)

        for d in (0, 1):
            ck_refs[d][0] = st[d * NPAIR:(d + 1) * NPAIR]
        ones2 = _ones2()
        ones1 = ones2[0:128]
        lane_u = lax.broadcasted_iota(jnp.int32, (HD, 128), 1) % HD
        pc = [slice(p * 128, (p + 1) * 128) for p in range(NPAIR)]

        def group(gi, carry):
            gs = (gi, ng - 1 - gi)
            rows8 = [pl.ds(pl.multiple_of(gs[d] * 8, 8), 8) for d in (0, 1)]
            blk = [[q[rows8[d], :] for q in ins[d][:5]] for d in (0, 1)]
            for d in (0, 1):
                _rows_to_tiles(ins[d][5], rows8[d], stage, vt_s, d * NPAIR)
            ss = [[st[d * NPAIR + p] for p in range(NPAIR)] for d in (0, 1)]
            for ui in range(9):
                us, ups = (ui, 7 - ui), (ui - 1, 8 - ui)
                lhs2, lhs1, where = [], [], {}
                for d in (0, 1):
                    if ui < 8:
                        where["sa", d] = len(lhs2) // NPAIR
                        lhs2 += [_split(ss[d][p] * blk[d][4][us[d]:us[d] + 1, pc[p]]) for p in range(NPAIR)]
                        where["vb", d] = len(lhs1) // NPAIR
                        for p in range(NPAIR):
                            vt = vt_s[d * NPAIR + p]
                            lhs1.append(jnp.where(lane_u == us[d], vt, jnp.zeros_like(vt)))
                    if ui > 0:
                        where["y", d] = len(lhs1) // NPAIR
                        lhs1 += [(ss[d][p] * blk[d][3][ups[d]:ups[d] + 1, pc[p]]).astype(BF16) for p in range(NPAIR)]
                if ui < 8:
                    res2 = jnp.dot(jnp.concatenate(lhs2, axis=0), ones2, preferred_element_type=F32)
                res1 = jnp.dot(jnp.concatenate(lhs1, axis=0), ones1, preferred_element_type=F32)
                for d in (0, 1):
                    d8, k8, b8, _, _ = blk[d]
                    u = us[d]
                    if ui < 8:
                        sa, vb = _tiles(res2, where["sa", d]), _tiles(res1, where["vb", d])
                        for p in range(NPAIR):
                            ss[d][p] = (ss[d][p] * d8[u:u + 1, pc[p]] - sa[p] * b8[u:u + 1, pc[p]]
                                        + vb[p] * k8[u:u + 1, pc[p]])
                    if ui > 0:
                        _put_cols(yt_s, d * NPAIR, ups[d], _tiles(res1, where["y", d]))
            for d in (0, 1):
                _tiles_to_rows(yt_s, d * NPAIR, y_refs[d], rows8[d])
                for p in range(NPAIR):
                    st[d * NPAIR + p] = ss[d][p]
            return carry

        lax.fori_loop(0, ng, group, 0)

    outs = pl.pallas_call(
        body, name="rwkv_scan_fwd", grid=(nc,), in_specs=in_specs, out_specs=out_specs, out_shape=out_shape,
        scratch_shapes=[pltpu.VMEM((2 * NPAIR, HD, 128), F32), pltpu.VMEM((2 * NPAIR, HD, 128), BF16),
                        pltpu.VMEM((2 * NPAIR, HD, 128), F32), pltpu.VMEM((2 * NPAIR, 128, 128), F32)],
        compiler_params=_params(("arbitrary",)),
    )(*operands)
    return [(outs[0], outs[1]), (outs[2], outs[3])]


def _scan2_bwd(per_dir, ps, kk, dy):
    s = ps.shape[0]
    nc, ng = s // TC, TC // 8
    in_specs, operands, out_specs, out_shape = [], [], [], []
    for d in (0, 1):
        row, rowv = _scan_specs(d, nc, False)
        dec, kd, b, ck = per_dir[d]
        in_specs += [row] * 5 + [rowv, row, _bs((1, NPAIR, HD, 128), lambda c: (nc - 1 - c, 0, 0, 0))]
        operands += [dec, kd, b, ps, kk, ps, dy, ck]
        out_specs += [row] * 6
        out_shape += [jax.ShapeDtypeStruct((s, RW), F32)] * 6

    def body(*refs):
        ins = [refs[0:8], refs[8:16]]
        outs = [refs[16:22], refs[22:28]]
        st, sa_s, vb_s, dy_s, ds, vt_s, dyt_s, dvt_s, stage = refs[28:]

        @pl.when(pl.program_id(0) == 0)
        def _():
            dvt_s[...] = jnp.zeros_like(dvt_s)
            stage[...] = jnp.zeros_like(stage)
            ds[...] = jnp.zeros_like(ds)

        for d in (0, 1):
            st[d * (TC + 1)] = ins[d][7][0]
        ones2 = _ones2()
        ones1 = ones2[0:128]
        lane_u = lax.broadcasted_iota(jnp.int32, (HD, 128), 1) % HD
        row_id = lax.broadcasted_iota(jnp.int32, (8, 128), 0)
        pc = [slice(p * 128, (p + 1) * 128) for p in range(NPAIR)]

        def load_rows(gs):
            return [[q[pl.ds(pl.multiple_of(gs[d] * 8, 8), 8), :] for q in ins[d][:5]] for d in (0, 1)]

        def fgroup(gi, carry):
            gs = (gi, ng - 1 - gi)
            blk = load_rows(gs)
            for d in (0, 1):
                rows8 = pl.ds(pl.multiple_of(gs[d] * 8, 8), 8)
                _rows_to_tiles(ins[d][5], rows8, stage, vt_s, d * NPAIR)
                _rows_to_tiles(ins[d][6], rows8, stage, dyt_s, d * NPAIR)
            ss = [[st[d * (TC + 1) + gi * 8, p] for p in range(NPAIR)] for d in (0, 1)]
            for ui in range(8):
                us = (ui, 7 - ui)
                i = gi * 8 + ui
                lhs2, lhs1 = [], []
                for d in (0, 1):
                    kk8 = blk[d][4]
                    lhs2 += [_split(ss[d][p] * kk8[us[d]:us[d] + 1, pc[p]]) for p in range(NPAIR)]
                    for tile_s in (vt_s, dyt_s):
                        for p in range(NPAIR):
                            t = tile_s[d * NPAIR + p]
                            lhs1.append(jnp.where(lane_u == us[d], t, jnp.zeros_like(t)))
                res2 = jnp.dot(jnp.concatenate(lhs2, axis=0), ones2, preferred_element_type=F32)
                res1 = jnp.dot(jnp.concatenate(lhs1, axis=0), ones1, preferred_element_type=F32)
                for d in (0, 1):
                    d8, k8, b8, _, _ = blk[d]
                    u = us[d]
                    sa, vb, dyb = _tiles(res2, d), _tiles(res1, 2 * d), _tiles(res1, 2 * d + 1)
                    for p in range(NPAIR):
                        sa_s[d * TC + i, p] = sa[p]
                        vb_s[d * TC + i, p] = vb[p]
                        dy_s[d * TC + i, p] = dyb[p]
                        ss[d][p] = ss[d][p] * d8[u:u + 1, pc[p]] - sa[p] * b8[u:u + 1, pc[p]] + vb[p] * k8[u:u + 1, pc[p]]
                        st[d * (TC + 1) + i + 1, p] = ss[d][p]
            return carry

        lax.fori_loop(0, ng, fgroup, 0)

        def bgroup(gj, carry):
            gi = ng - 1 - gj
            gs = (gi, ng - 1 - gi)
            blk = load_rows(gs)
            dss = [[ds[d * NPAIR + p] for p in range(NPAIR)] for d in (0, 1)]
            acc = [[[jnp.zeros((8, 128), F32) for _ in range(5)] for _ in range(NPAIR)] for _ in (0, 1)]
            for uj in range(8):
                ui = 7 - uj
                us = (ui, 7 - ui)
                i = gi * 8 + ui
                lhs2, lhs1, dyb = [], [], [None, None]
                for d in (0, 1):
                    _, k8, b8, r8, _ = blk[d]
                    u = us[d]
                    dyb[d] = [dy_s[d * TC + i, p] for p in range(NPAIR)]
                    for p in range(NPAIR):
                        dss[d][p] = dss[d][p] + dyb[d][p] * r8[u:u + 1, pc[p]]
                    lhs2 += [_split(dss[d][p] * b8[u:u + 1, pc[p]]) for p in range(NPAIR)]
                    lhs1 += [(dss[d][p] * k8[u:u + 1, pc[p]]).astype(BF16) for p in range(NPAIR)]
                res2 = jnp.dot(jnp.concatenate(lhs2, axis=0), ones2, preferred_element_type=F32)
                res1 = jnp.dot(jnp.concatenate(lhs1, axis=0), ones1, preferred_element_type=F32)
                for d in (0, 1):
                    d8, _, _, _, kk8 = blk[d]
                    u = us[d]
                    dsa, dvb = _tiles(res2, d), _tiles(res1, d)
                    _put_cols(dvt_s, d * NPAIR, u, dvb)
                    for p in range(NPAIR):
                        sp, sn = st[d * (TC + 1) + i, p], st[d * (TC + 1) + i + 1, p]
                        dsv = dss[d][p]
                        vals = (jnp.sum(sn * dyb[d][p], axis=0, keepdims=True), jnp.sum(dsv * sp, axis=0, keepdims=True),
                                -jnp.sum(dsv * sa_s[d * TC + i, p], axis=0, keepdims=True),
                                jnp.sum(dsv * vb_s[d * TC + i, p], axis=0, keepdims=True),
                                -jnp.sum(sp * dsa[p], axis=0, keepdims=True))
                        acc[d][p] = [jnp.where(row_id == u, o, a_) for o, a_ in zip(vals, acc[d][p])]
                        dss[d][p] = dsv * d8[u:u + 1, pc[p]] - dsa[p] * kk8[u:u + 1, pc[p]]
            for d in (0, 1):
                rows8 = pl.ds(pl.multiple_of(gs[d] * 8, 8), 8)
                _tiles_to_rows(dvt_s, d * NPAIR, outs[d][5], rows8)
                for p in range(NPAIR):
                    ds[d * NPAIR + p] = dss[d][p]
                    for o_ref, a_ in zip(outs[d][:5], acc[d][p]):
                        o_ref[rows8, pc[p]] = a_
            return carry

        lax.fori_loop(0, ng, bgroup, 0)

    chunk = lambda k: pltpu.VMEM((k, NPAIR, HD, 128), F32)
    pairs = lambda w, dt: pltpu.VMEM((2 * NPAIR, HD, w), dt)
    res = pl.pallas_call(
        body, name="rwkv_scan_bwd", grid=(nc,), in_specs=in_specs, out_specs=out_specs, out_shape=out_shape,
        scratch_shapes=[chunk(2 * (TC + 1)), chunk(2 * TC), chunk(2 * TC), chunk(2 * TC), pairs(128, F32),
                        pairs(128, BF16), pairs(128, BF16), pairs(128, F32), pltpu.VMEM((2 * NPAIR, 128, 128), F32)],
        compiler_params=_params(("arbitrary",)),
    )(*operands)
    return [res[0:6], res[6:12]]


MT = 256
MN = 256


def _merge_fwd(ya, yr, yx, wa, wr, wx, proj, gate_b):
    s = ya.shape[0]

    def body(ya_ref, yr_ref, yx_ref, wa_ref, wr_ref, wx_ref, m0, m1, m2, b0, b1, b2, o_ref):
        acc = jnp.zeros((MT, MN), F32)
        for y_ref, w_ref, m_ref, b_ref in ((ya_ref, wa_ref, m0, b0), (yr_ref, wr_ref, m1, b1), (yx_ref, wx_ref, m2, b2)):
            u = _dot(y_ref[...], w_ref[...], ((1,), (0,)))
            acc = acc + jax.nn.sigmoid(m_ref[...] + b_ref[...]) * u
        o_ref[...] = acc.astype(BF16)

    mg = lambda br: _bs((MT, MN), lambda i, j: (i, C_MG // MN + br * (D // MN) + j))
    gb = lambda br: _bs((1, MN), lambda i, j: (0, br * (D // MN) + j))
    return pl.pallas_call(
        body, name="merge_fwd", grid=(s // MT, D // MN),
        in_specs=[_bs((MT, RW), lambda i, j: (i, 0)), _bs((MT, RW), lambda i, j: (i, 0)), _bs((MT, XW), lambda i, j: (i, 0)),
                  _bs((RW, MN), lambda i, j: (0, j)), _bs((RW, MN), lambda i, j: (0, j)), _bs((XW, MN), lambda i, j: (0, j)),
                  mg(0), mg(1), mg(2), gb(0), gb(1), gb(2)],
        out_specs=_bs((MT, MN), lambda i, j: (i, j)),
        out_shape=jax.ShapeDtypeStruct((s, D), BF16),
        compiler_params=_params(("parallel", "arbitrary")),
    )(ya, yr, yx, wa, wr, wx, proj, proj, proj, gate_b, gate_b, gate_b)


def _out_fwd(merged, w_out, x, target):
    s = x.shape[0]
    tm, tn = min(512, s), 512

    def body(m_ref, w_ref, x_ref, t_ref, loss_ref, d_ref, d16_ref):
        @pl.when((pl.program_id(0) == 0) & (pl.program_id(1) == 0))
        def _():
            loss_ref[...] = jnp.zeros_like(loss_ref)

        out = x_ref[...] + jnp.dot(m_ref[...], w_ref[...], preferred_element_type=F32)
        err = out - t_ref[...]
        dout = err * (1.0 / D)
        d_ref[...] = dout
        d16_ref[...] = dout.astype(BF16)
        loss_ref[...] += jnp.sum(err * err)

    tile = _bs((tm, tn), lambda i, j: (i, j))
    return pl.pallas_call(
        body, name="out_fwd", grid=(s // tm, D // tn),
        in_specs=[_bs((tm, D), lambda i, j: (i, 0)), _bs((D, tn), lambda i, j: (0, j)), tile, tile],
        out_specs=[_bs((8, 128), lambda i, j: (0, 0)), tile, tile],
        out_shape=[jax.ShapeDtypeStruct((8, 128), F32), jax.ShapeDtypeStruct((s, D), F32),
                   jax.ShapeDtypeStruct((s, D), BF16)],
        compiler_params=_params(("arbitrary", "arbitrary")),
    )(merged, w_out, x, target)


def _merge_bwd(ya, yr, yx, wa, wr, wx, proj, gate_b, dmerged):
    s = ya.shape[0]

    def body(ya_ref, yr_ref, yx_ref, wa_ref, wr_ref, wx_ref, m0, m1, m2, b0, b1, b2, dm_ref,
             dg0, dg1, dg2, du0, du1, du2, dya_ref, dyr_ref, dyx_ref):
        @pl.when(pl.program_id(1) == 0)
        def _():
            dya_ref[...] = jnp.zeros_like(dya_ref)
            dyr_ref[...] = jnp.zeros_like(dyr_ref)
            dyx_ref[...] = jnp.zeros_like(dyx_ref)

        dm = dm_ref[...]
        for y_ref, w_ref, m_ref, b_ref, dg_ref, du_ref, dy_ref in (
                (ya_ref, wa_ref, m0, b0, dg0, du0, dya_ref), (yr_ref, wr_ref, m1, b1, dg1, du1, dyr_ref),
                (yx_ref, wx_ref, m2, b2, dg2, du2, dyx_ref)):
            w = w_ref[...]
            u = _dot(y_ref[...], w, ((1,), (0,)))
            gt = jax.nn.sigmoid(m_ref[...] + b_ref[...])
            dg_ref[...] = (dm * u * gt * (1.0 - gt)).astype(BF16)
            du = (dm * gt).astype(BF16)
            du_ref[...] = du
            dy_ref[...] += _dot(du, w, ((1,), (1,)))

    mg = lambda br: _bs((MT, MN), lambda i, j: (i, C_MG // MN + br * (D // MN) + j))
    gb = lambda br: _bs((1, MN), lambda i, j: (0, br * (D // MN) + j))
    tile = _bs((MT, MN), lambda i, j: (i, j))
    return pl.pallas_call(
        body, name="merge_bwd", grid=(s // MT, D // MN),
        in_specs=[_bs((MT, RW), lambda i, j: (i, 0)), _bs((MT, RW), lambda i, j: (i, 0)), _bs((MT, XW), lambda i, j: (i, 0)),
                  _bs((RW, MN), lambda i, j: (0, j)), _bs((RW, MN), lambda i, j: (0, j)), _bs((XW, MN), lambda i, j: (0, j)),
                  mg(0), mg(1), mg(2), gb(0), gb(1), gb(2), tile],
        out_specs=[tile] * 6 + [_bs((MT, RW), lambda i, j: (i, 0)), _bs((MT, RW), lambda i, j: (i, 0)),
                                _bs((MT, XW), lambda i, j: (i, 0))],
        out_shape=[jax.ShapeDtypeStruct((s, D), BF16)] * 6 + [jax.ShapeDtypeStruct((s, RW), F32),
                                                               jax.ShapeDtypeStruct((s, RW), F32),
                                                               jax.ShapeDtypeStruct((s, XW), F32)],
        compiler_params=_params(("parallel", "arbitrary")),
    )(ya, yr, yx, wa, wr, wx, proj, proj, proj, gate_b, gate_b, gate_b, dmerged)


def _colsum(a, name):
    m, n = a.shape
    tm, tn = min(512, m), 512

    def body(a_ref, o_ref):
        @pl.when(pl.program_id(1) == 0)
        def _():
            o_ref[...] = jnp.zeros_like(o_ref)

        o_ref[...] += jnp.sum(a_ref[...].astype(F32), axis=0, keepdims=True)

    return pl.pallas_call(
        body, name=name, grid=(n // tn, m // tm),
        in_specs=[_bs((tm, tn), lambda j, i: (i, j))], out_specs=_bs((1, tn), lambda j, i: (0, j)),
        out_shape=jax.ShapeDtypeStruct((1, n), F32),
        compiler_params=_params(("parallel", "arbitrary")),
    )(a)


def _in_bwd(dproj, w_in, x, g, dout):
    s = x.shape[0]
    tm, tk = min(512, s), 896
    nk = NIN // tk

    def body(dp_ref, w_ref, x_ref, g_ref, do_ref, gx_ref, gg_ref, acc):
        i, kk = pl.program_id(0), pl.program_id(1)

        @pl.when((i == 0) & (kk == 0))
        def _():
            gg_ref[...] = jnp.zeros_like(gg_ref)

        @pl.when(kk == 0)
        def _():
            acc[...] = jnp.zeros_like(acc)

        acc[...] += _dot(dp_ref[...], w_ref[...], ((1,), (1,)))

        @pl.when(kk == nk - 1)
        def _():
            xv, dh, gv = x_ref[...], acc[...], g_ref[...]
            r = lax.rsqrt(jnp.mean(xv * xv, axis=-1, keepdims=True) + NORM_EPS)
            xn = xv * r
            gg_ref[...] += jnp.sum(dh * xn, axis=0, keepdims=True)
            dxn = dh * gv
            dx = r * (dxn - xn * jnp.mean(dxn * xn, axis=-1, keepdims=True))
            gx_ref[...] = do_ref[...] + dx

    return pl.pallas_call(
        body, name="in_bwd", grid=(s // tm, nk),
        in_specs=[_bs((tm, tk), lambda i, kk: (i, kk)), _bs((D, tk), lambda i, kk: (0, kk)),
                  _bs((tm, D), lambda i, kk: (i, 0)), _bs((1, D), lambda i, kk: (0, 0)), _bs((tm, D), lambda i, kk: (i, 0))],
        out_specs=[_bs((tm, D), lambda i, kk: (i, 0)), _bs((1, D), lambda i, kk: (0, 0))],
        out_shape=[jax.ShapeDtypeStruct((s, D), F32), jax.ShapeDtypeStruct((1, D), F32)],
        scratch_shapes=[pltpu.VMEM((tm, D), F32)],
        compiler_params=_params(("arbitrary", "arbitrary")),
    )(dproj, w_in, x, g, dout)


def _adamw_math(w, g, m, v):
    m = ADAM_B1 * m + (1.0 - ADAM_B1) * g
    v = ADAM_B2 * v + (1.0 - ADAM_B2) * jnp.square(g)
    m_hat = m / (1.0 - ADAM_B1 ** ADAM_STEP)
    v_hat = v / (1.0 - ADAM_B2 ** ADAM_STEP)
    delta = -ADAM_LR * (m_hat / (jnp.sqrt(v_hat) + ADAM_EPS) + ADAM_WD * w)
    return delta, m, v


def _adamw(parts, w, m, v, name):
    rows, cols = w.shape
    tr = rows
    for cand in (256, 128, 64, 32, 16, 8):
        if rows % cand == 0 and cand * cols * 4 <= (1 << 20):
            tr = cand
            break
    n = len(parts)

    def body(*refs):
        g = refs[0][...].astype(F32)
        for r in refs[1:n]:
            g = g + r[...].astype(F32)
        w_ref, m_ref, v_ref, g_out, d_out, m_out, v_out = refs[n:]
        delta, m_new, v_new = _adamw_math(w_ref[...], g, m_ref[...], v_ref[...])
        g_out[...] = g
        d_out[...] = delta
        m_out[...] = m_new
        v_out[...] = v_new

    spec = _bs((tr, cols), lambda i: (i, 0))
    return pl.pallas_call(
        body, name=name, grid=(rows // tr,),
        in_specs=[spec] * (n + 3), out_specs=[spec] * 4,
        out_shape=[jax.ShapeDtypeStruct((rows, cols), F32)] * 4,
        compiler_params=_params(("parallel",)),
    )(*parts, w, m, v)


def _adamw_halves(mine, theirs, core, w, m, v, name):
    rows, cols = w.shape
    h = rows // 2
    tr = next(t for t in (256, 128, 64, 32, 16, 8) if h % t == 0 and t * cols * 4 <= (1 << 20))
    nt = h // tr

    def body(core_ref, mine_ref, theirs_ref, w_ref, m_ref, v_ref, g_out, d_out, m_out, v_out):
        is_mine = pl.program_id(0) // nt == core_ref[0]
        g = jnp.where(is_mine, mine_ref[...], theirs_ref[...])
        delta, m_new, v_new = _adamw_math(w_ref[...], g, m_ref[...], v_ref[...])
        g_out[...] = g
        d_out[...] = delta
        m_out[...] = m_new
        v_out[...] = v_new

    spec = _bs((tr, cols), lambda i, core_ref: (i, 0))
    return pl.pallas_call(
        body, name=name,
        grid_spec=pltpu.PrefetchScalarGridSpec(
            num_scalar_prefetch=1, grid=(2 * nt,),
            in_specs=[_bs((tr, cols), lambda i, core_ref: (jnp.clip(i - core_ref[0] * nt, 0, nt - 1), 0)),
                      _bs((tr, cols), lambda i, core_ref: (jnp.clip(i - (1 - core_ref[0]) * nt, 0, nt - 1), 0)),
                      spec, spec, spec],
            out_specs=[spec] * 4),
        out_shape=[jax.ShapeDtypeStruct((rows, cols), F32)] * 4,
        compiler_params=_params(("parallel",)),
    )(core, mine, theirs, w, m, v)


def _sum_parts(parts, name):
    rows, cols = parts[0].shape
    tr = rows
    for cand in (256, 128, 64, 32, 16, 8):
        if rows % cand == 0 and cand * cols * 4 <= (1 << 20):
            tr = cand
            break

    def body(*refs):
        acc = refs[0][...].astype(F32)
        for r in refs[1:-1]:
            acc = acc + r[...].astype(F32)
        refs[-1][...] = acc

    spec = _bs((tr, cols), lambda i: (i, 0))
    return pl.pallas_call(
        body, name=name, grid=(rows // tr,), in_specs=[spec] * len(parts), out_specs=spec,
        out_shape=jax.ShapeDtypeStruct((rows, cols), F32), compiler_params=_params(("parallel",)),
    )(*parts)


ANY = pl.BlockSpec(memory_space=pl.ANY)


def _other_chips(x, y):
    return [(1 - x, y), (x, 1 - y), (1 - x, 1 - y)]


def _gather_shards(arrays, name):
    n = len(arrays)

    def body(*refs):
        ins, outs = refs[:n], refs[n:2 * n]
        ici_send, ici_recv, d2d_send, d2d_recv, local_sems, own_recv = refs[2 * n:]
        x, y, c = lax.axis_index("x"), lax.axis_index("y"), lax.axis_index("c")
        me = 2 * x + y
        chips = _other_chips(x, y)

        def half(i, who):
            h = arrays[i].shape[0] // 2
            return pl.ds(who * h, h)

        def ici(i, j, src_chip, to):
            return pltpu.make_async_remote_copy(
                src_ref=ins[i].at[half(i, c)], dst_ref=outs[i].at[src_chip, half(i, c)], send_sem=ici_send.at[3 * i + j],
                recv_sem=ici_recv.at[3 * i + j], device_id=to, device_id_type=MESH)

        def d2d(i, j, src_chip, who):
            piece = outs[i].at[src_chip, half(i, who)]
            return pltpu.make_async_remote_copy(
                src_ref=piece, dst_ref=piece, send_sem=d2d_send.at[3 * i + j], recv_sem=d2d_recv.at[3 * i + j],
                device_id=(x, y, 1 - c), device_id_type=MESH)

        def own(i):
            return pltpu.make_async_remote_copy(
                src_ref=ins[i], dst_ref=outs[i].at[me], send_sem=local_sems.at[i], recv_sem=own_recv.at[i],
                device_id=(x, y, 1 - c), device_id_type=MESH)

        sends = []
        for i in range(n):
            cp = own(i)
            cp.start()
            sends.append(cp)
            for j, (px, py) in enumerate(chips):
                rc = ici(i, j, me, (px, py, c))
                rc.start()
                sends.append(rc)
        for i in range(n):
            for j, (px, py) in enumerate(chips):
                ici(i, j, 2 * px + py, (px, py, c)).wait_recv()
                fw = d2d(i, j, 2 * px + py, c)
                fw.start()
                sends.append(fw)
        for i in range(n):
            for j, (px, py) in enumerate(chips):
                d2d(i, j, 2 * px + py, 1 - c).wait_recv()
            own(i).wait_recv()
        for rc in sends:
            rc.wait_send()

    dma = lambda k: pltpu.SemaphoreType.DMA((k,))
    return pl.pallas_call(
        body, name=name, in_specs=[ANY] * n, out_specs=[ANY] * n,
        out_shape=[jax.ShapeDtypeStruct((4,) + a.shape, a.dtype) for a in arrays],
        scratch_shapes=[dma(3 * n), dma(3 * n), dma(3 * n), dma(3 * n), dma(n), dma(n)],
        compiler_params=pltpu.CompilerParams(has_side_effects=True),
    )(*arrays)


def _scatter_shards(stacks, name):
    n = len(stacks)

    def body(*refs):
        ins, outs = refs[:n], refs[n:2 * n]
        send_sems, recv_sems = refs[2 * n:]
        x, y, c = lax.axis_index("x"), lax.axis_index("y"), lax.axis_index("c")
        chips = _other_chips(x, y)
        sends = []
        for i in range(n):
            for j, (px, py) in enumerate(chips):
                rc = pltpu.make_async_remote_copy(
                    src_ref=ins[i].at[2 * px + py], dst_ref=outs[i].at[j], send_sem=send_sems.at[3 * i + j],
                    recv_sem=recv_sems.at[3 * i + j], device_id=(px, py, c), device_id_type=MESH)
                rc.start()
                sends.append(rc)
        for rc in sends:
            rc.wait_recv()
        for rc in sends:
            rc.wait_send()

    return pl.pallas_call(
        body, name=name, in_specs=[ANY] * n, out_specs=[ANY] * n,
        out_shape=[jax.ShapeDtypeStruct((3,) + a.shape[1:], a.dtype) for a in stacks],
        scratch_shapes=[pltpu.SemaphoreType.DMA((3 * n,)), pltpu.SemaphoreType.DMA((3 * n,))],
        compiler_params=pltpu.CompilerParams(has_side_effects=True),
    )(*stacks)


def _pair_exchange(stacks, name):
    n = len(stacks)

    def body(*refs):
        ins, outs = refs[:n], refs[n:2 * n]
        send_sems, recv_sems = refs[2 * n:]
        x, y, c = lax.axis_index("x"), lax.axis_index("y"), lax.axis_index("c")
        cps = []
        for i in range(n):
            h = stacks[i].shape[1] // 2
            rc = pltpu.make_async_remote_copy(
                src_ref=ins[i].at[:, pl.ds((1 - c) * h, h)], dst_ref=outs[i], send_sem=send_sems.at[i],
                recv_sem=recv_sems.at[i], device_id=(x, y, 1 - c), device_id_type=MESH)
            rc.start()
            cps.append(rc)
        for rc in cps:
            rc.wait_recv()
        for rc in cps:
            rc.wait_send()

    return pl.pallas_call(
        body, name=name, in_specs=[ANY] * n, out_specs=[ANY] * n,
        out_shape=[jax.ShapeDtypeStruct((4, a.shape[1] // 2) + a.shape[2:], a.dtype) for a in stacks],
        scratch_shapes=[pltpu.SemaphoreType.DMA((n,)), pltpu.SemaphoreType.DMA((n,))],
        compiler_params=pltpu.CompilerParams(has_side_effects=True),
    )(*stacks)


def _pair_sum(own, theirs, core, name):
    _, r, cols = own.shape
    h = r // 2
    tr = next(t for t in (256, 128, 64, 32, 16) if h % t == 0 and t * cols * 4 <= (1 << 20))
    nt = h // tr

    def body(core_ref, own_ref, th_ref, o32_ref, o16_ref):
        del core_ref
        acc = own_ref[...] + th_ref[...].astype(F32)
        o32_ref[...] = acc
        o16_ref[...] = acc.astype(BF16)

    out = _bs((1, tr, cols), lambda j, t, core_ref: (j, t, 0))
    return pl.pallas_call(
        body, name=name,
        grid_spec=pltpu.PrefetchScalarGridSpec(
            num_scalar_prefetch=1, grid=(4, nt),
            in_specs=[_bs((1, tr, cols), lambda j, t, core_ref: (j, core_ref[0] * nt + t, 0)), out],
            out_specs=[out, out]),
        out_shape=[jax.ShapeDtypeStruct((4, h, cols), F32), jax.ShapeDtypeStruct((4, h, cols), BF16)],
        compiler_params=_params(("parallel", "parallel")),
    )(core, own, theirs)


def _swap_sibling(arrays, name):
    n = len(arrays)

    def body(*refs):
        ins, outs = refs[:n], refs[n:2 * n]
        send_sems, recv_sems = refs[2 * n:]
        sib = (lax.axis_index("x"), lax.axis_index("y"), 1 - lax.axis_index("c"))
        cps = []
        for i in range(n):
            rc = pltpu.make_async_remote_copy(src_ref=ins[i], dst_ref=outs[i], send_sem=send_sems.at[i],
                                              recv_sem=recv_sems.at[i], device_id=sib, device_id_type=MESH)
            rc.start()
            cps.append(rc)
        for rc in cps:
            rc.wait_recv()
        for rc in cps:
            rc.wait_send()

    return pl.pallas_call(
        body, name=name, in_specs=[ANY] * n, out_specs=[ANY] * n,
        out_shape=[jax.ShapeDtypeStruct(a.shape, a.dtype) for a in arrays],
        scratch_shapes=[pltpu.SemaphoreType.DMA((n,)), pltpu.SemaphoreType.DMA((n,))],
        compiler_params=pltpu.CompilerParams(has_side_effects=True),
    )(*arrays)


def _all_reduce_small(v):
    rows = v.shape[0]

    def body(v_ref, o_ref, buf, send_sems, recv_sems):
        x, y, c = lax.axis_index("x"), lax.axis_index("y"), lax.axis_index("c")
        me = 4 * x + 2 * y + c
        buf[me] = v_ref[...]
        cps = []
        for kbits in range(1, 8):
            bx, by, bc = (kbits >> 2) & 1, (kbits >> 1) & 1, kbits & 1
            px = jnp.where(bx == 1, 1 - x, x)
            py = jnp.where(by == 1, 1 - y, y)
            pc = jnp.where(bc == 1, 1 - c, c)
            rc = pltpu.make_async_remote_copy(src_ref=v_ref, dst_ref=buf.at[me], send_sem=send_sems.at[kbits - 1],
                                              recv_sem=recv_sems.at[kbits - 1], device_id=(px, py, pc),
                                              device_id_type=MESH)
            rc.start()
            cps.append((rc, 4 * px + 2 * py + pc))
        for kbits, (rc, src) in enumerate(cps):
            pltpu.make_async_remote_copy(src_ref=v_ref, dst_ref=buf.at[src], send_sem=send_sems.at[kbits],
                                         recv_sem=recv_sems.at[kbits], device_id=(x, y, c),
                                         device_id_type=MESH).wait_recv()
        for rc, _ in cps:
            rc.wait_send()
        acc = buf[0]
        for d in range(1, 8):
            acc = acc + buf[d]
        o_ref[...] = acc

    return pl.pallas_call(
        body, name="all_reduce_small",
        in_specs=[pl.BlockSpec(memory_space=pltpu.VMEM)], out_specs=pl.BlockSpec(memory_space=pltpu.VMEM),
        out_shape=jax.ShapeDtypeStruct((rows, 128), F32),
        scratch_shapes=[pltpu.VMEM((8, rows, 128), F32), pltpu.SemaphoreType.DMA((7,)), pltpu.SemaphoreType.DMA((7,))],
        compiler_params=pltpu.CompilerParams(has_side_effects=True, vmem_limit_bytes=VMEM_LIMIT),
    )(v)


def _rope_tables(s):
    half = HD // 2
    inv = 10000.0 ** (-jnp.arange(half, dtype=F32) / half)
    ang = jnp.arange(s, dtype=F32)[:, None] * inv[None, :]
    cos, sin = jnp.cos(ang), jnp.sin(ang)
    return jnp.concatenate([cos, cos], axis=1), jnp.concatenate([sin, sin], axis=1)


def _local_step(x, mem, target, norm_g, mem_norm_g, w_in, gate_b, gq, gk, sink, wa, mu, k_k, k_a, r_k, w0, w2, a0, a2,
                ln_w, ln_b, wr, w_kv, gxq, gxk, wx, w_out):
    s = x.shape[0]
    cos, sin = _rope_tables(s)
    r_k = r_k.reshape(1, RW)

    proj, h = _proj_fwd(x, norm_g, w_in)
    ya = _attn_fwd(proj, cos, sin, gq, gk, sink)
    mkv, mn = _mem_kv(mem, mem_norm_g, w_kv)
    yx = _xattn_fwd(proj, mkv, gxq, gxk)
    ps = _shift_fwd(proj, mu)
    kk, dec0, kd0, b0, dec1, kd1, b1 = _pre_fwd(ps, k_k, k_a, w0, w2, a0, a2)
    (y0, ck0), (y1, ck1) = _scan2_fwd([(dec0, kd0, b0), (dec1, kd1, b1)], ps, kk)
    yr = _post_fwd(y0, y1, ps, kd0, kd1, proj, r_k, ln_w, ln_b)
    merged = _merge_fwd(ya, yr, yx, wa, wr, wx, proj, gate_b)
    loss_tile, dout, dout16 = _out_fwd(merged, w_out, x, target)
    loss_sum = loss_tile[0, 0]

    g = {}
    t16 = lambda a: a.astype(BF16).T
    sk = min(1024, s)
    dmerged = _matmul(dout16, w_out, mode="nt", m=s, n=D, k=D, tm=sk, tn=1024, tk=1024, name="dmerged")
    g["w_out"] = _matmul(merged.T, dout16, mode="nn", m=D, n=D, k=s, tm=1024, tn=1024, tk=sk, name="grad_w_out")
    dg0, dg1, dg2, du0, du1, du2, dya, dyr, dyx = _merge_bwd(ya, yr, yx, wa, wr, wx, proj, gate_b, dmerged)
    g["attn_w_o"] = _matmul(t16(ya), du0, mode="nn", m=RW, n=D, k=s, tm=RW, tn=1024, tk=s, name="grad_attn_w_o")
    g["rwkv_w_o"] = _matmul(t16(yr), du1, mode="nn", m=RW, n=D, k=s, tm=RW, tn=1024, tk=s, name="grad_rwkv_w_o")
    g["x_w_o"] = _matmul(t16(yx), du2, mode="nn", m=XW, n=D, k=s, tm=XW, tn=1024, tk=s, name="grad_x_w_o")
    dmg = jnp.concatenate([dg0, dg1, dg2], axis=1)
    g["gate_b"] = _colsum(dmg, "grad_gate_b")

    daq, dak, dav, dag, g["attn_q_norm_g"], g["attn_k_norm_g"], g["attn_sink"] = _attn_bwd(proj, cos, sin, gq, gk, sink, dya)

    dxq, dxg, dmkv, g["x_q_norm_g"], g["x_k_norm_g"] = _xattn_bwd(proj, mkv, gxq, gxk, dyx)
    g["x_w_kv"] = _matmul(mn, dmkv, mode="tn", m=D, n=2 * XW, k=NMEM, tm=512, tn=512, tk=NMEM, name="grad_x_w_kv")
    dmn = _matmul(dmkv, w_kv, mode="nt", m=NMEM, n=D, k=2 * XW, tm=NMEM, tn=512, tk=2 * XW, name="dmn")
    g["mem_norm_g"] = _mem_bwd(mem, mem_norm_g, dmn)

    dys, dr_p, dv_p, dkd0_p, dkd1_p, drg, g["rwkv_r_k"], g["rwkv_ln_w"], g["rwkv_ln_b"] = _post_bwd(
        y0, y1, ps, kd0, kd1, proj, r_k, ln_w, ln_b, dyr)
    (dr0, dd0, db0, dk0, dkk0, dv0), (dr1, dd1, db1, dk1, dkk1, dv1) = _scan2_bwd(
        [(dec0, kd0, b0, ck0), (dec1, kd1, b1, ck1)], ps, kk, dys)
    dr = dr_p + dr0 + dr1
    dv = dv_p + dv0 + dv1
    cts = (dkk0 + dkk1, dd0, dk0 + dkd0_p, db0, dd1, dk1 + dkd1_p, db1)
    dps, g["rwkv_k_k"], g["rwkv_k_a"], g["rwkv_w0"], g["rwkv_w2"], g["rwkv_a0"], g["rwkv_a2"] = _pre_bwd(
        ps, k_k, k_a, w0, w2, a0, a2, dr, dv, cts)
    drs, g["rwkv_mu"] = _shift_bwd(proj, mu, dps)

    dproj = jnp.concatenate([daq.astype(BF16), dak.astype(BF16), dav.astype(BF16), dag.astype(BF16), drs.astype(BF16),
                             drg.astype(BF16), dxq.astype(BF16), dxg.astype(BF16), dmg], axis=1)
    g["w_in"] = _matmul(h.T, dproj, mode="nn", m=D, n=NIN, k=s, tm=512, tn=896, tk=s, name="grad_w_in")
    grad_x, g["norm_g"] = _in_bwd(dproj, w_in, x, norm_g, dout)
    g["rwkv_r_k"] = g["rwkv_r_k"].reshape(AH, HD)
    return loss_sum, grad_x, g


WEIGHTS = ['norm_g', 'mem_norm_g', 'w_in', 'gate_b', 'attn_q_norm_g', 'attn_k_norm_g', 'attn_sink', 'attn_w_o',
           'rwkv_mu', 'rwkv_k_k', 'rwkv_k_a', 'rwkv_r_k', 'rwkv_w0', 'rwkv_w2', 'rwkv_a0', 'rwkv_a2', 'rwkv_ln_w',
           'rwkv_ln_b', 'rwkv_w_o', 'x_w_kv', 'x_q_norm_g', 'x_k_norm_g', 'x_w_o', 'w_out']
BIG = ['w_in', 'attn_w_o', 'rwkv_w_o', 'x_w_kv', 'x_w_o', 'w_out']
COL_SHARDED = ['w_in', 'attn_w_o', 'rwkv_w_o', 'x_w_o']
LORA = ['rwkv_w0', 'rwkv_w2', 'rwkv_a0', 'rwkv_a2']
SMALL = [n for n in WEIGHTS if n not in BIG]


def _unshard_cols(stack):
    return jnp.concatenate([stack[i] for i in range(4)], axis=-1)


def _shard_cols(full):
    w = full.shape[-1] // 4
    return [full[..., i * w:(i + 1) * w] for i in range(4)]


def kernel(x, mem, norm_g, mem_norm_g, w_in, gate_b, attn_q_norm_g, attn_k_norm_g, attn_sink, attn_w_o, rwkv_mu, rwkv_k_k, rwkv_k_a, rwkv_r_k, rwkv_w0, rwkv_w2, rwkv_a0, rwkv_a2, rwkv_ln_w, rwkv_ln_b, rwkv_w_o, x_w_kv, x_q_norm_g, x_k_norm_g, x_w_o, w_out, loss_target, m_norm_g, m_mem_norm_g, m_w_in, m_gate_b, m_attn_q_norm_g, m_attn_k_norm_g, m_attn_sink, m_attn_w_o, m_rwkv_mu, m_rwkv_k_k, m_rwkv_k_a, m_rwkv_r_k, m_rwkv_w0, m_rwkv_w2, m_rwkv_a0, m_rwkv_a2, m_rwkv_ln_w, m_rwkv_ln_b, m_rwkv_w_o, m_x_w_kv, m_x_q_norm_g, m_x_k_norm_g, m_x_w_o, m_w_out, v_norm_g, v_mem_norm_g, v_w_in, v_gate_b, v_attn_q_norm_g, v_attn_k_norm_g, v_attn_sink, v_attn_w_o, v_rwkv_mu, v_rwkv_k_k, v_rwkv_k_a, v_rwkv_r_k, v_rwkv_w0, v_rwkv_w2, v_rwkv_a0, v_rwkv_a2, v_rwkv_ln_w, v_rwkv_ln_b, v_rwkv_w_o, v_x_w_kv, v_x_q_norm_g, v_x_k_norm_g, v_x_w_o, v_w_out):
    args = dict(locals())
    canon = lambda a: a[0] if a.ndim > 2 else a
    w = {n: canon(args[n]) for n in WEIGHTS}
    m = {n: canon(args["m_" + n]) for n in WEIGHTS}
    v = {n: canon(args["v_" + n]) for n in WEIGHTS}
    shard = 2 * lax.axis_index("x") + lax.axis_index("y")

    local = [w[n].astype(BF16) for n in BIG] + [w[n].reshape(2, -1, w[n].shape[-1]) for n in LORA]
    stacks = dict(zip(BIG + LORA, _gather_shards(local, "gather_weights")))
    full = {}
    for n in COL_SHARDED:
        full[n] = _unshard_cols(stacks[n])
    for n in LORA:
        full[n] = _unshard_cols(stacks[n]).reshape(w[n].shape[:-1] + (RW,))
    full["x_w_kv"] = stacks["x_w_kv"].reshape(D, 2 * XW)
    full["w_out"] = stacks["w_out"].reshape(D, D)

    loss_sum, grad_x, g = _local_step(
        x[0], mem[0], loss_target[0], w["norm_g"], w["mem_norm_g"], full["w_in"], w["gate_b"], w["attn_q_norm_g"],
        w["attn_k_norm_g"], w["attn_sink"], full["attn_w_o"], w["rwkv_mu"], w["rwkv_k_k"], w["rwkv_k_a"], w["rwkv_r_k"],
        full["rwkv_w0"], full["rwkv_w2"], full["rwkv_a0"], full["rwkv_a2"], w["rwkv_ln_w"], w["rwkv_ln_b"],
        full["rwkv_w_o"], full["x_w_kv"], w["x_q_norm_g"], w["x_k_norm_g"], full["x_w_o"], full["w_out"])

    loss = lax.psum(0.5 * loss_sum / D, ("x", "y", "c"))

    def as_stack(n, dtype):
        if n in COL_SHARDED:
            return jnp.stack([p.astype(dtype) for p in _shard_cols(g[n])])
        return g[n].reshape((4, g[n].shape[0] // 4) + g[n].shape[1:]).astype(dtype)

    core = lax.axis_index("c").astype(jnp.int32).reshape(1)
    sibling = _pair_exchange([as_stack(n, BF16) for n in BIG], "pair_exchange")
    pair32, pair16 = [], []
    for n, th in zip(BIG, sibling):
        a32, a16 = _pair_sum(as_stack(n, F32), th, core, "pair_sum_" + n)
        pair32.append(a32)
        pair16.append(a16)
    recv = _scatter_shards(pair16, "scatter_grads")
    halves = []
    for n, p32, r in zip(BIG, pair32, recv):
        own = lax.dynamic_index_in_dim(p32, shard, 0, keepdims=False)
        halves.append(_sum_parts([own, r[0], r[1], r[2]], "sum_" + n))
    other_halves = _swap_sibling(halves, "swap_halves")

    out_g, out_d, out_m, out_v = {}, {}, {}, {}
    for n, mine, theirs in zip(BIG, halves, other_halves):
        out_g[n], out_d[n], out_m[n], out_v[n] = _adamw_halves(mine, theirs, core, w[n], m[n], v[n], "adamw_" + n)

    flat = jnp.concatenate([g[n].reshape(-1) for n in SMALL])
    total = flat.shape[0]
    padded = -(-total // 1024) * 1024
    flat = jnp.pad(flat, (0, padded - total)).reshape(padded // 128, 128)
    red = _all_reduce_small(flat).reshape(-1)
    off = 0
    gs = {}
    for n in SMALL:
        size = g[n].size
        t = red[off:off + size].reshape(g[n].shape)
        off += size
        if n in LORA:
            wd = t.shape[-1] // 4
            t = lax.dynamic_slice_in_dim(t, shard * wd, wd, axis=t.ndim - 1)
        gs[n] = t

    def pack(d):
        f = jnp.concatenate([d[n].reshape(-1) for n in SMALL])
        return jnp.pad(f, (0, -(-f.shape[0] // 1024) * 1024 - f.shape[0])).reshape(-1, 128)

    pg, pd, pm, pv = _adamw([pack(gs)], pack(w), pack(m), pack(v), "adamw_small")
    off = 0
    for n in SMALL:
        size = w[n].size
        for dst, src in ((out_g, pg), (out_d, pd), (out_m, pm), (out_v, pv)):
            dst[n] = src.reshape(-1)[off:off + size].reshape(w[n].shape)
        off += size

    lead = lambda d: [d[n][None] if args[n].ndim > 2 else d[n] for n in WEIGHTS]
    return (loss, grad_x[None], *lead(out_g), *lead(out_d), *lead(out_m), *lead(out_v))
```

```python
import functools

import jax
import jax.numpy as jnp
from jax import lax
from jax.experimental import pallas as pl
from jax.experimental.pallas import tpu as pltpu

F32 = jnp.float32
BF16 = jnp.bfloat16
HI = lax.Precision.HIGHEST
MESH = pl.DeviceIdType.MESH

D = 2048
NMEM = 256
NORM_EPS = 1e-6
NEG_INF = -1e30
GN_EPS = 64e-5
HD = 64
AH = 12
AKV = 4
RW = 768
XH = 4
XD = 128
XW = 512
NIN = 12544
RSW = 2560
C_AQ, C_AK, C_AV, C_AG, C_RS, C_RG, C_XQ, C_XG, C_MG = 0, 768, 1024, 1280, 2048, 4608, 5376, 5888, 6400
WIN = 384
QB = 128
TC = 16
NPAIR = 6

ADAM_LR, ADAM_B1, ADAM_B2, ADAM_EPS, ADAM_WD, ADAM_STEP = 0.001, 0.9, 0.999, 1e-08, 0.01, 10

VMEM_LIMIT = 56 * 1024 * 1024


def _bs(shape, imap):
    return pl.BlockSpec(shape, imap)


def _params(sem=None, vmem=VMEM_LIMIT):
    return pltpu.CompilerParams(dimension_semantics=sem, vmem_limit_bytes=vmem)


def _dot(a, b, dims):
    return lax.dot_general(a.astype(BF16), b.astype(BF16), (dims, ((), ())), preferred_element_type=F32)


@jax.custom_vjp
def _mm_nn(a, b):
    return _dot(a, b, ((1,), (0,)))


def _mm_nn_fwd(a, b):
    return _mm_nn(a, b), (a, b)


def _mm_nn_bwd(res, ct):
    a, b = res
    return _dot(ct, b, ((1,), (1,))), _dot(a, ct, ((0,), (0,)))


_mm_nn.defvjp(_mm_nn_fwd, _mm_nn_bwd)


@jax.custom_vjp
def _mm_nt(a, b):
    return _dot(a, b, ((1,), (1,)))


def _mm_nt_fwd(a, b):
    return _mm_nt(a, b), (a, b)


def _mm_nt_bwd(res, ct):
    a, b = res
    return _dot(ct, b, ((1,), (0,))), _dot(ct, a, ((0,), (0,)))


_mm_nt.defvjp(_mm_nt_fwd, _mm_nt_bwd)


def _seg_matrix(n, seg):
    r = lax.broadcasted_iota(jnp.int32, (n, n), 0) // seg
    c = lax.broadcasted_iota(jnp.int32, (n, n), 1) // seg
    return (r == c).astype(F32)


def _rot_matrix():
    r = lax.broadcasted_iota(jnp.int32, (HD, HD), 0)
    c = lax.broadcasted_iota(jnp.int32, (HD, HD), 1)
    return jnp.where(c == r + HD // 2, 1.0, 0.0).astype(F32) - jnp.where(c == r - HD // 2, 1.0, 0.0).astype(F32)


def _hdot(a, m):
    return jnp.dot(a, m, precision=HI, preferred_element_type=F32)


def _rms(t, g):
    return t * lax.rsqrt(jnp.mean(t * t, axis=-1, keepdims=True) + NORM_EPS) * g


def _silu(t):
    return t * jax.nn.sigmoid(t)


def _softplus(z):
    return jnp.maximum(z, 0.0) + jnp.log(1.0 + jnp.exp(-jnp.abs(z)))


def _matmul(a, b, *, mode, m, n, k, tm, tn, tk, name, a_off=(0, 0), b_off=(0, 0), out_dtype=F32):
    nk = k // tk
    if mode == "tn":
        a_spec = _bs((tk, tm), lambda i, j, kk: (kk + a_off[0], i + a_off[1]))
        dims = ((0,), (0,))
    else:
        a_spec = _bs((tm, tk), lambda i, j, kk: (i + a_off[0], kk + a_off[1]))
        dims = ((1,), (1,)) if mode == "nt" else ((1,), (0,))
    if mode == "nt":
        b_spec = _bs((tn, tk), lambda i, j, kk: (j + b_off[0], kk + b_off[1]))
    else:
        b_spec = _bs((tk, tn), lambda i, j, kk: (kk + b_off[0], j + b_off[1]))

    def body(a_ref, b_ref, o_ref, acc):
        kk = pl.program_id(2)

        @pl.when(kk == 0)
        def _():
            acc[...] = jnp.zeros_like(acc)

        acc[...] += _dot(a_ref[...], b_ref[...], dims)

        @pl.when(kk == nk - 1)
        def _():
            o_ref[...] = acc[...].astype(out_dtype)

    return pl.pallas_call(
        body, name=name, grid=(m // tm, n // tn, nk),
        in_specs=[a_spec, b_spec], out_specs=_bs((tm, tn), lambda i, j, kk: (i, j)),
        out_shape=jax.ShapeDtypeStruct((m, n), out_dtype),
        scratch_shapes=[pltpu.VMEM((tm, tn), F32)],
        compiler_params=_params(("parallel", "parallel", "arbitrary")),
    )(a, b)


def _proj_fwd(x, g, w):
    s = x.shape[0]
    tm, tn = min(512, s), 896

    def body(x_ref, g_ref, w_ref, o_ref, h_ref, hs):
        @pl.when(pl.program_id(1) == 0)
        def _():
            h = _rms(x_ref[...], g_ref[...]).astype(BF16)
            hs[...] = h
            h_ref[...] = h

        o_ref[...] = jnp.dot(hs[...], w_ref[...], preferred_element_type=F32)

    return pl.pallas_call(
        body, name="proj_fwd", grid=(s // tm, NIN // tn),
        in_specs=[_bs((tm, D), lambda i, j: (i, 0)), _bs((1, D), lambda i, j: (0, 0)), _bs((D, tn), lambda i, j: (0, j))],
        out_specs=[_bs((tm, tn), lambda i, j: (i, j)), _bs((tm, D), lambda i, j: (i, 0))],
        out_shape=[jax.ShapeDtypeStruct((s, NIN), F32), jax.ShapeDtypeStruct((s, D), BF16)],
        scratch_shapes=[pltpu.VMEM((tm, D), BF16)],
        compiler_params=_params(("parallel", "arbitrary")),
    )(x, g, w)


def _rope(t, cos, sin, rot):
    return t * cos + _hdot(t, rot) * sin


def _attn_tile(qs, ks, vs, gs, sinks, gq, gk, cq, sq, ck, sk, mask, rot):
    outs = []
    for hk in range(AKV):
        kh = _rope(_rms(ks[hk], gk), ck, sk, rot)
        for g in range(AH // AKV):
            h = hk * (AH // AKV) + g
            qh = _rope(_rms(qs[h], gq), cq, sq, rot)
            sc = _mm_nt(qh, kh) * (HD ** -0.5)
            sc = jnp.where(mask, sc, NEG_INF)
            mx = lax.stop_gradient(jnp.maximum(jnp.max(sc, axis=-1, keepdims=True), sinks[h]))
            p = jnp.exp(sc - mx)
            den = jnp.sum(p, axis=-1, keepdims=True) + jnp.exp(sinks[h] - mx)
            o = _mm_nn(p / den, vs[hk])
            outs.append(o * _silu(gs[h]))
    return outs


def _attn_load(n, s, aq_ref, ak_ref, av_ref, ag_refs, cos_ref, sin_ref, sink_ref):
    start = pl.multiple_of(jnp.clip((n - 1) * QB, 0, s - WIN), QB)
    q0 = pl.multiple_of(n * QB, QB)
    qs = [aq_ref[:, h * HD:(h + 1) * HD] for h in range(AH)]
    ks = [ak_ref[pl.ds(start, WIN), h * HD:(h + 1) * HD] for h in range(AKV)]
    vs = [av_ref[pl.ds(start, WIN), h * HD:(h + 1) * HD] for h in range(AKV)]
    gs = [ag_refs[h // 4][:, (h % 4) * HD:(h % 4 + 1) * HD] for h in range(AH)]
    sinks = [sink_ref[0:1, h:h + 1] for h in range(AH)]
    cq, sq = cos_ref[pl.ds(q0, QB), :], sin_ref[pl.ds(q0, QB), :]
    ck, sk = cos_ref[pl.ds(start, WIN), :], sin_ref[pl.ds(start, WIN), :]
    qpos = q0 + lax.broadcasted_iota(jnp.int32, (QB, WIN), 0)
    kpos = start + lax.broadcasted_iota(jnp.int32, (QB, WIN), 1)
    mask = jnp.abs(kpos - qpos) <= QB
    return start, qs, ks, vs, gs, sinks, cq, sq, ck, sk, mask


def _attn_specs(s):
    return [
        _bs((QB, 768), lambda n: (n, 0)),
        _bs((s, 256), lambda n: (0, C_AK // 256)),
        _bs((s, 256), lambda n: (0, C_AV // 256)),
        _bs((QB, 256), lambda n: (n, C_AG // 256)),
        _bs((QB, 256), lambda n: (n, C_AG // 256 + 1)),
        _bs((QB, 256), lambda n: (n, C_AG // 256 + 2)),
        _bs((s, HD), lambda n: (0, 0)),
        _bs((s, HD), lambda n: (0, 0)),
        _bs((1, HD), lambda n: (0, 0)),
        _bs((1, HD), lambda n: (0, 0)),
        _bs((1, AH), lambda n: (0, 0)),
    ]


def _attn_fwd(proj, cos, sin, gq, gk, sink):
    s = proj.shape[0]

    def body(aq_ref, ak_ref, av_ref, ag0, ag1, ag2, cos_ref, sin_ref, gq_ref, gk_ref, sink_ref, o_ref):
        n = pl.program_id(0)
        _, qs, ks, vs, gs, sinks, cq, sq, ck, sk, mask = _attn_load(
            n, s, aq_ref, ak_ref, av_ref, (ag0, ag1, ag2), cos_ref, sin_ref, sink_ref)
        outs = _attn_tile(qs, ks, vs, gs, sinks, gq_ref[...], gk_ref[...], cq, sq, ck, sk, mask, _rot_matrix())
        for h in range(AH):
            o_ref[:, h * HD:(h + 1) * HD] = outs[h]

    return pl.pallas_call(
        body, name="attn_fwd", grid=(s // QB,),
        in_specs=_attn_specs(s), out_specs=_bs((QB, 768), lambda n: (n, 0)),
        out_shape=jax.ShapeDtypeStruct((s, 768), F32),
        compiler_params=_params(("arbitrary",)),
    )(proj, proj, proj, proj, proj, proj, cos, sin, gq, gk, sink)


def _attn_bwd(proj, cos, sin, gq, gk, sink, dy):
    s = proj.shape[0]

    def body(aq_ref, ak_ref, av_ref, ag0, ag1, ag2, cos_ref, sin_ref, gq_ref, gk_ref, sink_ref, dy_ref,
             daq_ref, dak_ref, dav_ref, dag_ref, dgq_ref, dgk_ref, dsink_ref):
        n = pl.program_id(0)

        @pl.when(n == 0)
        def _():
            dak_ref[...] = jnp.zeros_like(dak_ref)
            dav_ref[...] = jnp.zeros_like(dav_ref)
            dgq_ref[...] = jnp.zeros_like(dgq_ref)
            dgk_ref[...] = jnp.zeros_like(dgk_ref)
            dsink_ref[...] = jnp.zeros_like(dsink_ref)

        start, qs, ks, vs, gs, sinks, cq, sq, ck, sk, mask = _attn_load(
            n, s, aq_ref, ak_ref, av_ref, (ag0, ag1, ag2), cos_ref, sin_ref, sink_ref)
        rot = _rot_matrix()

        def f(qs, ks, vs, gs, sinks, gq, gk):
            return _attn_tile(qs, ks, vs, gs, sinks, gq, gk, cq, sq, ck, sk, mask, rot)

        _, vjp = jax.vjp(f, qs, ks, vs, gs, sinks, gq_ref[...], gk_ref[...])
        dys = [dy_ref[:, h * HD:(h + 1) * HD] for h in range(AH)]
        dqs, dks, dvs, dgs, dsinks, dgq, dgk = vjp(dys)
        for h in range(AH):
            daq_ref[:, h * HD:(h + 1) * HD] = dqs[h]
            dag_ref[:, h * HD:(h + 1) * HD] = dgs[h]
            dsink_ref[0:1, h:h + 1] += dsinks[h]
        for h in range(AKV):
            dak_ref[pl.ds(start, WIN), h * HD:(h + 1) * HD] += dks[h]
            dav_ref[pl.ds(start, WIN), h * HD:(h + 1) * HD] += dvs[h]
        dgq_ref[...] += dgq
        dgk_ref[...] += dgk

    whole = lambda shape: _bs(shape, lambda n: (0, 0))
    return pl.pallas_call(
        body, name="attn_bwd", grid=(s // QB,),
        in_specs=_attn_specs(s) + [_bs((QB, 768), lambda n: (n, 0))],
        out_specs=[_bs((QB, 768), lambda n: (n, 0)), whole((s, 256)), whole((s, 256)), _bs((QB, 768), lambda n: (n, 0)),
                   whole((1, HD)), whole((1, HD)), whole((1, AH))],
        out_shape=[jax.ShapeDtypeStruct((s, 768), F32), jax.ShapeDtypeStruct((s, 256), F32),
                   jax.ShapeDtypeStruct((s, 256), F32), jax.ShapeDtypeStruct((s, 768), F32),
                   jax.ShapeDtypeStruct((1, HD), F32), jax.ShapeDtypeStruct((1, HD), F32),
                   jax.ShapeDtypeStruct((1, AH), F32)],
        compiler_params=_params(("arbitrary",)),
    )(proj, proj, proj, proj, proj, proj, cos, sin, gq, gk, sink, dy)


def _mem_kv(mem, g, w):
    def body(m_ref, g_ref, w_ref, o_ref, mn_ref):
        mn = _rms(m_ref[...], g_ref[...]).astype(BF16)
        mn_ref[...] = mn
        o_ref[...] = jnp.dot(mn, w_ref[...], preferred_element_type=F32)

    return pl.pallas_call(
        body, name="mem_kv",
        out_shape=[jax.ShapeDtypeStruct((NMEM, 2 * XW), F32), jax.ShapeDtypeStruct((NMEM, D), BF16)],
        compiler_params=_params(),
    )(mem, g, w)


def _xattn_tile(qs, gs, kms, vms, gxq, gxk):
    outs = []
    for h in range(XH):
        q = _rms(qs[h], gxq)
        km = _rms(kms[h], gxk)
        sc = _mm_nt(q, km) * (XD ** -0.5)
        mx = lax.stop_gradient(jnp.max(sc, axis=-1, keepdims=True))
        p = jnp.exp(sc - mx)
        p = p / jnp.sum(p, axis=-1, keepdims=True)
        outs.append(_mm_nn(p, vms[h]) * _silu(gs[h]))
    return outs


XT = 256


def _xattn_specs():
    return [
        _bs((XT, 256), lambda i: (i, C_XQ // 256)), _bs((XT, 256), lambda i: (i, C_XQ // 256 + 1)),
        _bs((XT, 256), lambda i: (i, C_XG // 256)), _bs((XT, 256), lambda i: (i, C_XG // 256 + 1)),
        _bs((NMEM, 2 * XW), lambda i: (0, 0)),
        _bs((1, XD), lambda i: (0, 0)), _bs((1, XD), lambda i: (0, 0)),
    ]


def _xattn_load(q0, q1, g0, g1, mkv_ref):
    qs = [(q0, q1)[h // 2][:, (h % 2) * XD:(h % 2 + 1) * XD] for h in range(XH)]
    gs = [(g0, g1)[h // 2][:, (h % 2) * XD:(h % 2 + 1) * XD] for h in range(XH)]
    kms = [mkv_ref[:, h * XD:(h + 1) * XD] for h in range(XH)]
    vms = [mkv_ref[:, XW + h * XD:XW + (h + 1) * XD] for h in range(XH)]
    return qs, gs, kms, vms


def _xattn_fwd(proj, mkv, gxq, gxk):
    s = proj.shape[0]

    def body(q0, q1, g0, g1, mkv_ref, gxq_ref, gxk_ref, o_ref):
        qs, gs, kms, vms = _xattn_load(q0, q1, g0, g1, mkv_ref)
        outs = _xattn_tile(qs, gs, kms, vms, gxq_ref[...], gxk_ref[...])
        for h in range(XH):
            o_ref[:, h * XD:(h + 1) * XD] = outs[h]

    return pl.pallas_call(
        body, name="xattn_fwd", grid=(s // XT,),
        in_specs=_xattn_specs(), out_specs=_bs((XT, XW), lambda i: (i, 0)),
        out_shape=jax.ShapeDtypeStruct((s, XW), F32),
        compiler_params=_params(("arbitrary",)),
    )(proj, proj, proj, proj, mkv, gxq, gxk)


def _xattn_bwd(proj, mkv, gxq, gxk, dy):
    s = proj.shape[0]

    def body(q0, q1, g0, g1, mkv_ref, gxq_ref, gxk_ref, dy_ref, dq_ref, dg_ref, dmkv_ref, dgxq_ref, dgxk_ref):
        @pl.when(pl.program_id(0) == 0)
        def _():
            dmkv_ref[...] = jnp.zeros_like(dmkv_ref)
            dgxq_ref[...] = jnp.zeros_like(dgxq_ref)
            dgxk_ref[...] = jnp.zeros_like(dgxk_ref)

        qs, gs, kms, vms = _xattn_load(q0, q1, g0, g1, mkv_ref)
        _, vjp = jax.vjp(_xattn_tile, qs, gs, kms, vms, gxq_ref[...], gxk_ref[...])
        dqs, dgs, dkms, dvms, dgxq, dgxk = vjp([dy_ref[:, h * XD:(h + 1) * XD] for h in range(XH)])
        for h in range(XH):
            dq_ref[:, h * XD:(h + 1) * XD] = dqs[h]
            dg_ref[:, h * XD:(h + 1) * XD] = dgs[h]
            dmkv_ref[:, h * XD:(h + 1) * XD] += dkms[h]
            dmkv_ref[:, XW + h * XD:XW + (h + 1) * XD] += dvms[h]
        dgxq_ref[...] += dgxq
        dgxk_ref[...] += dgxk

    whole = lambda shape: _bs(shape, lambda i: (0, 0))
    return pl.pallas_call(
        body, name="xattn_bwd", grid=(s // XT,),
        in_specs=_xattn_specs() + [_bs((XT, XW), lambda i: (i, 0))],
        out_specs=[_bs((XT, XW), lambda i: (i, 0)), _bs((XT, XW), lambda i: (i, 0)), whole((NMEM, 2 * XW)),
                   whole((1, XD)), whole((1, XD))],
        out_shape=[jax.ShapeDtypeStruct((s, XW), F32), jax.ShapeDtypeStruct((s, XW), F32),
                   jax.ShapeDtypeStruct((NMEM, 2 * XW), F32), jax.ShapeDtypeStruct((1, XD), F32),
                   jax.ShapeDtypeStruct((1, XD), F32)],
        compiler_params=_params(("arbitrary",)),
    )(proj, proj, proj, proj, mkv, gxq, gxk, dy)


def _mem_bwd(mem, g, dmn):
    def body(m_ref, dmn_ref, o_ref):
        m = m_ref[...]
        r = lax.rsqrt(jnp.mean(m * m, axis=-1, keepdims=True) + NORM_EPS)
        o_ref[...] = jnp.sum(dmn_ref[...] * m * r, axis=0, keepdims=True)

    del g
    return pl.pallas_call(body, name="mem_norm_bwd", out_shape=jax.ShapeDtypeStruct((1, D), F32),
                          compiler_params=_params())(mem, dmn)


SHIFT_W = 512


def _shift_rows(p, s):
    row = lax.broadcasted_iota(jnp.int32, p.shape, 0)
    prev = jnp.where(row == 0, 0.0, pltpu.roll(p, 1, 0))
    nxt = jnp.where(row == s - 1, 0.0, pltpu.roll(p, s - 1, 0))
    return prev, nxt


def _shift_fwd(proj, mu):
    s = proj.shape[0]

    def body(p_ref, mu_ref, o_ref):
        p = p_ref[...]
        prev, nxt = _shift_rows(p, s)
        o_ref[...] = p + mu_ref[...] * (0.5 * (prev + nxt) - p)

    return pl.pallas_call(
        body, name="shift_fwd", grid=(RSW // SHIFT_W,),
        in_specs=[_bs((s, SHIFT_W), lambda j: (0, C_RS // SHIFT_W + j)), _bs((1, SHIFT_W), lambda j: (0, j))],
        out_specs=_bs((s, SHIFT_W), lambda j: (0, j)),
        out_shape=jax.ShapeDtypeStruct((s, RSW), F32),
        compiler_params=_params(("parallel",)),
    )(proj, mu)


def _shift_bwd(proj, mu, dps):
    s = proj.shape[0]

    def body(p_ref, mu_ref, g_ref, o_ref, dmu_ref):
        p, g, mu_v = p_ref[...], g_ref[...], mu_ref[...]
        prev, nxt = _shift_rows(p, s)
        dmu_ref[...] = jnp.sum(g * (0.5 * (prev + nxt) - p), axis=0, keepdims=True)
        mg = mu_v * g
        down, up = _shift_rows(mg, s)
        o_ref[...] = g * (1.0 - mu_v) + 0.5 * (down + up)

    return pl.pallas_call(
        body, name="shift_bwd", grid=(RSW // SHIFT_W,),
        in_specs=[_bs((s, SHIFT_W), lambda j: (0, C_RS // SHIFT_W + j)), _bs((1, SHIFT_W), lambda j: (0, j)),
                  _bs((s, SHIFT_W), lambda j: (0, j))],
        out_specs=[_bs((s, SHIFT_W), lambda j: (0, j)), _bs((1, SHIFT_W), lambda j: (0, j))],
        out_shape=[jax.ShapeDtypeStruct((s, RSW), F32), jax.ShapeDtypeStruct((1, RSW), F32)],
        compiler_params=_params(("parallel",)),
    )(proj, mu, dps)


def _pre_tile(k, wf, wb, af, ab, k_k, k_a, w0s, w2s, a0s, a2s, seg):
    kx = k * k_k
    ss = _hdot(kx * kx, seg)
    kk = kx / jnp.maximum(jnp.sqrt(ss), 1e-12)
    outs = [kk]
    for d, (w_in, a_in) in enumerate(((wf, af), (wb, ab))):
        z = w0s[d] + _mm_nn(jnp.tanh(w_in), w2s[d])
        wd = -_softplus(-z) - 0.5
        dec = jnp.exp(-jnp.exp(wd))
        ad = jax.nn.sigmoid(a0s[d] + _mm_nn(a_in, a2s[d]))
        kd = k * (1.0 + (ad - 1.0) * k_a)
        outs += [dec, kd, kk * ad]
    return outs


PT = 256


def _pre_load(ps_ref, kk_ref, ka_ref, w0_ref, w2_ref, a0_ref, a2_ref):
    k = ps_ref[:, RW:2 * RW]
    wf, wb = ps_ref[:, 3 * RW:3 * RW + 64], ps_ref[:, 3 * RW + 64:3 * RW + 128]
    af, ab = ps_ref[:, 3 * RW + 128:3 * RW + 192], ps_ref[:, 3 * RW + 192:3 * RW + 256]
    w0s = [w0_ref[0:1, :], w0_ref[1:2, :]]
    a0s = [a0_ref[0:1, :], a0_ref[1:2, :]]
    w2s = [w2_ref[0], w2_ref[1]]
    a2s = [a2_ref[0], a2_ref[1]]
    return (k, wf, wb, af, ab, kk_ref[...], ka_ref[...], w0s, w2s, a0s, a2s)


def _pre_specs():
    c = lambda shape: _bs(shape, lambda i: tuple(0 for _ in shape))
    return [_bs((PT, RSW), lambda i: (i, 0)), c((1, RW)), c((1, RW)), c((2, RW)), c((2, 64, RW)), c((2, RW)),
            c((2, 64, RW))]


def _pre_fwd(ps, k_k, k_a, w0, w2, a0, a2):
    s = ps.shape[0]

    def body(ps_ref, kk_ref, ka_ref, w0_ref, w2_ref, a0_ref, a2_ref, *outs):
        args = _pre_load(ps_ref, kk_ref, ka_ref, w0_ref, w2_ref, a0_ref, a2_ref)
        res = _pre_tile(*args, _seg_matrix(RW, HD))
        for o_ref, v in zip(outs, res):
            o_ref[...] = v

    return pl.pallas_call(
        body, name="rwkv_pre_fwd", grid=(s // PT,),
        in_specs=_pre_specs(), out_specs=[_bs((PT, RW), lambda i: (i, 0))] * 7,
        out_shape=[jax.ShapeDtypeStruct((s, RW), F32)] * 7,
        compiler_params=_params(("parallel",)),
    )(ps, k_k, k_a, w0, w2, a0, a2)


def _pre_bwd(ps, k_k, k_a, w0, w2, a0, a2, dr, dv, cts):
    s = ps.shape[0]

    def body(ps_ref, kk_ref, ka_ref, w0_ref, w2_ref, a0_ref, a2_ref, dr_ref, dv_ref, c0, c1, c2, c3, c4, c5, c6,
             dps_ref, dkk_ref, dka_ref, dw0_ref, dw2_ref, da0_ref, da2_ref):
        @pl.when(pl.program_id(0) == 0)
        def _():
            for r in (dkk_ref, dka_ref, dw0_ref, dw2_ref, da0_ref, da2_ref):
                r[...] = jnp.zeros_like(r)

        args = _pre_load(ps_ref, kk_ref, ka_ref, w0_ref, w2_ref, a0_ref, a2_ref)
        seg = _seg_matrix(RW, HD)
        _, vjp = jax.vjp(lambda *a: _pre_tile(*a, seg), *args)
        dk, dwf, dwb, daf, dab, dk_k, dk_a, dw0s, dw2s, da0s, da2s = vjp([c[...] for c in (c0, c1, c2, c3, c4, c5, c6)])
        dps_ref[:, 0:RW] = dr_ref[...]
        dps_ref[:, RW:2 * RW] = dk
        dps_ref[:, 2 * RW:3 * RW] = dv_ref[...]
        for j, t in enumerate((dwf, dwb, daf, dab)):
            dps_ref[:, 3 * RW + 64 * j:3 * RW + 64 * (j + 1)] = t
        dkk_ref[...] += dk_k
        dka_ref[...] += dk_a
        for d in range(2):
            dw0_ref[d:d + 1, :] += dw0s[d]
            da0_ref[d:d + 1, :] += da0s[d]
            dw2_ref[d] += dw2s[d]
            da2_ref[d] += da2s[d]

    c = lambda shape: _bs(shape, lambda i: tuple(0 for _ in shape))
    row = _bs((PT, RW), lambda i: (i, 0))
    return pl.pallas_call(
        body, name="rwkv_pre_bwd", grid=(s // PT,),
        in_specs=_pre_specs() + [row] * 9,
        out_specs=[_bs((PT, RSW), lambda i: (i, 0)), c((1, RW)), c((1, RW)), c((2, RW)), c((2, 64, RW)), c((2, RW)),
                   c((2, 64, RW))],
        out_shape=[jax.ShapeDtypeStruct((s, RSW), F32), jax.ShapeDtypeStruct((1, RW), F32),
                   jax.ShapeDtypeStruct((1, RW), F32), jax.ShapeDtypeStruct((2, RW), F32),
                   jax.ShapeDtypeStruct((2, 64, RW), F32), jax.ShapeDtypeStruct((2, RW), F32),
                   jax.ShapeDtypeStruct((2, 64, RW), F32)],
        compiler_params=_params(("arbitrary",)),
    )(ps, k_k, k_a, w0, w2, a0, a2, dr, dv, *cts)


def _post_tile(y0, y1, r, v, kd0, kd1, rg, r_k, ln_w, ln_b, seg):
    ysum = y0 + y1
    bonus = (_hdot(r * kd0 * r_k, seg) + _hdot(r * kd1 * r_k, seg)) * v
    mean = _hdot(ysum, seg) * (1.0 / HD)
    cen = ysum - mean
    var = _hdot(cen * cen, seg) * (1.0 / HD)
    y = cen * lax.rsqrt(var + GN_EPS) * ln_w + ln_b + bonus
    return y * _silu(rg)


def _post_specs():
    row = _bs((PT, RW), lambda i: (i, 0))
    c = _bs((1, RW), lambda i: (0, 0))
    return [row, row, _bs((PT, RW), lambda i: (i, 0)), _bs((PT, RW), lambda i: (i, 2)), row, row,
            _bs((PT, RW), lambda i: (i, C_RG // RW)), c, c, c]


def _post_fwd(y0, y1, ps, kd0, kd1, proj, r_k, ln_w, ln_b):
    s = ps.shape[0]

    def body(y0_ref, y1_ref, r_ref, v_ref, kd0_ref, kd1_ref, rg_ref, rk_ref, lw_ref, lb_ref, o_ref):
        o_ref[...] = _post_tile(y0_ref[...], y1_ref[...], r_ref[...], v_ref[...], kd0_ref[...], kd1_ref[...],
                                rg_ref[...], rk_ref[...], lw_ref[...], lb_ref[...], _seg_matrix(RW, HD))

    return pl.pallas_call(
        body, name="rwkv_post_fwd", grid=(s // PT,),
        in_specs=_post_specs(), out_specs=_bs((PT, RW), lambda i: (i, 0)),
        out_shape=jax.ShapeDtypeStruct((s, RW), F32),
        compiler_params=_params(("parallel",)),
    )(y0, y1, ps, ps, kd0, kd1, proj, r_k, ln_w, ln_b)


def _post_bwd(y0, y1, ps, kd0, kd1, proj, r_k, ln_w, ln_b, dy):
    s = ps.shape[0]

    def body(y0_ref, y1_ref, r_ref, v_ref, kd0_ref, kd1_ref, rg_ref, rk_ref, lw_ref, lb_ref, dy_ref,
             dys_ref, dr_ref, dv_ref, dkd0_ref, dkd1_ref, drg_ref, drk_ref, dlw_ref, dlb_ref):
        @pl.when(pl.program_id(0) == 0)
        def _():
            for r in (drk_ref, dlw_ref, dlb_ref):
                r[...] = jnp.zeros_like(r)

        seg = _seg_matrix(RW, HD)
        args = [t[...] for t in (y0_ref, y1_ref, r_ref, v_ref, kd0_ref, kd1_ref, rg_ref, rk_ref, lw_ref, lb_ref)]
        _, vjp = jax.vjp(lambda *a: _post_tile(*a, seg), *args)
        dy0, _, dr, dv, dkd0, dkd1, drg, drk, dlw, dlb = vjp(dy_ref[...])
        dys_ref[...] = dy0
        dr_ref[...] = dr
        dv_ref[...] = dv
        dkd0_ref[...] = dkd0
        dkd1_ref[...] = dkd1
        drg_ref[...] = drg
        drk_ref[...] += drk
        dlw_ref[...] += dlw
        dlb_ref[...] += dlb

    row = _bs((PT, RW), lambda i: (i, 0))
    c = _bs((1, RW), lambda i: (0, 0))
    return pl.pallas_call(
        body, name="rwkv_post_bwd", grid=(s // PT,),
        in_specs=_post_specs() + [row], out_specs=[row] * 6 + [c] * 3,
        out_shape=[jax.ShapeDtypeStruct((s, RW), F32)] * 6 + [jax.ShapeDtypeStruct((1, RW), F32)] * 3,
        compiler_params=_params(("arbitrary",)),
    )(y0, y1, ps, ps, kd0, kd1, proj, r_k, ln_w, ln_b, dy)


def _ones2():
    r = lax.broadcasted_iota(jnp.int32, (256, 128), 0) % 128 // HD
    c = lax.broadcasted_iota(jnp.int32, (256, 128), 1) // HD
    return (r == c).astype(BF16)


def _split(p):
    hi = p.astype(BF16)
    lo = (p - hi.astype(F32)).astype(BF16)
    return jnp.concatenate([hi, lo], axis=1)


def _to_t8(a):
    s = a.shape[0]
    t = a.reshape(s // 8, 8, NPAIR, 2, HD).transpose(0, 2, 4, 3, 1)
    t = jnp.pad(t, ((0, 0), (0, 0), (0, 0), (0, 0), (0, HD - 8))).reshape(s // 8, NPAIR, HD, 128)
    hi = t.astype(BF16)
    lo = (t - hi.astype(F32)).astype(BF16)
    return jnp.concatenate([hi, lo], axis=-1)


def _from_t8(t8):
    g = t8.shape[0]
    t = t8.reshape(g, NPAIR, HD, 2, HD)[..., :8]
    return t.transpose(0, 4, 1, 3, 2).reshape(g * 8, RW)


def _scan_specs(direction, nc, fwd_order):
    def tb(c):
        sc = c if fwd_order else nc - 1 - c
        return sc if direction == 0 else nc - 1 - sc

    row = _bs((TC, RW), lambda c: (tb(c), 0))
    rowv = _bs((TC, RW), lambda c: (tb(c), 2))
    return row, rowv


def _put_t8(ref, g, u, tiles):
    for p in range(NPAIR):
        ref[g, p, :, u:u + 1] = tiles[p][:, u:u + 1]
        ref[g, p, :, HD + u:HD + u + 1] = tiles[p][:, HD + u:HD + u + 1]


def _scan_fwd(dec, kd, b, ps, kk, vl, direction):
    s = dec.shape[0]
    nc, ng = s // TC, TC // 8
    row, t8_in, t8_out = _scan_specs(direction, nc, True)
    n = NPAIR * HD

    def body(dec_ref, kd_ref, b_ref, r_ref, kk_ref, vl_ref, y8_ref, ck_ref, st):
        @pl.when(pl.program_id(0) == 0)
        def _():
            st[...] = jnp.zeros_like(st)

        ck_ref[0] = st[...]
        ones2 = _ones2()
        lane_u = lax.broadcasted_iota(jnp.int32, (HD, 256), 1) % HD
        tiles = lambda res, k: [res[k * n + p * HD:k * n + (p + 1) * HD] for p in range(NPAIR)]

        def group(gi, carry):
            g = gi if direction == 0 else ng - 1 - gi
            rows8 = pl.ds(pl.multiple_of(g * 8, 8), 8)
            d8, k8, b8, r8, kk8 = (q[rows8, :] for q in (dec_ref, kd_ref, b_ref, r_ref, kk_ref))
            pc = [slice(p * 128, (p + 1) * 128) for p in range(NPAIR)]
            ss = [st[p] for p in range(NPAIR)]
            u_prev = None
            for ui in range(8):
                u = ui if direction == 0 else 7 - ui
                lhs = [_split(ss[p] * kk8[u:u + 1, pc[p]]) for p in range(NPAIR)]
                for p in range(NPAIR):
                    vt = vl_ref[g, p]
                    lhs.append(jnp.where(lane_u == u, vt, jnp.zeros_like(vt)))
                if u_prev is not None:
                    lhs += [_split(ss[p] * r8[u_prev:u_prev + 1, pc[p]]) for p in range(NPAIR)]
                res = jnp.dot(jnp.concatenate(lhs, axis=0), ones2, preferred_element_type=F32)
                if u_prev is not None:
                    _put_t8(y8_ref, g, u_prev, tiles(res, 2))
                sa, vb = tiles(res, 0), tiles(res, 1)
                for p in range(NPAIR):
                    ss[p] = ss[p] * d8[u:u + 1, pc[p]] - sa[p] * b8[u:u + 1, pc[p]] + vb[p] * k8[u:u + 1, pc[p]]
                u_prev = u
            lhs = [_split(ss[p] * r8[u_prev:u_prev + 1, pc[p]]) for p in range(NPAIR)]
            res = jnp.dot(jnp.concatenate(lhs, axis=0), ones2, preferred_element_type=F32)
            _put_t8(y8_ref, g, u_prev, tiles(res, 0))
            for p in range(NPAIR):
                st[p] = ss[p]
            return carry

        lax.fori_loop(0, ng, group, 0)

    return pl.pallas_call(
        body, name=f"rwkv_scan_fwd{direction}", grid=(nc,),
        in_specs=[row, row, row, row, row, t8_in],
        out_specs=[t8_out, _bs((1, NPAIR, HD, 128), lambda c: (c, 0, 0, 0))],
        out_shape=[jax.ShapeDtypeStruct((s // 8, NPAIR, HD, 128), F32),
                   jax.ShapeDtypeStruct((nc, NPAIR, HD, 128), F32)],
        scratch_shapes=[pltpu.VMEM((NPAIR, HD, 128), F32)],
        compiler_params=_params(("arbitrary",)),
    )(dec, kd, b, ps, kk, vl)


def _scan_bwd(dec, kd, b, ps, kk, vl, dyl, ck, direction):
    s = dec.shape[0]
    nc, ng = s // TC, TC // 8
    row, t8_in, t8_out = _scan_specs(direction, nc, False)
    n = NPAIR * HD

    def body(dec_ref, kd_ref, b_ref, r_ref, kk_ref, vl_ref, dyl_ref, ck_ref,
             dr_ref, dd_ref, db_ref, dk_ref, dkk_ref, dv8_ref, st, sa_s, vb_s, dy_s, ds):
        @pl.when(pl.program_id(0) == 0)
        def _():
            ds[...] = jnp.zeros_like(ds)

        st[0] = ck_ref[0]
        ones2 = _ones2()
        lane_u = lax.broadcasted_iota(jnp.int32, (HD, 256), 1) % HD
        row_id = lax.broadcasted_iota(jnp.int32, (8, 128), 0)
        pc = [slice(p * 128, (p + 1) * 128) for p in range(NPAIR)]
        tiles = lambda res, k: [res[k * n + p * HD:k * n + (p + 1) * HD] for p in range(NPAIR)]

        def fgroup(gi, carry):
            g = gi if direction == 0 else ng - 1 - gi
            rows8 = pl.ds(pl.multiple_of(g * 8, 8), 8)
            d8, k8, b8, kk8 = (q[rows8, :] for q in (dec_ref, kd_ref, b_ref, kk_ref))
            ss = [st[gi * 8, p] for p in range(NPAIR)]
            for ui in range(8):
                u = ui if direction == 0 else 7 - ui
                i = gi * 8 + ui
                lhs = [_split(ss[p] * kk8[u:u + 1, pc[p]]) for p in range(NPAIR)]
                for ref in (vl_ref, dyl_ref):
                    for p in range(NPAIR):
                        t = ref[g, p]
                        lhs.append(jnp.where(lane_u == u, t, jnp.zeros_like(t)))
                res = jnp.dot(jnp.concatenate(lhs, axis=0), ones2, preferred_element_type=F32)
                sa, vb, dyb = tiles(res, 0), tiles(res, 1), tiles(res, 2)
                for p in range(NPAIR):
                    sa_s[i, p] = sa[p]
                    vb_s[i, p] = vb[p]
                    dy_s[i, p] = dyb[p]
                    ss[p] = ss[p] * d8[u:u + 1, pc[p]] - sa[p] * b8[u:u + 1, pc[p]] + vb[p] * k8[u:u + 1, pc[p]]
                    st[i + 1, p] = ss[p]
            return carry

        lax.fori_loop(0, ng, fgroup, 0)

        def bgroup(gj, carry):
            gi = ng - 1 - gj
            g = gi if direction == 0 else ng - 1 - gi
            rows8 = pl.ds(pl.multiple_of(g * 8, 8), 8)
            d8, k8, b8, r8, kk8 = (q[rows8, :] for q in (dec_ref, kd_ref, b_ref, r_ref, kk_ref))
            dss = [ds[p] for p in range(NPAIR)]
            acc = [[jnp.zeros((8, 128), F32) for _ in range(5)] for _ in range(NPAIR)]
            for uj in range(8):
                ui = 7 - uj
                u = ui if direction == 0 else 7 - ui
                i = gi * 8 + ui
                dyb = [dy_s[i, p] for p in range(NPAIR)]
                for p in range(NPAIR):
                    dss[p] = dss[p] + dyb[p] * r8[u:u + 1, pc[p]]
                lhs = [_split(dss[p] * b8[u:u + 1, pc[p]]) for p in range(NPAIR)]
                lhs += [_split(dss[p] * k8[u:u + 1, pc[p]]) for p in range(NPAIR)]
                res = jnp.dot(jnp.concatenate(lhs, axis=0), ones2, preferred_element_type=F32)
                dsa, dvb = tiles(res, 0), tiles(res, 1)
                _put_t8(dv8_ref, g, u, dvb)
                for p in range(NPAIR):
                    sp, sn = st[i, p], st[i + 1, p]
                    outs = (jnp.sum(sn * dyb[p], axis=0, keepdims=True), jnp.sum(dss[p] * sp, axis=0, keepdims=True),
                            -jnp.sum(dss[p] * sa_s[i, p], axis=0, keepdims=True),
                            jnp.sum(dss[p] * vb_s[i, p], axis=0, keepdims=True),
                            -jnp.sum(sp * dsa[p], axis=0, keepdims=True))
                    acc[p] = [jnp.where(row_id == u, o, a_) for o, a_ in zip(outs, acc[p])]
                    dss[p] = dss[p] * d8[u:u + 1, pc[p]] - dsa[p] * kk8[u:u + 1, pc[p]]
            for p in range(NPAIR):
                ds[p] = dss[p]
                for o_ref, a_ in zip((dr_ref, dd_ref, db_ref, dk_ref, dkk_ref), acc[p]):
                    o_ref[rows8, pc[p]] = a_
            return carry

        lax.fori_loop(0, ng, bgroup, 0)

    chunk = lambda k: pltpu.VMEM((k, NPAIR, HD, 128), F32)
    return pl.pallas_call(
        body, name=f"rwkv_scan_bwd{direction}", grid=(nc,),
        in_specs=[row, row, row, row, row, t8_in, t8_in, _bs((1, NPAIR, HD, 128), lambda c: (nc - 1 - c, 0, 0, 0))],
        out_specs=[row] * 5 + [t8_out],
        out_shape=[jax.ShapeDtypeStruct((s, RW), F32)] * 5 + [jax.ShapeDtypeStruct((s // 8, NPAIR, HD, 128), F32)],
        scratch_shapes=[chunk(TC + 1), chunk(TC), chunk(TC), chunk(TC), pltpu.VMEM((NPAIR, HD, 128), F32)],
        compiler_params=_params(("arbitrary",)),
    )(dec, kd, b, ps, kk, vl, dyl, ck)


def _tiles(res, k):
    n = NPAIR * HD
    return [res[k * n + p * HD:k * n + (p + 1) * HD] for p in range(NPAIR)]


def _rows_to_tiles(src_ref, rows8, stage, out_s, base):
    for p in range(NPAIR):
        stage[base + p, 0:8, 0:HD] = src_ref[rows8, p * 128:p * 128 + HD]
        stage[base + p, HD:HD + 8, 0:HD] = src_ref[rows8, p * 128 + HD:(p + 1) * 128]
        out_s[base + p] = stage[base + p].T[0:HD].astype(BF16)


def _tiles_to_rows(tile_s, base, dst_ref, rows8):
    for p in range(NPAIR):
        t = jnp.concatenate([tile_s[base + p], jnp.zeros((HD, 128), F32)], axis=0).T
        dst_ref[rows8, p * 128:p * 128 + HD] = t[0:8, 0:HD]
        dst_ref[rows8, p * 128 + HD:(p + 1) * 128] = t[HD:HD + 8, 0:HD]


def _put_cols(tile_s, base, u, tiles):
    for p in range(NPAIR):
        tile_s[base + p, :, u:u + 1] = tiles[p][:, u:u + 1]
        tile_s[base + p, :, HD + u:HD + u + 1] = tiles[p][:, HD + u:HD + u + 1]


def _scan2_fwd(per_dir, ps, kk):
    s = ps.shape[0]
    nc, ng = s // TC, TC // 8
    in_specs, operands, out_specs, out_shape = [], [], [], []
    for d in (0, 1):
        row, rowv = _scan_specs(d, nc, True)
        in_specs += [row] * 5 + [rowv]
        operands += list(per_dir[d]) + [ps, kk, ps]
        out_specs += [row, _bs((1, NPAIR, HD, 128), lambda c: (c, 0, 0, 0))]
        out_shape += [jax.ShapeDtypeStruct((s, RW), F32), jax.ShapeDtypeStruct((nc, NPAIR, HD, 128), F32)]

    def body(*refs):
        ins = [refs[0:6], refs[6:12]]
        y_refs, ck_refs = (refs[12], refs[14]), (refs[13], refs[15])
        st, vt_s, yt_s, stage = refs[16:]

        @pl.when(pl.program_id(0) == 0)
        def _():
            st[...] = jnp.zeros_like(st)
            yt_s[...] = jnp.zeros_like(yt_s)
            stage[...] = jnp.zeros_like(stage)

        for d in (0, 1):
            ck_refs[d][0] = st[d * NPAIR:(d + 1) * NPAIR]
        ones2 = _ones2()
        ones1 = ones2[0:128]
        lane_u = lax.broadcasted_iota(jnp.int32, (HD, 128), 1) % HD
        pc = [slice(p * 128, (p + 1) * 128) for p in range(NPAIR)]

        def group(gi, carry):
            gs = (gi, ng - 1 - gi)
            rows8 = [pl.ds(pl.multiple_of(gs[d] * 8, 8), 8) for d in (0, 1)]
            blk = [[q[rows8[d], :] for q in ins[d][:5]] for d in (0, 1)]
            for d in (0, 1):
                _rows_to_tiles(ins[d][5], rows8[d], stage, vt_s, d * NPAIR)
            ss = [[st[d * NPAIR + p] for p in range(NPAIR)] for d in (0, 1)]
            for ui in range(9):
                us, ups = (ui, 7 - ui), (ui - 1, 8 - ui)
                lhs1, where = [], {}
                for d in (0, 1):
                    if ui < 8:
                        where["sa", d] = len(lhs1) // NPAIR
                        lhs1 += [(ss[d][p] * blk[d][4][us[d]:us[d] + 1, pc[p]]).astype(BF16) for p in range(NPAIR)]
                        where["vb", d] = len(lhs1) // NPAIR
                        for p in range(NPAIR):
                            vt = vt_s[d * NPAIR + p]
                            lhs1.append(jnp.where(lane_u == us[d], vt, jnp.zeros_like(vt)))
                    if ui > 0:
                        where["y", d] = len(lhs1) // NPAIR
                        lhs1 += [(ss[d][p] * blk[d][3][ups[d]:ups[d] + 1, pc[p]]).astype(BF16) for p in range(NPAIR)]
                res1 = jnp.dot(jnp.concatenate(lhs1, axis=0), ones1, preferred_element_type=F32)
                for d in (0, 1):
                    d8, k8, b8, _, _ = blk[d]
                    u = us[d]
                    if ui < 8:
                        sa, vb = _tiles(res1, where["sa", d]), _tiles(res1, where["vb", d])
                        for p in range(NPAIR):
                            ss[d][p] = (ss[d][p] * d8[u:u + 1, pc[p]] - sa[p] * b8[u:u + 1, pc[p]]
                                        + vb[p] * k8[u:u + 1, pc[p]])
                    if ui > 0:
                        _put_cols(yt_s, d * NPAIR, ups[d], _tiles(res1, where["y", d]))
            for d in (0, 1):
                _tiles_to_rows(yt_s, d * NPAIR, y_refs[d], rows8[d])
                for p in range(NPAIR):
                    st[d * NPAIR + p] = ss[d][p]
            return carry

        for gi in range(ng):
            group(gi, 0)

    outs = pl.pallas_call(
        body, name="rwkv_scan_fwd", grid=(nc,), in_specs=in_specs, out_specs=out_specs, out_shape=out_shape,
        scratch_shapes=[pltpu.VMEM((2 * NPAIR, HD, 128), F32), pltpu.VMEM((2 * NPAIR, HD, 128), BF16),
                        pltpu.VMEM((2 * NPAIR, HD, 128), F32), pltpu.VMEM((2 * NPAIR, 128, 128), F32)],
        compiler_params=_params(("arbitrary",)),
    )(*operands)
    return [(outs[0], outs[1]), (outs[2], outs[3])]


def _scan2_bwd(per_dir, ps, kk, dy):
    s = ps.shape[0]
    nc, ng = s // TC, TC // 8
    in_specs, operands, out_specs, out_shape = [], [], [], []
    for d in (0, 1):
        row, rowv = _scan_specs(d, nc, False)
        dec, kd, b, ck = per_dir[d]
        in_specs += [row] * 5 + [rowv, row, _bs((1, NPAIR, HD, 128), lambda c: (nc - 1 - c, 0, 0, 0))]
        operands += [dec, kd, b, ps, kk, ps, dy, ck]
        out_specs += [row] * 6
        out_shape += [jax.ShapeDtypeStruct((s, RW), F32)] * 6

    def body(*refs):
        ins = [refs[0:8], refs[8:16]]
        outs = [refs[16:22], refs[22:28]]
        st, sa_s, vb_s, dy_s, ds, vt_s, dyt_s, dvt_s, stage = refs[28:]

        @pl.when(pl.program_id(0) == 0)
        def _():
            dvt_s[...] = jnp.zeros_like(dvt_s)
            stage[...] = jnp.zeros_like(stage)
            ds[...] = jnp.zeros_like(ds)

        for d in (0, 1):
            st[d * (TC + 1)] = ins[d][7][0]
        ones2 = _ones2()
        ones1 = ones2[0:128]
        lane_u = lax.broadcasted_iota(jnp.int32, (HD, 128), 1) % HD
        row_id = lax.broadcasted_iota(jnp.int32, (8, 128), 0)
        pc = [slice(p * 128, (p + 1) * 128) for p in range(NPAIR)]

        def load_rows(gs):
            return [[q[pl.ds(pl.multiple_of(gs[d] * 8, 8), 8), :] for q in ins[d][:5]] for d in (0, 1)]

        def fgroup(gi, carry):
            gs = (gi, ng - 1 - gi)
            blk = load_rows(gs)
            for d in (0, 1):
                rows8 = pl.ds(pl.multiple_of(gs[d] * 8, 8), 8)
                _rows_to_tiles(ins[d][5], rows8, stage, vt_s, d * NPAIR)
                _rows_to_tiles(ins[d][6], rows8, stage, dyt_s, d * NPAIR)
            ss = [[st[d * (TC + 1) + gi * 8, p] for p in range(NPAIR)] for d in (0, 1)]
            for ui in range(8):
                us = (ui, 7 - ui)
                i = gi * 8 + ui
                lhs1 = []
                for d in (0, 1):
                    kk8 = blk[d][4]
                    lhs1 += [(ss[d][p] * kk8[us[d]:us[d] + 1, pc[p]]).astype(BF16) for p in range(NPAIR)]
                    for tile_s in (vt_s, dyt_s):
                        for p in range(NPAIR):
                            t = tile_s[d * NPAIR + p]
                            lhs1.append(jnp.where(lane_u == us[d], t, jnp.zeros_like(t)))
                res1 = jnp.dot(jnp.concatenate(lhs1, axis=0), ones1, preferred_element_type=F32)
                for d in (0, 1):
                    d8, k8, b8, _, _ = blk[d]
                    u = us[d]
                    sa, vb, dyb = _tiles(res1, 3 * d), _tiles(res1, 3 * d + 1), _tiles(res1, 3 * d + 2)
                    for p in range(NPAIR):
                        sa_s[d * TC + i, p] = sa[p]
                        vb_s[d * TC + i, p] = vb[p]
                        dy_s[d * TC + i, p] = dyb[p]
                        ss[d][p] = ss[d][p] * d8[u:u + 1, pc[p]] - sa[p] * b8[u:u + 1, pc[p]] + vb[p] * k8[u:u + 1, pc[p]]
                        st[d * (TC + 1) + i + 1, p] = ss[d][p]
            return carry

        for gi in range(ng):
            fgroup(gi, 0)

        def bgroup(gj, carry):
            gi = ng - 1 - gj
            gs = (gi, ng - 1 - gi)
            blk = load_rows(gs)
            dss = [[ds[d * NPAIR + p] for p in range(NPAIR)] for d in (0, 1)]
            acc = [[[jnp.zeros((8, 128), F32) for _ in range(5)] for _ in range(NPAIR)] for _ in (0, 1)]
            for uj in range(8):
                ui = 7 - uj
                us = (ui, 7 - ui)
                i = gi * 8 + ui
                lhs1, dyb = [], [None, None]
                for d in (0, 1):
                    _, k8, b8, r8, _ = blk[d]
                    u = us[d]
                    dyb[d] = [dy_s[d * TC + i, p] for p in range(NPAIR)]
                    for p in range(NPAIR):
                        dss[d][p] = dss[d][p] + dyb[d][p] * r8[u:u + 1, pc[p]]
                    lhs1 += [(dss[d][p] * b8[u:u + 1, pc[p]]).astype(BF16) for p in range(NPAIR)]
                    lhs1 += [(dss[d][p] * k8[u:u + 1, pc[p]]).astype(BF16) for p in range(NPAIR)]
                res1 = jnp.dot(jnp.concatenate(lhs1, axis=0), ones1, preferred_element_type=F32)
                for d in (0, 1):
                    d8, _, _, _, kk8 = blk[d]
                    u = us[d]
                    dsa, dvb = _tiles(res1, 2 * d), _tiles(res1, 2 * d + 1)
                    _put_cols(dvt_s, d * NPAIR, u, dvb)
                    for p in range(NPAIR):
                        sp, sn = st[d * (TC + 1) + i, p], st[d * (TC + 1) + i + 1, p]
                        dsv = dss[d][p]
                        vals = (jnp.sum(sn * dyb[d][p], axis=0, keepdims=True), jnp.sum(dsv * sp, axis=0, keepdims=True),
                                -jnp.sum(dsv * sa_s[d * TC + i, p], axis=0, keepdims=True),
                                jnp.sum(dsv * vb_s[d * TC + i, p], axis=0, keepdims=True),
                                -jnp.sum(sp * dsa[p], axis=0, keepdims=True))
                        acc[d][p] = [jnp.where(row_id == u, o, a_) for o, a_ in zip(vals, acc[d][p])]
                        dss[d][p] = dsv * d8[u:u + 1, pc[p]] - dsa[p] * kk8[u:u + 1, pc[p]]
            for d in (0, 1):
                rows8 = pl.ds(pl.multiple_of(gs[d] * 8, 8), 8)
                _tiles_to_rows(dvt_s, d * NPAIR, outs[d][5], rows8)
                for p in range(NPAIR):
                    ds[d * NPAIR + p] = dss[d][p]
                    for o_ref, a_ in zip(outs[d][:5], acc[d][p]):
                        o_ref[rows8, pc[p]] = a_
            return carry

        for gj in range(ng):
            bgroup(gj, 0)

    chunk = lambda k: pltpu.VMEM((k, NPAIR, HD, 128), F32)
    pairs = lambda w, dt: pltpu.VMEM((2 * NPAIR, HD, w), dt)
    res = pl.pallas_call(
        body, name="rwkv_scan_bwd", grid=(nc,), in_specs=in_specs, out_specs=out_specs, out_shape=out_shape,
        scratch_shapes=[chunk(2 * (TC + 1)), chunk(2 * TC), chunk(2 * TC), chunk(2 * TC), pairs(128, F32),
                        pairs(128, BF16), pairs(128, BF16), pairs(128, F32), pltpu.VMEM((2 * NPAIR, 128, 128), F32)],
        compiler_params=_params(("arbitrary",)),
    )(*operands)
    return [res[0:6], res[6:12]]


MT = 256
MN = 256


def _merge_fwd(ya, yr, yx, wa, wr, wx, proj, gate_b):
    s = ya.shape[0]

    def body(ya_ref, yr_ref, yx_ref, wa_ref, wr_ref, wx_ref, m0, m1, m2, b0, b1, b2, o_ref):
        acc = jnp.zeros((MT, MN), F32)
        for y_ref, w_ref, m_ref, b_ref in ((ya_ref, wa_ref, m0, b0), (yr_ref, wr_ref, m1, b1), (yx_ref, wx_ref, m2, b2)):
            u = _dot(y_ref[...], w_ref[...], ((1,), (0,)))
            acc = acc + jax.nn.sigmoid(m_ref[...] + b_ref[...]) * u
        o_ref[...] = acc.astype(BF16)

    mg = lambda br: _bs((MT, MN), lambda i, j: (i, C_MG // MN + br * (D // MN) + j))
    gb = lambda br: _bs((1, MN), lambda i, j: (0, br * (D // MN) + j))
    return pl.pallas_call(
        body, name="merge_fwd", grid=(s // MT, D // MN),
        in_specs=[_bs((MT, RW), lambda i, j: (i, 0)), _bs((MT, RW), lambda i, j: (i, 0)), _bs((MT, XW), lambda i, j: (i, 0)),
                  _bs((RW, MN), lambda i, j: (0, j)), _bs((RW, MN), lambda i, j: (0, j)), _bs((XW, MN), lambda i, j: (0, j)),
                  mg(0), mg(1), mg(2), gb(0), gb(1), gb(2)],
        out_specs=_bs((MT, MN), lambda i, j: (i, j)),
        out_shape=jax.ShapeDtypeStruct((s, D), BF16),
        compiler_params=_params(("parallel", "arbitrary")),
    )(ya, yr, yx, wa, wr, wx, proj, proj, proj, gate_b, gate_b, gate_b)


def _out_fwd(merged, w_out, x, target):
    s = x.shape[0]
    tm, tn = min(512, s), 512

    def body(m_ref, w_ref, x_ref, t_ref, loss_ref, d_ref, d16_ref):
        @pl.when((pl.program_id(0) == 0) & (pl.program_id(1) == 0))
        def _():
            loss_ref[...] = jnp.zeros_like(loss_ref)

        out = x_ref[...] + jnp.dot(m_ref[...], w_ref[...], preferred_element_type=F32)
        err = out - t_ref[...]
        dout = err * (1.0 / D)
        d_ref[...] = dout
        d16_ref[...] = dout.astype(BF16)
        loss_ref[...] += jnp.sum(err * err)

    tile = _bs((tm, tn), lambda i, j: (i, j))
    return pl.pallas_call(
        body, name="out_fwd", grid=(s // tm, D // tn),
        in_specs=[_bs((tm, D), lambda i, j: (i, 0)), _bs((D, tn), lambda i, j: (0, j)), tile, tile],
        out_specs=[_bs((8, 128), lambda i, j: (0, 0)), tile, tile],
        out_shape=[jax.ShapeDtypeStruct((8, 128), F32), jax.ShapeDtypeStruct((s, D), F32),
                   jax.ShapeDtypeStruct((s, D), BF16)],
        compiler_params=_params(("arbitrary", "arbitrary")),
    )(merged, w_out, x, target)


def _merge_bwd(ya, yr, yx, wa, wr, wx, proj, gate_b, dmerged):
    s = ya.shape[0]

    def body(ya_ref, yr_ref, yx_ref, wa_ref, wr_ref, wx_ref, m0, m1, m2, b0, b1, b2, dm_ref,
             dg0, dg1, dg2, du0, du1, du2, dya_ref, dyr_ref, dyx_ref):
        @pl.when(pl.program_id(1) == 0)
        def _():
            dya_ref[...] = jnp.zeros_like(dya_ref)
            dyr_ref[...] = jnp.zeros_like(dyr_ref)
            dyx_ref[...] = jnp.zeros_like(dyx_ref)

        dm = dm_ref[...]
        for y_ref, w_ref, m_ref, b_ref, dg_ref, du_ref, dy_ref in (
                (ya_ref, wa_ref, m0, b0, dg0, du0, dya_ref), (yr_ref, wr_ref, m1, b1, dg1, du1, dyr_ref),
                (yx_ref, wx_ref, m2, b2, dg2, du2, dyx_ref)):
            w = w_ref[...]
            u = _dot(y_ref[...], w, ((1,), (0,)))
            gt = jax.nn.sigmoid(m_ref[...] + b_ref[...])
            dg_ref[...] = (dm * u * gt * (1.0 - gt)).astype(BF16)
            du = (dm * gt).astype(BF16)
            du_ref[...] = du
            dy_ref[...] += _dot(du, w, ((1,), (1,)))

    mg = lambda br: _bs((MT, MN), lambda i, j: (i, C_MG // MN + br * (D // MN) + j))
    gb = lambda br: _bs((1, MN), lambda i, j: (0, br * (D // MN) + j))
    tile = _bs((MT, MN), lambda i, j: (i, j))
    return pl.pallas_call(
        body, name="merge_bwd", grid=(s // MT, D // MN),
        in_specs=[_bs((MT, RW), lambda i, j: (i, 0)), _bs((MT, RW), lambda i, j: (i, 0)), _bs((MT, XW), lambda i, j: (i, 0)),
                  _bs((RW, MN), lambda i, j: (0, j)), _bs((RW, MN), lambda i, j: (0, j)), _bs((XW, MN), lambda i, j: (0, j)),
                  mg(0), mg(1), mg(2), gb(0), gb(1), gb(2), tile],
        out_specs=[tile] * 6 + [_bs((MT, RW), lambda i, j: (i, 0)), _bs((MT, RW), lambda i, j: (i, 0)),
                                _bs((MT, XW), lambda i, j: (i, 0))],
        out_shape=[jax.ShapeDtypeStruct((s, D), BF16)] * 6 + [jax.ShapeDtypeStruct((s, RW), F32),
                                                               jax.ShapeDtypeStruct((s, RW), F32),
                                                               jax.ShapeDtypeStruct((s, XW), F32)],
        compiler_params=_params(("parallel", "arbitrary")),
    )(ya, yr, yx, wa, wr, wx, proj, proj, proj, gate_b, gate_b, gate_b, dmerged)


def _colsum(a, name):
    m, n = a.shape
    tm, tn = min(512, m), 512

    def body(a_ref, o_ref):
        @pl.when(pl.program_id(1) == 0)
        def _():
            o_ref[...] = jnp.zeros_like(o_ref)

        o_ref[...] += jnp.sum(a_ref[...].astype(F32), axis=0, keepdims=True)

    return pl.pallas_call(
        body, name=name, grid=(n // tn, m // tm),
        in_specs=[_bs((tm, tn), lambda j, i: (i, j))], out_specs=_bs((1, tn), lambda j, i: (0, j)),
        out_shape=jax.ShapeDtypeStruct((1, n), F32),
        compiler_params=_params(("parallel", "arbitrary")),
    )(a)


def _in_bwd(dproj, w_in, x, g, dout):
    s = x.shape[0]
    tm, tk = min(512, s), 896
    nk = NIN // tk

    def body(dp_ref, w_ref, x_ref, g_ref, do_ref, gx_ref, gg_ref, acc):
        i, kk = pl.program_id(0), pl.program_id(1)

        @pl.when((i == 0) & (kk == 0))
        def _():
            gg_ref[...] = jnp.zeros_like(gg_ref)

        @pl.when(kk == 0)
        def _():
            acc[...] = jnp.zeros_like(acc)

        acc[...] += _dot(dp_ref[...], w_ref[...], ((1,), (1,)))

        @pl.when(kk == nk - 1)
        def _():
            xv, dh, gv = x_ref[...], acc[...], g_ref[...]
            r = lax.rsqrt(jnp.mean(xv * xv, axis=-1, keepdims=True) + NORM_EPS)
            xn = xv * r
            gg_ref[...] += jnp.sum(dh * xn, axis=0, keepdims=True)
            dxn = dh * gv
            dx = r * (dxn - xn * jnp.mean(dxn * xn, axis=-1, keepdims=True))
            gx_ref[...] = do_ref[...] + dx

    return pl.pallas_call(
        body, name="in_bwd", grid=(s // tm, nk),
        in_specs=[_bs((tm, tk), lambda i, kk: (i, kk)), _bs((D, tk), lambda i, kk: (0, kk)),
                  _bs((tm, D), lambda i, kk: (i, 0)), _bs((1, D), lambda i, kk: (0, 0)), _bs((tm, D), lambda i, kk: (i, 0))],
        out_specs=[_bs((tm, D), lambda i, kk: (i, 0)), _bs((1, D), lambda i, kk: (0, 0))],
        out_shape=[jax.ShapeDtypeStruct((s, D), F32), jax.ShapeDtypeStruct((1, D), F32)],
        scratch_shapes=[pltpu.VMEM((tm, D), F32)],
        compiler_params=_params(("arbitrary", "arbitrary")),
    )(dproj, w_in, x, g, dout)


def _adamw_math(w, g, m, v):
    m = ADAM_B1 * m + (1.0 - ADAM_B1) * g
    v = ADAM_B2 * v + (1.0 - ADAM_B2) * jnp.square(g)
    m_hat = m / (1.0 - ADAM_B1 ** ADAM_STEP)
    v_hat = v / (1.0 - ADAM_B2 ** ADAM_STEP)
    delta = -ADAM_LR * (m_hat / (jnp.sqrt(v_hat) + ADAM_EPS) + ADAM_WD * w)
    return delta, m, v


def _adamw(parts, w, m, v, name):
    rows, cols = w.shape
    tr = rows
    for cand in (256, 128, 64, 32, 16, 8):
        if rows % cand == 0 and cand * cols * 4 <= (1 << 20):
            tr = cand
            break
    n = len(parts)

    def body(*refs):
        g = refs[0][...].astype(F32)
        for r in refs[1:n]:
            g = g + r[...].astype(F32)
        w_ref, m_ref, v_ref, g_out, d_out, m_out, v_out = refs[n:]
        delta, m_new, v_new = _adamw_math(w_ref[...], g, m_ref[...], v_ref[...])
        g_out[...] = g
        d_out[...] = delta
        m_out[...] = m_new
        v_out[...] = v_new

    spec = _bs((tr, cols), lambda i: (i, 0))
    return pl.pallas_call(
        body, name=name, grid=(rows // tr,),
        in_specs=[spec] * (n + 3), out_specs=[spec] * 4,
        out_shape=[jax.ShapeDtypeStruct((rows, cols), F32)] * 4,
        compiler_params=_params(("parallel",)),
    )(*parts, w, m, v)


def _adamw_halves(mine, theirs, core, w, m, v, name):
    rows, cols = w.shape
    h = rows // 2
    tr = next(t for t in (256, 128, 64, 32, 16, 8) if h % t == 0 and t * cols * 4 <= (1 << 20))
    nt = h // tr

    def body(core_ref, mine_ref, theirs_ref, w_ref, m_ref, v_ref, g_out, d_out, m_out, v_out):
        is_mine = pl.program_id(0) // nt == core_ref[0]
        g = jnp.where(is_mine, mine_ref[...], theirs_ref[...])
        delta, m_new, v_new = _adamw_math(w_ref[...], g, m_ref[...], v_ref[...])
        g_out[...] = g
        d_out[...] = delta
        m_out[...] = m_new
        v_out[...] = v_new

    spec = _bs((tr, cols), lambda i, core_ref: (i, 0))
    return pl.pallas_call(
        body, name=name,
        grid_spec=pltpu.PrefetchScalarGridSpec(
            num_scalar_prefetch=1, grid=(2 * nt,),
            in_specs=[_bs((tr, cols), lambda i, core_ref: (jnp.clip(i - core_ref[0] * nt, 0, nt - 1), 0)),
                      _bs((tr, cols), lambda i, core_ref: (jnp.clip(i - (1 - core_ref[0]) * nt, 0, nt - 1), 0)),
                      spec, spec, spec],
            out_specs=[spec] * 4),
        out_shape=[jax.ShapeDtypeStruct((rows, cols), F32)] * 4,
        compiler_params=_params(("parallel",)),
    )(core, mine, theirs, w, m, v)


def _sum_parts(parts, name):
    rows, cols = parts[0].shape
    tr = rows
    for cand in (256, 128, 64, 32, 16, 8):
        if rows % cand == 0 and cand * cols * 4 <= (1 << 20):
            tr = cand
            break

    def body(*refs):
        acc = refs[0][...].astype(F32)
        for r in refs[1:-1]:
            acc = acc + r[...].astype(F32)
        refs[-1][...] = acc

    spec = _bs((tr, cols), lambda i: (i, 0))
    return pl.pallas_call(
        body, name=name, grid=(rows // tr,), in_specs=[spec] * len(parts), out_specs=spec,
        out_shape=jax.ShapeDtypeStruct((rows, cols), F32), compiler_params=_params(("parallel",)),
    )(*parts)


ANY = pl.BlockSpec(memory_space=pl.ANY)


def _other_chips(x, y):
    return [(1 - x, y), (x, 1 - y), (1 - x, 1 - y)]


def _gather_shards(arrays, name):
    n = len(arrays)

    def body(*refs):
        ins, outs = refs[:n], refs[n:2 * n]
        ici_send, ici_recv, d2d_send, d2d_recv, local_sems, own_recv = refs[2 * n:]
        x, y, c = lax.axis_index("x"), lax.axis_index("y"), lax.axis_index("c")
        me = 2 * x + y
        chips = _other_chips(x, y)

        def half(i, who):
            h = arrays[i].shape[0] // 2
            return pl.ds(who * h, h)

        def ici(i, j, src_chip, to):
            return pltpu.make_async_remote_copy(
                src_ref=ins[i].at[half(i, c)], dst_ref=outs[i].at[src_chip, half(i, c)], send_sem=ici_send.at[3 * i + j],
                recv_sem=ici_recv.at[3 * i + j], device_id=to, device_id_type=MESH)

        def d2d(i, j, src_chip, who):
            piece = outs[i].at[src_chip, half(i, who)]
            return pltpu.make_async_remote_copy(
                src_ref=piece, dst_ref=piece, send_sem=d2d_send.at[3 * i + j], recv_sem=d2d_recv.at[3 * i + j],
                device_id=(x, y, 1 - c), device_id_type=MESH)

        def own(i):
            return pltpu.make_async_remote_copy(
                src_ref=ins[i], dst_ref=outs[i].at[me], send_sem=local_sems.at[i], recv_sem=own_recv.at[i],
                device_id=(x, y, 1 - c), device_id_type=MESH)

        sends = []
        for i in range(n):
            cp = own(i)
            cp.start()
            sends.append(cp)
            for j, (px, py) in enumerate(chips):
                rc = ici(i, j, me, (px, py, c))
                rc.start()
                sends.append(rc)
        for i in range(n):
            for j, (px, py) in enumerate(chips):
                ici(i, j, 2 * px + py, (px, py, c)).wait_recv()
                fw = d2d(i, j, 2 * px + py, c)
                fw.start()
                sends.append(fw)
        for i in range(n):
            for j, (px, py) in enumerate(chips):
                d2d(i, j, 2 * px + py, 1 - c).wait_recv()
            own(i).wait_recv()
        for rc in sends:
            rc.wait_send()

    dma = lambda k: pltpu.SemaphoreType.DMA((k,))
    return pl.pallas_call(
        body, name=name, in_specs=[ANY] * n, out_specs=[ANY] * n,
        out_shape=[jax.ShapeDtypeStruct((4,) + a.shape, a.dtype) for a in arrays],
        scratch_shapes=[dma(3 * n), dma(3 * n), dma(3 * n), dma(3 * n), dma(n), dma(n)],
        compiler_params=pltpu.CompilerParams(has_side_effects=True),
    )(*arrays)


def _scatter_shards(stacks, name):
    n = len(stacks)

    def body(*refs):
        ins, outs = refs[:n], refs[n:2 * n]
        send_sems, recv_sems = refs[2 * n:]
        x, y, c = lax.axis_index("x"), lax.axis_index("y"), lax.axis_index("c")
        chips = _other_chips(x, y)
        sends = []
        for i in range(n):
            for j, (px, py) in enumerate(chips):
                rc = pltpu.make_async_remote_copy(
                    src_ref=ins[i].at[2 * px + py], dst_ref=outs[i].at[j], send_sem=send_sems.at[3 * i + j],
                    recv_sem=recv_sems.at[3 * i + j], device_id=(px, py, c), device_id_type=MESH)
                rc.start()
                sends.append(rc)
        for rc in sends:
            rc.wait_recv()
        for rc in sends:
            rc.wait_send()

    return pl.pallas_call(
        body, name=name, in_specs=[ANY] * n, out_specs=[ANY] * n,
        out_shape=[jax.ShapeDtypeStruct((3,) + a.shape[1:], a.dtype) for a in stacks],
        scratch_shapes=[pltpu.SemaphoreType.DMA((3 * n,)), pltpu.SemaphoreType.DMA((3 * n,))],
        compiler_params=pltpu.CompilerParams(has_side_effects=True),
    )(*stacks)


def _pair_exchange(stacks, name):
    n = len(stacks)

    def body(*refs):
        ins, outs = refs[:n], refs[n:2 * n]
        send_sems, recv_sems = refs[2 * n:]
        x, y, c = lax.axis_index("x"), lax.axis_index("y"), lax.axis_index("c")
        cps = []
        for i in range(n):
            h = stacks[i].shape[1] // 2
            rc = pltpu.make_async_remote_copy(
                src_ref=ins[i].at[:, pl.ds((1 - c) * h, h)], dst_ref=outs[i], send_sem=send_sems.at[i],
                recv_sem=recv_sems.at[i], device_id=(x, y, 1 - c), device_id_type=MESH)
            rc.start()
            cps.append(rc)
        for rc in cps:
            rc.wait_recv()
        for rc in cps:
            rc.wait_send()

    return pl.pallas_call(
        body, name=name, in_specs=[ANY] * n, out_specs=[ANY] * n,
        out_shape=[jax.ShapeDtypeStruct((4, a.shape[1] // 2) + a.shape[2:], a.dtype) for a in stacks],
        scratch_shapes=[pltpu.SemaphoreType.DMA((n,)), pltpu.SemaphoreType.DMA((n,))],
        compiler_params=pltpu.CompilerParams(has_side_effects=True),
    )(*stacks)


def _pair_sum(own, theirs, core, name):
    _, r, cols = own.shape
    h = r // 2
    tr = next(t for t in (256, 128, 64, 32, 16) if h % t == 0 and t * cols * 4 <= (1 << 20))
    nt = h // tr

    def body(core_ref, own_ref, th_ref, o32_ref, o16_ref):
        del core_ref
        acc = own_ref[...] + th_ref[...].astype(F32)
        o32_ref[...] = acc
        o16_ref[...] = acc.astype(BF16)

    out = _bs((1, tr, cols), lambda j, t, core_ref: (j, t, 0))
    return pl.pallas_call(
        body, name=name,
        grid_spec=pltpu.PrefetchScalarGridSpec(
            num_scalar_prefetch=1, grid=(4, nt),
            in_specs=[_bs((1, tr, cols), lambda j, t, core_ref: (j, core_ref[0] * nt + t, 0)), out],
            out_specs=[out, out]),
        out_shape=[jax.ShapeDtypeStruct((4, h, cols), F32), jax.ShapeDtypeStruct((4, h, cols), BF16)],
        compiler_params=_params(("parallel", "parallel")),
    )(core, own, theirs)


def _swap_sibling(arrays, name):
    n = len(arrays)

    def body(*refs):
        ins, outs = refs[:n], refs[n:2 * n]
        send_sems, recv_sems = refs[2 * n:]
        sib = (lax.axis_index("x"), lax.axis_index("y"), 1 - lax.axis_index("c"))
        cps = []
        for i in range(n):
            rc = pltpu.make_async_remote_copy(src_ref=ins[i], dst_ref=outs[i], send_sem=send_sems.at[i],
                                              recv_sem=recv_sems.at[i], device_id=sib, device_id_type=MESH)
            rc.start()
            cps.append(rc)
        for rc in cps:
            rc.wait_recv()
        for rc in cps:
            rc.wait_send()

    return pl.pallas_call(
        body, name=name, in_specs=[ANY] * n, out_specs=[ANY] * n,
        out_shape=[jax.ShapeDtypeStruct(a.shape, a.dtype) for a in arrays],
        scratch_shapes=[pltpu.SemaphoreType.DMA((n,)), pltpu.SemaphoreType.DMA((n,))],
        compiler_params=pltpu.CompilerParams(has_side_effects=True),
    )(*arrays)


def _all_reduce_small(v):
    rows = v.shape[0]

    def body(v_ref, o_ref, buf, send_sems, recv_sems):
        x, y, c = lax.axis_index("x"), lax.axis_index("y"), lax.axis_index("c")
        me = 4 * x + 2 * y + c
        buf[me] = v_ref[...]
        cps = []
        for kbits in range(1, 8):
            bx, by, bc = (kbits >> 2) & 1, (kbits >> 1) & 1, kbits & 1
            px = jnp.where(bx == 1, 1 - x, x)
            py = jnp.where(by == 1, 1 - y, y)
            pc = jnp.where(bc == 1, 1 - c, c)
            rc = pltpu.make_async_remote_copy(src_ref=v_ref, dst_ref=buf.at[me], send_sem=send_sems.at[kbits - 1],
                                              recv_sem=recv_sems.at[kbits - 1], device_id=(px, py, pc),
                                              device_id_type=MESH)
            rc.start()
            cps.append((rc, 4 * px + 2 * py + pc))
        for kbits, (rc, src) in enumerate(cps):
            pltpu.make_async_remote_copy(src_ref=v_ref, dst_ref=buf.at[src], send_sem=send_sems.at[kbits],
                                         recv_sem=recv_sems.at[kbits], device_id=(x, y, c),
                                         device_id_type=MESH).wait_recv()
        for rc, _ in cps:
            rc.wait_send()
        acc = buf[0]
        for d in range(1, 8):
            acc = acc + buf[d]
        o_ref[...] = acc

    return pl.pallas_call(
        body, name="all_reduce_small",
        in_specs=[pl.BlockSpec(memory_space=pltpu.VMEM)], out_specs=pl.BlockSpec(memory_space=pltpu.VMEM),
        out_shape=jax.ShapeDtypeStruct((rows, 128), F32),
        scratch_shapes=[pltpu.VMEM((8, rows, 128), F32), pltpu.SemaphoreType.DMA((7,)), pltpu.SemaphoreType.DMA((7,))],
        compiler_params=pltpu.CompilerParams(has_side_effects=True, vmem_limit_bytes=VMEM_LIMIT),
    )(v)


def _rope_tables(s):
    half = HD // 2
    inv = 10000.0 ** (-jnp.arange(half, dtype=F32) / half)
    ang = jnp.arange(s, dtype=F32)[:, None] * inv[None, :]
    cos, sin = jnp.cos(ang), jnp.sin(ang)
    return jnp.concatenate([cos, cos], axis=1), jnp.concatenate([sin, sin], axis=1)


def _local_step(x, mem, target, norm_g, mem_norm_g, w_in, gate_b, gq, gk, sink, wa, mu, k_k, k_a, r_k, w0, w2, a0, a2,
                ln_w, ln_b, wr, w_kv, gxq, gxk, wx, w_out):
    s = x.shape[0]
    cos, sin = _rope_tables(s)
    r_k = r_k.reshape(1, RW)

    proj, h = _proj_fwd(x, norm_g, w_in)
    ya = _attn_fwd(proj, cos, sin, gq, gk, sink)
    mkv, mn = _mem_kv(mem, mem_norm_g, w_kv)
    yx = _xattn_fwd(proj, mkv, gxq, gxk)
    ps = _shift_fwd(proj, mu)
    kk, dec0, kd0, b0, dec1, kd1, b1 = _pre_fwd(ps, k_k, k_a, w0, w2, a0, a2)
    (y0, ck0), (y1, ck1) = _scan2_fwd([(dec0, kd0, b0), (dec1, kd1, b1)], ps, kk)
    yr = _post_fwd(y0, y1, ps, kd0, kd1, proj, r_k, ln_w, ln_b)
    merged = _merge_fwd(ya, yr, yx, wa, wr, wx, proj, gate_b)
    loss_tile, dout, dout16 = _out_fwd(merged, w_out, x, target)
    loss_sum = loss_tile[0, 0]

    g = {}
    t16 = lambda a: a.astype(BF16).T
    sk = min(1024, s)
    dmerged = _matmul(dout16, w_out, mode="nt", m=s, n=D, k=D, tm=sk, tn=1024, tk=1024, name="dmerged")
    g["w_out"] = _matmul(merged.T, dout16, mode="nn", m=D, n=D, k=s, tm=1024, tn=1024, tk=sk, name="grad_w_out")
    dg0, dg1, dg2, du0, du1, du2, dya, dyr, dyx = _merge_bwd(ya, yr, yx, wa, wr, wx, proj, gate_b, dmerged)
    g["attn_w_o"] = _matmul(t16(ya), du0, mode="nn", m=RW, n=D, k=s, tm=RW, tn=1024, tk=s, name="grad_attn_w_o")
    g["rwkv_w_o"] = _matmul(t16(yr), du1, mode="nn", m=RW, n=D, k=s, tm=RW, tn=1024, tk=s, name="grad_rwkv_w_o")
    g["x_w_o"] = _matmul(t16(yx), du2, mode="nn", m=XW, n=D, k=s, tm=XW, tn=1024, tk=s, name="grad_x_w_o")
    dmg = jnp.concatenate([dg0, dg1, dg2], axis=1)
    g["gate_b"] = _colsum(dmg, "grad_gate_b")

    daq, dak, dav, dag, g["attn_q_norm_g"], g["attn_k_norm_g"], g["attn_sink"] = _attn_bwd(proj, cos, sin, gq, gk, sink, dya)

    dxq, dxg, dmkv, g["x_q_norm_g"], g["x_k_norm_g"] = _xattn_bwd(proj, mkv, gxq, gxk, dyx)
    g["x_w_kv"] = _matmul(mn, dmkv, mode="tn", m=D, n=2 * XW, k=NMEM, tm=512, tn=512, tk=NMEM, name="grad_x_w_kv")
    dmn = _matmul(dmkv, w_kv, mode="nt", m=NMEM, n=D, k=2 * XW, tm=NMEM, tn=512, tk=2 * XW, name="dmn")
    g["mem_norm_g"] = _mem_bwd(mem, mem_norm_g, dmn)

    dys, dr_p, dv_p, dkd0_p, dkd1_p, drg, g["rwkv_r_k"], g["rwkv_ln_w"], g["rwkv_ln_b"] = _post_bwd(
        y0, y1, ps, kd0, kd1, proj, r_k, ln_w, ln_b, dyr)
    (dr0, dd0, db0, dk0, dkk0, dv0), (dr1, dd1, db1, dk1, dkk1, dv1) = _scan2_bwd(
        [(dec0, kd0, b0, ck0), (dec1, kd1, b1, ck1)], ps, kk, dys)
    dr = dr_p + dr0 + dr1
    dv = dv_p + dv0 + dv1
    cts = (dkk0 + dkk1, dd0, dk0 + dkd0_p, db0, dd1, dk1 + dkd1_p, db1)
    dps, g["rwkv_k_k"], g["rwkv_k_a"], g["rwkv_w0"], g["rwkv_w2"], g["rwkv_a0"], g["rwkv_a2"] = _pre_bwd(
        ps, k_k, k_a, w0, w2, a0, a2, dr, dv, cts)
    drs, g["rwkv_mu"] = _shift_bwd(proj, mu, dps)

    dproj = jnp.concatenate([daq.astype(BF16), dak.astype(BF16), dav.astype(BF16), dag.astype(BF16), drs.astype(BF16),
                             drg.astype(BF16), dxq.astype(BF16), dxg.astype(BF16), dmg], axis=1)
    g["w_in"] = _matmul(h.T, dproj, mode="nn", m=D, n=NIN, k=s, tm=512, tn=896, tk=s, name="grad_w_in")
    grad_x, g["norm_g"] = _in_bwd(dproj, w_in, x, norm_g, dout)
    g["rwkv_r_k"] = g["rwkv_r_k"].reshape(AH, HD)
    return loss_sum, grad_x, g


WEIGHTS = ['norm_g', 'mem_norm_g', 'w_in', 'gate_b', 'attn_q_norm_g', 'attn_k_norm_g', 'attn_sink', 'attn_w_o',
           'rwkv_mu', 'rwkv_k_k', 'rwkv_k_a', 'rwkv_r_k', 'rwkv_w0', 'rwkv_w2', 'rwkv_a0', 'rwkv_a2', 'rwkv_ln_w',
           'rwkv_ln_b', 'rwkv_w_o', 'x_w_kv', 'x_q_norm_g', 'x_k_norm_g', 'x_w_o', 'w_out']
BIG = ['w_in', 'attn_w_o', 'rwkv_w_o', 'x_w_kv', 'x_w_o', 'w_out']
COL_SHARDED = ['w_in', 'attn_w_o', 'rwkv_w_o', 'x_w_o']
LORA = ['rwkv_w0', 'rwkv_w2', 'rwkv_a0', 'rwkv_a2']
SMALL = [n for n in WEIGHTS if n not in BIG]


def _unshard_cols(stack):
    return jnp.concatenate([stack[i] for i in range(4)], axis=-1)


def _shard_cols(full):
    w = full.shape[-1] // 4
    return [full[..., i * w:(i + 1) * w] for i in range(4)]


def kernel(x, mem, norm_g, mem_norm_g, w_in, gate_b, attn_q_norm_g, attn_k_norm_g, attn_sink, attn_w_o, rwkv_mu, rwkv_k_k, rwkv_k_a, rwkv_r_k, rwkv_w0, rwkv_w2, rwkv_a0, rwkv_a2, rwkv_ln_w, rwkv_ln_b, rwkv_w_o, x_w_kv, x_q_norm_g, x_k_norm_g, x_w_o, w_out, loss_target, m_norm_g, m_mem_norm_g, m_w_in, m_gate_b, m_attn_q_norm_g, m_attn_k_norm_g, m_attn_sink, m_attn_w_o, m_rwkv_mu, m_rwkv_k_k, m_rwkv_k_a, m_rwkv_r_k, m_rwkv_w0, m_rwkv_w2, m_rwkv_a0, m_rwkv_a2, m_rwkv_ln_w, m_rwkv_ln_b, m_rwkv_w_o, m_x_w_kv, m_x_q_norm_g, m_x_k_norm_g, m_x_w_o, m_w_out, v_norm_g, v_mem_norm_g, v_w_in, v_gate_b, v_attn_q_norm_g, v_attn_k_norm_g, v_attn_sink, v_attn_w_o, v_rwkv_mu, v_rwkv_k_k, v_rwkv_k_a, v_rwkv_r_k, v_rwkv_w0, v_rwkv_w2, v_rwkv_a0, v_rwkv_a2, v_rwkv_ln_w, v_rwkv_ln_b, v_rwkv_w_o, v_x_w_kv, v_x_q_norm_g, v_x_k_norm_g, v_x_w_o, v_w_out):
    args = dict(locals())
    canon = lambda a: a[0] if a.ndim > 2 else a
    w = {n: canon(args[n]) for n in WEIGHTS}
    m = {n: canon(args["m_" + n]) for n in WEIGHTS}
    v = {n: canon(args["v_" + n]) for n in WEIGHTS}
    shard = 2 * lax.axis_index("x") + lax.axis_index("y")

    local = [w[n].astype(BF16) for n in BIG] + [w[n].reshape(2, -1, w[n].shape[-1]) for n in LORA]
    stacks = dict(zip(BIG + LORA, _gather_shards(local, "gather_weights")))
    full = {}
    for n in COL_SHARDED:
        full[n] = _unshard_cols(stacks[n])
    for n in LORA:
        full[n] = _unshard_cols(stacks[n]).reshape(w[n].shape[:-1] + (RW,))
    full["x_w_kv"] = stacks["x_w_kv"].reshape(D, 2 * XW)
    full["w_out"] = stacks["w_out"].reshape(D, D)

    loss_sum, grad_x, g = _local_step(
        x[0], mem[0], loss_target[0], w["norm_g"], w["mem_norm_g"], full["w_in"], w["gate_b"], w["attn_q_norm_g"],
        w["attn_k_norm_g"], w["attn_sink"], full["attn_w_o"], w["rwkv_mu"], w["rwkv_k_k"], w["rwkv_k_a"], w["rwkv_r_k"],
        full["rwkv_w0"], full["rwkv_w2"], full["rwkv_a0"], full["rwkv_a2"], w["rwkv_ln_w"], w["rwkv_ln_b"],
        full["rwkv_w_o"], full["x_w_kv"], w["x_q_norm_g"], w["x_k_norm_g"], full["x_w_o"], full["w_out"])

    loss = lax.psum(0.5 * loss_sum / D, ("x", "y", "c"))

    def as_stack(n, dtype):
        if n in COL_SHARDED:
            return jnp.stack([p.astype(dtype) for p in _shard_cols(g[n])])
        return g[n].reshape((4, g[n].shape[0] // 4) + g[n].shape[1:]).astype(dtype)

    core = lax.axis_index("c").astype(jnp.int32).reshape(1)
    sibling = _pair_exchange([as_stack(n, BF16) for n in BIG], "pair_exchange")
    pair32, pair16 = [], []
    for n, th in zip(BIG, sibling):
        a32, a16 = _pair_sum(as_stack(n, F32), th, core, "pair_sum_" + n)
        pair32.append(a32)
        pair16.append(a16)
    recv = _scatter_shards(pair16, "scatter_grads")
    halves = []
    for n, p32, r in zip(BIG, pair32, recv):
        own = lax.dynamic_index_in_dim(p32, shard, 0, keepdims=False)
        halves.append(_sum_parts([own, r[0], r[1], r[2]], "sum_" + n))
    other_halves = _swap_sibling(halves, "swap_halves")

    out_g, out_d, out_m, out_v = {}, {}, {}, {}
    for n, mine, theirs in zip(BIG, halves, other_halves):
        out_g[n], out_d[n], out_m[n], out_v[n] = _adamw_halves(mine, theirs, core, w[n], m[n], v[n], "adamw_" + n)

    flat = jnp.concatenate([g[n].reshape(-1) for n in SMALL])
    total = flat.shape[0]
    padded = -(-total // 1024) * 1024
    flat = jnp.pad(flat, (0, padded - total)).reshape(padded // 128, 128)
    red = _all_reduce_small(flat).reshape(-1)
    off = 0
    gs = {}
    for n in SMALL:
        size = g[n].size
        t = red[off:off + size].reshape(g[n].shape)
        off += size
        if n in LORA:
            wd = t.shape[-1] // 4
            t = lax.dynamic_slice_in_dim(t, shard * wd, wd, axis=t.ndim - 1)
        gs[n] = t

    def pack(d):
        f = jnp.concatenate([d[n].reshape(-1) for n in SMALL])
        return jnp.pad(f, (0, -(-f.shape[0] // 1024) * 1024 - f.shape[0])).reshape(-1, 128)

    pg, pd, pm, pv = _adamw([pack(gs)], pack(w), pack(m), pack(v), "adamw_small")
    off = 0
    for n in SMALL:
        size = w[n].size
        for dst, src in ((out_g, pg), (out_d, pd), (out_m, pm), (out_v, pv)):
            dst[n] = src.reshape(-1)[off:off + size].reshape(w[n].shape)
        off += size

    lead = lambda d: [d[n][None] if args[n].ndim > 2 else d[n] for n in WEIGHTS]
    return (loss, grad_x[None], *lead(out_g), *lead(out_d), *lead(out_m), *lead(out_v))
```

```python
import functools

import jax
import jax.numpy as jnp
from jax import lax
from jax.experimental import pallas as pl
from jax.experimental.pallas import tpu as pltpu

F32 = jnp.float32
BF16 = jnp.bfloat16
HI = lax.Precision.HIGH
MESH = pl.DeviceIdType.MESH

D = 2048
NMEM = 256
NORM_EPS = 1e-6
NEG_INF = -1e30
GN_EPS = 64e-5
HD = 64
AH = 12
AKV = 4
RW = 768
XH = 4
XD = 128
XW = 512
NIN = 12544
RSW = 2560
C_AQ, C_AK, C_AV, C_AG, C_RS, C_RG, C_XQ, C_XG, C_MG = 0, 768, 1024, 1280, 2048, 4608, 5376, 5888, 6400
WIN = 384
QB = 128
TC = 16
NPAIR = 6

ADAM_LR, ADAM_B1, ADAM_B2, ADAM_EPS, ADAM_WD, ADAM_STEP = 0.001, 0.9, 0.999, 1e-08, 0.01, 10

VMEM_LIMIT = 56 * 1024 * 1024


def _bs(shape, imap):
    return pl.BlockSpec(shape, imap)


def _params(sem=None, vmem=VMEM_LIMIT):
    return pltpu.CompilerParams(dimension_semantics=sem, vmem_limit_bytes=vmem)


def _dot(a, b, dims):
    return lax.dot_general(a.astype(BF16), b.astype(BF16), (dims, ((), ())), preferred_element_type=F32)


@jax.custom_vjp
def _mm_nn(a, b):
    return _dot(a, b, ((1,), (0,)))


def _mm_nn_fwd(a, b):
    return _mm_nn(a, b), (a, b)


def _mm_nn_bwd(res, ct):
    a, b = res
    return _dot(ct, b, ((1,), (1,))), _dot(a, ct, ((0,), (0,)))


_mm_nn.defvjp(_mm_nn_fwd, _mm_nn_bwd)


@jax.custom_vjp
def _mm_nt(a, b):
    return _dot(a, b, ((1,), (1,)))


def _mm_nt_fwd(a, b):
    return _mm_nt(a, b), (a, b)


def _mm_nt_bwd(res, ct):
    a, b = res
    return _dot(ct, b, ((1,), (0,))), _dot(ct, a, ((0,), (0,)))


_mm_nt.defvjp(_mm_nt_fwd, _mm_nt_bwd)


def _seg_matrix(n, seg):
    r = lax.broadcasted_iota(jnp.int32, (n, n), 0) // seg
    c = lax.broadcasted_iota(jnp.int32, (n, n), 1) // seg
    return (r == c).astype(F32)


def _rot_matrix():
    r = lax.broadcasted_iota(jnp.int32, (HD, HD), 0)
    c = lax.broadcasted_iota(jnp.int32, (HD, HD), 1)
    return jnp.where(c == r + HD // 2, 1.0, 0.0).astype(F32) - jnp.where(c == r - HD // 2, 1.0, 0.0).astype(F32)


def _hdot(a, m):
    return jnp.dot(a, m, precision=HI, preferred_element_type=F32)


def _rms(t, g):
    return t * lax.rsqrt(jnp.mean(t * t, axis=-1, keepdims=True) + NORM_EPS) * g


def _silu(t):
    return t * jax.nn.sigmoid(t)


def _softplus(z):
    return jnp.maximum(z, 0.0) + jnp.log(1.0 + jnp.exp(-jnp.abs(z)))


def _matmul(a, b, *, mode, m, n, k, tm, tn, tk, name, a_off=(0, 0), b_off=(0, 0), out_dtype=F32):
    nk = k // tk
    if mode == "tn":
        a_spec = _bs((tk, tm), lambda i, j, kk: (kk + a_off[0], i + a_off[1]))
        dims = ((0,), (0,))
    else:
        a_spec = _bs((tm, tk), lambda i, j, kk: (i + a_off[0], kk + a_off[1]))
        dims = ((1,), (1,)) if mode == "nt" else ((1,), (0,))
    if mode == "nt":
        b_spec = _bs((tn, tk), lambda i, j, kk: (j + b_off[0], kk + b_off[1]))
    else:
        b_spec = _bs((tk, tn), lambda i, j, kk: (kk + b_off[0], j + b_off[1]))

    def body(a_ref, b_ref, o_ref, acc):
        kk = pl.program_id(2)

        @pl.when(kk == 0)
        def _():
            acc[...] = jnp.zeros_like(acc)

        acc[...] += _dot(a_ref[...], b_ref[...], dims)

        @pl.when(kk == nk - 1)
        def _():
            o_ref[...] = acc[...].astype(out_dtype)

    return pl.pallas_call(
        body, name=name, grid=(m // tm, n // tn, nk),
        in_specs=[a_spec, b_spec], out_specs=_bs((tm, tn), lambda i, j, kk: (i, j)),
        out_shape=jax.ShapeDtypeStruct((m, n), out_dtype),
        scratch_shapes=[pltpu.VMEM((tm, tn), F32)],
        compiler_params=_params(("parallel", "parallel", "arbitrary")),
    )(a, b)


def _grad_w_in(ht, dproj4):
    s = ht.shape[1]
    ws = NIN // 4
    tm, tk = 512, min(1024, s)
    nk = s // tk

    def body(a_ref, b_ref, o32_ref, o16_ref, acc):
        kk = pl.program_id(2)

        @pl.when(kk == 0)
        def _():
            acc[...] = jnp.zeros_like(acc)

        acc[...] += jnp.dot(a_ref[...], b_ref[0], preferred_element_type=F32)

        @pl.when(kk == nk - 1)
        def _():
            o32_ref[0] = acc[...]
            o16_ref[0] = acc[...].astype(BF16)

    out = _bs((1, tm, ws), lambda j, i, kk: (j, i, 0))
    return pl.pallas_call(
        body, name="grad_w_in", grid=(4, D // tm, nk),
        in_specs=[_bs((tm, tk), lambda j, i, kk: (i, kk)), _bs((1, tk, ws), lambda j, i, kk: (j, kk, 0))],
        out_specs=[out, out],
        out_shape=[jax.ShapeDtypeStruct((4, D, ws), F32), jax.ShapeDtypeStruct((4, D, ws), BF16)],
        scratch_shapes=[pltpu.VMEM((tm, ws), F32)],
        compiler_params=_params(("parallel", "parallel", "arbitrary")),
    )(ht, dproj4)


def _proj_fwd(x, g, w):
    s = x.shape[0]
    tm, tn = min(512, s), 896

    def body(x_ref, g_ref, w_ref, o_ref, h_ref, hs):
        @pl.when(pl.program_id(1) == 0)
        def _():
            h = _rms(x_ref[...], g_ref[...]).astype(BF16)
            hs[...] = h
            h_ref[...] = h

        o_ref[...] = jnp.dot(hs[...], w_ref[...], preferred_element_type=F32)

    return pl.pallas_call(
        body, name="proj_fwd", grid=(s // tm, NIN // tn),
        in_specs=[_bs((tm, D), lambda i, j: (i, 0)), _bs((1, D), lambda i, j: (0, 0)), _bs((D, tn), lambda i, j: (0, j))],
        out_specs=[_bs((tm, tn), lambda i, j: (i, j)), _bs((tm, D), lambda i, j: (i, 0))],
        out_shape=[jax.ShapeDtypeStruct((s, NIN), F32), jax.ShapeDtypeStruct((s, D), BF16)],
        scratch_shapes=[pltpu.VMEM((tm, D), BF16)],
        compiler_params=_params(("parallel", "arbitrary")),
    )(x, g, w)


def _rope(t, cos, sin, rot):
    return t * cos + _hdot(t, rot) * sin


def _attn_tile(qs, ks, vs, gs, sinks, gq, gk, cq, sq, ck, sk, mask, rot):
    outs = []
    for hk in range(AKV):
        kh = _rope(_rms(ks[hk], gk), ck, sk, rot)
        for g in range(AH // AKV):
            h = hk * (AH // AKV) + g
            qh = _rope(_rms(qs[h], gq), cq, sq, rot)
            sc = _mm_nt(qh, kh) * (HD ** -0.5)
            sc = jnp.where(mask, sc, NEG_INF)
            mx = lax.stop_gradient(jnp.maximum(jnp.max(sc, axis=-1, keepdims=True), sinks[h]))
            p = jnp.exp(sc - mx)
            den = jnp.sum(p, axis=-1, keepdims=True) + jnp.exp(sinks[h] - mx)
            o = _mm_nn(p / den, vs[hk])
            outs.append(o * _silu(gs[h]))
    return outs


def _attn_load(n, s, aq_ref, ak_ref, av_ref, ag_refs, cos_ref, sin_ref, sink_ref):
    start = pl.multiple_of(jnp.clip((n - 1) * QB, 0, s - WIN), QB)
    q0 = pl.multiple_of(n * QB, QB)
    qs = [aq_ref[:, h * HD:(h + 1) * HD] for h in range(AH)]
    ks = [ak_ref[pl.ds(start, WIN), h * HD:(h + 1) * HD] for h in range(AKV)]
    vs = [av_ref[pl.ds(start, WIN), h * HD:(h + 1) * HD] for h in range(AKV)]
    gs = [ag_refs[h // 4][:, (h % 4) * HD:(h % 4 + 1) * HD] for h in range(AH)]
    sinks = [sink_ref[0:1, h:h + 1] for h in range(AH)]
    cq, sq = cos_ref[pl.ds(q0, QB), :], sin_ref[pl.ds(q0, QB), :]
    ck, sk = cos_ref[pl.ds(start, WIN), :], sin_ref[pl.ds(start, WIN), :]
    qpos = q0 + lax.broadcasted_iota(jnp.int32, (QB, WIN), 0)
    kpos = start + lax.broadcasted_iota(jnp.int32, (QB, WIN), 1)
    mask = jnp.abs(kpos - qpos) <= QB
    return start, qs, ks, vs, gs, sinks, cq, sq, ck, sk, mask


def _attn_specs(s):
    return [
        _bs((QB, 768), lambda n: (n, 0)),
        _bs((s, 256), lambda n: (0, C_AK // 256)),
        _bs((s, 256), lambda n: (0, C_AV // 256)),
        _bs((QB, 256), lambda n: (n, C_AG // 256)),
        _bs((QB, 256), lambda n: (n, C_AG // 256 + 1)),
        _bs((QB, 256), lambda n: (n, C_AG // 256 + 2)),
        _bs((s, HD), lambda n: (0, 0)),
        _bs((s, HD), lambda n: (0, 0)),
        _bs((1, HD), lambda n: (0, 0)),
        _bs((1, HD), lambda n: (0, 0)),
        _bs((1, AH), lambda n: (0, 0)),
    ]


def _attn_fwd(proj, cos, sin, gq, gk, sink):
    s = proj.shape[0]

    def body(aq_ref, ak_ref, av_ref, ag0, ag1, ag2, cos_ref, sin_ref, gq_ref, gk_ref, sink_ref, o_ref):
        n = pl.program_id(0)
        _, qs, ks, vs, gs, sinks, cq, sq, ck, sk, mask = _attn_load(
            n, s, aq_ref, ak_ref, av_ref, (ag0, ag1, ag2), cos_ref, sin_ref, sink_ref)
        outs = _attn_tile(qs, ks, vs, gs, sinks, gq_ref[...], gk_ref[...], cq, sq, ck, sk, mask, _rot_matrix())
        for h in range(AH):
            o_ref[:, h * HD:(h + 1) * HD] = outs[h]

    return pl.pallas_call(
        body, name="attn_fwd", grid=(s // QB,),
        in_specs=_attn_specs(s), out_specs=_bs((QB, 768), lambda n: (n, 0)),
        out_shape=jax.ShapeDtypeStruct((s, 768), F32),
        compiler_params=_params(("arbitrary",)),
    )(proj, proj, proj, proj, proj, proj, cos, sin, gq, gk, sink)


def _attn_bwd(proj, cos, sin, gq, gk, sink, dy):
    s = proj.shape[0]

    def body(aq_ref, ak_ref, av_ref, ag0, ag1, ag2, cos_ref, sin_ref, gq_ref, gk_ref, sink_ref, dy_ref,
             daq_ref, dak_ref, dav_ref, dag_ref, dgq_ref, dgk_ref, dsink_ref):
        n = pl.program_id(0)

        @pl.when(n == 0)
        def _():
            dak_ref[...] = jnp.zeros_like(dak_ref)
            dav_ref[...] = jnp.zeros_like(dav_ref)
            dgq_ref[...] = jnp.zeros_like(dgq_ref)
            dgk_ref[...] = jnp.zeros_like(dgk_ref)
            dsink_ref[...] = jnp.zeros_like(dsink_ref)

        start, qs, ks, vs, gs, sinks, cq, sq, ck, sk, mask = _attn_load(
            n, s, aq_ref, ak_ref, av_ref, (ag0, ag1, ag2), cos_ref, sin_ref, sink_ref)
        rot = _rot_matrix()

        def f(qs, ks, vs, gs, sinks, gq, gk):
            return _attn_tile(qs, ks, vs, gs, sinks, gq, gk, cq, sq, ck, sk, mask, rot)

        _, vjp = jax.vjp(f, qs, ks, vs, gs, sinks, gq_ref[...], gk_ref[...])
        dys = [dy_ref[:, h * HD:(h + 1) * HD] for h in range(AH)]
        dqs, dks, dvs, dgs, dsinks, dgq, dgk = vjp(dys)
        for h in range(AH):
            daq_ref[:, h * HD:(h + 1) * HD] = dqs[h]
            dag_ref[:, h * HD:(h + 1) * HD] = dgs[h]
            dsink_ref[0:1, h:h + 1] += dsinks[h]
        for h in range(AKV):
            dak_ref[pl.ds(start, WIN), h * HD:(h + 1) * HD] += dks[h]
            dav_ref[pl.ds(start, WIN), h * HD:(h + 1) * HD] += dvs[h]
        dgq_ref[...] += dgq
        dgk_ref[...] += dgk

    whole = lambda shape: _bs(shape, lambda n: (0, 0))
    return pl.pallas_call(
        body, name="attn_bwd", grid=(s // QB,),
        in_specs=_attn_specs(s) + [_bs((QB, 768), lambda n: (n, 0))],
        out_specs=[_bs((QB, 768), lambda n: (n, 0)), whole((s, 256)), whole((s, 256)), _bs((QB, 768), lambda n: (n, 0)),
                   whole((1, HD)), whole((1, HD)), whole((1, AH))],
        out_shape=[jax.ShapeDtypeStruct((s, 768), F32), jax.ShapeDtypeStruct((s, 256), F32),
                   jax.ShapeDtypeStruct((s, 256), F32), jax.ShapeDtypeStruct((s, 768), F32),
                   jax.ShapeDtypeStruct((1, HD), F32), jax.ShapeDtypeStruct((1, HD), F32),
                   jax.ShapeDtypeStruct((1, AH), F32)],
        compiler_params=_params(("arbitrary",)),
    )(proj, proj, proj, proj, proj, proj, cos, sin, gq, gk, sink, dy)


def _mem_kv(mem, g, w):
    def body(m_ref, g_ref, w_ref, o_ref, mn_ref):
        mn = _rms(m_ref[...], g_ref[...]).astype(BF16)
        mn_ref[...] = mn
        o_ref[...] = jnp.dot(mn, w_ref[...], preferred_element_type=F32)

    return pl.pallas_call(
        body, name="mem_kv",
        out_shape=[jax.ShapeDtypeStruct((NMEM, 2 * XW), F32), jax.ShapeDtypeStruct((NMEM, D), BF16)],
        compiler_params=_params(),
    )(mem, g, w)


def _xattn_tile(qs, gs, kms, vms, gxq, gxk):
    outs = []
    for h in range(XH):
        q = _rms(qs[h], gxq)
        km = _rms(kms[h], gxk)
        sc = _mm_nt(q, km) * (XD ** -0.5)
        mx = lax.stop_gradient(jnp.max(sc, axis=-1, keepdims=True))
        p = jnp.exp(sc - mx)
        p = p / jnp.sum(p, axis=-1, keepdims=True)
        outs.append(_mm_nn(p, vms[h]) * _silu(gs[h]))
    return outs


XT = 256


def _xattn_specs():
    return [
        _bs((XT, 256), lambda i: (i, C_XQ // 256)), _bs((XT, 256), lambda i: (i, C_XQ // 256 + 1)),
        _bs((XT, 256), lambda i: (i, C_XG // 256)), _bs((XT, 256), lambda i: (i, C_XG // 256 + 1)),
        _bs((NMEM, 2 * XW), lambda i: (0, 0)),
        _bs((1, XD), lambda i: (0, 0)), _bs((1, XD), lambda i: (0, 0)),
    ]


def _xattn_load(q0, q1, g0, g1, mkv_ref):
    qs = [(q0, q1)[h // 2][:, (h % 2) * XD:(h % 2 + 1) * XD] for h in range(XH)]
    gs = [(g0, g1)[h // 2][:, (h % 2) * XD:(h % 2 + 1) * XD] for h in range(XH)]
    kms = [mkv_ref[:, h * XD:(h + 1) * XD] for h in range(XH)]
    vms = [mkv_ref[:, XW + h * XD:XW + (h + 1) * XD] for h in range(XH)]
    return qs, gs, kms, vms


def _xattn_fwd(proj, mkv, gxq, gxk):
    s = proj.shape[0]

    def body(q0, q1, g0, g1, mkv_ref, gxq_ref, gxk_ref, o_ref):
        qs, gs, kms, vms = _xattn_load(q0, q1, g0, g1, mkv_ref)
        outs = _xattn_tile(qs, gs, kms, vms, gxq_ref[...], gxk_ref[...])
        for h in range(XH):
            o_ref[:, h * XD:(h + 1) * XD] = outs[h]

    return pl.pallas_call(
        body, name="xattn_fwd", grid=(s // XT,),
        in_specs=_xattn_specs(), out_specs=_bs((XT, XW), lambda i: (i, 0)),
        out_shape=jax.ShapeDtypeStruct((s, XW), F32),
        compiler_params=_params(("arbitrary",)),
    )(proj, proj, proj, proj, mkv, gxq, gxk)


def _xattn_bwd(proj, mkv, gxq, gxk, dy):
    s = proj.shape[0]

    def body(q0, q1, g0, g1, mkv_ref, gxq_ref, gxk_ref, dy_ref, dq_ref, dg_ref, dmkv_ref, dgxq_ref, dgxk_ref):
        @pl.when(pl.program_id(0) == 0)
        def _():
            dmkv_ref[...] = jnp.zeros_like(dmkv_ref)
            dgxq_ref[...] = jnp.zeros_like(dgxq_ref)
            dgxk_ref[...] = jnp.zeros_like(dgxk_ref)

        qs, gs, kms, vms = _xattn_load(q0, q1, g0, g1, mkv_ref)
        _, vjp = jax.vjp(_xattn_tile, qs, gs, kms, vms, gxq_ref[...], gxk_ref[...])
        dqs, dgs, dkms, dvms, dgxq, dgxk = vjp([dy_ref[:, h * XD:(h + 1) * XD] for h in range(XH)])
        for h in range(XH):
            dq_ref[:, h * XD:(h + 1) * XD] = dqs[h]
            dg_ref[:, h * XD:(h + 1) * XD] = dgs[h]
            dmkv_ref[:, h * XD:(h + 1) * XD] += dkms[h]
            dmkv_ref[:, XW + h * XD:XW + (h + 1) * XD] += dvms[h]
        dgxq_ref[...] += dgxq
        dgxk_ref[...] += dgxk

    whole = lambda shape: _bs(shape, lambda i: (0, 0))
    return pl.pallas_call(
        body, name="xattn_bwd", grid=(s // XT,),
        in_specs=_xattn_specs() + [_bs((XT, XW), lambda i: (i, 0))],
        out_specs=[_bs((XT, XW), lambda i: (i, 0)), _bs((XT, XW), lambda i: (i, 0)), whole((NMEM, 2 * XW)),
                   whole((1, XD)), whole((1, XD))],
        out_shape=[jax.ShapeDtypeStruct((s, XW), F32), jax.ShapeDtypeStruct((s, XW), F32),
                   jax.ShapeDtypeStruct((NMEM, 2 * XW), F32), jax.ShapeDtypeStruct((1, XD), F32),
                   jax.ShapeDtypeStruct((1, XD), F32)],
        compiler_params=_params(("arbitrary",)),
    )(proj, proj, proj, proj, mkv, gxq, gxk, dy)


def _mem_bwd(mem, g, dmn):
    def body(m_ref, dmn_ref, o_ref):
        m = m_ref[...]
        r = lax.rsqrt(jnp.mean(m * m, axis=-1, keepdims=True) + NORM_EPS)
        o_ref[...] = jnp.sum(dmn_ref[...] * m * r, axis=0, keepdims=True)

    del g
    return pl.pallas_call(body, name="mem_norm_bwd", out_shape=jax.ShapeDtypeStruct((1, D), F32),
                          compiler_params=_params())(mem, dmn)


SHIFT_W = 512


def _shift_rows(p, s):
    row = lax.broadcasted_iota(jnp.int32, p.shape, 0)
    prev = jnp.where(row == 0, 0.0, pltpu.roll(p, 1, 0))
    nxt = jnp.where(row == s - 1, 0.0, pltpu.roll(p, s - 1, 0))
    return prev, nxt


def _shift_fwd(proj, mu):
    s = proj.shape[0]

    def body(p_ref, mu_ref, o_ref):
        p = p_ref[...]
        prev, nxt = _shift_rows(p, s)
        o_ref[...] = p + mu_ref[...] * (0.5 * (prev + nxt) - p)

    return pl.pallas_call(
        body, name="shift_fwd", grid=(RSW // SHIFT_W,),
        in_specs=[_bs((s, SHIFT_W), lambda j: (0, C_RS // SHIFT_W + j)), _bs((1, SHIFT_W), lambda j: (0, j))],
        out_specs=_bs((s, SHIFT_W), lambda j: (0, j)),
        out_shape=jax.ShapeDtypeStruct((s, RSW), F32),
        compiler_params=_params(("parallel",)),
    )(proj, mu)


def _shift_bwd(proj, mu, dps):
    s = proj.shape[0]

    def body(p_ref, mu_ref, g_ref, o_ref, dmu_ref):
        p, g, mu_v = p_ref[...], g_ref[...], mu_ref[...]
        prev, nxt = _shift_rows(p, s)
        dmu_ref[...] = jnp.sum(g * (0.5 * (prev + nxt) - p), axis=0, keepdims=True)
        mg = mu_v * g
        down, up = _shift_rows(mg, s)
        o_ref[...] = g * (1.0 - mu_v) + 0.5 * (down + up)

    return pl.pallas_call(
        body, name="shift_bwd", grid=(RSW // SHIFT_W,),
        in_specs=[_bs((s, SHIFT_W), lambda j: (0, C_RS // SHIFT_W + j)), _bs((1, SHIFT_W), lambda j: (0, j)),
                  _bs((s, SHIFT_W), lambda j: (0, j))],
        out_specs=[_bs((s, SHIFT_W), lambda j: (0, j)), _bs((1, SHIFT_W), lambda j: (0, j))],
        out_shape=[jax.ShapeDtypeStruct((s, RSW), F32), jax.ShapeDtypeStruct((1, RSW), F32)],
        compiler_params=_params(("parallel",)),
    )(proj, mu, dps)


def _pre_tile(k, wf, wb, af, ab, k_k, k_a, w0s, w2s, a0s, a2s, seg):
    kx = k * k_k
    ss = _hdot(kx * kx, seg)
    kk = kx / jnp.maximum(jnp.sqrt(ss), 1e-12)
    outs = [kk]
    for d, (w_in, a_in) in enumerate(((wf, af), (wb, ab))):
        z = w0s[d] + _mm_nn(jnp.tanh(w_in), w2s[d])
        wd = -_softplus(-z) - 0.5
        dec = jnp.exp(-jnp.exp(wd))
        ad = jax.nn.sigmoid(a0s[d] + _mm_nn(a_in, a2s[d]))
        kd = k * (1.0 + (ad - 1.0) * k_a)
        outs += [dec, kd, kk * ad]
    return outs


PT = 256


def _pre_load(ps_ref, kk_ref, ka_ref, w0_ref, w2_ref, a0_ref, a2_ref):
    k = ps_ref[:, RW:2 * RW]
    wf, wb = ps_ref[:, 3 * RW:3 * RW + 64], ps_ref[:, 3 * RW + 64:3 * RW + 128]
    af, ab = ps_ref[:, 3 * RW + 128:3 * RW + 192], ps_ref[:, 3 * RW + 192:3 * RW + 256]
    w0s = [w0_ref[0:1, :], w0_ref[1:2, :]]
    a0s = [a0_ref[0:1, :], a0_ref[1:2, :]]
    w2s = [w2_ref[0], w2_ref[1]]
    a2s = [a2_ref[0], a2_ref[1]]
    return (k, wf, wb, af, ab, kk_ref[...], ka_ref[...], w0s, w2s, a0s, a2s)


def _pre_specs():
    c = lambda shape: _bs(shape, lambda i: tuple(0 for _ in shape))
    return [_bs((PT, RSW), lambda i: (i, 0)), c((1, RW)), c((1, RW)), c((2, RW)), c((2, 64, RW)), c((2, RW)),
            c((2, 64, RW))]


def _pre_fwd(ps, k_k, k_a, w0, w2, a0, a2):
    s = ps.shape[0]

    def body(ps_ref, kk_ref, ka_ref, w0_ref, w2_ref, a0_ref, a2_ref, *outs):
        args = _pre_load(ps_ref, kk_ref, ka_ref, w0_ref, w2_ref, a0_ref, a2_ref)
        res = _pre_tile(*args, _seg_matrix(RW, HD))
        for o_ref, v in zip(outs, res):
            o_ref[...] = v

    return pl.pallas_call(
        body, name="rwkv_pre_fwd", grid=(s // PT,),
        in_specs=_pre_specs(), out_specs=[_bs((PT, RW), lambda i: (i, 0))] * 7,
        out_shape=[jax.ShapeDtypeStruct((s, RW), F32)] * 7,
        compiler_params=_params(("parallel",)),
    )(ps, k_k, k_a, w0, w2, a0, a2)


def _pre_bwd(ps, k_k, k_a, w0, w2, a0, a2, dr, dv, cts):
    s = ps.shape[0]

    def body(ps_ref, kk_ref, ka_ref, w0_ref, w2_ref, a0_ref, a2_ref, dr_ref, dv_ref, c0, c1, c2, c3, c4, c5, c6,
             dps_ref, dkk_ref, dka_ref, dw0_ref, dw2_ref, da0_ref, da2_ref):
        @pl.when(pl.program_id(0) == 0)
        def _():
            for r in (dkk_ref, dka_ref, dw0_ref, dw2_ref, da0_ref, da2_ref):
                r[...] = jnp.zeros_like(r)

        args = _pre_load(ps_ref, kk_ref, ka_ref, w0_ref, w2_ref, a0_ref, a2_ref)
        seg = _seg_matrix(RW, HD)
        _, vjp = jax.vjp(lambda *a: _pre_tile(*a, seg), *args)
        dk, dwf, dwb, daf, dab, dk_k, dk_a, dw0s, dw2s, da0s, da2s = vjp([c[...] for c in (c0, c1, c2, c3, c4, c5, c6)])
        dps_ref[:, 0:RW] = dr_ref[...]
        dps_ref[:, RW:2 * RW] = dk
        dps_ref[:, 2 * RW:3 * RW] = dv_ref[...]
        for j, t in enumerate((dwf, dwb, daf, dab)):
            dps_ref[:, 3 * RW + 64 * j:3 * RW + 64 * (j + 1)] = t
        dkk_ref[...] += dk_k
        dka_ref[...] += dk_a
        for d in range(2):
            dw0_ref[d:d + 1, :] += dw0s[d]
            da0_ref[d:d + 1, :] += da0s[d]
            dw2_ref[d] += dw2s[d]
            da2_ref[d] += da2s[d]

    c = lambda shape: _bs(shape, lambda i: tuple(0 for _ in shape))
    row = _bs((PT, RW), lambda i: (i, 0))
    return pl.pallas_call(
        body, name="rwkv_pre_bwd", grid=(s // PT,),
        in_specs=_pre_specs() + [row] * 9,
        out_specs=[_bs((PT, RSW), lambda i: (i, 0)), c((1, RW)), c((1, RW)), c((2, RW)), c((2, 64, RW)), c((2, RW)),
                   c((2, 64, RW))],
        out_shape=[jax.ShapeDtypeStruct((s, RSW), F32), jax.ShapeDtypeStruct((1, RW), F32),
                   jax.ShapeDtypeStruct((1, RW), F32), jax.ShapeDtypeStruct((2, RW), F32),
                   jax.ShapeDtypeStruct((2, 64, RW), F32), jax.ShapeDtypeStruct((2, RW), F32),
                   jax.ShapeDtypeStruct((2, 64, RW), F32)],
        compiler_params=_params(("arbitrary",)),
    )(ps, k_k, k_a, w0, w2, a0, a2, dr, dv, *cts)


def _post_tile(y0, y1, r, v, kd0, kd1, rg, r_k, ln_w, ln_b, seg):
    ysum = y0 + y1
    bonus = (_hdot(r * kd0 * r_k, seg) + _hdot(r * kd1 * r_k, seg)) * v
    mean = _hdot(ysum, seg) * (1.0 / HD)
    cen = ysum - mean
    var = _hdot(cen * cen, seg) * (1.0 / HD)
    y = cen * lax.rsqrt(var + GN_EPS) * ln_w + ln_b + bonus
    return y * _silu(rg)


def _post_specs():
    row = _bs((PT, RW), lambda i: (i, 0))
    c = _bs((1, RW), lambda i: (0, 0))
    return [row, row, _bs((PT, RW), lambda i: (i, 0)), _bs((PT, RW), lambda i: (i, 2)), row, row,
            _bs((PT, RW), lambda i: (i, C_RG // RW)), c, c, c]


def _post_fwd(y0, y1, ps, kd0, kd1, proj, r_k, ln_w, ln_b):
    s = ps.shape[0]

    def body(y0_ref, y1_ref, r_ref, v_ref, kd0_ref, kd1_ref, rg_ref, rk_ref, lw_ref, lb_ref, o_ref):
        o_ref[...] = _post_tile(y0_ref[...], y1_ref[...], r_ref[...], v_ref[...], kd0_ref[...], kd1_ref[...],
                                rg_ref[...], rk_ref[...], lw_ref[...], lb_ref[...], _seg_matrix(RW, HD))

    return pl.pallas_call(
        body, name="rwkv_post_fwd", grid=(s // PT,),
        in_specs=_post_specs(), out_specs=_bs((PT, RW), lambda i: (i, 0)),
        out_shape=jax.ShapeDtypeStruct((s, RW), F32),
        compiler_params=_params(("parallel",)),
    )(y0, y1, ps, ps, kd0, kd1, proj, r_k, ln_w, ln_b)


def _post_bwd(y0, y1, ps, kd0, kd1, proj, r_k, ln_w, ln_b, dy):
    s = ps.shape[0]

    def body(y0_ref, y1_ref, r_ref, v_ref, kd0_ref, kd1_ref, rg_ref, rk_ref, lw_ref, lb_ref, dy_ref,
             dys_ref, dr_ref, dv_ref, dkd0_ref, dkd1_ref, drg_ref, drk_ref, dlw_ref, dlb_ref):
        @pl.when(pl.program_id(0) == 0)
        def _():
            for r in (drk_ref, dlw_ref, dlb_ref):
                r[...] = jnp.zeros_like(r)

        seg = _seg_matrix(RW, HD)
        args = [t[...] for t in (y0_ref, y1_ref, r_ref, v_ref, kd0_ref, kd1_ref, rg_ref, rk_ref, lw_ref, lb_ref)]
        _, vjp = jax.vjp(lambda *a: _post_tile(*a, seg), *args)
        dy0, _, dr, dv, dkd0, dkd1, drg, drk, dlw, dlb = vjp(dy_ref[...])
        dys_ref[...] = dy0
        dr_ref[...] = dr
        dv_ref[...] = dv
        dkd0_ref[...] = dkd0
        dkd1_ref[...] = dkd1
        drg_ref[...] = drg
        drk_ref[...] += drk
        dlw_ref[...] += dlw
        dlb_ref[...] += dlb

    row = _bs((PT, RW), lambda i: (i, 0))
    c = _bs((1, RW), lambda i: (0, 0))
    return pl.pallas_call(
        body, name="rwkv_post_bwd", grid=(s // PT,),
        in_specs=_post_specs() + [row], out_specs=[row] * 6 + [c] * 3,
        out_shape=[jax.ShapeDtypeStruct((s, RW), F32)] * 6 + [jax.ShapeDtypeStruct((1, RW), F32)] * 3,
        compiler_params=_params(("arbitrary",)),
    )(y0, y1, ps, ps, kd0, kd1, proj, r_k, ln_w, ln_b, dy)


def _ones2():
    r = lax.broadcasted_iota(jnp.int32, (256, 128), 0) % 128 // HD
    c = lax.broadcasted_iota(jnp.int32, (256, 128), 1) // HD
    return (r == c).astype(BF16)


def _split(p):
    hi = p.astype(BF16)
    lo = (p - hi.astype(F32)).astype(BF16)
    return jnp.concatenate([hi, lo], axis=1)


def _to_t8(a):
    s = a.shape[0]
    t = a.reshape(s // 8, 8, NPAIR, 2, HD).transpose(0, 2, 4, 3, 1)
    t = jnp.pad(t, ((0, 0), (0, 0), (0, 0), (0, 0), (0, HD - 8))).reshape(s // 8, NPAIR, HD, 128)
    hi = t.astype(BF16)
    lo = (t - hi.astype(F32)).astype(BF16)
    return jnp.concatenate([hi, lo], axis=-1)


def _from_t8(t8):
    g = t8.shape[0]
    t = t8.reshape(g, NPAIR, HD, 2, HD)[..., :8]
    return t.transpose(0, 4, 1, 3, 2).reshape(g * 8, RW)


def _scan_specs(direction, nc, fwd_order):
    def tb(c):
        sc = c if fwd_order else nc - 1 - c
        return sc if direction == 0 else nc - 1 - sc

    row = _bs((TC, RW), lambda c: (tb(c), 0))
    rowv = _bs((TC, RW), lambda c: (tb(c), 2))
    return row, rowv


def _put_t8(ref, g, u, tiles):
    for p in range(NPAIR):
        ref[g, p, :, u:u + 1] = tiles[p][:, u:u + 1]
        ref[g, p, :, HD + u:HD + u + 1] = tiles[p][:, HD + u:HD + u + 1]


def _scan_fwd(dec, kd, b, ps, kk, vl, direction):
    s = dec.shape[0]
    nc, ng = s // TC, TC // 8
    row, t8_in, t8_out = _scan_specs(direction, nc, True)
    n = NPAIR * HD

    def body(dec_ref, kd_ref, b_ref, r_ref, kk_ref, vl_ref, y8_ref, ck_ref, st):
        @pl.when(pl.program_id(0) == 0)
        def _():
            st[...] = jnp.zeros_like(st)

        ck_ref[0] = st[...]
        ones2 = _ones2()
        lane_u = lax.broadcasted_iota(jnp.int32, (HD, 256), 1) % HD
        tiles = lambda res, k: [res[k * n + p * HD:k * n + (p + 1) * HD] for p in range(NPAIR)]

        def group(gi, carry):
            g = gi if direction == 0 else ng - 1 - gi
            rows8 = pl.ds(pl.multiple_of(g * 8, 8), 8)
            d8, k8, b8, r8, kk8 = (q[rows8, :] for q in (dec_ref, kd_ref, b_ref, r_ref, kk_ref))
            pc = [slice(p * 128, (p + 1) * 128) for p in range(NPAIR)]
            ss = [st[p] for p in range(NPAIR)]
            u_prev = None
            for ui in range(8):
                u = ui if direction == 0 else 7 - ui
                lhs = [_split(ss[p] * kk8[u:u + 1, pc[p]]) for p in range(NPAIR)]
                for p in range(NPAIR):
                    vt = vl_ref[g, p]
                    lhs.append(jnp.where(lane_u == u, vt, jnp.zeros_like(vt)))
                if u_prev is not None:
                    lhs += [_split(ss[p] * r8[u_prev:u_prev + 1, pc[p]]) for p in range(NPAIR)]
                res = jnp.dot(jnp.concatenate(lhs, axis=0), ones2, preferred_element_type=F32)
                if u_prev is not None:
                    _put_t8(y8_ref, g, u_prev, tiles(res, 2))
                sa, vb = tiles(res, 0), tiles(res, 1)
                for p in range(NPAIR):
                    ss[p] = ss[p] * d8[u:u + 1, pc[p]] - sa[p] * b8[u:u + 1, pc[p]] + vb[p] * k8[u:u + 1, pc[p]]
                u_prev = u
            lhs = [_split(ss[p] * r8[u_prev:u_prev + 1, pc[p]]) for p in range(NPAIR)]
            res = jnp.dot(jnp.concatenate(lhs, axis=0), ones2, preferred_element_type=F32)
            _put_t8(y8_ref, g, u_prev, tiles(res, 0))
            for p in range(NPAIR):
                st[p] = ss[p]
            return carry

        lax.fori_loop(0, ng, group, 0)

    return pl.pallas_call(
        body, name=f"rwkv_scan_fwd{direction}", grid=(nc,),
        in_specs=[row, row, row, row, row, t8_in],
        out_specs=[t8_out, _bs((1, NPAIR, HD, 128), lambda c: (c, 0, 0, 0))],
        out_shape=[jax.ShapeDtypeStruct((s // 8, NPAIR, HD, 128), F32),
                   jax.ShapeDtypeStruct((nc, NPAIR, HD, 128), F32)],
        scratch_shapes=[pltpu.VMEM((NPAIR, HD, 128), F32)],
        compiler_params=_params(("arbitrary",)),
    )(dec, kd, b, ps, kk, vl)


def _scan_bwd(dec, kd, b, ps, kk, vl, dyl, ck, direction):
    s = dec.shape[0]
    nc, ng = s // TC, TC // 8
    row, t8_in, t8_out = _scan_specs(direction, nc, False)
    n = NPAIR * HD

    def body(dec_ref, kd_ref, b_ref, r_ref, kk_ref, vl_ref, dyl_ref, ck_ref,
             dr_ref, dd_ref, db_ref, dk_ref, dkk_ref, dv8_ref, st, sa_s, vb_s, dy_s, ds):
        @pl.when(pl.program_id(0) == 0)
        def _():
            ds[...] = jnp.zeros_like(ds)

        st[0] = ck_ref[0]
        ones2 = _ones2()
        lane_u = lax.broadcasted_iota(jnp.int32, (HD, 256), 1) % HD
        row_id = lax.broadcasted_iota(jnp.int32, (8, 128), 0)
        pc = [slice(p * 128, (p + 1) * 128) for p in range(NPAIR)]
        tiles = lambda res, k: [res[k * n + p * HD:k * n + (p + 1) * HD] for p in range(NPAIR)]

        def fgroup(gi, carry):
            g = gi if direction == 0 else ng - 1 - gi
            rows8 = pl.ds(pl.multiple_of(g * 8, 8), 8)
            d8, k8, b8, kk8 = (q[rows8, :] for q in (dec_ref, kd_ref, b_ref, kk_ref))
            ss = [st[gi * 8, p] for p in range(NPAIR)]
            for ui in range(8):
                u = ui if direction == 0 else 7 - ui
                i = gi * 8 + ui
                lhs = [_split(ss[p] * kk8[u:u + 1, pc[p]]) for p in range(NPAIR)]
                for ref in (vl_ref, dyl_ref):
                    for p in range(NPAIR):
                        t = ref[g, p]
                        lhs.append(jnp.where(lane_u == u, t, jnp.zeros_like(t)))
                res = jnp.dot(jnp.concatenate(lhs, axis=0), ones2, preferred_element_type=F32)
                sa, vb, dyb = tiles(res, 0), tiles(res, 1), tiles(res, 2)
                for p in range(NPAIR):
                    sa_s[i, p] = sa[p]
                    vb_s[i, p] = vb[p]
                    dy_s[i, p] = dyb[p]
                    ss[p] = ss[p] * d8[u:u + 1, pc[p]] - sa[p] * b8[u:u + 1, pc[p]] + vb[p] * k8[u:u + 1, pc[p]]
                    st[i + 1, p] = ss[p]
            return carry

        lax.fori_loop(0, ng, fgroup, 0)

        def bgroup(gj, carry):
            gi = ng - 1 - gj
            g = gi if direction == 0 else ng - 1 - gi
            rows8 = pl.ds(pl.multiple_of(g * 8, 8), 8)
            d8, k8, b8, r8, kk8 = (q[rows8, :] for q in (dec_ref, kd_ref, b_ref, r_ref, kk_ref))
            dss = [ds[p] for p in range(NPAIR)]
            acc = [[jnp.zeros((8, 128), F32) for _ in range(5)] for _ in range(NPAIR)]
            for uj in range(8):
                ui = 7 - uj
                u = ui if direction == 0 else 7 - ui
                i = gi * 8 + ui
                dyb = [dy_s[i, p] for p in range(NPAIR)]
                for p in range(NPAIR):
                    dss[p] = dss[p] + dyb[p] * r8[u:u + 1, pc[p]]
                lhs = [_split(dss[p] * b8[u:u + 1, pc[p]]) for p in range(NPAIR)]
                lhs += [_split(dss[p] * k8[u:u + 1, pc[p]]) for p in range(NPAIR)]
                res = jnp.dot(jnp.concatenate(lhs, axis=0), ones2, preferred_element_type=F32)
                dsa, dvb = tiles(res, 0), tiles(res, 1)
                _put_t8(dv8_ref, g, u, dvb)
                for p in range(NPAIR):
                    sp, sn = st[i, p], st[i + 1, p]
                    outs = (jnp.sum(sn * dyb[p], axis=0, keepdims=True), jnp.sum(dss[p] * sp, axis=0, keepdims=True),
                            -jnp.sum(dss[p] * sa_s[i, p], axis=0, keepdims=True),
                            jnp.sum(dss[p] * vb_s[i, p], axis=0, keepdims=True),
                            -jnp.sum(sp * dsa[p], axis=0, keepdims=True))
                    acc[p] = [jnp.where(row_id == u, o, a_) for o, a_ in zip(outs, acc[p])]
                    dss[p] = dss[p] * d8[u:u + 1, pc[p]] - dsa[p] * kk8[u:u + 1, pc[p]]
            for p in range(NPAIR):
                ds[p] = dss[p]
                for o_ref, a_ in zip((dr_ref, dd_ref, db_ref, dk_ref, dkk_ref), acc[p]):
                    o_ref[rows8, pc[p]] = a_
            return carry

        lax.fori_loop(0, ng, bgroup, 0)

    chunk = lambda k: pltpu.VMEM((k, NPAIR, HD, 128), F32)
    return pl.pallas_call(
        body, name=f"rwkv_scan_bwd{direction}", grid=(nc,),
        in_specs=[row, row, row, row, row, t8_in, t8_in, _bs((1, NPAIR, HD, 128), lambda c: (nc - 1 - c, 0, 0, 0))],
        out_specs=[row] * 5 + [t8_out],
        out_shape=[jax.ShapeDtypeStruct((s, RW), F32)] * 5 + [jax.ShapeDtypeStruct((s // 8, NPAIR, HD, 128), F32)],
        scratch_shapes=[chunk(TC + 1), chunk(TC), chunk(TC), chunk(TC), pltpu.VMEM((NPAIR, HD, 128), F32)],
        compiler_params=_params(("arbitrary",)),
    )(dec, kd, b, ps, kk, vl, dyl, ck)


def _tiles(res, k):
    n = NPAIR * HD
    return [res[k * n + p * HD:k * n + (p + 1) * HD] for p in range(NPAIR)]


def _rows_to_tiles(src_ref, rows8, stage, out_s, base):
    for p in range(NPAIR):
        stage[base + p, 0:8, 0:HD] = src_ref[rows8, p * 128:p * 128 + HD]
        stage[base + p, HD:HD + 8, 0:HD] = src_ref[rows8, p * 128 + HD:(p + 1) * 128]
        out_s[base + p] = stage[base + p].T[0:HD].astype(BF16)


def _tiles_to_rows(tile_s, base, dst_ref, rows8):
    for p in range(NPAIR):
        t = jnp.concatenate([tile_s[base + p], jnp.zeros((HD, 128), F32)], axis=0).T
        dst_ref[rows8, p * 128:p * 128 + HD] = t[0:8, 0:HD]
        dst_ref[rows8, p * 128 + HD:(p + 1) * 128] = t[HD:HD + 8, 0:HD]


def _put_cols(tile_s, base, u, tiles):
    for p in range(NPAIR):
        tile_s[base + p, :, u:u + 1] = tiles[p][:, u:u + 1]
        tile_s[base + p, :, HD + u:HD + u + 1] = tiles[p][:, HD + u:HD + u + 1]


def _scan2_fwd(per_dir, ps, kk):
    s = ps.shape[0]
    nc, ng = s // TC, TC // 8
    in_specs, operands, out_specs, out_shape = [], [], [], []
    for d in (0, 1):
        row, rowv = _scan_specs(d, nc, True)
        in_specs += [row] * 5 + [rowv]
        operands += list(per_dir[d]) + [ps, kk, ps]
        out_specs += [row, _bs((1, NPAIR, HD, 128), lambda c: (c, 0, 0, 0))]
        out_shape += [jax.ShapeDtypeStruct((s, RW), F32), jax.ShapeDtypeStruct((nc, NPAIR, HD, 128), F32)]

    def body(*refs):
        ins = [refs[0:6], refs[6:12]]
        y_refs, ck_refs = (refs[12], refs[14]), (refs[13], refs[15])
        st, vt_s, yt_s, stage = refs[16:]

        @pl.when(pl.program_id(0) == 0)
        def _():
            st[...] = jnp.zeros_like(st)
            yt_s[...] = jnp.zeros_like(yt_s)
            stage[...] = jnp.zeros_like(stage)

        for d in (0, 1):
            ck_refs[d][0] = st[d * NPAIR:(d + 1) * NPAIR]
        ones2 = _ones2()
        ones1 = ones2[0:128]
        lane_u = lax.broadcasted_iota(jnp.int32, (HD, 128), 1) % HD
        pc = [slice(p * 128, (p + 1) * 128) for p in range(NPAIR)]

        def group(gi, carry):
            gs = (gi, ng - 1 - gi)
            rows8 = [pl.ds(pl.multiple_of(gs[d] * 8, 8), 8) for d in (0, 1)]
            blk = [[q[rows8[d], :] for q in ins[d][:5]] for d in (0, 1)]
            for d in (0, 1):
                _rows_to_tiles(ins[d][5], rows8[d], stage, vt_s, d * NPAIR)
            ss = [[st[d * NPAIR + p] for p in range(NPAIR)] for d in (0, 1)]
            for ui in range(9):
                us, ups = (ui, 7 - ui), (ui - 1, 8 - ui)
                lhs1, where = [], {}
                for d in (0, 1):
                    if ui < 8:
                        where["sa", d] = len(lhs1) // NPAIR
                        lhs1 += [(ss[d][p] * blk[d][4][us[d]:us[d] + 1, pc[p]]).astype(BF16) for p in range(NPAIR)]
                        where["vb", d] = len(lhs1) // NPAIR
                        for p in range(NPAIR):
                            vt = vt_s[d * NPAIR + p]
                            lhs1.append(jnp.where(lane_u == us[d], vt, jnp.zeros_like(vt)))
                    if ui > 0:
                        where["y", d] = len(lhs1) // NPAIR
                        lhs1 += [(ss[d][p] * blk[d][3][ups[d]:ups[d] + 1, pc[p]]).astype(BF16) for p in range(NPAIR)]
                res1 = jnp.dot(jnp.concatenate(lhs1, axis=0), ones1, preferred_element_type=F32)
                for d in (0, 1):
                    d8, k8, b8, _, _ = blk[d]
                    u = us[d]
                    if ui < 8:
                        sa, vb = _tiles(res1, where["sa", d]), _tiles(res1, where["vb", d])
                        for p in range(NPAIR):
                            ss[d][p] = (ss[d][p] * d8[u:u + 1, pc[p]] - sa[p] * b8[u:u + 1, pc[p]]
                                        + vb[p] * k8[u:u + 1, pc[p]])
                    if ui > 0:
                        _put_cols(yt_s, d * NPAIR, ups[d], _tiles(res1, where["y", d]))
            for d in (0, 1):
                _tiles_to_rows(yt_s, d * NPAIR, y_refs[d], rows8[d])
                for p in range(NPAIR):
                    st[d * NPAIR + p] = ss[d][p]
            return carry

        for gi in range(ng):
            group(gi, 0)

    outs = pl.pallas_call(
        body, name="rwkv_scan_fwd", grid=(nc,), in_specs=in_specs, out_specs=out_specs, out_shape=out_shape,
        scratch_shapes=[pltpu.VMEM((2 * NPAIR, HD, 128), F32), pltpu.VMEM((2 * NPAIR, HD, 128), BF16),
                        pltpu.VMEM((2 * NPAIR, HD, 128), F32), pltpu.VMEM((2 * NPAIR, 128, 128), F32)],
        compiler_params=_params(("arbitrary",)),
    )(*operands)
    return [(outs[0], outs[1]), (outs[2], outs[3])]


def _scan2_bwd(per_dir, ps, kk, dy):
    s = ps.shape[0]
    nc, ng = s // TC, TC // 8
    in_specs, operands, out_specs, out_shape = [], [], [], []
    for d in (0, 1):
        row, rowv = _scan_specs(d, nc, False)
        dec, kd, b, ck = per_dir[d]
        in_specs += [row] * 5 + [rowv, row, _bs((1, NPAIR, HD, 128), lambda c: (nc - 1 - c, 0, 0, 0))]
        operands += [dec, kd, b, ps, kk, ps, dy, ck]
        out_specs += [row] * 6
        out_shape += [jax.ShapeDtypeStruct((s, RW), F32)] * 6

    def body(*refs):
        ins = [refs[0:8], refs[8:16]]
        outs = [refs[16:22], refs[22:28]]
        st, sa_s, vb_s, dy_s, ds, vt_s, dyt_s, dvt_s, stage = refs[28:]

        @pl.when(pl.program_id(0) == 0)
        def _():
            dvt_s[...] = jnp.zeros_like(dvt_s)
            stage[...] = jnp.zeros_like(stage)
            ds[...] = jnp.zeros_like(ds)

        for d in (0, 1):
            st[d * (TC + 1)] = ins[d][7][0]
        ones2 = _ones2()
        ones1 = ones2[0:128]
        lane_u = lax.broadcasted_iota(jnp.int32, (HD, 128), 1) % HD
        row_id = lax.broadcasted_iota(jnp.int32, (8, 128), 0)
        pc = [slice(p * 128, (p + 1) * 128) for p in range(NPAIR)]

        def load_rows(gs):
            return [[q[pl.ds(pl.multiple_of(gs[d] * 8, 8), 8), :] for q in ins[d][:5]] for d in (0, 1)]

        def fgroup(gi, carry):
            gs = (gi, ng - 1 - gi)
            blk = load_rows(gs)
            for d in (0, 1):
                rows8 = pl.ds(pl.multiple_of(gs[d] * 8, 8), 8)
                _rows_to_tiles(ins[d][5], rows8, stage, vt_s, d * NPAIR)
                _rows_to_tiles(ins[d][6], rows8, stage, dyt_s, d * NPAIR)
            ss = [[st[d * (TC + 1) + gi * 8, p] for p in range(NPAIR)] for d in (0, 1)]
            for ui in range(8):
                us = (ui, 7 - ui)
                i = gi * 8 + ui
                lhs1 = []
                for d in (0, 1):
                    kk8 = blk[d][4]
                    lhs1 += [(ss[d][p] * kk8[us[d]:us[d] + 1, pc[p]]).astype(BF16) for p in range(NPAIR)]
                    for tile_s in (vt_s, dyt_s):
                        for p in range(NPAIR):
                            t = tile_s[d * NPAIR + p]
                            lhs1.append(jnp.where(lane_u == us[d], t, jnp.zeros_like(t)))
                res1 = jnp.dot(jnp.concatenate(lhs1, axis=0), ones1, preferred_element_type=F32)
                for d in (0, 1):
                    d8, k8, b8, _, _ = blk[d]
                    u = us[d]
                    sa, vb, dyb = _tiles(res1, 3 * d), _tiles(res1, 3 * d + 1), _tiles(res1, 3 * d + 2)
                    for p in range(NPAIR):
                        sa_s[d * TC + i, p] = sa[p]
                        vb_s[d * TC + i, p] = vb[p]
                        dy_s[d * TC + i, p] = dyb[p]
                        ss[d][p] = ss[d][p] * d8[u:u + 1, pc[p]] - sa[p] * b8[u:u + 1, pc[p]] + vb[p] * k8[u:u + 1, pc[p]]
                        st[d * (TC + 1) + i + 1, p] = ss[d][p]
            return carry

        for gi in range(ng):
            fgroup(gi, 0)

        def bgroup(gj, carry):
            gi = ng - 1 - gj
            gs = (gi, ng - 1 - gi)
            blk = load_rows(gs)
            dss = [[ds[d * NPAIR + p] for p in range(NPAIR)] for d in (0, 1)]
            acc = [[[jnp.zeros((8, 128), F32) for _ in range(5)] for _ in range(NPAIR)] for _ in (0, 1)]
            for uj in range(8):
                ui = 7 - uj
                us = (ui, 7 - ui)
                i = gi * 8 + ui
                lhs1, dyb = [], [None, None]
                for d in (0, 1):
                    _, k8, b8, r8, _ = blk[d]
                    u = us[d]
                    dyb[d] = [dy_s[d * TC + i, p] for p in range(NPAIR)]
                    for p in range(NPAIR):
                        dss[d][p] = dss[d][p] + dyb[d][p] * r8[u:u + 1, pc[p]]
                    lhs1 += [(dss[d][p] * b8[u:u + 1, pc[p]]).astype(BF16) for p in range(NPAIR)]
                    lhs1 += [(dss[d][p] * k8[u:u + 1, pc[p]]).astype(BF16) for p in range(NPAIR)]
                res1 = jnp.dot(jnp.concatenate(lhs1, axis=0), ones1, preferred_element_type=F32)
                for d in (0, 1):
                    d8, _, _, _, kk8 = blk[d]
                    u = us[d]
                    dsa, dvb = _tiles(res1, 2 * d), _tiles(res1, 2 * d + 1)
                    _put_cols(dvt_s, d * NPAIR, u, dvb)
                    for p in range(NPAIR):
                        sp, sn = st[d * (TC + 1) + i, p], st[d * (TC + 1) + i + 1, p]
                        dsv = dss[d][p]
                        vals = (jnp.sum(sn * dyb[d][p], axis=0, keepdims=True), jnp.sum(dsv * sp, axis=0, keepdims=True),
                                -jnp.sum(dsv * sa_s[d * TC + i, p], axis=0, keepdims=True),
                                jnp.sum(dsv * vb_s[d * TC + i, p], axis=0, keepdims=True),
                                -jnp.sum(sp * dsa[p], axis=0, keepdims=True))
                        acc[d][p] = [jnp.where(row_id == u, o, a_) for o, a_ in zip(vals, acc[d][p])]
                        dss[d][p] = dsv * d8[u:u + 1, pc[p]] - dsa[p] * kk8[u:u + 1, pc[p]]
            for d in (0, 1):
                rows8 = pl.ds(pl.multiple_of(gs[d] * 8, 8), 8)
                _tiles_to_rows(dvt_s, d * NPAIR, outs[d][5], rows8)
                for p in range(NPAIR):
                    ds[d * NPAIR + p] = dss[d][p]
                    for o_ref, a_ in zip(outs[d][:5], acc[d][p]):
                        o_ref[rows8, pc[p]] = a_
            return carry

        for gj in range(ng):
            bgroup(gj, 0)

    chunk = lambda k: pltpu.VMEM((k, NPAIR, HD, 128), F32)
    pairs = lambda w, dt: pltpu.VMEM((2 * NPAIR, HD, w), dt)
    res = pl.pallas_call(
        body, name="rwkv_scan_bwd", grid=(nc,), in_specs=in_specs, out_specs=out_specs, out_shape=out_shape,
        scratch_shapes=[chunk(2 * (TC + 1)), chunk(2 * TC), chunk(2 * TC), chunk(2 * TC), pairs(128, F32),
                        pairs(128, BF16), pairs(128, BF16), pairs(128, F32), pltpu.VMEM((2 * NPAIR, 128, 128), F32)],
        compiler_params=_params(("arbitrary",)),
    )(*operands)
    return [res[0:6], res[6:12]]


MT = 256
MN = 256


def _merge_fwd(ya, yr, yx, wa, wr, wx, proj, gate_b):
    s = ya.shape[0]

    def body(ya_ref, yr_ref, yx_ref, wa_ref, wr_ref, wx_ref, m0, m1, m2, b0, b1, b2, o_ref):
        acc = jnp.zeros((MT, MN), F32)
        for y_ref, w_ref, m_ref, b_ref in ((ya_ref, wa_ref, m0, b0), (yr_ref, wr_ref, m1, b1), (yx_ref, wx_ref, m2, b2)):
            u = _dot(y_ref[...], w_ref[...], ((1,), (0,)))
            acc = acc + jax.nn.sigmoid(m_ref[...] + b_ref[...]) * u
        o_ref[...] = acc.astype(BF16)

    mg = lambda br: _bs((MT, MN), lambda i, j: (i, C_MG // MN + br * (D // MN) + j))
    gb = lambda br: _bs((1, MN), lambda i, j: (0, br * (D // MN) + j))
    return pl.pallas_call(
        body, name="merge_fwd", grid=(s // MT, D // MN),
        in_specs=[_bs((MT, RW), lambda i, j: (i, 0)), _bs((MT, RW), lambda i, j: (i, 0)), _bs((MT, XW), lambda i, j: (i, 0)),
                  _bs((RW, MN), lambda i, j: (0, j)), _bs((RW, MN), lambda i, j: (0, j)), _bs((XW, MN), lambda i, j: (0, j)),
                  mg(0), mg(1), mg(2), gb(0), gb(1), gb(2)],
        out_specs=_bs((MT, MN), lambda i, j: (i, j)),
        out_shape=jax.ShapeDtypeStruct((s, D), BF16),
        compiler_params=_params(("parallel", "arbitrary")),
    )(ya, yr, yx, wa, wr, wx, proj, proj, proj, gate_b, gate_b, gate_b)


def _out_fwd(merged, w_out, x, target):
    s = x.shape[0]
    tm, tn = min(512, s), 512

    def body(m_ref, w_ref, x_ref, t_ref, loss_ref, d_ref, d16_ref):
        @pl.when((pl.program_id(0) == 0) & (pl.program_id(1) == 0))
        def _():
            loss_ref[...] = jnp.zeros_like(loss_ref)

        out = x_ref[...] + jnp.dot(m_ref[...], w_ref[...], preferred_element_type=F32)
        err = out - t_ref[...]
        dout = err * (1.0 / D)
        d_ref[...] = dout
        d16_ref[...] = dout.astype(BF16)
        loss_ref[...] += jnp.sum(err * err)

    tile = _bs((tm, tn), lambda i, j: (i, j))
    return pl.pallas_call(
        body, name="out_fwd", grid=(s // tm, D // tn),
        in_specs=[_bs((tm, D), lambda i, j: (i, 0)), _bs((D, tn), lambda i, j: (0, j)), tile, tile],
        out_specs=[_bs((8, 128), lambda i, j: (0, 0)), tile, tile],
        out_shape=[jax.ShapeDtypeStruct((8, 128), F32), jax.ShapeDtypeStruct((s, D), F32),
                   jax.ShapeDtypeStruct((s, D), BF16)],
        compiler_params=_params(("arbitrary", "arbitrary")),
    )(merged, w_out, x, target)


def _merge_bwd(ya, yr, yx, wa, wr, wx, proj, gate_b, dmerged):
    s = ya.shape[0]

    def body(ya_ref, yr_ref, yx_ref, wa_ref, wr_ref, wx_ref, m0, m1, m2, b0, b1, b2, dm_ref,
             dg0, dg1, dg2, du0, du1, du2, dya_ref, dyr_ref, dyx_ref):
        @pl.when(pl.program_id(1) == 0)
        def _():
            dya_ref[...] = jnp.zeros_like(dya_ref)
            dyr_ref[...] = jnp.zeros_like(dyr_ref)
            dyx_ref[...] = jnp.zeros_like(dyx_ref)

        dm = dm_ref[...]
        for y_ref, w_ref, m_ref, b_ref, dg_ref, du_ref, dy_ref in (
                (ya_ref, wa_ref, m0, b0, dg0, du0, dya_ref), (yr_ref, wr_ref, m1, b1, dg1, du1, dyr_ref),
                (yx_ref, wx_ref, m2, b2, dg2, du2, dyx_ref)):
            w = w_ref[...]
            u = _dot(y_ref[...], w, ((1,), (0,)))
            gt = jax.nn.sigmoid(m_ref[...] + b_ref[...])
            dg_ref[...] = (dm * u * gt * (1.0 - gt)).astype(BF16)
            du = (dm * gt).astype(BF16)
            du_ref[...] = du
            dy_ref[...] += _dot(du, w, ((1,), (1,)))

    mg = lambda br: _bs((MT, MN), lambda i, j: (i, C_MG // MN + br * (D // MN) + j))
    gb = lambda br: _bs((1, MN), lambda i, j: (0, br * (D // MN) + j))
    tile = _bs((MT, MN), lambda i, j: (i, j))
    return pl.pallas_call(
        body, name="merge_bwd", grid=(s // MT, D // MN),
        in_specs=[_bs((MT, RW), lambda i, j: (i, 0)), _bs((MT, RW), lambda i, j: (i, 0)), _bs((MT, XW), lambda i, j: (i, 0)),
                  _bs((RW, MN), lambda i, j: (0, j)), _bs((RW, MN), lambda i, j: (0, j)), _bs((XW, MN), lambda i, j: (0, j)),
                  mg(0), mg(1), mg(2), gb(0), gb(1), gb(2), tile],
        out_specs=[tile] * 6 + [_bs((MT, RW), lambda i, j: (i, 0)), _bs((MT, RW), lambda i, j: (i, 0)),
                                _bs((MT, XW), lambda i, j: (i, 0))],
        out_shape=[jax.ShapeDtypeStruct((s, D), BF16)] * 6 + [jax.ShapeDtypeStruct((s, RW), F32),
                                                               jax.ShapeDtypeStruct((s, RW), F32),
                                                               jax.ShapeDtypeStruct((s, XW), F32)],
        compiler_params=_params(("parallel", "arbitrary")),
    )(ya, yr, yx, wa, wr, wx, proj, proj, proj, gate_b, gate_b, gate_b, dmerged)


def _colsum(a, name):
    m, n = a.shape
    tm, tn = min(512, m), 512

    def body(a_ref, o_ref):
        @pl.when(pl.program_id(1) == 0)
        def _():
            o_ref[...] = jnp.zeros_like(o_ref)

        o_ref[...] += jnp.sum(a_ref[...].astype(F32), axis=0, keepdims=True)

    return pl.pallas_call(
        body, name=name, grid=(n // tn, m // tm),
        in_specs=[_bs((tm, tn), lambda j, i: (i, j))], out_specs=_bs((1, tn), lambda j, i: (0, j)),
        out_shape=jax.ShapeDtypeStruct((1, n), F32),
        compiler_params=_params(("parallel", "arbitrary")),
    )(a)


def _in_bwd(dproj, w_in, x, g, dout):
    s = x.shape[0]
    tm, tk = min(512, s), 896
    nk = NIN // tk

    def body(dp_ref, w_ref, x_ref, g_ref, do_ref, gx_ref, gg_ref, acc):
        i, kk = pl.program_id(0), pl.program_id(1)

        @pl.when((i == 0) & (kk == 0))
        def _():
            gg_ref[...] = jnp.zeros_like(gg_ref)

        @pl.when(kk == 0)
        def _():
            acc[...] = jnp.zeros_like(acc)

        acc[...] += _dot(dp_ref[...], w_ref[...], ((1,), (1,)))

        @pl.when(kk == nk - 1)
        def _():
            xv, dh, gv = x_ref[...], acc[...], g_ref[...]
            r = lax.rsqrt(jnp.mean(xv * xv, axis=-1, keepdims=True) + NORM_EPS)
            xn = xv * r
            gg_ref[...] += jnp.sum(dh * xn, axis=0, keepdims=True)
            dxn = dh * gv
            dx = r * (dxn - xn * jnp.mean(dxn * xn, axis=-1, keepdims=True))
            gx_ref[...] = do_ref[...] + dx

    return pl.pallas_call(
        body, name="in_bwd", grid=(s // tm, nk),
        in_specs=[_bs((tm, tk), lambda i, kk: (i, kk)), _bs((D, tk), lambda i, kk: (0, kk)),
                  _bs((tm, D), lambda i, kk: (i, 0)), _bs((1, D), lambda i, kk: (0, 0)), _bs((tm, D), lambda i, kk: (i, 0))],
        out_specs=[_bs((tm, D), lambda i, kk: (i, 0)), _bs((1, D), lambda i, kk: (0, 0))],
        out_shape=[jax.ShapeDtypeStruct((s, D), F32), jax.ShapeDtypeStruct((1, D), F32)],
        scratch_shapes=[pltpu.VMEM((tm, D), F32)],
        compiler_params=_params(("arbitrary", "arbitrary")),
    )(dproj, w_in, x, g, dout)


def _adamw_math(w, g, m, v):
    m = ADAM_B1 * m + (1.0 - ADAM_B1) * g
    v = ADAM_B2 * v + (1.0 - ADAM_B2) * jnp.square(g)
    m_hat = m / (1.0 - ADAM_B1 ** ADAM_STEP)
    v_hat = v / (1.0 - ADAM_B2 ** ADAM_STEP)
    delta = -ADAM_LR * (m_hat / (jnp.sqrt(v_hat) + ADAM_EPS) + ADAM_WD * w)
    return delta, m, v


def _adamw(parts, w, m, v, name):
    rows, cols = w.shape
    tr = rows
    for cand in (256, 128, 64, 32, 16, 8):
        if rows % cand == 0 and cand * cols * 4 <= (1 << 20):
            tr = cand
            break
    n = len(parts)

    def body(*refs):
        g = refs[0][...].astype(F32)
        for r in refs[1:n]:
            g = g + r[...].astype(F32)
        w_ref, m_ref, v_ref, g_out, d_out, m_out, v_out = refs[n:]
        delta, m_new, v_new = _adamw_math(w_ref[...], g, m_ref[...], v_ref[...])
        g_out[...] = g
        d_out[...] = delta
        m_out[...] = m_new
        v_out[...] = v_new

    spec = _bs((tr, cols), lambda i: (i, 0))
    return pl.pallas_call(
        body, name=name, grid=(rows // tr,),
        in_specs=[spec] * (n + 3), out_specs=[spec] * 4,
        out_shape=[jax.ShapeDtypeStruct((rows, cols), F32)] * 4,
        compiler_params=_params(("parallel",)),
    )(*parts, w, m, v)


def _adamw_halves(mine, theirs, core, w, m, v, name):
    rows, cols = w.shape
    h = rows // 2
    tr = next(t for t in (256, 128, 64, 32, 16, 8) if h % t == 0 and t * cols * 4 <= (1 << 20))
    nt = h // tr

    def body(core_ref, mine_ref, theirs_ref, w_ref, m_ref, v_ref, g_out, d_out, m_out, v_out):
        is_mine = pl.program_id(0) // nt == core_ref[0]
        g = jnp.where(is_mine, mine_ref[...], theirs_ref[...])
        delta, m_new, v_new = _adamw_math(w_ref[...], g, m_ref[...], v_ref[...])
        g_out[...] = g
        d_out[...] = delta
        m_out[...] = m_new
        v_out[...] = v_new

    spec = _bs((tr, cols), lambda i, core_ref: (i, 0))
    return pl.pallas_call(
        body, name=name,
        grid_spec=pltpu.PrefetchScalarGridSpec(
            num_scalar_prefetch=1, grid=(2 * nt,),
            in_specs=[_bs((tr, cols), lambda i, core_ref: (jnp.clip(i - core_ref[0] * nt, 0, nt - 1), 0)),
                      _bs((tr, cols), lambda i, core_ref: (jnp.clip(i - (1 - core_ref[0]) * nt, 0, nt - 1), 0)),
                      spec, spec, spec],
            out_specs=[spec] * 4),
        out_shape=[jax.ShapeDtypeStruct((rows, cols), F32)] * 4,
        compiler_params=_params(("parallel",)),
    )(core, mine, theirs, w, m, v)


def _sum_parts(parts, name):
    rows, cols = parts[0].shape
    tr = rows
    for cand in (256, 128, 64, 32, 16, 8):
        if rows % cand == 0 and cand * cols * 4 <= (1 << 20):
            tr = cand
            break

    def body(*refs):
        acc = refs[0][...].astype(F32)
        for r in refs[1:-1]:
            acc = acc + r[...].astype(F32)
        refs[-1][...] = acc

    spec = _bs((tr, cols), lambda i: (i, 0))
    return pl.pallas_call(
        body, name=name, grid=(rows // tr,), in_specs=[spec] * len(parts), out_specs=spec,
        out_shape=jax.ShapeDtypeStruct((rows, cols), F32), compiler_params=_params(("parallel",)),
    )(*parts)


ANY = pl.BlockSpec(memory_space=pl.ANY)


def _other_chips(x, y):
    return [(1 - x, y), (x, 1 - y), (1 - x, 1 - y)]


def _gather_shards(arrays, name):
    n = len(arrays)

    def body(*refs):
        ins, outs = refs[:n], refs[n:2 * n]
        ici_send, ici_recv, d2d_send, d2d_recv, local_sems, own_recv = refs[2 * n:]
        x, y, c = lax.axis_index("x"), lax.axis_index("y"), lax.axis_index("c")
        me = 2 * x + y
        chips = _other_chips(x, y)

        def half(i, who):
            h = arrays[i].shape[0] // 2
            return pl.ds(who * h, h)

        def ici(i, j, src_chip, to):
            return pltpu.make_async_remote_copy(
                src_ref=ins[i].at[half(i, c)], dst_ref=outs[i].at[src_chip, half(i, c)], send_sem=ici_send.at[3 * i + j],
                recv_sem=ici_recv.at[3 * i + j], device_id=to, device_id_type=MESH)

        def d2d(i, j, src_chip, who):
            piece = outs[i].at[src_chip, half(i, who)]
            return pltpu.make_async_remote_copy(
                src_ref=piece, dst_ref=piece, send_sem=d2d_send.at[3 * i + j], recv_sem=d2d_recv.at[3 * i + j],
                device_id=(x, y, 1 - c), device_id_type=MESH)

        def own(i):
            return pltpu.make_async_remote_copy(
                src_ref=ins[i], dst_ref=outs[i].at[me], send_sem=local_sems.at[i], recv_sem=own_recv.at[i],
                device_id=(x, y, 1 - c), device_id_type=MESH)

        sends = []
        for i in range(n):
            cp = own(i)
            cp.start()
            sends.append(cp)
            for j, (px, py) in enumerate(chips):
                rc = ici(i, j, me, (px, py, c))
                rc.start()
                sends.append(rc)
        for i in range(n):
            for j, (px, py) in enumerate(chips):
                ici(i, j, 2 * px + py, (px, py, c)).wait_recv()
                fw = d2d(i, j, 2 * px + py, c)
                fw.start()
                sends.append(fw)
        for i in range(n):
            for j, (px, py) in enumerate(chips):
                d2d(i, j, 2 * px + py, 1 - c).wait_recv()
            own(i).wait_recv()
        for rc in sends:
            rc.wait_send()

    dma = lambda k: pltpu.SemaphoreType.DMA((k,))
    return pl.pallas_call(
        body, name=name, in_specs=[ANY] * n, out_specs=[ANY] * n,
        out_shape=[jax.ShapeDtypeStruct((4,) + a.shape, a.dtype) for a in arrays],
        scratch_shapes=[dma(3 * n), dma(3 * n), dma(3 * n), dma(3 * n), dma(n), dma(n)],
        compiler_params=pltpu.CompilerParams(has_side_effects=True),
    )(*arrays)


def _scatter_shards(stacks, name):
    n = len(stacks)

    def body(*refs):
        ins, outs = refs[:n], refs[n:2 * n]
        send_sems, recv_sems = refs[2 * n:]
        x, y, c = lax.axis_index("x"), lax.axis_index("y"), lax.axis_index("c")
        chips = _other_chips(x, y)
        sends = []
        for i in range(n):
            for j, (px, py) in enumerate(chips):
                rc = pltpu.make_async_remote_copy(
                    src_ref=ins[i].at[2 * px + py], dst_ref=outs[i].at[j], send_sem=send_sems.at[3 * i + j],
                    recv_sem=recv_sems.at[3 * i + j], device_id=(px, py, c), device_id_type=MESH)
                rc.start()
                sends.append(rc)
        for rc in sends:
            rc.wait_recv()
        for rc in sends:
            rc.wait_send()

    return pl.pallas_call(
        body, name=name, in_specs=[ANY] * n, out_specs=[ANY] * n,
        out_shape=[jax.ShapeDtypeStruct((3,) + a.shape[1:], a.dtype) for a in stacks],
        scratch_shapes=[pltpu.SemaphoreType.DMA((3 * n,)), pltpu.SemaphoreType.DMA((3 * n,))],
        compiler_params=pltpu.CompilerParams(has_side_effects=True),
    )(*stacks)


def _pair_exchange(stacks, name):
    n = len(stacks)

    def body(*refs):
        ins, outs = refs[:n], refs[n:2 * n]
        send_sems, recv_sems = refs[2 * n:]
        x, y, c = lax.axis_index("x"), lax.axis_index("y"), lax.axis_index("c")
        cps = []
        for i in range(n):
            h = stacks[i].shape[1] // 2
            rc = pltpu.make_async_remote_copy(
                src_ref=ins[i].at[:, pl.ds((1 - c) * h, h)], dst_ref=outs[i], send_sem=send_sems.at[i],
                recv_sem=recv_sems.at[i], device_id=(x, y, 1 - c), device_id_type=MESH)
            rc.start()
            cps.append(rc)
        for rc in cps:
            rc.wait_recv()
        for rc in cps:
            rc.wait_send()

    return pl.pallas_call(
        body, name=name, in_specs=[ANY] * n, out_specs=[ANY] * n,
        out_shape=[jax.ShapeDtypeStruct((4, a.shape[1] // 2) + a.shape[2:], a.dtype) for a in stacks],
        scratch_shapes=[pltpu.SemaphoreType.DMA((n,)), pltpu.SemaphoreType.DMA((n,))],
        compiler_params=pltpu.CompilerParams(has_side_effects=True),
    )(*stacks)


def _pair_sum(own, theirs, core, name):
    _, r, cols = own.shape
    h = r // 2
    tr = next(t for t in (256, 128, 64, 32, 16) if h % t == 0 and t * cols * 4 <= (1 << 20))
    nt = h // tr

    def body(core_ref, own_ref, th_ref, o32_ref, o16_ref):
        del core_ref
        acc = own_ref[...] + th_ref[...].astype(F32)
        o32_ref[...] = acc
        o16_ref[...] = acc.astype(BF16)

    out = _bs((1, tr, cols), lambda j, t, core_ref: (j, t, 0))
    return pl.pallas_call(
        body, name=name,
        grid_spec=pltpu.PrefetchScalarGridSpec(
            num_scalar_prefetch=1, grid=(4, nt),
            in_specs=[_bs((1, tr, cols), lambda j, t, core_ref: (j, core_ref[0] * nt + t, 0)), out],
            out_specs=[out, out]),
        out_shape=[jax.ShapeDtypeStruct((4, h, cols), F32), jax.ShapeDtypeStruct((4, h, cols), BF16)],
        compiler_params=_params(("parallel", "parallel")),
    )(core, own, theirs)


def _swap_sibling(arrays, name):
    n = len(arrays)

    def body(*refs):
        ins, outs = refs[:n], refs[n:2 * n]
        send_sems, recv_sems = refs[2 * n:]
        sib = (lax.axis_index("x"), lax.axis_index("y"), 1 - lax.axis_index("c"))
        cps = []
        for i in range(n):
            rc = pltpu.make_async_remote_copy(src_ref=ins[i], dst_ref=outs[i], send_sem=send_sems.at[i],
                                              recv_sem=recv_sems.at[i], device_id=sib, device_id_type=MESH)
            rc.start()
            cps.append(rc)
        for rc in cps:
            rc.wait_recv()
        for rc in cps:
            rc.wait_send()

    return pl.pallas_call(
        body, name=name, in_specs=[ANY] * n, out_specs=[ANY] * n,
        out_shape=[jax.ShapeDtypeStruct(a.shape, a.dtype) for a in arrays],
        scratch_shapes=[pltpu.SemaphoreType.DMA((n,)), pltpu.SemaphoreType.DMA((n,))],
        compiler_params=pltpu.CompilerParams(has_side_effects=True),
    )(*arrays)


def _all_reduce_small(v):
    rows = v.shape[0]

    def body(v_ref, o_ref, buf, send_sems, recv_sems):
        x, y, c = lax.axis_index("x"), lax.axis_index("y"), lax.axis_index("c")
        me = 4 * x + 2 * y + c
        buf[me] = v_ref[...]
        cps = []
        for kbits in range(1, 8):
            bx, by, bc = (kbits >> 2) & 1, (kbits >> 1) & 1, kbits & 1
            px = jnp.where(bx == 1, 1 - x, x)
            py = jnp.where(by == 1, 1 - y, y)
            pc = jnp.where(bc == 1, 1 - c, c)
            rc = pltpu.make_async_remote_copy(src_ref=v_ref, dst_ref=buf.at[me], send_sem=send_sems.at[kbits - 1],
                                              recv_sem=recv_sems.at[kbits - 1], device_id=(px, py, pc),
                                              device_id_type=MESH)
            rc.start()
            cps.append((rc, 4 * px + 2 * py + pc))
        for kbits, (rc, src) in enumerate(cps):
            pltpu.make_async_remote_copy(src_ref=v_ref, dst_ref=buf.at[src], send_sem=send_sems.at[kbits],
                                         recv_sem=recv_sems.at[kbits], device_id=(x, y, c),
                                         device_id_type=MESH).wait_recv()
        for rc, _ in cps:
            rc.wait_send()
        acc = buf[0]
        for d in range(1, 8):
            acc = acc + buf[d]
        o_ref[...] = acc

    return pl.pallas_call(
        body, name="all_reduce_small",
        in_specs=[pl.BlockSpec(memory_space=pltpu.VMEM)], out_specs=pl.BlockSpec(memory_space=pltpu.VMEM),
        out_shape=jax.ShapeDtypeStruct((rows, 128), F32),
        scratch_shapes=[pltpu.VMEM((8, rows, 128), F32), pltpu.SemaphoreType.DMA((7,)), pltpu.SemaphoreType.DMA((7,))],
        compiler_params=pltpu.CompilerParams(has_side_effects=True, vmem_limit_bytes=VMEM_LIMIT),
    )(v)


def _rope_tables(s):
    half = HD // 2
    inv = 10000.0 ** (-jnp.arange(half, dtype=F32) / half)
    ang = jnp.arange(s, dtype=F32)[:, None] * inv[None, :]
    cos, sin = jnp.cos(ang), jnp.sin(ang)
    return jnp.concatenate([cos, cos], axis=1), jnp.concatenate([sin, sin], axis=1)


def _local_step(x, mem, target, norm_g, mem_norm_g, w_in, gate_b, gq, gk, sink, wa, mu, k_k, k_a, r_k, w0, w2, a0, a2,
                ln_w, ln_b, wr, w_kv, gxq, gxk, wx, w_out):
    s = x.shape[0]
    cos, sin = _rope_tables(s)
    r_k = r_k.reshape(1, RW)

    proj, h = _proj_fwd(x, norm_g, w_in)
    ya = _attn_fwd(proj, cos, sin, gq, gk, sink)
    mkv, mn = _mem_kv(mem, mem_norm_g, w_kv)
    yx = _xattn_fwd(proj, mkv, gxq, gxk)
    ps = _shift_fwd(proj, mu)
    kk, dec0, kd0, b0, dec1, kd1, b1 = _pre_fwd(ps, k_k, k_a, w0, w2, a0, a2)
    (y0, ck0), (y1, ck1) = _scan2_fwd([(dec0, kd0, b0), (dec1, kd1, b1)], ps, kk)
    yr = _post_fwd(y0, y1, ps, kd0, kd1, proj, r_k, ln_w, ln_b)
    merged = _merge_fwd(ya, yr, yx, wa, wr, wx, proj, gate_b)
    loss_tile, dout, dout16 = _out_fwd(merged, w_out, x, target)
    loss_sum = loss_tile[0, 0]

    g = {}
    t16 = lambda a: a.astype(BF16).T
    sk = min(1024, s)
    dmerged = _matmul(dout16, w_out, mode="nt", m=s, n=D, k=D, tm=sk, tn=1024, tk=1024, name="dmerged")
    g["w_out"] = _matmul(merged.T, dout16, mode="nn", m=D, n=D, k=s, tm=1024, tn=1024, tk=sk, name="grad_w_out")
    dg0, dg1, dg2, du0, du1, du2, dya, dyr, dyx = _merge_bwd(ya, yr, yx, wa, wr, wx, proj, gate_b, dmerged)
    g["attn_w_o"] = _matmul(t16(ya), du0, mode="nn", m=RW, n=D, k=s, tm=RW, tn=1024, tk=s, name="grad_attn_w_o")
    g["rwkv_w_o"] = _matmul(t16(yr), du1, mode="nn", m=RW, n=D, k=s, tm=RW, tn=1024, tk=s, name="grad_rwkv_w_o")
    g["x_w_o"] = _matmul(t16(yx), du2, mode="nn", m=XW, n=D, k=s, tm=XW, tn=1024, tk=s, name="grad_x_w_o")
    dmg = jnp.concatenate([dg0, dg1, dg2], axis=1)
    g["gate_b"] = _colsum(dmg, "grad_gate_b")

    daq, dak, dav, dag, g["attn_q_norm_g"], g["attn_k_norm_g"], g["attn_sink"] = _attn_bwd(proj, cos, sin, gq, gk, sink, dya)

    dxq, dxg, dmkv, g["x_q_norm_g"], g["x_k_norm_g"] = _xattn_bwd(proj, mkv, gxq, gxk, dyx)
    g["x_w_kv"] = _matmul(mn, dmkv, mode="tn", m=D, n=2 * XW, k=NMEM, tm=512, tn=512, tk=NMEM, name="grad_x_w_kv")
    dmn = _matmul(dmkv, w_kv, mode="nt", m=NMEM, n=D, k=2 * XW, tm=NMEM, tn=512, tk=2 * XW, name="dmn")
    g["mem_norm_g"] = _mem_bwd(mem, mem_norm_g, dmn)

    dys, dr_p, dv_p, dkd0_p, dkd1_p, drg, g["rwkv_r_k"], g["rwkv_ln_w"], g["rwkv_ln_b"] = _post_bwd(
        y0, y1, ps, kd0, kd1, proj, r_k, ln_w, ln_b, dyr)
    (dr0, dd0, db0, dk0, dkk0, dv0), (dr1, dd1, db1, dk1, dkk1, dv1) = _scan2_bwd(
        [(dec0, kd0, b0, ck0), (dec1, kd1, b1, ck1)], ps, kk, dys)
    dr = dr_p + dr0 + dr1
    dv = dv_p + dv0 + dv1
    cts = (dkk0 + dkk1, dd0, dk0 + dkd0_p, db0, dd1, dk1 + dkd1_p, db1)
    dps, g["rwkv_k_k"], g["rwkv_k_a"], g["rwkv_w0"], g["rwkv_w2"], g["rwkv_a0"], g["rwkv_a2"] = _pre_bwd(
        ps, k_k, k_a, w0, w2, a0, a2, dr, dv, cts)
    drs, g["rwkv_mu"] = _shift_bwd(proj, mu, dps)

    dproj = jnp.concatenate([daq.astype(BF16), dak.astype(BF16), dav.astype(BF16), dag.astype(BF16), drs.astype(BF16),
                             drg.astype(BF16), dxq.astype(BF16), dxg.astype(BF16), dmg], axis=1)
    dproj4 = jnp.stack([dproj[:, j * (NIN // 4):(j + 1) * (NIN // 4)] for j in range(4)])
    g["w_in"], g["w_in_bf16"] = _grad_w_in(h.T, dproj4)
    grad_x, g["norm_g"] = _in_bwd(dproj, w_in, x, norm_g, dout)
    g["rwkv_r_k"] = g["rwkv_r_k"].reshape(AH, HD)
    return loss_sum, grad_x, g


WEIGHTS = ['norm_g', 'mem_norm_g', 'w_in', 'gate_b', 'attn_q_norm_g', 'attn_k_norm_g', 'attn_sink', 'attn_w_o',
           'rwkv_mu', 'rwkv_k_k', 'rwkv_k_a', 'rwkv_r_k', 'rwkv_w0', 'rwkv_w2', 'rwkv_a0', 'rwkv_a2', 'rwkv_ln_w',
           'rwkv_ln_b', 'rwkv_w_o', 'x_w_kv', 'x_q_norm_g', 'x_k_norm_g', 'x_w_o', 'w_out']
BIG = ['w_in', 'attn_w_o', 'rwkv_w_o', 'x_w_kv', 'x_w_o', 'w_out']
COL_SHARDED = ['w_in', 'attn_w_o', 'rwkv_w_o', 'x_w_o']
LORA = ['rwkv_w0', 'rwkv_w2', 'rwkv_a0', 'rwkv_a2']
SMALL = [n for n in WEIGHTS if n not in BIG]


def _unshard_cols(stack):
    return jnp.concatenate([stack[i] for i in range(4)], axis=-1)


def _shard_cols(full):
    w = full.shape[-1] // 4
    return [full[..., i * w:(i + 1) * w] for i in range(4)]


def kernel(x, mem, norm_g, mem_norm_g, w_in, gate_b, attn_q_norm_g, attn_k_norm_g, attn_sink, attn_w_o, rwkv_mu, rwkv_k_k, rwkv_k_a, rwkv_r_k, rwkv_w0, rwkv_w2, rwkv_a0, rwkv_a2, rwkv_ln_w, rwkv_ln_b, rwkv_w_o, x_w_kv, x_q_norm_g, x_k_norm_g, x_w_o, w_out, loss_target, m_norm_g, m_mem_norm_g, m_w_in, m_gate_b, m_attn_q_norm_g, m_attn_k_norm_g, m_attn_sink, m_attn_w_o, m_rwkv_mu, m_rwkv_k_k, m_rwkv_k_a, m_rwkv_r_k, m_rwkv_w0, m_rwkv_w2, m_rwkv_a0, m_rwkv_a2, m_rwkv_ln_w, m_rwkv_ln_b, m_rwkv_w_o, m_x_w_kv, m_x_q_norm_g, m_x_k_norm_g, m_x_w_o, m_w_out, v_norm_g, v_mem_norm_g, v_w_in, v_gate_b, v_attn_q_norm_g, v_attn_k_norm_g, v_attn_sink, v_attn_w_o, v_rwkv_mu, v_rwkv_k_k, v_rwkv_k_a, v_rwkv_r_k, v_rwkv_w0, v_rwkv_w2, v_rwkv_a0, v_rwkv_a2, v_rwkv_ln_w, v_rwkv_ln_b, v_rwkv_w_o, v_x_w_kv, v_x_q_norm_g, v_x_k_norm_g, v_x_w_o, v_w_out):
    args = dict(locals())
    canon = lambda a: a[0] if a.ndim > 2 else a
    w = {n: canon(args[n]) for n in WEIGHTS}
    m = {n: canon(args["m_" + n]) for n in WEIGHTS}
    v = {n: canon(args["v_" + n]) for n in WEIGHTS}
    shard = 2 * lax.axis_index("x") + lax.axis_index("y")

    local = [w[n].astype(BF16) for n in BIG] + [w[n].reshape(2, -1, w[n].shape[-1]) for n in LORA]
    stacks = dict(zip(BIG + LORA, _gather_shards(local, "gather_weights")))
    full = {}
    for n in COL_SHARDED:
        full[n] = _unshard_cols(stacks[n])
    for n in LORA:
        full[n] = _unshard_cols(stacks[n]).reshape(w[n].shape[:-1] + (RW,))
    full["x_w_kv"] = stacks["x_w_kv"].reshape(D, 2 * XW)
    full["w_out"] = stacks["w_out"].reshape(D, D)

    loss_sum, grad_x, g = _local_step(
        x[0], mem[0], loss_target[0], w["norm_g"], w["mem_norm_g"], full["w_in"], w["gate_b"], w["attn_q_norm_g"],
        w["attn_k_norm_g"], w["attn_sink"], full["attn_w_o"], w["rwkv_mu"], w["rwkv_k_k"], w["rwkv_k_a"], w["rwkv_r_k"],
        full["rwkv_w0"], full["rwkv_w2"], full["rwkv_a0"], full["rwkv_a2"], w["rwkv_ln_w"], w["rwkv_ln_b"],
        full["rwkv_w_o"], full["x_w_kv"], w["x_q_norm_g"], w["x_k_norm_g"], full["x_w_o"], full["w_out"])

    loss = lax.psum(0.5 * loss_sum / D, ("x", "y", "c"))

    def as_stack(n, dtype):
        if n == "w_in":
            return g["w_in"] if dtype == F32 else g["w_in_bf16"]
        if n in COL_SHARDED:
            return jnp.stack([p.astype(dtype) for p in _shard_cols(g[n])])
        return g[n].reshape((4, g[n].shape[0] // 4) + g[n].shape[1:]).astype(dtype)

    core = lax.axis_index("c").astype(jnp.int32).reshape(1)
    sibling = _pair_exchange([as_stack(n, BF16) for n in BIG], "pair_exchange")
    pair32, pair16 = [], []
    for n, th in zip(BIG, sibling):
        a32, a16 = _pair_sum(as_stack(n, F32), th, core, "pair_sum_" + n)
        pair32.append(a32)
        pair16.append(a16)
    recv = _scatter_shards(pair16, "scatter_grads")
    halves = []
    for n, p32, r in zip(BIG, pair32, recv):
        own = lax.dynamic_index_in_dim(p32, shard, 0, keepdims=False)
        halves.append(_sum_parts([own, r[0], r[1], r[2]], "sum_" + n))
    other_halves = _swap_sibling(halves, "swap_halves")

    out_g, out_d, out_m, out_v = {}, {}, {}, {}
    for n, mine, theirs in zip(BIG, halves, other_halves):
        out_g[n], out_d[n], out_m[n], out_v[n] = _adamw_halves(mine, theirs, core, w[n], m[n], v[n], "adamw_" + n)

    flat = jnp.concatenate([g[n].reshape(-1) for n in SMALL])
    total = flat.shape[0]
    padded = -(-total // 1024) * 1024
    flat = jnp.pad(flat, (0, padded - total)).reshape(padded // 128, 128)
    red = _all_reduce_small(flat).reshape(-1)
    off = 0
    gs = {}
    for n in SMALL:
        size = g[n].size
        t = red[off:off + size].reshape(g[n].shape)
        off += size
        if n in LORA:
            wd = t.shape[-1] // 4
            t = lax.dynamic_slice_in_dim(t, shard * wd, wd, axis=t.ndim - 1)
        gs[n] = t

    def pack(d):
        f = jnp.concatenate([d[n].reshape(-1) for n in SMALL])
        return jnp.pad(f, (0, -(-f.shape[0] // 1024) * 1024 - f.shape[0])).reshape(-1, 128)

    pg, pd, pm, pv = _adamw([pack(gs)], pack(w), pack(m), pack(v), "adamw_small")
    off = 0
    for n in SMALL:
        size = w[n].size
        for dst, src in ((out_g, pg), (out_d, pd), (out_m, pm), (out_v, pv)):
            dst[n] = src.reshape(-1)[off:off + size].reshape(w[n].shape)
        off += size

    lead = lambda d: [d[n][None] if args[n].ndim > 2 else d[n] for n in WEIGHTS]
    return (loss, grad_x[None], *lead(out_g), *lead(out_d), *lead(out_m), *lead(out_v))
```

```python
import functools

import jax
import jax.numpy as jnp
from jax import lax
from jax.experimental import pallas as pl
from jax.experimental.pallas import tpu as pltpu

F32 = jnp.float32
BF16 = jnp.bfloat16
HI = lax.Precision.HIGH
MESH = pl.DeviceIdType.MESH

D = 2048
NMEM = 256
NORM_EPS = 1e-6
NEG_INF = -1e30
GN_EPS = 64e-5
HD = 64
AH = 12
AKV = 4
RW = 768
XH = 4
XD = 128
XW = 512
NIN = 12544
RSW = 2560
C_AQ, C_AK, C_AV, C_AG, C_RS, C_RG, C_XQ, C_XG, C_MG = 0, 768, 1024, 1280, 2048, 4608, 5376, 5888, 6400
WIN = 384
QB = 128
TC = 16
NPAIR = 6

ADAM_LR, ADAM_B1, ADAM_B2, ADAM_EPS, ADAM_WD, ADAM_STEP = 0.001, 0.9, 0.999, 1e-08, 0.01, 10

VMEM_LIMIT = 56 * 1024 * 1024


def _bs(shape, imap):
    return pl.BlockSpec(shape, imap)


def _params(sem=None, vmem=VMEM_LIMIT):
    return pltpu.CompilerParams(dimension_semantics=sem, vmem_limit_bytes=vmem)


def _dot(a, b, dims):
    return lax.dot_general(a.astype(BF16), b.astype(BF16), (dims, ((), ())), preferred_element_type=F32)


@jax.custom_vjp
def _mm_nn(a, b):
    return _dot(a, b, ((1,), (0,)))


def _mm_nn_fwd(a, b):
    return _mm_nn(a, b), (a, b)


def _mm_nn_bwd(res, ct):
    a, b = res
    return _dot(ct, b, ((1,), (1,))), _dot(a, ct, ((0,), (0,)))


_mm_nn.defvjp(_mm_nn_fwd, _mm_nn_bwd)


@jax.custom_vjp
def _mm_nt(a, b):
    return _dot(a, b, ((1,), (1,)))


def _mm_nt_fwd(a, b):
    return _mm_nt(a, b), (a, b)


def _mm_nt_bwd(res, ct):
    a, b = res
    return _dot(ct, b, ((1,), (0,))), _dot(ct, a, ((0,), (0,)))


_mm_nt.defvjp(_mm_nt_fwd, _mm_nt_bwd)


def _seg_matrix(n, seg):
    r = lax.broadcasted_iota(jnp.int32, (n, n), 0) // seg
    c = lax.broadcasted_iota(jnp.int32, (n, n), 1) // seg
    return (r == c).astype(F32)


def _rot_matrix():
    r = lax.broadcasted_iota(jnp.int32, (HD, HD), 0)
    c = lax.broadcasted_iota(jnp.int32, (HD, HD), 1)
    return jnp.where(c == r + HD // 2, 1.0, 0.0).astype(F32) - jnp.where(c == r - HD // 2, 1.0, 0.0).astype(F32)


def _hdot(a, m):
    return jnp.dot(a, m, precision=HI, preferred_element_type=F32)


def _rms(t, g):
    return t * lax.rsqrt(jnp.mean(t * t, axis=-1, keepdims=True) + NORM_EPS) * g


def _silu(t):
    return t * jax.nn.sigmoid(t)


def _softplus(z):
    return jnp.maximum(z, 0.0) + jnp.log(1.0 + jnp.exp(-jnp.abs(z)))


def _matmul(a, b, *, mode, m, n, k, tm, tn, tk, name, a_off=(0, 0), b_off=(0, 0), out_dtype=F32):
    nk = k // tk
    if mode == "tn":
        a_spec = _bs((tk, tm), lambda i, j, kk: (kk + a_off[0], i + a_off[1]))
        dims = ((0,), (0,))
    else:
        a_spec = _bs((tm, tk), lambda i, j, kk: (i + a_off[0], kk + a_off[1]))
        dims = ((1,), (1,)) if mode == "nt" else ((1,), (0,))
    if mode == "nt":
        b_spec = _bs((tn, tk), lambda i, j, kk: (j + b_off[0], kk + b_off[1]))
    else:
        b_spec = _bs((tk, tn), lambda i, j, kk: (kk + b_off[0], j + b_off[1]))

    def body(a_ref, b_ref, o_ref, acc):
        kk = pl.program_id(2)

        @pl.when(kk == 0)
        def _():
            acc[...] = jnp.zeros_like(acc)

        acc[...] += _dot(a_ref[...], b_ref[...], dims)

        @pl.when(kk == nk - 1)
        def _():
            o_ref[...] = acc[...].astype(out_dtype)

    return pl.pallas_call(
        body, name=name, grid=(m // tm, n // tn, nk),
        in_specs=[a_spec, b_spec], out_specs=_bs((tm, tn), lambda i, j, kk: (i, j)),
        out_shape=jax.ShapeDtypeStruct((m, n), out_dtype),
        scratch_shapes=[pltpu.VMEM((tm, tn), F32)],
        compiler_params=_params(("parallel", "parallel", "arbitrary")),
    )(a, b)


def _grad_w_in(ht, dproj4):
    s = ht.shape[1]
    ws = NIN // 4
    tm, tk = 512, min(1024, s)
    nk = s // tk

    def body(a_ref, b_ref, o32_ref, o16_ref, acc):
        kk = pl.program_id(2)

        @pl.when(kk == 0)
        def _():
            acc[...] = jnp.zeros_like(acc)

        acc[...] += jnp.dot(a_ref[...], b_ref[0], preferred_element_type=F32)

        @pl.when(kk == nk - 1)
        def _():
            o32_ref[0] = acc[...]
            o16_ref[0] = acc[...].astype(BF16)

    out = _bs((1, tm, ws), lambda j, i, kk: (j, i, 0))
    return pl.pallas_call(
        body, name="grad_w_in", grid=(4, D // tm, nk),
        in_specs=[_bs((tm, tk), lambda j, i, kk: (i, kk)), _bs((1, tk, ws), lambda j, i, kk: (j, kk, 0))],
        out_specs=[out, out],
        out_shape=[jax.ShapeDtypeStruct((4, D, ws), F32), jax.ShapeDtypeStruct((4, D, ws), BF16)],
        scratch_shapes=[pltpu.VMEM((tm, ws), F32)],
        compiler_params=_params(("parallel", "parallel", "arbitrary")),
    )(ht, dproj4)


def _proj_fwd(x, g, w):
    s = x.shape[0]
    tm, tn = min(512, s), 896

    def body(x_ref, g_ref, w_ref, o_ref, h_ref, hs):
        @pl.when(pl.program_id(1) == 0)
        def _():
            h = _rms(x_ref[...], g_ref[...]).astype(BF16)
            hs[...] = h
            h_ref[...] = h

        o_ref[...] = jnp.dot(hs[...], w_ref[...], preferred_element_type=F32)

    return pl.pallas_call(
        body, name="proj_fwd", grid=(s // tm, NIN // tn),
        in_specs=[_bs((tm, D), lambda i, j: (i, 0)), _bs((1, D), lambda i, j: (0, 0)), _bs((D, tn), lambda i, j: (0, j))],
        out_specs=[_bs((tm, tn), lambda i, j: (i, j)), _bs((tm, D), lambda i, j: (i, 0))],
        out_shape=[jax.ShapeDtypeStruct((s, NIN), F32), jax.ShapeDtypeStruct((s, D), BF16)],
        scratch_shapes=[pltpu.VMEM((tm, D), BF16)],
        compiler_params=_params(("parallel", "arbitrary")),
    )(x, g, w)


def _rope(t, cos, sin, rot):
    return t * cos + _hdot(t, rot) * sin


def _attn_tile(qs, ks, vs, gs, sinks, gq, gk, cq, sq, ck, sk, mask, rot):
    outs = []
    for hk in range(AKV):
        kh = _rope(_rms(ks[hk], gk), ck, sk, rot)
        for g in range(AH // AKV):
            h = hk * (AH // AKV) + g
            qh = _rope(_rms(qs[h], gq), cq, sq, rot)
            sc = _mm_nt(qh, kh) * (HD ** -0.5)
            sc = jnp.where(mask, sc, NEG_INF)
            mx = lax.stop_gradient(jnp.maximum(jnp.max(sc, axis=-1, keepdims=True), sinks[h]))
            p = jnp.exp(sc - mx)
            den = jnp.sum(p, axis=-1, keepdims=True) + jnp.exp(sinks[h] - mx)
            o = _mm_nn(p / den, vs[hk])
            outs.append(o * _silu(gs[h]))
    return outs


def _attn_load(n, s, aq_ref, ak_ref, av_ref, ag_refs, cos_ref, sin_ref, sink_ref):
    start = pl.multiple_of(jnp.clip((n - 1) * QB, 0, s - WIN), QB)
    q0 = pl.multiple_of(n * QB, QB)
    qs = [aq_ref[:, h * HD:(h + 1) * HD] for h in range(AH)]
    ks = [ak_ref[pl.ds(start, WIN), h * HD:(h + 1) * HD] for h in range(AKV)]
    vs = [av_ref[pl.ds(start, WIN), h * HD:(h + 1) * HD] for h in range(AKV)]
    gs = [ag_refs[h // 4][:, (h % 4) * HD:(h % 4 + 1) * HD] for h in range(AH)]
    sinks = [sink_ref[0:1, h:h + 1] for h in range(AH)]
    cq, sq = cos_ref[pl.ds(q0, QB), :], sin_ref[pl.ds(q0, QB), :]
    ck, sk = cos_ref[pl.ds(start, WIN), :], sin_ref[pl.ds(start, WIN), :]
    qpos = q0 + lax.broadcasted_iota(jnp.int32, (QB, WIN), 0)
    kpos = start + lax.broadcasted_iota(jnp.int32, (QB, WIN), 1)
    mask = jnp.abs(kpos - qpos) <= QB
    return start, qs, ks, vs, gs, sinks, cq, sq, ck, sk, mask


def _attn_specs(s):
    return [
        _bs((QB, 768), lambda n: (n, 0)),
        _bs((s, 256), lambda n: (0, C_AK // 256)),
        _bs((s, 256), lambda n: (0, C_AV // 256)),
        _bs((QB, 256), lambda n: (n, C_AG // 256)),
        _bs((QB, 256), lambda n: (n, C_AG // 256 + 1)),
        _bs((QB, 256), lambda n: (n, C_AG // 256 + 2)),
        _bs((s, HD), lambda n: (0, 0)),
        _bs((s, HD), lambda n: (0, 0)),
        _bs((1, HD), lambda n: (0, 0)),
        _bs((1, HD), lambda n: (0, 0)),
        _bs((1, AH), lambda n: (0, 0)),
    ]


def _attn_fwd(proj, cos, sin, gq, gk, sink):
    s = proj.shape[0]

    def body(aq_ref, ak_ref, av_ref, ag0, ag1, ag2, cos_ref, sin_ref, gq_ref, gk_ref, sink_ref, o_ref):
        n = pl.program_id(0)
        _, qs, ks, vs, gs, sinks, cq, sq, ck, sk, mask = _attn_load(
            n, s, aq_ref, ak_ref, av_ref, (ag0, ag1, ag2), cos_ref, sin_ref, sink_ref)
        outs = _attn_tile(qs, ks, vs, gs, sinks, gq_ref[...], gk_ref[...], cq, sq, ck, sk, mask, _rot_matrix())
        for h in range(AH):
            o_ref[:, h * HD:(h + 1) * HD] = outs[h]

    return pl.pallas_call(
        body, name="attn_fwd", grid=(s // QB,),
        in_specs=_attn_specs(s), out_specs=_bs((QB, 768), lambda n: (n, 0)),
        out_shape=jax.ShapeDtypeStruct((s, 768), F32),
        compiler_params=_params(("arbitrary",)),
    )(proj, proj, proj, proj, proj, proj, cos, sin, gq, gk, sink)


def _attn_bwd(proj, cos, sin, gq, gk, sink, dy):
    s = proj.shape[0]

    def body(aq_ref, ak_ref, av_ref, ag0, ag1, ag2, cos_ref, sin_ref, gq_ref, gk_ref, sink_ref, dy_ref,
             daq_ref, dak_ref, dav_ref, dag_ref, dgq_ref, dgk_ref, dsink_ref):
        n = pl.program_id(0)

        @pl.when(n == 0)
        def _():
            dak_ref[...] = jnp.zeros_like(dak_ref)
            dav_ref[...] = jnp.zeros_like(dav_ref)
            dgq_ref[...] = jnp.zeros_like(dgq_ref)
            dgk_ref[...] = jnp.zeros_like(dgk_ref)
            dsink_ref[...] = jnp.zeros_like(dsink_ref)

        start, qs, ks, vs, gs, sinks, cq, sq, ck, sk, mask = _attn_load(
            n, s, aq_ref, ak_ref, av_ref, (ag0, ag1, ag2), cos_ref, sin_ref, sink_ref)
        rot = _rot_matrix()

        def f(qs, ks, vs, gs, sinks, gq, gk):
            return _attn_tile(qs, ks, vs, gs, sinks, gq, gk, cq, sq, ck, sk, mask, rot)

        _, vjp = jax.vjp(f, qs, ks, vs, gs, sinks, gq_ref[...], gk_ref[...])
        dys = [dy_ref[:, h * HD:(h + 1) * HD] for h in range(AH)]
        dqs, dks, dvs, dgs, dsinks, dgq, dgk = vjp(dys)
        for h in range(AH):
            daq_ref[:, h * HD:(h + 1) * HD] = dqs[h]
            dag_ref[:, h * HD:(h + 1) * HD] = dgs[h]
            dsink_ref[0:1, h:h + 1] += dsinks[h]
        for h in range(AKV):
            dak_ref[pl.ds(start, WIN), h * HD:(h + 1) * HD] += dks[h]
            dav_ref[pl.ds(start, WIN), h * HD:(h + 1) * HD] += dvs[h]
        dgq_ref[...] += dgq
        dgk_ref[...] += dgk

    whole = lambda shape: _bs(shape, lambda n: (0, 0))
    return pl.pallas_call(
        body, name="attn_bwd", grid=(s // QB,),
        in_specs=_attn_specs(s) + [_bs((QB, 768), lambda n: (n, 0))],
        out_specs=[_bs((QB, 768), lambda n: (n, 0)), whole((s, 256)), whole((s, 256)), _bs((QB, 768), lambda n: (n, 0)),
                   whole((1, HD)), whole((1, HD)), whole((1, AH))],
        out_shape=[jax.ShapeDtypeStruct((s, 768), F32), jax.ShapeDtypeStruct((s, 256), F32),
                   jax.ShapeDtypeStruct((s, 256), F32), jax.ShapeDtypeStruct((s, 768), F32),
                   jax.ShapeDtypeStruct((1, HD), F32), jax.ShapeDtypeStruct((1, HD), F32),
                   jax.ShapeDtypeStruct((1, AH), F32)],
        compiler_params=_params(("arbitrary",)),
    )(proj, proj, proj, proj, proj, proj, cos, sin, gq, gk, sink, dy)


def _mem_kv(mem, g, w):
    def body(m_ref, g_ref, w_ref, o_ref, mn_ref):
        mn = _rms(m_ref[...], g_ref[...]).astype(BF16)
        mn_ref[...] = mn
        o_ref[...] = jnp.dot(mn, w_ref[...], preferred_element_type=F32)

    return pl.pallas_call(
        body, name="mem_kv",
        out_shape=[jax.ShapeDtypeStruct((NMEM, 2 * XW), F32), jax.ShapeDtypeStruct((NMEM, D), BF16)],
        compiler_params=_params(),
    )(mem, g, w)


def _xattn_tile(qs, gs, kms, vms, gxq, gxk):
    outs = []
    for h in range(XH):
        q = _rms(qs[h], gxq)
        km = _rms(kms[h], gxk)
        sc = _mm_nt(q, km) * (XD ** -0.5)
        mx = lax.stop_gradient(jnp.max(sc, axis=-1, keepdims=True))
        p = jnp.exp(sc - mx)
        p = p / jnp.sum(p, axis=-1, keepdims=True)
        outs.append(_mm_nn(p, vms[h]) * _silu(gs[h]))
    return outs


XT = 256


def _xattn_specs():
    return [
        _bs((XT, 256), lambda i: (i, C_XQ // 256)), _bs((XT, 256), lambda i: (i, C_XQ // 256 + 1)),
        _bs((XT, 256), lambda i: (i, C_XG // 256)), _bs((XT, 256), lambda i: (i, C_XG // 256 + 1)),
        _bs((NMEM, 2 * XW), lambda i: (0, 0)),
        _bs((1, XD), lambda i: (0, 0)), _bs((1, XD), lambda i: (0, 0)),
    ]


def _xattn_load(q0, q1, g0, g1, mkv_ref):
    qs = [(q0, q1)[h // 2][:, (h % 2) * XD:(h % 2 + 1) * XD] for h in range(XH)]
    gs = [(g0, g1)[h // 2][:, (h % 2) * XD:(h % 2 + 1) * XD] for h in range(XH)]
    kms = [mkv_ref[:, h * XD:(h + 1) * XD] for h in range(XH)]
    vms = [mkv_ref[:, XW + h * XD:XW + (h + 1) * XD] for h in range(XH)]
    return qs, gs, kms, vms


def _xattn_fwd(proj, mkv, gxq, gxk):
    s = proj.shape[0]

    def body(q0, q1, g0, g1, mkv_ref, gxq_ref, gxk_ref, o_ref):
        qs, gs, kms, vms = _xattn_load(q0, q1, g0, g1, mkv_ref)
        outs = _xattn_tile(qs, gs, kms, vms, gxq_ref[...], gxk_ref[...])
        for h in range(XH):
            o_ref[:, h * XD:(h + 1) * XD] = outs[h]

    return pl.pallas_call(
        body, name="xattn_fwd", grid=(s // XT,),
        in_specs=_xattn_specs(), out_specs=_bs((XT, XW), lambda i: (i, 0)),
        out_shape=jax.ShapeDtypeStruct((s, XW), F32),
        compiler_params=_params(("arbitrary",)),
    )(proj, proj, proj, proj, mkv, gxq, gxk)


def _xattn_bwd(proj, mkv, gxq, gxk, dy):
    s = proj.shape[0]

    def body(q0, q1, g0, g1, mkv_ref, gxq_ref, gxk_ref, dy_ref, dq_ref, dg_ref, dmkv_ref, dgxq_ref, dgxk_ref):
        @pl.when(pl.program_id(0) == 0)
        def _():
            dmkv_ref[...] = jnp.zeros_like(dmkv_ref)
            dgxq_ref[...] = jnp.zeros_like(dgxq_ref)
            dgxk_ref[...] = jnp.zeros_like(dgxk_ref)

        qs, gs, kms, vms = _xattn_load(q0, q1, g0, g1, mkv_ref)
        _, vjp = jax.vjp(_xattn_tile, qs, gs, kms, vms, gxq_ref[...], gxk_ref[...])
        dqs, dgs, dkms, dvms, dgxq, dgxk = vjp([dy_ref[:, h * XD:(h + 1) * XD] for h in range(XH)])
        for h in range(XH):
            dq_ref[:, h * XD:(h + 1) * XD] = dqs[h]
            dg_ref[:, h * XD:(h + 1) * XD] = dgs[h]
            dmkv_ref[:, h * XD:(h + 1) * XD] += dkms[h]
            dmkv_ref[:, XW + h * XD:XW + (h + 1) * XD] += dvms[h]
        dgxq_ref[...] += dgxq
        dgxk_ref[...] += dgxk

    whole = lambda shape: _bs(shape, lambda i: (0, 0))
    return pl.pallas_call(
        body, name="xattn_bwd", grid=(s // XT,),
        in_specs=_xattn_specs() + [_bs((XT, XW), lambda i: (i, 0))],
        out_specs=[_bs((XT, XW), lambda i: (i, 0)), _bs((XT, XW), lambda i: (i, 0)), whole((NMEM, 2 * XW)),
                   whole((1, XD)), whole((1, XD))],
        out_shape=[jax.ShapeDtypeStruct((s, XW), F32), jax.ShapeDtypeStruct((s, XW), F32),
                   jax.ShapeDtypeStruct((NMEM, 2 * XW), F32), jax.ShapeDtypeStruct((1, XD), F32),
                   jax.ShapeDtypeStruct((1, XD), F32)],
        compiler_params=_params(("arbitrary",)),
    )(proj, proj, proj, proj, mkv, gxq, gxk, dy)


def _mem_bwd(mem, g, dmn):
    def body(m_ref, dmn_ref, o_ref):
        m = m_ref[...]
        r = lax.rsqrt(jnp.mean(m * m, axis=-1, keepdims=True) + NORM_EPS)
        o_ref[...] = jnp.sum(dmn_ref[...] * m * r, axis=0, keepdims=True)

    del g
    return pl.pallas_call(body, name="mem_norm_bwd", out_shape=jax.ShapeDtypeStruct((1, D), F32),
                          compiler_params=_params())(mem, dmn)


SHIFT_W = 512


def _shift_rows(p, s):
    row = lax.broadcasted_iota(jnp.int32, p.shape, 0)
    prev = jnp.where(row == 0, 0.0, pltpu.roll(p, 1, 0))
    nxt = jnp.where(row == s - 1, 0.0, pltpu.roll(p, s - 1, 0))
    return prev, nxt


def _shift_fwd(proj, mu):
    s = proj.shape[0]

    def body(p_ref, mu_ref, o_ref):
        p = p_ref[...]
        prev, nxt = _shift_rows(p, s)
        o_ref[...] = p + mu_ref[...] * (0.5 * (prev + nxt) - p)

    return pl.pallas_call(
        body, name="shift_fwd", grid=(RSW // SHIFT_W,),
        in_specs=[_bs((s, SHIFT_W), lambda j: (0, C_RS // SHIFT_W + j)), _bs((1, SHIFT_W), lambda j: (0, j))],
        out_specs=_bs((s, SHIFT_W), lambda j: (0, j)),
        out_shape=jax.ShapeDtypeStruct((s, RSW), F32),
        compiler_params=_params(("parallel",)),
    )(proj, mu)


def _shift_bwd(proj, mu, dps):
    s = proj.shape[0]

    def body(p_ref, mu_ref, g_ref, o_ref, dmu_ref):
        p, g, mu_v = p_ref[...], g_ref[...], mu_ref[...]
        prev, nxt = _shift_rows(p, s)
        dmu_ref[...] = jnp.sum(g * (0.5 * (prev + nxt) - p), axis=0, keepdims=True)
        mg = mu_v * g
        down, up = _shift_rows(mg, s)
        o_ref[...] = g * (1.0 - mu_v) + 0.5 * (down + up)

    return pl.pallas_call(
        body, name="shift_bwd", grid=(RSW // SHIFT_W,),
        in_specs=[_bs((s, SHIFT_W), lambda j: (0, C_RS // SHIFT_W + j)), _bs((1, SHIFT_W), lambda j: (0, j)),
                  _bs((s, SHIFT_W), lambda j: (0, j))],
        out_specs=[_bs((s, SHIFT_W), lambda j: (0, j)), _bs((1, SHIFT_W), lambda j: (0, j))],
        out_shape=[jax.ShapeDtypeStruct((s, RSW), F32), jax.ShapeDtypeStruct((1, RSW), F32)],
        compiler_params=_params(("parallel",)),
    )(proj, mu, dps)


def _pre_tile(k, wf, wb, af, ab, k_k, k_a, w0s, w2s, a0s, a2s, seg):
    kx = k * k_k
    ss = _hdot(kx * kx, seg)
    kk = kx / jnp.maximum(jnp.sqrt(ss), 1e-12)
    outs = [kk]
    for d, (w_in, a_in) in enumerate(((wf, af), (wb, ab))):
        z = w0s[d] + _mm_nn(jnp.tanh(w_in), w2s[d])
        wd = -_softplus(-z) - 0.5
        dec = jnp.exp(-jnp.exp(wd))
        ad = jax.nn.sigmoid(a0s[d] + _mm_nn(a_in, a2s[d]))
        kd = k * (1.0 + (ad - 1.0) * k_a)
        outs += [dec, kd, kk * ad]
    return outs


PT = 256


def _pre_load(ps_ref, kk_ref, ka_ref, w0_ref, w2_ref, a0_ref, a2_ref):
    k = ps_ref[:, RW:2 * RW]
    wf, wb = ps_ref[:, 3 * RW:3 * RW + 64], ps_ref[:, 3 * RW + 64:3 * RW + 128]
    af, ab = ps_ref[:, 3 * RW + 128:3 * RW + 192], ps_ref[:, 3 * RW + 192:3 * RW + 256]
    w0s = [w0_ref[0:1, :], w0_ref[1:2, :]]
    a0s = [a0_ref[0:1, :], a0_ref[1:2, :]]
    w2s = [w2_ref[0], w2_ref[1]]
    a2s = [a2_ref[0], a2_ref[1]]
    return (k, wf, wb, af, ab, kk_ref[...], ka_ref[...], w0s, w2s, a0s, a2s)


def _pre_specs():
    c = lambda shape: _bs(shape, lambda i: tuple(0 for _ in shape))
    return [_bs((PT, RSW), lambda i: (i, 0)), c((1, RW)), c((1, RW)), c((2, RW)), c((2, 64, RW)), c((2, RW)),
            c((2, 64, RW))]


def _pre_fwd(ps, k_k, k_a, w0, w2, a0, a2):
    s = ps.shape[0]

    def body(ps_ref, kk_ref, ka_ref, w0_ref, w2_ref, a0_ref, a2_ref, *outs):
        args = _pre_load(ps_ref, kk_ref, ka_ref, w0_ref, w2_ref, a0_ref, a2_ref)
        res = _pre_tile(*args, _seg_matrix(RW, HD))
        for o_ref, v in zip(outs, res):
            o_ref[...] = v

    return pl.pallas_call(
        body, name="rwkv_pre_fwd", grid=(s // PT,),
        in_specs=_pre_specs(), out_specs=[_bs((PT, RW), lambda i: (i, 0))] * 7,
        out_shape=[jax.ShapeDtypeStruct((s, RW), F32)] * 7,
        compiler_params=_params(("parallel",)),
    )(ps, k_k, k_a, w0, w2, a0, a2)


def _pre_bwd(ps, k_k, k_a, w0, w2, a0, a2, dr, dv, cts):
    s = ps.shape[0]

    def body(ps_ref, kk_ref, ka_ref, w0_ref, w2_ref, a0_ref, a2_ref, dr_ref, dv_ref, c0, c1, c2, c3, c4, c5, c6,
             dps_ref, dkk_ref, dka_ref, dw0_ref, dw2_ref, da0_ref, da2_ref):
        @pl.when(pl.program_id(0) == 0)
        def _():
            for r in (dkk_ref, dka_ref, dw0_ref, dw2_ref, da0_ref, da2_ref):
                r[...] = jnp.zeros_like(r)

        args = _pre_load(ps_ref, kk_ref, ka_ref, w0_ref, w2_ref, a0_ref, a2_ref)
        seg = _seg_matrix(RW, HD)
        _, vjp = jax.vjp(lambda *a: _pre_tile(*a, seg), *args)
        dk, dwf, dwb, daf, dab, dk_k, dk_a, dw0s, dw2s, da0s, da2s = vjp([c[...] for c in (c0, c1, c2, c3, c4, c5, c6)])
        dps_ref[:, 0:RW] = dr_ref[...]
        dps_ref[:, RW:2 * RW] = dk
        dps_ref[:, 2 * RW:3 * RW] = dv_ref[...]
        for j, t in enumerate((dwf, dwb, daf, dab)):
            dps_ref[:, 3 * RW + 64 * j:3 * RW + 64 * (j + 1)] = t
        dkk_ref[...] += dk_k
        dka_ref[...] += dk_a
        for d in range(2):
            dw0_ref[d:d + 1, :] += dw0s[d]
            da0_ref[d:d + 1, :] += da0s[d]
            dw2_ref[d] += dw2s[d]
            da2_ref[d] += da2s[d]

    c = lambda shape: _bs(shape, lambda i: tuple(0 for _ in shape))
    row = _bs((PT, RW), lambda i: (i, 0))
    return pl.pallas_call(
        body, name="rwkv_pre_bwd", grid=(s // PT,),
        in_specs=_pre_specs() + [row] * 9,
        out_specs=[_bs((PT, RSW), lambda i: (i, 0)), c((1, RW)), c((1, RW)), c((2, RW)), c((2, 64, RW)), c((2, RW)),
                   c((2, 64, RW))],
        out_shape=[jax.ShapeDtypeStruct((s, RSW), F32), jax.ShapeDtypeStruct((1, RW), F32),
                   jax.ShapeDtypeStruct((1, RW), F32), jax.ShapeDtypeStruct((2, RW), F32),
                   jax.ShapeDtypeStruct((2, 64, RW), F32), jax.ShapeDtypeStruct((2, RW), F32),
                   jax.ShapeDtypeStruct((2, 64, RW), F32)],
        compiler_params=_params(("arbitrary",)),
    )(ps, k_k, k_a, w0, w2, a0, a2, dr, dv, *cts)


def _post_tile(y0, y1, r, v, kd0, kd1, rg, r_k, ln_w, ln_b, seg):
    ysum = y0 + y1
    bonus = (_hdot(r * kd0 * r_k, seg) + _hdot(r * kd1 * r_k, seg)) * v
    mean = _hdot(ysum, seg) * (1.0 / HD)
    cen = ysum - mean
    var = _hdot(cen * cen, seg) * (1.0 / HD)
    y = cen * lax.rsqrt(var + GN_EPS) * ln_w + ln_b + bonus
    return y * _silu(rg)


def _post_specs():
    row = _bs((PT, RW), lambda i: (i, 0))
    c = _bs((1, RW), lambda i: (0, 0))
    return [row, row, _bs((PT, RW), lambda i: (i, 0)), _bs((PT, RW), lambda i: (i, 2)), row, row,
            _bs((PT, RW), lambda i: (i, C_RG // RW)), c, c, c]


def _post_fwd(y0, y1, ps, kd0, kd1, proj, r_k, ln_w, ln_b):
    s = ps.shape[0]

    def body(y0_ref, y1_ref, r_ref, v_ref, kd0_ref, kd1_ref, rg_ref, rk_ref, lw_ref, lb_ref, o_ref):
        o_ref[...] = _post_tile(y0_ref[...], y1_ref[...], r_ref[...], v_ref[...], kd0_ref[...], kd1_ref[...],
                                rg_ref[...], rk_ref[...], lw_ref[...], lb_ref[...], _seg_matrix(RW, HD))

    return pl.pallas_call(
        body, name="rwkv_post_fwd", grid=(s // PT,),
        in_specs=_post_specs(), out_specs=_bs((PT, RW), lambda i: (i, 0)),
        out_shape=jax.ShapeDtypeStruct((s, RW), F32),
        compiler_params=_params(("parallel",)),
    )(y0, y1, ps, ps, kd0, kd1, proj, r_k, ln_w, ln_b)


def _post_bwd(y0, y1, ps, kd0, kd1, proj, r_k, ln_w, ln_b, dy):
    s = ps.shape[0]

    def body(y0_ref, y1_ref, r_ref, v_ref, kd0_ref, kd1_ref, rg_ref, rk_ref, lw_ref, lb_ref, dy_ref,
             dys_ref, dr_ref, dv_ref, dkd0_ref, dkd1_ref, drg_ref, drk_ref, dlw_ref, dlb_ref):
        @pl.when(pl.program_id(0) == 0)
        def _():
            for r in (drk_ref, dlw_ref, dlb_ref):
                r[...] = jnp.zeros_like(r)

        seg = _seg_matrix(RW, HD)
        args = [t[...] for t in (y0_ref, y1_ref, r_ref, v_ref, kd0_ref, kd1_ref, rg_ref, rk_ref, lw_ref, lb_ref)]
        _, vjp = jax.vjp(lambda *a: _post_tile(*a, seg), *args)
        dy0, _, dr, dv, dkd0, dkd1, drg, drk, dlw, dlb = vjp(dy_ref[...])
        dys_ref[...] = dy0
        dr_ref[...] = dr
        dv_ref[...] = dv
        dkd0_ref[...] = dkd0
        dkd1_ref[...] = dkd1
        drg_ref[...] = drg
        drk_ref[...] += drk
        dlw_ref[...] += dlw
        dlb_ref[...] += dlb

    row = _bs((PT, RW), lambda i: (i, 0))
    c = _bs((1, RW), lambda i: (0, 0))
    return pl.pallas_call(
        body, name="rwkv_post_bwd", grid=(s // PT,),
        in_specs=_post_specs() + [row], out_specs=[row] * 6 + [c] * 3,
        out_shape=[jax.ShapeDtypeStruct((s, RW), F32)] * 6 + [jax.ShapeDtypeStruct((1, RW), F32)] * 3,
        compiler_params=_params(("arbitrary",)),
    )(y0, y1, ps, ps, kd0, kd1, proj, r_k, ln_w, ln_b, dy)


def _ones2():
    r = lax.broadcasted_iota(jnp.int32, (256, 128), 0) % 128 // HD
    c = lax.broadcasted_iota(jnp.int32, (256, 128), 1) // HD
    return (r == c).astype(BF16)


def _split(p):
    hi = p.astype(BF16)
    lo = (p - hi.astype(F32)).astype(BF16)
    return jnp.concatenate([hi, lo], axis=1)


def _to_t8(a):
    s = a.shape[0]
    t = a.reshape(s // 8, 8, NPAIR, 2, HD).transpose(0, 2, 4, 3, 1)
    t = jnp.pad(t, ((0, 0), (0, 0), (0, 0), (0, 0), (0, HD - 8))).reshape(s // 8, NPAIR, HD, 128)
    hi = t.astype(BF16)
    lo = (t - hi.astype(F32)).astype(BF16)
    return jnp.concatenate([hi, lo], axis=-1)


def _from_t8(t8):
    g = t8.shape[0]
    t = t8.reshape(g, NPAIR, HD, 2, HD)[..., :8]
    return t.transpose(0, 4, 1, 3, 2).reshape(g * 8, RW)


def _scan_specs(direction, nc, fwd_order):
    def tb(c):
        sc = c if fwd_order else nc - 1 - c
        return sc if direction == 0 else nc - 1 - sc

    row = _bs((TC, RW), lambda c: (tb(c), 0))
    rowv = _bs((TC, RW), lambda c: (tb(c), 2))
    return row, rowv


def _put_t8(ref, g, u, tiles):
    for p in range(NPAIR):
        ref[g, p, :, u:u + 1] = tiles[p][:, u:u + 1]
        ref[g, p, :, HD + u:HD + u + 1] = tiles[p][:, HD + u:HD + u + 1]


def _scan_fwd(dec, kd, b, ps, kk, vl, direction):
    s = dec.shape[0]
    nc, ng = s // TC, TC // 8
    row, t8_in, t8_out = _scan_specs(direction, nc, True)
    n = NPAIR * HD

    def body(dec_ref, kd_ref, b_ref, r_ref, kk_ref, vl_ref, y8_ref, ck_ref, st):
        @pl.when(pl.program_id(0) == 0)
        def _():
            st[...] = jnp.zeros_like(st)

        ck_ref[0] = st[...]
        ones2 = _ones2()
        lane_u = lax.broadcasted_iota(jnp.int32, (HD, 256), 1) % HD
        tiles = lambda res, k: [res[k * n + p * HD:k * n + (p + 1) * HD] for p in range(NPAIR)]

        def group(gi, carry):
            g = gi if direction == 0 else ng - 1 - gi
            rows8 = pl.ds(pl.multiple_of(g * 8, 8), 8)
            d8, k8, b8, r8, kk8 = (q[rows8, :] for q in (dec_ref, kd_ref, b_ref, r_ref, kk_ref))
            pc = [slice(p * 128, (p + 1) * 128) for p in range(NPAIR)]
            ss = [st[p] for p in range(NPAIR)]
            u_prev = None
            for ui in range(8):
                u = ui if direction == 0 else 7 - ui
                lhs = [_split(ss[p] * kk8[u:u + 1, pc[p]]) for p in range(NPAIR)]
                for p in range(NPAIR):
                    vt = vl_ref[g, p]
                    lhs.append(jnp.where(lane_u == u, vt, jnp.zeros_like(vt)))
                if u_prev is not None:
                    lhs += [_split(ss[p] * r8[u_prev:u_prev + 1, pc[p]]) for p in range(NPAIR)]
                res = jnp.dot(jnp.concatenate(lhs, axis=0), ones2, preferred_element_type=F32)
                if u_prev is not None:
                    _put_t8(y8_ref, g, u_prev, tiles(res, 2))
                sa, vb = tiles(res, 0), tiles(res, 1)
                for p in range(NPAIR):
                    ss[p] = ss[p] * d8[u:u + 1, pc[p]] - sa[p] * b8[u:u + 1, pc[p]] + vb[p] * k8[u:u + 1, pc[p]]
                u_prev = u
            lhs = [_split(ss[p] * r8[u_prev:u_prev + 1, pc[p]]) for p in range(NPAIR)]
            res = jnp.dot(jnp.concatenate(lhs, axis=0), ones2, preferred_element_type=F32)
            _put_t8(y8_ref, g, u_prev, tiles(res, 0))
            for p in range(NPAIR):
                st[p] = ss[p]
            return carry

        lax.fori_loop(0, ng, group, 0)

    return pl.pallas_call(
        body, name=f"rwkv_scan_fwd{direction}", grid=(nc,),
        in_specs=[row, row, row, row, row, t8_in],
        out_specs=[t8_out, _bs((1, NPAIR, HD, 128), lambda c: (c, 0, 0, 0))],
        out_shape=[jax.ShapeDtypeStruct((s // 8, NPAIR, HD, 128), F32),
                   jax.ShapeDtypeStruct((nc, NPAIR, HD, 128), F32)],
        scratch_shapes=[pltpu.VMEM((NPAIR, HD, 128), F32)],
        compiler_params=_params(("arbitrary",)),
    )(dec, kd, b, ps, kk, vl)


def _scan_bwd(dec, kd, b, ps, kk, vl, dyl, ck, direction):
    s = dec.shape[0]
    nc, ng = s // TC, TC // 8
    row, t8_in, t8_out = _scan_specs(direction, nc, False)
    n = NPAIR * HD

    def body(dec_ref, kd_ref, b_ref, r_ref, kk_ref, vl_ref, dyl_ref, ck_ref,
             dr_ref, dd_ref, db_ref, dk_ref, dkk_ref, dv8_ref, st, sa_s, vb_s, dy_s, ds):
        @pl.when(pl.program_id(0) == 0)
        def _():
            ds[...] = jnp.zeros_like(ds)

        st[0] = ck_ref[0]
        ones2 = _ones2()
        lane_u = lax.broadcasted_iota(jnp.int32, (HD, 256), 1) % HD
        row_id = lax.broadcasted_iota(jnp.int32, (8, 128), 0)
        pc = [slice(p * 128, (p + 1) * 128) for p in range(NPAIR)]
        tiles = lambda res, k: [res[k * n + p * HD:k * n + (p + 1) * HD] for p in range(NPAIR)]

        def fgroup(gi, carry):
            g = gi if direction == 0 else ng - 1 - gi
            rows8 = pl.ds(pl.multiple_of(g * 8, 8), 8)
            d8, k8, b8, kk8 = (q[rows8, :] for q in (dec_ref, kd_ref, b_ref, kk_ref))
            ss = [st[gi * 8, p] for p in range(NPAIR)]
            for ui in range(8):
                u = ui if direction == 0 else 7 - ui
                i = gi * 8 + ui
                lhs = [_split(ss[p] * kk8[u:u + 1, pc[p]]) for p in range(NPAIR)]
                for ref in (vl_ref, dyl_ref):
                    for p in range(NPAIR):
                        t = ref[g, p]
                        lhs.append(jnp.where(lane_u == u, t, jnp.zeros_like(t)))
                res = jnp.dot(jnp.concatenate(lhs, axis=0), ones2, preferred_element_type=F32)
                sa, vb, dyb = tiles(res, 0), tiles(res, 1), tiles(res, 2)
                for p in range(NPAIR):
                    sa_s[i, p] = sa[p]
                    vb_s[i, p] = vb[p]
                    dy_s[i, p] = dyb[p]
                    ss[p] = ss[p] * d8[u:u + 1, pc[p]] - sa[p] * b8[u:u + 1, pc[p]] + vb[p] * k8[u:u + 1, pc[p]]
                    st[i + 1, p] = ss[p]
            return carry

        lax.fori_loop(0, ng, fgroup, 0)

        def bgroup(gj, carry):
            gi = ng - 1 - gj
            g = gi if direction == 0 else ng - 1 - gi
            rows8 = pl.ds(pl.multiple_of(g * 8, 8), 8)
            d8, k8, b8, r8, kk8 = (q[rows8, :] for q in (dec_ref, kd_ref, b_ref, r_ref, kk_ref))
            dss = [ds[p] for p in range(NPAIR)]
            acc = [[jnp.zeros((8, 128), F32) for _ in range(5)] for _ in range(NPAIR)]
            for uj in range(8):
                ui = 7 - uj
                u = ui if direction == 0 else 7 - ui
                i = gi * 8 + ui
                dyb = [dy_s[i, p] for p in range(NPAIR)]
                for p in range(NPAIR):
                    dss[p] = dss[p] + dyb[p] * r8[u:u + 1, pc[p]]
                lhs = [_split(dss[p] * b8[u:u + 1, pc[p]]) for p in range(NPAIR)]
                lhs += [_split(dss[p] * k8[u:u + 1, pc[p]]) for p in range(NPAIR)]
                res = jnp.dot(jnp.concatenate(lhs, axis=0), ones2, preferred_element_type=F32)
                dsa, dvb = tiles(res, 0), tiles(res, 1)
                _put_t8(dv8_ref, g, u, dvb)
                for p in range(NPAIR):
                    sp, sn = st[i, p], st[i + 1, p]
                    outs = (jnp.sum(sn * dyb[p], axis=0, keepdims=True), jnp.sum(dss[p] * sp, axis=0, keepdims=True),
                            -jnp.sum(dss[p] * sa_s[i, p], axis=0, keepdims=True),
                            jnp.sum(dss[p] * vb_s[i, p], axis=0, keepdims=True),
                            -jnp.sum(sp * dsa[p], axis=0, keepdims=True))
                    acc[p] = [jnp.where(row_id == u, o, a_) for o, a_ in zip(outs, acc[p])]
                    dss[p] = dss[p] * d8[u:u + 1, pc[p]] - dsa[p] * kk8[u:u + 1, pc[p]]
            for p in range(NPAIR):
                ds[p] = dss[p]
                for o_ref, a_ in zip((dr_ref, dd_ref, db_ref, dk_ref, dkk_ref), acc[p]):
                    o_ref[rows8, pc[p]] = a_
            return carry

        lax.fori_loop(0, ng, bgroup, 0)

    chunk = lambda k: pltpu.VMEM((k, NPAIR, HD, 128), F32)
    return pl.pallas_call(
        body, name=f"rwkv_scan_bwd{direction}", grid=(nc,),
        in_specs=[row, row, row, row, row, t8_in, t8_in, _bs((1, NPAIR, HD, 128), lambda c: (nc - 1 - c, 0, 0, 0))],
        out_specs=[row] * 5 + [t8_out],
        out_shape=[jax.ShapeDtypeStruct((s, RW), F32)] * 5 + [jax.ShapeDtypeStruct((s // 8, NPAIR, HD, 128), F32)],
        scratch_shapes=[chunk(TC + 1), chunk(TC), chunk(TC), chunk(TC), pltpu.VMEM((NPAIR, HD, 128), F32)],
        compiler_params=_params(("arbitrary",)),
    )(dec, kd, b, ps, kk, vl, dyl, ck)


def _tiles(res, k):
    n = NPAIR * HD
    return [res[k * n + p * HD:k * n + (p + 1) * HD] for p in range(NPAIR)]


def _rows_to_tiles(src_ref, rows8, stage, out_s, base):
    for p in range(NPAIR):
        stage[base + p, 0:8, 0:HD] = src_ref[rows8, p * 128:p * 128 + HD]
        stage[base + p, HD:HD + 8, 0:HD] = src_ref[rows8, p * 128 + HD:(p + 1) * 128]
        out_s[base + p] = stage[base + p].T[0:HD].astype(BF16)


def _tiles_to_rows(tile_s, base, dst_ref, rows8):
    for p in range(NPAIR):
        t = jnp.concatenate([tile_s[base + p], jnp.zeros((HD, 128), F32)], axis=0).T
        dst_ref[rows8, p * 128:p * 128 + HD] = t[0:8, 0:HD]
        dst_ref[rows8, p * 128 + HD:(p + 1) * 128] = t[HD:HD + 8, 0:HD]


def _put_cols(tile_s, base, u, tiles):
    for p in range(NPAIR):
        tile_s[base + p, :, u:u + 1] = tiles[p][:, u:u + 1]
        tile_s[base + p, :, HD + u:HD + u + 1] = tiles[p][:, HD + u:HD + u + 1]


def _scan2_fwd(per_dir, ps, kk):
    s = ps.shape[0]
    nc, ng = s // TC, TC // 8
    in_specs, operands, out_specs, out_shape = [], [], [], []
    for d in (0, 1):
        row, rowv = _scan_specs(d, nc, True)
        in_specs += [row] * 5 + [rowv]
        operands += list(per_dir[d]) + [ps, kk, ps]
        out_specs += [row, _bs((1, NPAIR, HD, 128), lambda c: (c, 0, 0, 0))]
        out_shape += [jax.ShapeDtypeStruct((s, RW), F32), jax.ShapeDtypeStruct((nc, NPAIR, HD, 128), F32)]

    def body(*refs):
        ins = [refs[0:6], refs[6:12]]
        y_refs, ck_refs = (refs[12], refs[14]), (refs[13], refs[15])
        st, vt_s, yt_s, stage = refs[16:]

        @pl.when(pl.program_id(0) == 0)
        def _():
            st[...] = jnp.zeros_like(st)
            yt_s[...] = jnp.zeros_like(yt_s)
            stage[...] = jnp.zeros_like(stage)

        for d in (0, 1):
            ck_refs[d][0] = st[d * NPAIR:(d + 1) * NPAIR]
        ones2 = _ones2()
        ones1 = ones2[0:128]
        lane_u = lax.broadcasted_iota(jnp.int32, (HD, 128), 1) % HD
        pc = [slice(p * 128, (p + 1) * 128) for p in range(NPAIR)]

        def group(gi, carry):
            gs = (gi, ng - 1 - gi)
            rows8 = [pl.ds(pl.multiple_of(gs[d] * 8, 8), 8) for d in (0, 1)]
            blk = [[q[rows8[d], :] for q in ins[d][:5]] for d in (0, 1)]
            for d in (0, 1):
                _rows_to_tiles(ins[d][5], rows8[d], stage, vt_s, d * NPAIR)
            ss = [[st[d * NPAIR + p] for p in range(NPAIR)] for d in (0, 1)]
            for ui in range(9):
                us, ups = (ui, 7 - ui), (ui - 1, 8 - ui)
                lhs1, where = [], {}
                for d in (0, 1):
                    if ui < 8:
                        where["sa", d] = len(lhs1) // NPAIR
                        lhs1 += [(ss[d][p] * blk[d][4][us[d]:us[d] + 1, pc[p]]).astype(BF16) for p in range(NPAIR)]
                        where["vb", d] = len(lhs1) // NPAIR
                        for p in range(NPAIR):
                            vt = vt_s[d * NPAIR + p]
                            lhs1.append(jnp.where(lane_u == us[d], vt, jnp.zeros_like(vt)))
                    if ui > 0:
                        where["y", d] = len(lhs1) // NPAIR
                        lhs1 += [(ss[d][p] * blk[d][3][ups[d]:ups[d] + 1, pc[p]]).astype(BF16) for p in range(NPAIR)]
                res1 = jnp.dot(jnp.concatenate(lhs1, axis=0), ones1, preferred_element_type=F32)
                for d in (0, 1):
                    d8, k8, b8, _, _ = blk[d]
                    u = us[d]
                    if ui < 8:
                        sa, vb = _tiles(res1, where["sa", d]), _tiles(res1, where["vb", d])
                        for p in range(NPAIR):
                            ss[d][p] = (ss[d][p] * d8[u:u + 1, pc[p]] - sa[p] * b8[u:u + 1, pc[p]]
                                        + vb[p] * k8[u:u + 1, pc[p]])
                    if ui > 0:
                        _put_cols(yt_s, d * NPAIR, ups[d], _tiles(res1, where["y", d]))
            for d in (0, 1):
                _tiles_to_rows(yt_s, d * NPAIR, y_refs[d], rows8[d])
                for p in range(NPAIR):
                    st[d * NPAIR + p] = ss[d][p]
            return carry

        for gi in range(ng):
            group(gi, 0)

    outs = pl.pallas_call(
        body, name="rwkv_scan_fwd", grid=(nc,), in_specs=in_specs, out_specs=out_specs, out_shape=out_shape,
        scratch_shapes=[pltpu.VMEM((2 * NPAIR, HD, 128), F32), pltpu.VMEM((2 * NPAIR, HD, 128), BF16),
                        pltpu.VMEM((2 * NPAIR, HD, 128), F32), pltpu.VMEM((2 * NPAIR, 128, 128), F32)],
        compiler_params=_params(("arbitrary",)),
    )(*operands)
    return [(outs[0], outs[1]), (outs[2], outs[3])]


def _scan2_bwd(per_dir, ps, kk, dy):
    s = ps.shape[0]
    nc, ng = s // TC, TC // 8
    in_specs, operands, out_specs, out_shape = [], [], [], []
    for d in (0, 1):
        row, rowv = _scan_specs(d, nc, False)
        dec, kd, b, ck = per_dir[d]
        in_specs += [row] * 5 + [rowv, row, _bs((1, NPAIR, HD, 128), lambda c: (nc - 1 - c, 0, 0, 0))]
        operands += [dec, kd, b, ps, kk, ps, dy, ck]
        out_specs += [row] * 6
        out_shape += [jax.ShapeDtypeStruct((s, RW), F32)] * 6

    def body(*refs):
        ins = [refs[0:8], refs[8:16]]
        outs = [refs[16:22], refs[22:28]]
        st, sa_s, vb_s, dy_s, ds, vt_s, dyt_s, dvt_s, stage = refs[28:]

        @pl.when(pl.program_id(0) == 0)
        def _():
            dvt_s[...] = jnp.zeros_like(dvt_s)
            stage[...] = jnp.zeros_like(stage)
            ds[...] = jnp.zeros_like(ds)

        for d in (0, 1):
            st[d * (TC + 1)] = ins[d][7][0]
        ones2 = _ones2()
        ones1 = ones2[0:128]
        lane_u = lax.broadcasted_iota(jnp.int32, (HD, 128), 1) % HD
        row_id = lax.broadcasted_iota(jnp.int32, (8, 128), 0)
        pc = [slice(p * 128, (p + 1) * 128) for p in range(NPAIR)]

        def load_rows(gs):
            return [[q[pl.ds(pl.multiple_of(gs[d] * 8, 8), 8), :] for q in ins[d][:5]] for d in (0, 1)]

        def fgroup(gi, carry):
            gs = (gi, ng - 1 - gi)
            blk = load_rows(gs)
            for d in (0, 1):
                rows8 = pl.ds(pl.multiple_of(gs[d] * 8, 8), 8)
                _rows_to_tiles(ins[d][5], rows8, stage, vt_s, d * NPAIR)
                _rows_to_tiles(ins[d][6], rows8, stage, dyt_s, d * NPAIR)
            ss = [[st[d * (TC + 1) + gi * 8, p] for p in range(NPAIR)] for d in (0, 1)]
            for ui in range(8):
                us = (ui, 7 - ui)
                i = gi * 8 + ui
                lhs1 = []
                for d in (0, 1):
                    kk8 = blk[d][4]
                    lhs1 += [(ss[d][p] * kk8[us[d]:us[d] + 1, pc[p]]).astype(BF16) for p in range(NPAIR)]
                    for tile_s in (vt_s, dyt_s):
                        for p in range(NPAIR):
                            t = tile_s[d * NPAIR + p]
                            lhs1.append(jnp.where(lane_u == us[d], t, jnp.zeros_like(t)))
                res1 = jnp.dot(jnp.concatenate(lhs1, axis=0), ones1, preferred_element_type=F32)
                for d in (0, 1):
                    d8, k8, b8, _, _ = blk[d]
                    u = us[d]
                    sa, vb, dyb = _tiles(res1, 3 * d), _tiles(res1, 3 * d + 1), _tiles(res1, 3 * d + 2)
                    for p in range(NPAIR):
                        sa_s[d * TC + i, p] = sa[p]
                        vb_s[d * TC + i, p] = vb[p]
                        dy_s[d * TC + i, p] = dyb[p]
                        ss[d][p] = ss[d][p] * d8[u:u + 1, pc[p]] - sa[p] * b8[u:u + 1, pc[p]] + vb[p] * k8[u:u + 1, pc[p]]
                        st[d * (TC + 1) + i + 1, p] = ss[d][p]
            return carry

        for gi in range(ng):
            fgroup(gi, 0)

        def bgroup(gj, carry):
            gi = ng - 1 - gj
            gs = (gi, ng - 1 - gi)
            blk = load_rows(gs)
            dss = [[ds[d * NPAIR + p] for p in range(NPAIR)] for d in (0, 1)]
            acc = [[[jnp.zeros((8, 128), F32) for _ in range(5)] for _ in range(NPAIR)] for _ in (0, 1)]
            for uj in range(8):
                ui = 7 - uj
                us = (ui, 7 - ui)
                i = gi * 8 + ui
                lhs1, dyb = [], [None, None]
                for d in (0, 1):
                    _, k8, b8, r8, _ = blk[d]
                    u = us[d]
                    dyb[d] = [dy_s[d * TC + i, p] for p in range(NPAIR)]
                    for p in range(NPAIR):
                        dss[d][p] = dss[d][p] + dyb[d][p] * r8[u:u + 1, pc[p]]
                    lhs1 += [(dss[d][p] * b8[u:u + 1, pc[p]]).astype(BF16) for p in range(NPAIR)]
                    lhs1 += [(dss[d][p] * k8[u:u + 1, pc[p]]).astype(BF16) for p in range(NPAIR)]
                res1 = jnp.dot(jnp.concatenate(lhs1, axis=0), ones1, preferred_element_type=F32)
                for d in (0, 1):
                    d8, _, _, _, kk8 = blk[d]
                    u = us[d]
                    dsa, dvb = _tiles(res1, 2 * d), _tiles(res1, 2 * d + 1)
                    _put_cols(dvt_s, d * NPAIR, u, dvb)
                    for p in range(NPAIR):
                        sp, sn = st[d * (TC + 1) + i, p], st[d * (TC + 1) + i + 1, p]
                        dsv = dss[d][p]
                        vals = (jnp.sum(sn * dyb[d][p], axis=0, keepdims=True), jnp.sum(dsv * sp, axis=0, keepdims=True),
                                -jnp.sum(dsv * sa_s[d * TC + i, p], axis=0, keepdims=True),
                                jnp.sum(dsv * vb_s[d * TC + i, p], axis=0, keepdims=True),
                                -jnp.sum(sp * dsa[p], axis=0, keepdims=True))
                        acc[d][p] = [jnp.where(row_id == u, o, a_) for o, a_ in zip(vals, acc[d][p])]
                        dss[d][p] = dsv * d8[u:u + 1, pc[p]] - dsa[p] * kk8[u:u + 1, pc[p]]
            for d in (0, 1):
                rows8 = pl.ds(pl.multiple_of(gs[d] * 8, 8), 8)
                _tiles_to_rows(dvt_s, d * NPAIR, outs[d][5], rows8)
                for p in range(NPAIR):
                    ds[d * NPAIR + p] = dss[d][p]
                    for o_ref, a_ in zip(outs[d][:5], acc[d][p]):
                        o_ref[rows8, pc[p]] = a_
            return carry

        for gj in range(ng):
            bgroup(gj, 0)

    chunk = lambda k: pltpu.VMEM((k, NPAIR, HD, 128), F32)
    pairs = lambda w, dt: pltpu.VMEM((2 * NPAIR, HD, w), dt)
    res = pl.pallas_call(
        body, name="rwkv_scan_bwd", grid=(nc,), in_specs=in_specs, out_specs=out_specs, out_shape=out_shape,
        scratch_shapes=[chunk(2 * (TC + 1)), chunk(2 * TC), chunk(2 * TC), chunk(2 * TC), pairs(128, F32),
                        pairs(128, BF16), pairs(128, BF16), pairs(128, F32), pltpu.VMEM((2 * NPAIR, 128, 128), F32)],
        compiler_params=_params(("arbitrary",)),
    )(*operands)
    return [res[0:6], res[6:12]]


MT = 256
MN = 256


def _merge_fwd(ya, yr, yx, wa, wr, wx, proj, gate_b):
    s = ya.shape[0]

    def body(ya_ref, yr_ref, yx_ref, wa_ref, wr_ref, wx_ref, m0, m1, m2, b0, b1, b2, o_ref):
        acc = jnp.zeros((MT, MN), F32)
        for y_ref, w_ref, m_ref, b_ref in ((ya_ref, wa_ref, m0, b0), (yr_ref, wr_ref, m1, b1), (yx_ref, wx_ref, m2, b2)):
            u = _dot(y_ref[...], w_ref[...], ((1,), (0,)))
            acc = acc + jax.nn.sigmoid(m_ref[...] + b_ref[...]) * u
        o_ref[...] = acc.astype(BF16)

    mg = lambda br: _bs((MT, MN), lambda i, j: (i, C_MG // MN + br * (D // MN) + j))
    gb = lambda br: _bs((1, MN), lambda i, j: (0, br * (D // MN) + j))
    return pl.pallas_call(
        body, name="merge_fwd", grid=(s // MT, D // MN),
        in_specs=[_bs((MT, RW), lambda i, j: (i, 0)), _bs((MT, RW), lambda i, j: (i, 0)), _bs((MT, XW), lambda i, j: (i, 0)),
                  _bs((RW, MN), lambda i, j: (0, j)), _bs((RW, MN), lambda i, j: (0, j)), _bs((XW, MN), lambda i, j: (0, j)),
                  mg(0), mg(1), mg(2), gb(0), gb(1), gb(2)],
        out_specs=_bs((MT, MN), lambda i, j: (i, j)),
        out_shape=jax.ShapeDtypeStruct((s, D), BF16),
        compiler_params=_params(("parallel", "arbitrary")),
    )(ya, yr, yx, wa, wr, wx, proj, proj, proj, gate_b, gate_b, gate_b)


def _out_fwd(merged, w_out, x, target):
    s = x.shape[0]
    tm, tn = min(512, s), 512

    def body(m_ref, w_ref, x_ref, t_ref, loss_ref, d_ref, d16_ref):
        @pl.when((pl.program_id(0) == 0) & (pl.program_id(1) == 0))
        def _():
            loss_ref[...] = jnp.zeros_like(loss_ref)

        out = x_ref[...] + jnp.dot(m_ref[...], w_ref[...], preferred_element_type=F32)
        err = out - t_ref[...]
        dout = err * (1.0 / D)
        d_ref[...] = dout
        d16_ref[...] = dout.astype(BF16)
        loss_ref[...] += jnp.sum(err * err)

    tile = _bs((tm, tn), lambda i, j: (i, j))
    return pl.pallas_call(
        body, name="out_fwd", grid=(s // tm, D // tn),
        in_specs=[_bs((tm, D), lambda i, j: (i, 0)), _bs((D, tn), lambda i, j: (0, j)), tile, tile],
        out_specs=[_bs((8, 128), lambda i, j: (0, 0)), tile, tile],
        out_shape=[jax.ShapeDtypeStruct((8, 128), F32), jax.ShapeDtypeStruct((s, D), F32),
                   jax.ShapeDtypeStruct((s, D), BF16)],
        compiler_params=_params(("arbitrary", "arbitrary")),
    )(merged, w_out, x, target)


def _merge_bwd(ya, yr, yx, wa, wr, wx, proj, gate_b, dmerged):
    s = ya.shape[0]

    def body(ya_ref, yr_ref, yx_ref, wa_ref, wr_ref, wx_ref, m0, m1, m2, b0, b1, b2, dm_ref,
             dg0, dg1, dg2, du0, du1, du2, dya_ref, dyr_ref, dyx_ref):
        @pl.when(pl.program_id(1) == 0)
        def _():
            dya_ref[...] = jnp.zeros_like(dya_ref)
            dyr_ref[...] = jnp.zeros_like(dyr_ref)
            dyx_ref[...] = jnp.zeros_like(dyx_ref)

        dm = dm_ref[...]
        for y_ref, w_ref, m_ref, b_ref, dg_ref, du_ref, dy_ref in (
                (ya_ref, wa_ref, m0, b0, dg0, du0, dya_ref), (yr_ref, wr_ref, m1, b1, dg1, du1, dyr_ref),
                (yx_ref, wx_ref, m2, b2, dg2, du2, dyx_ref)):
            w = w_ref[...]
            u = _dot(y_ref[...], w, ((1,), (0,)))
            gt = jax.nn.sigmoid(m_ref[...] + b_ref[...])
            dg_ref[...] = (dm * u * gt * (1.0 - gt)).astype(BF16)
            du = (dm * gt).astype(BF16)
            du_ref[...] = du
            dy_ref[...] += _dot(du, w, ((1,), (1,)))

    mg = lambda br: _bs((MT, MN), lambda i, j: (i, C_MG // MN + br * (D // MN) + j))
    gb = lambda br: _bs((1, MN), lambda i, j: (0, br * (D // MN) + j))
    tile = _bs((MT, MN), lambda i, j: (i, j))
    return pl.pallas_call(
        body, name="merge_bwd", grid=(s // MT, D // MN),
        in_specs=[_bs((MT, RW), lambda i, j: (i, 0)), _bs((MT, RW), lambda i, j: (i, 0)), _bs((MT, XW), lambda i, j: (i, 0)),
                  _bs((RW, MN), lambda i, j: (0, j)), _bs((RW, MN), lambda i, j: (0, j)), _bs((XW, MN), lambda i, j: (0, j)),
                  mg(0), mg(1), mg(2), gb(0), gb(1), gb(2), tile],
        out_specs=[tile] * 6 + [_bs((MT, RW), lambda i, j: (i, 0)), _bs((MT, RW), lambda i, j: (i, 0)),
                                _bs((MT, XW), lambda i, j: (i, 0))],
        out_shape=[jax.ShapeDtypeStruct((s, D), BF16)] * 6 + [jax.ShapeDtypeStruct((s, RW), F32),
                                                               jax.ShapeDtypeStruct((s, RW), F32),
                                                               jax.ShapeDtypeStruct((s, XW), F32)],
        compiler_params=_params(("parallel", "arbitrary")),
    )(ya, yr, yx, wa, wr, wx, proj, proj, proj, gate_b, gate_b, gate_b, dmerged)


def _colsum(a, name):
    m, n = a.shape
    tm, tn = min(512, m), 512

    def body(a_ref, o_ref):
        @pl.when(pl.program_id(1) == 0)
        def _():
            o_ref[...] = jnp.zeros_like(o_ref)

        o_ref[...] += jnp.sum(a_ref[...].astype(F32), axis=0, keepdims=True)

    return pl.pallas_call(
        body, name=name, grid=(n // tn, m // tm),
        in_specs=[_bs((tm, tn), lambda j, i: (i, j))], out_specs=_bs((1, tn), lambda j, i: (0, j)),
        out_shape=jax.ShapeDtypeStruct((1, n), F32),
        compiler_params=_params(("parallel", "arbitrary")),
    )(a)


def _in_bwd(dproj, w_in, x, g, dout, stacks=()):
    s = x.shape[0]
    tm, tk = min(512, s), 896
    nk = NIN // tk
    ni = s // tm
    n = len(stacks)

    def body(dp_ref, w_ref, x_ref, g_ref, do_ref, *rest):
        ins, (gx_ref, gg_ref), outs = rest[:n], rest[n:n + 2], rest[n + 2:2 * n + 2]
        acc = rest[2 * n + 2]
        i, kk = pl.program_id(0), pl.program_id(1)

        def copies():
            send_sems, recv_sems = rest[2 * n + 3:]
            px, py, c = lax.axis_index("x"), lax.axis_index("y"), lax.axis_index("c")
            return [pltpu.make_async_remote_copy(
                src_ref=ins[a].at[2 * qx + qy], dst_ref=outs[a].at[j], send_sem=send_sems.at[3 * a + j],
                recv_sem=recv_sems.at[3 * a + j], device_id=(qx, qy, c), device_id_type=MESH)
                for a in range(n) for j, (qx, qy) in enumerate(_other_chips(px, py))]

        @pl.when((i == 0) & (kk == 0))
        def _():
            gg_ref[...] = jnp.zeros_like(gg_ref)
            if n:
                for rc in copies():
                    rc.start()

        @pl.when(kk == 0)
        def _():
            acc[...] = jnp.zeros_like(acc)

        acc[...] += _dot(dp_ref[...], w_ref[...], ((1,), (1,)))

        @pl.when(kk == nk - 1)
        def _():
            xv, dh, gv = x_ref[...], acc[...], g_ref[...]
            r = lax.rsqrt(jnp.mean(xv * xv, axis=-1, keepdims=True) + NORM_EPS)
            xn = xv * r
            gg_ref[...] += jnp.sum(dh * xn, axis=0, keepdims=True)
            dxn = dh * gv
            dx = r * (dxn - xn * jnp.mean(dxn * xn, axis=-1, keepdims=True))
            gx_ref[...] = do_ref[...] + dx

        if n:
            @pl.when((i == ni - 1) & (kk == nk - 1))
            def _():
                for rc in copies():
                    rc.wait_recv()
                for rc in copies():
                    rc.wait_send()

    any_spec = pl.BlockSpec(memory_space=pl.ANY)
    res = pl.pallas_call(
        body, name="in_bwd", grid=(ni, nk),
        in_specs=[_bs((tm, tk), lambda i, kk: (i, kk)), _bs((D, tk), lambda i, kk: (0, kk)),
                  _bs((tm, D), lambda i, kk: (i, 0)), _bs((1, D), lambda i, kk: (0, 0)),
                  _bs((tm, D), lambda i, kk: (i, 0))] + [any_spec] * n,
        out_specs=[_bs((tm, D), lambda i, kk: (i, 0)), _bs((1, D), lambda i, kk: (0, 0))] + [any_spec] * n,
        out_shape=[jax.ShapeDtypeStruct((s, D), F32), jax.ShapeDtypeStruct((1, D), F32)]
        + [jax.ShapeDtypeStruct((3,) + a.shape[1:], a.dtype) for a in stacks],
        scratch_shapes=[pltpu.VMEM((tm, D), F32)]
        + ([pltpu.SemaphoreType.DMA((3 * n,)), pltpu.SemaphoreType.DMA((3 * n,))] if n else []),
        compiler_params=pltpu.CompilerParams(dimension_semantics=("arbitrary", "arbitrary"),
                                             vmem_limit_bytes=VMEM_LIMIT, has_side_effects=bool(n)),
    )(dproj, w_in, x, g, dout, *stacks)
    return res[0], res[1], list(res[2:])


def _adamw_math(w, g, m, v):
    m = ADAM_B1 * m + (1.0 - ADAM_B1) * g
    v = ADAM_B2 * v + (1.0 - ADAM_B2) * jnp.square(g)
    m_hat = m / (1.0 - ADAM_B1 ** ADAM_STEP)
    v_hat = v / (1.0 - ADAM_B2 ** ADAM_STEP)
    delta = -ADAM_LR * (m_hat / (jnp.sqrt(v_hat) + ADAM_EPS) + ADAM_WD * w)
    return delta, m, v


def _adamw(parts, w, m, v, name):
    rows, cols = w.shape
    tr = rows
    for cand in (256, 128, 64, 32, 16, 8):
        if rows % cand == 0 and cand * cols * 4 <= (1 << 20):
            tr = cand
            break
    n = len(parts)

    def body(*refs):
        g = refs[0][...].astype(F32)
        for r in refs[1:n]:
            g = g + r[...].astype(F32)
        w_ref, m_ref, v_ref, g_out, d_out, m_out, v_out = refs[n:]
        delta, m_new, v_new = _adamw_math(w_ref[...], g, m_ref[...], v_ref[...])
        g_out[...] = g
        d_out[...] = delta
        m_out[...] = m_new
        v_out[...] = v_new

    spec = _bs((tr, cols), lambda i: (i, 0))
    return pl.pallas_call(
        body, name=name, grid=(rows // tr,),
        in_specs=[spec] * (n + 3), out_specs=[spec] * 4,
        out_shape=[jax.ShapeDtypeStruct((rows, cols), F32)] * 4,
        compiler_params=_params(("parallel",)),
    )(*parts, w, m, v)


def _adamw_halves(mine, theirs, core, w, m, v, name):
    rows, cols = w.shape
    h = rows // 2
    tr = next(t for t in (256, 128, 64, 32, 16, 8) if h % t == 0 and t * cols * 4 <= (1 << 20))
    nt = h // tr

    def body(core_ref, mine_ref, theirs_ref, w_ref, m_ref, v_ref, g_out, d_out, m_out, v_out):
        is_mine = pl.program_id(0) // nt == core_ref[0]
        g = jnp.where(is_mine, mine_ref[...], theirs_ref[...])
        delta, m_new, v_new = _adamw_math(w_ref[...], g, m_ref[...], v_ref[...])
        g_out[...] = g
        d_out[...] = delta
        m_out[...] = m_new
        v_out[...] = v_new

    spec = _bs((tr, cols), lambda i, core_ref: (i, 0))
    return pl.pallas_call(
        body, name=name,
        grid_spec=pltpu.PrefetchScalarGridSpec(
            num_scalar_prefetch=1, grid=(2 * nt,),
            in_specs=[_bs((tr, cols), lambda i, core_ref: (jnp.clip(i - core_ref[0] * nt, 0, nt - 1), 0)),
                      _bs((tr, cols), lambda i, core_ref: (jnp.clip(i - (1 - core_ref[0]) * nt, 0, nt - 1), 0)),
                      spec, spec, spec],
            out_specs=[spec] * 4),
        out_shape=[jax.ShapeDtypeStruct((rows, cols), F32)] * 4,
        compiler_params=_params(("parallel",)),
    )(core, mine, theirs, w, m, v)


def _sum_parts(parts, name):
    rows, cols = parts[0].shape
    tr = rows
    for cand in (256, 128, 64, 32, 16, 8):
        if rows % cand == 0 and cand * cols * 4 <= (1 << 20):
            tr = cand
            break

    def body(*refs):
        acc = refs[0][...].astype(F32)
        for r in refs[1:-1]:
            acc = acc + r[...].astype(F32)
        refs[-1][...] = acc

    spec = _bs((tr, cols), lambda i: (i, 0))
    return pl.pallas_call(
        body, name=name, grid=(rows // tr,), in_specs=[spec] * len(parts), out_specs=spec,
        out_shape=jax.ShapeDtypeStruct((rows, cols), F32), compiler_params=_params(("parallel",)),
    )(*parts)


ANY = pl.BlockSpec(memory_space=pl.ANY)


def _other_chips(x, y):
    return [(1 - x, y), (x, 1 - y), (1 - x, 1 - y)]


def _gather_shards(arrays, name):
    n = len(arrays)

    def body(*refs):
        ins, outs = refs[:n], refs[n:2 * n]
        ici_send, ici_recv, d2d_send, d2d_recv, local_sems, own_recv = refs[2 * n:]
        x, y, c = lax.axis_index("x"), lax.axis_index("y"), lax.axis_index("c")
        me = 2 * x + y
        chips = _other_chips(x, y)

        def half(i, who):
            h = arrays[i].shape[0] // 2
            return pl.ds(who * h, h)

        def ici(i, j, src_chip, to):
            return pltpu.make_async_remote_copy(
                src_ref=ins[i].at[half(i, c)], dst_ref=outs[i].at[src_chip, half(i, c)], send_sem=ici_send.at[3 * i + j],
                recv_sem=ici_recv.at[3 * i + j], device_id=to, device_id_type=MESH)

        def d2d(i, j, src_chip, who):
            piece = outs[i].at[src_chip, half(i, who)]
            return pltpu.make_async_remote_copy(
                src_ref=piece, dst_ref=piece, send_sem=d2d_send.at[3 * i + j], recv_sem=d2d_recv.at[3 * i + j],
                device_id=(x, y, 1 - c), device_id_type=MESH)

        def own(i):
            return pltpu.make_async_remote_copy(
                src_ref=ins[i], dst_ref=outs[i].at[me], send_sem=local_sems.at[i], recv_sem=own_recv.at[i],
                device_id=(x, y, 1 - c), device_id_type=MESH)

        sends = []
        for i in range(n):
            cp = own(i)
            cp.start()
            sends.append(cp)
            for j, (px, py) in enumerate(chips):
                rc = ici(i, j, me, (px, py, c))
                rc.start()
                sends.append(rc)
        for i in range(n):
            for j, (px, py) in enumerate(chips):
                ici(i, j, 2 * px + py, (px, py, c)).wait_recv()
                fw = d2d(i, j, 2 * px + py, c)
                fw.start()
                sends.append(fw)
        for i in range(n):
            for j, (px, py) in enumerate(chips):
                d2d(i, j, 2 * px + py, 1 - c).wait_recv()
            own(i).wait_recv()
        for rc in sends:
            rc.wait_send()

    dma = lambda k: pltpu.SemaphoreType.DMA((k,))
    return pl.pallas_call(
        body, name=name, in_specs=[ANY] * n, out_specs=[ANY] * n,
        out_shape=[jax.ShapeDtypeStruct((4,) + a.shape, a.dtype) for a in arrays],
        scratch_shapes=[dma(3 * n), dma(3 * n), dma(3 * n), dma(3 * n), dma(n), dma(n)],
        compiler_params=pltpu.CompilerParams(has_side_effects=True),
    )(*arrays)


def _scatter_shards(stacks, name):
    n = len(stacks)

    def body(*refs):
        ins, outs = refs[:n], refs[n:2 * n]
        send_sems, recv_sems = refs[2 * n:]
        x, y, c = lax.axis_index("x"), lax.axis_index("y"), lax.axis_index("c")
        chips = _other_chips(x, y)
        sends = []
        for i in range(n):
            for j, (px, py) in enumerate(chips):
                rc = pltpu.make_async_remote_copy(
                    src_ref=ins[i].at[2 * px + py], dst_ref=outs[i].at[j], send_sem=send_sems.at[3 * i + j],
                    recv_sem=recv_sems.at[3 * i + j], device_id=(px, py, c), device_id_type=MESH)
                rc.start()
                sends.append(rc)
        for rc in sends:
            rc.wait_recv()
        for rc in sends:
            rc.wait_send()

    return pl.pallas_call(
        body, name=name, in_specs=[ANY] * n, out_specs=[ANY] * n,
        out_shape=[jax.ShapeDtypeStruct((3,) + a.shape[1:], a.dtype) for a in stacks],
        scratch_shapes=[pltpu.SemaphoreType.DMA((3 * n,)), pltpu.SemaphoreType.DMA((3 * n,))],
        compiler_params=pltpu.CompilerParams(has_side_effects=True),
    )(*stacks)


def _pair_exchange(stacks, name):
    n = len(stacks)

    def body(*refs):
        ins, outs = refs[:n], refs[n:2 * n]
        send_sems, recv_sems = refs[2 * n:]
        x, y, c = lax.axis_index("x"), lax.axis_index("y"), lax.axis_index("c")
        cps = []
        for i in range(n):
            h = stacks[i].shape[1] // 2
            rc = pltpu.make_async_remote_copy(
                src_ref=ins[i].at[:, pl.ds((1 - c) * h, h)], dst_ref=outs[i], send_sem=send_sems.at[i],
                recv_sem=recv_sems.at[i], device_id=(x, y, 1 - c), device_id_type=MESH)
            rc.start()
            cps.append(rc)
        for rc in cps:
            rc.wait_recv()
        for rc in cps:
            rc.wait_send()

    return pl.pallas_call(
        body, name=name, in_specs=[ANY] * n, out_specs=[ANY] * n,
        out_shape=[jax.ShapeDtypeStruct((4, a.shape[1] // 2) + a.shape[2:], a.dtype) for a in stacks],
        scratch_shapes=[pltpu.SemaphoreType.DMA((n,)), pltpu.SemaphoreType.DMA((n,))],
        compiler_params=pltpu.CompilerParams(has_side_effects=True),
    )(*stacks)


def _pair_sum(own, theirs, core, name):
    _, r, cols = own.shape
    h = r // 2
    tr = next(t for t in (256, 128, 64, 32, 16) if h % t == 0 and t * cols * 4 <= (1 << 20))
    nt = h // tr

    def body(core_ref, own_ref, th_ref, o32_ref, o16_ref):
        del core_ref
        acc = own_ref[...] + th_ref[...].astype(F32)
        o32_ref[...] = acc
        o16_ref[...] = acc.astype(BF16)

    out = _bs((1, tr, cols), lambda j, t, core_ref: (j, t, 0))
    return pl.pallas_call(
        body, name=name,
        grid_spec=pltpu.PrefetchScalarGridSpec(
            num_scalar_prefetch=1, grid=(4, nt),
            in_specs=[_bs((1, tr, cols), lambda j, t, core_ref: (j, core_ref[0] * nt + t, 0)), out],
            out_specs=[out, out]),
        out_shape=[jax.ShapeDtypeStruct((4, h, cols), F32), jax.ShapeDtypeStruct((4, h, cols), BF16)],
        compiler_params=_params(("parallel", "parallel")),
    )(core, own, theirs)


def _swap_sibling(arrays, name):
    n = len(arrays)

    def body(*refs):
        ins, outs = refs[:n], refs[n:2 * n]
        send_sems, recv_sems = refs[2 * n:]
        sib = (lax.axis_index("x"), lax.axis_index("y"), 1 - lax.axis_index("c"))
        cps = []
        for i in range(n):
            rc = pltpu.make_async_remote_copy(src_ref=ins[i], dst_ref=outs[i], send_sem=send_sems.at[i],
                                              recv_sem=recv_sems.at[i], device_id=sib, device_id_type=MESH)
            rc.start()
            cps.append(rc)
        for rc in cps:
            rc.wait_recv()
        for rc in cps:
            rc.wait_send()

    return pl.pallas_call(
        body, name=name, in_specs=[ANY] * n, out_specs=[ANY] * n,
        out_shape=[jax.ShapeDtypeStruct(a.shape, a.dtype) for a in arrays],
        scratch_shapes=[pltpu.SemaphoreType.DMA((n,)), pltpu.SemaphoreType.DMA((n,))],
        compiler_params=pltpu.CompilerParams(has_side_effects=True),
    )(*arrays)


def _all_reduce_small(v):
    rows = v.shape[0]

    def body(v_ref, o_ref, buf, send_sems, recv_sems):
        x, y, c = lax.axis_index("x"), lax.axis_index("y"), lax.axis_index("c")
        me = 4 * x + 2 * y + c
        buf[me] = v_ref[...]
        cps = []
        for kbits in range(1, 8):
            bx, by, bc = (kbits >> 2) & 1, (kbits >> 1) & 1, kbits & 1
            px = jnp.where(bx == 1, 1 - x, x)
            py = jnp.where(by == 1, 1 - y, y)
            pc = jnp.where(bc == 1, 1 - c, c)
            rc = pltpu.make_async_remote_copy(src_ref=v_ref, dst_ref=buf.at[me], send_sem=send_sems.at[kbits - 1],
                                              recv_sem=recv_sems.at[kbits - 1], device_id=(px, py, pc),
                                              device_id_type=MESH)
            rc.start()
            cps.append((rc, 4 * px + 2 * py + pc))
        for kbits, (rc, src) in enumerate(cps):
            pltpu.make_async_remote_copy(src_ref=v_ref, dst_ref=buf.at[src], send_sem=send_sems.at[kbits],
                                         recv_sem=recv_sems.at[kbits], device_id=(x, y, c),
                                         device_id_type=MESH).wait_recv()
        for rc, _ in cps:
            rc.wait_send()
        acc = buf[0]
        for d in range(1, 8):
            acc = acc + buf[d]
        o_ref[...] = acc

    return pl.pallas_call(
        body, name="all_reduce_small",
        in_specs=[pl.BlockSpec(memory_space=pltpu.VMEM)], out_specs=pl.BlockSpec(memory_space=pltpu.VMEM),
        out_shape=jax.ShapeDtypeStruct((rows, 128), F32),
        scratch_shapes=[pltpu.VMEM((8, rows, 128), F32), pltpu.SemaphoreType.DMA((7,)), pltpu.SemaphoreType.DMA((7,))],
        compiler_params=pltpu.CompilerParams(has_side_effects=True, vmem_limit_bytes=VMEM_LIMIT),
    )(v)


def _rope_tables(s):
    half = HD // 2
    inv = 10000.0 ** (-jnp.arange(half, dtype=F32) / half)
    ang = jnp.arange(s, dtype=F32)[:, None] * inv[None, :]
    cos, sin = jnp.cos(ang), jnp.sin(ang)
    return jnp.concatenate([cos, cos], axis=1), jnp.concatenate([sin, sin], axis=1)


def _local_step(x, mem, target, norm_g, mem_norm_g, w_in, gate_b, gq, gk, sink, wa, mu, k_k, k_a, r_k, w0, w2, a0, a2,
                ln_w, ln_b, wr, w_kv, gxq, gxk, wx, w_out):
    s = x.shape[0]
    cos, sin = _rope_tables(s)
    r_k = r_k.reshape(1, RW)

    proj, h = _proj_fwd(x, norm_g, w_in)
    ya = _attn_fwd(proj, cos, sin, gq, gk, sink)
    mkv, mn = _mem_kv(mem, mem_norm_g, w_kv)
    yx = _xattn_fwd(proj, mkv, gxq, gxk)
    ps = _shift_fwd(proj, mu)
    kk, dec0, kd0, b0, dec1, kd1, b1 = _pre_fwd(ps, k_k, k_a, w0, w2, a0, a2)
    (y0, ck0), (y1, ck1) = _scan2_fwd([(dec0, kd0, b0), (dec1, kd1, b1)], ps, kk)
    yr = _post_fwd(y0, y1, ps, kd0, kd1, proj, r_k, ln_w, ln_b)
    merged = _merge_fwd(ya, yr, yx, wa, wr, wx, proj, gate_b)
    loss_tile, dout, dout16 = _out_fwd(merged, w_out, x, target)
    loss_sum = loss_tile[0, 0]

    g = {}
    t16 = lambda a: a.astype(BF16).T
    sk = min(1024, s)
    dmerged = _matmul(dout16, w_out, mode="nt", m=s, n=D, k=D, tm=sk, tn=1024, tk=1024, name="dmerged")
    g["w_out"] = _matmul(merged.T, dout16, mode="nn", m=D, n=D, k=s, tm=1024, tn=1024, tk=sk, name="grad_w_out")
    dg0, dg1, dg2, du0, du1, du2, dya, dyr, dyx = _merge_bwd(ya, yr, yx, wa, wr, wx, proj, gate_b, dmerged)
    g["attn_w_o"] = _matmul(t16(ya), du0, mode="nn", m=RW, n=D, k=s, tm=RW, tn=1024, tk=s, name="grad_attn_w_o")
    g["rwkv_w_o"] = _matmul(t16(yr), du1, mode="nn", m=RW, n=D, k=s, tm=RW, tn=1024, tk=s, name="grad_rwkv_w_o")
    g["x_w_o"] = _matmul(t16(yx), du2, mode="nn", m=XW, n=D, k=s, tm=XW, tn=1024, tk=s, name="grad_x_w_o")
    dmg = jnp.concatenate([dg0, dg1, dg2], axis=1)
    g["gate_b"] = _colsum(dmg, "grad_gate_b")

    daq, dak, dav, dag, g["attn_q_norm_g"], g["attn_k_norm_g"], g["attn_sink"] = _attn_bwd(proj, cos, sin, gq, gk, sink, dya)

    dxq, dxg, dmkv, g["x_q_norm_g"], g["x_k_norm_g"] = _xattn_bwd(proj, mkv, gxq, gxk, dyx)
    g["x_w_kv"] = _matmul(mn, dmkv, mode="tn", m=D, n=2 * XW, k=NMEM, tm=512, tn=512, tk=NMEM, name="grad_x_w_kv")
    dmn = _matmul(dmkv, w_kv, mode="nt", m=NMEM, n=D, k=2 * XW, tm=NMEM, tn=512, tk=2 * XW, name="dmn")
    g["mem_norm_g"] = _mem_bwd(mem, mem_norm_g, dmn)

    dys, dr_p, dv_p, dkd0_p, dkd1_p, drg, g["rwkv_r_k"], g["rwkv_ln_w"], g["rwkv_ln_b"] = _post_bwd(
        y0, y1, ps, kd0, kd1, proj, r_k, ln_w, ln_b, dyr)
    (dr0, dd0, db0, dk0, dkk0, dv0), (dr1, dd1, db1, dk1, dkk1, dv1) = _scan2_bwd(
        [(dec0, kd0, b0, ck0), (dec1, kd1, b1, ck1)], ps, kk, dys)
    dr = dr_p + dr0 + dr1
    dv = dv_p + dv0 + dv1
    cts = (dkk0 + dkk1, dd0, dk0 + dkd0_p, db0, dd1, dk1 + dkd1_p, db1)
    dps, g["rwkv_k_k"], g["rwkv_k_a"], g["rwkv_w0"], g["rwkv_w2"], g["rwkv_a0"], g["rwkv_a2"] = _pre_bwd(
        ps, k_k, k_a, w0, w2, a0, a2, dr, dv, cts)
    drs, g["rwkv_mu"] = _shift_bwd(proj, mu, dps)

    dproj = jnp.concatenate([daq.astype(BF16), dak.astype(BF16), dav.astype(BF16), dag.astype(BF16), drs.astype(BF16),
                             drg.astype(BF16), dxq.astype(BF16), dxg.astype(BF16), dmg], axis=1)
    dproj4 = jnp.stack([dproj[:, j * (NIN // 4):(j + 1) * (NIN // 4)] for j in range(4)])
    g["w_in"], g["w_in_bf16"] = _grad_w_in(h.T, dproj4)
    g["rwkv_r_k"] = g["rwkv_r_k"].reshape(AH, HD)
    return loss_sum, g, (dproj, w_in, x, norm_g, dout)


WEIGHTS = ['norm_g', 'mem_norm_g', 'w_in', 'gate_b', 'attn_q_norm_g', 'attn_k_norm_g', 'attn_sink', 'attn_w_o',
           'rwkv_mu', 'rwkv_k_k', 'rwkv_k_a', 'rwkv_r_k', 'rwkv_w0', 'rwkv_w2', 'rwkv_a0', 'rwkv_a2', 'rwkv_ln_w',
           'rwkv_ln_b', 'rwkv_w_o', 'x_w_kv', 'x_q_norm_g', 'x_k_norm_g', 'x_w_o', 'w_out']
BIG = ['w_in', 'attn_w_o', 'rwkv_w_o', 'x_w_kv', 'x_w_o', 'w_out']
COL_SHARDED = ['w_in', 'attn_w_o', 'rwkv_w_o', 'x_w_o']
LORA = ['rwkv_w0', 'rwkv_w2', 'rwkv_a0', 'rwkv_a2']
SMALL = [n for n in WEIGHTS if n not in BIG]


def _unshard_cols(stack):
    return jnp.concatenate([stack[i] for i in range(4)], axis=-1)


def _shard_cols(full):
    w = full.shape[-1] // 4
    return [full[..., i * w:(i + 1) * w] for i in range(4)]


def kernel(x, mem, norm_g, mem_norm_g, w_in, gate_b, attn_q_norm_g, attn_k_norm_g, attn_sink, attn_w_o, rwkv_mu, rwkv_k_k, rwkv_k_a, rwkv_r_k, rwkv_w0, rwkv_w2, rwkv_a0, rwkv_a2, rwkv_ln_w, rwkv_ln_b, rwkv_w_o, x_w_kv, x_q_norm_g, x_k_norm_g, x_w_o, w_out, loss_target, m_norm_g, m_mem_norm_g, m_w_in, m_gate_b, m_attn_q_norm_g, m_attn_k_norm_g, m_attn_sink, m_attn_w_o, m_rwkv_mu, m_rwkv_k_k, m_rwkv_k_a, m_rwkv_r_k, m_rwkv_w0, m_rwkv_w2, m_rwkv_a0, m_rwkv_a2, m_rwkv_ln_w, m_rwkv_ln_b, m_rwkv_w_o, m_x_w_kv, m_x_q_norm_g, m_x_k_norm_g, m_x_w_o, m_w_out, v_norm_g, v_mem_norm_g, v_w_in, v_gate_b, v_attn_q_norm_g, v_attn_k_norm_g, v_attn_sink, v_attn_w_o, v_rwkv_mu, v_rwkv_k_k, v_rwkv_k_a, v_rwkv_r_k, v_rwkv_w0, v_rwkv_w2, v_rwkv_a0, v_rwkv_a2, v_rwkv_ln_w, v_rwkv_ln_b, v_rwkv_w_o, v_x_w_kv, v_x_q_norm_g, v_x_k_norm_g, v_x_w_o, v_w_out):
    args = dict(locals())
    canon = lambda a: a[0] if a.ndim > 2 else a
    w = {n: canon(args[n]) for n in WEIGHTS}
    m = {n: canon(args["m_" + n]) for n in WEIGHTS}
    v = {n: canon(args["v_" + n]) for n in WEIGHTS}
    shard = 2 * lax.axis_index("x") + lax.axis_index("y")

    local = [w[n].astype(BF16) for n in BIG] + [w[n].reshape(2, -1, w[n].shape[-1]) for n in LORA]
    stacks = dict(zip(BIG + LORA, _gather_shards(local, "gather_weights")))
    full = {}
    for n in COL_SHARDED:
        full[n] = _unshard_cols(stacks[n])
    for n in LORA:
        full[n] = _unshard_cols(stacks[n]).reshape(w[n].shape[:-1] + (RW,))
    full["x_w_kv"] = stacks["x_w_kv"].reshape(D, 2 * XW)
    full["w_out"] = stacks["w_out"].reshape(D, D)

    loss_sum, g, deferred = _local_step(
        x[0], mem[0], loss_target[0], w["norm_g"], w["mem_norm_g"], full["w_in"], w["gate_b"], w["attn_q_norm_g"],
        w["attn_k_norm_g"], w["attn_sink"], full["attn_w_o"], w["rwkv_mu"], w["rwkv_k_k"], w["rwkv_k_a"], w["rwkv_r_k"],
        full["rwkv_w0"], full["rwkv_w2"], full["rwkv_a0"], full["rwkv_a2"], w["rwkv_ln_w"], w["rwkv_ln_b"],
        full["rwkv_w_o"], full["x_w_kv"], w["x_q_norm_g"], w["x_k_norm_g"], full["x_w_o"], full["w_out"])

    loss = lax.psum(0.5 * loss_sum / D, ("x", "y", "c"))

    def as_stack(n, dtype):
        if n == "w_in":
            return g["w_in"] if dtype == F32 else g["w_in_bf16"]
        if n in COL_SHARDED:
            return jnp.stack([p.astype(dtype) for p in _shard_cols(g[n])])
        return g[n].reshape((4, g[n].shape[0] // 4) + g[n].shape[1:]).astype(dtype)

    core = lax.axis_index("c").astype(jnp.int32).reshape(1)
    sibling = _pair_exchange([as_stack(n, BF16) for n in BIG], "pair_exchange")
    pair32, pair16 = [], []
    for n, th in zip(BIG, sibling):
        a32, a16 = _pair_sum(as_stack(n, F32), th, core, "pair_sum_" + n)
        pair32.append(a32)
        pair16.append(a16)
    grad_x, g["norm_g"], recv = _in_bwd(*deferred, stacks=pair16)
    halves = []
    for n, p32, r in zip(BIG, pair32, recv):
        own = lax.dynamic_index_in_dim(p32, shard, 0, keepdims=False)
        halves.append(_sum_parts([own, r[0], r[1], r[2]], "sum_" + n))
    other_halves = _swap_sibling(halves, "swap_halves")

    out_g, out_d, out_m, out_v = {}, {}, {}, {}
    for n, mine, theirs in zip(BIG, halves, other_halves):
        out_g[n], out_d[n], out_m[n], out_v[n] = _adamw_halves(mine, theirs, core, w[n], m[n], v[n], "adamw_" + n)

    flat = jnp.concatenate([g[n].reshape(-1) for n in SMALL])
    total = flat.shape[0]
    padded = -(-total // 1024) * 1024
    flat = jnp.pad(flat, (0, padded - total)).reshape(padded // 128, 128)
    red = _all_reduce_small(flat).reshape(-1)
    off = 0
    gs = {}
    for n in SMALL:
        size = g[n].size
        t = red[off:off + size].reshape(g[n].shape)
        off += size
        if n in LORA:
            wd = t.shape[-1] // 4
            t = lax.dynamic_slice_in_dim(t, shard * wd, wd, axis=t.ndim - 1)
        gs[n] = t

    def pack(d):
        f = jnp.concatenate([d[n].reshape(-1) for n in SMALL])
        return jnp.pad(f, (0, -(-f.shape[0] // 1024) * 1024 - f.shape[0])).reshape(-1, 128)

    pg, pd, pm, pv = _adamw([pack(gs)], pack(w), pack(m), pack(v), "adamw_small")
    off = 0
    for n in SMALL:
        size = w[n].size
        for dst, src in ((out_g, pg), (out_d, pd), (out_m, pm), (out_v, pv)):
            dst[n] = src.reshape(-1)[off:off + size].reshape(w[n].shape)
        off += size

    lead = lambda d: [d[n][None] if args[n].ndim > 2 else d[n] for n in WEIGHTS]
    return (loss, grad_x[None], *lead(out_g), *lead(out_d), *lead(out_m), *lead(out_v))
```

```python
import functools

import jax
import jax.numpy as jnp
from jax import lax
from jax.experimental import pallas as pl
from jax.experimental.pallas import tpu as pltpu

F32 = jnp.float32
BF16 = jnp.bfloat16
HI = lax.Precision.HIGH
MESH = pl.DeviceIdType.MESH

D = 2048
NMEM = 256
NORM_EPS = 1e-6
NEG_INF = -1e30
GN_EPS = 64e-5
HD = 64
AH = 12
AKV = 4
RW = 768
XH = 4
XD = 128
XW = 512
NIN = 12544
RSW = 2560
C_AQ, C_AK, C_AV, C_AG, C_RS, C_RG, C_XQ, C_XG, C_MG = 0, 768, 1024, 1280, 2048, 4608, 5376, 5888, 6400
WIN = 384
QB = 128
TC = 16
NPAIR = 6

ADAM_LR, ADAM_B1, ADAM_B2, ADAM_EPS, ADAM_WD, ADAM_STEP = 0.001, 0.9, 0.999, 1e-08, 0.01, 10

VMEM_LIMIT = 56 * 1024 * 1024


def _bs(shape, imap):
    return pl.BlockSpec(shape, imap)


def _params(sem=None, vmem=VMEM_LIMIT):
    return pltpu.CompilerParams(dimension_semantics=sem, vmem_limit_bytes=vmem)


def _dot(a, b, dims):
    return lax.dot_general(a.astype(BF16), b.astype(BF16), (dims, ((), ())), preferred_element_type=F32)


@jax.custom_vjp
def _mm_nn(a, b):
    return _dot(a, b, ((1,), (0,)))


def _mm_nn_fwd(a, b):
    return _mm_nn(a, b), (a, b)


def _mm_nn_bwd(res, ct):
    a, b = res
    return _dot(ct, b, ((1,), (1,))), _dot(a, ct, ((0,), (0,)))


_mm_nn.defvjp(_mm_nn_fwd, _mm_nn_bwd)


@jax.custom_vjp
def _mm_nt(a, b):
    return _dot(a, b, ((1,), (1,)))


def _mm_nt_fwd(a, b):
    return _mm_nt(a, b), (a, b)


def _mm_nt_bwd(res, ct):
    a, b = res
    return _dot(ct, b, ((1,), (0,))), _dot(ct, a, ((0,), (0,)))


_mm_nt.defvjp(_mm_nt_fwd, _mm_nt_bwd)


def _seg_matrix(n, seg):
    r = lax.broadcasted_iota(jnp.int32, (n, n), 0) // seg
    c = lax.broadcasted_iota(jnp.int32, (n, n), 1) // seg
    return (r == c).astype(F32)


def _rot_matrix():
    r = lax.broadcasted_iota(jnp.int32, (HD, HD), 0)
    c = lax.broadcasted_iota(jnp.int32, (HD, HD), 1)
    return jnp.where(c == r + HD // 2, 1.0, 0.0).astype(F32) - jnp.where(c == r - HD // 2, 1.0, 0.0).astype(F32)


def _hdot(a, m):
    return jnp.dot(a, m, precision=HI, preferred_element_type=F32)


def _rms(t, g):
    return t * lax.rsqrt(jnp.mean(t * t, axis=-1, keepdims=True) + NORM_EPS) * g


def _silu(t):
    return t * jax.nn.sigmoid(t)


def _softplus(z):
    return jnp.maximum(z, 0.0) + jnp.log(1.0 + jnp.exp(-jnp.abs(z)))


def _matmul(a, b, *, mode, m, n, k, tm, tn, tk, name, a_off=(0, 0), b_off=(0, 0), out_dtype=F32):
    nk = k // tk
    if mode == "tn":
        a_spec = _bs((tk, tm), lambda i, j, kk: (kk + a_off[0], i + a_off[1]))
        dims = ((0,), (0,))
    else:
        a_spec = _bs((tm, tk), lambda i, j, kk: (i + a_off[0], kk + a_off[1]))
        dims = ((1,), (1,)) if mode == "nt" else ((1,), (0,))
    if mode == "nt":
        b_spec = _bs((tn, tk), lambda i, j, kk: (j + b_off[0], kk + b_off[1]))
    else:
        b_spec = _bs((tk, tn), lambda i, j, kk: (kk + b_off[0], j + b_off[1]))

    def body(a_ref, b_ref, o_ref, acc):
        kk = pl.program_id(2)

        @pl.when(kk == 0)
        def _():
            acc[...] = jnp.zeros_like(acc)

        acc[...] += _dot(a_ref[...], b_ref[...], dims)

        @pl.when(kk == nk - 1)
        def _():
            o_ref[...] = acc[...].astype(out_dtype)

    return pl.pallas_call(
        body, name=name, grid=(m // tm, n // tn, nk),
        in_specs=[a_spec, b_spec], out_specs=_bs((tm, tn), lambda i, j, kk: (i, j)),
        out_shape=jax.ShapeDtypeStruct((m, n), out_dtype),
        scratch_shapes=[pltpu.VMEM((tm, tn), F32)],
        compiler_params=_params(("parallel", "parallel", "arbitrary")),
    )(a, b)


def _grad_w_in(ht, dproj4):
    s = ht.shape[1]
    ws = NIN // 4
    tm, tk = 512, min(1024, s)
    nk = s // tk

    def body(a_ref, b_ref, o32_ref, o16_ref, acc):
        kk = pl.program_id(2)

        @pl.when(kk == 0)
        def _():
            acc[...] = jnp.zeros_like(acc)

        acc[...] += jnp.dot(a_ref[...], b_ref[0], preferred_element_type=F32)

        @pl.when(kk == nk - 1)
        def _():
            o32_ref[0] = acc[...]
            o16_ref[0] = acc[...].astype(BF16)

    out = _bs((1, tm, ws), lambda j, i, kk: (j, i, 0))
    return pl.pallas_call(
        body, name="grad_w_in", grid=(4, D // tm, nk),
        in_specs=[_bs((tm, tk), lambda j, i, kk: (i, kk)), _bs((1, tk, ws), lambda j, i, kk: (j, kk, 0))],
        out_specs=[out, out],
        out_shape=[jax.ShapeDtypeStruct((4, D, ws), F32), jax.ShapeDtypeStruct((4, D, ws), BF16)],
        scratch_shapes=[pltpu.VMEM((tm, ws), F32)],
        compiler_params=_params(("parallel", "parallel", "arbitrary")),
    )(ht, dproj4)


def _proj_fwd(x, g, w):
    s = x.shape[0]
    tm, tn = min(512, s), 896

    def body(x_ref, g_ref, w_ref, o_ref, h_ref, hs):
        @pl.when(pl.program_id(1) == 0)
        def _():
            h = _rms(x_ref[...], g_ref[...]).astype(BF16)
            hs[...] = h
            h_ref[...] = h

        o_ref[...] = jnp.dot(hs[...], w_ref[...], preferred_element_type=F32)

    return pl.pallas_call(
        body, name="proj_fwd", grid=(s // tm, NIN // tn),
        in_specs=[_bs((tm, D), lambda i, j: (i, 0)), _bs((1, D), lambda i, j: (0, 0)), _bs((D, tn), lambda i, j: (0, j))],
        out_specs=[_bs((tm, tn), lambda i, j: (i, j)), _bs((tm, D), lambda i, j: (i, 0))],
        out_shape=[jax.ShapeDtypeStruct((s, NIN), F32), jax.ShapeDtypeStruct((s, D), BF16)],
        scratch_shapes=[pltpu.VMEM((tm, D), BF16)],
        compiler_params=_params(("parallel", "arbitrary")),
    )(x, g, w)


def _rope(t, cos, sin, rot):
    return t * cos + _hdot(t, rot) * sin


def _attn_tile(qs, ks, vs, gs, sinks, gq, gk, cq, sq, ck, sk, mask, rot):
    outs = []
    for hk in range(AKV):
        kh = _rope(_rms(ks[hk], gk), ck, sk, rot)
        for g in range(AH // AKV):
            h = hk * (AH // AKV) + g
            qh = _rope(_rms(qs[h], gq), cq, sq, rot)
            sc = _mm_nt(qh, kh) * (HD ** -0.5)
            sc = jnp.where(mask, sc, NEG_INF)
            mx = lax.stop_gradient(jnp.maximum(jnp.max(sc, axis=-1, keepdims=True), sinks[h]))
            p = jnp.exp(sc - mx)
            den = jnp.sum(p, axis=-1, keepdims=True) + jnp.exp(sinks[h] - mx)
            o = _mm_nn(p / den, vs[hk])
            outs.append(o * _silu(gs[h]))
    return outs


def _attn_load(n, s, aq_ref, ak_ref, av_ref, ag_refs, cos_ref, sin_ref, sink_ref):
    start = pl.multiple_of(jnp.clip((n - 1) * QB, 0, s - WIN), QB)
    q0 = pl.multiple_of(n * QB, QB)
    qs = [aq_ref[:, h * HD:(h + 1) * HD] for h in range(AH)]
    ks = [ak_ref[pl.ds(start, WIN), h * HD:(h + 1) * HD] for h in range(AKV)]
    vs = [av_ref[pl.ds(start, WIN), h * HD:(h + 1) * HD] for h in range(AKV)]
    gs = [ag_refs[h // 4][:, (h % 4) * HD:(h % 4 + 1) * HD] for h in range(AH)]
    sinks = [sink_ref[0:1, h:h + 1] for h in range(AH)]
    cq, sq = cos_ref[pl.ds(q0, QB), :], sin_ref[pl.ds(q0, QB), :]
    ck, sk = cos_ref[pl.ds(start, WIN), :], sin_ref[pl.ds(start, WIN), :]
    qpos = q0 + lax.broadcasted_iota(jnp.int32, (QB, WIN), 0)
    kpos = start + lax.broadcasted_iota(jnp.int32, (QB, WIN), 1)
    mask = jnp.abs(kpos - qpos) <= QB
    return start, qs, ks, vs, gs, sinks, cq, sq, ck, sk, mask


def _attn_specs(s):
    return [
        _bs((QB, 768), lambda n: (n, 0)),
        _bs((s, 256), lambda n: (0, C_AK // 256)),
        _bs((s, 256), lambda n: (0, C_AV // 256)),
        _bs((QB, 256), lambda n: (n, C_AG // 256)),
        _bs((QB, 256), lambda n: (n, C_AG // 256 + 1)),
        _bs((QB, 256), lambda n: (n, C_AG // 256 + 2)),
        _bs((s, HD), lambda n: (0, 0)),
        _bs((s, HD), lambda n: (0, 0)),
        _bs((1, HD), lambda n: (0, 0)),
        _bs((1, HD), lambda n: (0, 0)),
        _bs((1, AH), lambda n: (0, 0)),
    ]


def _attn_fwd(proj, cos, sin, gq, gk, sink):
    s = proj.shape[0]

    def body(aq_ref, ak_ref, av_ref, ag0, ag1, ag2, cos_ref, sin_ref, gq_ref, gk_ref, sink_ref, o_ref):
        n = pl.program_id(0)
        _, qs, ks, vs, gs, sinks, cq, sq, ck, sk, mask = _attn_load(
            n, s, aq_ref, ak_ref, av_ref, (ag0, ag1, ag2), cos_ref, sin_ref, sink_ref)
        outs = _attn_tile(qs, ks, vs, gs, sinks, gq_ref[...], gk_ref[...], cq, sq, ck, sk, mask, _rot_matrix())
        for h in range(AH):
            o_ref[:, h * HD:(h + 1) * HD] = outs[h]

    return pl.pallas_call(
        body, name="attn_fwd", grid=(s // QB,),
        in_specs=_attn_specs(s), out_specs=_bs((QB, 768), lambda n: (n, 0)),
        out_shape=jax.ShapeDtypeStruct((s, 768), F32),
        compiler_params=_params(("arbitrary",)),
    )(proj, proj, proj, proj, proj, proj, cos, sin, gq, gk, sink)


def _attn_bwd(proj, cos, sin, gq, gk, sink, dy):
    s = proj.shape[0]

    def body(aq_ref, ak_ref, av_ref, ag0, ag1, ag2, cos_ref, sin_ref, gq_ref, gk_ref, sink_ref, dy_ref,
             daq_ref, dak_ref, dav_ref, dag_ref, dgq_ref, dgk_ref, dsink_ref):
        n = pl.program_id(0)

        @pl.when(n == 0)
        def _():
            dak_ref[...] = jnp.zeros_like(dak_ref)
            dav_ref[...] = jnp.zeros_like(dav_ref)
            dgq_ref[...] = jnp.zeros_like(dgq_ref)
            dgk_ref[...] = jnp.zeros_like(dgk_ref)
            dsink_ref[...] = jnp.zeros_like(dsink_ref)

        start, qs, ks, vs, gs, sinks, cq, sq, ck, sk, mask = _attn_load(
            n, s, aq_ref, ak_ref, av_ref, (ag0, ag1, ag2), cos_ref, sin_ref, sink_ref)
        rot = _rot_matrix()

        def f(qs, ks, vs, gs, sinks, gq, gk):
            return _attn_tile(qs, ks, vs, gs, sinks, gq, gk, cq, sq, ck, sk, mask, rot)

        _, vjp = jax.vjp(f, qs, ks, vs, gs, sinks, gq_ref[...], gk_ref[...])
        dys = [dy_ref[:, h * HD:(h + 1) * HD] for h in range(AH)]
        dqs, dks, dvs, dgs, dsinks, dgq, dgk = vjp(dys)
        for h in range(AH):
            daq_ref[:, h * HD:(h + 1) * HD] = dqs[h]
            dag_ref[:, h * HD:(h + 1) * HD] = dgs[h]
            dsink_ref[0:1, h:h + 1] += dsinks[h]
        for h in range(AKV):
            dak_ref[pl.ds(start, WIN), h * HD:(h + 1) * HD] += dks[h]
            dav_ref[pl.ds(start, WIN), h * HD:(h + 1) * HD] += dvs[h]
        dgq_ref[...] += dgq
        dgk_ref[...] += dgk

    whole = lambda shape: _bs(shape, lambda n: (0, 0))
    return pl.pallas_call(
        body, name="attn_bwd", grid=(s // QB,),
        in_specs=_attn_specs(s) + [_bs((QB, 768), lambda n: (n, 0))],
        out_specs=[_bs((QB, 768), lambda n: (n, 0)), whole((s, 256)), whole((s, 256)), _bs((QB, 768), lambda n: (n, 0)),
                   whole((1, HD)), whole((1, HD)), whole((1, AH))],
        out_shape=[jax.ShapeDtypeStruct((s, 768), F32), jax.ShapeDtypeStruct((s, 256), F32),
                   jax.ShapeDtypeStruct((s, 256), F32), jax.ShapeDtypeStruct((s, 768), F32),
                   jax.ShapeDtypeStruct((1, HD), F32), jax.ShapeDtypeStruct((1, HD), F32),
                   jax.ShapeDtypeStruct((1, AH), F32)],
        compiler_params=_params(("arbitrary",)),
    )(proj, proj, proj, proj, proj, proj, cos, sin, gq, gk, sink, dy)


def _mem_kv(mem, g, w):
    def body(m_ref, g_ref, w_ref, o_ref, mn_ref):
        mn = _rms(m_ref[...], g_ref[...]).astype(BF16)
        mn_ref[...] = mn
        o_ref[...] = jnp.dot(mn, w_ref[...], preferred_element_type=F32)

    return pl.pallas_call(
        body, name="mem_kv",
        out_shape=[jax.ShapeDtypeStruct((NMEM, 2 * XW), F32), jax.ShapeDtypeStruct((NMEM, D), BF16)],
        compiler_params=_params(),
    )(mem, g, w)


def _xattn_tile(qs, gs, kms, vms, gxq, gxk):
    outs = []
    for h in range(XH):
        q = _rms(qs[h], gxq)
        km = _rms(kms[h], gxk)
        sc = _mm_nt(q, km) * (XD ** -0.5)
        mx = lax.stop_gradient(jnp.max(sc, axis=-1, keepdims=True))
        p = jnp.exp(sc - mx)
        p = p / jnp.sum(p, axis=-1, keepdims=True)
        outs.append(_mm_nn(p, vms[h]) * _silu(gs[h]))
    return outs


XT = 256


def _xattn_specs():
    return [
        _bs((XT, 256), lambda i: (i, C_XQ // 256)), _bs((XT, 256), lambda i: (i, C_XQ // 256 + 1)),
        _bs((XT, 256), lambda i: (i, C_XG // 256)), _bs((XT, 256), lambda i: (i, C_XG // 256 + 1)),
        _bs((NMEM, 2 * XW), lambda i: (0, 0)),
        _bs((1, XD), lambda i: (0, 0)), _bs((1, XD), lambda i: (0, 0)),
    ]


def _xattn_load(q0, q1, g0, g1, mkv_ref):
    qs = [(q0, q1)[h // 2][:, (h % 2) * XD:(h % 2 + 1) * XD] for h in range(XH)]
    gs = [(g0, g1)[h // 2][:, (h % 2) * XD:(h % 2 + 1) * XD] for h in range(XH)]
    kms = [mkv_ref[:, h * XD:(h + 1) * XD] for h in range(XH)]
    vms = [mkv_ref[:, XW + h * XD:XW + (h + 1) * XD] for h in range(XH)]
    return qs, gs, kms, vms


def _xattn_fwd(proj, mkv, gxq, gxk):
    s = proj.shape[0]

    def body(q0, q1, g0, g1, mkv_ref, gxq_ref, gxk_ref, o_ref):
        qs, gs, kms, vms = _xattn_load(q0, q1, g0, g1, mkv_ref)
        outs = _xattn_tile(qs, gs, kms, vms, gxq_ref[...], gxk_ref[...])
        for h in range(XH):
            o_ref[:, h * XD:(h + 1) * XD] = outs[h]

    return pl.pallas_call(
        body, name="xattn_fwd", grid=(s // XT,),
        in_specs=_xattn_specs(), out_specs=_bs((XT, XW), lambda i: (i, 0)),
        out_shape=jax.ShapeDtypeStruct((s, XW), F32),
        compiler_params=_params(("arbitrary",)),
    )(proj, proj, proj, proj, mkv, gxq, gxk)


def _xattn_bwd(proj, mkv, gxq, gxk, dy):
    s = proj.shape[0]

    def body(q0, q1, g0, g1, mkv_ref, gxq_ref, gxk_ref, dy_ref, dq_ref, dg_ref, dmkv_ref, dgxq_ref, dgxk_ref):
        @pl.when(pl.program_id(0) == 0)
        def _():
            dmkv_ref[...] = jnp.zeros_like(dmkv_ref)
            dgxq_ref[...] = jnp.zeros_like(dgxq_ref)
            dgxk_ref[...] = jnp.zeros_like(dgxk_ref)

        qs, gs, kms, vms = _xattn_load(q0, q1, g0, g1, mkv_ref)
        _, vjp = jax.vjp(_xattn_tile, qs, gs, kms, vms, gxq_ref[...], gxk_ref[...])
        dqs, dgs, dkms, dvms, dgxq, dgxk = vjp([dy_ref[:, h * XD:(h + 1) * XD] for h in range(XH)])
        for h in range(XH):
            dq_ref[:, h * XD:(h + 1) * XD] = dqs[h]
            dg_ref[:, h * XD:(h + 1) * XD] = dgs[h]
            dmkv_ref[:, h * XD:(h + 1) * XD] += dkms[h]
            dmkv_ref[:, XW + h * XD:XW + (h + 1) * XD] += dvms[h]
        dgxq_ref[...] += dgxq
        dgxk_ref[...] += dgxk

    whole = lambda shape: _bs(shape, lambda i: (0, 0))
    return pl.pallas_call(
        body, name="xattn_bwd", grid=(s // XT,),
        in_specs=_xattn_specs() + [_bs((XT, XW), lambda i: (i, 0))],
        out_specs=[_bs((XT, XW), lambda i: (i, 0)), _bs((XT, XW), lambda i: (i, 0)), whole((NMEM, 2 * XW)),
                   whole((1, XD)), whole((1, XD))],
        out_shape=[jax.ShapeDtypeStruct((s, XW), F32), jax.ShapeDtypeStruct((s, XW), F32),
                   jax.ShapeDtypeStruct((NMEM, 2 * XW), F32), jax.ShapeDtypeStruct((1, XD), F32),
                   jax.ShapeDtypeStruct((1, XD), F32)],
        compiler_params=_params(("arbitrary",)),
    )(proj, proj, proj, proj, mkv, gxq, gxk, dy)


def _mem_bwd(mem, g, dmn):
    def body(m_ref, dmn_ref, o_ref):
        m = m_ref[...]
        r = lax.rsqrt(jnp.mean(m * m, axis=-1, keepdims=True) + NORM_EPS)
        o_ref[...] = jnp.sum(dmn_ref[...] * m * r, axis=0, keepdims=True)

    del g
    return pl.pallas_call(body, name="mem_norm_bwd", out_shape=jax.ShapeDtypeStruct((1, D), F32),
                          compiler_params=_params())(mem, dmn)


SHIFT_W = 512


def _shift_rows(p, s):
    row = lax.broadcasted_iota(jnp.int32, p.shape, 0)
    prev = jnp.where(row == 0, 0.0, pltpu.roll(p, 1, 0))
    nxt = jnp.where(row == s - 1, 0.0, pltpu.roll(p, s - 1, 0))
    return prev, nxt


def _shift_fwd(proj, mu):
    s = proj.shape[0]

    def body(p_ref, mu_ref, o_ref):
        p = p_ref[...]
        prev, nxt = _shift_rows(p, s)
        o_ref[...] = p + mu_ref[...] * (0.5 * (prev + nxt) - p)

    return pl.pallas_call(
        body, name="shift_fwd", grid=(RSW // SHIFT_W,),
        in_specs=[_bs((s, SHIFT_W), lambda j: (0, C_RS // SHIFT_W + j)), _bs((1, SHIFT_W), lambda j: (0, j))],
        out_specs=_bs((s, SHIFT_W), lambda j: (0, j)),
        out_shape=jax.ShapeDtypeStruct((s, RSW), F32),
        compiler_params=_params(("parallel",)),
    )(proj, mu)


def _shift_bwd(proj, mu, dps):
    s = proj.shape[0]

    def body(p_ref, mu_ref, g_ref, o_ref, dmu_ref):
        p, g, mu_v = p_ref[...], g_ref[...], mu_ref[...]
        prev, nxt = _shift_rows(p, s)
        dmu_ref[...] = jnp.sum(g * (0.5 * (prev + nxt) - p), axis=0, keepdims=True)
        mg = mu_v * g
        down, up = _shift_rows(mg, s)
        o_ref[...] = g * (1.0 - mu_v) + 0.5 * (down + up)

    return pl.pallas_call(
        body, name="shift_bwd", grid=(RSW // SHIFT_W,),
        in_specs=[_bs((s, SHIFT_W), lambda j: (0, C_RS // SHIFT_W + j)), _bs((1, SHIFT_W), lambda j: (0, j)),
                  _bs((s, SHIFT_W), lambda j: (0, j))],
        out_specs=[_bs((s, SHIFT_W), lambda j: (0, j)), _bs((1, SHIFT_W), lambda j: (0, j))],
        out_shape=[jax.ShapeDtypeStruct((s, RSW), F32), jax.ShapeDtypeStruct((1, RSW), F32)],
        compiler_params=_params(("parallel",)),
    )(proj, mu, dps)


def _pre_tile(k, wf, wb, af, ab, k_k, k_a, w0s, w2s, a0s, a2s, seg):
    kx = k * k_k
    ss = _hdot(kx * kx, seg)
    kk = kx / jnp.maximum(jnp.sqrt(ss), 1e-12)
    outs = [kk]
    for d, (w_in, a_in) in enumerate(((wf, af), (wb, ab))):
        z = w0s[d] + _mm_nn(jnp.tanh(w_in), w2s[d])
        wd = -_softplus(-z) - 0.5
        dec = jnp.exp(-jnp.exp(wd))
        ad = jax.nn.sigmoid(a0s[d] + _mm_nn(a_in, a2s[d]))
        kd = k * (1.0 + (ad - 1.0) * k_a)
        outs += [dec, kd, kk * ad]
    return outs


PT = 256


def _pre_load(ps_ref, kk_ref, ka_ref, w0_ref, w2_ref, a0_ref, a2_ref):
    k = ps_ref[:, RW:2 * RW]
    wf, wb = ps_ref[:, 3 * RW:3 * RW + 64], ps_ref[:, 3 * RW + 64:3 * RW + 128]
    af, ab = ps_ref[:, 3 * RW + 128:3 * RW + 192], ps_ref[:, 3 * RW + 192:3 * RW + 256]
    w0s = [w0_ref[0:1, :], w0_ref[1:2, :]]
    a0s = [a0_ref[0:1, :], a0_ref[1:2, :]]
    w2s = [w2_ref[0], w2_ref[1]]
    a2s = [a2_ref[0], a2_ref[1]]
    return (k, wf, wb, af, ab, kk_ref[...], ka_ref[...], w0s, w2s, a0s, a2s)


def _pre_specs():
    c = lambda shape: _bs(shape, lambda i: tuple(0 for _ in shape))
    return [_bs((PT, RSW), lambda i: (i, 0)), c((1, RW)), c((1, RW)), c((2, RW)), c((2, 64, RW)), c((2, RW)),
            c((2, 64, RW))]


def _pre_fwd(ps, k_k, k_a, w0, w2, a0, a2):
    s = ps.shape[0]

    def body(ps_ref, kk_ref, ka_ref, w0_ref, w2_ref, a0_ref, a2_ref, *outs):
        args = _pre_load(ps_ref, kk_ref, ka_ref, w0_ref, w2_ref, a0_ref, a2_ref)
        res = _pre_tile(*args, _seg_matrix(RW, HD))
        for o_ref, v in zip(outs, res):
            o_ref[...] = v

    return pl.pallas_call(
        body, name="rwkv_pre_fwd", grid=(s // PT,),
        in_specs=_pre_specs(), out_specs=[_bs((PT, RW), lambda i: (i, 0))] * 7,
        out_shape=[jax.ShapeDtypeStruct((s, RW), F32)] * 7,
        compiler_params=_params(("parallel",)),
    )(ps, k_k, k_a, w0, w2, a0, a2)


def _pre_bwd(ps, k_k, k_a, w0, w2, a0, a2, dr, dv, cts):
    s = ps.shape[0]

    def body(ps_ref, kk_ref, ka_ref, w0_ref, w2_ref, a0_ref, a2_ref, dr_ref, dv_ref, c0, c1, c2, c3, c4, c5, c6,
             dps_ref, dkk_ref, dka_ref, dw0_ref, dw2_ref, da0_ref, da2_ref):
        @pl.when(pl.program_id(0) == 0)
        def _():
            for r in (dkk_ref, dka_ref, dw0_ref, dw2_ref, da0_ref, da2_ref):
                r[...] = jnp.zeros_like(r)

        args = _pre_load(ps_ref, kk_ref, ka_ref, w0_ref, w2_ref, a0_ref, a2_ref)
        seg = _seg_matrix(RW, HD)
        _, vjp = jax.vjp(lambda *a: _pre_tile(*a, seg), *args)
        dk, dwf, dwb, daf, dab, dk_k, dk_a, dw0s, dw2s, da0s, da2s = vjp([c[...] for c in (c0, c1, c2, c3, c4, c5, c6)])
        dps_ref[:, 0:RW] = dr_ref[...]
        dps_ref[:, RW:2 * RW] = dk
        dps_ref[:, 2 * RW:3 * RW] = dv_ref[...]
        for j, t in enumerate((dwf, dwb, daf, dab)):
            dps_ref[:, 3 * RW + 64 * j:3 * RW + 64 * (j + 1)] = t
        dkk_ref[...] += dk_k
        dka_ref[...] += dk_a
        for d in range(2):
            dw0_ref[d:d + 1, :] += dw0s[d]
            da0_ref[d:d + 1, :] += da0s[d]
            dw2_ref[d] += dw2s[d]
            da2_ref[d] += da2s[d]

    c = lambda shape: _bs(shape, lambda i: tuple(0 for _ in shape))
    row = _bs((PT, RW), lambda i: (i, 0))
    return pl.pallas_call(
        body, name="rwkv_pre_bwd", grid=(s // PT,),
        in_specs=_pre_specs() + [row] * 9,
        out_specs=[_bs((PT, RSW), lambda i: (i, 0)), c((1, RW)), c((1, RW)), c((2, RW)), c((2, 64, RW)), c((2, RW)),
                   c((2, 64, RW))],
        out_shape=[jax.ShapeDtypeStruct((s, RSW), F32), jax.ShapeDtypeStruct((1, RW), F32),
                   jax.ShapeDtypeStruct((1, RW), F32), jax.ShapeDtypeStruct((2, RW), F32),
                   jax.ShapeDtypeStruct((2, 64, RW), F32), jax.ShapeDtypeStruct((2, RW), F32),
                   jax.ShapeDtypeStruct((2, 64, RW), F32)],
        compiler_params=_params(("arbitrary",)),
    )(ps, k_k, k_a, w0, w2, a0, a2, dr, dv, *cts)


def _post_tile(y0, y1, r, v, kd0, kd1, rg, r_k, ln_w, ln_b, seg):
    ysum = y0 + y1
    bonus = (_hdot(r * kd0 * r_k, seg) + _hdot(r * kd1 * r_k, seg)) * v
    mean = _hdot(ysum, seg) * (1.0 / HD)
    cen = ysum - mean
    var = _hdot(cen * cen, seg) * (1.0 / HD)
    y = cen * lax.rsqrt(var + GN_EPS) * ln_w + ln_b + bonus
    return y * _silu(rg)


def _post_specs():
    row = _bs((PT, RW), lambda i: (i, 0))
    c = _bs((1, RW), lambda i: (0, 0))
    return [row, row, _bs((PT, RW), lambda i: (i, 0)), _bs((PT, RW), lambda i: (i, 2)), row, row,
            _bs((PT, RW), lambda i: (i, C_RG // RW)), c, c, c]


def _post_fwd(y0, y1, ps, kd0, kd1, proj, r_k, ln_w, ln_b):
    s = ps.shape[0]

    def body(y0_ref, y1_ref, r_ref, v_ref, kd0_ref, kd1_ref, rg_ref, rk_ref, lw_ref, lb_ref, o_ref):
        o_ref[...] = _post_tile(y0_ref[...], y1_ref[...], r_ref[...], v_ref[...], kd0_ref[...], kd1_ref[...],
                                rg_ref[...], rk_ref[...], lw_ref[...], lb_ref[...], _seg_matrix(RW, HD))

    return pl.pallas_call(
        body, name="rwkv_post_fwd", grid=(s // PT,),
        in_specs=_post_specs(), out_specs=_bs((PT, RW), lambda i: (i, 0)),
        out_shape=jax.ShapeDtypeStruct((s, RW), F32),
        compiler_params=_params(("parallel",)),
    )(y0, y1, ps, ps, kd0, kd1, proj, r_k, ln_w, ln_b)


def _post_bwd(y0, y1, ps, kd0, kd1, proj, r_k, ln_w, ln_b, dy):
    s = ps.shape[0]

    def body(y0_ref, y1_ref, r_ref, v_ref, kd0_ref, kd1_ref, rg_ref, rk_ref, lw_ref, lb_ref, dy_ref,
             dys_ref, dr_ref, dv_ref, dkd0_ref, dkd1_ref, drg_ref, drk_ref, dlw_ref, dlb_ref):
        @pl.when(pl.program_id(0) == 0)
        def _():
            for r in (drk_ref, dlw_ref, dlb_ref):
                r[...] = jnp.zeros_like(r)

        seg = _seg_matrix(RW, HD)
        args = [t[...] for t in (y0_ref, y1_ref, r_ref, v_ref, kd0_ref, kd1_ref, rg_ref, rk_ref, lw_ref, lb_ref)]
        _, vjp = jax.vjp(lambda *a: _post_tile(*a, seg), *args)
        dy0, _, dr, dv, dkd0, dkd1, drg, drk, dlw, dlb = vjp(dy_ref[...])
        dys_ref[...] = dy0
        dr_ref[...] = dr
        dv_ref[...] = dv
        dkd0_ref[...] = dkd0
        dkd1_ref[...] = dkd1
        drg_ref[...] = drg
        drk_ref[...] += drk
        dlw_ref[...] += dlw
        dlb_ref[...] += dlb

    row = _bs((PT, RW), lambda i: (i, 0))
    c = _bs((1, RW), lambda i: (0, 0))
    return pl.pallas_call(
        body, name="rwkv_post_bwd", grid=(s // PT,),
        in_specs=_post_specs() + [row], out_specs=[row] * 6 + [c] * 3,
        out_shape=[jax.ShapeDtypeStruct((s, RW), F32)] * 6 + [jax.ShapeDtypeStruct((1, RW), F32)] * 3,
        compiler_params=_params(("arbitrary",)),
    )(y0, y1, ps, ps, kd0, kd1, proj, r_k, ln_w, ln_b, dy)


def _ones2():
    r = lax.broadcasted_iota(jnp.int32, (256, 128), 0) % 128 // HD
    c = lax.broadcasted_iota(jnp.int32, (256, 128), 1) // HD
    return (r == c).astype(BF16)


def _split(p):
    hi = p.astype(BF16)
    lo = (p - hi.astype(F32)).astype(BF16)
    return jnp.concatenate([hi, lo], axis=1)


def _to_t8(a):
    s = a.shape[0]
    t = a.reshape(s // 8, 8, NPAIR, 2, HD).transpose(0, 2, 4, 3, 1)
    t = jnp.pad(t, ((0, 0), (0, 0), (0, 0), (0, 0), (0, HD - 8))).reshape(s // 8, NPAIR, HD, 128)
    hi = t.astype(BF16)
    lo = (t - hi.astype(F32)).astype(BF16)
    return jnp.concatenate([hi, lo], axis=-1)


def _from_t8(t8):
    g = t8.shape[0]
    t = t8.reshape(g, NPAIR, HD, 2, HD)[..., :8]
    return t.transpose(0, 4, 1, 3, 2).reshape(g * 8, RW)


def _scan_specs(direction, nc, fwd_order):
    def tb(c):
        sc = c if fwd_order else nc - 1 - c
        return sc if direction == 0 else nc - 1 - sc

    row = _bs((TC, RW), lambda c: (tb(c), 0))
    rowv = _bs((TC, RW), lambda c: (tb(c), 2))
    return row, rowv


def _put_t8(ref, g, u, tiles):
    for p in range(NPAIR):
        ref[g, p, :, u:u + 1] = tiles[p][:, u:u + 1]
        ref[g, p, :, HD + u:HD + u + 1] = tiles[p][:, HD + u:HD + u + 1]


def _scan_fwd(dec, kd, b, ps, kk, vl, direction):
    s = dec.shape[0]
    nc, ng = s // TC, TC // 8
    row, t8_in, t8_out = _scan_specs(direction, nc, True)
    n = NPAIR * HD

    def body(dec_ref, kd_ref, b_ref, r_ref, kk_ref, vl_ref, y8_ref, ck_ref, st):
        @pl.when(pl.program_id(0) == 0)
        def _():
            st[...] = jnp.zeros_like(st)

        ck_ref[0] = st[...]
        ones2 = _ones2()
        lane_u = lax.broadcasted_iota(jnp.int32, (HD, 256), 1) % HD
        tiles = lambda res, k: [res[k * n + p * HD:k * n + (p + 1) * HD] for p in range(NPAIR)]

        def group(gi, carry):
            g = gi if direction == 0 else ng - 1 - gi
            rows8 = pl.ds(pl.multiple_of(g * 8, 8), 8)
            d8, k8, b8, r8, kk8 = (q[rows8, :] for q in (dec_ref, kd_ref, b_ref, r_ref, kk_ref))
            pc = [slice(p * 128, (p + 1) * 128) for p in range(NPAIR)]
            ss = [st[p] for p in range(NPAIR)]
            u_prev = None
            for ui in range(8):
                u = ui if direction == 0 else 7 - ui
                lhs = [_split(ss[p] * kk8[u:u + 1, pc[p]]) for p in range(NPAIR)]
                for p in range(NPAIR):
                    vt = vl_ref[g, p]
                    lhs.append(jnp.where(lane_u == u, vt, jnp.zeros_like(vt)))
                if u_prev is not None:
                    lhs += [_split(ss[p] * r8[u_prev:u_prev + 1, pc[p]]) for p in range(NPAIR)]
                res = jnp.dot(jnp.concatenate(lhs, axis=0), ones2, preferred_element_type=F32)
                if u_prev is not None:
                    _put_t8(y8_ref, g, u_prev, tiles(res, 2))
                sa, vb = tiles(res, 0), tiles(res, 1)
                for p in range(NPAIR):
                    ss[p] = ss[p] * d8[u:u + 1, pc[p]] - sa[p] * b8[u:u + 1, pc[p]] + vb[p] * k8[u:u + 1, pc[p]]
                u_prev = u
            lhs = [_split(ss[p] * r8[u_prev:u_prev + 1, pc[p]]) for p in range(NPAIR)]
            res = jnp.dot(jnp.concatenate(lhs, axis=0), ones2, preferred_element_type=F32)
            _put_t8(y8_ref, g, u_prev, tiles(res, 0))
            for p in range(NPAIR):
                st[p] = ss[p]
            return carry

        lax.fori_loop(0, ng, group, 0)

    return pl.pallas_call(
        body, name=f"rwkv_scan_fwd{direction}", grid=(nc,),
        in_specs=[row, row, row, row, row, t8_in],
        out_specs=[t8_out, _bs((1, NPAIR, HD, 128), lambda c: (c, 0, 0, 0))],
        out_shape=[jax.ShapeDtypeStruct((s // 8, NPAIR, HD, 128), F32),
                   jax.ShapeDtypeStruct((nc, NPAIR, HD, 128), F32)],
        scratch_shapes=[pltpu.VMEM((NPAIR, HD, 128), F32)],
        compiler_params=_params(("arbitrary",)),
    )(dec, kd, b, ps, kk, vl)


def _scan_bwd(dec, kd, b, ps, kk, vl, dyl, ck, direction):
    s = dec.shape[0]
    nc, ng = s // TC, TC // 8
    row, t8_in, t8_out = _scan_specs(direction, nc, False)
    n = NPAIR * HD

    def body(dec_ref, kd_ref, b_ref, r_ref, kk_ref, vl_ref, dyl_ref, ck_ref,
             dr_ref, dd_ref, db_ref, dk_ref, dkk_ref, dv8_ref, st, sa_s, vb_s, dy_s, ds):
        @pl.when(pl.program_id(0) == 0)
        def _():
            ds[...] = jnp.zeros_like(ds)

        st[0] = ck_ref[0]
        ones2 = _ones2()
        lane_u = lax.broadcasted_iota(jnp.int32, (HD, 256), 1) % HD
        row_id = lax.broadcasted_iota(jnp.int32, (8, 128), 0)
        pc = [slice(p * 128, (p + 1) * 128) for p in range(NPAIR)]
        tiles = lambda res, k: [res[k * n + p * HD:k * n + (p + 1) * HD] for p in range(NPAIR)]

        def fgroup(gi, carry):
            g = gi if direction == 0 else ng - 1 - gi
            rows8 = pl.ds(pl.multiple_of(g * 8, 8), 8)
            d8, k8, b8, kk8 = (q[rows8, :] for q in (dec_ref, kd_ref, b_ref, kk_ref))
            ss = [st[gi * 8, p] for p in range(NPAIR)]
            for ui in range(8):
                u = ui if direction == 0 else 7 - ui
                i = gi * 8 + ui
                lhs = [_split(ss[p] * kk8[u:u + 1, pc[p]]) for p in range(NPAIR)]
                for ref in (vl_ref, dyl_ref):
                    for p in range(NPAIR):
                        t = ref[g, p]
                        lhs.append(jnp.where(lane_u == u, t, jnp.zeros_like(t)))
                res = jnp.dot(jnp.concatenate(lhs, axis=0), ones2, preferred_element_type=F32)
                sa, vb, dyb = tiles(res, 0), tiles(res, 1), tiles(res, 2)
                for p in range(NPAIR):
                    sa_s[i, p] = sa[p]
                    vb_s[i, p] = vb[p]
                    dy_s[i, p] = dyb[p]
                    ss[p] = ss[p] * d8[u:u + 1, pc[p]] - sa[p] * b8[u:u + 1, pc[p]] + vb[p] * k8[u:u + 1, pc[p]]
                    st[i + 1, p] = ss[p]
            return carry

        lax.fori_loop(0, ng, fgroup, 0)

        def bgroup(gj, carry):
            gi = ng - 1 - gj
            g = gi if direction == 0 else ng - 1 - gi
            rows8 = pl.ds(pl.multiple_of(g * 8, 8), 8)
            d8, k8, b8, r8, kk8 = (q[rows8, :] for q in (dec_ref, kd_ref, b_ref, r_ref, kk_ref))
            dss = [ds[p] for p in range(NPAIR)]
            acc = [[jnp.zeros((8, 128), F32) for _ in range(5)] for _ in range(NPAIR)]
            for uj in range(8):
                ui = 7 - uj
                u = ui if direction == 0 else 7 - ui
                i = gi * 8 + ui
                dyb = [dy_s[i, p] for p in range(NPAIR)]
                for p in range(NPAIR):
                    dss[p] = dss[p] + dyb[p] * r8[u:u + 1, pc[p]]
                lhs = [_split(dss[p] * b8[u:u + 1, pc[p]]) for p in range(NPAIR)]
                lhs += [_split(dss[p] * k8[u:u + 1, pc[p]]) for p in range(NPAIR)]
                res = jnp.dot(jnp.concatenate(lhs, axis=0), ones2, preferred_element_type=F32)
                dsa, dvb = tiles(res, 0), tiles(res, 1)
                _put_t8(dv8_ref, g, u, dvb)
                for p in range(NPAIR):
                    sp, sn = st[i, p], st[i + 1, p]
                    outs = (jnp.sum(sn * dyb[p], axis=0, keepdims=True), jnp.sum(dss[p] * sp, axis=0, keepdims=True),
                            -jnp.sum(dss[p] * sa_s[i, p], axis=0, keepdims=True),
                            jnp.sum(dss[p] * vb_s[i, p], axis=0, keepdims=True),
                            -jnp.sum(sp * dsa[p], axis=0, keepdims=True))
                    acc[p] = [jnp.where(row_id == u, o, a_) for o, a_ in zip(outs, acc[p])]
                    dss[p] = dss[p] * d8[u:u + 1, pc[p]] - dsa[p] * kk8[u:u + 1, pc[p]]
            for p in range(NPAIR):
                ds[p] = dss[p]
                for o_ref, a_ in zip((dr_ref, dd_ref, db_ref, dk_ref, dkk_ref), acc[p]):
                    o_ref[rows8, pc[p]] = a_
            return carry

        lax.fori_loop(0, ng, bgroup, 0)

    chunk = lambda k: pltpu.VMEM((k, NPAIR, HD, 128), F32)
    return pl.pallas_call(
        body, name=f"rwkv_scan_bwd{direction}", grid=(nc,),
        in_specs=[row, row, row, row, row, t8_in, t8_in, _bs((1, NPAIR, HD, 128), lambda c: (nc - 1 - c, 0, 0, 0))],
        out_specs=[row] * 5 + [t8_out],
        out_shape=[jax.ShapeDtypeStruct((s, RW), F32)] * 5 + [jax.ShapeDtypeStruct((s // 8, NPAIR, HD, 128), F32)],
        scratch_shapes=[chunk(TC + 1), chunk(TC), chunk(TC), chunk(TC), pltpu.VMEM((NPAIR, HD, 128), F32)],
        compiler_params=_params(("arbitrary",)),
    )(dec, kd, b, ps, kk, vl, dyl, ck)


def _tiles(res, k):
    n = NPAIR * HD
    return [res[k * n + p * HD:k * n + (p + 1) * HD] for p in range(NPAIR)]


def _rows_to_tiles(src_ref, rows8, stage, out_s, base):
    for p in range(NPAIR):
        stage[base + p, 0:8, 0:HD] = src_ref[rows8, p * 128:p * 128 + HD]
        stage[base + p, HD:HD + 8, 0:HD] = src_ref[rows8, p * 128 + HD:(p + 1) * 128]
        out_s[base + p] = stage[base + p].T[0:HD].astype(BF16)


def _tiles_to_rows(tile_s, base, dst_ref, rows8):
    for p in range(NPAIR):
        t = jnp.concatenate([tile_s[base + p], jnp.zeros((HD, 128), F32)], axis=0).T
        dst_ref[rows8, p * 128:p * 128 + HD] = t[0:8, 0:HD]
        dst_ref[rows8, p * 128 + HD:(p + 1) * 128] = t[HD:HD + 8, 0:HD]


def _put_cols(tile_s, base, u, tiles):
    for p in range(NPAIR):
        tile_s[base + p, :, u:u + 1] = tiles[p][:, u:u + 1]
        tile_s[base + p, :, HD + u:HD + u + 1] = tiles[p][:, HD + u:HD + u + 1]


def _scan2_fwd(per_dir, ps, kk, gather=()):
    s = ps.shape[0]
    nc, ng = s // TC, TC // 8
    ngat = len(gather)
    in_specs, operands, out_specs, out_shape = [], [], [], []
    for d in (0, 1):
        row, rowv = _scan_specs(d, nc, True)
        in_specs += [row] * 5 + [rowv]
        operands += list(per_dir[d]) + [ps, kk, ps]
        out_specs += [row, _bs((1, NPAIR, HD, 128), lambda c: (c, 0, 0, 0))]
        out_shape += [jax.ShapeDtypeStruct((s, RW), F32), jax.ShapeDtypeStruct((nc, NPAIR, HD, 128), F32)]
    in_specs += [ANY] * ngat
    operands += list(gather)
    out_specs += [ANY] * ngat
    out_shape += _gather_out_shapes(gather)

    def body(*refs):
        ins = [refs[0:6], refs[6:12]]
        base = 12 + ngat
        y_refs, ck_refs = (refs[base], refs[base + 2]), (refs[base + 1], refs[base + 3])
        st, vt_s, yt_s, stage = refs[base + 4 + ngat:base + 8 + ngat]
        if ngat:
            g_start, g_forward, g_finish = _gather_phases(
                gather, refs[12:base], refs[base + 4:base + 4 + ngat], refs[base + 8 + ngat:])

        @pl.when(pl.program_id(0) == 0)
        def _():
            st[...] = jnp.zeros_like(st)
            yt_s[...] = jnp.zeros_like(yt_s)
            stage[...] = jnp.zeros_like(stage)
            if ngat:
                g_start()

        if ngat:
            @pl.when(pl.program_id(0) == nc // 2)
            def _():
                g_forward()

        for d in (0, 1):
            ck_refs[d][0] = st[d * NPAIR:(d + 1) * NPAIR]
        ones2 = _ones2()
        ones1 = ones2[0:128]
        lane_u = lax.broadcasted_iota(jnp.int32, (HD, 128), 1) % HD
        pc = [slice(p * 128, (p + 1) * 128) for p in range(NPAIR)]

        def group(gi, carry):
            gs = (gi, ng - 1 - gi)
            rows8 = [pl.ds(pl.multiple_of(gs[d] * 8, 8), 8) for d in (0, 1)]
            blk = [[q[rows8[d], :] for q in ins[d][:5]] for d in (0, 1)]
            for d in (0, 1):
                _rows_to_tiles(ins[d][5], rows8[d], stage, vt_s, d * NPAIR)
            ss = [[st[d * NPAIR + p] for p in range(NPAIR)] for d in (0, 1)]
            for ui in range(9):
                us, ups = (ui, 7 - ui), (ui - 1, 8 - ui)
                lhs1, where = [], {}
                for d in (0, 1):
                    if ui < 8:
                        where["sa", d] = len(lhs1) // NPAIR
                        lhs1 += [(ss[d][p] * blk[d][4][us[d]:us[d] + 1, pc[p]]).astype(BF16) for p in range(NPAIR)]
                        where["vb", d] = len(lhs1) // NPAIR
                        for p in range(NPAIR):
                            vt = vt_s[d * NPAIR + p]
                            lhs1.append(jnp.where(lane_u == us[d], vt, jnp.zeros_like(vt)))
                    if ui > 0:
                        where["y", d] = len(lhs1) // NPAIR
                        lhs1 += [(ss[d][p] * blk[d][3][ups[d]:ups[d] + 1, pc[p]]).astype(BF16) for p in range(NPAIR)]
                res1 = jnp.dot(jnp.concatenate(lhs1, axis=0), ones1, preferred_element_type=F32)
                for d in (0, 1):
                    d8, k8, b8, _, _ = blk[d]
                    u = us[d]
                    if ui < 8:
                        sa, vb = _tiles(res1, where["sa", d]), _tiles(res1, where["vb", d])
                        for p in range(NPAIR):
                            ss[d][p] = (ss[d][p] * d8[u:u + 1, pc[p]] - sa[p] * b8[u:u + 1, pc[p]]
                                        + vb[p] * k8[u:u + 1, pc[p]])
                    if ui > 0:
                        _put_cols(yt_s, d * NPAIR, ups[d], _tiles(res1, where["y", d]))
            for d in (0, 1):
                _tiles_to_rows(yt_s, d * NPAIR, y_refs[d], rows8[d])
                for p in range(NPAIR):
                    st[d * NPAIR + p] = ss[d][p]
            return carry

        for gi in range(ng):
            group(gi, 0)

        if ngat:
            @pl.when(pl.program_id(0) == nc - 1)
            def _():
                g_finish()

    outs = pl.pallas_call(
        body, name="rwkv_scan_fwd", grid=(nc,), in_specs=in_specs, out_specs=out_specs, out_shape=out_shape,
        scratch_shapes=[pltpu.VMEM((2 * NPAIR, HD, 128), F32), pltpu.VMEM((2 * NPAIR, HD, 128), BF16),
                        pltpu.VMEM((2 * NPAIR, HD, 128), F32), pltpu.VMEM((2 * NPAIR, 128, 128), F32)]
        + (_gather_sems(ngat) if ngat else []),
        compiler_params=pltpu.CompilerParams(dimension_semantics=("arbitrary",), vmem_limit_bytes=VMEM_LIMIT,
                                             has_side_effects=bool(ngat)),
    )(*operands)
    return [(outs[0], outs[1]), (outs[2], outs[3])], list(outs[4:])


def _scan2_bwd(per_dir, ps, kk, dy, scatter=()):
    s = ps.shape[0]
    nc, ng = s // TC, TC // 8
    nsc = len(scatter)
    in_specs, operands, out_specs, out_shape = [], [], [], []
    for d in (0, 1):
        row, rowv = _scan_specs(d, nc, False)
        dec, kd, b, ck = per_dir[d]
        in_specs += [row] * 5 + [rowv, row, _bs((1, NPAIR, HD, 128), lambda c: (nc - 1 - c, 0, 0, 0))]
        operands += [dec, kd, b, ps, kk, ps, dy, ck]
        out_specs += [row] * 6
        out_shape += [jax.ShapeDtypeStruct((s, RW), F32)] * 6
    in_specs += [ANY] * nsc
    operands += list(scatter)
    out_specs += [ANY] * nsc
    out_shape += _scatter_out_shapes(scatter)

    def body(*refs):
        ins = [refs[0:8], refs[8:16]]
        base = 16 + nsc
        outs = [refs[base:base + 6], refs[base + 6:base + 12]]
        st, sa_s, vb_s, dy_s, ds, vt_s, dyt_s, dvt_s, stage = refs[base + 12 + nsc:base + 21 + nsc]
        if nsc:
            s_start, s_finish = _scatter_phases(refs[16:base], refs[base + 12:base + 12 + nsc], refs[base + 21 + nsc:])

        @pl.when(pl.program_id(0) == 0)
        def _():
            dvt_s[...] = jnp.zeros_like(dvt_s)
            stage[...] = jnp.zeros_like(stage)
            ds[...] = jnp.zeros_like(ds)
            if nsc:
                s_start()

        for d in (0, 1):
            st[d * (TC + 1)] = ins[d][7][0]
        ones2 = _ones2()
        ones1 = ones2[0:128]
        lane_u = lax.broadcasted_iota(jnp.int32, (HD, 128), 1) % HD
        row_id = lax.broadcasted_iota(jnp.int32, (8, 128), 0)
        pc = [slice(p * 128, (p + 1) * 128) for p in range(NPAIR)]

        def load_rows(gs):
            return [[q[pl.ds(pl.multiple_of(gs[d] * 8, 8), 8), :] for q in ins[d][:5]] for d in (0, 1)]

        def fgroup(gi, carry):
            gs = (gi, ng - 1 - gi)
            blk = load_rows(gs)
            for d in (0, 1):
                rows8 = pl.ds(pl.multiple_of(gs[d] * 8, 8), 8)
                _rows_to_tiles(ins[d][5], rows8, stage, vt_s, d * NPAIR)
                _rows_to_tiles(ins[d][6], rows8, stage, dyt_s, d * NPAIR)
            ss = [[st[d * (TC + 1) + gi * 8, p] for p in range(NPAIR)] for d in (0, 1)]
            for ui in range(8):
                us = (ui, 7 - ui)
                i = gi * 8 + ui
                lhs1 = []
                for d in (0, 1):
                    kk8 = blk[d][4]
                    lhs1 += [(ss[d][p] * kk8[us[d]:us[d] + 1, pc[p]]).astype(BF16) for p in range(NPAIR)]
                    for tile_s in (vt_s, dyt_s):
                        for p in range(NPAIR):
                            t = tile_s[d * NPAIR + p]
                            lhs1.append(jnp.where(lane_u == us[d], t, jnp.zeros_like(t)))
                res1 = jnp.dot(jnp.concatenate(lhs1, axis=0), ones1, preferred_element_type=F32)
                for d in (0, 1):
                    d8, k8, b8, _, _ = blk[d]
                    u = us[d]
                    sa, vb, dyb = _tiles(res1, 3 * d), _tiles(res1, 3 * d + 1), _tiles(res1, 3 * d + 2)
                    for p in range(NPAIR):
                        sa_s[d * TC + i, p] = sa[p]
                        vb_s[d * TC + i, p] = vb[p]
                        dy_s[d * TC + i, p] = dyb[p]
                        ss[d][p] = ss[d][p] * d8[u:u + 1, pc[p]] - sa[p] * b8[u:u + 1, pc[p]] + vb[p] * k8[u:u + 1, pc[p]]
                        st[d * (TC + 1) + i + 1, p] = ss[d][p]
            return carry

        for gi in range(ng):
            fgroup(gi, 0)

        def bgroup(gj, carry):
            gi = ng - 1 - gj
            gs = (gi, ng - 1 - gi)
            blk = load_rows(gs)
            dss = [[ds[d * NPAIR + p] for p in range(NPAIR)] for d in (0, 1)]
            acc = [[[jnp.zeros((8, 128), F32) for _ in range(5)] for _ in range(NPAIR)] for _ in (0, 1)]
            for uj in range(8):
                ui = 7 - uj
                us = (ui, 7 - ui)
                i = gi * 8 + ui
                lhs1, dyb = [], [None, None]
                for d in (0, 1):
                    _, k8, b8, r8, _ = blk[d]
                    u = us[d]
                    dyb[d] = [dy_s[d * TC + i, p] for p in range(NPAIR)]
                    for p in range(NPAIR):
                        dss[d][p] = dss[d][p] + dyb[d][p] * r8[u:u + 1, pc[p]]
                    lhs1 += [(dss[d][p] * b8[u:u + 1, pc[p]]).astype(BF16) for p in range(NPAIR)]
                    lhs1 += [(dss[d][p] * k8[u:u + 1, pc[p]]).astype(BF16) for p in range(NPAIR)]
                res1 = jnp.dot(jnp.concatenate(lhs1, axis=0), ones1, preferred_element_type=F32)
                for d in (0, 1):
                    d8, _, _, _, kk8 = blk[d]
                    u = us[d]
                    dsa, dvb = _tiles(res1, 2 * d), _tiles(res1, 2 * d + 1)
                    _put_cols(dvt_s, d * NPAIR, u, dvb)
                    for p in range(NPAIR):
                        sp, sn = st[d * (TC + 1) + i, p], st[d * (TC + 1) + i + 1, p]
                        dsv = dss[d][p]
                        vals = (jnp.sum(sn * dyb[d][p], axis=0, keepdims=True), jnp.sum(dsv * sp, axis=0, keepdims=True),
                                -jnp.sum(dsv * sa_s[d * TC + i, p], axis=0, keepdims=True),
                                jnp.sum(dsv * vb_s[d * TC + i, p], axis=0, keepdims=True),
                                -jnp.sum(sp * dsa[p], axis=0, keepdims=True))
                        acc[d][p] = [jnp.where(row_id == u, o, a_) for o, a_ in zip(vals, acc[d][p])]
                        dss[d][p] = dsv * d8[u:u + 1, pc[p]] - dsa[p] * kk8[u:u + 1, pc[p]]
            for d in (0, 1):
                rows8 = pl.ds(pl.multiple_of(gs[d] * 8, 8), 8)
                _tiles_to_rows(dvt_s, d * NPAIR, outs[d][5], rows8)
                for p in range(NPAIR):
                    ds[d * NPAIR + p] = dss[d][p]
                    for o_ref, a_ in zip(outs[d][:5], acc[d][p]):
                        o_ref[rows8, pc[p]] = a_
            return carry

        for gj in range(ng):
            bgroup(gj, 0)

        if nsc:
            @pl.when(pl.program_id(0) == nc - 1)
            def _():
                s_finish()

    chunk = lambda k: pltpu.VMEM((k, NPAIR, HD, 128), F32)
    pairs = lambda w, dt: pltpu.VMEM((2 * NPAIR, HD, w), dt)
    res = pl.pallas_call(
        body, name="rwkv_scan_bwd", grid=(nc,), in_specs=in_specs, out_specs=out_specs, out_shape=out_shape,
        scratch_shapes=[chunk(2 * (TC + 1)), chunk(2 * TC), chunk(2 * TC), chunk(2 * TC), pairs(128, F32),
                        pairs(128, BF16), pairs(128, BF16), pairs(128, F32), pltpu.VMEM((2 * NPAIR, 128, 128), F32)]
        + _scatter_sems(nsc),
        compiler_params=pltpu.CompilerParams(dimension_semantics=("arbitrary",), vmem_limit_bytes=VMEM_LIMIT,
                                             has_side_effects=bool(nsc)),
    )(*operands)
    return [res[0:6], res[6:12]], list(res[12:])


MT = 256
MN = 256


def _merge_fwd(ya, yr, yx, wa, wr, wx, proj, gate_b):
    s = ya.shape[0]

    def body(ya_ref, yr_ref, yx_ref, wa_ref, wr_ref, wx_ref, m0, m1, m2, b0, b1, b2, o_ref):
        acc = jnp.zeros((MT, MN), F32)
        for y_ref, w_ref, m_ref, b_ref in ((ya_ref, wa_ref, m0, b0), (yr_ref, wr_ref, m1, b1), (yx_ref, wx_ref, m2, b2)):
            u = _dot(y_ref[...], w_ref[...], ((1,), (0,)))
            acc = acc + jax.nn.sigmoid(m_ref[...] + b_ref[...]) * u
        o_ref[...] = acc.astype(BF16)

    mg = lambda br: _bs((MT, MN), lambda i, j: (i, C_MG // MN + br * (D // MN) + j))
    gb = lambda br: _bs((1, MN), lambda i, j: (0, br * (D // MN) + j))
    return pl.pallas_call(
        body, name="merge_fwd", grid=(s // MT, D // MN),
        in_specs=[_bs((MT, RW), lambda i, j: (i, 0)), _bs((MT, RW), lambda i, j: (i, 0)), _bs((MT, XW), lambda i, j: (i, 0)),
                  _bs((RW, MN), lambda i, j: (0, j)), _bs((RW, MN), lambda i, j: (0, j)), _bs((XW, MN), lambda i, j: (0, j)),
                  mg(0), mg(1), mg(2), gb(0), gb(1), gb(2)],
        out_specs=_bs((MT, MN), lambda i, j: (i, j)),
        out_shape=jax.ShapeDtypeStruct((s, D), BF16),
        compiler_params=_params(("parallel", "arbitrary")),
    )(ya, yr, yx, wa, wr, wx, proj, proj, proj, gate_b, gate_b, gate_b)


def _out_fwd(merged, w_out, x, target):
    s = x.shape[0]
    tm, tn = min(512, s), 512

    def body(m_ref, w_ref, x_ref, t_ref, loss_ref, d_ref, d16_ref):
        @pl.when((pl.program_id(0) == 0) & (pl.program_id(1) == 0))
        def _():
            loss_ref[...] = jnp.zeros_like(loss_ref)

        out = x_ref[...] + jnp.dot(m_ref[...], w_ref[...], preferred_element_type=F32)
        err = out - t_ref[...]
        dout = err * (1.0 / D)
        d_ref[...] = dout
        d16_ref[...] = dout.astype(BF16)
        loss_ref[...] += jnp.sum(err * err)

    tile = _bs((tm, tn), lambda i, j: (i, j))
    return pl.pallas_call(
        body, name="out_fwd", grid=(s // tm, D // tn),
        in_specs=[_bs((tm, D), lambda i, j: (i, 0)), _bs((D, tn), lambda i, j: (0, j)), tile, tile],
        out_specs=[_bs((8, 128), lambda i, j: (0, 0)), tile, tile],
        out_shape=[jax.ShapeDtypeStruct((8, 128), F32), jax.ShapeDtypeStruct((s, D), F32),
                   jax.ShapeDtypeStruct((s, D), BF16)],
        compiler_params=_params(("arbitrary", "arbitrary")),
    )(merged, w_out, x, target)


def _merge_bwd(ya, yr, yx, wa, wr, wx, proj, gate_b, dmerged):
    s = ya.shape[0]

    def body(ya_ref, yr_ref, yx_ref, wa_ref, wr_ref, wx_ref, m0, m1, m2, b0, b1, b2, dm_ref,
             dg0, dg1, dg2, du0, du1, du2, dya_ref, dyr_ref, dyx_ref):
        @pl.when(pl.program_id(1) == 0)
        def _():
            dya_ref[...] = jnp.zeros_like(dya_ref)
            dyr_ref[...] = jnp.zeros_like(dyr_ref)
            dyx_ref[...] = jnp.zeros_like(dyx_ref)

        dm = dm_ref[...]
        for y_ref, w_ref, m_ref, b_ref, dg_ref, du_ref, dy_ref in (
                (ya_ref, wa_ref, m0, b0, dg0, du0, dya_ref), (yr_ref, wr_ref, m1, b1, dg1, du1, dyr_ref),
                (yx_ref, wx_ref, m2, b2, dg2, du2, dyx_ref)):
            w = w_ref[...]
            u = _dot(y_ref[...], w, ((1,), (0,)))
            gt = jax.nn.sigmoid(m_ref[...] + b_ref[...])
            dg_ref[...] = (dm * u * gt * (1.0 - gt)).astype(BF16)
            du = (dm * gt).astype(BF16)
            du_ref[...] = du
            dy_ref[...] += _dot(du, w, ((1,), (1,)))

    mg = lambda br: _bs((MT, MN), lambda i, j: (i, C_MG // MN + br * (D // MN) + j))
    gb = lambda br: _bs((1, MN), lambda i, j: (0, br * (D // MN) + j))
    tile = _bs((MT, MN), lambda i, j: (i, j))
    return pl.pallas_call(
        body, name="merge_bwd", grid=(s // MT, D // MN),
        in_specs=[_bs((MT, RW), lambda i, j: (i, 0)), _bs((MT, RW), lambda i, j: (i, 0)), _bs((MT, XW), lambda i, j: (i, 0)),
                  _bs((RW, MN), lambda i, j: (0, j)), _bs((RW, MN), lambda i, j: (0, j)), _bs((XW, MN), lambda i, j: (0, j)),
                  mg(0), mg(1), mg(2), gb(0), gb(1), gb(2), tile],
        out_specs=[tile] * 6 + [_bs((MT, RW), lambda i, j: (i, 0)), _bs((MT, RW), lambda i, j: (i, 0)),
                                _bs((MT, XW), lambda i, j: (i, 0))],
        out_shape=[jax.ShapeDtypeStruct((s, D), BF16)] * 6 + [jax.ShapeDtypeStruct((s, RW), F32),
                                                               jax.ShapeDtypeStruct((s, RW), F32),
                                                               jax.ShapeDtypeStruct((s, XW), F32)],
        compiler_params=_params(("parallel", "arbitrary")),
    )(ya, yr, yx, wa, wr, wx, proj, proj, proj, gate_b, gate_b, gate_b, dmerged)


def _colsum(a, name):
    m, n = a.shape
    tm, tn = min(512, m), 512

    def body(a_ref, o_ref):
        @pl.when(pl.program_id(1) == 0)
        def _():
            o_ref[...] = jnp.zeros_like(o_ref)

        o_ref[...] += jnp.sum(a_ref[...].astype(F32), axis=0, keepdims=True)

    return pl.pallas_call(
        body, name=name, grid=(n // tn, m // tm),
        in_specs=[_bs((tm, tn), lambda j, i: (i, j))], out_specs=_bs((1, tn), lambda j, i: (0, j)),
        out_shape=jax.ShapeDtypeStruct((1, n), F32),
        compiler_params=_params(("parallel", "arbitrary")),
    )(a)


def _in_bwd(dproj, w_in, x, g, dout, stacks=()):
    s = x.shape[0]
    tm, tk = min(512, s), 896
    nk = NIN // tk
    ni = s // tm
    n = len(stacks)

    def body(dp_ref, w_ref, x_ref, g_ref, do_ref, *rest):
        ins, (gx_ref, gg_ref), outs = rest[:n], rest[n:n + 2], rest[n + 2:2 * n + 2]
        acc = rest[2 * n + 2]
        i, kk = pl.program_id(0), pl.program_id(1)

        if n:
            start, finish = _scatter_phases(ins, outs, rest[2 * n + 3:])

        @pl.when((i == 0) & (kk == 0))
        def _():
            gg_ref[...] = jnp.zeros_like(gg_ref)
            if n:
                start()

        @pl.when(kk == 0)
        def _():
            acc[...] = jnp.zeros_like(acc)

        acc[...] += _dot(dp_ref[...], w_ref[...], ((1,), (1,)))

        @pl.when(kk == nk - 1)
        def _():
            xv, dh, gv = x_ref[...], acc[...], g_ref[...]
            r = lax.rsqrt(jnp.mean(xv * xv, axis=-1, keepdims=True) + NORM_EPS)
            xn = xv * r
            gg_ref[...] += jnp.sum(dh * xn, axis=0, keepdims=True)
            dxn = dh * gv
            dx = r * (dxn - xn * jnp.mean(dxn * xn, axis=-1, keepdims=True))
            gx_ref[...] = do_ref[...] + dx

        if n:
            @pl.when((i == ni - 1) & (kk == nk - 1))
            def _():
                finish()

    any_spec = pl.BlockSpec(memory_space=pl.ANY)
    res = pl.pallas_call(
        body, name="in_bwd", grid=(ni, nk),
        in_specs=[_bs((tm, tk), lambda i, kk: (i, kk)), _bs((D, tk), lambda i, kk: (0, kk)),
                  _bs((tm, D), lambda i, kk: (i, 0)), _bs((1, D), lambda i, kk: (0, 0)),
                  _bs((tm, D), lambda i, kk: (i, 0))] + [any_spec] * n,
        out_specs=[_bs((tm, D), lambda i, kk: (i, 0)), _bs((1, D), lambda i, kk: (0, 0))] + [any_spec] * n,
        out_shape=[jax.ShapeDtypeStruct((s, D), F32), jax.ShapeDtypeStruct((1, D), F32)] + _scatter_out_shapes(stacks),
        scratch_shapes=[pltpu.VMEM((tm, D), F32)] + _scatter_sems(n),
        compiler_params=pltpu.CompilerParams(dimension_semantics=("arbitrary", "arbitrary"),
                                             vmem_limit_bytes=VMEM_LIMIT, has_side_effects=bool(n)),
    )(dproj, w_in, x, g, dout, *stacks)
    return res[0], res[1], list(res[2:])


def _adamw_math(w, g, m, v):
    m = ADAM_B1 * m + (1.0 - ADAM_B1) * g
    v = ADAM_B2 * v + (1.0 - ADAM_B2) * jnp.square(g)
    m_hat = m / (1.0 - ADAM_B1 ** ADAM_STEP)
    v_hat = v / (1.0 - ADAM_B2 ** ADAM_STEP)
    delta = -ADAM_LR * (m_hat / (jnp.sqrt(v_hat) + ADAM_EPS) + ADAM_WD * w)
    return delta, m, v


def _adamw(parts, w, m, v, name):
    rows, cols = w.shape
    tr = rows
    for cand in (256, 128, 64, 32, 16, 8):
        if rows % cand == 0 and cand * cols * 4 <= (1 << 20):
            tr = cand
            break
    n = len(parts)

    def body(*refs):
        g = refs[0][...].astype(F32)
        for r in refs[1:n]:
            g = g + r[...].astype(F32)
        w_ref, m_ref, v_ref, g_out, d_out, m_out, v_out = refs[n:]
        delta, m_new, v_new = _adamw_math(w_ref[...], g, m_ref[...], v_ref[...])
        g_out[...] = g
        d_out[...] = delta
        m_out[...] = m_new
        v_out[...] = v_new

    spec = _bs((tr, cols), lambda i: (i, 0))
    return pl.pallas_call(
        body, name=name, grid=(rows // tr,),
        in_specs=[spec] * (n + 3), out_specs=[spec] * 4,
        out_shape=[jax.ShapeDtypeStruct((rows, cols), F32)] * 4,
        compiler_params=_params(("parallel",)),
    )(*parts, w, m, v)


def _adamw_halves(mine, theirs, core, w, m, v, name):
    rows, cols = w.shape
    h = rows // 2
    tr = next(t for t in (256, 128, 64, 32, 16, 8) if h % t == 0 and t * cols * 4 <= (1 << 20))
    nt = h // tr

    def body(core_ref, mine_ref, theirs_ref, w_ref, m_ref, v_ref, g_out, d_out, m_out, v_out):
        is_mine = pl.program_id(0) // nt == core_ref[0]
        g = jnp.where(is_mine, mine_ref[...], theirs_ref[...])
        delta, m_new, v_new = _adamw_math(w_ref[...], g, m_ref[...], v_ref[...])
        g_out[...] = g
        d_out[...] = delta
        m_out[...] = m_new
        v_out[...] = v_new

    spec = _bs((tr, cols), lambda i, core_ref: (i, 0))
    return pl.pallas_call(
        body, name=name,
        grid_spec=pltpu.PrefetchScalarGridSpec(
            num_scalar_prefetch=1, grid=(2 * nt,),
            in_specs=[_bs((tr, cols), lambda i, core_ref: (jnp.clip(i - core_ref[0] * nt, 0, nt - 1), 0)),
                      _bs((tr, cols), lambda i, core_ref: (jnp.clip(i - (1 - core_ref[0]) * nt, 0, nt - 1), 0)),
                      spec, spec, spec],
            out_specs=[spec] * 4),
        out_shape=[jax.ShapeDtypeStruct((rows, cols), F32)] * 4,
        compiler_params=_params(("parallel",)),
    )(core, mine, theirs, w, m, v)


def _sum_parts(parts, name):
    rows, cols = parts[0].shape
    tr = rows
    for cand in (256, 128, 64, 32, 16, 8):
        if rows % cand == 0 and cand * cols * 4 <= (1 << 20):
            tr = cand
            break

    def body(*refs):
        acc = refs[0][...].astype(F32)
        for r in refs[1:-1]:
            acc = acc + r[...].astype(F32)
        refs[-1][...] = acc

    spec = _bs((tr, cols), lambda i: (i, 0))
    return pl.pallas_call(
        body, name=name, grid=(rows // tr,), in_specs=[spec] * len(parts), out_specs=spec,
        out_shape=jax.ShapeDtypeStruct((rows, cols), F32), compiler_params=_params(("parallel",)),
    )(*parts)


ANY = pl.BlockSpec(memory_space=pl.ANY)


def _other_chips(x, y):
    return [(1 - x, y), (x, 1 - y), (1 - x, 1 - y)]


def _gather_shards(arrays, name):
    n = len(arrays)

    def body(*refs):
        start, forward, finish = _gather_phases(arrays, refs[:n], refs[n:2 * n], refs[2 * n:])
        start()
        forward()
        finish()

    return pl.pallas_call(
        body, name=name, in_specs=[ANY] * n, out_specs=[ANY] * n,
        out_shape=_gather_out_shapes(arrays), scratch_shapes=_gather_sems(n),
        compiler_params=pltpu.CompilerParams(has_side_effects=True),
    )(*arrays)


def _gather_out_shapes(arrays):
    return [jax.ShapeDtypeStruct((4,) + a.shape, a.dtype) for a in arrays]


def _gather_sems(n):
    dma = lambda k: pltpu.SemaphoreType.DMA((k,))
    return [dma(3 * n), dma(3 * n), dma(3 * n), dma(3 * n), dma(n), dma(n)]


def _gather_phases(arrays, ins, outs, sems):
    n = len(arrays)
    ici_send, ici_recv, d2d_send, d2d_recv, own_send, own_recv = sems

    def place():
        x, y, c = lax.axis_index("x"), lax.axis_index("y"), lax.axis_index("c")
        return x, y, c, 2 * x + y, _other_chips(x, y)

    def half(i, who):
        h = arrays[i].shape[0] // 2
        return pl.ds(who * h, h)

    def ici(i, j, src_chip, to, c):
        return pltpu.make_async_remote_copy(
            src_ref=ins[i].at[half(i, c)], dst_ref=outs[i].at[src_chip, half(i, c)], send_sem=ici_send.at[3 * i + j],
            recv_sem=ici_recv.at[3 * i + j], device_id=to, device_id_type=MESH)

    def d2d(i, j, src_chip, who, sib):
        piece = outs[i].at[src_chip, half(i, who)]
        return pltpu.make_async_remote_copy(
            src_ref=piece, dst_ref=piece, send_sem=d2d_send.at[3 * i + j], recv_sem=d2d_recv.at[3 * i + j],
            device_id=sib, device_id_type=MESH)

    def own(i, me, sib):
        return pltpu.make_async_remote_copy(
            src_ref=ins[i], dst_ref=outs[i].at[me], send_sem=own_send.at[i], recv_sem=own_recv.at[i],
            device_id=sib, device_id_type=MESH)

    def start():
        x, y, c, me, chips = place()
        for i in range(n):
            own(i, me, (x, y, 1 - c)).start()
            for j, (px, py) in enumerate(chips):
                ici(i, j, me, (px, py, c), c).start()

    def forward():
        x, y, c, me, chips = place()
        for i in range(n):
            for j, (px, py) in enumerate(chips):
                ici(i, j, 2 * px + py, (px, py, c), c).wait_recv()
                d2d(i, j, 2 * px + py, c, (x, y, 1 - c)).start()

    def finish():
        x, y, c, me, chips = place()
        sib = (x, y, 1 - c)
        for i in range(n):
            for j, (px, py) in enumerate(chips):
                d2d(i, j, 2 * px + py, 1 - c, sib).wait_recv()
            own(i, me, sib).wait_recv()
        for i in range(n):
            own(i, me, sib).wait_send()
            for j, (px, py) in enumerate(chips):
                ici(i, j, me, (px, py, c), c).wait_send()
                d2d(i, j, 2 * px + py, c, sib).wait_send()

    return start, forward, finish


def _scatter_phases(ins, outs, sems):
    send_sems, recv_sems = sems

    def copies():
        x, y, c = lax.axis_index("x"), lax.axis_index("y"), lax.axis_index("c")
        return [pltpu.make_async_remote_copy(
            src_ref=ins[a].at[2 * qx + qy], dst_ref=outs[a].at[j], send_sem=send_sems.at[3 * a + j],
            recv_sem=recv_sems.at[3 * a + j], device_id=(qx, qy, c), device_id_type=MESH)
            for a in range(len(ins)) for j, (qx, qy) in enumerate(_other_chips(x, y))]

    def start():
        for rc in copies():
            rc.start()

    def finish():
        for rc in copies():
            rc.wait_recv()
        for rc in copies():
            rc.wait_send()

    return start, finish


def _scatter_out_shapes(stacks):
    return [jax.ShapeDtypeStruct((3,) + a.shape[1:], a.dtype) for a in stacks]


def _scatter_sems(n):
    return [pltpu.SemaphoreType.DMA((3 * n,)), pltpu.SemaphoreType.DMA((3 * n,))] if n else []


def _scatter_shards(stacks, name):
    n = len(stacks)

    def body(*refs):
        ins, outs = refs[:n], refs[n:2 * n]
        send_sems, recv_sems = refs[2 * n:]
        x, y, c = lax.axis_index("x"), lax.axis_index("y"), lax.axis_index("c")
        chips = _other_chips(x, y)
        sends = []
        for i in range(n):
            for j, (px, py) in enumerate(chips):
                rc = pltpu.make_async_remote_copy(
                    src_ref=ins[i].at[2 * px + py], dst_ref=outs[i].at[j], send_sem=send_sems.at[3 * i + j],
                    recv_sem=recv_sems.at[3 * i + j], device_id=(px, py, c), device_id_type=MESH)
                rc.start()
                sends.append(rc)
        for rc in sends:
            rc.wait_recv()
        for rc in sends:
            rc.wait_send()

    return pl.pallas_call(
        body, name=name, in_specs=[ANY] * n, out_specs=[ANY] * n,
        out_shape=[jax.ShapeDtypeStruct((3,) + a.shape[1:], a.dtype) for a in stacks],
        scratch_shapes=[pltpu.SemaphoreType.DMA((3 * n,)), pltpu.SemaphoreType.DMA((3 * n,))],
        compiler_params=pltpu.CompilerParams(has_side_effects=True),
    )(*stacks)


def _pair_exchange(stacks, name):
    n = len(stacks)

    def body(*refs):
        ins, outs = refs[:n], refs[n:2 * n]
        send_sems, recv_sems = refs[2 * n:]
        x, y, c = lax.axis_index("x"), lax.axis_index("y"), lax.axis_index("c")
        cps = []
        for i in range(n):
            h = stacks[i].shape[1] // 2
            rc = pltpu.make_async_remote_copy(
                src_ref=ins[i].at[:, pl.ds((1 - c) * h, h)], dst_ref=outs[i], send_sem=send_sems.at[i],
                recv_sem=recv_sems.at[i], device_id=(x, y, 1 - c), device_id_type=MESH)
            rc.start()
            cps.append(rc)
        for rc in cps:
            rc.wait_recv()
        for rc in cps:
            rc.wait_send()

    return pl.pallas_call(
        body, name=name, in_specs=[ANY] * n, out_specs=[ANY] * n,
        out_shape=[jax.ShapeDtypeStruct((4, a.shape[1] // 2) + a.shape[2:], a.dtype) for a in stacks],
        scratch_shapes=[pltpu.SemaphoreType.DMA((n,)), pltpu.SemaphoreType.DMA((n,))],
        compiler_params=pltpu.CompilerParams(has_side_effects=True),
    )(*stacks)


def _pair_sum(own, theirs, core, name):
    _, r, cols = own.shape
    h = r // 2
    tr = next(t for t in (256, 128, 64, 32, 16) if h % t == 0 and t * cols * 4 <= (1 << 20))
    nt = h // tr

    def body(core_ref, own_ref, th_ref, o32_ref, o16_ref):
        del core_ref
        acc = own_ref[...] + th_ref[...].astype(F32)
        o32_ref[...] = acc
        o16_ref[...] = acc.astype(BF16)

    out = _bs((1, tr, cols), lambda j, t, core_ref: (j, t, 0))
    return pl.pallas_call(
        body, name=name,
        grid_spec=pltpu.PrefetchScalarGridSpec(
            num_scalar_prefetch=1, grid=(4, nt),
            in_specs=[_bs((1, tr, cols), lambda j, t, core_ref: (j, core_ref[0] * nt + t, 0)), out],
            out_specs=[out, out]),
        out_shape=[jax.ShapeDtypeStruct((4, h, cols), F32), jax.ShapeDtypeStruct((4, h, cols), BF16)],
        compiler_params=_params(("parallel", "parallel")),
    )(core, own, theirs)


def _swap_sibling(arrays, name):
    n = len(arrays)

    def body(*refs):
        ins, outs = refs[:n], refs[n:2 * n]
        send_sems, recv_sems = refs[2 * n:]
        sib = (lax.axis_index("x"), lax.axis_index("y"), 1 - lax.axis_index("c"))
        cps = []
        for i in range(n):
            rc = pltpu.make_async_remote_copy(src_ref=ins[i], dst_ref=outs[i], send_sem=send_sems.at[i],
                                              recv_sem=recv_sems.at[i], device_id=sib, device_id_type=MESH)
            rc.start()
            cps.append(rc)
        for rc in cps:
            rc.wait_recv()
        for rc in cps:
            rc.wait_send()

    return pl.pallas_call(
        body, name=name, in_specs=[ANY] * n, out_specs=[ANY] * n,
        out_shape=[jax.ShapeDtypeStruct(a.shape, a.dtype) for a in arrays],
        scratch_shapes=[pltpu.SemaphoreType.DMA((n,)), pltpu.SemaphoreType.DMA((n,))],
        compiler_params=pltpu.CompilerParams(has_side_effects=True),
    )(*arrays)


def _all_reduce_small(v):
    rows = v.shape[0]

    def body(v_ref, o_ref, buf, send_sems, recv_sems):
        x, y, c = lax.axis_index("x"), lax.axis_index("y"), lax.axis_index("c")
        me = 4 * x + 2 * y + c
        buf[me] = v_ref[...]
        cps = []
        for kbits in range(1, 8):
            bx, by, bc = (kbits >> 2) & 1, (kbits >> 1) & 1, kbits & 1
            px = jnp.where(bx == 1, 1 - x, x)
            py = jnp.where(by == 1, 1 - y, y)
            pc = jnp.where(bc == 1, 1 - c, c)
            rc = pltpu.make_async_remote_copy(src_ref=v_ref, dst_ref=buf.at[me], send_sem=send_sems.at[kbits - 1],
                                              recv_sem=recv_sems.at[kbits - 1], device_id=(px, py, pc),
                                              device_id_type=MESH)
            rc.start()
            cps.append((rc, 4 * px + 2 * py + pc))
        for kbits, (rc, src) in enumerate(cps):
            pltpu.make_async_remote_copy(src_ref=v_ref, dst_ref=buf.at[src], send_sem=send_sems.at[kbits],
                                         recv_sem=recv_sems.at[kbits], device_id=(x, y, c),
                                         device_id_type=MESH).wait_recv()
        for rc, _ in cps:
            rc.wait_send()
        acc = buf[0]
        for d in range(1, 8):
            acc = acc + buf[d]
        o_ref[...] = acc

    return pl.pallas_call(
        body, name="all_reduce_small",
        in_specs=[pl.BlockSpec(memory_space=pltpu.VMEM)], out_specs=pl.BlockSpec(memory_space=pltpu.VMEM),
        out_shape=jax.ShapeDtypeStruct((rows, 128), F32),
        scratch_shapes=[pltpu.VMEM((8, rows, 128), F32), pltpu.SemaphoreType.DMA((7,)), pltpu.SemaphoreType.DMA((7,))],
        compiler_params=pltpu.CompilerParams(has_side_effects=True, vmem_limit_bytes=VMEM_LIMIT),
    )(v)


def _rope_tables(s):
    half = HD // 2
    inv = 10000.0 ** (-jnp.arange(half, dtype=F32) / half)
    ang = jnp.arange(s, dtype=F32)[:, None] * inv[None, :]
    cos, sin = jnp.cos(ang), jnp.sin(ang)
    return jnp.concatenate([cos, cos], axis=1), jnp.concatenate([sin, sin], axis=1)


LATE = ['attn_w_o', 'rwkv_w_o', 'x_w_kv', 'x_w_o', 'w_out']


def _local_step(x, mem, target, norm_g, mem_norm_g, w_in, gate_b, gq, gk, sink, wa, mu, k_k, k_a, r_k, w0, w2, a0, a2,
                ln_w, ln_b, wr, w_kv, gxq, gxk, wx, w_out, late_shards=None, early_exchange=None):
    s = x.shape[0]
    cos, sin = _rope_tables(s)
    r_k = r_k.reshape(1, RW)

    proj, h = _proj_fwd(x, norm_g, w_in)
    ya = _attn_fwd(proj, cos, sin, gq, gk, sink)
    ps = _shift_fwd(proj, mu)
    kk, dec0, kd0, b0, dec1, kd1, b1 = _pre_fwd(ps, k_k, k_a, w0, w2, a0, a2)
    ((y0, ck0), (y1, ck1)), stacks = _scan2_fwd([(dec0, kd0, b0), (dec1, kd1, b1)], ps, kk, gather=late_shards or ())
    if late_shards:
        st = dict(zip(LATE, stacks))
        wa, wr, wx = (_unshard_cols(st[n]) for n in ('attn_w_o', 'rwkv_w_o', 'x_w_o'))
        w_kv, w_out = st['x_w_kv'].reshape(D, 2 * XW), st['w_out'].reshape(D, D)
    mkv, mn = _mem_kv(mem, mem_norm_g, w_kv)
    yx = _xattn_fwd(proj, mkv, gxq, gxk)
    yr = _post_fwd(y0, y1, ps, kd0, kd1, proj, r_k, ln_w, ln_b)
    merged = _merge_fwd(ya, yr, yx, wa, wr, wx, proj, gate_b)
    loss_tile, dout, dout16 = _out_fwd(merged, w_out, x, target)
    loss_sum = loss_tile[0, 0]

    g = {}
    t16 = lambda a: a.astype(BF16).T
    sk = min(1024, s)
    dmerged = _matmul(dout16, w_out, mode="nt", m=s, n=D, k=D, tm=sk, tn=1024, tk=1024, name="dmerged")
    g["w_out"] = _matmul(merged.T, dout16, mode="nn", m=D, n=D, k=s, tm=1024, tn=1024, tk=sk, name="grad_w_out")
    dg0, dg1, dg2, du0, du1, du2, dya, dyr, dyx = _merge_bwd(ya, yr, yx, wa, wr, wx, proj, gate_b, dmerged)
    g["attn_w_o"] = _matmul(t16(ya), du0, mode="nn", m=RW, n=D, k=s, tm=RW, tn=1024, tk=s, name="grad_attn_w_o")
    g["rwkv_w_o"] = _matmul(t16(yr), du1, mode="nn", m=RW, n=D, k=s, tm=RW, tn=1024, tk=s, name="grad_rwkv_w_o")
    g["x_w_o"] = _matmul(t16(yx), du2, mode="nn", m=XW, n=D, k=s, tm=XW, tn=1024, tk=s, name="grad_x_w_o")
    dmg = jnp.concatenate([dg0, dg1, dg2], axis=1)
    g["gate_b"] = _colsum(dmg, "grad_gate_b")

    daq, dak, dav, dag, g["attn_q_norm_g"], g["attn_k_norm_g"], g["attn_sink"] = _attn_bwd(proj, cos, sin, gq, gk, sink, dya)

    dxq, dxg, dmkv, g["x_q_norm_g"], g["x_k_norm_g"] = _xattn_bwd(proj, mkv, gxq, gxk, dyx)
    g["x_w_kv"] = _matmul(mn, dmkv, mode="tn", m=D, n=2 * XW, k=NMEM, tm=512, tn=512, tk=NMEM, name="grad_x_w_kv")
    dmn = _matmul(dmkv, w_kv, mode="nt", m=NMEM, n=D, k=2 * XW, tm=NMEM, tn=512, tk=2 * XW, name="dmn")
    g["mem_norm_g"] = _mem_bwd(mem, mem_norm_g, dmn)

    dys, dr_p, dv_p, dkd0_p, dkd1_p, drg, g["rwkv_r_k"], g["rwkv_ln_w"], g["rwkv_ln_b"] = _post_bwd(
        y0, y1, ps, kd0, kd1, proj, r_k, ln_w, ln_b, dyr)
    sent = early_exchange(g) if early_exchange else ()
    ((dr0, dd0, db0, dk0, dkk0, dv0), (dr1, dd1, db1, dk1, dkk1, dv1)), received = _scan2_bwd(
        [(dec0, kd0, b0, ck0), (dec1, kd1, b1, ck1)], ps, kk, dys, scatter=sent)
    dr = dr_p + dr0 + dr1
    dv = dv_p + dv0 + dv1
    cts = (dkk0 + dkk1, dd0, dk0 + dkd0_p, db0, dd1, dk1 + dkd1_p, db1)
    dps, g["rwkv_k_k"], g["rwkv_k_a"], g["rwkv_w0"], g["rwkv_w2"], g["rwkv_a0"], g["rwkv_a2"] = _pre_bwd(
        ps, k_k, k_a, w0, w2, a0, a2, dr, dv, cts)
    drs, g["rwkv_mu"] = _shift_bwd(proj, mu, dps)

    dproj = jnp.concatenate([daq.astype(BF16), dak.astype(BF16), dav.astype(BF16), dag.astype(BF16), drs.astype(BF16),
                             drg.astype(BF16), dxq.astype(BF16), dxg.astype(BF16), dmg], axis=1)
    dproj4 = jnp.stack([dproj[:, j * (NIN // 4):(j + 1) * (NIN // 4)] for j in range(4)])
    g["w_in"], g["w_in_bf16"] = _grad_w_in(h.T, dproj4)
    g["rwkv_r_k"] = g["rwkv_r_k"].reshape(AH, HD)
    return loss_sum, g, (dproj, w_in, x, norm_g, dout), received


WEIGHTS = ['norm_g', 'mem_norm_g', 'w_in', 'gate_b', 'attn_q_norm_g', 'attn_k_norm_g', 'attn_sink', 'attn_w_o',
           'rwkv_mu', 'rwkv_k_k', 'rwkv_k_a', 'rwkv_r_k', 'rwkv_w0', 'rwkv_w2', 'rwkv_a0', 'rwkv_a2', 'rwkv_ln_w',
           'rwkv_ln_b', 'rwkv_w_o', 'x_w_kv', 'x_q_norm_g', 'x_k_norm_g', 'x_w_o', 'w_out']
BIG = ['w_in', 'attn_w_o', 'rwkv_w_o', 'x_w_kv', 'x_w_o', 'w_out']
COL_SHARDED = ['w_in', 'attn_w_o', 'rwkv_w_o', 'x_w_o']
LORA = ['rwkv_w0', 'rwkv_w2', 'rwkv_a0', 'rwkv_a2']
SMALL = [n for n in WEIGHTS if n not in BIG]


def _unshard_cols(stack):
    return jnp.concatenate([stack[i] for i in range(4)], axis=-1)


def _shard_cols(full):
    w = full.shape[-1] // 4
    return [full[..., i * w:(i + 1) * w] for i in range(4)]


def kernel(x, mem, norm_g, mem_norm_g, w_in, gate_b, attn_q_norm_g, attn_k_norm_g, attn_sink, attn_w_o, rwkv_mu, rwkv_k_k, rwkv_k_a, rwkv_r_k, rwkv_w0, rwkv_w2, rwkv_a0, rwkv_a2, rwkv_ln_w, rwkv_ln_b, rwkv_w_o, x_w_kv, x_q_norm_g, x_k_norm_g, x_w_o, w_out, loss_target, m_norm_g, m_mem_norm_g, m_w_in, m_gate_b, m_attn_q_norm_g, m_attn_k_norm_g, m_attn_sink, m_attn_w_o, m_rwkv_mu, m_rwkv_k_k, m_rwkv_k_a, m_rwkv_r_k, m_rwkv_w0, m_rwkv_w2, m_rwkv_a0, m_rwkv_a2, m_rwkv_ln_w, m_rwkv_ln_b, m_rwkv_w_o, m_x_w_kv, m_x_q_norm_g, m_x_k_norm_g, m_x_w_o, m_w_out, v_norm_g, v_mem_norm_g, v_w_in, v_gate_b, v_attn_q_norm_g, v_attn_k_norm_g, v_attn_sink, v_attn_w_o, v_rwkv_mu, v_rwkv_k_k, v_rwkv_k_a, v_rwkv_r_k, v_rwkv_w0, v_rwkv_w2, v_rwkv_a0, v_rwkv_a2, v_rwkv_ln_w, v_rwkv_ln_b, v_rwkv_w_o, v_x_w_kv, v_x_q_norm_g, v_x_k_norm_g, v_x_w_o, v_w_out):
    args = dict(locals())
    canon = lambda a: a[0] if a.ndim > 2 else a
    w = {n: canon(args[n]) for n in WEIGHTS}
    m = {n: canon(args["m_" + n]) for n in WEIGHTS}
    v = {n: canon(args["v_" + n]) for n in WEIGHTS}
    shard = 2 * lax.axis_index("x") + lax.axis_index("y")

    now = ["w_in"] + LORA
    local = [w["w_in"].astype(BF16)] + [w[n].reshape(2, -1, w[n].shape[-1]) for n in LORA]
    stacks = dict(zip(now, _gather_shards(local, "gather_weights")))
    full = {"w_in": _unshard_cols(stacks["w_in"])}
    for n in LORA:
        full[n] = _unshard_cols(stacks[n]).reshape(w[n].shape[:-1] + (RW,))

    core = lax.axis_index("c").astype(jnp.int32).reshape(1)
    pair32 = {}

    def as_stack(g, n, dtype):
        if n == "w_in":
            return g["w_in"] if dtype == F32 else g["w_in_bf16"]
        if n in COL_SHARDED:
            return jnp.stack([p.astype(dtype) for p in _shard_cols(g[n])])
        return g[n].reshape((4, g[n].shape[0] // 4) + g[n].shape[1:]).astype(dtype)

    def pair_sums(g, names, tag):
        sibling = _pair_exchange([as_stack(g, n, BF16) for n in names], "pair_exchange_" + tag)
        sent = []
        for n, th in zip(names, sibling):
            pair32[n], a16 = _pair_sum(as_stack(g, n, F32), th, core, "pair_sum_" + n)
            sent.append(a16)
        return sent

    loss_sum, g, deferred, recv_late = _local_step(
        x[0], mem[0], loss_target[0], w["norm_g"], w["mem_norm_g"], full["w_in"], w["gate_b"], w["attn_q_norm_g"],
        w["attn_k_norm_g"], w["attn_sink"], None, w["rwkv_mu"], w["rwkv_k_k"], w["rwkv_k_a"], w["rwkv_r_k"],
        full["rwkv_w0"], full["rwkv_w2"], full["rwkv_a0"], full["rwkv_a2"], w["rwkv_ln_w"], w["rwkv_ln_b"],
        None, None, w["x_q_norm_g"], w["x_k_norm_g"], None, None,
        late_shards=[w[n].astype(BF16) for n in LATE], early_exchange=lambda g: pair_sums(g, LATE, "late"))

    loss = lax.psum(0.5 * loss_sum / D, ("x", "y", "c"))

    grad_x, g["norm_g"], recv_w_in = _in_bwd(*deferred, stacks=pair_sums(g, ["w_in"], "w_in"))
    halves = []
    for n, r in zip(BIG, recv_w_in + recv_late):
        own = lax.dynamic_index_in_dim(pair32[n], shard, 0, keepdims=False)
        halves.append(_sum_parts([own, r[0], r[1], r[2]], "sum_" + n))
    other_halves = _swap_sibling(halves, "swap_halves")

    out_g, out_d, out_m, out_v = {}, {}, {}, {}
    for n, mine, theirs in zip(BIG, halves, other_halves):
        out_g[n], out_d[n], out_m[n], out_v[n] = _adamw_halves(mine, theirs, core, w[n], m[n], v[n], "adamw_" + n)

    flat = jnp.concatenate([g[n].reshape(-1) for n in SMALL])
    total = flat.shape[0]
    padded = -(-total // 1024) * 1024
    flat = jnp.pad(flat, (0, padded - total)).reshape(padded // 128, 128)
    red = _all_reduce_small(flat).reshape(-1)
    off = 0
    gs = {}
    for n in SMALL:
        size = g[n].size
        t = red[off:off + size].reshape(g[n].shape)
        off += size
        if n in LORA:
            wd = t.shape[-1] // 4
            t = lax.dynamic_slice_in_dim(t, shard * wd, wd, axis=t.ndim - 1)
        gs[n] = t

    def pack(d):
        f = jnp.concatenate([d[n].reshape(-1) for n in SMALL])
        return jnp.pad(f, (0, -(-f.shape[0] // 1024) * 1024 - f.shape[0])).reshape(-1, 128)

    pg, pd, pm, pv = _adamw([pack(gs)], pack(w), pack(m), pack(v), "adamw_small")
    off = 0
    for n in SMALL:
        size = w[n].size
        for dst, src in ((out_g, pg), (out_d, pd), (out_m, pm), (out_v, pv)):
            dst[n] = src.reshape(-1)[off:off + size].reshape(w[n].shape)
        off += size

    lead = lambda d: [d[n][None] if args[n].ndim > 2 else d[n] for n in WEIGHTS]
    return (loss, grad_x[None], *lead(out_g), *lead(out_d), *lead(out_m), *lead(out_v))
```

```python
import functools

import jax
import jax.numpy as jnp
from jax import lax
from jax.experimental import pallas as pl
from jax.experimental.pallas import tpu as pltpu

F32 = jnp.float32
BF16 = jnp.bfloat16
HI = lax.Precision.HIGH
MESH = pl.DeviceIdType.MESH

D = 2048
NMEM = 256
NORM_EPS = 1e-6
NEG_INF = -1e30
GN_EPS = 64e-5
HD = 64
AH = 12
AKV = 4
RW = 768
XH = 4
XD = 128
XW = 512
NIN = 12544
RSW = 2560
C_AQ, C_AK, C_AV, C_AG, C_RS, C_RG, C_XQ, C_XG, C_MG = 0, 768, 1024, 1280, 2048, 4608, 5376, 5888, 6400
WIN = 384
QB = 128
TC = 16
NPAIR = 6

ADAM_LR, ADAM_B1, ADAM_B2, ADAM_EPS, ADAM_WD, ADAM_STEP = 0.001, 0.9, 0.999, 1e-08, 0.01, 10

VMEM_LIMIT = 56 * 1024 * 1024


def _bs(shape, imap):
    return pl.BlockSpec(shape, imap)


def _params(sem=None, vmem=VMEM_LIMIT):
    return pltpu.CompilerParams(dimension_semantics=sem, vmem_limit_bytes=vmem)


def _dot(a, b, dims):
    return lax.dot_general(a.astype(BF16), b.astype(BF16), (dims, ((), ())), preferred_element_type=F32)


@jax.custom_vjp
def _mm_nn(a, b):
    return _dot(a, b, ((1,), (0,)))


def _mm_nn_fwd(a, b):
    return _mm_nn(a, b), (a, b)


def _mm_nn_bwd(res, ct):
    a, b = res
    return _dot(ct, b, ((1,), (1,))), _dot(a, ct, ((0,), (0,)))


_mm_nn.defvjp(_mm_nn_fwd, _mm_nn_bwd)


@jax.custom_vjp
def _mm_nt(a, b):
    return _dot(a, b, ((1,), (1,)))


def _mm_nt_fwd(a, b):
    return _mm_nt(a, b), (a, b)


def _mm_nt_bwd(res, ct):
    a, b = res
    return _dot(ct, b, ((1,), (0,))), _dot(ct, a, ((0,), (0,)))


_mm_nt.defvjp(_mm_nt_fwd, _mm_nt_bwd)


def _seg_matrix(n, seg):
    r = lax.broadcasted_iota(jnp.int32, (n, n), 0) // seg
    c = lax.broadcasted_iota(jnp.int32, (n, n), 1) // seg
    return (r == c).astype(F32)


def _rot_matrix():
    r = lax.broadcasted_iota(jnp.int32, (HD, HD), 0)
    c = lax.broadcasted_iota(jnp.int32, (HD, HD), 1)
    return jnp.where(c == r + HD // 2, 1.0, 0.0).astype(F32) - jnp.where(c == r - HD // 2, 1.0, 0.0).astype(F32)


def _hdot(a, m):
    return jnp.dot(a, m, precision=HI, preferred_element_type=F32)


def _rms(t, g):
    return t * lax.rsqrt(jnp.mean(t * t, axis=-1, keepdims=True) + NORM_EPS) * g


def _silu(t):
    return t * jax.nn.sigmoid(t)


def _softplus(z):
    return jnp.maximum(z, 0.0) + jnp.log(1.0 + jnp.exp(-jnp.abs(z)))


def _matmul(a, b, *, mode, m, n, k, tm, tn, tk, name, a_off=(0, 0), b_off=(0, 0), out_dtype=F32):
    nk = k // tk
    if mode == "tn":
        a_spec = _bs((tk, tm), lambda i, j, kk: (kk + a_off[0], i + a_off[1]))
        dims = ((0,), (0,))
    else:
        a_spec = _bs((tm, tk), lambda i, j, kk: (i + a_off[0], kk + a_off[1]))
        dims = ((1,), (1,)) if mode == "nt" else ((1,), (0,))
    if mode == "nt":
        b_spec = _bs((tn, tk), lambda i, j, kk: (j + b_off[0], kk + b_off[1]))
    else:
        b_spec = _bs((tk, tn), lambda i, j, kk: (kk + b_off[0], j + b_off[1]))

    def body(a_ref, b_ref, o_ref, acc):
        kk = pl.program_id(2)

        @pl.when(kk == 0)
        def _():
            acc[...] = jnp.zeros_like(acc)

        acc[...] += _dot(a_ref[...], b_ref[...], dims)

        @pl.when(kk == nk - 1)
        def _():
            o_ref[...] = acc[...].astype(out_dtype)

    return pl.pallas_call(
        body, name=name, grid=(m // tm, n // tn, nk),
        in_specs=[a_spec, b_spec], out_specs=_bs((tm, tn), lambda i, j, kk: (i, j)),
        out_shape=jax.ShapeDtypeStruct((m, n), out_dtype),
        scratch_shapes=[pltpu.VMEM((tm, tn), F32)],
        compiler_params=_params(("parallel", "parallel", "arbitrary")),
    )(a, b)


def _grad_w_in(ht, dproj4):
    s = ht.shape[1]
    ws = NIN // 4
    tm, tk = 512, min(1024, s)
    nk = s // tk

    def body(a_ref, b_ref, o32_ref, o16_ref, acc):
        kk = pl.program_id(2)

        @pl.when(kk == 0)
        def _():
            acc[...] = jnp.zeros_like(acc)

        acc[...] += jnp.dot(a_ref[...], b_ref[0], preferred_element_type=F32)

        @pl.when(kk == nk - 1)
        def _():
            o32_ref[0] = acc[...]
            o16_ref[0] = acc[...].astype(BF16)

    out = _bs((1, tm, ws), lambda j, i, kk: (j, i, 0))
    return pl.pallas_call(
        body, name="grad_w_in", grid=(4, D // tm, nk),
        in_specs=[_bs((tm, tk), lambda j, i, kk: (i, kk)), _bs((1, tk, ws), lambda j, i, kk: (j, kk, 0))],
        out_specs=[out, out],
        out_shape=[jax.ShapeDtypeStruct((4, D, ws), F32), jax.ShapeDtypeStruct((4, D, ws), BF16)],
        scratch_shapes=[pltpu.VMEM((tm, ws), F32)],
        compiler_params=_params(("parallel", "parallel", "arbitrary")),
    )(ht, dproj4)


def _proj_fwd(x, g, w):
    s = x.shape[0]
    tm, tn = min(512, s), 896

    def body(x_ref, g_ref, w_ref, o_ref, h_ref, hs):
        @pl.when(pl.program_id(1) == 0)
        def _():
            h = _rms(x_ref[...], g_ref[...]).astype(BF16)
            hs[...] = h
            h_ref[...] = h

        o_ref[...] = jnp.dot(hs[...], w_ref[...], preferred_element_type=F32)

    return pl.pallas_call(
        body, name="proj_fwd", grid=(s // tm, NIN // tn),
        in_specs=[_bs((tm, D), lambda i, j: (i, 0)), _bs((1, D), lambda i, j: (0, 0)), _bs((D, tn), lambda i, j: (0, j))],
        out_specs=[_bs((tm, tn), lambda i, j: (i, j)), _bs((tm, D), lambda i, j: (i, 0))],
        out_shape=[jax.ShapeDtypeStruct((s, NIN), F32), jax.ShapeDtypeStruct((s, D), BF16)],
        scratch_shapes=[pltpu.VMEM((tm, D), BF16)],
        compiler_params=_params(("parallel", "arbitrary")),
    )(x, g, w)


def _rope(t, cos, sin, rot):
    return t * cos + _hdot(t, rot) * sin


def _attn_tile(qs, ks, vs, gs, sinks, gq, gk, cq, sq, ck, sk, mask, rot):
    heads = range(AH)
    kv = [h // (AH // AKV) for h in heads]
    kh = [_rope(_rms(ks[j], gk), ck, sk, rot) for j in range(AKV)]
    qh = [_rope(_rms(qs[h], gq), cq, sq, rot) for h in heads]
    sc = [jnp.where(mask, _mm_nt(qh[h], kh[kv[h]]) * (HD ** -0.5), NEG_INF) for h in heads]
    mx = [lax.stop_gradient(jnp.maximum(jnp.max(sc[h], axis=-1, keepdims=True), sinks[h])) for h in heads]
    p = [jnp.exp(sc[h] - mx[h]) for h in heads]
    den = [jnp.sum(p[h], axis=-1, keepdims=True) + jnp.exp(sinks[h] - mx[h]) for h in heads]
    o = [_mm_nn(p[h] / den[h], vs[kv[h]]) for h in heads]
    return [o[h] * _silu(gs[h]) for h in heads]


def _attn_load(n, s, aq_ref, ak_ref, av_ref, ag_refs, cos_ref, sin_ref, sink_ref):
    start = pl.multiple_of(jnp.clip((n - 1) * QB, 0, s - WIN), QB)
    q0 = pl.multiple_of(n * QB, QB)
    qs = [aq_ref[:, h * HD:(h + 1) * HD] for h in range(AH)]
    ks = [ak_ref[pl.ds(start, WIN), h * HD:(h + 1) * HD] for h in range(AKV)]
    vs = [av_ref[pl.ds(start, WIN), h * HD:(h + 1) * HD] for h in range(AKV)]
    gs = [ag_refs[h // 4][:, (h % 4) * HD:(h % 4 + 1) * HD] for h in range(AH)]
    sinks = [sink_ref[0:1, h:h + 1] for h in range(AH)]
    cq, sq = cos_ref[pl.ds(q0, QB), :], sin_ref[pl.ds(q0, QB), :]
    ck, sk = cos_ref[pl.ds(start, WIN), :], sin_ref[pl.ds(start, WIN), :]
    qpos = q0 + lax.broadcasted_iota(jnp.int32, (QB, WIN), 0)
    kpos = start + lax.broadcasted_iota(jnp.int32, (QB, WIN), 1)
    mask = jnp.abs(kpos - qpos) <= QB
    return start, qs, ks, vs, gs, sinks, cq, sq, ck, sk, mask


def _attn_specs(s):
    return [
        _bs((QB, 768), lambda n: (n, 0)),
        _bs((s, 256), lambda n: (0, C_AK // 256)),
        _bs((s, 256), lambda n: (0, C_AV // 256)),
        _bs((QB, 256), lambda n: (n, C_AG // 256)),
        _bs((QB, 256), lambda n: (n, C_AG // 256 + 1)),
        _bs((QB, 256), lambda n: (n, C_AG // 256 + 2)),
        _bs((s, HD), lambda n: (0, 0)),
        _bs((s, HD), lambda n: (0, 0)),
        _bs((1, HD), lambda n: (0, 0)),
        _bs((1, HD), lambda n: (0, 0)),
        _bs((1, AH), lambda n: (0, 0)),
    ]


def _attn_fwd(proj, cos, sin, gq, gk, sink):
    s = proj.shape[0]

    def body(aq_ref, ak_ref, av_ref, ag0, ag1, ag2, cos_ref, sin_ref, gq_ref, gk_ref, sink_ref, o_ref):
        n = pl.program_id(0)
        _, qs, ks, vs, gs, sinks, cq, sq, ck, sk, mask = _attn_load(
            n, s, aq_ref, ak_ref, av_ref, (ag0, ag1, ag2), cos_ref, sin_ref, sink_ref)
        outs = _attn_tile(qs, ks, vs, gs, sinks, gq_ref[...], gk_ref[...], cq, sq, ck, sk, mask, _rot_matrix())
        for h in range(AH):
            o_ref[:, h * HD:(h + 1) * HD] = outs[h]

    return pl.pallas_call(
        body, name="attn_fwd", grid=(s // QB,),
        in_specs=_attn_specs(s), out_specs=_bs((QB, 768), lambda n: (n, 0)),
        out_shape=jax.ShapeDtypeStruct((s, 768), F32),
        compiler_params=_params(("arbitrary",)),
    )(proj, proj, proj, proj, proj, proj, cos, sin, gq, gk, sink)


def _attn_bwd(proj, cos, sin, gq, gk, sink, dy):
    s = proj.shape[0]

    def body(aq_ref, ak_ref, av_ref, ag0, ag1, ag2, cos_ref, sin_ref, gq_ref, gk_ref, sink_ref, dy_ref,
             daq_ref, dak_ref, dav_ref, dag_ref, dgq_ref, dgk_ref, dsink_ref):
        n = pl.program_id(0)

        @pl.when(n == 0)
        def _():
            dak_ref[...] = jnp.zeros_like(dak_ref)
            dav_ref[...] = jnp.zeros_like(dav_ref)
            dgq_ref[...] = jnp.zeros_like(dgq_ref)
            dgk_ref[...] = jnp.zeros_like(dgk_ref)
            dsink_ref[...] = jnp.zeros_like(dsink_ref)

        start, qs, ks, vs, gs, sinks, cq, sq, ck, sk, mask = _attn_load(
            n, s, aq_ref, ak_ref, av_ref, (ag0, ag1, ag2), cos_ref, sin_ref, sink_ref)
        rot = _rot_matrix()

        def f(qs, ks, vs, gs, sinks, gq, gk):
            return _attn_tile(qs, ks, vs, gs, sinks, gq, gk, cq, sq, ck, sk, mask, rot)

        _, vjp = jax.vjp(f, qs, ks, vs, gs, sinks, gq_ref[...], gk_ref[...])
        dys = [dy_ref[:, h * HD:(h + 1) * HD] for h in range(AH)]
        dqs, dks, dvs, dgs, dsinks, dgq, dgk = vjp(dys)
        for h in range(AH):
            daq_ref[:, h * HD:(h + 1) * HD] = dqs[h]
            dag_ref[:, h * HD:(h + 1) * HD] = dgs[h]
            dsink_ref[0:1, h:h + 1] += dsinks[h]
        for h in range(AKV):
            dak_ref[pl.ds(start, WIN), h * HD:(h + 1) * HD] += dks[h]
            dav_ref[pl.ds(start, WIN), h * HD:(h + 1) * HD] += dvs[h]
        dgq_ref[...] += dgq
        dgk_ref[...] += dgk

    whole = lambda shape: _bs(shape, lambda n: (0, 0))
    return pl.pallas_call(
        body, name="attn_bwd", grid=(s // QB,),
        in_specs=_attn_specs(s) + [_bs((QB, 768), lambda n: (n, 0))],
        out_specs=[_bs((QB, 768), lambda n: (n, 0)), whole((s, 256)), whole((s, 256)), _bs((QB, 768), lambda n: (n, 0)),
                   whole((1, HD)), whole((1, HD)), whole((1, AH))],
        out_shape=[jax.ShapeDtypeStruct((s, 768), F32), jax.ShapeDtypeStruct((s, 256), F32),
                   jax.ShapeDtypeStruct((s, 256), F32), jax.ShapeDtypeStruct((s, 768), F32),
                   jax.ShapeDtypeStruct((1, HD), F32), jax.ShapeDtypeStruct((1, HD), F32),
                   jax.ShapeDtypeStruct((1, AH), F32)],
        compiler_params=_params(("arbitrary",)),
    )(proj, proj, proj, proj, proj, proj, cos, sin, gq, gk, sink, dy)


@jax.custom_vjp
def _rot_half(t):
    n = t.shape[1]
    lane = lax.broadcasted_iota(jnp.int32, t.shape, 1) % HD
    return jnp.where(lane < HD // 2, -pltpu.roll(t, n - HD // 2, 1), pltpu.roll(t, HD // 2, 1))


_rot_half.defvjp(lambda t: (_rot_half(t), None), lambda _, ct: (-_rot_half(ct),))


def _head_rms(t, g, seg):
    return t * lax.rsqrt(_hdot(t * t, seg) * (1.0 / HD) + NORM_EPS) * g


def _wattn_tile(aq, ak, av, ag, sinks, gq, gk, cq, sq, ck, sk, mask, segq, segk, spread, head_of_lane):
    q = _head_rms(aq, gq, segq)
    q = q * cq + _rot_half(q) * sq
    k = _head_rms(ak, gk, segk)
    k = k * ck + _rot_half(k) * sk
    kx = _hdot(k, spread)
    vx = _hdot(av, spread)
    o = jnp.zeros_like(aq)
    for h in range(AH):
        mine = head_of_lane == h
        sc = _mm_nt(jnp.where(mine, q, 0.0), kx) * (HD ** -0.5)
        sc = jnp.where(mask, sc, NEG_INF)
        mx = lax.stop_gradient(jnp.maximum(jnp.max(sc, axis=-1, keepdims=True), sinks[h]))
        p = jnp.exp(sc - mx)
        den = jnp.sum(p, axis=-1, keepdims=True) + jnp.exp(sinks[h] - mx)
        o = o + _mm_nn(p / den, jnp.where(mine, vx, 0.0))
    return o * _silu(ag)


def _wattn_consts():
    r = lax.broadcasted_iota(jnp.int32, (256, 768), 0)
    c = lax.broadcasted_iota(jnp.int32, (256, 768), 1)
    spread = ((r // HD == c // HD // (AH // AKV)) & (r % HD == c % HD)).astype(F32)
    head_of_lane = lax.broadcasted_iota(jnp.int32, (1, 768), 1) // HD
    return _seg_matrix(768, HD), _seg_matrix(256, HD), spread, head_of_lane


def _wattn_load(n, s, aq_ref, ak_ref, av_ref, ag_refs, cq_ref, sq_ref, ck_ref, sk_ref, sink_ref):
    start = pl.multiple_of(jnp.clip((n - 1) * QB, 0, s - WIN), QB)
    q0 = pl.multiple_of(n * QB, QB)
    win = pl.ds(start, WIN)
    ag = jnp.concatenate([r[...] for r in ag_refs], axis=1)
    sinks = [sink_ref[0:1, h:h + 1] for h in range(AH)]
    qpos = q0 + lax.broadcasted_iota(jnp.int32, (QB, WIN), 0)
    kpos = start + lax.broadcasted_iota(jnp.int32, (QB, WIN), 1)
    mask = jnp.abs(kpos - qpos) <= QB
    tabs = (cq_ref[pl.ds(q0, QB), :], sq_ref[pl.ds(q0, QB), :], ck_ref[win, :], sk_ref[win, :])
    return start, (aq_ref[...], ak_ref[win, :], av_ref[win, :], ag, sinks), tabs, mask


def _wattn_specs(s):
    whole = lambda w: _bs((s, w), lambda n: (0, 0))
    one = lambda w: _bs((1, w), lambda n: (0, 0))
    return [
        _bs((QB, 768), lambda n: (n, 0)),
        _bs((s, 256), lambda n: (0, C_AK // 256)),
        _bs((s, 256), lambda n: (0, C_AV // 256)),
        _bs((QB, 256), lambda n: (n, C_AG // 256)),
        _bs((QB, 256), lambda n: (n, C_AG // 256 + 1)),
        _bs((QB, 256), lambda n: (n, C_AG // 256 + 2)),
        whole(768), whole(768), whole(256), whole(256),
        one(768), one(256), one(AH),
    ]


def _wattn_tables(cos, sin, gq, gk):
    t = lambda a, k: jnp.tile(a, (1, k))
    return t(cos, AH), t(sin, AH), t(cos, AKV), t(sin, AKV), t(gq, AH), t(gk, AKV)


def _wattn_fwd(proj, cos, sin, gq, gk, sink):
    s = proj.shape[0]

    def body(aq_ref, ak_ref, av_ref, ag0, ag1, ag2, cq_ref, sq_ref, ck_ref, sk_ref, gq_ref, gk_ref, sink_ref, o_ref):
        _, (aq, ak, av, ag, sinks), tabs, mask = _wattn_load(
            pl.program_id(0), s, aq_ref, ak_ref, av_ref, (ag0, ag1, ag2), cq_ref, sq_ref, ck_ref, sk_ref, sink_ref)
        o_ref[...] = _wattn_tile(aq, ak, av, ag, sinks, gq_ref[...], gk_ref[...], *tabs, mask, *_wattn_consts())

    return pl.pallas_call(
        body, name="attn_fwd", grid=(s // QB,),
        in_specs=_wattn_specs(s), out_specs=_bs((QB, 768), lambda n: (n, 0)),
        out_shape=jax.ShapeDtypeStruct((s, 768), F32),
        compiler_params=_params(("arbitrary",)),
    )(proj, proj, proj, proj, proj, proj, *_wattn_tables(cos, sin, gq, gk), sink)


def _wattn_bwd(proj, cos, sin, gq, gk, sink, dy):
    s = proj.shape[0]

    def body(aq_ref, ak_ref, av_ref, ag0, ag1, ag2, cq_ref, sq_ref, ck_ref, sk_ref, gq_ref, gk_ref, sink_ref, dy_ref,
             daq_ref, dak_ref, dav_ref, dag_ref, dgq_ref, dgk_ref, dsink_ref):
        n = pl.program_id(0)

        @pl.when(n == 0)
        def _():
            for r in (dak_ref, dav_ref, dgq_ref, dgk_ref, dsink_ref):
                r[...] = jnp.zeros_like(r)

        start, args, tabs, mask = _wattn_load(
            n, s, aq_ref, ak_ref, av_ref, (ag0, ag1, ag2), cq_ref, sq_ref, ck_ref, sk_ref, sink_ref)
        consts = _wattn_consts()
        _, vjp = jax.vjp(lambda aq, ak, av, ag, sinks, gq_t, gk_t: _wattn_tile(
            aq, ak, av, ag, sinks, gq_t, gk_t, *tabs, mask, *consts), *args, gq_ref[...], gk_ref[...])
        daq, dak, dav, dag, dsinks, dgq, dgk = vjp(dy_ref[...])
        daq_ref[...] = daq
        dag_ref[...] = dag
        win = pl.ds(start, WIN)
        dak_ref[win, :] += dak
        dav_ref[win, :] += dav
        dgq_ref[...] += dgq
        dgk_ref[...] += dgk
        for h in range(AH):
            dsink_ref[0:1, h:h + 1] += dsinks[h]

    whole = lambda shape: _bs(shape, lambda n: (0, 0))
    blk = _bs((QB, 768), lambda n: (n, 0))
    res = pl.pallas_call(
        body, name="attn_bwd", grid=(s // QB,),
        in_specs=_wattn_specs(s) + [blk],
        out_specs=[blk, whole((s, 256)), whole((s, 256)), blk, whole((1, 768)), whole((1, 256)), whole((1, AH))],
        out_shape=[jax.ShapeDtypeStruct((s, 768), F32), jax.ShapeDtypeStruct((s, 256), F32),
                   jax.ShapeDtypeStruct((s, 256), F32), jax.ShapeDtypeStruct((s, 768), F32),
                   jax.ShapeDtypeStruct((1, 768), F32), jax.ShapeDtypeStruct((1, 256), F32),
                   jax.ShapeDtypeStruct((1, AH), F32)],
        compiler_params=_params(("arbitrary",)),
    )(proj, proj, proj, proj, proj, proj, *_wattn_tables(cos, sin, gq, gk), sink, dy)
    daq, dak, dav, dag, dgq, dgk, dsink = res
    return (daq, dak, dav, dag, dgq.reshape(AH, HD).sum(0, keepdims=True), dgk.reshape(AKV, HD).sum(0, keepdims=True),
            dsink)


def _mem_kv(mem, g, w):
    def body(m_ref, g_ref, w_ref, o_ref, mn_ref):
        mn = _rms(m_ref[...], g_ref[...]).astype(BF16)
        mn_ref[...] = mn
        o_ref[...] = jnp.dot(mn, w_ref[...], preferred_element_type=F32)

    return pl.pallas_call(
        body, name="mem_kv",
        out_shape=[jax.ShapeDtypeStruct((NMEM, 2 * XW), F32), jax.ShapeDtypeStruct((NMEM, D), BF16)],
        compiler_params=_params(),
    )(mem, g, w)


def _xattn_tile(qs, gs, kms, vms, gxq, gxk):
    heads = range(XH)
    q = [_rms(qs[h], gxq) for h in heads]
    km = [_rms(kms[h], gxk) for h in heads]
    sc = [_mm_nt(q[h], km[h]) * (XD ** -0.5) for h in heads]
    p = [jnp.exp(sc[h] - lax.stop_gradient(jnp.max(sc[h], axis=-1, keepdims=True))) for h in heads]
    p = [p[h] / jnp.sum(p[h], axis=-1, keepdims=True) for h in heads]
    return [_mm_nn(p[h], vms[h]) * _silu(gs[h]) for h in heads]


XT = 256


def _xattn_specs():
    return [
        _bs((XT, 256), lambda i: (i, C_XQ // 256)), _bs((XT, 256), lambda i: (i, C_XQ // 256 + 1)),
        _bs((XT, 256), lambda i: (i, C_XG // 256)), _bs((XT, 256), lambda i: (i, C_XG // 256 + 1)),
        _bs((NMEM, 2 * XW), lambda i: (0, 0)),
        _bs((1, XD), lambda i: (0, 0)), _bs((1, XD), lambda i: (0, 0)),
    ]


def _xattn_load(q0, q1, g0, g1, mkv_ref):
    qs = [(q0, q1)[h // 2][:, (h % 2) * XD:(h % 2 + 1) * XD] for h in range(XH)]
    gs = [(g0, g1)[h // 2][:, (h % 2) * XD:(h % 2 + 1) * XD] for h in range(XH)]
    kms = [mkv_ref[:, h * XD:(h + 1) * XD] for h in range(XH)]
    vms = [mkv_ref[:, XW + h * XD:XW + (h + 1) * XD] for h in range(XH)]
    return qs, gs, kms, vms


def _xattn_fwd(proj, mkv, gxq, gxk):
    s = proj.shape[0]

    def body(q0, q1, g0, g1, mkv_ref, gxq_ref, gxk_ref, o_ref):
        qs, gs, kms, vms = _xattn_load(q0, q1, g0, g1, mkv_ref)
        outs = _xattn_tile(qs, gs, kms, vms, gxq_ref[...], gxk_ref[...])
        for h in range(XH):
            o_ref[:, h * XD:(h + 1) * XD] = outs[h]

    return pl.pallas_call(
        body, name="xattn_fwd", grid=(s // XT,),
        in_specs=_xattn_specs(), out_specs=_bs((XT, XW), lambda i: (i, 0)),
        out_shape=jax.ShapeDtypeStruct((s, XW), F32),
        compiler_params=_params(("arbitrary",)),
    )(proj, proj, proj, proj, mkv, gxq, gxk)


def _xattn_bwd(proj, mkv, gxq, gxk, dy):
    s = proj.shape[0]

    def body(q0, q1, g0, g1, mkv_ref, gxq_ref, gxk_ref, dy_ref, dq_ref, dg_ref, dmkv_ref, dgxq_ref, dgxk_ref):
        @pl.when(pl.program_id(0) == 0)
        def _():
            dmkv_ref[...] = jnp.zeros_like(dmkv_ref)
            dgxq_ref[...] = jnp.zeros_like(dgxq_ref)
            dgxk_ref[...] = jnp.zeros_like(dgxk_ref)

        qs, gs, kms, vms = _xattn_load(q0, q1, g0, g1, mkv_ref)
        _, vjp = jax.vjp(_xattn_tile, qs, gs, kms, vms, gxq_ref[...], gxk_ref[...])
        dqs, dgs, dkms, dvms, dgxq, dgxk = vjp([dy_ref[:, h * XD:(h + 1) * XD] for h in range(XH)])
        for h in range(XH):
            dq_ref[:, h * XD:(h + 1) * XD] = dqs[h]
            dg_ref[:, h * XD:(h + 1) * XD] = dgs[h]
            dmkv_ref[:, h * XD:(h + 1) * XD] += dkms[h]
            dmkv_ref[:, XW + h * XD:XW + (h + 1) * XD] += dvms[h]
        dgxq_ref[...] += dgxq
        dgxk_ref[...] += dgxk

    whole = lambda shape: _bs(shape, lambda i: (0, 0))
    return pl.pallas_call(
        body, name="xattn_bwd", grid=(s // XT,),
        in_specs=_xattn_specs() + [_bs((XT, XW), lambda i: (i, 0))],
        out_specs=[_bs((XT, XW), lambda i: (i, 0)), _bs((XT, XW), lambda i: (i, 0)), whole((NMEM, 2 * XW)),
                   whole((1, XD)), whole((1, XD))],
        out_shape=[jax.ShapeDtypeStruct((s, XW), F32), jax.ShapeDtypeStruct((s, XW), F32),
                   jax.ShapeDtypeStruct((NMEM, 2 * XW), F32), jax.ShapeDtypeStruct((1, XD), F32),
                   jax.ShapeDtypeStruct((1, XD), F32)],
        compiler_params=_params(("arbitrary",)),
    )(proj, proj, proj, proj, mkv, gxq, gxk, dy)


def _mem_bwd(mem, g, dmn):
    def body(m_ref, dmn_ref, o_ref):
        m = m_ref[...]
        r = lax.rsqrt(jnp.mean(m * m, axis=-1, keepdims=True) + NORM_EPS)
        o_ref[...] = jnp.sum(dmn_ref[...] * m * r, axis=0, keepdims=True)

    del g
    return pl.pallas_call(body, name="mem_norm_bwd", out_shape=jax.ShapeDtypeStruct((1, D), F32),
                          compiler_params=_params())(mem, dmn)


SHIFT_W = 512


def _shift_rows(p, s):
    row = lax.broadcasted_iota(jnp.int32, p.shape, 0)
    prev = jnp.where(row == 0, 0.0, pltpu.roll(p, 1, 0))
    nxt = jnp.where(row == s - 1, 0.0, pltpu.roll(p, s - 1, 0))
    return prev, nxt


def _shift_fwd(proj, mu):
    s = proj.shape[0]

    def body(p_ref, mu_ref, o_ref):
        p = p_ref[...]
        prev, nxt = _shift_rows(p, s)
        o_ref[...] = p + mu_ref[...] * (0.5 * (prev + nxt) - p)

    return pl.pallas_call(
        body, name="shift_fwd", grid=(RSW // SHIFT_W,),
        in_specs=[_bs((s, SHIFT_W), lambda j: (0, C_RS // SHIFT_W + j)), _bs((1, SHIFT_W), lambda j: (0, j))],
        out_specs=_bs((s, SHIFT_W), lambda j: (0, j)),
        out_shape=jax.ShapeDtypeStruct((s, RSW), F32),
        compiler_params=_params(("parallel",)),
    )(proj, mu)


def _shift_bwd(proj, mu, dps):
    s = proj.shape[0]

    def body(p_ref, mu_ref, g_ref, o_ref, dmu_ref):
        p, g, mu_v = p_ref[...], g_ref[...], mu_ref[...]
        prev, nxt = _shift_rows(p, s)
        dmu_ref[...] = jnp.sum(g * (0.5 * (prev + nxt) - p), axis=0, keepdims=True)
        mg = mu_v * g
        down, up = _shift_rows(mg, s)
        o_ref[...] = g * (1.0 - mu_v) + 0.5 * (down + up)

    return pl.pallas_call(
        body, name="shift_bwd", grid=(RSW // SHIFT_W,),
        in_specs=[_bs((s, SHIFT_W), lambda j: (0, C_RS // SHIFT_W + j)), _bs((1, SHIFT_W), lambda j: (0, j)),
                  _bs((s, SHIFT_W), lambda j: (0, j))],
        out_specs=[_bs((s, SHIFT_W), lambda j: (0, j)), _bs((1, SHIFT_W), lambda j: (0, j))],
        out_shape=[jax.ShapeDtypeStruct((s, RSW), F32), jax.ShapeDtypeStruct((1, RSW), F32)],
        compiler_params=_params(("parallel",)),
    )(proj, mu, dps)


def _pre_tile(k, wf, wb, af, ab, k_k, k_a, w0s, w2s, a0s, a2s, seg):
    kx = k * k_k
    ss = _hdot(kx * kx, seg)
    kk = kx / jnp.maximum(jnp.sqrt(ss), 1e-12)
    outs = [kk]
    for d, (w_in, a_in) in enumerate(((wf, af), (wb, ab))):
        z = w0s[d] + _mm_nn(jnp.tanh(w_in), w2s[d])
        wd = -_softplus(-z) - 0.5
        dec = jnp.exp(-jnp.exp(wd))
        ad = jax.nn.sigmoid(a0s[d] + _mm_nn(a_in, a2s[d]))
        kd = k * (1.0 + (ad - 1.0) * k_a)
        outs += [dec, kd, kk * ad]
    return outs


PT = 256


def _pre_load(ps_ref, kk_ref, ka_ref, w0_ref, w2_ref, a0_ref, a2_ref):
    k = ps_ref[:, RW:2 * RW]
    wf, wb = ps_ref[:, 3 * RW:3 * RW + 64], ps_ref[:, 3 * RW + 64:3 * RW + 128]
    af, ab = ps_ref[:, 3 * RW + 128:3 * RW + 192], ps_ref[:, 3 * RW + 192:3 * RW + 256]
    w0s = [w0_ref[0:1, :], w0_ref[1:2, :]]
    a0s = [a0_ref[0:1, :], a0_ref[1:2, :]]
    w2s = [w2_ref[0], w2_ref[1]]
    a2s = [a2_ref[0], a2_ref[1]]
    return (k, wf, wb, af, ab, kk_ref[...], ka_ref[...], w0s, w2s, a0s, a2s)


def _pre_specs():
    c = lambda shape: _bs(shape, lambda i: tuple(0 for _ in shape))
    return [_bs((PT, RSW), lambda i: (i, 0)), c((1, RW)), c((1, RW)), c((2, RW)), c((2, 64, RW)), c((2, RW)),
            c((2, 64, RW))]


def _pre_fwd(ps, k_k, k_a, w0, w2, a0, a2):
    s = ps.shape[0]

    def body(ps_ref, kk_ref, ka_ref, w0_ref, w2_ref, a0_ref, a2_ref, *outs):
        args = _pre_load(ps_ref, kk_ref, ka_ref, w0_ref, w2_ref, a0_ref, a2_ref)
        res = _pre_tile(*args, _seg_matrix(RW, HD))
        for o_ref, v in zip(outs, res):
            o_ref[...] = v

    return pl.pallas_call(
        body, name="rwkv_pre_fwd", grid=(s // PT,),
        in_specs=_pre_specs(), out_specs=[_bs((PT, RW), lambda i: (i, 0))] * 7,
        out_shape=[jax.ShapeDtypeStruct((s, RW), F32)] * 7,
        compiler_params=_params(("parallel",)),
    )(ps, k_k, k_a, w0, w2, a0, a2)


def _pre_bwd(ps, k_k, k_a, w0, w2, a0, a2, dr, dv, cts):
    s = ps.shape[0]

    def body(ps_ref, kk_ref, ka_ref, w0_ref, w2_ref, a0_ref, a2_ref, dr_ref, dv_ref, c0, c1, c2, c3, c4, c5, c6,
             dps_ref, dkk_ref, dka_ref, dw0_ref, dw2_ref, da0_ref, da2_ref):
        @pl.when(pl.program_id(0) == 0)
        def _():
            for r in (dkk_ref, dka_ref, dw0_ref, dw2_ref, da0_ref, da2_ref):
                r[...] = jnp.zeros_like(r)

        args = _pre_load(ps_ref, kk_ref, ka_ref, w0_ref, w2_ref, a0_ref, a2_ref)
        seg = _seg_matrix(RW, HD)
        _, vjp = jax.vjp(lambda *a: _pre_tile(*a, seg), *args)
        dk, dwf, dwb, daf, dab, dk_k, dk_a, dw0s, dw2s, da0s, da2s = vjp([c[...] for c in (c0, c1, c2, c3, c4, c5, c6)])
        dps_ref[:, 0:RW] = dr_ref[...]
        dps_ref[:, RW:2 * RW] = dk
        dps_ref[:, 2 * RW:3 * RW] = dv_ref[...]
        for j, t in enumerate((dwf, dwb, daf, dab)):
            dps_ref[:, 3 * RW + 64 * j:3 * RW + 64 * (j + 1)] = t
        dkk_ref[...] += dk_k
        dka_ref[...] += dk_a
        for d in range(2):
            dw0_ref[d:d + 1, :] += dw0s[d]
            da0_ref[d:d + 1, :] += da0s[d]
            dw2_ref[d] += dw2s[d]
            da2_ref[d] += da2s[d]

    c = lambda shape: _bs(shape, lambda i: tuple(0 for _ in shape))
    row = _bs((PT, RW), lambda i: (i, 0))
    return pl.pallas_call(
        body, name="rwkv_pre_bwd", grid=(s // PT,),
        in_specs=_pre_specs() + [row] * 9,
        out_specs=[_bs((PT, RSW), lambda i: (i, 0)), c((1, RW)), c((1, RW)), c((2, RW)), c((2, 64, RW)), c((2, RW)),
                   c((2, 64, RW))],
        out_shape=[jax.ShapeDtypeStruct((s, RSW), F32), jax.ShapeDtypeStruct((1, RW), F32),
                   jax.ShapeDtypeStruct((1, RW), F32), jax.ShapeDtypeStruct((2, RW), F32),
                   jax.ShapeDtypeStruct((2, 64, RW), F32), jax.ShapeDtypeStruct((2, RW), F32),
                   jax.ShapeDtypeStruct((2, 64, RW), F32)],
        compiler_params=_params(("arbitrary",)),
    )(ps, k_k, k_a, w0, w2, a0, a2, dr, dv, *cts)


def _post_tile(y0, y1, r, v, kd0, kd1, rg, r_k, ln_w, ln_b, seg):
    ysum = y0 + y1
    bonus = (_hdot(r * kd0 * r_k, seg) + _hdot(r * kd1 * r_k, seg)) * v
    mean = _hdot(ysum, seg) * (1.0 / HD)
    cen = ysum - mean
    var = _hdot(cen * cen, seg) * (1.0 / HD)
    y = cen * lax.rsqrt(var + GN_EPS) * ln_w + ln_b + bonus
    return y * _silu(rg)


def _post_specs():
    row = _bs((PT, RW), lambda i: (i, 0))
    c = _bs((1, RW), lambda i: (0, 0))
    return [row, row, _bs((PT, RW), lambda i: (i, 0)), _bs((PT, RW), lambda i: (i, 2)), row, row,
            _bs((PT, RW), lambda i: (i, C_RG // RW)), c, c, c]


def _post_fwd(y0, y1, ps, kd0, kd1, proj, r_k, ln_w, ln_b):
    s = ps.shape[0]

    def body(y0_ref, y1_ref, r_ref, v_ref, kd0_ref, kd1_ref, rg_ref, rk_ref, lw_ref, lb_ref, o_ref):
        o_ref[...] = _post_tile(y0_ref[...], y1_ref[...], r_ref[...], v_ref[...], kd0_ref[...], kd1_ref[...],
                                rg_ref[...], rk_ref[...], lw_ref[...], lb_ref[...], _seg_matrix(RW, HD))

    return pl.pallas_call(
        body, name="rwkv_post_fwd", grid=(s // PT,),
        in_specs=_post_specs(), out_specs=_bs((PT, RW), lambda i: (i, 0)),
        out_shape=jax.ShapeDtypeStruct((s, RW), F32),
        compiler_params=_params(("parallel",)),
    )(y0, y1, ps, ps, kd0, kd1, proj, r_k, ln_w, ln_b)


def _post_bwd(y0, y1, ps, kd0, kd1, proj, r_k, ln_w, ln_b, dy):
    s = ps.shape[0]

    def body(y0_ref, y1_ref, r_ref, v_ref, kd0_ref, kd1_ref, rg_ref, rk_ref, lw_ref, lb_ref, dy_ref,
             dys_ref, dr_ref, dv_ref, dkd0_ref, dkd1_ref, drg_ref, drk_ref, dlw_ref, dlb_ref):
        @pl.when(pl.program_id(0) == 0)
        def _():
            for r in (drk_ref, dlw_ref, dlb_ref):
                r[...] = jnp.zeros_like(r)

        seg = _seg_matrix(RW, HD)
        args = [t[...] for t in (y0_ref, y1_ref, r_ref, v_ref, kd0_ref, kd1_ref, rg_ref, rk_ref, lw_ref, lb_ref)]
        _, vjp = jax.vjp(lambda *a: _post_tile(*a, seg), *args)
        dy0, _, dr, dv, dkd0, dkd1, drg, drk, dlw, dlb = vjp(dy_ref[...])
        dys_ref[...] = dy0
        dr_ref[...] = dr
        dv_ref[...] = dv
        dkd0_ref[...] = dkd0
        dkd1_ref[...] = dkd1
        drg_ref[...] = drg
        drk_ref[...] += drk
        dlw_ref[...] += dlw
        dlb_ref[...] += dlb

    row = _bs((PT, RW), lambda i: (i, 0))
    c = _bs((1, RW), lambda i: (0, 0))
    return pl.pallas_call(
        body, name="rwkv_post_bwd", grid=(s // PT,),
        in_specs=_post_specs() + [row], out_specs=[row] * 6 + [c] * 3,
        out_shape=[jax.ShapeDtypeStruct((s, RW), F32)] * 6 + [jax.ShapeDtypeStruct((1, RW), F32)] * 3,
        compiler_params=_params(("arbitrary",)),
    )(y0, y1, ps, ps, kd0, kd1, proj, r_k, ln_w, ln_b, dy)


def _ones2():
    r = lax.broadcasted_iota(jnp.int32, (256, 128), 0) % 128 // HD
    c = lax.broadcasted_iota(jnp.int32, (256, 128), 1) // HD
    return (r == c).astype(BF16)


def _split(p):
    hi = p.astype(BF16)
    lo = (p - hi.astype(F32)).astype(BF16)
    return jnp.concatenate([hi, lo], axis=1)


def _to_t8(a):
    s = a.shape[0]
    t = a.reshape(s // 8, 8, NPAIR, 2, HD).transpose(0, 2, 4, 3, 1)
    t = jnp.pad(t, ((0, 0), (0, 0), (0, 0), (0, 0), (0, HD - 8))).reshape(s // 8, NPAIR, HD, 128)
    hi = t.astype(BF16)
    lo = (t - hi.astype(F32)).astype(BF16)
    return jnp.concatenate([hi, lo], axis=-1)


def _from_t8(t8):
    g = t8.shape[0]
    t = t8.reshape(g, NPAIR, HD, 2, HD)[..., :8]
    return t.transpose(0, 4, 1, 3, 2).reshape(g * 8, RW)


def _scan_specs(direction, nc, fwd_order):
    def tb(c):
        sc = c if fwd_order else nc - 1 - c
        return sc if direction == 0 else nc - 1 - sc

    row = _bs((TC, RW), lambda c: (tb(c), 0))
    rowv = _bs((TC, RW), lambda c: (tb(c), 2))
    return row, rowv


def _put_t8(ref, g, u, tiles):
    for p in range(NPAIR):
        ref[g, p, :, u:u + 1] = tiles[p][:, u:u + 1]
        ref[g, p, :, HD + u:HD + u + 1] = tiles[p][:, HD + u:HD + u + 1]


def _scan_fwd(dec, kd, b, ps, kk, vl, direction):
    s = dec.shape[0]
    nc, ng = s // TC, TC // 8
    row, t8_in, t8_out = _scan_specs(direction, nc, True)
    n = NPAIR * HD

    def body(dec_ref, kd_ref, b_ref, r_ref, kk_ref, vl_ref, y8_ref, ck_ref, st):
        @pl.when(pl.program_id(0) == 0)
        def _():
            st[...] = jnp.zeros_like(st)

        ck_ref[0] = st[...]
        ones2 = _ones2()
        lane_u = lax.broadcasted_iota(jnp.int32, (HD, 256), 1) % HD
        tiles = lambda res, k: [res[k * n + p * HD:k * n + (p + 1) * HD] for p in range(NPAIR)]

        def group(gi, carry):
            g = gi if direction == 0 else ng - 1 - gi
            rows8 = pl.ds(pl.multiple_of(g * 8, 8), 8)
            d8, k8, b8, r8, kk8 = (q[rows8, :] for q in (dec_ref, kd_ref, b_ref, r_ref, kk_ref))
            pc = [slice(p * 128, (p + 1) * 128) for p in range(NPAIR)]
            ss = [st[p] for p in range(NPAIR)]
            u_prev = None
            for ui in range(8):
                u = ui if direction == 0 else 7 - ui
                lhs = [_split(ss[p] * kk8[u:u + 1, pc[p]]) for p in range(NPAIR)]
                for p in range(NPAIR):
                    vt = vl_ref[g, p]
                    lhs.append(jnp.where(lane_u == u, vt, jnp.zeros_like(vt)))
                if u_prev is not None:
                    lhs += [_split(ss[p] * r8[u_prev:u_prev + 1, pc[p]]) for p in range(NPAIR)]
                res = jnp.dot(jnp.concatenate(lhs, axis=0), ones2, preferred_element_type=F32)
                if u_prev is not None:
                    _put_t8(y8_ref, g, u_prev, tiles(res, 2))
                sa, vb = tiles(res, 0), tiles(res, 1)
                for p in range(NPAIR):
                    ss[p] = ss[p] * d8[u:u + 1, pc[p]] - sa[p] * b8[u:u + 1, pc[p]] + vb[p] * k8[u:u + 1, pc[p]]
                u_prev = u
            lhs = [_split(ss[p] * r8[u_prev:u_prev + 1, pc[p]]) for p in range(NPAIR)]
            res = jnp.dot(jnp.concatenate(lhs, axis=0), ones2, preferred_element_type=F32)
            _put_t8(y8_ref, g, u_prev, tiles(res, 0))
            for p in range(NPAIR):
                st[p] = ss[p]
            return carry

        lax.fori_loop(0, ng, group, 0)

    return pl.pallas_call(
        body, name=f"rwkv_scan_fwd{direction}", grid=(nc,),
        in_specs=[row, row, row, row, row, t8_in],
        out_specs=[t8_out, _bs((1, NPAIR, HD, 128), lambda c: (c, 0, 0, 0))],
        out_shape=[jax.ShapeDtypeStruct((s // 8, NPAIR, HD, 128), F32),
                   jax.ShapeDtypeStruct((nc, NPAIR, HD, 128), F32)],
        scratch_shapes=[pltpu.VMEM((NPAIR, HD, 128), F32)],
        compiler_params=_params(("arbitrary",)),
    )(dec, kd, b, ps, kk, vl)


def _scan_bwd(dec, kd, b, ps, kk, vl, dyl, ck, direction):
    s = dec.shape[0]
    nc, ng = s // TC, TC // 8
    row, t8_in, t8_out = _scan_specs(direction, nc, False)
    n = NPAIR * HD

    def body(dec_ref, kd_ref, b_ref, r_ref, kk_ref, vl_ref, dyl_ref, ck_ref,
             dr_ref, dd_ref, db_ref, dk_ref, dkk_ref, dv8_ref, st, sa_s, vb_s, dy_s, ds):
        @pl.when(pl.program_id(0) == 0)
        def _():
            ds[...] = jnp.zeros_like(ds)

        st[0] = ck_ref[0]
        ones2 = _ones2()
        lane_u = lax.broadcasted_iota(jnp.int32, (HD, 256), 1) % HD
        row_id = lax.broadcasted_iota(jnp.int32, (8, 128), 0)
        pc = [slice(p * 128, (p + 1) * 128) for p in range(NPAIR)]
        tiles = lambda res, k: [res[k * n + p * HD:k * n + (p + 1) * HD] for p in range(NPAIR)]

        def fgroup(gi, carry):
            g = gi if direction == 0 else ng - 1 - gi
            rows8 = pl.ds(pl.multiple_of(g * 8, 8), 8)
            d8, k8, b8, kk8 = (q[rows8, :] for q in (dec_ref, kd_ref, b_ref, kk_ref))
            ss = [st[gi * 8, p] for p in range(NPAIR)]
            for ui in range(8):
                u = ui if direction == 0 else 7 - ui
                i = gi * 8 + ui
                lhs = [_split(ss[p] * kk8[u:u + 1, pc[p]]) for p in range(NPAIR)]
                for ref in (vl_ref, dyl_ref):
                    for p in range(NPAIR):
                        t = ref[g, p]
                        lhs.append(jnp.where(lane_u == u, t, jnp.zeros_like(t)))
                res = jnp.dot(jnp.concatenate(lhs, axis=0), ones2, preferred_element_type=F32)
                sa, vb, dyb = tiles(res, 0), tiles(res, 1), tiles(res, 2)
                for p in range(NPAIR):
                    sa_s[i, p] = sa[p]
                    vb_s[i, p] = vb[p]
                    dy_s[i, p] = dyb[p]
                    ss[p] = ss[p] * d8[u:u + 1, pc[p]] - sa[p] * b8[u:u + 1, pc[p]] + vb[p] * k8[u:u + 1, pc[p]]
                    st[i + 1, p] = ss[p]
            return carry

        lax.fori_loop(0, ng, fgroup, 0)

        def bgroup(gj, carry):
            gi = ng - 1 - gj
            g = gi if direction == 0 else ng - 1 - gi
            rows8 = pl.ds(pl.multiple_of(g * 8, 8), 8)
            d8, k8, b8, r8, kk8 = (q[rows8, :] for q in (dec_ref, kd_ref, b_ref, r_ref, kk_ref))
            dss = [ds[p] for p in range(NPAIR)]
            acc = [[jnp.zeros((8, 128), F32) for _ in range(5)] for _ in range(NPAIR)]
            for uj in range(8):
                ui = 7 - uj
                u = ui if direction == 0 else 7 - ui
                i = gi * 8 + ui
                dyb = [dy_s[i, p] for p in range(NPAIR)]
                for p in range(NPAIR):
                    dss[p] = dss[p] + dyb[p] * r8[u:u + 1, pc[p]]
                lhs = [_split(dss[p] * b8[u:u + 1, pc[p]]) for p in range(NPAIR)]
                lhs += [_split(dss[p] * k8[u:u + 1, pc[p]]) for p in range(NPAIR)]
                res = jnp.dot(jnp.concatenate(lhs, axis=0), ones2, preferred_element_type=F32)
                dsa, dvb = tiles(res, 0), tiles(res, 1)
                _put_t8(dv8_ref, g, u, dvb)
                for p in range(NPAIR):
                    sp, sn = st[i, p], st[i + 1, p]
                    outs = (jnp.sum(sn * dyb[p], axis=0, keepdims=True), jnp.sum(dss[p] * sp, axis=0, keepdims=True),
                            -jnp.sum(dss[p] * sa_s[i, p], axis=0, keepdims=True),
                            jnp.sum(dss[p] * vb_s[i, p], axis=0, keepdims=True),
                            -jnp.sum(sp * dsa[p], axis=0, keepdims=True))
                    acc[p] = [jnp.where(row_id == u, o, a_) for o, a_ in zip(outs, acc[p])]
                    dss[p] = dss[p] * d8[u:u + 1, pc[p]] - dsa[p] * kk8[u:u + 1, pc[p]]
            for p in range(NPAIR):
                ds[p] = dss[p]
                for o_ref, a_ in zip((dr_ref, dd_ref, db_ref, dk_ref, dkk_ref), acc[p]):
                    o_ref[rows8, pc[p]] = a_
            return carry

        lax.fori_loop(0, ng, bgroup, 0)

    chunk = lambda k: pltpu.VMEM((k, NPAIR, HD, 128), F32)
    return pl.pallas_call(
        body, name=f"rwkv_scan_bwd{direction}", grid=(nc,),
        in_specs=[row, row, row, row, row, t8_in, t8_in, _bs((1, NPAIR, HD, 128), lambda c: (nc - 1 - c, 0, 0, 0))],
        out_specs=[row] * 5 + [t8_out],
        out_shape=[jax.ShapeDtypeStruct((s, RW), F32)] * 5 + [jax.ShapeDtypeStruct((s // 8, NPAIR, HD, 128), F32)],
        scratch_shapes=[chunk(TC + 1), chunk(TC), chunk(TC), chunk(TC), pltpu.VMEM((NPAIR, HD, 128), F32)],
        compiler_params=_params(("arbitrary",)),
    )(dec, kd, b, ps, kk, vl, dyl, ck)


def _tiles(res, k):
    n = NPAIR * HD
    return [res[k * n + p * HD:k * n + (p + 1) * HD] for p in range(NPAIR)]


def _rows_to_tiles(src_ref, rows8, stage, out_s, base):
    for p in range(NPAIR):
        stage[base + p, 0:8, 0:HD] = src_ref[rows8, p * 128:p * 128 + HD]
        stage[base + p, HD:HD + 8, 0:HD] = src_ref[rows8, p * 128 + HD:(p + 1) * 128]
        out_s[base + p] = stage[base + p].T[0:HD].astype(BF16)


def _tiles_to_rows(tile_s, base, dst_ref, rows8):
    for p in range(NPAIR):
        t = jnp.concatenate([tile_s[base + p], jnp.zeros((HD, 128), F32)], axis=0).T
        dst_ref[rows8, p * 128:p * 128 + HD] = t[0:8, 0:HD]
        dst_ref[rows8, p * 128 + HD:(p + 1) * 128] = t[HD:HD + 8, 0:HD]


def _put_cols(tile_s, base, u, tiles):
    for p in range(NPAIR):
        tile_s[base + p, :, u:u + 1] = tiles[p][:, u:u + 1]
        tile_s[base + p, :, HD + u:HD + u + 1] = tiles[p][:, HD + u:HD + u + 1]


def _scan2_fwd(per_dir, ps, kk, gather=()):
    s = ps.shape[0]
    nc, ng = s // TC, TC // 8
    ngat = len(gather)
    in_specs, operands, out_specs, out_shape = [], [], [], []
    for d in (0, 1):
        row, rowv = _scan_specs(d, nc, True)
        in_specs += [row] * 5 + [rowv]
        operands += list(per_dir[d]) + [ps, kk, ps]
        out_specs += [row, _bs((1, NPAIR, HD, 128), lambda c: (c, 0, 0, 0))]
        out_shape += [jax.ShapeDtypeStruct((s, RW), F32), jax.ShapeDtypeStruct((nc, NPAIR, HD, 128), F32)]
    in_specs += [ANY] * ngat
    operands += list(gather)
    out_specs += [ANY] * ngat
    out_shape += _gather_out_shapes(gather)

    def body(*refs):
        ins = [refs[0:6], refs[6:12]]
        base = 12 + ngat
        y_refs, ck_refs = (refs[base], refs[base + 2]), (refs[base + 1], refs[base + 3])
        st, vt_s, yt_s, stage = refs[base + 4 + ngat:base + 8 + ngat]
        if ngat:
            g_start, g_forward, g_finish = _gather_phases(
                gather, refs[12:base], refs[base + 4:base + 4 + ngat], refs[base + 8 + ngat:])

        @pl.when(pl.program_id(0) == 0)
        def _():
            st[...] = jnp.zeros_like(st)
            yt_s[...] = jnp.zeros_like(yt_s)
            stage[...] = jnp.zeros_like(stage)
            if ngat:
                g_start()

        if ngat:
            @pl.when(pl.program_id(0) == nc // 2)
            def _():
                g_forward()

        for d in (0, 1):
            ck_refs[d][0] = st[d * NPAIR:(d + 1) * NPAIR]
        ones2 = _ones2()
        ones1 = ones2[0:128]
        lane_u = lax.broadcasted_iota(jnp.int32, (HD, 128), 1) % HD
        pc = [slice(p * 128, (p + 1) * 128) for p in range(NPAIR)]

        def group(gi, carry):
            gs = (gi, ng - 1 - gi)
            rows8 = [pl.ds(pl.multiple_of(gs[d] * 8, 8), 8) for d in (0, 1)]
            blk = [[q[rows8[d], :] for q in ins[d][:5]] for d in (0, 1)]
            for d in (0, 1):
                _rows_to_tiles(ins[d][5], rows8[d], stage, vt_s, d * NPAIR)
            ss = [[st[d * NPAIR + p] for p in range(NPAIR)] for d in (0, 1)]
            for ui in range(9):
                us, ups = (ui, 7 - ui), (ui - 1, 8 - ui)
                lhs1, where = [], {}
                for d in (0, 1):
                    if ui < 8:
                        where["sa", d] = len(lhs1) // NPAIR
                        lhs1 += [(ss[d][p] * blk[d][4][us[d]:us[d] + 1, pc[p]]).astype(BF16) for p in range(NPAIR)]
                        where["vb", d] = len(lhs1) // NPAIR
                        for p in range(NPAIR):
                            vt = vt_s[d * NPAIR + p]
                            lhs1.append(jnp.where(lane_u == us[d], vt, jnp.zeros_like(vt)))
                    if ui > 0:
                        where["y", d] = len(lhs1) // NPAIR
                        lhs1 += [(ss[d][p] * blk[d][3][ups[d]:ups[d] + 1, pc[p]]).astype(BF16) for p in range(NPAIR)]
                res1 = jnp.dot(jnp.concatenate(lhs1, axis=0), ones1, preferred_element_type=F32)
                for d in (0, 1):
                    d8, k8, b8, _, _ = blk[d]
                    u = us[d]
                    if ui < 8:
                        sa, vb = _tiles(res1, where["sa", d]), _tiles(res1, where["vb", d])
                        for p in range(NPAIR):
                            ss[d][p] = (ss[d][p] * d8[u:u + 1, pc[p]] - sa[p] * b8[u:u + 1, pc[p]]
                                        + vb[p] * k8[u:u + 1, pc[p]])
                    if ui > 0:
                        _put_cols(yt_s, d * NPAIR, ups[d], _tiles(res1, where["y", d]))
            for d in (0, 1):
                _tiles_to_rows(yt_s, d * NPAIR, y_refs[d], rows8[d])
                for p in range(NPAIR):
                    st[d * NPAIR + p] = ss[d][p]
            return carry

        for gi in range(ng):
            group(gi, 0)

        if ngat:
            @pl.when(pl.program_id(0) == nc - 1)
            def _():
                g_finish()

    outs = pl.pallas_call(
        body, name="rwkv_scan_fwd", grid=(nc,), in_specs=in_specs, out_specs=out_specs, out_shape=out_shape,
        scratch_shapes=[pltpu.VMEM((2 * NPAIR, HD, 128), F32), pltpu.VMEM((2 * NPAIR, HD, 128), BF16),
                        pltpu.VMEM((2 * NPAIR, HD, 128), F32), pltpu.VMEM((2 * NPAIR, 128, 128), F32)]
        + (_gather_sems(ngat) if ngat else []),
        compiler_params=pltpu.CompilerParams(dimension_semantics=("arbitrary",), vmem_limit_bytes=VMEM_LIMIT,
                                             has_side_effects=bool(ngat)),
    )(*operands)
    return [(outs[0], outs[1]), (outs[2], outs[3])], list(outs[4:])


def _scan2_bwd(per_dir, ps, kk, dy, scatter=()):
    s = ps.shape[0]
    nc, ng = s // TC, TC // 8
    nsc = len(scatter)
    in_specs, operands, out_specs, out_shape = [], [], [], []
    for d in (0, 1):
        row, rowv = _scan_specs(d, nc, False)
        dec, kd, b, ck = per_dir[d]
        in_specs += [row] * 5 + [rowv, row, _bs((1, NPAIR, HD, 128), lambda c: (nc - 1 - c, 0, 0, 0))]
        operands += [dec, kd, b, ps, kk, ps, dy, ck]
        out_specs += [row] * 6
        out_shape += [jax.ShapeDtypeStruct((s, RW), F32)] * 6
    in_specs += [ANY] * nsc
    operands += list(scatter)
    out_specs += [ANY] * nsc
    out_shape += _scatter_out_shapes(scatter)

    def body(*refs):
        ins = [refs[0:8], refs[8:16]]
        base = 16 + nsc
        outs = [refs[base:base + 6], refs[base + 6:base + 12]]
        st, sa_s, vb_s, dy_s, ds, vt_s, dyt_s, dvt_s, stage = refs[base + 12 + nsc:base + 21 + nsc]
        if nsc:
            s_start, s_finish = _scatter_phases(refs[16:base], refs[base + 12:base + 12 + nsc], refs[base + 21 + nsc:])

        @pl.when(pl.program_id(0) == 0)
        def _():
            dvt_s[...] = jnp.zeros_like(dvt_s)
            stage[...] = jnp.zeros_like(stage)
            ds[...] = jnp.zeros_like(ds)
            if nsc:
                s_start()

        for d in (0, 1):
            st[d * (TC + 1)] = ins[d][7][0]
        ones2 = _ones2()
        ones1 = ones2[0:128]
        lane_u = lax.broadcasted_iota(jnp.int32, (HD, 128), 1) % HD
        row_id = lax.broadcasted_iota(jnp.int32, (8, 128), 0)
        pc = [slice(p * 128, (p + 1) * 128) for p in range(NPAIR)]

        def load_rows(gs):
            return [[q[pl.ds(pl.multiple_of(gs[d] * 8, 8), 8), :] for q in ins[d][:5]] for d in (0, 1)]

        def fgroup(gi, carry):
            gs = (gi, ng - 1 - gi)
            blk = load_rows(gs)
            for d in (0, 1):
                rows8 = pl.ds(pl.multiple_of(gs[d] * 8, 8), 8)
                _rows_to_tiles(ins[d][5], rows8, stage, vt_s, d * NPAIR)
                _rows_to_tiles(ins[d][6], rows8, stage, dyt_s, d * NPAIR)
            ss = [[st[d * (TC + 1) + gi * 8, p] for p in range(NPAIR)] for d in (0, 1)]
            for ui in range(8):
                us = (ui, 7 - ui)
                i = gi * 8 + ui
                lhs1 = []
                for d in (0, 1):
                    kk8 = blk[d][4]
                    lhs1 += [(ss[d][p] * kk8[us[d]:us[d] + 1, pc[p]]).astype(BF16) for p in range(NPAIR)]
                    for tile_s in (vt_s, dyt_s):
                        for p in range(NPAIR):
                            t = tile_s[d * NPAIR + p]
                            lhs1.append(jnp.where(lane_u == us[d], t, jnp.zeros_like(t)))
                res1 = jnp.dot(jnp.concatenate(lhs1, axis=0), ones1, preferred_element_type=F32)
                for d in (0, 1):
                    d8, k8, b8, _, _ = blk[d]
                    u = us[d]
                    sa, vb, dyb = _tiles(res1, 3 * d), _tiles(res1, 3 * d + 1), _tiles(res1, 3 * d + 2)
                    for p in range(NPAIR):
                        sa_s[d * TC + i, p] = sa[p]
                        vb_s[d * TC + i, p] = vb[p]
                        dy_s[d * TC + i, p] = dyb[p]
                        ss[d][p] = ss[d][p] * d8[u:u + 1, pc[p]] - sa[p] * b8[u:u + 1, pc[p]] + vb[p] * k8[u:u + 1, pc[p]]
                        st[d * (TC + 1) + i + 1, p] = ss[d][p]
            return carry

        for gi in range(ng):
            fgroup(gi, 0)

        def bgroup(gj, carry):
            gi = ng - 1 - gj
            gs = (gi, ng - 1 - gi)
            blk = load_rows(gs)
            dss = [[ds[d * NPAIR + p] for p in range(NPAIR)] for d in (0, 1)]
            acc = [[[jnp.zeros((8, 128), F32) for _ in range(5)] for _ in range(NPAIR)] for _ in (0, 1)]
            for uj in range(8):
                ui = 7 - uj
                us = (ui, 7 - ui)
                i = gi * 8 + ui
                lhs1, dyb = [], [None, None]
                for d in (0, 1):
                    _, k8, b8, r8, _ = blk[d]
                    u = us[d]
                    dyb[d] = [dy_s[d * TC + i, p] for p in range(NPAIR)]
                    for p in range(NPAIR):
                        dss[d][p] = dss[d][p] + dyb[d][p] * r8[u:u + 1, pc[p]]
                    lhs1 += [(dss[d][p] * b8[u:u + 1, pc[p]]).astype(BF16) for p in range(NPAIR)]
                    lhs1 += [(dss[d][p] * k8[u:u + 1, pc[p]]).astype(BF16) for p in range(NPAIR)]
                res1 = jnp.dot(jnp.concatenate(lhs1, axis=0), ones1, preferred_element_type=F32)
                for d in (0, 1):
                    d8, _, _, _, kk8 = blk[d]
                    u = us[d]
                    dsa, dvb = _tiles(res1, 2 * d), _tiles(res1, 2 * d + 1)
                    _put_cols(dvt_s, d * NPAIR, u, dvb)
                    for p in range(NPAIR):
                        sp, sn = st[d * (TC + 1) + i, p], st[d * (TC + 1) + i + 1, p]
                        dsv = dss[d][p]
                        vals = (jnp.sum(sn * dyb[d][p], axis=0, keepdims=True), jnp.sum(dsv * sp, axis=0, keepdims=True),
                                -jnp.sum(dsv * sa_s[d * TC + i, p], axis=0, keepdims=True),
                                jnp.sum(dsv * vb_s[d * TC + i, p], axis=0, keepdims=True),
                                -jnp.sum(sp * dsa[p], axis=0, keepdims=True))
                        acc[d][p] = [jnp.where(row_id == u, o, a_) for o, a_ in zip(vals, acc[d][p])]
                        dss[d][p] = dsv * d8[u:u + 1, pc[p]] - dsa[p] * kk8[u:u + 1, pc[p]]
            for d in (0, 1):
                rows8 = pl.ds(pl.multiple_of(gs[d] * 8, 8), 8)
                _tiles_to_rows(dvt_s, d * NPAIR, outs[d][5], rows8)
                for p in range(NPAIR):
                    ds[d * NPAIR + p] = dss[d][p]
                    for o_ref, a_ in zip(outs[d][:5], acc[d][p]):
                        o_ref[rows8, pc[p]] = a_
            return carry

        for gj in range(ng):
            bgroup(gj, 0)

        if nsc:
            @pl.when(pl.program_id(0) == nc - 1)
            def _():
                s_finish()

    chunk = lambda k: pltpu.VMEM((k, NPAIR, HD, 128), F32)
    pairs = lambda w, dt: pltpu.VMEM((2 * NPAIR, HD, w), dt)
    res = pl.pallas_call(
        body, name="rwkv_scan_bwd", grid=(nc,), in_specs=in_specs, out_specs=out_specs, out_shape=out_shape,
        scratch_shapes=[chunk(2 * (TC + 1)), chunk(2 * TC), chunk(2 * TC), chunk(2 * TC), pairs(128, F32),
                        pairs(128, BF16), pairs(128, BF16), pairs(128, F32), pltpu.VMEM((2 * NPAIR, 128, 128), F32)]
        + _scatter_sems(nsc),
        compiler_params=pltpu.CompilerParams(dimension_semantics=("arbitrary",), vmem_limit_bytes=VMEM_LIMIT,
                                             has_side_effects=bool(nsc)),
    )(*operands)
    return [res[0:6], res[6:12]], list(res[12:])


MT = 512
MN = 256


def _merge_fwd(ya, yr, yx, wa, wr, wx, proj, gate_b):
    s = ya.shape[0]

    def body(ya_ref, yr_ref, yx_ref, wa_ref, wr_ref, wx_ref, m0, m1, m2, b0, b1, b2, o_ref):
        acc = jnp.zeros((MT, MN), F32)
        for y_ref, w_ref, m_ref, b_ref in ((ya_ref, wa_ref, m0, b0), (yr_ref, wr_ref, m1, b1), (yx_ref, wx_ref, m2, b2)):
            u = _dot(y_ref[...], w_ref[...], ((1,), (0,)))
            acc = acc + jax.nn.sigmoid(m_ref[...] + b_ref[...]) * u
        o_ref[...] = acc.astype(BF16)

    mg = lambda br: _bs((MT, MN), lambda i, j: (i, C_MG // MN + br * (D // MN) + j))
    gb = lambda br: _bs((1, MN), lambda i, j: (0, br * (D // MN) + j))
    return pl.pallas_call(
        body, name="merge_fwd", grid=(s // MT, D // MN),
        in_specs=[_bs((MT, RW), lambda i, j: (i, 0)), _bs((MT, RW), lambda i, j: (i, 0)), _bs((MT, XW), lambda i, j: (i, 0)),
                  _bs((RW, MN), lambda i, j: (0, j)), _bs((RW, MN), lambda i, j: (0, j)), _bs((XW, MN), lambda i, j: (0, j)),
                  mg(0), mg(1), mg(2), gb(0), gb(1), gb(2)],
        out_specs=_bs((MT, MN), lambda i, j: (i, j)),
        out_shape=jax.ShapeDtypeStruct((s, D), BF16),
        compiler_params=_params(("parallel", "arbitrary")),
    )(ya, yr, yx, wa, wr, wx, proj, proj, proj, gate_b, gate_b, gate_b)


def _out_fwd(merged, w_out, x, target):
    s = x.shape[0]
    tm, tn = min(512, s), 512

    def body(m_ref, w_ref, x_ref, t_ref, loss_ref, d_ref, d16_ref):
        @pl.when((pl.program_id(0) == 0) & (pl.program_id(1) == 0))
        def _():
            loss_ref[...] = jnp.zeros_like(loss_ref)

        out = x_ref[...] + jnp.dot(m_ref[...], w_ref[...], preferred_element_type=F32)
        err = out - t_ref[...]
        dout = err * (1.0 / D)
        d_ref[...] = dout
        d16_ref[...] = dout.astype(BF16)
        loss_ref[...] += jnp.sum(err * err)

    tile = _bs((tm, tn), lambda i, j: (i, j))
    return pl.pallas_call(
        body, name="out_fwd", grid=(s // tm, D // tn),
        in_specs=[_bs((tm, D), lambda i, j: (i, 0)), _bs((D, tn), lambda i, j: (0, j)), tile, tile],
        out_specs=[_bs((8, 128), lambda i, j: (0, 0)), tile, tile],
        out_shape=[jax.ShapeDtypeStruct((8, 128), F32), jax.ShapeDtypeStruct((s, D), F32),
                   jax.ShapeDtypeStruct((s, D), BF16)],
        compiler_params=_params(("arbitrary", "arbitrary")),
    )(merged, w_out, x, target)


def _merge_bwd(ya, yr, yx, wa, wr, wx, proj, gate_b, dmerged):
    s = ya.shape[0]

    def body(ya_ref, yr_ref, yx_ref, wa_ref, wr_ref, wx_ref, m0, m1, m2, b0, b1, b2, dm_ref,
             dg0, dg1, dg2, du0, du1, du2, dya_ref, dyr_ref, dyx_ref):
        @pl.when(pl.program_id(1) == 0)
        def _():
            dya_ref[...] = jnp.zeros_like(dya_ref)
            dyr_ref[...] = jnp.zeros_like(dyr_ref)
            dyx_ref[...] = jnp.zeros_like(dyx_ref)

        dm = dm_ref[...]
        branches = ((ya_ref, wa_ref, m0, b0, dg0, du0, dya_ref), (yr_ref, wr_ref, m1, b1, dg1, du1, dyr_ref),
                    (yx_ref, wx_ref, m2, b2, dg2, du2, dyx_ref))
        ws = [br[1][...] for br in branches]
        us = [_dot(br[0][...], w, ((1,), (0,))) for br, w in zip(branches, ws)]
        gts = [jax.nn.sigmoid(br[2][...] + br[3][...]) for br in branches]
        dus = [(dm * gt).astype(BF16) for gt in gts]
        for br, w, u, gt, du in zip(branches, ws, us, gts, dus):
            br[4][...] = (dm * u * gt * (1.0 - gt)).astype(BF16)
            br[5][...] = du
            br[6][...] += _dot(du, w, ((1,), (1,)))

    mg = lambda br: _bs((MT, MN), lambda i, j: (i, C_MG // MN + br * (D // MN) + j))
    gb = lambda br: _bs((1, MN), lambda i, j: (0, br * (D // MN) + j))
    tile = _bs((MT, MN), lambda i, j: (i, j))
    return pl.pallas_call(
        body, name="merge_bwd", grid=(s // MT, D // MN),
        in_specs=[_bs((MT, RW), lambda i, j: (i, 0)), _bs((MT, RW), lambda i, j: (i, 0)), _bs((MT, XW), lambda i, j: (i, 0)),
                  _bs((RW, MN), lambda i, j: (0, j)), _bs((RW, MN), lambda i, j: (0, j)), _bs((XW, MN), lambda i, j: (0, j)),
                  mg(0), mg(1), mg(2), gb(0), gb(1), gb(2), tile],
        out_specs=[tile] * 6 + [_bs((MT, RW), lambda i, j: (i, 0)), _bs((MT, RW), lambda i, j: (i, 0)),
                                _bs((MT, XW), lambda i, j: (i, 0))],
        out_shape=[jax.ShapeDtypeStruct((s, D), BF16)] * 6 + [jax.ShapeDtypeStruct((s, RW), F32),
                                                               jax.ShapeDtypeStruct((s, RW), F32),
                                                               jax.ShapeDtypeStruct((s, XW), F32)],
        compiler_params=_params(("parallel", "arbitrary")),
    )(ya, yr, yx, wa, wr, wx, proj, proj, proj, gate_b, gate_b, gate_b, dmerged)


def _colsum(a, name):
    m, n = a.shape
    tm, tn = min(512, m), 512

    def body(a_ref, o_ref):
        @pl.when(pl.program_id(1) == 0)
        def _():
            o_ref[...] = jnp.zeros_like(o_ref)

        o_ref[...] += jnp.sum(a_ref[...].astype(F32), axis=0, keepdims=True)

    return pl.pallas_call(
        body, name=name, grid=(n // tn, m // tm),
        in_specs=[_bs((tm, tn), lambda j, i: (i, j))], out_specs=_bs((1, tn), lambda j, i: (0, j)),
        out_shape=jax.ShapeDtypeStruct((1, n), F32),
        compiler_params=_params(("parallel", "arbitrary")),
    )(a)


def _in_bwd(dproj, w_in, x, g, dout, stacks=()):
    s = x.shape[0]
    tm, tk = min(512, s), 896
    nk = NIN // tk
    ni = s // tm
    n = len(stacks)

    def body(dp_ref, w_ref, x_ref, g_ref, do_ref, *rest):
        ins, (gx_ref, gg_ref), outs = rest[:n], rest[n:n + 2], rest[n + 2:2 * n + 2]
        acc = rest[2 * n + 2]
        i, kk = pl.program_id(0), pl.program_id(1)

        if n:
            start, finish = _scatter_phases(ins, outs, rest[2 * n + 3:])

        @pl.when((i == 0) & (kk == 0))
        def _():
            gg_ref[...] = jnp.zeros_like(gg_ref)
            if n:
                start()

        @pl.when(kk == 0)
        def _():
            acc[...] = jnp.zeros_like(acc)

        acc[...] += _dot(dp_ref[...], w_ref[...], ((1,), (1,)))

        @pl.when(kk == nk - 1)
        def _():
            xv, dh, gv = x_ref[...], acc[...], g_ref[...]
            r = lax.rsqrt(jnp.mean(xv * xv, axis=-1, keepdims=True) + NORM_EPS)
            xn = xv * r
            gg_ref[...] += jnp.sum(dh * xn, axis=0, keepdims=True)
            dxn = dh * gv
            dx = r * (dxn - xn * jnp.mean(dxn * xn, axis=-1, keepdims=True))
            gx_ref[...] = do_ref[...] + dx

        if n:
            @pl.when((i == ni - 1) & (kk == nk - 1))
            def _():
                finish()

    any_spec = pl.BlockSpec(memory_space=pl.ANY)
    res = pl.pallas_call(
        body, name="in_bwd", grid=(ni, nk),
        in_specs=[_bs((tm, tk), lambda i, kk: (i, kk)), _bs((D, tk), lambda i, kk: (0, kk)),
                  _bs((tm, D), lambda i, kk: (i, 0)), _bs((1, D), lambda i, kk: (0, 0)),
                  _bs((tm, D), lambda i, kk: (i, 0))] + [any_spec] * n,
        out_specs=[_bs((tm, D), lambda i, kk: (i, 0)), _bs((1, D), lambda i, kk: (0, 0))] + [any_spec] * n,
        out_shape=[jax.ShapeDtypeStruct((s, D), F32), jax.ShapeDtypeStruct((1, D), F32)] + _scatter_out_shapes(stacks),
        scratch_shapes=[pltpu.VMEM((tm, D), F32)] + _scatter_sems(n),
        compiler_params=pltpu.CompilerParams(dimension_semantics=("arbitrary", "arbitrary"),
                                             vmem_limit_bytes=VMEM_LIMIT, has_side_effects=bool(n)),
    )(dproj, w_in, x, g, dout, *stacks)
    return res[0], res[1], list(res[2:])


def _adamw_math(w, g, m, v):
    m = ADAM_B1 * m + (1.0 - ADAM_B1) * g
    v = ADAM_B2 * v + (1.0 - ADAM_B2) * jnp.square(g)
    m_hat = m / (1.0 - ADAM_B1 ** ADAM_STEP)
    v_hat = v / (1.0 - ADAM_B2 ** ADAM_STEP)
    delta = -ADAM_LR * (m_hat / (jnp.sqrt(v_hat) + ADAM_EPS) + ADAM_WD * w)
    return delta, m, v


def _adamw(parts, w, m, v, name):
    rows, cols = w.shape
    tr = rows
    for cand in (256, 128, 64, 32, 16, 8):
        if rows % cand == 0 and cand * cols * 4 <= (1 << 20):
            tr = cand
            break
    n = len(parts)

    def body(*refs):
        g = refs[0][...].astype(F32)
        for r in refs[1:n]:
            g = g + r[...].astype(F32)
        w_ref, m_ref, v_ref, g_out, d_out, m_out, v_out = refs[n:]
        delta, m_new, v_new = _adamw_math(w_ref[...], g, m_ref[...], v_ref[...])
        g_out[...] = g
        d_out[...] = delta
        m_out[...] = m_new
        v_out[...] = v_new

    spec = _bs((tr, cols), lambda i: (i, 0))
    return pl.pallas_call(
        body, name=name, grid=(rows // tr,),
        in_specs=[spec] * (n + 3), out_specs=[spec] * 4,
        out_shape=[jax.ShapeDtypeStruct((rows, cols), F32)] * 4,
        compiler_params=_params(("parallel",)),
    )(*parts, w, m, v)


def _adamw_halves(mine, theirs, core, w, m, v, name):
    rows, cols = w.shape
    h = rows // 2
    tr = next(t for t in (256, 128, 64, 32, 16, 8) if h % t == 0 and t * cols * 4 <= (1 << 20))
    nt = h // tr

    def body(core_ref, mine_ref, theirs_ref, w_ref, m_ref, v_ref, g_out, d_out, m_out, v_out):
        is_mine = pl.program_id(0) // nt == core_ref[0]
        g = jnp.where(is_mine, mine_ref[...], theirs_ref[...])
        delta, m_new, v_new = _adamw_math(w_ref[...], g, m_ref[...], v_ref[...])
        g_out[...] = g
        d_out[...] = delta
        m_out[...] = m_new
        v_out[...] = v_new

    spec = _bs((tr, cols), lambda i, core_ref: (i, 0))
    return pl.pallas_call(
        body, name=name,
        grid_spec=pltpu.PrefetchScalarGridSpec(
            num_scalar_prefetch=1, grid=(2 * nt,),
            in_specs=[_bs((tr, cols), lambda i, core_ref: (jnp.clip(i - core_ref[0] * nt, 0, nt - 1), 0)),
                      _bs((tr, cols), lambda i, core_ref: (jnp.clip(i - (1 - core_ref[0]) * nt, 0, nt - 1), 0)),
                      spec, spec, spec],
            out_specs=[spec] * 4),
        out_shape=[jax.ShapeDtypeStruct((rows, cols), F32)] * 4,
        compiler_params=_params(("parallel",)),
    )(core, mine, theirs, w, m, v)


def _sum_parts(parts, name):
    rows, cols = parts[0].shape
    tr = rows
    for cand in (256, 128, 64, 32, 16, 8):
        if rows % cand == 0 and cand * cols * 4 <= (1 << 20):
            tr = cand
            break

    def body(*refs):
        acc = refs[0][...].astype(F32)
        for r in refs[1:-1]:
            acc = acc + r[...].astype(F32)
        refs[-1][...] = acc

    spec = _bs((tr, cols), lambda i: (i, 0))
    return pl.pallas_call(
        body, name=name, grid=(rows // tr,), in_specs=[spec] * len(parts), out_specs=spec,
        out_shape=jax.ShapeDtypeStruct((rows, cols), F32), compiler_params=_params(("parallel",)),
    )(*parts)


ANY = pl.BlockSpec(memory_space=pl.ANY)


def _other_chips(x, y):
    return [(1 - x, y), (x, 1 - y), (1 - x, 1 - y)]


def _gather_shards(arrays, name):
    n = len(arrays)

    def body(*refs):
        start, forward, finish = _gather_phases(arrays, refs[:n], refs[n:2 * n], refs[2 * n:])
        start()
        forward()
        finish()

    return pl.pallas_call(
        body, name=name, in_specs=[ANY] * n, out_specs=[ANY] * n,
        out_shape=_gather_out_shapes(arrays), scratch_shapes=_gather_sems(n),
        compiler_params=pltpu.CompilerParams(has_side_effects=True),
    )(*arrays)


def _gather_out_shapes(arrays):
    return [jax.ShapeDtypeStruct((4,) + a.shape, a.dtype) for a in arrays]


def _gather_sems(n):
    dma = lambda k: pltpu.SemaphoreType.DMA((k,))
    return [dma(3 * n), dma(3 * n), dma(3 * n), dma(3 * n), dma(n), dma(n)]


def _gather_phases(arrays, ins, outs, sems):
    n = len(arrays)
    ici_send, ici_recv, d2d_send, d2d_recv, own_send, own_recv = sems

    def place():
        x, y, c = lax.axis_index("x"), lax.axis_index("y"), lax.axis_index("c")
        return x, y, c, 2 * x + y, _other_chips(x, y)

    def half(i, who):
        h = arrays[i].shape[0] // 2
        return pl.ds(who * h, h)

    def ici(i, j, src_chip, to, c):
        return pltpu.make_async_remote_copy(
            src_ref=ins[i].at[half(i, c)], dst_ref=outs[i].at[src_chip, half(i, c)], send_sem=ici_send.at[3 * i + j],
            recv_sem=ici_recv.at[3 * i + j], device_id=to, device_id_type=MESH)

    def d2d(i, j, src_chip, who, sib):
        piece = outs[i].at[src_chip, half(i, who)]
        return pltpu.make_async_remote_copy(
            src_ref=piece, dst_ref=piece, send_sem=d2d_send.at[3 * i + j], recv_sem=d2d_recv.at[3 * i + j],
            device_id=sib, device_id_type=MESH)

    def own(i, me, sib):
        return pltpu.make_async_remote_copy(
            src_ref=ins[i], dst_ref=outs[i].at[me], send_sem=own_send.at[i], recv_sem=own_recv.at[i],
            device_id=sib, device_id_type=MESH)

    def start():
        x, y, c, me, chips = place()
        for i in range(n):
            own(i, me, (x, y, 1 - c)).start()
            for j, (px, py) in enumerate(chips):
                ici(i, j, me, (px, py, c), c).start()

    def forward():
        x, y, c, me, chips = place()
        for i in range(n):
            for j, (px, py) in enumerate(chips):
                ici(i, j, 2 * px + py, (px, py, c), c).wait_recv()
                d2d(i, j, 2 * px + py, c, (x, y, 1 - c)).start()

    def finish():
        x, y, c, me, chips = place()
        sib = (x, y, 1 - c)
        for i in range(n):
            for j, (px, py) in enumerate(chips):
                d2d(i, j, 2 * px + py, 1 - c, sib).wait_recv()
            own(i, me, sib).wait_recv()
        for i in range(n):
            own(i, me, sib).wait_send()
            for j, (px, py) in enumerate(chips):
                ici(i, j, me, (px, py, c), c).wait_send()
                d2d(i, j, 2 * px + py, c, sib).wait_send()

    return start, forward, finish


def _scatter_phases(ins, outs, sems):
    send_sems, recv_sems = sems

    def copies():
        x, y, c = lax.axis_index("x"), lax.axis_index("y"), lax.axis_index("c")
        return [pltpu.make_async_remote_copy(
            src_ref=ins[a].at[2 * qx + qy], dst_ref=outs[a].at[j], send_sem=send_sems.at[3 * a + j],
            recv_sem=recv_sems.at[3 * a + j], device_id=(qx, qy, c), device_id_type=MESH)
            for a in range(len(ins)) for j, (qx, qy) in enumerate(_other_chips(x, y))]

    def start():
        for rc in copies():
            rc.start()

    def finish():
        for rc in copies():
            rc.wait_recv()
        for rc in copies():
            rc.wait_send()

    return start, finish


def _scatter_out_shapes(stacks):
    return [jax.ShapeDtypeStruct((3,) + a.shape[1:], a.dtype) for a in stacks]


def _scatter_sems(n):
    return [pltpu.SemaphoreType.DMA((3 * n,)), pltpu.SemaphoreType.DMA((3 * n,))] if n else []


def _scatter_shards(stacks, name):
    n = len(stacks)

    def body(*refs):
        ins, outs = refs[:n], refs[n:2 * n]
        send_sems, recv_sems = refs[2 * n:]
        x, y, c = lax.axis_index("x"), lax.axis_index("y"), lax.axis_index("c")
        chips = _other_chips(x, y)
        sends = []
        for i in range(n):
            for j, (px, py) in enumerate(chips):
                rc = pltpu.make_async_remote_copy(
                    src_ref=ins[i].at[2 * px + py], dst_ref=outs[i].at[j], send_sem=send_sems.at[3 * i + j],
                    recv_sem=recv_sems.at[3 * i + j], device_id=(px, py, c), device_id_type=MESH)
                rc.start()
                sends.append(rc)
        for rc in sends:
            rc.wait_recv()
        for rc in sends:
            rc.wait_send()

    return pl.pallas_call(
        body, name=name, in_specs=[ANY] * n, out_specs=[ANY] * n,
        out_shape=[jax.ShapeDtypeStruct((3,) + a.shape[1:], a.dtype) for a in stacks],
        scratch_shapes=[pltpu.SemaphoreType.DMA((3 * n,)), pltpu.SemaphoreType.DMA((3 * n,))],
        compiler_params=pltpu.CompilerParams(has_side_effects=True),
    )(*stacks)


def _pair_exchange(stacks, name):
    n = len(stacks)

    def body(*refs):
        ins, outs = refs[:n], refs[n:2 * n]
        send_sems, recv_sems = refs[2 * n:]
        x, y, c = lax.axis_index("x"), lax.axis_index("y"), lax.axis_index("c")
        cps = []
        for i in range(n):
            h = stacks[i].shape[1] // 2
            rc = pltpu.make_async_remote_copy(
                src_ref=ins[i].at[:, pl.ds((1 - c) * h, h)], dst_ref=outs[i], send_sem=send_sems.at[i],
                recv_sem=recv_sems.at[i], device_id=(x, y, 1 - c), device_id_type=MESH)
            rc.start()
            cps.append(rc)
        for rc in cps:
            rc.wait_recv()
        for rc in cps:
            rc.wait_send()

    return pl.pallas_call(
        body, name=name, in_specs=[ANY] * n, out_specs=[ANY] * n,
        out_shape=[jax.ShapeDtypeStruct((4, a.shape[1] // 2) + a.shape[2:], a.dtype) for a in stacks],
        scratch_shapes=[pltpu.SemaphoreType.DMA((n,)), pltpu.SemaphoreType.DMA((n,))],
        compiler_params=pltpu.CompilerParams(has_side_effects=True),
    )(*stacks)


def _pair_sum(own, theirs, core, name):
    _, r, cols = own.shape
    h = r // 2
    tr = next(t for t in (256, 128, 64, 32, 16) if h % t == 0 and t * cols * 4 <= (1 << 20))
    nt = h // tr

    def body(core_ref, own_ref, th_ref, o32_ref, o16_ref):
        del core_ref
        acc = own_ref[...] + th_ref[...].astype(F32)
        o32_ref[...] = acc
        o16_ref[...] = acc.astype(BF16)

    out = _bs((1, tr, cols), lambda j, t, core_ref: (j, t, 0))
    return pl.pallas_call(
        body, name=name,
        grid_spec=pltpu.PrefetchScalarGridSpec(
            num_scalar_prefetch=1, grid=(4, nt),
            in_specs=[_bs((1, tr, cols), lambda j, t, core_ref: (j, core_ref[0] * nt + t, 0)), out],
            out_specs=[out, out]),
        out_shape=[jax.ShapeDtypeStruct((4, h, cols), F32), jax.ShapeDtypeStruct((4, h, cols), BF16)],
        compiler_params=_params(("parallel", "parallel")),
    )(core, own, theirs)


def _swap_sibling(arrays, name):
    n = len(arrays)

    def body(*refs):
        ins, outs = refs[:n], refs[n:2 * n]
        send_sems, recv_sems = refs[2 * n:]
        sib = (lax.axis_index("x"), lax.axis_index("y"), 1 - lax.axis_index("c"))
        cps = []
        for i in range(n):
            rc = pltpu.make_async_remote_copy(src_ref=ins[i], dst_ref=outs[i], send_sem=send_sems.at[i],
                                              recv_sem=recv_sems.at[i], device_id=sib, device_id_type=MESH)
            rc.start()
            cps.append(rc)
        for rc in cps:
            rc.wait_recv()
        for rc in cps:
            rc.wait_send()

    return pl.pallas_call(
        body, name=name, in_specs=[ANY] * n, out_specs=[ANY] * n,
        out_shape=[jax.ShapeDtypeStruct(a.shape, a.dtype) for a in arrays],
        scratch_shapes=[pltpu.SemaphoreType.DMA((n,)), pltpu.SemaphoreType.DMA((n,))],
        compiler_params=pltpu.CompilerParams(has_side_effects=True),
    )(*arrays)


def _all_reduce_small(v):
    rows = v.shape[0]

    def body(v_ref, o_ref, buf, send_sems, recv_sems):
        x, y, c = lax.axis_index("x"), lax.axis_index("y"), lax.axis_index("c")
        me = 4 * x + 2 * y + c
        buf[me] = v_ref[...]
        cps = []
        for kbits in range(1, 8):
            bx, by, bc = (kbits >> 2) & 1, (kbits >> 1) & 1, kbits & 1
            px = jnp.where(bx == 1, 1 - x, x)
            py = jnp.where(by == 1, 1 - y, y)
            pc = jnp.where(bc == 1, 1 - c, c)
            rc = pltpu.make_async_remote_copy(src_ref=v_ref, dst_ref=buf.at[me], send_sem=send_sems.at[kbits - 1],
                                              recv_sem=recv_sems.at[kbits - 1], device_id=(px, py, pc),
                                              device_id_type=MESH)
            rc.start()
            cps.append((rc, 4 * px + 2 * py + pc))
        for kbits, (rc, src) in enumerate(cps):
            pltpu.make_async_remote_copy(src_ref=v_ref, dst_ref=buf.at[src], send_sem=send_sems.at[kbits],
                                         recv_sem=recv_sems.at[kbits], device_id=(x, y, c),
                                         device_id_type=MESH).wait_recv()
        for rc, _ in cps:
            rc.wait_send()
        acc = buf[0]
        for d in range(1, 8):
            acc = acc + buf[d]
        o_ref[...] = acc

    return pl.pallas_call(
        body, name="all_reduce_small",
        in_specs=[pl.BlockSpec(memory_space=pltpu.VMEM)], out_specs=pl.BlockSpec(memory_space=pltpu.VMEM),
        out_shape=jax.ShapeDtypeStruct((rows, 128), F32),
        scratch_shapes=[pltpu.VMEM((8, rows, 128), F32), pltpu.SemaphoreType.DMA((7,)), pltpu.SemaphoreType.DMA((7,))],
        compiler_params=pltpu.CompilerParams(has_side_effects=True, vmem_limit_bytes=VMEM_LIMIT),
    )(v)


def _rope_tables(s):
    half = HD // 2
    inv = 10000.0 ** (-jnp.arange(half, dtype=F32) / half)
    ang = jnp.arange(s, dtype=F32)[:, None] * inv[None, :]
    cos, sin = jnp.cos(ang), jnp.sin(ang)
    return jnp.concatenate([cos, cos], axis=1), jnp.concatenate([sin, sin], axis=1)


LATE = ['attn_w_o', 'rwkv_w_o', 'x_w_kv', 'x_w_o', 'w_out']


def _local_step(x, mem, target, norm_g, mem_norm_g, w_in, gate_b, gq, gk, sink, wa, mu, k_k, k_a, r_k, w0, w2, a0, a2,
                ln_w, ln_b, wr, w_kv, gxq, gxk, wx, w_out, late_shards=None, early_exchange=None):
    s = x.shape[0]
    cos, sin = _rope_tables(s)
    r_k = r_k.reshape(1, RW)

    proj, h = _proj_fwd(x, norm_g, w_in)
    ya = _attn_fwd(proj, cos, sin, gq, gk, sink)
    ps = _shift_fwd(proj, mu)
    kk, dec0, kd0, b0, dec1, kd1, b1 = _pre_fwd(ps, k_k, k_a, w0, w2, a0, a2)
    ((y0, ck0), (y1, ck1)), stacks = _scan2_fwd([(dec0, kd0, b0), (dec1, kd1, b1)], ps, kk, gather=late_shards or ())
    if late_shards:
        st = dict(zip(LATE, stacks))
        wa, wr, wx = (_unshard_cols(st[n]) for n in ('attn_w_o', 'rwkv_w_o', 'x_w_o'))
        w_kv, w_out = st['x_w_kv'].reshape(D, 2 * XW), st['w_out'].reshape(D, D)
    mkv, mn = _mem_kv(mem, mem_norm_g, w_kv)
    yx = _xattn_fwd(proj, mkv, gxq, gxk)
    yr = _post_fwd(y0, y1, ps, kd0, kd1, proj, r_k, ln_w, ln_b)
    merged = _merge_fwd(ya, yr, yx, wa, wr, wx, proj, gate_b)
    loss_tile, dout, dout16 = _out_fwd(merged, w_out, x, target)
    loss_sum = loss_tile[0, 0]

    g = {}
    t16 = lambda a: a.astype(BF16).T
    sk = min(1024, s)
    dmerged = _matmul(dout16, w_out, mode="nt", m=s, n=D, k=D, tm=sk, tn=1024, tk=1024, name="dmerged")
    g["w_out"] = _matmul(merged.T, dout16, mode="nn", m=D, n=D, k=s, tm=1024, tn=1024, tk=sk, name="grad_w_out")
    dg0, dg1, dg2, du0, du1, du2, dya, dyr, dyx = _merge_bwd(ya, yr, yx, wa, wr, wx, proj, gate_b, dmerged)
    g["attn_w_o"] = _matmul(t16(ya), du0, mode="nn", m=RW, n=D, k=s, tm=RW, tn=1024, tk=s, name="grad_attn_w_o")
    g["rwkv_w_o"] = _matmul(t16(yr), du1, mode="nn", m=RW, n=D, k=s, tm=RW, tn=1024, tk=s, name="grad_rwkv_w_o")
    g["x_w_o"] = _matmul(t16(yx), du2, mode="nn", m=XW, n=D, k=s, tm=XW, tn=1024, tk=s, name="grad_x_w_o")
    dmg = jnp.concatenate([dg0, dg1, dg2], axis=1)
    g["gate_b"] = _colsum(dmg, "grad_gate_b")

    daq, dak, dav, dag, g["attn_q_norm_g"], g["attn_k_norm_g"], g["attn_sink"] = _attn_bwd(proj, cos, sin, gq, gk, sink, dya)

    dxq, dxg, dmkv, g["x_q_norm_g"], g["x_k_norm_g"] = _xattn_bwd(proj, mkv, gxq, gxk, dyx)
    g["x_w_kv"] = _matmul(mn, dmkv, mode="tn", m=D, n=2 * XW, k=NMEM, tm=512, tn=512, tk=NMEM, name="grad_x_w_kv")
    dmn = _matmul(dmkv, w_kv, mode="nt", m=NMEM, n=D, k=2 * XW, tm=NMEM, tn=512, tk=2 * XW, name="dmn")
    g["mem_norm_g"] = _mem_bwd(mem, mem_norm_g, dmn)

    dys, dr_p, dv_p, dkd0_p, dkd1_p, drg, g["rwkv_r_k"], g["rwkv_ln_w"], g["rwkv_ln_b"] = _post_bwd(
        y0, y1, ps, kd0, kd1, proj, r_k, ln_w, ln_b, dyr)
    sent = early_exchange(g) if early_exchange else ()
    ((dr0, dd0, db0, dk0, dkk0, dv0), (dr1, dd1, db1, dk1, dkk1, dv1)), received = _scan2_bwd(
        [(dec0, kd0, b0, ck0), (dec1, kd1, b1, ck1)], ps, kk, dys, scatter=sent)
    dr = dr_p + dr0 + dr1
    dv = dv_p + dv0 + dv1
    cts = (dkk0 + dkk1, dd0, dk0 + dkd0_p, db0, dd1, dk1 + dkd1_p, db1)
    dps, g["rwkv_k_k"], g["rwkv_k_a"], g["rwkv_w0"], g["rwkv_w2"], g["rwkv_a0"], g["rwkv_a2"] = _pre_bwd(
        ps, k_k, k_a, w0, w2, a0, a2, dr, dv, cts)
    drs, g["rwkv_mu"] = _shift_bwd(proj, mu, dps)

    dproj = jnp.concatenate([daq.astype(BF16), dak.astype(BF16), dav.astype(BF16), dag.astype(BF16), drs.astype(BF16),
                             drg.astype(BF16), dxq.astype(BF16), dxg.astype(BF16), dmg], axis=1)
    dproj4 = jnp.stack([dproj[:, j * (NIN // 4):(j + 1) * (NIN // 4)] for j in range(4)])
    g["w_in"], g["w_in_bf16"] = _grad_w_in(h.T, dproj4)
    g["rwkv_r_k"] = g["rwkv_r_k"].reshape(AH, HD)
    return loss_sum, g, (dproj, w_in, x, norm_g, dout), received


WEIGHTS = ['norm_g', 'mem_norm_g', 'w_in', 'gate_b', 'attn_q_norm_g', 'attn_k_norm_g', 'attn_sink', 'attn_w_o',
           'rwkv_mu', 'rwkv_k_k', 'rwkv_k_a', 'rwkv_r_k', 'rwkv_w0', 'rwkv_w2', 'rwkv_a0', 'rwkv_a2', 'rwkv_ln_w',
           'rwkv_ln_b', 'rwkv_w_o', 'x_w_kv', 'x_q_norm_g', 'x_k_norm_g', 'x_w_o', 'w_out']
BIG = ['w_in', 'attn_w_o', 'rwkv_w_o', 'x_w_kv', 'x_w_o', 'w_out']
COL_SHARDED = ['w_in', 'attn_w_o', 'rwkv_w_o', 'x_w_o']
LORA = ['rwkv_w0', 'rwkv_w2', 'rwkv_a0', 'rwkv_a2']
SMALL = [n for n in WEIGHTS if n not in BIG]


def _unshard_cols(stack):
    return jnp.concatenate([stack[i] for i in range(4)], axis=-1)


def _shard_cols(full):
    w = full.shape[-1] // 4
    return [full[..., i * w:(i + 1) * w] for i in range(4)]


def kernel(x, mem, norm_g, mem_norm_g, w_in, gate_b, attn_q_norm_g, attn_k_norm_g, attn_sink, attn_w_o, rwkv_mu, rwkv_k_k, rwkv_k_a, rwkv_r_k, rwkv_w0, rwkv_w2, rwkv_a0, rwkv_a2, rwkv_ln_w, rwkv_ln_b, rwkv_w_o, x_w_kv, x_q_norm_g, x_k_norm_g, x_w_o, w_out, loss_target, m_norm_g, m_mem_norm_g, m_w_in, m_gate_b, m_attn_q_norm_g, m_attn_k_norm_g, m_attn_sink, m_attn_w_o, m_rwkv_mu, m_rwkv_k_k, m_rwkv_k_a, m_rwkv_r_k, m_rwkv_w0, m_rwkv_w2, m_rwkv_a0, m_rwkv_a2, m_rwkv_ln_w, m_rwkv_ln_b, m_rwkv_w_o, m_x_w_kv, m_x_q_norm_g, m_x_k_norm_g, m_x_w_o, m_w_out, v_norm_g, v_mem_norm_g, v_w_in, v_gate_b, v_attn_q_norm_g, v_attn_k_norm_g, v_attn_sink, v_attn_w_o, v_rwkv_mu, v_rwkv_k_k, v_rwkv_k_a, v_rwkv_r_k, v_rwkv_w0, v_rwkv_w2, v_rwkv_a0, v_rwkv_a2, v_rwkv_ln_w, v_rwkv_ln_b, v_rwkv_w_o, v_x_w_kv, v_x_q_norm_g, v_x_k_norm_g, v_x_w_o, v_w_out):
    args = dict(locals())
    canon = lambda a: a[0] if a.ndim > 2 else a
    w = {n: canon(args[n]) for n in WEIGHTS}
    m = {n: canon(args["m_" + n]) for n in WEIGHTS}
    v = {n: canon(args["v_" + n]) for n in WEIGHTS}
    shard = 2 * lax.axis_index("x") + lax.axis_index("y")

    now = ["w_in"] + LORA
    local = [w["w_in"].astype(BF16)] + [w[n].reshape(2, -1, w[n].shape[-1]) for n in LORA]
    stacks = dict(zip(now, _gather_shards(local, "gather_weights")))
    full = {"w_in": _unshard_cols(stacks["w_in"])}
    for n in LORA:
        full[n] = _unshard_cols(stacks[n]).reshape(w[n].shape[:-1] + (RW,))

    core = lax.axis_index("c").astype(jnp.int32).reshape(1)
    pair32 = {}

    def as_stack(g, n, dtype):
        if n == "w_in":
            return g["w_in"] if dtype == F32 else g["w_in_bf16"]
        if n in COL_SHARDED:
            return jnp.stack([p.astype(dtype) for p in _shard_cols(g[n])])
        return g[n].reshape((4, g[n].shape[0] // 4) + g[n].shape[1:]).astype(dtype)

    def pair_sums(g, names, tag):
        sibling = _pair_exchange([as_stack(g, n, BF16) for n in names], "pair_exchange_" + tag)
        sent = []
        for n, th in zip(names, sibling):
            pair32[n], a16 = _pair_sum(as_stack(g, n, F32), th, core, "pair_sum_" + n)
            sent.append(a16)
        return sent

    loss_sum, g, deferred, recv_late = _local_step(
        x[0], mem[0], loss_target[0], w["norm_g"], w["mem_norm_g"], full["w_in"], w["gate_b"], w["attn_q_norm_g"],
        w["attn_k_norm_g"], w["attn_sink"], None, w["rwkv_mu"], w["rwkv_k_k"], w["rwkv_k_a"], w["rwkv_r_k"],
        full["rwkv_w0"], full["rwkv_w2"], full["rwkv_a0"], full["rwkv_a2"], w["rwkv_ln_w"], w["rwkv_ln_b"],
        None, None, w["x_q_norm_g"], w["x_k_norm_g"], None, None,
        late_shards=[w[n].astype(BF16) for n in LATE], early_exchange=lambda g: pair_sums(g, LATE, "late"))

    loss = lax.psum(0.5 * loss_sum / D, ("x", "y", "c"))

    grad_x, g["norm_g"], recv_w_in = _in_bwd(*deferred, stacks=pair_sums(g, ["w_in"], "w_in"))
    halves = []
    for n, r in zip(BIG, recv_w_in + recv_late):
        own = lax.dynamic_index_in_dim(pair32[n], shard, 0, keepdims=False)
        halves.append(_sum_parts([own, r[0], r[1], r[2]], "sum_" + n))
    other_halves = _swap_sibling(halves, "swap_halves")

    out_g, out_d, out_m, out_v = {}, {}, {}, {}
    for n, mine, theirs in zip(BIG, halves, other_halves):
        out_g[n], out_d[n], out_m[n], out_v[n] = _adamw_halves(mine, theirs, core, w[n], m[n], v[n], "adamw_" + n)

    flat = jnp.concatenate([g[n].reshape(-1) for n in SMALL])
    total = flat.shape[0]
    padded = -(-total // 1024) * 1024
    flat = jnp.pad(flat, (0, padded - total)).reshape(padded // 128, 128)
    red = _all_reduce_small(flat).reshape(-1)
    off = 0
    gs = {}
    for n in SMALL:
        size = g[n].size
        t = red[off:off + size].reshape(g[n].shape)
        off += size
        if n in LORA:
            wd = t.shape[-1] // 4
            t = lax.dynamic_slice_in_dim(t, shard * wd, wd, axis=t.ndim - 1)
        gs[n] = t

    def pack(d):
        f = jnp.concatenate([d[n].reshape(-1) for n in SMALL])
        return jnp.pad(f, (0, -(-f.shape[0] // 1024) * 1024 - f.shape[0])).reshape(-1, 128)

    pg, pd, pm, pv = _adamw([pack(gs)], pack(w), pack(m), pack(v), "adamw_small")
    off = 0
    for n in SMALL:
        size = w[n].size
        for dst, src in ((out_g, pg), (out_d, pd), (out_m, pm), (out_v, pv)):
            dst[n] = src.reshape(-1)[off:off + size].reshape(w[n].shape)
        off += size

    lead = lambda d: [d[n][None] if args[n].ndim > 2 else d[n] for n in WEIGHTS]
    return (loss, grad_x[None], *lead(out_g), *lead(out_d), *lead(out_m), *lead(out_v))
```

```python
import functools

import jax
import jax.numpy as jnp
from jax import lax
from jax.experimental import pallas as pl
from jax.experimental.pallas import tpu as pltpu

F32 = jnp.float32
BF16 = jnp.bfloat16
HI = lax.Precision.HIGH
MESH = pl.DeviceIdType.MESH

D = 2048
NMEM = 256
NORM_EPS = 1e-6
NEG_INF = -1e30
GN_EPS = 64e-5
HD = 64
AH = 12
AKV = 4
RW = 768
XH = 4
XD = 128
XW = 512
NIN = 12544
RSW = 2560
C_AQ, C_AK, C_AV, C_AG, C_RS, C_RG, C_XQ, C_XG, C_MG = 0, 768, 1024, 1280, 2048, 4608, 5376, 5888, 6400
WIN = 384
QB = 128
TC = 16
NPAIR = 6

ADAM_LR, ADAM_B1, ADAM_B2, ADAM_EPS, ADAM_WD, ADAM_STEP = 0.001, 0.9, 0.999, 1e-08, 0.01, 10

VMEM_LIMIT = 56 * 1024 * 1024


def _bs(shape, imap):
    return pl.BlockSpec(shape, imap)


def _params(sem=None, vmem=VMEM_LIMIT):
    return pltpu.CompilerParams(dimension_semantics=sem, vmem_limit_bytes=vmem)


def _dot(a, b, dims):
    return lax.dot_general(a.astype(BF16), b.astype(BF16), (dims, ((), ())), preferred_element_type=F32)


@jax.custom_vjp
def _mm_nn(a, b):
    return _dot(a, b, ((1,), (0,)))


def _mm_nn_fwd(a, b):
    return _mm_nn(a, b), (a, b)


def _mm_nn_bwd(res, ct):
    a, b = res
    return _dot(ct, b, ((1,), (1,))), _dot(a, ct, ((0,), (0,)))


_mm_nn.defvjp(_mm_nn_fwd, _mm_nn_bwd)


@jax.custom_vjp
def _mm_nt(a, b):
    return _dot(a, b, ((1,), (1,)))


def _mm_nt_fwd(a, b):
    return _mm_nt(a, b), (a, b)


def _mm_nt_bwd(res, ct):
    a, b = res
    return _dot(ct, b, ((1,), (0,))), _dot(ct, a, ((0,), (0,)))


_mm_nt.defvjp(_mm_nt_fwd, _mm_nt_bwd)


def _seg_matrix(n, seg):
    r = lax.broadcasted_iota(jnp.int32, (n, n), 0) // seg
    c = lax.broadcasted_iota(jnp.int32, (n, n), 1) // seg
    return (r == c).astype(F32)


def _rot_matrix():
    r = lax.broadcasted_iota(jnp.int32, (HD, HD), 0)
    c = lax.broadcasted_iota(jnp.int32, (HD, HD), 1)
    return jnp.where(c == r + HD // 2, 1.0, 0.0).astype(F32) - jnp.where(c == r - HD // 2, 1.0, 0.0).astype(F32)


def _hdot(a, m):
    return jnp.dot(a, m, precision=HI, preferred_element_type=F32)


def _rms(t, g):
    return t * lax.rsqrt(jnp.mean(t * t, axis=-1, keepdims=True) + NORM_EPS) * g


def _silu(t):
    return t * jax.nn.sigmoid(t)


def _softplus(z):
    return jnp.maximum(z, 0.0) + jnp.log(1.0 + jnp.exp(-jnp.abs(z)))


def _matmul(a, b, *, mode, m, n, k, tm, tn, tk, name, a_off=(0, 0), b_off=(0, 0), out_dtype=F32):
    nk = k // tk
    if mode == "tn":
        a_spec = _bs((tk, tm), lambda i, j, kk: (kk + a_off[0], i + a_off[1]))
        dims = ((0,), (0,))
    else:
        a_spec = _bs((tm, tk), lambda i, j, kk: (i + a_off[0], kk + a_off[1]))
        dims = ((1,), (1,)) if mode == "nt" else ((1,), (0,))
    if mode == "nt":
        b_spec = _bs((tn, tk), lambda i, j, kk: (j + b_off[0], kk + b_off[1]))
    else:
        b_spec = _bs((tk, tn), lambda i, j, kk: (kk + b_off[0], j + b_off[1]))

    def body(a_ref, b_ref, o_ref, acc):
        kk = pl.program_id(2)

        @pl.when(kk == 0)
        def _():
            acc[...] = jnp.zeros_like(acc)

        acc[...] += _dot(a_ref[...], b_ref[...], dims)

        @pl.when(kk == nk - 1)
        def _():
            o_ref[...] = acc[...].astype(out_dtype)

    return pl.pallas_call(
        body, name=name, grid=(m // tm, n // tn, nk),
        in_specs=[a_spec, b_spec], out_specs=_bs((tm, tn), lambda i, j, kk: (i, j)),
        out_shape=jax.ShapeDtypeStruct((m, n), out_dtype),
        scratch_shapes=[pltpu.VMEM((tm, tn), F32)],
        compiler_params=_params(("parallel", "parallel", "arbitrary")),
    )(a, b)


def _grad_w_in(ht, dproj4):
    s = ht.shape[1]
    ws = NIN // 4
    tm, tk = 256, s
    nk = s // tk

    def body(a_ref, b_ref, o32_ref, o16_ref, acc):
        kk = pl.program_id(2)

        @pl.when(kk == 0)
        def _():
            acc[...] = jnp.zeros_like(acc)

        acc[...] += jnp.dot(a_ref[...], b_ref[0], preferred_element_type=F32)

        @pl.when(kk == nk - 1)
        def _():
            o32_ref[0] = acc[...]
            o16_ref[0] = acc[...].astype(BF16)

    out = _bs((1, tm, ws), lambda j, i, kk: (j, i, 0))
    return pl.pallas_call(
        body, name="grad_w_in", grid=(4, D // tm, nk),
        in_specs=[_bs((tm, tk), lambda j, i, kk: (i, kk)), _bs((1, tk, ws), lambda j, i, kk: (j, kk, 0))],
        out_specs=[out, out],
        out_shape=[jax.ShapeDtypeStruct((4, D, ws), F32), jax.ShapeDtypeStruct((4, D, ws), BF16)],
        scratch_shapes=[pltpu.VMEM((tm, ws), F32)],
        compiler_params=_params(("parallel", "parallel", "arbitrary")),
    )(ht, dproj4)


def _proj_fwd(x, g, w):
    s = x.shape[0]
    tm, tn = min(1024, s), 896

    def body(x_ref, g_ref, w_ref, o_ref, h_ref, hs):
        @pl.when(pl.program_id(1) == 0)
        def _():
            h = _rms(x_ref[...], g_ref[...]).astype(BF16)
            hs[...] = h
            h_ref[...] = h

        o_ref[...] = jnp.dot(hs[...], w_ref[...], preferred_element_type=F32)

    return pl.pallas_call(
        body, name="proj_fwd", grid=(s // tm, NIN // tn),
        in_specs=[_bs((tm, D), lambda i, j: (i, 0)), _bs((1, D), lambda i, j: (0, 0)), _bs((D, tn), lambda i, j: (0, j))],
        out_specs=[_bs((tm, tn), lambda i, j: (i, j)), _bs((tm, D), lambda i, j: (i, 0))],
        out_shape=[jax.ShapeDtypeStruct((s, NIN), F32), jax.ShapeDtypeStruct((s, D), BF16)],
        scratch_shapes=[pltpu.VMEM((tm, D), BF16)],
        compiler_params=_params(("parallel", "arbitrary")),
    )(x, g, w)


def _rope(t, cos, sin, rot):
    return t * cos + _hdot(t, rot) * sin


def _attn_tile(qs, ks, vs, gs, sinks, gq, gk, cq, sq, ck, sk, mask, rot):
    heads = range(AH)
    kv = [h // (AH // AKV) for h in heads]
    kh = [_rope(_rms(ks[j], gk), ck, sk, rot) for j in range(AKV)]
    qh = [_rope(_rms(qs[h], gq), cq, sq, rot) for h in heads]
    sc = [jnp.where(mask, _mm_nt(qh[h], kh[kv[h]]) * (HD ** -0.5), NEG_INF) for h in heads]
    mx = [lax.stop_gradient(jnp.maximum(jnp.max(sc[h], axis=-1, keepdims=True), sinks[h])) for h in heads]
    p = [jnp.exp(sc[h] - mx[h]) for h in heads]
    den = [jnp.sum(p[h], axis=-1, keepdims=True) + jnp.exp(sinks[h] - mx[h]) for h in heads]
    o = [_mm_nn(p[h] / den[h], vs[kv[h]]) for h in heads]
    return [o[h] * _silu(gs[h]) for h in heads]


def _attn_load(n, s, aq_ref, ak_ref, av_ref, ag_refs, cos_ref, sin_ref, sink_ref):
    start = pl.multiple_of(jnp.clip((n - 1) * QB, 0, s - WIN), QB)
    q0 = pl.multiple_of(n * QB, QB)
    qs = [aq_ref[:, h * HD:(h + 1) * HD] for h in range(AH)]
    ks = [ak_ref[pl.ds(start, WIN), h * HD:(h + 1) * HD] for h in range(AKV)]
    vs = [av_ref[pl.ds(start, WIN), h * HD:(h + 1) * HD] for h in range(AKV)]
    gs = [ag_refs[h // 4][:, (h % 4) * HD:(h % 4 + 1) * HD] for h in range(AH)]
    sinks = [sink_ref[0:1, h:h + 1] for h in range(AH)]
    cq, sq = cos_ref[pl.ds(q0, QB), :], sin_ref[pl.ds(q0, QB), :]
    ck, sk = cos_ref[pl.ds(start, WIN), :], sin_ref[pl.ds(start, WIN), :]
    qpos = q0 + lax.broadcasted_iota(jnp.int32, (QB, WIN), 0)
    kpos = start + lax.broadcasted_iota(jnp.int32, (QB, WIN), 1)
    mask = jnp.abs(kpos - qpos) <= QB
    return start, qs, ks, vs, gs, sinks, cq, sq, ck, sk, mask


def _attn_specs(s):
    return [
        _bs((QB, 768), lambda n: (n, 0)),
        _bs((s, 256), lambda n: (0, C_AK // 256)),
        _bs((s, 256), lambda n: (0, C_AV // 256)),
        _bs((QB, 256), lambda n: (n, C_AG // 256)),
        _bs((QB, 256), lambda n: (n, C_AG // 256 + 1)),
        _bs((QB, 256), lambda n: (n, C_AG // 256 + 2)),
        _bs((s, HD), lambda n: (0, 0)),
        _bs((s, HD), lambda n: (0, 0)),
        _bs((1, HD), lambda n: (0, 0)),
        _bs((1, HD), lambda n: (0, 0)),
        _bs((1, AH), lambda n: (0, 0)),
    ]


def _attn_fwd(proj, cos, sin, gq, gk, sink):
    s = proj.shape[0]

    def body(aq_ref, ak_ref, av_ref, ag0, ag1, ag2, cos_ref, sin_ref, gq_ref, gk_ref, sink_ref, o_ref):
        n = pl.program_id(0)
        _, qs, ks, vs, gs, sinks, cq, sq, ck, sk, mask = _attn_load(
            n, s, aq_ref, ak_ref, av_ref, (ag0, ag1, ag2), cos_ref, sin_ref, sink_ref)
        outs = _attn_tile(qs, ks, vs, gs, sinks, gq_ref[...], gk_ref[...], cq, sq, ck, sk, mask, _rot_matrix())
        for h in range(AH):
            o_ref[:, h * HD:(h + 1) * HD] = outs[h]

    return pl.pallas_call(
        body, name="attn_fwd", grid=(s // QB,),
        in_specs=_attn_specs(s), out_specs=_bs((QB, 768), lambda n: (n, 0)),
        out_shape=jax.ShapeDtypeStruct((s, 768), F32),
        compiler_params=_params(("arbitrary",)),
    )(proj, proj, proj, proj, proj, proj, cos, sin, gq, gk, sink)


def _attn_bwd(proj, cos, sin, gq, gk, sink, dy):
    s = proj.shape[0]

    def body(aq_ref, ak_ref, av_ref, ag0, ag1, ag2, cos_ref, sin_ref, gq_ref, gk_ref, sink_ref, dy_ref,
             daq_ref, dak_ref, dav_ref, dag_ref, dgq_ref, dgk_ref, dsink_ref):
        n = pl.program_id(0)

        @pl.when(n == 0)
        def _():
            dak_ref[...] = jnp.zeros_like(dak_ref)
            dav_ref[...] = jnp.zeros_like(dav_ref)
            dgq_ref[...] = jnp.zeros_like(dgq_ref)
            dgk_ref[...] = jnp.zeros_like(dgk_ref)
            dsink_ref[...] = jnp.zeros_like(dsink_ref)

        start, qs, ks, vs, gs, sinks, cq, sq, ck, sk, mask = _attn_load(
            n, s, aq_ref, ak_ref, av_ref, (ag0, ag1, ag2), cos_ref, sin_ref, sink_ref)
        rot = _rot_matrix()

        def f(qs, ks, vs, gs, sinks, gq, gk):
            return _attn_tile(qs, ks, vs, gs, sinks, gq, gk, cq, sq, ck, sk, mask, rot)

        _, vjp = jax.vjp(f, qs, ks, vs, gs, sinks, gq_ref[...], gk_ref[...])
        dys = [dy_ref[:, h * HD:(h + 1) * HD] for h in range(AH)]
        dqs, dks, dvs, dgs, dsinks, dgq, dgk = vjp(dys)
        for h in range(AH):
            daq_ref[:, h * HD:(h + 1) * HD] = dqs[h]
            dag_ref[:, h * HD:(h + 1) * HD] = dgs[h]
            dsink_ref[0:1, h:h + 1] += dsinks[h]
        for h in range(AKV):
            dak_ref[pl.ds(start, WIN), h * HD:(h + 1) * HD] += dks[h]
            dav_ref[pl.ds(start, WIN), h * HD:(h + 1) * HD] += dvs[h]
        dgq_ref[...] += dgq
        dgk_ref[...] += dgk

    whole = lambda shape: _bs(shape, lambda n: (0, 0))
    return pl.pallas_call(
        body, name="attn_bwd", grid=(s // QB,),
        in_specs=_attn_specs(s) + [_bs((QB, 768), lambda n: (n, 0))],
        out_specs=[_bs((QB, 768), lambda n: (n, 0)), whole((s, 256)), whole((s, 256)), _bs((QB, 768), lambda n: (n, 0)),
                   whole((1, HD)), whole((1, HD)), whole((1, AH))],
        out_shape=[jax.ShapeDtypeStruct((s, 768), F32), jax.ShapeDtypeStruct((s, 256), F32),
                   jax.ShapeDtypeStruct((s, 256), F32), jax.ShapeDtypeStruct((s, 768), F32),
                   jax.ShapeDtypeStruct((1, HD), F32), jax.ShapeDtypeStruct((1, HD), F32),
                   jax.ShapeDtypeStruct((1, AH), F32)],
        compiler_params=_params(("arbitrary",)),
    )(proj, proj, proj, proj, proj, proj, cos, sin, gq, gk, sink, dy)


@jax.custom_vjp
def _rot_half(t):
    n = t.shape[1]
    lane = lax.broadcasted_iota(jnp.int32, t.shape, 1) % HD
    return jnp.where(lane < HD // 2, -pltpu.roll(t, n - HD // 2, 1), pltpu.roll(t, HD // 2, 1))


_rot_half.defvjp(lambda t: (_rot_half(t), None), lambda _, ct: (-_rot_half(ct),))


def _head_rms(t, g, seg):
    return t * lax.rsqrt(_hdot(t * t, seg) * (1.0 / HD) + NORM_EPS) * g


def _wattn_tile(aq, ak, av, ag, sinks, gq, gk, cq, sq, ck, sk, mask, segq, segk, spread, head_of_lane):
    q = _head_rms(aq, gq, segq)
    q = q * cq + _rot_half(q) * sq
    k = _head_rms(ak, gk, segk)
    k = k * ck + _rot_half(k) * sk
    kx = _hdot(k, spread)
    vx = _hdot(av, spread)
    o = jnp.zeros_like(aq)
    for h in range(AH):
        mine = head_of_lane == h
        sc = _mm_nt(jnp.where(mine, q, 0.0), kx) * (HD ** -0.5)
        sc = jnp.where(mask, sc, NEG_INF)
        mx = lax.stop_gradient(jnp.maximum(jnp.max(sc, axis=-1, keepdims=True), sinks[h]))
        p = jnp.exp(sc - mx)
        den = jnp.sum(p, axis=-1, keepdims=True) + jnp.exp(sinks[h] - mx)
        o = o + _mm_nn(p / den, jnp.where(mine, vx, 0.0))
    return o * _silu(ag)


def _wattn_consts():
    r = lax.broadcasted_iota(jnp.int32, (256, 768), 0)
    c = lax.broadcasted_iota(jnp.int32, (256, 768), 1)
    spread = ((r // HD == c // HD // (AH // AKV)) & (r % HD == c % HD)).astype(F32)
    head_of_lane = lax.broadcasted_iota(jnp.int32, (1, 768), 1) // HD
    return _seg_matrix(768, HD), _seg_matrix(256, HD), spread, head_of_lane


def _wattn_load(n, s, aq_ref, ak_ref, av_ref, ag_refs, cq_ref, sq_ref, ck_ref, sk_ref, sink_ref):
    start = pl.multiple_of(jnp.clip((n - 1) * QB, 0, s - WIN), QB)
    q0 = pl.multiple_of(n * QB, QB)
    win = pl.ds(start, WIN)
    ag = jnp.concatenate([r[...] for r in ag_refs], axis=1)
    sinks = [sink_ref[0:1, h:h + 1] for h in range(AH)]
    qpos = q0 + lax.broadcasted_iota(jnp.int32, (QB, WIN), 0)
    kpos = start + lax.broadcasted_iota(jnp.int32, (QB, WIN), 1)
    mask = jnp.abs(kpos - qpos) <= QB
    tabs = (cq_ref[pl.ds(q0, QB), :], sq_ref[pl.ds(q0, QB), :], ck_ref[win, :], sk_ref[win, :])
    return start, (aq_ref[...], ak_ref[win, :], av_ref[win, :], ag, sinks), tabs, mask


def _wattn_specs(s):
    whole = lambda w: _bs((s, w), lambda n: (0, 0))
    one = lambda w: _bs((1, w), lambda n: (0, 0))
    return [
        _bs((QB, 768), lambda n: (n, 0)),
        _bs((s, 256), lambda n: (0, C_AK // 256)),
        _bs((s, 256), lambda n: (0, C_AV // 256)),
        _bs((QB, 256), lambda n: (n, C_AG // 256)),
        _bs((QB, 256), lambda n: (n, C_AG // 256 + 1)),
        _bs((QB, 256), lambda n: (n, C_AG // 256 + 2)),
        whole(768), whole(768), whole(256), whole(256),
        one(768), one(256), one(AH),
    ]


def _wattn_tables(cos, sin, gq, gk):
    t = lambda a, k: jnp.tile(a, (1, k))
    return t(cos, AH), t(sin, AH), t(cos, AKV), t(sin, AKV), t(gq, AH), t(gk, AKV)


def _wattn_fwd(proj, cos, sin, gq, gk, sink):
    s = proj.shape[0]

    def body(aq_ref, ak_ref, av_ref, ag0, ag1, ag2, cq_ref, sq_ref, ck_ref, sk_ref, gq_ref, gk_ref, sink_ref, o_ref):
        _, (aq, ak, av, ag, sinks), tabs, mask = _wattn_load(
            pl.program_id(0), s, aq_ref, ak_ref, av_ref, (ag0, ag1, ag2), cq_ref, sq_ref, ck_ref, sk_ref, sink_ref)
        o_ref[...] = _wattn_tile(aq, ak, av, ag, sinks, gq_ref[...], gk_ref[...], *tabs, mask, *_wattn_consts())

    return pl.pallas_call(
        body, name="attn_fwd", grid=(s // QB,),
        in_specs=_wattn_specs(s), out_specs=_bs((QB, 768), lambda n: (n, 0)),
        out_shape=jax.ShapeDtypeStruct((s, 768), F32),
        compiler_params=_params(("arbitrary",)),
    )(proj, proj, proj, proj, proj, proj, *_wattn_tables(cos, sin, gq, gk), sink)


def _wattn_bwd(proj, cos, sin, gq, gk, sink, dy):
    s = proj.shape[0]

    def body(aq_ref, ak_ref, av_ref, ag0, ag1, ag2, cq_ref, sq_ref, ck_ref, sk_ref, gq_ref, gk_ref, sink_ref, dy_ref,
             daq_ref, dak_ref, dav_ref, dag_ref, dgq_ref, dgk_ref, dsink_ref):
        n = pl.program_id(0)

        @pl.when(n == 0)
        def _():
            for r in (dak_ref, dav_ref, dgq_ref, dgk_ref, dsink_ref):
                r[...] = jnp.zeros_like(r)

        start, args, tabs, mask = _wattn_load(
            n, s, aq_ref, ak_ref, av_ref, (ag0, ag1, ag2), cq_ref, sq_ref, ck_ref, sk_ref, sink_ref)
        consts = _wattn_consts()
        _, vjp = jax.vjp(lambda aq, ak, av, ag, sinks, gq_t, gk_t: _wattn_tile(
            aq, ak, av, ag, sinks, gq_t, gk_t, *tabs, mask, *consts), *args, gq_ref[...], gk_ref[...])
        daq, dak, dav, dag, dsinks, dgq, dgk = vjp(dy_ref[...])
        daq_ref[...] = daq
        dag_ref[...] = dag
        win = pl.ds(start, WIN)
        dak_ref[win, :] += dak
        dav_ref[win, :] += dav
        dgq_ref[...] += dgq
        dgk_ref[...] += dgk
        for h in range(AH):
            dsink_ref[0:1, h:h + 1] += dsinks[h]

    whole = lambda shape: _bs(shape, lambda n: (0, 0))
    blk = _bs((QB, 768), lambda n: (n, 0))
    res = pl.pallas_call(
        body, name="attn_bwd", grid=(s // QB,),
        in_specs=_wattn_specs(s) + [blk],
        out_specs=[blk, whole((s, 256)), whole((s, 256)), blk, whole((1, 768)), whole((1, 256)), whole((1, AH))],
        out_shape=[jax.ShapeDtypeStruct((s, 768), F32), jax.ShapeDtypeStruct((s, 256), F32),
                   jax.ShapeDtypeStruct((s, 256), F32), jax.ShapeDtypeStruct((s, 768), F32),
                   jax.ShapeDtypeStruct((1, 768), F32), jax.ShapeDtypeStruct((1, 256), F32),
                   jax.ShapeDtypeStruct((1, AH), F32)],
        compiler_params=_params(("arbitrary",)),
    )(proj, proj, proj, proj, proj, proj, *_wattn_tables(cos, sin, gq, gk), sink, dy)
    daq, dak, dav, dag, dgq, dgk, dsink = res
    return (daq, dak, dav, dag, dgq.reshape(AH, HD).sum(0, keepdims=True), dgk.reshape(AKV, HD).sum(0, keepdims=True),
            dsink)


def _mem_kv(mem, g, w):
    def body(m_ref, g_ref, w_ref, o_ref, mn_ref):
        mn = _rms(m_ref[...], g_ref[...]).astype(BF16)
        mn_ref[...] = mn
        o_ref[...] = jnp.dot(mn, w_ref[...], preferred_element_type=F32)

    return pl.pallas_call(
        body, name="mem_kv",
        out_shape=[jax.ShapeDtypeStruct((NMEM, 2 * XW), F32), jax.ShapeDtypeStruct((NMEM, D), BF16)],
        compiler_params=_params(),
    )(mem, g, w)


def _xattn_tile(qs, gs, kms, vms, gxq, gxk):
    heads = range(XH)
    q = [_rms(qs[h], gxq) for h in heads]
    km = [_rms(kms[h], gxk) for h in heads]
    sc = [_mm_nt(q[h], km[h]) * (XD ** -0.5) for h in heads]
    p = [jnp.exp(sc[h] - lax.stop_gradient(jnp.max(sc[h], axis=-1, keepdims=True))) for h in heads]
    p = [p[h] / jnp.sum(p[h], axis=-1, keepdims=True) for h in heads]
    return [_mm_nn(p[h], vms[h]) * _silu(gs[h]) for h in heads]


XT = 256


def _xattn_specs():
    return [
        _bs((XT, 256), lambda i: (i, C_XQ // 256)), _bs((XT, 256), lambda i: (i, C_XQ // 256 + 1)),
        _bs((XT, 256), lambda i: (i, C_XG // 256)), _bs((XT, 256), lambda i: (i, C_XG // 256 + 1)),
        _bs((NMEM, 2 * XW), lambda i: (0, 0)),
        _bs((1, XD), lambda i: (0, 0)), _bs((1, XD), lambda i: (0, 0)),
    ]


def _xattn_load(q0, q1, g0, g1, mkv_ref):
    qs = [(q0, q1)[h // 2][:, (h % 2) * XD:(h % 2 + 1) * XD] for h in range(XH)]
    gs = [(g0, g1)[h // 2][:, (h % 2) * XD:(h % 2 + 1) * XD] for h in range(XH)]
    kms = [mkv_ref[:, h * XD:(h + 1) * XD] for h in range(XH)]
    vms = [mkv_ref[:, XW + h * XD:XW + (h + 1) * XD] for h in range(XH)]
    return qs, gs, kms, vms


def _xattn_fwd(proj, mkv, gxq, gxk):
    s = proj.shape[0]

    def body(q0, q1, g0, g1, mkv_ref, gxq_ref, gxk_ref, o_ref):
        qs, gs, kms, vms = _xattn_load(q0, q1, g0, g1, mkv_ref)
        outs = _xattn_tile(qs, gs, kms, vms, gxq_ref[...], gxk_ref[...])
        for h in range(XH):
            o_ref[:, h * XD:(h + 1) * XD] = outs[h]

    return pl.pallas_call(
        body, name="xattn_fwd", grid=(s // XT,),
        in_specs=_xattn_specs(), out_specs=_bs((XT, XW), lambda i: (i, 0)),
        out_shape=jax.ShapeDtypeStruct((s, XW), F32),
        compiler_params=_params(("arbitrary",)),
    )(proj, proj, proj, proj, mkv, gxq, gxk)


def _xattn_bwd(proj, mkv, gxq, gxk, dy):
    s = proj.shape[0]

    def body(q0, q1, g0, g1, mkv_ref, gxq_ref, gxk_ref, dy_ref, dq_ref, dg_ref, dmkv_ref, dgxq_ref, dgxk_ref):
        @pl.when(pl.program_id(0) == 0)
        def _():
            dmkv_ref[...] = jnp.zeros_like(dmkv_ref)
            dgxq_ref[...] = jnp.zeros_like(dgxq_ref)
            dgxk_ref[...] = jnp.zeros_like(dgxk_ref)

        qs, gs, kms, vms = _xattn_load(q0, q1, g0, g1, mkv_ref)
        _, vjp = jax.vjp(_xattn_tile, qs, gs, kms, vms, gxq_ref[...], gxk_ref[...])
        dqs, dgs, dkms, dvms, dgxq, dgxk = vjp([dy_ref[:, h * XD:(h + 1) * XD] for h in range(XH)])
        for h in range(XH):
            dq_ref[:, h * XD:(h + 1) * XD] = dqs[h]
            dg_ref[:, h * XD:(h + 1) * XD] = dgs[h]
            dmkv_ref[:, h * XD:(h + 1) * XD] += dkms[h]
            dmkv_ref[:, XW + h * XD:XW + (h + 1) * XD] += dvms[h]
        dgxq_ref[...] += dgxq
        dgxk_ref[...] += dgxk

    whole = lambda shape: _bs(shape, lambda i: (0, 0))
    return pl.pallas_call(
        body, name="xattn_bwd", grid=(s // XT,),
        in_specs=_xattn_specs() + [_bs((XT, XW), lambda i: (i, 0))],
        out_specs=[_bs((XT, XW), lambda i: (i, 0)), _bs((XT, XW), lambda i: (i, 0)), whole((NMEM, 2 * XW)),
                   whole((1, XD)), whole((1, XD))],
        out_shape=[jax.ShapeDtypeStruct((s, XW), F32), jax.ShapeDtypeStruct((s, XW), F32),
                   jax.ShapeDtypeStruct((NMEM, 2 * XW), F32), jax.ShapeDtypeStruct((1, XD), F32),
                   jax.ShapeDtypeStruct((1, XD), F32)],
        compiler_params=_params(("arbitrary",)),
    )(proj, proj, proj, proj, mkv, gxq, gxk, dy)


def _mem_bwd(mem, g, dmn):
    def body(m_ref, dmn_ref, o_ref):
        m = m_ref[...]
        r = lax.rsqrt(jnp.mean(m * m, axis=-1, keepdims=True) + NORM_EPS)
        o_ref[...] = jnp.sum(dmn_ref[...] * m * r, axis=0, keepdims=True)

    del g
    return pl.pallas_call(body, name="mem_norm_bwd", out_shape=jax.ShapeDtypeStruct((1, D), F32),
                          compiler_params=_params())(mem, dmn)


SHIFT_W = 512


def _shift_rows(p, s):
    row = lax.broadcasted_iota(jnp.int32, p.shape, 0)
    prev = jnp.where(row == 0, 0.0, pltpu.roll(p, 1, 0))
    nxt = jnp.where(row == s - 1, 0.0, pltpu.roll(p, s - 1, 0))
    return prev, nxt


def _shift_fwd(proj, mu):
    s = proj.shape[0]

    def body(p_ref, mu_ref, o_ref):
        p = p_ref[...]
        prev, nxt = _shift_rows(p, s)
        o_ref[...] = p + mu_ref[...] * (0.5 * (prev + nxt) - p)

    return pl.pallas_call(
        body, name="shift_fwd", grid=(RSW // SHIFT_W,),
        in_specs=[_bs((s, SHIFT_W), lambda j: (0, C_RS // SHIFT_W + j)), _bs((1, SHIFT_W), lambda j: (0, j))],
        out_specs=_bs((s, SHIFT_W), lambda j: (0, j)),
        out_shape=jax.ShapeDtypeStruct((s, RSW), F32),
        compiler_params=_params(("parallel",)),
    )(proj, mu)


def _shift_bwd(proj, mu, dps):
    s = proj.shape[0]

    def body(p_ref, mu_ref, g_ref, o_ref, dmu_ref):
        p, g, mu_v = p_ref[...], g_ref[...], mu_ref[...]
        prev, nxt = _shift_rows(p, s)
        dmu_ref[...] = jnp.sum(g * (0.5 * (prev + nxt) - p), axis=0, keepdims=True)
        mg = mu_v * g
        down, up = _shift_rows(mg, s)
        o_ref[...] = g * (1.0 - mu_v) + 0.5 * (down + up)

    return pl.pallas_call(
        body, name="shift_bwd", grid=(RSW // SHIFT_W,),
        in_specs=[_bs((s, SHIFT_W), lambda j: (0, C_RS // SHIFT_W + j)), _bs((1, SHIFT_W), lambda j: (0, j)),
                  _bs((s, SHIFT_W), lambda j: (0, j))],
        out_specs=[_bs((s, SHIFT_W), lambda j: (0, j)), _bs((1, SHIFT_W), lambda j: (0, j))],
        out_shape=[jax.ShapeDtypeStruct((s, RSW), F32), jax.ShapeDtypeStruct((1, RSW), F32)],
        compiler_params=_params(("parallel",)),
    )(proj, mu, dps)


def _pre_tile(k, wf, wb, af, ab, k_k, k_a, w0s, w2s, a0s, a2s, seg):
    kx = k * k_k
    ss = _hdot(kx * kx, seg)
    kk = kx / jnp.maximum(jnp.sqrt(ss), 1e-12)
    outs = [kk]
    for d, (w_in, a_in) in enumerate(((wf, af), (wb, ab))):
        z = w0s[d] + _mm_nn(jnp.tanh(w_in), w2s[d])
        wd = -_softplus(-z) - 0.5
        dec = jnp.exp(-jnp.exp(wd))
        ad = jax.nn.sigmoid(a0s[d] + _mm_nn(a_in, a2s[d]))
        kd = k * (1.0 + (ad - 1.0) * k_a)
        outs += [dec, kd, kk * ad]
    return outs


PT = 256


def _pre_load(ps_ref, kk_ref, ka_ref, w0_ref, w2_ref, a0_ref, a2_ref):
    k = ps_ref[:, RW:2 * RW]
    wf, wb = ps_ref[:, 3 * RW:3 * RW + 64], ps_ref[:, 3 * RW + 64:3 * RW + 128]
    af, ab = ps_ref[:, 3 * RW + 128:3 * RW + 192], ps_ref[:, 3 * RW + 192:3 * RW + 256]
    w0s = [w0_ref[0:1, :], w0_ref[1:2, :]]
    a0s = [a0_ref[0:1, :], a0_ref[1:2, :]]
    w2s = [w2_ref[0], w2_ref[1]]
    a2s = [a2_ref[0], a2_ref[1]]
    return (k, wf, wb, af, ab, kk_ref[...], ka_ref[...], w0s, w2s, a0s, a2s)


def _pre_specs():
    c = lambda shape: _bs(shape, lambda i: tuple(0 for _ in shape))
    return [_bs((PT, RSW), lambda i: (i, 0)), c((1, RW)), c((1, RW)), c((2, RW)), c((2, 64, RW)), c((2, RW)),
            c((2, 64, RW))]


def _pre_fwd(ps, k_k, k_a, w0, w2, a0, a2):
    s = ps.shape[0]

    def body(ps_ref, kk_ref, ka_ref, w0_ref, w2_ref, a0_ref, a2_ref, *outs):
        args = _pre_load(ps_ref, kk_ref, ka_ref, w0_ref, w2_ref, a0_ref, a2_ref)
        res = _pre_tile(*args, _seg_matrix(RW, HD))
        for o_ref, v in zip(outs, res):
            o_ref[...] = v

    return pl.pallas_call(
        body, name="rwkv_pre_fwd", grid=(s // PT,),
        in_specs=_pre_specs(), out_specs=[_bs((PT, RW), lambda i: (i, 0))] * 7,
        out_shape=[jax.ShapeDtypeStruct((s, RW), F32)] * 7,
        compiler_params=_params(("parallel",)),
    )(ps, k_k, k_a, w0, w2, a0, a2)


def _pre_bwd(ps, k_k, k_a, w0, w2, a0, a2, dr, dv, cts):
    s = ps.shape[0]

    def body(ps_ref, kk_ref, ka_ref, w0_ref, w2_ref, a0_ref, a2_ref, dr_ref, dv_ref, c0, c1, c2, c3, c4, c5, c6,
             dps_ref, dkk_ref, dka_ref, dw0_ref, dw2_ref, da0_ref, da2_ref):
        @pl.when(pl.program_id(0) == 0)
        def _():
            for r in (dkk_ref, dka_ref, dw0_ref, dw2_ref, da0_ref, da2_ref):
                r[...] = jnp.zeros_like(r)

        args = _pre_load(ps_ref, kk_ref, ka_ref, w0_ref, w2_ref, a0_ref, a2_ref)
        seg = _seg_matrix(RW, HD)
        _, vjp = jax.vjp(lambda *a: _pre_tile(*a, seg), *args)
        dk, dwf, dwb, daf, dab, dk_k, dk_a, dw0s, dw2s, da0s, da2s = vjp([c[...] for c in (c0, c1, c2, c3, c4, c5, c6)])
        dps_ref[:, 0:RW] = dr_ref[...]
        dps_ref[:, RW:2 * RW] = dk
        dps_ref[:, 2 * RW:3 * RW] = dv_ref[...]
        for j, t in enumerate((dwf, dwb, daf, dab)):
            dps_ref[:, 3 * RW + 64 * j:3 * RW + 64 * (j + 1)] = t
        dkk_ref[...] += dk_k
        dka_ref[...] += dk_a
        for d in range(2):
            dw0_ref[d:d + 1, :] += dw0s[d]
            da0_ref[d:d + 1, :] += da0s[d]
            dw2_ref[d] += dw2s[d]
            da2_ref[d] += da2s[d]

    c = lambda shape: _bs(shape, lambda i: tuple(0 for _ in shape))
    row = _bs((PT, RW), lambda i: (i, 0))
    return pl.pallas_call(
        body, name="rwkv_pre_bwd", grid=(s // PT,),
        in_specs=_pre_specs() + [row] * 9,
        out_specs=[_bs((PT, RSW), lambda i: (i, 0)), c((1, RW)), c((1, RW)), c((2, RW)), c((2, 64, RW)), c((2, RW)),
                   c((2, 64, RW))],
        out_shape=[jax.ShapeDtypeStruct((s, RSW), F32), jax.ShapeDtypeStruct((1, RW), F32),
                   jax.ShapeDtypeStruct((1, RW), F32), jax.ShapeDtypeStruct((2, RW), F32),
                   jax.ShapeDtypeStruct((2, 64, RW), F32), jax.ShapeDtypeStruct((2, RW), F32),
                   jax.ShapeDtypeStruct((2, 64, RW), F32)],
        compiler_params=_params(("arbitrary",)),
    )(ps, k_k, k_a, w0, w2, a0, a2, dr, dv, *cts)


def _post_tile(y0, y1, r, v, kd0, kd1, rg, r_k, ln_w, ln_b, seg):
    ysum = y0 + y1
    bonus = (_hdot(r * kd0 * r_k, seg) + _hdot(r * kd1 * r_k, seg)) * v
    mean = _hdot(ysum, seg) * (1.0 / HD)
    cen = ysum - mean
    var = _hdot(cen * cen, seg) * (1.0 / HD)
    y = cen * lax.rsqrt(var + GN_EPS) * ln_w + ln_b + bonus
    return y * _silu(rg)


def _post_specs():
    row = _bs((PT, RW), lambda i: (i, 0))
    c = _bs((1, RW), lambda i: (0, 0))
    return [row, row, _bs((PT, RW), lambda i: (i, 0)), _bs((PT, RW), lambda i: (i, 2)), row, row,
            _bs((PT, RW), lambda i: (i, C_RG // RW)), c, c, c]


def _post_fwd(y0, y1, ps, kd0, kd1, proj, r_k, ln_w, ln_b):
    s = ps.shape[0]

    def body(y0_ref, y1_ref, r_ref, v_ref, kd0_ref, kd1_ref, rg_ref, rk_ref, lw_ref, lb_ref, o_ref):
        o_ref[...] = _post_tile(y0_ref[...], y1_ref[...], r_ref[...], v_ref[...], kd0_ref[...], kd1_ref[...],
                                rg_ref[...], rk_ref[...], lw_ref[...], lb_ref[...], _seg_matrix(RW, HD))

    return pl.pallas_call(
        body, name="rwkv_post_fwd", grid=(s // PT,),
        in_specs=_post_specs(), out_specs=_bs((PT, RW), lambda i: (i, 0)),
        out_shape=jax.ShapeDtypeStruct((s, RW), F32),
        compiler_params=_params(("parallel",)),
    )(y0, y1, ps, ps, kd0, kd1, proj, r_k, ln_w, ln_b)


def _post_bwd(y0, y1, ps, kd0, kd1, proj, r_k, ln_w, ln_b, dy):
    s = ps.shape[0]

    def body(y0_ref, y1_ref, r_ref, v_ref, kd0_ref, kd1_ref, rg_ref, rk_ref, lw_ref, lb_ref, dy_ref,
             dys_ref, dr_ref, dv_ref, dkd0_ref, dkd1_ref, drg_ref, drk_ref, dlw_ref, dlb_ref):
        @pl.when(pl.program_id(0) == 0)
        def _():
            for r in (drk_ref, dlw_ref, dlb_ref):
                r[...] = jnp.zeros_like(r)

        seg = _seg_matrix(RW, HD)
        args = [t[...] for t in (y0_ref, y1_ref, r_ref, v_ref, kd0_ref, kd1_ref, rg_ref, rk_ref, lw_ref, lb_ref)]
        _, vjp = jax.vjp(lambda *a: _post_tile(*a, seg), *args)
        dy0, _, dr, dv, dkd0, dkd1, drg, drk, dlw, dlb = vjp(dy_ref[...])
        dys_ref[...] = dy0
        dr_ref[...] = dr
        dv_ref[...] = dv
        dkd0_ref[...] = dkd0
        dkd1_ref[...] = dkd1
        drg_ref[...] = drg
        drk_ref[...] += drk
        dlw_ref[...] += dlw
        dlb_ref[...] += dlb

    row = _bs((PT, RW), lambda i: (i, 0))
    c = _bs((1, RW), lambda i: (0, 0))
    return pl.pallas_call(
        body, name="rwkv_post_bwd", grid=(s // PT,),
        in_specs=_post_specs() + [row], out_specs=[row] * 6 + [c] * 3,
        out_shape=[jax.ShapeDtypeStruct((s, RW), F32)] * 6 + [jax.ShapeDtypeStruct((1, RW), F32)] * 3,
        compiler_params=_params(("arbitrary",)),
    )(y0, y1, ps, ps, kd0, kd1, proj, r_k, ln_w, ln_b, dy)


def _ones2():
    r = lax.broadcasted_iota(jnp.int32, (256, 128), 0) % 128 // HD
    c = lax.broadcasted_iota(jnp.int32, (256, 128), 1) // HD
    return (r == c).astype(BF16)


def _split(p):
    hi = p.astype(BF16)
    lo = (p - hi.astype(F32)).astype(BF16)
    return jnp.concatenate([hi, lo], axis=1)


def _to_t8(a):
    s = a.shape[0]
    t = a.reshape(s // 8, 8, NPAIR, 2, HD).transpose(0, 2, 4, 3, 1)
    t = jnp.pad(t, ((0, 0), (0, 0), (0, 0), (0, 0), (0, HD - 8))).reshape(s // 8, NPAIR, HD, 128)
    hi = t.astype(BF16)
    lo = (t - hi.astype(F32)).astype(BF16)
    return jnp.concatenate([hi, lo], axis=-1)


def _from_t8(t8):
    g = t8.shape[0]
    t = t8.reshape(g, NPAIR, HD, 2, HD)[..., :8]
    return t.transpose(0, 4, 1, 3, 2).reshape(g * 8, RW)


def _scan_specs(direction, nc, fwd_order):
    def tb(c):
        sc = c if fwd_order else nc - 1 - c
        return sc if direction == 0 else nc - 1 - sc

    row = _bs((TC, RW), lambda c: (tb(c), 0))
    rowv = _bs((TC, RW), lambda c: (tb(c), 2))
    return row, rowv


def _put_t8(ref, g, u, tiles):
    for p in range(NPAIR):
        ref[g, p, :, u:u + 1] = tiles[p][:, u:u + 1]
        ref[g, p, :, HD + u:HD + u + 1] = tiles[p][:, HD + u:HD + u + 1]


def _scan_fwd(dec, kd, b, ps, kk, vl, direction):
    s = dec.shape[0]
    nc, ng = s // TC, TC // 8
    row, t8_in, t8_out = _scan_specs(direction, nc, True)
    n = NPAIR * HD

    def body(dec_ref, kd_ref, b_ref, r_ref, kk_ref, vl_ref, y8_ref, ck_ref, st):
        @pl.when(pl.program_id(0) == 0)
        def _():
            st[...] = jnp.zeros_like(st)

        ck_ref[0] = st[...]
        ones2 = _ones2()
        lane_u = lax.broadcasted_iota(jnp.int32, (HD, 256), 1) % HD
        tiles = lambda res, k: [res[k * n + p * HD:k * n + (p + 1) * HD] for p in range(NPAIR)]

        def group(gi, carry):
            g = gi if direction == 0 else ng - 1 - gi
            rows8 = pl.ds(pl.multiple_of(g * 8, 8), 8)
            d8, k8, b8, r8, kk8 = (q[rows8, :] for q in (dec_ref, kd_ref, b_ref, r_ref, kk_ref))
            pc = [slice(p * 128, (p + 1) * 128) for p in range(NPAIR)]
            ss = [st[p] for p in range(NPAIR)]
            u_prev = None
            for ui in range(8):
                u = ui if direction == 0 else 7 - ui
                lhs = [_split(ss[p] * kk8[u:u + 1, pc[p]]) for p in range(NPAIR)]
                for p in range(NPAIR):
                    vt = vl_ref[g, p]
                    lhs.append(jnp.where(lane_u == u, vt, jnp.zeros_like(vt)))
                if u_prev is not None:
                    lhs += [_split(ss[p] * r8[u_prev:u_prev + 1, pc[p]]) for p in range(NPAIR)]
                res = jnp.dot(jnp.concatenate(lhs, axis=0), ones2, preferred_element_type=F32)
                if u_prev is not None:
                    _put_t8(y8_ref, g, u_prev, tiles(res, 2))
                sa, vb = tiles(res, 0), tiles(res, 1)
                for p in range(NPAIR):
                    ss[p] = ss[p] * d8[u:u + 1, pc[p]] - sa[p] * b8[u:u + 1, pc[p]] + vb[p] * k8[u:u + 1, pc[p]]
                u_prev = u
            lhs = [_split(ss[p] * r8[u_prev:u_prev + 1, pc[p]]) for p in range(NPAIR)]
            res = jnp.dot(jnp.concatenate(lhs, axis=0), ones2, preferred_element_type=F32)
            _put_t8(y8_ref, g, u_prev, tiles(res, 0))
            for p in range(NPAIR):
                st[p] = ss[p]
            return carry

        lax.fori_loop(0, ng, group, 0)

    return pl.pallas_call(
        body, name=f"rwkv_scan_fwd{direction}", grid=(nc,),
        in_specs=[row, row, row, row, row, t8_in],
        out_specs=[t8_out, _bs((1, NPAIR, HD, 128), lambda c: (c, 0, 0, 0))],
        out_shape=[jax.ShapeDtypeStruct((s // 8, NPAIR, HD, 128), F32),
                   jax.ShapeDtypeStruct((nc, NPAIR, HD, 128), F32)],
        scratch_shapes=[pltpu.VMEM((NPAIR, HD, 128), F32)],
        compiler_params=_params(("arbitrary",)),
    )(dec, kd, b, ps, kk, vl)


def _scan_bwd(dec, kd, b, ps, kk, vl, dyl, ck, direction):
    s = dec.shape[0]
    nc, ng = s // TC, TC // 8
    row, t8_in, t8_out = _scan_specs(direction, nc, False)
    n = NPAIR * HD

    def body(dec_ref, kd_ref, b_ref, r_ref, kk_ref, vl_ref, dyl_ref, ck_ref,
             dr_ref, dd_ref, db_ref, dk_ref, dkk_ref, dv8_ref, st, sa_s, vb_s, dy_s, ds):
        @pl.when(pl.program_id(0) == 0)
        def _():
            ds[...] = jnp.zeros_like(ds)

        st[0] = ck_ref[0]
        ones2 = _ones2()
        lane_u = lax.broadcasted_iota(jnp.int32, (HD, 256), 1) % HD
        row_id = lax.broadcasted_iota(jnp.int32, (8, 128), 0)
        pc = [slice(p * 128, (p + 1) * 128) for p in range(NPAIR)]
        tiles = lambda res, k: [res[k * n + p * HD:k * n + (p + 1) * HD] for p in range(NPAIR)]

        def fgroup(gi, carry):
            g = gi if direction == 0 else ng - 1 - gi
            rows8 = pl.ds(pl.multiple_of(g * 8, 8), 8)
            d8, k8, b8, kk8 = (q[rows8, :] for q in (dec_ref, kd_ref, b_ref, kk_ref))
            ss = [st[gi * 8, p] for p in range(NPAIR)]
            for ui in range(8):
                u = ui if direction == 0 else 7 - ui
                i = gi * 8 + ui
                lhs = [_split(ss[p] * kk8[u:u + 1, pc[p]]) for p in range(NPAIR)]
                for ref in (vl_ref, dyl_ref):
                    for p in range(NPAIR):
                        t = ref[g, p]
                        lhs.append(jnp.where(lane_u == u, t, jnp.zeros_like(t)))
                res = jnp.dot(jnp.concatenate(lhs, axis=0), ones2, preferred_element_type=F32)
                sa, vb, dyb = tiles(res, 0), tiles(res, 1), tiles(res, 2)
                for p in range(NPAIR):
                    sa_s[i, p] = sa[p]
                    vb_s[i, p] = vb[p]
                    dy_s[i, p] = dyb[p]
                    ss[p] = ss[p] * d8[u:u + 1, pc[p]] - sa[p] * b8[u:u + 1, pc[p]] + vb[p] * k8[u:u + 1, pc[p]]
                    st[i + 1, p] = ss[p]
            return carry

        lax.fori_loop(0, ng, fgroup, 0)

        def bgroup(gj, carry):
            gi = ng - 1 - gj
            g = gi if direction == 0 else ng - 1 - gi
            rows8 = pl.ds(pl.multiple_of(g * 8, 8), 8)
            d8, k8, b8, r8, kk8 = (q[rows8, :] for q in (dec_ref, kd_ref, b_ref, r_ref, kk_ref))
            dss = [ds[p] for p in range(NPAIR)]
            acc = [[jnp.zeros((8, 128), F32) for _ in range(5)] for _ in range(NPAIR)]
            for uj in range(8):
                ui = 7 - uj
                u = ui if direction == 0 else 7 - ui
                i = gi * 8 + ui
                dyb = [dy_s[i, p] for p in range(NPAIR)]
                for p in range(NPAIR):
                    dss[p] = dss[p] + dyb[p] * r8[u:u + 1, pc[p]]
                lhs = [_split(dss[p] * b8[u:u + 1, pc[p]]) for p in range(NPAIR)]
                lhs += [_split(dss[p] * k8[u:u + 1, pc[p]]) for p in range(NPAIR)]
                res = jnp.dot(jnp.concatenate(lhs, axis=0), ones2, preferred_element_type=F32)
                dsa, dvb = tiles(res, 0), tiles(res, 1)
                _put_t8(dv8_ref, g, u, dvb)
                for p in range(NPAIR):
                    sp, sn = st[i, p], st[i + 1, p]
                    outs = (jnp.sum(sn * dyb[p], axis=0, keepdims=True), jnp.sum(dss[p] * sp, axis=0, keepdims=True),
                            -jnp.sum(dss[p] * sa_s[i, p], axis=0, keepdims=True),
                            jnp.sum(dss[p] * vb_s[i, p], axis=0, keepdims=True),
                            -jnp.sum(sp * dsa[p], axis=0, keepdims=True))
                    acc[p] = [jnp.where(row_id == u, o, a_) for o, a_ in zip(outs, acc[p])]
                    dss[p] = dss[p] * d8[u:u + 1, pc[p]] - dsa[p] * kk8[u:u + 1, pc[p]]
            for p in range(NPAIR):
                ds[p] = dss[p]
                for o_ref, a_ in zip((dr_ref, dd_ref, db_ref, dk_ref, dkk_ref), acc[p]):
                    o_ref[rows8, pc[p]] = a_
            return carry

        lax.fori_loop(0, ng, bgroup, 0)

    chunk = lambda k: pltpu.VMEM((k, NPAIR, HD, 128), F32)
    return pl.pallas_call(
        body, name=f"rwkv_scan_bwd{direction}", grid=(nc,),
        in_specs=[row, row, row, row, row, t8_in, t8_in, _bs((1, NPAIR, HD, 128), lambda c: (nc - 1 - c, 0, 0, 0))],
        out_specs=[row] * 5 + [t8_out],
        out_shape=[jax.ShapeDtypeStruct((s, RW), F32)] * 5 + [jax.ShapeDtypeStruct((s // 8, NPAIR, HD, 128), F32)],
        scratch_shapes=[chunk(TC + 1), chunk(TC), chunk(TC), chunk(TC), pltpu.VMEM((NPAIR, HD, 128), F32)],
        compiler_params=_params(("arbitrary",)),
    )(dec, kd, b, ps, kk, vl, dyl, ck)


def _tiles(res, k):
    n = NPAIR * HD
    return [res[k * n + p * HD:k * n + (p + 1) * HD] for p in range(NPAIR)]


def _rows_to_tiles(src_ref, rows8, stage, out_s, base):
    for p in range(NPAIR):
        stage[base + p, 0:8, 0:HD] = src_ref[rows8, p * 128:p * 128 + HD]
        stage[base + p, HD:HD + 8, 0:HD] = src_ref[rows8, p * 128 + HD:(p + 1) * 128]
        out_s[base + p] = stage[base + p].T[0:HD].astype(BF16)


def _tiles_to_rows(tile_s, base, dst_ref, rows8):
    for p in range(NPAIR):
        t = jnp.concatenate([tile_s[base + p], jnp.zeros((HD, 128), F32)], axis=0).T
        dst_ref[rows8, p * 128:p * 128 + HD] = t[0:8, 0:HD]
        dst_ref[rows8, p * 128 + HD:(p + 1) * 128] = t[HD:HD + 8, 0:HD]


def _put_cols(tile_s, base, u, tiles):
    for p in range(NPAIR):
        tile_s[base + p, :, u:u + 1] = tiles[p][:, u:u + 1]
        tile_s[base + p, :, HD + u:HD + u + 1] = tiles[p][:, HD + u:HD + u + 1]


def _scan2_fwd(per_dir, ps, kk, gather=()):
    s = ps.shape[0]
    nc, ng = s // TC, TC // 8
    ngat = len(gather)
    in_specs, operands, out_specs, out_shape = [], [], [], []
    for d in (0, 1):
        row, rowv = _scan_specs(d, nc, True)
        in_specs += [row] * 5 + [rowv]
        operands += list(per_dir[d]) + [ps, kk, ps]
        out_specs += [row, _bs((1, NPAIR, HD, 128), lambda c: (c, 0, 0, 0))]
        out_shape += [jax.ShapeDtypeStruct((s, RW), F32), jax.ShapeDtypeStruct((nc, NPAIR, HD, 128), F32)]
    in_specs += [ANY] * ngat
    operands += list(gather)
    out_specs += [ANY] * ngat
    out_shape += _gather_out_shapes(gather)

    def body(*refs):
        ins = [refs[0:6], refs[6:12]]
        base = 12 + ngat
        y_refs, ck_refs = (refs[base], refs[base + 2]), (refs[base + 1], refs[base + 3])
        st, vt_s, yt_s, stage = refs[base + 4 + ngat:base + 8 + ngat]
        if ngat:
            g_start, g_forward, g_finish = _gather_phases(
                gather, refs[12:base], refs[base + 4:base + 4 + ngat], refs[base + 8 + ngat:])

        @pl.when(pl.program_id(0) == 0)
        def _():
            st[...] = jnp.zeros_like(st)
            yt_s[...] = jnp.zeros_like(yt_s)
            stage[...] = jnp.zeros_like(stage)
            if ngat:
                g_start()

        if ngat:
            @pl.when(pl.program_id(0) == nc // 2)
            def _():
                g_forward()

        for d in (0, 1):
            ck_refs[d][0] = st[d * NPAIR:(d + 1) * NPAIR]
        ones2 = _ones2()
        ones1 = ones2[0:128]
        lane_u = lax.broadcasted_iota(jnp.int32, (HD, 128), 1) % HD
        pc = [slice(p * 128, (p + 1) * 128) for p in range(NPAIR)]

        def group(gi, carry):
            gs = (gi, ng - 1 - gi)
            rows8 = [pl.ds(pl.multiple_of(gs[d] * 8, 8), 8) for d in (0, 1)]
            blk = [[q[rows8[d], :] for q in ins[d][:5]] for d in (0, 1)]
            for d in (0, 1):
                _rows_to_tiles(ins[d][5], rows8[d], stage, vt_s, d * NPAIR)
            ss = [[st[d * NPAIR + p] for p in range(NPAIR)] for d in (0, 1)]
            for ui in range(9):
                us, ups = (ui, 7 - ui), (ui - 1, 8 - ui)
                lhs1, where = [], {}
                for d in (0, 1):
                    if ui < 8:
                        where["sa", d] = len(lhs1) // NPAIR
                        lhs1 += [(ss[d][p] * blk[d][4][us[d]:us[d] + 1, pc[p]]).astype(BF16) for p in range(NPAIR)]
                        where["vb", d] = len(lhs1) // NPAIR
                        for p in range(NPAIR):
                            vt = vt_s[d * NPAIR + p]
                            lhs1.append(jnp.where(lane_u == us[d], vt, jnp.zeros_like(vt)))
                    if ui > 0:
                        where["y", d] = len(lhs1) // NPAIR
                        lhs1 += [(ss[d][p] * blk[d][3][ups[d]:ups[d] + 1, pc[p]]).astype(BF16) for p in range(NPAIR)]
                res1 = jnp.dot(jnp.concatenate(lhs1, axis=0), ones1, preferred_element_type=F32)
                for d in (0, 1):
                    d8, k8, b8, _, _ = blk[d]
                    u = us[d]
                    if ui < 8:
                        sa, vb = _tiles(res1, where["sa", d]), _tiles(res1, where["vb", d])
                        for p in range(NPAIR):
                            ss[d][p] = (ss[d][p] * d8[u:u + 1, pc[p]] - sa[p] * b8[u:u + 1, pc[p]]
                                        + vb[p] * k8[u:u + 1, pc[p]])
                    if ui > 0:
                        _put_cols(yt_s, d * NPAIR, ups[d], _tiles(res1, where["y", d]))
            for d in (0, 1):
                _tiles_to_rows(yt_s, d * NPAIR, y_refs[d], rows8[d])
                for p in range(NPAIR):
                    st[d * NPAIR + p] = ss[d][p]
            return carry

        for gi in range(ng):
            group(gi, 0)

        if ngat:
            @pl.when(pl.program_id(0) == nc - 1)
            def _():
                g_finish()

    outs = pl.pallas_call(
        body, name="rwkv_scan_fwd", grid=(nc,), in_specs=in_specs, out_specs=out_specs, out_shape=out_shape,
        scratch_shapes=[pltpu.VMEM((2 * NPAIR, HD, 128), F32), pltpu.VMEM((2 * NPAIR, HD, 128), BF16),
                        pltpu.VMEM((2 * NPAIR, HD, 128), F32), pltpu.VMEM((2 * NPAIR, 128, 128), F32)]
        + (_gather_sems(ngat) if ngat else []),
        compiler_params=pltpu.CompilerParams(dimension_semantics=("arbitrary",), vmem_limit_bytes=VMEM_LIMIT,
                                             has_side_effects=bool(ngat)),
    )(*operands)
    return [(outs[0], outs[1]), (outs[2], outs[3])], list(outs[4:])


def _scan2_bwd(per_dir, ps, kk, dy, scatter=()):
    s = ps.shape[0]
    nc, ng = s // TC, TC // 8
    nsc = len(scatter)
    in_specs, operands, out_specs, out_shape = [], [], [], []
    for d in (0, 1):
        row, rowv = _scan_specs(d, nc, False)
        dec, kd, b, ck = per_dir[d]
        in_specs += [row] * 5 + [rowv, row, _bs((1, NPAIR, HD, 128), lambda c: (nc - 1 - c, 0, 0, 0))]
        operands += [dec, kd, b, ps, kk, ps, dy, ck]
        out_specs += [row] * 6
        out_shape += [jax.ShapeDtypeStruct((s, RW), F32)] * 6
    in_specs += [ANY] * nsc
    operands += list(scatter)
    out_specs += [ANY] * nsc
    out_shape += _scatter_out_shapes(scatter)

    def body(*refs):
        ins = [refs[0:8], refs[8:16]]
        base = 16 + nsc
        outs = [refs[base:base + 6], refs[base + 6:base + 12]]
        st, sa_s, vb_s, dy_s, ds, vt_s, dyt_s, dvt_s, stage = refs[base + 12 + nsc:base + 21 + nsc]
        if nsc:
            s_start, s_finish = _scatter_phases(refs[16:base], refs[base + 12:base + 12 + nsc], refs[base + 21 + nsc:])

        @pl.when(pl.program_id(0) == 0)
        def _():
            dvt_s[...] = jnp.zeros_like(dvt_s)
            stage[...] = jnp.zeros_like(stage)
            ds[...] = jnp.zeros_like(ds)
            if nsc:
                s_start()

        for d in (0, 1):
            st[d * (TC + 1)] = ins[d][7][0]
        ones2 = _ones2()
        ones1 = ones2[0:128]
        lane_u = lax.broadcasted_iota(jnp.int32, (HD, 128), 1) % HD
        row_id = lax.broadcasted_iota(jnp.int32, (8, 128), 0)
        pc = [slice(p * 128, (p + 1) * 128) for p in range(NPAIR)]

        def load_rows(gs):
            return [[q[pl.ds(pl.multiple_of(gs[d] * 8, 8), 8), :] for q in ins[d][:5]] for d in (0, 1)]

        def fgroup(gi, carry):
            gs = (gi, ng - 1 - gi)
            blk = load_rows(gs)
            for d in (0, 1):
                rows8 = pl.ds(pl.multiple_of(gs[d] * 8, 8), 8)
                _rows_to_tiles(ins[d][5], rows8, stage, vt_s, d * NPAIR)
                _rows_to_tiles(ins[d][6], rows8, stage, dyt_s, d * NPAIR)
            ss = [[st[d * (TC + 1) + gi * 8, p] for p in range(NPAIR)] for d in (0, 1)]
            for ui in range(8):
                us = (ui, 7 - ui)
                i = gi * 8 + ui
                lhs1 = []
                for d in (0, 1):
                    kk8 = blk[d][4]
                    lhs1 += [(ss[d][p] * kk8[us[d]:us[d] + 1, pc[p]]).astype(BF16) for p in range(NPAIR)]
                    for tile_s in (vt_s, dyt_s):
                        for p in range(NPAIR):
                            t = tile_s[d * NPAIR + p]
                            lhs1.append(jnp.where(lane_u == us[d], t, jnp.zeros_like(t)))
                res1 = jnp.dot(jnp.concatenate(lhs1, axis=0), ones1, preferred_element_type=F32)
                for d in (0, 1):
                    d8, k8, b8, _, _ = blk[d]
                    u = us[d]
                    sa, vb, dyb = _tiles(res1, 3 * d), _tiles(res1, 3 * d + 1), _tiles(res1, 3 * d + 2)
                    for p in range(NPAIR):
                        sa_s[d * TC + i, p] = sa[p]
                        vb_s[d * TC + i, p] = vb[p]
                        dy_s[d * TC + i, p] = dyb[p]
                        ss[d][p] = ss[d][p] * d8[u:u + 1, pc[p]] - sa[p] * b8[u:u + 1, pc[p]] + vb[p] * k8[u:u + 1, pc[p]]
                        st[d * (TC + 1) + i + 1, p] = ss[d][p]
            return carry

        for gi in range(ng):
            fgroup(gi, 0)

        def bgroup(gj, carry):
            gi = ng - 1 - gj
            gs = (gi, ng - 1 - gi)
            blk = load_rows(gs)
            dss = [[ds[d * NPAIR + p] for p in range(NPAIR)] for d in (0, 1)]
            acc = [[[jnp.zeros((8, 128), F32) for _ in range(5)] for _ in range(NPAIR)] for _ in (0, 1)]
            for uj in range(8):
                ui = 7 - uj
                us = (ui, 7 - ui)
                i = gi * 8 + ui
                lhs1, dyb = [], [None, None]
                for d in (0, 1):
                    _, k8, b8, r8, _ = blk[d]
                    u = us[d]
                    dyb[d] = [dy_s[d * TC + i, p] for p in range(NPAIR)]
                    for p in range(NPAIR):
                        dss[d][p] = dss[d][p] + dyb[d][p] * r8[u:u + 1, pc[p]]
                    lhs1 += [(dss[d][p] * b8[u:u + 1, pc[p]]).astype(BF16) for p in range(NPAIR)]
                    lhs1 += [(dss[d][p] * k8[u:u + 1, pc[p]]).astype(BF16) for p in range(NPAIR)]
                res1 = jnp.dot(jnp.concatenate(lhs1, axis=0), ones1, preferred_element_type=F32)
                for d in (0, 1):
                    d8, _, _, _, kk8 = blk[d]
                    u = us[d]
                    dsa, dvb = _tiles(res1, 2 * d), _tiles(res1, 2 * d + 1)
                    _put_cols(dvt_s, d * NPAIR, u, dvb)
                    for p in range(NPAIR):
                        sp, sn = st[d * (TC + 1) + i, p], st[d * (TC + 1) + i + 1, p]
                        dsv = dss[d][p]
                        vals = (jnp.sum(sn * dyb[d][p], axis=0, keepdims=True), jnp.sum(dsv * sp, axis=0, keepdims=True),
                                -jnp.sum(dsv * sa_s[d * TC + i, p], axis=0, keepdims=True),
                                jnp.sum(dsv * vb_s[d * TC + i, p], axis=0, keepdims=True),
                                -jnp.sum(sp * dsa[p], axis=0, keepdims=True))
                        acc[d][p] = [jnp.where(row_id == u, o, a_) for o, a_ in zip(vals, acc[d][p])]
                        dss[d][p] = dsv * d8[u:u + 1, pc[p]] - dsa[p] * kk8[u:u + 1, pc[p]]
            for d in (0, 1):
                rows8 = pl.ds(pl.multiple_of(gs[d] * 8, 8), 8)
                _tiles_to_rows(dvt_s, d * NPAIR, outs[d][5], rows8)
                for p in range(NPAIR):
                    ds[d * NPAIR + p] = dss[d][p]
                    for o_ref, a_ in zip(outs[d][:5], acc[d][p]):
                        o_ref[rows8, pc[p]] = a_
            return carry

        for gj in range(ng):
            bgroup(gj, 0)

        if nsc:
            @pl.when(pl.program_id(0) == nc - 1)
            def _():
                s_finish()

    chunk = lambda k: pltpu.VMEM((k, NPAIR, HD, 128), F32)
    pairs = lambda w, dt: pltpu.VMEM((2 * NPAIR, HD, w), dt)
    res = pl.pallas_call(
        body, name="rwkv_scan_bwd", grid=(nc,), in_specs=in_specs, out_specs=out_specs, out_shape=out_shape,
        scratch_shapes=[chunk(2 * (TC + 1)), chunk(2 * TC), chunk(2 * TC), chunk(2 * TC), pairs(128, F32),
                        pairs(128, BF16), pairs(128, BF16), pairs(128, F32), pltpu.VMEM((2 * NPAIR, 128, 128), F32)]
        + _scatter_sems(nsc),
        compiler_params=pltpu.CompilerParams(dimension_semantics=("arbitrary",), vmem_limit_bytes=VMEM_LIMIT,
                                             has_side_effects=bool(nsc)),
    )(*operands)
    return [res[0:6], res[6:12]], list(res[12:])


MT = 512
MN = 256


def _merge_fwd(ya, yr, yx, wa, wr, wx, proj, gate_b):
    s = ya.shape[0]

    def body(ya_ref, yr_ref, yx_ref, wa_ref, wr_ref, wx_ref, m0, m1, m2, b0, b1, b2, o_ref):
        acc = jnp.zeros((MT, MN), F32)
        for y_ref, w_ref, m_ref, b_ref in ((ya_ref, wa_ref, m0, b0), (yr_ref, wr_ref, m1, b1), (yx_ref, wx_ref, m2, b2)):
            u = _dot(y_ref[...], w_ref[...], ((1,), (0,)))
            acc = acc + jax.nn.sigmoid(m_ref[...] + b_ref[...]) * u
        o_ref[...] = acc.astype(BF16)

    mg = lambda br: _bs((MT, MN), lambda i, j: (i, C_MG // MN + br * (D // MN) + j))
    gb = lambda br: _bs((1, MN), lambda i, j: (0, br * (D // MN) + j))
    return pl.pallas_call(
        body, name="merge_fwd", grid=(s // MT, D // MN),
        in_specs=[_bs((MT, RW), lambda i, j: (i, 0)), _bs((MT, RW), lambda i, j: (i, 0)), _bs((MT, XW), lambda i, j: (i, 0)),
                  _bs((RW, MN), lambda i, j: (0, j)), _bs((RW, MN), lambda i, j: (0, j)), _bs((XW, MN), lambda i, j: (0, j)),
                  mg(0), mg(1), mg(2), gb(0), gb(1), gb(2)],
        out_specs=_bs((MT, MN), lambda i, j: (i, j)),
        out_shape=jax.ShapeDtypeStruct((s, D), BF16),
        compiler_params=_params(("parallel", "arbitrary")),
    )(ya, yr, yx, wa, wr, wx, proj, proj, proj, gate_b, gate_b, gate_b)


def _out_fwd(merged, w_out, x, target):
    s = x.shape[0]
    tm, tn = min(512, s), 512

    def body(m_ref, w_ref, x_ref, t_ref, loss_ref, d_ref, d16_ref):
        @pl.when((pl.program_id(0) == 0) & (pl.program_id(1) == 0))
        def _():
            loss_ref[...] = jnp.zeros_like(loss_ref)

        out = x_ref[...] + jnp.dot(m_ref[...], w_ref[...], preferred_element_type=F32)
        err = out - t_ref[...]
        dout = err * (1.0 / D)
        d_ref[...] = dout
        d16_ref[...] = dout.astype(BF16)
        loss_ref[...] += jnp.sum(err * err)

    tile = _bs((tm, tn), lambda i, j: (i, j))
    return pl.pallas_call(
        body, name="out_fwd", grid=(s // tm, D // tn),
        in_specs=[_bs((tm, D), lambda i, j: (i, 0)), _bs((D, tn), lambda i, j: (0, j)), tile, tile],
        out_specs=[_bs((8, 128), lambda i, j: (0, 0)), tile, tile],
        out_shape=[jax.ShapeDtypeStruct((8, 128), F32), jax.ShapeDtypeStruct((s, D), F32),
                   jax.ShapeDtypeStruct((s, D), BF16)],
        compiler_params=_params(("arbitrary", "arbitrary")),
    )(merged, w_out, x, target)


def _merge_bwd(ya, yr, yx, wa, wr, wx, proj, gate_b, dmerged):
    s = ya.shape[0]

    def body(ya_ref, yr_ref, yx_ref, wa_ref, wr_ref, wx_ref, m0, m1, m2, b0, b1, b2, dm_ref,
             dg0, dg1, dg2, du0, du1, du2, dya_ref, dyr_ref, dyx_ref):
        @pl.when(pl.program_id(1) == 0)
        def _():
            dya_ref[...] = jnp.zeros_like(dya_ref)
            dyr_ref[...] = jnp.zeros_like(dyr_ref)
            dyx_ref[...] = jnp.zeros_like(dyx_ref)

        dm = dm_ref[...]
        branches = ((ya_ref, wa_ref, m0, b0, dg0, du0, dya_ref), (yr_ref, wr_ref, m1, b1, dg1, du1, dyr_ref),
                    (yx_ref, wx_ref, m2, b2, dg2, du2, dyx_ref))
        ws = [br[1][...] for br in branches]
        us = [_dot(br[0][...], w, ((1,), (0,))) for br, w in zip(branches, ws)]
        gts = [jax.nn.sigmoid(br[2][...] + br[3][...]) for br in branches]
        dus = [(dm * gt).astype(BF16) for gt in gts]
        for br, w, u, gt, du in zip(branches, ws, us, gts, dus):
            br[4][...] = (dm * u * gt * (1.0 - gt)).astype(BF16)
            br[5][...] = du
            br[6][...] += _dot(du, w, ((1,), (1,)))

    mg = lambda br: _bs((MT, MN), lambda i, j: (i, C_MG // MN + br * (D // MN) + j))
    gb = lambda br: _bs((1, MN), lambda i, j: (0, br * (D // MN) + j))
    tile = _bs((MT, MN), lambda i, j: (i, j))
    return pl.pallas_call(
        body, name="merge_bwd", grid=(s // MT, D // MN),
        in_specs=[_bs((MT, RW), lambda i, j: (i, 0)), _bs((MT, RW), lambda i, j: (i, 0)), _bs((MT, XW), lambda i, j: (i, 0)),
                  _bs((RW, MN), lambda i, j: (0, j)), _bs((RW, MN), lambda i, j: (0, j)), _bs((XW, MN), lambda i, j: (0, j)),
                  mg(0), mg(1), mg(2), gb(0), gb(1), gb(2), tile],
        out_specs=[tile] * 6 + [_bs((MT, RW), lambda i, j: (i, 0)), _bs((MT, RW), lambda i, j: (i, 0)),
                                _bs((MT, XW), lambda i, j: (i, 0))],
        out_shape=[jax.ShapeDtypeStruct((s, D), BF16)] * 6 + [jax.ShapeDtypeStruct((s, RW), F32),
                                                               jax.ShapeDtypeStruct((s, RW), F32),
                                                               jax.ShapeDtypeStruct((s, XW), F32)],
        compiler_params=_params(("parallel", "arbitrary")),
    )(ya, yr, yx, wa, wr, wx, proj, proj, proj, gate_b, gate_b, gate_b, dmerged)


def _colsum(a, name):
    m, n = a.shape
    tm, tn = min(512, m), 512

    def body(a_ref, o_ref):
        @pl.when(pl.program_id(1) == 0)
        def _():
            o_ref[...] = jnp.zeros_like(o_ref)

        o_ref[...] += jnp.sum(a_ref[...].astype(F32), axis=0, keepdims=True)

    return pl.pallas_call(
        body, name=name, grid=(n // tn, m // tm),
        in_specs=[_bs((tm, tn), lambda j, i: (i, j))], out_specs=_bs((1, tn), lambda j, i: (0, j)),
        out_shape=jax.ShapeDtypeStruct((1, n), F32),
        compiler_params=_params(("parallel", "arbitrary")),
    )(a)


def _in_bwd(dproj, w_in, x, g, dout, stacks=()):
    s = x.shape[0]
    tm, tk = min(512, s), 896
    nk = NIN // tk
    ni = s // tm
    n = len(stacks)

    def body(dp_ref, w_ref, x_ref, g_ref, do_ref, *rest):
        ins, (gx_ref, gg_ref), outs = rest[:n], rest[n:n + 2], rest[n + 2:2 * n + 2]
        acc = rest[2 * n + 2]
        i, kk = pl.program_id(0), pl.program_id(1)

        if n:
            start, finish = _scatter_phases(ins, outs, rest[2 * n + 3:])

        @pl.when((i == 0) & (kk == 0))
        def _():
            gg_ref[...] = jnp.zeros_like(gg_ref)
            if n:
                start()

        @pl.when(kk == 0)
        def _():
            acc[...] = jnp.zeros_like(acc)

        acc[...] += _dot(dp_ref[...], w_ref[...], ((1,), (1,)))

        @pl.when(kk == nk - 1)
        def _():
            xv, dh, gv = x_ref[...], acc[...], g_ref[...]
            r = lax.rsqrt(jnp.mean(xv * xv, axis=-1, keepdims=True) + NORM_EPS)
            xn = xv * r
            gg_ref[...] += jnp.sum(dh * xn, axis=0, keepdims=True)
            dxn = dh * gv
            dx = r * (dxn - xn * jnp.mean(dxn * xn, axis=-1, keepdims=True))
            gx_ref[...] = do_ref[...] + dx

        if n:
            @pl.when((i == ni - 1) & (kk == nk - 1))
            def _():
                finish()

    any_spec = pl.BlockSpec(memory_space=pl.ANY)
    res = pl.pallas_call(
        body, name="in_bwd", grid=(ni, nk),
        in_specs=[_bs((tm, tk), lambda i, kk: (i, kk)), _bs((D, tk), lambda i, kk: (0, kk)),
                  _bs((tm, D), lambda i, kk: (i, 0)), _bs((1, D), lambda i, kk: (0, 0)),
                  _bs((tm, D), lambda i, kk: (i, 0))] + [any_spec] * n,
        out_specs=[_bs((tm, D), lambda i, kk: (i, 0)), _bs((1, D), lambda i, kk: (0, 0))] + [any_spec] * n,
        out_shape=[jax.ShapeDtypeStruct((s, D), F32), jax.ShapeDtypeStruct((1, D), F32)] + _scatter_out_shapes(stacks),
        scratch_shapes=[pltpu.VMEM((tm, D), F32)] + _scatter_sems(n),
        compiler_params=pltpu.CompilerParams(dimension_semantics=("arbitrary", "arbitrary"),
                                             vmem_limit_bytes=VMEM_LIMIT, has_side_effects=bool(n)),
    )(dproj, w_in, x, g, dout, *stacks)
    return res[0], res[1], list(res[2:])


def _adamw_math(w, g, m, v):
    m = ADAM_B1 * m + (1.0 - ADAM_B1) * g
    v = ADAM_B2 * v + (1.0 - ADAM_B2) * jnp.square(g)
    m_hat = m / (1.0 - ADAM_B1 ** ADAM_STEP)
    v_hat = v / (1.0 - ADAM_B2 ** ADAM_STEP)
    delta = -ADAM_LR * (m_hat / (jnp.sqrt(v_hat) + ADAM_EPS) + ADAM_WD * w)
    return delta, m, v


def _adamw(parts, w, m, v, name):
    rows, cols = w.shape
    tr = rows
    for cand in (256, 128, 64, 32, 16, 8):
        if rows % cand == 0 and cand * cols * 4 <= (1 << 20):
            tr = cand
            break
    n = len(parts)

    def body(*refs):
        g = refs[0][...].astype(F32)
        for r in refs[1:n]:
            g = g + r[...].astype(F32)
        w_ref, m_ref, v_ref, g_out, d_out, m_out, v_out = refs[n:]
        delta, m_new, v_new = _adamw_math(w_ref[...], g, m_ref[...], v_ref[...])
        g_out[...] = g
        d_out[...] = delta
        m_out[...] = m_new
        v_out[...] = v_new

    spec = _bs((tr, cols), lambda i: (i, 0))
    return pl.pallas_call(
        body, name=name, grid=(rows // tr,),
        in_specs=[spec] * (n + 3), out_specs=[spec] * 4,
        out_shape=[jax.ShapeDtypeStruct((rows, cols), F32)] * 4,
        compiler_params=_params(("parallel",)),
    )(*parts, w, m, v)


def _adamw_halves(mine, theirs, core, w, m, v, name):
    rows, cols = w.shape
    h = rows // 2
    tr = next(t for t in (256, 128, 64, 32, 16, 8) if h % t == 0 and t * cols * 4 <= (1 << 20))
    nt = h // tr

    def body(core_ref, mine_ref, theirs_ref, w_ref, m_ref, v_ref, g_out, d_out, m_out, v_out):
        is_mine = pl.program_id(0) // nt == core_ref[0]
        g = jnp.where(is_mine, mine_ref[...], theirs_ref[...])
        delta, m_new, v_new = _adamw_math(w_ref[...], g, m_ref[...], v_ref[...])
        g_out[...] = g
        d_out[...] = delta
        m_out[...] = m_new
        v_out[...] = v_new

    spec = _bs((tr, cols), lambda i, core_ref: (i, 0))
    return pl.pallas_call(
        body, name=name,
        grid_spec=pltpu.PrefetchScalarGridSpec(
            num_scalar_prefetch=1, grid=(2 * nt,),
            in_specs=[_bs((tr, cols), lambda i, core_ref: (jnp.clip(i - core_ref[0] * nt, 0, nt - 1), 0)),
                      _bs((tr, cols), lambda i, core_ref: (jnp.clip(i - (1 - core_ref[0]) * nt, 0, nt - 1), 0)),
                      spec, spec, spec],
            out_specs=[spec] * 4),
        out_shape=[jax.ShapeDtypeStruct((rows, cols), F32)] * 4,
        compiler_params=_params(("parallel",)),
    )(core, mine, theirs, w, m, v)


def _sum_parts(parts, name):
    rows, cols = parts[0].shape
    tr = rows
    for cand in (256, 128, 64, 32, 16, 8):
        if rows % cand == 0 and cand * cols * 4 <= (1 << 20):
            tr = cand
            break

    def body(*refs):
        acc = refs[0][...].astype(F32)
        for r in refs[1:-1]:
            acc = acc + r[...].astype(F32)
        refs[-1][...] = acc

    spec = _bs((tr, cols), lambda i: (i, 0))
    return pl.pallas_call(
        body, name=name, grid=(rows // tr,), in_specs=[spec] * len(parts), out_specs=spec,
        out_shape=jax.ShapeDtypeStruct((rows, cols), F32), compiler_params=_params(("parallel",)),
    )(*parts)


ANY = pl.BlockSpec(memory_space=pl.ANY)


def _other_chips(x, y):
    return [(1 - x, y), (x, 1 - y), (1 - x, 1 - y)]


def _gather_shards(arrays, name):
    n = len(arrays)

    def body(*refs):
        start, forward, finish = _gather_phases(arrays, refs[:n], refs[n:2 * n], refs[2 * n:])
        start()
        forward()
        finish()

    return pl.pallas_call(
        body, name=name, in_specs=[ANY] * n, out_specs=[ANY] * n,
        out_shape=_gather_out_shapes(arrays), scratch_shapes=_gather_sems(n),
        compiler_params=pltpu.CompilerParams(has_side_effects=True),
    )(*arrays)


def _gather_out_shapes(arrays):
    return [jax.ShapeDtypeStruct((4,) + a.shape, a.dtype) for a in arrays]


def _gather_sems(n):
    dma = lambda k: pltpu.SemaphoreType.DMA((k,))
    return [dma(3 * n), dma(3 * n), dma(3 * n), dma(3 * n), dma(n), dma(n)]


def _gather_phases(arrays, ins, outs, sems):
    n = len(arrays)
    ici_send, ici_recv, d2d_send, d2d_recv, own_send, own_recv = sems

    def place():
        x, y, c = lax.axis_index("x"), lax.axis_index("y"), lax.axis_index("c")
        return x, y, c, 2 * x + y, _other_chips(x, y)

    def half(i, who):
        h = arrays[i].shape[0] // 2
        return pl.ds(who * h, h)

    def ici(i, j, src_chip, to, c):
        return pltpu.make_async_remote_copy(
            src_ref=ins[i].at[half(i, c)], dst_ref=outs[i].at[src_chip, half(i, c)], send_sem=ici_send.at[3 * i + j],
            recv_sem=ici_recv.at[3 * i + j], device_id=to, device_id_type=MESH)

    def d2d(i, j, src_chip, who, sib):
        piece = outs[i].at[src_chip, half(i, who)]
        return pltpu.make_async_remote_copy(
            src_ref=piece, dst_ref=piece, send_sem=d2d_send.at[3 * i + j], recv_sem=d2d_recv.at[3 * i + j],
            device_id=sib, device_id_type=MESH)

    def own(i, me, sib):
        return pltpu.make_async_remote_copy(
            src_ref=ins[i], dst_ref=outs[i].at[me], send_sem=own_send.at[i], recv_sem=own_recv.at[i],
            device_id=sib, device_id_type=MESH)

    def start():
        x, y, c, me, chips = place()
        for i in range(n):
            own(i, me, (x, y, 1 - c)).start()
            for j, (px, py) in enumerate(chips):
                ici(i, j, me, (px, py, c), c).start()

    def forward():
        x, y, c, me, chips = place()
        for i in range(n):
            for j, (px, py) in enumerate(chips):
                ici(i, j, 2 * px + py, (px, py, c), c).wait_recv()
                d2d(i, j, 2 * px + py, c, (x, y, 1 - c)).start()

    def finish():
        x, y, c, me, chips = place()
        sib = (x, y, 1 - c)
        for i in range(n):
            for j, (px, py) in enumerate(chips):
                d2d(i, j, 2 * px + py, 1 - c, sib).wait_recv()
            own(i, me, sib).wait_recv()
        for i in range(n):
            own(i, me, sib).wait_send()
            for j, (px, py) in enumerate(chips):
                ici(i, j, me, (px, py, c), c).wait_send()
                d2d(i, j, 2 * px + py, c, sib).wait_send()

    return start, forward, finish


def _scatter_phases(ins, outs, sems):
    send_sems, recv_sems = sems

    def copies():
        x, y, c = lax.axis_index("x"), lax.axis_index("y"), lax.axis_index("c")
        return [pltpu.make_async_remote_copy(
            src_ref=ins[a].at[2 * qx + qy], dst_ref=outs[a].at[j], send_sem=send_sems.at[3 * a + j],
            recv_sem=recv_sems.at[3 * a + j], device_id=(qx, qy, c), device_id_type=MESH)
            for a in range(len(ins)) for j, (qx, qy) in enumerate(_other_chips(x, y))]

    def start():
        for rc in copies():
            rc.start()

    def finish():
        for rc in copies():
            rc.wait_recv()
        for rc in copies():
            rc.wait_send()

    return start, finish


def _scatter_out_shapes(stacks):
    return [jax.ShapeDtypeStruct((3,) + a.shape[1:], a.dtype) for a in stacks]


def _scatter_sems(n):
    return [pltpu.SemaphoreType.DMA((3 * n,)), pltpu.SemaphoreType.DMA((3 * n,))] if n else []


def _scatter_shards(stacks, name):
    n = len(stacks)

    def body(*refs):
        ins, outs = refs[:n], refs[n:2 * n]
        send_sems, recv_sems = refs[2 * n:]
        x, y, c = lax.axis_index("x"), lax.axis_index("y"), lax.axis_index("c")
        chips = _other_chips(x, y)
        sends = []
        for i in range(n):
            for j, (px, py) in enumerate(chips):
                rc = pltpu.make_async_remote_copy(
                    src_ref=ins[i].at[2 * px + py], dst_ref=outs[i].at[j], send_sem=send_sems.at[3 * i + j],
                    recv_sem=recv_sems.at[3 * i + j], device_id=(px, py, c), device_id_type=MESH)
                rc.start()
                sends.append(rc)
        for rc in sends:
            rc.wait_recv()
        for rc in sends:
            rc.wait_send()

    return pl.pallas_call(
        body, name=name, in_specs=[ANY] * n, out_specs=[ANY] * n,
        out_shape=[jax.ShapeDtypeStruct((3,) + a.shape[1:], a.dtype) for a in stacks],
        scratch_shapes=[pltpu.SemaphoreType.DMA((3 * n,)), pltpu.SemaphoreType.DMA((3 * n,))],
        compiler_params=pltpu.CompilerParams(has_side_effects=True),
    )(*stacks)


def _pair_exchange(stacks, name):
    n = len(stacks)

    def body(*refs):
        ins, outs = refs[:n], refs[n:2 * n]
        send_sems, recv_sems = refs[2 * n:]
        x, y, c = lax.axis_index("x"), lax.axis_index("y"), lax.axis_index("c")
        cps = []
        for i in range(n):
            h = stacks[i].shape[1] // 2
            rc = pltpu.make_async_remote_copy(
                src_ref=ins[i].at[:, pl.ds((1 - c) * h, h)], dst_ref=outs[i], send_sem=send_sems.at[i],
                recv_sem=recv_sems.at[i], device_id=(x, y, 1 - c), device_id_type=MESH)
            rc.start()
            cps.append(rc)
        for rc in cps:
            rc.wait_recv()
        for rc in cps:
            rc.wait_send()

    return pl.pallas_call(
        body, name=name, in_specs=[ANY] * n, out_specs=[ANY] * n,
        out_shape=[jax.ShapeDtypeStruct((4, a.shape[1] // 2) + a.shape[2:], a.dtype) for a in stacks],
        scratch_shapes=[pltpu.SemaphoreType.DMA((n,)), pltpu.SemaphoreType.DMA((n,))],
        compiler_params=pltpu.CompilerParams(has_side_effects=True),
    )(*stacks)


def _pair_sum(own, theirs, core, name):
    _, r, cols = own.shape
    h = r // 2
    tr = next(t for t in (256, 128, 64, 32, 16) if h % t == 0 and t * cols * 4 <= (1 << 20))
    nt = h // tr

    def body(core_ref, own_ref, th_ref, o32_ref, o16_ref):
        del core_ref
        acc = own_ref[...] + th_ref[...].astype(F32)
        o32_ref[...] = acc
        o16_ref[...] = acc.astype(BF16)

    out = _bs((1, tr, cols), lambda j, t, core_ref: (j, t, 0))
    return pl.pallas_call(
        body, name=name,
        grid_spec=pltpu.PrefetchScalarGridSpec(
            num_scalar_prefetch=1, grid=(4, nt),
            in_specs=[_bs((1, tr, cols), lambda j, t, core_ref: (j, core_ref[0] * nt + t, 0)), out],
            out_specs=[out, out]),
        out_shape=[jax.ShapeDtypeStruct((4, h, cols), F32), jax.ShapeDtypeStruct((4, h, cols), BF16)],
        compiler_params=_params(("parallel", "parallel")),
    )(core, own, theirs)


def _swap_sibling(arrays, name):
    n = len(arrays)

    def body(*refs):
        ins, outs = refs[:n], refs[n:2 * n]
        send_sems, recv_sems = refs[2 * n:]
        sib = (lax.axis_index("x"), lax.axis_index("y"), 1 - lax.axis_index("c"))
        cps = []
        for i in range(n):
            rc = pltpu.make_async_remote_copy(src_ref=ins[i], dst_ref=outs[i], send_sem=send_sems.at[i],
                                              recv_sem=recv_sems.at[i], device_id=sib, device_id_type=MESH)
            rc.start()
            cps.append(rc)
        for rc in cps:
            rc.wait_recv()
        for rc in cps:
            rc.wait_send()

    return pl.pallas_call(
        body, name=name, in_specs=[ANY] * n, out_specs=[ANY] * n,
        out_shape=[jax.ShapeDtypeStruct(a.shape, a.dtype) for a in arrays],
        scratch_shapes=[pltpu.SemaphoreType.DMA((n,)), pltpu.SemaphoreType.DMA((n,))],
        compiler_params=pltpu.CompilerParams(has_side_effects=True),
    )(*arrays)


def _all_reduce_small(v):
    rows = v.shape[0]
    h = rows // 2

    def body(v_ref, o_ref, sib, pair, buf, send_sems, recv_sems):
        x, y, c = lax.axis_index("x"), lax.axis_index("y"), lax.axis_index("c")
        me = 2 * x + y
        sibling = (x, y, 1 - c)
        mine = pl.ds(pl.multiple_of(c * h, 8), h)
        theirs = pl.ds(pl.multiple_of((1 - c) * h, 8), h)

        def copy(src, dst, k, to):
            return pltpu.make_async_remote_copy(src_ref=src, dst_ref=dst, send_sem=send_sems.at[k],
                                                recv_sem=recv_sems.at[k], device_id=to, device_id_type=MESH)

        swap = copy(v_ref, sib, 0, sibling)
        swap.start()
        swap.wait_recv()
        pair[...] = v_ref[...] + sib[...]
        buf[me] = pair[mine, :]
        out = [copy(buf.at[me], buf.at[me], 1 + j, (px, py, c)) for j, (px, py) in enumerate(_other_chips(x, y))]
        for rc in out:
            rc.start()
        for j, (px, py) in enumerate(_other_chips(x, y)):
            copy(buf.at[me], buf.at[2 * px + py], 1 + j, (px, py, c)).wait_recv()
        o_ref[mine, :] = (buf[0] + buf[1]) + (buf[2] + buf[3])
        back = copy(o_ref.at[mine], o_ref.at[mine], 4, sibling)
        back.start()
        copy(o_ref.at[theirs], o_ref.at[theirs], 4, sibling).wait_recv()
        for rc in [swap, back] + out:
            rc.wait_send()

    vmem = pl.BlockSpec(memory_space=pltpu.VMEM)
    return pl.pallas_call(
        body, name="all_reduce_small", in_specs=[vmem], out_specs=vmem,
        out_shape=jax.ShapeDtypeStruct((rows, 128), F32),
        scratch_shapes=[pltpu.VMEM((rows, 128), F32), pltpu.VMEM((rows, 128), F32), pltpu.VMEM((4, h, 128), F32),
                        pltpu.SemaphoreType.DMA((5,)), pltpu.SemaphoreType.DMA((5,))],
        compiler_params=pltpu.CompilerParams(has_side_effects=True, vmem_limit_bytes=VMEM_LIMIT),
    )(v)


def _rope_tables(s):
    half = HD // 2
    inv = 10000.0 ** (-jnp.arange(half, dtype=F32) / half)
    ang = jnp.arange(s, dtype=F32)[:, None] * inv[None, :]
    cos, sin = jnp.cos(ang), jnp.sin(ang)
    return jnp.concatenate([cos, cos], axis=1), jnp.concatenate([sin, sin], axis=1)


LATE = ['attn_w_o', 'rwkv_w_o', 'x_w_kv', 'x_w_o', 'w_out']


def _local_step(x, mem, target, norm_g, mem_norm_g, w_in, gate_b, gq, gk, sink, wa, mu, k_k, k_a, r_k, w0, w2, a0, a2,
                ln_w, ln_b, wr, w_kv, gxq, gxk, wx, w_out, late_shards=None, early_exchange=None):
    s = x.shape[0]
    cos, sin = _rope_tables(s)
    r_k = r_k.reshape(1, RW)

    proj, h = _proj_fwd(x, norm_g, w_in)
    ya = _attn_fwd(proj, cos, sin, gq, gk, sink)
    ps = _shift_fwd(proj, mu)
    kk, dec0, kd0, b0, dec1, kd1, b1 = _pre_fwd(ps, k_k, k_a, w0, w2, a0, a2)
    ((y0, ck0), (y1, ck1)), stacks = _scan2_fwd([(dec0, kd0, b0), (dec1, kd1, b1)], ps, kk, gather=late_shards or ())
    if late_shards:
        st = dict(zip(LATE, stacks))
        wa, wr, wx = (_unshard_cols(st[n]) for n in ('attn_w_o', 'rwkv_w_o', 'x_w_o'))
        w_kv, w_out = st['x_w_kv'].reshape(D, 2 * XW), st['w_out'].reshape(D, D)
    mkv, mn = _mem_kv(mem, mem_norm_g, w_kv)
    yx = _xattn_fwd(proj, mkv, gxq, gxk)
    yr = _post_fwd(y0, y1, ps, kd0, kd1, proj, r_k, ln_w, ln_b)
    merged = _merge_fwd(ya, yr, yx, wa, wr, wx, proj, gate_b)
    loss_tile, dout, dout16 = _out_fwd(merged, w_out, x, target)
    loss_sum = loss_tile[0, 0]

    g = {}
    t16 = lambda a: a.astype(BF16).T
    sk = min(1024, s)
    dmerged = _matmul(dout16, w_out, mode="nt", m=s, n=D, k=D, tm=sk, tn=1024, tk=1024, name="dmerged")
    g["w_out"] = _matmul(merged.T, dout16, mode="nn", m=D, n=D, k=s, tm=1024, tn=1024, tk=sk, name="grad_w_out")
    dg0, dg1, dg2, du0, du1, du2, dya, dyr, dyx = _merge_bwd(ya, yr, yx, wa, wr, wx, proj, gate_b, dmerged)
    g["attn_w_o"] = _matmul(t16(ya), du0, mode="nn", m=RW, n=D, k=s, tm=RW, tn=1024, tk=s, name="grad_attn_w_o")
    g["rwkv_w_o"] = _matmul(t16(yr), du1, mode="nn", m=RW, n=D, k=s, tm=RW, tn=1024, tk=s, name="grad_rwkv_w_o")
    g["x_w_o"] = _matmul(t16(yx), du2, mode="nn", m=XW, n=D, k=s, tm=XW, tn=1024, tk=s, name="grad_x_w_o")
    dmg = jnp.concatenate([dg0, dg1, dg2], axis=1)
    g["gate_b"] = _colsum(dmg, "grad_gate_b")

    daq, dak, dav, dag, g["attn_q_norm_g"], g["attn_k_norm_g"], g["attn_sink"] = _attn_bwd(proj, cos, sin, gq, gk, sink, dya)

    dxq, dxg, dmkv, g["x_q_norm_g"], g["x_k_norm_g"] = _xattn_bwd(proj, mkv, gxq, gxk, dyx)
    g["x_w_kv"] = _matmul(mn, dmkv, mode="tn", m=D, n=2 * XW, k=NMEM, tm=512, tn=512, tk=NMEM, name="grad_x_w_kv")
    dmn = _matmul(dmkv, w_kv, mode="nt", m=NMEM, n=D, k=2 * XW, tm=NMEM, tn=512, tk=2 * XW, name="dmn")
    g["mem_norm_g"] = _mem_bwd(mem, mem_norm_g, dmn)

    dys, dr_p, dv_p, dkd0_p, dkd1_p, drg, g["rwkv_r_k"], g["rwkv_ln_w"], g["rwkv_ln_b"] = _post_bwd(
        y0, y1, ps, kd0, kd1, proj, r_k, ln_w, ln_b, dyr)
    sent = early_exchange(g) if early_exchange else ()
    ((dr0, dd0, db0, dk0, dkk0, dv0), (dr1, dd1, db1, dk1, dkk1, dv1)), received = _scan2_bwd(
        [(dec0, kd0, b0, ck0), (dec1, kd1, b1, ck1)], ps, kk, dys, scatter=sent)
    dr = dr_p + dr0 + dr1
    dv = dv_p + dv0 + dv1
    cts = (dkk0 + dkk1, dd0, dk0 + dkd0_p, db0, dd1, dk1 + dkd1_p, db1)
    dps, g["rwkv_k_k"], g["rwkv_k_a"], g["rwkv_w0"], g["rwkv_w2"], g["rwkv_a0"], g["rwkv_a2"] = _pre_bwd(
        ps, k_k, k_a, w0, w2, a0, a2, dr, dv, cts)
    drs, g["rwkv_mu"] = _shift_bwd(proj, mu, dps)

    dproj = jnp.concatenate([daq.astype(BF16), dak.astype(BF16), dav.astype(BF16), dag.astype(BF16), drs.astype(BF16),
                             drg.astype(BF16), dxq.astype(BF16), dxg.astype(BF16), dmg], axis=1)
    dproj4 = jnp.stack([dproj[:, j * (NIN // 4):(j + 1) * (NIN // 4)] for j in range(4)])
    g["w_in"], g["w_in_bf16"] = _grad_w_in(h.T, dproj4)
    g["rwkv_r_k"] = g["rwkv_r_k"].reshape(AH, HD)
    return loss_sum, g, (dproj, w_in, x, norm_g, dout), received


WEIGHTS = ['norm_g', 'mem_norm_g', 'w_in', 'gate_b', 'attn_q_norm_g', 'attn_k_norm_g', 'attn_sink', 'attn_w_o',
           'rwkv_mu', 'rwkv_k_k', 'rwkv_k_a', 'rwkv_r_k', 'rwkv_w0', 'rwkv_w2', 'rwkv_a0', 'rwkv_a2', 'rwkv_ln_w',
           'rwkv_ln_b', 'rwkv_w_o', 'x_w_kv', 'x_q_norm_g', 'x_k_norm_g', 'x_w_o', 'w_out']
BIG = ['w_in', 'attn_w_o', 'rwkv_w_o', 'x_w_kv', 'x_w_o', 'w_out']
COL_SHARDED = ['w_in', 'attn_w_o', 'rwkv_w_o', 'x_w_o']
LORA = ['rwkv_w0', 'rwkv_w2', 'rwkv_a0', 'rwkv_a2']
SMALL = [n for n in WEIGHTS if n not in BIG]


def _unshard_cols(stack):
    return jnp.concatenate([stack[i] for i in range(4)], axis=-1)


def _shard_cols(full):
    w = full.shape[-1] // 4
    return [full[..., i * w:(i + 1) * w] for i in range(4)]


def kernel(x, mem, norm_g, mem_norm_g, w_in, gate_b, attn_q_norm_g, attn_k_norm_g, attn_sink, attn_w_o, rwkv_mu, rwkv_k_k, rwkv_k_a, rwkv_r_k, rwkv_w0, rwkv_w2, rwkv_a0, rwkv_a2, rwkv_ln_w, rwkv_ln_b, rwkv_w_o, x_w_kv, x_q_norm_g, x_k_norm_g, x_w_o, w_out, loss_target, m_norm_g, m_mem_norm_g, m_w_in, m_gate_b, m_attn_q_norm_g, m_attn_k_norm_g, m_attn_sink, m_attn_w_o, m_rwkv_mu, m_rwkv_k_k, m_rwkv_k_a, m_rwkv_r_k, m_rwkv_w0, m_rwkv_w2, m_rwkv_a0, m_rwkv_a2, m_rwkv_ln_w, m_rwkv_ln_b, m_rwkv_w_o, m_x_w_kv, m_x_q_norm_g, m_x_k_norm_g, m_x_w_o, m_w_out, v_norm_g, v_mem_norm_g, v_w_in, v_gate_b, v_attn_q_norm_g, v_attn_k_norm_g, v_attn_sink, v_attn_w_o, v_rwkv_mu, v_rwkv_k_k, v_rwkv_k_a, v_rwkv_r_k, v_rwkv_w0, v_rwkv_w2, v_rwkv_a0, v_rwkv_a2, v_rwkv_ln_w, v_rwkv_ln_b, v_rwkv_w_o, v_x_w_kv, v_x_q_norm_g, v_x_k_norm_g, v_x_w_o, v_w_out):
    args = dict(locals())
    canon = lambda a: a[0] if a.ndim > 2 else a
    w = {n: canon(args[n]) for n in WEIGHTS}
    m = {n: canon(args["m_" + n]) for n in WEIGHTS}
    v = {n: canon(args["v_" + n]) for n in WEIGHTS}
    shard = 2 * lax.axis_index("x") + lax.axis_index("y")

    now = ["w_in"] + LORA
    local = [w["w_in"].astype(BF16)] + [w[n].reshape(2, -1, w[n].shape[-1]) for n in LORA]
    stacks = dict(zip(now, _gather_shards(local, "gather_weights")))
    full = {"w_in": _unshard_cols(stacks["w_in"])}
    for n in LORA:
        full[n] = _unshard_cols(stacks[n]).reshape(w[n].shape[:-1] + (RW,))

    core = lax.axis_index("c").astype(jnp.int32).reshape(1)
    pair32 = {}

    def as_stack(g, n, dtype):
        if n == "w_in":
            return g["w_in"] if dtype == F32 else g["w_in_bf16"]
        if n in COL_SHARDED:
            return jnp.stack([p.astype(dtype) for p in _shard_cols(g[n])])
        return g[n].reshape((4, g[n].shape[0] // 4) + g[n].shape[1:]).astype(dtype)

    def pair_sums(g, names, tag):
        sibling = _pair_exchange([as_stack(g, n, BF16) for n in names], "pair_exchange_" + tag)
        sent = []
        for n, th in zip(names, sibling):
            pair32[n], a16 = _pair_sum(as_stack(g, n, F32), th, core, "pair_sum_" + n)
            sent.append(a16)
        return sent

    loss_sum, g, deferred, recv_late = _local_step(
        x[0], mem[0], loss_target[0], w["norm_g"], w["mem_norm_g"], full["w_in"], w["gate_b"], w["attn_q_norm_g"],
        w["attn_k_norm_g"], w["attn_sink"], None, w["rwkv_mu"], w["rwkv_k_k"], w["rwkv_k_a"], w["rwkv_r_k"],
        full["rwkv_w0"], full["rwkv_w2"], full["rwkv_a0"], full["rwkv_a2"], w["rwkv_ln_w"], w["rwkv_ln_b"],
        None, None, w["x_q_norm_g"], w["x_k_norm_g"], None, None,
        late_shards=[w[n].astype(BF16) for n in LATE], early_exchange=lambda g: pair_sums(g, LATE, "late"))

    loss = lax.psum(0.5 * loss_sum / D, ("x", "y", "c"))

    grad_x, g["norm_g"], recv_w_in = _in_bwd(*deferred, stacks=pair_sums(g, ["w_in"], "w_in"))
    halves = []
    for n, r in zip(BIG, recv_w_in + recv_late):
        own = lax.dynamic_index_in_dim(pair32[n], shard, 0, keepdims=False)
        halves.append(_sum_parts([own, r[0], r[1], r[2]], "sum_" + n))
    other_halves = _swap_sibling(halves, "swap_halves")

    out_g, out_d, out_m, out_v = {}, {}, {}, {}
    for n, mine, theirs in zip(BIG, halves, other_halves):
        out_g[n], out_d[n], out_m[n], out_v[n] = _adamw_halves(mine, theirs, core, w[n], m[n], v[n], "adamw_" + n)

    flat = jnp.concatenate([g[n].reshape(-1) for n in SMALL])
    total = flat.shape[0]
    padded = -(-total // 2048) * 2048
    flat = jnp.pad(flat, (0, padded - total)).reshape(padded // 128, 128)
    red = _all_reduce_small(flat).reshape(-1)
    off = 0
    gs = {}
    for n in SMALL:
        size = g[n].size
        t = red[off:off + size].reshape(g[n].shape)
        off += size
        if n in LORA:
            wd = t.shape[-1] // 4
            t = lax.dynamic_slice_in_dim(t, shard * wd, wd, axis=t.ndim - 1)
        gs[n] = t

    def pack(d):
        f = jnp.concatenate([d[n].reshape(-1) for n in SMALL])
        return jnp.pad(f, (0, -(-f.shape[0] // 1024) * 1024 - f.shape[0])).reshape(-1, 128)

    pg, pd, pm, pv = _adamw([pack(gs)], pack(w), pack(m), pack(v), "adamw_small")
    off = 0
    for n in SMALL:
        size = w[n].size
        for dst, src in ((out_g, pg), (out_d, pd), (out_m, pm), (out_v, pv)):
            dst[n] = src.reshape(-1)[off:off + size].reshape(w[n].shape)
        off += size

    lead = lambda d: [d[n][None] if args[n].ndim > 2 else d[n] for n in WEIGHTS]
    return (loss, grad_x[None], *lead(out_g), *lead(out_d), *lead(out_m), *lead(out_v))
```

```python
import jax
import jax.numpy as jnp
from jax import lax
from jax.experimental import pallas as pl
from jax.experimental.pallas import tpu as pltpu

F32 = jnp.float32
BF16 = jnp.bfloat16
HI = lax.Precision.HIGH
MESH = pl.DeviceIdType.MESH

D = 2048
NMEM = 256
NORM_EPS = 1e-6
NEG_INF = -1e30
GN_EPS = 64e-5
HD = 64
AH = 12
AKV = 4
RW = 768
XH = 4
XD = 128
XW = 512
NIN = 12544
RSW = 2560
C_AQ, C_AK, C_AV, C_AG, C_RS, C_RG, C_XQ, C_XG, C_MG = 0, 768, 1024, 1280, 2048, 4608, 5376, 5888, 6400
WIN = 384
QB = 128
TC = 16
NPAIR = 6

ADAM_LR, ADAM_B1, ADAM_B2, ADAM_EPS, ADAM_WD, ADAM_STEP = 0.001, 0.9, 0.999, 1e-08, 0.01, 10

VMEM_LIMIT = 56 * 1024 * 1024


def _bs(shape, imap):
    return pl.BlockSpec(shape, imap)


def _params(sem=None, vmem=VMEM_LIMIT):
    return pltpu.CompilerParams(dimension_semantics=sem, vmem_limit_bytes=vmem)


def _dot(a, b, dims):
    return lax.dot_general(a.astype(BF16), b.astype(BF16), (dims, ((), ())), preferred_element_type=F32)


@jax.custom_vjp
def _mm_nn(a, b):
    return _dot(a, b, ((1,), (0,)))


def _mm_nn_fwd(a, b):
    return _mm_nn(a, b), (a, b)


def _mm_nn_bwd(res, ct):
    a, b = res
    return _dot(ct, b, ((1,), (1,))), _dot(a, ct, ((0,), (0,)))


_mm_nn.defvjp(_mm_nn_fwd, _mm_nn_bwd)


@jax.custom_vjp
def _mm_nt(a, b):
    return _dot(a, b, ((1,), (1,)))


def _mm_nt_fwd(a, b):
    return _mm_nt(a, b), (a, b)


def _mm_nt_bwd(res, ct):
    a, b = res
    return _dot(ct, b, ((1,), (0,))), _dot(ct, a, ((0,), (0,)))


_mm_nt.defvjp(_mm_nt_fwd, _mm_nt_bwd)


def _seg_matrix(n, seg):
    r = lax.broadcasted_iota(jnp.int32, (n, n), 0) // seg
    c = lax.broadcasted_iota(jnp.int32, (n, n), 1) // seg
    return (r == c).astype(F32)


def _rot_matrix():
    r = lax.broadcasted_iota(jnp.int32, (HD, HD), 0)
    c = lax.broadcasted_iota(jnp.int32, (HD, HD), 1)
    return jnp.where(c == r + HD // 2, 1.0, 0.0).astype(F32) - jnp.where(c == r - HD // 2, 1.0, 0.0).astype(F32)


def _hdot(a, m):
    return jnp.dot(a, m, precision=HI, preferred_element_type=F32)


def _rms(t, g):
    return t * lax.rsqrt(jnp.mean(t * t, axis=-1, keepdims=True) + NORM_EPS) * g


def _silu(t):
    return t * jax.nn.sigmoid(t)


def _softplus(z):
    return jnp.maximum(z, 0.0) + jnp.log(1.0 + jnp.exp(-jnp.abs(z)))


def _matmul(a, b, *, mode, m, n, k, tm, tn, tk, name, a_off=(0, 0), b_off=(0, 0), out_dtype=F32, shards=0, twin=False):
    nk = k // tk
    if mode == "tn":
        a_spec = _bs((tk, tm), lambda i, j, kk: (kk + a_off[0], i + a_off[1]))
        dims = ((0,), (0,))
    else:
        a_spec = _bs((tm, tk), lambda i, j, kk: (i + a_off[0], kk + a_off[1]))
        dims = ((1,), (1,)) if mode == "nt" else ((1,), (0,))
    if mode == "nt":
        b_spec = _bs((tn, tk), lambda i, j, kk: (j + b_off[0], kk + b_off[1]))
    else:
        b_spec = _bs((tk, tn), lambda i, j, kk: (kk + b_off[0], j + b_off[1]))
    if shards:
        per = n // shards // tn
        o_spec = _bs((1, tm, tn), lambda i, j, kk: (j // per, i, j % per))
        o_shape = (shards, m, n // shards)
    else:
        o_spec = _bs((tm, tn), lambda i, j, kk: (i, j))
        o_shape = (m, n)

    def body(a_ref, b_ref, *rest):
        o_refs, acc = rest[:-1], rest[-1]
        kk = pl.program_id(2)

        @pl.when(kk == 0)
        def _():
            acc[...] = jnp.zeros_like(acc)

        acc[...] += _dot(a_ref[...], b_ref[...], dims)

        @pl.when(kk == nk - 1)
        def _():
            for o_ref in o_refs:
                o_ref[...] = acc[...].astype(o_ref.dtype).reshape(o_ref.shape)

    dtypes = [out_dtype, BF16] if twin else [out_dtype]
    res = pl.pallas_call(
        body, name=name, grid=(m // tm, n // tn, nk),
        in_specs=[a_spec, b_spec], out_specs=[o_spec] * len(dtypes),
        out_shape=[jax.ShapeDtypeStruct(o_shape, dt) for dt in dtypes],
        scratch_shapes=[pltpu.VMEM((tm, tn), F32)],
        compiler_params=_params(("parallel", "parallel", "arbitrary")),
    )(a, b)
    return res if twin else res[0]


def _grad_w_in(ht, dproj4):
    s = ht.shape[1]
    ws = NIN // 4
    tm, tk = 256, s
    nk = s // tk

    def body(a_ref, b_ref, o32_ref, o16_ref, acc):
        kk = pl.program_id(2)

        @pl.when(kk == 0)
        def _():
            acc[...] = jnp.zeros_like(acc)

        acc[...] += jnp.dot(a_ref[...], b_ref[0], preferred_element_type=F32)

        @pl.when(kk == nk - 1)
        def _():
            o32_ref[0] = acc[...]
            o16_ref[0] = acc[...].astype(BF16)

    out = _bs((1, tm, ws), lambda j, i, kk: (j, i, 0))
    return pl.pallas_call(
        body, name="grad_w_in", grid=(4, D // tm, nk),
        in_specs=[_bs((tm, tk), lambda j, i, kk: (i, kk)), _bs((1, tk, ws), lambda j, i, kk: (j, kk, 0))],
        out_specs=[out, out],
        out_shape=[jax.ShapeDtypeStruct((4, D, ws), F32), jax.ShapeDtypeStruct((4, D, ws), BF16)],
        scratch_shapes=[pltpu.VMEM((tm, ws), F32)],
        compiler_params=_params(("parallel", "parallel", "arbitrary")),
    )(ht, dproj4)


def _proj_fwd(x, g, w):
    s = x.shape[0]
    tm, tn = min(1024, s), 896

    def body(x_ref, g_ref, w_ref, o_ref, h_ref, hs):
        @pl.when(pl.program_id(1) == 0)
        def _():
            h = _rms(x_ref[...], g_ref[...]).astype(BF16)
            hs[...] = h
            h_ref[...] = h

        o_ref[...] = jnp.dot(hs[...], w_ref[...], preferred_element_type=F32)

    return pl.pallas_call(
        body, name="proj_fwd", grid=(s // tm, NIN // tn),
        in_specs=[_bs((tm, D), lambda i, j: (i, 0)), _bs((1, D), lambda i, j: (0, 0)), _bs((D, tn), lambda i, j: (0, j))],
        out_specs=[_bs((tm, tn), lambda i, j: (i, j)), _bs((tm, D), lambda i, j: (i, 0))],
        out_shape=[jax.ShapeDtypeStruct((s, NIN), F32), jax.ShapeDtypeStruct((s, D), BF16)],
        scratch_shapes=[pltpu.VMEM((tm, D), BF16)],
        compiler_params=_params(("parallel", "arbitrary")),
    )(x, g, w)


def _rope(t, cos, sin, rot):
    return t * cos + _hdot(t, rot) * sin


def _attn_tile(qs, ks, vs, gs, sinks, gq, gk, cq, sq, ck, sk, mask, rot):
    heads = range(AH)
    kv = [h // (AH // AKV) for h in heads]
    kh = [_rope(_rms(ks[j], gk), ck, sk, rot) for j in range(AKV)]
    qh = [_rope(_rms(qs[h], gq), cq, sq, rot) for h in heads]
    sc = [jnp.where(mask, _mm_nt(qh[h], kh[kv[h]]) * (HD ** -0.5), NEG_INF) for h in heads]
    mx = [lax.stop_gradient(jnp.maximum(jnp.max(sc[h], axis=-1, keepdims=True), sinks[h])) for h in heads]
    p = [jnp.exp(sc[h] - mx[h]) for h in heads]
    den = [jnp.sum(p[h], axis=-1, keepdims=True) + jnp.exp(sinks[h] - mx[h]) for h in heads]
    o = [_mm_nn(p[h] / den[h], vs[kv[h]]) for h in heads]
    return [o[h] * _silu(gs[h]) for h in heads]


def _attn_load(n, s, aq_ref, ak_ref, av_ref, ag_refs, cos_ref, sin_ref, sink_ref):
    start = pl.multiple_of(jnp.clip((n - 1) * QB, 0, s - WIN), QB)
    q0 = pl.multiple_of(n * QB, QB)
    qs = [aq_ref[:, h * HD:(h + 1) * HD] for h in range(AH)]
    ks = [ak_ref[pl.ds(start, WIN), h * HD:(h + 1) * HD] for h in range(AKV)]
    vs = [av_ref[pl.ds(start, WIN), h * HD:(h + 1) * HD] for h in range(AKV)]
    gs = [ag_refs[h // 4][:, (h % 4) * HD:(h % 4 + 1) * HD] for h in range(AH)]
    sinks = [sink_ref[0:1, h:h + 1] for h in range(AH)]
    cq, sq = cos_ref[pl.ds(q0, QB), :], sin_ref[pl.ds(q0, QB), :]
    ck, sk = cos_ref[pl.ds(start, WIN), :], sin_ref[pl.ds(start, WIN), :]
    qpos = q0 + lax.broadcasted_iota(jnp.int32, (QB, WIN), 0)
    kpos = start + lax.broadcasted_iota(jnp.int32, (QB, WIN), 1)
    mask = jnp.abs(kpos - qpos) <= QB
    return start, qs, ks, vs, gs, sinks, cq, sq, ck, sk, mask


def _attn_specs(s):
    return [
        _bs((QB, 768), lambda n: (n, 0)),
        _bs((s, 256), lambda n: (0, C_AK // 256)),
        _bs((s, 256), lambda n: (0, C_AV // 256)),
        _bs((QB, 256), lambda n: (n, C_AG // 256)),
        _bs((QB, 256), lambda n: (n, C_AG // 256 + 1)),
        _bs((QB, 256), lambda n: (n, C_AG // 256 + 2)),
        _bs((s, HD), lambda n: (0, 0)),
        _bs((s, HD), lambda n: (0, 0)),
        _bs((1, HD), lambda n: (0, 0)),
        _bs((1, HD), lambda n: (0, 0)),
        _bs((1, AH), lambda n: (0, 0)),
    ]


def _attn_fwd(proj, cos, sin, gq, gk, sink):
    s = proj.shape[0]

    def body(aq_ref, ak_ref, av_ref, ag0, ag1, ag2, cos_ref, sin_ref, gq_ref, gk_ref, sink_ref, o_ref):
        n = pl.program_id(0)
        _, qs, ks, vs, gs, sinks, cq, sq, ck, sk, mask = _attn_load(
            n, s, aq_ref, ak_ref, av_ref, (ag0, ag1, ag2), cos_ref, sin_ref, sink_ref)
        outs = _attn_tile(qs, ks, vs, gs, sinks, gq_ref[...], gk_ref[...], cq, sq, ck, sk, mask, _rot_matrix())
        for h in range(AH):
            o_ref[:, h * HD:(h + 1) * HD] = outs[h]

    return pl.pallas_call(
        body, name="attn_fwd", grid=(s // QB,),
        in_specs=_attn_specs(s), out_specs=_bs((QB, 768), lambda n: (n, 0)),
        out_shape=jax.ShapeDtypeStruct((s, 768), F32),
        compiler_params=_params(("arbitrary",)),
    )(proj, proj, proj, proj, proj, proj, cos, sin, gq, gk, sink)


def _attn_bwd(proj, cos, sin, gq, gk, sink, dy):
    s = proj.shape[0]

    def body(aq_ref, ak_ref, av_ref, ag0, ag1, ag2, cos_ref, sin_ref, gq_ref, gk_ref, sink_ref, dy_ref,
             daq_ref, dak_ref, dav_ref, dag_ref, dgq_ref, dgk_ref, dsink_ref):
        n = pl.program_id(0)

        @pl.when(n == 0)
        def _():
            dak_ref[...] = jnp.zeros_like(dak_ref)
            dav_ref[...] = jnp.zeros_like(dav_ref)
            dgq_ref[...] = jnp.zeros_like(dgq_ref)
            dgk_ref[...] = jnp.zeros_like(dgk_ref)
            dsink_ref[...] = jnp.zeros_like(dsink_ref)

        start, qs, ks, vs, gs, sinks, cq, sq, ck, sk, mask = _attn_load(
            n, s, aq_ref, ak_ref, av_ref, (ag0, ag1, ag2), cos_ref, sin_ref, sink_ref)
        rot = _rot_matrix()

        def f(qs, ks, vs, gs, sinks, gq, gk):
            return _attn_tile(qs, ks, vs, gs, sinks, gq, gk, cq, sq, ck, sk, mask, rot)

        _, vjp = jax.vjp(f, qs, ks, vs, gs, sinks, gq_ref[...], gk_ref[...])
        dys = [dy_ref[:, h * HD:(h + 1) * HD] for h in range(AH)]
        dqs, dks, dvs, dgs, dsinks, dgq, dgk = vjp(dys)
        for h in range(AH):
            daq_ref[:, h * HD:(h + 1) * HD] = dqs[h]
            dag_ref[:, h * HD:(h + 1) * HD] = dgs[h]
            dsink_ref[0:1, h:h + 1] += dsinks[h]
        for h in range(AKV):
            dak_ref[pl.ds(start, WIN), h * HD:(h + 1) * HD] += dks[h]
            dav_ref[pl.ds(start, WIN), h * HD:(h + 1) * HD] += dvs[h]
        dgq_ref[...] += dgq
        dgk_ref[...] += dgk

    whole = lambda shape: _bs(shape, lambda n: (0, 0))
    return pl.pallas_call(
        body, name="attn_bwd", grid=(s // QB,),
        in_specs=_attn_specs(s) + [_bs((QB, 768), lambda n: (n, 0))],
        out_specs=[_bs((QB, 768), lambda n: (n, 0)), whole((s, 256)), whole((s, 256)), _bs((QB, 768), lambda n: (n, 0)),
                   whole((1, HD)), whole((1, HD)), whole((1, AH))],
        out_shape=[jax.ShapeDtypeStruct((s, 768), F32), jax.ShapeDtypeStruct((s, 256), F32),
                   jax.ShapeDtypeStruct((s, 256), F32), jax.ShapeDtypeStruct((s, 768), F32),
                   jax.ShapeDtypeStruct((1, HD), F32), jax.ShapeDtypeStruct((1, HD), F32),
                   jax.ShapeDtypeStruct((1, AH), F32)],
        compiler_params=_params(("arbitrary",)),
    )(proj, proj, proj, proj, proj, proj, cos, sin, gq, gk, sink, dy)


def _mem_kv(mem, g, w):
    def body(m_ref, g_ref, w_ref, o_ref, mn_ref):
        mn = _rms(m_ref[...], g_ref[...]).astype(BF16)
        mn_ref[...] = mn
        o_ref[...] = jnp.dot(mn, w_ref[...], preferred_element_type=F32)

    return pl.pallas_call(
        body, name="mem_kv",
        out_shape=[jax.ShapeDtypeStruct((NMEM, 2 * XW), F32), jax.ShapeDtypeStruct((NMEM, D), BF16)],
        compiler_params=_params(),
    )(mem, g, w)


def _xattn_tile(qs, gs, kms, vms, gxq, gxk):
    heads = range(XH)
    q = [_rms(qs[h], gxq) for h in heads]
    km = [_rms(kms[h], gxk) for h in heads]
    sc = [_mm_nt(q[h], km[h]) * (XD ** -0.5) for h in heads]
    p = [jnp.exp(sc[h] - lax.stop_gradient(jnp.max(sc[h], axis=-1, keepdims=True))) for h in heads]
    p = [p[h] / jnp.sum(p[h], axis=-1, keepdims=True) for h in heads]
    return [_mm_nn(p[h], vms[h]) * _silu(gs[h]) for h in heads]


XT = 256


def _xattn_specs():
    return [
        _bs((XT, 256), lambda i: (i, C_XQ // 256)), _bs((XT, 256), lambda i: (i, C_XQ // 256 + 1)),
        _bs((XT, 256), lambda i: (i, C_XG // 256)), _bs((XT, 256), lambda i: (i, C_XG // 256 + 1)),
        _bs((NMEM, 2 * XW), lambda i: (0, 0)),
        _bs((1, XD), lambda i: (0, 0)), _bs((1, XD), lambda i: (0, 0)),
    ]


def _xattn_load(q0, q1, g0, g1, mkv_ref):
    qs = [(q0, q1)[h // 2][:, (h % 2) * XD:(h % 2 + 1) * XD] for h in range(XH)]
    gs = [(g0, g1)[h // 2][:, (h % 2) * XD:(h % 2 + 1) * XD] for h in range(XH)]
    kms = [mkv_ref[:, h * XD:(h + 1) * XD] for h in range(XH)]
    vms = [mkv_ref[:, XW + h * XD:XW + (h + 1) * XD] for h in range(XH)]
    return qs, gs, kms, vms


def _xattn_fwd(proj, mkv, gxq, gxk):
    s = proj.shape[0]

    def body(q0, q1, g0, g1, mkv_ref, gxq_ref, gxk_ref, o_ref):
        qs, gs, kms, vms = _xattn_load(q0, q1, g0, g1, mkv_ref)
        outs = _xattn_tile(qs, gs, kms, vms, gxq_ref[...], gxk_ref[...])
        for h in range(XH):
            o_ref[:, h * XD:(h + 1) * XD] = outs[h]

    return pl.pallas_call(
        body, name="xattn_fwd", grid=(s // XT,),
        in_specs=_xattn_specs(), out_specs=_bs((XT, XW), lambda i: (i, 0)),
        out_shape=jax.ShapeDtypeStruct((s, XW), F32),
        compiler_params=_params(("arbitrary",)),
    )(proj, proj, proj, proj, mkv, gxq, gxk)


def _xattn_bwd(proj, mkv, gxq, gxk, dy):
    s = proj.shape[0]

    def body(q0, q1, g0, g1, mkv_ref, gxq_ref, gxk_ref, dy_ref, dq_ref, dg_ref, dmkv_ref, dgxq_ref, dgxk_ref):
        @pl.when(pl.program_id(0) == 0)
        def _():
            dmkv_ref[...] = jnp.zeros_like(dmkv_ref)
            dgxq_ref[...] = jnp.zeros_like(dgxq_ref)
            dgxk_ref[...] = jnp.zeros_like(dgxk_ref)

        qs, gs, kms, vms = _xattn_load(q0, q1, g0, g1, mkv_ref)
        _, vjp = jax.vjp(_xattn_tile, qs, gs, kms, vms, gxq_ref[...], gxk_ref[...])
        dqs, dgs, dkms, dvms, dgxq, dgxk = vjp([dy_ref[:, h * XD:(h + 1) * XD] for h in range(XH)])
        for h in range(XH):
            dq_ref[:, h * XD:(h + 1) * XD] = dqs[h]
            dg_ref[:, h * XD:(h + 1) * XD] = dgs[h]
            dmkv_ref[:, h * XD:(h + 1) * XD] += dkms[h]
            dmkv_ref[:, XW + h * XD:XW + (h + 1) * XD] += dvms[h]
        dgxq_ref[...] += dgxq
        dgxk_ref[...] += dgxk

    whole = lambda shape: _bs(shape, lambda i: (0, 0))
    return pl.pallas_call(
        body, name="xattn_bwd", grid=(s // XT,),
        in_specs=_xattn_specs() + [_bs((XT, XW), lambda i: (i, 0))],
        out_specs=[_bs((XT, XW), lambda i: (i, 0)), _bs((XT, XW), lambda i: (i, 0)), whole((NMEM, 2 * XW)),
                   whole((1, XD)), whole((1, XD))],
        out_shape=[jax.ShapeDtypeStruct((s, XW), F32), jax.ShapeDtypeStruct((s, XW), F32),
                   jax.ShapeDtypeStruct((NMEM, 2 * XW), F32), jax.ShapeDtypeStruct((1, XD), F32),
                   jax.ShapeDtypeStruct((1, XD), F32)],
        compiler_params=_params(("arbitrary",)),
    )(proj, proj, proj, proj, mkv, gxq, gxk, dy)


def _mem_bwd(mem, dmn):
    def body(m_ref, dmn_ref, o_ref):
        m = m_ref[...]
        r = lax.rsqrt(jnp.mean(m * m, axis=-1, keepdims=True) + NORM_EPS)
        o_ref[...] = jnp.sum(dmn_ref[...] * m * r, axis=0, keepdims=True)

    return pl.pallas_call(body, name="mem_norm_bwd", out_shape=jax.ShapeDtypeStruct((1, D), F32),
                          compiler_params=_params())(mem, dmn)


SHIFT_W = 512


def _shift_rows(p, s):
    row = lax.broadcasted_iota(jnp.int32, p.shape, 0)
    prev = jnp.where(row == 0, 0.0, pltpu.roll(p, 1, 0))
    nxt = jnp.where(row == s - 1, 0.0, pltpu.roll(p, s - 1, 0))
    return prev, nxt


def _shift_fwd(proj, mu):
    s = proj.shape[0]

    def body(p_ref, mu_ref, o_ref):
        p = p_ref[...]
        prev, nxt = _shift_rows(p, s)
        o_ref[...] = p + mu_ref[...] * (0.5 * (prev + nxt) - p)

    return pl.pallas_call(
        body, name="shift_fwd", grid=(RSW // SHIFT_W,),
        in_specs=[_bs((s, SHIFT_W), lambda j: (0, C_RS // SHIFT_W + j)), _bs((1, SHIFT_W), lambda j: (0, j))],
        out_specs=_bs((s, SHIFT_W), lambda j: (0, j)),
        out_shape=jax.ShapeDtypeStruct((s, RSW), F32),
        compiler_params=_params(("parallel",)),
    )(proj, mu)


def _shift_bwd(proj, mu, dps):
    s = proj.shape[0]

    def body(p_ref, mu_ref, g_ref, o_ref, dmu_ref):
        p, g, mu_v = p_ref[...], g_ref[...], mu_ref[...]
        prev, nxt = _shift_rows(p, s)
        dmu_ref[...] = jnp.sum(g * (0.5 * (prev + nxt) - p), axis=0, keepdims=True)
        mg = mu_v * g
        down, up = _shift_rows(mg, s)
        o_ref[...] = g * (1.0 - mu_v) + 0.5 * (down + up)

    return pl.pallas_call(
        body, name="shift_bwd", grid=(RSW // SHIFT_W,),
        in_specs=[_bs((s, SHIFT_W), lambda j: (0, C_RS // SHIFT_W + j)), _bs((1, SHIFT_W), lambda j: (0, j)),
                  _bs((s, SHIFT_W), lambda j: (0, j))],
        out_specs=[_bs((s, SHIFT_W), lambda j: (0, j)), _bs((1, SHIFT_W), lambda j: (0, j))],
        out_shape=[jax.ShapeDtypeStruct((s, RSW), F32), jax.ShapeDtypeStruct((1, RSW), F32)],
        compiler_params=_params(("parallel",)),
    )(proj, mu, dps)


def _pre_tile(k, wf, wb, af, ab, k_k, k_a, w0s, w2s, a0s, a2s, seg):
    kx = k * k_k
    ss = _hdot(kx * kx, seg)
    kk = kx / jnp.maximum(jnp.sqrt(ss), 1e-12)
    outs = [kk]
    for d, (w_in, a_in) in enumerate(((wf, af), (wb, ab))):
        z = w0s[d] + _mm_nn(jnp.tanh(w_in), w2s[d])
        wd = -_softplus(-z) - 0.5
        dec = jnp.exp(-jnp.exp(wd))
        ad = jax.nn.sigmoid(a0s[d] + _mm_nn(a_in, a2s[d]))
        kd = k * (1.0 + (ad - 1.0) * k_a)
        outs += [dec, kd, kk * ad]
    return outs


PT = 256


def _pre_load(ps_ref, kk_ref, ka_ref, w0_ref, w2_ref, a0_ref, a2_ref):
    k = ps_ref[:, RW:2 * RW]
    wf, wb = ps_ref[:, 3 * RW:3 * RW + 64], ps_ref[:, 3 * RW + 64:3 * RW + 128]
    af, ab = ps_ref[:, 3 * RW + 128:3 * RW + 192], ps_ref[:, 3 * RW + 192:3 * RW + 256]
    w0s = [w0_ref[0:1, :], w0_ref[1:2, :]]
    a0s = [a0_ref[0:1, :], a0_ref[1:2, :]]
    w2s = [w2_ref[0], w2_ref[1]]
    a2s = [a2_ref[0], a2_ref[1]]
    return (k, wf, wb, af, ab, kk_ref[...], ka_ref[...], w0s, w2s, a0s, a2s)


def _pre_specs():
    c = lambda shape: _bs(shape, lambda i: tuple(0 for _ in shape))
    return [_bs((PT, RSW), lambda i: (i, 0)), c((1, RW)), c((1, RW)), c((2, RW)), c((2, 64, RW)), c((2, RW)),
            c((2, 64, RW))]


def _pre_fwd(ps, k_k, k_a, w0, w2, a0, a2):
    s = ps.shape[0]

    def body(ps_ref, kk_ref, ka_ref, w0_ref, w2_ref, a0_ref, a2_ref, *outs):
        args = _pre_load(ps_ref, kk_ref, ka_ref, w0_ref, w2_ref, a0_ref, a2_ref)
        res = _pre_tile(*args, _seg_matrix(RW, HD))
        for o_ref, v in zip(outs, res):
            o_ref[...] = v

    return pl.pallas_call(
        body, name="rwkv_pre_fwd", grid=(s // PT,),
        in_specs=_pre_specs(), out_specs=[_bs((PT, RW), lambda i: (i, 0))] * 7,
        out_shape=[jax.ShapeDtypeStruct((s, RW), F32)] * 7,
        compiler_params=_params(("parallel",)),
    )(ps, k_k, k_a, w0, w2, a0, a2)


def _pre_bwd(ps, k_k, k_a, w0, w2, a0, a2, dr, dv, cts):
    s = ps.shape[0]

    def body(ps_ref, kk_ref, ka_ref, w0_ref, w2_ref, a0_ref, a2_ref, dr_ref, dv_ref, c0, c1, c2, c3, c4, c5, c6,
             dps_ref, dkk_ref, dka_ref, dw0_ref, dw2_ref, da0_ref, da2_ref):
        @pl.when(pl.program_id(0) == 0)
        def _():
            for r in (dkk_ref, dka_ref, dw0_ref, dw2_ref, da0_ref, da2_ref):
                r[...] = jnp.zeros_like(r)

        args = _pre_load(ps_ref, kk_ref, ka_ref, w0_ref, w2_ref, a0_ref, a2_ref)
        seg = _seg_matrix(RW, HD)
        _, vjp = jax.vjp(lambda *a: _pre_tile(*a, seg), *args)
        dk, dwf, dwb, daf, dab, dk_k, dk_a, dw0s, dw2s, da0s, da2s = vjp([c[...] for c in (c0, c1, c2, c3, c4, c5, c6)])
        dps_ref[:, 0:RW] = dr_ref[...]
        dps_ref[:, RW:2 * RW] = dk
        dps_ref[:, 2 * RW:3 * RW] = dv_ref[...]
        for j, t in enumerate((dwf, dwb, daf, dab)):
            dps_ref[:, 3 * RW + 64 * j:3 * RW + 64 * (j + 1)] = t
        dkk_ref[...] += dk_k
        dka_ref[...] += dk_a
        for d in range(2):
            dw0_ref[d:d + 1, :] += dw0s[d]
            da0_ref[d:d + 1, :] += da0s[d]
            dw2_ref[d] += dw2s[d]
            da2_ref[d] += da2s[d]

    c = lambda shape: _bs(shape, lambda i: tuple(0 for _ in shape))
    row = _bs((PT, RW), lambda i: (i, 0))
    return pl.pallas_call(
        body, name="rwkv_pre_bwd", grid=(s // PT,),
        in_specs=_pre_specs() + [row] * 9,
        out_specs=[_bs((PT, RSW), lambda i: (i, 0)), c((1, RW)), c((1, RW)), c((2, RW)), c((2, 64, RW)), c((2, RW)),
                   c((2, 64, RW))],
        out_shape=[jax.ShapeDtypeStruct((s, RSW), F32), jax.ShapeDtypeStruct((1, RW), F32),
                   jax.ShapeDtypeStruct((1, RW), F32), jax.ShapeDtypeStruct((2, RW), F32),
                   jax.ShapeDtypeStruct((2, 64, RW), F32), jax.ShapeDtypeStruct((2, RW), F32),
                   jax.ShapeDtypeStruct((2, 64, RW), F32)],
        compiler_params=_params(("arbitrary",)),
    )(ps, k_k, k_a, w0, w2, a0, a2, dr, dv, *cts)


def _post_tile(y0, y1, r, v, kd0, kd1, rg, r_k, ln_w, ln_b, seg):
    ysum = y0 + y1
    bonus = (_hdot(r * kd0 * r_k, seg) + _hdot(r * kd1 * r_k, seg)) * v
    mean = _hdot(ysum, seg) * (1.0 / HD)
    cen = ysum - mean
    var = _hdot(cen * cen, seg) * (1.0 / HD)
    y = cen * lax.rsqrt(var + GN_EPS) * ln_w + ln_b + bonus
    return y * _silu(rg)


def _post_specs():
    row = _bs((PT, RW), lambda i: (i, 0))
    c = _bs((1, RW), lambda i: (0, 0))
    return [row, row, _bs((PT, RW), lambda i: (i, 0)), _bs((PT, RW), lambda i: (i, 2)), row, row,
            _bs((PT, RW), lambda i: (i, C_RG // RW)), c, c, c]


def _post_fwd(y0, y1, ps, kd0, kd1, proj, r_k, ln_w, ln_b):
    s = ps.shape[0]

    def body(y0_ref, y1_ref, r_ref, v_ref, kd0_ref, kd1_ref, rg_ref, rk_ref, lw_ref, lb_ref, o_ref):
        o_ref[...] = _post_tile(y0_ref[...], y1_ref[...], r_ref[...], v_ref[...], kd0_ref[...], kd1_ref[...],
                                rg_ref[...], rk_ref[...], lw_ref[...], lb_ref[...], _seg_matrix(RW, HD))

    return pl.pallas_call(
        body, name="rwkv_post_fwd", grid=(s // PT,),
        in_specs=_post_specs(), out_specs=_bs((PT, RW), lambda i: (i, 0)),
        out_shape=jax.ShapeDtypeStruct((s, RW), F32),
        compiler_params=_params(("parallel",)),
    )(y0, y1, ps, ps, kd0, kd1, proj, r_k, ln_w, ln_b)


def _post_bwd(y0, y1, ps, kd0, kd1, proj, r_k, ln_w, ln_b, dy):
    s = ps.shape[0]

    def body(y0_ref, y1_ref, r_ref, v_ref, kd0_ref, kd1_ref, rg_ref, rk_ref, lw_ref, lb_ref, dy_ref,
             dys_ref, dr_ref, dv_ref, dkd0_ref, dkd1_ref, drg_ref, drk_ref, dlw_ref, dlb_ref):
        @pl.when(pl.program_id(0) == 0)
        def _():
            for r in (drk_ref, dlw_ref, dlb_ref):
                r[...] = jnp.zeros_like(r)

        seg = _seg_matrix(RW, HD)
        args = [t[...] for t in (y0_ref, y1_ref, r_ref, v_ref, kd0_ref, kd1_ref, rg_ref, rk_ref, lw_ref, lb_ref)]
        _, vjp = jax.vjp(lambda *a: _post_tile(*a, seg), *args)
        dy0, _, dr, dv, dkd0, dkd1, drg, drk, dlw, dlb = vjp(dy_ref[...])
        dys_ref[...] = dy0
        dr_ref[...] = dr
        dv_ref[...] = dv
        dkd0_ref[...] = dkd0
        dkd1_ref[...] = dkd1
        drg_ref[...] = drg
        drk_ref[...] += drk
        dlw_ref[...] += dlw
        dlb_ref[...] += dlb

    row = _bs((PT, RW), lambda i: (i, 0))
    c = _bs((1, RW), lambda i: (0, 0))
    return pl.pallas_call(
        body, name="rwkv_post_bwd", grid=(s // PT,),
        in_specs=_post_specs() + [row], out_specs=[row] * 6 + [c] * 3,
        out_shape=[jax.ShapeDtypeStruct((s, RW), F32)] * 6 + [jax.ShapeDtypeStruct((1, RW), F32)] * 3,
        compiler_params=_params(("arbitrary",)),
    )(y0, y1, ps, ps, kd0, kd1, proj, r_k, ln_w, ln_b, dy)


def _ones1():
    r = lax.broadcasted_iota(jnp.int32, (128, 128), 0) // HD
    c = lax.broadcasted_iota(jnp.int32, (128, 128), 1) // HD
    return (r == c).astype(BF16)


def _scan_specs(direction, nc, fwd_order):
    def tb(c):
        sc = c if fwd_order else nc - 1 - c
        return sc if direction == 0 else nc - 1 - sc

    row = _bs((TC, RW), lambda c: (tb(c), 0))
    rowv = _bs((TC, RW), lambda c: (tb(c), 2))
    return row, rowv


def _tiles(res, k):
    n = NPAIR * HD
    return [res[k * n + p * HD:k * n + (p + 1) * HD] for p in range(NPAIR)]


def _rows_to_tiles(src_ref, rows8, stage, out_s, base):
    for p in range(NPAIR):
        stage[base + p, 0:8, 0:HD] = src_ref[rows8, p * 128:p * 128 + HD]
        stage[base + p, HD:HD + 8, 0:HD] = src_ref[rows8, p * 128 + HD:(p + 1) * 128]
        out_s[base + p] = stage[base + p].T[0:HD].astype(BF16)


def _tiles_to_rows(tile_s, base, dst_ref, rows8):
    for p in range(NPAIR):
        t = jnp.concatenate([tile_s[base + p], jnp.zeros((HD, 128), F32)], axis=0).T
        dst_ref[rows8, p * 128:p * 128 + HD] = t[0:8, 0:HD]
        dst_ref[rows8, p * 128 + HD:(p + 1) * 128] = t[HD:HD + 8, 0:HD]


def _put_cols(tile_s, base, u, tiles):
    for p in range(NPAIR):
        tile_s[base + p, :, u:u + 1] = tiles[p][:, u:u + 1]
        tile_s[base + p, :, HD + u:HD + u + 1] = tiles[p][:, HD + u:HD + u + 1]


def _scan2_fwd(per_dir, ps, kk, gather=()):
    s = ps.shape[0]
    nc, ng = s // TC, TC // 8
    ngat = len(gather)
    in_specs, operands, out_specs, out_shape = [], [], [], []
    for d in (0, 1):
        row, rowv = _scan_specs(d, nc, True)
        in_specs += [row] * 5 + [rowv]
        operands += list(per_dir[d]) + [ps, kk, ps]
        out_specs += [row, _bs((1, NPAIR, HD, 128), lambda c: (c, 0, 0, 0))]
        out_shape += [jax.ShapeDtypeStruct((s, RW), F32), jax.ShapeDtypeStruct((nc, NPAIR, HD, 128), F32)]
    in_specs += [ANY] * ngat
    operands += list(gather)
    out_specs += [ANY] * ngat
    out_shape += _gather_out_shapes(gather)

    def body(*refs):
        ins = [refs[0:6], refs[6:12]]
        base = 12 + ngat
        y_refs, ck_refs = (refs[base], refs[base + 2]), (refs[base + 1], refs[base + 3])
        st, vt_s, yt_s, stage = refs[base + 4 + ngat:base + 8 + ngat]
        if ngat:
            g_start, g_forward, g_finish = _gather_phases(
                gather, refs[12:base], refs[base + 4:base + 4 + ngat], refs[base + 8 + ngat:])

        @pl.when(pl.program_id(0) == 0)
        def _():
            st[...] = jnp.zeros_like(st)
            yt_s[...] = jnp.zeros_like(yt_s)
            stage[...] = jnp.zeros_like(stage)
            if ngat:
                g_start()

        if ngat:
            @pl.when(pl.program_id(0) == nc // 2)
            def _():
                g_forward()

        for d in (0, 1):
            ck_refs[d][0] = st[d * NPAIR:(d + 1) * NPAIR]
        ones1 = _ones1()
        lane_u = lax.broadcasted_iota(jnp.int32, (HD, 128), 1) % HD
        pc = [slice(p * 128, (p + 1) * 128) for p in range(NPAIR)]

        def group(gi, carry):
            gs = (gi, ng - 1 - gi)
            rows8 = [pl.ds(pl.multiple_of(gs[d] * 8, 8), 8) for d in (0, 1)]
            blk = [[q[rows8[d], :] for q in ins[d][:5]] for d in (0, 1)]
            for d in (0, 1):
                _rows_to_tiles(ins[d][5], rows8[d], stage, vt_s, d * NPAIR)
            ss = [[st[d * NPAIR + p] for p in range(NPAIR)] for d in (0, 1)]
            for ui in range(9):
                us, ups = (ui, 7 - ui), (ui - 1, 8 - ui)
                lhs1, where = [], {}
                for d in (0, 1):
                    if ui < 8:
                        where["sa", d] = len(lhs1) // NPAIR
                        lhs1 += [(ss[d][p] * blk[d][4][us[d]:us[d] + 1, pc[p]]).astype(BF16) for p in range(NPAIR)]
                        where["vb", d] = len(lhs1) // NPAIR
                        for p in range(NPAIR):
                            vt = vt_s[d * NPAIR + p]
                            lhs1.append(jnp.where(lane_u == us[d], vt, jnp.zeros_like(vt)))
                    if ui > 0:
                        where["y", d] = len(lhs1) // NPAIR
                        lhs1 += [(ss[d][p] * blk[d][3][ups[d]:ups[d] + 1, pc[p]]).astype(BF16) for p in range(NPAIR)]
                res1 = jnp.dot(jnp.concatenate(lhs1, axis=0), ones1, preferred_element_type=F32)
                for d in (0, 1):
                    d8, k8, b8, _, _ = blk[d]
                    u = us[d]
                    if ui < 8:
                        sa, vb = _tiles(res1, where["sa", d]), _tiles(res1, where["vb", d])
                        for p in range(NPAIR):
                            ss[d][p] = (ss[d][p] * d8[u:u + 1, pc[p]] - sa[p] * b8[u:u + 1, pc[p]]
                                        + vb[p] * k8[u:u + 1, pc[p]])
                    if ui > 0:
                        _put_cols(yt_s, d * NPAIR, ups[d], _tiles(res1, where["y", d]))
            for d in (0, 1):
                _tiles_to_rows(yt_s, d * NPAIR, y_refs[d], rows8[d])
                for p in range(NPAIR):
                    st[d * NPAIR + p] = ss[d][p]
            return carry

        for gi in range(ng):
            group(gi, 0)

        if ngat:
            @pl.when(pl.program_id(0) == nc - 1)
            def _():
                g_finish()

    outs = pl.pallas_call(
        body, name="rwkv_scan_fwd", grid=(nc,), in_specs=in_specs, out_specs=out_specs, out_shape=out_shape,
        scratch_shapes=[pltpu.VMEM((2 * NPAIR, HD, 128), F32), pltpu.VMEM((2 * NPAIR, HD, 128), BF16),
                        pltpu.VMEM((2 * NPAIR, HD, 128), F32), pltpu.VMEM((2 * NPAIR, 128, 128), F32)]
        + (_gather_sems(ngat) if ngat else []),
        compiler_params=pltpu.CompilerParams(dimension_semantics=("arbitrary",), vmem_limit_bytes=VMEM_LIMIT,
                                             has_side_effects=bool(ngat)),
    )(*operands)
    return [(outs[0], outs[1]), (outs[2], outs[3])], list(outs[4:])


def _scan2_bwd(per_dir, ps, kk, dy, scatter=()):
    s = ps.shape[0]
    nc, ng = s // TC, TC // 8
    nsc = len(scatter)
    in_specs, operands, out_specs, out_shape = [], [], [], []
    for d in (0, 1):
        row, rowv = _scan_specs(d, nc, False)
        dec, kd, b, ck = per_dir[d]
        in_specs += [row] * 5 + [rowv, row, _bs((1, NPAIR, HD, 128), lambda c: (nc - 1 - c, 0, 0, 0))]
        operands += [dec, kd, b, ps, kk, ps, dy, ck]
        out_specs += [row] * 6
        out_shape += [jax.ShapeDtypeStruct((s, RW), F32)] * 6
    in_specs += [ANY] * nsc
    operands += list(scatter)
    out_specs += [ANY] * nsc
    out_shape += _scatter_out_shapes(scatter)

    def body(*refs):
        ins = [refs[0:8], refs[8:16]]
        base = 16 + nsc
        outs = [refs[base:base + 6], refs[base + 6:base + 12]]
        st, sa_s, vb_s, dy_s, ds, vt_s, dyt_s, dvt_s, stage = refs[base + 12 + nsc:base + 21 + nsc]
        if nsc:
            s_start, s_finish = _scatter_phases(refs[16:base], refs[base + 12:base + 12 + nsc], refs[base + 21 + nsc:])

        @pl.when(pl.program_id(0) == 0)
        def _():
            dvt_s[...] = jnp.zeros_like(dvt_s)
            stage[...] = jnp.zeros_like(stage)
            ds[...] = jnp.zeros_like(ds)
            if nsc:
                s_start()

        for d in (0, 1):
            st[d * (TC + 1)] = ins[d][7][0]
        ones1 = _ones1()
        lane_u = lax.broadcasted_iota(jnp.int32, (HD, 128), 1) % HD
        row_id = lax.broadcasted_iota(jnp.int32, (8, 128), 0)
        pc = [slice(p * 128, (p + 1) * 128) for p in range(NPAIR)]

        def load_rows(gs):
            return [[q[pl.ds(pl.multiple_of(gs[d] * 8, 8), 8), :] for q in ins[d][:5]] for d in (0, 1)]

        def fgroup(gi, carry):
            gs = (gi, ng - 1 - gi)
            blk = load_rows(gs)
            for d in (0, 1):
                rows8 = pl.ds(pl.multiple_of(gs[d] * 8, 8), 8)
                _rows_to_tiles(ins[d][5], rows8, stage, vt_s, d * NPAIR)
                _rows_to_tiles(ins[d][6], rows8, stage, dyt_s, d * NPAIR)
            ss = [[st[d * (TC + 1) + gi * 8, p] for p in range(NPAIR)] for d in (0, 1)]
            for ui in range(8):
                us = (ui, 7 - ui)
                i = gi * 8 + ui
                lhs1 = []
                for d in (0, 1):
                    kk8 = blk[d][4]
                    lhs1 += [(ss[d][p] * kk8[us[d]:us[d] + 1, pc[p]]).astype(BF16) for p in range(NPAIR)]
                    for tile_s in (vt_s, dyt_s):
                        for p in range(NPAIR):
                            t = tile_s[d * NPAIR + p]
                            lhs1.append(jnp.where(lane_u == us[d], t, jnp.zeros_like(t)))
                res1 = jnp.dot(jnp.concatenate(lhs1, axis=0), ones1, preferred_element_type=F32)
                for d in (0, 1):
                    d8, k8, b8, _, _ = blk[d]
                    u = us[d]
                    sa, vb, dyb = _tiles(res1, 3 * d), _tiles(res1, 3 * d + 1), _tiles(res1, 3 * d + 2)
                    for p in range(NPAIR):
                        sa_s[d * TC + i, p] = sa[p]
                        vb_s[d * TC + i, p] = vb[p]
                        dy_s[d * TC + i, p] = dyb[p]
                        ss[d][p] = ss[d][p] * d8[u:u + 1, pc[p]] - sa[p] * b8[u:u + 1, pc[p]] + vb[p] * k8[u:u + 1, pc[p]]
                        st[d * (TC + 1) + i + 1, p] = ss[d][p]
            return carry

        for gi in range(ng):
            fgroup(gi, 0)

        def bgroup(gj, carry):
            gi = ng - 1 - gj
            gs = (gi, ng - 1 - gi)
            blk = load_rows(gs)
            dss = [[ds[d * NPAIR + p] for p in range(NPAIR)] for d in (0, 1)]
            acc = [[[jnp.zeros((8, 128), F32) for _ in range(5)] for _ in range(NPAIR)] for _ in (0, 1)]
            for uj in range(8):
                ui = 7 - uj
                us = (ui, 7 - ui)
                i = gi * 8 + ui
                lhs1, dyb = [], [None, None]
                for d in (0, 1):
                    _, k8, b8, r8, _ = blk[d]
                    u = us[d]
                    dyb[d] = [dy_s[d * TC + i, p] for p in range(NPAIR)]
                    for p in range(NPAIR):
                        dss[d][p] = dss[d][p] + dyb[d][p] * r8[u:u + 1, pc[p]]
                    lhs1 += [(dss[d][p] * b8[u:u + 1, pc[p]]).astype(BF16) for p in range(NPAIR)]
                    lhs1 += [(dss[d][p] * k8[u:u + 1, pc[p]]).astype(BF16) for p in range(NPAIR)]
                res1 = jnp.dot(jnp.concatenate(lhs1, axis=0), ones1, preferred_element_type=F32)
                for d in (0, 1):
                    d8, _, _, _, kk8 = blk[d]
                    u = us[d]
                    dsa, dvb = _tiles(res1, 2 * d), _tiles(res1, 2 * d + 1)
                    _put_cols(dvt_s, d * NPAIR, u, dvb)
                    for p in range(NPAIR):
                        sp, sn = st[d * (TC + 1) + i, p], st[d * (TC + 1) + i + 1, p]
                        dsv = dss[d][p]
                        vals = (jnp.sum(sn * dyb[d][p], axis=0, keepdims=True), jnp.sum(dsv * sp, axis=0, keepdims=True),
                                -jnp.sum(dsv * sa_s[d * TC + i, p], axis=0, keepdims=True),
                                jnp.sum(dsv * vb_s[d * TC + i, p], axis=0, keepdims=True),
                                -jnp.sum(sp * dsa[p], axis=0, keepdims=True))
                        acc[d][p] = [jnp.where(row_id == u, o, a_) for o, a_ in zip(vals, acc[d][p])]
                        dss[d][p] = dsv * d8[u:u + 1, pc[p]] - dsa[p] * kk8[u:u + 1, pc[p]]
            for d in (0, 1):
                rows8 = pl.ds(pl.multiple_of(gs[d] * 8, 8), 8)
                _tiles_to_rows(dvt_s, d * NPAIR, outs[d][5], rows8)
                for p in range(NPAIR):
                    ds[d * NPAIR + p] = dss[d][p]
                    for o_ref, a_ in zip(outs[d][:5], acc[d][p]):
                        o_ref[rows8, pc[p]] = a_
            return carry

        for gj in range(ng):
            bgroup(gj, 0)

        if nsc:
            @pl.when(pl.program_id(0) == nc - 1)
            def _():
                s_finish()

    chunk = lambda k: pltpu.VMEM((k, NPAIR, HD, 128), F32)
    pairs = lambda w, dt: pltpu.VMEM((2 * NPAIR, HD, w), dt)
    res = pl.pallas_call(
        body, name="rwkv_scan_bwd", grid=(nc,), in_specs=in_specs, out_specs=out_specs, out_shape=out_shape,
        scratch_shapes=[chunk(2 * (TC + 1)), chunk(2 * TC), chunk(2 * TC), chunk(2 * TC), pairs(128, F32),
                        pairs(128, BF16), pairs(128, BF16), pairs(128, F32), pltpu.VMEM((2 * NPAIR, 128, 128), F32)]
        + _scatter_sems(nsc),
        compiler_params=pltpu.CompilerParams(dimension_semantics=("arbitrary",), vmem_limit_bytes=VMEM_LIMIT,
                                             has_side_effects=bool(nsc)),
    )(*operands)
    return [res[0:6], res[6:12]], list(res[12:])


MT = 512
MN = 256


def _merge_fwd(ya, yr, yx, wa, wr, wx, proj, gate_b):
    s = ya.shape[0]

    def body(ya_ref, yr_ref, yx_ref, wa_ref, wr_ref, wx_ref, m0, m1, m2, b0, b1, b2, o_ref):
        acc = jnp.zeros((MT, MN), F32)
        for y_ref, w_ref, m_ref, b_ref in ((ya_ref, wa_ref, m0, b0), (yr_ref, wr_ref, m1, b1), (yx_ref, wx_ref, m2, b2)):
            u = _dot(y_ref[...], w_ref[...], ((1,), (0,)))
            acc = acc + jax.nn.sigmoid(m_ref[...] + b_ref[...]) * u
        o_ref[...] = acc.astype(BF16)

    mg = lambda br: _bs((MT, MN), lambda i, j: (i, C_MG // MN + br * (D // MN) + j))
    gb = lambda br: _bs((1, MN), lambda i, j: (0, br * (D // MN) + j))
    return pl.pallas_call(
        body, name="merge_fwd", grid=(s // MT, D // MN),
        in_specs=[_bs((MT, RW), lambda i, j: (i, 0)), _bs((MT, RW), lambda i, j: (i, 0)), _bs((MT, XW), lambda i, j: (i, 0)),
                  _bs((RW, MN), lambda i, j: (0, j)), _bs((RW, MN), lambda i, j: (0, j)), _bs((XW, MN), lambda i, j: (0, j)),
                  mg(0), mg(1), mg(2), gb(0), gb(1), gb(2)],
        out_specs=_bs((MT, MN), lambda i, j: (i, j)),
        out_shape=jax.ShapeDtypeStruct((s, D), BF16),
        compiler_params=_params(("parallel", "arbitrary")),
    )(ya, yr, yx, wa, wr, wx, proj, proj, proj, gate_b, gate_b, gate_b)


def _out_fwd(merged, w_out, x, target):
    s = x.shape[0]
    tm, tn = min(512, s), 512

    def body(m_ref, w_ref, x_ref, t_ref, loss_ref, d_ref, d16_ref):
        @pl.when((pl.program_id(0) == 0) & (pl.program_id(1) == 0))
        def _():
            loss_ref[...] = jnp.zeros_like(loss_ref)

        out = x_ref[...] + jnp.dot(m_ref[...], w_ref[...], preferred_element_type=F32)
        err = out - t_ref[...]
        dout = err * (1.0 / D)
        d_ref[...] = dout
        d16_ref[...] = dout.astype(BF16)
        loss_ref[...] += jnp.sum(err * err)

    tile = _bs((tm, tn), lambda i, j: (i, j))
    return pl.pallas_call(
        body, name="out_fwd", grid=(s // tm, D // tn),
        in_specs=[_bs((tm, D), lambda i, j: (i, 0)), _bs((D, tn), lambda i, j: (0, j)), tile, tile],
        out_specs=[_bs((8, 128), lambda i, j: (0, 0)), tile, tile],
        out_shape=[jax.ShapeDtypeStruct((8, 128), F32), jax.ShapeDtypeStruct((s, D), F32),
                   jax.ShapeDtypeStruct((s, D), BF16)],
        compiler_params=_params(("arbitrary", "arbitrary")),
    )(merged, w_out, x, target)


def _merge_bwd(ya, yr, yx, wa, wr, wx, proj, gate_b, dmerged):
    s = ya.shape[0]

    def body(ya_ref, yr_ref, yx_ref, wa_ref, wr_ref, wx_ref, m0, m1, m2, b0, b1, b2, dm_ref,
             dg0, dg1, dg2, du0, du1, du2, dya_ref, dyr_ref, dyx_ref):
        @pl.when(pl.program_id(1) == 0)
        def _():
            dya_ref[...] = jnp.zeros_like(dya_ref)
            dyr_ref[...] = jnp.zeros_like(dyr_ref)
            dyx_ref[...] = jnp.zeros_like(dyx_ref)

        dm = dm_ref[...]
        branches = ((ya_ref, wa_ref, m0, b0, dg0, du0, dya_ref), (yr_ref, wr_ref, m1, b1, dg1, du1, dyr_ref),
                    (yx_ref, wx_ref, m2, b2, dg2, du2, dyx_ref))
        ws = [br[1][...] for br in branches]
        us = [_dot(br[0][...], w, ((1,), (0,))) for br, w in zip(branches, ws)]
        gts = [jax.nn.sigmoid(br[2][...] + br[3][...]) for br in branches]
        dus = [(dm * gt).astype(BF16) for gt in gts]
        for br, w, u, gt, du in zip(branches, ws, us, gts, dus):
            br[4][...] = (dm * u * gt * (1.0 - gt)).astype(BF16)
            br[5][...] = du
            br[6][...] += _dot(du, w, ((1,), (1,)))

    mg = lambda br: _bs((MT, MN), lambda i, j: (i, C_MG // MN + br * (D // MN) + j))
    gb = lambda br: _bs((1, MN), lambda i, j: (0, br * (D // MN) + j))
    tile = _bs((MT, MN), lambda i, j: (i, j))
    return pl.pallas_call(
        body, name="merge_bwd", grid=(s // MT, D // MN),
        in_specs=[_bs((MT, RW), lambda i, j: (i, 0)), _bs((MT, RW), lambda i, j: (i, 0)), _bs((MT, XW), lambda i, j: (i, 0)),
                  _bs((RW, MN), lambda i, j: (0, j)), _bs((RW, MN), lambda i, j: (0, j)), _bs((XW, MN), lambda i, j: (0, j)),
                  mg(0), mg(1), mg(2), gb(0), gb(1), gb(2), tile],
        out_specs=[tile] * 6 + [_bs((MT, RW), lambda i, j: (i, 0)), _bs((MT, RW), lambda i, j: (i, 0)),
                                _bs((MT, XW), lambda i, j: (i, 0))],
        out_shape=[jax.ShapeDtypeStruct((s, D), BF16)] * 6 + [jax.ShapeDtypeStruct((s, RW), F32),
                                                               jax.ShapeDtypeStruct((s, RW), F32),
                                                               jax.ShapeDtypeStruct((s, XW), F32)],
        compiler_params=_params(("parallel", "arbitrary")),
    )(ya, yr, yx, wa, wr, wx, proj, proj, proj, gate_b, gate_b, gate_b, dmerged)


def _colsum(a, name):
    m, n = a.shape
    tm, tn = min(512, m), 512

    def body(a_ref, o_ref):
        @pl.when(pl.program_id(1) == 0)
        def _():
            o_ref[...] = jnp.zeros_like(o_ref)

        o_ref[...] += jnp.sum(a_ref[...].astype(F32), axis=0, keepdims=True)

    return pl.pallas_call(
        body, name=name, grid=(n // tn, m // tm),
        in_specs=[_bs((tm, tn), lambda j, i: (i, j))], out_specs=_bs((1, tn), lambda j, i: (0, j)),
        out_shape=jax.ShapeDtypeStruct((1, n), F32),
        compiler_params=_params(("parallel", "arbitrary")),
    )(a)


def _in_bwd(dproj, w_in, x, g, dout, stacks=()):
    s = x.shape[0]
    tm, tk = min(512, s), 896
    nk = NIN // tk
    ni = s // tm
    n = len(stacks)

    def body(dp_ref, w_ref, x_ref, g_ref, do_ref, *rest):
        ins, (gx_ref, gg_ref), outs = rest[:n], rest[n:n + 2], rest[n + 2:2 * n + 2]
        acc = rest[2 * n + 2]
        i, kk = pl.program_id(0), pl.program_id(1)

        if n:
            start, finish = _scatter_phases(ins, outs, rest[2 * n + 3:])

        @pl.when((i == 0) & (kk == 0))
        def _():
            gg_ref[...] = jnp.zeros_like(gg_ref)
            if n:
                start()

        @pl.when(kk == 0)
        def _():
            acc[...] = jnp.zeros_like(acc)

        acc[...] += _dot(dp_ref[...], w_ref[...], ((1,), (1,)))

        @pl.when(kk == nk - 1)
        def _():
            xv, dh, gv = x_ref[...], acc[...], g_ref[...]
            r = lax.rsqrt(jnp.mean(xv * xv, axis=-1, keepdims=True) + NORM_EPS)
            xn = xv * r
            gg_ref[...] += jnp.sum(dh * xn, axis=0, keepdims=True)
            dxn = dh * gv
            dx = r * (dxn - xn * jnp.mean(dxn * xn, axis=-1, keepdims=True))
            gx_ref[...] = do_ref[...] + dx

        if n:
            @pl.when((i == ni - 1) & (kk == nk - 1))
            def _():
                finish()

    any_spec = pl.BlockSpec(memory_space=pl.ANY)
    res = pl.pallas_call(
        body, name="in_bwd", grid=(ni, nk),
        in_specs=[_bs((tm, tk), lambda i, kk: (i, kk)), _bs((D, tk), lambda i, kk: (0, kk)),
                  _bs((tm, D), lambda i, kk: (i, 0)), _bs((1, D), lambda i, kk: (0, 0)),
                  _bs((tm, D), lambda i, kk: (i, 0))] + [any_spec] * n,
        out_specs=[_bs((tm, D), lambda i, kk: (i, 0)), _bs((1, D), lambda i, kk: (0, 0))] + [any_spec] * n,
        out_shape=[jax.ShapeDtypeStruct((s, D), F32), jax.ShapeDtypeStruct((1, D), F32)] + _scatter_out_shapes(stacks),
        scratch_shapes=[pltpu.VMEM((tm, D), F32)] + _scatter_sems(n),
        compiler_params=pltpu.CompilerParams(dimension_semantics=("arbitrary", "arbitrary"),
                                             vmem_limit_bytes=VMEM_LIMIT, has_side_effects=bool(n)),
    )(dproj, w_in, x, g, dout, *stacks)
    return res[0], res[1], list(res[2:])


def _adamw_math(w, g, m, v):
    m = ADAM_B1 * m + (1.0 - ADAM_B1) * g
    v = ADAM_B2 * v + (1.0 - ADAM_B2) * jnp.square(g)
    m_hat = m / (1.0 - ADAM_B1 ** ADAM_STEP)
    v_hat = v / (1.0 - ADAM_B2 ** ADAM_STEP)
    delta = -ADAM_LR * (m_hat / (jnp.sqrt(v_hat) + ADAM_EPS) + ADAM_WD * w)
    return delta, m, v


def _adamw(parts, w, m, v, name):
    rows, cols = w.shape
    tr = rows
    for cand in (256, 128, 64, 32, 16, 8):
        if rows % cand == 0 and cand * cols * 4 <= (1 << 20):
            tr = cand
            break
    n = len(parts)

    def body(*refs):
        g = refs[0][...].astype(F32)
        for r in refs[1:n]:
            g = g + r[...].astype(F32)
        w_ref, m_ref, v_ref, g_out, d_out, m_out, v_out = refs[n:]
        delta, m_new, v_new = _adamw_math(w_ref[...], g, m_ref[...], v_ref[...])
        g_out[...] = g
        d_out[...] = delta
        m_out[...] = m_new
        v_out[...] = v_new

    spec = _bs((tr, cols), lambda i: (i, 0))
    return pl.pallas_call(
        body, name=name, grid=(rows // tr,),
        in_specs=[spec] * (n + 3), out_specs=[spec] * 4,
        out_shape=[jax.ShapeDtypeStruct((rows, cols), F32)] * 4,
        compiler_params=_params(("parallel",)),
    )(*parts, w, m, v)


def _adamw_halves(mine, theirs, core, w, m, v, name):
    rows, cols = w.shape
    h = rows // 2
    tr = next(t for t in (256, 128, 64, 32, 16, 8) if h % t == 0 and t * cols * 4 <= (1 << 20))
    nt = h // tr

    def body(core_ref, mine_ref, theirs_ref, w_ref, m_ref, v_ref, g_out, d_out, m_out, v_out):
        is_mine = pl.program_id(0) // nt == core_ref[0]
        g = jnp.where(is_mine, mine_ref[...], theirs_ref[...])
        delta, m_new, v_new = _adamw_math(w_ref[...], g, m_ref[...], v_ref[...])
        g_out[...] = g
        d_out[...] = delta
        m_out[...] = m_new
        v_out[...] = v_new

    spec = _bs((tr, cols), lambda i, core_ref: (i, 0))
    return pl.pallas_call(
        body, name=name,
        grid_spec=pltpu.PrefetchScalarGridSpec(
            num_scalar_prefetch=1, grid=(2 * nt,),
            in_specs=[_bs((tr, cols), lambda i, core_ref: (jnp.clip(i - core_ref[0] * nt, 0, nt - 1), 0)),
                      _bs((tr, cols), lambda i, core_ref: (jnp.clip(i - (1 - core_ref[0]) * nt, 0, nt - 1), 0)),
                      spec, spec, spec],
            out_specs=[spec] * 4),
        out_shape=[jax.ShapeDtypeStruct((rows, cols), F32)] * 4,
        compiler_params=_params(("parallel",)),
    )(core, mine, theirs, w, m, v)


def _sum_parts(parts, name):
    rows, cols = parts[0].shape
    tr = rows
    for cand in (256, 128, 64, 32, 16, 8):
        if rows % cand == 0 and cand * cols * 4 <= (1 << 20):
            tr = cand
            break

    def body(*refs):
        acc = refs[0][...].astype(F32)
        for r in refs[1:-1]:
            acc = acc + r[...].astype(F32)
        refs[-1][...] = acc

    spec = _bs((tr, cols), lambda i: (i, 0))
    return pl.pallas_call(
        body, name=name, grid=(rows // tr,), in_specs=[spec] * len(parts), out_specs=spec,
        out_shape=jax.ShapeDtypeStruct((rows, cols), F32), compiler_params=_params(("parallel",)),
    )(*parts)


ANY = pl.BlockSpec(memory_space=pl.ANY)


def _other_chips(x, y):
    return [(1 - x, y), (x, 1 - y), (1 - x, 1 - y)]


def _gather_shards(arrays, name):
    n = len(arrays)

    def body(*refs):
        start, forward, finish = _gather_phases(arrays, refs[:n], refs[n:2 * n], refs[2 * n:])
        start()
        forward()
        finish()

    return pl.pallas_call(
        body, name=name, in_specs=[ANY] * n, out_specs=[ANY] * n,
        out_shape=_gather_out_shapes(arrays), scratch_shapes=_gather_sems(n),
        compiler_params=pltpu.CompilerParams(has_side_effects=True),
    )(*arrays)


def _gather_out_shapes(arrays):
    return [jax.ShapeDtypeStruct((4,) + a.shape, a.dtype) for a in arrays]


def _gather_sems(n):
    dma = lambda k: pltpu.SemaphoreType.DMA((k,))
    return [dma(3 * n), dma(3 * n), dma(3 * n), dma(3 * n), dma(n), dma(n)]


def _gather_phases(arrays, ins, outs, sems):
    n = len(arrays)
    ici_send, ici_recv, d2d_send, d2d_recv, own_send, own_recv = sems

    def place():
        x, y, c = lax.axis_index("x"), lax.axis_index("y"), lax.axis_index("c")
        return x, y, c, 2 * x + y, _other_chips(x, y)

    def half(i, who):
        h = arrays[i].shape[0] // 2
        return pl.ds(who * h, h)

    def ici(i, j, src_chip, to, c):
        return pltpu.make_async_remote_copy(
            src_ref=ins[i].at[half(i, c)], dst_ref=outs[i].at[src_chip, half(i, c)], send_sem=ici_send.at[3 * i + j],
            recv_sem=ici_recv.at[3 * i + j], device_id=to, device_id_type=MESH)

    def d2d(i, j, src_chip, who, sib):
        piece = outs[i].at[src_chip, half(i, who)]
        return pltpu.make_async_remote_copy(
            src_ref=piece, dst_ref=piece, send_sem=d2d_send.at[3 * i + j], recv_sem=d2d_recv.at[3 * i + j],
            device_id=sib, device_id_type=MESH)

    def own(i, me, sib):
        return pltpu.make_async_remote_copy(
            src_ref=ins[i], dst_ref=outs[i].at[me], send_sem=own_send.at[i], recv_sem=own_recv.at[i],
            device_id=sib, device_id_type=MESH)

    def start():
        x, y, c, me, chips = place()
        for i in range(n):
            own(i, me, (x, y, 1 - c)).start()
            for j, (px, py) in enumerate(chips):
                ici(i, j, me, (px, py, c), c).start()

    def forward():
        x, y, c, me, chips = place()
        for i in range(n):
            for j, (px, py) in enumerate(chips):
                ici(i, j, 2 * px + py, (px, py, c), c).wait_recv()
                d2d(i, j, 2 * px + py, c, (x, y, 1 - c)).start()

    def finish():
        x, y, c, me, chips = place()
        sib = (x, y, 1 - c)
        for i in range(n):
            for j, (px, py) in enumerate(chips):
                d2d(i, j, 2 * px + py, 1 - c, sib).wait_recv()
            own(i, me, sib).wait_recv()
        for i in range(n):
            own(i, me, sib).wait_send()
            for j, (px, py) in enumerate(chips):
                ici(i, j, me, (px, py, c), c).wait_send()
                d2d(i, j, 2 * px + py, c, sib).wait_send()

    return start, forward, finish


def _scatter_phases(ins, outs, sems):
    send_sems, recv_sems = sems

    def copies():
        x, y, c = lax.axis_index("x"), lax.axis_index("y"), lax.axis_index("c")
        return [pltpu.make_async_remote_copy(
            src_ref=ins[a].at[2 * qx + qy], dst_ref=outs[a].at[j], send_sem=send_sems.at[3 * a + j],
            recv_sem=recv_sems.at[3 * a + j], device_id=(qx, qy, c), device_id_type=MESH)
            for a in range(len(ins)) for j, (qx, qy) in enumerate(_other_chips(x, y))]

    def start():
        for rc in copies():
            rc.start()

    def finish():
        for rc in copies():
            rc.wait_recv()
        for rc in copies():
            rc.wait_send()

    return start, finish


def _scatter_out_shapes(stacks):
    return [jax.ShapeDtypeStruct((3,) + a.shape[1:], a.dtype) for a in stacks]


def _scatter_sems(n):
    return [pltpu.SemaphoreType.DMA((3 * n,)), pltpu.SemaphoreType.DMA((3 * n,))] if n else []


def _pair_exchange(stacks, name):
    n = len(stacks)

    def body(*refs):
        ins, outs = refs[:n], refs[n:2 * n]
        send_sems, recv_sems = refs[2 * n:]
        x, y, c = lax.axis_index("x"), lax.axis_index("y"), lax.axis_index("c")
        cps = []
        for i in range(n):
            h = stacks[i].shape[1] // 2
            rc = pltpu.make_async_remote_copy(
                src_ref=ins[i].at[:, pl.ds((1 - c) * h, h)], dst_ref=outs[i], send_sem=send_sems.at[i],
                recv_sem=recv_sems.at[i], device_id=(x, y, 1 - c), device_id_type=MESH)
            rc.start()
            cps.append(rc)
        for rc in cps:
            rc.wait_recv()
        for rc in cps:
            rc.wait_send()

    return pl.pallas_call(
        body, name=name, in_specs=[ANY] * n, out_specs=[ANY] * n,
        out_shape=[jax.ShapeDtypeStruct((4, a.shape[1] // 2) + a.shape[2:], a.dtype) for a in stacks],
        scratch_shapes=[pltpu.SemaphoreType.DMA((n,)), pltpu.SemaphoreType.DMA((n,))],
        compiler_params=pltpu.CompilerParams(has_side_effects=True),
    )(*stacks)


def _pair_sum(own, theirs, core, name):
    _, r, cols = own.shape
    h = r // 2
    tr = next(t for t in (256, 128, 64, 32, 16) if h % t == 0 and t * cols * 4 <= (1 << 20))
    nt = h // tr

    def body(core_ref, own_ref, th_ref, o32_ref, o16_ref):
        del core_ref
        acc = own_ref[...] + th_ref[...].astype(F32)
        o32_ref[...] = acc
        o16_ref[...] = acc.astype(BF16)

    out = _bs((1, tr, cols), lambda j, t, core_ref: (j, t, 0))
    return pl.pallas_call(
        body, name=name,
        grid_spec=pltpu.PrefetchScalarGridSpec(
            num_scalar_prefetch=1, grid=(4, nt),
            in_specs=[_bs((1, tr, cols), lambda j, t, core_ref: (j, core_ref[0] * nt + t, 0)), out],
            out_specs=[out, out]),
        out_shape=[jax.ShapeDtypeStruct((4, h, cols), F32), jax.ShapeDtypeStruct((4, h, cols), BF16)],
        compiler_params=_params(("parallel", "parallel")),
    )(core, own, theirs)


def _swap_sibling(arrays, name):
    n = len(arrays)

    def body(*refs):
        ins, outs = refs[:n], refs[n:2 * n]
        send_sems, recv_sems = refs[2 * n:]
        sib = (lax.axis_index("x"), lax.axis_index("y"), 1 - lax.axis_index("c"))
        cps = []
        for i in range(n):
            rc = pltpu.make_async_remote_copy(src_ref=ins[i], dst_ref=outs[i], send_sem=send_sems.at[i],
                                              recv_sem=recv_sems.at[i], device_id=sib, device_id_type=MESH)
            rc.start()
            cps.append(rc)
        for rc in cps:
            rc.wait_recv()
        for rc in cps:
            rc.wait_send()

    return pl.pallas_call(
        body, name=name, in_specs=[ANY] * n, out_specs=[ANY] * n,
        out_shape=[jax.ShapeDtypeStruct(a.shape, a.dtype) for a in arrays],
        scratch_shapes=[pltpu.SemaphoreType.DMA((n,)), pltpu.SemaphoreType.DMA((n,))],
        compiler_params=pltpu.CompilerParams(has_side_effects=True),
    )(*arrays)


def _all_reduce_small(v):
    rows = v.shape[0]
    h = rows // 2

    def body(v_ref, o_ref, sib, pair, buf, send_sems, recv_sems):
        x, y, c = lax.axis_index("x"), lax.axis_index("y"), lax.axis_index("c")
        me = 2 * x + y
        sibling = (x, y, 1 - c)
        mine = pl.ds(pl.multiple_of(c * h, 8), h)
        theirs = pl.ds(pl.multiple_of((1 - c) * h, 8), h)

        def copy(src, dst, k, to):
            return pltpu.make_async_remote_copy(src_ref=src, dst_ref=dst, send_sem=send_sems.at[k],
                                                recv_sem=recv_sems.at[k], device_id=to, device_id_type=MESH)

        swap = copy(v_ref, sib, 0, sibling)
        swap.start()
        swap.wait_recv()
        pair[...] = v_ref[...] + sib[...]
        buf[me] = pair[mine, :]
        out = [copy(buf.at[me], buf.at[me], 1 + j, (px, py, c)) for j, (px, py) in enumerate(_other_chips(x, y))]
        for rc in out:
            rc.start()
        for j, (px, py) in enumerate(_other_chips(x, y)):
            copy(buf.at[me], buf.at[2 * px + py], 1 + j, (px, py, c)).wait_recv()
        o_ref[mine, :] = (buf[0] + buf[1]) + (buf[2] + buf[3])
        back = copy(o_ref.at[mine], o_ref.at[mine], 4, sibling)
        back.start()
        copy(o_ref.at[theirs], o_ref.at[theirs], 4, sibling).wait_recv()
        for rc in [swap, back] + out:
            rc.wait_send()

    vmem = pl.BlockSpec(memory_space=pltpu.VMEM)
    return pl.pallas_call(
        body, name="all_reduce_small", in_specs=[vmem], out_specs=vmem,
        out_shape=jax.ShapeDtypeStruct((rows, 128), F32),
        scratch_shapes=[pltpu.VMEM((rows, 128), F32), pltpu.VMEM((rows, 128), F32), pltpu.VMEM((4, h, 128), F32),
                        pltpu.SemaphoreType.DMA((5,)), pltpu.SemaphoreType.DMA((5,))],
        compiler_params=pltpu.CompilerParams(has_side_effects=True, vmem_limit_bytes=VMEM_LIMIT),
    )(v)


def _rope_tables(s):
    half = HD // 2
    inv = 10000.0 ** (-jnp.arange(half, dtype=F32) / half)
    ang = jnp.arange(s, dtype=F32)[:, None] * inv[None, :]
    cos, sin = jnp.cos(ang), jnp.sin(ang)
    return jnp.concatenate([cos, cos], axis=1), jnp.concatenate([sin, sin], axis=1)


LATE = ['attn_w_o', 'rwkv_w_o', 'x_w_kv', 'x_w_o', 'w_out']


def _local_step(x, mem, target, norm_g, mem_norm_g, w_in, gate_b, gq, gk, sink, wa, mu, k_k, k_a, r_k, w0, w2, a0, a2,
                ln_w, ln_b, wr, w_kv, gxq, gxk, wx, w_out, late_shards=None, early_exchange=None):
    s = x.shape[0]
    cos, sin = _rope_tables(s)
    r_k = r_k.reshape(1, RW)

    proj, h = _proj_fwd(x, norm_g, w_in)
    ya = _attn_fwd(proj, cos, sin, gq, gk, sink)
    ps = _shift_fwd(proj, mu)
    kk, dec0, kd0, b0, dec1, kd1, b1 = _pre_fwd(ps, k_k, k_a, w0, w2, a0, a2)
    ((y0, ck0), (y1, ck1)), stacks = _scan2_fwd([(dec0, kd0, b0), (dec1, kd1, b1)], ps, kk, gather=late_shards or ())
    if late_shards:
        st = dict(zip(LATE, stacks))
        wa, wr, wx = (_unshard_cols(st[n]) for n in ('attn_w_o', 'rwkv_w_o', 'x_w_o'))
        w_kv, w_out = st['x_w_kv'].reshape(D, 2 * XW), st['w_out'].reshape(D, D)
    mkv, mn = _mem_kv(mem, mem_norm_g, w_kv)
    yx = _xattn_fwd(proj, mkv, gxq, gxk)
    yr = _post_fwd(y0, y1, ps, kd0, kd1, proj, r_k, ln_w, ln_b)
    merged = _merge_fwd(ya, yr, yx, wa, wr, wx, proj, gate_b)
    loss_tile, dout, dout16 = _out_fwd(merged, w_out, x, target)
    loss_sum = loss_tile[0, 0]

    g = {}
    t16 = lambda a: a.astype(BF16).T
    sk = min(1024, s)
    dmerged = _matmul(dout16, w_out, mode="nt", m=s, n=D, k=D, tm=sk, tn=1024, tk=1024, name="dmerged")
    g["w_out"], g["w_out_bf16"] = _matmul(merged.T, dout16, mode="nn", m=D, n=D, k=s, tm=1024, tn=1024, tk=sk,
                                          name="grad_w_out", twin=True)
    dg0, dg1, dg2, du0, du1, du2, dya, dyr, dyx = _merge_bwd(ya, yr, yx, wa, wr, wx, proj, gate_b, dmerged)
    for n, y, du in (("attn_w_o", ya, du0), ("rwkv_w_o", yr, du1), ("x_w_o", yx, du2)):
        g[n], g[n + "_bf16"] = _matmul(t16(y), du, mode="nn", m=y.shape[1], n=D, k=s, tm=y.shape[1], tn=512, tk=s,
                                       name="grad_" + n, shards=4, twin=True)
    dmg = jnp.concatenate([dg0, dg1, dg2], axis=1)
    g["gate_b"] = _colsum(dmg, "grad_gate_b")

    daq, dak, dav, dag, g["attn_q_norm_g"], g["attn_k_norm_g"], g["attn_sink"] = _attn_bwd(proj, cos, sin, gq, gk, sink, dya)

    dxq, dxg, dmkv, g["x_q_norm_g"], g["x_k_norm_g"] = _xattn_bwd(proj, mkv, gxq, gxk, dyx)
    g["x_w_kv"], g["x_w_kv_bf16"] = _matmul(mn, dmkv, mode="tn", m=D, n=2 * XW, k=NMEM, tm=512, tn=512, tk=NMEM,
                                            name="grad_x_w_kv", twin=True)
    dmn = _matmul(dmkv, w_kv, mode="nt", m=NMEM, n=D, k=2 * XW, tm=NMEM, tn=512, tk=2 * XW, name="dmn")
    g["mem_norm_g"] = _mem_bwd(mem, dmn)

    dys, dr_p, dv_p, dkd0_p, dkd1_p, drg, g["rwkv_r_k"], g["rwkv_ln_w"], g["rwkv_ln_b"] = _post_bwd(
        y0, y1, ps, kd0, kd1, proj, r_k, ln_w, ln_b, dyr)
    sent = early_exchange(g) if early_exchange else ()
    ((dr0, dd0, db0, dk0, dkk0, dv0), (dr1, dd1, db1, dk1, dkk1, dv1)), received = _scan2_bwd(
        [(dec0, kd0, b0, ck0), (dec1, kd1, b1, ck1)], ps, kk, dys, scatter=sent)
    dr = dr_p + dr0 + dr1
    dv = dv_p + dv0 + dv1
    cts = (dkk0 + dkk1, dd0, dk0 + dkd0_p, db0, dd1, dk1 + dkd1_p, db1)
    dps, g["rwkv_k_k"], g["rwkv_k_a"], g["rwkv_w0"], g["rwkv_w2"], g["rwkv_a0"], g["rwkv_a2"] = _pre_bwd(
        ps, k_k, k_a, w0, w2, a0, a2, dr, dv, cts)
    drs, g["rwkv_mu"] = _shift_bwd(proj, mu, dps)

    dproj = jnp.concatenate([daq.astype(BF16), dak.astype(BF16), dav.astype(BF16), dag.astype(BF16), drs.astype(BF16),
                             drg.astype(BF16), dxq.astype(BF16), dxg.astype(BF16), dmg], axis=1)
    dproj4 = jnp.stack([dproj[:, j * (NIN // 4):(j + 1) * (NIN // 4)] for j in range(4)])
    g["w_in"], g["w_in_bf16"] = _grad_w_in(h.T, dproj4)
    g["rwkv_r_k"] = g["rwkv_r_k"].reshape(AH, HD)
    return loss_sum, g, (dproj, w_in, x, norm_g, dout), received


WEIGHTS = ['norm_g', 'mem_norm_g', 'w_in', 'gate_b', 'attn_q_norm_g', 'attn_k_norm_g', 'attn_sink', 'attn_w_o',
           'rwkv_mu', 'rwkv_k_k', 'rwkv_k_a', 'rwkv_r_k', 'rwkv_w0', 'rwkv_w2', 'rwkv_a0', 'rwkv_a2', 'rwkv_ln_w',
           'rwkv_ln_b', 'rwkv_w_o', 'x_w_kv', 'x_q_norm_g', 'x_k_norm_g', 'x_w_o', 'w_out']
BIG = ['w_in', 'attn_w_o', 'rwkv_w_o', 'x_w_kv', 'x_w_o', 'w_out']
COL_SHARDED = ['w_in', 'attn_w_o', 'rwkv_w_o', 'x_w_o']
LORA = ['rwkv_w0', 'rwkv_w2', 'rwkv_a0', 'rwkv_a2']
SMALL = [n for n in WEIGHTS if n not in BIG]


def _unshard_cols(stack):
    return jnp.concatenate([stack[i] for i in range(4)], axis=-1)


def kernel(x, mem, norm_g, mem_norm_g, w_in, gate_b, attn_q_norm_g, attn_k_norm_g, attn_sink, attn_w_o, rwkv_mu, rwkv_k_k, rwkv_k_a, rwkv_r_k, rwkv_w0, rwkv_w2, rwkv_a0, rwkv_a2, rwkv_ln_w, rwkv_ln_b, rwkv_w_o, x_w_kv, x_q_norm_g, x_k_norm_g, x_w_o, w_out, loss_target, m_norm_g, m_mem_norm_g, m_w_in, m_gate_b, m_attn_q_norm_g, m_attn_k_norm_g, m_attn_sink, m_attn_w_o, m_rwkv_mu, m_rwkv_k_k, m_rwkv_k_a, m_rwkv_r_k, m_rwkv_w0, m_rwkv_w2, m_rwkv_a0, m_rwkv_a2, m_rwkv_ln_w, m_rwkv_ln_b, m_rwkv_w_o, m_x_w_kv, m_x_q_norm_g, m_x_k_norm_g, m_x_w_o, m_w_out, v_norm_g, v_mem_norm_g, v_w_in, v_gate_b, v_attn_q_norm_g, v_attn_k_norm_g, v_attn_sink, v_attn_w_o, v_rwkv_mu, v_rwkv_k_k, v_rwkv_k_a, v_rwkv_r_k, v_rwkv_w0, v_rwkv_w2, v_rwkv_a0, v_rwkv_a2, v_rwkv_ln_w, v_rwkv_ln_b, v_rwkv_w_o, v_x_w_kv, v_x_q_norm_g, v_x_k_norm_g, v_x_w_o, v_w_out):
    args = dict(locals())
    canon = lambda a: a[0] if a.ndim > 2 else a
    w = {n: canon(args[n]) for n in WEIGHTS}
    m = {n: canon(args["m_" + n]) for n in WEIGHTS}
    v = {n: canon(args["v_" + n]) for n in WEIGHTS}
    shard = 2 * lax.axis_index("x") + lax.axis_index("y")

    now = ["w_in"] + LORA
    local = [w["w_in"].astype(BF16)] + [w[n].reshape(2, -1, w[n].shape[-1]) for n in LORA]
    stacks = dict(zip(now, _gather_shards(local, "gather_weights")))
    full = {"w_in": _unshard_cols(stacks["w_in"])}
    for n in LORA:
        full[n] = _unshard_cols(stacks[n]).reshape(w[n].shape[:-1] + (RW,))

    core = lax.axis_index("c").astype(jnp.int32).reshape(1)
    pair32 = {}

    def as_stack(g, n, dtype):
        t = g[n] if dtype == F32 else g[n + "_bf16"]
        return t if n in COL_SHARDED else t.reshape((4, t.shape[0] // 4) + t.shape[1:])

    def pair_sums(g, names, tag):
        sibling = _pair_exchange([as_stack(g, n, BF16) for n in names], "pair_exchange_" + tag)
        sent = []
        for n, th in zip(names, sibling):
            pair32[n], a16 = _pair_sum(as_stack(g, n, F32), th, core, "pair_sum_" + n)
            sent.append(a16)
        return sent

    loss_sum, g, deferred, recv_late = _local_step(
        x[0], mem[0], loss_target[0], w["norm_g"], w["mem_norm_g"], full["w_in"], w["gate_b"], w["attn_q_norm_g"],
        w["attn_k_norm_g"], w["attn_sink"], None, w["rwkv_mu"], w["rwkv_k_k"], w["rwkv_k_a"], w["rwkv_r_k"],
        full["rwkv_w0"], full["rwkv_w2"], full["rwkv_a0"], full["rwkv_a2"], w["rwkv_ln_w"], w["rwkv_ln_b"],
        None, None, w["x_q_norm_g"], w["x_k_norm_g"], None, None,
        late_shards=[w[n].astype(BF16) for n in LATE], early_exchange=lambda g: pair_sums(g, LATE, "late"))

    loss = lax.psum(0.5 * loss_sum / D, ("x", "y", "c"))

    grad_x, g["norm_g"], recv_w_in = _in_bwd(*deferred, stacks=pair_sums(g, ["w_in"], "w_in"))
    halves = []
    for n, r in zip(BIG, recv_w_in + recv_late):
        own = lax.dynamic_index_in_dim(pair32[n], shard, 0, keepdims=False)
        halves.append(_sum_parts([own, r[0], r[1], r[2]], "sum_" + n))
    other_halves = _swap_sibling(halves, "swap_halves")

    out_g, out_d, out_m, out_v = {}, {}, {}, {}
    for n, mine, theirs in zip(BIG, halves, other_halves):
        out_g[n], out_d[n], out_m[n], out_v[n] = _adamw_halves(mine, theirs, core, w[n], m[n], v[n], "adamw_" + n)

    flat = jnp.concatenate([g[n].reshape(-1) for n in SMALL])
    total = flat.shape[0]
    padded = -(-total // 2048) * 2048
    flat = jnp.pad(flat, (0, padded - total)).reshape(padded // 128, 128)
    red = _all_reduce_small(flat).reshape(-1)
    off = 0
    gs = {}
    for n in SMALL:
        size = g[n].size
        t = red[off:off + size].reshape(g[n].shape)
        off += size
        if n in LORA:
            wd = t.shape[-1] // 4
            t = lax.dynamic_slice_in_dim(t, shard * wd, wd, axis=t.ndim - 1)
        gs[n] = t

    def pack(d):
        f = jnp.concatenate([d[n].reshape(-1) for n in SMALL])
        return jnp.pad(f, (0, -(-f.shape[0] // 1024) * 1024 - f.shape[0])).reshape(-1, 128)

    pg, pd, pm, pv = _adamw([pack(gs)], pack(w), pack(m), pack(v), "adamw_small")
    off = 0
    for n in SMALL:
        size = w[n].size
        for dst, src in ((out_g, pg), (out_d, pd), (out_m, pm), (out_v, pv)):
            dst[n] = src.reshape(-1)[off:off + size].reshape(w[n].shape)
        off += size

    lead = lambda d: [d[n][None] if args[n].ndim > 2 else d[n] for n in WEIGHTS]
    return (loss, grad_x[None], *lead(out_g), *lead(out_d), *lead(out_m), *lead(out_v))
```

```python
import jax
import jax.numpy as jnp
from jax import lax
from jax.experimental import pallas as pl
from jax.experimental.pallas import tpu as pltpu

F32 = jnp.float32
BF16 = jnp.bfloat16
HI = lax.Precision.HIGH
MESH = pl.DeviceIdType.MESH

D = 2048
NMEM = 256
NORM_EPS = 1e-6
NEG_INF = -1e30
GN_EPS = 64e-5
HD = 64
AH = 12
AKV = 4
RW = 768
XH = 4
XD = 128
XW = 512
NIN = 12544
RSW = 2560
C_AQ, C_AK, C_AV, C_AG, C_RS, C_RG, C_XQ, C_XG, C_MG = 0, 768, 1024, 1280, 2048, 4608, 5376, 5888, 6400
WIN = 384
QB = 128
TC = 16
NPAIR = 6

ADAM_LR, ADAM_B1, ADAM_B2, ADAM_EPS, ADAM_WD, ADAM_STEP = 0.001, 0.9, 0.999, 1e-08, 0.01, 10

VMEM_LIMIT = 56 * 1024 * 1024


def _bs(shape, imap):
    return pl.BlockSpec(shape, imap)


def _params(sem=None, vmem=VMEM_LIMIT):
    return pltpu.CompilerParams(dimension_semantics=sem, vmem_limit_bytes=vmem)


def _dot(a, b, dims):
    return lax.dot_general(a.astype(BF16), b.astype(BF16), (dims, ((), ())), preferred_element_type=F32)


@jax.custom_vjp
def _mm_nn(a, b):
    return _dot(a, b, ((1,), (0,)))


def _mm_nn_fwd(a, b):
    return _mm_nn(a, b), (a, b)


def _mm_nn_bwd(res, ct):
    a, b = res
    return _dot(ct, b, ((1,), (1,))), _dot(a, ct, ((0,), (0,)))


_mm_nn.defvjp(_mm_nn_fwd, _mm_nn_bwd)


@jax.custom_vjp
def _mm_nt(a, b):
    return _dot(a, b, ((1,), (1,)))


def _mm_nt_fwd(a, b):
    return _mm_nt(a, b), (a, b)


def _mm_nt_bwd(res, ct):
    a, b = res
    return _dot(ct, b, ((1,), (0,))), _dot(ct, a, ((0,), (0,)))


_mm_nt.defvjp(_mm_nt_fwd, _mm_nt_bwd)


def _seg_matrix(n, seg):
    r = lax.broadcasted_iota(jnp.int32, (n, n), 0) // seg
    c = lax.broadcasted_iota(jnp.int32, (n, n), 1) // seg
    return (r == c).astype(F32)


def _rot_matrix():
    r = lax.broadcasted_iota(jnp.int32, (HD, HD), 0)
    c = lax.broadcasted_iota(jnp.int32, (HD, HD), 1)
    return jnp.where(c == r + HD // 2, 1.0, 0.0).astype(F32) - jnp.where(c == r - HD // 2, 1.0, 0.0).astype(F32)


def _hdot(a, m):
    return jnp.dot(a, m, precision=HI, preferred_element_type=F32)


def _rms(t, g):
    return t * lax.rsqrt(jnp.mean(t * t, axis=-1, keepdims=True) + NORM_EPS) * g


def _silu(t):
    return t * jax.nn.sigmoid(t)


def _softplus(z):
    return jnp.maximum(z, 0.0) + jnp.log(1.0 + jnp.exp(-jnp.abs(z)))


def _matmul(a, b, *, mode, m, n, k, tm, tn, tk, name, a_off=(0, 0), b_off=(0, 0), out_dtype=F32, shards=0, twin=False):
    nk = k // tk
    if mode == "tn":
        a_spec = _bs((tk, tm), lambda i, j, kk: (kk + a_off[0], i + a_off[1]))
        dims = ((0,), (0,))
    else:
        a_spec = _bs((tm, tk), lambda i, j, kk: (i + a_off[0], kk + a_off[1]))
        dims = ((1,), (1,)) if mode == "nt" else ((1,), (0,))
    if mode == "nt":
        b_spec = _bs((tn, tk), lambda i, j, kk: (j + b_off[0], kk + b_off[1]))
    else:
        b_spec = _bs((tk, tn), lambda i, j, kk: (kk + b_off[0], j + b_off[1]))
    if shards:
        per = n // shards // tn
        o_spec = _bs((1, tm, tn), lambda i, j, kk: (j // per, i, j % per))
        o_shape = (shards, m, n // shards)
    else:
        o_spec = _bs((tm, tn), lambda i, j, kk: (i, j))
        o_shape = (m, n)

    def body(a_ref, b_ref, *rest):
        o_refs, acc = rest[:-1], rest[-1]
        kk = pl.program_id(2)

        @pl.when(kk == 0)
        def _():
            acc[...] = jnp.zeros_like(acc)

        acc[...] += _dot(a_ref[...], b_ref[...], dims)

        @pl.when(kk == nk - 1)
        def _():
            for o_ref in o_refs:
                o_ref[...] = acc[...].astype(o_ref.dtype).reshape(o_ref.shape)

    dtypes = [out_dtype, BF16] if twin else [out_dtype]
    res = pl.pallas_call(
        body, name=name, grid=(m // tm, n // tn, nk),
        in_specs=[a_spec, b_spec], out_specs=[o_spec] * len(dtypes),
        out_shape=[jax.ShapeDtypeStruct(o_shape, dt) for dt in dtypes],
        scratch_shapes=[pltpu.VMEM((tm, tn), F32)],
        compiler_params=_params(("parallel", "parallel", "arbitrary")),
    )(a, b)
    return res if twin else res[0]


def _grad_w_in(ht, dproj4):
    s = ht.shape[1]
    ws = NIN // 4
    tm, tk = 256, s
    nk = s // tk

    def body(a_ref, b_ref, o32_ref, o16_ref, acc):
        kk = pl.program_id(2)

        @pl.when(kk == 0)
        def _():
            acc[...] = jnp.zeros_like(acc)

        acc[...] += jnp.dot(a_ref[...], b_ref[0], preferred_element_type=F32)

        @pl.when(kk == nk - 1)
        def _():
            o32_ref[0] = acc[...]
            o16_ref[0] = acc[...].astype(BF16)

    out = _bs((1, tm, ws), lambda j, i, kk: (j, i, 0))
    return pl.pallas_call(
        body, name="grad_w_in", grid=(4, D // tm, nk),
        in_specs=[_bs((tm, tk), lambda j, i, kk: (i, kk)), _bs((1, tk, ws), lambda j, i, kk: (j, kk, 0))],
        out_specs=[out, out],
        out_shape=[jax.ShapeDtypeStruct((4, D, ws), F32), jax.ShapeDtypeStruct((4, D, ws), BF16)],
        scratch_shapes=[pltpu.VMEM((tm, ws), F32)],
        compiler_params=_params(("parallel", "parallel", "arbitrary")),
    )(ht, dproj4)


def _proj_fwd(x, g, w):
    s = x.shape[0]
    tm, tn = min(1024, s), 896

    def body(x_ref, g_ref, w_ref, o_ref, h_ref, hs):
        @pl.when(pl.program_id(1) == 0)
        def _():
            h = _rms(x_ref[...], g_ref[...]).astype(BF16)
            hs[...] = h
            h_ref[...] = h

        o_ref[...] = jnp.dot(hs[...], w_ref[...], preferred_element_type=F32)

    return pl.pallas_call(
        body, name="proj_fwd", grid=(s // tm, NIN // tn),
        in_specs=[_bs((tm, D), lambda i, j: (i, 0)), _bs((1, D), lambda i, j: (0, 0)), _bs((D, tn), lambda i, j: (0, j))],
        out_specs=[_bs((tm, tn), lambda i, j: (i, j)), _bs((tm, D), lambda i, j: (i, 0))],
        out_shape=[jax.ShapeDtypeStruct((s, NIN), F32), jax.ShapeDtypeStruct((s, D), BF16)],
        scratch_shapes=[pltpu.VMEM((tm, D), BF16)],
        compiler_params=_params(("parallel", "arbitrary")),
    )(x, g, w)


def _rope(t, cos, sin, rot):
    return t * cos + _hdot(t, rot) * sin


def _attn_tile(qs, ks, vs, gs, sinks, gq, gk, cq, sq, ck, sk, mask, rot):
    heads = range(AH)
    kv = [h // (AH // AKV) for h in heads]
    kh = [_rope(_rms(ks[j], gk), ck, sk, rot) for j in range(AKV)]
    qh = [_rope(_rms(qs[h], gq), cq, sq, rot) for h in heads]
    sc = [jnp.where(mask, _mm_nt(qh[h], kh[kv[h]]) * (HD ** -0.5), NEG_INF) for h in heads]
    mx = [lax.stop_gradient(jnp.maximum(jnp.max(sc[h], axis=-1, keepdims=True), sinks[h])) for h in heads]
    p = [jnp.exp(sc[h] - mx[h]) for h in heads]
    den = [jnp.sum(p[h], axis=-1, keepdims=True) + jnp.exp(sinks[h] - mx[h]) for h in heads]
    o = [_mm_nn(p[h] / den[h], vs[kv[h]]) for h in heads]
    return [o[h] * _silu(gs[h]) for h in heads]


def _attn_load(n, s, aq_ref, ak_ref, av_ref, ag_refs, cos_ref, sin_ref, sink_ref):
    start = pl.multiple_of(jnp.clip((n - 1) * QB, 0, s - WIN), QB)
    q0 = pl.multiple_of(n * QB, QB)
    qs = [aq_ref[:, h * HD:(h + 1) * HD] for h in range(AH)]
    ks = [ak_ref[pl.ds(start, WIN), h * HD:(h + 1) * HD] for h in range(AKV)]
    vs = [av_ref[pl.ds(start, WIN), h * HD:(h + 1) * HD] for h in range(AKV)]
    gs = [ag_refs[h // 4][:, (h % 4) * HD:(h % 4 + 1) * HD] for h in range(AH)]
    sinks = [sink_ref[0:1, h:h + 1] for h in range(AH)]
    cq, sq = cos_ref[pl.ds(q0, QB), :], sin_ref[pl.ds(q0, QB), :]
    ck, sk = cos_ref[pl.ds(start, WIN), :], sin_ref[pl.ds(start, WIN), :]
    qpos = q0 + lax.broadcasted_iota(jnp.int32, (QB, WIN), 0)
    kpos = start + lax.broadcasted_iota(jnp.int32, (QB, WIN), 1)
    mask = jnp.abs(kpos - qpos) <= QB
    return start, qs, ks, vs, gs, sinks, cq, sq, ck, sk, mask


def _attn_specs(s):
    return [
        _bs((QB, 768), lambda n: (n, 0)),
        _bs((s, 256), lambda n: (0, C_AK // 256)),
        _bs((s, 256), lambda n: (0, C_AV // 256)),
        _bs((QB, 256), lambda n: (n, C_AG // 256)),
        _bs((QB, 256), lambda n: (n, C_AG // 256 + 1)),
        _bs((QB, 256), lambda n: (n, C_AG // 256 + 2)),
        _bs((s, HD), lambda n: (0, 0)),
        _bs((s, HD), lambda n: (0, 0)),
        _bs((1, HD), lambda n: (0, 0)),
        _bs((1, HD), lambda n: (0, 0)),
        _bs((1, AH), lambda n: (0, 0)),
    ]


def _attn_fwd(proj, cos, sin, gq, gk, sink):
    s = proj.shape[0]

    def body(aq_ref, ak_ref, av_ref, ag0, ag1, ag2, cos_ref, sin_ref, gq_ref, gk_ref, sink_ref, o_ref):
        n = pl.program_id(0)
        _, qs, ks, vs, gs, sinks, cq, sq, ck, sk, mask = _attn_load(
            n, s, aq_ref, ak_ref, av_ref, (ag0, ag1, ag2), cos_ref, sin_ref, sink_ref)
        outs = _attn_tile(qs, ks, vs, gs, sinks, gq_ref[...], gk_ref[...], cq, sq, ck, sk, mask, _rot_matrix())
        for h in range(AH):
            o_ref[:, h * HD:(h + 1) * HD] = outs[h]

    return pl.pallas_call(
        body, name="attn_fwd", grid=(s // QB,),
        in_specs=_attn_specs(s), out_specs=_bs((QB, 768), lambda n: (n, 0)),
        out_shape=jax.ShapeDtypeStruct((s, 768), F32),
        compiler_params=_params(("arbitrary",)),
    )(proj, proj, proj, proj, proj, proj, cos, sin, gq, gk, sink)


def _attn_bwd(proj, cos, sin, gq, gk, sink, dy):
    s = proj.shape[0]

    def body(aq_ref, ak_ref, av_ref, ag0, ag1, ag2, cos_ref, sin_ref, gq_ref, gk_ref, sink_ref, dy_ref,
             daq_ref, dak_ref, dav_ref, dag_ref, dgq_ref, dgk_ref, dsink_ref):
        n = pl.program_id(0)

        @pl.when(n == 0)
        def _():
            dak_ref[...] = jnp.zeros_like(dak_ref)
            dav_ref[...] = jnp.zeros_like(dav_ref)
            dgq_ref[...] = jnp.zeros_like(dgq_ref)
            dgk_ref[...] = jnp.zeros_like(dgk_ref)
            dsink_ref[...] = jnp.zeros_like(dsink_ref)

        start, qs, ks, vs, gs, sinks, cq, sq, ck, sk, mask = _attn_load(
            n, s, aq_ref, ak_ref, av_ref, (ag0, ag1, ag2), cos_ref, sin_ref, sink_ref)
        rot = _rot_matrix()

        def f(qs, ks, vs, gs, sinks, gq, gk):
            return _attn_tile(qs, ks, vs, gs, sinks, gq, gk, cq, sq, ck, sk, mask, rot)

        _, vjp = jax.vjp(f, qs, ks, vs, gs, sinks, gq_ref[...], gk_ref[...])
        dys = [dy_ref[:, h * HD:(h + 1) * HD] for h in range(AH)]
        dqs, dks, dvs, dgs, dsinks, dgq, dgk = vjp(dys)
        for h in range(AH):
            daq_ref[:, h * HD:(h + 1) * HD] = dqs[h]
            dag_ref[:, h * HD:(h + 1) * HD] = dgs[h]
            dsink_ref[0:1, h:h + 1] += dsinks[h]
        for h in range(AKV):
            dak_ref[pl.ds(start, WIN), h * HD:(h + 1) * HD] += dks[h]
            dav_ref[pl.ds(start, WIN), h * HD:(h + 1) * HD] += dvs[h]
        dgq_ref[...] += dgq
        dgk_ref[...] += dgk

    whole = lambda shape: _bs(shape, lambda n: (0, 0))
    return pl.pallas_call(
        body, name="attn_bwd", grid=(s // QB,),
        in_specs=_attn_specs(s) + [_bs((QB, 768), lambda n: (n, 0))],
        out_specs=[_bs((QB, 768), lambda n: (n, 0)), whole((s, 256)), whole((s, 256)), _bs((QB, 768), lambda n: (n, 0)),
                   whole((1, HD)), whole((1, HD)), whole((1, AH))],
        out_shape=[jax.ShapeDtypeStruct((s, 768), F32), jax.ShapeDtypeStruct((s, 256), F32),
                   jax.ShapeDtypeStruct((s, 256), F32), jax.ShapeDtypeStruct((s, 768), F32),
                   jax.ShapeDtypeStruct((1, HD), F32), jax.ShapeDtypeStruct((1, HD), F32),
                   jax.ShapeDtypeStruct((1, AH), F32)],
        compiler_params=_params(("arbitrary",)),
    )(proj, proj, proj, proj, proj, proj, cos, sin, gq, gk, sink, dy)


def _mem_kv(mem, g, w):
    def body(m_ref, g_ref, w_ref, o_ref, mn_ref):
        mn = _rms(m_ref[...], g_ref[...]).astype(BF16)
        mn_ref[...] = mn
        o_ref[...] = jnp.dot(mn, w_ref[...], preferred_element_type=F32)

    return pl.pallas_call(
        body, name="mem_kv",
        out_shape=[jax.ShapeDtypeStruct((NMEM, 2 * XW), F32), jax.ShapeDtypeStruct((NMEM, D), BF16)],
        compiler_params=_params(),
    )(mem, g, w)


def _xattn_tile(qs, gs, kms, vms, gxq, gxk):
    heads = range(XH)
    q = [_rms(qs[h], gxq) for h in heads]
    km = [_rms(kms[h], gxk) for h in heads]
    sc = [_mm_nt(q[h], km[h]) * (XD ** -0.5) for h in heads]
    p = [jnp.exp(sc[h] - lax.stop_gradient(jnp.max(sc[h], axis=-1, keepdims=True))) for h in heads]
    p = [p[h] / jnp.sum(p[h], axis=-1, keepdims=True) for h in heads]
    return [_mm_nn(p[h], vms[h]) * _silu(gs[h]) for h in heads]


XT = 256


def _xattn_specs():
    return [
        _bs((XT, 256), lambda i: (i, C_XQ // 256)), _bs((XT, 256), lambda i: (i, C_XQ // 256 + 1)),
        _bs((XT, 256), lambda i: (i, C_XG // 256)), _bs((XT, 256), lambda i: (i, C_XG // 256 + 1)),
        _bs((NMEM, 2 * XW), lambda i: (0, 0)),
        _bs((1, XD), lambda i: (0, 0)), _bs((1, XD), lambda i: (0, 0)),
    ]


def _xattn_load(q0, q1, g0, g1, mkv_ref):
    qs = [(q0, q1)[h // 2][:, (h % 2) * XD:(h % 2 + 1) * XD] for h in range(XH)]
    gs = [(g0, g1)[h // 2][:, (h % 2) * XD:(h % 2 + 1) * XD] for h in range(XH)]
    kms = [mkv_ref[:, h * XD:(h + 1) * XD] for h in range(XH)]
    vms = [mkv_ref[:, XW + h * XD:XW + (h + 1) * XD] for h in range(XH)]
    return qs, gs, kms, vms


def _xattn_fwd(proj, mkv, gxq, gxk):
    s = proj.shape[0]

    def body(q0, q1, g0, g1, mkv_ref, gxq_ref, gxk_ref, o_ref):
        qs, gs, kms, vms = _xattn_load(q0, q1, g0, g1, mkv_ref)
        outs = _xattn_tile(qs, gs, kms, vms, gxq_ref[...], gxk_ref[...])
        for h in range(XH):
            o_ref[:, h * XD:(h + 1) * XD] = outs[h]

    return pl.pallas_call(
        body, name="xattn_fwd", grid=(s // XT,),
        in_specs=_xattn_specs(), out_specs=_bs((XT, XW), lambda i: (i, 0)),
        out_shape=jax.ShapeDtypeStruct((s, XW), F32),
        compiler_params=_params(("arbitrary",)),
    )(proj, proj, proj, proj, mkv, gxq, gxk)


def _xattn_bwd(proj, mkv, gxq, gxk, dy):
    s = proj.shape[0]

    def body(q0, q1, g0, g1, mkv_ref, gxq_ref, gxk_ref, dy_ref, dq_ref, dg_ref, dmkv_ref, dgxq_ref, dgxk_ref):
        @pl.when(pl.program_id(0) == 0)
        def _():
            dmkv_ref[...] = jnp.zeros_like(dmkv_ref)
            dgxq_ref[...] = jnp.zeros_like(dgxq_ref)
            dgxk_ref[...] = jnp.zeros_like(dgxk_ref)

        qs, gs, kms, vms = _xattn_load(q0, q1, g0, g1, mkv_ref)
        _, vjp = jax.vjp(_xattn_tile, qs, gs, kms, vms, gxq_ref[...], gxk_ref[...])
        dqs, dgs, dkms, dvms, dgxq, dgxk = vjp([dy_ref[:, h * XD:(h + 1) * XD] for h in range(XH)])
        for h in range(XH):
            dq_ref[:, h * XD:(h + 1) * XD] = dqs[h]
            dg_ref[:, h * XD:(h + 1) * XD] = dgs[h]
            dmkv_ref[:, h * XD:(h + 1) * XD] += dkms[h]
            dmkv_ref[:, XW + h * XD:XW + (h + 1) * XD] += dvms[h]
        dgxq_ref[...] += dgxq
        dgxk_ref[...] += dgxk

    whole = lambda shape: _bs(shape, lambda i: (0, 0))
    return pl.pallas_call(
        body, name="xattn_bwd", grid=(s // XT,),
        in_specs=_xattn_specs() + [_bs((XT, XW), lambda i: (i, 0))],
        out_specs=[_bs((XT, XW), lambda i: (i, 0)), _bs((XT, XW), lambda i: (i, 0)), whole((NMEM, 2 * XW)),
                   whole((1, XD)), whole((1, XD))],
        out_shape=[jax.ShapeDtypeStruct((s, XW), F32), jax.ShapeDtypeStruct((s, XW), F32),
                   jax.ShapeDtypeStruct((NMEM, 2 * XW), F32), jax.ShapeDtypeStruct((1, XD), F32),
                   jax.ShapeDtypeStruct((1, XD), F32)],
        compiler_params=_params(("arbitrary",)),
    )(proj, proj, proj, proj, mkv, gxq, gxk, dy)


def _mem_bwd(mem, dmn):
    def body(m_ref, dmn_ref, o_ref):
        m = m_ref[...]
        r = lax.rsqrt(jnp.mean(m * m, axis=-1, keepdims=True) + NORM_EPS)
        o_ref[...] = jnp.sum(dmn_ref[...] * m * r, axis=0, keepdims=True)

    return pl.pallas_call(body, name="mem_norm_bwd", out_shape=jax.ShapeDtypeStruct((1, D), F32),
                          compiler_params=_params())(mem, dmn)


SHIFT_W = 512


def _shift_rows(p, s):
    row = lax.broadcasted_iota(jnp.int32, p.shape, 0)
    prev = jnp.where(row == 0, 0.0, pltpu.roll(p, 1, 0))
    nxt = jnp.where(row == s - 1, 0.0, pltpu.roll(p, s - 1, 0))
    return prev, nxt


def _shift_fwd(proj, mu):
    s = proj.shape[0]

    def body(p_ref, mu_ref, o_ref):
        p = p_ref[...]
        prev, nxt = _shift_rows(p, s)
        o_ref[...] = p + mu_ref[...] * (0.5 * (prev + nxt) - p)

    return pl.pallas_call(
        body, name="shift_fwd", grid=(RSW // SHIFT_W,),
        in_specs=[_bs((s, SHIFT_W), lambda j: (0, C_RS // SHIFT_W + j)), _bs((1, SHIFT_W), lambda j: (0, j))],
        out_specs=_bs((s, SHIFT_W), lambda j: (0, j)),
        out_shape=jax.ShapeDtypeStruct((s, RSW), F32),
        compiler_params=_params(("parallel",)),
    )(proj, mu)


def _shift_bwd(proj, mu, dps):
    s = proj.shape[0]

    def body(p_ref, mu_ref, g_ref, o_ref, dmu_ref):
        p, g, mu_v = p_ref[...], g_ref[...], mu_ref[...]
        prev, nxt = _shift_rows(p, s)
        dmu_ref[...] = jnp.sum(g * (0.5 * (prev + nxt) - p), axis=0, keepdims=True)
        mg = mu_v * g
        down, up = _shift_rows(mg, s)
        o_ref[...] = g * (1.0 - mu_v) + 0.5 * (down + up)

    return pl.pallas_call(
        body, name="shift_bwd", grid=(RSW // SHIFT_W,),
        in_specs=[_bs((s, SHIFT_W), lambda j: (0, C_RS // SHIFT_W + j)), _bs((1, SHIFT_W), lambda j: (0, j)),
                  _bs((s, SHIFT_W), lambda j: (0, j))],
        out_specs=[_bs((s, SHIFT_W), lambda j: (0, j)), _bs((1, SHIFT_W), lambda j: (0, j))],
        out_shape=[jax.ShapeDtypeStruct((s, RSW), F32), jax.ShapeDtypeStruct((1, RSW), F32)],
        compiler_params=_params(("parallel",)),
    )(proj, mu, dps)


def _pre_tile(k, wf, wb, af, ab, k_k, k_a, w0s, w2s, a0s, a2s, seg):
    kx = k * k_k
    ss = _hdot(kx * kx, seg)
    kk = kx / jnp.maximum(jnp.sqrt(ss), 1e-12)
    outs = [kk]
    for d, (w_in, a_in) in enumerate(((wf, af), (wb, ab))):
        z = w0s[d] + _mm_nn(jnp.tanh(w_in), w2s[d])
        wd = -_softplus(-z) - 0.5
        dec = jnp.exp(-jnp.exp(wd))
        ad = jax.nn.sigmoid(a0s[d] + _mm_nn(a_in, a2s[d]))
        kd = k * (1.0 + (ad - 1.0) * k_a)
        outs += [dec, kd, kk * ad]
    return outs


PT = 256


def _pre_load(ps_ref, kk_ref, ka_ref, w0_ref, w2_ref, a0_ref, a2_ref):
    k = ps_ref[:, RW:2 * RW]
    wf, wb = ps_ref[:, 3 * RW:3 * RW + 64], ps_ref[:, 3 * RW + 64:3 * RW + 128]
    af, ab = ps_ref[:, 3 * RW + 128:3 * RW + 192], ps_ref[:, 3 * RW + 192:3 * RW + 256]
    w0s = [w0_ref[0:1, :], w0_ref[1:2, :]]
    a0s = [a0_ref[0:1, :], a0_ref[1:2, :]]
    w2s = [w2_ref[0], w2_ref[1]]
    a2s = [a2_ref[0], a2_ref[1]]
    return (k, wf, wb, af, ab, kk_ref[...], ka_ref[...], w0s, w2s, a0s, a2s)


def _pre_specs():
    c = lambda shape: _bs(shape, lambda i: tuple(0 for _ in shape))
    return [_bs((PT, RSW), lambda i: (i, 0)), c((1, RW)), c((1, RW)), c((2, RW)), c((2, 64, RW)), c((2, RW)),
            c((2, 64, RW))]


def _pre_fwd(ps, k_k, k_a, w0, w2, a0, a2):
    s = ps.shape[0]

    def body(ps_ref, kk_ref, ka_ref, w0_ref, w2_ref, a0_ref, a2_ref, *outs):
        args = _pre_load(ps_ref, kk_ref, ka_ref, w0_ref, w2_ref, a0_ref, a2_ref)
        res = _pre_tile(*args, _seg_matrix(RW, HD))
        for o_ref, v in zip(outs, res):
            o_ref[...] = v

    return pl.pallas_call(
        body, name="rwkv_pre_fwd", grid=(s // PT,),
        in_specs=_pre_specs(), out_specs=[_bs((PT, RW), lambda i: (i, 0))] * 7,
        out_shape=[jax.ShapeDtypeStruct((s, RW), F32)] * 7,
        compiler_params=_params(("parallel",)),
    )(ps, k_k, k_a, w0, w2, a0, a2)


def _pre_bwd(ps, k_k, k_a, w0, w2, a0, a2, dr, dv, cts):
    s = ps.shape[0]

    def body(ps_ref, kk_ref, ka_ref, w0_ref, w2_ref, a0_ref, a2_ref, dr_ref, dv_ref, c0, c1, c2, c3, c4, c5, c6,
             dps_ref, dkk_ref, dka_ref, dw0_ref, dw2_ref, da0_ref, da2_ref):
        @pl.when(pl.program_id(0) == 0)
        def _():
            for r in (dkk_ref, dka_ref, dw0_ref, dw2_ref, da0_ref, da2_ref):
                r[...] = jnp.zeros_like(r)

        args = _pre_load(ps_ref, kk_ref, ka_ref, w0_ref, w2_ref, a0_ref, a2_ref)
        seg = _seg_matrix(RW, HD)
        _, vjp = jax.vjp(lambda *a: _pre_tile(*a, seg), *args)
        dk, dwf, dwb, daf, dab, dk_k, dk_a, dw0s, dw2s, da0s, da2s = vjp([c[...] for c in (c0, c1, c2, c3, c4, c5, c6)])
        dps_ref[:, 0:RW] = dr_ref[...]
        dps_ref[:, RW:2 * RW] = dk
        dps_ref[:, 2 * RW:3 * RW] = dv_ref[...]
        for j, t in enumerate((dwf, dwb, daf, dab)):
            dps_ref[:, 3 * RW + 64 * j:3 * RW + 64 * (j + 1)] = t
        dkk_ref[...] += dk_k
        dka_ref[...] += dk_a
        for d in range(2):
            dw0_ref[d:d + 1, :] += dw0s[d]
            da0_ref[d:d + 1, :] += da0s[d]
            dw2_ref[d] += dw2s[d]
            da2_ref[d] += da2s[d]

    c = lambda shape: _bs(shape, lambda i: tuple(0 for _ in shape))
    row = _bs((PT, RW), lambda i: (i, 0))
    return pl.pallas_call(
        body, name="rwkv_pre_bwd", grid=(s // PT,),
        in_specs=_pre_specs() + [row] * 9,
        out_specs=[_bs((PT, RSW), lambda i: (i, 0)), c((1, RW)), c((1, RW)), c((2, RW)), c((2, 64, RW)), c((2, RW)),
                   c((2, 64, RW))],
        out_shape=[jax.ShapeDtypeStruct((s, RSW), F32), jax.ShapeDtypeStruct((1, RW), F32),
                   jax.ShapeDtypeStruct((1, RW), F32), jax.ShapeDtypeStruct((2, RW), F32),
                   jax.ShapeDtypeStruct((2, 64, RW), F32), jax.ShapeDtypeStruct((2, RW), F32),
                   jax.ShapeDtypeStruct((2, 64, RW), F32)],
        compiler_params=_params(("arbitrary",)),
    )(ps, k_k, k_a, w0, w2, a0, a2, dr, dv, *cts)


def _post_tile(y0, y1, r, v, kd0, kd1, rg, r_k, ln_w, ln_b, seg):
    ysum = y0 + y1
    bonus = (_hdot(r * kd0 * r_k, seg) + _hdot(r * kd1 * r_k, seg)) * v
    mean = _hdot(ysum, seg) * (1.0 / HD)
    cen = ysum - mean
    var = _hdot(cen * cen, seg) * (1.0 / HD)
    y = cen * lax.rsqrt(var + GN_EPS) * ln_w + ln_b + bonus
    return y * _silu(rg)


def _post_specs():
    row = _bs((PT, RW), lambda i: (i, 0))
    c = _bs((1, RW), lambda i: (0, 0))
    return [row, row, _bs((PT, RW), lambda i: (i, 0)), _bs((PT, RW), lambda i: (i, 2)), row, row,
            _bs((PT, RW), lambda i: (i, C_RG // RW)), c, c, c]


def _post_fwd(y0, y1, ps, kd0, kd1, proj, r_k, ln_w, ln_b):
    s = ps.shape[0]

    def body(y0_ref, y1_ref, r_ref, v_ref, kd0_ref, kd1_ref, rg_ref, rk_ref, lw_ref, lb_ref, o_ref):
        o_ref[...] = _post_tile(y0_ref[...], y1_ref[...], r_ref[...], v_ref[...], kd0_ref[...], kd1_ref[...],
                                rg_ref[...], rk_ref[...], lw_ref[...], lb_ref[...], _seg_matrix(RW, HD))

    return pl.pallas_call(
        body, name="rwkv_post_fwd", grid=(s // PT,),
        in_specs=_post_specs(), out_specs=_bs((PT, RW), lambda i: (i, 0)),
        out_shape=jax.ShapeDtypeStruct((s, RW), F32),
        compiler_params=_params(("parallel",)),
    )(y0, y1, ps, ps, kd0, kd1, proj, r_k, ln_w, ln_b)


def _post_bwd(y0, y1, ps, kd0, kd1, proj, r_k, ln_w, ln_b, dy):
    s = ps.shape[0]

    def body(y0_ref, y1_ref, r_ref, v_ref, kd0_ref, kd1_ref, rg_ref, rk_ref, lw_ref, lb_ref, dy_ref,
             dys_ref, dr_ref, dv_ref, dkd0_ref, dkd1_ref, drg_ref, drk_ref, dlw_ref, dlb_ref):
        @pl.when(pl.program_id(0) == 0)
        def _():
            for r in (drk_ref, dlw_ref, dlb_ref):
                r[...] = jnp.zeros_like(r)

        seg = _seg_matrix(RW, HD)
        args = [t[...] for t in (y0_ref, y1_ref, r_ref, v_ref, kd0_ref, kd1_ref, rg_ref, rk_ref, lw_ref, lb_ref)]
        _, vjp = jax.vjp(lambda *a: _post_tile(*a, seg), *args)
        dy0, _, dr, dv, dkd0, dkd1, drg, drk, dlw, dlb = vjp(dy_ref[...])
        dys_ref[...] = dy0
        dr_ref[...] = dr
        dv_ref[...] = dv
        dkd0_ref[...] = dkd0
        dkd1_ref[...] = dkd1
        drg_ref[...] = drg
        drk_ref[...] += drk
        dlw_ref[...] += dlw
        dlb_ref[...] += dlb

    row = _bs((PT, RW), lambda i: (i, 0))
    c = _bs((1, RW), lambda i: (0, 0))
    return pl.pallas_call(
        body, name="rwkv_post_bwd", grid=(s // PT,),
        in_specs=_post_specs() + [row], out_specs=[row] * 6 + [c] * 3,
        out_shape=[jax.ShapeDtypeStruct((s, RW), F32)] * 6 + [jax.ShapeDtypeStruct((1, RW), F32)] * 3,
        compiler_params=_params(("arbitrary",)),
    )(y0, y1, ps, ps, kd0, kd1, proj, r_k, ln_w, ln_b, dy)


def _ones1():
    r = lax.broadcasted_iota(jnp.int32, (128, 128), 0) // HD
    c = lax.broadcasted_iota(jnp.int32, (128, 128), 1) // HD
    return (r == c).astype(BF16)


def _scan_specs(direction, nc, fwd_order):
    def tb(c):
        sc = c if fwd_order else nc - 1 - c
        return sc if direction == 0 else nc - 1 - sc

    row = _bs((TC, RW), lambda c: (tb(c), 0))
    rowv = _bs((TC, RW), lambda c: (tb(c), 2))
    return row, rowv


def _tiles(res, k):
    n = NPAIR * HD
    return [res[k * n + p * HD:k * n + (p + 1) * HD] for p in range(NPAIR)]


def _rows_to_tiles(src_ref, rows8, stage, out_s, base):
    for p in range(NPAIR):
        stage[base + p, 0:8, 0:HD] = src_ref[rows8, p * 128:p * 128 + HD]
        stage[base + p, HD:HD + 8, 0:HD] = src_ref[rows8, p * 128 + HD:(p + 1) * 128]
        out_s[base + p] = stage[base + p].T[0:HD].astype(BF16)


def _tiles_to_rows(tile_s, base, dst_ref, rows8):
    for p in range(NPAIR):
        t = jnp.concatenate([tile_s[base + p], jnp.zeros((HD, 128), F32)], axis=0).T
        dst_ref[rows8, p * 128:p * 128 + HD] = t[0:8, 0:HD]
        dst_ref[rows8, p * 128 + HD:(p + 1) * 128] = t[HD:HD + 8, 0:HD]


def _put_cols(tile_s, base, u, tiles):
    for p in range(NPAIR):
        tile_s[base + p, :, u:u + 1] = tiles[p][:, u:u + 1]
        tile_s[base + p, :, HD + u:HD + u + 1] = tiles[p][:, HD + u:HD + u + 1]


def _scan2_fwd(per_dir, ps, kk, gather=()):
    s = ps.shape[0]
    nc, ng = s // TC, TC // 8
    ngat = len(gather)
    in_specs, operands, out_specs, out_shape = [], [], [], []
    for d in (0, 1):
        row, rowv = _scan_specs(d, nc, True)
        in_specs += [row] * 5 + [rowv]
        operands += list(per_dir[d]) + [ps, kk, ps]
        out_specs += [row, _bs((1, NPAIR, HD, 128), lambda c: (c, 0, 0, 0))]
        out_shape += [jax.ShapeDtypeStruct((s, RW), F32), jax.ShapeDtypeStruct((nc, NPAIR, HD, 128), F32)]
    in_specs += [ANY] * ngat
    operands += list(gather)
    out_specs += [ANY] * ngat
    out_shape += _gather_out_shapes(gather)

    def body(*refs):
        ins = [refs[0:6], refs[6:12]]
        base = 12 + ngat
        y_refs, ck_refs = (refs[base], refs[base + 2]), (refs[base + 1], refs[base + 3])
        st, vt_s, yt_s, stage = refs[base + 4 + ngat:base + 8 + ngat]
        if ngat:
            g_start, g_forward, g_finish = _gather_phases(
                gather, refs[12:base], refs[base + 4:base + 4 + ngat], refs[base + 8 + ngat:])

        @pl.when(pl.program_id(0) == 0)
        def _():
            st[...] = jnp.zeros_like(st)
            yt_s[...] = jnp.zeros_like(yt_s)
            stage[...] = jnp.zeros_like(stage)
            if ngat:
                g_start()

        if ngat:
            @pl.when(pl.program_id(0) == nc // 2)
            def _():
                g_forward()

        for d in (0, 1):
            ck_refs[d][0] = st[d * NPAIR:(d + 1) * NPAIR]
        ones1 = _ones1()
        lane_u = lax.broadcasted_iota(jnp.int32, (HD, 128), 1) % HD
        pc = [slice(p * 128, (p + 1) * 128) for p in range(NPAIR)]

        def group(gi, carry):
            gs = (gi, ng - 1 - gi)
            rows8 = [pl.ds(pl.multiple_of(gs[d] * 8, 8), 8) for d in (0, 1)]
            blk = [[q[rows8[d], :] for q in ins[d][:5]] for d in (0, 1)]
            for d in (0, 1):
                _rows_to_tiles(ins[d][5], rows8[d], stage, vt_s, d * NPAIR)
            ss = [[st[d * NPAIR + p] for p in range(NPAIR)] for d in (0, 1)]
            for ui in range(9):
                us, ups = (ui, 7 - ui), (ui - 1, 8 - ui)
                lhs1, where = [], {}
                for d in (0, 1):
                    if ui < 8:
                        where["sa", d] = len(lhs1) // NPAIR
                        lhs1 += [(ss[d][p] * blk[d][4][us[d]:us[d] + 1, pc[p]]).astype(BF16) for p in range(NPAIR)]
                        where["vb", d] = len(lhs1) // NPAIR
                        for p in range(NPAIR):
                            vt = vt_s[d * NPAIR + p]
                            lhs1.append(jnp.where(lane_u == us[d], vt, jnp.zeros_like(vt)))
                    if ui > 0:
                        where["y", d] = len(lhs1) // NPAIR
                        lhs1 += [(ss[d][p] * blk[d][3][ups[d]:ups[d] + 1, pc[p]]).astype(BF16) for p in range(NPAIR)]
                res1 = jnp.dot(jnp.concatenate(lhs1, axis=0), ones1, preferred_element_type=F32)
                for d in (0, 1):
                    d8, k8, b8, _, _ = blk[d]
                    u = us[d]
                    if ui < 8:
                        sa, vb = _tiles(res1, where["sa", d]), _tiles(res1, where["vb", d])
                        for p in range(NPAIR):
                            ss[d][p] = (ss[d][p] * d8[u:u + 1, pc[p]] - sa[p] * b8[u:u + 1, pc[p]]
                                        + vb[p] * k8[u:u + 1, pc[p]])
                    if ui > 0:
                        _put_cols(yt_s, d * NPAIR, ups[d], _tiles(res1, where["y", d]))
            for d in (0, 1):
                _tiles_to_rows(yt_s, d * NPAIR, y_refs[d], rows8[d])
                for p in range(NPAIR):
                    st[d * NPAIR + p] = ss[d][p]
            return carry

        for gi in range(ng):
            group(gi, 0)

        if ngat:
            @pl.when(pl.program_id(0) == nc - 1)
            def _():
                g_finish()

    outs = pl.pallas_call(
        body, name="rwkv_scan_fwd", grid=(nc,), in_specs=in_specs, out_specs=out_specs, out_shape=out_shape,
        scratch_shapes=[pltpu.VMEM((2 * NPAIR, HD, 128), F32), pltpu.VMEM((2 * NPAIR, HD, 128), BF16),
                        pltpu.VMEM((2 * NPAIR, HD, 128), F32), pltpu.VMEM((2 * NPAIR, 128, 128), F32)]
        + (_gather_sems(ngat) if ngat else []),
        compiler_params=pltpu.CompilerParams(dimension_semantics=("arbitrary",), vmem_limit_bytes=VMEM_LIMIT,
                                             has_side_effects=bool(ngat)),
    )(*operands)
    return [(outs[0], outs[1]), (outs[2], outs[3])], list(outs[4:])


def _scan2_bwd(per_dir, ps, kk, dy, scatter=()):
    s = ps.shape[0]
    nc, ng = s // TC, TC // 8
    nsc = len(scatter)
    in_specs, operands, out_specs, out_shape = [], [], [], []
    for d in (0, 1):
        row, rowv = _scan_specs(d, nc, False)
        dec, kd, b, ck = per_dir[d]
        in_specs += [row] * 5 + [rowv, row, _bs((1, NPAIR, HD, 128), lambda c: (nc - 1 - c, 0, 0, 0))]
        operands += [dec, kd, b, ps, kk, ps, dy, ck]
        out_specs += [row] * 6
        out_shape += [jax.ShapeDtypeStruct((s, RW), F32)] * 6
    in_specs += [ANY] * nsc
    operands += list(scatter)
    out_specs += [ANY] * nsc
    out_shape += _scatter_out_shapes(scatter)

    def body(*refs):
        ins = [refs[0:8], refs[8:16]]
        base = 16 + nsc
        outs = [refs[base:base + 6], refs[base + 6:base + 12]]
        st, sa_s, vb_s, dy_s, ds, vt_s, dyt_s, dvt_s, stage = refs[base + 12 + nsc:base + 21 + nsc]
        if nsc:
            s_start, s_finish = _scatter_phases(refs[16:base], refs[base + 12:base + 12 + nsc], refs[base + 21 + nsc:])

        @pl.when(pl.program_id(0) == 0)
        def _():
            dvt_s[...] = jnp.zeros_like(dvt_s)
            stage[...] = jnp.zeros_like(stage)
            ds[...] = jnp.zeros_like(ds)
            if nsc:
                s_start()

        for d in (0, 1):
            st[d * (TC + 1)] = ins[d][7][0]
        ones1 = _ones1()
        lane_u = lax.broadcasted_iota(jnp.int32, (HD, 128), 1) % HD
        row_id = lax.broadcasted_iota(jnp.int32, (8, 128), 0)
        pc = [slice(p * 128, (p + 1) * 128) for p in range(NPAIR)]

        def load_rows(gs):
            return [[q[pl.ds(pl.multiple_of(gs[d] * 8, 8), 8), :] for q in ins[d][:5]] for d in (0, 1)]

        def fgroup(gi, carry):
            gs = (gi, ng - 1 - gi)
            blk = load_rows(gs)
            for d in (0, 1):
                rows8 = pl.ds(pl.multiple_of(gs[d] * 8, 8), 8)
                _rows_to_tiles(ins[d][5], rows8, stage, vt_s, d * NPAIR)
                _rows_to_tiles(ins[d][6], rows8, stage, dyt_s, d * NPAIR)
            ss = [[st[d * (TC + 1) + gi * 8, p] for p in range(NPAIR)] for d in (0, 1)]
            for ui in range(8):
                us = (ui, 7 - ui)
                i = gi * 8 + ui
                lhs1 = []
                for d in (0, 1):
                    kk8 = blk[d][4]
                    lhs1 += [(ss[d][p] * kk8[us[d]:us[d] + 1, pc[p]]).astype(BF16) for p in range(NPAIR)]
                    for tile_s in (vt_s, dyt_s):
                        for p in range(NPAIR):
                            t = tile_s[d * NPAIR + p]
                            lhs1.append(jnp.where(lane_u == us[d], t, jnp.zeros_like(t)))
                res1 = jnp.dot(jnp.concatenate(lhs1, axis=0), ones1, preferred_element_type=F32)
                for d in (0, 1):
                    d8, k8, b8, _, _ = blk[d]
                    u = us[d]
                    sa, vb, dyb = _tiles(res1, 3 * d), _tiles(res1, 3 * d + 1), _tiles(res1, 3 * d + 2)
                    for p in range(NPAIR):
                        sa_s[d * TC + i, p] = sa[p]
                        vb_s[d * TC + i, p] = vb[p]
                        dy_s[d * TC + i, p] = dyb[p]
                        ss[d][p] = ss[d][p] * d8[u:u + 1, pc[p]] - sa[p] * b8[u:u + 1, pc[p]] + vb[p] * k8[u:u + 1, pc[p]]
                        st[d * (TC + 1) + i + 1, p] = ss[d][p]
            return carry

        for gi in range(ng):
            fgroup(gi, 0)

        def bgroup(gj, carry):
            gi = ng - 1 - gj
            gs = (gi, ng - 1 - gi)
            blk = load_rows(gs)
            dss = [[ds[d * NPAIR + p] for p in range(NPAIR)] for d in (0, 1)]
            acc = [[[jnp.zeros((8, 128), F32) for _ in range(5)] for _ in range(NPAIR)] for _ in (0, 1)]
            for uj in range(8):
                ui = 7 - uj
                us = (ui, 7 - ui)
                i = gi * 8 + ui
                lhs1, dyb = [], [None, None]
                for d in (0, 1):
                    _, k8, b8, r8, _ = blk[d]
                    u = us[d]
                    dyb[d] = [dy_s[d * TC + i, p] for p in range(NPAIR)]
                    for p in range(NPAIR):
                        dss[d][p] = dss[d][p] + dyb[d][p] * r8[u:u + 1, pc[p]]
                    lhs1 += [(dss[d][p] * b8[u:u + 1, pc[p]]).astype(BF16) for p in range(NPAIR)]
                    lhs1 += [(dss[d][p] * k8[u:u + 1, pc[p]]).astype(BF16) for p in range(NPAIR)]
                res1 = jnp.dot(jnp.concatenate(lhs1, axis=0), ones1, preferred_element_type=F32)
                for d in (0, 1):
                    d8, _, _, _, kk8 = blk[d]
                    u = us[d]
                    dsa, dvb = _tiles(res1, 2 * d), _tiles(res1, 2 * d + 1)
                    _put_cols(dvt_s, d * NPAIR, u, dvb)
                    for p in range(NPAIR):
                        sp, sn = st[d * (TC + 1) + i, p], st[d * (TC + 1) + i + 1, p]
                        dsv = dss[d][p]
                        vals = (jnp.sum(sn * dyb[d][p], axis=0, keepdims=True), jnp.sum(dsv * sp, axis=0, keepdims=True),
                                -jnp.sum(dsv * sa_s[d * TC + i, p], axis=0, keepdims=True),
                                jnp.sum(dsv * vb_s[d * TC + i, p], axis=0, keepdims=True),
                                -jnp.sum(sp * dsa[p], axis=0, keepdims=True))
                        acc[d][p] = [jnp.where(row_id == u, o, a_) for o, a_ in zip(vals, acc[d][p])]
                        dss[d][p] = dsv * d8[u:u + 1, pc[p]] - dsa[p] * kk8[u:u + 1, pc[p]]
            for d in (0, 1):
                rows8 = pl.ds(pl.multiple_of(gs[d] * 8, 8), 8)
                _tiles_to_rows(dvt_s, d * NPAIR, outs[d][5], rows8)
                for p in range(NPAIR):
                    ds[d * NPAIR + p] = dss[d][p]
                    for o_ref, a_ in zip(outs[d][:5], acc[d][p]):
                        o_ref[rows8, pc[p]] = a_
            return carry

        for gj in range(ng):
            bgroup(gj, 0)

        if nsc:
            @pl.when(pl.program_id(0) == nc - 1)
            def _():
                s_finish()

    chunk = lambda k: pltpu.VMEM((k, NPAIR, HD, 128), F32)
    pairs = lambda w, dt: pltpu.VMEM((2 * NPAIR, HD, w), dt)
    res = pl.pallas_call(
        body, name="rwkv_scan_bwd", grid=(nc,), in_specs=in_specs, out_specs=out_specs, out_shape=out_shape,
        scratch_shapes=[chunk(2 * (TC + 1)), chunk(2 * TC), chunk(2 * TC), chunk(2 * TC), pairs(128, F32),
                        pairs(128, BF16), pairs(128, BF16), pairs(128, F32), pltpu.VMEM((2 * NPAIR, 128, 128), F32)]
        + _scatter_sems(nsc),
        compiler_params=pltpu.CompilerParams(dimension_semantics=("arbitrary",), vmem_limit_bytes=VMEM_LIMIT,
                                             has_side_effects=bool(nsc)),
    )(*operands)
    return [res[0:6], res[6:12]], list(res[12:])


MT = 512
MN = 256


def _merge_fwd(ya, yr, yx, wa, wr, wx, proj, gate_b):
    s = ya.shape[0]

    def body(ya_ref, yr_ref, yx_ref, wa_ref, wr_ref, wx_ref, m0, m1, m2, b0, b1, b2, o_ref):
        acc = jnp.zeros((MT, MN), F32)
        for y_ref, w_ref, m_ref, b_ref in ((ya_ref, wa_ref, m0, b0), (yr_ref, wr_ref, m1, b1), (yx_ref, wx_ref, m2, b2)):
            u = _dot(y_ref[...], w_ref[...], ((1,), (0,)))
            acc = acc + jax.nn.sigmoid(m_ref[...] + b_ref[...]) * u
        o_ref[...] = acc.astype(BF16)

    mg = lambda br: _bs((MT, MN), lambda i, j: (i, C_MG // MN + br * (D // MN) + j))
    gb = lambda br: _bs((1, MN), lambda i, j: (0, br * (D // MN) + j))
    return pl.pallas_call(
        body, name="merge_fwd", grid=(s // MT, D // MN),
        in_specs=[_bs((MT, RW), lambda i, j: (i, 0)), _bs((MT, RW), lambda i, j: (i, 0)), _bs((MT, XW), lambda i, j: (i, 0)),
                  _bs((RW, MN), lambda i, j: (0, j)), _bs((RW, MN), lambda i, j: (0, j)), _bs((XW, MN), lambda i, j: (0, j)),
                  mg(0), mg(1), mg(2), gb(0), gb(1), gb(2)],
        out_specs=_bs((MT, MN), lambda i, j: (i, j)),
        out_shape=jax.ShapeDtypeStruct((s, D), BF16),
        compiler_params=_params(("parallel", "arbitrary")),
    )(ya, yr, yx, wa, wr, wx, proj, proj, proj, gate_b, gate_b, gate_b)


def _out_fwd(merged, w_out, x, target):
    s = x.shape[0]
    tm, tn = min(1024, s), 512

    def body(m_ref, w_ref, x_ref, t_ref, loss_ref, d_ref, d16_ref):
        @pl.when((pl.program_id(0) == 0) & (pl.program_id(1) == 0))
        def _():
            loss_ref[...] = jnp.zeros_like(loss_ref)

        out = x_ref[...] + jnp.dot(m_ref[...], w_ref[...], preferred_element_type=F32)
        err = out - t_ref[...]
        dout = err * (1.0 / D)
        d_ref[...] = dout
        d16_ref[...] = dout.astype(BF16)
        loss_ref[...] += jnp.sum(err * err)

    tile = _bs((tm, tn), lambda i, j: (i, j))
    return pl.pallas_call(
        body, name="out_fwd", grid=(s // tm, D // tn),
        in_specs=[_bs((tm, D), lambda i, j: (i, 0)), _bs((D, tn), lambda i, j: (0, j)), tile, tile],
        out_specs=[_bs((8, 128), lambda i, j: (0, 0)), tile, tile],
        out_shape=[jax.ShapeDtypeStruct((8, 128), F32), jax.ShapeDtypeStruct((s, D), F32),
                   jax.ShapeDtypeStruct((s, D), BF16)],
        compiler_params=_params(("arbitrary", "arbitrary")),
    )(merged, w_out, x, target)


def _merge_bwd(ya, yr, yx, wa, wr, wx, proj, gate_b, dmerged):
    s = ya.shape[0]

    def body(ya_ref, yr_ref, yx_ref, wa_ref, wr_ref, wx_ref, m0, m1, m2, b0, b1, b2, dm_ref,
             dg0, dg1, dg2, du0, du1, du2, dya_ref, dyr_ref, dyx_ref):
        @pl.when(pl.program_id(1) == 0)
        def _():
            dya_ref[...] = jnp.zeros_like(dya_ref)
            dyr_ref[...] = jnp.zeros_like(dyr_ref)
            dyx_ref[...] = jnp.zeros_like(dyx_ref)

        dm = dm_ref[...]
        branches = ((ya_ref, wa_ref, m0, b0, dg0, du0, dya_ref), (yr_ref, wr_ref, m1, b1, dg1, du1, dyr_ref),
                    (yx_ref, wx_ref, m2, b2, dg2, du2, dyx_ref))
        ws = [br[1][...] for br in branches]
        us = [_dot(br[0][...], w, ((1,), (0,))) for br, w in zip(branches, ws)]
        gts = [jax.nn.sigmoid(br[2][...] + br[3][...]) for br in branches]
        dus = [(dm * gt).astype(BF16) for gt in gts]
        for br, w, u, gt, du in zip(branches, ws, us, gts, dus):
            br[4][...] = (dm * u * gt * (1.0 - gt)).astype(BF16)
            br[5][...] = du
            br[6][...] += _dot(du, w, ((1,), (1,)))

    mg = lambda br: _bs((MT, MN), lambda i, j: (i, C_MG // MN + br * (D // MN) + j))
    gb = lambda br: _bs((1, MN), lambda i, j: (0, br * (D // MN) + j))
    tile = _bs((MT, MN), lambda i, j: (i, j))
    return pl.pallas_call(
        body, name="merge_bwd", grid=(s // MT, D // MN),
        in_specs=[_bs((MT, RW), lambda i, j: (i, 0)), _bs((MT, RW), lambda i, j: (i, 0)), _bs((MT, XW), lambda i, j: (i, 0)),
                  _bs((RW, MN), lambda i, j: (0, j)), _bs((RW, MN), lambda i, j: (0, j)), _bs((XW, MN), lambda i, j: (0, j)),
                  mg(0), mg(1), mg(2), gb(0), gb(1), gb(2), tile],
        out_specs=[tile] * 6 + [_bs((MT, RW), lambda i, j: (i, 0)), _bs((MT, RW), lambda i, j: (i, 0)),
                                _bs((MT, XW), lambda i, j: (i, 0))],
        out_shape=[jax.ShapeDtypeStruct((s, D), BF16)] * 6 + [jax.ShapeDtypeStruct((s, RW), F32),
                                                               jax.ShapeDtypeStruct((s, RW), F32),
                                                               jax.ShapeDtypeStruct((s, XW), F32)],
        compiler_params=_params(("parallel", "arbitrary")),
    )(ya, yr, yx, wa, wr, wx, proj, proj, proj, gate_b, gate_b, gate_b, dmerged)


def _colsum(a, name):
    m, n = a.shape
    tm, tn = min(2048, m), 512

    def body(a_ref, o_ref):
        @pl.when(pl.program_id(1) == 0)
        def _():
            o_ref[...] = jnp.zeros_like(o_ref)

        o_ref[...] += jnp.sum(a_ref[...].astype(F32), axis=0, keepdims=True)

    return pl.pallas_call(
        body, name=name, grid=(n // tn, m // tm),
        in_specs=[_bs((tm, tn), lambda j, i: (i, j))], out_specs=_bs((1, tn), lambda j, i: (0, j)),
        out_shape=jax.ShapeDtypeStruct((1, n), F32),
        compiler_params=_params(("parallel", "arbitrary")),
    )(a)


def _in_bwd(dproj, w_in, x, g, dout, stacks=()):
    s = x.shape[0]
    tm, tk = min(512, s), 896
    nk = NIN // tk
    ni = s // tm
    n = len(stacks)

    def body(dp_ref, w_ref, x_ref, g_ref, do_ref, *rest):
        ins, (gx_ref, gg_ref), outs = rest[:n], rest[n:n + 2], rest[n + 2:2 * n + 2]
        acc = rest[2 * n + 2]
        i, kk = pl.program_id(0), pl.program_id(1)

        if n:
            start, finish = _scatter_phases(ins, outs, rest[2 * n + 3:])

        @pl.when((i == 0) & (kk == 0))
        def _():
            gg_ref[...] = jnp.zeros_like(gg_ref)
            if n:
                start()

        @pl.when(kk == 0)
        def _():
            acc[...] = jnp.zeros_like(acc)

        acc[...] += _dot(dp_ref[...], w_ref[...], ((1,), (1,)))

        @pl.when(kk == nk - 1)
        def _():
            xv, dh, gv = x_ref[...], acc[...], g_ref[...]
            r = lax.rsqrt(jnp.mean(xv * xv, axis=-1, keepdims=True) + NORM_EPS)
            xn = xv * r
            gg_ref[...] += jnp.sum(dh * xn, axis=0, keepdims=True)
            dxn = dh * gv
            dx = r * (dxn - xn * jnp.mean(dxn * xn, axis=-1, keepdims=True))
            gx_ref[...] = do_ref[...] + dx

        if n:
            @pl.when((i == ni - 1) & (kk == nk - 1))
            def _():
                finish()

    any_spec = pl.BlockSpec(memory_space=pl.ANY)
    res = pl.pallas_call(
        body, name="in_bwd", grid=(ni, nk),
        in_specs=[_bs((tm, tk), lambda i, kk: (i, kk)), _bs((D, tk), lambda i, kk: (0, kk)),
                  _bs((tm, D), lambda i, kk: (i, 0)), _bs((1, D), lambda i, kk: (0, 0)),
                  _bs((tm, D), lambda i, kk: (i, 0))] + [any_spec] * n,
        out_specs=[_bs((tm, D), lambda i, kk: (i, 0)), _bs((1, D), lambda i, kk: (0, 0))] + [any_spec] * n,
        out_shape=[jax.ShapeDtypeStruct((s, D), F32), jax.ShapeDtypeStruct((1, D), F32)] + _scatter_out_shapes(stacks),
        scratch_shapes=[pltpu.VMEM((tm, D), F32)] + _scatter_sems(n),
        compiler_params=pltpu.CompilerParams(dimension_semantics=("arbitrary", "arbitrary"),
                                             vmem_limit_bytes=VMEM_LIMIT, has_side_effects=bool(n)),
    )(dproj, w_in, x, g, dout, *stacks)
    return res[0], res[1], list(res[2:])


def _adamw_math(w, g, m, v):
    m = ADAM_B1 * m + (1.0 - ADAM_B1) * g
    v = ADAM_B2 * v + (1.0 - ADAM_B2) * jnp.square(g)
    m_hat = m / (1.0 - ADAM_B1 ** ADAM_STEP)
    v_hat = v / (1.0 - ADAM_B2 ** ADAM_STEP)
    delta = -ADAM_LR * (m_hat / (jnp.sqrt(v_hat) + ADAM_EPS) + ADAM_WD * w)
    return delta, m, v


def _adamw(parts, w, m, v, name):
    rows, cols = w.shape
    tr = rows
    for cand in (256, 128, 64, 32, 16, 8):
        if rows % cand == 0 and cand * cols * 4 <= (1 << 20):
            tr = cand
            break
    n = len(parts)

    def body(*refs):
        g = refs[0][...].astype(F32)
        for r in refs[1:n]:
            g = g + r[...].astype(F32)
        w_ref, m_ref, v_ref, g_out, d_out, m_out, v_out = refs[n:]
        delta, m_new, v_new = _adamw_math(w_ref[...], g, m_ref[...], v_ref[...])
        g_out[...] = g
        d_out[...] = delta
        m_out[...] = m_new
        v_out[...] = v_new

    spec = _bs((tr, cols), lambda i: (i, 0))
    return pl.pallas_call(
        body, name=name, grid=(rows // tr,),
        in_specs=[spec] * (n + 3), out_specs=[spec] * 4,
        out_shape=[jax.ShapeDtypeStruct((rows, cols), F32)] * 4,
        compiler_params=_params(("parallel",)),
    )(*parts, w, m, v)


def _adamw_halves(mine, theirs, core, w, m, v, name):
    rows, cols = w.shape
    h = rows // 2
    tr = next(t for t in (256, 128, 64, 32, 16, 8) if h % t == 0 and t * cols * 4 <= (1 << 20))
    nt = h // tr

    def body(core_ref, mine_ref, theirs_ref, w_ref, m_ref, v_ref, g_out, d_out, m_out, v_out):
        is_mine = pl.program_id(0) // nt == core_ref[0]
        g = jnp.where(is_mine, mine_ref[...], theirs_ref[...])
        delta, m_new, v_new = _adamw_math(w_ref[...], g, m_ref[...], v_ref[...])
        g_out[...] = g
        d_out[...] = delta
        m_out[...] = m_new
        v_out[...] = v_new

    spec = _bs((tr, cols), lambda i, core_ref: (i, 0))
    return pl.pallas_call(
        body, name=name,
        grid_spec=pltpu.PrefetchScalarGridSpec(
            num_scalar_prefetch=1, grid=(2 * nt,),
            in_specs=[_bs((tr, cols), lambda i, core_ref: (jnp.clip(i - core_ref[0] * nt, 0, nt - 1), 0)),
                      _bs((tr, cols), lambda i, core_ref: (jnp.clip(i - (1 - core_ref[0]) * nt, 0, nt - 1), 0)),
                      spec, spec, spec],
            out_specs=[spec] * 4),
        out_shape=[jax.ShapeDtypeStruct((rows, cols), F32)] * 4,
        compiler_params=_params(("parallel",)),
    )(core, mine, theirs, w, m, v)


def _sum_parts(parts, name):
    rows, cols = parts[0].shape
    tr = rows
    for cand in (256, 128, 64, 32, 16, 8):
        if rows % cand == 0 and cand * cols * 4 <= (1 << 20):
            tr = cand
            break

    def body(*refs):
        acc = refs[0][...].astype(F32)
        for r in refs[1:-1]:
            acc = acc + r[...].astype(F32)
        refs[-1][...] = acc

    spec = _bs((tr, cols), lambda i: (i, 0))
    return pl.pallas_call(
        body, name=name, grid=(rows // tr,), in_specs=[spec] * len(parts), out_specs=spec,
        out_shape=jax.ShapeDtypeStruct((rows, cols), F32), compiler_params=_params(("parallel",)),
    )(*parts)


ANY = pl.BlockSpec(memory_space=pl.ANY)


def _other_chips(x, y):
    return [(1 - x, y), (x, 1 - y), (1 - x, 1 - y)]


def _gather_shards(arrays, name):
    n = len(arrays)

    def body(*refs):
        start, forward, finish = _gather_phases(arrays, refs[:n], refs[n:2 * n], refs[2 * n:])
        start()
        forward()
        finish()

    return pl.pallas_call(
        body, name=name, in_specs=[ANY] * n, out_specs=[ANY] * n,
        out_shape=_gather_out_shapes(arrays), scratch_shapes=_gather_sems(n),
        compiler_params=pltpu.CompilerParams(has_side_effects=True),
    )(*arrays)


def _gather_out_shapes(arrays):
    return [jax.ShapeDtypeStruct((4,) + a.shape, a.dtype) for a in arrays]


def _gather_sems(n):
    dma = lambda k: pltpu.SemaphoreType.DMA((k,))
    return [dma(3 * n), dma(3 * n), dma(3 * n), dma(3 * n), dma(n), dma(n)]


def _gather_phases(arrays, ins, outs, sems):
    n = len(arrays)
    ici_send, ici_recv, d2d_send, d2d_recv, own_send, own_recv = sems

    def place():
        x, y, c = lax.axis_index("x"), lax.axis_index("y"), lax.axis_index("c")
        return x, y, c, 2 * x + y, _other_chips(x, y)

    def half(i, who):
        h = arrays[i].shape[0] // 2
        return pl.ds(who * h, h)

    def ici(i, j, src_chip, to, c):
        return pltpu.make_async_remote_copy(
            src_ref=ins[i].at[half(i, c)], dst_ref=outs[i].at[src_chip, half(i, c)], send_sem=ici_send.at[3 * i + j],
            recv_sem=ici_recv.at[3 * i + j], device_id=to, device_id_type=MESH)

    def d2d(i, j, src_chip, who, sib):
        piece = outs[i].at[src_chip, half(i, who)]
        return pltpu.make_async_remote_copy(
            src_ref=piece, dst_ref=piece, send_sem=d2d_send.at[3 * i + j], recv_sem=d2d_recv.at[3 * i + j],
            device_id=sib, device_id_type=MESH)

    def own(i, me, sib):
        return pltpu.make_async_remote_copy(
            src_ref=ins[i], dst_ref=outs[i].at[me], send_sem=own_send.at[i], recv_sem=own_recv.at[i],
            device_id=sib, device_id_type=MESH)

    def start():
        x, y, c, me, chips = place()
        for i in range(n):
            own(i, me, (x, y, 1 - c)).start()
            for j, (px, py) in enumerate(chips):
                ici(i, j, me, (px, py, c), c).start()

    def forward():
        x, y, c, me, chips = place()
        for i in range(n):
            for j, (px, py) in enumerate(chips):
                ici(i, j, 2 * px + py, (px, py, c), c).wait_recv()
                d2d(i, j, 2 * px + py, c, (x, y, 1 - c)).start()

    def finish():
        x, y, c, me, chips = place()
        sib = (x, y, 1 - c)
        for i in range(n):
            for j, (px, py) in enumerate(chips):
                d2d(i, j, 2 * px + py, 1 - c, sib).wait_recv()
            own(i, me, sib).wait_recv()
        for i in range(n):
            own(i, me, sib).wait_send()
            for j, (px, py) in enumerate(chips):
                ici(i, j, me, (px, py, c), c).wait_send()
                d2d(i, j, 2 * px + py, c, sib).wait_send()

    return start, forward, finish


def _scatter_phases(ins, outs, sems):
    send_sems, recv_sems = sems

    def copies():
        x, y, c = lax.axis_index("x"), lax.axis_index("y"), lax.axis_index("c")
        return [pltpu.make_async_remote_copy(
            src_ref=ins[a].at[2 * qx + qy], dst_ref=outs[a].at[j], send_sem=send_sems.at[3 * a + j],
            recv_sem=recv_sems.at[3 * a + j], device_id=(qx, qy, c), device_id_type=MESH)
            for a in range(len(ins)) for j, (qx, qy) in enumerate(_other_chips(x, y))]

    def start():
        for rc in copies():
            rc.start()

    def finish():
        for rc in copies():
            rc.wait_recv()
        for rc in copies():
            rc.wait_send()

    return start, finish


def _scatter_out_shapes(stacks):
    return [jax.ShapeDtypeStruct((3,) + a.shape[1:], a.dtype) for a in stacks]


def _scatter_sems(n):
    return [pltpu.SemaphoreType.DMA((3 * n,)), pltpu.SemaphoreType.DMA((3 * n,))] if n else []


def _pair_exchange(stacks, name):
    n = len(stacks)

    def body(*refs):
        ins, outs = refs[:n], refs[n:2 * n]
        send_sems, recv_sems = refs[2 * n:]
        x, y, c = lax.axis_index("x"), lax.axis_index("y"), lax.axis_index("c")
        cps = []
        for i in range(n):
            h = stacks[i].shape[1] // 2
            rc = pltpu.make_async_remote_copy(
                src_ref=ins[i].at[:, pl.ds((1 - c) * h, h)], dst_ref=outs[i], send_sem=send_sems.at[i],
                recv_sem=recv_sems.at[i], device_id=(x, y, 1 - c), device_id_type=MESH)
            rc.start()
            cps.append(rc)
        for rc in cps:
            rc.wait_recv()
        for rc in cps:
            rc.wait_send()

    return pl.pallas_call(
        body, name=name, in_specs=[ANY] * n, out_specs=[ANY] * n,
        out_shape=[jax.ShapeDtypeStruct((4, a.shape[1] // 2) + a.shape[2:], a.dtype) for a in stacks],
        scratch_shapes=[pltpu.SemaphoreType.DMA((n,)), pltpu.SemaphoreType.DMA((n,))],
        compiler_params=pltpu.CompilerParams(has_side_effects=True),
    )(*stacks)


def _pair_sum(own, theirs, core, name):
    _, r, cols = own.shape
    h = r // 2
    tr = next(t for t in (256, 128, 64, 32, 16) if h % t == 0 and t * cols * 4 <= (1 << 20))
    nt = h // tr

    def body(core_ref, own_ref, th_ref, o32_ref, o16_ref):
        del core_ref
        acc = own_ref[...] + th_ref[...].astype(F32)
        o32_ref[...] = acc
        o16_ref[...] = acc.astype(BF16)

    out = _bs((1, tr, cols), lambda j, t, core_ref: (j, t, 0))
    return pl.pallas_call(
        body, name=name,
        grid_spec=pltpu.PrefetchScalarGridSpec(
            num_scalar_prefetch=1, grid=(4, nt),
            in_specs=[_bs((1, tr, cols), lambda j, t, core_ref: (j, core_ref[0] * nt + t, 0)), out],
            out_specs=[out, out]),
        out_shape=[jax.ShapeDtypeStruct((4, h, cols), F32), jax.ShapeDtypeStruct((4, h, cols), BF16)],
        compiler_params=_params(("parallel", "parallel")),
    )(core, own, theirs)


def _swap_sibling(arrays, name):
    n = len(arrays)

    def body(*refs):
        ins, outs = refs[:n], refs[n:2 * n]
        send_sems, recv_sems = refs[2 * n:]
        sib = (lax.axis_index("x"), lax.axis_index("y"), 1 - lax.axis_index("c"))
        cps = []
        for i in range(n):
            rc = pltpu.make_async_remote_copy(src_ref=ins[i], dst_ref=outs[i], send_sem=send_sems.at[i],
                                              recv_sem=recv_sems.at[i], device_id=sib, device_id_type=MESH)
            rc.start()
            cps.append(rc)
        for rc in cps:
            rc.wait_recv()
        for rc in cps:
            rc.wait_send()

    return pl.pallas_call(
        body, name=name, in_specs=[ANY] * n, out_specs=[ANY] * n,
        out_shape=[jax.ShapeDtypeStruct(a.shape, a.dtype) for a in arrays],
        scratch_shapes=[pltpu.SemaphoreType.DMA((n,)), pltpu.SemaphoreType.DMA((n,))],
        compiler_params=pltpu.CompilerParams(has_side_effects=True),
    )(*arrays)


def _all_reduce_small(v):
    rows = v.shape[0]
    h = rows // 2

    def body(v_ref, o_ref, sib, pair, buf, send_sems, recv_sems):
        x, y, c = lax.axis_index("x"), lax.axis_index("y"), lax.axis_index("c")
        me = 2 * x + y
        sibling = (x, y, 1 - c)
        mine = pl.ds(pl.multiple_of(c * h, 8), h)
        theirs = pl.ds(pl.multiple_of((1 - c) * h, 8), h)

        def copy(src, dst, k, to):
            return pltpu.make_async_remote_copy(src_ref=src, dst_ref=dst, send_sem=send_sems.at[k],
                                                recv_sem=recv_sems.at[k], device_id=to, device_id_type=MESH)

        swap = copy(v_ref, sib, 0, sibling)
        swap.start()
        swap.wait_recv()
        pair[...] = v_ref[...] + sib[...]
        buf[me] = pair[mine, :]
        out = [copy(buf.at[me], buf.at[me], 1 + j, (px, py, c)) for j, (px, py) in enumerate(_other_chips(x, y))]
        for rc in out:
            rc.start()
        for j, (px, py) in enumerate(_other_chips(x, y)):
            copy(buf.at[me], buf.at[2 * px + py], 1 + j, (px, py, c)).wait_recv()
        o_ref[mine, :] = (buf[0] + buf[1]) + (buf[2] + buf[3])
        back = copy(o_ref.at[mine], o_ref.at[mine], 4, sibling)
        back.start()
        copy(o_ref.at[theirs], o_ref.at[theirs], 4, sibling).wait_recv()
        for rc in [swap, back] + out:
            rc.wait_send()

    vmem = pl.BlockSpec(memory_space=pltpu.VMEM)
    return pl.pallas_call(
        body, name="all_reduce_small", in_specs=[vmem], out_specs=vmem,
        out_shape=jax.ShapeDtypeStruct((rows, 128), F32),
        scratch_shapes=[pltpu.VMEM((rows, 128), F32), pltpu.VMEM((rows, 128), F32), pltpu.VMEM((4, h, 128), F32),
                        pltpu.SemaphoreType.DMA((5,)), pltpu.SemaphoreType.DMA((5,))],
        compiler_params=pltpu.CompilerParams(has_side_effects=True, vmem_limit_bytes=VMEM_LIMIT),
    )(v)


def _rope_tables(s):
    half = HD // 2
    inv = 10000.0 ** (-jnp.arange(half, dtype=F32) / half)
    ang = jnp.arange(s, dtype=F32)[:, None] * inv[None, :]
    cos, sin = jnp.cos(ang), jnp.sin(ang)
    return jnp.concatenate([cos, cos], axis=1), jnp.concatenate([sin, sin], axis=1)


LATE = ['attn_w_o', 'rwkv_w_o', 'x_w_kv', 'x_w_o', 'w_out']


def _local_step(x, mem, target, norm_g, mem_norm_g, w_in, gate_b, gq, gk, sink, wa, mu, k_k, k_a, r_k, w0, w2, a0, a2,
                ln_w, ln_b, wr, w_kv, gxq, gxk, wx, w_out, late_shards=None, early_exchange=None):
    s = x.shape[0]
    cos, sin = _rope_tables(s)
    r_k = r_k.reshape(1, RW)

    proj, h = _proj_fwd(x, norm_g, w_in)
    ya = _attn_fwd(proj, cos, sin, gq, gk, sink)
    ps = _shift_fwd(proj, mu)
    kk, dec0, kd0, b0, dec1, kd1, b1 = _pre_fwd(ps, k_k, k_a, w0, w2, a0, a2)
    ((y0, ck0), (y1, ck1)), stacks = _scan2_fwd([(dec0, kd0, b0), (dec1, kd1, b1)], ps, kk, gather=late_shards or ())
    if late_shards:
        st = dict(zip(LATE, stacks))
        wa, wr, wx = (_unshard_cols(st[n]) for n in ('attn_w_o', 'rwkv_w_o', 'x_w_o'))
        w_kv, w_out = st['x_w_kv'].reshape(D, 2 * XW), st['w_out'].reshape(D, D)
    mkv, mn = _mem_kv(mem, mem_norm_g, w_kv)
    yx = _xattn_fwd(proj, mkv, gxq, gxk)
    yr = _post_fwd(y0, y1, ps, kd0, kd1, proj, r_k, ln_w, ln_b)
    merged = _merge_fwd(ya, yr, yx, wa, wr, wx, proj, gate_b)
    loss_tile, dout, dout16 = _out_fwd(merged, w_out, x, target)
    loss_sum = loss_tile[0, 0]

    g = {}
    t16 = lambda a: a.astype(BF16).T
    sk = min(1024, s)
    dmerged = _matmul(dout16, w_out, mode="nt", m=s, n=D, k=D, tm=sk, tn=1024, tk=1024, name="dmerged")
    g["w_out"], g["w_out_bf16"] = _matmul(merged.T, dout16, mode="nn", m=D, n=D, k=s, tm=1024, tn=1024, tk=sk,
                                          name="grad_w_out", twin=True)
    dg0, dg1, dg2, du0, du1, du2, dya, dyr, dyx = _merge_bwd(ya, yr, yx, wa, wr, wx, proj, gate_b, dmerged)
    for n, y, du in (("attn_w_o", ya, du0), ("rwkv_w_o", yr, du1), ("x_w_o", yx, du2)):
        g[n], g[n + "_bf16"] = _matmul(t16(y), du, mode="nn", m=y.shape[1], n=D, k=s, tm=y.shape[1], tn=512, tk=s,
                                       name="grad_" + n, shards=4, twin=True)
    dmg = jnp.concatenate([dg0, dg1, dg2], axis=1)
    g["gate_b"] = _colsum(dmg, "grad_gate_b")

    daq, dak, dav, dag, g["attn_q_norm_g"], g["attn_k_norm_g"], g["attn_sink"] = _attn_bwd(proj, cos, sin, gq, gk, sink, dya)

    dxq, dxg, dmkv, g["x_q_norm_g"], g["x_k_norm_g"] = _xattn_bwd(proj, mkv, gxq, gxk, dyx)
    g["x_w_kv"], g["x_w_kv_bf16"] = _matmul(mn, dmkv, mode="tn", m=D, n=2 * XW, k=NMEM, tm=512, tn=512, tk=NMEM,
                                            name="grad_x_w_kv", twin=True)
    dmn = _matmul(dmkv, w_kv, mode="nt", m=NMEM, n=D, k=2 * XW, tm=NMEM, tn=512, tk=2 * XW, name="dmn")
    g["mem_norm_g"] = _mem_bwd(mem, dmn)

    dys, dr_p, dv_p, dkd0_p, dkd1_p, drg, g["rwkv_r_k"], g["rwkv_ln_w"], g["rwkv_ln_b"] = _post_bwd(
        y0, y1, ps, kd0, kd1, proj, r_k, ln_w, ln_b, dyr)
    sent = early_exchange(g) if early_exchange else ()
    ((dr0, dd0, db0, dk0, dkk0, dv0), (dr1, dd1, db1, dk1, dkk1, dv1)), received = _scan2_bwd(
        [(dec0, kd0, b0, ck0), (dec1, kd1, b1, ck1)], ps, kk, dys, scatter=sent)
    dr = dr_p + dr0 + dr1
    dv = dv_p + dv0 + dv1
    cts = (dkk0 + dkk1, dd0, dk0 + dkd0_p, db0, dd1, dk1 + dkd1_p, db1)
    dps, g["rwkv_k_k"], g["rwkv_k_a"], g["rwkv_w0"], g["rwkv_w2"], g["rwkv_a0"], g["rwkv_a2"] = _pre_bwd(
        ps, k_k, k_a, w0, w2, a0, a2, dr, dv, cts)
    drs, g["rwkv_mu"] = _shift_bwd(proj, mu, dps)

    dproj = jnp.concatenate([daq.astype(BF16), dak.astype(BF16), dav.astype(BF16), dag.astype(BF16), drs.astype(BF16),
                             drg.astype(BF16), dxq.astype(BF16), dxg.astype(BF16), dmg], axis=1)
    dproj4 = jnp.stack([dproj[:, j * (NIN // 4):(j + 1) * (NIN // 4)] for j in range(4)])
    g["w_in"], g["w_in_bf16"] = _grad_w_in(h.T, dproj4)
    g["rwkv_r_k"] = g["rwkv_r_k"].reshape(AH, HD)
    return loss_sum, g, (dproj, w_in, x, norm_g, dout), received


WEIGHTS = ['norm_g', 'mem_norm_g', 'w_in', 'gate_b', 'attn_q_norm_g', 'attn_k_norm_g', 'attn_sink', 'attn_w_o',
           'rwkv_mu', 'rwkv_k_k', 'rwkv_k_a', 'rwkv_r_k', 'rwkv_w0', 'rwkv_w2', 'rwkv_a0', 'rwkv_a2', 'rwkv_ln_w',
           'rwkv_ln_b', 'rwkv_w_o', 'x_w_kv', 'x_q_norm_g', 'x_k_norm_g', 'x_w_o', 'w_out']
BIG = ['w_in', 'attn_w_o', 'rwkv_w_o', 'x_w_kv', 'x_w_o', 'w_out']
COL_SHARDED = ['w_in', 'attn_w_o', 'rwkv_w_o', 'x_w_o']
LORA = ['rwkv_w0', 'rwkv_w2', 'rwkv_a0', 'rwkv_a2']
SMALL = [n for n in WEIGHTS if n not in BIG]


def _unshard_cols(stack):
    return jnp.concatenate([stack[i] for i in range(4)], axis=-1)


def kernel(x, mem, norm_g, mem_norm_g, w_in, gate_b, attn_q_norm_g, attn_k_norm_g, attn_sink, attn_w_o, rwkv_mu, rwkv_k_k, rwkv_k_a, rwkv_r_k, rwkv_w0, rwkv_w2, rwkv_a0, rwkv_a2, rwkv_ln_w, rwkv_ln_b, rwkv_w_o, x_w_kv, x_q_norm_g, x_k_norm_g, x_w_o, w_out, loss_target, m_norm_g, m_mem_norm_g, m_w_in, m_gate_b, m_attn_q_norm_g, m_attn_k_norm_g, m_attn_sink, m_attn_w_o, m_rwkv_mu, m_rwkv_k_k, m_rwkv_k_a, m_rwkv_r_k, m_rwkv_w0, m_rwkv_w2, m_rwkv_a0, m_rwkv_a2, m_rwkv_ln_w, m_rwkv_ln_b, m_rwkv_w_o, m_x_w_kv, m_x_q_norm_g, m_x_k_norm_g, m_x_w_o, m_w_out, v_norm_g, v_mem_norm_g, v_w_in, v_gate_b, v_attn_q_norm_g, v_attn_k_norm_g, v_attn_sink, v_attn_w_o, v_rwkv_mu, v_rwkv_k_k, v_rwkv_k_a, v_rwkv_r_k, v_rwkv_w0, v_rwkv_w2, v_rwkv_a0, v_rwkv_a2, v_rwkv_ln_w, v_rwkv_ln_b, v_rwkv_w_o, v_x_w_kv, v_x_q_norm_g, v_x_k_norm_g, v_x_w_o, v_w_out):
    args = dict(locals())
    canon = lambda a: a[0] if a.ndim > 2 else a
    w = {n: canon(args[n]) for n in WEIGHTS}
    m = {n: canon(args["m_" + n]) for n in WEIGHTS}
    v = {n: canon(args["v_" + n]) for n in WEIGHTS}
    shard = 2 * lax.axis_index("x") + lax.axis_index("y")

    now = ["w_in"] + LORA
    local = [w["w_in"].astype(BF16)] + [w[n].reshape(2, -1, w[n].shape[-1]) for n in LORA]
    stacks = dict(zip(now, _gather_shards(local, "gather_weights")))
    full = {"w_in": _unshard_cols(stacks["w_in"])}
    for n in LORA:
        full[n] = _unshard_cols(stacks[n]).reshape(w[n].shape[:-1] + (RW,))

    core = lax.axis_index("c").astype(jnp.int32).reshape(1)
    pair32 = {}

    def as_stack(g, n, dtype):
        t = g[n] if dtype == F32 else g[n + "_bf16"]
        return t if n in COL_SHARDED else t.reshape((4, t.shape[0] // 4) + t.shape[1:])

    def pair_sums(g, names, tag):
        sibling = _pair_exchange([as_stack(g, n, BF16) for n in names], "pair_exchange_" + tag)
        sent = []
        for n, th in zip(names, sibling):
            pair32[n], a16 = _pair_sum(as_stack(g, n, F32), th, core, "pair_sum_" + n)
            sent.append(a16)
        return sent

    loss_sum, g, deferred, recv_late = _local_step(
        x[0], mem[0], loss_target[0], w["norm_g"], w["mem_norm_g"], full["w_in"], w["gate_b"], w["attn_q_norm_g"],
        w["attn_k_norm_g"], w["attn_sink"], None, w["rwkv_mu"], w["rwkv_k_k"], w["rwkv_k_a"], w["rwkv_r_k"],
        full["rwkv_w0"], full["rwkv_w2"], full["rwkv_a0"], full["rwkv_a2"], w["rwkv_ln_w"], w["rwkv_ln_b"],
        None, None, w["x_q_norm_g"], w["x_k_norm_g"], None, None,
        late_shards=[w[n].astype(BF16) for n in LATE], early_exchange=lambda g: pair_sums(g, LATE, "late"))

    loss = lax.psum(0.5 * loss_sum / D, ("x", "y", "c"))

    grad_x, g["norm_g"], recv_w_in = _in_bwd(*deferred, stacks=pair_sums(g, ["w_in"], "w_in"))
    halves = []
    for n, r in zip(BIG, recv_w_in + recv_late):
        own = lax.dynamic_index_in_dim(pair32[n], shard, 0, keepdims=False)
        halves.append(_sum_parts([own, r[0], r[1], r[2]], "sum_" + n))
    other_halves = _swap_sibling(halves, "swap_halves")

    out_g, out_d, out_m, out_v = {}, {}, {}, {}
    for n, mine, theirs in zip(BIG, halves, other_halves):
        out_g[n], out_d[n], out_m[n], out_v[n] = _adamw_halves(mine, theirs, core, w[n], m[n], v[n], "adamw_" + n)

    flat = jnp.concatenate([g[n].reshape(-1) for n in SMALL])
    total = flat.shape[0]
    padded = -(-total // 2048) * 2048
    flat = jnp.pad(flat, (0, padded - total)).reshape(padded // 128, 128)
    red = _all_reduce_small(flat).reshape(-1)
    off = 0
    gs = {}
    for n in SMALL:
        size = g[n].size
        t = red[off:off + size].reshape(g[n].shape)
        off += size
        if n in LORA:
            wd = t.shape[-1] // 4
            t = lax.dynamic_slice_in_dim(t, shard * wd, wd, axis=t.ndim - 1)
        gs[n] = t

    def pack(d):
        f = jnp.concatenate([d[n].reshape(-1) for n in SMALL])
        return jnp.pad(f, (0, -(-f.shape[0] // 1024) * 1024 - f.shape[0])).reshape(-1, 128)

    pg, pd, pm, pv = _adamw([pack(gs)], pack(w), pack(m), pack(v), "adamw_small")
    off = 0
    for n in SMALL:
        size = w[n].size
        for dst, src in ((out_g, pg), (out_d, pd), (out_m, pm), (out_v, pv)):
            dst[n] = src.reshape(-1)[off:off + size].reshape(w[n].shape)
        off += size

    lead = lambda d: [d[n][None] if args[n].ndim > 2 else d[n] for n in WEIGHTS]
    return (loss, grad_x[None], *lead(out_g), *lead(out_d), *lead(out_m), *lead(out_v))
```

```python
import jax
import jax.numpy as jnp
from jax import lax
from jax.experimental import pallas as pl
from jax.experimental.pallas import tpu as pltpu

F32 = jnp.float32
BF16 = jnp.bfloat16
HI = lax.Precision.HIGH
MESH = pl.DeviceIdType.MESH

D = 2048
NMEM = 256
NORM_EPS = 1e-6
NEG_INF = -1e30
GN_EPS = 64e-5
HD = 64
AH = 12
AKV = 4
RW = 768
XH = 4
XD = 128
XW = 512
NIN = 12544
RSW = 2560
C_AQ, C_AK, C_AV, C_AG, C_RS, C_RG, C_XQ, C_XG, C_MG = 0, 768, 1024, 1280, 2048, 4608, 5376, 5888, 6400
WIN = 384
QB = 128
TC = 16
NPAIR = 6

ADAM_LR, ADAM_B1, ADAM_B2, ADAM_EPS, ADAM_WD, ADAM_STEP = 0.001, 0.9, 0.999, 1e-08, 0.01, 10

VMEM_LIMIT = 56 * 1024 * 1024


def _bs(shape, imap):
    return pl.BlockSpec(shape, imap)


def _params(sem=None, vmem=VMEM_LIMIT):
    return pltpu.CompilerParams(dimension_semantics=sem, vmem_limit_bytes=vmem)


def _dot(a, b, dims):
    return lax.dot_general(a.astype(BF16), b.astype(BF16), (dims, ((), ())), preferred_element_type=F32)


@jax.custom_vjp
def _mm_nn(a, b):
    return _dot(a, b, ((1,), (0,)))


def _mm_nn_fwd(a, b):
    return _mm_nn(a, b), (a, b)


def _mm_nn_bwd(res, ct):
    a, b = res
    return _dot(ct, b, ((1,), (1,))), _dot(a, ct, ((0,), (0,)))


_mm_nn.defvjp(_mm_nn_fwd, _mm_nn_bwd)


@jax.custom_vjp
def _mm_nt(a, b):
    return _dot(a, b, ((1,), (1,)))


def _mm_nt_fwd(a, b):
    return _mm_nt(a, b), (a, b)


def _mm_nt_bwd(res, ct):
    a, b = res
    return _dot(ct, b, ((1,), (0,))), _dot(ct, a, ((0,), (0,)))


_mm_nt.defvjp(_mm_nt_fwd, _mm_nt_bwd)


def _seg_matrix(n, seg):
    r = lax.broadcasted_iota(jnp.int32, (n, n), 0) // seg
    c = lax.broadcasted_iota(jnp.int32, (n, n), 1) // seg
    return (r == c).astype(F32)


def _rot_matrix():
    r = lax.broadcasted_iota(jnp.int32, (HD, HD), 0)
    c = lax.broadcasted_iota(jnp.int32, (HD, HD), 1)
    return jnp.where(c == r + HD // 2, 1.0, 0.0).astype(F32) - jnp.where(c == r - HD // 2, 1.0, 0.0).astype(F32)


def _hdot(a, m):
    return jnp.dot(a, m, precision=HI, preferred_element_type=F32)


def _rms(t, g):
    return t * lax.rsqrt(jnp.mean(t * t, axis=-1, keepdims=True) + NORM_EPS) * g


def _silu(t):
    return t * jax.nn.sigmoid(t)


def _softplus(z):
    return jnp.maximum(z, 0.0) + jnp.log(1.0 + jnp.exp(-jnp.abs(z)))


def _matmul(a, b, *, mode, m, n, k, tm, tn, tk, name, a_off=(0, 0), b_off=(0, 0), out_dtype=F32, shards=0, twin=False):
    nk = k // tk
    if mode == "tn":
        a_spec = _bs((tk, tm), lambda i, j, kk: (kk + a_off[0], i + a_off[1]))
        dims = ((0,), (0,))
    else:
        a_spec = _bs((tm, tk), lambda i, j, kk: (i + a_off[0], kk + a_off[1]))
        dims = ((1,), (1,)) if mode == "nt" else ((1,), (0,))
    if mode == "nt":
        b_spec = _bs((tn, tk), lambda i, j, kk: (j + b_off[0], kk + b_off[1]))
    else:
        b_spec = _bs((tk, tn), lambda i, j, kk: (kk + b_off[0], j + b_off[1]))
    if shards:
        per = n // shards // tn
        o_spec = _bs((1, tm, tn), lambda i, j, kk: (j // per, i, j % per))
        o_shape = (shards, m, n // shards)
    else:
        o_spec = _bs((tm, tn), lambda i, j, kk: (i, j))
        o_shape = (m, n)

    def body(a_ref, b_ref, *rest):
        o_refs, acc = rest[:-1], rest[-1]
        kk = pl.program_id(2)

        @pl.when(kk == 0)
        def _():
            acc[...] = jnp.zeros_like(acc)

        acc[...] += _dot(a_ref[...], b_ref[...], dims)

        @pl.when(kk == nk - 1)
        def _():
            for o_ref in o_refs:
                o_ref[...] = acc[...].astype(o_ref.dtype).reshape(o_ref.shape)

    dtypes = [out_dtype, BF16] if twin else [out_dtype]
    res = pl.pallas_call(
        body, name=name, grid=(m // tm, n // tn, nk),
        in_specs=[a_spec, b_spec], out_specs=[o_spec] * len(dtypes),
        out_shape=[jax.ShapeDtypeStruct(o_shape, dt) for dt in dtypes],
        scratch_shapes=[pltpu.VMEM((tm, tn), F32)],
        compiler_params=_params(("parallel", "parallel", "arbitrary")),
    )(a, b)
    return res if twin else res[0]


def _grad_w_in(ht, dproj4):
    s = ht.shape[1]
    ws = NIN // 4
    tm, tk = 256, s
    nk = s // tk

    def body(a_ref, b_ref, o32_ref, o16_ref, acc):
        kk = pl.program_id(2)

        @pl.when(kk == 0)
        def _():
            acc[...] = jnp.zeros_like(acc)

        acc[...] += jnp.dot(a_ref[...], b_ref[0], preferred_element_type=F32)

        @pl.when(kk == nk - 1)
        def _():
            o32_ref[0] = acc[...]
            o16_ref[0] = acc[...].astype(BF16)

    out = _bs((1, tm, ws), lambda j, i, kk: (j, i, 0))
    return pl.pallas_call(
        body, name="grad_w_in", grid=(4, D // tm, nk),
        in_specs=[_bs((tm, tk), lambda j, i, kk: (i, kk)), _bs((1, tk, ws), lambda j, i, kk: (j, kk, 0))],
        out_specs=[out, out],
        out_shape=[jax.ShapeDtypeStruct((4, D, ws), F32), jax.ShapeDtypeStruct((4, D, ws), BF16)],
        scratch_shapes=[pltpu.VMEM((tm, ws), F32)],
        compiler_params=_params(("parallel", "parallel", "arbitrary")),
    )(ht, dproj4)


def _proj_fwd(x, g, w):
    s = x.shape[0]
    tm, tn = min(1024, s), 896

    def body(x_ref, g_ref, w_ref, o_ref, h_ref, hs):
        @pl.when(pl.program_id(1) == 0)
        def _():
            h = _rms(x_ref[...], g_ref[...]).astype(BF16)
            hs[...] = h
            h_ref[...] = h

        o_ref[...] = jnp.dot(hs[...], w_ref[...], preferred_element_type=F32)

    return pl.pallas_call(
        body, name="proj_fwd", grid=(s // tm, NIN // tn),
        in_specs=[_bs((tm, D), lambda i, j: (i, 0)), _bs((1, D), lambda i, j: (0, 0)), _bs((D, tn), lambda i, j: (0, j))],
        out_specs=[_bs((tm, tn), lambda i, j: (i, j)), _bs((tm, D), lambda i, j: (i, 0))],
        out_shape=[jax.ShapeDtypeStruct((s, NIN), F32), jax.ShapeDtypeStruct((s, D), BF16)],
        scratch_shapes=[pltpu.VMEM((tm, D), BF16)],
        compiler_params=_params(("parallel", "arbitrary")),
    )(x, g, w)


def _rope(t, cos, sin, rot):
    return t * cos + _hdot(t, rot) * sin


def _attn_tile(qs, ks, vs, gs, sinks, gq, gk, cq, sq, ck, sk, mask, rot):
    heads = range(AH)
    kv = [h // (AH // AKV) for h in heads]
    kh = [_rope(_rms(ks[j], gk), ck, sk, rot) for j in range(AKV)]
    qh = [_rope(_rms(qs[h], gq), cq, sq, rot) for h in heads]
    sc = [jnp.where(mask, _mm_nt(qh[h], kh[kv[h]]) * (HD ** -0.5), NEG_INF) for h in heads]
    mx = [lax.stop_gradient(jnp.maximum(jnp.max(sc[h], axis=-1, keepdims=True), sinks[h])) for h in heads]
    p = [jnp.exp(sc[h] - mx[h]) for h in heads]
    den = [jnp.sum(p[h], axis=-1, keepdims=True) + jnp.exp(sinks[h] - mx[h]) for h in heads]
    o = [_mm_nn(p[h] / den[h], vs[kv[h]]) for h in heads]
    return [o[h] * _silu(gs[h]) for h in heads]


def _attn_load(n, s, aq_ref, ak_ref, av_ref, ag_refs, cos_ref, sin_ref, sink_ref):
    start = pl.multiple_of(jnp.clip((n - 1) * QB, 0, s - WIN), QB)
    q0 = pl.multiple_of(n * QB, QB)
    qs = [aq_ref[:, h * HD:(h + 1) * HD] for h in range(AH)]
    ks = [ak_ref[pl.ds(start, WIN), h * HD:(h + 1) * HD] for h in range(AKV)]
    vs = [av_ref[pl.ds(start, WIN), h * HD:(h + 1) * HD] for h in range(AKV)]
    gs = [ag_refs[h // 4][:, (h % 4) * HD:(h % 4 + 1) * HD] for h in range(AH)]
    sinks = [sink_ref[0:1, h:h + 1] for h in range(AH)]
    cq, sq = cos_ref[pl.ds(q0, QB), :], sin_ref[pl.ds(q0, QB), :]
    ck, sk = cos_ref[pl.ds(start, WIN), :], sin_ref[pl.ds(start, WIN), :]
    qpos = q0 + lax.broadcasted_iota(jnp.int32, (QB, WIN), 0)
    kpos = start + lax.broadcasted_iota(jnp.int32, (QB, WIN), 1)
    mask = jnp.abs(kpos - qpos) <= QB
    return start, qs, ks, vs, gs, sinks, cq, sq, ck, sk, mask


def _attn_specs(s):
    return [
        _bs((QB, 768), lambda n: (n, 0)),
        _bs((s, 256), lambda n: (0, C_AK // 256)),
        _bs((s, 256), lambda n: (0, C_AV // 256)),
        _bs((QB, 256), lambda n: (n, C_AG // 256)),
        _bs((QB, 256), lambda n: (n, C_AG // 256 + 1)),
        _bs((QB, 256), lambda n: (n, C_AG // 256 + 2)),
        _bs((s, HD), lambda n: (0, 0)),
        _bs((s, HD), lambda n: (0, 0)),
        _bs((1, HD), lambda n: (0, 0)),
        _bs((1, HD), lambda n: (0, 0)),
        _bs((1, AH), lambda n: (0, 0)),
    ]


def _attn_fwd(proj, cos, sin, gq, gk, sink):
    s = proj.shape[0]

    def body(aq_ref, ak_ref, av_ref, ag0, ag1, ag2, cos_ref, sin_ref, gq_ref, gk_ref, sink_ref, o_ref, ot_ref):
        n = pl.program_id(0)
        _, qs, ks, vs, gs, sinks, cq, sq, ck, sk, mask = _attn_load(
            n, s, aq_ref, ak_ref, av_ref, (ag0, ag1, ag2), cos_ref, sin_ref, sink_ref)
        outs = _attn_tile(qs, ks, vs, gs, sinks, gq_ref[...], gk_ref[...], cq, sq, ck, sk, mask, _rot_matrix())
        for h in range(AH):
            o_ref[:, h * HD:(h + 1) * HD] = outs[h]
        ot_ref[...] = o_ref[...].T.astype(BF16)

    return pl.pallas_call(
        body, name="attn_fwd", grid=(s // QB,),
        in_specs=_attn_specs(s), out_specs=[_bs((QB, 768), lambda n: (n, 0)), _bs((768, QB), lambda n: (0, n))],
        out_shape=[jax.ShapeDtypeStruct((s, 768), F32), jax.ShapeDtypeStruct((768, s), BF16)],
        compiler_params=_params(("arbitrary",)),
    )(proj, proj, proj, proj, proj, proj, cos, sin, gq, gk, sink)


def _attn_bwd(proj, cos, sin, gq, gk, sink, dy):
    s = proj.shape[0]

    def body(aq_ref, ak_ref, av_ref, ag0, ag1, ag2, cos_ref, sin_ref, gq_ref, gk_ref, sink_ref, dy_ref,
             daq_ref, dak_ref, dav_ref, dag_ref, dgq_ref, dgk_ref, dsink_ref):
        n = pl.program_id(0)

        @pl.when(n == 0)
        def _():
            dak_ref[...] = jnp.zeros_like(dak_ref)
            dav_ref[...] = jnp.zeros_like(dav_ref)
            dgq_ref[...] = jnp.zeros_like(dgq_ref)
            dgk_ref[...] = jnp.zeros_like(dgk_ref)
            dsink_ref[...] = jnp.zeros_like(dsink_ref)

        start, qs, ks, vs, gs, sinks, cq, sq, ck, sk, mask = _attn_load(
            n, s, aq_ref, ak_ref, av_ref, (ag0, ag1, ag2), cos_ref, sin_ref, sink_ref)
        rot = _rot_matrix()

        def f(qs, ks, vs, gs, sinks, gq, gk):
            return _attn_tile(qs, ks, vs, gs, sinks, gq, gk, cq, sq, ck, sk, mask, rot)

        _, vjp = jax.vjp(f, qs, ks, vs, gs, sinks, gq_ref[...], gk_ref[...])
        dys = [dy_ref[:, h * HD:(h + 1) * HD] for h in range(AH)]
        dqs, dks, dvs, dgs, dsinks, dgq, dgk = vjp(dys)
        for h in range(AH):
            daq_ref[:, h * HD:(h + 1) * HD] = dqs[h]
            dag_ref[:, h * HD:(h + 1) * HD] = dgs[h]
            dsink_ref[0:1, h:h + 1] += dsinks[h]
        for h in range(AKV):
            dak_ref[pl.ds(start, WIN), h * HD:(h + 1) * HD] += dks[h]
            dav_ref[pl.ds(start, WIN), h * HD:(h + 1) * HD] += dvs[h]
        dgq_ref[...] += dgq
        dgk_ref[...] += dgk

    whole = lambda shape: _bs(shape, lambda n: (0, 0))
    return pl.pallas_call(
        body, name="attn_bwd", grid=(s // QB,),
        in_specs=_attn_specs(s) + [_bs((QB, 768), lambda n: (n, 0))],
        out_specs=[_bs((QB, 768), lambda n: (n, 0)), whole((s, 256)), whole((s, 256)), _bs((QB, 768), lambda n: (n, 0)),
                   whole((1, HD)), whole((1, HD)), whole((1, AH))],
        out_shape=[jax.ShapeDtypeStruct((s, 768), F32), jax.ShapeDtypeStruct((s, 256), F32),
                   jax.ShapeDtypeStruct((s, 256), F32), jax.ShapeDtypeStruct((s, 768), F32),
                   jax.ShapeDtypeStruct((1, HD), F32), jax.ShapeDtypeStruct((1, HD), F32),
                   jax.ShapeDtypeStruct((1, AH), F32)],
        compiler_params=_params(("arbitrary",)),
    )(proj, proj, proj, proj, proj, proj, cos, sin, gq, gk, sink, dy)


def _mem_kv(mem, g, w):
    def body(m_ref, g_ref, w_ref, o_ref, mn_ref):
        mn = _rms(m_ref[...], g_ref[...]).astype(BF16)
        mn_ref[...] = mn
        o_ref[...] = jnp.dot(mn, w_ref[...], preferred_element_type=F32)

    return pl.pallas_call(
        body, name="mem_kv",
        out_shape=[jax.ShapeDtypeStruct((NMEM, 2 * XW), F32), jax.ShapeDtypeStruct((NMEM, D), BF16)],
        compiler_params=_params(),
    )(mem, g, w)


def _xattn_tile(qs, gs, kms, vms, gxq, gxk):
    heads = range(XH)
    q = [_rms(qs[h], gxq) for h in heads]
    km = [_rms(kms[h], gxk) for h in heads]
    sc = [_mm_nt(q[h], km[h]) * (XD ** -0.5) for h in heads]
    p = [jnp.exp(sc[h] - lax.stop_gradient(jnp.max(sc[h], axis=-1, keepdims=True))) for h in heads]
    p = [p[h] / jnp.sum(p[h], axis=-1, keepdims=True) for h in heads]
    return [_mm_nn(p[h], vms[h]) * _silu(gs[h]) for h in heads]


XT = 256


def _xattn_specs():
    return [
        _bs((XT, 256), lambda i: (i, C_XQ // 256)), _bs((XT, 256), lambda i: (i, C_XQ // 256 + 1)),
        _bs((XT, 256), lambda i: (i, C_XG // 256)), _bs((XT, 256), lambda i: (i, C_XG // 256 + 1)),
        _bs((NMEM, 2 * XW), lambda i: (0, 0)),
        _bs((1, XD), lambda i: (0, 0)), _bs((1, XD), lambda i: (0, 0)),
    ]


def _xattn_load(q0, q1, g0, g1, mkv_ref):
    qs = [(q0, q1)[h // 2][:, (h % 2) * XD:(h % 2 + 1) * XD] for h in range(XH)]
    gs = [(g0, g1)[h // 2][:, (h % 2) * XD:(h % 2 + 1) * XD] for h in range(XH)]
    kms = [mkv_ref[:, h * XD:(h + 1) * XD] for h in range(XH)]
    vms = [mkv_ref[:, XW + h * XD:XW + (h + 1) * XD] for h in range(XH)]
    return qs, gs, kms, vms


def _xattn_fwd(proj, mkv, gxq, gxk):
    s = proj.shape[0]

    def body(q0, q1, g0, g1, mkv_ref, gxq_ref, gxk_ref, o_ref, ot_ref):
        qs, gs, kms, vms = _xattn_load(q0, q1, g0, g1, mkv_ref)
        outs = _xattn_tile(qs, gs, kms, vms, gxq_ref[...], gxk_ref[...])
        for h in range(XH):
            o_ref[:, h * XD:(h + 1) * XD] = outs[h]
        ot_ref[...] = o_ref[...].T.astype(BF16)

    return pl.pallas_call(
        body, name="xattn_fwd", grid=(s // XT,),
        in_specs=_xattn_specs(), out_specs=[_bs((XT, XW), lambda i: (i, 0)), _bs((XW, XT), lambda i: (0, i))],
        out_shape=[jax.ShapeDtypeStruct((s, XW), F32), jax.ShapeDtypeStruct((XW, s), BF16)],
        compiler_params=_params(("arbitrary",)),
    )(proj, proj, proj, proj, mkv, gxq, gxk)


def _xattn_bwd(proj, mkv, gxq, gxk, dy):
    s = proj.shape[0]

    def body(q0, q1, g0, g1, mkv_ref, gxq_ref, gxk_ref, dy_ref, dq_ref, dg_ref, dmkv_ref, dgxq_ref, dgxk_ref):
        @pl.when(pl.program_id(0) == 0)
        def _():
            dmkv_ref[...] = jnp.zeros_like(dmkv_ref)
            dgxq_ref[...] = jnp.zeros_like(dgxq_ref)
            dgxk_ref[...] = jnp.zeros_like(dgxk_ref)

        qs, gs, kms, vms = _xattn_load(q0, q1, g0, g1, mkv_ref)
        _, vjp = jax.vjp(_xattn_tile, qs, gs, kms, vms, gxq_ref[...], gxk_ref[...])
        dqs, dgs, dkms, dvms, dgxq, dgxk = vjp([dy_ref[:, h * XD:(h + 1) * XD] for h in range(XH)])
        for h in range(XH):
            dq_ref[:, h * XD:(h + 1) * XD] = dqs[h]
            dg_ref[:, h * XD:(h + 1) * XD] = dgs[h]
            dmkv_ref[:, h * XD:(h + 1) * XD] += dkms[h]
            dmkv_ref[:, XW + h * XD:XW + (h + 1) * XD] += dvms[h]
        dgxq_ref[...] += dgxq
        dgxk_ref[...] += dgxk

    whole = lambda shape: _bs(shape, lambda i: (0, 0))
    return pl.pallas_call(
        body, name="xattn_bwd", grid=(s // XT,),
        in_specs=_xattn_specs() + [_bs((XT, XW), lambda i: (i, 0))],
        out_specs=[_bs((XT, XW), lambda i: (i, 0)), _bs((XT, XW), lambda i: (i, 0)), whole((NMEM, 2 * XW)),
                   whole((1, XD)), whole((1, XD))],
        out_shape=[jax.ShapeDtypeStruct((s, XW), F32), jax.ShapeDtypeStruct((s, XW), F32),
                   jax.ShapeDtypeStruct((NMEM, 2 * XW), F32), jax.ShapeDtypeStruct((1, XD), F32),
                   jax.ShapeDtypeStruct((1, XD), F32)],
        compiler_params=_params(("arbitrary",)),
    )(proj, proj, proj, proj, mkv, gxq, gxk, dy)


def _mem_bwd(mem, dmn):
    def body(m_ref, dmn_ref, o_ref):
        m = m_ref[...]
        r = lax.rsqrt(jnp.mean(m * m, axis=-1, keepdims=True) + NORM_EPS)
        o_ref[...] = jnp.sum(dmn_ref[...] * m * r, axis=0, keepdims=True)

    return pl.pallas_call(body, name="mem_norm_bwd", out_shape=jax.ShapeDtypeStruct((1, D), F32),
                          compiler_params=_params())(mem, dmn)


SHIFT_W = 512


def _shift_rows(p, s):
    row = lax.broadcasted_iota(jnp.int32, p.shape, 0)
    prev = jnp.where(row == 0, 0.0, pltpu.roll(p, 1, 0))
    nxt = jnp.where(row == s - 1, 0.0, pltpu.roll(p, s - 1, 0))
    return prev, nxt


def _shift_fwd(proj, mu):
    s = proj.shape[0]

    def body(p_ref, mu_ref, o_ref):
        p = p_ref[...]
        prev, nxt = _shift_rows(p, s)
        o_ref[...] = p + mu_ref[...] * (0.5 * (prev + nxt) - p)

    return pl.pallas_call(
        body, name="shift_fwd", grid=(RSW // SHIFT_W,),
        in_specs=[_bs((s, SHIFT_W), lambda j: (0, C_RS // SHIFT_W + j)), _bs((1, SHIFT_W), lambda j: (0, j))],
        out_specs=_bs((s, SHIFT_W), lambda j: (0, j)),
        out_shape=jax.ShapeDtypeStruct((s, RSW), F32),
        compiler_params=_params(("parallel",)),
    )(proj, mu)


def _shift_bwd(proj, mu, dps):
    s = proj.shape[0]

    def body(p_ref, mu_ref, g_ref, o_ref, dmu_ref):
        p, g, mu_v = p_ref[...], g_ref[...], mu_ref[...]
        prev, nxt = _shift_rows(p, s)
        dmu_ref[...] = jnp.sum(g * (0.5 * (prev + nxt) - p), axis=0, keepdims=True)
        mg = mu_v * g
        down, up = _shift_rows(mg, s)
        o_ref[...] = g * (1.0 - mu_v) + 0.5 * (down + up)

    return pl.pallas_call(
        body, name="shift_bwd", grid=(RSW // SHIFT_W,),
        in_specs=[_bs((s, SHIFT_W), lambda j: (0, C_RS // SHIFT_W + j)), _bs((1, SHIFT_W), lambda j: (0, j)),
                  _bs((s, SHIFT_W), lambda j: (0, j))],
        out_specs=[_bs((s, SHIFT_W), lambda j: (0, j)), _bs((1, SHIFT_W), lambda j: (0, j))],
        out_shape=[jax.ShapeDtypeStruct((s, RSW), F32), jax.ShapeDtypeStruct((1, RSW), F32)],
        compiler_params=_params(("parallel",)),
    )(proj, mu, dps)


def _pre_tile(k, wf, wb, af, ab, k_k, k_a, w0s, w2s, a0s, a2s, seg):
    kx = k * k_k
    ss = _hdot(kx * kx, seg)
    kk = kx / jnp.maximum(jnp.sqrt(ss), 1e-12)
    outs = [kk]
    for d, (w_in, a_in) in enumerate(((wf, af), (wb, ab))):
        z = w0s[d] + _mm_nn(jnp.tanh(w_in), w2s[d])
        wd = -_softplus(-z) - 0.5
        dec = jnp.exp(-jnp.exp(wd))
        ad = jax.nn.sigmoid(a0s[d] + _mm_nn(a_in, a2s[d]))
        kd = k * (1.0 + (ad - 1.0) * k_a)
        outs += [dec, kd, kk * ad]
    return outs


PT = 256


def _pre_load(ps_ref, kk_ref, ka_ref, w0_ref, w2_ref, a0_ref, a2_ref):
    k = ps_ref[:, RW:2 * RW]
    wf, wb = ps_ref[:, 3 * RW:3 * RW + 64], ps_ref[:, 3 * RW + 64:3 * RW + 128]
    af, ab = ps_ref[:, 3 * RW + 128:3 * RW + 192], ps_ref[:, 3 * RW + 192:3 * RW + 256]
    w0s = [w0_ref[0:1, :], w0_ref[1:2, :]]
    a0s = [a0_ref[0:1, :], a0_ref[1:2, :]]
    w2s = [w2_ref[0], w2_ref[1]]
    a2s = [a2_ref[0], a2_ref[1]]
    return (k, wf, wb, af, ab, kk_ref[...], ka_ref[...], w0s, w2s, a0s, a2s)


def _pre_specs():
    c = lambda shape: _bs(shape, lambda i: tuple(0 for _ in shape))
    return [_bs((PT, RSW), lambda i: (i, 0)), c((1, RW)), c((1, RW)), c((2, RW)), c((2, 64, RW)), c((2, RW)),
            c((2, 64, RW))]


def _pre_fwd(ps, k_k, k_a, w0, w2, a0, a2):
    s = ps.shape[0]

    def body(ps_ref, kk_ref, ka_ref, w0_ref, w2_ref, a0_ref, a2_ref, *outs):
        args = _pre_load(ps_ref, kk_ref, ka_ref, w0_ref, w2_ref, a0_ref, a2_ref)
        res = _pre_tile(*args, _seg_matrix(RW, HD))
        for o_ref, v in zip(outs, res):
            o_ref[...] = v

    return pl.pallas_call(
        body, name="rwkv_pre_fwd", grid=(s // PT,),
        in_specs=_pre_specs(), out_specs=[_bs((PT, RW), lambda i: (i, 0))] * 7,
        out_shape=[jax.ShapeDtypeStruct((s, RW), F32)] * 7,
        compiler_params=_params(("parallel",)),
    )(ps, k_k, k_a, w0, w2, a0, a2)


def _pre_bwd(ps, k_k, k_a, w0, w2, a0, a2, dr, dv, cts):
    s = ps.shape[0]

    def body(ps_ref, kk_ref, ka_ref, w0_ref, w2_ref, a0_ref, a2_ref, dr_ref, dv_ref, c0, c1, c2, c3, c4, c5, c6,
             dps_ref, dkk_ref, dka_ref, dw0_ref, dw2_ref, da0_ref, da2_ref):
        @pl.when(pl.program_id(0) == 0)
        def _():
            for r in (dkk_ref, dka_ref, dw0_ref, dw2_ref, da0_ref, da2_ref):
                r[...] = jnp.zeros_like(r)

        args = _pre_load(ps_ref, kk_ref, ka_ref, w0_ref, w2_ref, a0_ref, a2_ref)
        seg = _seg_matrix(RW, HD)
        _, vjp = jax.vjp(lambda *a: _pre_tile(*a, seg), *args)
        dk, dwf, dwb, daf, dab, dk_k, dk_a, dw0s, dw2s, da0s, da2s = vjp([c[...] for c in (c0, c1, c2, c3, c4, c5, c6)])
        dps_ref[:, 0:RW] = dr_ref[...]
        dps_ref[:, RW:2 * RW] = dk
        dps_ref[:, 2 * RW:3 * RW] = dv_ref[...]
        for j, t in enumerate((dwf, dwb, daf, dab)):
            dps_ref[:, 3 * RW + 64 * j:3 * RW + 64 * (j + 1)] = t
        dkk_ref[...] += dk_k
        dka_ref[...] += dk_a
        for d in range(2):
            dw0_ref[d:d + 1, :] += dw0s[d]
            da0_ref[d:d + 1, :] += da0s[d]
            dw2_ref[d] += dw2s[d]
            da2_ref[d] += da2s[d]

    c = lambda shape: _bs(shape, lambda i: tuple(0 for _ in shape))
    row = _bs((PT, RW), lambda i: (i, 0))
    return pl.pallas_call(
        body, name="rwkv_pre_bwd", grid=(s // PT,),
        in_specs=_pre_specs() + [row] * 9,
        out_specs=[_bs((PT, RSW), lambda i: (i, 0)), c((1, RW)), c((1, RW)), c((2, RW)), c((2, 64, RW)), c((2, RW)),
                   c((2, 64, RW))],
        out_shape=[jax.ShapeDtypeStruct((s, RSW), F32), jax.ShapeDtypeStruct((1, RW), F32),
                   jax.ShapeDtypeStruct((1, RW), F32), jax.ShapeDtypeStruct((2, RW), F32),
                   jax.ShapeDtypeStruct((2, 64, RW), F32), jax.ShapeDtypeStruct((2, RW), F32),
                   jax.ShapeDtypeStruct((2, 64, RW), F32)],
        compiler_params=_params(("arbitrary",)),
    )(ps, k_k, k_a, w0, w2, a0, a2, dr, dv, *cts)


def _post_tile(y0, y1, r, v, kd0, kd1, rg, r_k, ln_w, ln_b, seg):
    ysum = y0 + y1
    bonus = (_hdot(r * kd0 * r_k, seg) + _hdot(r * kd1 * r_k, seg)) * v
    mean = _hdot(ysum, seg) * (1.0 / HD)
    cen = ysum - mean
    var = _hdot(cen * cen, seg) * (1.0 / HD)
    y = cen * lax.rsqrt(var + GN_EPS) * ln_w + ln_b + bonus
    return y * _silu(rg)


def _post_specs():
    row = _bs((PT, RW), lambda i: (i, 0))
    c = _bs((1, RW), lambda i: (0, 0))
    return [row, row, _bs((PT, RW), lambda i: (i, 0)), _bs((PT, RW), lambda i: (i, 2)), row, row,
            _bs((PT, RW), lambda i: (i, C_RG // RW)), c, c, c]


def _post_fwd(y0, y1, ps, kd0, kd1, proj, r_k, ln_w, ln_b):
    s = ps.shape[0]

    def body(y0_ref, y1_ref, r_ref, v_ref, kd0_ref, kd1_ref, rg_ref, rk_ref, lw_ref, lb_ref, o_ref, ot_ref):
        y = _post_tile(y0_ref[...], y1_ref[...], r_ref[...], v_ref[...], kd0_ref[...], kd1_ref[...],
                       rg_ref[...], rk_ref[...], lw_ref[...], lb_ref[...], _seg_matrix(RW, HD))
        o_ref[...] = y
        ot_ref[...] = y.T.astype(BF16)

    return pl.pallas_call(
        body, name="rwkv_post_fwd", grid=(s // PT,),
        in_specs=_post_specs(), out_specs=[_bs((PT, RW), lambda i: (i, 0)), _bs((RW, PT), lambda i: (0, i))],
        out_shape=[jax.ShapeDtypeStruct((s, RW), F32), jax.ShapeDtypeStruct((RW, s), BF16)],
        compiler_params=_params(("parallel",)),
    )(y0, y1, ps, ps, kd0, kd1, proj, r_k, ln_w, ln_b)


def _post_bwd(y0, y1, ps, kd0, kd1, proj, r_k, ln_w, ln_b, dy):
    s = ps.shape[0]

    def body(y0_ref, y1_ref, r_ref, v_ref, kd0_ref, kd1_ref, rg_ref, rk_ref, lw_ref, lb_ref, dy_ref,
             dys_ref, dr_ref, dv_ref, dkd0_ref, dkd1_ref, drg_ref, drk_ref, dlw_ref, dlb_ref):
        @pl.when(pl.program_id(0) == 0)
        def _():
            for r in (drk_ref, dlw_ref, dlb_ref):
                r[...] = jnp.zeros_like(r)

        seg = _seg_matrix(RW, HD)
        args = [t[...] for t in (y0_ref, y1_ref, r_ref, v_ref, kd0_ref, kd1_ref, rg_ref, rk_ref, lw_ref, lb_ref)]
        _, vjp = jax.vjp(lambda *a: _post_tile(*a, seg), *args)
        dy0, _, dr, dv, dkd0, dkd1, drg, drk, dlw, dlb = vjp(dy_ref[...])
        dys_ref[...] = dy0
        dr_ref[...] = dr
        dv_ref[...] = dv
        dkd0_ref[...] = dkd0
        dkd1_ref[...] = dkd1
        drg_ref[...] = drg
        drk_ref[...] += drk
        dlw_ref[...] += dlw
        dlb_ref[...] += dlb

    row = _bs((PT, RW), lambda i: (i, 0))
    c = _bs((1, RW), lambda i: (0, 0))
    return pl.pallas_call(
        body, name="rwkv_post_bwd", grid=(s // PT,),
        in_specs=_post_specs() + [row], out_specs=[row] * 6 + [c] * 3,
        out_shape=[jax.ShapeDtypeStruct((s, RW), F32)] * 6 + [jax.ShapeDtypeStruct((1, RW), F32)] * 3,
        compiler_params=_params(("arbitrary",)),
    )(y0, y1, ps, ps, kd0, kd1, proj, r_k, ln_w, ln_b, dy)


def _ones1():
    r = lax.broadcasted_iota(jnp.int32, (128, 128), 0) // HD
    c = lax.broadcasted_iota(jnp.int32, (128, 128), 1) // HD
    return (r == c).astype(BF16)


def _scan_specs(direction, nc, fwd_order):
    def tb(c):
        sc = c if fwd_order else nc - 1 - c
        return sc if direction == 0 else nc - 1 - sc

    row = _bs((TC, RW), lambda c: (tb(c), 0))
    rowv = _bs((TC, RW), lambda c: (tb(c), 2))
    return row, rowv


def _tiles(res, k):
    n = NPAIR * HD
    return [res[k * n + p * HD:k * n + (p + 1) * HD] for p in range(NPAIR)]


def _rows_to_tiles(src_ref, rows8, stage, out_s, base):
    for p in range(NPAIR):
        stage[base + p, 0:8, 0:HD] = src_ref[rows8, p * 128:p * 128 + HD]
        stage[base + p, HD:HD + 8, 0:HD] = src_ref[rows8, p * 128 + HD:(p + 1) * 128]
        out_s[base + p] = stage[base + p].T[0:HD].astype(BF16)


def _tiles_to_rows(tile_s, base, dst_ref, rows8):
    for p in range(NPAIR):
        t = jnp.concatenate([tile_s[base + p], jnp.zeros((HD, 128), F32)], axis=0).T
        dst_ref[rows8, p * 128:p * 128 + HD] = t[0:8, 0:HD]
        dst_ref[rows8, p * 128 + HD:(p + 1) * 128] = t[HD:HD + 8, 0:HD]


def _put_cols(tile_s, base, u, tiles):
    for p in range(NPAIR):
        tile_s[base + p, :, u:u + 1] = tiles[p][:, u:u + 1]
        tile_s[base + p, :, HD + u:HD + u + 1] = tiles[p][:, HD + u:HD + u + 1]


def _scan2_fwd(per_dir, ps, kk, gather=()):
    s = ps.shape[0]
    nc, ng = s // TC, TC // 8
    ngat = len(gather)
    in_specs, operands, out_specs, out_shape = [], [], [], []
    for d in (0, 1):
        row, rowv = _scan_specs(d, nc, True)
        in_specs += [row] * 5 + [rowv]
        operands += list(per_dir[d]) + [ps, kk, ps]
        out_specs += [row, _bs((1, NPAIR, HD, 128), lambda c: (c, 0, 0, 0))]
        out_shape += [jax.ShapeDtypeStruct((s, RW), F32), jax.ShapeDtypeStruct((nc, NPAIR, HD, 128), F32)]
    in_specs += [ANY] * ngat
    operands += list(gather)
    out_specs += [ANY] * ngat
    out_shape += _gather_out_shapes(gather)

    def body(*refs):
        ins = [refs[0:6], refs[6:12]]
        base = 12 + ngat
        y_refs, ck_refs = (refs[base], refs[base + 2]), (refs[base + 1], refs[base + 3])
        st, vt_s, yt_s, stage = refs[base + 4 + ngat:base + 8 + ngat]
        if ngat:
            g_start, g_forward, g_finish = _gather_phases(
                gather, refs[12:base], refs[base + 4:base + 4 + ngat], refs[base + 8 + ngat:])

        @pl.when(pl.program_id(0) == 0)
        def _():
            st[...] = jnp.zeros_like(st)
            yt_s[...] = jnp.zeros_like(yt_s)
            stage[...] = jnp.zeros_like(stage)
            if ngat:
                g_start()

        if ngat:
            @pl.when(pl.program_id(0) == nc // 2)
            def _():
                g_forward()

        for d in (0, 1):
            ck_refs[d][0] = st[d * NPAIR:(d + 1) * NPAIR]
        ones1 = _ones1()
        lane_u = lax.broadcasted_iota(jnp.int32, (HD, 128), 1) % HD
        pc = [slice(p * 128, (p + 1) * 128) for p in range(NPAIR)]

        def group(gi, carry):
            gs = (gi, ng - 1 - gi)
            rows8 = [pl.ds(pl.multiple_of(gs[d] * 8, 8), 8) for d in (0, 1)]
            blk = [[q[rows8[d], :] for q in ins[d][:5]] for d in (0, 1)]
            for d in (0, 1):
                _rows_to_tiles(ins[d][5], rows8[d], stage, vt_s, d * NPAIR)
            ss = [[st[d * NPAIR + p] for p in range(NPAIR)] for d in (0, 1)]
            for ui in range(9):
                us, ups = (ui, 7 - ui), (ui - 1, 8 - ui)
                lhs1, where = [], {}
                for d in (0, 1):
                    if ui < 8:
                        where["sa", d] = len(lhs1) // NPAIR
                        lhs1 += [(ss[d][p] * blk[d][4][us[d]:us[d] + 1, pc[p]]).astype(BF16) for p in range(NPAIR)]
                        where["vb", d] = len(lhs1) // NPAIR
                        for p in range(NPAIR):
                            vt = vt_s[d * NPAIR + p]
                            lhs1.append(jnp.where(lane_u == us[d], vt, jnp.zeros_like(vt)))
                    if ui > 0:
                        where["y", d] = len(lhs1) // NPAIR
                        lhs1 += [(ss[d][p] * blk[d][3][ups[d]:ups[d] + 1, pc[p]]).astype(BF16) for p in range(NPAIR)]
                res1 = jnp.dot(jnp.concatenate(lhs1, axis=0), ones1, preferred_element_type=F32)
                for d in (0, 1):
                    d8, k8, b8, _, _ = blk[d]
                    u = us[d]
                    if ui < 8:
                        sa, vb = _tiles(res1, where["sa", d]), _tiles(res1, where["vb", d])
                        for p in range(NPAIR):
                            ss[d][p] = (ss[d][p] * d8[u:u + 1, pc[p]] - sa[p] * b8[u:u + 1, pc[p]]
                                        + vb[p] * k8[u:u + 1, pc[p]])
                    if ui > 0:
                        _put_cols(yt_s, d * NPAIR, ups[d], _tiles(res1, where["y", d]))
            for d in (0, 1):
                _tiles_to_rows(yt_s, d * NPAIR, y_refs[d], rows8[d])
                for p in range(NPAIR):
                    st[d * NPAIR + p] = ss[d][p]
            return carry

        for gi in range(ng):
            group(gi, 0)

        if ngat:
            @pl.when(pl.program_id(0) == nc - 1)
            def _():
                g_finish()

    outs = pl.pallas_call(
        body, name="rwkv_scan_fwd", grid=(nc,), in_specs=in_specs, out_specs=out_specs, out_shape=out_shape,
        scratch_shapes=[pltpu.VMEM((2 * NPAIR, HD, 128), F32), pltpu.VMEM((2 * NPAIR, HD, 128), BF16),
                        pltpu.VMEM((2 * NPAIR, HD, 128), F32), pltpu.VMEM((2 * NPAIR, 128, 128), F32)]
        + (_gather_sems(ngat) if ngat else []),
        compiler_params=pltpu.CompilerParams(dimension_semantics=("arbitrary",), vmem_limit_bytes=VMEM_LIMIT,
                                             has_side_effects=bool(ngat)),
    )(*operands)
    return [(outs[0], outs[1]), (outs[2], outs[3])], list(outs[4:])


def _scan2_bwd(per_dir, ps, kk, dy, scatter=()):
    s = ps.shape[0]
    nc, ng = s // TC, TC // 8
    nsc = len(scatter)
    in_specs, operands, out_specs, out_shape = [], [], [], []
    for d in (0, 1):
        row, rowv = _scan_specs(d, nc, False)
        dec, kd, b, ck = per_dir[d]
        in_specs += [row] * 5 + [rowv, row, _bs((1, NPAIR, HD, 128), lambda c: (nc - 1 - c, 0, 0, 0))]
        operands += [dec, kd, b, ps, kk, ps, dy, ck]
        out_specs += [row] * 6
        out_shape += [jax.ShapeDtypeStruct((s, RW), F32)] * 6
    in_specs += [ANY] * nsc
    operands += list(scatter)
    out_specs += [ANY] * nsc
    out_shape += _scatter_out_shapes(scatter)

    def body(*refs):
        ins = [refs[0:8], refs[8:16]]
        base = 16 + nsc
        outs = [refs[base:base + 6], refs[base + 6:base + 12]]
        st, sa_s, vb_s, dy_s, ds, vt_s, dyt_s, dvt_s, stage = refs[base + 12 + nsc:base + 21 + nsc]
        if nsc:
            s_start, s_finish = _scatter_phases(refs[16:base], refs[base + 12:base + 12 + nsc], refs[base + 21 + nsc:])

        @pl.when(pl.program_id(0) == 0)
        def _():
            dvt_s[...] = jnp.zeros_like(dvt_s)
            stage[...] = jnp.zeros_like(stage)
            ds[...] = jnp.zeros_like(ds)
            if nsc:
                s_start()

        for d in (0, 1):
            st[d * (TC + 1)] = ins[d][7][0]
        ones1 = _ones1()
        lane_u = lax.broadcasted_iota(jnp.int32, (HD, 128), 1) % HD
        row_id = lax.broadcasted_iota(jnp.int32, (8, 128), 0)
        pc = [slice(p * 128, (p + 1) * 128) for p in range(NPAIR)]

        def load_rows(gs):
            return [[q[pl.ds(pl.multiple_of(gs[d] * 8, 8), 8), :] for q in ins[d][:5]] for d in (0, 1)]

        def fgroup(gi, carry):
            gs = (gi, ng - 1 - gi)
            blk = load_rows(gs)
            for d in (0, 1):
                rows8 = pl.ds(pl.multiple_of(gs[d] * 8, 8), 8)
                _rows_to_tiles(ins[d][5], rows8, stage, vt_s, d * NPAIR)
                _rows_to_tiles(ins[d][6], rows8, stage, dyt_s, d * NPAIR)
            ss = [[st[d * (TC + 1) + gi * 8, p] for p in range(NPAIR)] for d in (0, 1)]
            for ui in range(8):
                us = (ui, 7 - ui)
                i = gi * 8 + ui
                lhs1 = []
                for d in (0, 1):
                    kk8 = blk[d][4]
                    lhs1 += [(ss[d][p] * kk8[us[d]:us[d] + 1, pc[p]]).astype(BF16) for p in range(NPAIR)]
                    for tile_s in (vt_s, dyt_s):
                        for p in range(NPAIR):
                            t = tile_s[d * NPAIR + p]
                            lhs1.append(jnp.where(lane_u == us[d], t, jnp.zeros_like(t)))
                res1 = jnp.dot(jnp.concatenate(lhs1, axis=0), ones1, preferred_element_type=F32)
                for d in (0, 1):
                    d8, k8, b8, _, _ = blk[d]
                    u = us[d]
                    sa, vb, dyb = _tiles(res1, 3 * d), _tiles(res1, 3 * d + 1), _tiles(res1, 3 * d + 2)
                    for p in range(NPAIR):
                        sa_s[d * TC + i, p] = sa[p]
                        vb_s[d * TC + i, p] = vb[p]
                        dy_s[d * TC + i, p] = dyb[p]
                        ss[d][p] = ss[d][p] * d8[u:u + 1, pc[p]] - sa[p] * b8[u:u + 1, pc[p]] + vb[p] * k8[u:u + 1, pc[p]]
                        st[d * (TC + 1) + i + 1, p] = ss[d][p]
            return carry

        for gi in range(ng):
            fgroup(gi, 0)

        def bgroup(gj, carry):
            gi = ng - 1 - gj
            gs = (gi, ng - 1 - gi)
            blk = load_rows(gs)
            dss = [[ds[d * NPAIR + p] for p in range(NPAIR)] for d in (0, 1)]
            acc = [[[jnp.zeros((8, 128), F32) for _ in range(5)] for _ in range(NPAIR)] for _ in (0, 1)]
            for uj in range(8):
                ui = 7 - uj
                us = (ui, 7 - ui)
                i = gi * 8 + ui
                lhs1, dyb = [], [None, None]
                for d in (0, 1):
                    _, k8, b8, r8, _ = blk[d]
                    u = us[d]
                    dyb[d] = [dy_s[d * TC + i, p] for p in range(NPAIR)]
                    for p in range(NPAIR):
                        dss[d][p] = dss[d][p] + dyb[d][p] * r8[u:u + 1, pc[p]]
                    lhs1 += [(dss[d][p] * b8[u:u + 1, pc[p]]).astype(BF16) for p in range(NPAIR)]
                    lhs1 += [(dss[d][p] * k8[u:u + 1, pc[p]]).astype(BF16) for p in range(NPAIR)]
                res1 = jnp.dot(jnp.concatenate(lhs1, axis=0), ones1, preferred_element_type=F32)
                for d in (0, 1):
                    d8, _, _, _, kk8 = blk[d]
                    u = us[d]
                    dsa, dvb = _tiles(res1, 2 * d), _tiles(res1, 2 * d + 1)
                    _put_cols(dvt_s, d * NPAIR, u, dvb)
                    for p in range(NPAIR):
                        sp, sn = st[d * (TC + 1) + i, p], st[d * (TC + 1) + i + 1, p]
                        dsv = dss[d][p]
                        vals = (jnp.sum(sn * dyb[d][p], axis=0, keepdims=True), jnp.sum(dsv * sp, axis=0, keepdims=True),
                                -jnp.sum(dsv * sa_s[d * TC + i, p], axis=0, keepdims=True),
                                jnp.sum(dsv * vb_s[d * TC + i, p], axis=0, keepdims=True),
                                -jnp.sum(sp * dsa[p], axis=0, keepdims=True))
                        acc[d][p] = [jnp.where(row_id == u, o, a_) for o, a_ in zip(vals, acc[d][p])]
                        dss[d][p] = dsv * d8[u:u + 1, pc[p]] - dsa[p] * kk8[u:u + 1, pc[p]]
            for d in (0, 1):
                rows8 = pl.ds(pl.multiple_of(gs[d] * 8, 8), 8)
                _tiles_to_rows(dvt_s, d * NPAIR, outs[d][5], rows8)
                for p in range(NPAIR):
                    ds[d * NPAIR + p] = dss[d][p]
                    for o_ref, a_ in zip(outs[d][:5], acc[d][p]):
                        o_ref[rows8, pc[p]] = a_
            return carry

        for gj in range(ng):
            bgroup(gj, 0)

        if nsc:
            @pl.when(pl.program_id(0) == nc - 1)
            def _():
                s_finish()

    chunk = lambda k: pltpu.VMEM((k, NPAIR, HD, 128), F32)
    pairs = lambda w, dt: pltpu.VMEM((2 * NPAIR, HD, w), dt)
    res = pl.pallas_call(
        body, name="rwkv_scan_bwd", grid=(nc,), in_specs=in_specs, out_specs=out_specs, out_shape=out_shape,
        scratch_shapes=[chunk(2 * (TC + 1)), chunk(2 * TC), chunk(2 * TC), chunk(2 * TC), pairs(128, F32),
                        pairs(128, BF16), pairs(128, BF16), pairs(128, F32), pltpu.VMEM((2 * NPAIR, 128, 128), F32)]
        + _scatter_sems(nsc),
        compiler_params=pltpu.CompilerParams(dimension_semantics=("arbitrary",), vmem_limit_bytes=VMEM_LIMIT,
                                             has_side_effects=bool(nsc)),
    )(*operands)
    return [res[0:6], res[6:12]], list(res[12:])


MT = 512
MN = 256


def _merge_fwd(ya, yr, yx, wa, wr, wx, proj, gate_b):
    s = ya.shape[0]

    def body(ya_ref, yr_ref, yx_ref, wa_ref, wr_ref, wx_ref, m0, m1, m2, b0, b1, b2, o_ref, ot_ref):
        acc = jnp.zeros((MT, MN), F32)
        for y_ref, w_ref, m_ref, b_ref in ((ya_ref, wa_ref, m0, b0), (yr_ref, wr_ref, m1, b1), (yx_ref, wx_ref, m2, b2)):
            u = _dot(y_ref[...], w_ref[...], ((1,), (0,)))
            acc = acc + jax.nn.sigmoid(m_ref[...] + b_ref[...]) * u
        o_ref[...] = acc.astype(BF16)
        ot_ref[...] = acc.T.astype(BF16)

    mg = lambda br: _bs((MT, MN), lambda i, j: (i, C_MG // MN + br * (D // MN) + j))
    gb = lambda br: _bs((1, MN), lambda i, j: (0, br * (D // MN) + j))
    return pl.pallas_call(
        body, name="merge_fwd", grid=(s // MT, D // MN),
        in_specs=[_bs((MT, RW), lambda i, j: (i, 0)), _bs((MT, RW), lambda i, j: (i, 0)), _bs((MT, XW), lambda i, j: (i, 0)),
                  _bs((RW, MN), lambda i, j: (0, j)), _bs((RW, MN), lambda i, j: (0, j)), _bs((XW, MN), lambda i, j: (0, j)),
                  mg(0), mg(1), mg(2), gb(0), gb(1), gb(2)],
        out_specs=[_bs((MT, MN), lambda i, j: (i, j)), _bs((MN, MT), lambda i, j: (j, i))],
        out_shape=[jax.ShapeDtypeStruct((s, D), BF16), jax.ShapeDtypeStruct((D, s), BF16)],
        compiler_params=_params(("parallel", "arbitrary")),
    )(ya, yr, yx, wa, wr, wx, proj, proj, proj, gate_b, gate_b, gate_b)


def _out_fwd(merged, w_out, x, target):
    s = x.shape[0]
    tm, tn = min(1024, s), 512

    def body(m_ref, w_ref, x_ref, t_ref, loss_ref, d_ref, d16_ref):
        @pl.when((pl.program_id(0) == 0) & (pl.program_id(1) == 0))
        def _():
            loss_ref[...] = jnp.zeros_like(loss_ref)

        out = x_ref[...] + jnp.dot(m_ref[...], w_ref[...], preferred_element_type=F32)
        err = out - t_ref[...]
        dout = err * (1.0 / D)
        d_ref[...] = dout
        d16_ref[...] = dout.astype(BF16)
        loss_ref[...] += jnp.sum(err * err)

    tile = _bs((tm, tn), lambda i, j: (i, j))
    return pl.pallas_call(
        body, name="out_fwd", grid=(s // tm, D // tn),
        in_specs=[_bs((tm, D), lambda i, j: (i, 0)), _bs((D, tn), lambda i, j: (0, j)), tile, tile],
        out_specs=[_bs((8, 128), lambda i, j: (0, 0)), tile, tile],
        out_shape=[jax.ShapeDtypeStruct((8, 128), F32), jax.ShapeDtypeStruct((s, D), F32),
                   jax.ShapeDtypeStruct((s, D), BF16)],
        compiler_params=_params(("arbitrary", "arbitrary")),
    )(merged, w_out, x, target)


def _merge_bwd(ya, yr, yx, wa, wr, wx, proj, gate_b, dmerged):
    s = ya.shape[0]

    def body(ya_ref, yr_ref, yx_ref, wa_ref, wr_ref, wx_ref, m0, m1, m2, b0, b1, b2, dm_ref,
             dg0, dg1, dg2, du0, du1, du2, dya_ref, dyr_ref, dyx_ref):
        @pl.when(pl.program_id(1) == 0)
        def _():
            dya_ref[...] = jnp.zeros_like(dya_ref)
            dyr_ref[...] = jnp.zeros_like(dyr_ref)
            dyx_ref[...] = jnp.zeros_like(dyx_ref)

        dm = dm_ref[...]
        branches = ((ya_ref, wa_ref, m0, b0, dg0, du0, dya_ref), (yr_ref, wr_ref, m1, b1, dg1, du1, dyr_ref),
                    (yx_ref, wx_ref, m2, b2, dg2, du2, dyx_ref))
        ws = [br[1][...] for br in branches]
        us = [_dot(br[0][...], w, ((1,), (0,))) for br, w in zip(branches, ws)]
        gts = [jax.nn.sigmoid(br[2][...] + br[3][...]) for br in branches]
        dus = [(dm * gt).astype(BF16) for gt in gts]
        for br, w, u, gt, du in zip(branches, ws, us, gts, dus):
            br[4][...] = (dm * u * gt * (1.0 - gt)).astype(BF16)
            br[5][...] = du
            br[6][...] += _dot(du, w, ((1,), (1,)))

    mg = lambda br: _bs((MT, MN), lambda i, j: (i, C_MG // MN + br * (D // MN) + j))
    gb = lambda br: _bs((1, MN), lambda i, j: (0, br * (D // MN) + j))
    tile = _bs((MT, MN), lambda i, j: (i, j))
    return pl.pallas_call(
        body, name="merge_bwd", grid=(s // MT, D // MN),
        in_specs=[_bs((MT, RW), lambda i, j: (i, 0)), _bs((MT, RW), lambda i, j: (i, 0)), _bs((MT, XW), lambda i, j: (i, 0)),
                  _bs((RW, MN), lambda i, j: (0, j)), _bs((RW, MN), lambda i, j: (0, j)), _bs((XW, MN), lambda i, j: (0, j)),
                  mg(0), mg(1), mg(2), gb(0), gb(1), gb(2), tile],
        out_specs=[tile] * 6 + [_bs((MT, RW), lambda i, j: (i, 0)), _bs((MT, RW), lambda i, j: (i, 0)),
                                _bs((MT, XW), lambda i, j: (i, 0))],
        out_shape=[jax.ShapeDtypeStruct((s, D), BF16)] * 6 + [jax.ShapeDtypeStruct((s, RW), F32),
                                                               jax.ShapeDtypeStruct((s, RW), F32),
                                                               jax.ShapeDtypeStruct((s, XW), F32)],
        compiler_params=_params(("parallel", "arbitrary")),
    )(ya, yr, yx, wa, wr, wx, proj, proj, proj, gate_b, gate_b, gate_b, dmerged)


def _colsum(a, name):
    m, n = a.shape
    tm, tn = min(2048, m), 512

    def body(a_ref, o_ref):
        @pl.when(pl.program_id(1) == 0)
        def _():
            o_ref[...] = jnp.zeros_like(o_ref)

        o_ref[...] += jnp.sum(a_ref[...].astype(F32), axis=0, keepdims=True)

    return pl.pallas_call(
        body, name=name, grid=(n // tn, m // tm),
        in_specs=[_bs((tm, tn), lambda j, i: (i, j))], out_specs=_bs((1, tn), lambda j, i: (0, j)),
        out_shape=jax.ShapeDtypeStruct((1, n), F32),
        compiler_params=_params(("parallel", "arbitrary")),
    )(a)


def _in_bwd(dproj, w_in, x, g, dout, stacks=()):
    s = x.shape[0]
    tm, tk = min(512, s), 896
    nk = NIN // tk
    ni = s // tm
    n = len(stacks)

    def body(dp_ref, w_ref, x_ref, g_ref, do_ref, *rest):
        ins, (gx_ref, gg_ref), outs = rest[:n], rest[n:n + 2], rest[n + 2:2 * n + 2]
        acc = rest[2 * n + 2]
        i, kk = pl.program_id(0), pl.program_id(1)

        if n:
            start, finish = _scatter_phases(ins, outs, rest[2 * n + 3:])

        @pl.when((i == 0) & (kk == 0))
        def _():
            gg_ref[...] = jnp.zeros_like(gg_ref)
            if n:
                start()

        @pl.when(kk == 0)
        def _():
            acc[...] = jnp.zeros_like(acc)

        acc[...] += _dot(dp_ref[...], w_ref[...], ((1,), (1,)))

        @pl.when(kk == nk - 1)
        def _():
            xv, dh, gv = x_ref[...], acc[...], g_ref[...]
            r = lax.rsqrt(jnp.mean(xv * xv, axis=-1, keepdims=True) + NORM_EPS)
            xn = xv * r
            gg_ref[...] += jnp.sum(dh * xn, axis=0, keepdims=True)
            dxn = dh * gv
            dx = r * (dxn - xn * jnp.mean(dxn * xn, axis=-1, keepdims=True))
            gx_ref[...] = do_ref[...] + dx

        if n:
            @pl.when((i == ni - 1) & (kk == nk - 1))
            def _():
                finish()

    any_spec = pl.BlockSpec(memory_space=pl.ANY)
    res = pl.pallas_call(
        body, name="in_bwd", grid=(ni, nk),
        in_specs=[_bs((tm, tk), lambda i, kk: (i, kk)), _bs((D, tk), lambda i, kk: (0, kk)),
                  _bs((tm, D), lambda i, kk: (i, 0)), _bs((1, D), lambda i, kk: (0, 0)),
                  _bs((tm, D), lambda i, kk: (i, 0))] + [any_spec] * n,
        out_specs=[_bs((tm, D), lambda i, kk: (i, 0)), _bs((1, D), lambda i, kk: (0, 0))] + [any_spec] * n,
        out_shape=[jax.ShapeDtypeStruct((s, D), F32), jax.ShapeDtypeStruct((1, D), F32)] + _scatter_out_shapes(stacks),
        scratch_shapes=[pltpu.VMEM((tm, D), F32)] + _scatter_sems(n),
        compiler_params=pltpu.CompilerParams(dimension_semantics=("arbitrary", "arbitrary"),
                                             vmem_limit_bytes=VMEM_LIMIT, has_side_effects=bool(n)),
    )(dproj, w_in, x, g, dout, *stacks)
    return res[0], res[1], list(res[2:])


def _adamw_math(w, g, m, v):
    m = ADAM_B1 * m + (1.0 - ADAM_B1) * g
    v = ADAM_B2 * v + (1.0 - ADAM_B2) * jnp.square(g)
    m_hat = m / (1.0 - ADAM_B1 ** ADAM_STEP)
    v_hat = v / (1.0 - ADAM_B2 ** ADAM_STEP)
    delta = -ADAM_LR * (m_hat / (jnp.sqrt(v_hat) + ADAM_EPS) + ADAM_WD * w)
    return delta, m, v


def _adamw(parts, w, m, v, name):
    rows, cols = w.shape
    tr = rows
    for cand in (256, 128, 64, 32, 16, 8):
        if rows % cand == 0 and cand * cols * 4 <= (1 << 20):
            tr = cand
            break
    n = len(parts)

    def body(*refs):
        g = refs[0][...].astype(F32)
        for r in refs[1:n]:
            g = g + r[...].astype(F32)
        w_ref, m_ref, v_ref, g_out, d_out, m_out, v_out = refs[n:]
        delta, m_new, v_new = _adamw_math(w_ref[...], g, m_ref[...], v_ref[...])
        g_out[...] = g
        d_out[...] = delta
        m_out[...] = m_new
        v_out[...] = v_new

    spec = _bs((tr, cols), lambda i: (i, 0))
    return pl.pallas_call(
        body, name=name, grid=(rows // tr,),
        in_specs=[spec] * (n + 3), out_specs=[spec] * 4,
        out_shape=[jax.ShapeDtypeStruct((rows, cols), F32)] * 4,
        compiler_params=_params(("parallel",)),
    )(*parts, w, m, v)


def _adamw_halves(mine, theirs, core, w, m, v, name):
    rows, cols = w.shape
    h = rows // 2
    tr = next(t for t in (256, 128, 64, 32, 16, 8) if h % t == 0 and t * cols * 4 <= (1 << 20))
    nt = h // tr

    def body(core_ref, mine_ref, theirs_ref, w_ref, m_ref, v_ref, g_out, d_out, m_out, v_out):
        is_mine = pl.program_id(0) // nt == core_ref[0]
        g = jnp.where(is_mine, mine_ref[...], theirs_ref[...])
        delta, m_new, v_new = _adamw_math(w_ref[...], g, m_ref[...], v_ref[...])
        g_out[...] = g
        d_out[...] = delta
        m_out[...] = m_new
        v_out[...] = v_new

    spec = _bs((tr, cols), lambda i, core_ref: (i, 0))
    return pl.pallas_call(
        body, name=name,
        grid_spec=pltpu.PrefetchScalarGridSpec(
            num_scalar_prefetch=1, grid=(2 * nt,),
            in_specs=[_bs((tr, cols), lambda i, core_ref: (jnp.clip(i - core_ref[0] * nt, 0, nt - 1), 0)),
                      _bs((tr, cols), lambda i, core_ref: (jnp.clip(i - (1 - core_ref[0]) * nt, 0, nt - 1), 0)),
                      spec, spec, spec],
            out_specs=[spec] * 4),
        out_shape=[jax.ShapeDtypeStruct((rows, cols), F32)] * 4,
        compiler_params=_params(("parallel",)),
    )(core, mine, theirs, w, m, v)


def _sum_parts(parts, name):
    rows, cols = parts[0].shape
    tr = rows
    for cand in (256, 128, 64, 32, 16, 8):
        if rows % cand == 0 and cand * cols * 4 <= (1 << 20):
            tr = cand
            break

    def body(*refs):
        acc = refs[0][...].astype(F32)
        for r in refs[1:-1]:
            acc = acc + r[...].astype(F32)
        refs[-1][...] = acc

    spec = _bs((tr, cols), lambda i: (i, 0))
    return pl.pallas_call(
        body, name=name, grid=(rows // tr,), in_specs=[spec] * len(parts), out_specs=spec,
        out_shape=jax.ShapeDtypeStruct((rows, cols), F32), compiler_params=_params(("parallel",)),
    )(*parts)


ANY = pl.BlockSpec(memory_space=pl.ANY)


def _other_chips(x, y):
    return [(1 - x, y), (x, 1 - y), (1 - x, 1 - y)]


def _gather_shards(arrays, name):
    n = len(arrays)

    def body(*refs):
        start, forward, finish = _gather_phases(arrays, refs[:n], refs[n:2 * n], refs[2 * n:])
        start()
        forward()
        finish()

    return pl.pallas_call(
        body, name=name, in_specs=[ANY] * n, out_specs=[ANY] * n,
        out_shape=_gather_out_shapes(arrays), scratch_shapes=_gather_sems(n),
        compiler_params=pltpu.CompilerParams(has_side_effects=True),
    )(*arrays)


def _gather_out_shapes(arrays):
    return [jax.ShapeDtypeStruct((4,) + a.shape, a.dtype) for a in arrays]


def _gather_sems(n):
    dma = lambda k: pltpu.SemaphoreType.DMA((k,))
    return [dma(3 * n), dma(3 * n), dma(3 * n), dma(3 * n), dma(n), dma(n)]


def _gather_phases(arrays, ins, outs, sems):
    n = len(arrays)
    ici_send, ici_recv, d2d_send, d2d_recv, own_send, own_recv = sems

    def place():
        x, y, c = lax.axis_index("x"), lax.axis_index("y"), lax.axis_index("c")
        return x, y, c, 2 * x + y, _other_chips(x, y)

    def half(i, who):
        h = arrays[i].shape[0] // 2
        return pl.ds(who * h, h)

    def ici(i, j, src_chip, to, c):
        return pltpu.make_async_remote_copy(
            src_ref=ins[i].at[half(i, c)], dst_ref=outs[i].at[src_chip, half(i, c)], send_sem=ici_send.at[3 * i + j],
            recv_sem=ici_recv.at[3 * i + j], device_id=to, device_id_type=MESH)

    def d2d(i, j, src_chip, who, sib):
        piece = outs[i].at[src_chip, half(i, who)]
        return pltpu.make_async_remote_copy(
            src_ref=piece, dst_ref=piece, send_sem=d2d_send.at[3 * i + j], recv_sem=d2d_recv.at[3 * i + j],
            device_id=sib, device_id_type=MESH)

    def own(i, me, sib):
        return pltpu.make_async_remote_copy(
            src_ref=ins[i], dst_ref=outs[i].at[me], send_sem=own_send.at[i], recv_sem=own_recv.at[i],
            device_id=sib, device_id_type=MESH)

    def start():
        x, y, c, me, chips = place()
        for i in range(n):
            own(i, me, (x, y, 1 - c)).start()
            for j, (px, py) in enumerate(chips):
                ici(i, j, me, (px, py, c), c).start()

    def forward():
        x, y, c, me, chips = place()
        for i in range(n):
            for j, (px, py) in enumerate(chips):
                ici(i, j, 2 * px + py, (px, py, c), c).wait_recv()
                d2d(i, j, 2 * px + py, c, (x, y, 1 - c)).start()

    def finish():
        x, y, c, me, chips = place()
        sib = (x, y, 1 - c)
        for i in range(n):
            for j, (px, py) in enumerate(chips):
                d2d(i, j, 2 * px + py, 1 - c, sib).wait_recv()
            own(i, me, sib).wait_recv()
        for i in range(n):
            own(i, me, sib).wait_send()
            for j, (px, py) in enumerate(chips):
                ici(i, j, me, (px, py, c), c).wait_send()
                d2d(i, j, 2 * px + py, c, sib).wait_send()

    return start, forward, finish


def _scatter_phases(ins, outs, sems):
    send_sems, recv_sems = sems

    def copies():
        x, y, c = lax.axis_index("x"), lax.axis_index("y"), lax.axis_index("c")
        return [pltpu.make_async_remote_copy(
            src_ref=ins[a].at[2 * qx + qy], dst_ref=outs[a].at[j], send_sem=send_sems.at[3 * a + j],
            recv_sem=recv_sems.at[3 * a + j], device_id=(qx, qy, c), device_id_type=MESH)
            for a in range(len(ins)) for j, (qx, qy) in enumerate(_other_chips(x, y))]

    def start():
        for rc in copies():
            rc.start()

    def finish():
        for rc in copies():
            rc.wait_recv()
        for rc in copies():
            rc.wait_send()

    return start, finish


def _scatter_out_shapes(stacks):
    return [jax.ShapeDtypeStruct((3,) + a.shape[1:], a.dtype) for a in stacks]


def _scatter_sems(n):
    return [pltpu.SemaphoreType.DMA((3 * n,)), pltpu.SemaphoreType.DMA((3 * n,))] if n else []


def _pair_exchange(stacks, name):
    n = len(stacks)

    def body(*refs):
        ins, outs = refs[:n], refs[n:2 * n]
        send_sems, recv_sems = refs[2 * n:]
        x, y, c = lax.axis_index("x"), lax.axis_index("y"), lax.axis_index("c")
        cps = []
        for i in range(n):
            h = stacks[i].shape[1] // 2
            rc = pltpu.make_async_remote_copy(
                src_ref=ins[i].at[:, pl.ds((1 - c) * h, h)], dst_ref=outs[i], send_sem=send_sems.at[i],
                recv_sem=recv_sems.at[i], device_id=(x, y, 1 - c), device_id_type=MESH)
            rc.start()
            cps.append(rc)
        for rc in cps:
            rc.wait_recv()
        for rc in cps:
            rc.wait_send()

    return pl.pallas_call(
        body, name=name, in_specs=[ANY] * n, out_specs=[ANY] * n,
        out_shape=[jax.ShapeDtypeStruct((4, a.shape[1] // 2) + a.shape[2:], a.dtype) for a in stacks],
        scratch_shapes=[pltpu.SemaphoreType.DMA((n,)), pltpu.SemaphoreType.DMA((n,))],
        compiler_params=pltpu.CompilerParams(has_side_effects=True),
    )(*stacks)


def _pair_sum(own, theirs, core, name):
    _, r, cols = own.shape
    h = r // 2
    tr = next(t for t in (256, 128, 64, 32, 16) if h % t == 0 and t * cols * 4 <= (1 << 20))
    nt = h // tr

    def body(core_ref, own_ref, th_ref, o32_ref, o16_ref):
        del core_ref
        acc = own_ref[...] + th_ref[...].astype(F32)
        o32_ref[...] = acc
        o16_ref[...] = acc.astype(BF16)

    out = _bs((1, tr, cols), lambda j, t, core_ref: (j, t, 0))
    return pl.pallas_call(
        body, name=name,
        grid_spec=pltpu.PrefetchScalarGridSpec(
            num_scalar_prefetch=1, grid=(4, nt),
            in_specs=[_bs((1, tr, cols), lambda j, t, core_ref: (j, core_ref[0] * nt + t, 0)), out],
            out_specs=[out, out]),
        out_shape=[jax.ShapeDtypeStruct((4, h, cols), F32), jax.ShapeDtypeStruct((4, h, cols), BF16)],
        compiler_params=_params(("parallel", "parallel")),
    )(core, own, theirs)


def _swap_sibling(arrays, name):
    n = len(arrays)

    def body(*refs):
        ins, outs = refs[:n], refs[n:2 * n]
        send_sems, recv_sems = refs[2 * n:]
        sib = (lax.axis_index("x"), lax.axis_index("y"), 1 - lax.axis_index("c"))
        cps = []
        for i in range(n):
            rc = pltpu.make_async_remote_copy(src_ref=ins[i], dst_ref=outs[i], send_sem=send_sems.at[i],
                                              recv_sem=recv_sems.at[i], device_id=sib, device_id_type=MESH)
            rc.start()
            cps.append(rc)
        for rc in cps:
            rc.wait_recv()
        for rc in cps:
            rc.wait_send()

    return pl.pallas_call(
        body, name=name, in_specs=[ANY] * n, out_specs=[ANY] * n,
        out_shape=[jax.ShapeDtypeStruct(a.shape, a.dtype) for a in arrays],
        scratch_shapes=[pltpu.SemaphoreType.DMA((n,)), pltpu.SemaphoreType.DMA((n,))],
        compiler_params=pltpu.CompilerParams(has_side_effects=True),
    )(*arrays)


def _all_reduce_small(v):
    rows = v.shape[0]
    h = rows // 2

    def body(v_ref, o_ref, sib, pair, buf, send_sems, recv_sems):
        x, y, c = lax.axis_index("x"), lax.axis_index("y"), lax.axis_index("c")
        me = 2 * x + y
        sibling = (x, y, 1 - c)
        mine = pl.ds(pl.multiple_of(c * h, 8), h)
        theirs = pl.ds(pl.multiple_of((1 - c) * h, 8), h)

        def copy(src, dst, k, to):
            return pltpu.make_async_remote_copy(src_ref=src, dst_ref=dst, send_sem=send_sems.at[k],
                                                recv_sem=recv_sems.at[k], device_id=to, device_id_type=MESH)

        swap = copy(v_ref, sib, 0, sibling)
        swap.start()
        swap.wait_recv()
        pair[...] = v_ref[...] + sib[...]
        buf[me] = pair[mine, :]
        out = [copy(buf.at[me], buf.at[me], 1 + j, (px, py, c)) for j, (px, py) in enumerate(_other_chips(x, y))]
        for rc in out:
            rc.start()
        for j, (px, py) in enumerate(_other_chips(x, y)):
            copy(buf.at[me], buf.at[2 * px + py], 1 + j, (px, py, c)).wait_recv()
        o_ref[mine, :] = (buf[0] + buf[1]) + (buf[2] + buf[3])
        back = copy(o_ref.at[mine], o_ref.at[mine], 4, sibling)
        back.start()
        copy(o_ref.at[theirs], o_ref.at[theirs], 4, sibling).wait_recv()
        for rc in [swap, back] + out:
            rc.wait_send()

    vmem = pl.BlockSpec(memory_space=pltpu.VMEM)
    return pl.pallas_call(
        body, name="all_reduce_small", in_specs=[vmem], out_specs=vmem,
        out_shape=jax.ShapeDtypeStruct((rows, 128), F32),
        scratch_shapes=[pltpu.VMEM((rows, 128), F32), pltpu.VMEM((rows, 128), F32), pltpu.VMEM((4, h, 128), F32),
                        pltpu.SemaphoreType.DMA((5,)), pltpu.SemaphoreType.DMA((5,))],
        compiler_params=pltpu.CompilerParams(has_side_effects=True, vmem_limit_bytes=VMEM_LIMIT),
    )(v)


def _rope_tables(s):
    half = HD // 2
    inv = 10000.0 ** (-jnp.arange(half, dtype=F32) / half)
    ang = jnp.arange(s, dtype=F32)[:, None] * inv[None, :]
    cos, sin = jnp.cos(ang), jnp.sin(ang)
    return jnp.concatenate([cos, cos], axis=1), jnp.concatenate([sin, sin], axis=1)


LATE = ['attn_w_o', 'rwkv_w_o', 'x_w_kv', 'x_w_o', 'w_out']


def _local_step(x, mem, target, norm_g, mem_norm_g, w_in, gate_b, gq, gk, sink, wa, mu, k_k, k_a, r_k, w0, w2, a0, a2,
                ln_w, ln_b, wr, w_kv, gxq, gxk, wx, w_out, late_shards=None, early_exchange=None):
    s = x.shape[0]
    cos, sin = _rope_tables(s)
    r_k = r_k.reshape(1, RW)

    proj, h = _proj_fwd(x, norm_g, w_in)
    ya, ya_t = _attn_fwd(proj, cos, sin, gq, gk, sink)
    ps = _shift_fwd(proj, mu)
    kk, dec0, kd0, b0, dec1, kd1, b1 = _pre_fwd(ps, k_k, k_a, w0, w2, a0, a2)
    ((y0, ck0), (y1, ck1)), stacks = _scan2_fwd([(dec0, kd0, b0), (dec1, kd1, b1)], ps, kk, gather=late_shards or ())
    if late_shards:
        st = dict(zip(LATE, stacks))
        wa, wr, wx = (_unshard_cols(st[n]) for n in ('attn_w_o', 'rwkv_w_o', 'x_w_o'))
        w_kv, w_out = st['x_w_kv'].reshape(D, 2 * XW), st['w_out'].reshape(D, D)
    mkv, mn = _mem_kv(mem, mem_norm_g, w_kv)
    yx, yx_t = _xattn_fwd(proj, mkv, gxq, gxk)
    yr, yr_t = _post_fwd(y0, y1, ps, kd0, kd1, proj, r_k, ln_w, ln_b)
    merged, merged_t = _merge_fwd(ya, yr, yx, wa, wr, wx, proj, gate_b)
    loss_tile, dout, dout16 = _out_fwd(merged, w_out, x, target)
    loss_sum = loss_tile[0, 0]

    g = {}
    sk = min(1024, s)
    dmerged = _matmul(dout16, w_out, mode="nt", m=s, n=D, k=D, tm=sk, tn=1024, tk=1024, name="dmerged")
    g["w_out"], g["w_out_bf16"] = _matmul(merged_t, dout16, mode="nn", m=D, n=D, k=s, tm=1024, tn=1024, tk=sk,
                                          name="grad_w_out", twin=True)
    dg0, dg1, dg2, du0, du1, du2, dya, dyr, dyx = _merge_bwd(ya, yr, yx, wa, wr, wx, proj, gate_b, dmerged)
    for n, yt, du in (("attn_w_o", ya_t, du0), ("rwkv_w_o", yr_t, du1), ("x_w_o", yx_t, du2)):
        g[n], g[n + "_bf16"] = _matmul(yt, du, mode="nn", m=yt.shape[0], n=D, k=s, tm=yt.shape[0], tn=512, tk=s,
                                       name="grad_" + n, shards=4, twin=True)
    dmg = jnp.concatenate([dg0, dg1, dg2], axis=1)
    g["gate_b"] = _colsum(dmg, "grad_gate_b")

    daq, dak, dav, dag, g["attn_q_norm_g"], g["attn_k_norm_g"], g["attn_sink"] = _attn_bwd(proj, cos, sin, gq, gk, sink, dya)

    dxq, dxg, dmkv, g["x_q_norm_g"], g["x_k_norm_g"] = _xattn_bwd(proj, mkv, gxq, gxk, dyx)
    g["x_w_kv"], g["x_w_kv_bf16"] = _matmul(mn, dmkv, mode="tn", m=D, n=2 * XW, k=NMEM, tm=512, tn=512, tk=NMEM,
                                            name="grad_x_w_kv", twin=True)
    dmn = _matmul(dmkv, w_kv, mode="nt", m=NMEM, n=D, k=2 * XW, tm=NMEM, tn=512, tk=2 * XW, name="dmn")
    g["mem_norm_g"] = _mem_bwd(mem, dmn)

    dys, dr_p, dv_p, dkd0_p, dkd1_p, drg, g["rwkv_r_k"], g["rwkv_ln_w"], g["rwkv_ln_b"] = _post_bwd(
        y0, y1, ps, kd0, kd1, proj, r_k, ln_w, ln_b, dyr)
    sent = early_exchange(g) if early_exchange else ()
    ((dr0, dd0, db0, dk0, dkk0, dv0), (dr1, dd1, db1, dk1, dkk1, dv1)), received = _scan2_bwd(
        [(dec0, kd0, b0, ck0), (dec1, kd1, b1, ck1)], ps, kk, dys, scatter=sent)
    dr = dr_p + dr0 + dr1
    dv = dv_p + dv0 + dv1
    cts = (dkk0 + dkk1, dd0, dk0 + dkd0_p, db0, dd1, dk1 + dkd1_p, db1)
    dps, g["rwkv_k_k"], g["rwkv_k_a"], g["rwkv_w0"], g["rwkv_w2"], g["rwkv_a0"], g["rwkv_a2"] = _pre_bwd(
        ps, k_k, k_a, w0, w2, a0, a2, dr, dv, cts)
    drs, g["rwkv_mu"] = _shift_bwd(proj, mu, dps)

    dproj = jnp.concatenate([daq.astype(BF16), dak.astype(BF16), dav.astype(BF16), dag.astype(BF16), drs.astype(BF16),
                             drg.astype(BF16), dxq.astype(BF16), dxg.astype(BF16), dmg], axis=1)
    dproj4 = jnp.stack([dproj[:, j * (NIN // 4):(j + 1) * (NIN // 4)] for j in range(4)])
    g["w_in"], g["w_in_bf16"] = _grad_w_in(h.T, dproj4)
    g["rwkv_r_k"] = g["rwkv_r_k"].reshape(AH, HD)
    return loss_sum, g, (dproj, w_in, x, norm_g, dout), received


WEIGHTS = ['norm_g', 'mem_norm_g', 'w_in', 'gate_b', 'attn_q_norm_g', 'attn_k_norm_g', 'attn_sink', 'attn_w_o',
           'rwkv_mu', 'rwkv_k_k', 'rwkv_k_a', 'rwkv_r_k', 'rwkv_w0', 'rwkv_w2', 'rwkv_a0', 'rwkv_a2', 'rwkv_ln_w',
           'rwkv_ln_b', 'rwkv_w_o', 'x_w_kv', 'x_q_norm_g', 'x_k_norm_g', 'x_w_o', 'w_out']
BIG = ['w_in', 'attn_w_o', 'rwkv_w_o', 'x_w_kv', 'x_w_o', 'w_out']
COL_SHARDED = ['w_in', 'attn_w_o', 'rwkv_w_o', 'x_w_o']
LORA = ['rwkv_w0', 'rwkv_w2', 'rwkv_a0', 'rwkv_a2']
SMALL = [n for n in WEIGHTS if n not in BIG]


def _unshard_cols(stack):
    return jnp.concatenate([stack[i] for i in range(4)], axis=-1)


def kernel(x, mem, norm_g, mem_norm_g, w_in, gate_b, attn_q_norm_g, attn_k_norm_g, attn_sink, attn_w_o, rwkv_mu, rwkv_k_k, rwkv_k_a, rwkv_r_k, rwkv_w0, rwkv_w2, rwkv_a0, rwkv_a2, rwkv_ln_w, rwkv_ln_b, rwkv_w_o, x_w_kv, x_q_norm_g, x_k_norm_g, x_w_o, w_out, loss_target, m_norm_g, m_mem_norm_g, m_w_in, m_gate_b, m_attn_q_norm_g, m_attn_k_norm_g, m_attn_sink, m_attn_w_o, m_rwkv_mu, m_rwkv_k_k, m_rwkv_k_a, m_rwkv_r_k, m_rwkv_w0, m_rwkv_w2, m_rwkv_a0, m_rwkv_a2, m_rwkv_ln_w, m_rwkv_ln_b, m_rwkv_w_o, m_x_w_kv, m_x_q_norm_g, m_x_k_norm_g, m_x_w_o, m_w_out, v_norm_g, v_mem_norm_g, v_w_in, v_gate_b, v_attn_q_norm_g, v_attn_k_norm_g, v_attn_sink, v_attn_w_o, v_rwkv_mu, v_rwkv_k_k, v_rwkv_k_a, v_rwkv_r_k, v_rwkv_w0, v_rwkv_w2, v_rwkv_a0, v_rwkv_a2, v_rwkv_ln_w, v_rwkv_ln_b, v_rwkv_w_o, v_x_w_kv, v_x_q_norm_g, v_x_k_norm_g, v_x_w_o, v_w_out):
    args = dict(locals())
    canon = lambda a: a[0] if a.ndim > 2 else a
    w = {n: canon(args[n]) for n in WEIGHTS}
    m = {n: canon(args["m_" + n]) for n in WEIGHTS}
    v = {n: canon(args["v_" + n]) for n in WEIGHTS}
    shard = 2 * lax.axis_index("x") + lax.axis_index("y")

    now = ["w_in"] + LORA
    local = [w["w_in"].astype(BF16)] + [w[n].reshape(2, -1, w[n].shape[-1]) for n in LORA]
    stacks = dict(zip(now, _gather_shards(local, "gather_weights")))
    full = {"w_in": _unshard_cols(stacks["w_in"])}
    for n in LORA:
        full[n] = _unshard_cols(stacks[n]).reshape(w[n].shape[:-1] + (RW,))

    core = lax.axis_index("c").astype(jnp.int32).reshape(1)
    pair32 = {}

    def as_stack(g, n, dtype):
        t = g[n] if dtype == F32 else g[n + "_bf16"]
        return t if n in COL_SHARDED else t.reshape((4, t.shape[0] // 4) + t.shape[1:])

    def pair_sums(g, names, tag):
        sibling = _pair_exchange([as_stack(g, n, BF16) for n in names], "pair_exchange_" + tag)
        sent = []
        for n, th in zip(names, sibling):
            pair32[n], a16 = _pair_sum(as_stack(g, n, F32), th, core, "pair_sum_" + n)
            sent.append(a16)
        return sent

    loss_sum, g, deferred, recv_late = _local_step(
        x[0], mem[0], loss_target[0], w["norm_g"], w["mem_norm_g"], full["w_in"], w["gate_b"], w["attn_q_norm_g"],
        w["attn_k_norm_g"], w["attn_sink"], None, w["rwkv_mu"], w["rwkv_k_k"], w["rwkv_k_a"], w["rwkv_r_k"],
        full["rwkv_w0"], full["rwkv_w2"], full["rwkv_a0"], full["rwkv_a2"], w["rwkv_ln_w"], w["rwkv_ln_b"],
        None, None, w["x_q_norm_g"], w["x_k_norm_g"], None, None,
        late_shards=[w[n].astype(BF16) for n in LATE], early_exchange=lambda g: pair_sums(g, LATE, "late"))

    loss = lax.psum(0.5 * loss_sum / D, ("x", "y", "c"))

    grad_x, g["norm_g"], recv_w_in = _in_bwd(*deferred, stacks=pair_sums(g, ["w_in"], "w_in"))
    halves = []
    for n, r in zip(BIG, recv_w_in + recv_late):
        own = lax.dynamic_index_in_dim(pair32[n], shard, 0, keepdims=False)
        halves.append(_sum_parts([own, r[0], r[1], r[2]], "sum_" + n))
    other_halves = _swap_sibling(halves, "swap_halves")

    out_g, out_d, out_m, out_v = {}, {}, {}, {}
    for n, mine, theirs in zip(BIG, halves, other_halves):
        out_g[n], out_d[n], out_m[n], out_v[n] = _adamw_halves(mine, theirs, core, w[n], m[n], v[n], "adamw_" + n)

    flat = jnp.concatenate([g[n].reshape(-1) for n in SMALL])
    total = flat.shape[0]
    padded = -(-total // 2048) * 2048
    flat = jnp.pad(flat, (0, padded - total)).reshape(padded // 128, 128)
    red = _all_reduce_small(flat).reshape(-1)
    off = 0
    gs = {}
    for n in SMALL:
        size = g[n].size
        t = red[off:off + size].reshape(g[n].shape)
        off += size
        if n in LORA:
            wd = t.shape[-1] // 4
            t = lax.dynamic_slice_in_dim(t, shard * wd, wd, axis=t.ndim - 1)
        gs[n] = t

    def pack(d):
        f = jnp.concatenate([d[n].reshape(-1) for n in SMALL])
        return jnp.pad(f, (0, -(-f.shape[0] // 1024) * 1024 - f.shape[0])).reshape(-1, 128)

    pg, pd, pm, pv = _adamw([pack(gs)], pack(w), pack(m), pack(v), "adamw_small")
    off = 0
    for n in SMALL:
        size = w[n].size
        for dst, src in ((out_g, pg), (out_d, pd), (out_m, pm), (out_v, pv)):
            dst[n] = src.reshape(-1)[off:off + size].reshape(w[n].shape)
        off += size

    lead = lambda d: [d[n][None] if args[n].ndim > 2 else d[n] for n in WEIGHTS]
    return (loss, grad_x[None], *lead(out_g), *lead(out_d), *lead(out_m), *lead(out_v))
```

```python
import jax
import jax.numpy as jnp
from jax import lax
from jax.experimental import pallas as pl
from jax.experimental.pallas import tpu as pltpu

F32 = jnp.float32
BF16 = jnp.bfloat16
HI = lax.Precision.HIGH
MESH = pl.DeviceIdType.MESH

D = 2048
NMEM = 256
NORM_EPS = 1e-6
NEG_INF = -1e30
GN_EPS = 64e-5
HD = 64
AH = 12
AKV = 4
RW = 768
XH = 4
XD = 128
XW = 512
NIN = 12544
RSW = 2560
C_AQ, C_AK, C_AV, C_AG, C_RS, C_RG, C_XQ, C_XG, C_MG = 0, 768, 1024, 1280, 2048, 4608, 5376, 5888, 6400
WINDOW = 128
QB = 256
WIN = QB + 2 * WINDOW
TC = 16
NPAIR = 6

ADAM_LR, ADAM_B1, ADAM_B2, ADAM_EPS, ADAM_WD, ADAM_STEP = 0.001, 0.9, 0.999, 1e-08, 0.01, 10

VMEM_LIMIT = 56 * 1024 * 1024


def _bs(shape, imap):
    return pl.BlockSpec(shape, imap)


def _params(sem=None, vmem=VMEM_LIMIT):
    return pltpu.CompilerParams(dimension_semantics=sem, vmem_limit_bytes=vmem)


def _dot(a, b, dims):
    return lax.dot_general(a.astype(BF16), b.astype(BF16), (dims, ((), ())), preferred_element_type=F32)


@jax.custom_vjp
def _mm_nn(a, b):
    return _dot(a, b, ((1,), (0,)))


def _mm_nn_fwd(a, b):
    return _mm_nn(a, b), (a, b)


def _mm_nn_bwd(res, ct):
    a, b = res
    return _dot(ct, b, ((1,), (1,))), _dot(a, ct, ((0,), (0,)))


_mm_nn.defvjp(_mm_nn_fwd, _mm_nn_bwd)


@jax.custom_vjp
def _mm_nt(a, b):
    return _dot(a, b, ((1,), (1,)))


def _mm_nt_fwd(a, b):
    return _mm_nt(a, b), (a, b)


def _mm_nt_bwd(res, ct):
    a, b = res
    return _dot(ct, b, ((1,), (0,))), _dot(ct, a, ((0,), (0,)))


_mm_nt.defvjp(_mm_nt_fwd, _mm_nt_bwd)


def _seg_matrix(n, seg):
    r = lax.broadcasted_iota(jnp.int32, (n, n), 0) // seg
    c = lax.broadcasted_iota(jnp.int32, (n, n), 1) // seg
    return (r == c).astype(F32)


def _rot_matrix():
    r = lax.broadcasted_iota(jnp.int32, (HD, HD), 0)
    c = lax.broadcasted_iota(jnp.int32, (HD, HD), 1)
    return jnp.where(c == r + HD // 2, 1.0, 0.0).astype(F32) - jnp.where(c == r - HD // 2, 1.0, 0.0).astype(F32)


def _hdot(a, m):
    return jnp.dot(a, m, precision=HI, preferred_element_type=F32)


def _rms(t, g):
    return t * lax.rsqrt(jnp.mean(t * t, axis=-1, keepdims=True) + NORM_EPS) * g


def _silu(t):
    return t * jax.nn.sigmoid(t)


def _softplus(z):
    return jnp.maximum(z, 0.0) + jnp.log(1.0 + jnp.exp(-jnp.abs(z)))


def _matmul(a, b, *, mode, m, n, k, tm, tn, tk, name, a_off=(0, 0), b_off=(0, 0), out_dtype=F32, shards=0, twin=False):
    nk = k // tk
    if mode == "tn":
        a_spec = _bs((tk, tm), lambda i, j, kk: (kk + a_off[0], i + a_off[1]))
        dims = ((0,), (0,))
    else:
        a_spec = _bs((tm, tk), lambda i, j, kk: (i + a_off[0], kk + a_off[1]))
        dims = ((1,), (1,)) if mode == "nt" else ((1,), (0,))
    if mode == "nt":
        b_spec = _bs((tn, tk), lambda i, j, kk: (j + b_off[0], kk + b_off[1]))
    else:
        b_spec = _bs((tk, tn), lambda i, j, kk: (kk + b_off[0], j + b_off[1]))
    if shards:
        per = n // shards // tn
        o_spec = _bs((1, tm, tn), lambda i, j, kk: (j // per, i, j % per))
        o_shape = (shards, m, n // shards)
    else:
        o_spec = _bs((tm, tn), lambda i, j, kk: (i, j))
        o_shape = (m, n)

    def body(a_ref, b_ref, *rest):
        o_refs, acc = rest[:-1], rest[-1]
        kk = pl.program_id(2)

        @pl.when(kk == 0)
        def _():
            acc[...] = jnp.zeros_like(acc)

        acc[...] += _dot(a_ref[...], b_ref[...], dims)

        @pl.when(kk == nk - 1)
        def _():
            for o_ref in o_refs:
                o_ref[...] = acc[...].astype(o_ref.dtype).reshape(o_ref.shape)

    dtypes = [out_dtype, BF16] if twin else [out_dtype]
    res = pl.pallas_call(
        body, name=name, grid=(m // tm, n // tn, nk),
        in_specs=[a_spec, b_spec], out_specs=[o_spec] * len(dtypes),
        out_shape=[jax.ShapeDtypeStruct(o_shape, dt) for dt in dtypes],
        scratch_shapes=[pltpu.VMEM((tm, tn), F32)],
        compiler_params=_params(("parallel", "parallel", "arbitrary")),
    )(a, b)
    return res if twin else res[0]


def _grad_w_in(ht, dproj4):
    s = ht.shape[1]
    ws = NIN // 4
    tm, tk = 256, s
    nk = s // tk

    def body(a_ref, b_ref, o32_ref, o16_ref, acc):
        kk = pl.program_id(2)

        @pl.when(kk == 0)
        def _():
            acc[...] = jnp.zeros_like(acc)

        acc[...] += jnp.dot(a_ref[...], b_ref[0], preferred_element_type=F32)

        @pl.when(kk == nk - 1)
        def _():
            o32_ref[0] = acc[...]
            o16_ref[0] = acc[...].astype(BF16)

    out = _bs((1, tm, ws), lambda j, i, kk: (j, i, 0))
    return pl.pallas_call(
        body, name="grad_w_in", grid=(4, D // tm, nk),
        in_specs=[_bs((tm, tk), lambda j, i, kk: (i, kk)), _bs((1, tk, ws), lambda j, i, kk: (j, kk, 0))],
        out_specs=[out, out],
        out_shape=[jax.ShapeDtypeStruct((4, D, ws), F32), jax.ShapeDtypeStruct((4, D, ws), BF16)],
        scratch_shapes=[pltpu.VMEM((tm, ws), F32)],
        compiler_params=_params(("parallel", "parallel", "arbitrary")),
    )(ht, dproj4)


def _proj_fwd(x, g, w):
    s = x.shape[0]
    tm, tn = min(1024, s), 896

    def body(x_ref, g_ref, w_ref, o_ref, h_ref, hs):
        @pl.when(pl.program_id(1) == 0)
        def _():
            h = _rms(x_ref[...], g_ref[...]).astype(BF16)
            hs[...] = h
            h_ref[...] = h

        o_ref[...] = jnp.dot(hs[...], w_ref[...], preferred_element_type=F32)

    return pl.pallas_call(
        body, name="proj_fwd", grid=(s // tm, NIN // tn),
        in_specs=[_bs((tm, D), lambda i, j: (i, 0)), _bs((1, D), lambda i, j: (0, 0)), _bs((D, tn), lambda i, j: (0, j))],
        out_specs=[_bs((tm, tn), lambda i, j: (i, j)), _bs((tm, D), lambda i, j: (i, 0))],
        out_shape=[jax.ShapeDtypeStruct((s, NIN), F32), jax.ShapeDtypeStruct((s, D), BF16)],
        scratch_shapes=[pltpu.VMEM((tm, D), BF16)],
        compiler_params=_params(("parallel", "arbitrary")),
    )(x, g, w)


def _rope(t, cos, sin, rot):
    return t * cos + _hdot(t, rot) * sin


def _attn_tile(qs, ks, vs, gs, sinks, gq, gk, cq, sq, ck, sk, mask, rot):
    heads = range(AH)
    kv = [h // (AH // AKV) for h in heads]
    kh = [_rope(_rms(ks[j], gk), ck, sk, rot) for j in range(AKV)]
    qh = [_rope(_rms(qs[h], gq), cq, sq, rot) for h in heads]
    sc = [jnp.where(mask, _mm_nt(qh[h], kh[kv[h]]) * (HD ** -0.5), NEG_INF) for h in heads]
    mx = [lax.stop_gradient(jnp.maximum(jnp.max(sc[h], axis=-1, keepdims=True), sinks[h])) for h in heads]
    p = [jnp.exp(sc[h] - mx[h]) for h in heads]
    den = [jnp.sum(p[h], axis=-1, keepdims=True) + jnp.exp(sinks[h] - mx[h]) for h in heads]
    o = [_mm_nn(p[h] / den[h], vs[kv[h]]) for h in heads]
    return [o[h] * _silu(gs[h]) for h in heads]


def _attn_load(n, s, aq_ref, ak_ref, av_ref, ag_refs, cos_ref, sin_ref, sink_ref):
    start = pl.multiple_of(jnp.clip(n * QB - WINDOW, 0, s - WIN), WINDOW)
    q0 = pl.multiple_of(n * QB, QB)
    qs = [aq_ref[:, h * HD:(h + 1) * HD] for h in range(AH)]
    ks = [ak_ref[pl.ds(start, WIN), h * HD:(h + 1) * HD] for h in range(AKV)]
    vs = [av_ref[pl.ds(start, WIN), h * HD:(h + 1) * HD] for h in range(AKV)]
    gs = [ag_refs[h // 4][:, (h % 4) * HD:(h % 4 + 1) * HD] for h in range(AH)]
    sinks = [sink_ref[0:1, h:h + 1] for h in range(AH)]
    cq, sq = cos_ref[pl.ds(q0, QB), :], sin_ref[pl.ds(q0, QB), :]
    ck, sk = cos_ref[pl.ds(start, WIN), :], sin_ref[pl.ds(start, WIN), :]
    qpos = q0 + lax.broadcasted_iota(jnp.int32, (QB, WIN), 0)
    kpos = start + lax.broadcasted_iota(jnp.int32, (QB, WIN), 1)
    mask = jnp.abs(kpos - qpos) <= WINDOW
    return start, qs, ks, vs, gs, sinks, cq, sq, ck, sk, mask


def _attn_specs(s):
    return [
        _bs((QB, 768), lambda n: (n, 0)),
        _bs((s, 256), lambda n: (0, C_AK // 256)),
        _bs((s, 256), lambda n: (0, C_AV // 256)),
        _bs((QB, 256), lambda n: (n, C_AG // 256)),
        _bs((QB, 256), lambda n: (n, C_AG // 256 + 1)),
        _bs((QB, 256), lambda n: (n, C_AG // 256 + 2)),
        _bs((s, HD), lambda n: (0, 0)),
        _bs((s, HD), lambda n: (0, 0)),
        _bs((1, HD), lambda n: (0, 0)),
        _bs((1, HD), lambda n: (0, 0)),
        _bs((1, AH), lambda n: (0, 0)),
    ]


def _attn_fwd(proj, cos, sin, gq, gk, sink):
    s = proj.shape[0]

    def body(aq_ref, ak_ref, av_ref, ag0, ag1, ag2, cos_ref, sin_ref, gq_ref, gk_ref, sink_ref, o_ref, ot_ref):
        n = pl.program_id(0)
        _, qs, ks, vs, gs, sinks, cq, sq, ck, sk, mask = _attn_load(
            n, s, aq_ref, ak_ref, av_ref, (ag0, ag1, ag2), cos_ref, sin_ref, sink_ref)
        outs = _attn_tile(qs, ks, vs, gs, sinks, gq_ref[...], gk_ref[...], cq, sq, ck, sk, mask, _rot_matrix())
        for h in range(AH):
            o_ref[:, h * HD:(h + 1) * HD] = outs[h]
        ot_ref[...] = o_ref[...].T.astype(BF16)

    return pl.pallas_call(
        body, name="attn_fwd", grid=(s // QB,),
        in_specs=_attn_specs(s), out_specs=[_bs((QB, 768), lambda n: (n, 0)), _bs((768, QB), lambda n: (0, n))],
        out_shape=[jax.ShapeDtypeStruct((s, 768), F32), jax.ShapeDtypeStruct((768, s), BF16)],
        compiler_params=_params(("arbitrary",)),
    )(proj, proj, proj, proj, proj, proj, cos, sin, gq, gk, sink)


def _attn_bwd(proj, cos, sin, gq, gk, sink, dy):
    s = proj.shape[0]

    def body(aq_ref, ak_ref, av_ref, ag0, ag1, ag2, cos_ref, sin_ref, gq_ref, gk_ref, sink_ref, dy_ref,
             daq_ref, dak_ref, dav_ref, dag_ref, dgq_ref, dgk_ref, dsink_ref):
        n = pl.program_id(0)

        @pl.when(n == 0)
        def _():
            dak_ref[...] = jnp.zeros_like(dak_ref)
            dav_ref[...] = jnp.zeros_like(dav_ref)
            dgq_ref[...] = jnp.zeros_like(dgq_ref)
            dgk_ref[...] = jnp.zeros_like(dgk_ref)
            dsink_ref[...] = jnp.zeros_like(dsink_ref)

        start, qs, ks, vs, gs, sinks, cq, sq, ck, sk, mask = _attn_load(
            n, s, aq_ref, ak_ref, av_ref, (ag0, ag1, ag2), cos_ref, sin_ref, sink_ref)
        rot = _rot_matrix()

        def f(qs, ks, vs, gs, sinks, gq, gk):
            return _attn_tile(qs, ks, vs, gs, sinks, gq, gk, cq, sq, ck, sk, mask, rot)

        _, vjp = jax.vjp(f, qs, ks, vs, gs, sinks, gq_ref[...], gk_ref[...])
        dys = [dy_ref[:, h * HD:(h + 1) * HD] for h in range(AH)]
        dqs, dks, dvs, dgs, dsinks, dgq, dgk = vjp(dys)
        for h in range(AH):
            daq_ref[:, h * HD:(h + 1) * HD] = dqs[h]
            dag_ref[:, h * HD:(h + 1) * HD] = dgs[h]
            dsink_ref[0:1, h:h + 1] += dsinks[h]
        for h in range(AKV):
            dak_ref[pl.ds(start, WIN), h * HD:(h + 1) * HD] += dks[h]
            dav_ref[pl.ds(start, WIN), h * HD:(h + 1) * HD] += dvs[h]
        dgq_ref[...] += dgq
        dgk_ref[...] += dgk

    whole = lambda shape: _bs(shape, lambda n: (0, 0))
    return pl.pallas_call(
        body, name="attn_bwd", grid=(s // QB,),
        in_specs=_attn_specs(s) + [_bs((QB, 768), lambda n: (n, 0))],
        out_specs=[_bs((QB, 768), lambda n: (n, 0)), whole((s, 256)), whole((s, 256)), _bs((QB, 768), lambda n: (n, 0)),
                   whole((1, HD)), whole((1, HD)), whole((1, AH))],
        out_shape=[jax.ShapeDtypeStruct((s, 768), F32), jax.ShapeDtypeStruct((s, 256), F32),
                   jax.ShapeDtypeStruct((s, 256), F32), jax.ShapeDtypeStruct((s, 768), F32),
                   jax.ShapeDtypeStruct((1, HD), F32), jax.ShapeDtypeStruct((1, HD), F32),
                   jax.ShapeDtypeStruct((1, AH), F32)],
        compiler_params=_params(("arbitrary",)),
    )(proj, proj, proj, proj, proj, proj, cos, sin, gq, gk, sink, dy)


def _mem_kv(mem, g, w):
    def body(m_ref, g_ref, w_ref, o_ref, mn_ref):
        mn = _rms(m_ref[...], g_ref[...]).astype(BF16)
        mn_ref[...] = mn
        o_ref[...] = jnp.dot(mn, w_ref[...], preferred_element_type=F32)

    return pl.pallas_call(
        body, name="mem_kv",
        out_shape=[jax.ShapeDtypeStruct((NMEM, 2 * XW), F32), jax.ShapeDtypeStruct((NMEM, D), BF16)],
        compiler_params=_params(),
    )(mem, g, w)


def _xattn_tile(qs, gs, kms, vms, gxq, gxk):
    heads = range(XH)
    q = [_rms(qs[h], gxq) for h in heads]
    km = [_rms(kms[h], gxk) for h in heads]
    sc = [_mm_nt(q[h], km[h]) * (XD ** -0.5) for h in heads]
    p = [jnp.exp(sc[h] - lax.stop_gradient(jnp.max(sc[h], axis=-1, keepdims=True))) for h in heads]
    p = [p[h] / jnp.sum(p[h], axis=-1, keepdims=True) for h in heads]
    return [_mm_nn(p[h], vms[h]) * _silu(gs[h]) for h in heads]


XT = 256


def _xattn_specs():
    return [
        _bs((XT, 256), lambda i: (i, C_XQ // 256)), _bs((XT, 256), lambda i: (i, C_XQ // 256 + 1)),
        _bs((XT, 256), lambda i: (i, C_XG // 256)), _bs((XT, 256), lambda i: (i, C_XG // 256 + 1)),
        _bs((NMEM, 2 * XW), lambda i: (0, 0)),
        _bs((1, XD), lambda i: (0, 0)), _bs((1, XD), lambda i: (0, 0)),
    ]


def _xattn_load(q0, q1, g0, g1, mkv_ref):
    qs = [(q0, q1)[h // 2][:, (h % 2) * XD:(h % 2 + 1) * XD] for h in range(XH)]
    gs = [(g0, g1)[h // 2][:, (h % 2) * XD:(h % 2 + 1) * XD] for h in range(XH)]
    kms = [mkv_ref[:, h * XD:(h + 1) * XD] for h in range(XH)]
    vms = [mkv_ref[:, XW + h * XD:XW + (h + 1) * XD] for h in range(XH)]
    return qs, gs, kms, vms


def _xattn_fwd(proj, mkv, gxq, gxk):
    s = proj.shape[0]

    def body(q0, q1, g0, g1, mkv_ref, gxq_ref, gxk_ref, o_ref, ot_ref):
        qs, gs, kms, vms = _xattn_load(q0, q1, g0, g1, mkv_ref)
        outs = _xattn_tile(qs, gs, kms, vms, gxq_ref[...], gxk_ref[...])
        for h in range(XH):
            o_ref[:, h * XD:(h + 1) * XD] = outs[h]
        ot_ref[...] = o_ref[...].T.astype(BF16)

    return pl.pallas_call(
        body, name="xattn_fwd", grid=(s // XT,),
        in_specs=_xattn_specs(), out_specs=[_bs((XT, XW), lambda i: (i, 0)), _bs((XW, XT), lambda i: (0, i))],
        out_shape=[jax.ShapeDtypeStruct((s, XW), F32), jax.ShapeDtypeStruct((XW, s), BF16)],
        compiler_params=_params(("arbitrary",)),
    )(proj, proj, proj, proj, mkv, gxq, gxk)


def _xattn_bwd(proj, mkv, gxq, gxk, dy):
    s = proj.shape[0]

    def body(q0, q1, g0, g1, mkv_ref, gxq_ref, gxk_ref, dy_ref, dq_ref, dg_ref, dmkv_ref, dgxq_ref, dgxk_ref):
        @pl.when(pl.program_id(0) == 0)
        def _():
            dmkv_ref[...] = jnp.zeros_like(dmkv_ref)
            dgxq_ref[...] = jnp.zeros_like(dgxq_ref)
            dgxk_ref[...] = jnp.zeros_like(dgxk_ref)

        qs, gs, kms, vms = _xattn_load(q0, q1, g0, g1, mkv_ref)
        _, vjp = jax.vjp(_xattn_tile, qs, gs, kms, vms, gxq_ref[...], gxk_ref[...])
        dqs, dgs, dkms, dvms, dgxq, dgxk = vjp([dy_ref[:, h * XD:(h + 1) * XD] for h in range(XH)])
        for h in range(XH):
            dq_ref[:, h * XD:(h + 1) * XD] = dqs[h]
            dg_ref[:, h * XD:(h + 1) * XD] = dgs[h]
            dmkv_ref[:, h * XD:(h + 1) * XD] += dkms[h]
            dmkv_ref[:, XW + h * XD:XW + (h + 1) * XD] += dvms[h]
        dgxq_ref[...] += dgxq
        dgxk_ref[...] += dgxk

    whole = lambda shape: _bs(shape, lambda i: (0, 0))
    return pl.pallas_call(
        body, name="xattn_bwd", grid=(s // XT,),
        in_specs=_xattn_specs() + [_bs((XT, XW), lambda i: (i, 0))],
        out_specs=[_bs((XT, XW), lambda i: (i, 0)), _bs((XT, XW), lambda i: (i, 0)), whole((NMEM, 2 * XW)),
                   whole((1, XD)), whole((1, XD))],
        out_shape=[jax.ShapeDtypeStruct((s, XW), F32), jax.ShapeDtypeStruct((s, XW), F32),
                   jax.ShapeDtypeStruct((NMEM, 2 * XW), F32), jax.ShapeDtypeStruct((1, XD), F32),
                   jax.ShapeDtypeStruct((1, XD), F32)],
        compiler_params=_params(("arbitrary",)),
    )(proj, proj, proj, proj, mkv, gxq, gxk, dy)


def _mem_bwd(mem, dmn):
    def body(m_ref, dmn_ref, o_ref):
        m = m_ref[...]
        r = lax.rsqrt(jnp.mean(m * m, axis=-1, keepdims=True) + NORM_EPS)
        o_ref[...] = jnp.sum(dmn_ref[...] * m * r, axis=0, keepdims=True)

    return pl.pallas_call(body, name="mem_norm_bwd", out_shape=jax.ShapeDtypeStruct((1, D), F32),
                          compiler_params=_params())(mem, dmn)


SHIFT_W = 512


def _shift_rows(p, s):
    row = lax.broadcasted_iota(jnp.int32, p.shape, 0)
    prev = jnp.where(row == 0, 0.0, pltpu.roll(p, 1, 0))
    nxt = jnp.where(row == s - 1, 0.0, pltpu.roll(p, s - 1, 0))
    return prev, nxt


def _shift_fwd(proj, mu):
    s = proj.shape[0]

    def body(p_ref, mu_ref, o_ref):
        p = p_ref[...]
        prev, nxt = _shift_rows(p, s)
        o_ref[...] = p + mu_ref[...] * (0.5 * (prev + nxt) - p)

    return pl.pallas_call(
        body, name="shift_fwd", grid=(RSW // SHIFT_W,),
        in_specs=[_bs((s, SHIFT_W), lambda j: (0, C_RS // SHIFT_W + j)), _bs((1, SHIFT_W), lambda j: (0, j))],
        out_specs=_bs((s, SHIFT_W), lambda j: (0, j)),
        out_shape=jax.ShapeDtypeStruct((s, RSW), F32),
        compiler_params=_params(("parallel",)),
    )(proj, mu)


def _shift_bwd(proj, mu, dps):
    s = proj.shape[0]

    def body(p_ref, mu_ref, g_ref, o_ref, dmu_ref):
        p, g, mu_v = p_ref[...], g_ref[...], mu_ref[...]
        prev, nxt = _shift_rows(p, s)
        dmu_ref[...] = jnp.sum(g * (0.5 * (prev + nxt) - p), axis=0, keepdims=True)
        mg = mu_v * g
        down, up = _shift_rows(mg, s)
        o_ref[...] = g * (1.0 - mu_v) + 0.5 * (down + up)

    return pl.pallas_call(
        body, name="shift_bwd", grid=(RSW // SHIFT_W,),
        in_specs=[_bs((s, SHIFT_W), lambda j: (0, C_RS // SHIFT_W + j)), _bs((1, SHIFT_W), lambda j: (0, j)),
                  _bs((s, SHIFT_W), lambda j: (0, j))],
        out_specs=[_bs((s, SHIFT_W), lambda j: (0, j)), _bs((1, SHIFT_W), lambda j: (0, j))],
        out_shape=[jax.ShapeDtypeStruct((s, RSW), F32), jax.ShapeDtypeStruct((1, RSW), F32)],
        compiler_params=_params(("parallel",)),
    )(proj, mu, dps)


def _pre_tile(k, wf, wb, af, ab, k_k, k_a, w0s, w2s, a0s, a2s, seg):
    kx = k * k_k
    ss = _hdot(kx * kx, seg)
    kk = kx / jnp.maximum(jnp.sqrt(ss), 1e-12)
    outs = [kk]
    for d, (w_in, a_in) in enumerate(((wf, af), (wb, ab))):
        z = w0s[d] + _mm_nn(jnp.tanh(w_in), w2s[d])
        wd = -_softplus(-z) - 0.5
        dec = jnp.exp(-jnp.exp(wd))
        ad = jax.nn.sigmoid(a0s[d] + _mm_nn(a_in, a2s[d]))
        kd = k * (1.0 + (ad - 1.0) * k_a)
        outs += [dec, kd, kk * ad]
    return outs


PT = 256


def _pre_load(ps_ref, kk_ref, ka_ref, w0_ref, w2_ref, a0_ref, a2_ref):
    k = ps_ref[:, RW:2 * RW]
    wf, wb = ps_ref[:, 3 * RW:3 * RW + 64], ps_ref[:, 3 * RW + 64:3 * RW + 128]
    af, ab = ps_ref[:, 3 * RW + 128:3 * RW + 192], ps_ref[:, 3 * RW + 192:3 * RW + 256]
    w0s = [w0_ref[0:1, :], w0_ref[1:2, :]]
    a0s = [a0_ref[0:1, :], a0_ref[1:2, :]]
    w2s = [w2_ref[0], w2_ref[1]]
    a2s = [a2_ref[0], a2_ref[1]]
    return (k, wf, wb, af, ab, kk_ref[...], ka_ref[...], w0s, w2s, a0s, a2s)


def _pre_specs():
    c = lambda shape: _bs(shape, lambda i: tuple(0 for _ in shape))
    return [_bs((PT, RSW), lambda i: (i, 0)), c((1, RW)), c((1, RW)), c((2, RW)), c((2, 64, RW)), c((2, RW)),
            c((2, 64, RW))]


def _pre_fwd(ps, k_k, k_a, w0, w2, a0, a2):
    s = ps.shape[0]

    def body(ps_ref, kk_ref, ka_ref, w0_ref, w2_ref, a0_ref, a2_ref, *outs):
        args = _pre_load(ps_ref, kk_ref, ka_ref, w0_ref, w2_ref, a0_ref, a2_ref)
        res = _pre_tile(*args, _seg_matrix(RW, HD))
        for o_ref, v in zip(outs, res):
            o_ref[...] = v

    return pl.pallas_call(
        body, name="rwkv_pre_fwd", grid=(s // PT,),
        in_specs=_pre_specs(), out_specs=[_bs((PT, RW), lambda i: (i, 0))] * 7,
        out_shape=[jax.ShapeDtypeStruct((s, RW), F32)] * 7,
        compiler_params=_params(("parallel",)),
    )(ps, k_k, k_a, w0, w2, a0, a2)


def _pre_bwd(ps, k_k, k_a, w0, w2, a0, a2, dr, dv, cts):
    s = ps.shape[0]

    def body(ps_ref, kk_ref, ka_ref, w0_ref, w2_ref, a0_ref, a2_ref, dr_ref, dv_ref, c0, c1, c2, c3, c4, c5, c6,
             dps_ref, dkk_ref, dka_ref, dw0_ref, dw2_ref, da0_ref, da2_ref):
        @pl.when(pl.program_id(0) == 0)
        def _():
            for r in (dkk_ref, dka_ref, dw0_ref, dw2_ref, da0_ref, da2_ref):
                r[...] = jnp.zeros_like(r)

        args = _pre_load(ps_ref, kk_ref, ka_ref, w0_ref, w2_ref, a0_ref, a2_ref)
        seg = _seg_matrix(RW, HD)
        _, vjp = jax.vjp(lambda *a: _pre_tile(*a, seg), *args)
        dk, dwf, dwb, daf, dab, dk_k, dk_a, dw0s, dw2s, da0s, da2s = vjp([c[...] for c in (c0, c1, c2, c3, c4, c5, c6)])
        dps_ref[:, 0:RW] = dr_ref[...]
        dps_ref[:, RW:2 * RW] = dk
        dps_ref[:, 2 * RW:3 * RW] = dv_ref[...]
        for j, t in enumerate((dwf, dwb, daf, dab)):
            dps_ref[:, 3 * RW + 64 * j:3 * RW + 64 * (j + 1)] = t
        dkk_ref[...] += dk_k
        dka_ref[...] += dk_a
        for d in range(2):
            dw0_ref[d:d + 1, :] += dw0s[d]
            da0_ref[d:d + 1, :] += da0s[d]
            dw2_ref[d] += dw2s[d]
            da2_ref[d] += da2s[d]

    c = lambda shape: _bs(shape, lambda i: tuple(0 for _ in shape))
    row = _bs((PT, RW), lambda i: (i, 0))
    return pl.pallas_call(
        body, name="rwkv_pre_bwd", grid=(s // PT,),
        in_specs=_pre_specs() + [row] * 9,
        out_specs=[_bs((PT, RSW), lambda i: (i, 0)), c((1, RW)), c((1, RW)), c((2, RW)), c((2, 64, RW)), c((2, RW)),
                   c((2, 64, RW))],
        out_shape=[jax.ShapeDtypeStruct((s, RSW), F32), jax.ShapeDtypeStruct((1, RW), F32),
                   jax.ShapeDtypeStruct((1, RW), F32), jax.ShapeDtypeStruct((2, RW), F32),
                   jax.ShapeDtypeStruct((2, 64, RW), F32), jax.ShapeDtypeStruct((2, RW), F32),
                   jax.ShapeDtypeStruct((2, 64, RW), F32)],
        compiler_params=_params(("arbitrary",)),
    )(ps, k_k, k_a, w0, w2, a0, a2, dr, dv, *cts)


def _post_tile(y0, y1, r, v, kd0, kd1, rg, r_k, ln_w, ln_b, seg):
    ysum = y0 + y1
    bonus = (_hdot(r * kd0 * r_k, seg) + _hdot(r * kd1 * r_k, seg)) * v
    mean = _hdot(ysum, seg) * (1.0 / HD)
    cen = ysum - mean
    var = _hdot(cen * cen, seg) * (1.0 / HD)
    y = cen * lax.rsqrt(var + GN_EPS) * ln_w + ln_b + bonus
    return y * _silu(rg)


def _post_specs():
    row = _bs((PT, RW), lambda i: (i, 0))
    c = _bs((1, RW), lambda i: (0, 0))
    return [row, row, _bs((PT, RW), lambda i: (i, 0)), _bs((PT, RW), lambda i: (i, 2)), row, row,
            _bs((PT, RW), lambda i: (i, C_RG // RW)), c, c, c]


def _post_fwd(y0, y1, ps, kd0, kd1, proj, r_k, ln_w, ln_b):
    s = ps.shape[0]

    def body(y0_ref, y1_ref, r_ref, v_ref, kd0_ref, kd1_ref, rg_ref, rk_ref, lw_ref, lb_ref, o_ref, ot_ref):
        y = _post_tile(y0_ref[...], y1_ref[...], r_ref[...], v_ref[...], kd0_ref[...], kd1_ref[...],
                       rg_ref[...], rk_ref[...], lw_ref[...], lb_ref[...], _seg_matrix(RW, HD))
        o_ref[...] = y
        ot_ref[...] = y.T.astype(BF16)

    return pl.pallas_call(
        body, name="rwkv_post_fwd", grid=(s // PT,),
        in_specs=_post_specs(), out_specs=[_bs((PT, RW), lambda i: (i, 0)), _bs((RW, PT), lambda i: (0, i))],
        out_shape=[jax.ShapeDtypeStruct((s, RW), F32), jax.ShapeDtypeStruct((RW, s), BF16)],
        compiler_params=_params(("parallel",)),
    )(y0, y1, ps, ps, kd0, kd1, proj, r_k, ln_w, ln_b)


def _post_bwd(y0, y1, ps, kd0, kd1, proj, r_k, ln_w, ln_b, dy):
    s = ps.shape[0]

    def body(y0_ref, y1_ref, r_ref, v_ref, kd0_ref, kd1_ref, rg_ref, rk_ref, lw_ref, lb_ref, dy_ref,
             dys_ref, dr_ref, dv_ref, dkd0_ref, dkd1_ref, drg_ref, drk_ref, dlw_ref, dlb_ref):
        @pl.when(pl.program_id(0) == 0)
        def _():
            for r in (drk_ref, dlw_ref, dlb_ref):
                r[...] = jnp.zeros_like(r)

        seg = _seg_matrix(RW, HD)
        args = [t[...] for t in (y0_ref, y1_ref, r_ref, v_ref, kd0_ref, kd1_ref, rg_ref, rk_ref, lw_ref, lb_ref)]
        _, vjp = jax.vjp(lambda *a: _post_tile(*a, seg), *args)
        dy0, _, dr, dv, dkd0, dkd1, drg, drk, dlw, dlb = vjp(dy_ref[...])
        dys_ref[...] = dy0
        dr_ref[...] = dr
        dv_ref[...] = dv
        dkd0_ref[...] = dkd0
        dkd1_ref[...] = dkd1
        drg_ref[...] = drg
        drk_ref[...] += drk
        dlw_ref[...] += dlw
        dlb_ref[...] += dlb

    row = _bs((PT, RW), lambda i: (i, 0))
    c = _bs((1, RW), lambda i: (0, 0))
    return pl.pallas_call(
        body, name="rwkv_post_bwd", grid=(s // PT,),
        in_specs=_post_specs() + [row], out_specs=[row] * 6 + [c] * 3,
        out_shape=[jax.ShapeDtypeStruct((s, RW), F32)] * 6 + [jax.ShapeDtypeStruct((1, RW), F32)] * 3,
        compiler_params=_params(("arbitrary",)),
    )(y0, y1, ps, ps, kd0, kd1, proj, r_k, ln_w, ln_b, dy)


def _ones1():
    r = lax.broadcasted_iota(jnp.int32, (128, 128), 0) // HD
    c = lax.broadcasted_iota(jnp.int32, (128, 128), 1) // HD
    return (r == c).astype(BF16)


def _scan_specs(direction, nc, fwd_order):
    def tb(c):
        sc = c if fwd_order else nc - 1 - c
        return sc if direction == 0 else nc - 1 - sc

    row = _bs((TC, RW), lambda c: (tb(c), 0))
    rowv = _bs((TC, RW), lambda c: (tb(c), 2))
    return row, rowv


def _tiles(res, k):
    n = NPAIR * HD
    return [res[k * n + p * HD:k * n + (p + 1) * HD] for p in range(NPAIR)]


def _rows_to_tiles(src_ref, rows8, stage, out_s, base):
    for p in range(NPAIR):
        stage[base + p, 0:8, 0:HD] = src_ref[rows8, p * 128:p * 128 + HD]
        stage[base + p, HD:HD + 8, 0:HD] = src_ref[rows8, p * 128 + HD:(p + 1) * 128]
        out_s[base + p] = stage[base + p].T[0:HD].astype(BF16)


def _tiles_to_rows(tile_s, base, dst_ref, rows8):
    for p in range(NPAIR):
        t = jnp.concatenate([tile_s[base + p], jnp.zeros((HD, 128), F32)], axis=0).T
        dst_ref[rows8, p * 128:p * 128 + HD] = t[0:8, 0:HD]
        dst_ref[rows8, p * 128 + HD:(p + 1) * 128] = t[HD:HD + 8, 0:HD]


def _put_cols(tile_s, base, u, tiles):
    for p in range(NPAIR):
        tile_s[base + p, :, u:u + 1] = tiles[p][:, u:u + 1]
        tile_s[base + p, :, HD + u:HD + u + 1] = tiles[p][:, HD + u:HD + u + 1]


def _scan2_fwd(per_dir, ps, kk, gather=()):
    s = ps.shape[0]
    nc, ng = s // TC, TC // 8
    ngat = len(gather)
    in_specs, operands, out_specs, out_shape = [], [], [], []
    for d in (0, 1):
        row, rowv = _scan_specs(d, nc, True)
        in_specs += [row] * 5 + [rowv]
        operands += list(per_dir[d]) + [ps, kk, ps]
        out_specs += [row, _bs((1, NPAIR, HD, 128), lambda c: (c, 0, 0, 0))]
        out_shape += [jax.ShapeDtypeStruct((s, RW), F32), jax.ShapeDtypeStruct((nc, NPAIR, HD, 128), F32)]
    in_specs += [ANY] * ngat
    operands += list(gather)
    out_specs += [ANY] * ngat
    out_shape += _gather_out_shapes(gather)

    def body(*refs):
        ins = [refs[0:6], refs[6:12]]
        base = 12 + ngat
        y_refs, ck_refs = (refs[base], refs[base + 2]), (refs[base + 1], refs[base + 3])
        st, vt_s, yt_s, stage = refs[base + 4 + ngat:base + 8 + ngat]
        if ngat:
            g_start, g_forward, g_finish = _gather_phases(
                gather, refs[12:base], refs[base + 4:base + 4 + ngat], refs[base + 8 + ngat:])

        @pl.when(pl.program_id(0) == 0)
        def _():
            st[...] = jnp.zeros_like(st)
            yt_s[...] = jnp.zeros_like(yt_s)
            stage[...] = jnp.zeros_like(stage)
            if ngat:
                g_start()

        if ngat:
            @pl.when(pl.program_id(0) == nc // 2)
            def _():
                g_forward()

        for d in (0, 1):
            ck_refs[d][0] = st[d * NPAIR:(d + 1) * NPAIR]
        ones1 = _ones1()
        lane_u = lax.broadcasted_iota(jnp.int32, (HD, 128), 1) % HD
        pc = [slice(p * 128, (p + 1) * 128) for p in range(NPAIR)]

        def group(gi, carry):
            gs = (gi, ng - 1 - gi)
            rows8 = [pl.ds(pl.multiple_of(gs[d] * 8, 8), 8) for d in (0, 1)]
            blk = [[q[rows8[d], :] for q in ins[d][:5]] for d in (0, 1)]
            for d in (0, 1):
                _rows_to_tiles(ins[d][5], rows8[d], stage, vt_s, d * NPAIR)
            ss = [[st[d * NPAIR + p] for p in range(NPAIR)] for d in (0, 1)]
            for ui in range(9):
                us, ups = (ui, 7 - ui), (ui - 1, 8 - ui)
                lhs1, where = [], {}
                for d in (0, 1):
                    if ui < 8:
                        where["sa", d] = len(lhs1) // NPAIR
                        lhs1 += [(ss[d][p] * blk[d][4][us[d]:us[d] + 1, pc[p]]).astype(BF16) for p in range(NPAIR)]
                        where["vb", d] = len(lhs1) // NPAIR
                        for p in range(NPAIR):
                            vt = vt_s[d * NPAIR + p]
                            lhs1.append(jnp.where(lane_u == us[d], vt, jnp.zeros_like(vt)))
                    if ui > 0:
                        where["y", d] = len(lhs1) // NPAIR
                        lhs1 += [(ss[d][p] * blk[d][3][ups[d]:ups[d] + 1, pc[p]]).astype(BF16) for p in range(NPAIR)]
                res1 = jnp.dot(jnp.concatenate(lhs1, axis=0), ones1, preferred_element_type=F32)
                for d in (0, 1):
                    d8, k8, b8, _, _ = blk[d]
                    u = us[d]
                    if ui < 8:
                        sa, vb = _tiles(res1, where["sa", d]), _tiles(res1, where["vb", d])
                        for p in range(NPAIR):
                            ss[d][p] = (ss[d][p] * d8[u:u + 1, pc[p]] - sa[p] * b8[u:u + 1, pc[p]]
                                        + vb[p] * k8[u:u + 1, pc[p]])
                    if ui > 0:
                        _put_cols(yt_s, d * NPAIR, ups[d], _tiles(res1, where["y", d]))
            for d in (0, 1):
                _tiles_to_rows(yt_s, d * NPAIR, y_refs[d], rows8[d])
                for p in range(NPAIR):
                    st[d * NPAIR + p] = ss[d][p]
            return carry

        for gi in range(ng):
            group(gi, 0)

        if ngat:
            @pl.when(pl.program_id(0) == nc - 1)
            def _():
                g_finish()

    outs = pl.pallas_call(
        body, name="rwkv_scan_fwd", grid=(nc,), in_specs=in_specs, out_specs=out_specs, out_shape=out_shape,
        scratch_shapes=[pltpu.VMEM((2 * NPAIR, HD, 128), F32), pltpu.VMEM((2 * NPAIR, HD, 128), BF16),
                        pltpu.VMEM((2 * NPAIR, HD, 128), F32), pltpu.VMEM((2 * NPAIR, 128, 128), F32)]
        + (_gather_sems(ngat) if ngat else []),
        compiler_params=pltpu.CompilerParams(dimension_semantics=("arbitrary",), vmem_limit_bytes=VMEM_LIMIT,
                                             has_side_effects=bool(ngat)),
    )(*operands)
    return [(outs[0], outs[1]), (outs[2], outs[3])], list(outs[4:])


def _scan2_bwd(per_dir, ps, kk, dy, scatter=()):
    s = ps.shape[0]
    nc, ng = s // TC, TC // 8
    nsc = len(scatter)
    in_specs, operands, out_specs, out_shape = [], [], [], []
    for d in (0, 1):
        row, rowv = _scan_specs(d, nc, False)
        dec, kd, b, ck = per_dir[d]
        in_specs += [row] * 5 + [rowv, row, _bs((1, NPAIR, HD, 128), lambda c: (nc - 1 - c, 0, 0, 0))]
        operands += [dec, kd, b, ps, kk, ps, dy, ck]
        out_specs += [row] * 6
        out_shape += [jax.ShapeDtypeStruct((s, RW), F32)] * 6
    in_specs += [ANY] * nsc
    operands += list(scatter)
    out_specs += [ANY] * nsc
    out_shape += _scatter_out_shapes(scatter)

    def body(*refs):
        ins = [refs[0:8], refs[8:16]]
        base = 16 + nsc
        outs = [refs[base:base + 6], refs[base + 6:base + 12]]
        st, sa_s, vb_s, dy_s, ds, vt_s, dyt_s, dvt_s, stage = refs[base + 12 + nsc:base + 21 + nsc]
        if nsc:
            s_start, s_finish = _scatter_phases(refs[16:base], refs[base + 12:base + 12 + nsc], refs[base + 21 + nsc:])

        @pl.when(pl.program_id(0) == 0)
        def _():
            dvt_s[...] = jnp.zeros_like(dvt_s)
            stage[...] = jnp.zeros_like(stage)
            ds[...] = jnp.zeros_like(ds)
            if nsc:
                s_start()

        for d in (0, 1):
            st[d * (TC + 1)] = ins[d][7][0]
        ones1 = _ones1()
        lane_u = lax.broadcasted_iota(jnp.int32, (HD, 128), 1) % HD
        row_id = lax.broadcasted_iota(jnp.int32, (8, 128), 0)
        pc = [slice(p * 128, (p + 1) * 128) for p in range(NPAIR)]

        def load_rows(gs):
            return [[q[pl.ds(pl.multiple_of(gs[d] * 8, 8), 8), :] for q in ins[d][:5]] for d in (0, 1)]

        def fgroup(gi, carry):
            gs = (gi, ng - 1 - gi)
            blk = load_rows(gs)
            for d in (0, 1):
                rows8 = pl.ds(pl.multiple_of(gs[d] * 8, 8), 8)
                _rows_to_tiles(ins[d][5], rows8, stage, vt_s, d * NPAIR)
                _rows_to_tiles(ins[d][6], rows8, stage, dyt_s, d * NPAIR)
            ss = [[st[d * (TC + 1) + gi * 8, p] for p in range(NPAIR)] for d in (0, 1)]
            for ui in range(8):
                us = (ui, 7 - ui)
                i = gi * 8 + ui
                lhs1 = []
                for d in (0, 1):
                    kk8 = blk[d][4]
                    lhs1 += [(ss[d][p] * kk8[us[d]:us[d] + 1, pc[p]]).astype(BF16) for p in range(NPAIR)]
                    for tile_s in (vt_s, dyt_s):
                        for p in range(NPAIR):
                            t = tile_s[d * NPAIR + p]
                            lhs1.append(jnp.where(lane_u == us[d], t, jnp.zeros_like(t)))
                res1 = jnp.dot(jnp.concatenate(lhs1, axis=0), ones1, preferred_element_type=F32)
                for d in (0, 1):
                    d8, k8, b8, _, _ = blk[d]
                    u = us[d]
                    sa, vb, dyb = _tiles(res1, 3 * d), _tiles(res1, 3 * d + 1), _tiles(res1, 3 * d + 2)
                    for p in range(NPAIR):
                        sa_s[d * TC + i, p] = sa[p]
                        vb_s[d * TC + i, p] = vb[p]
                        dy_s[d * TC + i, p] = dyb[p]
                        ss[d][p] = ss[d][p] * d8[u:u + 1, pc[p]] - sa[p] * b8[u:u + 1, pc[p]] + vb[p] * k8[u:u + 1, pc[p]]
                        st[d * (TC + 1) + i + 1, p] = ss[d][p]
            return carry

        for gi in range(ng):
            fgroup(gi, 0)

        def bgroup(gj, carry):
            gi = ng - 1 - gj
            gs = (gi, ng - 1 - gi)
            blk = load_rows(gs)
            dss = [[ds[d * NPAIR + p] for p in range(NPAIR)] for d in (0, 1)]
            acc = [[[jnp.zeros((8, 128), F32) for _ in range(5)] for _ in range(NPAIR)] for _ in (0, 1)]
            for uj in range(8):
                ui = 7 - uj
                us = (ui, 7 - ui)
                i = gi * 8 + ui
                lhs1, dyb = [], [None, None]
                for d in (0, 1):
                    _, k8, b8, r8, _ = blk[d]
                    u = us[d]
                    dyb[d] = [dy_s[d * TC + i, p] for p in range(NPAIR)]
                    for p in range(NPAIR):
                        dss[d][p] = dss[d][p] + dyb[d][p] * r8[u:u + 1, pc[p]]
                    lhs1 += [(dss[d][p] * b8[u:u + 1, pc[p]]).astype(BF16) for p in range(NPAIR)]
                    lhs1 += [(dss[d][p] * k8[u:u + 1, pc[p]]).astype(BF16) for p in range(NPAIR)]
                res1 = jnp.dot(jnp.concatenate(lhs1, axis=0), ones1, preferred_element_type=F32)
                for d in (0, 1):
                    d8, _, _, _, kk8 = blk[d]
                    u = us[d]
                    dsa, dvb = _tiles(res1, 2 * d), _tiles(res1, 2 * d + 1)
                    _put_cols(dvt_s, d * NPAIR, u, dvb)
                    for p in range(NPAIR):
                        sp, sn = st[d * (TC + 1) + i, p], st[d * (TC + 1) + i + 1, p]
                        dsv = dss[d][p]
                        vals = (jnp.sum(sn * dyb[d][p], axis=0, keepdims=True), jnp.sum(dsv * sp, axis=0, keepdims=True),
                                -jnp.sum(dsv * sa_s[d * TC + i, p], axis=0, keepdims=True),
                                jnp.sum(dsv * vb_s[d * TC + i, p], axis=0, keepdims=True),
                                -jnp.sum(sp * dsa[p], axis=0, keepdims=True))
                        acc[d][p] = [jnp.where(row_id == u, o, a_) for o, a_ in zip(vals, acc[d][p])]
                        dss[d][p] = dsv * d8[u:u + 1, pc[p]] - dsa[p] * kk8[u:u + 1, pc[p]]
            for d in (0, 1):
                rows8 = pl.ds(pl.multiple_of(gs[d] * 8, 8), 8)
                _tiles_to_rows(dvt_s, d * NPAIR, outs[d][5], rows8)
                for p in range(NPAIR):
                    ds[d * NPAIR + p] = dss[d][p]
                    for o_ref, a_ in zip(outs[d][:5], acc[d][p]):
                        o_ref[rows8, pc[p]] = a_
            return carry

        for gj in range(ng):
            bgroup(gj, 0)

        if nsc:
            @pl.when(pl.program_id(0) == nc - 1)
            def _():
                s_finish()

    chunk = lambda k: pltpu.VMEM((k, NPAIR, HD, 128), F32)
    pairs = lambda w, dt: pltpu.VMEM((2 * NPAIR, HD, w), dt)
    res = pl.pallas_call(
        body, name="rwkv_scan_bwd", grid=(nc,), in_specs=in_specs, out_specs=out_specs, out_shape=out_shape,
        scratch_shapes=[chunk(2 * (TC + 1)), chunk(2 * TC), chunk(2 * TC), chunk(2 * TC), pairs(128, F32),
                        pairs(128, BF16), pairs(128, BF16), pairs(128, F32), pltpu.VMEM((2 * NPAIR, 128, 128), F32)]
        + _scatter_sems(nsc),
        compiler_params=pltpu.CompilerParams(dimension_semantics=("arbitrary",), vmem_limit_bytes=VMEM_LIMIT,
                                             has_side_effects=bool(nsc)),
    )(*operands)
    return [res[0:6], res[6:12]], list(res[12:])


MT = 512
MN = 256


def _merge_fwd(ya, yr, yx, wa, wr, wx, proj, gate_b):
    s = ya.shape[0]

    def body(ya_ref, yr_ref, yx_ref, wa_ref, wr_ref, wx_ref, m0, m1, m2, b0, b1, b2, o_ref, ot_ref):
        acc = jnp.zeros((MT, MN), F32)
        for y_ref, w_ref, m_ref, b_ref in ((ya_ref, wa_ref, m0, b0), (yr_ref, wr_ref, m1, b1), (yx_ref, wx_ref, m2, b2)):
            u = _dot(y_ref[...], w_ref[...], ((1,), (0,)))
            acc = acc + jax.nn.sigmoid(m_ref[...] + b_ref[...]) * u
        o_ref[...] = acc.astype(BF16)
        ot_ref[...] = acc.T.astype(BF16)

    mg = lambda br: _bs((MT, MN), lambda i, j: (i, C_MG // MN + br * (D // MN) + j))
    gb = lambda br: _bs((1, MN), lambda i, j: (0, br * (D // MN) + j))
    return pl.pallas_call(
        body, name="merge_fwd", grid=(s // MT, D // MN),
        in_specs=[_bs((MT, RW), lambda i, j: (i, 0)), _bs((MT, RW), lambda i, j: (i, 0)), _bs((MT, XW), lambda i, j: (i, 0)),
                  _bs((RW, MN), lambda i, j: (0, j)), _bs((RW, MN), lambda i, j: (0, j)), _bs((XW, MN), lambda i, j: (0, j)),
                  mg(0), mg(1), mg(2), gb(0), gb(1), gb(2)],
        out_specs=[_bs((MT, MN), lambda i, j: (i, j)), _bs((MN, MT), lambda i, j: (j, i))],
        out_shape=[jax.ShapeDtypeStruct((s, D), BF16), jax.ShapeDtypeStruct((D, s), BF16)],
        compiler_params=_params(("parallel", "arbitrary")),
    )(ya, yr, yx, wa, wr, wx, proj, proj, proj, gate_b, gate_b, gate_b)


def _out_fwd(merged, w_out, x, target):
    s = x.shape[0]
    tm, tn = min(1024, s), 512

    def body(m_ref, w_ref, x_ref, t_ref, loss_ref, d_ref, d16_ref):
        @pl.when((pl.program_id(0) == 0) & (pl.program_id(1) == 0))
        def _():
            loss_ref[...] = jnp.zeros_like(loss_ref)

        out = x_ref[...] + jnp.dot(m_ref[...], w_ref[...], preferred_element_type=F32)
        err = out - t_ref[...]
        dout = err * (1.0 / D)
        d_ref[...] = dout
        d16_ref[...] = dout.astype(BF16)
        loss_ref[...] += jnp.sum(err * err)

    tile = _bs((tm, tn), lambda i, j: (i, j))
    return pl.pallas_call(
        body, name="out_fwd", grid=(s // tm, D // tn),
        in_specs=[_bs((tm, D), lambda i, j: (i, 0)), _bs((D, tn), lambda i, j: (0, j)), tile, tile],
        out_specs=[_bs((8, 128), lambda i, j: (0, 0)), tile, tile],
        out_shape=[jax.ShapeDtypeStruct((8, 128), F32), jax.ShapeDtypeStruct((s, D), F32),
                   jax.ShapeDtypeStruct((s, D), BF16)],
        compiler_params=_params(("arbitrary", "arbitrary")),
    )(merged, w_out, x, target)


def _merge_bwd(ya, yr, yx, wa, wr, wx, proj, gate_b, dmerged):
    s = ya.shape[0]

    def body(ya_ref, yr_ref, yx_ref, wa_ref, wr_ref, wx_ref, m0, m1, m2, b0, b1, b2, dm_ref,
             dg0, dg1, dg2, du0, du1, du2, dya_ref, dyr_ref, dyx_ref):
        @pl.when(pl.program_id(1) == 0)
        def _():
            dya_ref[...] = jnp.zeros_like(dya_ref)
            dyr_ref[...] = jnp.zeros_like(dyr_ref)
            dyx_ref[...] = jnp.zeros_like(dyx_ref)

        dm = dm_ref[...]
        branches = ((ya_ref, wa_ref, m0, b0, dg0, du0, dya_ref), (yr_ref, wr_ref, m1, b1, dg1, du1, dyr_ref),
                    (yx_ref, wx_ref, m2, b2, dg2, du2, dyx_ref))
        ws = [br[1][...] for br in branches]
        us = [_dot(br[0][...], w, ((1,), (0,))) for br, w in zip(branches, ws)]
        gts = [jax.nn.sigmoid(br[2][...] + br[3][...]) for br in branches]
        dus = [(dm * gt).astype(BF16) for gt in gts]
        for br, w, u, gt, du in zip(branches, ws, us, gts, dus):
            br[4][...] = (dm * u * gt * (1.0 - gt)).astype(BF16)
            br[5][...] = du
            br[6][...] += _dot(du, w, ((1,), (1,)))

    mg = lambda br: _bs((MT, MN), lambda i, j: (i, C_MG // MN + br * (D // MN) + j))
    gb = lambda br: _bs((1, MN), lambda i, j: (0, br * (D // MN) + j))
    tile = _bs((MT, MN), lambda i, j: (i, j))
    return pl.pallas_call(
        body, name="merge_bwd", grid=(s // MT, D // MN),
        in_specs=[_bs((MT, RW), lambda i, j: (i, 0)), _bs((MT, RW), lambda i, j: (i, 0)), _bs((MT, XW), lambda i, j: (i, 0)),
                  _bs((RW, MN), lambda i, j: (0, j)), _bs((RW, MN), lambda i, j: (0, j)), _bs((XW, MN), lambda i, j: (0, j)),
                  mg(0), mg(1), mg(2), gb(0), gb(1), gb(2), tile],
        out_specs=[tile] * 6 + [_bs((MT, RW), lambda i, j: (i, 0)), _bs((MT, RW), lambda i, j: (i, 0)),
                                _bs((MT, XW), lambda i, j: (i, 0))],
        out_shape=[jax.ShapeDtypeStruct((s, D), BF16)] * 6 + [jax.ShapeDtypeStruct((s, RW), F32),
                                                               jax.ShapeDtypeStruct((s, RW), F32),
                                                               jax.ShapeDtypeStruct((s, XW), F32)],
        compiler_params=_params(("parallel", "arbitrary")),
    )(ya, yr, yx, wa, wr, wx, proj, proj, proj, gate_b, gate_b, gate_b, dmerged)


def _colsum(a, name):
    m, n = a.shape
    tm, tn = min(2048, m), 512

    def body(a_ref, o_ref):
        @pl.when(pl.program_id(1) == 0)
        def _():
            o_ref[...] = jnp.zeros_like(o_ref)

        o_ref[...] += jnp.sum(a_ref[...].astype(F32), axis=0, keepdims=True)

    return pl.pallas_call(
        body, name=name, grid=(n // tn, m // tm),
        in_specs=[_bs((tm, tn), lambda j, i: (i, j))], out_specs=_bs((1, tn), lambda j, i: (0, j)),
        out_shape=jax.ShapeDtypeStruct((1, n), F32),
        compiler_params=_params(("parallel", "arbitrary")),
    )(a)


def _in_bwd(dproj, w_in, x, g, dout, stacks=()):
    s = x.shape[0]
    tm, tk = min(512, s), 896
    nk = NIN // tk
    ni = s // tm
    n = len(stacks)

    def body(dp_ref, w_ref, x_ref, g_ref, do_ref, *rest):
        ins, (gx_ref, gg_ref), outs = rest[:n], rest[n:n + 2], rest[n + 2:2 * n + 2]
        acc = rest[2 * n + 2]
        i, kk = pl.program_id(0), pl.program_id(1)

        if n:
            start, finish = _scatter_phases(ins, outs, rest[2 * n + 3:])

        @pl.when((i == 0) & (kk == 0))
        def _():
            gg_ref[...] = jnp.zeros_like(gg_ref)
            if n:
                start()

        @pl.when(kk == 0)
        def _():
            acc[...] = jnp.zeros_like(acc)

        acc[...] += _dot(dp_ref[...], w_ref[...], ((1,), (1,)))

        @pl.when(kk == nk - 1)
        def _():
            xv, dh, gv = x_ref[...], acc[...], g_ref[...]
            r = lax.rsqrt(jnp.mean(xv * xv, axis=-1, keepdims=True) + NORM_EPS)
            xn = xv * r
            gg_ref[...] += jnp.sum(dh * xn, axis=0, keepdims=True)
            dxn = dh * gv
            dx = r * (dxn - xn * jnp.mean(dxn * xn, axis=-1, keepdims=True))
            gx_ref[...] = do_ref[...] + dx

        if n:
            @pl.when((i == ni - 1) & (kk == nk - 1))
            def _():
                finish()

    any_spec = pl.BlockSpec(memory_space=pl.ANY)
    res = pl.pallas_call(
        body, name="in_bwd", grid=(ni, nk),
        in_specs=[_bs((tm, tk), lambda i, kk: (i, kk)), _bs((D, tk), lambda i, kk: (0, kk)),
                  _bs((tm, D), lambda i, kk: (i, 0)), _bs((1, D), lambda i, kk: (0, 0)),
                  _bs((tm, D), lambda i, kk: (i, 0))] + [any_spec] * n,
        out_specs=[_bs((tm, D), lambda i, kk: (i, 0)), _bs((1, D), lambda i, kk: (0, 0))] + [any_spec] * n,
        out_shape=[jax.ShapeDtypeStruct((s, D), F32), jax.ShapeDtypeStruct((1, D), F32)] + _scatter_out_shapes(stacks),
        scratch_shapes=[pltpu.VMEM((tm, D), F32)] + _scatter_sems(n),
        compiler_params=pltpu.CompilerParams(dimension_semantics=("arbitrary", "arbitrary"),
                                             vmem_limit_bytes=VMEM_LIMIT, has_side_effects=bool(n)),
    )(dproj, w_in, x, g, dout, *stacks)
    return res[0], res[1], list(res[2:])


def _adamw_math(w, g, m, v):
    m = ADAM_B1 * m + (1.0 - ADAM_B1) * g
    v = ADAM_B2 * v + (1.0 - ADAM_B2) * jnp.square(g)
    m_hat = m / (1.0 - ADAM_B1 ** ADAM_STEP)
    v_hat = v / (1.0 - ADAM_B2 ** ADAM_STEP)
    delta = -ADAM_LR * (m_hat / (jnp.sqrt(v_hat) + ADAM_EPS) + ADAM_WD * w)
    return delta, m, v


def _adamw(parts, w, m, v, name):
    rows, cols = w.shape
    tr = rows
    for cand in (256, 128, 64, 32, 16, 8):
        if rows % cand == 0 and cand * cols * 4 <= (1 << 20):
            tr = cand
            break
    n = len(parts)

    def body(*refs):
        g = refs[0][...].astype(F32)
        for r in refs[1:n]:
            g = g + r[...].astype(F32)
        w_ref, m_ref, v_ref, g_out, d_out, m_out, v_out = refs[n:]
        delta, m_new, v_new = _adamw_math(w_ref[...], g, m_ref[...], v_ref[...])
        g_out[...] = g
        d_out[...] = delta
        m_out[...] = m_new
        v_out[...] = v_new

    spec = _bs((tr, cols), lambda i: (i, 0))
    return pl.pallas_call(
        body, name=name, grid=(rows // tr,),
        in_specs=[spec] * (n + 3), out_specs=[spec] * 4,
        out_shape=[jax.ShapeDtypeStruct((rows, cols), F32)] * 4,
        compiler_params=_params(("parallel",)),
    )(*parts, w, m, v)


def _adamw_halves(mine, theirs, core, w, m, v, name):
    rows, cols = w.shape
    h = rows // 2
    tr = next(t for t in (256, 128, 64, 32, 16, 8) if h % t == 0 and t * cols * 4 <= (1 << 20))
    nt = h // tr

    def body(core_ref, mine_ref, theirs_ref, w_ref, m_ref, v_ref, g_out, d_out, m_out, v_out):
        is_mine = pl.program_id(0) // nt == core_ref[0]
        g = jnp.where(is_mine, mine_ref[...], theirs_ref[...])
        delta, m_new, v_new = _adamw_math(w_ref[...], g, m_ref[...], v_ref[...])
        g_out[...] = g
        d_out[...] = delta
        m_out[...] = m_new
        v_out[...] = v_new

    spec = _bs((tr, cols), lambda i, core_ref: (i, 0))
    return pl.pallas_call(
        body, name=name,
        grid_spec=pltpu.PrefetchScalarGridSpec(
            num_scalar_prefetch=1, grid=(2 * nt,),
            in_specs=[_bs((tr, cols), lambda i, core_ref: (jnp.clip(i - core_ref[0] * nt, 0, nt - 1), 0)),
                      _bs((tr, cols), lambda i, core_ref: (jnp.clip(i - (1 - core_ref[0]) * nt, 0, nt - 1), 0)),
                      spec, spec, spec],
            out_specs=[spec] * 4),
        out_shape=[jax.ShapeDtypeStruct((rows, cols), F32)] * 4,
        compiler_params=_params(("parallel",)),
    )(core, mine, theirs, w, m, v)


def _sum_parts(parts, name):
    rows, cols = parts[0].shape
    tr = rows
    for cand in (256, 128, 64, 32, 16, 8):
        if rows % cand == 0 and cand * cols * 4 <= (1 << 20):
            tr = cand
            break

    def body(*refs):
        acc = refs[0][...].astype(F32)
        for r in refs[1:-1]:
            acc = acc + r[...].astype(F32)
        refs[-1][...] = acc

    spec = _bs((tr, cols), lambda i: (i, 0))
    return pl.pallas_call(
        body, name=name, grid=(rows // tr,), in_specs=[spec] * len(parts), out_specs=spec,
        out_shape=jax.ShapeDtypeStruct((rows, cols), F32), compiler_params=_params(("parallel",)),
    )(*parts)


ANY = pl.BlockSpec(memory_space=pl.ANY)


def _other_chips(x, y):
    return [(1 - x, y), (x, 1 - y), (1 - x, 1 - y)]


def _gather_shards(arrays, name):
    n = len(arrays)

    def body(*refs):
        start, forward, finish = _gather_phases(arrays, refs[:n], refs[n:2 * n], refs[2 * n:])
        start()
        forward()
        finish()

    return pl.pallas_call(
        body, name=name, in_specs=[ANY] * n, out_specs=[ANY] * n,
        out_shape=_gather_out_shapes(arrays), scratch_shapes=_gather_sems(n),
        compiler_params=pltpu.CompilerParams(has_side_effects=True),
    )(*arrays)


def _gather_out_shapes(arrays):
    return [jax.ShapeDtypeStruct((4,) + a.shape, a.dtype) for a in arrays]


def _gather_sems(n):
    dma = lambda k: pltpu.SemaphoreType.DMA((k,))
    return [dma(3 * n), dma(3 * n), dma(3 * n), dma(3 * n), dma(n), dma(n)]


def _gather_phases(arrays, ins, outs, sems):
    n = len(arrays)
    ici_send, ici_recv, d2d_send, d2d_recv, own_send, own_recv = sems

    def place():
        x, y, c = lax.axis_index("x"), lax.axis_index("y"), lax.axis_index("c")
        return x, y, c, 2 * x + y, _other_chips(x, y)

    def half(i, who):
        h = arrays[i].shape[0] // 2
        return pl.ds(who * h, h)

    def ici(i, j, src_chip, to, c):
        return pltpu.make_async_remote_copy(
            src_ref=ins[i].at[half(i, c)], dst_ref=outs[i].at[src_chip, half(i, c)], send_sem=ici_send.at[3 * i + j],
            recv_sem=ici_recv.at[3 * i + j], device_id=to, device_id_type=MESH)

    def d2d(i, j, src_chip, who, sib):
        piece = outs[i].at[src_chip, half(i, who)]
        return pltpu.make_async_remote_copy(
            src_ref=piece, dst_ref=piece, send_sem=d2d_send.at[3 * i + j], recv_sem=d2d_recv.at[3 * i + j],
            device_id=sib, device_id_type=MESH)

    def own(i, me, sib):
        return pltpu.make_async_remote_copy(
            src_ref=ins[i], dst_ref=outs[i].at[me], send_sem=own_send.at[i], recv_sem=own_recv.at[i],
            device_id=sib, device_id_type=MESH)

    def start():
        x, y, c, me, chips = place()
        for i in range(n):
            own(i, me, (x, y, 1 - c)).start()
            for j, (px, py) in enumerate(chips):
                ici(i, j, me, (px, py, c), c).start()

    def forward():
        x, y, c, me, chips = place()
        for i in range(n):
            for j, (px, py) in enumerate(chips):
                ici(i, j, 2 * px + py, (px, py, c), c).wait_recv()
                d2d(i, j, 2 * px + py, c, (x, y, 1 - c)).start()

    def finish():
        x, y, c, me, chips = place()
        sib = (x, y, 1 - c)
        for i in range(n):
            for j, (px, py) in enumerate(chips):
                d2d(i, j, 2 * px + py, 1 - c, sib).wait_recv()
            own(i, me, sib).wait_recv()
        for i in range(n):
            own(i, me, sib).wait_send()
            for j, (px, py) in enumerate(chips):
                ici(i, j, me, (px, py, c), c).wait_send()
                d2d(i, j, 2 * px + py, c, sib).wait_send()

    return start, forward, finish


def _scatter_phases(ins, outs, sems):
    send_sems, recv_sems = sems

    def copies():
        x, y, c = lax.axis_index("x"), lax.axis_index("y"), lax.axis_index("c")
        return [pltpu.make_async_remote_copy(
            src_ref=ins[a].at[2 * qx + qy], dst_ref=outs[a].at[j], send_sem=send_sems.at[3 * a + j],
            recv_sem=recv_sems.at[3 * a + j], device_id=(qx, qy, c), device_id_type=MESH)
            for a in range(len(ins)) for j, (qx, qy) in enumerate(_other_chips(x, y))]

    def start():
        for rc in copies():
            rc.start()

    def finish():
        for rc in copies():
            rc.wait_recv()
        for rc in copies():
            rc.wait_send()

    return start, finish


def _scatter_out_shapes(stacks):
    return [jax.ShapeDtypeStruct((3,) + a.shape[1:], a.dtype) for a in stacks]


def _scatter_sems(n):
    return [pltpu.SemaphoreType.DMA((3 * n,)), pltpu.SemaphoreType.DMA((3 * n,))] if n else []


def _pair_exchange(stacks, name):
    n = len(stacks)

    def body(*refs):
        ins, outs = refs[:n], refs[n:2 * n]
        send_sems, recv_sems = refs[2 * n:]
        x, y, c = lax.axis_index("x"), lax.axis_index("y"), lax.axis_index("c")
        cps = []
        for i in range(n):
            h = stacks[i].shape[1] // 2
            rc = pltpu.make_async_remote_copy(
                src_ref=ins[i].at[:, pl.ds((1 - c) * h, h)], dst_ref=outs[i], send_sem=send_sems.at[i],
                recv_sem=recv_sems.at[i], device_id=(x, y, 1 - c), device_id_type=MESH)
            rc.start()
            cps.append(rc)
        for rc in cps:
            rc.wait_recv()
        for rc in cps:
            rc.wait_send()

    return pl.pallas_call(
        body, name=name, in_specs=[ANY] * n, out_specs=[ANY] * n,
        out_shape=[jax.ShapeDtypeStruct((4, a.shape[1] // 2) + a.shape[2:], a.dtype) for a in stacks],
        scratch_shapes=[pltpu.SemaphoreType.DMA((n,)), pltpu.SemaphoreType.DMA((n,))],
        compiler_params=pltpu.CompilerParams(has_side_effects=True),
    )(*stacks)


def _pair_sum(own, theirs, core, name):
    _, r, cols = own.shape
    h = r // 2
    tr = next(t for t in (256, 128, 64, 32, 16) if h % t == 0 and t * cols * 4 <= (1 << 20))
    nt = h // tr

    def body(core_ref, own_ref, th_ref, o32_ref, o16_ref):
        del core_ref
        acc = own_ref[...] + th_ref[...].astype(F32)
        o32_ref[...] = acc
        o16_ref[...] = acc.astype(BF16)

    out = _bs((1, tr, cols), lambda j, t, core_ref: (j, t, 0))
    return pl.pallas_call(
        body, name=name,
        grid_spec=pltpu.PrefetchScalarGridSpec(
            num_scalar_prefetch=1, grid=(4, nt),
            in_specs=[_bs((1, tr, cols), lambda j, t, core_ref: (j, core_ref[0] * nt + t, 0)), out],
            out_specs=[out, out]),
        out_shape=[jax.ShapeDtypeStruct((4, h, cols), F32), jax.ShapeDtypeStruct((4, h, cols), BF16)],
        compiler_params=_params(("parallel", "parallel")),
    )(core, own, theirs)


def _swap_sibling(arrays, name):
    n = len(arrays)

    def body(*refs):
        ins, outs = refs[:n], refs[n:2 * n]
        send_sems, recv_sems = refs[2 * n:]
        sib = (lax.axis_index("x"), lax.axis_index("y"), 1 - lax.axis_index("c"))
        cps = []
        for i in range(n):
            rc = pltpu.make_async_remote_copy(src_ref=ins[i], dst_ref=outs[i], send_sem=send_sems.at[i],
                                              recv_sem=recv_sems.at[i], device_id=sib, device_id_type=MESH)
            rc.start()
            cps.append(rc)
        for rc in cps:
            rc.wait_recv()
        for rc in cps:
            rc.wait_send()

    return pl.pallas_call(
        body, name=name, in_specs=[ANY] * n, out_specs=[ANY] * n,
        out_shape=[jax.ShapeDtypeStruct(a.shape, a.dtype) for a in arrays],
        scratch_shapes=[pltpu.SemaphoreType.DMA((n,)), pltpu.SemaphoreType.DMA((n,))],
        compiler_params=pltpu.CompilerParams(has_side_effects=True),
    )(*arrays)


def _all_reduce_small(v):
    rows = v.shape[0]
    h = rows // 2

    def body(v_ref, o_ref, sib, pair, buf, send_sems, recv_sems):
        x, y, c = lax.axis_index("x"), lax.axis_index("y"), lax.axis_index("c")
        me = 2 * x + y
        sibling = (x, y, 1 - c)
        mine = pl.ds(pl.multiple_of(c * h, 8), h)
        theirs = pl.ds(pl.multiple_of((1 - c) * h, 8), h)

        def copy(src, dst, k, to):
            return pltpu.make_async_remote_copy(src_ref=src, dst_ref=dst, send_sem=send_sems.at[k],
                                                recv_sem=recv_sems.at[k], device_id=to, device_id_type=MESH)

        swap = copy(v_ref, sib, 0, sibling)
        swap.start()
        swap.wait_recv()
        pair[...] = v_ref[...] + sib[...]
        buf[me] = pair[mine, :]
        out = [copy(buf.at[me], buf.at[me], 1 + j, (px, py, c)) for j, (px, py) in enumerate(_other_chips(x, y))]
        for rc in out:
            rc.start()
        for j, (px, py) in enumerate(_other_chips(x, y)):
            copy(buf.at[me], buf.at[2 * px + py], 1 + j, (px, py, c)).wait_recv()
        o_ref[mine, :] = (buf[0] + buf[1]) + (buf[2] + buf[3])
        back = copy(o_ref.at[mine], o_ref.at[mine], 4, sibling)
        back.start()
        copy(o_ref.at[theirs], o_ref.at[theirs], 4, sibling).wait_recv()
        for rc in [swap, back] + out:
            rc.wait_send()

    vmem = pl.BlockSpec(memory_space=pltpu.VMEM)
    return pl.pallas_call(
        body, name="all_reduce_small", in_specs=[vmem], out_specs=vmem,
        out_shape=jax.ShapeDtypeStruct((rows, 128), F32),
        scratch_shapes=[pltpu.VMEM((rows, 128), F32), pltpu.VMEM((rows, 128), F32), pltpu.VMEM((4, h, 128), F32),
                        pltpu.SemaphoreType.DMA((5,)), pltpu.SemaphoreType.DMA((5,))],
        compiler_params=pltpu.CompilerParams(has_side_effects=True, vmem_limit_bytes=VMEM_LIMIT),
    )(v)


def _rope_tables(s):
    half = HD // 2
    inv = 10000.0 ** (-jnp.arange(half, dtype=F32) / half)
    ang = jnp.arange(s, dtype=F32)[:, None] * inv[None, :]
    cos, sin = jnp.cos(ang), jnp.sin(ang)
    return jnp.concatenate([cos, cos], axis=1), jnp.concatenate([sin, sin], axis=1)


LATE = ['attn_w_o', 'rwkv_w_o', 'x_w_kv', 'x_w_o', 'w_out']


def _local_step(x, mem, target, norm_g, mem_norm_g, w_in, gate_b, gq, gk, sink, wa, mu, k_k, k_a, r_k, w0, w2, a0, a2,
                ln_w, ln_b, wr, w_kv, gxq, gxk, wx, w_out, late_shards=None, early_exchange=None):
    s = x.shape[0]
    cos, sin = _rope_tables(s)
    r_k = r_k.reshape(1, RW)

    proj, h = _proj_fwd(x, norm_g, w_in)
    ya, ya_t = _attn_fwd(proj, cos, sin, gq, gk, sink)
    ps = _shift_fwd(proj, mu)
    kk, dec0, kd0, b0, dec1, kd1, b1 = _pre_fwd(ps, k_k, k_a, w0, w2, a0, a2)
    ((y0, ck0), (y1, ck1)), stacks = _scan2_fwd([(dec0, kd0, b0), (dec1, kd1, b1)], ps, kk, gather=late_shards or ())
    if late_shards:
        st = dict(zip(LATE, stacks))
        wa, wr, wx = (_unshard_cols(st[n]) for n in ('attn_w_o', 'rwkv_w_o', 'x_w_o'))
        w_kv, w_out = st['x_w_kv'].reshape(D, 2 * XW), st['w_out'].reshape(D, D)
    mkv, mn = _mem_kv(mem, mem_norm_g, w_kv)
    yx, yx_t = _xattn_fwd(proj, mkv, gxq, gxk)
    yr, yr_t = _post_fwd(y0, y1, ps, kd0, kd1, proj, r_k, ln_w, ln_b)
    merged, merged_t = _merge_fwd(ya, yr, yx, wa, wr, wx, proj, gate_b)
    loss_tile, dout, dout16 = _out_fwd(merged, w_out, x, target)
    loss_sum = loss_tile[0, 0]

    g = {}
    sk = min(1024, s)
    dmerged = _matmul(dout16, w_out, mode="nt", m=s, n=D, k=D, tm=sk, tn=1024, tk=1024, name="dmerged")
    g["w_out"], g["w_out_bf16"] = _matmul(merged_t, dout16, mode="nn", m=D, n=D, k=s, tm=1024, tn=1024, tk=sk,
                                          name="grad_w_out", twin=True)
    dg0, dg1, dg2, du0, du1, du2, dya, dyr, dyx = _merge_bwd(ya, yr, yx, wa, wr, wx, proj, gate_b, dmerged)
    for n, yt, du in (("attn_w_o", ya_t, du0), ("rwkv_w_o", yr_t, du1), ("x_w_o", yx_t, du2)):
        g[n], g[n + "_bf16"] = _matmul(yt, du, mode="nn", m=yt.shape[0], n=D, k=s, tm=yt.shape[0], tn=512, tk=s,
                                       name="grad_" + n, shards=4, twin=True)
    dmg = jnp.concatenate([dg0, dg1, dg2], axis=1)
    g["gate_b"] = _colsum(dmg, "grad_gate_b")

    daq, dak, dav, dag, g["attn_q_norm_g"], g["attn_k_norm_g"], g["attn_sink"] = _attn_bwd(proj, cos, sin, gq, gk, sink, dya)

    dxq, dxg, dmkv, g["x_q_norm_g"], g["x_k_norm_g"] = _xattn_bwd(proj, mkv, gxq, gxk, dyx)
    g["x_w_kv"], g["x_w_kv_bf16"] = _matmul(mn, dmkv, mode="tn", m=D, n=2 * XW, k=NMEM, tm=512, tn=512, tk=NMEM,
                                            name="grad_x_w_kv", twin=True)
    dmn = _matmul(dmkv, w_kv, mode="nt", m=NMEM, n=D, k=2 * XW, tm=NMEM, tn=512, tk=2 * XW, name="dmn")
    g["mem_norm_g"] = _mem_bwd(mem, dmn)

    dys, dr_p, dv_p, dkd0_p, dkd1_p, drg, g["rwkv_r_k"], g["rwkv_ln_w"], g["rwkv_ln_b"] = _post_bwd(
        y0, y1, ps, kd0, kd1, proj, r_k, ln_w, ln_b, dyr)
    sent = early_exchange(g) if early_exchange else ()
    ((dr0, dd0, db0, dk0, dkk0, dv0), (dr1, dd1, db1, dk1, dkk1, dv1)), received = _scan2_bwd(
        [(dec0, kd0, b0, ck0), (dec1, kd1, b1, ck1)], ps, kk, dys, scatter=sent)
    dr = dr_p + dr0 + dr1
    dv = dv_p + dv0 + dv1
    cts = (dkk0 + dkk1, dd0, dk0 + dkd0_p, db0, dd1, dk1 + dkd1_p, db1)
    dps, g["rwkv_k_k"], g["rwkv_k_a"], g["rwkv_w0"], g["rwkv_w2"], g["rwkv_a0"], g["rwkv_a2"] = _pre_bwd(
        ps, k_k, k_a, w0, w2, a0, a2, dr, dv, cts)
    drs, g["rwkv_mu"] = _shift_bwd(proj, mu, dps)

    dproj = jnp.concatenate([daq.astype(BF16), dak.astype(BF16), dav.astype(BF16), dag.astype(BF16), drs.astype(BF16),
                             drg.astype(BF16), dxq.astype(BF16), dxg.astype(BF16), dmg], axis=1)
    dproj4 = jnp.stack([dproj[:, j * (NIN // 4):(j + 1) * (NIN // 4)] for j in range(4)])
    g["w_in"], g["w_in_bf16"] = _grad_w_in(h.T, dproj4)
    g["rwkv_r_k"] = g["rwkv_r_k"].reshape(AH, HD)
    return loss_sum, g, (dproj, w_in, x, norm_g, dout), received


WEIGHTS = ['norm_g', 'mem_norm_g', 'w_in', 'gate_b', 'attn_q_norm_g', 'attn_k_norm_g', 'attn_sink', 'attn_w_o',
           'rwkv_mu', 'rwkv_k_k', 'rwkv_k_a', 'rwkv_r_k', 'rwkv_w0', 'rwkv_w2', 'rwkv_a0', 'rwkv_a2', 'rwkv_ln_w',
           'rwkv_ln_b', 'rwkv_w_o', 'x_w_kv', 'x_q_norm_g', 'x_k_norm_g', 'x_w_o', 'w_out']
BIG = ['w_in', 'attn_w_o', 'rwkv_w_o', 'x_w_kv', 'x_w_o', 'w_out']
COL_SHARDED = ['w_in', 'attn_w_o', 'rwkv_w_o', 'x_w_o']
LORA = ['rwkv_w0', 'rwkv_w2', 'rwkv_a0', 'rwkv_a2']
SMALL = [n for n in WEIGHTS if n not in BIG]


def _unshard_cols(stack):
    return jnp.concatenate([stack[i] for i in range(4)], axis=-1)


def kernel(x, mem, norm_g, mem_norm_g, w_in, gate_b, attn_q_norm_g, attn_k_norm_g, attn_sink, attn_w_o, rwkv_mu, rwkv_k_k, rwkv_k_a, rwkv_r_k, rwkv_w0, rwkv_w2, rwkv_a0, rwkv_a2, rwkv_ln_w, rwkv_ln_b, rwkv_w_o, x_w_kv, x_q_norm_g, x_k_norm_g, x_w_o, w_out, loss_target, m_norm_g, m_mem_norm_g, m_w_in, m_gate_b, m_attn_q_norm_g, m_attn_k_norm_g, m_attn_sink, m_attn_w_o, m_rwkv_mu, m_rwkv_k_k, m_rwkv_k_a, m_rwkv_r_k, m_rwkv_w0, m_rwkv_w2, m_rwkv_a0, m_rwkv_a2, m_rwkv_ln_w, m_rwkv_ln_b, m_rwkv_w_o, m_x_w_kv, m_x_q_norm_g, m_x_k_norm_g, m_x_w_o, m_w_out, v_norm_g, v_mem_norm_g, v_w_in, v_gate_b, v_attn_q_norm_g, v_attn_k_norm_g, v_attn_sink, v_attn_w_o, v_rwkv_mu, v_rwkv_k_k, v_rwkv_k_a, v_rwkv_r_k, v_rwkv_w0, v_rwkv_w2, v_rwkv_a0, v_rwkv_a2, v_rwkv_ln_w, v_rwkv_ln_b, v_rwkv_w_o, v_x_w_kv, v_x_q_norm_g, v_x_k_norm_g, v_x_w_o, v_w_out):
    args = dict(locals())
    canon = lambda a: a[0] if a.ndim > 2 else a
    w = {n: canon(args[n]) for n in WEIGHTS}
    m = {n: canon(args["m_" + n]) for n in WEIGHTS}
    v = {n: canon(args["v_" + n]) for n in WEIGHTS}
    shard = 2 * lax.axis_index("x") + lax.axis_index("y")

    now = ["w_in"] + LORA
    local = [w["w_in"].astype(BF16)] + [w[n].reshape(2, -1, w[n].shape[-1]) for n in LORA]
    stacks = dict(zip(now, _gather_shards(local, "gather_weights")))
    full = {"w_in": _unshard_cols(stacks["w_in"])}
    for n in LORA:
        full[n] = _unshard_cols(stacks[n]).reshape(w[n].shape[:-1] + (RW,))

    core = lax.axis_index("c").astype(jnp.int32).reshape(1)
    pair32 = {}

    def as_stack(g, n, dtype):
        t = g[n] if dtype == F32 else g[n + "_bf16"]
        return t if n in COL_SHARDED else t.reshape((4, t.shape[0] // 4) + t.shape[1:])

    def pair_sums(g, names, tag):
        sibling = _pair_exchange([as_stack(g, n, BF16) for n in names], "pair_exchange_" + tag)
        sent = []
        for n, th in zip(names, sibling):
            pair32[n], a16 = _pair_sum(as_stack(g, n, F32), th, core, "pair_sum_" + n)
            sent.append(a16)
        return sent

    loss_sum, g, deferred, recv_late = _local_step(
        x[0], mem[0], loss_target[0], w["norm_g"], w["mem_norm_g"], full["w_in"], w["gate_b"], w["attn_q_norm_g"],
        w["attn_k_norm_g"], w["attn_sink"], None, w["rwkv_mu"], w["rwkv_k_k"], w["rwkv_k_a"], w["rwkv_r_k"],
        full["rwkv_w0"], full["rwkv_w2"], full["rwkv_a0"], full["rwkv_a2"], w["rwkv_ln_w"], w["rwkv_ln_b"],
        None, None, w["x_q_norm_g"], w["x_k_norm_g"], None, None,
        late_shards=[w[n].astype(BF16) for n in LATE], early_exchange=lambda g: pair_sums(g, LATE, "late"))

    loss = lax.psum(0.5 * loss_sum / D, ("x", "y", "c"))

    grad_x, g["norm_g"], recv_w_in = _in_bwd(*deferred, stacks=pair_sums(g, ["w_in"], "w_in"))
    halves = []
    for n, r in zip(BIG, recv_w_in + recv_late):
        own = lax.dynamic_index_in_dim(pair32[n], shard, 0, keepdims=False)
        halves.append(_sum_parts([own, r[0], r[1], r[2]], "sum_" + n))
    other_halves = _swap_sibling(halves, "swap_halves")

    out_g, out_d, out_m, out_v = {}, {}, {}, {}
    for n, mine, theirs in zip(BIG, halves, other_halves):
        out_g[n], out_d[n], out_m[n], out_v[n] = _adamw_halves(mine, theirs, core, w[n], m[n], v[n], "adamw_" + n)

    flat = jnp.concatenate([g[n].reshape(-1) for n in SMALL])
    total = flat.shape[0]
    padded = -(-total // 2048) * 2048
    flat = jnp.pad(flat, (0, padded - total)).reshape(padded // 128, 128)
    red = _all_reduce_small(flat).reshape(-1)
    off = 0
    gs = {}
    for n in SMALL:
        size = g[n].size
        t = red[off:off + size].reshape(g[n].shape)
        off += size
        if n in LORA:
            wd = t.shape[-1] // 4
            t = lax.dynamic_slice_in_dim(t, shard * wd, wd, axis=t.ndim - 1)
        gs[n] = t

    def pack(d):
        f = jnp.concatenate([d[n].reshape(-1) for n in SMALL])
        return jnp.pad(f, (0, -(-f.shape[0] // 1024) * 1024 - f.shape[0])).reshape(-1, 128)

    pg, pd, pm, pv = _adamw([pack(gs)], pack(w), pack(m), pack(v), "adamw_small")
    off = 0
    for n in SMALL:
        size = w[n].size
        for dst, src in ((out_g, pg), (out_d, pd), (out_m, pm), (out_v, pv)):
            dst[n] = src.reshape(-1)[off:off + size].reshape(w[n].shape)
        off += size

    lead = lambda d: [d[n][None] if args[n].ndim > 2 else d[n] for n in WEIGHTS]
    return (loss, grad_x[None], *lead(out_g), *lead(out_d), *lead(out_m), *lead(out_v))
```

```python
import jax
import jax.numpy as jnp
from jax import lax
from jax.experimental import pallas as pl
from jax.experimental.pallas import tpu as pltpu

F32 = jnp.float32
BF16 = jnp.bfloat16
HI = lax.Precision.HIGH
MESH = pl.DeviceIdType.MESH

D = 2048
NMEM = 256
NORM_EPS = 1e-6
NEG_INF = -1e30
GN_EPS = 64e-5
HD = 64
AH = 12
AKV = 4
RW = 768
XH = 4
XD = 128
XW = 512
NIN = 12544
RSW = 2560
C_AQ, C_AK, C_AV, C_AG, C_RS, C_RG, C_XQ, C_XG, C_MG = 0, 768, 1024, 1280, 2048, 4608, 5376, 5888, 6400
WINDOW = 128
QB = 256
WIN = QB + 2 * WINDOW
TC = 16
NPAIR = 6

ADAM_LR, ADAM_B1, ADAM_B2, ADAM_EPS, ADAM_WD, ADAM_STEP = 0.001, 0.9, 0.999, 1e-08, 0.01, 10

VMEM_LIMIT = 56 * 1024 * 1024


def _bs(shape, imap):
    return pl.BlockSpec(shape, imap)


def _params(sem=None, vmem=VMEM_LIMIT):
    return pltpu.CompilerParams(dimension_semantics=sem, vmem_limit_bytes=vmem)


def _dot(a, b, dims):
    return lax.dot_general(a.astype(BF16), b.astype(BF16), (dims, ((), ())), preferred_element_type=F32)


@jax.custom_vjp
def _mm_nn(a, b):
    return _dot(a, b, ((1,), (0,)))


def _mm_nn_fwd(a, b):
    return _mm_nn(a, b), (a, b)


def _mm_nn_bwd(res, ct):
    a, b = res
    return _dot(ct, b, ((1,), (1,))), _dot(a, ct, ((0,), (0,)))


_mm_nn.defvjp(_mm_nn_fwd, _mm_nn_bwd)


@jax.custom_vjp
def _mm_nt(a, b):
    return _dot(a, b, ((1,), (1,)))


def _mm_nt_fwd(a, b):
    return _mm_nt(a, b), (a, b)


def _mm_nt_bwd(res, ct):
    a, b = res
    return _dot(ct, b, ((1,), (0,))), _dot(ct, a, ((0,), (0,)))


_mm_nt.defvjp(_mm_nt_fwd, _mm_nt_bwd)


def _seg_matrix(n, seg):
    r = lax.broadcasted_iota(jnp.int32, (n, n), 0) // seg
    c = lax.broadcasted_iota(jnp.int32, (n, n), 1) // seg
    return (r == c).astype(F32)


def _rot_matrix():
    r = lax.broadcasted_iota(jnp.int32, (HD, HD), 0)
    c = lax.broadcasted_iota(jnp.int32, (HD, HD), 1)
    return jnp.where(c == r + HD // 2, 1.0, 0.0).astype(F32) - jnp.where(c == r - HD // 2, 1.0, 0.0).astype(F32)


def _hdot(a, m):
    return jnp.dot(a, m, precision=HI, preferred_element_type=F32)


def _rms(t, g):
    return t * lax.rsqrt(jnp.mean(t * t, axis=-1, keepdims=True) + NORM_EPS) * g


def _silu(t):
    return t * jax.nn.sigmoid(t)


def _softplus(z):
    return jnp.maximum(z, 0.0) + jnp.log(1.0 + jnp.exp(-jnp.abs(z)))


def _matmul(a, b, *, mode, m, n, k, tm, tn, tk, name, a_off=(0, 0), b_off=(0, 0), out_dtype=F32, shards=0, twin=False):
    nk = k // tk
    if mode == "tn":
        a_spec = _bs((tk, tm), lambda i, j, kk: (kk + a_off[0], i + a_off[1]))
        dims = ((0,), (0,))
    else:
        a_spec = _bs((tm, tk), lambda i, j, kk: (i + a_off[0], kk + a_off[1]))
        dims = ((1,), (1,)) if mode == "nt" else ((1,), (0,))
    if mode == "nt":
        b_spec = _bs((tn, tk), lambda i, j, kk: (j + b_off[0], kk + b_off[1]))
    else:
        b_spec = _bs((tk, tn), lambda i, j, kk: (kk + b_off[0], j + b_off[1]))
    if shards:
        per = n // shards // tn
        o_spec = _bs((1, tm, tn), lambda i, j, kk: (j // per, i, j % per))
        o_shape = (shards, m, n // shards)
    else:
        o_spec = _bs((tm, tn), lambda i, j, kk: (i, j))
        o_shape = (m, n)

    def body(a_ref, b_ref, *rest):
        o_refs, acc = rest[:-1], rest[-1]
        kk = pl.program_id(2)

        @pl.when(kk == 0)
        def _():
            acc[...] = jnp.zeros_like(acc)

        acc[...] += _dot(a_ref[...], b_ref[...], dims)

        @pl.when(kk == nk - 1)
        def _():
            for o_ref in o_refs:
                o_ref[...] = acc[...].astype(o_ref.dtype).reshape(o_ref.shape)

    dtypes = [out_dtype, BF16] if twin else [out_dtype]
    res = pl.pallas_call(
        body, name=name, grid=(m // tm, n // tn, nk),
        in_specs=[a_spec, b_spec], out_specs=[o_spec] * len(dtypes),
        out_shape=[jax.ShapeDtypeStruct(o_shape, dt) for dt in dtypes],
        scratch_shapes=[pltpu.VMEM((tm, tn), F32)],
        compiler_params=_params(("parallel", "parallel", "arbitrary")),
    )(a, b)
    return res if twin else res[0]


def _grad_w_in(ht, dproj4):
    s = ht.shape[1]
    ws = NIN // 4
    tm, tk = 256, s
    nk = s // tk

    def body(a_ref, b_ref, o32_ref, o16_ref, acc):
        kk = pl.program_id(2)

        @pl.when(kk == 0)
        def _():
            acc[...] = jnp.zeros_like(acc)

        acc[...] += jnp.dot(a_ref[...], b_ref[0], preferred_element_type=F32)

        @pl.when(kk == nk - 1)
        def _():
            o32_ref[0] = acc[...]
            o16_ref[0] = acc[...].astype(BF16)

    out = _bs((1, tm, ws), lambda j, i, kk: (j, i, 0))
    return pl.pallas_call(
        body, name="grad_w_in", grid=(4, D // tm, nk),
        in_specs=[_bs((tm, tk), lambda j, i, kk: (i, kk)), _bs((1, tk, ws), lambda j, i, kk: (j, kk, 0))],
        out_specs=[out, out],
        out_shape=[jax.ShapeDtypeStruct((4, D, ws), F32), jax.ShapeDtypeStruct((4, D, ws), BF16)],
        scratch_shapes=[pltpu.VMEM((tm, ws), F32)],
        compiler_params=_params(("parallel", "parallel", "arbitrary")),
    )(ht, dproj4)


def _proj_fwd(x, g, w):
    s = x.shape[0]
    tm, tn = min(1024, s), 896

    def body(x_ref, g_ref, w_ref, o_ref, h_ref, hs):
        @pl.when(pl.program_id(1) == 0)
        def _():
            h = _rms(x_ref[...], g_ref[...]).astype(BF16)
            hs[...] = h
            h_ref[...] = h

        o_ref[...] = jnp.dot(hs[...], w_ref[...], preferred_element_type=F32)

    return pl.pallas_call(
        body, name="proj_fwd", grid=(s // tm, NIN // tn),
        in_specs=[_bs((tm, D), lambda i, j: (i, 0)), _bs((1, D), lambda i, j: (0, 0)), _bs((D, tn), lambda i, j: (0, j))],
        out_specs=[_bs((tm, tn), lambda i, j: (i, j)), _bs((tm, D), lambda i, j: (i, 0))],
        out_shape=[jax.ShapeDtypeStruct((s, NIN), F32), jax.ShapeDtypeStruct((s, D), BF16)],
        scratch_shapes=[pltpu.VMEM((tm, D), BF16)],
        compiler_params=_params(("parallel", "arbitrary")),
    )(x, g, w)


def _rope(t, cos, sin, rot):
    return t * cos + _hdot(t, rot) * sin


def _attn_tile(qs, ks, vs, gs, sinks, gq, gk, cq, sq, ck, sk, mask, rot):
    heads = range(AH)
    kv = [h // (AH // AKV) for h in heads]
    kh = [_rope(_rms(ks[j], gk), ck, sk, rot) for j in range(AKV)]
    qh = [_rope(_rms(qs[h], gq), cq, sq, rot) for h in heads]
    sc = [jnp.where(mask, _mm_nt(qh[h], kh[kv[h]]) * (HD ** -0.5), NEG_INF) for h in heads]
    mx = [lax.stop_gradient(jnp.maximum(jnp.max(sc[h], axis=-1, keepdims=True), sinks[h])) for h in heads]
    p = [jnp.exp(sc[h] - mx[h]) for h in heads]
    den = [jnp.sum(p[h], axis=-1, keepdims=True) + jnp.exp(sinks[h] - mx[h]) for h in heads]
    o = [_mm_nn(p[h] / den[h], vs[kv[h]]) for h in heads]
    return [o[h] * _silu(gs[h]) for h in heads]


def _attn_load(n, s, aq_ref, ak_ref, av_ref, ag_refs, cos_ref, sin_ref, sink_ref):
    start = pl.multiple_of(jnp.clip(n * QB - WINDOW, 0, s - WIN), WINDOW)
    q0 = pl.multiple_of(n * QB, QB)
    qs = [aq_ref[:, h * HD:(h + 1) * HD] for h in range(AH)]
    ks = [ak_ref[pl.ds(start, WIN), h * HD:(h + 1) * HD] for h in range(AKV)]
    vs = [av_ref[pl.ds(start, WIN), h * HD:(h + 1) * HD] for h in range(AKV)]
    gs = [ag_refs[h // 4][:, (h % 4) * HD:(h % 4 + 1) * HD] for h in range(AH)]
    sinks = [sink_ref[0:1, h:h + 1] for h in range(AH)]
    cq, sq = cos_ref[pl.ds(q0, QB), :], sin_ref[pl.ds(q0, QB), :]
    ck, sk = cos_ref[pl.ds(start, WIN), :], sin_ref[pl.ds(start, WIN), :]
    qpos = q0 + lax.broadcasted_iota(jnp.int32, (QB, WIN), 0)
    kpos = start + lax.broadcasted_iota(jnp.int32, (QB, WIN), 1)
    mask = jnp.abs(kpos - qpos) <= WINDOW
    return start, qs, ks, vs, gs, sinks, cq, sq, ck, sk, mask


def _attn_specs(s):
    return [
        _bs((QB, 768), lambda n: (n, 0)),
        _bs((s, 256), lambda n: (0, C_AK // 256)),
        _bs((s, 256), lambda n: (0, C_AV // 256)),
        _bs((QB, 256), lambda n: (n, C_AG // 256)),
        _bs((QB, 256), lambda n: (n, C_AG // 256 + 1)),
        _bs((QB, 256), lambda n: (n, C_AG // 256 + 2)),
        _bs((s, HD), lambda n: (0, 0)),
        _bs((s, HD), lambda n: (0, 0)),
        _bs((1, HD), lambda n: (0, 0)),
        _bs((1, HD), lambda n: (0, 0)),
        _bs((1, AH), lambda n: (0, 0)),
    ]


def _attn_fwd(proj, cos, sin, gq, gk, sink):
    s = proj.shape[0]

    def body(aq_ref, ak_ref, av_ref, ag0, ag1, ag2, cos_ref, sin_ref, gq_ref, gk_ref, sink_ref, o_ref, ot_ref):
        n = pl.program_id(0)
        _, qs, ks, vs, gs, sinks, cq, sq, ck, sk, mask = _attn_load(
            n, s, aq_ref, ak_ref, av_ref, (ag0, ag1, ag2), cos_ref, sin_ref, sink_ref)
        outs = _attn_tile(qs, ks, vs, gs, sinks, gq_ref[...], gk_ref[...], cq, sq, ck, sk, mask, _rot_matrix())
        for h in range(AH):
            o_ref[:, h * HD:(h + 1) * HD] = outs[h]
        ot_ref[...] = o_ref[...].T.astype(BF16)

    return pl.pallas_call(
        body, name="attn_fwd", grid=(s // QB,),
        in_specs=_attn_specs(s), out_specs=[_bs((QB, 768), lambda n: (n, 0)), _bs((768, QB), lambda n: (0, n))],
        out_shape=[jax.ShapeDtypeStruct((s, 768), F32), jax.ShapeDtypeStruct((768, s), BF16)],
        compiler_params=_params(("arbitrary",)),
    )(proj, proj, proj, proj, proj, proj, cos, sin, gq, gk, sink)


def _attn_bwd(proj, cos, sin, gq, gk, sink, dy):
    s = proj.shape[0]

    def body(aq_ref, ak_ref, av_ref, ag0, ag1, ag2, cos_ref, sin_ref, gq_ref, gk_ref, sink_ref, dy_ref,
             daq_ref, dak_ref, dav_ref, dag_ref, dgq_ref, dgk_ref, dsink_ref):
        n = pl.program_id(0)

        @pl.when(n == 0)
        def _():
            dak_ref[...] = jnp.zeros_like(dak_ref)
            dav_ref[...] = jnp.zeros_like(dav_ref)
            dgq_ref[...] = jnp.zeros_like(dgq_ref)
            dgk_ref[...] = jnp.zeros_like(dgk_ref)
            dsink_ref[...] = jnp.zeros_like(dsink_ref)

        start, qs, ks, vs, gs, sinks, cq, sq, ck, sk, mask = _attn_load(
            n, s, aq_ref, ak_ref, av_ref, (ag0, ag1, ag2), cos_ref, sin_ref, sink_ref)
        rot = _rot_matrix()

        def f(qs, ks, vs, gs, sinks, gq, gk):
            return _attn_tile(qs, ks, vs, gs, sinks, gq, gk, cq, sq, ck, sk, mask, rot)

        _, vjp = jax.vjp(f, qs, ks, vs, gs, sinks, gq_ref[...], gk_ref[...])
        dys = [dy_ref[:, h * HD:(h + 1) * HD] for h in range(AH)]
        dqs, dks, dvs, dgs, dsinks, dgq, dgk = vjp(dys)
        for h in range(AH):
            daq_ref[:, h * HD:(h + 1) * HD] = dqs[h]
            dag_ref[:, h * HD:(h + 1) * HD] = dgs[h]
            dsink_ref[0:1, h:h + 1] += dsinks[h]
        for h in range(AKV):
            dak_ref[pl.ds(start, WIN), h * HD:(h + 1) * HD] += dks[h]
            dav_ref[pl.ds(start, WIN), h * HD:(h + 1) * HD] += dvs[h]
        dgq_ref[...] += dgq
        dgk_ref[...] += dgk

    whole = lambda shape: _bs(shape, lambda n: (0, 0))
    return pl.pallas_call(
        body, name="attn_bwd", grid=(s // QB,),
        in_specs=_attn_specs(s) + [_bs((QB, 768), lambda n: (n, 0))],
        out_specs=[_bs((QB, 768), lambda n: (n, 0)), whole((s, 256)), whole((s, 256)), _bs((QB, 768), lambda n: (n, 0)),
                   whole((1, HD)), whole((1, HD)), whole((1, AH))],
        out_shape=[jax.ShapeDtypeStruct((s, 768), F32), jax.ShapeDtypeStruct((s, 256), F32),
                   jax.ShapeDtypeStruct((s, 256), F32), jax.ShapeDtypeStruct((s, 768), F32),
                   jax.ShapeDtypeStruct((1, HD), F32), jax.ShapeDtypeStruct((1, HD), F32),
                   jax.ShapeDtypeStruct((1, AH), F32)],
        compiler_params=_params(("arbitrary",)),
    )(proj, proj, proj, proj, proj, proj, cos, sin, gq, gk, sink, dy)


def _mem_kv(mem, g, w):
    def body(m_ref, g_ref, w_ref, o_ref, mn_ref):
        mn = _rms(m_ref[...], g_ref[...]).astype(BF16)
        mn_ref[...] = mn
        o_ref[...] = jnp.dot(mn, w_ref[...], preferred_element_type=F32)

    return pl.pallas_call(
        body, name="mem_kv",
        out_shape=[jax.ShapeDtypeStruct((NMEM, 2 * XW), F32), jax.ShapeDtypeStruct((NMEM, D), BF16)],
        compiler_params=_params(),
    )(mem, g, w)


def _xattn_tile(qs, gs, kms, vms, gxq, gxk):
    heads = range(XH)
    q = [_rms(qs[h], gxq) for h in heads]
    km = [_rms(kms[h], gxk) for h in heads]
    sc = [_mm_nt(q[h], km[h]) * (XD ** -0.5) for h in heads]
    p = [jnp.exp(sc[h] - lax.stop_gradient(jnp.max(sc[h], axis=-1, keepdims=True))) for h in heads]
    p = [p[h] / jnp.sum(p[h], axis=-1, keepdims=True) for h in heads]
    return [_mm_nn(p[h], vms[h]) * _silu(gs[h]) for h in heads]


XT = 512


def _xattn_specs():
    return [
        _bs((XT, 256), lambda i: (i, C_XQ // 256)), _bs((XT, 256), lambda i: (i, C_XQ // 256 + 1)),
        _bs((XT, 256), lambda i: (i, C_XG // 256)), _bs((XT, 256), lambda i: (i, C_XG // 256 + 1)),
        _bs((NMEM, 2 * XW), lambda i: (0, 0)),
        _bs((1, XD), lambda i: (0, 0)), _bs((1, XD), lambda i: (0, 0)),
    ]


def _xattn_load(q0, q1, g0, g1, mkv_ref):
    qs = [(q0, q1)[h // 2][:, (h % 2) * XD:(h % 2 + 1) * XD] for h in range(XH)]
    gs = [(g0, g1)[h // 2][:, (h % 2) * XD:(h % 2 + 1) * XD] for h in range(XH)]
    kms = [mkv_ref[:, h * XD:(h + 1) * XD] for h in range(XH)]
    vms = [mkv_ref[:, XW + h * XD:XW + (h + 1) * XD] for h in range(XH)]
    return qs, gs, kms, vms


def _xattn_fwd(proj, mkv, gxq, gxk):
    s = proj.shape[0]

    def body(q0, q1, g0, g1, mkv_ref, gxq_ref, gxk_ref, o_ref, ot_ref):
        qs, gs, kms, vms = _xattn_load(q0, q1, g0, g1, mkv_ref)
        outs = _xattn_tile(qs, gs, kms, vms, gxq_ref[...], gxk_ref[...])
        for h in range(XH):
            o_ref[:, h * XD:(h + 1) * XD] = outs[h]
        ot_ref[...] = o_ref[...].T.astype(BF16)

    return pl.pallas_call(
        body, name="xattn_fwd", grid=(s // XT,),
        in_specs=_xattn_specs(), out_specs=[_bs((XT, XW), lambda i: (i, 0)), _bs((XW, XT), lambda i: (0, i))],
        out_shape=[jax.ShapeDtypeStruct((s, XW), F32), jax.ShapeDtypeStruct((XW, s), BF16)],
        compiler_params=_params(("arbitrary",)),
    )(proj, proj, proj, proj, mkv, gxq, gxk)


def _xattn_bwd(proj, mkv, gxq, gxk, dy):
    s = proj.shape[0]

    def body(q0, q1, g0, g1, mkv_ref, gxq_ref, gxk_ref, dy_ref, dq_ref, dg_ref, dmkv_ref, dgxq_ref, dgxk_ref):
        @pl.when(pl.program_id(0) == 0)
        def _():
            dmkv_ref[...] = jnp.zeros_like(dmkv_ref)
            dgxq_ref[...] = jnp.zeros_like(dgxq_ref)
            dgxk_ref[...] = jnp.zeros_like(dgxk_ref)

        qs, gs, kms, vms = _xattn_load(q0, q1, g0, g1, mkv_ref)
        _, vjp = jax.vjp(_xattn_tile, qs, gs, kms, vms, gxq_ref[...], gxk_ref[...])
        dqs, dgs, dkms, dvms, dgxq, dgxk = vjp([dy_ref[:, h * XD:(h + 1) * XD] for h in range(XH)])
        for h in range(XH):
            dq_ref[:, h * XD:(h + 1) * XD] = dqs[h]
            dg_ref[:, h * XD:(h + 1) * XD] = dgs[h]
            dmkv_ref[:, h * XD:(h + 1) * XD] += dkms[h]
            dmkv_ref[:, XW + h * XD:XW + (h + 1) * XD] += dvms[h]
        dgxq_ref[...] += dgxq
        dgxk_ref[...] += dgxk

    whole = lambda shape: _bs(shape, lambda i: (0, 0))
    return pl.pallas_call(
        body, name="xattn_bwd", grid=(s // XT,),
        in_specs=_xattn_specs() + [_bs((XT, XW), lambda i: (i, 0))],
        out_specs=[_bs((XT, XW), lambda i: (i, 0)), _bs((XT, XW), lambda i: (i, 0)), whole((NMEM, 2 * XW)),
                   whole((1, XD)), whole((1, XD))],
        out_shape=[jax.ShapeDtypeStruct((s, XW), F32), jax.ShapeDtypeStruct((s, XW), F32),
                   jax.ShapeDtypeStruct((NMEM, 2 * XW), F32), jax.ShapeDtypeStruct((1, XD), F32),
                   jax.ShapeDtypeStruct((1, XD), F32)],
        compiler_params=_params(("arbitrary",)),
    )(proj, proj, proj, proj, mkv, gxq, gxk, dy)


def _mem_bwd(mem, dmn):
    def body(m_ref, dmn_ref, o_ref):
        m = m_ref[...]
        r = lax.rsqrt(jnp.mean(m * m, axis=-1, keepdims=True) + NORM_EPS)
        o_ref[...] = jnp.sum(dmn_ref[...] * m * r, axis=0, keepdims=True)

    return pl.pallas_call(body, name="mem_norm_bwd", out_shape=jax.ShapeDtypeStruct((1, D), F32),
                          compiler_params=_params())(mem, dmn)


SHIFT_W = 512


def _shift_rows(p, s):
    row = lax.broadcasted_iota(jnp.int32, p.shape, 0)
    prev = jnp.where(row == 0, 0.0, pltpu.roll(p, 1, 0))
    nxt = jnp.where(row == s - 1, 0.0, pltpu.roll(p, s - 1, 0))
    return prev, nxt


def _shift_fwd(proj, mu):
    s = proj.shape[0]

    def body(p_ref, mu_ref, o_ref):
        p = p_ref[...]
        prev, nxt = _shift_rows(p, s)
        o_ref[...] = p + mu_ref[...] * (0.5 * (prev + nxt) - p)

    return pl.pallas_call(
        body, name="shift_fwd", grid=(RSW // SHIFT_W,),
        in_specs=[_bs((s, SHIFT_W), lambda j: (0, C_RS // SHIFT_W + j)), _bs((1, SHIFT_W), lambda j: (0, j))],
        out_specs=_bs((s, SHIFT_W), lambda j: (0, j)),
        out_shape=jax.ShapeDtypeStruct((s, RSW), F32),
        compiler_params=_params(("parallel",)),
    )(proj, mu)


def _shift_bwd(proj, mu, dps):
    s = proj.shape[0]

    def body(p_ref, mu_ref, g_ref, o_ref, dmu_ref):
        p, g, mu_v = p_ref[...], g_ref[...], mu_ref[...]
        prev, nxt = _shift_rows(p, s)
        dmu_ref[...] = jnp.sum(g * (0.5 * (prev + nxt) - p), axis=0, keepdims=True)
        mg = mu_v * g
        down, up = _shift_rows(mg, s)
        o_ref[...] = g * (1.0 - mu_v) + 0.5 * (down + up)

    return pl.pallas_call(
        body, name="shift_bwd", grid=(RSW // SHIFT_W,),
        in_specs=[_bs((s, SHIFT_W), lambda j: (0, C_RS // SHIFT_W + j)), _bs((1, SHIFT_W), lambda j: (0, j)),
                  _bs((s, SHIFT_W), lambda j: (0, j))],
        out_specs=[_bs((s, SHIFT_W), lambda j: (0, j)), _bs((1, SHIFT_W), lambda j: (0, j))],
        out_shape=[jax.ShapeDtypeStruct((s, RSW), F32), jax.ShapeDtypeStruct((1, RSW), F32)],
        compiler_params=_params(("parallel",)),
    )(proj, mu, dps)


def _pre_tile(k, wf, wb, af, ab, k_k, k_a, w0s, w2s, a0s, a2s, seg):
    kx = k * k_k
    ss = _hdot(kx * kx, seg)
    kk = kx / jnp.maximum(jnp.sqrt(ss), 1e-12)
    outs = [kk]
    for d, (w_in, a_in) in enumerate(((wf, af), (wb, ab))):
        z = w0s[d] + _mm_nn(jnp.tanh(w_in), w2s[d])
        wd = -_softplus(-z) - 0.5
        dec = jnp.exp(-jnp.exp(wd))
        ad = jax.nn.sigmoid(a0s[d] + _mm_nn(a_in, a2s[d]))
        kd = k * (1.0 + (ad - 1.0) * k_a)
        outs += [dec, kd, kk * ad]
    return outs


PT = 256


def _pre_load(ps_ref, kk_ref, ka_ref, w0_ref, w2_ref, a0_ref, a2_ref):
    k = ps_ref[:, RW:2 * RW]
    wf, wb = ps_ref[:, 3 * RW:3 * RW + 64], ps_ref[:, 3 * RW + 64:3 * RW + 128]
    af, ab = ps_ref[:, 3 * RW + 128:3 * RW + 192], ps_ref[:, 3 * RW + 192:3 * RW + 256]
    w0s = [w0_ref[0:1, :], w0_ref[1:2, :]]
    a0s = [a0_ref[0:1, :], a0_ref[1:2, :]]
    w2s = [w2_ref[0], w2_ref[1]]
    a2s = [a2_ref[0], a2_ref[1]]
    return (k, wf, wb, af, ab, kk_ref[...], ka_ref[...], w0s, w2s, a0s, a2s)


def _pre_specs():
    c = lambda shape: _bs(shape, lambda i: tuple(0 for _ in shape))
    return [_bs((PT, RSW), lambda i: (i, 0)), c((1, RW)), c((1, RW)), c((2, RW)), c((2, 64, RW)), c((2, RW)),
            c((2, 64, RW))]


def _pre_fwd(ps, k_k, k_a, w0, w2, a0, a2):
    s = ps.shape[0]

    def body(ps_ref, kk_ref, ka_ref, w0_ref, w2_ref, a0_ref, a2_ref, *outs):
        args = _pre_load(ps_ref, kk_ref, ka_ref, w0_ref, w2_ref, a0_ref, a2_ref)
        res = _pre_tile(*args, _seg_matrix(RW, HD))
        for o_ref, v in zip(outs, res):
            o_ref[...] = v

    return pl.pallas_call(
        body, name="rwkv_pre_fwd", grid=(s // PT,),
        in_specs=_pre_specs(), out_specs=[_bs((PT, RW), lambda i: (i, 0))] * 7,
        out_shape=[jax.ShapeDtypeStruct((s, RW), F32)] * 7,
        compiler_params=_params(("parallel",)),
    )(ps, k_k, k_a, w0, w2, a0, a2)


def _pre_bwd(ps, k_k, k_a, w0, w2, a0, a2, dr, dv, cts):
    s = ps.shape[0]

    def body(ps_ref, kk_ref, ka_ref, w0_ref, w2_ref, a0_ref, a2_ref, dr_ref, dv_ref, c0, c1, c2, c3, c4, c5, c6,
             dps_ref, dkk_ref, dka_ref, dw0_ref, dw2_ref, da0_ref, da2_ref):
        @pl.when(pl.program_id(0) == 0)
        def _():
            for r in (dkk_ref, dka_ref, dw0_ref, dw2_ref, da0_ref, da2_ref):
                r[...] = jnp.zeros_like(r)

        args = _pre_load(ps_ref, kk_ref, ka_ref, w0_ref, w2_ref, a0_ref, a2_ref)
        seg = _seg_matrix(RW, HD)
        _, vjp = jax.vjp(lambda *a: _pre_tile(*a, seg), *args)
        dk, dwf, dwb, daf, dab, dk_k, dk_a, dw0s, dw2s, da0s, da2s = vjp([c[...] for c in (c0, c1, c2, c3, c4, c5, c6)])
        dps_ref[:, 0:RW] = dr_ref[...]
        dps_ref[:, RW:2 * RW] = dk
        dps_ref[:, 2 * RW:3 * RW] = dv_ref[...]
        for j, t in enumerate((dwf, dwb, daf, dab)):
            dps_ref[:, 3 * RW + 64 * j:3 * RW + 64 * (j + 1)] = t
        dkk_ref[...] += dk_k
        dka_ref[...] += dk_a
        for d in range(2):
            dw0_ref[d:d + 1, :] += dw0s[d]
            da0_ref[d:d + 1, :] += da0s[d]
            dw2_ref[d] += dw2s[d]
            da2_ref[d] += da2s[d]

    c = lambda shape: _bs(shape, lambda i: tuple(0 for _ in shape))
    row = _bs((PT, RW), lambda i: (i, 0))
    return pl.pallas_call(
        body, name="rwkv_pre_bwd", grid=(s // PT,),
        in_specs=_pre_specs() + [row] * 9,
        out_specs=[_bs((PT, RSW), lambda i: (i, 0)), c((1, RW)), c((1, RW)), c((2, RW)), c((2, 64, RW)), c((2, RW)),
                   c((2, 64, RW))],
        out_shape=[jax.ShapeDtypeStruct((s, RSW), F32), jax.ShapeDtypeStruct((1, RW), F32),
                   jax.ShapeDtypeStruct((1, RW), F32), jax.ShapeDtypeStruct((2, RW), F32),
                   jax.ShapeDtypeStruct((2, 64, RW), F32), jax.ShapeDtypeStruct((2, RW), F32),
                   jax.ShapeDtypeStruct((2, 64, RW), F32)],
        compiler_params=_params(("arbitrary",)),
    )(ps, k_k, k_a, w0, w2, a0, a2, dr, dv, *cts)


def _post_tile(y0, y1, r, v, kd0, kd1, rg, r_k, ln_w, ln_b, seg):
    ysum = y0 + y1
    bonus = (_hdot(r * kd0 * r_k, seg) + _hdot(r * kd1 * r_k, seg)) * v
    mean = _hdot(ysum, seg) * (1.0 / HD)
    cen = ysum - mean
    var = _hdot(cen * cen, seg) * (1.0 / HD)
    y = cen * lax.rsqrt(var + GN_EPS) * ln_w + ln_b + bonus
    return y * _silu(rg)


def _post_specs():
    row = _bs((PT, RW), lambda i: (i, 0))
    c = _bs((1, RW), lambda i: (0, 0))
    return [row, row, _bs((PT, RW), lambda i: (i, 0)), _bs((PT, RW), lambda i: (i, 2)), row, row,
            _bs((PT, RW), lambda i: (i, C_RG // RW)), c, c, c]


def _post_fwd(y0, y1, ps, kd0, kd1, proj, r_k, ln_w, ln_b):
    s = ps.shape[0]

    def body(y0_ref, y1_ref, r_ref, v_ref, kd0_ref, kd1_ref, rg_ref, rk_ref, lw_ref, lb_ref, o_ref, ot_ref):
        y = _post_tile(y0_ref[...], y1_ref[...], r_ref[...], v_ref[...], kd0_ref[...], kd1_ref[...],
                       rg_ref[...], rk_ref[...], lw_ref[...], lb_ref[...], _seg_matrix(RW, HD))
        o_ref[...] = y
        ot_ref[...] = y.T.astype(BF16)

    return pl.pallas_call(
        body, name="rwkv_post_fwd", grid=(s // PT,),
        in_specs=_post_specs(), out_specs=[_bs((PT, RW), lambda i: (i, 0)), _bs((RW, PT), lambda i: (0, i))],
        out_shape=[jax.ShapeDtypeStruct((s, RW), F32), jax.ShapeDtypeStruct((RW, s), BF16)],
        compiler_params=_params(("parallel",)),
    )(y0, y1, ps, ps, kd0, kd1, proj, r_k, ln_w, ln_b)


def _post_bwd(y0, y1, ps, kd0, kd1, proj, r_k, ln_w, ln_b, dy):
    s = ps.shape[0]

    def body(y0_ref, y1_ref, r_ref, v_ref, kd0_ref, kd1_ref, rg_ref, rk_ref, lw_ref, lb_ref, dy_ref,
             dys_ref, dr_ref, dv_ref, dkd0_ref, dkd1_ref, drg_ref, drk_ref, dlw_ref, dlb_ref):
        @pl.when(pl.program_id(0) == 0)
        def _():
            for r in (drk_ref, dlw_ref, dlb_ref):
                r[...] = jnp.zeros_like(r)

        seg = _seg_matrix(RW, HD)
        args = [t[...] for t in (y0_ref, y1_ref, r_ref, v_ref, kd0_ref, kd1_ref, rg_ref, rk_ref, lw_ref, lb_ref)]
        _, vjp = jax.vjp(lambda *a: _post_tile(*a, seg), *args)
        dy0, _, dr, dv, dkd0, dkd1, drg, drk, dlw, dlb = vjp(dy_ref[...])
        dys_ref[...] = dy0
        dr_ref[...] = dr
        dv_ref[...] = dv
        dkd0_ref[...] = dkd0
        dkd1_ref[...] = dkd1
        drg_ref[...] = drg
        drk_ref[...] += drk
        dlw_ref[...] += dlw
        dlb_ref[...] += dlb

    row = _bs((PT, RW), lambda i: (i, 0))
    c = _bs((1, RW), lambda i: (0, 0))
    return pl.pallas_call(
        body, name="rwkv_post_bwd", grid=(s // PT,),
        in_specs=_post_specs() + [row], out_specs=[row] * 6 + [c] * 3,
        out_shape=[jax.ShapeDtypeStruct((s, RW), F32)] * 6 + [jax.ShapeDtypeStruct((1, RW), F32)] * 3,
        compiler_params=_params(("arbitrary",)),
    )(y0, y1, ps, ps, kd0, kd1, proj, r_k, ln_w, ln_b, dy)


def _ones1():
    r = lax.broadcasted_iota(jnp.int32, (128, 128), 0) // HD
    c = lax.broadcasted_iota(jnp.int32, (128, 128), 1) // HD
    return (r == c).astype(BF16)


def _scan_specs(direction, nc, fwd_order):
    def tb(c):
        sc = c if fwd_order else nc - 1 - c
        return sc if direction == 0 else nc - 1 - sc

    row = _bs((TC, RW), lambda c: (tb(c), 0))
    rowv = _bs((TC, RW), lambda c: (tb(c), 2))
    return row, rowv


def _tiles(res, k):
    n = NPAIR * HD
    return [res[k * n + p * HD:k * n + (p + 1) * HD] for p in range(NPAIR)]


def _rows_to_tiles(src_ref, rows8, stage, out_s, base):
    for p in range(NPAIR):
        stage[base + p, 0:8, 0:HD] = src_ref[rows8, p * 128:p * 128 + HD]
        stage[base + p, HD:HD + 8, 0:HD] = src_ref[rows8, p * 128 + HD:(p + 1) * 128]
        out_s[base + p] = stage[base + p].T[0:HD].astype(BF16)


def _tiles_to_rows(tile_s, base, dst_ref, rows8):
    for p in range(NPAIR):
        t = jnp.concatenate([tile_s[base + p], jnp.zeros((HD, 128), F32)], axis=0).T
        dst_ref[rows8, p * 128:p * 128 + HD] = t[0:8, 0:HD]
        dst_ref[rows8, p * 128 + HD:(p + 1) * 128] = t[HD:HD + 8, 0:HD]


def _put_cols(tile_s, base, u, tiles):
    for p in range(NPAIR):
        tile_s[base + p, :, u:u + 1] = tiles[p][:, u:u + 1]
        tile_s[base + p, :, HD + u:HD + u + 1] = tiles[p][:, HD + u:HD + u + 1]


def _scan2_fwd(per_dir, ps, kk, gather=()):
    s = ps.shape[0]
    nc, ng = s // TC, TC // 8
    ngat = len(gather)
    in_specs, operands, out_specs, out_shape = [], [], [], []
    for d in (0, 1):
        row, rowv = _scan_specs(d, nc, True)
        in_specs += [row] * 5 + [rowv]
        operands += list(per_dir[d]) + [ps, kk, ps]
        out_specs += [row, _bs((1, NPAIR, HD, 128), lambda c: (c, 0, 0, 0))]
        out_shape += [jax.ShapeDtypeStruct((s, RW), F32), jax.ShapeDtypeStruct((nc, NPAIR, HD, 128), F32)]
    in_specs += [ANY] * ngat
    operands += list(gather)
    out_specs += [ANY] * ngat
    out_shape += _gather_out_shapes(gather)

    def body(*refs):
        ins = [refs[0:6], refs[6:12]]
        base = 12 + ngat
        y_refs, ck_refs = (refs[base], refs[base + 2]), (refs[base + 1], refs[base + 3])
        st, vt_s, yt_s, stage = refs[base + 4 + ngat:base + 8 + ngat]
        if ngat:
            g_start, g_forward, g_finish = _gather_phases(
                gather, refs[12:base], refs[base + 4:base + 4 + ngat], refs[base + 8 + ngat:])

        @pl.when(pl.program_id(0) == 0)
        def _():
            st[...] = jnp.zeros_like(st)
            yt_s[...] = jnp.zeros_like(yt_s)
            stage[...] = jnp.zeros_like(stage)
            if ngat:
                g_start()

        if ngat:
            @pl.when(pl.program_id(0) == nc // 2)
            def _():
                g_forward()

        for d in (0, 1):
            ck_refs[d][0] = st[d * NPAIR:(d + 1) * NPAIR]
        ones1 = _ones1()
        lane_u = lax.broadcasted_iota(jnp.int32, (HD, 128), 1) % HD
        pc = [slice(p * 128, (p + 1) * 128) for p in range(NPAIR)]

        def group(gi, carry):
            gs = (gi, ng - 1 - gi)
            rows8 = [pl.ds(pl.multiple_of(gs[d] * 8, 8), 8) for d in (0, 1)]
            blk = [[q[rows8[d], :] for q in ins[d][:5]] for d in (0, 1)]
            for d in (0, 1):
                _rows_to_tiles(ins[d][5], rows8[d], stage, vt_s, d * NPAIR)
            ss = [[st[d * NPAIR + p] for p in range(NPAIR)] for d in (0, 1)]
            for ui in range(9):
                us, ups = (ui, 7 - ui), (ui - 1, 8 - ui)
                lhs1, where = [], {}
                for d in (0, 1):
                    if ui < 8:
                        where["sa", d] = len(lhs1) // NPAIR
                        lhs1 += [(ss[d][p] * blk[d][4][us[d]:us[d] + 1, pc[p]]).astype(BF16) for p in range(NPAIR)]
                        where["vb", d] = len(lhs1) // NPAIR
                        for p in range(NPAIR):
                            vt = vt_s[d * NPAIR + p]
                            lhs1.append(jnp.where(lane_u == us[d], vt, jnp.zeros_like(vt)))
                    if ui > 0:
                        where["y", d] = len(lhs1) // NPAIR
                        lhs1 += [(ss[d][p] * blk[d][3][ups[d]:ups[d] + 1, pc[p]]).astype(BF16) for p in range(NPAIR)]
                res1 = jnp.dot(jnp.concatenate(lhs1, axis=0), ones1, preferred_element_type=F32)
                for d in (0, 1):
                    d8, k8, b8, _, _ = blk[d]
                    u = us[d]
                    if ui < 8:
                        sa, vb = _tiles(res1, where["sa", d]), _tiles(res1, where["vb", d])
                        for p in range(NPAIR):
                            ss[d][p] = (ss[d][p] * d8[u:u + 1, pc[p]] - sa[p] * b8[u:u + 1, pc[p]]
                                        + vb[p] * k8[u:u + 1, pc[p]])
                    if ui > 0:
                        _put_cols(yt_s, d * NPAIR, ups[d], _tiles(res1, where["y", d]))
            for d in (0, 1):
                _tiles_to_rows(yt_s, d * NPAIR, y_refs[d], rows8[d])
                for p in range(NPAIR):
                    st[d * NPAIR + p] = ss[d][p]
            return carry

        for gi in range(ng):
            group(gi, 0)

        if ngat:
            @pl.when(pl.program_id(0) == nc - 1)
            def _():
                g_finish()

    outs = pl.pallas_call(
        body, name="rwkv_scan_fwd", grid=(nc,), in_specs=in_specs, out_specs=out_specs, out_shape=out_shape,
        scratch_shapes=[pltpu.VMEM((2 * NPAIR, HD, 128), F32), pltpu.VMEM((2 * NPAIR, HD, 128), BF16),
                        pltpu.VMEM((2 * NPAIR, HD, 128), F32), pltpu.VMEM((2 * NPAIR, 128, 128), F32)]
        + (_gather_sems(ngat) if ngat else []),
        compiler_params=pltpu.CompilerParams(dimension_semantics=("arbitrary",), vmem_limit_bytes=VMEM_LIMIT,
                                             has_side_effects=bool(ngat)),
    )(*operands)
    return [(outs[0], outs[1]), (outs[2], outs[3])], list(outs[4:])


def _scan2_bwd(per_dir, ps, kk, dy, scatter=()):
    s = ps.shape[0]
    nc, ng = s // TC, TC // 8
    nsc = len(scatter)
    in_specs, operands, out_specs, out_shape = [], [], [], []
    for d in (0, 1):
        row, rowv = _scan_specs(d, nc, False)
        dec, kd, b, ck = per_dir[d]
        in_specs += [row] * 5 + [rowv, row, _bs((1, NPAIR, HD, 128), lambda c: (nc - 1 - c, 0, 0, 0))]
        operands += [dec, kd, b, ps, kk, ps, dy, ck]
        out_specs += [row] * 6
        out_shape += [jax.ShapeDtypeStruct((s, RW), F32)] * 6
    in_specs += [ANY] * nsc
    operands += list(scatter)
    out_specs += [ANY] * nsc
    out_shape += _scatter_out_shapes(scatter)

    def body(*refs):
        ins = [refs[0:8], refs[8:16]]
        base = 16 + nsc
        outs = [refs[base:base + 6], refs[base + 6:base + 12]]
        st, sa_s, vb_s, dy_s, ds, vt_s, dyt_s, dvt_s, stage = refs[base + 12 + nsc:base + 21 + nsc]
        if nsc:
            s_start, s_finish = _scatter_phases(refs[16:base], refs[base + 12:base + 12 + nsc], refs[base + 21 + nsc:])

        @pl.when(pl.program_id(0) == 0)
        def _():
            dvt_s[...] = jnp.zeros_like(dvt_s)
            stage[...] = jnp.zeros_like(stage)
            ds[...] = jnp.zeros_like(ds)
            if nsc:
                s_start()

        for d in (0, 1):
            st[d * (TC + 1)] = ins[d][7][0]
        ones1 = _ones1()
        lane_u = lax.broadcasted_iota(jnp.int32, (HD, 128), 1) % HD
        row_id = lax.broadcasted_iota(jnp.int32, (8, 128), 0)
        pc = [slice(p * 128, (p + 1) * 128) for p in range(NPAIR)]

        def load_rows(gs):
            return [[q[pl.ds(pl.multiple_of(gs[d] * 8, 8), 8), :] for q in ins[d][:5]] for d in (0, 1)]

        def fgroup(gi, carry):
            gs = (gi, ng - 1 - gi)
            blk = load_rows(gs)
            for d in (0, 1):
                rows8 = pl.ds(pl.multiple_of(gs[d] * 8, 8), 8)
                _rows_to_tiles(ins[d][5], rows8, stage, vt_s, d * NPAIR)
                _rows_to_tiles(ins[d][6], rows8, stage, dyt_s, d * NPAIR)
            ss = [[st[d * (TC + 1) + gi * 8, p] for p in range(NPAIR)] for d in (0, 1)]
            for ui in range(8):
                us = (ui, 7 - ui)
                i = gi * 8 + ui
                lhs1 = []
                for d in (0, 1):
                    kk8 = blk[d][4]
                    lhs1 += [(ss[d][p] * kk8[us[d]:us[d] + 1, pc[p]]).astype(BF16) for p in range(NPAIR)]
                    for tile_s in (vt_s, dyt_s):
                        for p in range(NPAIR):
                            t = tile_s[d * NPAIR + p]
                            lhs1.append(jnp.where(lane_u == us[d], t, jnp.zeros_like(t)))
                res1 = jnp.dot(jnp.concatenate(lhs1, axis=0), ones1, preferred_element_type=F32)
                for d in (0, 1):
                    d8, k8, b8, _, _ = blk[d]
                    u = us[d]
                    sa, vb, dyb = _tiles(res1, 3 * d), _tiles(res1, 3 * d + 1), _tiles(res1, 3 * d + 2)
                    for p in range(NPAIR):
                        sa_s[d * TC + i, p] = sa[p]
                        vb_s[d * TC + i, p] = vb[p]
                        dy_s[d * TC + i, p] = dyb[p]
                        ss[d][p] = ss[d][p] * d8[u:u + 1, pc[p]] - sa[p] * b8[u:u + 1, pc[p]] + vb[p] * k8[u:u + 1, pc[p]]
                        st[d * (TC + 1) + i + 1, p] = ss[d][p]
            return carry

        for gi in range(ng):
            fgroup(gi, 0)

        def bgroup(gj, carry):
            gi = ng - 1 - gj
            gs = (gi, ng - 1 - gi)
            blk = load_rows(gs)
            dss = [[ds[d * NPAIR + p] for p in range(NPAIR)] for d in (0, 1)]
            acc = [[[jnp.zeros((8, 128), F32) for _ in range(5)] for _ in range(NPAIR)] for _ in (0, 1)]
            for uj in range(8):
                ui = 7 - uj
                us = (ui, 7 - ui)
                i = gi * 8 + ui
                lhs1, dyb = [], [None, None]
                for d in (0, 1):
                    _, k8, b8, r8, _ = blk[d]
                    u = us[d]
                    dyb[d] = [dy_s[d * TC + i, p] for p in range(NPAIR)]
                    for p in range(NPAIR):
                        dss[d][p] = dss[d][p] + dyb[d][p] * r8[u:u + 1, pc[p]]
                    lhs1 += [(dss[d][p] * b8[u:u + 1, pc[p]]).astype(BF16) for p in range(NPAIR)]
                    lhs1 += [(dss[d][p] * k8[u:u + 1, pc[p]]).astype(BF16) for p in range(NPAIR)]
                res1 = jnp.dot(jnp.concatenate(lhs1, axis=0), ones1, preferred_element_type=F32)
                for d in (0, 1):
                    d8, _, _, _, kk8 = blk[d]
                    u = us[d]
                    dsa, dvb = _tiles(res1, 2 * d), _tiles(res1, 2 * d + 1)
                    _put_cols(dvt_s, d * NPAIR, u, dvb)
                    for p in range(NPAIR):
                        sp, sn = st[d * (TC + 1) + i, p], st[d * (TC + 1) + i + 1, p]
                        dsv = dss[d][p]
                        vals = (jnp.sum(sn * dyb[d][p], axis=0, keepdims=True), jnp.sum(dsv * sp, axis=0, keepdims=True),
                                -jnp.sum(dsv * sa_s[d * TC + i, p], axis=0, keepdims=True),
                                jnp.sum(dsv * vb_s[d * TC + i, p], axis=0, keepdims=True),
                                -jnp.sum(sp * dsa[p], axis=0, keepdims=True))
                        acc[d][p] = [jnp.where(row_id == u, o, a_) for o, a_ in zip(vals, acc[d][p])]
                        dss[d][p] = dsv * d8[u:u + 1, pc[p]] - dsa[p] * kk8[u:u + 1, pc[p]]
            for d in (0, 1):
                rows8 = pl.ds(pl.multiple_of(gs[d] * 8, 8), 8)
                _tiles_to_rows(dvt_s, d * NPAIR, outs[d][5], rows8)
                for p in range(NPAIR):
                    ds[d * NPAIR + p] = dss[d][p]
                    for o_ref, a_ in zip(outs[d][:5], acc[d][p]):
                        o_ref[rows8, pc[p]] = a_
            return carry

        for gj in range(ng):
            bgroup(gj, 0)

        if nsc:
            @pl.when(pl.program_id(0) == nc - 1)
            def _():
                s_finish()

    chunk = lambda k: pltpu.VMEM((k, NPAIR, HD, 128), F32)
    pairs = lambda w, dt: pltpu.VMEM((2 * NPAIR, HD, w), dt)
    res = pl.pallas_call(
        body, name="rwkv_scan_bwd", grid=(nc,), in_specs=in_specs, out_specs=out_specs, out_shape=out_shape,
        scratch_shapes=[chunk(2 * (TC + 1)), chunk(2 * TC), chunk(2 * TC), chunk(2 * TC), pairs(128, F32),
                        pairs(128, BF16), pairs(128, BF16), pairs(128, F32), pltpu.VMEM((2 * NPAIR, 128, 128), F32)]
        + _scatter_sems(nsc),
        compiler_params=pltpu.CompilerParams(dimension_semantics=("arbitrary",), vmem_limit_bytes=VMEM_LIMIT,
                                             has_side_effects=bool(nsc)),
    )(*operands)
    return [res[0:6], res[6:12]], list(res[12:])


MT = 512
MN = 256


def _merge_fwd(ya, yr, yx, wa, wr, wx, proj, gate_b):
    s = ya.shape[0]

    def body(ya_ref, yr_ref, yx_ref, wa_ref, wr_ref, wx_ref, m0, m1, m2, b0, b1, b2, o_ref, ot_ref):
        acc = jnp.zeros((MT, MN), F32)
        for y_ref, w_ref, m_ref, b_ref in ((ya_ref, wa_ref, m0, b0), (yr_ref, wr_ref, m1, b1), (yx_ref, wx_ref, m2, b2)):
            u = _dot(y_ref[...], w_ref[...], ((1,), (0,)))
            acc = acc + jax.nn.sigmoid(m_ref[...] + b_ref[...]) * u
        o_ref[...] = acc.astype(BF16)
        ot_ref[...] = acc.T.astype(BF16)

    mg = lambda br: _bs((MT, MN), lambda i, j: (i, C_MG // MN + br * (D // MN) + j))
    gb = lambda br: _bs((1, MN), lambda i, j: (0, br * (D // MN) + j))
    return pl.pallas_call(
        body, name="merge_fwd", grid=(s // MT, D // MN),
        in_specs=[_bs((MT, RW), lambda i, j: (i, 0)), _bs((MT, RW), lambda i, j: (i, 0)), _bs((MT, XW), lambda i, j: (i, 0)),
                  _bs((RW, MN), lambda i, j: (0, j)), _bs((RW, MN), lambda i, j: (0, j)), _bs((XW, MN), lambda i, j: (0, j)),
                  mg(0), mg(1), mg(2), gb(0), gb(1), gb(2)],
        out_specs=[_bs((MT, MN), lambda i, j: (i, j)), _bs((MN, MT), lambda i, j: (j, i))],
        out_shape=[jax.ShapeDtypeStruct((s, D), BF16), jax.ShapeDtypeStruct((D, s), BF16)],
        compiler_params=_params(("parallel", "arbitrary")),
    )(ya, yr, yx, wa, wr, wx, proj, proj, proj, gate_b, gate_b, gate_b)


def _out_fwd(merged, w_out, x, target):
    s = x.shape[0]
    tm, tn = min(1024, s), 512

    def body(m_ref, w_ref, x_ref, t_ref, loss_ref, d_ref, d16_ref):
        @pl.when((pl.program_id(0) == 0) & (pl.program_id(1) == 0))
        def _():
            loss_ref[...] = jnp.zeros_like(loss_ref)

        out = x_ref[...] + jnp.dot(m_ref[...], w_ref[...], preferred_element_type=F32)
        err = out - t_ref[...]
        dout = err * (1.0 / D)
        d_ref[...] = dout
        d16_ref[...] = dout.astype(BF16)
        loss_ref[...] += jnp.sum(err * err)

    tile = _bs((tm, tn), lambda i, j: (i, j))
    return pl.pallas_call(
        body, name="out_fwd", grid=(s // tm, D // tn),
        in_specs=[_bs((tm, D), lambda i, j: (i, 0)), _bs((D, tn), lambda i, j: (0, j)), tile, tile],
        out_specs=[_bs((8, 128), lambda i, j: (0, 0)), tile, tile],
        out_shape=[jax.ShapeDtypeStruct((8, 128), F32), jax.ShapeDtypeStruct((s, D), F32),
                   jax.ShapeDtypeStruct((s, D), BF16)],
        compiler_params=_params(("arbitrary", "arbitrary")),
    )(merged, w_out, x, target)


def _merge_bwd(ya, yr, yx, wa, wr, wx, proj, gate_b, dmerged):
    s = ya.shape[0]

    def body(ya_ref, yr_ref, yx_ref, wa_ref, wr_ref, wx_ref, m0, m1, m2, b0, b1, b2, dm_ref,
             dg0, dg1, dg2, du0, du1, du2, dya_ref, dyr_ref, dyx_ref):
        @pl.when(pl.program_id(1) == 0)
        def _():
            dya_ref[...] = jnp.zeros_like(dya_ref)
            dyr_ref[...] = jnp.zeros_like(dyr_ref)
            dyx_ref[...] = jnp.zeros_like(dyx_ref)

        dm = dm_ref[...]
        branches = ((ya_ref, wa_ref, m0, b0, dg0, du0, dya_ref), (yr_ref, wr_ref, m1, b1, dg1, du1, dyr_ref),
                    (yx_ref, wx_ref, m2, b2, dg2, du2, dyx_ref))
        ws = [br[1][...] for br in branches]
        us = [_dot(br[0][...], w, ((1,), (0,))) for br, w in zip(branches, ws)]
        gts = [jax.nn.sigmoid(br[2][...] + br[3][...]) for br in branches]
        dus = [(dm * gt).astype(BF16) for gt in gts]
        for br, w, u, gt, du in zip(branches, ws, us, gts, dus):
            br[4][...] = (dm * u * gt * (1.0 - gt)).astype(BF16)
            br[5][...] = du
            br[6][...] += _dot(du, w, ((1,), (1,)))

    mg = lambda br: _bs((MT, MN), lambda i, j: (i, C_MG // MN + br * (D // MN) + j))
    gb = lambda br: _bs((1, MN), lambda i, j: (0, br * (D // MN) + j))
    tile = _bs((MT, MN), lambda i, j: (i, j))
    return pl.pallas_call(
        body, name="merge_bwd", grid=(s // MT, D // MN),
        in_specs=[_bs((MT, RW), lambda i, j: (i, 0)), _bs((MT, RW), lambda i, j: (i, 0)), _bs((MT, XW), lambda i, j: (i, 0)),
                  _bs((RW, MN), lambda i, j: (0, j)), _bs((RW, MN), lambda i, j: (0, j)), _bs((XW, MN), lambda i, j: (0, j)),
                  mg(0), mg(1), mg(2), gb(0), gb(1), gb(2), tile],
        out_specs=[tile] * 6 + [_bs((MT, RW), lambda i, j: (i, 0)), _bs((MT, RW), lambda i, j: (i, 0)),
                                _bs((MT, XW), lambda i, j: (i, 0))],
        out_shape=[jax.ShapeDtypeStruct((s, D), BF16)] * 6 + [jax.ShapeDtypeStruct((s, RW), F32),
                                                               jax.ShapeDtypeStruct((s, RW), F32),
                                                               jax.ShapeDtypeStruct((s, XW), F32)],
        compiler_params=_params(("parallel", "arbitrary")),
    )(ya, yr, yx, wa, wr, wx, proj, proj, proj, gate_b, gate_b, gate_b, dmerged)


def _colsum(a, name):
    m, n = a.shape
    tm, tn = min(2048, m), 512

    def body(a_ref, o_ref):
        @pl.when(pl.program_id(1) == 0)
        def _():
            o_ref[...] = jnp.zeros_like(o_ref)

        o_ref[...] += jnp.sum(a_ref[...].astype(F32), axis=0, keepdims=True)

    return pl.pallas_call(
        body, name=name, grid=(n // tn, m // tm),
        in_specs=[_bs((tm, tn), lambda j, i: (i, j))], out_specs=_bs((1, tn), lambda j, i: (0, j)),
        out_shape=jax.ShapeDtypeStruct((1, n), F32),
        compiler_params=_params(("parallel", "arbitrary")),
    )(a)


def _in_bwd(dproj, w_in, x, g, dout, stacks=()):
    s = x.shape[0]
    tm, tk = min(512, s), 896
    nk = NIN // tk
    ni = s // tm
    n = len(stacks)

    def body(dp_ref, w_ref, x_ref, g_ref, do_ref, *rest):
        ins, (gx_ref, gg_ref), outs = rest[:n], rest[n:n + 2], rest[n + 2:2 * n + 2]
        acc = rest[2 * n + 2]
        i, kk = pl.program_id(0), pl.program_id(1)

        if n:
            start, finish = _scatter_phases(ins, outs, rest[2 * n + 3:])

        @pl.when((i == 0) & (kk == 0))
        def _():
            gg_ref[...] = jnp.zeros_like(gg_ref)
            if n:
                start()

        @pl.when(kk == 0)
        def _():
            acc[...] = jnp.zeros_like(acc)

        acc[...] += _dot(dp_ref[...], w_ref[...], ((1,), (1,)))

        @pl.when(kk == nk - 1)
        def _():
            xv, dh, gv = x_ref[...], acc[...], g_ref[...]
            r = lax.rsqrt(jnp.mean(xv * xv, axis=-1, keepdims=True) + NORM_EPS)
            xn = xv * r
            gg_ref[...] += jnp.sum(dh * xn, axis=0, keepdims=True)
            dxn = dh * gv
            dx = r * (dxn - xn * jnp.mean(dxn * xn, axis=-1, keepdims=True))
            gx_ref[...] = do_ref[...] + dx

        if n:
            @pl.when((i == ni - 1) & (kk == nk - 1))
            def _():
                finish()

    any_spec = pl.BlockSpec(memory_space=pl.ANY)
    res = pl.pallas_call(
        body, name="in_bwd", grid=(ni, nk),
        in_specs=[_bs((tm, tk), lambda i, kk: (i, kk)), _bs((D, tk), lambda i, kk: (0, kk)),
                  _bs((tm, D), lambda i, kk: (i, 0)), _bs((1, D), lambda i, kk: (0, 0)),
                  _bs((tm, D), lambda i, kk: (i, 0))] + [any_spec] * n,
        out_specs=[_bs((tm, D), lambda i, kk: (i, 0)), _bs((1, D), lambda i, kk: (0, 0))] + [any_spec] * n,
        out_shape=[jax.ShapeDtypeStruct((s, D), F32), jax.ShapeDtypeStruct((1, D), F32)] + _scatter_out_shapes(stacks),
        scratch_shapes=[pltpu.VMEM((tm, D), F32)] + _scatter_sems(n),
        compiler_params=pltpu.CompilerParams(dimension_semantics=("arbitrary", "arbitrary"),
                                             vmem_limit_bytes=VMEM_LIMIT, has_side_effects=bool(n)),
    )(dproj, w_in, x, g, dout, *stacks)
    return res[0], res[1], list(res[2:])


def _adamw_math(w, g, m, v):
    m = ADAM_B1 * m + (1.0 - ADAM_B1) * g
    v = ADAM_B2 * v + (1.0 - ADAM_B2) * jnp.square(g)
    m_hat = m / (1.0 - ADAM_B1 ** ADAM_STEP)
    v_hat = v / (1.0 - ADAM_B2 ** ADAM_STEP)
    delta = -ADAM_LR * (m_hat / (jnp.sqrt(v_hat) + ADAM_EPS) + ADAM_WD * w)
    return delta, m, v


def _adamw(parts, w, m, v, name):
    rows, cols = w.shape
    tr = rows
    for cand in (256, 128, 64, 32, 16, 8):
        if rows % cand == 0 and cand * cols * 4 <= (1 << 20):
            tr = cand
            break
    n = len(parts)

    def body(*refs):
        g = refs[0][...].astype(F32)
        for r in refs[1:n]:
            g = g + r[...].astype(F32)
        w_ref, m_ref, v_ref, g_out, d_out, m_out, v_out = refs[n:]
        delta, m_new, v_new = _adamw_math(w_ref[...], g, m_ref[...], v_ref[...])
        g_out[...] = g
        d_out[...] = delta
        m_out[...] = m_new
        v_out[...] = v_new

    spec = _bs((tr, cols), lambda i: (i, 0))
    return pl.pallas_call(
        body, name=name, grid=(rows // tr,),
        in_specs=[spec] * (n + 3), out_specs=[spec] * 4,
        out_shape=[jax.ShapeDtypeStruct((rows, cols), F32)] * 4,
        compiler_params=_params(("parallel",)),
    )(*parts, w, m, v)


def _adamw_halves(mine, theirs, core, w, m, v, name):
    rows, cols = w.shape
    h = rows // 2
    tr = next(t for t in (256, 128, 64, 32, 16, 8) if h % t == 0 and t * cols * 4 <= (1 << 20))
    nt = h // tr

    def body(core_ref, mine_ref, theirs_ref, w_ref, m_ref, v_ref, g_out, d_out, m_out, v_out):
        is_mine = pl.program_id(0) // nt == core_ref[0]
        g = jnp.where(is_mine, mine_ref[...], theirs_ref[...])
        delta, m_new, v_new = _adamw_math(w_ref[...], g, m_ref[...], v_ref[...])
        g_out[...] = g
        d_out[...] = delta
        m_out[...] = m_new
        v_out[...] = v_new

    spec = _bs((tr, cols), lambda i, core_ref: (i, 0))
    return pl.pallas_call(
        body, name=name,
        grid_spec=pltpu.PrefetchScalarGridSpec(
            num_scalar_prefetch=1, grid=(2 * nt,),
            in_specs=[_bs((tr, cols), lambda i, core_ref: (jnp.clip(i - core_ref[0] * nt, 0, nt - 1), 0)),
                      _bs((tr, cols), lambda i, core_ref: (jnp.clip(i - (1 - core_ref[0]) * nt, 0, nt - 1), 0)),
                      spec, spec, spec],
            out_specs=[spec] * 4),
        out_shape=[jax.ShapeDtypeStruct((rows, cols), F32)] * 4,
        compiler_params=_params(("parallel",)),
    )(core, mine, theirs, w, m, v)


def _sum_parts(parts, name):
    rows, cols = parts[0].shape
    tr = rows
    for cand in (256, 128, 64, 32, 16, 8):
        if rows % cand == 0 and cand * cols * 4 <= (1 << 20):
            tr = cand
            break

    def body(*refs):
        acc = refs[0][...].astype(F32)
        for r in refs[1:-1]:
            acc = acc + r[...].astype(F32)
        refs[-1][...] = acc

    spec = _bs((tr, cols), lambda i: (i, 0))
    return pl.pallas_call(
        body, name=name, grid=(rows // tr,), in_specs=[spec] * len(parts), out_specs=spec,
        out_shape=jax.ShapeDtypeStruct((rows, cols), F32), compiler_params=_params(("parallel",)),
    )(*parts)


ANY = pl.BlockSpec(memory_space=pl.ANY)


def _other_chips(x, y):
    return [(1 - x, y), (x, 1 - y), (1 - x, 1 - y)]


def _gather_shards(arrays, name):
    n = len(arrays)

    def body(*refs):
        start, forward, finish = _gather_phases(arrays, refs[:n], refs[n:2 * n], refs[2 * n:])
        start()
        forward()
        finish()

    return pl.pallas_call(
        body, name=name, in_specs=[ANY] * n, out_specs=[ANY] * n,
        out_shape=_gather_out_shapes(arrays), scratch_shapes=_gather_sems(n),
        compiler_params=pltpu.CompilerParams(has_side_effects=True),
    )(*arrays)


def _gather_out_shapes(arrays):
    return [jax.ShapeDtypeStruct((4,) + a.shape, a.dtype) for a in arrays]


def _gather_sems(n):
    dma = lambda k: pltpu.SemaphoreType.DMA((k,))
    return [dma(3 * n), dma(3 * n), dma(3 * n), dma(3 * n), dma(n), dma(n)]


def _gather_phases(arrays, ins, outs, sems):
    n = len(arrays)
    ici_send, ici_recv, d2d_send, d2d_recv, own_send, own_recv = sems

    def place():
        x, y, c = lax.axis_index("x"), lax.axis_index("y"), lax.axis_index("c")
        return x, y, c, 2 * x + y, _other_chips(x, y)

    def half(i, who):
        h = arrays[i].shape[0] // 2
        return pl.ds(who * h, h)

    def ici(i, j, src_chip, to, c):
        return pltpu.make_async_remote_copy(
            src_ref=ins[i].at[half(i, c)], dst_ref=outs[i].at[src_chip, half(i, c)], send_sem=ici_send.at[3 * i + j],
            recv_sem=ici_recv.at[3 * i + j], device_id=to, device_id_type=MESH)

    def d2d(i, j, src_chip, who, sib):
        piece = outs[i].at[src_chip, half(i, who)]
        return pltpu.make_async_remote_copy(
            src_ref=piece, dst_ref=piece, send_sem=d2d_send.at[3 * i + j], recv_sem=d2d_recv.at[3 * i + j],
            device_id=sib, device_id_type=MESH)

    def own(i, me, sib):
        return pltpu.make_async_remote_copy(
            src_ref=ins[i], dst_ref=outs[i].at[me], send_sem=own_send.at[i], recv_sem=own_recv.at[i],
            device_id=sib, device_id_type=MESH)

    def start():
        x, y, c, me, chips = place()
        for i in range(n):
            own(i, me, (x, y, 1 - c)).start()
            for j, (px, py) in enumerate(chips):
                ici(i, j, me, (px, py, c), c).start()

    def forward():
        x, y, c, me, chips = place()
        for i in range(n):
            for j, (px, py) in enumerate(chips):
                ici(i, j, 2 * px + py, (px, py, c), c).wait_recv()
                d2d(i, j, 2 * px + py, c, (x, y, 1 - c)).start()

    def finish():
        x, y, c, me, chips = place()
        sib = (x, y, 1 - c)
        for i in range(n):
            for j, (px, py) in enumerate(chips):
                d2d(i, j, 2 * px + py, 1 - c, sib).wait_recv()
            own(i, me, sib).wait_recv()
        for i in range(n):
            own(i, me, sib).wait_send()
            for j, (px, py) in enumerate(chips):
                ici(i, j, me, (px, py, c), c).wait_send()
                d2d(i, j, 2 * px + py, c, sib).wait_send()

    return start, forward, finish


def _scatter_phases(ins, outs, sems):
    send_sems, recv_sems = sems

    def copies():
        x, y, c = lax.axis_index("x"), lax.axis_index("y"), lax.axis_index("c")
        return [pltpu.make_async_remote_copy(
            src_ref=ins[a].at[2 * qx + qy], dst_ref=outs[a].at[j], send_sem=send_sems.at[3 * a + j],
            recv_sem=recv_sems.at[3 * a + j], device_id=(qx, qy, c), device_id_type=MESH)
            for a in range(len(ins)) for j, (qx, qy) in enumerate(_other_chips(x, y))]

    def start():
        for rc in copies():
            rc.start()

    def finish():
        for rc in copies():
            rc.wait_recv()
        for rc in copies():
            rc.wait_send()

    return start, finish


def _scatter_out_shapes(stacks):
    return [jax.ShapeDtypeStruct((3,) + a.shape[1:], a.dtype) for a in stacks]


def _scatter_sems(n):
    return [pltpu.SemaphoreType.DMA((3 * n,)), pltpu.SemaphoreType.DMA((3 * n,))] if n else []


def _pair_exchange(stacks, name):
    n = len(stacks)

    def body(*refs):
        ins, outs = refs[:n], refs[n:2 * n]
        send_sems, recv_sems = refs[2 * n:]
        x, y, c = lax.axis_index("x"), lax.axis_index("y"), lax.axis_index("c")
        cps = []
        for i in range(n):
            h = stacks[i].shape[1] // 2
            rc = pltpu.make_async_remote_copy(
                src_ref=ins[i].at[:, pl.ds((1 - c) * h, h)], dst_ref=outs[i], send_sem=send_sems.at[i],
                recv_sem=recv_sems.at[i], device_id=(x, y, 1 - c), device_id_type=MESH)
            rc.start()
            cps.append(rc)
        for rc in cps:
            rc.wait_recv()
        for rc in cps:
            rc.wait_send()

    return pl.pallas_call(
        body, name=name, in_specs=[ANY] * n, out_specs=[ANY] * n,
        out_shape=[jax.ShapeDtypeStruct((4, a.shape[1] // 2) + a.shape[2:], a.dtype) for a in stacks],
        scratch_shapes=[pltpu.SemaphoreType.DMA((n,)), pltpu.SemaphoreType.DMA((n,))],
        compiler_params=pltpu.CompilerParams(has_side_effects=True),
    )(*stacks)


def _pair_sum(own, theirs, core, name):
    _, r, cols = own.shape
    h = r // 2
    tr = next(t for t in (256, 128, 64, 32, 16) if h % t == 0 and t * cols * 4 <= (1 << 20))
    nt = h // tr

    def body(core_ref, own_ref, th_ref, o32_ref, o16_ref):
        del core_ref
        acc = own_ref[...] + th_ref[...].astype(F32)
        o32_ref[...] = acc
        o16_ref[...] = acc.astype(BF16)

    out = _bs((1, tr, cols), lambda j, t, core_ref: (j, t, 0))
    return pl.pallas_call(
        body, name=name,
        grid_spec=pltpu.PrefetchScalarGridSpec(
            num_scalar_prefetch=1, grid=(4, nt),
            in_specs=[_bs((1, tr, cols), lambda j, t, core_ref: (j, core_ref[0] * nt + t, 0)), out],
            out_specs=[out, out]),
        out_shape=[jax.ShapeDtypeStruct((4, h, cols), F32), jax.ShapeDtypeStruct((4, h, cols), BF16)],
        compiler_params=_params(("parallel", "parallel")),
    )(core, own, theirs)


def _swap_sibling(arrays, name):
    n = len(arrays)

    def body(*refs):
        ins, outs = refs[:n], refs[n:2 * n]
        send_sems, recv_sems = refs[2 * n:]
        sib = (lax.axis_index("x"), lax.axis_index("y"), 1 - lax.axis_index("c"))
        cps = []
        for i in range(n):
            rc = pltpu.make_async_remote_copy(src_ref=ins[i], dst_ref=outs[i], send_sem=send_sems.at[i],
                                              recv_sem=recv_sems.at[i], device_id=sib, device_id_type=MESH)
            rc.start()
            cps.append(rc)
        for rc in cps:
            rc.wait_recv()
        for rc in cps:
            rc.wait_send()

    return pl.pallas_call(
        body, name=name, in_specs=[ANY] * n, out_specs=[ANY] * n,
        out_shape=[jax.ShapeDtypeStruct(a.shape, a.dtype) for a in arrays],
        scratch_shapes=[pltpu.SemaphoreType.DMA((n,)), pltpu.SemaphoreType.DMA((n,))],
        compiler_params=pltpu.CompilerParams(has_side_effects=True),
    )(*arrays)


def _all_reduce_small(v):
    rows = v.shape[0]
    h = rows // 2

    def body(v_ref, o_ref, sib, pair, buf, send_sems, recv_sems):
        x, y, c = lax.axis_index("x"), lax.axis_index("y"), lax.axis_index("c")
        me = 2 * x + y
        sibling = (x, y, 1 - c)
        mine = pl.ds(pl.multiple_of(c * h, 8), h)
        theirs = pl.ds(pl.multiple_of((1 - c) * h, 8), h)

        def copy(src, dst, k, to):
            return pltpu.make_async_remote_copy(src_ref=src, dst_ref=dst, send_sem=send_sems.at[k],
                                                recv_sem=recv_sems.at[k], device_id=to, device_id_type=MESH)

        swap = copy(v_ref, sib, 0, sibling)
        swap.start()
        swap.wait_recv()
        pair[...] = v_ref[...] + sib[...]
        buf[me] = pair[mine, :]
        out = [copy(buf.at[me], buf.at[me], 1 + j, (px, py, c)) for j, (px, py) in enumerate(_other_chips(x, y))]
        for rc in out:
            rc.start()
        for j, (px, py) in enumerate(_other_chips(x, y)):
            copy(buf.at[me], buf.at[2 * px + py], 1 + j, (px, py, c)).wait_recv()
        o_ref[mine, :] = (buf[0] + buf[1]) + (buf[2] + buf[3])
        back = copy(o_ref.at[mine], o_ref.at[mine], 4, sibling)
        back.start()
        copy(o_ref.at[theirs], o_ref.at[theirs], 4, sibling).wait_recv()
        for rc in [swap, back] + out:
            rc.wait_send()

    vmem = pl.BlockSpec(memory_space=pltpu.VMEM)
    return pl.pallas_call(
        body, name="all_reduce_small", in_specs=[vmem], out_specs=vmem,
        out_shape=jax.ShapeDtypeStruct((rows, 128), F32),
        scratch_shapes=[pltpu.VMEM((rows, 128), F32), pltpu.VMEM((rows, 128), F32), pltpu.VMEM((4, h, 128), F32),
                        pltpu.SemaphoreType.DMA((5,)), pltpu.SemaphoreType.DMA((5,))],
        compiler_params=pltpu.CompilerParams(has_side_effects=True, vmem_limit_bytes=VMEM_LIMIT),
    )(v)


def _rope_tables(s):
    half = HD // 2
    inv = 10000.0 ** (-jnp.arange(half, dtype=F32) / half)
    ang = jnp.arange(s, dtype=F32)[:, None] * inv[None, :]
    cos, sin = jnp.cos(ang), jnp.sin(ang)
    return jnp.concatenate([cos, cos], axis=1), jnp.concatenate([sin, sin], axis=1)


LATE = ['attn_w_o', 'rwkv_w_o', 'x_w_kv', 'x_w_o', 'w_out']


def _local_step(x, mem, target, norm_g, mem_norm_g, w_in, gate_b, gq, gk, sink, wa, mu, k_k, k_a, r_k, w0, w2, a0, a2,
                ln_w, ln_b, wr, w_kv, gxq, gxk, wx, w_out, late_shards=None, early_exchange=None):
    s = x.shape[0]
    cos, sin = _rope_tables(s)
    r_k = r_k.reshape(1, RW)

    proj, h = _proj_fwd(x, norm_g, w_in)
    ya, ya_t = _attn_fwd(proj, cos, sin, gq, gk, sink)
    ps = _shift_fwd(proj, mu)
    kk, dec0, kd0, b0, dec1, kd1, b1 = _pre_fwd(ps, k_k, k_a, w0, w2, a0, a2)
    ((y0, ck0), (y1, ck1)), stacks = _scan2_fwd([(dec0, kd0, b0), (dec1, kd1, b1)], ps, kk, gather=late_shards or ())
    if late_shards:
        st = dict(zip(LATE, stacks))
        wa, wr, wx = (_unshard_cols(st[n]) for n in ('attn_w_o', 'rwkv_w_o', 'x_w_o'))
        w_kv, w_out = st['x_w_kv'].reshape(D, 2 * XW), st['w_out'].reshape(D, D)
    mkv, mn = _mem_kv(mem, mem_norm_g, w_kv)
    yx, yx_t = _xattn_fwd(proj, mkv, gxq, gxk)
    yr, yr_t = _post_fwd(y0, y1, ps, kd0, kd1, proj, r_k, ln_w, ln_b)
    merged, merged_t = _merge_fwd(ya, yr, yx, wa, wr, wx, proj, gate_b)
    loss_tile, dout, dout16 = _out_fwd(merged, w_out, x, target)
    loss_sum = loss_tile[0, 0]

    g = {}
    sk = min(1024, s)
    dmerged = _matmul(dout16, w_out, mode="nt", m=s, n=D, k=D, tm=sk, tn=1024, tk=1024, name="dmerged")
    g["w_out"], g["w_out_bf16"] = _matmul(merged_t, dout16, mode="nn", m=D, n=D, k=s, tm=1024, tn=1024, tk=sk,
                                          name="grad_w_out", twin=True)
    dg0, dg1, dg2, du0, du1, du2, dya, dyr, dyx = _merge_bwd(ya, yr, yx, wa, wr, wx, proj, gate_b, dmerged)
    for n, yt, du in (("attn_w_o", ya_t, du0), ("rwkv_w_o", yr_t, du1), ("x_w_o", yx_t, du2)):
        g[n], g[n + "_bf16"] = _matmul(yt, du, mode="nn", m=yt.shape[0], n=D, k=s, tm=yt.shape[0], tn=512, tk=s,
                                       name="grad_" + n, shards=4, twin=True)
    dmg = jnp.concatenate([dg0, dg1, dg2], axis=1)
    g["gate_b"] = _colsum(dmg, "grad_gate_b")

    daq, dak, dav, dag, g["attn_q_norm_g"], g["attn_k_norm_g"], g["attn_sink"] = _attn_bwd(proj, cos, sin, gq, gk, sink, dya)

    dxq, dxg, dmkv, g["x_q_norm_g"], g["x_k_norm_g"] = _xattn_bwd(proj, mkv, gxq, gxk, dyx)
    g["x_w_kv"], g["x_w_kv_bf16"] = _matmul(mn, dmkv, mode="tn", m=D, n=2 * XW, k=NMEM, tm=512, tn=512, tk=NMEM,
                                            name="grad_x_w_kv", twin=True)
    dmn = _matmul(dmkv, w_kv, mode="nt", m=NMEM, n=D, k=2 * XW, tm=NMEM, tn=512, tk=2 * XW, name="dmn")
    g["mem_norm_g"] = _mem_bwd(mem, dmn)

    dys, dr_p, dv_p, dkd0_p, dkd1_p, drg, g["rwkv_r_k"], g["rwkv_ln_w"], g["rwkv_ln_b"] = _post_bwd(
        y0, y1, ps, kd0, kd1, proj, r_k, ln_w, ln_b, dyr)
    sent = early_exchange(g) if early_exchange else ()
    ((dr0, dd0, db0, dk0, dkk0, dv0), (dr1, dd1, db1, dk1, dkk1, dv1)), received = _scan2_bwd(
        [(dec0, kd0, b0, ck0), (dec1, kd1, b1, ck1)], ps, kk, dys, scatter=sent)
    dr = dr_p + dr0 + dr1
    dv = dv_p + dv0 + dv1
    cts = (dkk0 + dkk1, dd0, dk0 + dkd0_p, db0, dd1, dk1 + dkd1_p, db1)
    dps, g["rwkv_k_k"], g["rwkv_k_a"], g["rwkv_w0"], g["rwkv_w2"], g["rwkv_a0"], g["rwkv_a2"] = _pre_bwd(
        ps, k_k, k_a, w0, w2, a0, a2, dr, dv, cts)
    drs, g["rwkv_mu"] = _shift_bwd(proj, mu, dps)

    dproj = jnp.concatenate([daq.astype(BF16), dak.astype(BF16), dav.astype(BF16), dag.astype(BF16), drs.astype(BF16),
                             drg.astype(BF16), dxq.astype(BF16), dxg.astype(BF16), dmg], axis=1)
    dproj4 = jnp.stack([dproj[:, j * (NIN // 4):(j + 1) * (NIN // 4)] for j in range(4)])
    g["w_in"], g["w_in_bf16"] = _grad_w_in(h.T, dproj4)
    g["rwkv_r_k"] = g["rwkv_r_k"].reshape(AH, HD)
    return loss_sum, g, (dproj, w_in, x, norm_g, dout), received


WEIGHTS = ['norm_g', 'mem_norm_g', 'w_in', 'gate_b', 'attn_q_norm_g', 'attn_k_norm_g', 'attn_sink', 'attn_w_o',
           'rwkv_mu', 'rwkv_k_k', 'rwkv_k_a', 'rwkv_r_k', 'rwkv_w0', 'rwkv_w2', 'rwkv_a0', 'rwkv_a2', 'rwkv_ln_w',
           'rwkv_ln_b', 'rwkv_w_o', 'x_w_kv', 'x_q_norm_g', 'x_k_norm_g', 'x_w_o', 'w_out']
BIG = ['w_in', 'attn_w_o', 'rwkv_w_o', 'x_w_kv', 'x_w_o', 'w_out']
COL_SHARDED = ['w_in', 'attn_w_o', 'rwkv_w_o', 'x_w_o']
LORA = ['rwkv_w0', 'rwkv_w2', 'rwkv_a0', 'rwkv_a2']
SMALL = [n for n in WEIGHTS if n not in BIG]


def _unshard_cols(stack):
    return jnp.concatenate([stack[i] for i in range(4)], axis=-1)


def kernel(x, mem, norm_g, mem_norm_g, w_in, gate_b, attn_q_norm_g, attn_k_norm_g, attn_sink, attn_w_o, rwkv_mu, rwkv_k_k, rwkv_k_a, rwkv_r_k, rwkv_w0, rwkv_w2, rwkv_a0, rwkv_a2, rwkv_ln_w, rwkv_ln_b, rwkv_w_o, x_w_kv, x_q_norm_g, x_k_norm_g, x_w_o, w_out, loss_target, m_norm_g, m_mem_norm_g, m_w_in, m_gate_b, m_attn_q_norm_g, m_attn_k_norm_g, m_attn_sink, m_attn_w_o, m_rwkv_mu, m_rwkv_k_k, m_rwkv_k_a, m_rwkv_r_k, m_rwkv_w0, m_rwkv_w2, m_rwkv_a0, m_rwkv_a2, m_rwkv_ln_w, m_rwkv_ln_b, m_rwkv_w_o, m_x_w_kv, m_x_q_norm_g, m_x_k_norm_g, m_x_w_o, m_w_out, v_norm_g, v_mem_norm_g, v_w_in, v_gate_b, v_attn_q_norm_g, v_attn_k_norm_g, v_attn_sink, v_attn_w_o, v_rwkv_mu, v_rwkv_k_k, v_rwkv_k_a, v_rwkv_r_k, v_rwkv_w0, v_rwkv_w2, v_rwkv_a0, v_rwkv_a2, v_rwkv_ln_w, v_rwkv_ln_b, v_rwkv_w_o, v_x_w_kv, v_x_q_norm_g, v_x_k_norm_g, v_x_w_o, v_w_out):
    args = dict(locals())
    canon = lambda a: a[0] if a.ndim > 2 else a
    w = {n: canon(args[n]) for n in WEIGHTS}
    m = {n: canon(args["m_" + n]) for n in WEIGHTS}
    v = {n: canon(args["v_" + n]) for n in WEIGHTS}
    shard = 2 * lax.axis_index("x") + lax.axis_index("y")

    now = ["w_in"] + LORA
    local = [w["w_in"].astype(BF16)] + [w[n].reshape(2, -1, w[n].shape[-1]) for n in LORA]
    stacks = dict(zip(now, _gather_shards(local, "gather_weights")))
    full = {"w_in": _unshard_cols(stacks["w_in"])}
    for n in LORA:
        full[n] = _unshard_cols(stacks[n]).reshape(w[n].shape[:-1] + (RW,))

    core = lax.axis_index("c").astype(jnp.int32).reshape(1)
    pair32 = {}

    def as_stack(g, n, dtype):
        t = g[n] if dtype == F32 else g[n + "_bf16"]
        return t if n in COL_SHARDED else t.reshape((4, t.shape[0] // 4) + t.shape[1:])

    def pair_sums(g, names, tag):
        sibling = _pair_exchange([as_stack(g, n, BF16) for n in names], "pair_exchange_" + tag)
        sent = []
        for n, th in zip(names, sibling):
            pair32[n], a16 = _pair_sum(as_stack(g, n, F32), th, core, "pair_sum_" + n)
            sent.append(a16)
        return sent

    loss_sum, g, deferred, recv_late = _local_step(
        x[0], mem[0], loss_target[0], w["norm_g"], w["mem_norm_g"], full["w_in"], w["gate_b"], w["attn_q_norm_g"],
        w["attn_k_norm_g"], w["attn_sink"], None, w["rwkv_mu"], w["rwkv_k_k"], w["rwkv_k_a"], w["rwkv_r_k"],
        full["rwkv_w0"], full["rwkv_w2"], full["rwkv_a0"], full["rwkv_a2"], w["rwkv_ln_w"], w["rwkv_ln_b"],
        None, None, w["x_q_norm_g"], w["x_k_norm_g"], None, None,
        late_shards=[w[n].astype(BF16) for n in LATE], early_exchange=lambda g: pair_sums(g, LATE, "late"))

    loss = lax.psum(0.5 * loss_sum / D, ("x", "y", "c"))

    grad_x, g["norm_g"], recv_w_in = _in_bwd(*deferred, stacks=pair_sums(g, ["w_in"], "w_in"))
    halves = []
    for n, r in zip(BIG, recv_w_in + recv_late):
        own = lax.dynamic_index_in_dim(pair32[n], shard, 0, keepdims=False)
        halves.append(_sum_parts([own, r[0], r[1], r[2]], "sum_" + n))
    other_halves = _swap_sibling(halves, "swap_halves")

    out_g, out_d, out_m, out_v = {}, {}, {}, {}
    for n, mine, theirs in zip(BIG, halves, other_halves):
        out_g[n], out_d[n], out_m[n], out_v[n] = _adamw_halves(mine, theirs, core, w[n], m[n], v[n], "adamw_" + n)

    flat = jnp.concatenate([g[n].reshape(-1) for n in SMALL])
    total = flat.shape[0]
    padded = -(-total // 2048) * 2048
    flat = jnp.pad(flat, (0, padded - total)).reshape(padded // 128, 128)
    red = _all_reduce_small(flat).reshape(-1)
    off = 0
    gs = {}
    for n in SMALL:
        size = g[n].size
        t = red[off:off + size].reshape(g[n].shape)
        off += size
        if n in LORA:
            wd = t.shape[-1] // 4
            t = lax.dynamic_slice_in_dim(t, shard * wd, wd, axis=t.ndim - 1)
        gs[n] = t

    def pack(d):
        f = jnp.concatenate([d[n].reshape(-1) for n in SMALL])
        return jnp.pad(f, (0, -(-f.shape[0] // 1024) * 1024 - f.shape[0])).reshape(-1, 128)

    pg, pd, pm, pv = _adamw([pack(gs)], pack(w), pack(m), pack(v), "adamw_small")
    off = 0
    for n in SMALL:
        size = w[n].size
        for dst, src in ((out_g, pg), (out_d, pd), (out_m, pm), (out_v, pv)):
            dst[n] = src.reshape(-1)[off:off + size].reshape(w[n].shape)
        off += size

    lead = lambda d: [d[n][None] if args[n].ndim > 2 else d[n] for n in WEIGHTS]
    return (loss, grad_x[None], *lead(out_g), *lead(out_d), *lead(out_m), *lead(out_v))
```
